```python
import math
import jax, jax.numpy as jnp
from jax import lax
import numpy as np

D_MODEL = 1024
BATCH = 8
SEQ = 8192
DEPTH = 1

HEAD_DIM = 64
N_ATT_HEADS = 12
ATT_WIDTH = N_ATT_HEADS * HEAD_DIM
DILATED_PATTERNS = ((128, 1), (512, 4), (2048, 16))
ATT_BLOCK = 128
SSM_EXPAND = 2
SSM_INNER = SSM_EXPAND * D_MODEL
SSM_HEAD_DIM = 64
SSM_HEADS = SSM_INNER // SSM_HEAD_DIM
SSM_GROUPS = 8
SSM_STATE = 128
SSM_CONV = 4
SSM_CHUNK = 128
CONV_DIM = SSM_INNER + 2 * SSM_GROUPS * SSM_STATE
FFN_HIDDEN = 4 * D_MODEL
N_BRANCHES = 2
IN_SPLITS = (ATT_WIDTH, ATT_WIDTH, ATT_WIDTH, SSM_INNER, CONV_DIM, SSM_HEADS, N_BRANCHES * D_MODEL)
IN_PROJ_WIDTH = sum(IN_SPLITS)
RMS_EPS = 1e-6

kernel_name = "hybrid_dilated_attn_mamba2_gated_block"


def rmsnorm(x, w):
    x32 = x.astype(jnp.float32)
    y = x32 * lax.rsqrt(jnp.mean(x32 * x32, axis=-1, keepdims=True) + RMS_EPS)
    return (y * w.astype(jnp.float32)).astype(x.dtype)


def alibi_slopes(n):
    def pow2(m):
        start = 2.0 ** (-8.0 / m)
        return [start ** (i + 1) for i in range(m)]
    if (n & (n - 1)) == 0:
        s = pow2(n)
    else:
        c = 2 ** int(math.floor(math.log2(n)))
        s = pow2(c) + pow2(2 * c)[0::2][: n - c]
    return jnp.asarray(np.array(s, dtype=np.float32))


def dilated_window_attention(q, k, v, slopes, window, dilation):
    b, s, h, dh = q.shape
    band = window // dilation
    blk = ATT_BLOCK
    span = dilation * blk
    s_pad = -(-s // span) * span
    pad = ((0, 0), (0, s_pad - s), (0, 0), (0, 0))
    q, k, v = [jnp.pad(t, pad) for t in (q, k, v)]
    nb = s_pad // span
    qb = q.reshape(b, nb, blk, dilation, h, dh)
    kb = k.reshape(b, nb, blk, dilation, h, dh)
    vb = v.reshape(b, nb, blk, dilation, h, dh)

    def with_prev(t):
        prev = jnp.concatenate([jnp.zeros_like(t[:, :1]), t[:, :-1]], axis=1)
        return jnp.concatenate([prev, t], axis=2)

    kk, vv = with_prev(kb), with_prev(vb)
    scores = jnp.einsum('bnirhd,bnjrhd->bnrhij', qb, kk,
                        preferred_element_type=jnp.float32) * (dh ** -0.5)
    i_idx = jnp.arange(blk)[:, None]
    j_idx = jnp.arange(2 * blk)[None, :]
    dist = blk + i_idx - j_idx
    n_idx = jnp.arange(nb)[:, None, None]
    valid = (dist >= 0) & (dist <= band) & ((n_idx > 0) | (j_idx >= blk))
    bias = -slopes[:, None, None] * (dist * dilation).astype(jnp.float32)
    scores = scores + bias[None, None, None]
    scores = jnp.where(valid[None, :, None, None], scores, -jnp.inf)
    m = jnp.max(scores, axis=-1, keepdims=True)
    p = jnp.exp(scores - m)
    l = jnp.sum(p, axis=-1)
    o = jnp.einsum('bnrhij,bnjrhd->bnirhd', p, vv.astype(jnp.float32))
    l_t = jnp.transpose(l, (0, 1, 4, 2, 3))
    m_t = jnp.transpose(m[..., 0], (0, 1, 4, 2, 3))
    o = o / l_t[..., None]
    o = o.reshape(b, s_pad, h, dh)[:, :s]
    return o, m_t.reshape(b, s_pad, h)[:, :s], l_t.reshape(b, s_pad, h)[:, :s]


def ssd_chunked(x, dt, a, bmat, cmat):
    b, s, g, hg, p = x.shape
    n = bmat.shape[-1]
    ch = SSM_CHUNK
    s_pad = -(-s // ch) * ch
    padlen = s_pad - s
    x = jnp.pad(x.astype(jnp.float32), ((0, 0), (0, padlen), (0, 0), (0, 0), (0, 0)))
    dt = jnp.pad(dt, ((0, 0), (0, padlen), (0, 0), (0, 0)))
    bmat = jnp.pad(bmat.astype(jnp.float32), ((0, 0), (0, padlen), (0, 0), (0, 0)))
    cmat = jnp.pad(cmat.astype(jnp.float32), ((0, 0), (0, padlen), (0, 0), (0, 0)))
    c = s_pad // ch
    xd = (x * dt[..., None]).reshape(b, c, ch, g, hg, p)
    la = jnp.moveaxis((dt * a).reshape(b, c, ch, g, hg), 2, -1)
    bc = bmat.reshape(b, c, ch, g, n)
    cc = cmat.reshape(b, c, ch, g, n)
    a_cs = jnp.cumsum(la, axis=-1)
    tril = jnp.tril(jnp.ones((ch, ch), dtype=bool))
    decay = jnp.exp(jnp.where(tril, a_cs[..., :, None] - a_cs[..., None, :], -jnp.inf))
    cb = jnp.einsum('bclgn,bcsgn->bcgls', cc, bc)
    y_diag = jnp.einsum('bcghls,bcsghp->bclghp', cb[:, :, :, None] * decay, xd)
    decay_states = jnp.exp(a_cs[..., -1:] - a_cs)
    states = jnp.einsum('bclgn,bcghl,bclghp->bcghpn', bc, decay_states, xd)
    chunk_decay = jnp.exp(a_cs[..., -1])

    def step(hstate, inp):
        st, dec = inp
        return hstate * dec[..., None, None] + st, hstate

    init = jnp.zeros((b, g, hg, p, n), jnp.float32)
    _, prev = lax.scan(step, init, (jnp.moveaxis(states, 1, 0), jnp.moveaxis(chunk_decay, 1, 0)))
    prev = jnp.moveaxis(prev, 0, 1)
    y_off = jnp.einsum('bclgn,bcghpn,bcghl->bclghp', cc, prev, jnp.exp(a_cs))
    return (y_diag + y_off).reshape(b, s_pad, g, hg, p)[:, :s]


def mamba2_mixer(z, xbc, dt_raw, conv_w, conv_b, dt_bias, a_log, d_skip, norm_w):
    b, s, _ = z.shape
    hg = SSM_HEADS // SSM_GROUPS
    xbc = lax.conv_general_dilated(xbc, conv_w[:, None, :], window_strides=(1,),
                                   padding=[(SSM_CONV - 1, 0)],
                                   dimension_numbers=('NWC', 'WIO', 'NWC'),
                                   feature_group_count=CONV_DIM) + conv_b
    xbc = jax.nn.silu(xbc)
    xs, bm, cm = jnp.split(xbc, [SSM_INNER, SSM_INNER + SSM_GROUPS * SSM_STATE], axis=-1)
    xh = xs.reshape(b, s, SSM_GROUPS, hg, SSM_HEAD_DIM)
    bm = bm.reshape(b, s, SSM_GROUPS, SSM_STATE)
    cm = cm.reshape(b, s, SSM_GROUPS, SSM_STATE)
    dt = jax.nn.softplus((dt_raw + dt_bias).astype(jnp.float32)).reshape(b, s, SSM_GROUPS, hg)
    a = -jnp.exp(a_log.astype(jnp.float32)).reshape(SSM_GROUPS, hg)
    y = ssd_chunked(xh, dt, a, bm, cm)
    y = y + d_skip.astype(jnp.float32).reshape(SSM_GROUPS, hg)[:, :, None] * xh.astype(jnp.float32)
    y = y.reshape(b, s, SSM_INNER) * jax.nn.silu(z.astype(jnp.float32))
    yg = y.reshape(b, s, SSM_GROUPS, SSM_INNER // SSM_GROUPS)
    yg = yg * lax.rsqrt(jnp.mean(yg * yg, axis=-1, keepdims=True) + RMS_EPS)
    y = yg.reshape(b, s, SSM_INNER) * norm_w.astype(jnp.float32)
    return y.astype(z.dtype)


def _fwd_setup_inputs(seed: int = 0) -> dict:
    key = jax.random.key(seed)
    ks = jax.random.split(key, 20)
    f32 = jnp.float32

    def nrm(k, shape, scale):
        return jax.random.normal(k, shape, f32) * scale

    def gain(k, width):
        return 1.0 + 0.05 * jax.random.normal(k, (DEPTH, width), f32)

    dt0 = jnp.exp(jax.random.uniform(ks[7], (DEPTH, SSM_HEADS), f32,
                                     math.log(1e-3), math.log(1e-1)))
    return {
        "x": jax.random.normal(ks[0], (BATCH, SEQ, D_MODEL), f32),
        "norm_mix_pre_w": gain(ks[1], D_MODEL),
        "w_in": nrm(ks[2], (DEPTH, D_MODEL, IN_PROJ_WIDTH), D_MODEL ** -0.5),
        "b_gate": nrm(ks[3], (DEPTH, N_BRANCHES * D_MODEL), 0.02),
        "conv_w": nrm(ks[4], (DEPTH, SSM_CONV, CONV_DIM), SSM_CONV ** -0.5),
        "conv_b": nrm(ks[5], (DEPTH, CONV_DIM), 0.02),
        "dt_bias": dt0 + jnp.log(-jnp.expm1(-dt0)),
        "a_log": jnp.log(jax.random.uniform(ks[8], (DEPTH, SSM_HEADS), f32, 1.0, 16.0)),
        "d_skip": 1.0 + 0.1 * jax.random.normal(ks[9], (DEPTH, SSM_HEADS), f32),
        "ssm_norm_w": gain(ks[10], SSM_INNER),
        "w_att_proj": nrm(ks[11], (DEPTH, ATT_WIDTH, D_MODEL), ATT_WIDTH ** -0.5),
        "w_ssm_proj": nrm(ks[12], (DEPTH, SSM_INNER, D_MODEL), SSM_INNER ** -0.5),
        "w_out": nrm(ks[13], (DEPTH, D_MODEL, D_MODEL), D_MODEL ** -0.5),
        "norm_mix_post_w": gain(ks[14], D_MODEL),
        "norm_ffn_pre_w": gain(ks[15], D_MODEL),
        "w_up": nrm(ks[16], (DEPTH, D_MODEL, FFN_HIDDEN), D_MODEL ** -0.5),
        "w_down": nrm(ks[17], (DEPTH, FFN_HIDDEN, D_MODEL), FFN_HIDDEN ** -0.5),
        "norm_ffn_post_w": gain(ks[18], D_MODEL),
    }


def _fwd_reference(x, norm_mix_pre_w, w_in, b_gate, conv_w, conv_b, dt_bias, a_log, d_skip,
              ssm_norm_w, w_att_proj, w_ssm_proj, w_out, norm_mix_post_w, norm_ffn_pre_w,
              w_up, w_down, norm_ffn_post_w):
    b, s, _ = x.shape
    slopes = alibi_slopes(N_ATT_HEADS)
    offsets = [int(o) for o in np.cumsum(IN_SPLITS)[:-1]]
    h = x
    for layer in range(DEPTH):
        u = rmsnorm(h, norm_mix_pre_w[layer])
        proj = u @ w_in[layer]
        q, k, v, z, xbc, dt_raw, gate_logits = jnp.split(proj, offsets, axis=-1)
        q = q.reshape(b, s, N_ATT_HEADS, HEAD_DIM)
        k = k.reshape(b, s, N_ATT_HEADS, HEAD_DIM)
        v = v.reshape(b, s, N_ATT_HEADS, HEAD_DIM)
        outs, maxes, dens = [], [], []
        for window, dilation in DILATED_PATTERNS:
            o_g, m_g, l_g = dilated_window_attention(q, k, v, slopes, window, dilation)
            outs.append(o_g)
            maxes.append(m_g)
            dens.append(l_g)
        m_all = jnp.stack(maxes)
        wts = jnp.exp(m_all - jnp.max(m_all, axis=0, keepdims=True)) * jnp.stack(dens)
        att = jnp.sum(wts[..., None] * jnp.stack(outs), axis=0) / jnp.sum(wts, axis=0)[..., None]
        att = att.reshape(b, s, ATT_WIDTH).astype(x.dtype) @ w_att_proj[layer]
        ssm = mamba2_mixer(z, xbc, dt_raw, conv_w[layer], conv_b[layer], dt_bias[layer],
                           a_log[layer], d_skip[layer], ssm_norm_w[layer])
        ssm = ssm @ w_ssm_proj[layer]
        gates = jax.nn.sigmoid(gate_logits + b_gate[layer])
        g_att, g_ssm = jnp.split(gates, 2, axis=-1)
        mixed = (g_att * att + g_ssm * ssm) @ w_out[layer]
        h = h + rmsnorm(mixed, norm_mix_post_w[layer])
        f = rmsnorm(h, norm_ffn_pre_w[layer])
        f = jnp.square(jax.nn.relu(f @ w_up[layer])) @ w_down[layer]
        h = h + rmsnorm(f, norm_ffn_post_w[layer])
    return h


import jax as _jax
import jax.numpy as _jnp

TWIN_FORMAT = 'train_step'
FWD_PARAMS = ['x', 'norm_mix_pre_w', 'w_in', 'b_gate', 'conv_w', 'conv_b', 'dt_bias', 'a_log', 'd_skip', 'ssm_norm_w', 'w_att_proj', 'w_ssm_proj', 'w_out', 'norm_mix_post_w', 'norm_ffn_pre_w', 'w_up', 'w_down', 'norm_ffn_post_w']
TWIN_WEIGHTS = ['norm_mix_pre_w', 'w_in', 'b_gate', 'conv_w', 'conv_b', 'dt_bias', 'a_log', 'd_skip', 'ssm_norm_w', 'w_att_proj', 'w_ssm_proj', 'w_out', 'norm_mix_post_w', 'norm_ffn_pre_w', 'w_up', 'w_down', 'norm_ffn_post_w']
TWIN_DIFF_INPUT = 'x'
TWIN_INPUTS = ['x', 'norm_mix_pre_w', 'w_in', 'b_gate', 'conv_w', 'conv_b', 'dt_bias', 'a_log', 'd_skip', 'ssm_norm_w', 'w_att_proj', 'w_ssm_proj', 'w_out', 'norm_mix_post_w', 'norm_ffn_pre_w', 'w_up', 'w_down', 'norm_ffn_post_w', 'loss_target', 'm_norm_mix_pre_w', 'm_w_in', 'm_b_gate', 'm_conv_w', 'm_conv_b', 'm_dt_bias', 'm_a_log', 'm_d_skip', 'm_ssm_norm_w', 'm_w_att_proj', 'm_w_ssm_proj', 'm_w_out', 'm_norm_mix_post_w', 'm_norm_ffn_pre_w', 'm_w_up', 'm_w_down', 'm_norm_ffn_post_w', 'v_norm_mix_pre_w', 'v_w_in', 'v_b_gate', 'v_conv_w', 'v_conv_b', 'v_dt_bias', 'v_a_log', 'v_d_skip', 'v_ssm_norm_w', 'v_w_att_proj', 'v_w_ssm_proj', 'v_w_out', 'v_norm_mix_post_w', 'v_norm_ffn_pre_w', 'v_w_up', 'v_w_down', 'v_norm_ffn_post_w']
TWIN_OUTPUTS = ['loss', 'grad_x', 'grad_norm_mix_pre_w', 'grad_w_in', 'grad_b_gate', 'grad_conv_w', 'grad_conv_b', 'grad_dt_bias', 'grad_a_log', 'grad_d_skip', 'grad_ssm_norm_w', 'grad_w_att_proj', 'grad_w_ssm_proj', 'grad_w_out', 'grad_norm_mix_post_w', 'grad_norm_ffn_pre_w', 'grad_w_up', 'grad_w_down', 'grad_norm_ffn_post_w', 'delta_norm_mix_pre_w', 'delta_w_in', 'delta_b_gate', 'delta_conv_w', 'delta_conv_b', 'delta_dt_bias', 'delta_a_log', 'delta_d_skip', 'delta_ssm_norm_w', 'delta_w_att_proj', 'delta_w_ssm_proj', 'delta_w_out', 'delta_norm_mix_post_w', 'delta_norm_ffn_pre_w', 'delta_w_up', 'delta_w_down', 'delta_norm_ffn_post_w', 'new_m_norm_mix_pre_w', 'new_m_w_in', 'new_m_b_gate', 'new_m_conv_w', 'new_m_conv_b', 'new_m_dt_bias', 'new_m_a_log', 'new_m_d_skip', 'new_m_ssm_norm_w', 'new_m_w_att_proj', 'new_m_w_ssm_proj', 'new_m_w_out', 'new_m_norm_mix_post_w', 'new_m_norm_ffn_pre_w', 'new_m_w_up', 'new_m_w_down', 'new_m_norm_ffn_post_w', 'new_v_norm_mix_pre_w', 'new_v_w_in', 'new_v_b_gate', 'new_v_conv_w', 'new_v_conv_b', 'new_v_dt_bias', 'new_v_a_log', 'new_v_d_skip', 'new_v_ssm_norm_w', 'new_v_w_att_proj', 'new_v_w_ssm_proj', 'new_v_w_out', 'new_v_norm_mix_post_w', 'new_v_norm_ffn_pre_w', 'new_v_w_up', 'new_v_w_down', 'new_v_norm_ffn_post_w']
TWIN_LEAF_KINDS = {'loss': 'loss', 'grad_x': 'grad_x', 'grad_norm_mix_pre_w': 'grad_w', 'grad_w_in': 'grad_w', 'grad_b_gate': 'grad_w', 'grad_conv_w': 'grad_w', 'grad_conv_b': 'grad_w', 'grad_dt_bias': 'grad_w', 'grad_a_log': 'grad_w', 'grad_d_skip': 'grad_w', 'grad_ssm_norm_w': 'grad_w', 'grad_w_att_proj': 'grad_w', 'grad_w_ssm_proj': 'grad_w', 'grad_w_out': 'grad_w', 'grad_norm_mix_post_w': 'grad_w', 'grad_norm_ffn_pre_w': 'grad_w', 'grad_w_up': 'grad_w', 'grad_w_down': 'grad_w', 'grad_norm_ffn_post_w': 'grad_w', 'delta_norm_mix_pre_w': 'delta_w', 'delta_w_in': 'delta_w', 'delta_b_gate': 'delta_w', 'delta_conv_w': 'delta_w', 'delta_conv_b': 'delta_w', 'delta_dt_bias': 'delta_w', 'delta_a_log': 'delta_w', 'delta_d_skip': 'delta_w', 'delta_ssm_norm_w': 'delta_w', 'delta_w_att_proj': 'delta_w', 'delta_w_ssm_proj': 'delta_w', 'delta_w_out': 'delta_w', 'delta_norm_mix_post_w': 'delta_w', 'delta_norm_ffn_pre_w': 'delta_w', 'delta_w_up': 'delta_w', 'delta_w_down': 'delta_w', 'delta_norm_ffn_post_w': 'delta_w', 'new_m_norm_mix_pre_w': 'new_m', 'new_m_w_in': 'new_m', 'new_m_b_gate': 'new_m', 'new_m_conv_w': 'new_m', 'new_m_conv_b': 'new_m', 'new_m_dt_bias': 'new_m', 'new_m_a_log': 'new_m', 'new_m_d_skip': 'new_m', 'new_m_ssm_norm_w': 'new_m', 'new_m_w_att_proj': 'new_m', 'new_m_w_ssm_proj': 'new_m', 'new_m_w_out': 'new_m', 'new_m_norm_mix_post_w': 'new_m', 'new_m_norm_ffn_pre_w': 'new_m', 'new_m_w_up': 'new_m', 'new_m_w_down': 'new_m', 'new_m_norm_ffn_post_w': 'new_m', 'new_v_norm_mix_pre_w': 'new_v', 'new_v_w_in': 'new_v', 'new_v_b_gate': 'new_v', 'new_v_conv_w': 'new_v', 'new_v_conv_b': 'new_v', 'new_v_dt_bias': 'new_v', 'new_v_a_log': 'new_v', 'new_v_d_skip': 'new_v', 'new_v_ssm_norm_w': 'new_v', 'new_v_w_att_proj': 'new_v', 'new_v_w_ssm_proj': 'new_v', 'new_v_w_out': 'new_v', 'new_v_norm_mix_post_w': 'new_v', 'new_v_norm_ffn_pre_w': 'new_v', 'new_v_w_up': 'new_v', 'new_v_w_down': 'new_v', 'new_v_norm_ffn_post_w': 'new_v'}


def _forward(args):
    return _fwd_reference(*[args[k] for k in FWD_PARAMS])


def _output_shape():
    def fwd():
        inp = _fwd_setup_inputs(0)
        return _fwd_reference(*[inp[k] for k in FWD_PARAMS])
    out = _jax.eval_shape(fwd)
    return out.shape, out.dtype

N_MICROBATCH = 1
ADAM_LR = 0.001
ADAM_B1 = 0.9
ADAM_B2 = 0.999
ADAM_EPS = 1e-08
ADAM_WD = 0.01
ADAM_STEP = 10
PER_EXAMPLE_BATCH_AXIS = {'x': 0, 'loss_target': 0}
SHARED_INPUTS = []
_WEIGHT_DTYPES = {'norm_mix_pre_w': _jnp.float32, 'w_in': _jnp.float32, 'b_gate': _jnp.float32, 'conv_w': _jnp.float32, 'conv_b': _jnp.float32, 'dt_bias': _jnp.float32, 'a_log': _jnp.float32, 'd_skip': _jnp.float32, 'ssm_norm_w': _jnp.float32, 'w_att_proj': _jnp.float32, 'w_ssm_proj': _jnp.float32, 'w_out': _jnp.float32, 'norm_mix_post_w': _jnp.float32, 'norm_ffn_pre_w': _jnp.float32, 'w_up': _jnp.float32, 'w_down': _jnp.float32, 'norm_ffn_post_w': _jnp.float32}
MOMENT_SCALE = {'norm_mix_pre_w': 1.224559e+00, 'w_in': 3.535636e-01, 'b_gate': 1.364060e+00, 'conv_w': 1.278314e+00, 'conv_b': 5.043083e+00, 'dt_bias': 7.203967e-01, 'a_log': 8.280541e+00, 'd_skip': 5.624492e+00, 'ssm_norm_w': 2.939272e+00, 'w_att_proj': 5.130359e-01, 'w_ssm_proj': 4.395643e+00, 'w_out': 4.676690e+00, 'norm_mix_post_w': 6.412086e+01, 'norm_ffn_pre_w': 1.876671e+00, 'w_up': 9.705135e-01, 'w_down': 5.031481e+00, 'norm_ffn_post_w': 6.594628e+01}


def _to_microbatches(a, axis):
    t = _jnp.moveaxis(a, axis, 0)
    t = t.reshape((N_MICROBATCH, t.shape[0] // N_MICROBATCH) + t.shape[1:])
    return _jnp.moveaxis(t, 1, axis + 1)


def setup_inputs(seed: int = 0) -> dict:
    inp = _fwd_setup_inputs(seed)
    key = _jax.random.fold_in(_jax.random.key(seed), 7919)
    shape, _ = _output_shape()
    out = dict(inp)
    out["loss_target"] = _jax.random.normal(_jax.random.fold_in(key, 0), shape, _jnp.float32)
    for i, name in enumerate(TWIN_WEIGHTS):
        w = inp[name].astype(_jnp.float32)
        if MOMENT_SCALE is None:
            s = _jnp.sqrt(_jnp.mean(_jnp.square(w)) + 1e-30)
        else:
            s = MOMENT_SCALE[name]
        km, kv = _jax.random.split(_jax.random.fold_in(key, i + 1))
        out[name] = w
        out["m_" + name] = s * _jax.random.normal(km, w.shape, _jnp.float32)
        out["v_" + name] = (s * s) * _jax.random.uniform(kv, w.shape, _jnp.float32, 0.5, 1.5)
    if N_MICROBATCH > 1:
        for name, axis in PER_EXAMPLE_BATCH_AXIS.items():
            out[name] = _to_microbatches(out[name], axis)
    return {'x': out['x'], 'norm_mix_pre_w': out['norm_mix_pre_w'], 'w_in': out['w_in'], 'b_gate': out['b_gate'], 'conv_w': out['conv_w'], 'conv_b': out['conv_b'], 'dt_bias': out['dt_bias'], 'a_log': out['a_log'], 'd_skip': out['d_skip'], 'ssm_norm_w': out['ssm_norm_w'], 'w_att_proj': out['w_att_proj'], 'w_ssm_proj': out['w_ssm_proj'], 'w_out': out['w_out'], 'norm_mix_post_w': out['norm_mix_post_w'], 'norm_ffn_pre_w': out['norm_ffn_pre_w'], 'w_up': out['w_up'], 'w_down': out['w_down'], 'norm_ffn_post_w': out['norm_ffn_post_w'], 'loss_target': out['loss_target'], 'm_norm_mix_pre_w': out['m_norm_mix_pre_w'], 'm_w_in': out['m_w_in'], 'm_b_gate': out['m_b_gate'], 'm_conv_w': out['m_conv_w'], 'm_conv_b': out['m_conv_b'], 'm_dt_bias': out['m_dt_bias'], 'm_a_log': out['m_a_log'], 'm_d_skip': out['m_d_skip'], 'm_ssm_norm_w': out['m_ssm_norm_w'], 'm_w_att_proj': out['m_w_att_proj'], 'm_w_ssm_proj': out['m_w_ssm_proj'], 'm_w_out': out['m_w_out'], 'm_norm_mix_post_w': out['m_norm_mix_post_w'], 'm_norm_ffn_pre_w': out['m_norm_ffn_pre_w'], 'm_w_up': out['m_w_up'], 'm_w_down': out['m_w_down'], 'm_norm_ffn_post_w': out['m_norm_ffn_post_w'], 'v_norm_mix_pre_w': out['v_norm_mix_pre_w'], 'v_w_in': out['v_w_in'], 'v_b_gate': out['v_b_gate'], 'v_conv_w': out['v_conv_w'], 'v_conv_b': out['v_conv_b'], 'v_dt_bias': out['v_dt_bias'], 'v_a_log': out['v_a_log'], 'v_d_skip': out['v_d_skip'], 'v_ssm_norm_w': out['v_ssm_norm_w'], 'v_w_att_proj': out['v_w_att_proj'], 'v_w_ssm_proj': out['v_w_ssm_proj'], 'v_w_out': out['v_w_out'], 'v_norm_mix_post_w': out['v_norm_mix_post_w'], 'v_norm_ffn_pre_w': out['v_norm_ffn_pre_w'], 'v_w_up': out['v_w_up'], 'v_w_down': out['v_w_down'], 'v_norm_ffn_post_w': out['v_norm_ffn_post_w']}


def _loss(weights, diff, rest, loss_target):
    with _jax.named_scope("forward"):
        args = {**rest, TWIN_DIFF_INPUT: diff, **{k: w.astype(_WEIGHT_DTYPES[k]) for k, w in weights.items()}}
        y = _forward(args)
    with _jax.named_scope("loss_head"):
        err = _jnp.square(y.astype(_jnp.float32) - loss_target)
        return 0.5 * _jnp.sum(_jnp.mean(err, axis=-1)) if err.ndim else 0.5 * err


def _adamw(w, g, m, v):
    m = ADAM_B1 * m + (1.0 - ADAM_B1) * g
    v = ADAM_B2 * v + (1.0 - ADAM_B2) * _jnp.square(g)
    m_hat = m / (1.0 - ADAM_B1 ** ADAM_STEP)
    v_hat = v / (1.0 - ADAM_B2 ** ADAM_STEP)
    delta = -ADAM_LR * (m_hat / (_jnp.sqrt(v_hat) + ADAM_EPS) + ADAM_WD * w)
    return delta, m, v


def reference(x, norm_mix_pre_w, w_in, b_gate, conv_w, conv_b, dt_bias, a_log, d_skip, ssm_norm_w, w_att_proj, w_ssm_proj, w_out, norm_mix_post_w, norm_ffn_pre_w, w_up, w_down, norm_ffn_post_w, loss_target, m_norm_mix_pre_w, m_w_in, m_b_gate, m_conv_w, m_conv_b, m_dt_bias, m_a_log, m_d_skip, m_ssm_norm_w, m_w_att_proj, m_w_ssm_proj, m_w_out, m_norm_mix_post_w, m_norm_ffn_pre_w, m_w_up, m_w_down, m_norm_ffn_post_w, v_norm_mix_pre_w, v_w_in, v_b_gate, v_conv_w, v_conv_b, v_dt_bias, v_a_log, v_d_skip, v_ssm_norm_w, v_w_att_proj, v_w_ssm_proj, v_w_out, v_norm_mix_post_w, v_norm_ffn_pre_w, v_w_up, v_w_down, v_norm_ffn_post_w):
    given = dict(x=x, norm_mix_pre_w=norm_mix_pre_w, w_in=w_in, b_gate=b_gate, conv_w=conv_w, conv_b=conv_b, dt_bias=dt_bias, a_log=a_log, d_skip=d_skip, ssm_norm_w=ssm_norm_w, w_att_proj=w_att_proj, w_ssm_proj=w_ssm_proj, w_out=w_out, norm_mix_post_w=norm_mix_post_w, norm_ffn_pre_w=norm_ffn_pre_w, w_up=w_up, w_down=w_down, norm_ffn_post_w=norm_ffn_post_w, loss_target=loss_target, m_norm_mix_pre_w=m_norm_mix_pre_w, m_w_in=m_w_in, m_b_gate=m_b_gate, m_conv_w=m_conv_w, m_conv_b=m_conv_b, m_dt_bias=m_dt_bias, m_a_log=m_a_log, m_d_skip=m_d_skip, m_ssm_norm_w=m_ssm_norm_w, m_w_att_proj=m_w_att_proj, m_w_ssm_proj=m_w_ssm_proj, m_w_out=m_w_out, m_norm_mix_post_w=m_norm_mix_post_w, m_norm_ffn_pre_w=m_norm_ffn_pre_w, m_w_up=m_w_up, m_w_down=m_w_down, m_norm_ffn_post_w=m_norm_ffn_post_w, v_norm_mix_pre_w=v_norm_mix_pre_w, v_w_in=v_w_in, v_b_gate=v_b_gate, v_conv_w=v_conv_w, v_conv_b=v_conv_b, v_dt_bias=v_dt_bias, v_a_log=v_a_log, v_d_skip=v_d_skip, v_ssm_norm_w=v_ssm_norm_w, v_w_att_proj=v_w_att_proj, v_w_ssm_proj=v_w_ssm_proj, v_w_out=v_w_out, v_norm_mix_post_w=v_norm_mix_post_w, v_norm_ffn_pre_w=v_norm_ffn_pre_w, v_w_up=v_w_up, v_w_down=v_w_down, v_norm_ffn_post_w=v_norm_ffn_post_w)
    weights = {n: given[n] for n in TWIN_WEIGHTS}
    shared = {n: given[n] for n in SHARED_INPUTS}
    per_example = {n: given[n] for n in ['x']}
    grad_fn = _jax.value_and_grad(_loss, argnums=(0, 1))

    def one_microbatch(ex, loss_target):
        ex = dict(ex)
        diff = ex.pop(TWIN_DIFF_INPUT)
        return grad_fn(weights, diff, {**shared, **ex}, loss_target)

    if N_MICROBATCH == 1:
        loss, (grad_w, grad_x) = one_microbatch(per_example, given["loss_target"])
    else:
        def body(carry, xs):
            loss_sum, grad_sum = carry
            l_k, (gw_k, gx_k) = one_microbatch(xs[0], xs[1])
            with _jax.named_scope("update"):
                return (loss_sum + l_k, _jax.tree.map(_jnp.add, grad_sum, gw_k)), gx_k

        init = (_jnp.zeros((), _jnp.float32), _jax.tree.map(_jnp.zeros_like, weights))
        (loss, grad_w), grad_x = _jax.lax.scan(body, init, (per_example, given["loss_target"]))
    with _jax.named_scope("update"):
        delta_w, new_m, new_v = {}, {}, {}
        for n in TWIN_WEIGHTS:
            delta_w[n], new_m[n], new_v[n] = _adamw(weights[n], grad_w[n], given["m_" + n], given["v_" + n])
    return (loss, grad_x, *[grad_w[n] for n in TWIN_WEIGHTS], *[delta_w[n] for n in TWIN_WEIGHTS],
            *[new_m[n] for n in TWIN_WEIGHTS], *[new_v[n] for n in TWIN_WEIGHTS])
```

```python
import functools
import math

import numpy as np
import jax
import jax.numpy as jnp
from jax import lax
from jax.experimental import pallas as pl
from jax.experimental.pallas import tpu as pltpu

F32 = jnp.float32
BF16 = jnp.bfloat16

D_MODEL = 1024
HEAD_DIM = 64
N_ATT_HEADS = 12
ATT_WIDTH = N_ATT_HEADS * HEAD_DIM
DILATIONS = (1, 4, 16)
ATT_BLOCK = 128
SSM_INNER = 2048
SSM_HEADS = 32
SSM_GROUPS = 8
HEADS_PER_GROUP = SSM_HEADS // SSM_GROUPS
SSM_HEAD_DIM = 64
SSM_STATE = 128
SSM_CONV = 4
SSM_CHUNK = 128
CONV_DIM = SSM_INNER + 2 * SSM_GROUPS * SSM_STATE
FFN_HIDDEN = 4 * D_MODEL
IN_SPLITS = (ATT_WIDTH, ATT_WIDTH, ATT_WIDTH, SSM_INNER, CONV_DIM, SSM_HEADS, 2 * D_MODEL)
IN_PROJ_WIDTH = sum(IN_SPLITS)
RMS_EPS = 1e-6
LANES = 128
NEG_BIG = -1e30

ADAM_LR = 0.001
ADAM_B1 = 0.9
ADAM_B2 = 0.999
ADAM_EPS = 1e-08
ADAM_WD = 0.01
ADAM_STEP = 10

N_CHIPS = 4
N_DEV = 8
VMEM_LIMIT = 56 * 1024 * 1024
MESH = pl.DeviceIdType.MESH


def _alibi_slopes(n):
    def pow2(m):
        start = 2.0 ** (-8.0 / m)
        return [start ** (i + 1) for i in range(m)]
    if (n & (n - 1)) == 0:
        s = pow2(n)
    else:
        c = 2 ** int(math.floor(math.log2(n)))
        s = pow2(c) + pow2(2 * c)[0::2][: n - c]
    return [float(v) for v in np.array(s, dtype=np.float32)]


def _params(sem):
    return pltpu.CompilerParams(dimension_semantics=sem, vmem_limit_bytes=VMEM_LIMIT)


def _dot(a, b):
    return lax.dot_general(a, b, (((1,), (0,)), ((), ())), preferred_element_type=F32)


def _dot_nt(a, b):
    return lax.dot_general(a, b, (((1,), (1,)), ((), ())), preferred_element_type=F32)


def _dot_tn(a, b):
    return lax.dot_general(a, b, (((0,), (0,)), ((), ())), preferred_element_type=F32)


def _dot_hi(a, b):
    return lax.dot_general(a, b, (((1,), (0,)), ((), ())), preferred_element_type=F32,
                           precision=lax.Precision.HIGHEST)


def _dot_tn_hi(a, b):
    return lax.dot_general(a, b, (((0,), (0,)), ((), ())), preferred_element_type=F32,
                           precision=lax.Precision.HIGHEST)


def _b(x):
    return x.astype(BF16)


def _sigmoid(x):
    return 1.0 / (1.0 + jnp.exp(-x))


def _pick(n, cands):
    for c in cands:
        if n % c == 0:
            return c
    raise ValueError(f"no tile for {n}")


def _mm_nn(a, b, out_dtype, name, acc=None):
    M, K = a.shape
    _, N = b.shape
    tm = 512
    tn = _pick(N, (1024, 768, 512, 256, 128))
    tk = K if K <= 2304 else _pick(K, (2048, 1024))
    nk = K // tk
    has_acc = acc is not None

    def body(*refs):
        if has_acc:
            a_ref, b_ref, c_ref, o_ref, acc_ref = refs
        else:
            a_ref, b_ref, o_ref, acc_ref = refs
        k = pl.program_id(2)
        part = _dot(_b(a_ref[...]), _b(b_ref[...]))

        @pl.when(k == 0)
        def _():
            acc_ref[...] = part

        @pl.when(k > 0)
        def _():
            acc_ref[...] += part

        @pl.when(k == nk - 1)
        def _():
            r = acc_ref[...]
            if has_acc:
                r = r + c_ref[...]
            o_ref[...] = r.astype(out_dtype)

    in_specs = [pl.BlockSpec((tm, tk), lambda j, i, k: (i, k)),
                pl.BlockSpec((tk, tn), lambda j, i, k: (k, j))]
    args = [a, b]
    if has_acc:
        in_specs.append(pl.BlockSpec((tm, tn), lambda j, i, k: (i, j)))
        args.append(acc)
    return pl.pallas_call(
        body, name=name, grid=(N // tn, M // tm, nk),
        in_specs=in_specs,
        out_specs=pl.BlockSpec((tm, tn), lambda j, i, k: (i, j)),
        out_shape=jax.ShapeDtypeStruct((M, N), out_dtype),
        scratch_shapes=[pltpu.VMEM((tm, tn), F32)],
        compiler_params=_params(("parallel", "parallel", "arbitrary")),
    )(*args)


def _mm_tn(a, b, name):
    S, Ka = a.shape
    _, N = b.shape
    tka = _pick(Ka, (1024, 768, 512))
    tn = _pick(N, (1024, 768, 512, 256, 128))
    ts = 1024 if S % 1024 == 0 else 512
    ns = S // ts

    def body(a_ref, b_ref, o_ref, acc_ref):
        s = pl.program_id(2)
        part = _dot_tn(_b(a_ref[...]), _b(b_ref[...]))

        @pl.when(s == 0)
        def _():
            acc_ref[...] = part

        @pl.when(s > 0)
        def _():
            acc_ref[...] += part

        @pl.when(s == ns - 1)
        def _():
            o_ref[...] = acc_ref[...]

    return pl.pallas_call(
        body, name=name, grid=(Ka // tka, N // tn, ns),
        in_specs=[pl.BlockSpec((ts, tka), lambda i, j, s: (s, i)),
                  pl.BlockSpec((ts, tn), lambda i, j, s: (s, j))],
        out_specs=pl.BlockSpec((tka, tn), lambda i, j, s: (i, j)),
        out_shape=jax.ShapeDtypeStruct((Ka, N), F32),
        scratch_shapes=[pltpu.VMEM((tka, tn), F32)],
        compiler_params=_params(("parallel", "parallel", "arbitrary")),
    )(a, b)


def _row_call(body, row_ins, full_ins, row_outs, acc_outs, bs, name):
    S = row_ins[0].shape[0]
    assert S % bs == 0
    in_specs = [pl.BlockSpec((bs, a.shape[1]), lambda i: (i, 0)) for a in row_ins]
    in_specs += [pl.BlockSpec(a.shape, lambda i: (0, 0)) for a in full_ins]
    out_specs = [pl.BlockSpec((bs, c), lambda i: (i, 0)) for c, _ in row_outs]
    out_specs += [pl.BlockSpec(s, lambda i: (0, 0)) for s in acc_outs]
    out_shape = [jax.ShapeDtypeStruct((S, c), dt) for c, dt in row_outs]
    out_shape += [jax.ShapeDtypeStruct(s, F32) for s in acc_outs]
    return pl.pallas_call(
        body, name=name, grid=(S // bs,), in_specs=in_specs, out_specs=out_specs, out_shape=out_shape,
        compiler_params=_params(("arbitrary",)),
    )(*row_ins, *full_ins)


def _rms_vals(x, w):
    r = lax.rsqrt(jnp.mean(x * x, axis=-1, keepdims=True) + RMS_EPS)
    return x * r * w


def _rms_bwd_vals(x, w, dy):
    r = lax.rsqrt(jnp.mean(x * x, axis=-1, keepdims=True) + RMS_EPS)
    xn = x * r
    g = dy * w
    dx = r * (g - xn * jnp.mean(g * xn, axis=-1, keepdims=True))
    dw = jnp.sum(dy * xn, axis=0, keepdims=True)
    return dx, dw


def _acc_add(ref, val):
    @pl.when(pl.program_id(0) == 0)
    def _():
        ref[...] = val

    @pl.when(pl.program_id(0) > 0)
    def _():
        ref[...] += val


def _rms_fwd(x, w):
    def body(x_ref, w_ref, o_ref):
        o_ref[...] = _b(_rms_vals(x_ref[...], w_ref[...]))
    return _row_call(body, [x], [w], [(x.shape[1], BF16)], [], 512, "rms_fwd")[0]


def _gate_fwd(att_o, ssm_o, gl, b_gate):
    def body(a_ref, s_ref, g_ref, b_ref, o_ref):
        g = _sigmoid(g_ref[...] + b_ref[...])
        o_ref[...] = _b(g[:, :D_MODEL] * a_ref[...] + g[:, D_MODEL:] * s_ref[...])
    return _row_call(body, [att_o, ssm_o, gl], [b_gate], [(D_MODEL, BF16)], [], 512, "gate_fwd")[0]


def _post_pre(x, mixed, w_post, w_pre):
    def body(x_ref, m_ref, wp_ref, wn_ref, h_ref, f_ref):
        h = x_ref[...] + _rms_vals(m_ref[...], wp_ref[...])
        h_ref[...] = h
        f_ref[...] = _b(_rms_vals(h, wn_ref[...]))
    return _row_call(body, [x, mixed], [w_post, w_pre], [(D_MODEL, F32), (D_MODEL, BF16)], [], 512,
                     "post_pre")


def _relu2(up):
    def body(u_ref, o_ref):
        r = jnp.maximum(u_ref[...], 0.0)
        o_ref[...] = _b(r * r)
    return _row_call(body, [up], [], [(up.shape[1], BF16)], [], 256, "relu2")[0]


def _final(h1, down, w_post, target):
    def body(h_ref, d_ref, t_ref, w_ref, dh_ref, dd_ref, loss_ref, dw_ref):
        dn = d_ref[...]
        w = w_ref[...]
        err = h_ref[...] + _rms_vals(dn, w) - t_ref[...]
        row = jnp.mean(err * err, axis=-1, keepdims=True)
        part = 0.5 * jnp.sum(row, axis=0, keepdims=True)
        dh = err * (1.0 / D_MODEL)
        dh_ref[...] = dh
        dx, dw = _rms_bwd_vals(dn, w, dh)
        dd_ref[...] = _b(dx)
        _acc_add(loss_ref, jnp.broadcast_to(part, (1, LANES)))
        _acc_add(dw_ref, dw)
    return _row_call(body, [h1, down, target], [w_post], [(D_MODEL, F32), (D_MODEL, BF16)],
                     [(1, LANES), (1, D_MODEL)], 512, "final_loss")


def _dup(da, up):
    def body(a_ref, u_ref, o_ref):
        o_ref[...] = _b(a_ref[...] * (2.0 * jnp.maximum(u_ref[...], 0.0)))
    return _row_call(body, [da, up], [], [(up.shape[1], BF16)], [], 256, "relu2_bwd")[0]


def _mid_bwd(dh2, df, h1, mixed, w_pre, w_post):
    def body(dh_ref, df_ref, h_ref, m_ref, wn_ref, wp_ref, dh1_ref, dm_ref, dwn_ref, dwp_ref):
        dx, dwn = _rms_bwd_vals(h_ref[...], wn_ref[...], df_ref[...])
        dh1 = dh_ref[...] + dx
        dh1_ref[...] = dh1
        dm, dwp = _rms_bwd_vals(m_ref[...], wp_ref[...], dh1)
        dm_ref[...] = _b(dm)
        _acc_add(dwn_ref, dwn)
        _acc_add(dwp_ref, dwp)
    return _row_call(body, [dh2, df, h1, mixed], [w_pre, w_post], [(D_MODEL, F32), (D_MODEL, BF16)],
                     [(1, D_MODEL), (1, D_MODEL)], 512, "mid_bwd")


def _gate_bwd(dmi, att_o, ssm_o, gl, b_gate):
    def body(d_ref, a_ref, s_ref, g_ref, b_ref, da_ref, ds_ref, dg_ref, db_ref):
        g = _sigmoid(g_ref[...] + b_ref[...])
        d = d_ref[...]
        ga, gs = g[:, :D_MODEL], g[:, D_MODEL:]
        da_ref[...] = _b(ga * d)
        ds_ref[...] = _b(gs * d)
        dga = d * a_ref[...] * ga * (1.0 - ga)
        dgs = d * s_ref[...] * gs * (1.0 - gs)
        dg_ref[:, :D_MODEL] = _b(dga)
        dg_ref[:, D_MODEL:] = _b(dgs)
        _acc_add(db_ref.at[:, pl.ds(0, D_MODEL)], jnp.sum(dga, axis=0, keepdims=True))
        _acc_add(db_ref.at[:, pl.ds(D_MODEL, D_MODEL)], jnp.sum(dgs, axis=0, keepdims=True))
    return _row_call(body, [dmi, att_o, ssm_o, gl], [b_gate],
                     [(D_MODEL, BF16), (D_MODEL, BF16), (2 * D_MODEL, BF16)], [(1, 2 * D_MODEL)], 256,
                     "gate_bwd")


def _first_bwd(dh1, du, x, w_pre):
    def body(dh_ref, du_ref, x_ref, w_ref, dx_ref, dw_ref):
        dx, dw = _rms_bwd_vals(x_ref[...], w_ref[...], du_ref[...])
        dx_ref[...] = dh_ref[...] + dx
        _acc_add(dw_ref, dw)
    return _row_call(body, [dh1, du, x], [w_pre], [(D_MODEL, F32)], [(1, D_MODEL)], 512, "first_bwd")


def _group_rms(t):
    gw = SSM_INNER // SSM_GROUPS
    out = []
    for g in range(SSM_GROUPS):
        tg = t[:, g * gw:(g + 1) * gw]
        out.append(lax.rsqrt(jnp.mean(tg * tg, axis=-1, keepdims=True) + RMS_EPS))
    return out


def _gnorm_fwd(y, z, w):
    gw = SSM_INNER // SSM_GROUPS

    def body(y_ref, z_ref, w_ref, o_ref):
        zz = z_ref[...]
        t = y_ref[...] * (zz * _sigmoid(zz))
        rs = _group_rms(t)
        for g in range(SSM_GROUPS):
            sl = slice(g * gw, (g + 1) * gw)
            o_ref[:, sl] = _b(t[:, sl] * rs[g] * w_ref[:, sl])
    return _row_call(body, [y, z], [w], [(SSM_INNER, BF16)], [], 256, "gnorm_fwd")[0]


def _gnorm_bwd(dout, y, z, w):
    gw = SSM_INNER // SSM_GROUPS

    def body(d_ref, y_ref, z_ref, w_ref, dy_ref, dz_ref, dw_ref):
        zz = z_ref[...]
        yy = y_ref[...]
        sg = _sigmoid(zz)
        sz = zz * sg
        t = yy * sz
        rs = _group_rms(t)
        for g in range(SSM_GROUPS):
            sl = slice(g * gw, (g + 1) * gw)
            tn = t[:, sl] * rs[g]
            d = d_ref[:, sl]
            gg = d * w_ref[:, sl]
            dt = rs[g] * (gg - tn * jnp.mean(gg * tn, axis=-1, keepdims=True))
            dy_ref[:, sl] = dt * sz[:, sl]
            dz_ref[:, sl] = _b(dt * yy[:, sl] * (sg[:, sl] * (1.0 + zz[:, sl] * (1.0 - sg[:, sl]))))
            _acc_add(dw_ref.at[:, pl.ds(g * gw, gw)], jnp.sum(d * tn, axis=0, keepdims=True))
    return _row_call(body, [dout, y, z], [w], [(SSM_INNER, F32), (SSM_INNER, BF16)], [(1, SSM_INNER)], 256,
                     "gnorm_bwd")


def _to_pat(a, d):
    if d == 1:
        return a
    S, C = a.shape
    return a.reshape(S // d, d, C).transpose(1, 0, 2).reshape(S, C)


def _from_pat(a, d):
    if d == 1:
        return a
    S, C = a.shape
    return a.reshape(d, S // d, C).transpose(1, 0, 2).reshape(S, C)


def _head_col(stat, h):
    return stat[:, h:h + 1]


def _attn_fwd(q, k, v, d):
    S = q.shape[0]
    blk = ATT_BLOCK
    nblk = S // blk
    nbs = nblk // d
    slopes = _alibi_slopes(N_ATT_HEADS)
    scale = HEAD_DIM ** -0.5

    def body(q_ref, kc_ref, kp_ref, vc_ref, vp_ref, o_ref, m_ref, l_ref):
        n = pl.program_id(0)
        has_prev = (n % nbs) != 0
        ii = lax.broadcasted_iota(jnp.int32, (blk, blk), 0)
        jj = lax.broadcasted_iota(jnp.int32, (blk, blk), 1)
        dist_c = (ii - jj).astype(F32)
        dist_p = dist_c + float(blk)
        ok_c = ii >= jj
        ok_p = jnp.logical_and(jj >= ii, has_prev)
        lane = lax.broadcasted_iota(jnp.int32, (blk, LANES), 1)
        m_all = jnp.zeros((blk, LANES), F32)
        l_all = jnp.zeros((blk, LANES), F32)
        for h in range(N_ATT_HEADS):
            sl = slice(h * HEAD_DIM, (h + 1) * HEAD_DIM)
            qh = q_ref[:, sl]
            bias = slopes[h] * float(d)
            sc = jnp.where(ok_c, _dot_nt(qh, kc_ref[:, sl]) * scale - bias * dist_c, NEG_BIG)
            sp = jnp.where(ok_p, _dot_nt(qh, kp_ref[:, sl]) * scale - bias * dist_p, NEG_BIG)
            m = jnp.maximum(jnp.max(sc, axis=-1, keepdims=True), jnp.max(sp, axis=-1, keepdims=True))
            pc = jnp.exp(sc - m)
            pp = jnp.exp(sp - m)
            l = jnp.sum(pc, axis=-1, keepdims=True) + jnp.sum(pp, axis=-1, keepdims=True)
            o_ref[:, sl] = _dot(_b(pc), vc_ref[:, sl]) + _dot(_b(pp), vp_ref[:, sl])
            m_all = jnp.where(lane == h, m, m_all)
            l_all = jnp.where(lane == h, l, l_all)
        m_ref[...] = m_all
        l_ref[...] = l_all

    cur = pl.BlockSpec((blk, ATT_WIDTH), lambda n: (n, 0))
    prev = pl.BlockSpec((blk, ATT_WIDTH), lambda n: (jnp.maximum(n - 1, 0), 0))
    stat = pl.BlockSpec((blk, LANES), lambda n: (n, 0))
    return pl.pallas_call(
        body, name=f"attn_fwd_d{d}", grid=(nblk,),
        in_specs=[cur, cur, prev, cur, prev],
        out_specs=[cur, stat, stat],
        out_shape=[jax.ShapeDtypeStruct((S, ATT_WIDTH), F32), jax.ShapeDtypeStruct((S, LANES), F32),
                   jax.ShapeDtypeStruct((S, LANES), F32)],
        compiler_params=_params(("parallel",)),
    )(q, k, k, v, v)


def _attn_combine(os, ms, ls):
    def body(o1, o2, o3, m1, m2, m3, l1, l2, l3, att_ref, lse_ref):
        mm = [m1[...], m2[...], m3[...]]
        big = jnp.maximum(jnp.maximum(mm[0], mm[1]), mm[2])
        es = [jnp.exp(m - big) for m in mm]
        den = es[0] * l1[...] + es[1] * l2[...] + es[2] * l3[...]
        lse_ref[...] = big + jnp.log(den)
        inv = 1.0 / den
        for h in range(N_ATT_HEADS):
            sl = slice(h * HEAD_DIM, (h + 1) * HEAD_DIM)
            num = (_head_col(es[0], h) * o1[:, sl] + _head_col(es[1], h) * o2[:, sl]
                   + _head_col(es[2], h) * o3[:, sl])
            att_ref[:, sl] = num * _head_col(inv, h)
    return _row_call(body, list(os) + list(ms) + list(ls), [], [(ATT_WIDTH, F32), (LANES, F32)], [], 256,
                     "attn_combine")


def _attn_delta(d_att, att):
    def body(d_ref, a_ref, dl_ref, db_ref):
        dd = d_ref[...]
        prod = dd * a_ref[...]
        lane = lax.broadcasted_iota(jnp.int32, (dd.shape[0], LANES), 1)
        acc = jnp.zeros((dd.shape[0], LANES), F32)
        for h in range(N_ATT_HEADS):
            s = jnp.sum(prod[:, h * HEAD_DIM:(h + 1) * HEAD_DIM], axis=-1, keepdims=True)
            acc = jnp.where(lane == h, s, acc)
        dl_ref[...] = acc
        db_ref[...] = _b(dd)
    return _row_call(body, [d_att, att], [], [(LANES, F32), (ATT_WIDTH, BF16)], [], 512, "attn_delta")


def _attn_bwd(q, k, v, do, lse, delta, d):
    S = q.shape[0]
    blk = ATT_BLOCK
    nblk = S // blk
    nbs = nblk // d
    slopes = _alibi_slopes(N_ATT_HEADS)
    scale = HEAD_DIM ** -0.5

    def body(qc_ref, qn_ref, k_ref, v_ref, doc_ref, don_ref, lc_ref, ln_ref, dc_ref, dn_ref,
             dq_ref, dk_ref, dv_ref, carry_ref):
        n = pl.program_id(0)
        has_next = ((n + 1) % nbs) != 0

        @pl.when(n == 0)
        def _():
            carry_ref[...] = jnp.zeros_like(carry_ref)

        ii = lax.broadcasted_iota(jnp.int32, (blk, blk), 0)
        jj = lax.broadcasted_iota(jnp.int32, (blk, blk), 1)
        dist_c = (ii - jj).astype(F32)
        dist_p = dist_c + float(blk)
        ok_c = ii >= jj
        ok_p = jnp.logical_and(jj >= ii, has_next)
        for h in range(N_ATT_HEADS):
            sl = slice(h * HEAD_DIM, (h + 1) * HEAD_DIM)
            bias = slopes[h] * float(d)
            kh = k_ref[:, sl]
            vh = v_ref[:, sl]
            qh = qc_ref[:, sl]
            doh = doc_ref[:, sl]
            s = jnp.where(ok_c, _dot_nt(qh, kh) * scale - bias * dist_c - _head_col(lc_ref[...], h), NEG_BIG)
            p = jnp.exp(s)
            ds = p * (_dot_nt(doh, vh) - _head_col(dc_ref[...], h)) * scale
            pb, dsb = _b(p), _b(ds)
            dv = _dot_tn(pb, doh)
            dk = _dot_tn(dsb, qh)
            dq_ref[:, sl] = _dot(dsb, kh) + carry_ref[:, sl]
            qh = qn_ref[:, sl]
            doh = don_ref[:, sl]
            s = jnp.where(ok_p, _dot_nt(qh, kh) * scale - bias * dist_p - _head_col(ln_ref[...], h), NEG_BIG)
            p = jnp.exp(s)
            ds = p * (_dot_nt(doh, vh) - _head_col(dn_ref[...], h)) * scale
            pb, dsb = _b(p), _b(ds)
            dv_ref[:, sl] = dv + _dot_tn(pb, doh)
            dk_ref[:, sl] = dk + _dot_tn(dsb, qh)
            carry_ref[:, sl] = _dot(dsb, kh)

    cur = pl.BlockSpec((blk, ATT_WIDTH), lambda n: (n, 0))
    nxt = pl.BlockSpec((blk, ATT_WIDTH), lambda n: (jnp.minimum(n + 1, nblk - 1), 0))
    scur = pl.BlockSpec((blk, LANES), lambda n: (n, 0))
    snxt = pl.BlockSpec((blk, LANES), lambda n: (jnp.minimum(n + 1, nblk - 1), 0))
    shp = jax.ShapeDtypeStruct((S, ATT_WIDTH), F32)
    return pl.pallas_call(
        body, name=f"attn_bwd_d{d}", grid=(nblk,),
        in_specs=[cur, nxt, cur, cur, cur, nxt, scur, snxt, scur, snxt],
        out_specs=[cur, cur, cur],
        out_shape=[shp, shp, shp],
        scratch_shapes=[pltpu.VMEM((blk, ATT_WIDTH), F32)],
        compiler_params=_params(("arbitrary",)),
    )(q, q, k, v, do, do, lse, lse, delta, delta)


def _sum_qkv(dqs, dks, dvs):
    def body(q1, q2, q3, k1, k2, k3, v1, v2, v3, o_ref):
        o_ref[:, 0:ATT_WIDTH] = _b(q1[...] + q2[...] + q3[...])
        o_ref[:, ATT_WIDTH:2 * ATT_WIDTH] = _b(k1[...] + k2[...] + k3[...])
        o_ref[:, 2 * ATT_WIDTH:] = _b(v1[...] + v2[...] + v3[...])
    return _row_call(body, list(dqs) + list(dks) + list(dvs), [], [(3 * ATT_WIDTH, BF16)], [], 256,
                     "sum_dqkv")[0]


CONV_COLS = 1024
CONV_ROWS = 512
HALO = 8


def _conv_fwd(xbc, conv_w, conv_b):
    S, C = xbc.shape
    bs, bc = CONV_ROWS, CONV_COLS
    nr = S // bs

    def body(x_ref, halo_ref, w_ref, b_ref, o_ref, xs_ref):
        r = pl.program_id(1)
        xs_ref[pl.ds(HALO, bs), :] = x_ref[...]
        xs_ref[pl.ds(0, HALO), :] = jnp.where(r > 0, halo_ref[...], 0.0)
        pre = b_ref[...] + w_ref[3:4, :] * x_ref[...]
        for j in range(SSM_CONV - 1):
            pre = pre + w_ref[j:j + 1, :] * xs_ref[pl.ds(HALO - 3 + j, bs), :]
        o_ref[...] = pre * _sigmoid(pre)

    return pl.pallas_call(
        body, name="conv_fwd", grid=(C // bc, nr),
        in_specs=[pl.BlockSpec((bs, bc), lambda c, r: (r, c)),
                  pl.BlockSpec((HALO, bc), lambda c, r: (jnp.maximum(r * (bs // HALO) - 1, 0), c)),
                  pl.BlockSpec((SSM_CONV, bc), lambda c, r: (0, c)),
                  pl.BlockSpec((1, bc), lambda c, r: (0, c))],
        out_specs=pl.BlockSpec((bs, bc), lambda c, r: (r, c)),
        out_shape=jax.ShapeDtypeStruct((S, C), F32),
        scratch_shapes=[pltpu.VMEM((bs + HALO, bc), F32)],
        compiler_params=_params(("parallel", "arbitrary")),
    )(xbc, xbc, conv_w, conv_b)


def _conv_bwd(xbc, dact, conv_w, conv_b, col0):
    S, C = xbc.shape
    Cp = dact.shape[1]
    bs, bc = CONV_ROWS, min(CONV_COLS, Cp)
    nr = S // bs
    cb0 = col0 // bc
    last_halo = S // HALO - 1

    def body(x_ref, xp_ref, xn_ref, d_ref, dn_ref, w_ref, b_ref, dx_ref, dw_ref, db_ref,
             xs_ref, dp_ref):
        r = pl.program_id(1)
        xs_ref[pl.ds(0, HALO), :] = jnp.where(r > 0, xp_ref[...], 0.0)
        xs_ref[pl.ds(HALO, bs), :] = x_ref[...]
        xs_ref[pl.ds(HALO + bs, HALO), :] = xn_ref[...]
        ext = bs + HALO
        pre = b_ref[...] + jnp.zeros((ext, bc), F32)
        for j in range(SSM_CONV):
            pre = pre + w_ref[j:j + 1, :] * xs_ref[pl.ds(HALO - 3 + j, ext), :]
        sg = _sigmoid(pre)
        dsilu = sg * (1.0 + pre * (1.0 - sg))
        dp_ref[pl.ds(0, bs), :] = d_ref[...] * dsilu[:bs]
        dp_ref[pl.ds(bs, HALO), :] = jnp.where(r < nr - 1, dn_ref[...], 0.0) * dsilu[bs:]
        dx = jnp.zeros((bs, bc), F32)
        for j in range(SSM_CONV):
            dx = dx + w_ref[j:j + 1, :] * dp_ref[pl.ds(3 - j, bs), :]
        dx_ref[...] = _b(dx)
        dpre = dp_ref[pl.ds(0, bs), :]
        for j in range(SSM_CONV):
            part = jnp.sum(dpre * xs_ref[pl.ds(HALO - 3 + j, bs), :], axis=0, keepdims=True)

            @pl.when(r == 0)
            def _():
                dw_ref[j:j + 1, :] = part

            @pl.when(r > 0)
            def _():
                dw_ref[j:j + 1, :] += part
        part = jnp.sum(dpre, axis=0, keepdims=True)

        @pl.when(r == 0)
        def _():
            db_ref[...] = part

        @pl.when(r > 0)
        def _():
            db_ref[...] += part

    hb = bs // HALO
    return pl.pallas_call(
        body, name=f"conv_bwd_{col0}", grid=(Cp // bc, nr),
        in_specs=[pl.BlockSpec((bs, bc), lambda c, r: (r, cb0 + c)),
                  pl.BlockSpec((HALO, bc), lambda c, r: (jnp.maximum(r * hb - 1, 0), cb0 + c)),
                  pl.BlockSpec((HALO, bc), lambda c, r: (jnp.minimum((r + 1) * hb, last_halo), cb0 + c)),
                  pl.BlockSpec((bs, bc), lambda c, r: (r, c)),
                  pl.BlockSpec((HALO, bc), lambda c, r: (jnp.minimum((r + 1) * hb, last_halo), c)),
                  pl.BlockSpec((SSM_CONV, bc), lambda c, r: (0, cb0 + c)),
                  pl.BlockSpec((1, bc), lambda c, r: (0, cb0 + c))],
        out_specs=[pl.BlockSpec((bs, bc), lambda c, r: (r, c)),
                   pl.BlockSpec((SSM_CONV, bc), lambda c, r: (0, c)),
                   pl.BlockSpec((1, bc), lambda c, r: (0, c))],
        out_shape=[jax.ShapeDtypeStruct((S, Cp), BF16), jax.ShapeDtypeStruct((SSM_CONV, Cp), F32),
                   jax.ShapeDtypeStruct((1, Cp), F32)],
        scratch_shapes=[pltpu.VMEM((bs + 2 * HALO, bc), F32), pltpu.VMEM((bs + HALO, bc), F32)],
        compiler_params=_params(("parallel", "arbitrary")),
    )(xbc, xbc, xbc, dact, dact, conv_w, conv_b)


def _softplus(x):
    return jnp.maximum(x, 0.0) + jnp.log(1.0 + jnp.exp(-jnp.abs(x)))


def _ssd_common(dtr_ref, bias_ref, a_ref, g):
    ch = SSM_CHUNK
    x = dtr_ref[...] + bias_ref[...]
    dt_all = _softplus(x)
    r = lax.broadcasted_iota(jnp.int32, (LANES, LANES), 0)
    c = lax.broadcasted_iota(jnp.int32, (LANES, LANES), 1)
    sel = jnp.where(jnp.logical_and(r == HEADS_PER_GROUP * g + c, c < HEADS_PER_GROUP), 1.0, 0.0)
    dt4 = _dot_hi(dt_all, sel)
    la4 = _dot_hi(dt_all * a_ref[...], sel)
    ii = lax.broadcasted_iota(jnp.int32, (ch, ch), 0)
    jj = lax.broadcasted_iota(jnp.int32, (ch, ch), 1)
    tril = jnp.where(ii >= jj, 1.0, 0.0)
    acs = _dot_hi(tril, la4)
    return x, sel, dt4, acs, acs.T, ii >= jj


def _row8(v):
    return jnp.broadcast_to(v, (8, v.shape[1]))


def _ssd_fwd(xact, dt_raw, dt_bias, a_neg, d_skip):
    S = xact.shape[0]
    ch = SSM_CHUNK
    nch = S // ch
    hg = HEADS_PER_GROUP
    gw = hg * SSM_HEAD_DIM
    b_off = SSM_INNER // SSM_STATE
    c_off = b_off + SSM_GROUPS

    def body(x_ref, b_ref, c_ref, dtr_ref, bias_ref, a_ref, dsk_ref, y_ref, hs_ref, h_ref):
        c = pl.program_id(0)
        g = pl.program_id(1)

        @pl.when(jnp.logical_and(c == 0, g == 0))
        def _():
            h_ref[...] = jnp.zeros_like(h_ref)

        _, sel, dt4, acs, acs_t, low = _ssd_common(dtr_ref, bias_ref, a_ref, g)
        dsk4 = _dot_hi(_row8(dsk_ref[...]), sel)
        bb = _b(b_ref[...])
        cc = _b(c_ref[...])
        cb = _dot_nt(cc, bb)
        for j in range(hg):
            sl = slice(j * SSM_HEAD_DIM, (j + 1) * SSM_HEAD_DIM)
            acol = acs[:, j:j + 1]
            arow = acs_t[j:j + 1, :]
            alast = acs[ch - 1:ch, j:j + 1]
            decay = jnp.exp(jnp.where(low, acol - arow, -jnp.inf))
            xh = x_ref[:, sl]
            xd = xh * dt4[:, j:j + 1]
            hj = h_ref[hg * g + j]
            y = _dot(_b(cb * decay), _b(xd))
            y = y + _dot_nt(cc, _b(hj)) * jnp.exp(acol)
            y_ref[:, sl] = y + dsk4[0:1, j:j + 1] * xh
            hs_ref[0, j] = hj
            st = _dot_tn(_b(xd * jnp.exp(alast - acol)), bb)
            h_ref[hg * g + j] = hj * jnp.exp(alast) + st

    small = pl.BlockSpec((1, LANES), lambda c, g: (0, 0))
    return pl.pallas_call(
        body, name="ssd_fwd", grid=(nch, SSM_GROUPS),
        in_specs=[pl.BlockSpec((ch, gw), lambda c, g: (c, g)),
                  pl.BlockSpec((ch, SSM_STATE), lambda c, g: (c, b_off + g)),
                  pl.BlockSpec((ch, SSM_STATE), lambda c, g: (c, c_off + g)),
                  pl.BlockSpec((ch, LANES), lambda c, g: (c, 0)),
                  small, small, small],
        out_specs=[pl.BlockSpec((ch, gw), lambda c, g: (c, g)),
                   pl.BlockSpec((1, hg, SSM_HEAD_DIM, SSM_STATE), lambda c, g: (c, g, 0, 0))],
        out_shape=[jax.ShapeDtypeStruct((S, SSM_INNER), F32),
                   jax.ShapeDtypeStruct((nch, SSM_HEADS, SSM_HEAD_DIM, SSM_STATE), F32)],
        scratch_shapes=[pltpu.VMEM((SSM_HEADS, SSM_HEAD_DIM, SSM_STATE), F32)],
        compiler_params=_params(("arbitrary", "arbitrary")),
    )(xact, xact, xact, dt_raw, dt_bias, a_neg, d_skip)


def _ssd_bwd(xact, dt_raw, dt_bias, a_neg, d_skip, hs, dy):
    S = xact.shape[0]
    ch = SSM_CHUNK
    nch = S // ch
    hg = HEADS_PER_GROUP
    gw = hg * SSM_HEAD_DIM
    b_off = SSM_INNER // SSM_STATE
    c_off = b_off + SSM_GROUPS

    def body(x_ref, b_ref, c_ref, dtr_ref, bias_ref, a_ref, dsk_ref, hs_ref, dy_ref,
             dx_ref, db_ref, dc_ref, ddt_ref, st_ref, dh_ref, ddt_acc):
        step = pl.program_id(0)
        g = pl.program_id(1)

        @pl.when(jnp.logical_and(step == 0, g == 0))
        def _():
            dh_ref[...] = jnp.zeros_like(dh_ref)
            st_ref[...] = jnp.zeros_like(st_ref)

        @pl.when(g == 0)
        def _():
            ddt_acc[...] = jnp.zeros_like(ddt_acc)

        xraw, sel, dt4, acs, acs_t, low = _ssd_common(dtr_ref, bias_ref, a_ref, g)
        a4 = _dot_hi(_row8(a_ref[...]), sel)[0:1, :]
        dsk4 = _dot_hi(_row8(dsk_ref[...]), sel)
        bf = b_ref[...]
        cf = c_ref[...]
        bb = _b(bf)
        cc = _b(cf)
        cb = _dot_nt(cc, bb)
        lane = lax.broadcasted_iota(jnp.int32, (ch, LANES), 1)
        rowi = lax.broadcasted_iota(jnp.int32, (ch, 1), 0)
        ones = jnp.ones((ch, LANES), F32)
        dcb = jnp.zeros((ch, ch), F32)
        dc_acc = jnp.zeros((ch, SSM_STATE), F32)
        db_acc = jnp.zeros((ch, SSM_STATE), F32)
        dacs4 = jnp.zeros((ch, LANES), F32)
        ddt4 = jnp.zeros((ch, LANES), F32)
        dd4 = jnp.zeros((1, LANES), F32)
        lane1 = lax.broadcasted_iota(jnp.int32, (1, LANES), 1)
        for j in range(hg):
            sl = slice(j * SSM_HEAD_DIM, (j + 1) * SSM_HEAD_DIM)
            acol = acs[:, j:j + 1]
            arow = acs_t[j:j + 1, :]
            alast = acs[ch - 1:ch, j:j + 1]
            decay = jnp.exp(jnp.where(low, acol - arow, -jnp.inf))
            ea = jnp.exp(acol)
            dsd = jnp.exp(alast - acol)
            cd = jnp.exp(alast)
            dtc = dt4[:, j:j + 1]
            xh = x_ref[:, sl]
            xd = xh * dtc
            xdb = _b(xd)
            hj = hs_ref[0, j]
            hjb = _b(hj)
            dhn = dh_ref[hg * g + j]
            dyj = dy_ref[:, sl]
            dyb = _b(dyj)
            lm = cb * decay
            dxh = dsk4[0:1, j:j + 1] * dyj
            dd4 = jnp.where(lane1 == j, jnp.sum(jnp.sum(dyj * xh, axis=1, keepdims=True), axis=0,
                                                keepdims=True), dd4)
            dlm = _dot_nt(dyb, xdb)
            dxd = _dot_tn(_b(lm), dyb)
            gm = dlm * lm
            dcb = dcb + dlm * decay
            dac = jnp.sum(gm, axis=1, keepdims=True) - _dot_tn_hi(gm, ones)[:, 0:1]
            zz = _dot_nt(cc, hjb)
            dzb = _b(dyj * ea)
            dac = dac + jnp.sum(dyj * zz, axis=1, keepdims=True) * ea
            dc_acc = dc_acc + _dot(dzb, hjb)
            dh_in = _dot_tn(dzb, cc)
            dsb = _b(dhn)
            ww = _dot_nt(bb, dsb)
            dxd = dxd + ww * dsd
            dds = jnp.sum(ww * xd, axis=1, keepdims=True) * dsd
            db_acc = db_acc + _dot(_b(xd * dsd), dsb)
            dac = dac - dds
            dal = (jnp.sum(dds, axis=0, keepdims=True)
                   + jnp.sum(jnp.sum(dhn * hj, axis=1, keepdims=True), axis=0, keepdims=True) * cd)
            dh_ref[hg * g + j] = dh_in + dhn * cd
            dac = dac + jnp.where(rowi == ch - 1, dal, 0.0)
            dacs4 = jnp.where(lane == j, dac, dacs4)
            dx_ref[:, sl] = dxh + dxd * dtc
            ddt4 = jnp.where(lane == j, jnp.sum(dxd * xh, axis=1, keepdims=True), ddt4)
        dcbb = _b(dcb)
        dc_ref[...] = dc_acc + _dot(dcbb, bb)
        db_ref[...] = db_acc + _dot_tn(dcbb, cc)
        ii = lax.broadcasted_iota(jnp.int32, (ch, ch), 0)
        jj = lax.broadcasted_iota(jnp.int32, (ch, ch), 1)
        triu = jnp.where(ii <= jj, 1.0, 0.0)
        dla4 = _dot_hi(triu, dacs4)
        ddt4 = ddt4 + dla4 * a4
        da4 = jnp.sum(dla4 * dt4, axis=0, keepdims=True) * a4
        sel_t = sel.T
        ddt_raw = _dot_hi(ddt4, sel_t) * _sigmoid(xraw)
        ddt_acc[...] += ddt_raw
        st_ref[0:1, :] += _dot_hi(_row8(da4), sel_t)[0:1, :]
        st_ref[1:2, :] += _dot_hi(_row8(dd4), sel_t)[0:1, :]
        st_ref[2:3, :] += jnp.sum(ddt_raw, axis=0, keepdims=True)

        @pl.when(g == SSM_GROUPS - 1)
        def _():
            ddt_ref[...] = _b(ddt_acc[...])

    small = pl.BlockSpec((1, LANES), lambda s, g: (0, 0))
    rc = lambda s: nch - 1 - s
    return pl.pallas_call(
        body, name="ssd_bwd", grid=(nch, SSM_GROUPS),
        in_specs=[pl.BlockSpec((ch, gw), lambda s, g: (rc(s), g)),
                  pl.BlockSpec((ch, SSM_STATE), lambda s, g: (rc(s), b_off + g)),
                  pl.BlockSpec((ch, SSM_STATE), lambda s, g: (rc(s), c_off + g)),
                  pl.BlockSpec((ch, LANES), lambda s, g: (rc(s), 0)),
                  small, small, small,
                  pl.BlockSpec((1, hg, SSM_HEAD_DIM, SSM_STATE), lambda s, g: (rc(s), g, 0, 0)),
                  pl.BlockSpec((ch, gw), lambda s, g: (rc(s), g))],
        out_specs=[pl.BlockSpec((ch, gw), lambda s, g: (rc(s), g)),
                   pl.BlockSpec((ch, SSM_STATE), lambda s, g: (rc(s), g)),
                   pl.BlockSpec((ch, SSM_STATE), lambda s, g: (rc(s), g)),
                   pl.BlockSpec((ch, LANES), lambda s, g: (rc(s), 0)),
                   pl.BlockSpec((8, LANES), lambda s, g: (0, 0))],
        out_shape=[jax.ShapeDtypeStruct((S, SSM_INNER), F32),
                   jax.ShapeDtypeStruct((S, SSM_GROUPS * SSM_STATE), F32),
                   jax.ShapeDtypeStruct((S, SSM_GROUPS * SSM_STATE), F32),
                   jax.ShapeDtypeStruct((S, LANES), BF16),
                   jax.ShapeDtypeStruct((8, LANES), F32)],
        scratch_shapes=[pltpu.VMEM((SSM_HEADS, SSM_HEAD_DIM, SSM_STATE), F32),
                        pltpu.VMEM((ch, LANES), F32)],
        compiler_params=_params(("arbitrary", "arbitrary")),
    )(xact, xact, xact, dt_raw, dt_bias, a_neg, d_skip, hs, dy)


def _pad_lanes(v, n=LANES):
    return jnp.pad(v, ((0, 0), (0, n - v.shape[1])))


def _local_step(x, target, w):
    offs = np.cumsum((0,) + IN_SPLITS)
    w_in = w["w_in"]
    w_qkv = w_in[:, offs[0]:offs[3]]
    w_z = w_in[:, offs[3]:offs[4]]
    w_xbc = w_in[:, offs[4]:offs[5]]
    w_dt = _pad_lanes(w_in[:, offs[5]:offs[6]])
    w_g = w_in[:, offs[6]:offs[7]]
    dt_bias = _pad_lanes(w["dt_bias"])
    a_neg = _pad_lanes(-jnp.exp(w["a_log"]))
    d_skip = _pad_lanes(w["d_skip"])

    u = _rms_fwd(x, w["norm_mix_pre_w"])
    qkv = _mm_nn(u, w_qkv, BF16, "proj_qkv")
    z = _mm_nn(u, w_z, F32, "proj_z")
    xbc = _mm_nn(u, w_xbc, F32, "proj_xbc")
    dt_raw = _mm_nn(u, w_dt, F32, "proj_dt")
    gl = _mm_nn(u, w_g, F32, "proj_gate")

    q, k, v = (qkv[:, i * ATT_WIDTH:(i + 1) * ATT_WIDTH] for i in range(3))
    pats, os_, ms_, ls_ = [], [], [], []
    for d in DILATIONS:
        qp, kp, vp = _to_pat(q, d), _to_pat(k, d), _to_pat(v, d)
        pats.append((qp, kp, vp))
        o, m, l = _attn_fwd(qp, kp, vp, d)
        os_.append(_from_pat(o, d))
        ms_.append(_from_pat(m, d))
        ls_.append(_from_pat(l, d))
    att, lse = _attn_combine(os_, ms_, ls_)
    att_o = _mm_nn(att, w["w_att_proj"], F32, "att_proj")

    xact = _conv_fwd(xbc, w["conv_w"], w["conv_b"])
    y_ssd, hs = _ssd_fwd(xact, dt_raw, dt_bias, a_neg, d_skip)
    ssm_y = _gnorm_fwd(y_ssd, z, w["ssm_norm_w"])
    ssm_o = _mm_nn(ssm_y, w["w_ssm_proj"], F32, "ssm_proj")

    mi = _gate_fwd(att_o, ssm_o, gl, w["b_gate"])
    mixed = _mm_nn(mi, w["w_out"], F32, "out_proj")
    h1, f = _post_pre(x, mixed, w["norm_mix_post_w"], w["norm_ffn_pre_w"])
    up = _mm_nn(f, w["w_up"], F32, "ffn_up")
    act = _relu2(up)
    down = _mm_nn(act, w["w_down"], F32, "ffn_down")
    dh2, d_down, loss, g_ffn_post = _final(h1, down, w["norm_ffn_post_w"], target)

    g = {"norm_ffn_post_w": g_ffn_post}
    g["w_down"] = _mm_tn(act, d_down, "dw_down")
    da = _mm_nn(d_down, w["w_down"].T, F32, "d_act")
    dup = _dup(da, up)
    g["w_up"] = _mm_tn(f, dup, "dw_up")
    df = _mm_nn(dup, w["w_up"].T, F32, "d_f")
    dh1, d_mixed, g["norm_ffn_pre_w"], g["norm_mix_post_w"] = _mid_bwd(
        dh2, df, h1, mixed, w["norm_ffn_pre_w"], w["norm_mix_post_w"])
    g["w_out"] = _mm_tn(mi, d_mixed, "dw_out")
    dmi = _mm_nn(d_mixed, w["w_out"].T, F32, "d_mi")
    d_att_o, d_ssm_o, dgl, g["b_gate"] = _gate_bwd(dmi, att_o, ssm_o, gl, w["b_gate"])

    g["w_att_proj"] = _mm_tn(att, d_att_o, "dw_att_proj")
    d_att = _mm_nn(d_att_o, w["w_att_proj"].T, F32, "d_att")
    delta, d_att_b = _attn_delta(d_att, att)
    dqs, dks, dvs = [], [], []
    for d, (qp, kp, vp) in zip(DILATIONS, pats):
        dq, dk, dv = _attn_bwd(qp, kp, vp, _to_pat(d_att_b, d), _to_pat(lse, d), _to_pat(delta, d), d)
        dqs.append(_from_pat(dq, d))
        dks.append(_from_pat(dk, d))
        dvs.append(_from_pat(dv, d))
    dqkv = _sum_qkv(dqs, dks, dvs)

    g["w_ssm_proj"] = _mm_tn(ssm_y, d_ssm_o, "dw_ssm_proj")
    d_ssm_y = _mm_nn(d_ssm_o, w["w_ssm_proj"].T, F32, "d_ssm_y")
    dy_ssd, dz, g["ssm_norm_w"] = _gnorm_bwd(d_ssm_y, y_ssd, z, w["ssm_norm_w"])
    dxs, dbm, dcm, ddt_raw, stats = _ssd_bwd(xact, dt_raw, dt_bias, a_neg, d_skip, hs, dy_ssd)
    g["a_log"] = stats[0:1, :SSM_HEADS]
    g["d_skip"] = stats[1:2, :SSM_HEADS]
    g["dt_bias"] = stats[2:3, :SSM_HEADS]
    nbc = SSM_GROUPS * SSM_STATE
    dx_x, dcw_x, dcb_x = _conv_bwd(xbc, dxs, w["conv_w"], w["conv_b"], 0)
    dx_b, dcw_b, dcb_b = _conv_bwd(xbc, dbm, w["conv_w"], w["conv_b"], SSM_INNER)
    dx_c, dcw_c, dcb_c = _conv_bwd(xbc, dcm, w["conv_w"], w["conv_b"], SSM_INNER + nbc)
    g["conv_w"] = jnp.concatenate([dcw_x, dcw_b, dcw_c], axis=1)
    g["conv_b"] = jnp.concatenate([dcb_x, dcb_b, dcb_c], axis=1)

    pieces = [(dqkv, w_qkv), (dz, w_z), (dx_x, w_xbc[:, :SSM_INNER]),
              (dx_b, w_xbc[:, SSM_INNER:SSM_INNER + nbc]), (dx_c, w_xbc[:, SSM_INNER + nbc:]),
              (ddt_raw, w_dt), (dgl, w_g)]
    du = None
    gw = []
    for i, (dp, wp) in enumerate(pieces):
        gw.append(_mm_tn(u, dp, f"dw_in_{i}"))
        du = _mm_nn(dp, wp.T, F32, f"d_u_{i}", acc=du)
    gw[5] = gw[5][:, :SSM_HEADS]
    g["w_in"] = jnp.concatenate(gw, axis=1)
    grad_x, g["norm_mix_pre_w"] = _first_bwd(dh1, du, x, w["norm_mix_pre_w"])
    return loss, grad_x, g


BIG = ("w_in", "w_att_proj", "w_ssm_proj", "w_out", "w_up", "w_down")
BIG_FULL_SHAPES = {"w_in": (D_MODEL, IN_PROJ_WIDTH), "w_att_proj": (ATT_WIDTH, D_MODEL),
                   "w_ssm_proj": (SSM_INNER, D_MODEL), "w_out": (D_MODEL, D_MODEL),
                   "w_up": (D_MODEL, FFN_HIDDEN), "w_down": (FFN_HIDDEN, D_MODEL)}
BIG_COL_SHARDED = {"w_in": True, "w_att_proj": True, "w_ssm_proj": False, "w_out": False, "w_up": True,
                   "w_down": False}
PACK_COLS = 1024
PACK_ROWS = 5760
PACK_HALF = PACK_ROWS // 2
PACK_BLOCK = 576
SMALL = ("norm_mix_pre_w", "b_gate", "conv_b", "dt_bias", "a_log", "d_skip", "ssm_norm_w",
         "norm_mix_post_w", "norm_ffn_pre_w", "norm_ffn_post_w")
SMALL_ROWS = 232


def _shard_shape(name):
    r, c = BIG_FULL_SHAPES[name]
    return (r, c // N_CHIPS) if BIG_COL_SHARDED[name] else (r // N_CHIPS, c)


def _pack(shards, dtype):
    flat = [shards[n].astype(dtype).reshape(-1, PACK_COLS) for n in BIG]
    rows = sum(f.shape[0] for f in flat)
    flat.append(jnp.zeros((PACK_ROWS - rows, PACK_COLS), dtype))
    return jnp.concatenate(flat, axis=0)


def _unpack(packed):
    out, r0 = {}, 0
    for n in BIG:
        shp = _shard_shape(n)
        rows = shp[0] * shp[1] // PACK_COLS
        out[n] = packed[r0:r0 + rows].reshape(shp)
        r0 += rows
    return out


def _unpack_full(gathered):
    out, r0 = {}, 0
    for n in BIG:
        shp = _shard_shape(n)
        rows = shp[0] * shp[1] // PACK_COLS
        sh = gathered[:, r0:r0 + rows].reshape((N_CHIPS,) + shp)
        if BIG_COL_SHARDED[n]:
            out[n] = sh.transpose(1, 0, 2).reshape(BIG_FULL_SHAPES[n])
        else:
            out[n] = sh.reshape(BIG_FULL_SHAPES[n])
        r0 += rows
    return out


def _pack_full(grads):
    parts = []
    rows_total = 0
    for n in BIG:
        shp = _shard_shape(n)
        gfull = grads[n]
        if BIG_COL_SHARDED[n]:
            sh = gfull.reshape(shp[0], N_CHIPS, shp[1]).transpose(1, 0, 2)
        else:
            sh = gfull.reshape((N_CHIPS,) + shp)
        parts.append(sh.reshape(N_CHIPS, -1, PACK_COLS))
        rows_total += parts[-1].shape[1]
    parts.append(jnp.zeros((N_CHIPS, PACK_ROWS - rows_total, PACK_COLS), F32))
    return jnp.concatenate(parts, axis=1)


def _mesh_pos():
    return lax.axis_index("x"), lax.axis_index("y"), lax.axis_index("c")


def _other_chips(x, y):
    return [(1 - x, y), (x, 1 - y), (1 - x, 1 - y)]


ANY = pl.BlockSpec(memory_space=pl.ANY)


def _allgather_packed(wpack):
    half = PACK_HALF

    def body(w_ref, out_ref, send_sems, recv_sems, local_sem):
        x, y, c = _mesh_pos()
        me = 2 * x + y
        sibling = (x, y, 1 - c)
        chips = _other_chips(x, y)

        def rows(chip, h):
            return out_ref.at[chip, pl.ds(h * half, half), :]

        def copy(k, chip, h, to, src=None):
            return pltpu.make_async_remote_copy(
                src_ref=rows(chip, h) if src is None else src, dst_ref=rows(chip, h),
                send_sem=send_sems.at[k], recv_sem=recv_sems.at[k], device_id=to, device_id_type=MESH)

        mine = pltpu.make_async_copy(w_ref, out_ref.at[me], local_sem)
        mine.start()
        mine_half = w_ref.at[pl.ds(c * half, half), :]
        first = [copy(j, me, c, (*chip, c), src=mine_half) for j, chip in enumerate(chips)]
        for cp in first:
            cp.start()
        passed = [copy(3 + j, 2 * chip[0] + chip[1], c, sibling) for j, chip in enumerate(chips)]
        for j, chip in enumerate(chips):
            copy(j, 2 * chip[0] + chip[1], c, (x, y, c)).wait_recv()
            passed[j].start()
        for j, chip in enumerate(chips):
            copy(3 + j, 2 * chip[0] + chip[1], 1 - c, (x, y, c)).wait_recv()
        for cp in first + passed:
            cp.wait_send()
        mine.wait()

    return pl.pallas_call(
        body, name="allgather_weights",
        out_shape=jax.ShapeDtypeStruct((N_CHIPS,) + wpack.shape, wpack.dtype),
        in_specs=[ANY], out_specs=ANY,
        scratch_shapes=[pltpu.SemaphoreType.DMA((6,)), pltpu.SemaphoreType.DMA((6,)),
                        pltpu.SemaphoreType.DMA],
        compiler_params=pltpu.CompilerParams(has_side_effects=True),
    )(wpack)


def _exchange_halves(gpack):
    half = PACK_HALF

    def body(g_ref, out_ref, send_sem, recv_sem):
        x, y, c = _mesh_pos()
        cp = pltpu.make_async_remote_copy(
            src_ref=g_ref.at[:, pl.ds((1 - c) * half, half), :], dst_ref=out_ref,
            send_sem=send_sem, recv_sem=recv_sem, device_id=(x, y, 1 - c), device_id_type=MESH)
        cp.start()
        cp.wait()

    return pl.pallas_call(
        body, name="rs_pair_exchange",
        out_shape=jax.ShapeDtypeStruct((N_CHIPS, half, PACK_COLS), F32),
        in_specs=[ANY], out_specs=ANY,
        scratch_shapes=[pltpu.SemaphoreType.DMA, pltpu.SemaphoreType.DMA],
        compiler_params=pltpu.CompilerParams(has_side_effects=True),
    )(gpack)


def _pair_add(gpack, recv, c_idx):
    nb = PACK_HALF // PACK_BLOCK

    def body(c_ref, g_ref, r_ref, o_ref):
        o_ref[...] = g_ref[...] + r_ref[...]

    blk = (1, PACK_BLOCK, PACK_COLS)
    return pl.pallas_call(
        body, name="rs_pair_add",
        grid_spec=pltpu.PrefetchScalarGridSpec(
            num_scalar_prefetch=1, grid=(N_CHIPS, nb),
            in_specs=[pl.BlockSpec(blk, lambda s, i, c: (s, c[0] * nb + i, 0)),
                      pl.BlockSpec(blk, lambda s, i, c: (s, i, 0))],
            out_specs=pl.BlockSpec(blk, lambda s, i, c: (s, i, 0))),
        out_shape=jax.ShapeDtypeStruct((N_CHIPS, PACK_HALF, PACK_COLS), F32),
        compiler_params=_params(("arbitrary", "arbitrary")),
    )(c_idx, gpack, recv)


def _exchange_chips(ppack):
    def body(p_ref, out_ref, send_sems, recv_sems):
        x, y, c = _mesh_pos()
        chips = _other_chips(x, y)
        cps = [pltpu.make_async_remote_copy(
            src_ref=p_ref.at[2 * chip[0] + chip[1]], dst_ref=out_ref.at[j],
            send_sem=send_sems.at[j], recv_sem=recv_sems.at[j], device_id=(*chip, c), device_id_type=MESH)
            for j, chip in enumerate(chips)]
        for cp in cps:
            cp.start()
        for cp in cps:
            cp.wait_recv()
        for cp in cps:
            cp.wait_send()

    return pl.pallas_call(
        body, name="rs_chip_exchange",
        out_shape=jax.ShapeDtypeStruct((N_CHIPS - 1, PACK_HALF, PACK_COLS), F32),
        in_specs=[ANY], out_specs=ANY,
        scratch_shapes=[pltpu.SemaphoreType.DMA((3,)), pltpu.SemaphoreType.DMA((3,))],
        compiler_params=pltpu.CompilerParams(has_side_effects=True),
    )(ppack)


def _chip_add(ppack, recv, me_idx):
    nb = PACK_HALF // PACK_BLOCK

    def body(m_ref, p_ref, r0_ref, r1_ref, r2_ref, o_ref):
        o_ref[...] = ((p_ref[0] + r0_ref[0]) + r1_ref[0]) + r2_ref[0]

    blk = (1, PACK_BLOCK, PACK_COLS)
    return pl.pallas_call(
        body, name="rs_chip_add",
        grid_spec=pltpu.PrefetchScalarGridSpec(
            num_scalar_prefetch=1, grid=(nb,),
            in_specs=[pl.BlockSpec(blk, lambda i, m: (m[0], i, 0)),
                      pl.BlockSpec(blk, lambda i, m: (0, i, 0)),
                      pl.BlockSpec(blk, lambda i, m: (1, i, 0)),
                      pl.BlockSpec(blk, lambda i, m: (2, i, 0))],
            out_specs=pl.BlockSpec((PACK_BLOCK, PACK_COLS), lambda i, m: (i, 0))),
        out_shape=jax.ShapeDtypeStruct((PACK_HALF, PACK_COLS), F32),
        compiler_params=_params(("arbitrary",)),
    )(me_idx, ppack, recv, recv, recv)


def _share_halves(qhalf):
    half = PACK_HALF

    def body(q_ref, out_ref, send_sem, recv_sem, local_sem):
        x, y, c = _mesh_pos()
        mine = pltpu.make_async_copy(q_ref, out_ref.at[pl.ds(c * half, half), :], local_sem)
        mine.start()
        cp = pltpu.make_async_remote_copy(
            src_ref=q_ref, dst_ref=out_ref.at[pl.ds(c * half, half), :],
            send_sem=send_sem, recv_sem=recv_sem, device_id=(x, y, 1 - c), device_id_type=MESH)
        cp.start()
        pltpu.make_async_remote_copy(
            src_ref=q_ref, dst_ref=out_ref.at[pl.ds((1 - c) * half, half), :],
            send_sem=send_sem, recv_sem=recv_sem, device_id=(x, y, 1 - c), device_id_type=MESH).wait_recv()
        cp.wait_send()
        mine.wait()

    return pl.pallas_call(
        body, name="rs_share_halves",
        out_shape=jax.ShapeDtypeStruct((PACK_ROWS, PACK_COLS), F32),
        in_specs=[ANY], out_specs=ANY,
        scratch_shapes=[pltpu.SemaphoreType.DMA, pltpu.SemaphoreType.DMA, pltpu.SemaphoreType.DMA],
        compiler_params=pltpu.CompilerParams(has_side_effects=True),
    )(qhalf)


def _allreduce_small(part, name):
    rows = part.shape[0]

    def body(p_ref, out_ref, buf, send_sems, recv_sems, local_sem):
        x, y, c = _mesh_pos()
        me, sibling = (x, y, c), (x, y, 1 - c)
        chips = _other_chips(x, y)

        def slot(px, py, pc):
            return buf.at[pl.ds((4 * px + 2 * py + pc) * rows, rows), :]

        def copy(k, block, to, src=None):
            return pltpu.make_async_remote_copy(
                src_ref=slot(*block) if src is None else src, dst_ref=slot(*block),
                send_sem=send_sems.at[k], recv_sem=recv_sems.at[k], device_id=to, device_id_type=MESH)

        mine = pltpu.make_async_copy(p_ref, slot(*me), local_sem)
        mine.start()
        first = [copy(0, me, sibling, src=p_ref)]
        first += [copy(1 + j, me, (*chip, c), src=p_ref) for j, chip in enumerate(chips)]
        for cp in first:
            cp.start()
        passed = [copy(4 + j, (*chip, c), sibling) for j, chip in enumerate(chips)]
        for j, chip in enumerate(chips):
            copy(1 + j, (*chip, c), me).wait_recv()
            passed[j].start()
        copy(0, sibling, me).wait_recv()
        for j, chip in enumerate(chips):
            copy(4 + j, (*chip, 1 - c), me).wait_recv()
        for cp in first + passed:
            cp.wait_send()
        mine.wait()
        acc = buf[pl.ds(0, rows), :]
        for k in range(1, N_DEV):
            acc = acc + buf[pl.ds(k * rows, rows), :]
        out_ref[...] = acc

    return pl.pallas_call(
        body, name=name,
        out_shape=jax.ShapeDtypeStruct(part.shape, F32),
        in_specs=[pl.BlockSpec(memory_space=pltpu.VMEM)],
        out_specs=pl.BlockSpec(memory_space=pltpu.VMEM),
        scratch_shapes=[pltpu.VMEM((N_DEV * rows, LANES), F32), pltpu.SemaphoreType.DMA((7,)),
                        pltpu.SemaphoreType.DMA((7,)), pltpu.SemaphoreType.DMA],
        compiler_params=pltpu.CompilerParams(has_side_effects=True),
    )(part)


def _adamw(w, g, m, v, name):
    R, C = w.shape
    bs = _pick(R, (128, 64, 32, 8)) if R % 8 == 0 else R
    c1 = 1.0 / (1.0 - ADAM_B1 ** ADAM_STEP)
    c2 = 1.0 / (1.0 - ADAM_B2 ** ADAM_STEP)

    def body(w_ref, g_ref, m_ref, v_ref, d_ref, nm_ref, nv_ref):
        gg = g_ref[...]
        nm = ADAM_B1 * m_ref[...] + (1.0 - ADAM_B1) * gg
        nv = ADAM_B2 * v_ref[...] + (1.0 - ADAM_B2) * (gg * gg)
        nm_ref[...] = nm
        nv_ref[...] = nv
        d_ref[...] = -ADAM_LR * ((nm * c1) / (jnp.sqrt(nv * c2) + ADAM_EPS) + ADAM_WD * w_ref[...])

    spec = pl.BlockSpec((bs, C), lambda i: (i, 0))
    shp = jax.ShapeDtypeStruct((R, C), F32)
    return pl.pallas_call(
        body, name=name, grid=(R // bs,), in_specs=[spec] * 4, out_specs=[spec] * 3, out_shape=[shp] * 3,
        compiler_params=_params(("parallel",)),
    )(w, g, m, v)


WEIGHTS = ("norm_mix_pre_w", "w_in", "b_gate", "conv_w", "conv_b", "dt_bias", "a_log", "d_skip",
           "ssm_norm_w", "w_att_proj", "w_ssm_proj", "w_out", "norm_mix_post_w", "norm_ffn_pre_w", "w_up",
           "w_down", "norm_ffn_post_w")


def _flat_small(vals, conv_w_full):
    flat = [vals[n].reshape(-1) for n in SMALL] + [conv_w_full.reshape(-1)]
    v = jnp.concatenate(flat)
    return jnp.pad(v, (0, SMALL_ROWS * LANES - v.shape[0])).reshape(SMALL_ROWS, LANES)


def kernel(x, norm_mix_pre_w, w_in, b_gate, conv_w, conv_b, dt_bias, a_log, d_skip, ssm_norm_w, w_att_proj, w_ssm_proj, w_out, norm_mix_post_w, norm_ffn_pre_w, w_up, w_down, norm_ffn_post_w, loss_target, m_norm_mix_pre_w, m_w_in, m_b_gate, m_conv_w, m_conv_b, m_dt_bias, m_a_log, m_d_skip, m_ssm_norm_w, m_w_att_proj, m_w_ssm_proj, m_w_out, m_norm_mix_post_w, m_norm_ffn_pre_w, m_w_up, m_w_down, m_norm_ffn_post_w, v_norm_mix_pre_w, v_w_in, v_b_gate, v_conv_w, v_conv_b, v_dt_bias, v_a_log, v_d_skip, v_ssm_norm_w, v_w_att_proj, v_w_ssm_proj, v_w_out, v_norm_mix_post_w, v_norm_ffn_pre_w, v_w_up, v_w_down, v_norm_ffn_post_w):
    args = locals()

    def strip(a):
        return a[0] if a.ndim == 3 else a

    wts = {n: strip(args[n]) for n in WEIGHTS}
    mom = {n: strip(args["m_" + n]) for n in WEIGHTS}
    var = {n: strip(args["v_" + n]) for n in WEIGHTS}
    xi, yi, ci = _mesh_pos()
    chip = 2 * xi + yi

    gathered = _allgather_packed(_pack({n: wts[n] for n in BIG}, BF16))
    full = _unpack_full(gathered)
    cw_cols = CONV_DIM // N_CHIPS
    conv_slab = lax.dynamic_update_slice(jnp.zeros((SSM_CONV, CONV_DIM), F32),
                                         jnp.where(ci == 0, wts["conv_w"], 0.0), (0, chip * cw_cols))
    small_in = jnp.pad(conv_slab.reshape(-1), (0, SMALL_ROWS * LANES - SSM_CONV * CONV_DIM))
    conv_full = _allreduce_small(small_in.reshape(SMALL_ROWS, LANES), "gather_conv_w")
    full["conv_w"] = conv_full.reshape(-1)[:SSM_CONV * CONV_DIM].reshape(SSM_CONV, CONV_DIM)
    for n in SMALL:
        full[n] = wts[n]

    loss_part, grad_x, g = _local_step(x[0], loss_target[0], full)
    loss = lax.psum(loss_part[0, 0], ("x", "y", "c"))

    gpack = _pack_full(g)
    recv = _exchange_halves(gpack)
    ppack = _pair_add(gpack, recv, ci.reshape(1).astype(jnp.int32))
    recv3 = _exchange_chips(ppack)
    qhalf = _chip_add(ppack, recv3, chip.reshape(1).astype(jnp.int32))
    gshard = _unpack(_share_halves(qhalf))
    small_sum = _allreduce_small(_flat_small(g, g["conv_w"]), "allreduce_small_grads").reshape(-1)
    grads, off = {}, 0
    for n in SMALL:
        sz = wts[n].size
        grads[n] = small_sum[off:off + sz].reshape(wts[n].shape)
        off += sz
    conv_g = small_sum[off:off + SSM_CONV * CONV_DIM].reshape(SSM_CONV, CONV_DIM)
    grads["conv_w"] = lax.dynamic_slice(conv_g, (0, chip * cw_cols), (SSM_CONV, cw_cols))
    grads.update(gshard)

    delta, new_m, new_v = {}, {}, {}
    for n in BIG:
        delta[n], new_m[n], new_v[n] = _adamw(wts[n], grads[n], mom[n], var[n], f"adamw_{n}")
    small_names = SMALL + ("conv_w",)

    def pack_small(d):
        v = jnp.concatenate([d[n].reshape(-1) for n in small_names])
        rows = -(-v.shape[0] // (8 * LANES)) * 8
        return jnp.pad(v, (0, rows * LANES - v.shape[0])).reshape(rows, LANES)

    ds, ms, vs = _adamw(pack_small(wts), pack_small(grads), pack_small(mom), pack_small(var), "adamw_small")
    off = 0
    for n in small_names:
        sz = wts[n].size
        for dst, src in ((delta, ds), (new_m, ms), (new_v, vs)):
            dst[n] = src.reshape(-1)[off:off + sz].reshape(wts[n].shape)
        off += sz

    out = [loss, grad_x[None]]
    for d in (grads, delta, new_m, new_v):
        out += [d[n][None] if args[n].ndim == 3 else d[n] for n in WEIGHTS]
    return tuple(out)
```

```python
import functools
import math

import numpy as np
import jax
import jax.numpy as jnp
from jax import lax
from jax.experimental import pallas as pl
from jax.experimental.pallas import tpu as pltpu

F32 = jnp.float32
BF16 = jnp.bfloat16

D_MODEL = 1024
HEAD_DIM = 64
N_ATT_HEADS = 12
ATT_WIDTH = N_ATT_HEADS * HEAD_DIM
DILATIONS = (1, 4, 16)
ATT_BLOCK = 128
SSM_INNER = 2048
SSM_HEADS = 32
SSM_GROUPS = 8
HEADS_PER_GROUP = SSM_HEADS // SSM_GROUPS
SSM_HEAD_DIM = 64
SSM_STATE = 128
SSM_CONV = 4
SSM_CHUNK = 128
CONV_DIM = SSM_INNER + 2 * SSM_GROUPS * SSM_STATE
FFN_HIDDEN = 4 * D_MODEL
IN_SPLITS = (ATT_WIDTH, ATT_WIDTH, ATT_WIDTH, SSM_INNER, CONV_DIM, SSM_HEADS, 2 * D_MODEL)
IN_PROJ_WIDTH = sum(IN_SPLITS)
RMS_EPS = 1e-6
LANES = 128
NEG_BIG = -1e30

ADAM_LR = 0.001
ADAM_B1 = 0.9
ADAM_B2 = 0.999
ADAM_EPS = 1e-08
ADAM_WD = 0.01
ADAM_STEP = 10

N_CHIPS = 4
N_DEV = 8
VMEM_LIMIT = 56 * 1024 * 1024
MESH = pl.DeviceIdType.MESH


def _alibi_slopes(n):
    def pow2(m):
        start = 2.0 ** (-8.0 / m)
        return [start ** (i + 1) for i in range(m)]
    if (n & (n - 1)) == 0:
        s = pow2(n)
    else:
        c = 2 ** int(math.floor(math.log2(n)))
        s = pow2(c) + pow2(2 * c)[0::2][: n - c]
    return [float(v) for v in np.array(s, dtype=np.float32)]


def _params(sem):
    return pltpu.CompilerParams(dimension_semantics=sem, vmem_limit_bytes=VMEM_LIMIT)


def _dot(a, b):
    return lax.dot_general(a, b, (((1,), (0,)), ((), ())), preferred_element_type=F32)


def _dot_nt(a, b):
    return lax.dot_general(a, b, (((1,), (1,)), ((), ())), preferred_element_type=F32)


def _dot_tn(a, b):
    return lax.dot_general(a, b, (((0,), (0,)), ((), ())), preferred_element_type=F32)


def _dot_hi(a, b):
    return lax.dot_general(a, b, (((1,), (0,)), ((), ())), preferred_element_type=F32,
                           precision=lax.Precision.HIGHEST)


def _dot_tn_hi(a, b):
    return lax.dot_general(a, b, (((0,), (0,)), ((), ())), preferred_element_type=F32,
                           precision=lax.Precision.HIGHEST)


def _b(x):
    return x.astype(BF16)


def _sigmoid(x):
    return 1.0 / (1.0 + jnp.exp(-x))


def _pick(n, cands):
    for c in cands:
        if n % c == 0:
            return c
    raise ValueError(f"no tile for {n}")


def _mm_nn(a, b, out_dtype, name, acc=None):
    M, K = a.shape
    _, N = b.shape
    tm = 512
    tn = _pick(N, (1024, 768, 512, 256, 128))
    tk = K if K <= 2304 else _pick(K, (2048, 1024))
    nk = K // tk
    has_acc = acc is not None

    def body(*refs):
        if has_acc:
            a_ref, b_ref, c_ref, o_ref, acc_ref = refs
        else:
            a_ref, b_ref, o_ref, acc_ref = refs
        k = pl.program_id(2)
        part = _dot(_b(a_ref[...]), _b(b_ref[...]))

        @pl.when(k == 0)
        def _():
            acc_ref[...] = part

        @pl.when(k > 0)
        def _():
            acc_ref[...] += part

        @pl.when(k == nk - 1)
        def _():
            r = acc_ref[...]
            if has_acc:
                r = r + c_ref[...]
            o_ref[...] = r.astype(out_dtype)

    in_specs = [pl.BlockSpec((tm, tk), lambda j, i, k: (i, k)),
                pl.BlockSpec((tk, tn), lambda j, i, k: (k, j))]
    args = [a, b]
    if has_acc:
        in_specs.append(pl.BlockSpec((tm, tn), lambda j, i, k: (i, j)))
        args.append(acc)
    return pl.pallas_call(
        body, name=name, grid=(N // tn, M // tm, nk),
        in_specs=in_specs,
        out_specs=pl.BlockSpec((tm, tn), lambda j, i, k: (i, j)),
        out_shape=jax.ShapeDtypeStruct((M, N), out_dtype),
        scratch_shapes=[pltpu.VMEM((tm, tn), F32)],
        compiler_params=_params(("parallel", "parallel", "arbitrary")),
    )(*args)


def _mm_tn(a, b, name):
    S, Ka = a.shape
    _, N = b.shape
    tka = _pick(Ka, (1024, 768, 512))
    tn = _pick(N, (1024, 768, 512, 256, 128))
    ts = 1024 if S % 1024 == 0 else 512
    ns = S // ts

    def body(a_ref, b_ref, o_ref, acc_ref):
        s = pl.program_id(2)
        part = _dot_tn(_b(a_ref[...]), _b(b_ref[...]))

        @pl.when(s == 0)
        def _():
            acc_ref[...] = part

        @pl.when(s > 0)
        def _():
            acc_ref[...] += part

        @pl.when(s == ns - 1)
        def _():
            o_ref[...] = acc_ref[...]

    return pl.pallas_call(
        body, name=name, grid=(Ka // tka, N // tn, ns),
        in_specs=[pl.BlockSpec((ts, tka), lambda i, j, s: (s, i)),
                  pl.BlockSpec((ts, tn), lambda i, j, s: (s, j))],
        out_specs=pl.BlockSpec((tka, tn), lambda i, j, s: (i, j)),
        out_shape=jax.ShapeDtypeStruct((Ka, N), F32),
        scratch_shapes=[pltpu.VMEM((tka, tn), F32)],
        compiler_params=_params(("parallel", "parallel", "arbitrary")),
    )(a, b)


def _row_call(body, row_ins, full_ins, row_outs, acc_outs, bs, name):
    S = row_ins[0].shape[0]
    assert S % bs == 0
    in_specs = [pl.BlockSpec((bs, a.shape[1]), lambda i: (i, 0)) for a in row_ins]
    in_specs += [pl.BlockSpec(a.shape, lambda i: (0, 0)) for a in full_ins]
    out_specs = [pl.BlockSpec((bs, c), lambda i: (i, 0)) for c, _ in row_outs]
    out_specs += [pl.BlockSpec(s, lambda i: (0, 0)) for s in acc_outs]
    out_shape = [jax.ShapeDtypeStruct((S, c), dt) for c, dt in row_outs]
    out_shape += [jax.ShapeDtypeStruct(s, F32) for s in acc_outs]
    return pl.pallas_call(
        body, name=name, grid=(S // bs,), in_specs=in_specs, out_specs=out_specs, out_shape=out_shape,
        compiler_params=_params(("arbitrary",)),
    )(*row_ins, *full_ins)


def _rms_vals(x, w):
    r = lax.rsqrt(jnp.mean(x * x, axis=-1, keepdims=True) + RMS_EPS)
    return x * r * w


def _rms_bwd_vals(x, w, dy):
    r = lax.rsqrt(jnp.mean(x * x, axis=-1, keepdims=True) + RMS_EPS)
    xn = x * r
    g = dy * w
    dx = r * (g - xn * jnp.mean(g * xn, axis=-1, keepdims=True))
    dw = jnp.sum(dy * xn, axis=0, keepdims=True)
    return dx, dw


def _acc_add(ref, val):
    @pl.when(pl.program_id(0) == 0)
    def _():
        ref[...] = val

    @pl.when(pl.program_id(0) > 0)
    def _():
        ref[...] += val


def _rms_fwd(x, w):
    def body(x_ref, w_ref, o_ref):
        o_ref[...] = _b(_rms_vals(x_ref[...], w_ref[...]))
    return _row_call(body, [x], [w], [(x.shape[1], BF16)], [], 512, "rms_fwd")[0]


def _gate_fwd(att_o, ssm_o, gl, b_gate):
    def body(a_ref, s_ref, g_ref, b_ref, o_ref):
        g = _sigmoid(g_ref[...] + b_ref[...])
        o_ref[...] = _b(g[:, :D_MODEL] * a_ref[...] + g[:, D_MODEL:] * s_ref[...])
    return _row_call(body, [att_o, ssm_o, gl], [b_gate], [(D_MODEL, BF16)], [], 512, "gate_fwd")[0]


def _post_pre(x, mixed, w_post, w_pre):
    def body(x_ref, m_ref, wp_ref, wn_ref, h_ref, f_ref):
        h = x_ref[...] + _rms_vals(m_ref[...], wp_ref[...])
        h_ref[...] = h
        f_ref[...] = _b(_rms_vals(h, wn_ref[...]))
    return _row_call(body, [x, mixed], [w_post, w_pre], [(D_MODEL, F32), (D_MODEL, BF16)], [], 512,
                     "post_pre")


def _relu2(up):
    def body(u_ref, o_ref):
        r = jnp.maximum(u_ref[...], 0.0)
        o_ref[...] = _b(r * r)
    return _row_call(body, [up], [], [(up.shape[1], BF16)], [], 256, "relu2")[0]


def _final(h1, down, w_post, target):
    def body(h_ref, d_ref, t_ref, w_ref, dh_ref, dd_ref, loss_ref, dw_ref):
        dn = d_ref[...]
        w = w_ref[...]
        err = h_ref[...] + _rms_vals(dn, w) - t_ref[...]
        row = jnp.mean(err * err, axis=-1, keepdims=True)
        part = 0.5 * jnp.sum(row, axis=0, keepdims=True)
        dh = err * (1.0 / D_MODEL)
        dh_ref[...] = dh
        dx, dw = _rms_bwd_vals(dn, w, dh)
        dd_ref[...] = _b(dx)
        _acc_add(loss_ref, jnp.broadcast_to(part, (1, LANES)))
        _acc_add(dw_ref, dw)
    return _row_call(body, [h1, down, target], [w_post], [(D_MODEL, F32), (D_MODEL, BF16)],
                     [(1, LANES), (1, D_MODEL)], 512, "final_loss")


def _dup(da, up):
    def body(a_ref, u_ref, o_ref):
        o_ref[...] = _b(a_ref[...] * (2.0 * jnp.maximum(u_ref[...], 0.0)))
    return _row_call(body, [da, up], [], [(up.shape[1], BF16)], [], 256, "relu2_bwd")[0]


def _mid_bwd(dh2, df, h1, mixed, w_pre, w_post):
    def body(dh_ref, df_ref, h_ref, m_ref, wn_ref, wp_ref, dh1_ref, dm_ref, dwn_ref, dwp_ref):
        dx, dwn = _rms_bwd_vals(h_ref[...], wn_ref[...], df_ref[...])
        dh1 = dh_ref[...] + dx
        dh1_ref[...] = dh1
        dm, dwp = _rms_bwd_vals(m_ref[...], wp_ref[...], dh1)
        dm_ref[...] = _b(dm)
        _acc_add(dwn_ref, dwn)
        _acc_add(dwp_ref, dwp)
    return _row_call(body, [dh2, df, h1, mixed], [w_pre, w_post], [(D_MODEL, F32), (D_MODEL, BF16)],
                     [(1, D_MODEL), (1, D_MODEL)], 512, "mid_bwd")


def _gate_bwd(dmi, att_o, ssm_o, gl, b_gate):
    def body(d_ref, a_ref, s_ref, g_ref, b_ref, da_ref, ds_ref, dg_ref, db_ref):
        g = _sigmoid(g_ref[...] + b_ref[...])
        d = d_ref[...]
        ga, gs = g[:, :D_MODEL], g[:, D_MODEL:]
        da_ref[...] = _b(ga * d)
        ds_ref[...] = _b(gs * d)
        dga = d * a_ref[...] * ga * (1.0 - ga)
        dgs = d * s_ref[...] * gs * (1.0 - gs)
        dg_ref[:, :D_MODEL] = _b(dga)
        dg_ref[:, D_MODEL:] = _b(dgs)
        _acc_add(db_ref.at[:, pl.ds(0, D_MODEL)], jnp.sum(dga, axis=0, keepdims=True))
        _acc_add(db_ref.at[:, pl.ds(D_MODEL, D_MODEL)], jnp.sum(dgs, axis=0, keepdims=True))
    return _row_call(body, [dmi, att_o, ssm_o, gl], [b_gate],
                     [(D_MODEL, BF16), (D_MODEL, BF16), (2 * D_MODEL, BF16)], [(1, 2 * D_MODEL)], 256,
                     "gate_bwd")


def _first_bwd(dh1, du, x, w_pre):
    def body(dh_ref, du_ref, x_ref, w_ref, dx_ref, dw_ref):
        dx, dw = _rms_bwd_vals(x_ref[...], w_ref[...], du_ref[...])
        dx_ref[...] = dh_ref[...] + dx
        _acc_add(dw_ref, dw)
    return _row_call(body, [dh1, du, x], [w_pre], [(D_MODEL, F32)], [(1, D_MODEL)], 512, "first_bwd")


def _group_rms(t):
    gw = SSM_INNER // SSM_GROUPS
    out = []
    for g in range(SSM_GROUPS):
        tg = t[:, g * gw:(g + 1) * gw]
        out.append(lax.rsqrt(jnp.mean(tg * tg, axis=-1, keepdims=True) + RMS_EPS))
    return out


def _gnorm_fwd(y, z, w):
    gw = SSM_INNER // SSM_GROUPS

    def body(y_ref, z_ref, w_ref, o_ref):
        zz = z_ref[...]
        t = y_ref[...] * (zz * _sigmoid(zz))
        rs = _group_rms(t)
        for g in range(SSM_GROUPS):
            sl = slice(g * gw, (g + 1) * gw)
            o_ref[:, sl] = _b(t[:, sl] * rs[g] * w_ref[:, sl])
    return _row_call(body, [y, z], [w], [(SSM_INNER, BF16)], [], 256, "gnorm_fwd")[0]


def _gnorm_bwd(dout, y, z, w):
    gw = SSM_INNER // SSM_GROUPS

    def body(d_ref, y_ref, z_ref, w_ref, dy_ref, dz_ref, dw_ref):
        zz = z_ref[...]
        yy = y_ref[...]
        sg = _sigmoid(zz)
        sz = zz * sg
        t = yy * sz
        rs = _group_rms(t)
        for g in range(SSM_GROUPS):
            sl = slice(g * gw, (g + 1) * gw)
            tn = t[:, sl] * rs[g]
            d = d_ref[:, sl]
            gg = d * w_ref[:, sl]
            dt = rs[g] * (gg - tn * jnp.mean(gg * tn, axis=-1, keepdims=True))
            dy_ref[:, sl] = dt * sz[:, sl]
            dz_ref[:, sl] = _b(dt * yy[:, sl] * (sg[:, sl] * (1.0 + zz[:, sl] * (1.0 - sg[:, sl]))))
            _acc_add(dw_ref.at[:, pl.ds(g * gw, gw)], jnp.sum(d * tn, axis=0, keepdims=True))
    return _row_call(body, [dout, y, z], [w], [(SSM_INNER, F32), (SSM_INNER, BF16)], [(1, SSM_INNER)], 256,
                     "gnorm_bwd")


def _to_pat(a, d):
    if d == 1:
        return a
    S, C = a.shape
    return a.reshape(S // d, d, C).transpose(1, 0, 2).reshape(S, C)


def _from_pat(a, d):
    if d == 1:
        return a
    S, C = a.shape
    return a.reshape(d, S // d, C).transpose(1, 0, 2).reshape(S, C)


def _head_col(stat, h):
    return stat[:, h:h + 1]


def _attn_fwd(q, k, v, d):
    S = q.shape[0]
    blk = ATT_BLOCK
    nblk = S // blk
    nbs = nblk // d
    slopes = _alibi_slopes(N_ATT_HEADS)
    scale = HEAD_DIM ** -0.5

    def body(q_ref, kc_ref, kp_ref, vc_ref, vp_ref, o_ref, m_ref, l_ref):
        n = pl.program_id(0)
        has_prev = (n % nbs) != 0
        ii = lax.broadcasted_iota(jnp.int32, (blk, blk), 0)
        jj = lax.broadcasted_iota(jnp.int32, (blk, blk), 1)
        dist_c = (ii - jj).astype(F32)
        dist_p = dist_c + float(blk)
        ok_c = ii >= jj
        ok_p = jnp.logical_and(jj >= ii, has_prev)
        lane = lax.broadcasted_iota(jnp.int32, (blk, LANES), 1)
        m_all = jnp.zeros((blk, LANES), F32)
        l_all = jnp.zeros((blk, LANES), F32)
        for h in range(N_ATT_HEADS):
            sl = slice(h * HEAD_DIM, (h + 1) * HEAD_DIM)
            qh = q_ref[:, sl]
            bias = slopes[h] * float(d)
            sc = jnp.where(ok_c, _dot_nt(qh, kc_ref[:, sl]) * scale - bias * dist_c, NEG_BIG)
            sp = jnp.where(ok_p, _dot_nt(qh, kp_ref[:, sl]) * scale - bias * dist_p, NEG_BIG)
            m = jnp.maximum(jnp.max(sc, axis=-1, keepdims=True), jnp.max(sp, axis=-1, keepdims=True))
            pc = jnp.exp(sc - m)
            pp = jnp.exp(sp - m)
            l = jnp.sum(pc, axis=-1, keepdims=True) + jnp.sum(pp, axis=-1, keepdims=True)
            o_ref[:, sl] = _dot(_b(pc), vc_ref[:, sl]) + _dot(_b(pp), vp_ref[:, sl])
            m_all = jnp.where(lane == h, m, m_all)
            l_all = jnp.where(lane == h, l, l_all)
        m_ref[...] = m_all
        l_ref[...] = l_all

    cur = pl.BlockSpec((blk, ATT_WIDTH), lambda n: (n, 0))
    prev = pl.BlockSpec((blk, ATT_WIDTH), lambda n: (jnp.maximum(n - 1, 0), 0))
    stat = pl.BlockSpec((blk, LANES), lambda n: (n, 0))
    return pl.pallas_call(
        body, name=f"attn_fwd_d{d}", grid=(nblk,),
        in_specs=[cur, cur, prev, cur, prev],
        out_specs=[cur, stat, stat],
        out_shape=[jax.ShapeDtypeStruct((S, ATT_WIDTH), F32), jax.ShapeDtypeStruct((S, LANES), F32),
                   jax.ShapeDtypeStruct((S, LANES), F32)],
        compiler_params=_params(("parallel",)),
    )(q, k, k, v, v)


def _attn_combine(os, ms, ls):
    def body(o1, o2, o3, m1, m2, m3, l1, l2, l3, att_ref, lse_ref):
        mm = [m1[...], m2[...], m3[...]]
        big = jnp.maximum(jnp.maximum(mm[0], mm[1]), mm[2])
        es = [jnp.exp(m - big) for m in mm]
        den = es[0] * l1[...] + es[1] * l2[...] + es[2] * l3[...]
        lse_ref[...] = big + jnp.log(den)
        inv = 1.0 / den
        for h in range(N_ATT_HEADS):
            sl = slice(h * HEAD_DIM, (h + 1) * HEAD_DIM)
            num = (_head_col(es[0], h) * o1[:, sl] + _head_col(es[1], h) * o2[:, sl]
                   + _head_col(es[2], h) * o3[:, sl])
            att_ref[:, sl] = num * _head_col(inv, h)
    return _row_call(body, list(os) + list(ms) + list(ls), [], [(ATT_WIDTH, F32), (LANES, F32)], [], 256,
                     "attn_combine")


def _attn_delta(d_att, att):
    def body(d_ref, a_ref, dl_ref, db_ref):
        dd = d_ref[...]
        prod = dd * a_ref[...]
        lane = lax.broadcasted_iota(jnp.int32, (dd.shape[0], LANES), 1)
        acc = jnp.zeros((dd.shape[0], LANES), F32)
        for h in range(N_ATT_HEADS):
            s = jnp.sum(prod[:, h * HEAD_DIM:(h + 1) * HEAD_DIM], axis=-1, keepdims=True)
            acc = jnp.where(lane == h, s, acc)
        dl_ref[...] = acc
        db_ref[...] = _b(dd)
    return _row_call(body, [d_att, att], [], [(LANES, F32), (ATT_WIDTH, BF16)], [], 512, "attn_delta")


def _attn_bwd(q, k, v, do, lse, delta, d):
    S = q.shape[0]
    blk = ATT_BLOCK
    nblk = S // blk
    nbs = nblk // d
    slopes = _alibi_slopes(N_ATT_HEADS)
    scale = HEAD_DIM ** -0.5

    def body(qc_ref, qn_ref, k_ref, v_ref, doc_ref, don_ref, lc_ref, ln_ref, dc_ref, dn_ref,
             dq_ref, dk_ref, dv_ref, carry_ref):
        n = pl.program_id(0)
        has_next = ((n + 1) % nbs) != 0

        @pl.when(n == 0)
        def _():
            carry_ref[...] = jnp.zeros_like(carry_ref)

        ii = lax.broadcasted_iota(jnp.int32, (blk, blk), 0)
        jj = lax.broadcasted_iota(jnp.int32, (blk, blk), 1)
        dist_c = (ii - jj).astype(F32)
        dist_p = dist_c + float(blk)
        ok_c = ii >= jj
        ok_p = jnp.logical_and(jj >= ii, has_next)
        for h in range(N_ATT_HEADS):
            sl = slice(h * HEAD_DIM, (h + 1) * HEAD_DIM)
            bias = slopes[h] * float(d)
            kh = k_ref[:, sl]
            vh = v_ref[:, sl]
            qh = qc_ref[:, sl]
            doh = doc_ref[:, sl]
            s = jnp.where(ok_c, _dot_nt(qh, kh) * scale - bias * dist_c - _head_col(lc_ref[...], h), NEG_BIG)
            p = jnp.exp(s)
            ds = p * (_dot_nt(doh, vh) - _head_col(dc_ref[...], h)) * scale
            pb, dsb = _b(p), _b(ds)
            dv = _dot_tn(pb, doh)
            dk = _dot_tn(dsb, qh)
            dq_ref[:, sl] = _dot(dsb, kh) + carry_ref[:, sl]
            qh = qn_ref[:, sl]
            doh = don_ref[:, sl]
            s = jnp.where(ok_p, _dot_nt(qh, kh) * scale - bias * dist_p - _head_col(ln_ref[...], h), NEG_BIG)
            p = jnp.exp(s)
            ds = p * (_dot_nt(doh, vh) - _head_col(dn_ref[...], h)) * scale
            pb, dsb = _b(p), _b(ds)
            dv_ref[:, sl] = dv + _dot_tn(pb, doh)
            dk_ref[:, sl] = dk + _dot_tn(dsb, qh)
            carry_ref[:, sl] = _dot(dsb, kh)

    cur = pl.BlockSpec((blk, ATT_WIDTH), lambda n: (n, 0))
    nxt = pl.BlockSpec((blk, ATT_WIDTH), lambda n: (jnp.minimum(n + 1, nblk - 1), 0))
    scur = pl.BlockSpec((blk, LANES), lambda n: (n, 0))
    snxt = pl.BlockSpec((blk, LANES), lambda n: (jnp.minimum(n + 1, nblk - 1), 0))
    shp = jax.ShapeDtypeStruct((S, ATT_WIDTH), F32)
    return pl.pallas_call(
        body, name=f"attn_bwd_d{d}", grid=(nblk,),
        in_specs=[cur, nxt, cur, cur, cur, nxt, scur, snxt, scur, snxt],
        out_specs=[cur, cur, cur],
        out_shape=[shp, shp, shp],
        scratch_shapes=[pltpu.VMEM((blk, ATT_WIDTH), F32)],
        compiler_params=_params(("arbitrary",)),
    )(q, q, k, v, do, do, lse, lse, delta, delta)


def _head_pair_masks(x):
    lane = lax.broadcasted_iota(jnp.int32, x.shape, 1)
    zero = jnp.zeros_like(x)
    return jnp.where(lane < HEAD_DIM, x, zero), jnp.where(lane >= HEAD_DIM, x, zero)


def _attn_fwd2(q, k, v, d):
    S = q.shape[0]
    blk = ATT_BLOCK
    nblk = S // blk
    nbs = nblk // d
    slopes = _alibi_slopes(N_ATT_HEADS)
    scale = HEAD_DIM ** -0.5

    def body(q_ref, kc_ref, kp_ref, vc_ref, vp_ref, o_ref, m_ref, l_ref):
        n = pl.program_id(0)
        has_prev = (n % nbs) != 0
        ii = lax.broadcasted_iota(jnp.int32, (blk, 2 * blk), 0)
        jj = lax.broadcasted_iota(jnp.int32, (blk, 2 * blk), 1)
        dist_i = blk + ii - jj
        dist = dist_i.astype(F32)
        ok = jnp.logical_and(jnp.logical_and(dist_i >= 0, dist_i <= blk), jnp.logical_or(jj >= blk, has_prev))
        lane = lax.broadcasted_iota(jnp.int32, (blk, LANES), 1)
        m_all = jnp.zeros((blk, LANES), F32)
        l_all = jnp.zeros((blk, LANES), F32)
        for pr in range(N_ATT_HEADS // 2):
            sl = slice(pr * LANES, (pr + 1) * LANES)
            kcat = jnp.concatenate([kp_ref[:, sl], kc_ref[:, sl]], axis=0)
            vcat = jnp.concatenate([vp_ref[:, sl], vc_ref[:, sl]], axis=0)
            ps = []
            for h, qh in zip((2 * pr, 2 * pr + 1), _head_pair_masks(q_ref[:, sl])):
                s = jnp.where(ok, _dot_nt(qh, kcat) * scale - (slopes[h] * float(d)) * dist, NEG_BIG)
                m = jnp.max(s, axis=-1, keepdims=True)
                p = jnp.exp(s - m)
                l = jnp.sum(p, axis=-1, keepdims=True)
                m_all = jnp.where(lane == h, m, m_all)
                l_all = jnp.where(lane == h, l, l_all)
                ps.append(_b(p))
            o_ref[:, sl] = _dot(jnp.concatenate(ps, axis=1), jnp.concatenate(_head_pair_masks(vcat), axis=0))
        m_ref[...] = m_all
        l_ref[...] = l_all

    cur = pl.BlockSpec((blk, ATT_WIDTH), lambda n: (n, 0))
    prev = pl.BlockSpec((blk, ATT_WIDTH), lambda n: (jnp.maximum(n - 1, 0), 0))
    stat = pl.BlockSpec((blk, LANES), lambda n: (n, 0))
    return pl.pallas_call(
        body, name=f"attn_fwd_d{d}", grid=(nblk,),
        in_specs=[cur, cur, prev, cur, prev],
        out_specs=[cur, stat, stat],
        out_shape=[jax.ShapeDtypeStruct((S, ATT_WIDTH), F32), jax.ShapeDtypeStruct((S, LANES), F32),
                   jax.ShapeDtypeStruct((S, LANES), F32)],
        compiler_params=_params(("parallel",)),
    )(q, k, k, v, v)


def _attn_bwd2(q, k, v, do, lse, delta, d):
    S = q.shape[0]
    blk = ATT_BLOCK
    nblk = S // blk
    nbs = nblk // d
    slopes = _alibi_slopes(N_ATT_HEADS)
    scale = HEAD_DIM ** -0.5

    def body(qc_ref, qn_ref, k_ref, v_ref, doc_ref, don_ref, lc_ref, ln_ref, dc_ref, dn_ref,
             dq_ref, dk_ref, dv_ref, carry_ref):
        n = pl.program_id(0)
        has_next = ((n + 1) % nbs) != 0

        @pl.when(n == 0)
        def _():
            carry_ref[...] = jnp.zeros_like(carry_ref)

        rr = lax.broadcasted_iota(jnp.int32, (2 * blk, blk), 0)
        jj = lax.broadcasted_iota(jnp.int32, (2 * blk, blk), 1)
        dist_i = rr - jj
        dist = dist_i.astype(F32)
        ok = jnp.logical_or(jnp.logical_and(rr < blk, dist_i >= 0),
                            jnp.logical_and(jnp.logical_and(rr >= blk, dist_i <= blk), has_next))
        lcat = jnp.concatenate([lc_ref[...], ln_ref[...]], axis=0)
        dcat = jnp.concatenate([dc_ref[...], dn_ref[...]], axis=0)
        for pr in range(N_ATT_HEADS // 2):
            sl = slice(pr * LANES, (pr + 1) * LANES)
            qcat = jnp.concatenate([qc_ref[:, sl], qn_ref[:, sl]], axis=0)
            docat = jnp.concatenate([doc_ref[:, sl], don_ref[:, sl]], axis=0)
            k2 = k_ref[:, sl]
            v2 = v_ref[:, sl]
            qm = _head_pair_masks(qcat)
            dom = _head_pair_masks(docat)
            pbs, dsbs = [], []
            for h, qh, doh in zip((2 * pr, 2 * pr + 1), qm, dom):
                s = jnp.where(ok, _dot_nt(qh, k2) * scale - (slopes[h] * float(d)) * dist - lcat[:, h:h + 1],
                              NEG_BIG)
                p = jnp.exp(s)
                ds = p * (_dot_nt(doh, v2) - dcat[:, h:h + 1]) * scale
                pbs.append(_b(p))
                dsbs.append(_b(ds))
            dv_ref[:, sl] = _dot_tn(jnp.concatenate(pbs, axis=0), jnp.concatenate(dom, axis=0))
            dk_ref[:, sl] = _dot_tn(jnp.concatenate(dsbs, axis=0), jnp.concatenate(qm, axis=0))
            dq = _dot(jnp.concatenate(dsbs, axis=1), jnp.concatenate(_head_pair_masks(k2), axis=0))
            dq_ref[:, sl] = dq[:blk] + carry_ref[:, sl]
            carry_ref[:, sl] = dq[blk:]

    cur = pl.BlockSpec((blk, ATT_WIDTH), lambda n: (n, 0))
    nxt = pl.BlockSpec((blk, ATT_WIDTH), lambda n: (jnp.minimum(n + 1, nblk - 1), 0))
    scur = pl.BlockSpec((blk, LANES), lambda n: (n, 0))
    snxt = pl.BlockSpec((blk, LANES), lambda n: (jnp.minimum(n + 1, nblk - 1), 0))
    shp = jax.ShapeDtypeStruct((S, ATT_WIDTH), F32)
    return pl.pallas_call(
        body, name=f"attn_bwd_d{d}", grid=(nblk,),
        in_specs=[cur, nxt, cur, cur, cur, nxt, scur, snxt, scur, snxt],
        out_specs=[cur, cur, cur],
        out_shape=[shp, shp, shp],
        scratch_shapes=[pltpu.VMEM((blk, ATT_WIDTH), F32)],
        compiler_params=_params(("arbitrary",)),
    )(q, q, k, v, do, do, lse, lse, delta, delta)


def _sum_qkv(dqs, dks, dvs):
    def body(q1, q2, q3, k1, k2, k3, v1, v2, v3, o_ref):
        o_ref[:, 0:ATT_WIDTH] = _b(q1[...] + q2[...] + q3[...])
        o_ref[:, ATT_WIDTH:2 * ATT_WIDTH] = _b(k1[...] + k2[...] + k3[...])
        o_ref[:, 2 * ATT_WIDTH:] = _b(v1[...] + v2[...] + v3[...])
    return _row_call(body, list(dqs) + list(dks) + list(dvs), [], [(3 * ATT_WIDTH, BF16)], [], 256,
                     "sum_dqkv")[0]


CONV_COLS = 1024
CONV_ROWS = 512
HALO = 8


def _conv_fwd(xbc, conv_w, conv_b):
    S, C = xbc.shape
    bs, bc = CONV_ROWS, CONV_COLS
    nr = S // bs

    def body(x_ref, halo_ref, w_ref, b_ref, o_ref, xs_ref):
        r = pl.program_id(1)
        xs_ref[pl.ds(HALO, bs), :] = x_ref[...]
        xs_ref[pl.ds(0, HALO), :] = jnp.where(r > 0, halo_ref[...], 0.0)
        pre = b_ref[...] + w_ref[3:4, :] * x_ref[...]
        for j in range(SSM_CONV - 1):
            pre = pre + w_ref[j:j + 1, :] * xs_ref[pl.ds(HALO - 3 + j, bs), :]
        o_ref[...] = pre * _sigmoid(pre)

    return pl.pallas_call(
        body, name="conv_fwd", grid=(C // bc, nr),
        in_specs=[pl.BlockSpec((bs, bc), lambda c, r: (r, c)),
                  pl.BlockSpec((HALO, bc), lambda c, r: (jnp.maximum(r * (bs // HALO) - 1, 0), c)),
                  pl.BlockSpec((SSM_CONV, bc), lambda c, r: (0, c)),
                  pl.BlockSpec((1, bc), lambda c, r: (0, c))],
        out_specs=pl.BlockSpec((bs, bc), lambda c, r: (r, c)),
        out_shape=jax.ShapeDtypeStruct((S, C), F32),
        scratch_shapes=[pltpu.VMEM((bs + HALO, bc), F32)],
        compiler_params=_params(("parallel", "arbitrary")),
    )(xbc, xbc, conv_w, conv_b)


def _conv_bwd(xbc, dact, conv_w, conv_b, col0):
    S, C = xbc.shape
    Cp = dact.shape[1]
    bs, bc = CONV_ROWS, min(CONV_COLS, Cp)
    nr = S // bs
    cb0 = col0 // bc
    last_halo = S // HALO - 1

    def body(x_ref, xp_ref, xn_ref, d_ref, dn_ref, w_ref, b_ref, dx_ref, dw_ref, db_ref,
             xs_ref, dp_ref):
        r = pl.program_id(1)
        xs_ref[pl.ds(0, HALO), :] = jnp.where(r > 0, xp_ref[...], 0.0)
        xs_ref[pl.ds(HALO, bs), :] = x_ref[...]
        xs_ref[pl.ds(HALO + bs, HALO), :] = xn_ref[...]
        ext = bs + HALO
        pre = b_ref[...] + jnp.zeros((ext, bc), F32)
        for j in range(SSM_CONV):
            pre = pre + w_ref[j:j + 1, :] * xs_ref[pl.ds(HALO - 3 + j, ext), :]
        sg = _sigmoid(pre)
        dsilu = sg * (1.0 + pre * (1.0 - sg))
        dp_ref[pl.ds(0, bs), :] = d_ref[...] * dsilu[:bs]
        dp_ref[pl.ds(bs, HALO), :] = jnp.where(r < nr - 1, dn_ref[...], 0.0) * dsilu[bs:]
        dx = jnp.zeros((bs, bc), F32)
        for j in range(SSM_CONV):
            dx = dx + w_ref[j:j + 1, :] * dp_ref[pl.ds(3 - j, bs), :]
        dx_ref[...] = _b(dx)
        dpre = dp_ref[pl.ds(0, bs), :]
        for j in range(SSM_CONV):
            part = jnp.sum(dpre * xs_ref[pl.ds(HALO - 3 + j, bs), :], axis=0, keepdims=True)

            @pl.when(r == 0)
            def _():
                dw_ref[j:j + 1, :] = part

            @pl.when(r > 0)
            def _():
                dw_ref[j:j + 1, :] += part
        part = jnp.sum(dpre, axis=0, keepdims=True)

        @pl.when(r == 0)
        def _():
            db_ref[...] = part

        @pl.when(r > 0)
        def _():
            db_ref[...] += part

    hb = bs // HALO
    return pl.pallas_call(
        body, name=f"conv_bwd_{col0}", grid=(Cp // bc, nr),
        in_specs=[pl.BlockSpec((bs, bc), lambda c, r: (r, cb0 + c)),
                  pl.BlockSpec((HALO, bc), lambda c, r: (jnp.maximum(r * hb - 1, 0), cb0 + c)),
                  pl.BlockSpec((HALO, bc), lambda c, r: (jnp.minimum((r + 1) * hb, last_halo), cb0 + c)),
                  pl.BlockSpec((bs, bc), lambda c, r: (r, c)),
                  pl.BlockSpec((HALO, bc), lambda c, r: (jnp.minimum((r + 1) * hb, last_halo), c)),
                  pl.BlockSpec((SSM_CONV, bc), lambda c, r: (0, cb0 + c)),
                  pl.BlockSpec((1, bc), lambda c, r: (0, cb0 + c))],
        out_specs=[pl.BlockSpec((bs, bc), lambda c, r: (r, c)),
                   pl.BlockSpec((SSM_CONV, bc), lambda c, r: (0, c)),
                   pl.BlockSpec((1, bc), lambda c, r: (0, c))],
        out_shape=[jax.ShapeDtypeStruct((S, Cp), BF16), jax.ShapeDtypeStruct((SSM_CONV, Cp), F32),
                   jax.ShapeDtypeStruct((1, Cp), F32)],
        scratch_shapes=[pltpu.VMEM((bs + 2 * HALO, bc), F32), pltpu.VMEM((bs + HALO, bc), F32)],
        compiler_params=_params(("parallel", "arbitrary")),
    )(xbc, xbc, xbc, dact, dact, conv_w, conv_b)


def _softplus(x):
    return jnp.maximum(x, 0.0) + jnp.log(1.0 + jnp.exp(-jnp.abs(x)))


def _ssd_common(dtr_ref, bias_ref, a_ref, g):
    ch = SSM_CHUNK
    x = dtr_ref[...] + bias_ref[...]
    dt_all = _softplus(x)
    r = lax.broadcasted_iota(jnp.int32, (LANES, LANES), 0)
    c = lax.broadcasted_iota(jnp.int32, (LANES, LANES), 1)
    sel = jnp.where(jnp.logical_and(r == HEADS_PER_GROUP * g + c, c < HEADS_PER_GROUP), 1.0, 0.0)
    dt4 = _dot_hi(dt_all, sel)
    la4 = _dot_hi(dt_all * a_ref[...], sel)
    ii = lax.broadcasted_iota(jnp.int32, (ch, ch), 0)
    jj = lax.broadcasted_iota(jnp.int32, (ch, ch), 1)
    tril = jnp.where(ii >= jj, 1.0, 0.0)
    acs = _dot_hi(tril, la4)
    return x, sel, dt4, acs, acs.T, ii >= jj


def _row8(v):
    return jnp.broadcast_to(v, (8, v.shape[1]))


def _ssd_fwd(xact, dt_raw, dt_bias, a_neg, d_skip):
    S = xact.shape[0]
    ch = SSM_CHUNK
    nch = S // ch
    hg = HEADS_PER_GROUP
    gw = hg * SSM_HEAD_DIM
    b_off = SSM_INNER // SSM_STATE
    c_off = b_off + SSM_GROUPS

    def body(x_ref, b_ref, c_ref, dtr_ref, bias_ref, a_ref, dsk_ref, y_ref, hs_ref, h_ref):
        c = pl.program_id(0)
        g = pl.program_id(1)

        @pl.when(jnp.logical_and(c == 0, g == 0))
        def _():
            h_ref[...] = jnp.zeros_like(h_ref)

        _, sel, dt4, acs, acs_t, low = _ssd_common(dtr_ref, bias_ref, a_ref, g)
        dsk4 = _dot_hi(_row8(dsk_ref[...]), sel)
        bb = _b(b_ref[...])
        cc = _b(c_ref[...])
        cb = _dot_nt(cc, bb)
        for j in range(hg):
            sl = slice(j * SSM_HEAD_DIM, (j + 1) * SSM_HEAD_DIM)
            acol = acs[:, j:j + 1]
            arow = acs_t[j:j + 1, :]
            alast = acs[ch - 1:ch, j:j + 1]
            decay = jnp.exp(jnp.where(low, acol - arow, -jnp.inf))
            xh = x_ref[:, sl]
            xd = xh * dt4[:, j:j + 1]
            hj = h_ref[hg * g + j]
            y = _dot(_b(cb * decay), _b(xd))
            y = y + _dot_nt(cc, _b(hj)) * jnp.exp(acol)
            y_ref[:, sl] = y + dsk4[0:1, j:j + 1] * xh
            hs_ref[0, j] = hj
            st = _dot_tn(_b(xd * jnp.exp(alast - acol)), bb)
            h_ref[hg * g + j] = hj * jnp.exp(alast) + st

    small = pl.BlockSpec((1, LANES), lambda c, g: (0, 0))
    return pl.pallas_call(
        body, name="ssd_fwd", grid=(nch, SSM_GROUPS),
        in_specs=[pl.BlockSpec((ch, gw), lambda c, g: (c, g)),
                  pl.BlockSpec((ch, SSM_STATE), lambda c, g: (c, b_off + g)),
                  pl.BlockSpec((ch, SSM_STATE), lambda c, g: (c, c_off + g)),
                  pl.BlockSpec((ch, LANES), lambda c, g: (c, 0)),
                  small, small, small],
        out_specs=[pl.BlockSpec((ch, gw), lambda c, g: (c, g)),
                   pl.BlockSpec((1, hg, SSM_HEAD_DIM, SSM_STATE), lambda c, g: (c, g, 0, 0))],
        out_shape=[jax.ShapeDtypeStruct((S, SSM_INNER), F32),
                   jax.ShapeDtypeStruct((nch, SSM_HEADS, SSM_HEAD_DIM, SSM_STATE), F32)],
        scratch_shapes=[pltpu.VMEM((SSM_HEADS, SSM_HEAD_DIM, SSM_STATE), F32)],
        compiler_params=_params(("arbitrary", "arbitrary")),
    )(xact, xact, xact, dt_raw, dt_bias, a_neg, d_skip)


def _ssd_bwd(xact, dt_raw, dt_bias, a_neg, d_skip, hs, dy):
    S = xact.shape[0]
    ch = SSM_CHUNK
    nch = S // ch
    hg = HEADS_PER_GROUP
    gw = hg * SSM_HEAD_DIM
    b_off = SSM_INNER // SSM_STATE
    c_off = b_off + SSM_GROUPS

    def body(x_ref, b_ref, c_ref, dtr_ref, bias_ref, a_ref, dsk_ref, hs_ref, dy_ref,
             dx_ref, db_ref, dc_ref, ddt_ref, st_ref, dh_ref, ddt_acc):
        step = pl.program_id(0)
        g = pl.program_id(1)

        @pl.when(jnp.logical_and(step == 0, g == 0))
        def _():
            dh_ref[...] = jnp.zeros_like(dh_ref)
            st_ref[...] = jnp.zeros_like(st_ref)

        @pl.when(g == 0)
        def _():
            ddt_acc[...] = jnp.zeros_like(ddt_acc)

        xraw, sel, dt4, acs, acs_t, low = _ssd_common(dtr_ref, bias_ref, a_ref, g)
        a4 = _dot_hi(_row8(a_ref[...]), sel)[0:1, :]
        dsk4 = _dot_hi(_row8(dsk_ref[...]), sel)
        bf = b_ref[...]
        cf = c_ref[...]
        bb = _b(bf)
        cc = _b(cf)
        cb = _dot_nt(cc, bb)
        lane = lax.broadcasted_iota(jnp.int32, (ch, LANES), 1)
        rowi = lax.broadcasted_iota(jnp.int32, (ch, 1), 0)
        ones = jnp.ones((ch, LANES), F32)
        dcb = jnp.zeros((ch, ch), F32)
        dc_acc = jnp.zeros((ch, SSM_STATE), F32)
        db_acc = jnp.zeros((ch, SSM_STATE), F32)
        dacs4 = jnp.zeros((ch, LANES), F32)
        ddt4 = jnp.zeros((ch, LANES), F32)
        dd4 = jnp.zeros((1, LANES), F32)
        lane1 = lax.broadcasted_iota(jnp.int32, (1, LANES), 1)
        for j in range(hg):
            sl = slice(j * SSM_HEAD_DIM, (j + 1) * SSM_HEAD_DIM)
            acol = acs[:, j:j + 1]
            arow = acs_t[j:j + 1, :]
            alast = acs[ch - 1:ch, j:j + 1]
            decay = jnp.exp(jnp.where(low, acol - arow, -jnp.inf))
            ea = jnp.exp(acol)
            dsd = jnp.exp(alast - acol)
            cd = jnp.exp(alast)
            dtc = dt4[:, j:j + 1]
            xh = x_ref[:, sl]
            xd = xh * dtc
            xdb = _b(xd)
            hj = hs_ref[0, j]
            hjb = _b(hj)
            dhn = dh_ref[hg * g + j]
            dyj = dy_ref[:, sl]
            dyb = _b(dyj)
            lm = cb * decay
            dxh = dsk4[0:1, j:j + 1] * dyj
            dd4 = jnp.where(lane1 == j, jnp.sum(jnp.sum(dyj * xh, axis=1, keepdims=True), axis=0,
                                                keepdims=True), dd4)
            dlm = _dot_nt(dyb, xdb)
            dxd = _dot_tn(_b(lm), dyb)
            gm = dlm * lm
            dcb = dcb + dlm * decay
            dac = jnp.sum(gm, axis=1, keepdims=True) - _dot_tn_hi(gm, ones)[:, 0:1]
            zz = _dot_nt(cc, hjb)
            dzb = _b(dyj * ea)
            dac = dac + jnp.sum(dyj * zz, axis=1, keepdims=True) * ea
            dc_acc = dc_acc + _dot(dzb, hjb)
            dh_in = _dot_tn(dzb, cc)
            dsb = _b(dhn)
            ww = _dot_nt(bb, dsb)
            dxd = dxd + ww * dsd
            dds = jnp.sum(ww * xd, axis=1, keepdims=True) * dsd
            db_acc = db_acc + _dot(_b(xd * dsd), dsb)
            dac = dac - dds
            dal = (jnp.sum(dds, axis=0, keepdims=True)
                   + jnp.sum(jnp.sum(dhn * hj, axis=1, keepdims=True), axis=0, keepdims=True) * cd)
            dh_ref[hg * g + j] = dh_in + dhn * cd
            dac = dac + jnp.where(rowi == ch - 1, dal, 0.0)
            dacs4 = jnp.where(lane == j, dac, dacs4)
            dx_ref[:, sl] = dxh + dxd * dtc
            ddt4 = jnp.where(lane == j, jnp.sum(dxd * xh, axis=1, keepdims=True), ddt4)
        dcbb = _b(dcb)
        dc_ref[...] = dc_acc + _dot(dcbb, bb)
        db_ref[...] = db_acc + _dot_tn(dcbb, cc)
        ii = lax.broadcasted_iota(jnp.int32, (ch, ch), 0)
        jj = lax.broadcasted_iota(jnp.int32, (ch, ch), 1)
        triu = jnp.where(ii <= jj, 1.0, 0.0)
        dla4 = _dot_hi(triu, dacs4)
        ddt4 = ddt4 + dla4 * a4
        da4 = jnp.sum(dla4 * dt4, axis=0, keepdims=True) * a4
        sel_t = sel.T
        ddt_raw = _dot_hi(ddt4, sel_t) * _sigmoid(xraw)
        ddt_acc[...] += ddt_raw
        st_ref[0:1, :] += _dot_hi(_row8(da4), sel_t)[0:1, :]
        st_ref[1:2, :] += _dot_hi(_row8(dd4), sel_t)[0:1, :]
        st_ref[2:3, :] += jnp.sum(ddt_raw, axis=0, keepdims=True)

        @pl.when(g == SSM_GROUPS - 1)
        def _():
            ddt_ref[...] = _b(ddt_acc[...])

    small = pl.BlockSpec((1, LANES), lambda s, g: (0, 0))
    rc = lambda s: nch - 1 - s
    return pl.pallas_call(
        body, name="ssd_bwd", grid=(nch, SSM_GROUPS),
        in_specs=[pl.BlockSpec((ch, gw), lambda s, g: (rc(s), g)),
                  pl.BlockSpec((ch, SSM_STATE), lambda s, g: (rc(s), b_off + g)),
                  pl.BlockSpec((ch, SSM_STATE), lambda s, g: (rc(s), c_off + g)),
                  pl.BlockSpec((ch, LANES), lambda s, g: (rc(s), 0)),
                  small, small, small,
                  pl.BlockSpec((1, hg, SSM_HEAD_DIM, SSM_STATE), lambda s, g: (rc(s), g, 0, 0)),
                  pl.BlockSpec((ch, gw), lambda s, g: (rc(s), g))],
        out_specs=[pl.BlockSpec((ch, gw), lambda s, g: (rc(s), g)),
                   pl.BlockSpec((ch, SSM_STATE), lambda s, g: (rc(s), g)),
                   pl.BlockSpec((ch, SSM_STATE), lambda s, g: (rc(s), g)),
                   pl.BlockSpec((ch, LANES), lambda s, g: (rc(s), 0)),
                   pl.BlockSpec((8, LANES), lambda s, g: (0, 0))],
        out_shape=[jax.ShapeDtypeStruct((S, SSM_INNER), F32),
                   jax.ShapeDtypeStruct((S, SSM_GROUPS * SSM_STATE), F32),
                   jax.ShapeDtypeStruct((S, SSM_GROUPS * SSM_STATE), F32),
                   jax.ShapeDtypeStruct((S, LANES), BF16),
                   jax.ShapeDtypeStruct((8, LANES), F32)],
        scratch_shapes=[pltpu.VMEM((SSM_HEADS, SSM_HEAD_DIM, SSM_STATE), F32),
                        pltpu.VMEM((ch, LANES), F32)],
        compiler_params=_params(("arbitrary", "arbitrary")),
    )(xact, xact, xact, dt_raw, dt_bias, a_neg, d_skip, hs, dy)


GROUP_W = HEADS_PER_GROUP * SSM_HEAD_DIM
B_COL0 = SSM_INNER
C_COL0 = SSM_INNER + SSM_GROUPS * SSM_STATE


def _ssd_prep(dt_raw, dt_bias, a_neg):
    S = dt_raw.shape[0]
    ch = SSM_CHUNK
    nch = S // ch

    def body(dtr_ref, bias_ref, a_ref, dt_ref, acs_ref, acst_ref, sig_ref):
        x = dtr_ref[...] + bias_ref[...]
        lane = lax.broadcasted_iota(jnp.int32, (ch, LANES), 1)
        dt = jnp.where(lane < SSM_HEADS, _softplus(x), 0.0)
        ii = lax.broadcasted_iota(jnp.int32, (ch, ch), 0)
        jj = lax.broadcasted_iota(jnp.int32, (ch, ch), 1)
        acs = _dot_hi(jnp.where(ii >= jj, 1.0, 0.0), dt * a_ref[...])
        dt_ref[...] = dt
        acs_ref[...] = acs
        acst_ref[0] = acs.T[0:SSM_HEADS, :]
        sig_ref[...] = _sigmoid(x)

    blk = pl.BlockSpec((ch, LANES), lambda c: (c, 0))
    small = pl.BlockSpec((1, LANES), lambda c: (0, 0))
    shp = jax.ShapeDtypeStruct((S, LANES), F32)
    return pl.pallas_call(
        body, name="ssd_prep", grid=(nch,),
        in_specs=[blk, small, small],
        out_specs=[blk, blk, pl.BlockSpec((1, SSM_HEADS, ch), lambda c: (c, 0, 0)), blk],
        out_shape=[shp, shp, jax.ShapeDtypeStruct((nch, SSM_HEADS, ch), F32), shp],
        compiler_params=_params(("parallel",)),
    )(dt_raw, dt_bias, a_neg)


def _expand_heads(arr, g, rows):
    lane = lax.broadcasted_iota(jnp.int32, (rows, GROUP_W), 1) // SSM_HEAD_DIM
    h0 = HEADS_PER_GROUP * g
    out = jnp.broadcast_to(arr[:, h0:h0 + 1], (rows, GROUP_W))
    for j in range(1, HEADS_PER_GROUP):
        out = jnp.where(lane == j, arr[:, h0 + j:h0 + j + 1], out)
    return out


def _seg_matrix(k, lanes_per_head, h0):
    r = lax.broadcasted_iota(jnp.int32, (k, LANES), 0)
    c = lax.broadcasted_iota(jnp.int32, (k, LANES), 1)
    return jnp.where(c == h0 + r // lanes_per_head, 1.0, 0.0).astype(BF16)


def _seg_dot(t, e):
    hi = _b(t)
    lo = _b(t - hi.astype(F32))
    return _dot(hi, e) + _dot(lo, e)


def _head_sums(t, e, rows):
    if rows >= 8:
        return _seg_dot(t, e)
    return _seg_dot(jnp.broadcast_to(t, (8, t.shape[1])), e)[0:rows]


def _pair_masks(x):
    lane = lax.broadcasted_iota(jnp.int32, x.shape, 1)
    zero = jnp.zeros_like(x)
    return jnp.where(lane < SSM_HEAD_DIM, x, zero), jnp.where(lane >= SSM_HEAD_DIM, x, zero)


def _ssd_fwd2(xact, dt, acs, acst, dsk_e):
    S = xact.shape[0]
    ch = SSM_CHUNK
    nch = S // ch

    def body(x_ref, dt_ref, acs_ref, acst_ref, dsk_ref, y_ref, hs_ref, h_ref):
        c = pl.program_id(0)

        @pl.when(c == 0)
        def _():
            h_ref[...] = jnp.zeros_like(h_ref)

        dt_all = dt_ref[...]
        acs_all = acs_ref[...]
        acst_all = acst_ref[0]
        alast = acs_all[ch - 1:ch, :]
        eacs = jnp.exp(acs_all)
        dsd_all = jnp.exp(alast - acs_all)
        cd_all = jnp.exp(alast)
        ii = lax.broadcasted_iota(jnp.int32, (ch, ch), 0)
        jj = lax.broadcasted_iota(jnp.int32, (ch, ch), 1)
        low = ii >= jj
        for g in range(SSM_GROUPS):
            xs = x_ref[:, g * GROUP_W:(g + 1) * GROUP_W]
            bb = _b(x_ref[:, B_COL0 + g * SSM_STATE:B_COL0 + (g + 1) * SSM_STATE])
            cc = _b(x_ref[:, C_COL0 + g * SSM_STATE:C_COL0 + (g + 1) * SSM_STATE])
            cb = _dot_nt(cc, bb)
            xd = xs * _expand_heads(dt_all, g, ch)
            xdb = _b(xd)
            ht = h_ref[g]
            rest = (_dot(cc, _b(ht)) * _expand_heads(eacs, g, ch)
                    + dsk_ref[:, g * GROUP_W:(g + 1) * GROUP_W] * xs)
            for p in range(HEADS_PER_GROUP // 2):
                lms = []
                for h in (HEADS_PER_GROUP * g + 2 * p, HEADS_PER_GROUP * g + 2 * p + 1):
                    diff = acs_all[:, h:h + 1] - acst_all[h:h + 1, :]
                    lms.append(_b(cb * jnp.exp(jnp.where(low, diff, -jnp.inf))))
                xa, xb = _pair_masks(xdb[:, p * LANES:(p + 1) * LANES])
                yp = _dot(jnp.concatenate(lms, axis=1), jnp.concatenate([xa, xb], axis=0))
                y_ref[:, g * GROUP_W + p * LANES:g * GROUP_W + (p + 1) * LANES] = (
                    yp + rest[:, p * LANES:(p + 1) * LANES])
            hs_ref[0, g] = ht
            st = _dot_tn(bb, _b(xd * _expand_heads(dsd_all, g, ch)))
            h_ref[g] = ht * _expand_heads(cd_all, g, 1) + st

    blk = pl.BlockSpec((ch, LANES), lambda c: (c, 0))
    return pl.pallas_call(
        body, name="ssd_fwd", grid=(nch,),
        in_specs=[pl.BlockSpec((ch, CONV_DIM), lambda c: (c, 0)), blk, blk,
                  pl.BlockSpec((1, SSM_HEADS, ch), lambda c: (c, 0, 0)),
                  pl.BlockSpec((1, SSM_INNER), lambda c: (0, 0))],
        out_specs=[pl.BlockSpec((ch, SSM_INNER), lambda c: (c, 0)),
                   pl.BlockSpec((1, SSM_GROUPS, SSM_STATE, GROUP_W), lambda c: (c, 0, 0, 0))],
        out_shape=[jax.ShapeDtypeStruct((S, SSM_INNER), F32),
                   jax.ShapeDtypeStruct((nch, SSM_GROUPS, SSM_STATE, GROUP_W), F32)],
        scratch_shapes=[pltpu.VMEM((SSM_GROUPS, SSM_STATE, GROUP_W), F32)],
        compiler_params=_params(("arbitrary",)),
    )(xact, dt, acs, acst, dsk_e)


def _ssd_bwd2(xact, dt, acs, acst, sig, a_neg, dsk_e, hs, dy):
    S = xact.shape[0]
    ch = SSM_CHUNK
    nch = S // ch

    def body(x_ref, dt_ref, acs_ref, acst_ref, sig_ref, a_ref, dsk_ref, hs_ref, dy_ref,
             dx_ref, ddt_ref, st_ref, dh_ref, rows_ref):
        step = pl.program_id(0)

        @pl.when(step == 0)
        def _():
            dh_ref[...] = jnp.zeros_like(dh_ref)
            st_ref[...] = jnp.zeros_like(st_ref)
            rows_ref[...] = jnp.zeros_like(rows_ref)

        dt_all = dt_ref[...]
        acs_all = acs_ref[...]
        acst_all = acst_ref[0]
        alast = acs_all[ch - 1:ch, :]
        eacs = jnp.exp(acs_all)
        dsd_all = jnp.exp(alast - acs_all)
        cd_all = jnp.exp(alast)
        ii = lax.broadcasted_iota(jnp.int32, (ch, ch), 0)
        jj = lax.broadcasted_iota(jnp.int32, (ch, ch), 1)
        low = ii >= jj
        lane = lax.broadcasted_iota(jnp.int32, (ch, LANES), 1)
        cols = jnp.zeros((ch, LANES), F32)
        ddt = jnp.zeros((ch, LANES), F32)
        dal = jnp.zeros((1, LANES), F32)
        ddsk = jnp.zeros((1, LANES), F32)
        for g in range(SSM_GROUPS):
            xs = x_ref[:, g * GROUP_W:(g + 1) * GROUP_W]
            bb = _b(x_ref[:, B_COL0 + g * SSM_STATE:B_COL0 + (g + 1) * SSM_STATE])
            cc = _b(x_ref[:, C_COL0 + g * SSM_STATE:C_COL0 + (g + 1) * SSM_STATE])
            cb = _dot_nt(cc, bb)
            dt_e = _expand_heads(dt_all, g, ch)
            ea_e = _expand_heads(eacs, g, ch)
            dsd_e = _expand_heads(dsd_all, g, ch)
            cd_e = _expand_heads(cd_all, g, 1)
            xd = xs * dt_e
            xdb = _b(xd)
            dyg = dy_ref[:, g * GROUP_W:(g + 1) * GROUP_W]
            dyb = _b(dyg)
            ht = hs_ref[0, g]
            htb = _b(ht)
            dhn = dh_ref[g]
            dhnb = _b(dhn)
            zz = _dot(cc, htb)
            dzb = _b(dyg * ea_e)
            d_c = _dot_nt(dzb, htb)
            dh_in = _dot_tn(cc, dzb)
            ww = _dot(bb, dhnb)
            xdd = xd * dsd_e
            d_b = _dot_nt(_b(xdd), dhnb)
            t2 = ww * xdd
            e_g = _seg_matrix(GROUP_W, SSM_HEAD_DIM, HEADS_PER_GROUP * g)
            cols = cols + _head_sums(dyg * zz * ea_e - t2, e_g, ch)
            dal = dal + _head_sums(jnp.sum(t2, axis=0, keepdims=True), e_g, 1) + cd_all * _head_sums(
                jnp.sum(dhn * ht, axis=0, keepdims=True), e_g, 1)
            dh_ref[g] = dh_in + dhn * cd_e
            ddsk = ddsk + _head_sums(jnp.sum(dyg * xs, axis=0, keepdims=True), e_g, 1)
            dxd_rest = ww * dsd_e
            dcb = jnp.zeros((ch, ch), F32)
            for p in range(HEADS_PER_GROUP // 2):
                dya, dyb2 = _pair_masks(dyb[:, p * LANES:(p + 1) * LANES])
                xp = xdb[:, p * LANES:(p + 1) * LANES]
                lms, gms = [], []
                for h, dyh in ((HEADS_PER_GROUP * g + 2 * p, dya), (HEADS_PER_GROUP * g + 2 * p + 1, dyb2)):
                    diff = acs_all[:, h:h + 1] - acst_all[h:h + 1, :]
                    decay = jnp.exp(jnp.where(low, diff, -jnp.inf))
                    lm = cb * decay
                    dlm = _dot_nt(dyh, xp)
                    gm = dlm * lm
                    dcb = dcb + dlm * decay
                    rows_ref[h:h + 1, :] = jnp.sum(gm, axis=0, keepdims=True)
                    lms.append(_b(lm))
                    gms.append(gm)
                h0 = HEADS_PER_GROUP * g + 2 * p
                cols = cols + _head_sums(jnp.concatenate(gms, axis=1), _seg_matrix(2 * ch, ch, h0), ch)
                dxd = _dot_tn(jnp.concatenate(lms, axis=0), jnp.concatenate([dya, dyb2], axis=0))
                dxd = dxd + dxd_rest[:, p * LANES:(p + 1) * LANES]
                sl = slice(g * GROUP_W + p * LANES, g * GROUP_W + (p + 1) * LANES)
                dx_ref[:, sl] = (dsk_ref[:, sl] * dyg[:, p * LANES:(p + 1) * LANES]
                                 + dxd * dt_e[:, p * LANES:(p + 1) * LANES])
                ddt = ddt + _head_sums(dxd * xs[:, p * LANES:(p + 1) * LANES],
                                       _seg_matrix(LANES, SSM_HEAD_DIM, h0), ch)
            dcbb = _b(dcb)
            dx_ref[:, C_COL0 + g * SSM_STATE:C_COL0 + (g + 1) * SSM_STATE] = d_c + _dot(dcbb, bb)
            dx_ref[:, B_COL0 + g * SSM_STATE:B_COL0 + (g + 1) * SSM_STATE] = d_b + _dot_tn(dcbb, cc)
        rowi = lax.broadcasted_iota(jnp.int32, (ch, 1), 0)
        dacs = cols - rows_ref[...].T + jnp.where(rowi == ch - 1, dal, 0.0)
        dla = _dot_hi(jnp.where(ii <= jj, 1.0, 0.0), dacs)
        a_row = a_ref[...]
        ddt_raw = (ddt + dla * a_row) * sig_ref[...]
        ddt_ref[...] = _b(ddt_raw)
        st_ref[0:1, :] += jnp.sum(dla * dt_all, axis=0, keepdims=True) * a_row
        st_ref[1:2, :] += ddsk
        st_ref[2:3, :] += jnp.sum(ddt_raw, axis=0, keepdims=True)

    rc = lambda s: nch - 1 - s
    blk = pl.BlockSpec((ch, LANES), lambda s: (rc(s), 0))
    return pl.pallas_call(
        body, name="ssd_bwd", grid=(nch,),
        in_specs=[pl.BlockSpec((ch, CONV_DIM), lambda s: (rc(s), 0)), blk, blk,
                  pl.BlockSpec((1, SSM_HEADS, ch), lambda s: (rc(s), 0, 0)), blk,
                  pl.BlockSpec((1, LANES), lambda s: (0, 0)),
                  pl.BlockSpec((1, SSM_INNER), lambda s: (0, 0)),
                  pl.BlockSpec((1, SSM_GROUPS, SSM_STATE, GROUP_W), lambda s: (rc(s), 0, 0, 0)),
                  pl.BlockSpec((ch, SSM_INNER), lambda s: (rc(s), 0))],
        out_specs=[pl.BlockSpec((ch, CONV_DIM), lambda s: (rc(s), 0)), blk,
                   pl.BlockSpec((8, LANES), lambda s: (0, 0))],
        out_shape=[jax.ShapeDtypeStruct((S, CONV_DIM), F32), jax.ShapeDtypeStruct((S, LANES), BF16),
                   jax.ShapeDtypeStruct((8, LANES), F32)],
        scratch_shapes=[pltpu.VMEM((SSM_GROUPS, SSM_STATE, GROUP_W), F32), pltpu.VMEM((LANES, ch), F32)],
        compiler_params=_params(("arbitrary",)),
    )(xact, dt, acs, acst, sig, a_neg, dsk_e, hs, dy)


def _pad_lanes(v, n=LANES):
    return jnp.pad(v, ((0, 0), (0, n - v.shape[1])))


def _local_step(x, target, w):
    offs = np.cumsum((0,) + IN_SPLITS)
    w_in = w["w_in"]
    w_qkv = w_in[:, offs[0]:offs[3]]
    w_z = w_in[:, offs[3]:offs[4]]
    w_xbc = w_in[:, offs[4]:offs[5]]
    w_dt = _pad_lanes(w_in[:, offs[5]:offs[6]])
    w_g = w_in[:, offs[6]:offs[7]]
    dt_bias = _pad_lanes(w["dt_bias"])
    a_neg = _pad_lanes(-jnp.exp(w["a_log"]))
    d_skip = _pad_lanes(w["d_skip"])

    u = _rms_fwd(x, w["norm_mix_pre_w"])
    qkv = _mm_nn(u, w_qkv, BF16, "proj_qkv")
    z = _mm_nn(u, w_z, F32, "proj_z")
    xbc = _mm_nn(u, w_xbc, F32, "proj_xbc")
    dt_raw = _mm_nn(u, w_dt, F32, "proj_dt")
    gl = _mm_nn(u, w_g, F32, "proj_gate")

    q, k, v = (qkv[:, i * ATT_WIDTH:(i + 1) * ATT_WIDTH] for i in range(3))
    pats, os_, ms_, ls_ = [], [], [], []
    for d in DILATIONS:
        qp, kp, vp = _to_pat(q, d), _to_pat(k, d), _to_pat(v, d)
        pats.append((qp, kp, vp))
        o, m, l = _attn_fwd2(qp, kp, vp, d)
        os_.append(_from_pat(o, d))
        ms_.append(_from_pat(m, d))
        ls_.append(_from_pat(l, d))
    att, lse = _attn_combine(os_, ms_, ls_)
    att_o = _mm_nn(att, w["w_att_proj"], F32, "att_proj")

    xact = _conv_fwd(xbc, w["conv_w"], w["conv_b"])
    dsk_e = jnp.repeat(w["d_skip"], SSM_HEAD_DIM, axis=1)
    dt, acs, acst, sig = _ssd_prep(dt_raw, dt_bias, a_neg)
    y_ssd, hs = _ssd_fwd2(xact, dt, acs, acst, dsk_e)
    ssm_y = _gnorm_fwd(y_ssd, z, w["ssm_norm_w"])
    ssm_o = _mm_nn(ssm_y, w["w_ssm_proj"], F32, "ssm_proj")

    mi = _gate_fwd(att_o, ssm_o, gl, w["b_gate"])
    mixed = _mm_nn(mi, w["w_out"], F32, "out_proj")
    h1, f = _post_pre(x, mixed, w["norm_mix_post_w"], w["norm_ffn_pre_w"])
    up = _mm_nn(f, w["w_up"], F32, "ffn_up")
    act = _relu2(up)
    down = _mm_nn(act, w["w_down"], F32, "ffn_down")
    dh2, d_down, loss, g_ffn_post = _final(h1, down, w["norm_ffn_post_w"], target)

    g = {"norm_ffn_post_w": g_ffn_post}
    g["w_down"] = _mm_tn(act, d_down, "dw_down")
    da = _mm_nn(d_down, w["w_down"].T, F32, "d_act")
    dup = _dup(da, up)
    g["w_up"] = _mm_tn(f, dup, "dw_up")
    df = _mm_nn(dup, w["w_up"].T, F32, "d_f")
    dh1, d_mixed, g["norm_ffn_pre_w"], g["norm_mix_post_w"] = _mid_bwd(
        dh2, df, h1, mixed, w["norm_ffn_pre_w"], w["norm_mix_post_w"])
    g["w_out"] = _mm_tn(mi, d_mixed, "dw_out")
    dmi = _mm_nn(d_mixed, w["w_out"].T, F32, "d_mi")
    d_att_o, d_ssm_o, dgl, g["b_gate"] = _gate_bwd(dmi, att_o, ssm_o, gl, w["b_gate"])

    g["w_att_proj"] = _mm_tn(att, d_att_o, "dw_att_proj")
    d_att = _mm_nn(d_att_o, w["w_att_proj"].T, F32, "d_att")
    delta, d_att_b = _attn_delta(d_att, att)
    dqs, dks, dvs = [], [], []
    for d, (qp, kp, vp) in zip(DILATIONS, pats):
        dq, dk, dv = _attn_bwd2(qp, kp, vp, _to_pat(d_att_b, d), _to_pat(lse, d), _to_pat(delta, d), d)
        dqs.append(_from_pat(dq, d))
        dks.append(_from_pat(dk, d))
        dvs.append(_from_pat(dv, d))
    dqkv = _sum_qkv(dqs, dks, dvs)

    g["w_ssm_proj"] = _mm_tn(ssm_y, d_ssm_o, "dw_ssm_proj")
    d_ssm_y = _mm_nn(d_ssm_o, w["w_ssm_proj"].T, F32, "d_ssm_y")
    dy_ssd, dz, g["ssm_norm_w"] = _gnorm_bwd(d_ssm_y, y_ssd, z, w["ssm_norm_w"])
    dxact, ddt_raw, stats = _ssd_bwd2(xact, dt, acs, acst, sig, a_neg, dsk_e, hs, dy_ssd)
    g["a_log"] = stats[0:1, :SSM_HEADS]
    g["d_skip"] = stats[1:2, :SSM_HEADS]
    g["dt_bias"] = stats[2:3, :SSM_HEADS]
    dxbc, g["conv_w"], g["conv_b"] = _conv_bwd(xbc, dxact, w["conv_w"], w["conv_b"], 0)

    pieces = [(dqkv, w_qkv), (dz, w_z), (dxbc, w_xbc), (ddt_raw, w_dt), (dgl, w_g)]
    du = None
    gw = []
    for i, (dp, wp) in enumerate(pieces):
        gw.append(_mm_tn(u, dp, f"dw_in_{i}"))
        du = _mm_nn(dp, wp.T, F32, f"d_u_{i}", acc=du)
    gw[3] = gw[3][:, :SSM_HEADS]
    g["w_in"] = jnp.concatenate(gw, axis=1)
    grad_x, g["norm_mix_pre_w"] = _first_bwd(dh1, du, x, w["norm_mix_pre_w"])
    return loss, grad_x, g


BIG = ("w_in", "w_att_proj", "w_ssm_proj", "w_out", "w_up", "w_down")
BIG_FULL_SHAPES = {"w_in": (D_MODEL, IN_PROJ_WIDTH), "w_att_proj": (ATT_WIDTH, D_MODEL),
                   "w_ssm_proj": (SSM_INNER, D_MODEL), "w_out": (D_MODEL, D_MODEL),
                   "w_up": (D_MODEL, FFN_HIDDEN), "w_down": (FFN_HIDDEN, D_MODEL)}
BIG_COL_SHARDED = {"w_in": True, "w_att_proj": True, "w_ssm_proj": False, "w_out": False, "w_up": True,
                   "w_down": False}
PACK_COLS = 1024
PACK_ROWS = 5760
PACK_HALF = PACK_ROWS // 2
PACK_BLOCK = 576
SMALL = ("norm_mix_pre_w", "b_gate", "conv_b", "dt_bias", "a_log", "d_skip", "ssm_norm_w",
         "norm_mix_post_w", "norm_ffn_pre_w", "norm_ffn_post_w")
SMALL_ROWS = 232


def _shard_shape(name):
    r, c = BIG_FULL_SHAPES[name]
    return (r, c // N_CHIPS) if BIG_COL_SHARDED[name] else (r // N_CHIPS, c)


def _pack(shards, dtype):
    flat = [shards[n].astype(dtype).reshape(-1, PACK_COLS) for n in BIG]
    rows = sum(f.shape[0] for f in flat)
    flat.append(jnp.zeros((PACK_ROWS - rows, PACK_COLS), dtype))
    return jnp.concatenate(flat, axis=0)


def _unpack(packed):
    out, r0 = {}, 0
    for n in BIG:
        shp = _shard_shape(n)
        rows = shp[0] * shp[1] // PACK_COLS
        out[n] = packed[r0:r0 + rows].reshape(shp)
        r0 += rows
    return out


def _unpack_full(gathered):
    out, r0 = {}, 0
    for n in BIG:
        shp = _shard_shape(n)
        rows = shp[0] * shp[1] // PACK_COLS
        sh = gathered[:, r0:r0 + rows].reshape((N_CHIPS,) + shp)
        if BIG_COL_SHARDED[n]:
            out[n] = sh.transpose(1, 0, 2).reshape(BIG_FULL_SHAPES[n])
        else:
            out[n] = sh.reshape(BIG_FULL_SHAPES[n])
        r0 += rows
    return out


def _pack_full(grads):
    parts = []
    rows_total = 0
    for n in BIG:
        shp = _shard_shape(n)
        gfull = grads[n]
        if BIG_COL_SHARDED[n]:
            sh = gfull.reshape(shp[0], N_CHIPS, shp[1]).transpose(1, 0, 2)
        else:
            sh = gfull.reshape((N_CHIPS,) + shp)
        parts.append(sh.reshape(N_CHIPS, -1, PACK_COLS))
        rows_total += parts[-1].shape[1]
    parts.append(jnp.zeros((N_CHIPS, PACK_ROWS - rows_total, PACK_COLS), F32))
    return jnp.concatenate(parts, axis=1)


def _mesh_pos():
    return lax.axis_index("x"), lax.axis_index("y"), lax.axis_index("c")


def _other_chips(x, y):
    return [(1 - x, y), (x, 1 - y), (1 - x, 1 - y)]


ANY = pl.BlockSpec(memory_space=pl.ANY)


def _allgather_packed(wpack):
    half = PACK_HALF

    def body(w_ref, out_ref, send_sems, recv_sems, local_sem):
        x, y, c = _mesh_pos()
        me = 2 * x + y
        sibling = (x, y, 1 - c)
        chips = _other_chips(x, y)

        def rows(chip, h):
            return out_ref.at[chip, pl.ds(h * half, half), :]

        def copy(k, chip, h, to, src=None):
            return pltpu.make_async_remote_copy(
                src_ref=rows(chip, h) if src is None else src, dst_ref=rows(chip, h),
                send_sem=send_sems.at[k], recv_sem=recv_sems.at[k], device_id=to, device_id_type=MESH)

        mine = pltpu.make_async_copy(w_ref, out_ref.at[me], local_sem)
        mine.start()
        mine_half = w_ref.at[pl.ds(c * half, half), :]
        first = [copy(j, me, c, (*chip, c), src=mine_half) for j, chip in enumerate(chips)]
        for cp in first:
            cp.start()
        passed = [copy(3 + j, 2 * chip[0] + chip[1], c, sibling) for j, chip in enumerate(chips)]
        for j, chip in enumerate(chips):
            copy(j, 2 * chip[0] + chip[1], c, (x, y, c)).wait_recv()
            passed[j].start()
        for j, chip in enumerate(chips):
            copy(3 + j, 2 * chip[0] + chip[1], 1 - c, (x, y, c)).wait_recv()
        for cp in first + passed:
            cp.wait_send()
        mine.wait()

    return pl.pallas_call(
        body, name="allgather_weights",
        out_shape=jax.ShapeDtypeStruct((N_CHIPS,) + wpack.shape, wpack.dtype),
        in_specs=[ANY], out_specs=ANY,
        scratch_shapes=[pltpu.SemaphoreType.DMA((6,)), pltpu.SemaphoreType.DMA((6,)),
                        pltpu.SemaphoreType.DMA],
        compiler_params=pltpu.CompilerParams(has_side_effects=True),
    )(wpack)


def _exchange_halves(gpack):
    half = PACK_HALF

    def body(g_ref, out_ref, send_sem, recv_sem):
        x, y, c = _mesh_pos()
        cp = pltpu.make_async_remote_copy(
            src_ref=g_ref.at[:, pl.ds((1 - c) * half, half), :], dst_ref=out_ref,
            send_sem=send_sem, recv_sem=recv_sem, device_id=(x, y, 1 - c), device_id_type=MESH)
        cp.start()
        cp.wait()

    return pl.pallas_call(
        body, name="rs_pair_exchange",
        out_shape=jax.ShapeDtypeStruct((N_CHIPS, half, PACK_COLS), F32),
        in_specs=[ANY], out_specs=ANY,
        scratch_shapes=[pltpu.SemaphoreType.DMA, pltpu.SemaphoreType.DMA],
        compiler_params=pltpu.CompilerParams(has_side_effects=True),
    )(gpack)


def _pair_add(gpack, recv, c_idx):
    nb = PACK_HALF // PACK_BLOCK

    def body(c_ref, g_ref, r_ref, o_ref):
        o_ref[...] = g_ref[...] + r_ref[...]

    blk = (1, PACK_BLOCK, PACK_COLS)
    return pl.pallas_call(
        body, name="rs_pair_add",
        grid_spec=pltpu.PrefetchScalarGridSpec(
            num_scalar_prefetch=1, grid=(N_CHIPS, nb),
            in_specs=[pl.BlockSpec(blk, lambda s, i, c: (s, c[0] * nb + i, 0)),
                      pl.BlockSpec(blk, lambda s, i, c: (s, i, 0))],
            out_specs=pl.BlockSpec(blk, lambda s, i, c: (s, i, 0))),
        out_shape=jax.ShapeDtypeStruct((N_CHIPS, PACK_HALF, PACK_COLS), F32),
        compiler_params=_params(("arbitrary", "arbitrary")),
    )(c_idx, gpack, recv)


def _exchange_chips(ppack):
    def body(p_ref, out_ref, send_sems, recv_sems):
        x, y, c = _mesh_pos()
        chips = _other_chips(x, y)
        cps = [pltpu.make_async_remote_copy(
            src_ref=p_ref.at[2 * chip[0] + chip[1]], dst_ref=out_ref.at[j],
            send_sem=send_sems.at[j], recv_sem=recv_sems.at[j], device_id=(*chip, c), device_id_type=MESH)
            for j, chip in enumerate(chips)]
        for cp in cps:
            cp.start()
        for cp in cps:
            cp.wait_recv()
        for cp in cps:
            cp.wait_send()

    return pl.pallas_call(
        body, name="rs_chip_exchange",
        out_shape=jax.ShapeDtypeStruct((N_CHIPS - 1, PACK_HALF, PACK_COLS), F32),
        in_specs=[ANY], out_specs=ANY,
        scratch_shapes=[pltpu.SemaphoreType.DMA((3,)), pltpu.SemaphoreType.DMA((3,))],
        compiler_params=pltpu.CompilerParams(has_side_effects=True),
    )(ppack)


def _chip_add(ppack, recv, me_idx):
    nb = PACK_HALF // PACK_BLOCK

    def body(m_ref, p_ref, r0_ref, r1_ref, r2_ref, o_ref):
        o_ref[...] = ((p_ref[0] + r0_ref[0]) + r1_ref[0]) + r2_ref[0]

    blk = (1, PACK_BLOCK, PACK_COLS)
    return pl.pallas_call(
        body, name="rs_chip_add",
        grid_spec=pltpu.PrefetchScalarGridSpec(
            num_scalar_prefetch=1, grid=(nb,),
            in_specs=[pl.BlockSpec(blk, lambda i, m: (m[0], i, 0)),
                      pl.BlockSpec(blk, lambda i, m: (0, i, 0)),
                      pl.BlockSpec(blk, lambda i, m: (1, i, 0)),
                      pl.BlockSpec(blk, lambda i, m: (2, i, 0))],
            out_specs=pl.BlockSpec((PACK_BLOCK, PACK_COLS), lambda i, m: (i, 0))),
        out_shape=jax.ShapeDtypeStruct((PACK_HALF, PACK_COLS), F32),
        compiler_params=_params(("arbitrary",)),
    )(me_idx, ppack, recv, recv, recv)


def _share_halves(qhalf):
    half = PACK_HALF

    def body(q_ref, out_ref, send_sem, recv_sem, local_sem):
        x, y, c = _mesh_pos()
        mine = pltpu.make_async_copy(q_ref, out_ref.at[pl.ds(c * half, half), :], local_sem)
        mine.start()
        cp = pltpu.make_async_remote_copy(
            src_ref=q_ref, dst_ref=out_ref.at[pl.ds(c * half, half), :],
            send_sem=send_sem, recv_sem=recv_sem, device_id=(x, y, 1 - c), device_id_type=MESH)
        cp.start()
        pltpu.make_async_remote_copy(
            src_ref=q_ref, dst_ref=out_ref.at[pl.ds((1 - c) * half, half), :],
            send_sem=send_sem, recv_sem=recv_sem, device_id=(x, y, 1 - c), device_id_type=MESH).wait_recv()
        cp.wait_send()
        mine.wait()

    return pl.pallas_call(
        body, name="rs_share_halves",
        out_shape=jax.ShapeDtypeStruct((PACK_ROWS, PACK_COLS), F32),
        in_specs=[ANY], out_specs=ANY,
        scratch_shapes=[pltpu.SemaphoreType.DMA, pltpu.SemaphoreType.DMA, pltpu.SemaphoreType.DMA],
        compiler_params=pltpu.CompilerParams(has_side_effects=True),
    )(qhalf)


def _allreduce_small(part, name):
    rows = part.shape[0]

    def body(p_ref, out_ref, buf, send_sems, recv_sems, local_sem):
        x, y, c = _mesh_pos()
        me, sibling = (x, y, c), (x, y, 1 - c)
        chips = _other_chips(x, y)

        def slot(px, py, pc):
            return buf.at[pl.ds((4 * px + 2 * py + pc) * rows, rows), :]

        def copy(k, block, to, src=None):
            return pltpu.make_async_remote_copy(
                src_ref=slot(*block) if src is None else src, dst_ref=slot(*block),
                send_sem=send_sems.at[k], recv_sem=recv_sems.at[k], device_id=to, device_id_type=MESH)

        mine = pltpu.make_async_copy(p_ref, slot(*me), local_sem)
        mine.start()
        first = [copy(0, me, sibling, src=p_ref)]
        first += [copy(1 + j, me, (*chip, c), src=p_ref) for j, chip in enumerate(chips)]
        for cp in first:
            cp.start()
        passed = [copy(4 + j, (*chip, c), sibling) for j, chip in enumerate(chips)]
        for j, chip in enumerate(chips):
            copy(1 + j, (*chip, c), me).wait_recv()
            passed[j].start()
        copy(0, sibling, me).wait_recv()
        for j, chip in enumerate(chips):
            copy(4 + j, (*chip, 1 - c), me).wait_recv()
        for cp in first + passed:
            cp.wait_send()
        mine.wait()
        acc = buf[pl.ds(0, rows), :]
        for k in range(1, N_DEV):
            acc = acc + buf[pl.ds(k * rows, rows), :]
        out_ref[...] = acc

    return pl.pallas_call(
        body, name=name,
        out_shape=jax.ShapeDtypeStruct(part.shape, F32),
        in_specs=[pl.BlockSpec(memory_space=pltpu.VMEM)],
        out_specs=pl.BlockSpec(memory_space=pltpu.VMEM),
        scratch_shapes=[pltpu.VMEM((N_DEV * rows, LANES), F32), pltpu.SemaphoreType.DMA((7,)),
                        pltpu.SemaphoreType.DMA((7,)), pltpu.SemaphoreType.DMA],
        compiler_params=pltpu.CompilerParams(has_side_effects=True),
    )(part)


def _adamw(w, g, m, v, name):
    R, C = w.shape
    bs = _pick(R, (128, 64, 32, 8)) if R % 8 == 0 else R
    c1 = 1.0 / (1.0 - ADAM_B1 ** ADAM_STEP)
    c2 = 1.0 / (1.0 - ADAM_B2 ** ADAM_STEP)

    def body(w_ref, g_ref, m_ref, v_ref, d_ref, nm_ref, nv_ref):
        gg = g_ref[...]
        nm = ADAM_B1 * m_ref[...] + (1.0 - ADAM_B1) * gg
        nv = ADAM_B2 * v_ref[...] + (1.0 - ADAM_B2) * (gg * gg)
        nm_ref[...] = nm
        nv_ref[...] = nv
        d_ref[...] = -ADAM_LR * ((nm * c1) / (jnp.sqrt(nv * c2) + ADAM_EPS) + ADAM_WD * w_ref[...])

    spec = pl.BlockSpec((bs, C), lambda i: (i, 0))
    shp = jax.ShapeDtypeStruct((R, C), F32)
    return pl.pallas_call(
        body, name=name, grid=(R // bs,), in_specs=[spec] * 4, out_specs=[spec] * 3, out_shape=[shp] * 3,
        compiler_params=_params(("parallel",)),
    )(w, g, m, v)


WEIGHTS = ("norm_mix_pre_w", "w_in", "b_gate", "conv_w", "conv_b", "dt_bias", "a_log", "d_skip",
           "ssm_norm_w", "w_att_proj", "w_ssm_proj", "w_out", "norm_mix_post_w", "norm_ffn_pre_w", "w_up",
           "w_down", "norm_ffn_post_w")


def _flat_small(vals, conv_w_full):
    flat = [vals[n].reshape(-1) for n in SMALL] + [conv_w_full.reshape(-1)]
    v = jnp.concatenate(flat)
    return jnp.pad(v, (0, SMALL_ROWS * LANES - v.shape[0])).reshape(SMALL_ROWS, LANES)


def kernel(x, norm_mix_pre_w, w_in, b_gate, conv_w, conv_b, dt_bias, a_log, d_skip, ssm_norm_w, w_att_proj, w_ssm_proj, w_out, norm_mix_post_w, norm_ffn_pre_w, w_up, w_down, norm_ffn_post_w, loss_target, m_norm_mix_pre_w, m_w_in, m_b_gate, m_conv_w, m_conv_b, m_dt_bias, m_a_log, m_d_skip, m_ssm_norm_w, m_w_att_proj, m_w_ssm_proj, m_w_out, m_norm_mix_post_w, m_norm_ffn_pre_w, m_w_up, m_w_down, m_norm_ffn_post_w, v_norm_mix_pre_w, v_w_in, v_b_gate, v_conv_w, v_conv_b, v_dt_bias, v_a_log, v_d_skip, v_ssm_norm_w, v_w_att_proj, v_w_ssm_proj, v_w_out, v_norm_mix_post_w, v_norm_ffn_pre_w, v_w_up, v_w_down, v_norm_ffn_post_w):
    args = locals()

    def strip(a):
        return a[0] if a.ndim == 3 else a

    wts = {n: strip(args[n]) for n in WEIGHTS}
    mom = {n: strip(args["m_" + n]) for n in WEIGHTS}
    var = {n: strip(args["v_" + n]) for n in WEIGHTS}
    xi, yi, ci = _mesh_pos()
    chip = 2 * xi + yi

    gathered = _allgather_packed(_pack({n: wts[n] for n in BIG}, BF16))
    full = _unpack_full(gathered)
    cw_cols = CONV_DIM // N_CHIPS
    conv_slab = lax.dynamic_update_slice(jnp.zeros((SSM_CONV, CONV_DIM), F32),
                                         jnp.where(ci == 0, wts["conv_w"], 0.0), (0, chip * cw_cols))
    small_in = jnp.pad(conv_slab.reshape(-1), (0, SMALL_ROWS * LANES - SSM_CONV * CONV_DIM))
    conv_full = _allreduce_small(small_in.reshape(SMALL_ROWS, LANES), "gather_conv_w")
    full["conv_w"] = conv_full.reshape(-1)[:SSM_CONV * CONV_DIM].reshape(SSM_CONV, CONV_DIM)
    for n in SMALL:
        full[n] = wts[n]

    loss_part, grad_x, g = _local_step(x[0], loss_target[0], full)
    loss = lax.psum(loss_part[0, 0], ("x", "y", "c"))

    gpack = _pack_full(g)
    recv = _exchange_halves(gpack)
    ppack = _pair_add(gpack, recv, ci.reshape(1).astype(jnp.int32))
    recv3 = _exchange_chips(ppack)
    qhalf = _chip_add(ppack, recv3, chip.reshape(1).astype(jnp.int32))
    gshard = _unpack(_share_halves(qhalf))
    small_sum = _allreduce_small(_flat_small(g, g["conv_w"]), "allreduce_small_grads").reshape(-1)
    grads, off = {}, 0
    for n in SMALL:
        sz = wts[n].size
        grads[n] = small_sum[off:off + sz].reshape(wts[n].shape)
        off += sz
    conv_g = small_sum[off:off + SSM_CONV * CONV_DIM].reshape(SSM_CONV, CONV_DIM)
    grads["conv_w"] = lax.dynamic_slice(conv_g, (0, chip * cw_cols), (SSM_CONV, cw_cols))
    grads.update(gshard)

    delta, new_m, new_v = {}, {}, {}
    for n in BIG:
        delta[n], new_m[n], new_v[n] = _adamw(wts[n], grads[n], mom[n], var[n], f"adamw_{n}")
    small_names = SMALL + ("conv_w",)

    def pack_small(d):
        v = jnp.concatenate([d[n].reshape(-1) for n in small_names])
        rows = -(-v.shape[0] // (8 * LANES)) * 8
        return jnp.pad(v, (0, rows * LANES - v.shape[0])).reshape(rows, LANES)

    ds, ms, vs = _adamw(pack_small(wts), pack_small(grads), pack_small(mom), pack_small(var), "adamw_small")
    off = 0
    for n in small_names:
        sz = wts[n].size
        for dst, src in ((delta, ds), (new_m, ms), (new_v, vs)):
            dst[n] = src.reshape(-1)[off:off + sz].reshape(wts[n].shape)
        off += sz

    out = [loss, grad_x[None]]
    for d in (grads, delta, new_m, new_v):
        out += [d[n][None] if args[n].ndim == 3 else d[n] for n in WEIGHTS]
    return tuple(out)
```

```python
import functools
import math

import numpy as np
import jax
import jax.numpy as jnp
from jax import lax
from jax.experimental import pallas as pl
from jax.experimental.pallas import tpu as pltpu

F32 = jnp.float32
BF16 = jnp.bfloat16

D_MODEL = 1024
HEAD_DIM = 64
N_ATT_HEADS = 12
ATT_WIDTH = N_ATT_HEADS * HEAD_DIM
DILATIONS = (1, 4, 16)
ATT_BLOCK = 128
SSM_INNER = 2048
SSM_HEADS = 32
SSM_GROUPS = 8
HEADS_PER_GROUP = SSM_HEADS // SSM_GROUPS
SSM_HEAD_DIM = 64
SSM_STATE = 128
SSM_CONV = 4
SSM_CHUNK = 128
CONV_DIM = SSM_INNER + 2 * SSM_GROUPS * SSM_STATE
FFN_HIDDEN = 4 * D_MODEL
IN_SPLITS = (ATT_WIDTH, ATT_WIDTH, ATT_WIDTH, SSM_INNER, CONV_DIM, SSM_HEADS, 2 * D_MODEL)
IN_PROJ_WIDTH = sum(IN_SPLITS)
RMS_EPS = 1e-6
LANES = 128
NEG_BIG = -1e30

ADAM_LR = 0.001
ADAM_B1 = 0.9
ADAM_B2 = 0.999
ADAM_EPS = 1e-08
ADAM_WD = 0.01
ADAM_STEP = 10

N_CHIPS = 4
N_DEV = 8
VMEM_LIMIT = 56 * 1024 * 1024
MESH = pl.DeviceIdType.MESH


def _alibi_slopes(n):
    def pow2(m):
        start = 2.0 ** (-8.0 / m)
        return [start ** (i + 1) for i in range(m)]
    if (n & (n - 1)) == 0:
        s = pow2(n)
    else:
        c = 2 ** int(math.floor(math.log2(n)))
        s = pow2(c) + pow2(2 * c)[0::2][: n - c]
    return [float(v) for v in np.array(s, dtype=np.float32)]


def _params(sem):
    return pltpu.CompilerParams(dimension_semantics=sem, vmem_limit_bytes=VMEM_LIMIT)


def _dot(a, b):
    return lax.dot_general(a, b, (((1,), (0,)), ((), ())), preferred_element_type=F32)


def _dot_nt(a, b):
    return lax.dot_general(a, b, (((1,), (1,)), ((), ())), preferred_element_type=F32)


def _dot_tn(a, b):
    return lax.dot_general(a, b, (((0,), (0,)), ((), ())), preferred_element_type=F32)


def _dot_hi(a, b):
    return lax.dot_general(a, b, (((1,), (0,)), ((), ())), preferred_element_type=F32,
                           precision=lax.Precision.HIGHEST)


def _dot_tn_hi(a, b):
    return lax.dot_general(a, b, (((0,), (0,)), ((), ())), preferred_element_type=F32,
                           precision=lax.Precision.HIGHEST)


def _b(x):
    return x.astype(BF16)


def _sigmoid(x):
    return 1.0 / (1.0 + jnp.exp(-x))


def _pick(n, cands):
    for c in cands:
        if n % c == 0:
            return c
    raise ValueError(f"no tile for {n}")


def _mm_nn(a, b, out_dtype, name, acc=None, mode=None, extra=None):
    M, K = a.shape
    _, N = b.shape
    tm = 512
    tn = _pick(N, (1024, 768, 512, 256, 128))
    tk = K if K <= 2304 else _pick(K, (2048, 1024))
    nk = K // tk
    side = acc if acc is not None else extra
    n_out = 2 if mode == "relu2" else 1

    def body(*refs):
        a_ref, b_ref = refs[0], refs[1]
        s_ref = refs[2] if side is not None else None
        o_refs = refs[-n_out - (nk > 1):len(refs) - (nk > 1)]

        def finish(r):
            if mode == "relu2":
                r = jnp.maximum(r, 0.0)
                o_refs[0][...] = _b(r)
                o_refs[1][...] = _b(r * r)
            elif mode == "mul2":
                o_refs[0][...] = _b(r * (2.0 * s_ref[...].astype(F32)))
            else:
                if acc is not None:
                    r = r + s_ref[...]
                o_refs[0][...] = r.astype(out_dtype)

        part = _dot(_b(a_ref[...]), _b(b_ref[...]))
        if nk == 1:
            finish(part)
            return
        acc_ref = refs[-1]
        k = pl.program_id(2)

        @pl.when(k == 0)
        def _():
            acc_ref[...] = part

        @pl.when(jnp.logical_and(k > 0, k < nk - 1))
        def _():
            acc_ref[...] += part

        @pl.when(k == nk - 1)
        def _():
            finish(acc_ref[...] + part)

    tile = pl.BlockSpec((tm, tn), lambda j, i, k: (i, j))
    in_specs = [pl.BlockSpec((tm, tk), lambda j, i, k: (i, k)),
                pl.BlockSpec((tk, tn), lambda j, i, k: (k, j))]
    args = [a, b]
    if side is not None:
        in_specs.append(tile)
        args.append(side)
    odt = BF16 if mode in ("relu2", "mul2") else out_dtype
    outs = pl.pallas_call(
        body, name=name, grid=(N // tn, M // tm, nk),
        in_specs=in_specs,
        out_specs=[tile] * n_out,
        out_shape=[jax.ShapeDtypeStruct((M, N), odt)] * n_out,
        scratch_shapes=[pltpu.VMEM((tm, tn), F32)] if nk > 1 else [],
        compiler_params=_params(("parallel", "parallel", "arbitrary")),
    )(*args)
    return outs if n_out > 1 else outs[0]


def _mm_tn(a, b, name):
    S, Ka = a.shape
    _, N = b.shape
    tka = _pick(Ka, (1024, 768, 512))
    tn = _pick(N, (1024, 768, 512, 256, 128))
    ts = 1024 if S % 1024 == 0 else 512
    ns = S // ts

    def body(a_ref, b_ref, o_ref, acc_ref):
        s = pl.program_id(2)
        part = _dot_tn(_b(a_ref[...]), _b(b_ref[...]))

        @pl.when(s == 0)
        def _():
            acc_ref[...] = part

        @pl.when(s > 0)
        def _():
            acc_ref[...] += part

        @pl.when(s == ns - 1)
        def _():
            o_ref[...] = acc_ref[...]

    return pl.pallas_call(
        body, name=name, grid=(Ka // tka, N // tn, ns),
        in_specs=[pl.BlockSpec((ts, tka), lambda i, j, s: (s, i)),
                  pl.BlockSpec((ts, tn), lambda i, j, s: (s, j))],
        out_specs=pl.BlockSpec((tka, tn), lambda i, j, s: (i, j)),
        out_shape=jax.ShapeDtypeStruct((Ka, N), F32),
        scratch_shapes=[pltpu.VMEM((tka, tn), F32)],
        compiler_params=_params(("parallel", "parallel", "arbitrary")),
    )(a, b)


def _row_call(body, row_ins, full_ins, row_outs, acc_outs, bs, name):
    S = row_ins[0].shape[0]
    assert S % bs == 0
    in_specs = [pl.BlockSpec((bs, a.shape[1]), lambda i: (i, 0)) for a in row_ins]
    in_specs += [pl.BlockSpec(a.shape, lambda i: (0, 0)) for a in full_ins]
    out_specs = [pl.BlockSpec((bs, c), lambda i: (i, 0)) for c, _ in row_outs]
    out_specs += [pl.BlockSpec(s, lambda i: (0, 0)) for s in acc_outs]
    out_shape = [jax.ShapeDtypeStruct((S, c), dt) for c, dt in row_outs]
    out_shape += [jax.ShapeDtypeStruct(s, F32) for s in acc_outs]
    return pl.pallas_call(
        body, name=name, grid=(S // bs,), in_specs=in_specs, out_specs=out_specs, out_shape=out_shape,
        compiler_params=_params(("arbitrary",)),
    )(*row_ins, *full_ins)


def _rms_vals(x, w):
    r = lax.rsqrt(jnp.mean(x * x, axis=-1, keepdims=True) + RMS_EPS)
    return x * r * w


def _rms_bwd_vals(x, w, dy):
    r = lax.rsqrt(jnp.mean(x * x, axis=-1, keepdims=True) + RMS_EPS)
    xn = x * r
    g = dy * w
    dx = r * (g - xn * jnp.mean(g * xn, axis=-1, keepdims=True))
    dw = jnp.sum(dy * xn, axis=0, keepdims=True)
    return dx, dw


def _acc_add(ref, val):
    @pl.when(pl.program_id(0) == 0)
    def _():
        ref[...] = val

    @pl.when(pl.program_id(0) > 0)
    def _():
        ref[...] += val


def _rms_fwd(x, w):
    def body(x_ref, w_ref, o_ref):
        o_ref[...] = _b(_rms_vals(x_ref[...], w_ref[...]))
    return _row_call(body, [x], [w], [(x.shape[1], BF16)], [], 512, "rms_fwd")[0]


def _gate_fwd(att_o, ssm_o, gl, b_gate):
    def body(a_ref, s_ref, g_ref, b_ref, o_ref):
        g = _sigmoid(g_ref[...] + b_ref[...])
        o_ref[...] = _b(g[:, :D_MODEL] * a_ref[...] + g[:, D_MODEL:] * s_ref[...])
    return _row_call(body, [att_o, ssm_o, gl], [b_gate], [(D_MODEL, BF16)], [], 512, "gate_fwd")[0]


def _post_pre(x, mixed, w_post, w_pre):
    def body(x_ref, m_ref, wp_ref, wn_ref, h_ref, f_ref):
        h = x_ref[...] + _rms_vals(m_ref[...], wp_ref[...])
        h_ref[...] = h
        f_ref[...] = _b(_rms_vals(h, wn_ref[...]))
    return _row_call(body, [x, mixed], [w_post, w_pre], [(D_MODEL, F32), (D_MODEL, BF16)], [], 512,
                     "post_pre")


def _relu2(up):
    def body(u_ref, o_ref):
        r = jnp.maximum(u_ref[...], 0.0)
        o_ref[...] = _b(r * r)
    return _row_call(body, [up], [], [(up.shape[1], BF16)], [], 256, "relu2")[0]


def _final(h1, down, w_post, target):
    def body(h_ref, d_ref, t_ref, w_ref, dh_ref, dd_ref, loss_ref, dw_ref):
        dn = d_ref[...]
        w = w_ref[...]
        err = h_ref[...] + _rms_vals(dn, w) - t_ref[...]
        row = jnp.mean(err * err, axis=-1, keepdims=True)
        part = 0.5 * jnp.sum(row, axis=0, keepdims=True)
        dh = err * (1.0 / D_MODEL)
        dh_ref[...] = dh
        dx, dw = _rms_bwd_vals(dn, w, dh)
        dd_ref[...] = _b(dx)
        _acc_add(loss_ref, jnp.broadcast_to(part, (1, LANES)))
        _acc_add(dw_ref, dw)
    return _row_call(body, [h1, down, target], [w_post], [(D_MODEL, F32), (D_MODEL, BF16)],
                     [(1, LANES), (1, D_MODEL)], 512, "final_loss")


def _dup(da, up):
    def body(a_ref, u_ref, o_ref):
        o_ref[...] = _b(a_ref[...] * (2.0 * jnp.maximum(u_ref[...], 0.0)))
    return _row_call(body, [da, up], [], [(up.shape[1], BF16)], [], 256, "relu2_bwd")[0]


def _mid_bwd(dh2, df, h1, mixed, w_pre, w_post):
    def body(dh_ref, df_ref, h_ref, m_ref, wn_ref, wp_ref, dh1_ref, dm_ref, dwn_ref, dwp_ref):
        dx, dwn = _rms_bwd_vals(h_ref[...], wn_ref[...], df_ref[...])
        dh1 = dh_ref[...] + dx
        dh1_ref[...] = dh1
        dm, dwp = _rms_bwd_vals(m_ref[...], wp_ref[...], dh1)
        dm_ref[...] = _b(dm)
        _acc_add(dwn_ref, dwn)
        _acc_add(dwp_ref, dwp)
    return _row_call(body, [dh2, df, h1, mixed], [w_pre, w_post], [(D_MODEL, F32), (D_MODEL, BF16)],
                     [(1, D_MODEL), (1, D_MODEL)], 512, "mid_bwd")


def _gate_bwd(dmi, att_o, ssm_o, gl, b_gate):
    def body(d_ref, a_ref, s_ref, g_ref, b_ref, da_ref, ds_ref, dg_ref, db_ref):
        g = _sigmoid(g_ref[...] + b_ref[...])
        d = d_ref[...]
        ga, gs = g[:, :D_MODEL], g[:, D_MODEL:]
        da_ref[...] = _b(ga * d)
        ds_ref[...] = _b(gs * d)
        dga = d * a_ref[...] * ga * (1.0 - ga)
        dgs = d * s_ref[...] * gs * (1.0 - gs)
        dg_ref[:, :D_MODEL] = _b(dga)
        dg_ref[:, D_MODEL:] = _b(dgs)
        _acc_add(db_ref.at[:, pl.ds(0, D_MODEL)], jnp.sum(dga, axis=0, keepdims=True))
        _acc_add(db_ref.at[:, pl.ds(D_MODEL, D_MODEL)], jnp.sum(dgs, axis=0, keepdims=True))
    return _row_call(body, [dmi, att_o, ssm_o, gl], [b_gate],
                     [(D_MODEL, BF16), (D_MODEL, BF16), (2 * D_MODEL, BF16)], [(1, 2 * D_MODEL)], 256,
                     "gate_bwd")


def _first_bwd(dh1, du, x, w_pre):
    def body(dh_ref, du_ref, x_ref, w_ref, dx_ref, dw_ref):
        dx, dw = _rms_bwd_vals(x_ref[...], w_ref[...], du_ref[...])
        dx_ref[...] = dh_ref[...] + dx
        _acc_add(dw_ref, dw)
    return _row_call(body, [dh1, du, x], [w_pre], [(D_MODEL, F32)], [(1, D_MODEL)], 512, "first_bwd")


def _group_rms(t):
    gw = SSM_INNER // SSM_GROUPS
    out = []
    for g in range(SSM_GROUPS):
        tg = t[:, g * gw:(g + 1) * gw]
        out.append(lax.rsqrt(jnp.mean(tg * tg, axis=-1, keepdims=True) + RMS_EPS))
    return out


def _gnorm_fwd(y, z, w):
    gw = SSM_INNER // SSM_GROUPS

    def body(y_ref, z_ref, w_ref, o_ref):
        zz = z_ref[...]
        t = y_ref[...] * (zz * _sigmoid(zz))
        rs = _group_rms(t)
        for g in range(SSM_GROUPS):
            sl = slice(g * gw, (g + 1) * gw)
            o_ref[:, sl] = _b(t[:, sl] * rs[g] * w_ref[:, sl])
    return _row_call(body, [y, z], [w], [(SSM_INNER, BF16)], [], 256, "gnorm_fwd")[0]


def _gnorm_bwd(dout, y, z, w):
    gw = SSM_INNER // SSM_GROUPS

    def body(d_ref, y_ref, z_ref, w_ref, dy_ref, dz_ref, dw_ref):
        zz = z_ref[...]
        yy = y_ref[...]
        sg = _sigmoid(zz)
        sz = zz * sg
        t = yy * sz
        rs = _group_rms(t)
        for g in range(SSM_GROUPS):
            sl = slice(g * gw, (g + 1) * gw)
            tn = t[:, sl] * rs[g]
            d = d_ref[:, sl]
            gg = d * w_ref[:, sl]
            dt = rs[g] * (gg - tn * jnp.mean(gg * tn, axis=-1, keepdims=True))
            dy_ref[:, sl] = dt * sz[:, sl]
            dz_ref[:, sl] = _b(dt * yy[:, sl] * (sg[:, sl] * (1.0 + zz[:, sl] * (1.0 - sg[:, sl]))))
            _acc_add(dw_ref.at[:, pl.ds(g * gw, gw)], jnp.sum(d * tn, axis=0, keepdims=True))
    return _row_call(body, [dout, y, z], [w], [(SSM_INNER, F32), (SSM_INNER, BF16)], [(1, SSM_INNER)], 256,
                     "gnorm_bwd")


def _to_pat(a, d):
    if d == 1:
        return a
    S, C = a.shape
    return a.reshape(S // d, d, C).transpose(1, 0, 2).reshape(S, C)


def _from_pat(a, d):
    if d == 1:
        return a
    S, C = a.shape
    return a.reshape(d, S // d, C).transpose(1, 0, 2).reshape(S, C)


def _head_col(stat, h):
    return stat[:, h:h + 1]


def _attn_fwd(q, k, v, d):
    S = q.shape[0]
    blk = ATT_BLOCK
    nblk = S // blk
    nbs = nblk // d
    slopes = _alibi_slopes(N_ATT_HEADS)
    scale = HEAD_DIM ** -0.5

    def body(q_ref, kc_ref, kp_ref, vc_ref, vp_ref, o_ref, m_ref, l_ref):
        n = pl.program_id(0)
        has_prev = (n % nbs) != 0
        ii = lax.broadcasted_iota(jnp.int32, (blk, blk), 0)
        jj = lax.broadcasted_iota(jnp.int32, (blk, blk), 1)
        dist_c = (ii - jj).astype(F32)
        dist_p = dist_c + float(blk)
        ok_c = ii >= jj
        ok_p = jnp.logical_and(jj >= ii, has_prev)
        lane = lax.broadcasted_iota(jnp.int32, (blk, LANES), 1)
        m_all = jnp.zeros((blk, LANES), F32)
        l_all = jnp.zeros((blk, LANES), F32)
        for h in range(N_ATT_HEADS):
            sl = slice(h * HEAD_DIM, (h + 1) * HEAD_DIM)
            qh = q_ref[:, sl]
            bias = slopes[h] * float(d)
            sc = jnp.where(ok_c, _dot_nt(qh, kc_ref[:, sl]) * scale - bias * dist_c, NEG_BIG)
            sp = jnp.where(ok_p, _dot_nt(qh, kp_ref[:, sl]) * scale - bias * dist_p, NEG_BIG)
            m = jnp.maximum(jnp.max(sc, axis=-1, keepdims=True), jnp.max(sp, axis=-1, keepdims=True))
            pc = jnp.exp(sc - m)
            pp = jnp.exp(sp - m)
            l = jnp.sum(pc, axis=-1, keepdims=True) + jnp.sum(pp, axis=-1, keepdims=True)
            o_ref[:, sl] = _dot(_b(pc), vc_ref[:, sl]) + _dot(_b(pp), vp_ref[:, sl])
            m_all = jnp.where(lane == h, m, m_all)
            l_all = jnp.where(lane == h, l, l_all)
        m_ref[...] = m_all
        l_ref[...] = l_all

    cur = pl.BlockSpec((blk, ATT_WIDTH), lambda n: (n, 0))
    prev = pl.BlockSpec((blk, ATT_WIDTH), lambda n: (jnp.maximum(n - 1, 0), 0))
    stat = pl.BlockSpec((blk, LANES), lambda n: (n, 0))
    return pl.pallas_call(
        body, name=f"attn_fwd_d{d}", grid=(nblk,),
        in_specs=[cur, cur, prev, cur, prev],
        out_specs=[cur, stat, stat],
        out_shape=[jax.ShapeDtypeStruct((S, ATT_WIDTH), F32), jax.ShapeDtypeStruct((S, LANES), F32),
                   jax.ShapeDtypeStruct((S, LANES), F32)],
        compiler_params=_params(("parallel",)),
    )(q, k, k, v, v)


def _attn_combine(os, ms, ls):
    def body(o1, o2, o3, m1, m2, m3, l1, l2, l3, att_ref, lse_ref):
        mm = [m1[...], m2[...], m3[...]]
        big = jnp.maximum(jnp.maximum(mm[0], mm[1]), mm[2])
        es = [jnp.exp(m - big) for m in mm]
        den = es[0] * l1[...] + es[1] * l2[...] + es[2] * l3[...]
        lse_ref[...] = big + jnp.log(den)
        inv = 1.0 / den
        for h in range(N_ATT_HEADS):
            sl = slice(h * HEAD_DIM, (h + 1) * HEAD_DIM)
            num = (_head_col(es[0], h) * o1[:, sl] + _head_col(es[1], h) * o2[:, sl]
                   + _head_col(es[2], h) * o3[:, sl])
            att_ref[:, sl] = num * _head_col(inv, h)
    return _row_call(body, list(os) + list(ms) + list(ls), [], [(ATT_WIDTH, F32), (LANES, F32)], [], 256,
                     "attn_combine")


def _attn_delta(d_att, att):
    def body(d_ref, a_ref, dl_ref, db_ref):
        dd = d_ref[...]
        prod = dd * a_ref[...]
        lane = lax.broadcasted_iota(jnp.int32, (dd.shape[0], LANES), 1)
        acc = jnp.zeros((dd.shape[0], LANES), F32)
        for h in range(N_ATT_HEADS):
            s = jnp.sum(prod[:, h * HEAD_DIM:(h + 1) * HEAD_DIM], axis=-1, keepdims=True)
            acc = jnp.where(lane == h, s, acc)
        dl_ref[...] = acc
        db_ref[...] = _b(dd)
    return _row_call(body, [d_att, att], [], [(LANES, F32), (ATT_WIDTH, BF16)], [], 512, "attn_delta")


def _attn_bwd(q, k, v, do, lse, delta, d):
    S = q.shape[0]
    blk = ATT_BLOCK
    nblk = S // blk
    nbs = nblk // d
    slopes = _alibi_slopes(N_ATT_HEADS)
    scale = HEAD_DIM ** -0.5

    def body(qc_ref, qn_ref, k_ref, v_ref, doc_ref, don_ref, lc_ref, ln_ref, dc_ref, dn_ref,
             dq_ref, dk_ref, dv_ref, carry_ref):
        n = pl.program_id(0)
        has_next = ((n + 1) % nbs) != 0

        @pl.when(n == 0)
        def _():
            carry_ref[...] = jnp.zeros_like(carry_ref)

        ii = lax.broadcasted_iota(jnp.int32, (blk, blk), 0)
        jj = lax.broadcasted_iota(jnp.int32, (blk, blk), 1)
        dist_c = (ii - jj).astype(F32)
        dist_p = dist_c + float(blk)
        ok_c = ii >= jj
        ok_p = jnp.logical_and(jj >= ii, has_next)
        for h in range(N_ATT_HEADS):
            sl = slice(h * HEAD_DIM, (h + 1) * HEAD_DIM)
            bias = slopes[h] * float(d)
            kh = k_ref[:, sl]
            vh = v_ref[:, sl]
            qh = qc_ref[:, sl]
            doh = doc_ref[:, sl]
            s = jnp.where(ok_c, _dot_nt(qh, kh) * scale - bias * dist_c - _head_col(lc_ref[...], h), NEG_BIG)
            p = jnp.exp(s)
            ds = p * (_dot_nt(doh, vh) - _head_col(dc_ref[...], h)) * scale
            pb, dsb = _b(p), _b(ds)
            dv = _dot_tn(pb, doh)
            dk = _dot_tn(dsb, qh)
            dq_ref[:, sl] = _dot(dsb, kh) + carry_ref[:, sl]
            qh = qn_ref[:, sl]
            doh = don_ref[:, sl]
            s = jnp.where(ok_p, _dot_nt(qh, kh) * scale - bias * dist_p - _head_col(ln_ref[...], h), NEG_BIG)
            p = jnp.exp(s)
            ds = p * (_dot_nt(doh, vh) - _head_col(dn_ref[...], h)) * scale
            pb, dsb = _b(p), _b(ds)
            dv_ref[:, sl] = dv + _dot_tn(pb, doh)
            dk_ref[:, sl] = dk + _dot_tn(dsb, qh)
            carry_ref[:, sl] = _dot(dsb, kh)

    cur = pl.BlockSpec((blk, ATT_WIDTH), lambda n: (n, 0))
    nxt = pl.BlockSpec((blk, ATT_WIDTH), lambda n: (jnp.minimum(n + 1, nblk - 1), 0))
    scur = pl.BlockSpec((blk, LANES), lambda n: (n, 0))
    snxt = pl.BlockSpec((blk, LANES), lambda n: (jnp.minimum(n + 1, nblk - 1), 0))
    shp = jax.ShapeDtypeStruct((S, ATT_WIDTH), F32)
    return pl.pallas_call(
        body, name=f"attn_bwd_d{d}", grid=(nblk,),
        in_specs=[cur, nxt, cur, cur, cur, nxt, scur, snxt, scur, snxt],
        out_specs=[cur, cur, cur],
        out_shape=[shp, shp, shp],
        scratch_shapes=[pltpu.VMEM((blk, ATT_WIDTH), F32)],
        compiler_params=_params(("arbitrary",)),
    )(q, q, k, v, do, do, lse, lse, delta, delta)


def _head_pair_masks(x):
    lane = lax.broadcasted_iota(jnp.int32, x.shape, 1)
    zero = jnp.zeros_like(x)
    return jnp.where(lane < HEAD_DIM, x, zero), jnp.where(lane >= HEAD_DIM, x, zero)


def _attn_fwd2(q, k, v, d):
    S = q.shape[0]
    blk = ATT_BLOCK
    nblk = S // blk
    nbs = nblk // d
    slopes = _alibi_slopes(N_ATT_HEADS)
    scale = HEAD_DIM ** -0.5

    def body(q_ref, kc_ref, kp_ref, vc_ref, vp_ref, o_ref, m_ref, l_ref):
        n = pl.program_id(0)
        has_prev = (n % nbs) != 0
        ii = lax.broadcasted_iota(jnp.int32, (blk, 2 * blk), 0)
        jj = lax.broadcasted_iota(jnp.int32, (blk, 2 * blk), 1)
        dist_i = blk + ii - jj
        dist = dist_i.astype(F32)
        ok = jnp.logical_and(jnp.logical_and(dist_i >= 0, dist_i <= blk), jnp.logical_or(jj >= blk, has_prev))
        lane = lax.broadcasted_iota(jnp.int32, (blk, LANES), 1)
        m_all = jnp.zeros((blk, LANES), F32)
        l_all = jnp.zeros((blk, LANES), F32)
        for pr in range(N_ATT_HEADS // 2):
            sl = slice(pr * LANES, (pr + 1) * LANES)
            kcat = jnp.concatenate([kp_ref[:, sl], kc_ref[:, sl]], axis=0)
            vcat = jnp.concatenate([vp_ref[:, sl], vc_ref[:, sl]], axis=0)
            ps = []
            for h, qh in zip((2 * pr, 2 * pr + 1), _head_pair_masks(q_ref[:, sl])):
                s = jnp.where(ok, _dot_nt(qh, kcat) * scale - (slopes[h] * float(d)) * dist, NEG_BIG)
                m = jnp.max(s, axis=-1, keepdims=True)
                p = jnp.exp(s - m)
                l = jnp.sum(p, axis=-1, keepdims=True)
                m_all = jnp.where(lane == h, m, m_all)
                l_all = jnp.where(lane == h, l, l_all)
                ps.append(_b(p))
            o_ref[:, sl] = _dot(jnp.concatenate(ps, axis=1), jnp.concatenate(_head_pair_masks(vcat), axis=0))
        m_ref[...] = m_all
        l_ref[...] = l_all

    cur = pl.BlockSpec((blk, ATT_WIDTH), lambda n: (n, 0))
    prev = pl.BlockSpec((blk, ATT_WIDTH), lambda n: (jnp.maximum(n - 1, 0), 0))
    stat = pl.BlockSpec((blk, LANES), lambda n: (n, 0))
    return pl.pallas_call(
        body, name=f"attn_fwd_d{d}", grid=(nblk,),
        in_specs=[cur, cur, prev, cur, prev],
        out_specs=[cur, stat, stat],
        out_shape=[jax.ShapeDtypeStruct((S, ATT_WIDTH), F32), jax.ShapeDtypeStruct((S, LANES), F32),
                   jax.ShapeDtypeStruct((S, LANES), F32)],
        compiler_params=_params(("parallel",)),
    )(q, k, k, v, v)


def _attn_bwd2(q, k, v, do, lse, delta, d):
    S = q.shape[0]
    blk = ATT_BLOCK
    nblk = S // blk
    nbs = nblk // d
    slopes = _alibi_slopes(N_ATT_HEADS)
    scale = HEAD_DIM ** -0.5

    def body(qc_ref, qn_ref, k_ref, v_ref, doc_ref, don_ref, lc_ref, ln_ref, dc_ref, dn_ref,
             dq_ref, dk_ref, dv_ref, carry_ref):
        n = pl.program_id(0)
        has_next = ((n + 1) % nbs) != 0

        @pl.when(n == 0)
        def _():
            carry_ref[...] = jnp.zeros_like(carry_ref)

        rr = lax.broadcasted_iota(jnp.int32, (2 * blk, blk), 0)
        jj = lax.broadcasted_iota(jnp.int32, (2 * blk, blk), 1)
        dist_i = rr - jj
        dist = dist_i.astype(F32)
        ok = jnp.logical_or(jnp.logical_and(rr < blk, dist_i >= 0),
                            jnp.logical_and(jnp.logical_and(rr >= blk, dist_i <= blk), has_next))
        lcat = jnp.concatenate([lc_ref[...], ln_ref[...]], axis=0)
        dcat = jnp.concatenate([dc_ref[...], dn_ref[...]], axis=0)
        for pr in range(N_ATT_HEADS // 2):
            sl = slice(pr * LANES, (pr + 1) * LANES)
            qcat = jnp.concatenate([qc_ref[:, sl], qn_ref[:, sl]], axis=0)
            docat = jnp.concatenate([doc_ref[:, sl], don_ref[:, sl]], axis=0)
            k2 = k_ref[:, sl]
            v2 = v_ref[:, sl]
            qm = _head_pair_masks(qcat)
            dom = _head_pair_masks(docat)
            pbs, dsbs = [], []
            for h, qh, doh in zip((2 * pr, 2 * pr + 1), qm, dom):
                s = jnp.where(ok, _dot_nt(qh, k2) * scale - (slopes[h] * float(d)) * dist - lcat[:, h:h + 1],
                              NEG_BIG)
                p = jnp.exp(s)
                ds = p * (_dot_nt(doh, v2) - dcat[:, h:h + 1]) * scale
                pbs.append(_b(p))
                dsbs.append(_b(ds))
            dv_ref[:, sl] = _dot_tn(jnp.concatenate(pbs, axis=0), jnp.concatenate(dom, axis=0))
            dk_ref[:, sl] = _dot_tn(jnp.concatenate(dsbs, axis=0), jnp.concatenate(qm, axis=0))
            dq = _dot(jnp.concatenate(dsbs, axis=1), jnp.concatenate(_head_pair_masks(k2), axis=0))
            dq_ref[:, sl] = dq[:blk] + carry_ref[:, sl]
            carry_ref[:, sl] = dq[blk:]

    cur = pl.BlockSpec((blk, ATT_WIDTH), lambda n: (n, 0))
    nxt = pl.BlockSpec((blk, ATT_WIDTH), lambda n: (jnp.minimum(n + 1, nblk - 1), 0))
    scur = pl.BlockSpec((blk, LANES), lambda n: (n, 0))
    snxt = pl.BlockSpec((blk, LANES), lambda n: (jnp.minimum(n + 1, nblk - 1), 0))
    shp = jax.ShapeDtypeStruct((S, ATT_WIDTH), F32)
    return pl.pallas_call(
        body, name=f"attn_bwd_d{d}", grid=(nblk,),
        in_specs=[cur, nxt, cur, cur, cur, nxt, scur, snxt, scur, snxt],
        out_specs=[cur, cur, cur],
        out_shape=[shp, shp, shp],
        scratch_shapes=[pltpu.VMEM((blk, ATT_WIDTH), F32)],
        compiler_params=_params(("arbitrary",)),
    )(q, q, k, v, do, do, lse, lse, delta, delta)


def _sum_qkv(dqs, dks, dvs):
    def body(q1, q2, q3, k1, k2, k3, v1, v2, v3, o_ref):
        o_ref[:, 0:ATT_WIDTH] = _b(q1[...] + q2[...] + q3[...])
        o_ref[:, ATT_WIDTH:2 * ATT_WIDTH] = _b(k1[...] + k2[...] + k3[...])
        o_ref[:, 2 * ATT_WIDTH:] = _b(v1[...] + v2[...] + v3[...])
    return _row_call(body, list(dqs) + list(dks) + list(dvs), [], [(3 * ATT_WIDTH, BF16)], [], 256,
                     "sum_dqkv")[0]


CONV_COLS = 1024
CONV_ROWS = 512
HALO = 8


def _conv_fwd(xbc, conv_w, conv_b):
    S, C = xbc.shape
    bs, bc = CONV_ROWS, CONV_COLS
    nr = S // bs

    def body(x_ref, halo_ref, w_ref, b_ref, o_ref, xs_ref):
        r = pl.program_id(1)
        xs_ref[pl.ds(HALO, bs), :] = x_ref[...]
        xs_ref[pl.ds(0, HALO), :] = jnp.where(r > 0, halo_ref[...], 0.0)
        pre = b_ref[...] + w_ref[3:4, :] * x_ref[...]
        for j in range(SSM_CONV - 1):
            pre = pre + w_ref[j:j + 1, :] * xs_ref[pl.ds(HALO - 3 + j, bs), :]
        o_ref[...] = pre * _sigmoid(pre)

    return pl.pallas_call(
        body, name="conv_fwd", grid=(C // bc, nr),
        in_specs=[pl.BlockSpec((bs, bc), lambda c, r: (r, c)),
                  pl.BlockSpec((HALO, bc), lambda c, r: (jnp.maximum(r * (bs // HALO) - 1, 0), c)),
                  pl.BlockSpec((SSM_CONV, bc), lambda c, r: (0, c)),
                  pl.BlockSpec((1, bc), lambda c, r: (0, c))],
        out_specs=pl.BlockSpec((bs, bc), lambda c, r: (r, c)),
        out_shape=jax.ShapeDtypeStruct((S, C), F32),
        scratch_shapes=[pltpu.VMEM((bs + HALO, bc), F32)],
        compiler_params=_params(("parallel", "arbitrary")),
    )(xbc, xbc, conv_w, conv_b)


def _conv_bwd(xbc, dact, conv_w, conv_b, col0):
    S, C = xbc.shape
    Cp = dact.shape[1]
    bs, bc = CONV_ROWS, min(CONV_COLS, Cp)
    nr = S // bs
    cb0 = col0 // bc
    last_halo = S // HALO - 1

    def body(x_ref, xp_ref, xn_ref, d_ref, dn_ref, w_ref, b_ref, dx_ref, dw_ref, db_ref,
             xs_ref, dp_ref):
        r = pl.program_id(1)
        xs_ref[pl.ds(0, HALO), :] = jnp.where(r > 0, xp_ref[...], 0.0)
        xs_ref[pl.ds(HALO, bs), :] = x_ref[...]
        xs_ref[pl.ds(HALO + bs, HALO), :] = xn_ref[...]
        ext = bs + HALO
        pre = b_ref[...] + jnp.zeros((ext, bc), F32)
        for j in range(SSM_CONV):
            pre = pre + w_ref[j:j + 1, :] * xs_ref[pl.ds(HALO - 3 + j, ext), :]
        sg = _sigmoid(pre)
        dsilu = sg * (1.0 + pre * (1.0 - sg))
        dp_ref[pl.ds(0, bs), :] = d_ref[...] * dsilu[:bs]
        dp_ref[pl.ds(bs, HALO), :] = jnp.where(r < nr - 1, dn_ref[...], 0.0) * dsilu[bs:]
        dx = jnp.zeros((bs, bc), F32)
        for j in range(SSM_CONV):
            dx = dx + w_ref[j:j + 1, :] * dp_ref[pl.ds(3 - j, bs), :]
        dx_ref[...] = _b(dx)
        dpre = dp_ref[pl.ds(0, bs), :]
        for j in range(SSM_CONV):
            part = jnp.sum(dpre * xs_ref[pl.ds(HALO - 3 + j, bs), :], axis=0, keepdims=True)

            @pl.when(r == 0)
            def _():
                dw_ref[j:j + 1, :] = part

            @pl.when(r > 0)
            def _():
                dw_ref[j:j + 1, :] += part
        part = jnp.sum(dpre, axis=0, keepdims=True)

        @pl.when(r == 0)
        def _():
            db_ref[...] = part

        @pl.when(r > 0)
        def _():
            db_ref[...] += part

    hb = bs // HALO
    return pl.pallas_call(
        body, name=f"conv_bwd_{col0}", grid=(Cp // bc, nr),
        in_specs=[pl.BlockSpec((bs, bc), lambda c, r: (r, cb0 + c)),
                  pl.BlockSpec((HALO, bc), lambda c, r: (jnp.maximum(r * hb - 1, 0), cb0 + c)),
                  pl.BlockSpec((HALO, bc), lambda c, r: (jnp.minimum((r + 1) * hb, last_halo), cb0 + c)),
                  pl.BlockSpec((bs, bc), lambda c, r: (r, c)),
                  pl.BlockSpec((HALO, bc), lambda c, r: (jnp.minimum((r + 1) * hb, last_halo), c)),
                  pl.BlockSpec((SSM_CONV, bc), lambda c, r: (0, cb0 + c)),
                  pl.BlockSpec((1, bc), lambda c, r: (0, cb0 + c))],
        out_specs=[pl.BlockSpec((bs, bc), lambda c, r: (r, c)),
                   pl.BlockSpec((SSM_CONV, bc), lambda c, r: (0, c)),
                   pl.BlockSpec((1, bc), lambda c, r: (0, c))],
        out_shape=[jax.ShapeDtypeStruct((S, Cp), BF16), jax.ShapeDtypeStruct((SSM_CONV, Cp), F32),
                   jax.ShapeDtypeStruct((1, Cp), F32)],
        scratch_shapes=[pltpu.VMEM((bs + 2 * HALO, bc), F32), pltpu.VMEM((bs + HALO, bc), F32)],
        compiler_params=_params(("parallel", "arbitrary")),
    )(xbc, xbc, xbc, dact, dact, conv_w, conv_b)


def _shift_down(x, k, top_src):
    r8 = lax.broadcasted_iota(jnp.int32, (HALO, x.shape[1]), 0)
    rolled = pltpu.roll(x, k, 0)
    top = jnp.where(r8 < k, pltpu.roll(top_src, k, 0), rolled[0:HALO])
    if x.shape[0] == HALO:
        return top
    return jnp.concatenate([top, rolled[HALO:]], axis=0)


def _shift_up(x, k, bottom_src):
    n = x.shape[0]
    r8 = lax.broadcasted_iota(jnp.int32, (HALO, x.shape[1]), 0)
    rolled = pltpu.roll(x, n - k, 0)
    bottom = jnp.where(r8 >= HALO - k, pltpu.roll(bottom_src, HALO - k, 0), rolled[n - HALO:n])
    return jnp.concatenate([rolled[:n - HALO], bottom], axis=0)


def _conv_pre(x, top_src, w_ref, b_ref):
    shifted = [x] + [_shift_down(x, k, top_src) for k in range(1, SSM_CONV)]
    pre = b_ref[...] + w_ref[SSM_CONV - 1:SSM_CONV, :] * x
    for k in range(1, SSM_CONV):
        pre = pre + w_ref[SSM_CONV - 1 - k:SSM_CONV - k, :] * shifted[k]
    return pre, shifted


def _conv_fwd2(xbc, conv_w, conv_b):
    S, C = xbc.shape
    bs, bc = CONV_ROWS, CONV_COLS
    nr = S // bs

    def body(x_ref, halo_ref, w_ref, b_ref, o_ref):
        r = pl.program_id(1)
        halo = jnp.where(r > 0, halo_ref[...], 0.0)
        pre, _ = _conv_pre(x_ref[...], halo, w_ref, b_ref)
        o_ref[...] = pre * _sigmoid(pre)

    return pl.pallas_call(
        body, name="conv_fwd", grid=(C // bc, nr),
        in_specs=[pl.BlockSpec((bs, bc), lambda c, r: (r, c)),
                  pl.BlockSpec((HALO, bc), lambda c, r: (jnp.maximum(r * (bs // HALO) - 1, 0), c)),
                  pl.BlockSpec((SSM_CONV, bc), lambda c, r: (0, c)),
                  pl.BlockSpec((1, bc), lambda c, r: (0, c))],
        out_specs=pl.BlockSpec((bs, bc), lambda c, r: (r, c)),
        out_shape=jax.ShapeDtypeStruct((S, C), F32),
        compiler_params=_params(("parallel", "arbitrary")),
    )(xbc, xbc, conv_w, conv_b)


def _conv_bwd2(xbc, dact, conv_w, conv_b):
    S, C = xbc.shape
    bs, bc = CONV_ROWS, CONV_COLS
    nr = S // bs
    hb = bs // HALO
    last_halo = S // HALO - 1

    def dsilu(pre):
        sg = _sigmoid(pre)
        return sg * (1.0 + pre * (1.0 - sg))

    def body(x_ref, xp_ref, xn_ref, d_ref, dn_ref, w_ref, b_ref, dx_ref, dw_ref, db_ref):
        r = pl.program_id(1)
        x = x_ref[...]
        pre, shifted = _conv_pre(x, jnp.where(r > 0, xp_ref[...], 0.0), w_ref, b_ref)
        dpre = d_ref[...] * dsilu(pre)
        pre_n, _ = _conv_pre(xn_ref[...], x[bs - HALO:bs], w_ref, b_ref)
        dpre_n = jnp.where(r < nr - 1, dn_ref[...], 0.0) * dsilu(pre_n)
        dx = w_ref[SSM_CONV - 1:SSM_CONV, :] * dpre
        for k in range(1, SSM_CONV):
            dx = dx + w_ref[SSM_CONV - 1 - k:SSM_CONV - k, :] * _shift_up(dpre, k, dpre_n)
        dx_ref[...] = _b(dx)
        parts = [jnp.sum(dpre * shifted[SSM_CONV - 1 - j], axis=0, keepdims=True) for j in range(SSM_CONV)]
        dbp = jnp.sum(dpre, axis=0, keepdims=True)

        @pl.when(r == 0)
        def _():
            for j in range(SSM_CONV):
                dw_ref[j:j + 1, :] = parts[j]
            db_ref[...] = dbp

        @pl.when(r > 0)
        def _():
            for j in range(SSM_CONV):
                dw_ref[j:j + 1, :] += parts[j]
            db_ref[...] += dbp

    return pl.pallas_call(
        body, name="conv_bwd", grid=(C // bc, nr),
        in_specs=[pl.BlockSpec((bs, bc), lambda c, r: (r, c)),
                  pl.BlockSpec((HALO, bc), lambda c, r: (jnp.maximum(r * hb - 1, 0), c)),
                  pl.BlockSpec((HALO, bc), lambda c, r: (jnp.minimum((r + 1) * hb, last_halo), c)),
                  pl.BlockSpec((bs, bc), lambda c, r: (r, c)),
                  pl.BlockSpec((HALO, bc), lambda c, r: (jnp.minimum((r + 1) * hb, last_halo), c)),
                  pl.BlockSpec((SSM_CONV, bc), lambda c, r: (0, c)),
                  pl.BlockSpec((1, bc), lambda c, r: (0, c))],
        out_specs=[pl.BlockSpec((bs, bc), lambda c, r: (r, c)),
                   pl.BlockSpec((SSM_CONV, bc), lambda c, r: (0, c)),
                   pl.BlockSpec((1, bc), lambda c, r: (0, c))],
        out_shape=[jax.ShapeDtypeStruct((S, C), BF16), jax.ShapeDtypeStruct((SSM_CONV, C), F32),
                   jax.ShapeDtypeStruct((1, C), F32)],
        compiler_params=_params(("parallel", "arbitrary")),
    )(xbc, xbc, xbc, dact, dact, conv_w, conv_b)


def _softplus(x):
    return jnp.maximum(x, 0.0) + jnp.log(1.0 + jnp.exp(-jnp.abs(x)))


def _ssd_common(dtr_ref, bias_ref, a_ref, g):
    ch = SSM_CHUNK
    x = dtr_ref[...] + bias_ref[...]
    dt_all = _softplus(x)
    r = lax.broadcasted_iota(jnp.int32, (LANES, LANES), 0)
    c = lax.broadcasted_iota(jnp.int32, (LANES, LANES), 1)
    sel = jnp.where(jnp.logical_and(r == HEADS_PER_GROUP * g + c, c < HEADS_PER_GROUP), 1.0, 0.0)
    dt4 = _dot_hi(dt_all, sel)
    la4 = _dot_hi(dt_all * a_ref[...], sel)
    ii = lax.broadcasted_iota(jnp.int32, (ch, ch), 0)
    jj = lax.broadcasted_iota(jnp.int32, (ch, ch), 1)
    tril = jnp.where(ii >= jj, 1.0, 0.0)
    acs = _dot_hi(tril, la4)
    return x, sel, dt4, acs, acs.T, ii >= jj


def _row8(v):
    return jnp.broadcast_to(v, (8, v.shape[1]))


def _ssd_fwd(xact, dt_raw, dt_bias, a_neg, d_skip):
    S = xact.shape[0]
    ch = SSM_CHUNK
    nch = S // ch
    hg = HEADS_PER_GROUP
    gw = hg * SSM_HEAD_DIM
    b_off = SSM_INNER // SSM_STATE
    c_off = b_off + SSM_GROUPS

    def body(x_ref, b_ref, c_ref, dtr_ref, bias_ref, a_ref, dsk_ref, y_ref, hs_ref, h_ref):
        c = pl.program_id(0)
        g = pl.program_id(1)

        @pl.when(jnp.logical_and(c == 0, g == 0))
        def _():
            h_ref[...] = jnp.zeros_like(h_ref)

        _, sel, dt4, acs, acs_t, low = _ssd_common(dtr_ref, bias_ref, a_ref, g)
        dsk4 = _dot_hi(_row8(dsk_ref[...]), sel)
        bb = _b(b_ref[...])
        cc = _b(c_ref[...])
        cb = _dot_nt(cc, bb)
        for j in range(hg):
            sl = slice(j * SSM_HEAD_DIM, (j + 1) * SSM_HEAD_DIM)
            acol = acs[:, j:j + 1]
            arow = acs_t[j:j + 1, :]
            alast = acs[ch - 1:ch, j:j + 1]
            decay = jnp.exp(jnp.where(low, acol - arow, -jnp.inf))
            xh = x_ref[:, sl]
            xd = xh * dt4[:, j:j + 1]
            hj = h_ref[hg * g + j]
            y = _dot(_b(cb * decay), _b(xd))
            y = y + _dot_nt(cc, _b(hj)) * jnp.exp(acol)
            y_ref[:, sl] = y + dsk4[0:1, j:j + 1] * xh
            hs_ref[0, j] = hj
            st = _dot_tn(_b(xd * jnp.exp(alast - acol)), bb)
            h_ref[hg * g + j] = hj * jnp.exp(alast) + st

    small = pl.BlockSpec((1, LANES), lambda c, g: (0, 0))
    return pl.pallas_call(
        body, name="ssd_fwd", grid=(nch, SSM_GROUPS),
        in_specs=[pl.BlockSpec((ch, gw), lambda c, g: (c, g)),
                  pl.BlockSpec((ch, SSM_STATE), lambda c, g: (c, b_off + g)),
                  pl.BlockSpec((ch, SSM_STATE), lambda c, g: (c, c_off + g)),
                  pl.BlockSpec((ch, LANES), lambda c, g: (c, 0)),
                  small, small, small],
        out_specs=[pl.BlockSpec((ch, gw), lambda c, g: (c, g)),
                   pl.BlockSpec((1, hg, SSM_HEAD_DIM, SSM_STATE), lambda c, g: (c, g, 0, 0))],
        out_shape=[jax.ShapeDtypeStruct((S, SSM_INNER), F32),
                   jax.ShapeDtypeStruct((nch, SSM_HEADS, SSM_HEAD_DIM, SSM_STATE), F32)],
        scratch_shapes=[pltpu.VMEM((SSM_HEADS, SSM_HEAD_DIM, SSM_STATE), F32)],
        compiler_params=_params(("arbitrary", "arbitrary")),
    )(xact, xact, xact, dt_raw, dt_bias, a_neg, d_skip)


def _ssd_bwd(xact, dt_raw, dt_bias, a_neg, d_skip, hs, dy):
    S = xact.shape[0]
    ch = SSM_CHUNK
    nch = S // ch
    hg = HEADS_PER_GROUP
    gw = hg * SSM_HEAD_DIM
    b_off = SSM_INNER // SSM_STATE
    c_off = b_off + SSM_GROUPS

    def body(x_ref, b_ref, c_ref, dtr_ref, bias_ref, a_ref, dsk_ref, hs_ref, dy_ref,
             dx_ref, db_ref, dc_ref, ddt_ref, st_ref, dh_ref, ddt_acc):
        step = pl.program_id(0)
        g = pl.program_id(1)

        @pl.when(jnp.logical_and(step == 0, g == 0))
        def _():
            dh_ref[...] = jnp.zeros_like(dh_ref)
            st_ref[...] = jnp.zeros_like(st_ref)

        @pl.when(g == 0)
        def _():
            ddt_acc[...] = jnp.zeros_like(ddt_acc)

        xraw, sel, dt4, acs, acs_t, low = _ssd_common(dtr_ref, bias_ref, a_ref, g)
        a4 = _dot_hi(_row8(a_ref[...]), sel)[0:1, :]
        dsk4 = _dot_hi(_row8(dsk_ref[...]), sel)
        bf = b_ref[...]
        cf = c_ref[...]
        bb = _b(bf)
        cc = _b(cf)
        cb = _dot_nt(cc, bb)
        lane = lax.broadcasted_iota(jnp.int32, (ch, LANES), 1)
        rowi = lax.broadcasted_iota(jnp.int32, (ch, 1), 0)
        ones = jnp.ones((ch, LANES), F32)
        dcb = jnp.zeros((ch, ch), F32)
        dc_acc = jnp.zeros((ch, SSM_STATE), F32)
        db_acc = jnp.zeros((ch, SSM_STATE), F32)
        dacs4 = jnp.zeros((ch, LANES), F32)
        ddt4 = jnp.zeros((ch, LANES), F32)
        dd4 = jnp.zeros((1, LANES), F32)
        lane1 = lax.broadcasted_iota(jnp.int32, (1, LANES), 1)
        for j in range(hg):
            sl = slice(j * SSM_HEAD_DIM, (j + 1) * SSM_HEAD_DIM)
            acol = acs[:, j:j + 1]
            arow = acs_t[j:j + 1, :]
            alast = acs[ch - 1:ch, j:j + 1]
            decay = jnp.exp(jnp.where(low, acol - arow, -jnp.inf))
            ea = jnp.exp(acol)
            dsd = jnp.exp(alast - acol)
            cd = jnp.exp(alast)
            dtc = dt4[:, j:j + 1]
            xh = x_ref[:, sl]
            xd = xh * dtc
            xdb = _b(xd)
            hj = hs_ref[0, j]
            hjb = _b(hj)
            dhn = dh_ref[hg * g + j]
            dyj = dy_ref[:, sl]
            dyb = _b(dyj)
            lm = cb * decay
            dxh = dsk4[0:1, j:j + 1] * dyj
            dd4 = jnp.where(lane1 == j, jnp.sum(jnp.sum(dyj * xh, axis=1, keepdims=True), axis=0,
                                                keepdims=True), dd4)
            dlm = _dot_nt(dyb, xdb)
            dxd = _dot_tn(_b(lm), dyb)
            gm = dlm * lm
            dcb = dcb + dlm * decay
            dac = jnp.sum(gm, axis=1, keepdims=True) - _dot_tn_hi(gm, ones)[:, 0:1]
            zz = _dot_nt(cc, hjb)
            dzb = _b(dyj * ea)
            dac = dac + jnp.sum(dyj * zz, axis=1, keepdims=True) * ea
            dc_acc = dc_acc + _dot(dzb, hjb)
            dh_in = _dot_tn(dzb, cc)
            dsb = _b(dhn)
            ww = _dot_nt(bb, dsb)
            dxd = dxd + ww * dsd
            dds = jnp.sum(ww * xd, axis=1, keepdims=True) * dsd
            db_acc = db_acc + _dot(_b(xd * dsd), dsb)
            dac = dac - dds
            dal = (jnp.sum(dds, axis=0, keepdims=True)
                   + jnp.sum(jnp.sum(dhn * hj, axis=1, keepdims=True), axis=0, keepdims=True) * cd)
            dh_ref[hg * g + j] = dh_in + dhn * cd
            dac = dac + jnp.where(rowi == ch - 1, dal, 0.0)
            dacs4 = jnp.where(lane == j, dac, dacs4)
            dx_ref[:, sl] = dxh + dxd * dtc
            ddt4 = jnp.where(lane == j, jnp.sum(dxd * xh, axis=1, keepdims=True), ddt4)
        dcbb = _b(dcb)
        dc_ref[...] = dc_acc + _dot(dcbb, bb)
        db_ref[...] = db_acc + _dot_tn(dcbb, cc)
        ii = lax.broadcasted_iota(jnp.int32, (ch, ch), 0)
        jj = lax.broadcasted_iota(jnp.int32, (ch, ch), 1)
        triu = jnp.where(ii <= jj, 1.0, 0.0)
        dla4 = _dot_hi(triu, dacs4)
        ddt4 = ddt4 + dla4 * a4
        da4 = jnp.sum(dla4 * dt4, axis=0, keepdims=True) * a4
        sel_t = sel.T
        ddt_raw = _dot_hi(ddt4, sel_t) * _sigmoid(xraw)
        ddt_acc[...] += ddt_raw
        st_ref[0:1, :] += _dot_hi(_row8(da4), sel_t)[0:1, :]
        st_ref[1:2, :] += _dot_hi(_row8(dd4), sel_t)[0:1, :]
        st_ref[2:3, :] += jnp.sum(ddt_raw, axis=0, keepdims=True)

        @pl.when(g == SSM_GROUPS - 1)
        def _():
            ddt_ref[...] = _b(ddt_acc[...])

    small = pl.BlockSpec((1, LANES), lambda s, g: (0, 0))
    rc = lambda s: nch - 1 - s
    return pl.pallas_call(
        body, name="ssd_bwd", grid=(nch, SSM_GROUPS),
        in_specs=[pl.BlockSpec((ch, gw), lambda s, g: (rc(s), g)),
                  pl.BlockSpec((ch, SSM_STATE), lambda s, g: (rc(s), b_off + g)),
                  pl.BlockSpec((ch, SSM_STATE), lambda s, g: (rc(s), c_off + g)),
                  pl.BlockSpec((ch, LANES), lambda s, g: (rc(s), 0)),
                  small, small, small,
                  pl.BlockSpec((1, hg, SSM_HEAD_DIM, SSM_STATE), lambda s, g: (rc(s), g, 0, 0)),
                  pl.BlockSpec((ch, gw), lambda s, g: (rc(s), g))],
        out_specs=[pl.BlockSpec((ch, gw), lambda s, g: (rc(s), g)),
                   pl.BlockSpec((ch, SSM_STATE), lambda s, g: (rc(s), g)),
                   pl.BlockSpec((ch, SSM_STATE), lambda s, g: (rc(s), g)),
                   pl.BlockSpec((ch, LANES), lambda s, g: (rc(s), 0)),
                   pl.BlockSpec((8, LANES), lambda s, g: (0, 0))],
        out_shape=[jax.ShapeDtypeStruct((S, SSM_INNER), F32),
                   jax.ShapeDtypeStruct((S, SSM_GROUPS * SSM_STATE), F32),
                   jax.ShapeDtypeStruct((S, SSM_GROUPS * SSM_STATE), F32),
                   jax.ShapeDtypeStruct((S, LANES), BF16),
                   jax.ShapeDtypeStruct((8, LANES), F32)],
        scratch_shapes=[pltpu.VMEM((SSM_HEADS, SSM_HEAD_DIM, SSM_STATE), F32),
                        pltpu.VMEM((ch, LANES), F32)],
        compiler_params=_params(("arbitrary", "arbitrary")),
    )(xact, xact, xact, dt_raw, dt_bias, a_neg, d_skip, hs, dy)


GROUP_W = HEADS_PER_GROUP * SSM_HEAD_DIM
B_COL0 = SSM_INNER
C_COL0 = SSM_INNER + SSM_GROUPS * SSM_STATE


def _ssd_prep(dt_raw, dt_bias, a_neg):
    S = dt_raw.shape[0]
    ch = SSM_CHUNK
    nch = S // ch

    def body(dtr_ref, bias_ref, a_ref, dt_ref, acs_ref, acst_ref, sig_ref):
        x = dtr_ref[...] + bias_ref[...]
        lane = lax.broadcasted_iota(jnp.int32, (ch, LANES), 1)
        dt = jnp.where(lane < SSM_HEADS, _softplus(x), 0.0)
        ii = lax.broadcasted_iota(jnp.int32, (ch, ch), 0)
        jj = lax.broadcasted_iota(jnp.int32, (ch, ch), 1)
        acs = _dot_hi(jnp.where(ii >= jj, 1.0, 0.0), dt * a_ref[...])
        dt_ref[...] = dt
        acs_ref[...] = acs
        acst_ref[0] = acs.T[0:SSM_HEADS, :]
        sig_ref[...] = _sigmoid(x)

    blk = pl.BlockSpec((ch, LANES), lambda c: (c, 0))
    small = pl.BlockSpec((1, LANES), lambda c: (0, 0))
    shp = jax.ShapeDtypeStruct((S, LANES), F32)
    return pl.pallas_call(
        body, name="ssd_prep", grid=(nch,),
        in_specs=[blk, small, small],
        out_specs=[blk, blk, pl.BlockSpec((1, SSM_HEADS, ch), lambda c: (c, 0, 0)), blk],
        out_shape=[shp, shp, jax.ShapeDtypeStruct((nch, SSM_HEADS, ch), F32), shp],
        compiler_params=_params(("parallel",)),
    )(dt_raw, dt_bias, a_neg)


def _expand_heads(arr, g, rows):
    lane = lax.broadcasted_iota(jnp.int32, (rows, GROUP_W), 1) // SSM_HEAD_DIM
    h0 = HEADS_PER_GROUP * g
    out = jnp.broadcast_to(arr[:, h0:h0 + 1], (rows, GROUP_W))
    for j in range(1, HEADS_PER_GROUP):
        out = jnp.where(lane == j, arr[:, h0 + j:h0 + j + 1], out)
    return out


def _seg_matrix(k, lanes_per_head, h0):
    r = lax.broadcasted_iota(jnp.int32, (k, LANES), 0)
    c = lax.broadcasted_iota(jnp.int32, (k, LANES), 1)
    return jnp.where(c == h0 + r // lanes_per_head, 1.0, 0.0).astype(BF16)


def _seg_dot(t, e):
    hi = _b(t)
    lo = _b(t - hi.astype(F32))
    return _dot(hi, e) + _dot(lo, e)


def _head_sums(t, e, rows):
    if rows >= 8:
        return _seg_dot(t, e)
    return _seg_dot(jnp.broadcast_to(t, (8, t.shape[1])), e)[0:rows]


def _pair_masks(x):
    lane = lax.broadcasted_iota(jnp.int32, x.shape, 1)
    zero = jnp.zeros_like(x)
    return jnp.where(lane < SSM_HEAD_DIM, x, zero), jnp.where(lane >= SSM_HEAD_DIM, x, zero)


def _ssd_fwd2(xact, dt, acs, acst, dsk_e):
    S = xact.shape[0]
    ch = SSM_CHUNK
    nch = S // ch

    def body(x_ref, dt_ref, acs_ref, acst_ref, dsk_ref, y_ref, hs_ref, h_ref):
        c = pl.program_id(0)

        @pl.when(c == 0)
        def _():
            h_ref[...] = jnp.zeros_like(h_ref)

        dt_all = dt_ref[...]
        acs_all = acs_ref[...]
        acst_all = acst_ref[0]
        alast = acs_all[ch - 1:ch, :]
        eacs = jnp.exp(acs_all)
        dsd_all = jnp.exp(alast - acs_all)
        cd_all = jnp.exp(alast)
        ii = lax.broadcasted_iota(jnp.int32, (ch, ch), 0)
        jj = lax.broadcasted_iota(jnp.int32, (ch, ch), 1)
        low = ii >= jj
        for g in range(SSM_GROUPS):
            xs = x_ref[:, g * GROUP_W:(g + 1) * GROUP_W]
            bb = _b(x_ref[:, B_COL0 + g * SSM_STATE:B_COL0 + (g + 1) * SSM_STATE])
            cc = _b(x_ref[:, C_COL0 + g * SSM_STATE:C_COL0 + (g + 1) * SSM_STATE])
            cb = _dot_nt(cc, bb)
            xd = xs * _expand_heads(dt_all, g, ch)
            xdb = _b(xd)
            ht = h_ref[g]
            rest = (_dot(cc, _b(ht)) * _expand_heads(eacs, g, ch)
                    + dsk_ref[:, g * GROUP_W:(g + 1) * GROUP_W] * xs)
            for p in range(HEADS_PER_GROUP // 2):
                lms = []
                for h in (HEADS_PER_GROUP * g + 2 * p, HEADS_PER_GROUP * g + 2 * p + 1):
                    diff = acs_all[:, h:h + 1] - acst_all[h:h + 1, :]
                    lms.append(_b(cb * jnp.exp(jnp.where(low, diff, -jnp.inf))))
                xa, xb = _pair_masks(xdb[:, p * LANES:(p + 1) * LANES])
                yp = _dot(jnp.concatenate(lms, axis=1), jnp.concatenate([xa, xb], axis=0))
                y_ref[:, g * GROUP_W + p * LANES:g * GROUP_W + (p + 1) * LANES] = (
                    yp + rest[:, p * LANES:(p + 1) * LANES])
            hs_ref[0, g] = ht
            st = _dot_tn(bb, _b(xd * _expand_heads(dsd_all, g, ch)))
            h_ref[g] = ht * _expand_heads(cd_all, g, 1) + st

    blk = pl.BlockSpec((ch, LANES), lambda c: (c, 0))
    return pl.pallas_call(
        body, name="ssd_fwd", grid=(nch,),
        in_specs=[pl.BlockSpec((ch, CONV_DIM), lambda c: (c, 0)), blk, blk,
                  pl.BlockSpec((1, SSM_HEADS, ch), lambda c: (c, 0, 0)),
                  pl.BlockSpec((1, SSM_INNER), lambda c: (0, 0))],
        out_specs=[pl.BlockSpec((ch, SSM_INNER), lambda c: (c, 0)),
                   pl.BlockSpec((1, SSM_GROUPS, SSM_STATE, GROUP_W), lambda c: (c, 0, 0, 0))],
        out_shape=[jax.ShapeDtypeStruct((S, SSM_INNER), F32),
                   jax.ShapeDtypeStruct((nch, SSM_GROUPS, SSM_STATE, GROUP_W), F32)],
        scratch_shapes=[pltpu.VMEM((SSM_GROUPS, SSM_STATE, GROUP_W), F32)],
        compiler_params=_params(("arbitrary",)),
    )(xact, dt, acs, acst, dsk_e)


def _ssd_bwd2(xact, dt, acs, acst, sig, a_neg, dsk_e, hs, dy):
    S = xact.shape[0]
    ch = SSM_CHUNK
    nch = S // ch

    def body(x_ref, dt_ref, acs_ref, acst_ref, sig_ref, a_ref, dsk_ref, hs_ref, dy_ref,
             dx_ref, ddt_ref, st_ref, dh_ref, rows_ref):
        step = pl.program_id(0)

        @pl.when(step == 0)
        def _():
            dh_ref[...] = jnp.zeros_like(dh_ref)
            st_ref[...] = jnp.zeros_like(st_ref)
            rows_ref[...] = jnp.zeros_like(rows_ref)

        dt_all = dt_ref[...]
        acs_all = acs_ref[...]
        acst_all = acst_ref[0]
        alast = acs_all[ch - 1:ch, :]
        eacs = jnp.exp(acs_all)
        dsd_all = jnp.exp(alast - acs_all)
        cd_all = jnp.exp(alast)
        ii = lax.broadcasted_iota(jnp.int32, (ch, ch), 0)
        jj = lax.broadcasted_iota(jnp.int32, (ch, ch), 1)
        low = ii >= jj
        lane = lax.broadcasted_iota(jnp.int32, (ch, LANES), 1)
        cols = jnp.zeros((ch, LANES), F32)
        ddt = jnp.zeros((ch, LANES), F32)
        dal = jnp.zeros((1, LANES), F32)
        ddsk = jnp.zeros((1, LANES), F32)
        for g in range(SSM_GROUPS):
            xs = x_ref[:, g * GROUP_W:(g + 1) * GROUP_W]
            bb = _b(x_ref[:, B_COL0 + g * SSM_STATE:B_COL0 + (g + 1) * SSM_STATE])
            cc = _b(x_ref[:, C_COL0 + g * SSM_STATE:C_COL0 + (g + 1) * SSM_STATE])
            cb = _dot_nt(cc, bb)
            dt_e = _expand_heads(dt_all, g, ch)
            ea_e = _expand_heads(eacs, g, ch)
            dsd_e = _expand_heads(dsd_all, g, ch)
            cd_e = _expand_heads(cd_all, g, 1)
            xd = xs * dt_e
            xdb = _b(xd)
            dyg = dy_ref[:, g * GROUP_W:(g + 1) * GROUP_W]
            dyb = _b(dyg)
            ht = hs_ref[0, g]
            htb = _b(ht)
            dhn = dh_ref[g]
            dhnb = _b(dhn)
            zz = _dot(cc, htb)
            dzb = _b(dyg * ea_e)
            d_c = _dot_nt(dzb, htb)
            dh_in = _dot_tn(cc, dzb)
            ww = _dot(bb, dhnb)
            xdd = xd * dsd_e
            d_b = _dot_nt(_b(xdd), dhnb)
            t2 = ww * xdd
            e_g = _seg_matrix(GROUP_W, SSM_HEAD_DIM, HEADS_PER_GROUP * g)
            cols = cols + _head_sums(dyg * zz * ea_e - t2, e_g, ch)
            dal = dal + _head_sums(jnp.sum(t2, axis=0, keepdims=True), e_g, 1) + cd_all * _head_sums(
                jnp.sum(dhn * ht, axis=0, keepdims=True), e_g, 1)
            dh_ref[g] = dh_in + dhn * cd_e
            ddsk = ddsk + _head_sums(jnp.sum(dyg * xs, axis=0, keepdims=True), e_g, 1)
            dxd_rest = ww * dsd_e
            dcb = jnp.zeros((ch, ch), F32)
            for p in range(HEADS_PER_GROUP // 2):
                dya, dyb2 = _pair_masks(dyb[:, p * LANES:(p + 1) * LANES])
                xp = xdb[:, p * LANES:(p + 1) * LANES]
                lms, gms = [], []
                for h, dyh in ((HEADS_PER_GROUP * g + 2 * p, dya), (HEADS_PER_GROUP * g + 2 * p + 1, dyb2)):
                    diff = acs_all[:, h:h + 1] - acst_all[h:h + 1, :]
                    decay = jnp.exp(jnp.where(low, diff, -jnp.inf))
                    lm = cb * decay
                    dlm = _dot_nt(dyh, xp)
                    gm = dlm * lm
                    dcb = dcb + dlm * decay
                    rows_ref[h:h + 1, :] = jnp.sum(gm, axis=0, keepdims=True)
                    lms.append(_b(lm))
                    gms.append(gm)
                h0 = HEADS_PER_GROUP * g + 2 * p
                cols = cols + _head_sums(jnp.concatenate(gms, axis=1), _seg_matrix(2 * ch, ch, h0), ch)
                dxd = _dot_tn(jnp.concatenate(lms, axis=0), jnp.concatenate([dya, dyb2], axis=0))
                dxd = dxd + dxd_rest[:, p * LANES:(p + 1) * LANES]
                sl = slice(g * GROUP_W + p * LANES, g * GROUP_W + (p + 1) * LANES)
                dx_ref[:, sl] = (dsk_ref[:, sl] * dyg[:, p * LANES:(p + 1) * LANES]
                                 + dxd * dt_e[:, p * LANES:(p + 1) * LANES])
                ddt = ddt + _head_sums(dxd * xs[:, p * LANES:(p + 1) * LANES],
                                       _seg_matrix(LANES, SSM_HEAD_DIM, h0), ch)
            dcbb = _b(dcb)
            dx_ref[:, C_COL0 + g * SSM_STATE:C_COL0 + (g + 1) * SSM_STATE] = d_c + _dot(dcbb, bb)
            dx_ref[:, B_COL0 + g * SSM_STATE:B_COL0 + (g + 1) * SSM_STATE] = d_b + _dot_tn(dcbb, cc)
        rowi = lax.broadcasted_iota(jnp.int32, (ch, 1), 0)
        dacs = cols - rows_ref[...].T + jnp.where(rowi == ch - 1, dal, 0.0)
        dla = _dot_hi(jnp.where(ii <= jj, 1.0, 0.0), dacs)
        a_row = a_ref[...]
        ddt_raw = (ddt + dla * a_row) * sig_ref[...]
        ddt_ref[...] = _b(ddt_raw)
        st_ref[0:1, :] += jnp.sum(dla * dt_all, axis=0, keepdims=True) * a_row
        st_ref[1:2, :] += ddsk
        st_ref[2:3, :] += jnp.sum(ddt_raw, axis=0, keepdims=True)

    rc = lambda s: nch - 1 - s
    blk = pl.BlockSpec((ch, LANES), lambda s: (rc(s), 0))
    return pl.pallas_call(
        body, name="ssd_bwd", grid=(nch,),
        in_specs=[pl.BlockSpec((ch, CONV_DIM), lambda s: (rc(s), 0)), blk, blk,
                  pl.BlockSpec((1, SSM_HEADS, ch), lambda s: (rc(s), 0, 0)), blk,
                  pl.BlockSpec((1, LANES), lambda s: (0, 0)),
                  pl.BlockSpec((1, SSM_INNER), lambda s: (0, 0)),
                  pl.BlockSpec((1, SSM_GROUPS, SSM_STATE, GROUP_W), lambda s: (rc(s), 0, 0, 0)),
                  pl.BlockSpec((ch, SSM_INNER), lambda s: (rc(s), 0))],
        out_specs=[pl.BlockSpec((ch, CONV_DIM), lambda s: (rc(s), 0)), blk,
                   pl.BlockSpec((8, LANES), lambda s: (0, 0))],
        out_shape=[jax.ShapeDtypeStruct((S, CONV_DIM), F32), jax.ShapeDtypeStruct((S, LANES), BF16),
                   jax.ShapeDtypeStruct((8, LANES), F32)],
        scratch_shapes=[pltpu.VMEM((SSM_GROUPS, SSM_STATE, GROUP_W), F32), pltpu.VMEM((LANES, ch), F32)],
        compiler_params=_params(("arbitrary",)),
    )(xact, dt, acs, acst, sig, a_neg, dsk_e, hs, dy)


def _pad_lanes(v, n=LANES):
    return jnp.pad(v, ((0, 0), (0, n - v.shape[1])))


def _local_step(x, target, w):
    offs = np.cumsum((0,) + IN_SPLITS)
    w_in = w["w_in"]
    w_qkv = w_in[:, offs[0]:offs[3]]
    w_z = w_in[:, offs[3]:offs[4]]
    w_xbc = w_in[:, offs[4]:offs[5]]
    w_dt = _pad_lanes(w_in[:, offs[5]:offs[6]])
    w_g = w_in[:, offs[6]:offs[7]]
    dt_bias = _pad_lanes(w["dt_bias"])
    a_neg = _pad_lanes(-jnp.exp(w["a_log"]))
    d_skip = _pad_lanes(w["d_skip"])

    u = _rms_fwd(x, w["norm_mix_pre_w"])
    qkv = _mm_nn(u, w_qkv, BF16, "proj_qkv")
    z = _mm_nn(u, w_z, F32, "proj_z")
    xbc = _mm_nn(u, w_xbc, F32, "proj_xbc")
    dt_raw = _mm_nn(u, w_dt, F32, "proj_dt")
    gl = _mm_nn(u, w_g, F32, "proj_gate")

    q, k, v = (qkv[:, i * ATT_WIDTH:(i + 1) * ATT_WIDTH] for i in range(3))
    pats, os_, ms_, ls_ = [], [], [], []
    for d in DILATIONS:
        qp, kp, vp = _to_pat(q, d), _to_pat(k, d), _to_pat(v, d)
        pats.append((qp, kp, vp))
        o, m, l = _attn_fwd2(qp, kp, vp, d)
        os_.append(_from_pat(o, d))
        ms_.append(_from_pat(m, d))
        ls_.append(_from_pat(l, d))
    att, lse = _attn_combine(os_, ms_, ls_)
    att_o = _mm_nn(att, w["w_att_proj"], F32, "att_proj")

    xact = _conv_fwd2(xbc, w["conv_w"], w["conv_b"])
    dsk_e = jnp.repeat(w["d_skip"], SSM_HEAD_DIM, axis=1)
    dt, acs, acst, sig = _ssd_prep(dt_raw, dt_bias, a_neg)
    y_ssd, hs = _ssd_fwd2(xact, dt, acs, acst, dsk_e)
    ssm_y = _gnorm_fwd(y_ssd, z, w["ssm_norm_w"])
    ssm_o = _mm_nn(ssm_y, w["w_ssm_proj"], F32, "ssm_proj")

    mi = _gate_fwd(att_o, ssm_o, gl, w["b_gate"])
    mixed = _mm_nn(mi, w["w_out"], F32, "out_proj")
    h1, f = _post_pre(x, mixed, w["norm_mix_post_w"], w["norm_ffn_pre_w"])
    r_up, act = _mm_nn(f, w["w_up"], BF16, "ffn_up", mode="relu2")
    down = _mm_nn(act, w["w_down"], F32, "ffn_down")
    dh2, d_down, loss, g_ffn_post = _final(h1, down, w["norm_ffn_post_w"], target)

    g = {"norm_ffn_post_w": g_ffn_post}
    g["w_down"] = _mm_tn(act, d_down, "dw_down")
    dup = _mm_nn(d_down, w["w_down"].T, BF16, "d_act", mode="mul2", extra=r_up)
    g["w_up"] = _mm_tn(f, dup, "dw_up")
    df = _mm_nn(dup, w["w_up"].T, F32, "d_f")
    dh1, d_mixed, g["norm_ffn_pre_w"], g["norm_mix_post_w"] = _mid_bwd(
        dh2, df, h1, mixed, w["norm_ffn_pre_w"], w["norm_mix_post_w"])
    g["w_out"] = _mm_tn(mi, d_mixed, "dw_out")
    dmi = _mm_nn(d_mixed, w["w_out"].T, F32, "d_mi")
    d_att_o, d_ssm_o, dgl, g["b_gate"] = _gate_bwd(dmi, att_o, ssm_o, gl, w["b_gate"])

    g["w_att_proj"] = _mm_tn(att, d_att_o, "dw_att_proj")
    d_att = _mm_nn(d_att_o, w["w_att_proj"].T, F32, "d_att")
    delta, d_att_b = _attn_delta(d_att, att)
    dqs, dks, dvs = [], [], []
    for d, (qp, kp, vp) in zip(DILATIONS, pats):
        dq, dk, dv = _attn_bwd2(qp, kp, vp, _to_pat(d_att_b, d), _to_pat(lse, d), _to_pat(delta, d), d)
        dqs.append(_from_pat(dq, d))
        dks.append(_from_pat(dk, d))
        dvs.append(_from_pat(dv, d))
    dqkv = _sum_qkv(dqs, dks, dvs)

    g["w_ssm_proj"] = _mm_tn(ssm_y, d_ssm_o, "dw_ssm_proj")
    d_ssm_y = _mm_nn(d_ssm_o, w["w_ssm_proj"].T, F32, "d_ssm_y")
    dy_ssd, dz, g["ssm_norm_w"] = _gnorm_bwd(d_ssm_y, y_ssd, z, w["ssm_norm_w"])
    dxact, ddt_raw, stats = _ssd_bwd2(xact, dt, acs, acst, sig, a_neg, dsk_e, hs, dy_ssd)
    g["a_log"] = stats[0:1, :SSM_HEADS]
    g["d_skip"] = stats[1:2, :SSM_HEADS]
    g["dt_bias"] = stats[2:3, :SSM_HEADS]
    dxbc, g["conv_w"], g["conv_b"] = _conv_bwd2(xbc, dxact, w["conv_w"], w["conv_b"])

    pieces = [(dqkv, w_qkv), (dz, w_z), (dxbc, w_xbc), (ddt_raw, w_dt), (dgl, w_g)]
    du = None
    gw = []
    for i, (dp, wp) in enumerate(pieces):
        gw.append(_mm_tn(u, dp, f"dw_in_{i}"))
        du = _mm_nn(dp, wp.T, F32, f"d_u_{i}", acc=du)
    gw[3] = gw[3][:, :SSM_HEADS]
    g["w_in"] = jnp.concatenate(gw, axis=1)
    grad_x, g["norm_mix_pre_w"] = _first_bwd(dh1, du, x, w["norm_mix_pre_w"])
    return loss, grad_x, g


BIG = ("w_in", "w_att_proj", "w_ssm_proj", "w_out", "w_up", "w_down")
BIG_FULL_SHAPES = {"w_in": (D_MODEL, IN_PROJ_WIDTH), "w_att_proj": (ATT_WIDTH, D_MODEL),
                   "w_ssm_proj": (SSM_INNER, D_MODEL), "w_out": (D_MODEL, D_MODEL),
                   "w_up": (D_MODEL, FFN_HIDDEN), "w_down": (FFN_HIDDEN, D_MODEL)}
BIG_COL_SHARDED = {"w_in": True, "w_att_proj": True, "w_ssm_proj": False, "w_out": False, "w_up": True,
                   "w_down": False}
PACK_COLS = 1024
PACK_ROWS = 5760
PACK_HALF = PACK_ROWS // 2
PACK_BLOCK = 576
SMALL = ("norm_mix_pre_w", "b_gate", "conv_b", "dt_bias", "a_log", "d_skip", "ssm_norm_w",
         "norm_mix_post_w", "norm_ffn_pre_w", "norm_ffn_post_w")
SMALL_ROWS = 232


def _shard_shape(name):
    r, c = BIG_FULL_SHAPES[name]
    return (r, c // N_CHIPS) if BIG_COL_SHARDED[name] else (r // N_CHIPS, c)


def _pack(shards, dtype):
    flat = [shards[n].astype(dtype).reshape(-1, PACK_COLS) for n in BIG]
    rows = sum(f.shape[0] for f in flat)
    flat.append(jnp.zeros((PACK_ROWS - rows, PACK_COLS), dtype))
    return jnp.concatenate(flat, axis=0)


def _unpack(packed):
    out, r0 = {}, 0
    for n in BIG:
        shp = _shard_shape(n)
        rows = shp[0] * shp[1] // PACK_COLS
        out[n] = packed[r0:r0 + rows].reshape(shp)
        r0 += rows
    return out


def _unpack_full(gathered):
    out, r0 = {}, 0
    for n in BIG:
        shp = _shard_shape(n)
        rows = shp[0] * shp[1] // PACK_COLS
        sh = gathered[:, r0:r0 + rows].reshape((N_CHIPS,) + shp)
        if BIG_COL_SHARDED[n]:
            out[n] = sh.transpose(1, 0, 2).reshape(BIG_FULL_SHAPES[n])
        else:
            out[n] = sh.reshape(BIG_FULL_SHAPES[n])
        r0 += rows
    return out


def _pack_full(grads):
    parts = []
    rows_total = 0
    for n in BIG:
        shp = _shard_shape(n)
        gfull = grads[n]
        if BIG_COL_SHARDED[n]:
            sh = gfull.reshape(shp[0], N_CHIPS, shp[1]).transpose(1, 0, 2)
        else:
            sh = gfull.reshape((N_CHIPS,) + shp)
        parts.append(sh.reshape(N_CHIPS, -1, PACK_COLS))
        rows_total += parts[-1].shape[1]
    parts.append(jnp.zeros((N_CHIPS, PACK_ROWS - rows_total, PACK_COLS), F32))
    return jnp.concatenate(parts, axis=1)


def _mesh_pos():
    return lax.axis_index("x"), lax.axis_index("y"), lax.axis_index("c")


def _other_chips(x, y):
    return [(1 - x, y), (x, 1 - y), (1 - x, 1 - y)]


ANY = pl.BlockSpec(memory_space=pl.ANY)


def _allgather_packed(wpack):
    half = PACK_HALF

    def body(w_ref, out_ref, send_sems, recv_sems):
        x, y, c = _mesh_pos()
        me = 2 * x + y
        sibling = (x, y, 1 - c)
        chips = _other_chips(x, y)

        def rows(chip, h):
            return out_ref.at[chip, pl.ds(h * half, half), :]

        def copy(k, chip, h, to, src=None):
            return pltpu.make_async_remote_copy(
                src_ref=rows(chip, h) if src is None else src, dst_ref=rows(chip, h),
                send_sem=send_sems.at[k], recv_sem=recv_sems.at[k], device_id=to, device_id_type=MESH)

        mine_half = w_ref.at[pl.ds(c * half, half), :]
        first = [copy(j, me, c, (*chip, c), src=mine_half) for j, chip in enumerate(chips)]
        for cp in first:
            cp.start()
        passed = [copy(3 + j, 2 * chip[0] + chip[1], c, sibling) for j, chip in enumerate(chips)]
        for j, chip in enumerate(chips):
            copy(j, 2 * chip[0] + chip[1], c, (x, y, c)).wait_recv()
            passed[j].start()
        for j, chip in enumerate(chips):
            copy(3 + j, 2 * chip[0] + chip[1], 1 - c, (x, y, c)).wait_recv()
        for cp in first + passed:
            cp.wait_send()

    return pl.pallas_call(
        body, name="allgather_weights",
        out_shape=jax.ShapeDtypeStruct((N_CHIPS,) + wpack.shape, wpack.dtype),
        in_specs=[ANY], out_specs=ANY,
        scratch_shapes=[pltpu.SemaphoreType.DMA((6,)), pltpu.SemaphoreType.DMA((6,))],
        compiler_params=pltpu.CompilerParams(has_side_effects=True),
    )(wpack)


def _exchange_halves(gpack):
    half = PACK_HALF

    def body(g_ref, out_ref, send_sem, recv_sem):
        x, y, c = _mesh_pos()
        cp = pltpu.make_async_remote_copy(
            src_ref=g_ref.at[:, pl.ds((1 - c) * half, half), :], dst_ref=out_ref,
            send_sem=send_sem, recv_sem=recv_sem, device_id=(x, y, 1 - c), device_id_type=MESH)
        cp.start()
        cp.wait()

    return pl.pallas_call(
        body, name="rs_pair_exchange",
        out_shape=jax.ShapeDtypeStruct((N_CHIPS, half, PACK_COLS), F32),
        in_specs=[ANY], out_specs=ANY,
        scratch_shapes=[pltpu.SemaphoreType.DMA, pltpu.SemaphoreType.DMA],
        compiler_params=pltpu.CompilerParams(has_side_effects=True),
    )(gpack)


def _pair_add(gpack, recv, c_idx):
    nb = PACK_HALF // PACK_BLOCK

    def body(c_ref, g_ref, r_ref, o_ref):
        o_ref[...] = _b(g_ref[...] + r_ref[...])

    blk = (1, PACK_BLOCK, PACK_COLS)
    return pl.pallas_call(
        body, name="rs_pair_add",
        grid_spec=pltpu.PrefetchScalarGridSpec(
            num_scalar_prefetch=1, grid=(N_CHIPS, nb),
            in_specs=[pl.BlockSpec(blk, lambda s, i, c: (s, c[0] * nb + i, 0)),
                      pl.BlockSpec(blk, lambda s, i, c: (s, i, 0))],
            out_specs=pl.BlockSpec(blk, lambda s, i, c: (s, i, 0))),
        out_shape=jax.ShapeDtypeStruct((N_CHIPS, PACK_HALF, PACK_COLS), BF16),
        compiler_params=_params(("arbitrary", "arbitrary")),
    )(c_idx, gpack, recv)


def _exchange_chips(ppack):
    def body(p_ref, out_ref, send_sems, recv_sems):
        x, y, c = _mesh_pos()
        chips = _other_chips(x, y)
        cps = [pltpu.make_async_remote_copy(
            src_ref=p_ref.at[2 * chip[0] + chip[1]], dst_ref=out_ref.at[j],
            send_sem=send_sems.at[j], recv_sem=recv_sems.at[j], device_id=(*chip, c), device_id_type=MESH)
            for j, chip in enumerate(chips)]
        for cp in cps:
            cp.start()
        for cp in cps:
            cp.wait_recv()
        for cp in cps:
            cp.wait_send()

    return pl.pallas_call(
        body, name="rs_chip_exchange",
        out_shape=jax.ShapeDtypeStruct((N_CHIPS - 1, PACK_HALF, PACK_COLS), ppack.dtype),
        in_specs=[ANY], out_specs=ANY,
        scratch_shapes=[pltpu.SemaphoreType.DMA((3,)), pltpu.SemaphoreType.DMA((3,))],
        compiler_params=pltpu.CompilerParams(has_side_effects=True),
    )(ppack)


def _chip_add(ppack, recv, me_idx):
    nb = PACK_HALF // PACK_BLOCK

    def body(m_ref, p_ref, r0_ref, r1_ref, r2_ref, o_ref):
        o_ref[...] = ((p_ref[0].astype(F32) + r0_ref[0].astype(F32)) + r1_ref[0].astype(F32)) + r2_ref[0].astype(F32)

    blk = (1, PACK_BLOCK, PACK_COLS)
    return pl.pallas_call(
        body, name="rs_chip_add",
        grid_spec=pltpu.PrefetchScalarGridSpec(
            num_scalar_prefetch=1, grid=(nb,),
            in_specs=[pl.BlockSpec(blk, lambda i, m: (m[0], i, 0)),
                      pl.BlockSpec(blk, lambda i, m: (0, i, 0)),
                      pl.BlockSpec(blk, lambda i, m: (1, i, 0)),
                      pl.BlockSpec(blk, lambda i, m: (2, i, 0))],
            out_specs=pl.BlockSpec((PACK_BLOCK, PACK_COLS), lambda i, m: (i, 0))),
        out_shape=jax.ShapeDtypeStruct((PACK_HALF, PACK_COLS), F32),
        compiler_params=_params(("arbitrary",)),
    )(me_idx, ppack, recv, recv, recv)


def _share_halves(qhalf):
    def body(q_ref, out_ref, send_sem, recv_sem):
        x, y, c = _mesh_pos()
        cp = pltpu.make_async_remote_copy(
            src_ref=q_ref, dst_ref=out_ref, send_sem=send_sem, recv_sem=recv_sem,
            device_id=(x, y, 1 - c), device_id_type=MESH)
        cp.start()
        cp.wait()

    return pl.pallas_call(
        body, name="rs_share_halves",
        out_shape=jax.ShapeDtypeStruct(qhalf.shape, F32),
        in_specs=[ANY], out_specs=ANY,
        scratch_shapes=[pltpu.SemaphoreType.DMA, pltpu.SemaphoreType.DMA],
        compiler_params=pltpu.CompilerParams(has_side_effects=True),
    )(qhalf)


def _allreduce_small(part, name):
    rows = part.shape[0]

    def body(p_ref, out_ref, buf, send_sems, recv_sems, local_sem):
        x, y, c = _mesh_pos()
        me, sibling = (x, y, c), (x, y, 1 - c)
        chips = _other_chips(x, y)

        def slot(px, py, pc):
            return buf.at[pl.ds((4 * px + 2 * py + pc) * rows, rows), :]

        def copy(k, block, to, src=None):
            return pltpu.make_async_remote_copy(
                src_ref=slot(*block) if src is None else src, dst_ref=slot(*block),
                send_sem=send_sems.at[k], recv_sem=recv_sems.at[k], device_id=to, device_id_type=MESH)

        mine = pltpu.make_async_copy(p_ref, slot(*me), local_sem)
        mine.start()
        first = [copy(0, me, sibling, src=p_ref)]
        first += [copy(1 + j, me, (*chip, c), src=p_ref) for j, chip in enumerate(chips)]
        for cp in first:
            cp.start()
        passed = [copy(4 + j, (*chip, c), sibling) for j, chip in enumerate(chips)]
        for j, chip in enumerate(chips):
            copy(1 + j, (*chip, c), me).wait_recv()
            passed[j].start()
        copy(0, sibling, me).wait_recv()
        for j, chip in enumerate(chips):
            copy(4 + j, (*chip, 1 - c), me).wait_recv()
        for cp in first + passed:
            cp.wait_send()
        mine.wait()
        acc = buf[pl.ds(0, rows), :]
        for k in range(1, N_DEV):
            acc = acc + buf[pl.ds(k * rows, rows), :]
        out_ref[...] = acc

    return pl.pallas_call(
        body, name=name,
        out_shape=jax.ShapeDtypeStruct(part.shape, F32),
        in_specs=[pl.BlockSpec(memory_space=pltpu.VMEM)],
        out_specs=pl.BlockSpec(memory_space=pltpu.VMEM),
        scratch_shapes=[pltpu.VMEM((N_DEV * rows, LANES), F32), pltpu.SemaphoreType.DMA((7,)),
                        pltpu.SemaphoreType.DMA((7,)), pltpu.SemaphoreType.DMA],
        compiler_params=pltpu.CompilerParams(has_side_effects=True),
    )(part)


def _adamw(w, g, m, v, name):
    R, C = w.shape
    bs = _pick(R, (128, 64, 32, 8)) if R % 8 == 0 else R
    c1 = 1.0 / (1.0 - ADAM_B1 ** ADAM_STEP)
    c2 = 1.0 / (1.0 - ADAM_B2 ** ADAM_STEP)

    def body(w_ref, g_ref, m_ref, v_ref, d_ref, nm_ref, nv_ref):
        gg = g_ref[...]
        nm = ADAM_B1 * m_ref[...] + (1.0 - ADAM_B1) * gg
        nv = ADAM_B2 * v_ref[...] + (1.0 - ADAM_B2) * (gg * gg)
        nm_ref[...] = nm
        nv_ref[...] = nv
        d_ref[...] = -ADAM_LR * ((nm * c1) / (jnp.sqrt(nv * c2) + ADAM_EPS) + ADAM_WD * w_ref[...])

    spec = pl.BlockSpec((bs, C), lambda i: (i, 0))
    shp = jax.ShapeDtypeStruct((R, C), F32)
    return pl.pallas_call(
        body, name=name, grid=(R // bs,), in_specs=[spec] * 4, out_specs=[spec] * 3, out_shape=[shp] * 3,
        compiler_params=_params(("parallel",)),
    )(w, g, m, v)


WEIGHTS = ("norm_mix_pre_w", "w_in", "b_gate", "conv_w", "conv_b", "dt_bias", "a_log", "d_skip",
           "ssm_norm_w", "w_att_proj", "w_ssm_proj", "w_out", "norm_mix_post_w", "norm_ffn_pre_w", "w_up",
           "w_down", "norm_ffn_post_w")


def _flat_small(vals, conv_w_full):
    flat = [vals[n].reshape(-1) for n in SMALL] + [conv_w_full.reshape(-1)]
    v = jnp.concatenate(flat)
    return jnp.pad(v, (0, SMALL_ROWS * LANES - v.shape[0])).reshape(SMALL_ROWS, LANES)


def kernel(x, norm_mix_pre_w, w_in, b_gate, conv_w, conv_b, dt_bias, a_log, d_skip, ssm_norm_w, w_att_proj, w_ssm_proj, w_out, norm_mix_post_w, norm_ffn_pre_w, w_up, w_down, norm_ffn_post_w, loss_target, m_norm_mix_pre_w, m_w_in, m_b_gate, m_conv_w, m_conv_b, m_dt_bias, m_a_log, m_d_skip, m_ssm_norm_w, m_w_att_proj, m_w_ssm_proj, m_w_out, m_norm_mix_post_w, m_norm_ffn_pre_w, m_w_up, m_w_down, m_norm_ffn_post_w, v_norm_mix_pre_w, v_w_in, v_b_gate, v_conv_w, v_conv_b, v_dt_bias, v_a_log, v_d_skip, v_ssm_norm_w, v_w_att_proj, v_w_ssm_proj, v_w_out, v_norm_mix_post_w, v_norm_ffn_pre_w, v_w_up, v_w_down, v_norm_ffn_post_w):
    args = locals()

    def strip(a):
        return a[0] if a.ndim == 3 else a

    wts = {n: strip(args[n]) for n in WEIGHTS}
    mom = {n: strip(args["m_" + n]) for n in WEIGHTS}
    var = {n: strip(args["v_" + n]) for n in WEIGHTS}
    xi, yi, ci = _mesh_pos()
    chip = 2 * xi + yi

    wpack = _pack({n: wts[n] for n in BIG}, BF16)
    gathered = lax.dynamic_update_slice(_allgather_packed(wpack), wpack[None], (chip, 0, 0))
    full = _unpack_full(gathered)
    cw_cols = CONV_DIM // N_CHIPS
    conv_slab = lax.dynamic_update_slice(jnp.zeros((SSM_CONV, CONV_DIM), F32),
                                         jnp.where(ci == 0, wts["conv_w"], 0.0), (0, chip * cw_cols))
    small_in = jnp.pad(conv_slab.reshape(-1), (0, SMALL_ROWS * LANES - SSM_CONV * CONV_DIM))
    conv_full = _allreduce_small(small_in.reshape(SMALL_ROWS, LANES), "gather_conv_w")
    full["conv_w"] = conv_full.reshape(-1)[:SSM_CONV * CONV_DIM].reshape(SSM_CONV, CONV_DIM)
    for n in SMALL:
        full[n] = wts[n]

    loss_part, grad_x, g = _local_step(x[0], loss_target[0], full)
    loss = lax.psum(loss_part[0, 0], ("x", "y", "c"))

    gpack = _pack_full(g)
    recv = _exchange_halves(gpack)
    ppack = _pair_add(gpack, recv, ci.reshape(1).astype(jnp.int32))
    recv3 = _exchange_chips(ppack)
    qhalf = _chip_add(ppack, recv3, chip.reshape(1).astype(jnp.int32))
    other = _share_halves(qhalf)
    south = ci == 0
    gshard = _unpack(jnp.concatenate([jnp.where(south, qhalf, other), jnp.where(south, other, qhalf)], axis=0))
    small_sum = _allreduce_small(_flat_small(g, g["conv_w"]), "allreduce_small_grads").reshape(-1)
    grads, off = {}, 0
    for n in SMALL:
        sz = wts[n].size
        grads[n] = small_sum[off:off + sz].reshape(wts[n].shape)
        off += sz
    conv_g = small_sum[off:off + SSM_CONV * CONV_DIM].reshape(SSM_CONV, CONV_DIM)
    grads["conv_w"] = lax.dynamic_slice(conv_g, (0, chip * cw_cols), (SSM_CONV, cw_cols))
    grads.update(gshard)

    delta, new_m, new_v = {}, {}, {}
    for n in BIG:
        delta[n], new_m[n], new_v[n] = _adamw(wts[n], grads[n], mom[n], var[n], f"adamw_{n}")
    small_names = SMALL + ("conv_w",)

    def pack_small(d):
        v = jnp.concatenate([d[n].reshape(-1) for n in small_names])
        rows = -(-v.shape[0] // (8 * LANES)) * 8
        return jnp.pad(v, (0, rows * LANES - v.shape[0])).reshape(rows, LANES)

    ds, ms, vs = _adamw(pack_small(wts), pack_small(grads), pack_small(mom), pack_small(var), "adamw_small")
    off = 0
    for n in small_names:
        sz = wts[n].size
        for dst, src in ((delta, ds), (new_m, ms), (new_v, vs)):
            dst[n] = src.reshape(-1)[off:off + sz].reshape(wts[n].shape)
        off += sz

    out = [loss, grad_x[None]]
    for d in (grads, delta, new_m, new_v):
        out += [d[n][None] if args[n].ndim == 3 else d[n] for n in WEIGHTS]
    return tuple(out)
```

```python
import functools
import math

import numpy as np
import jax
import jax.numpy as jnp
from jax import lax
from jax.experimental import pallas as pl
from jax.experimental.pallas import tpu as pltpu

F32 = jnp.float32
BF16 = jnp.bfloat16

D_MODEL = 1024
HEAD_DIM = 64
N_ATT_HEADS = 12
ATT_WIDTH = N_ATT_HEADS * HEAD_DIM
DILATIONS = (1, 4, 16)
ATT_BLOCK = 128
SSM_INNER = 2048
SSM_HEADS = 32
SSM_GROUPS = 8
HEADS_PER_GROUP = SSM_HEADS // SSM_GROUPS
SSM_HEAD_DIM = 64
SSM_STATE = 128
SSM_CONV = 4
SSM_CHUNK = 128
CONV_DIM = SSM_INNER + 2 * SSM_GROUPS * SSM_STATE
FFN_HIDDEN = 4 * D_MODEL
IN_SPLITS = (ATT_WIDTH, ATT_WIDTH, ATT_WIDTH, SSM_INNER, CONV_DIM, SSM_HEADS, 2 * D_MODEL)
IN_PROJ_WIDTH = sum(IN_SPLITS)
RMS_EPS = 1e-6
LANES = 128
NEG_BIG = -1e30

ADAM_LR = 0.001
ADAM_B1 = 0.9
ADAM_B2 = 0.999
ADAM_EPS = 1e-08
ADAM_WD = 0.01
ADAM_STEP = 10

N_CHIPS = 4
N_DEV = 8
VMEM_LIMIT = 56 * 1024 * 1024
MESH = pl.DeviceIdType.MESH


def _alibi_slopes(n):
    def pow2(m):
        start = 2.0 ** (-8.0 / m)
        return [start ** (i + 1) for i in range(m)]
    if (n & (n - 1)) == 0:
        s = pow2(n)
    else:
        c = 2 ** int(math.floor(math.log2(n)))
        s = pow2(c) + pow2(2 * c)[0::2][: n - c]
    return [float(v) for v in np.array(s, dtype=np.float32)]


def _params(sem):
    return pltpu.CompilerParams(dimension_semantics=sem, vmem_limit_bytes=VMEM_LIMIT)


def _dot(a, b):
    return lax.dot_general(a, b, (((1,), (0,)), ((), ())), preferred_element_type=F32)


def _dot_nt(a, b):
    return lax.dot_general(a, b, (((1,), (1,)), ((), ())), preferred_element_type=F32)


def _dot_tn(a, b):
    return lax.dot_general(a, b, (((0,), (0,)), ((), ())), preferred_element_type=F32)


def _dot_hi(a, b):
    return lax.dot_general(a, b, (((1,), (0,)), ((), ())), preferred_element_type=F32,
                           precision=lax.Precision.HIGHEST)


def _dot_tn_hi(a, b):
    return lax.dot_general(a, b, (((0,), (0,)), ((), ())), preferred_element_type=F32,
                           precision=lax.Precision.HIGHEST)


def _b(x):
    return x.astype(BF16)


def _sigmoid(x):
    return 1.0 / (1.0 + jnp.exp(-x))


def _pick(n, cands):
    for c in cands:
        if n % c == 0:
            return c
    raise ValueError(f"no tile for {n}")


def _mm_nn(a, b, out_dtype, name, acc=None, mode=None, extra=None):
    M, K = a.shape
    _, N = b.shape
    tm = 512
    tn = _pick(N, (1024, 768, 512, 256, 128))
    tk = K if K <= 2304 else _pick(K, (2048, 1024))
    nk = K // tk
    side = acc if acc is not None else extra
    n_out = 2 if mode == "relu2" else 1

    def body(*refs):
        a_ref, b_ref = refs[0], refs[1]
        s_ref = refs[2] if side is not None else None
        o_refs = refs[-n_out - (nk > 1):len(refs) - (nk > 1)]

        def finish(r):
            if mode == "relu2":
                r = jnp.maximum(r, 0.0)
                o_refs[0][...] = _b(r)
                o_refs[1][...] = _b(r * r)
            elif mode == "mul2":
                o_refs[0][...] = _b(r * (2.0 * s_ref[...].astype(F32)))
            else:
                if acc is not None:
                    r = r + s_ref[...]
                o_refs[0][...] = r.astype(out_dtype)

        part = _dot(_b(a_ref[...]), _b(b_ref[...]))
        if nk == 1:
            finish(part)
            return
        acc_ref = refs[-1]
        k = pl.program_id(2)

        @pl.when(k == 0)
        def _():
            acc_ref[...] = part

        @pl.when(jnp.logical_and(k > 0, k < nk - 1))
        def _():
            acc_ref[...] += part

        @pl.when(k == nk - 1)
        def _():
            finish(acc_ref[...] + part)

    tile = pl.BlockSpec((tm, tn), lambda j, i, k: (i, j))
    in_specs = [pl.BlockSpec((tm, tk), lambda j, i, k: (i, k)),
                pl.BlockSpec((tk, tn), lambda j, i, k: (k, j))]
    args = [a, b]
    if side is not None:
        in_specs.append(tile)
        args.append(side)
    odt = BF16 if mode in ("relu2", "mul2") else out_dtype
    outs = pl.pallas_call(
        body, name=name, grid=(N // tn, M // tm, nk),
        in_specs=in_specs,
        out_specs=[tile] * n_out,
        out_shape=[jax.ShapeDtypeStruct((M, N), odt)] * n_out,
        scratch_shapes=[pltpu.VMEM((tm, tn), F32)] if nk > 1 else [],
        compiler_params=_params(("parallel", "parallel", "arbitrary")),
    )(*args)
    return outs if n_out > 1 else outs[0]


def _mm_tn(a, b, name):
    S, Ka = a.shape
    _, N = b.shape
    tka = _pick(Ka, (1024, 768, 512))
    tn = _pick(N, (1024, 768, 512, 256, 128))
    ts = 1024 if S % 1024 == 0 else 512
    ns = S // ts

    def body(a_ref, b_ref, o_ref, acc_ref):
        s = pl.program_id(2)
        part = _dot_tn(_b(a_ref[...]), _b(b_ref[...]))

        @pl.when(s == 0)
        def _():
            acc_ref[...] = part

        @pl.when(s > 0)
        def _():
            acc_ref[...] += part

        @pl.when(s == ns - 1)
        def _():
            o_ref[...] = acc_ref[...]

    return pl.pallas_call(
        body, name=name, grid=(Ka // tka, N // tn, ns),
        in_specs=[pl.BlockSpec((ts, tka), lambda i, j, s: (s, i)),
                  pl.BlockSpec((ts, tn), lambda i, j, s: (s, j))],
        out_specs=pl.BlockSpec((tka, tn), lambda i, j, s: (i, j)),
        out_shape=jax.ShapeDtypeStruct((Ka, N), F32),
        scratch_shapes=[pltpu.VMEM((tka, tn), F32)],
        compiler_params=_params(("parallel", "parallel", "arbitrary")),
    )(a, b)


def _row_call(body, row_ins, full_ins, row_outs, acc_outs, bs, name):
    S = row_ins[0].shape[0]
    assert S % bs == 0
    in_specs = [pl.BlockSpec((bs, a.shape[1]), lambda i: (i, 0)) for a in row_ins]
    in_specs += [pl.BlockSpec(a.shape, lambda i: (0, 0)) for a in full_ins]
    out_specs = [pl.BlockSpec((bs, c), lambda i: (i, 0)) for c, _ in row_outs]
    out_specs += [pl.BlockSpec(s, lambda i: (0, 0)) for s in acc_outs]
    out_shape = [jax.ShapeDtypeStruct((S, c), dt) for c, dt in row_outs]
    out_shape += [jax.ShapeDtypeStruct(s, F32) for s in acc_outs]
    return pl.pallas_call(
        body, name=name, grid=(S // bs,), in_specs=in_specs, out_specs=out_specs, out_shape=out_shape,
        compiler_params=_params(("arbitrary",)),
    )(*row_ins, *full_ins)


def _rms_vals(x, w):
    r = lax.rsqrt(jnp.mean(x * x, axis=-1, keepdims=True) + RMS_EPS)
    return x * r * w


def _rms_bwd_vals(x, w, dy):
    r = lax.rsqrt(jnp.mean(x * x, axis=-1, keepdims=True) + RMS_EPS)
    xn = x * r
    g = dy * w
    dx = r * (g - xn * jnp.mean(g * xn, axis=-1, keepdims=True))
    dw = jnp.sum(dy * xn, axis=0, keepdims=True)
    return dx, dw


def _acc_add(ref, val):
    @pl.when(pl.program_id(0) == 0)
    def _():
        ref[...] = val

    @pl.when(pl.program_id(0) > 0)
    def _():
        ref[...] += val


def _rms_fwd(x, w):
    def body(x_ref, w_ref, o_ref):
        o_ref[...] = _b(_rms_vals(x_ref[...], w_ref[...]))
    return _row_call(body, [x], [w], [(x.shape[1], BF16)], [], 512, "rms_fwd")[0]


def _gate_fwd(att_o, ssm_o, gl, b_gate):
    def body(a_ref, s_ref, g_ref, b_ref, o_ref):
        g = _sigmoid(g_ref[...] + b_ref[...])
        o_ref[...] = _b(g[:, :D_MODEL] * a_ref[...] + g[:, D_MODEL:] * s_ref[...])
    return _row_call(body, [att_o, ssm_o, gl], [b_gate], [(D_MODEL, BF16)], [], 512, "gate_fwd")[0]


def _post_pre(x, mixed, w_post, w_pre):
    def body(x_ref, m_ref, wp_ref, wn_ref, h_ref, f_ref):
        h = x_ref[...] + _rms_vals(m_ref[...], wp_ref[...])
        h_ref[...] = h
        f_ref[...] = _b(_rms_vals(h, wn_ref[...]))
    return _row_call(body, [x, mixed], [w_post, w_pre], [(D_MODEL, F32), (D_MODEL, BF16)], [], 512,
                     "post_pre")


def _relu2(up):
    def body(u_ref, o_ref):
        r = jnp.maximum(u_ref[...], 0.0)
        o_ref[...] = _b(r * r)
    return _row_call(body, [up], [], [(up.shape[1], BF16)], [], 256, "relu2")[0]


def _final(h1, down, w_post, target):
    def body(h_ref, d_ref, t_ref, w_ref, dh_ref, dd_ref, loss_ref, dw_ref):
        dn = d_ref[...]
        w = w_ref[...]
        err = h_ref[...] + _rms_vals(dn, w) - t_ref[...]
        row = jnp.mean(err * err, axis=-1, keepdims=True)
        part = 0.5 * jnp.sum(row, axis=0, keepdims=True)
        dh = err * (1.0 / D_MODEL)
        dh_ref[...] = dh
        dx, dw = _rms_bwd_vals(dn, w, dh)
        dd_ref[...] = _b(dx)
        _acc_add(loss_ref, jnp.broadcast_to(part, (1, LANES)))
        _acc_add(dw_ref, dw)
    return _row_call(body, [h1, down, target], [w_post], [(D_MODEL, F32), (D_MODEL, BF16)],
                     [(1, LANES), (1, D_MODEL)], 512, "final_loss")


def _dup(da, up):
    def body(a_ref, u_ref, o_ref):
        o_ref[...] = _b(a_ref[...] * (2.0 * jnp.maximum(u_ref[...], 0.0)))
    return _row_call(body, [da, up], [], [(up.shape[1], BF16)], [], 256, "relu2_bwd")[0]


def _mid_bwd(dh2, df, h1, mixed, w_pre, w_post):
    def body(dh_ref, df_ref, h_ref, m_ref, wn_ref, wp_ref, dh1_ref, dm_ref, dwn_ref, dwp_ref):
        dx, dwn = _rms_bwd_vals(h_ref[...], wn_ref[...], df_ref[...])
        dh1 = dh_ref[...] + dx
        dh1_ref[...] = dh1
        dm, dwp = _rms_bwd_vals(m_ref[...], wp_ref[...], dh1)
        dm_ref[...] = _b(dm)
        _acc_add(dwn_ref, dwn)
        _acc_add(dwp_ref, dwp)
    return _row_call(body, [dh2, df, h1, mixed], [w_pre, w_post], [(D_MODEL, F32), (D_MODEL, BF16)],
                     [(1, D_MODEL), (1, D_MODEL)], 512, "mid_bwd")


def _gate_bwd(dmi, att_o, ssm_o, gl, b_gate):
    def body(d_ref, a_ref, s_ref, g_ref, b_ref, da_ref, ds_ref, dg_ref, db_ref):
        g = _sigmoid(g_ref[...] + b_ref[...])
        d = d_ref[...]
        ga, gs = g[:, :D_MODEL], g[:, D_MODEL:]
        da_ref[...] = _b(ga * d)
        ds_ref[...] = _b(gs * d)
        dga = d * a_ref[...] * ga * (1.0 - ga)
        dgs = d * s_ref[...] * gs * (1.0 - gs)
        dg_ref[:, :D_MODEL] = _b(dga)
        dg_ref[:, D_MODEL:] = _b(dgs)
        _acc_add(db_ref.at[:, pl.ds(0, D_MODEL)], jnp.sum(dga, axis=0, keepdims=True))
        _acc_add(db_ref.at[:, pl.ds(D_MODEL, D_MODEL)], jnp.sum(dgs, axis=0, keepdims=True))
    return _row_call(body, [dmi, att_o, ssm_o, gl], [b_gate],
                     [(D_MODEL, BF16), (D_MODEL, BF16), (2 * D_MODEL, BF16)], [(1, 2 * D_MODEL)], 256,
                     "gate_bwd")


def _first_bwd(dh1, du, x, w_pre):
    def body(dh_ref, du_ref, x_ref, w_ref, dx_ref, dw_ref):
        dx, dw = _rms_bwd_vals(x_ref[...], w_ref[...], du_ref[...])
        dx_ref[...] = dh_ref[...] + dx
        _acc_add(dw_ref, dw)
    return _row_call(body, [dh1, du, x], [w_pre], [(D_MODEL, F32)], [(1, D_MODEL)], 512, "first_bwd")


def _group_rms(t):
    gw = SSM_INNER // SSM_GROUPS
    out = []
    for g in range(SSM_GROUPS):
        tg = t[:, g * gw:(g + 1) * gw]
        out.append(lax.rsqrt(jnp.mean(tg * tg, axis=-1, keepdims=True) + RMS_EPS))
    return out


def _gnorm_fwd(y, z, w):
    gw = SSM_INNER // SSM_GROUPS

    def body(y_ref, z_ref, w_ref, o_ref):
        zz = z_ref[...]
        t = y_ref[...] * (zz * _sigmoid(zz))
        rs = _group_rms(t)
        for g in range(SSM_GROUPS):
            sl = slice(g * gw, (g + 1) * gw)
            o_ref[:, sl] = _b(t[:, sl] * rs[g] * w_ref[:, sl])
    return _row_call(body, [y, z], [w], [(SSM_INNER, BF16)], [], 256, "gnorm_fwd")[0]


def _gnorm_bwd(dout, y, z, w):
    gw = SSM_INNER // SSM_GROUPS

    def body(d_ref, y_ref, z_ref, w_ref, dy_ref, dz_ref, dw_ref):
        zz = z_ref[...]
        yy = y_ref[...]
        sg = _sigmoid(zz)
        sz = zz * sg
        t = yy * sz
        rs = _group_rms(t)
        for g in range(SSM_GROUPS):
            sl = slice(g * gw, (g + 1) * gw)
            tn = t[:, sl] * rs[g]
            d = d_ref[:, sl]
            gg = d * w_ref[:, sl]
            dt = rs[g] * (gg - tn * jnp.mean(gg * tn, axis=-1, keepdims=True))
            dy_ref[:, sl] = dt * sz[:, sl]
            dz_ref[:, sl] = _b(dt * yy[:, sl] * (sg[:, sl] * (1.0 + zz[:, sl] * (1.0 - sg[:, sl]))))
            _acc_add(dw_ref.at[:, pl.ds(g * gw, gw)], jnp.sum(d * tn, axis=0, keepdims=True))
    return _row_call(body, [dout, y, z], [w], [(SSM_INNER, F32), (SSM_INNER, BF16)], [(1, SSM_INNER)], 256,
                     "gnorm_bwd")


def _to_pat(a, d):
    if d == 1:
        return a
    S, C = a.shape
    return a.reshape(S // d, d, C).transpose(1, 0, 2).reshape(S, C)


def _from_pat(a, d):
    if d == 1:
        return a
    S, C = a.shape
    return a.reshape(d, S // d, C).transpose(1, 0, 2).reshape(S, C)


def _head_col(stat, h):
    return stat[:, h:h + 1]


def _attn_fwd(q, k, v, d):
    S = q.shape[0]
    blk = ATT_BLOCK
    nblk = S // blk
    nbs = nblk // d
    slopes = _alibi_slopes(N_ATT_HEADS)
    scale = HEAD_DIM ** -0.5

    def body(q_ref, kc_ref, kp_ref, vc_ref, vp_ref, o_ref, m_ref, l_ref):
        n = pl.program_id(0)
        has_prev = (n % nbs) != 0
        ii = lax.broadcasted_iota(jnp.int32, (blk, blk), 0)
        jj = lax.broadcasted_iota(jnp.int32, (blk, blk), 1)
        dist_c = (ii - jj).astype(F32)
        dist_p = dist_c + float(blk)
        ok_c = ii >= jj
        ok_p = jnp.logical_and(jj >= ii, has_prev)
        lane = lax.broadcasted_iota(jnp.int32, (blk, LANES), 1)
        m_all = jnp.zeros((blk, LANES), F32)
        l_all = jnp.zeros((blk, LANES), F32)
        for h in range(N_ATT_HEADS):
            sl = slice(h * HEAD_DIM, (h + 1) * HEAD_DIM)
            qh = q_ref[:, sl]
            bias = slopes[h] * float(d)
            sc = jnp.where(ok_c, _dot_nt(qh, kc_ref[:, sl]) * scale - bias * dist_c, NEG_BIG)
            sp = jnp.where(ok_p, _dot_nt(qh, kp_ref[:, sl]) * scale - bias * dist_p, NEG_BIG)
            m = jnp.maximum(jnp.max(sc, axis=-1, keepdims=True), jnp.max(sp, axis=-1, keepdims=True))
            pc = jnp.exp(sc - m)
            pp = jnp.exp(sp - m)
            l = jnp.sum(pc, axis=-1, keepdims=True) + jnp.sum(pp, axis=-1, keepdims=True)
            o_ref[:, sl] = _dot(_b(pc), vc_ref[:, sl]) + _dot(_b(pp), vp_ref[:, sl])
            m_all = jnp.where(lane == h, m, m_all)
            l_all = jnp.where(lane == h, l, l_all)
        m_ref[...] = m_all
        l_ref[...] = l_all

    cur = pl.BlockSpec((blk, ATT_WIDTH), lambda n: (n, 0))
    prev = pl.BlockSpec((blk, ATT_WIDTH), lambda n: (jnp.maximum(n - 1, 0), 0))
    stat = pl.BlockSpec((blk, LANES), lambda n: (n, 0))
    return pl.pallas_call(
        body, name=f"attn_fwd_d{d}", grid=(nblk,),
        in_specs=[cur, cur, prev, cur, prev],
        out_specs=[cur, stat, stat],
        out_shape=[jax.ShapeDtypeStruct((S, ATT_WIDTH), F32), jax.ShapeDtypeStruct((S, LANES), F32),
                   jax.ShapeDtypeStruct((S, LANES), F32)],
        compiler_params=_params(("parallel",)),
    )(q, k, k, v, v)


def _attn_combine(os, ms, ls):
    def body(o1, o2, o3, m1, m2, m3, l1, l2, l3, att_ref, lse_ref):
        mm = [m1[...], m2[...], m3[...]]
        big = jnp.maximum(jnp.maximum(mm[0], mm[1]), mm[2])
        es = [jnp.exp(m - big) for m in mm]
        den = es[0] * l1[...] + es[1] * l2[...] + es[2] * l3[...]
        lse_ref[...] = big + jnp.log(den)
        inv = 1.0 / den
        for h in range(N_ATT_HEADS):
            sl = slice(h * HEAD_DIM, (h + 1) * HEAD_DIM)
            num = (_head_col(es[0], h) * o1[:, sl] + _head_col(es[1], h) * o2[:, sl]
                   + _head_col(es[2], h) * o3[:, sl])
            att_ref[:, sl] = num * _head_col(inv, h)
    return _row_call(body, list(os) + list(ms) + list(ls), [], [(ATT_WIDTH, F32), (LANES, F32)], [], 256,
                     "attn_combine")


def _attn_delta(d_att, att):
    def body(d_ref, a_ref, dl_ref, db_ref):
        dd = d_ref[...]
        prod = dd * a_ref[...]
        lane = lax.broadcasted_iota(jnp.int32, (dd.shape[0], LANES), 1)
        acc = jnp.zeros((dd.shape[0], LANES), F32)
        for h in range(N_ATT_HEADS):
            s = jnp.sum(prod[:, h * HEAD_DIM:(h + 1) * HEAD_DIM], axis=-1, keepdims=True)
            acc = jnp.where(lane == h, s, acc)
        dl_ref[...] = acc
        db_ref[...] = _b(dd)
    return _row_call(body, [d_att, att], [], [(LANES, F32), (ATT_WIDTH, BF16)], [], 512, "attn_delta")


def _attn_bwd(q, k, v, do, lse, delta, d):
    S = q.shape[0]
    blk = ATT_BLOCK
    nblk = S // blk
    nbs = nblk // d
    slopes = _alibi_slopes(N_ATT_HEADS)
    scale = HEAD_DIM ** -0.5

    def body(qc_ref, qn_ref, k_ref, v_ref, doc_ref, don_ref, lc_ref, ln_ref, dc_ref, dn_ref,
             dq_ref, dk_ref, dv_ref, carry_ref):
        n = pl.program_id(0)
        has_next = ((n + 1) % nbs) != 0

        @pl.when(n == 0)
        def _():
            carry_ref[...] = jnp.zeros_like(carry_ref)

        ii = lax.broadcasted_iota(jnp.int32, (blk, blk), 0)
        jj = lax.broadcasted_iota(jnp.int32, (blk, blk), 1)
        dist_c = (ii - jj).astype(F32)
        dist_p = dist_c + float(blk)
        ok_c = ii >= jj
        ok_p = jnp.logical_and(jj >= ii, has_next)
        for h in range(N_ATT_HEADS):
            sl = slice(h * HEAD_DIM, (h + 1) * HEAD_DIM)
            bias = slopes[h] * float(d)
            kh = k_ref[:, sl]
            vh = v_ref[:, sl]
            qh = qc_ref[:, sl]
            doh = doc_ref[:, sl]
            s = jnp.where(ok_c, _dot_nt(qh, kh) * scale - bias * dist_c - _head_col(lc_ref[...], h), NEG_BIG)
            p = jnp.exp(s)
            ds = p * (_dot_nt(doh, vh) - _head_col(dc_ref[...], h)) * scale
            pb, dsb = _b(p), _b(ds)
            dv = _dot_tn(pb, doh)
            dk = _dot_tn(dsb, qh)
            dq_ref[:, sl] = _dot(dsb, kh) + carry_ref[:, sl]
            qh = qn_ref[:, sl]
            doh = don_ref[:, sl]
            s = jnp.where(ok_p, _dot_nt(qh, kh) * scale - bias * dist_p - _head_col(ln_ref[...], h), NEG_BIG)
            p = jnp.exp(s)
            ds = p * (_dot_nt(doh, vh) - _head_col(dn_ref[...], h)) * scale
            pb, dsb = _b(p), _b(ds)
            dv_ref[:, sl] = dv + _dot_tn(pb, doh)
            dk_ref[:, sl] = dk + _dot_tn(dsb, qh)
            carry_ref[:, sl] = _dot(dsb, kh)

    cur = pl.BlockSpec((blk, ATT_WIDTH), lambda n: (n, 0))
    nxt = pl.BlockSpec((blk, ATT_WIDTH), lambda n: (jnp.minimum(n + 1, nblk - 1), 0))
    scur = pl.BlockSpec((blk, LANES), lambda n: (n, 0))
    snxt = pl.BlockSpec((blk, LANES), lambda n: (jnp.minimum(n + 1, nblk - 1), 0))
    shp = jax.ShapeDtypeStruct((S, ATT_WIDTH), F32)
    return pl.pallas_call(
        body, name=f"attn_bwd_d{d}", grid=(nblk,),
        in_specs=[cur, nxt, cur, cur, cur, nxt, scur, snxt, scur, snxt],
        out_specs=[cur, cur, cur],
        out_shape=[shp, shp, shp],
        scratch_shapes=[pltpu.VMEM((blk, ATT_WIDTH), F32)],
        compiler_params=_params(("arbitrary",)),
    )(q, q, k, v, do, do, lse, lse, delta, delta)


def _head_pair_masks(x):
    lane = lax.broadcasted_iota(jnp.int32, x.shape, 1)
    zero = jnp.zeros_like(x)
    return jnp.where(lane < HEAD_DIM, x, zero), jnp.where(lane >= HEAD_DIM, x, zero)


def _attn_fwd2(qkv, d):
    S = qkv.shape[0]
    blk = ATT_BLOCK
    nblk = S // blk
    nbs = nblk // d
    slopes = _alibi_slopes(N_ATT_HEADS)
    scale = HEAD_DIM ** -0.5

    def body(q_ref, kc_ref, kp_ref, vc_ref, vp_ref, o_ref, m_ref, l_ref):
        n = pl.program_id(0)
        has_prev = (n % nbs) != 0
        ii = lax.broadcasted_iota(jnp.int32, (blk, 2 * blk), 0)
        jj = lax.broadcasted_iota(jnp.int32, (blk, 2 * blk), 1)
        dist_i = blk + ii - jj
        dist = dist_i.astype(F32)
        ok = jnp.logical_and(jnp.logical_and(dist_i >= 0, dist_i <= blk), jnp.logical_or(jj >= blk, has_prev))
        lane = lax.broadcasted_iota(jnp.int32, (blk, LANES), 1)
        m_all = jnp.zeros((blk, LANES), F32)
        l_all = jnp.zeros((blk, LANES), F32)
        for pr in range(N_ATT_HEADS // 2):
            sl = slice(pr * LANES, (pr + 1) * LANES)
            kcat = jnp.concatenate([kp_ref[:, sl], kc_ref[:, sl]], axis=0)
            vcat = jnp.concatenate([vp_ref[:, sl], vc_ref[:, sl]], axis=0)
            ps = []
            for h, qh in zip((2 * pr, 2 * pr + 1), _head_pair_masks(q_ref[:, sl])):
                s = jnp.where(ok, _dot_nt(qh, kcat) * scale - (slopes[h] * float(d)) * dist, NEG_BIG)
                m = jnp.max(s, axis=-1, keepdims=True)
                p = jnp.exp(s - m)
                l = jnp.sum(p, axis=-1, keepdims=True)
                m_all = jnp.where(lane == h, m, m_all)
                l_all = jnp.where(lane == h, l, l_all)
                ps.append(_b(p))
            o_ref[:, sl] = _dot(jnp.concatenate(ps, axis=1), jnp.concatenate(_head_pair_masks(vcat), axis=0))
        m_ref[...] = m_all
        l_ref[...] = l_all

    cur = lambda c: pl.BlockSpec((blk, ATT_WIDTH), lambda n: (n, c))
    prev = lambda c: pl.BlockSpec((blk, ATT_WIDTH), lambda n: (jnp.maximum(n - 1, 0), c))
    stat = pl.BlockSpec((blk, LANES), lambda n: (n, 0))
    return pl.pallas_call(
        body, name=f"attn_fwd_d{d}", grid=(nblk,),
        in_specs=[cur(0), cur(1), prev(1), cur(2), prev(2)],
        out_specs=[cur(0), stat, stat],
        out_shape=[jax.ShapeDtypeStruct((S, ATT_WIDTH), F32), jax.ShapeDtypeStruct((S, LANES), F32),
                   jax.ShapeDtypeStruct((S, LANES), F32)],
        compiler_params=_params(("parallel",)),
    )(qkv, qkv, qkv, qkv, qkv)


def _attn_bwd2(qkv, do, lse, delta, d):
    S = qkv.shape[0]
    blk = ATT_BLOCK
    nblk = S // blk
    nbs = nblk // d
    slopes = _alibi_slopes(N_ATT_HEADS)
    scale = HEAD_DIM ** -0.5

    def body(qc_ref, qn_ref, k_ref, v_ref, doc_ref, don_ref, lc_ref, ln_ref, dc_ref, dn_ref,
             dq_ref, dk_ref, dv_ref, carry_ref):
        n = pl.program_id(0)
        has_next = ((n + 1) % nbs) != 0

        @pl.when(n == 0)
        def _():
            carry_ref[...] = jnp.zeros_like(carry_ref)

        rr = lax.broadcasted_iota(jnp.int32, (2 * blk, blk), 0)
        jj = lax.broadcasted_iota(jnp.int32, (2 * blk, blk), 1)
        dist_i = rr - jj
        dist = dist_i.astype(F32)
        ok = jnp.logical_or(jnp.logical_and(rr < blk, dist_i >= 0),
                            jnp.logical_and(jnp.logical_and(rr >= blk, dist_i <= blk), has_next))
        lcat = jnp.concatenate([lc_ref[...], ln_ref[...]], axis=0)
        dcat = jnp.concatenate([dc_ref[...], dn_ref[...]], axis=0)
        for pr in range(N_ATT_HEADS // 2):
            sl = slice(pr * LANES, (pr + 1) * LANES)
            qcat = jnp.concatenate([qc_ref[:, sl], qn_ref[:, sl]], axis=0)
            docat = jnp.concatenate([doc_ref[:, sl], don_ref[:, sl]], axis=0)
            k2 = k_ref[:, sl]
            v2 = v_ref[:, sl]
            qm = _head_pair_masks(qcat)
            dom = _head_pair_masks(docat)
            pbs, dsbs = [], []
            for h, qh, doh in zip((2 * pr, 2 * pr + 1), qm, dom):
                s = jnp.where(ok, _dot_nt(qh, k2) * scale - (slopes[h] * float(d)) * dist - lcat[:, h:h + 1],
                              NEG_BIG)
                p = jnp.exp(s)
                ds = p * (_dot_nt(doh, v2) - dcat[:, h:h + 1]) * scale
                pbs.append(_b(p))
                dsbs.append(_b(ds))
            dv_ref[:, sl] = _dot_tn(jnp.concatenate(pbs, axis=0), jnp.concatenate(dom, axis=0))
            dk_ref[:, sl] = _dot_tn(jnp.concatenate(dsbs, axis=0), jnp.concatenate(qm, axis=0))
            dq = _dot(jnp.concatenate(dsbs, axis=1), jnp.concatenate(_head_pair_masks(k2), axis=0))
            dq_ref[:, sl] = dq[:blk] + carry_ref[:, sl]
            carry_ref[:, sl] = dq[blk:]

    cur = lambda c: pl.BlockSpec((blk, ATT_WIDTH), lambda n: (n, c))
    nxt = lambda c: pl.BlockSpec((blk, ATT_WIDTH), lambda n: (jnp.minimum(n + 1, nblk - 1), c))
    scur = pl.BlockSpec((blk, LANES), lambda n: (n, 0))
    snxt = pl.BlockSpec((blk, LANES), lambda n: (jnp.minimum(n + 1, nblk - 1), 0))
    shp = jax.ShapeDtypeStruct((S, ATT_WIDTH), F32)
    return pl.pallas_call(
        body, name=f"attn_bwd_d{d}", grid=(nblk,),
        in_specs=[cur(0), nxt(0), cur(1), cur(2), cur(0), nxt(0), scur, snxt, scur, snxt],
        out_specs=[cur(0), cur(0), cur(0)],
        out_shape=[shp, shp, shp],
        scratch_shapes=[pltpu.VMEM((blk, ATT_WIDTH), F32)],
        compiler_params=_params(("arbitrary",)),
    )(qkv, qkv, qkv, qkv, do, do, lse, lse, delta, delta)


LAYOUT_TILE = 512
DILATED = tuple(d for d in DILATIONS if d > 1)


def _pat_spec(d, cols, col_block=0):
    return pl.BlockSpec((d, LAYOUT_TILE // d, cols), lambda i: (0, i, col_block))


def _pat_view(a, d):
    return a.reshape(d, a.shape[0] // d, a.shape[1])


def _qkv_layouts(qkv):
    S, C = qkv.shape
    t = LAYOUT_TILE

    def body(x_ref, nat_ref, *refs):
        pat_refs, slab = refs[:-1], refs[-1]
        nat_ref[...] = _b(x_ref[...])
        _to_slabs(slab, x_ref)
        for d, p_ref in zip(DILATED, pat_refs):
            _gather_pattern(p_ref, slab, d, BF16)

    outs = pl.pallas_call(
        body, name="qkv_layouts", grid=(S // t,),
        in_specs=[pl.BlockSpec((t, C), lambda i: (i, 0))],
        out_specs=[pl.BlockSpec((t, C), lambda i: (i, 0))] + [_pat_spec(d, C) for d in DILATED],
        out_shape=[jax.ShapeDtypeStruct((S, C), BF16)]
        + [jax.ShapeDtypeStruct((d, S // d, C), BF16) for d in DILATED],
        scratch_shapes=[pltpu.VMEM((C // LANES, t, LANES), F32)],
        compiler_params=_params(("parallel",)),
    )(qkv)
    return [outs[0]] + [o.reshape(S, C) for o in outs[1:]]


def _to_slabs(slab_ref, src_ref):
    for cb in range(slab_ref.shape[0]):
        slab_ref[cb] = src_ref[:, cb * LANES:(cb + 1) * LANES].astype(F32)


def _gather_pattern(dst_ref, slab_ref, d, dtype):
    t = slab_ref.shape[1]
    for cb in range(slab_ref.shape[0]):
        one = slab_ref.at[cb]
        for r in range(d):
            dst_ref[r, :, cb * LANES:(cb + 1) * LANES] = one[pl.ds(r, t // d, stride=d), :].astype(dtype)


def _scatter_pattern(slab_ref, src_ref, d, add=False):
    t = slab_ref.shape[1]
    for cb in range(slab_ref.shape[0]):
        one = slab_ref.at[cb]
        for r in range(d):
            idx = pl.ds(r, t // d, stride=d)
            val = src_ref[r, :, cb * LANES:(cb + 1) * LANES]
            if add:
                val = val + one[idx, :]
            one[idx, :] = val


def _attn_combine2(os, ms, ls):
    S = os[0].shape[0]
    t = LAYOUT_TILE

    def body(o1, o2, o3, m1, m2, m3, l1, l2, l3, att_ref, lse_ref, so2, so3, sm2, sm3, sl2, sl3):
        for d, src, dst in ((DILATED[0], o2, so2), (DILATED[1], o3, so3), (DILATED[0], m2, sm2),
                            (DILATED[1], m3, sm3), (DILATED[0], l2, sl2), (DILATED[1], l3, sl3)):
            _scatter_pattern(dst, src, d)
        mm = [m1[...], sm2[0], sm3[0]]
        big = jnp.maximum(jnp.maximum(mm[0], mm[1]), mm[2])
        es = [jnp.exp(m - big) for m in mm]
        den = es[0] * l1[...] + es[1] * sl2[0] + es[2] * sl3[0]
        lse_ref[...] = big + jnp.log(den)
        inv = 1.0 / den
        for h in range(N_ATT_HEADS):
            sl = slice(h * HEAD_DIM, (h + 1) * HEAD_DIM)
            cb, hl = divmod(h, 2)
            sll = slice(hl * HEAD_DIM, (hl + 1) * HEAD_DIM)
            num = (_head_col(es[0], h) * o1[:, sl] + _head_col(es[1], h) * so2[cb, :, sll]
                   + _head_col(es[2], h) * so3[cb, :, sll])
            att_ref[:, sl] = num * _head_col(inv, h)

    def specs(c):
        return [pl.BlockSpec((t, c), lambda i: (i, 0))] + [_pat_spec(d, c) for d in DILATED]

    args = [os[0]] + [_pat_view(o, d) for o, d in zip(os[1:], DILATED)]
    args += [ms[0]] + [_pat_view(m, d) for m, d in zip(ms[1:], DILATED)]
    args += [ls[0]] + [_pat_view(l, d) for l, d in zip(ls[1:], DILATED)]
    return pl.pallas_call(
        body, name="attn_combine", grid=(S // t,),
        in_specs=specs(ATT_WIDTH) + specs(LANES) + specs(LANES),
        out_specs=[pl.BlockSpec((t, ATT_WIDTH), lambda i: (i, 0)), pl.BlockSpec((t, LANES), lambda i: (i, 0))],
        out_shape=[jax.ShapeDtypeStruct((S, ATT_WIDTH), F32), jax.ShapeDtypeStruct((S, LANES), F32)],
        scratch_shapes=[pltpu.VMEM((ATT_WIDTH // LANES, t, LANES), F32)] * 2
        + [pltpu.VMEM((1, t, LANES), F32)] * 4,
        compiler_params=_params(("parallel",)),
    )(*args)


def _attn_delta2(d_att, att, lse):
    S = d_att.shape[0]
    t = LAYOUT_TILE

    def body(d_ref, a_ref, l_ref, *refs):
        out_refs, d_slab, l_slab, dl_slab = refs[:-3], refs[-3], refs[-2], refs[-1]
        dd = d_ref[...]
        prod = dd * a_ref[...]
        lane = lax.broadcasted_iota(jnp.int32, (t, LANES), 1)
        acc = jnp.zeros((t, LANES), F32)
        for h in range(N_ATT_HEADS):
            s = jnp.sum(prod[:, h * HEAD_DIM:(h + 1) * HEAD_DIM], axis=-1, keepdims=True)
            acc = jnp.where(lane == h, s, acc)
        out_refs[0][...] = _b(dd)
        out_refs[1][...] = acc
        _to_slabs(d_slab, d_ref)
        l_slab[0] = l_ref[...]
        dl_slab[0] = acc
        for k, d in enumerate(DILATED):
            db_ref, ls_ref, dl_ref = out_refs[2 + 3 * k:5 + 3 * k]
            _gather_pattern(db_ref, d_slab, d, BF16)
            _gather_pattern(ls_ref, l_slab, d, F32)
            _gather_pattern(dl_ref, dl_slab, d, F32)

    nat = lambda c: pl.BlockSpec((t, c), lambda i: (i, 0))
    out_specs = [nat(ATT_WIDTH), nat(LANES)]
    out_shape = [jax.ShapeDtypeStruct((S, ATT_WIDTH), BF16), jax.ShapeDtypeStruct((S, LANES), F32)]
    for d in DILATED:
        out_specs += [_pat_spec(d, ATT_WIDTH), _pat_spec(d, LANES), _pat_spec(d, LANES)]
        out_shape += [jax.ShapeDtypeStruct((d, S // d, ATT_WIDTH), BF16),
                      jax.ShapeDtypeStruct((d, S // d, LANES), F32),
                      jax.ShapeDtypeStruct((d, S // d, LANES), F32)]
    outs = pl.pallas_call(
        body, name="attn_delta", grid=(S // t,),
        in_specs=[nat(ATT_WIDTH), nat(ATT_WIDTH), nat(LANES)],
        out_specs=out_specs, out_shape=out_shape,
        scratch_shapes=[pltpu.VMEM((ATT_WIDTH // LANES, t, LANES), F32), pltpu.VMEM((1, t, LANES), F32),
                        pltpu.VMEM((1, t, LANES), F32)],
        compiler_params=_params(("parallel",)),
    )(d_att, att, lse)
    res = [(outs[0], lse, outs[1])]
    for k in range(len(DILATED)):
        db, ls, dl = outs[2 + 3 * k:5 + 3 * k]
        res.append((db.reshape(S, ATT_WIDTH), ls.reshape(S, LANES), dl.reshape(S, LANES)))
    return res


def _sum_qkv2(dqs, dks, dvs):
    S = dqs[0].shape[0]
    t = LAYOUT_TILE

    def body(*refs):
        o_ref, scr = refs[-2], refs[-1]
        for part in range(3):
            nat_ref, p_refs = refs[3 * part], refs[3 * part + 1:3 * part + 3]
            _to_slabs(scr, nat_ref)
            for d, p_ref in zip(DILATED, p_refs):
                _scatter_pattern(scr, p_ref, d, add=True)
            for cb in range(ATT_WIDTH // LANES):
                o_ref[:, part * ATT_WIDTH + cb * LANES:part * ATT_WIDTH + (cb + 1) * LANES] = _b(scr[cb])

    in_specs, args = [], []
    for group in (dqs, dks, dvs):
        in_specs += [pl.BlockSpec((t, ATT_WIDTH), lambda i: (i, 0))] + [_pat_spec(d, ATT_WIDTH) for d in DILATED]
        args += [group[0]] + [_pat_view(a, d) for a, d in zip(group[1:], DILATED)]
    return pl.pallas_call(
        body, name="sum_dqkv", grid=(S // t,),
        in_specs=in_specs,
        out_specs=pl.BlockSpec((t, 3 * ATT_WIDTH), lambda i: (i, 0)),
        out_shape=jax.ShapeDtypeStruct((S, 3 * ATT_WIDTH), BF16),
        scratch_shapes=[pltpu.VMEM((ATT_WIDTH // LANES, t, LANES), F32)],
        compiler_params=_params(("parallel",)),
    )(*args)


def _sum_qkv(dqs, dks, dvs):
    def body(q1, q2, q3, k1, k2, k3, v1, v2, v3, o_ref):
        o_ref[:, 0:ATT_WIDTH] = _b(q1[...] + q2[...] + q3[...])
        o_ref[:, ATT_WIDTH:2 * ATT_WIDTH] = _b(k1[...] + k2[...] + k3[...])
        o_ref[:, 2 * ATT_WIDTH:] = _b(v1[...] + v2[...] + v3[...])
    return _row_call(body, list(dqs) + list(dks) + list(dvs), [], [(3 * ATT_WIDTH, BF16)], [], 256,
                     "sum_dqkv")[0]


CONV_COLS = 1024
CONV_ROWS = 512
HALO = 8


def _conv_fwd(xbc, conv_w, conv_b):
    S, C = xbc.shape
    bs, bc = CONV_ROWS, CONV_COLS
    nr = S // bs

    def body(x_ref, halo_ref, w_ref, b_ref, o_ref, xs_ref):
        r = pl.program_id(1)
        xs_ref[pl.ds(HALO, bs), :] = x_ref[...]
        xs_ref[pl.ds(0, HALO), :] = jnp.where(r > 0, halo_ref[...], 0.0)
        pre = b_ref[...] + w_ref[3:4, :] * x_ref[...]
        for j in range(SSM_CONV - 1):
            pre = pre + w_ref[j:j + 1, :] * xs_ref[pl.ds(HALO - 3 + j, bs), :]
        o_ref[...] = pre * _sigmoid(pre)

    return pl.pallas_call(
        body, name="conv_fwd", grid=(C // bc, nr),
        in_specs=[pl.BlockSpec((bs, bc), lambda c, r: (r, c)),
                  pl.BlockSpec((HALO, bc), lambda c, r: (jnp.maximum(r * (bs // HALO) - 1, 0), c)),
                  pl.BlockSpec((SSM_CONV, bc), lambda c, r: (0, c)),
                  pl.BlockSpec((1, bc), lambda c, r: (0, c))],
        out_specs=pl.BlockSpec((bs, bc), lambda c, r: (r, c)),
        out_shape=jax.ShapeDtypeStruct((S, C), F32),
        scratch_shapes=[pltpu.VMEM((bs + HALO, bc), F32)],
        compiler_params=_params(("parallel", "arbitrary")),
    )(xbc, xbc, conv_w, conv_b)


def _conv_bwd(xbc, dact, conv_w, conv_b, col0):
    S, C = xbc.shape
    Cp = dact.shape[1]
    bs, bc = CONV_ROWS, min(CONV_COLS, Cp)
    nr = S // bs
    cb0 = col0 // bc
    last_halo = S // HALO - 1

    def body(x_ref, xp_ref, xn_ref, d_ref, dn_ref, w_ref, b_ref, dx_ref, dw_ref, db_ref,
             xs_ref, dp_ref):
        r = pl.program_id(1)
        xs_ref[pl.ds(0, HALO), :] = jnp.where(r > 0, xp_ref[...], 0.0)
        xs_ref[pl.ds(HALO, bs), :] = x_ref[...]
        xs_ref[pl.ds(HALO + bs, HALO), :] = xn_ref[...]
        ext = bs + HALO
        pre = b_ref[...] + jnp.zeros((ext, bc), F32)
        for j in range(SSM_CONV):
            pre = pre + w_ref[j:j + 1, :] * xs_ref[pl.ds(HALO - 3 + j, ext), :]
        sg = _sigmoid(pre)
        dsilu = sg * (1.0 + pre * (1.0 - sg))
        dp_ref[pl.ds(0, bs), :] = d_ref[...] * dsilu[:bs]
        dp_ref[pl.ds(bs, HALO), :] = jnp.where(r < nr - 1, dn_ref[...], 0.0) * dsilu[bs:]
        dx = jnp.zeros((bs, bc), F32)
        for j in range(SSM_CONV):
            dx = dx + w_ref[j:j + 1, :] * dp_ref[pl.ds(3 - j, bs), :]
        dx_ref[...] = _b(dx)
        dpre = dp_ref[pl.ds(0, bs), :]
        for j in range(SSM_CONV):
            part = jnp.sum(dpre * xs_ref[pl.ds(HALO - 3 + j, bs), :], axis=0, keepdims=True)

            @pl.when(r == 0)
            def _():
                dw_ref[j:j + 1, :] = part

            @pl.when(r > 0)
            def _():
                dw_ref[j:j + 1, :] += part
        part = jnp.sum(dpre, axis=0, keepdims=True)

        @pl.when(r == 0)
        def _():
            db_ref[...] = part

        @pl.when(r > 0)
        def _():
            db_ref[...] += part

    hb = bs // HALO
    return pl.pallas_call(
        body, name=f"conv_bwd_{col0}", grid=(Cp // bc, nr),
        in_specs=[pl.BlockSpec((bs, bc), lambda c, r: (r, cb0 + c)),
                  pl.BlockSpec((HALO, bc), lambda c, r: (jnp.maximum(r * hb - 1, 0), cb0 + c)),
                  pl.BlockSpec((HALO, bc), lambda c, r: (jnp.minimum((r + 1) * hb, last_halo), cb0 + c)),
                  pl.BlockSpec((bs, bc), lambda c, r: (r, c)),
                  pl.BlockSpec((HALO, bc), lambda c, r: (jnp.minimum((r + 1) * hb, last_halo), c)),
                  pl.BlockSpec((SSM_CONV, bc), lambda c, r: (0, cb0 + c)),
                  pl.BlockSpec((1, bc), lambda c, r: (0, cb0 + c))],
        out_specs=[pl.BlockSpec((bs, bc), lambda c, r: (r, c)),
                   pl.BlockSpec((SSM_CONV, bc), lambda c, r: (0, c)),
                   pl.BlockSpec((1, bc), lambda c, r: (0, c))],
        out_shape=[jax.ShapeDtypeStruct((S, Cp), BF16), jax.ShapeDtypeStruct((SSM_CONV, Cp), F32),
                   jax.ShapeDtypeStruct((1, Cp), F32)],
        scratch_shapes=[pltpu.VMEM((bs + 2 * HALO, bc), F32), pltpu.VMEM((bs + HALO, bc), F32)],
        compiler_params=_params(("parallel", "arbitrary")),
    )(xbc, xbc, xbc, dact, dact, conv_w, conv_b)


def _shift_down(x, k, top_src):
    r8 = lax.broadcasted_iota(jnp.int32, (HALO, x.shape[1]), 0)
    rolled = pltpu.roll(x, k, 0)
    top = jnp.where(r8 < k, pltpu.roll(top_src, k, 0), rolled[0:HALO])
    if x.shape[0] == HALO:
        return top
    return jnp.concatenate([top, rolled[HALO:]], axis=0)


def _shift_up(x, k, bottom_src):
    n = x.shape[0]
    r8 = lax.broadcasted_iota(jnp.int32, (HALO, x.shape[1]), 0)
    rolled = pltpu.roll(x, n - k, 0)
    bottom = jnp.where(r8 >= HALO - k, pltpu.roll(bottom_src, HALO - k, 0), rolled[n - HALO:n])
    return jnp.concatenate([rolled[:n - HALO], bottom], axis=0)


def _conv_pre(x, top_src, w_ref, b_ref):
    shifted = [x] + [_shift_down(x, k, top_src) for k in range(1, SSM_CONV)]
    pre = b_ref[...] + w_ref[SSM_CONV - 1:SSM_CONV, :] * x
    for k in range(1, SSM_CONV):
        pre = pre + w_ref[SSM_CONV - 1 - k:SSM_CONV - k, :] * shifted[k]
    return pre, shifted


def _conv_fwd2(xbc, conv_w, conv_b):
    S, C = xbc.shape
    bs, bc = CONV_ROWS, CONV_COLS
    nr = S // bs

    def body(x_ref, halo_ref, w_ref, b_ref, o_ref):
        r = pl.program_id(1)
        halo = jnp.where(r > 0, halo_ref[...], 0.0)
        pre, _ = _conv_pre(x_ref[...], halo, w_ref, b_ref)
        o_ref[...] = pre * _sigmoid(pre)

    return pl.pallas_call(
        body, name="conv_fwd", grid=(C // bc, nr),
        in_specs=[pl.BlockSpec((bs, bc), lambda c, r: (r, c)),
                  pl.BlockSpec((HALO, bc), lambda c, r: (jnp.maximum(r * (bs // HALO) - 1, 0), c)),
                  pl.BlockSpec((SSM_CONV, bc), lambda c, r: (0, c)),
                  pl.BlockSpec((1, bc), lambda c, r: (0, c))],
        out_specs=pl.BlockSpec((bs, bc), lambda c, r: (r, c)),
        out_shape=jax.ShapeDtypeStruct((S, C), F32),
        compiler_params=_params(("parallel", "arbitrary")),
    )(xbc, xbc, conv_w, conv_b)


def _conv_bwd2(xbc, dact, conv_w, conv_b):
    S, C = xbc.shape
    bs, bc = CONV_ROWS, CONV_COLS
    nr = S // bs
    hb = bs // HALO
    last_halo = S // HALO - 1

    def dsilu(pre):
        sg = _sigmoid(pre)
        return sg * (1.0 + pre * (1.0 - sg))

    def body(x_ref, xp_ref, xn_ref, d_ref, dn_ref, w_ref, b_ref, dx_ref, dw_ref, db_ref):
        r = pl.program_id(1)
        x = x_ref[...]
        pre, shifted = _conv_pre(x, jnp.where(r > 0, xp_ref[...], 0.0), w_ref, b_ref)
        dpre = d_ref[...] * dsilu(pre)
        pre_n, _ = _conv_pre(xn_ref[...], x[bs - HALO:bs], w_ref, b_ref)
        dpre_n = jnp.where(r < nr - 1, dn_ref[...], 0.0) * dsilu(pre_n)
        dx = w_ref[SSM_CONV - 1:SSM_CONV, :] * dpre
        for k in range(1, SSM_CONV):
            dx = dx + w_ref[SSM_CONV - 1 - k:SSM_CONV - k, :] * _shift_up(dpre, k, dpre_n)
        dx_ref[...] = _b(dx)
        parts = [jnp.sum(dpre * shifted[SSM_CONV - 1 - j], axis=0, keepdims=True) for j in range(SSM_CONV)]
        dbp = jnp.sum(dpre, axis=0, keepdims=True)

        @pl.when(r == 0)
        def _():
            for j in range(SSM_CONV):
                dw_ref[j:j + 1, :] = parts[j]
            db_ref[...] = dbp

        @pl.when(r > 0)
        def _():
            for j in range(SSM_CONV):
                dw_ref[j:j + 1, :] += parts[j]
            db_ref[...] += dbp

    return pl.pallas_call(
        body, name="conv_bwd", grid=(C // bc, nr),
        in_specs=[pl.BlockSpec((bs, bc), lambda c, r: (r, c)),
                  pl.BlockSpec((HALO, bc), lambda c, r: (jnp.maximum(r * hb - 1, 0), c)),
                  pl.BlockSpec((HALO, bc), lambda c, r: (jnp.minimum((r + 1) * hb, last_halo), c)),
                  pl.BlockSpec((bs, bc), lambda c, r: (r, c)),
                  pl.BlockSpec((HALO, bc), lambda c, r: (jnp.minimum((r + 1) * hb, last_halo), c)),
                  pl.BlockSpec((SSM_CONV, bc), lambda c, r: (0, c)),
                  pl.BlockSpec((1, bc), lambda c, r: (0, c))],
        out_specs=[pl.BlockSpec((bs, bc), lambda c, r: (r, c)),
                   pl.BlockSpec((SSM_CONV, bc), lambda c, r: (0, c)),
                   pl.BlockSpec((1, bc), lambda c, r: (0, c))],
        out_shape=[jax.ShapeDtypeStruct((S, C), BF16), jax.ShapeDtypeStruct((SSM_CONV, C), F32),
                   jax.ShapeDtypeStruct((1, C), F32)],
        compiler_params=_params(("parallel", "arbitrary")),
    )(xbc, xbc, xbc, dact, dact, conv_w, conv_b)


def _softplus(x):
    return jnp.maximum(x, 0.0) + jnp.log(1.0 + jnp.exp(-jnp.abs(x)))


def _ssd_common(dtr_ref, bias_ref, a_ref, g):
    ch = SSM_CHUNK
    x = dtr_ref[...] + bias_ref[...]
    dt_all = _softplus(x)
    r = lax.broadcasted_iota(jnp.int32, (LANES, LANES), 0)
    c = lax.broadcasted_iota(jnp.int32, (LANES, LANES), 1)
    sel = jnp.where(jnp.logical_and(r == HEADS_PER_GROUP * g + c, c < HEADS_PER_GROUP), 1.0, 0.0)
    dt4 = _dot_hi(dt_all, sel)
    la4 = _dot_hi(dt_all * a_ref[...], sel)
    ii = lax.broadcasted_iota(jnp.int32, (ch, ch), 0)
    jj = lax.broadcasted_iota(jnp.int32, (ch, ch), 1)
    tril = jnp.where(ii >= jj, 1.0, 0.0)
    acs = _dot_hi(tril, la4)
    return x, sel, dt4, acs, acs.T, ii >= jj


def _row8(v):
    return jnp.broadcast_to(v, (8, v.shape[1]))


def _ssd_fwd(xact, dt_raw, dt_bias, a_neg, d_skip):
    S = xact.shape[0]
    ch = SSM_CHUNK
    nch = S // ch
    hg = HEADS_PER_GROUP
    gw = hg * SSM_HEAD_DIM
    b_off = SSM_INNER // SSM_STATE
    c_off = b_off + SSM_GROUPS

    def body(x_ref, b_ref, c_ref, dtr_ref, bias_ref, a_ref, dsk_ref, y_ref, hs_ref, h_ref):
        c = pl.program_id(0)
        g = pl.program_id(1)

        @pl.when(jnp.logical_and(c == 0, g == 0))
        def _():
            h_ref[...] = jnp.zeros_like(h_ref)

        _, sel, dt4, acs, acs_t, low = _ssd_common(dtr_ref, bias_ref, a_ref, g)
        dsk4 = _dot_hi(_row8(dsk_ref[...]), sel)
        bb = _b(b_ref[...])
        cc = _b(c_ref[...])
        cb = _dot_nt(cc, bb)
        for j in range(hg):
            sl = slice(j * SSM_HEAD_DIM, (j + 1) * SSM_HEAD_DIM)
            acol = acs[:, j:j + 1]
            arow = acs_t[j:j + 1, :]
            alast = acs[ch - 1:ch, j:j + 1]
            decay = jnp.exp(jnp.where(low, acol - arow, -jnp.inf))
            xh = x_ref[:, sl]
            xd = xh * dt4[:, j:j + 1]
            hj = h_ref[hg * g + j]
            y = _dot(_b(cb * decay), _b(xd))
            y = y + _dot_nt(cc, _b(hj)) * jnp.exp(acol)
            y_ref[:, sl] = y + dsk4[0:1, j:j + 1] * xh
            hs_ref[0, j] = hj
            st = _dot_tn(_b(xd * jnp.exp(alast - acol)), bb)
            h_ref[hg * g + j] = hj * jnp.exp(alast) + st

    small = pl.BlockSpec((1, LANES), lambda c, g: (0, 0))
    return pl.pallas_call(
        body, name="ssd_fwd", grid=(nch, SSM_GROUPS),
        in_specs=[pl.BlockSpec((ch, gw), lambda c, g: (c, g)),
                  pl.BlockSpec((ch, SSM_STATE), lambda c, g: (c, b_off + g)),
                  pl.BlockSpec((ch, SSM_STATE), lambda c, g: (c, c_off + g)),
                  pl.BlockSpec((ch, LANES), lambda c, g: (c, 0)),
                  small, small, small],
        out_specs=[pl.BlockSpec((ch, gw), lambda c, g: (c, g)),
                   pl.BlockSpec((1, hg, SSM_HEAD_DIM, SSM_STATE), lambda c, g: (c, g, 0, 0))],
        out_shape=[jax.ShapeDtypeStruct((S, SSM_INNER), F32),
                   jax.ShapeDtypeStruct((nch, SSM_HEADS, SSM_HEAD_DIM, SSM_STATE), F32)],
        scratch_shapes=[pltpu.VMEM((SSM_HEADS, SSM_HEAD_DIM, SSM_STATE), F32)],
        compiler_params=_params(("arbitrary", "arbitrary")),
    )(xact, xact, xact, dt_raw, dt_bias, a_neg, d_skip)


def _ssd_bwd(xact, dt_raw, dt_bias, a_neg, d_skip, hs, dy):
    S = xact.shape[0]
    ch = SSM_CHUNK
    nch = S // ch
    hg = HEADS_PER_GROUP
    gw = hg * SSM_HEAD_DIM
    b_off = SSM_INNER // SSM_STATE
    c_off = b_off + SSM_GROUPS

    def body(x_ref, b_ref, c_ref, dtr_ref, bias_ref, a_ref, dsk_ref, hs_ref, dy_ref,
             dx_ref, db_ref, dc_ref, ddt_ref, st_ref, dh_ref, ddt_acc):
        step = pl.program_id(0)
        g = pl.program_id(1)

        @pl.when(jnp.logical_and(step == 0, g == 0))
        def _():
            dh_ref[...] = jnp.zeros_like(dh_ref)
            st_ref[...] = jnp.zeros_like(st_ref)

        @pl.when(g == 0)
        def _():
            ddt_acc[...] = jnp.zeros_like(ddt_acc)

        xraw, sel, dt4, acs, acs_t, low = _ssd_common(dtr_ref, bias_ref, a_ref, g)
        a4 = _dot_hi(_row8(a_ref[...]), sel)[0:1, :]
        dsk4 = _dot_hi(_row8(dsk_ref[...]), sel)
        bf = b_ref[...]
        cf = c_ref[...]
        bb = _b(bf)
        cc = _b(cf)
        cb = _dot_nt(cc, bb)
        lane = lax.broadcasted_iota(jnp.int32, (ch, LANES), 1)
        rowi = lax.broadcasted_iota(jnp.int32, (ch, 1), 0)
        ones = jnp.ones((ch, LANES), F32)
        dcb = jnp.zeros((ch, ch), F32)
        dc_acc = jnp.zeros((ch, SSM_STATE), F32)
        db_acc = jnp.zeros((ch, SSM_STATE), F32)
        dacs4 = jnp.zeros((ch, LANES), F32)
        ddt4 = jnp.zeros((ch, LANES), F32)
        dd4 = jnp.zeros((1, LANES), F32)
        lane1 = lax.broadcasted_iota(jnp.int32, (1, LANES), 1)
        for j in range(hg):
            sl = slice(j * SSM_HEAD_DIM, (j + 1) * SSM_HEAD_DIM)
            acol = acs[:, j:j + 1]
            arow = acs_t[j:j + 1, :]
            alast = acs[ch - 1:ch, j:j + 1]
            decay = jnp.exp(jnp.where(low, acol - arow, -jnp.inf))
            ea = jnp.exp(acol)
            dsd = jnp.exp(alast - acol)
            cd = jnp.exp(alast)
            dtc = dt4[:, j:j + 1]
            xh = x_ref[:, sl]
            xd = xh * dtc
            xdb = _b(xd)
            hj = hs_ref[0, j]
            hjb = _b(hj)
            dhn = dh_ref[hg * g + j]
            dyj = dy_ref[:, sl]
            dyb = _b(dyj)
            lm = cb * decay
            dxh = dsk4[0:1, j:j + 1] * dyj
            dd4 = jnp.where(lane1 == j, jnp.sum(jnp.sum(dyj * xh, axis=1, keepdims=True), axis=0,
                                                keepdims=True), dd4)
            dlm = _dot_nt(dyb, xdb)
            dxd = _dot_tn(_b(lm), dyb)
            gm = dlm * lm
            dcb = dcb + dlm * decay
            dac = jnp.sum(gm, axis=1, keepdims=True) - _dot_tn_hi(gm, ones)[:, 0:1]
            zz = _dot_nt(cc, hjb)
            dzb = _b(dyj * ea)
            dac = dac + jnp.sum(dyj * zz, axis=1, keepdims=True) * ea
            dc_acc = dc_acc + _dot(dzb, hjb)
            dh_in = _dot_tn(dzb, cc)
            dsb = _b(dhn)
            ww = _dot_nt(bb, dsb)
            dxd = dxd + ww * dsd
            dds = jnp.sum(ww * xd, axis=1, keepdims=True) * dsd
            db_acc = db_acc + _dot(_b(xd * dsd), dsb)
            dac = dac - dds
            dal = (jnp.sum(dds, axis=0, keepdims=True)
                   + jnp.sum(jnp.sum(dhn * hj, axis=1, keepdims=True), axis=0, keepdims=True) * cd)
            dh_ref[hg * g + j] = dh_in + dhn * cd
            dac = dac + jnp.where(rowi == ch - 1, dal, 0.0)
            dacs4 = jnp.where(lane == j, dac, dacs4)
            dx_ref[:, sl] = dxh + dxd * dtc
            ddt4 = jnp.where(lane == j, jnp.sum(dxd * xh, axis=1, keepdims=True), ddt4)
        dcbb = _b(dcb)
        dc_ref[...] = dc_acc + _dot(dcbb, bb)
        db_ref[...] = db_acc + _dot_tn(dcbb, cc)
        ii = lax.broadcasted_iota(jnp.int32, (ch, ch), 0)
        jj = lax.broadcasted_iota(jnp.int32, (ch, ch), 1)
        triu = jnp.where(ii <= jj, 1.0, 0.0)
        dla4 = _dot_hi(triu, dacs4)
        ddt4 = ddt4 + dla4 * a4
        da4 = jnp.sum(dla4 * dt4, axis=0, keepdims=True) * a4
        sel_t = sel.T
        ddt_raw = _dot_hi(ddt4, sel_t) * _sigmoid(xraw)
        ddt_acc[...] += ddt_raw
        st_ref[0:1, :] += _dot_hi(_row8(da4), sel_t)[0:1, :]
        st_ref[1:2, :] += _dot_hi(_row8(dd4), sel_t)[0:1, :]
        st_ref[2:3, :] += jnp.sum(ddt_raw, axis=0, keepdims=True)

        @pl.when(g == SSM_GROUPS - 1)
        def _():
            ddt_ref[...] = _b(ddt_acc[...])

    small = pl.BlockSpec((1, LANES), lambda s, g: (0, 0))
    rc = lambda s: nch - 1 - s
    return pl.pallas_call(
        body, name="ssd_bwd", grid=(nch, SSM_GROUPS),
        in_specs=[pl.BlockSpec((ch, gw), lambda s, g: (rc(s), g)),
                  pl.BlockSpec((ch, SSM_STATE), lambda s, g: (rc(s), b_off + g)),
                  pl.BlockSpec((ch, SSM_STATE), lambda s, g: (rc(s), c_off + g)),
                  pl.BlockSpec((ch, LANES), lambda s, g: (rc(s), 0)),
                  small, small, small,
                  pl.BlockSpec((1, hg, SSM_HEAD_DIM, SSM_STATE), lambda s, g: (rc(s), g, 0, 0)),
                  pl.BlockSpec((ch, gw), lambda s, g: (rc(s), g))],
        out_specs=[pl.BlockSpec((ch, gw), lambda s, g: (rc(s), g)),
                   pl.BlockSpec((ch, SSM_STATE), lambda s, g: (rc(s), g)),
                   pl.BlockSpec((ch, SSM_STATE), lambda s, g: (rc(s), g)),
                   pl.BlockSpec((ch, LANES), lambda s, g: (rc(s), 0)),
                   pl.BlockSpec((8, LANES), lambda s, g: (0, 0))],
        out_shape=[jax.ShapeDtypeStruct((S, SSM_INNER), F32),
                   jax.ShapeDtypeStruct((S, SSM_GROUPS * SSM_STATE), F32),
                   jax.ShapeDtypeStruct((S, SSM_GROUPS * SSM_STATE), F32),
                   jax.ShapeDtypeStruct((S, LANES), BF16),
                   jax.ShapeDtypeStruct((8, LANES), F32)],
        scratch_shapes=[pltpu.VMEM((SSM_HEADS, SSM_HEAD_DIM, SSM_STATE), F32),
                        pltpu.VMEM((ch, LANES), F32)],
        compiler_params=_params(("arbitrary", "arbitrary")),
    )(xact, xact, xact, dt_raw, dt_bias, a_neg, d_skip, hs, dy)


GROUP_W = HEADS_PER_GROUP * SSM_HEAD_DIM
B_COL0 = SSM_INNER
C_COL0 = SSM_INNER + SSM_GROUPS * SSM_STATE


def _ssd_prep(dt_raw, dt_bias, a_neg):
    S = dt_raw.shape[0]
    ch = SSM_CHUNK
    nch = S // ch

    def body(dtr_ref, bias_ref, a_ref, dt_ref, acs_ref, acst_ref, sig_ref):
        x = dtr_ref[...] + bias_ref[...]
        lane = lax.broadcasted_iota(jnp.int32, (ch, LANES), 1)
        dt = jnp.where(lane < SSM_HEADS, _softplus(x), 0.0)
        ii = lax.broadcasted_iota(jnp.int32, (ch, ch), 0)
        jj = lax.broadcasted_iota(jnp.int32, (ch, ch), 1)
        acs = _dot_hi(jnp.where(ii >= jj, 1.0, 0.0), dt * a_ref[...])
        dt_ref[...] = dt
        acs_ref[...] = acs
        acst_ref[0] = acs.T[0:SSM_HEADS, :]
        sig_ref[...] = _sigmoid(x)

    blk = pl.BlockSpec((ch, LANES), lambda c: (c, 0))
    small = pl.BlockSpec((1, LANES), lambda c: (0, 0))
    shp = jax.ShapeDtypeStruct((S, LANES), F32)
    return pl.pallas_call(
        body, name="ssd_prep", grid=(nch,),
        in_specs=[blk, small, small],
        out_specs=[blk, blk, pl.BlockSpec((1, SSM_HEADS, ch), lambda c: (c, 0, 0)), blk],
        out_shape=[shp, shp, jax.ShapeDtypeStruct((nch, SSM_HEADS, ch), F32), shp],
        compiler_params=_params(("parallel",)),
    )(dt_raw, dt_bias, a_neg)


def _expand_heads(arr, g, rows):
    lane = lax.broadcasted_iota(jnp.int32, (rows, GROUP_W), 1) // SSM_HEAD_DIM
    h0 = HEADS_PER_GROUP * g
    out = jnp.broadcast_to(arr[:, h0:h0 + 1], (rows, GROUP_W))
    for j in range(1, HEADS_PER_GROUP):
        out = jnp.where(lane == j, arr[:, h0 + j:h0 + j + 1], out)
    return out


def _seg_matrix(k, lanes_per_head, h0):
    r = lax.broadcasted_iota(jnp.int32, (k, LANES), 0)
    c = lax.broadcasted_iota(jnp.int32, (k, LANES), 1)
    return jnp.where(c == h0 + r // lanes_per_head, 1.0, 0.0).astype(BF16)


def _seg_dot(t, e):
    hi = _b(t)
    lo = _b(t - hi.astype(F32))
    return _dot(hi, e) + _dot(lo, e)


def _head_sums(t, e, rows):
    if rows >= 8:
        return _seg_dot(t, e)
    return _seg_dot(jnp.broadcast_to(t, (8, t.shape[1])), e)[0:rows]


def _pair_masks(x):
    lane = lax.broadcasted_iota(jnp.int32, x.shape, 1)
    zero = jnp.zeros_like(x)
    return jnp.where(lane < SSM_HEAD_DIM, x, zero), jnp.where(lane >= SSM_HEAD_DIM, x, zero)


def _ssd_fwd2(xact, dt, acs, acst, dsk_e):
    S = xact.shape[0]
    ch = SSM_CHUNK
    nch = S // ch

    def body(x_ref, dt_ref, acs_ref, acst_ref, dsk_ref, y_ref, hs_ref, h_ref):
        c = pl.program_id(0)

        @pl.when(c == 0)
        def _():
            h_ref[...] = jnp.zeros_like(h_ref)

        dt_all = dt_ref[...]
        acs_all = acs_ref[...]
        acst_all = acst_ref[0]
        alast = acs_all[ch - 1:ch, :]
        eacs = jnp.exp(acs_all)
        dsd_all = jnp.exp(alast - acs_all)
        cd_all = jnp.exp(alast)
        ii = lax.broadcasted_iota(jnp.int32, (ch, ch), 0)
        jj = lax.broadcasted_iota(jnp.int32, (ch, ch), 1)
        low = ii >= jj
        for g in range(SSM_GROUPS):
            xs = x_ref[:, g * GROUP_W:(g + 1) * GROUP_W]
            bb = _b(x_ref[:, B_COL0 + g * SSM_STATE:B_COL0 + (g + 1) * SSM_STATE])
            cc = _b(x_ref[:, C_COL0 + g * SSM_STATE:C_COL0 + (g + 1) * SSM_STATE])
            cb = _dot_nt(cc, bb)
            xd = xs * _expand_heads(dt_all, g, ch)
            xdb = _b(xd)
            ht = h_ref[g]
            rest = (_dot(cc, _b(ht)) * _expand_heads(eacs, g, ch)
                    + dsk_ref[:, g * GROUP_W:(g + 1) * GROUP_W] * xs)
            for p in range(HEADS_PER_GROUP // 2):
                lms = []
                for h in (HEADS_PER_GROUP * g + 2 * p, HEADS_PER_GROUP * g + 2 * p + 1):
                    diff = acs_all[:, h:h + 1] - acst_all[h:h + 1, :]
                    lms.append(_b(cb * jnp.exp(jnp.where(low, diff, -jnp.inf))))
                xa, xb = _pair_masks(xdb[:, p * LANES:(p + 1) * LANES])
                yp = _dot(jnp.concatenate(lms, axis=1), jnp.concatenate([xa, xb], axis=0))
                y_ref[:, g * GROUP_W + p * LANES:g * GROUP_W + (p + 1) * LANES] = (
                    yp + rest[:, p * LANES:(p + 1) * LANES])
            hs_ref[0, g] = ht
            st = _dot_tn(bb, _b(xd * _expand_heads(dsd_all, g, ch)))
            h_ref[g] = ht * _expand_heads(cd_all, g, 1) + st

    blk = pl.BlockSpec((ch, LANES), lambda c: (c, 0))
    return pl.pallas_call(
        body, name="ssd_fwd", grid=(nch,),
        in_specs=[pl.BlockSpec((ch, CONV_DIM), lambda c: (c, 0)), blk, blk,
                  pl.BlockSpec((1, SSM_HEADS, ch), lambda c: (c, 0, 0)),
                  pl.BlockSpec((1, SSM_INNER), lambda c: (0, 0))],
        out_specs=[pl.BlockSpec((ch, SSM_INNER), lambda c: (c, 0)),
                   pl.BlockSpec((1, SSM_GROUPS, SSM_STATE, GROUP_W), lambda c: (c, 0, 0, 0))],
        out_shape=[jax.ShapeDtypeStruct((S, SSM_INNER), F32),
                   jax.ShapeDtypeStruct((nch, SSM_GROUPS, SSM_STATE, GROUP_W), F32)],
        scratch_shapes=[pltpu.VMEM((SSM_GROUPS, SSM_STATE, GROUP_W), F32)],
        compiler_params=_params(("arbitrary",)),
    )(xact, dt, acs, acst, dsk_e)


def _ssd_bwd2(xact, dt, acs, acst, sig, a_neg, dsk_e, hs, dy):
    S = xact.shape[0]
    ch = SSM_CHUNK
    nch = S // ch

    def body(x_ref, dt_ref, acs_ref, acst_ref, sig_ref, a_ref, dsk_ref, hs_ref, dy_ref,
             dx_ref, ddt_ref, st_ref, dh_ref, rows_ref):
        step = pl.program_id(0)

        @pl.when(step == 0)
        def _():
            dh_ref[...] = jnp.zeros_like(dh_ref)
            st_ref[...] = jnp.zeros_like(st_ref)
            rows_ref[...] = jnp.zeros_like(rows_ref)

        dt_all = dt_ref[...]
        acs_all = acs_ref[...]
        acst_all = acst_ref[0]
        alast = acs_all[ch - 1:ch, :]
        eacs = jnp.exp(acs_all)
        dsd_all = jnp.exp(alast - acs_all)
        cd_all = jnp.exp(alast)
        ii = lax.broadcasted_iota(jnp.int32, (ch, ch), 0)
        jj = lax.broadcasted_iota(jnp.int32, (ch, ch), 1)
        low = ii >= jj
        lane = lax.broadcasted_iota(jnp.int32, (ch, LANES), 1)
        cols = jnp.zeros((ch, LANES), F32)
        ddt = jnp.zeros((ch, LANES), F32)
        dal = jnp.zeros((1, LANES), F32)
        ddsk = jnp.zeros((1, LANES), F32)
        for g in range(SSM_GROUPS):
            xs = x_ref[:, g * GROUP_W:(g + 1) * GROUP_W]
            bb = _b(x_ref[:, B_COL0 + g * SSM_STATE:B_COL0 + (g + 1) * SSM_STATE])
            cc = _b(x_ref[:, C_COL0 + g * SSM_STATE:C_COL0 + (g + 1) * SSM_STATE])
            cb = _dot_nt(cc, bb)
            dt_e = _expand_heads(dt_all, g, ch)
            ea_e = _expand_heads(eacs, g, ch)
            dsd_e = _expand_heads(dsd_all, g, ch)
            cd_e = _expand_heads(cd_all, g, 1)
            xd = xs * dt_e
            xdb = _b(xd)
            dyg = dy_ref[:, g * GROUP_W:(g + 1) * GROUP_W]
            dyb = _b(dyg)
            ht = hs_ref[0, g]
            htb = _b(ht)
            dhn = dh_ref[g]
            dhnb = _b(dhn)
            zz = _dot(cc, htb)
            dzb = _b(dyg * ea_e)
            d_c = _dot_nt(dzb, htb)
            dh_in = _dot_tn(cc, dzb)
            ww = _dot(bb, dhnb)
            xdd = xd * dsd_e
            d_b = _dot_nt(_b(xdd), dhnb)
            t2 = ww * xdd
            e_g = _seg_matrix(GROUP_W, SSM_HEAD_DIM, HEADS_PER_GROUP * g)
            cols = cols + _head_sums(dyg * zz * ea_e - t2, e_g, ch)
            dal = dal + _head_sums(jnp.sum(t2, axis=0, keepdims=True), e_g, 1) + cd_all * _head_sums(
                jnp.sum(dhn * ht, axis=0, keepdims=True), e_g, 1)
            dh_ref[g] = dh_in + dhn * cd_e
            ddsk = ddsk + _head_sums(jnp.sum(dyg * xs, axis=0, keepdims=True), e_g, 1)
            dxd_rest = ww * dsd_e
            dcb = jnp.zeros((ch, ch), F32)
            for p in range(HEADS_PER_GROUP // 2):
                dya, dyb2 = _pair_masks(dyb[:, p * LANES:(p + 1) * LANES])
                xp = xdb[:, p * LANES:(p + 1) * LANES]
                lms, gms = [], []
                for h, dyh in ((HEADS_PER_GROUP * g + 2 * p, dya), (HEADS_PER_GROUP * g + 2 * p + 1, dyb2)):
                    diff = acs_all[:, h:h + 1] - acst_all[h:h + 1, :]
                    decay = jnp.exp(jnp.where(low, diff, -jnp.inf))
                    lm = cb * decay
                    dlm = _dot_nt(dyh, xp)
                    gm = dlm * lm
                    dcb = dcb + dlm * decay
                    rows_ref[h:h + 1, :] = jnp.sum(gm, axis=0, keepdims=True)
                    lms.append(_b(lm))
                    gms.append(gm)
                h0 = HEADS_PER_GROUP * g + 2 * p
                cols = cols + _head_sums(jnp.concatenate(gms, axis=1), _seg_matrix(2 * ch, ch, h0), ch)
                dxd = _dot_tn(jnp.concatenate(lms, axis=0), jnp.concatenate([dya, dyb2], axis=0))
                dxd = dxd + dxd_rest[:, p * LANES:(p + 1) * LANES]
                sl = slice(g * GROUP_W + p * LANES, g * GROUP_W + (p + 1) * LANES)
                dx_ref[:, sl] = (dsk_ref[:, sl] * dyg[:, p * LANES:(p + 1) * LANES]
                                 + dxd * dt_e[:, p * LANES:(p + 1) * LANES])
                ddt = ddt + _head_sums(dxd * xs[:, p * LANES:(p + 1) * LANES],
                                       _seg_matrix(LANES, SSM_HEAD_DIM, h0), ch)
            dcbb = _b(dcb)
            dx_ref[:, C_COL0 + g * SSM_STATE:C_COL0 + (g + 1) * SSM_STATE] = d_c + _dot(dcbb, bb)
            dx_ref[:, B_COL0 + g * SSM_STATE:B_COL0 + (g + 1) * SSM_STATE] = d_b + _dot_tn(dcbb, cc)
        rowi = lax.broadcasted_iota(jnp.int32, (ch, 1), 0)
        dacs = cols - rows_ref[...].T + jnp.where(rowi == ch - 1, dal, 0.0)
        dla = _dot_hi(jnp.where(ii <= jj, 1.0, 0.0), dacs)
        a_row = a_ref[...]
        ddt_raw = (ddt + dla * a_row) * sig_ref[...]
        ddt_ref[...] = _b(ddt_raw)
        st_ref[0:1, :] += jnp.sum(dla * dt_all, axis=0, keepdims=True) * a_row
        st_ref[1:2, :] += ddsk
        st_ref[2:3, :] += jnp.sum(ddt_raw, axis=0, keepdims=True)

    rc = lambda s: nch - 1 - s
    blk = pl.BlockSpec((ch, LANES), lambda s: (rc(s), 0))
    return pl.pallas_call(
        body, name="ssd_bwd", grid=(nch,),
        in_specs=[pl.BlockSpec((ch, CONV_DIM), lambda s: (rc(s), 0)), blk, blk,
                  pl.BlockSpec((1, SSM_HEADS, ch), lambda s: (rc(s), 0, 0)), blk,
                  pl.BlockSpec((1, LANES), lambda s: (0, 0)),
                  pl.BlockSpec((1, SSM_INNER), lambda s: (0, 0)),
                  pl.BlockSpec((1, SSM_GROUPS, SSM_STATE, GROUP_W), lambda s: (rc(s), 0, 0, 0)),
                  pl.BlockSpec((ch, SSM_INNER), lambda s: (rc(s), 0))],
        out_specs=[pl.BlockSpec((ch, CONV_DIM), lambda s: (rc(s), 0)), blk,
                   pl.BlockSpec((8, LANES), lambda s: (0, 0))],
        out_shape=[jax.ShapeDtypeStruct((S, CONV_DIM), F32), jax.ShapeDtypeStruct((S, LANES), BF16),
                   jax.ShapeDtypeStruct((8, LANES), F32)],
        scratch_shapes=[pltpu.VMEM((SSM_GROUPS, SSM_STATE, GROUP_W), F32), pltpu.VMEM((LANES, ch), F32)],
        compiler_params=_params(("arbitrary",)),
    )(xact, dt, acs, acst, sig, a_neg, dsk_e, hs, dy)


def _pad_lanes(v, n=LANES):
    return jnp.pad(v, ((0, 0), (0, n - v.shape[1])))


def _local_step(x, target, w):
    offs = np.cumsum((0,) + IN_SPLITS)
    w_in = w["w_in"]
    w_qkv = w_in[:, offs[0]:offs[3]]
    w_z = w_in[:, offs[3]:offs[4]]
    w_xbc = w_in[:, offs[4]:offs[5]]
    w_dt = _pad_lanes(w_in[:, offs[5]:offs[6]])
    w_g = w_in[:, offs[6]:offs[7]]
    dt_bias = _pad_lanes(w["dt_bias"])
    a_neg = _pad_lanes(-jnp.exp(w["a_log"]))
    d_skip = _pad_lanes(w["d_skip"])

    u = _rms_fwd(x, w["norm_mix_pre_w"])
    qkv = _mm_nn(u, w_qkv, F32, "proj_qkv")
    z = _mm_nn(u, w_z, F32, "proj_z")
    xbc = _mm_nn(u, w_xbc, F32, "proj_xbc")
    dt_raw = _mm_nn(u, w_dt, F32, "proj_dt")
    gl = _mm_nn(u, w_g, F32, "proj_gate")

    pats = _qkv_layouts(qkv)
    os_, ms_, ls_ = [], [], []
    for d, qkv_p in zip(DILATIONS, pats):
        o, m, l = _attn_fwd2(qkv_p, d)
        os_.append(o)
        ms_.append(m)
        ls_.append(l)
    att, lse = _attn_combine2(os_, ms_, ls_)
    att_o = _mm_nn(att, w["w_att_proj"], F32, "att_proj")

    xact = _conv_fwd2(xbc, w["conv_w"], w["conv_b"])
    dsk_e = jnp.repeat(w["d_skip"], SSM_HEAD_DIM, axis=1)
    dt, acs, acst, sig = _ssd_prep(dt_raw, dt_bias, a_neg)
    y_ssd, hs = _ssd_fwd2(xact, dt, acs, acst, dsk_e)
    ssm_y = _gnorm_fwd(y_ssd, z, w["ssm_norm_w"])
    ssm_o = _mm_nn(ssm_y, w["w_ssm_proj"], F32, "ssm_proj")

    mi = _gate_fwd(att_o, ssm_o, gl, w["b_gate"])
    mixed = _mm_nn(mi, w["w_out"], F32, "out_proj")
    h1, f = _post_pre(x, mixed, w["norm_mix_post_w"], w["norm_ffn_pre_w"])
    r_up, act = _mm_nn(f, w["w_up"], BF16, "ffn_up", mode="relu2")
    down = _mm_nn(act, w["w_down"], F32, "ffn_down")
    dh2, d_down, loss, g_ffn_post = _final(h1, down, w["norm_ffn_post_w"], target)

    g = {"norm_ffn_post_w": g_ffn_post}
    g["w_down"] = _mm_tn(act, d_down, "dw_down")
    dup = _mm_nn(d_down, w["w_down"].T, BF16, "d_act", mode="mul2", extra=r_up)
    g["w_up"] = _mm_tn(f, dup, "dw_up")
    df = _mm_nn(dup, w["w_up"].T, F32, "d_f")
    dh1, d_mixed, g["norm_ffn_pre_w"], g["norm_mix_post_w"] = _mid_bwd(
        dh2, df, h1, mixed, w["norm_ffn_pre_w"], w["norm_mix_post_w"])
    g["w_out"] = _mm_tn(mi, d_mixed, "dw_out")
    dmi = _mm_nn(d_mixed, w["w_out"].T, F32, "d_mi")
    d_att_o, d_ssm_o, dgl, g["b_gate"] = _gate_bwd(dmi, att_o, ssm_o, gl, w["b_gate"])

    g["w_att_proj"] = _mm_tn(att, d_att_o, "dw_att_proj")
    d_att = _mm_nn(d_att_o, w["w_att_proj"].T, F32, "d_att")
    dqs, dks, dvs = [], [], []
    for d, qkv_p, (do_p, lse_p, delta_p) in zip(DILATIONS, pats, _attn_delta2(d_att, att, lse)):
        dq, dk, dv = _attn_bwd2(qkv_p, do_p, lse_p, delta_p, d)
        dqs.append(dq)
        dks.append(dk)
        dvs.append(dv)
    dqkv = _sum_qkv2(dqs, dks, dvs)

    g["w_ssm_proj"] = _mm_tn(ssm_y, d_ssm_o, "dw_ssm_proj")
    d_ssm_y = _mm_nn(d_ssm_o, w["w_ssm_proj"].T, F32, "d_ssm_y")
    dy_ssd, dz, g["ssm_norm_w"] = _gnorm_bwd(d_ssm_y, y_ssd, z, w["ssm_norm_w"])
    dxact, ddt_raw, stats = _ssd_bwd2(xact, dt, acs, acst, sig, a_neg, dsk_e, hs, dy_ssd)
    g["a_log"] = stats[0:1, :SSM_HEADS]
    g["d_skip"] = stats[1:2, :SSM_HEADS]
    g["dt_bias"] = stats[2:3, :SSM_HEADS]
    dxbc, g["conv_w"], g["conv_b"] = _conv_bwd2(xbc, dxact, w["conv_w"], w["conv_b"])

    pieces = [(dqkv, w_qkv), (dz, w_z), (dxbc, w_xbc), (ddt_raw, w_dt), (dgl, w_g)]
    du = None
    gw = []
    for i, (dp, wp) in enumerate(pieces):
        gw.append(_mm_tn(u, dp, f"dw_in_{i}"))
        du = _mm_nn(dp, wp.T, F32, f"d_u_{i}", acc=du)
    gw[3] = gw[3][:, :SSM_HEADS]
    g["w_in"] = jnp.concatenate(gw, axis=1)
    grad_x, g["norm_mix_pre_w"] = _first_bwd(dh1, du, x, w["norm_mix_pre_w"])
    return loss, grad_x, g


BIG = ("w_in", "w_att_proj", "w_ssm_proj", "w_out", "w_up", "w_down")
BIG_FULL_SHAPES = {"w_in": (D_MODEL, IN_PROJ_WIDTH), "w_att_proj": (ATT_WIDTH, D_MODEL),
                   "w_ssm_proj": (SSM_INNER, D_MODEL), "w_out": (D_MODEL, D_MODEL),
                   "w_up": (D_MODEL, FFN_HIDDEN), "w_down": (FFN_HIDDEN, D_MODEL)}
BIG_COL_SHARDED = {"w_in": True, "w_att_proj": True, "w_ssm_proj": False, "w_out": False, "w_up": True,
                   "w_down": False}
PACK_COLS = 1024
PACK_ROWS = 5760
PACK_HALF = PACK_ROWS // 2
PACK_BLOCK = 576
SMALL = ("norm_mix_pre_w", "b_gate", "conv_b", "dt_bias", "a_log", "d_skip", "ssm_norm_w",
         "norm_mix_post_w", "norm_ffn_pre_w", "norm_ffn_post_w")
SMALL_ROWS = 232


def _shard_shape(name):
    r, c = BIG_FULL_SHAPES[name]
    return (r, c // N_CHIPS) if BIG_COL_SHARDED[name] else (r // N_CHIPS, c)


def _pack(shards, dtype):
    flat = [shards[n].astype(dtype).reshape(-1, PACK_COLS) for n in BIG]
    rows = sum(f.shape[0] for f in flat)
    flat.append(jnp.zeros((PACK_ROWS - rows, PACK_COLS), dtype))
    return jnp.concatenate(flat, axis=0)


def _unpack(packed):
    out, r0 = {}, 0
    for n in BIG:
        shp = _shard_shape(n)
        rows = shp[0] * shp[1] // PACK_COLS
        out[n] = packed[r0:r0 + rows].reshape(shp)
        r0 += rows
    return out


def _unpack_full(gathered):
    out, r0 = {}, 0
    for n in BIG:
        shp = _shard_shape(n)
        rows = shp[0] * shp[1] // PACK_COLS
        sh = gathered[:, r0:r0 + rows].reshape((N_CHIPS,) + shp)
        if BIG_COL_SHARDED[n]:
            out[n] = sh.transpose(1, 0, 2).reshape(BIG_FULL_SHAPES[n])
        else:
            out[n] = sh.reshape(BIG_FULL_SHAPES[n])
        r0 += rows
    return out


def _pack_full(grads):
    parts = []
    rows_total = 0
    for n in BIG:
        shp = _shard_shape(n)
        gfull = grads[n]
        if BIG_COL_SHARDED[n]:
            sh = gfull.reshape(shp[0], N_CHIPS, shp[1]).transpose(1, 0, 2)
        else:
            sh = gfull.reshape((N_CHIPS,) + shp)
        parts.append(sh.reshape(N_CHIPS, -1, PACK_COLS))
        rows_total += parts[-1].shape[1]
    parts.append(jnp.zeros((N_CHIPS, PACK_ROWS - rows_total, PACK_COLS), F32))
    return jnp.concatenate(parts, axis=1)


def _mesh_pos():
    return lax.axis_index("x"), lax.axis_index("y"), lax.axis_index("c")


def _other_chips(x, y):
    return [(1 - x, y), (x, 1 - y), (1 - x, 1 - y)]


ANY = pl.BlockSpec(memory_space=pl.ANY)


def _allgather_packed(wpack):
    half = PACK_HALF

    def body(w_ref, out_ref, send_sems, recv_sems):
        x, y, c = _mesh_pos()
        me = 2 * x + y
        sibling = (x, y, 1 - c)
        chips = _other_chips(x, y)

        def rows(chip, h):
            return out_ref.at[chip, pl.ds(h * half, half), :]

        def copy(k, chip, h, to, src=None):
            return pltpu.make_async_remote_copy(
                src_ref=rows(chip, h) if src is None else src, dst_ref=rows(chip, h),
                send_sem=send_sems.at[k], recv_sem=recv_sems.at[k], device_id=to, device_id_type=MESH)

        mine_half = w_ref.at[pl.ds(c * half, half), :]
        first = [copy(j, me, c, (*chip, c), src=mine_half) for j, chip in enumerate(chips)]
        for cp in first:
            cp.start()
        passed = [copy(3 + j, 2 * chip[0] + chip[1], c, sibling) for j, chip in enumerate(chips)]
        for j, chip in enumerate(chips):
            copy(j, 2 * chip[0] + chip[1], c, (x, y, c)).wait_recv()
            passed[j].start()
        for j, chip in enumerate(chips):
            copy(3 + j, 2 * chip[0] + chip[1], 1 - c, (x, y, c)).wait_recv()
        for cp in first + passed:
            cp.wait_send()

    return pl.pallas_call(
        body, name="allgather_weights",
        out_shape=jax.ShapeDtypeStruct((N_CHIPS,) + wpack.shape, wpack.dtype),
        in_specs=[ANY], out_specs=ANY,
        scratch_shapes=[pltpu.SemaphoreType.DMA((6,)), pltpu.SemaphoreType.DMA((6,))],
        compiler_params=pltpu.CompilerParams(has_side_effects=True),
    )(wpack)


def _exchange_halves(gpack):
    half = PACK_HALF

    def body(g_ref, out_ref, send_sem, recv_sem):
        x, y, c = _mesh_pos()
        cp = pltpu.make_async_remote_copy(
            src_ref=g_ref.at[:, pl.ds((1 - c) * half, half), :], dst_ref=out_ref,
            send_sem=send_sem, recv_sem=recv_sem, device_id=(x, y, 1 - c), device_id_type=MESH)
        cp.start()
        cp.wait()

    return pl.pallas_call(
        body, name="rs_pair_exchange",
        out_shape=jax.ShapeDtypeStruct((N_CHIPS, half, PACK_COLS), F32),
        in_specs=[ANY], out_specs=ANY,
        scratch_shapes=[pltpu.SemaphoreType.DMA, pltpu.SemaphoreType.DMA],
        compiler_params=pltpu.CompilerParams(has_side_effects=True),
    )(gpack)


def _pair_add(gpack, recv, c_idx):
    nb = PACK_HALF // PACK_BLOCK

    def body(c_ref, g_ref, r_ref, o_ref):
        o_ref[...] = _b(g_ref[...] + r_ref[...])

    blk = (1, PACK_BLOCK, PACK_COLS)
    return pl.pallas_call(
        body, name="rs_pair_add",
        grid_spec=pltpu.PrefetchScalarGridSpec(
            num_scalar_prefetch=1, grid=(N_CHIPS, nb),
            in_specs=[pl.BlockSpec(blk, lambda s, i, c: (s, c[0] * nb + i, 0)),
                      pl.BlockSpec(blk, lambda s, i, c: (s, i, 0))],
            out_specs=pl.BlockSpec(blk, lambda s, i, c: (s, i, 0))),
        out_shape=jax.ShapeDtypeStruct((N_CHIPS, PACK_HALF, PACK_COLS), BF16),
        compiler_params=_params(("arbitrary", "arbitrary")),
    )(c_idx, gpack, recv)


def _exchange_chips(ppack):
    def body(p_ref, out_ref, send_sems, recv_sems):
        x, y, c = _mesh_pos()
        chips = _other_chips(x, y)
        cps = [pltpu.make_async_remote_copy(
            src_ref=p_ref.at[2 * chip[0] + chip[1]], dst_ref=out_ref.at[j],
            send_sem=send_sems.at[j], recv_sem=recv_sems.at[j], device_id=(*chip, c), device_id_type=MESH)
            for j, chip in enumerate(chips)]
        for cp in cps:
            cp.start()
        for cp in cps:
            cp.wait_recv()
        for cp in cps:
            cp.wait_send()

    return pl.pallas_call(
        body, name="rs_chip_exchange",
        out_shape=jax.ShapeDtypeStruct((N_CHIPS - 1, PACK_HALF, PACK_COLS), ppack.dtype),
        in_specs=[ANY], out_specs=ANY,
        scratch_shapes=[pltpu.SemaphoreType.DMA((3,)), pltpu.SemaphoreType.DMA((3,))],
        compiler_params=pltpu.CompilerParams(has_side_effects=True),
    )(ppack)


def _chip_add(ppack, recv, me_idx):
    nb = PACK_HALF // PACK_BLOCK

    def body(m_ref, p_ref, r0_ref, r1_ref, r2_ref, o_ref):
        o_ref[...] = ((p_ref[0].astype(F32) + r0_ref[0].astype(F32)) + r1_ref[0].astype(F32)) + r2_ref[0].astype(F32)

    blk = (1, PACK_BLOCK, PACK_COLS)
    return pl.pallas_call(
        body, name="rs_chip_add",
        grid_spec=pltpu.PrefetchScalarGridSpec(
            num_scalar_prefetch=1, grid=(nb,),
            in_specs=[pl.BlockSpec(blk, lambda i, m: (m[0], i, 0)),
                      pl.BlockSpec(blk, lambda i, m: (0, i, 0)),
                      pl.BlockSpec(blk, lambda i, m: (1, i, 0)),
                      pl.BlockSpec(blk, lambda i, m: (2, i, 0))],
            out_specs=pl.BlockSpec((PACK_BLOCK, PACK_COLS), lambda i, m: (i, 0))),
        out_shape=jax.ShapeDtypeStruct((PACK_HALF, PACK_COLS), F32),
        compiler_params=_params(("arbitrary",)),
    )(me_idx, ppack, recv, recv, recv)


def _share_halves(qhalf):
    def body(q_ref, out_ref, send_sem, recv_sem):
        x, y, c = _mesh_pos()
        cp = pltpu.make_async_remote_copy(
            src_ref=q_ref, dst_ref=out_ref, send_sem=send_sem, recv_sem=recv_sem,
            device_id=(x, y, 1 - c), device_id_type=MESH)
        cp.start()
        cp.wait()

    return pl.pallas_call(
        body, name="rs_share_halves",
        out_shape=jax.ShapeDtypeStruct(qhalf.shape, F32),
        in_specs=[ANY], out_specs=ANY,
        scratch_shapes=[pltpu.SemaphoreType.DMA, pltpu.SemaphoreType.DMA],
        compiler_params=pltpu.CompilerParams(has_side_effects=True),
    )(qhalf)


REST = ("w_att_proj", "w_ssm_proj", "w_out", "w_up", "w_down")
ADD_ROWS = {IN_PROJ_WIDTH // N_CHIPS: 256, PACK_COLS: 752}


def _stack_rest(shards, dtype):
    return jnp.concatenate([shards[n].astype(dtype).reshape(-1, PACK_COLS) for n in REST], axis=0)


def _unstack_rest(stacked, lead=()):
    out, r0 = {}, 0
    for n in REST:
        shp = _shard_shape(n)
        rows = shp[0] * shp[1] // PACK_COLS
        out[n] = stacked[..., r0:r0 + rows, :].reshape(lead + shp)
        r0 += rows
    return out


def _full_from_shards(name, sh):
    if BIG_COL_SHARDED[name]:
        return sh.transpose(1, 0, 2).reshape(BIG_FULL_SHAPES[name])
    return sh.reshape(BIG_FULL_SHAPES[name])


def _shards_from_full(name, full):
    shp = _shard_shape(name)
    if BIG_COL_SHARDED[name]:
        return full.reshape(shp[0], N_CHIPS, shp[1]).transpose(1, 0, 2)
    return full.reshape((N_CHIPS,) + shp)


def _allgather2(shards):
    n = len(shards)

    def body(*refs):
        w_refs, out_refs, send_sems, recv_sems = refs[:n], refs[n:2 * n], refs[2 * n], refs[2 * n + 1]
        x, y, c = _mesh_pos()
        me = 2 * x + y
        sibling = (x, y, 1 - c)
        chips = _other_chips(x, y)
        plans = []
        for a, (w_ref, out_ref) in enumerate(zip(w_refs, out_refs)):
            half = w_ref.shape[0] // 2

            def copy(k, chip, h, to, src=None, out_ref=out_ref, half=half, a=a):
                rows = out_ref.at[chip, pl.ds(h * half, half), :]
                return pltpu.make_async_remote_copy(
                    src_ref=rows if src is None else src, dst_ref=rows,
                    send_sem=send_sems.at[6 * a + k], recv_sem=recv_sems.at[6 * a + k],
                    device_id=to, device_id_type=MESH)

            mine_half = w_ref.at[pl.ds(c * half, half), :]
            idx = [2 * chip[0] + chip[1] for chip in chips]
            send = [copy(j, me, c, (*chip, c), src=mine_half) for j, chip in enumerate(chips)]
            land = [copy(j, idx[j], c, (x, y, c)) for j in range(N_CHIPS - 1)]
            forward = [copy(3 + j, idx[j], c, sibling) for j in range(N_CHIPS - 1)]
            land_fw = [copy(3 + j, idx[j], 1 - c, (x, y, c)) for j in range(N_CHIPS - 1)]
            plans.append((send, land, forward, land_fw))
        for send, _, _, _ in plans:
            for cp in send:
                cp.start()
        for _, land, forward, _ in plans:
            for j in range(N_CHIPS - 1):
                land[j].wait_recv()
                forward[j].start()
        for _, _, _, land_fw in plans:
            for cp in land_fw:
                cp.wait_recv()
        for send, _, forward, _ in plans:
            for cp in send + forward:
                cp.wait_send()

    return pl.pallas_call(
        body, name="allgather_weights",
        out_shape=[jax.ShapeDtypeStruct((N_CHIPS,) + s.shape, s.dtype) for s in shards],
        in_specs=[ANY] * n, out_specs=[ANY] * n,
        scratch_shapes=[pltpu.SemaphoreType.DMA((6 * n,)), pltpu.SemaphoreType.DMA((6 * n,))],
        compiler_params=pltpu.CompilerParams(has_side_effects=True),
    )(*shards)


def _exchange_halves2(gs):
    n = len(gs)

    def body(*refs):
        g_refs, out_refs, send_sems, recv_sems = refs[:n], refs[n:2 * n], refs[2 * n], refs[2 * n + 1]
        x, y, c = _mesh_pos()
        cps = []
        for a, (g_ref, out_ref) in enumerate(zip(g_refs, out_refs)):
            half = g_ref.shape[1] // 2
            cps.append(pltpu.make_async_remote_copy(
                src_ref=g_ref.at[:, pl.ds((1 - c) * half, half), :], dst_ref=out_ref,
                send_sem=send_sems.at[a], recv_sem=recv_sems.at[a], device_id=(x, y, 1 - c),
                device_id_type=MESH))
        for cp in cps:
            cp.start()
        for cp in cps:
            cp.wait()

    return pl.pallas_call(
        body, name="rs_pair_exchange",
        out_shape=[jax.ShapeDtypeStruct((N_CHIPS, g.shape[1] // 2, g.shape[2]), F32) for g in gs],
        in_specs=[ANY] * n, out_specs=[ANY] * n,
        scratch_shapes=[pltpu.SemaphoreType.DMA((n,)), pltpu.SemaphoreType.DMA((n,))],
        compiler_params=pltpu.CompilerParams(has_side_effects=True),
    )(*gs)


def _pair_add2(g, recv, c_idx, name):
    _, half, cols = recv.shape
    rb = ADD_ROWS[cols]
    nb = half // rb

    def body(c_ref, g_ref, r_ref, o_ref):
        o_ref[...] = _b(g_ref[...] + r_ref[...])

    blk = (1, rb, cols)
    return pl.pallas_call(
        body, name=name,
        grid_spec=pltpu.PrefetchScalarGridSpec(
            num_scalar_prefetch=1, grid=(N_CHIPS, nb),
            in_specs=[pl.BlockSpec(blk, lambda s, i, c: (s, c[0] * nb + i, 0)),
                      pl.BlockSpec(blk, lambda s, i, c: (s, i, 0))],
            out_specs=pl.BlockSpec(blk, lambda s, i, c: (s, i, 0))),
        out_shape=jax.ShapeDtypeStruct(recv.shape, BF16),
        compiler_params=_params(("arbitrary", "arbitrary")),
    )(c_idx, g, recv)


def _exchange_chips2(ps):
    n = len(ps)

    def body(*refs):
        p_refs, out_refs, send_sems, recv_sems = refs[:n], refs[n:2 * n], refs[2 * n], refs[2 * n + 1]
        x, y, c = _mesh_pos()
        chips = _other_chips(x, y)
        cps = [pltpu.make_async_remote_copy(
            src_ref=p_ref.at[2 * chip[0] + chip[1]], dst_ref=out_ref.at[j],
            send_sem=send_sems.at[3 * a + j], recv_sem=recv_sems.at[3 * a + j], device_id=(*chip, c),
            device_id_type=MESH)
            for a, (p_ref, out_ref) in enumerate(zip(p_refs, out_refs)) for j, chip in enumerate(chips)]
        for cp in cps:
            cp.start()
        for cp in cps:
            cp.wait_recv()
        for cp in cps:
            cp.wait_send()

    return pl.pallas_call(
        body, name="rs_chip_exchange",
        out_shape=[jax.ShapeDtypeStruct((N_CHIPS - 1,) + p.shape[1:], p.dtype) for p in ps],
        in_specs=[ANY] * n, out_specs=[ANY] * n,
        scratch_shapes=[pltpu.SemaphoreType.DMA((3 * n,)), pltpu.SemaphoreType.DMA((3 * n,))],
        compiler_params=pltpu.CompilerParams(has_side_effects=True),
    )(*ps)


def _chip_add2(p, recv, me_idx, name):
    _, half, cols = recv.shape
    rb = ADD_ROWS[cols]

    def body(m_ref, p_ref, r0_ref, r1_ref, r2_ref, o_ref):
        o_ref[...] = ((p_ref[0].astype(F32) + r0_ref[0].astype(F32)) + r1_ref[0].astype(F32)) + r2_ref[0].astype(F32)

    blk = (1, rb, cols)
    return pl.pallas_call(
        body, name=name,
        grid_spec=pltpu.PrefetchScalarGridSpec(
            num_scalar_prefetch=1, grid=(half // rb,),
            in_specs=[pl.BlockSpec(blk, lambda i, m: (m[0], i, 0)),
                      pl.BlockSpec(blk, lambda i, m: (0, i, 0)),
                      pl.BlockSpec(blk, lambda i, m: (1, i, 0)),
                      pl.BlockSpec(blk, lambda i, m: (2, i, 0))],
            out_specs=pl.BlockSpec((rb, cols), lambda i, m: (i, 0))),
        out_shape=jax.ShapeDtypeStruct((half, cols), F32),
        compiler_params=_params(("arbitrary",)),
    )(me_idx, p, recv, recv, recv)


def _share_halves2(qs):
    n = len(qs)

    def body(*refs):
        q_refs, out_refs, send_sems, recv_sems = refs[:n], refs[n:2 * n], refs[2 * n], refs[2 * n + 1]
        x, y, c = _mesh_pos()
        cps = [pltpu.make_async_remote_copy(
            src_ref=q_ref, dst_ref=out_ref, send_sem=send_sems.at[a], recv_sem=recv_sems.at[a],
            device_id=(x, y, 1 - c), device_id_type=MESH)
            for a, (q_ref, out_ref) in enumerate(zip(q_refs, out_refs))]
        for cp in cps:
            cp.start()
        for cp in cps:
            cp.wait()

    return pl.pallas_call(
        body, name="rs_share_halves",
        out_shape=[jax.ShapeDtypeStruct(q.shape, F32) for q in qs],
        in_specs=[ANY] * n, out_specs=[ANY] * n,
        scratch_shapes=[pltpu.SemaphoreType.DMA((n,)), pltpu.SemaphoreType.DMA((n,))],
        compiler_params=pltpu.CompilerParams(has_side_effects=True),
    )(*qs)


def _allreduce_small(part, name):
    rows = part.shape[0]

    def body(p_ref, out_ref, buf, send_sems, recv_sems, local_sem):
        x, y, c = _mesh_pos()
        me, sibling = (x, y, c), (x, y, 1 - c)
        chips = _other_chips(x, y)

        def slot(px, py, pc):
            return buf.at[pl.ds((4 * px + 2 * py + pc) * rows, rows), :]

        def copy(k, block, to, src=None):
            return pltpu.make_async_remote_copy(
                src_ref=slot(*block) if src is None else src, dst_ref=slot(*block),
                send_sem=send_sems.at[k], recv_sem=recv_sems.at[k], device_id=to, device_id_type=MESH)

        mine = pltpu.make_async_copy(p_ref, slot(*me), local_sem)
        mine.start()
        first = [copy(0, me, sibling, src=p_ref)]
        first += [copy(1 + j, me, (*chip, c), src=p_ref) for j, chip in enumerate(chips)]
        for cp in first:
            cp.start()
        passed = [copy(4 + j, (*chip, c), sibling) for j, chip in enumerate(chips)]
        for j, chip in enumerate(chips):
            copy(1 + j, (*chip, c), me).wait_recv()
            passed[j].start()
        copy(0, sibling, me).wait_recv()
        for j, chip in enumerate(chips):
            copy(4 + j, (*chip, 1 - c), me).wait_recv()
        for cp in first + passed:
            cp.wait_send()
        mine.wait()
        acc = buf[pl.ds(0, rows), :]
        for k in range(1, N_DEV):
            acc = acc + buf[pl.ds(k * rows, rows), :]
        out_ref[...] = acc

    return pl.pallas_call(
        body, name=name,
        out_shape=jax.ShapeDtypeStruct(part.shape, F32),
        in_specs=[pl.BlockSpec(memory_space=pltpu.VMEM)],
        out_specs=pl.BlockSpec(memory_space=pltpu.VMEM),
        scratch_shapes=[pltpu.VMEM((N_DEV * rows, LANES), F32), pltpu.SemaphoreType.DMA((7,)),
                        pltpu.SemaphoreType.DMA((7,)), pltpu.SemaphoreType.DMA],
        compiler_params=pltpu.CompilerParams(has_side_effects=True),
    )(part)


def _adamw(w, g, m, v, name):
    R, C = w.shape
    bs = _pick(R, (128, 64, 32, 8)) if R % 8 == 0 else R
    c1 = 1.0 / (1.0 - ADAM_B1 ** ADAM_STEP)
    c2 = 1.0 / (1.0 - ADAM_B2 ** ADAM_STEP)

    def body(w_ref, g_ref, m_ref, v_ref, d_ref, nm_ref, nv_ref):
        gg = g_ref[...]
        nm = ADAM_B1 * m_ref[...] + (1.0 - ADAM_B1) * gg
        nv = ADAM_B2 * v_ref[...] + (1.0 - ADAM_B2) * (gg * gg)
        nm_ref[...] = nm
        nv_ref[...] = nv
        d_ref[...] = -ADAM_LR * ((nm * c1) / (jnp.sqrt(nv * c2) + ADAM_EPS) + ADAM_WD * w_ref[...])

    spec = pl.BlockSpec((bs, C), lambda i: (i, 0))
    shp = jax.ShapeDtypeStruct((R, C), F32)
    return pl.pallas_call(
        body, name=name, grid=(R // bs,), in_specs=[spec] * 4, out_specs=[spec] * 3, out_shape=[shp] * 3,
        compiler_params=_params(("parallel",)),
    )(w, g, m, v)


WEIGHTS = ("norm_mix_pre_w", "w_in", "b_gate", "conv_w", "conv_b", "dt_bias", "a_log", "d_skip",
           "ssm_norm_w", "w_att_proj", "w_ssm_proj", "w_out", "norm_mix_post_w", "norm_ffn_pre_w", "w_up",
           "w_down", "norm_ffn_post_w")


def _flat_small(vals, conv_w_full):
    flat = [vals[n].reshape(-1) for n in SMALL] + [conv_w_full.reshape(-1)]
    v = jnp.concatenate(flat)
    return jnp.pad(v, (0, SMALL_ROWS * LANES - v.shape[0])).reshape(SMALL_ROWS, LANES)


def kernel(x, norm_mix_pre_w, w_in, b_gate, conv_w, conv_b, dt_bias, a_log, d_skip, ssm_norm_w, w_att_proj, w_ssm_proj, w_out, norm_mix_post_w, norm_ffn_pre_w, w_up, w_down, norm_ffn_post_w, loss_target, m_norm_mix_pre_w, m_w_in, m_b_gate, m_conv_w, m_conv_b, m_dt_bias, m_a_log, m_d_skip, m_ssm_norm_w, m_w_att_proj, m_w_ssm_proj, m_w_out, m_norm_mix_post_w, m_norm_ffn_pre_w, m_w_up, m_w_down, m_norm_ffn_post_w, v_norm_mix_pre_w, v_w_in, v_b_gate, v_conv_w, v_conv_b, v_dt_bias, v_a_log, v_d_skip, v_ssm_norm_w, v_w_att_proj, v_w_ssm_proj, v_w_out, v_norm_mix_post_w, v_norm_ffn_pre_w, v_w_up, v_w_down, v_norm_ffn_post_w):
    args = locals()

    def strip(a):
        return a[0] if a.ndim == 3 else a

    wts = {n: strip(args[n]) for n in WEIGHTS}
    mom = {n: strip(args["m_" + n]) for n in WEIGHTS}
    var = {n: strip(args["v_" + n]) for n in WEIGHTS}
    xi, yi, ci = _mesh_pos()
    chip = 2 * xi + yi

    mine = [wts["w_in"].astype(BF16), _stack_rest(wts, BF16)]
    got = [lax.dynamic_update_slice(g_, m_[None], (chip, 0, 0)) for g_, m_ in zip(_allgather2(mine), mine)]
    full = {"w_in": _full_from_shards("w_in", got[0])}
    for n, sh in _unstack_rest(got[1], (N_CHIPS,)).items():
        full[n] = _full_from_shards(n, sh)
    cw_cols = CONV_DIM // N_CHIPS
    conv_slab = lax.dynamic_update_slice(jnp.zeros((SSM_CONV, CONV_DIM), F32),
                                         jnp.where(ci == 0, wts["conv_w"], 0.0), (0, chip * cw_cols))
    small_in = jnp.pad(conv_slab.reshape(-1), (0, SMALL_ROWS * LANES - SSM_CONV * CONV_DIM))
    conv_full = _allreduce_small(small_in.reshape(SMALL_ROWS, LANES), "gather_conv_w")
    full["conv_w"] = conv_full.reshape(-1)[:SSM_CONV * CONV_DIM].reshape(SSM_CONV, CONV_DIM)
    for n in SMALL:
        full[n] = wts[n]

    loss_part, grad_x, g = _local_step(x[0], loss_target[0], full)
    loss = lax.psum(loss_part[0, 0], ("x", "y", "c"))

    gs = [_shards_from_full("w_in", g["w_in"]),
          jnp.concatenate([_shards_from_full(n, g[n]).reshape(N_CHIPS, -1, PACK_COLS) for n in REST], axis=1)]
    c_arr = ci.reshape(1).astype(jnp.int32)
    chip_arr = chip.reshape(1).astype(jnp.int32)
    recv = _exchange_halves2(gs)
    ps = [_pair_add2(g_, r_, c_arr, f"rs_pair_add_{i}") for i, (g_, r_) in enumerate(zip(gs, recv))]
    recv3 = _exchange_chips2(ps)
    qs = [_chip_add2(p_, r_, chip_arr, f"rs_chip_add_{i}") for i, (p_, r_) in enumerate(zip(ps, recv3))]
    others = _share_halves2(qs)
    south = ci == 0
    whole = [jnp.concatenate([jnp.where(south, q_, o_), jnp.where(south, o_, q_)], axis=0)
             for q_, o_ in zip(qs, others)]
    gshard = {"w_in": whole[0], **_unstack_rest(whole[1])}
    small_sum = _allreduce_small(_flat_small(g, g["conv_w"]), "allreduce_small_grads").reshape(-1)
    grads, off = {}, 0
    for n in SMALL:
        sz = wts[n].size
        grads[n] = small_sum[off:off + sz].reshape(wts[n].shape)
        off += sz
    conv_g = small_sum[off:off + SSM_CONV * CONV_DIM].reshape(SSM_CONV, CONV_DIM)
    grads["conv_w"] = lax.dynamic_slice(conv_g, (0, chip * cw_cols), (SSM_CONV, cw_cols))
    grads.update(gshard)

    delta, new_m, new_v = {}, {}, {}
    for n in BIG:
        delta[n], new_m[n], new_v[n] = _adamw(wts[n], grads[n], mom[n], var[n], f"adamw_{n}")
    small_names = SMALL + ("conv_w",)

    def pack_small(d):
        v = jnp.concatenate([d[n].reshape(-1) for n in small_names])
        rows = -(-v.shape[0] // (8 * LANES)) * 8
        return jnp.pad(v, (0, rows * LANES - v.shape[0])).reshape(rows, LANES)

    ds, ms, vs = _adamw(pack_small(wts), pack_small(grads), pack_small(mom), pack_small(var), "adamw_small")
    off = 0
    for n in small_names:
        sz = wts[n].size
        for dst, src in ((delta, ds), (new_m, ms), (new_v, vs)):
            dst[n] = src.reshape(-1)[off:off + sz].reshape(wts[n].shape)
        off += sz

    out = [loss, grad_x[None]]
    for d in (grads, delta, new_m, new_v):
        out += [d[n][None] if args[n].ndim == 3 else d[n] for n in WEIGHTS]
    return tuple(out)
```

```python
import functools
import math

import numpy as np
import jax
import jax.numpy as jnp
from jax import lax
from jax.experimental import pallas as pl
from jax.experimental.pallas import tpu as pltpu

F32 = jnp.float32
BF16 = jnp.bfloat16

D_MODEL = 1024
HEAD_DIM = 64
N_ATT_HEADS = 12
ATT_WIDTH = N_ATT_HEADS * HEAD_DIM
DILATIONS = (1, 4, 16)
ATT_BLOCK = 128
SSM_INNER = 2048
SSM_HEADS = 32
SSM_GROUPS = 8
HEADS_PER_GROUP = SSM_HEADS // SSM_GROUPS
SSM_HEAD_DIM = 64
SSM_STATE = 128
SSM_CONV = 4
SSM_CHUNK = 128
CONV_DIM = SSM_INNER + 2 * SSM_GROUPS * SSM_STATE
FFN_HIDDEN = 4 * D_MODEL
IN_SPLITS = (ATT_WIDTH, ATT_WIDTH, ATT_WIDTH, SSM_INNER, CONV_DIM, SSM_HEADS, 2 * D_MODEL)
IN_PROJ_WIDTH = sum(IN_SPLITS)
RMS_EPS = 1e-6
LANES = 128
NEG_BIG = -1e30

ADAM_LR = 0.001
ADAM_B1 = 0.9
ADAM_B2 = 0.999
ADAM_EPS = 1e-08
ADAM_WD = 0.01
ADAM_STEP = 10

N_CHIPS = 4
N_DEV = 8
VMEM_LIMIT = 56 * 1024 * 1024
MESH = pl.DeviceIdType.MESH


def _alibi_slopes(n):
    def pow2(m):
        start = 2.0 ** (-8.0 / m)
        return [start ** (i + 1) for i in range(m)]
    if (n & (n - 1)) == 0:
        s = pow2(n)
    else:
        c = 2 ** int(math.floor(math.log2(n)))
        s = pow2(c) + pow2(2 * c)[0::2][: n - c]
    return [float(v) for v in np.array(s, dtype=np.float32)]


def _params(sem):
    return pltpu.CompilerParams(dimension_semantics=sem, vmem_limit_bytes=VMEM_LIMIT)


def _dot(a, b):
    return lax.dot_general(a, b, (((1,), (0,)), ((), ())), preferred_element_type=F32)


def _dot_nt(a, b):
    return lax.dot_general(a, b, (((1,), (1,)), ((), ())), preferred_element_type=F32)


def _dot_tn(a, b):
    return lax.dot_general(a, b, (((0,), (0,)), ((), ())), preferred_element_type=F32)


def _dot_hi(a, b):
    return lax.dot_general(a, b, (((1,), (0,)), ((), ())), preferred_element_type=F32,
                           precision=lax.Precision.HIGHEST)


def _dot_tn_hi(a, b):
    return lax.dot_general(a, b, (((0,), (0,)), ((), ())), preferred_element_type=F32,
                           precision=lax.Precision.HIGHEST)


def _b(x):
    return x.astype(BF16)


def _sigmoid(x):
    return 1.0 / (1.0 + jnp.exp(-x))


def _pick(n, cands):
    for c in cands:
        if n % c == 0:
            return c
    raise ValueError(f"no tile for {n}")


class _Comm:
    def __init__(self, plan, ins, outs, n_sems):
        self.start, self.finish = plan
        self.ins, self.outs, self.n_sems = list(ins), list(outs), n_sems


def _mm_nn(a, b, out_dtype, name, acc=None, mode=None, extra=None, comm=None):
    M, K = a.shape
    _, N = b.shape
    tm = 1024 if M % 1024 == 0 else 512
    tn = _pick(N, (1024, 768, 512, 256, 128))
    tk = K if K <= 2304 else _pick(K, (2048, 1024))
    nk = K // tk
    nj, ni = N // tn, M // tm
    side = acc if acc is not None else extra
    n_out = 2 if mode == "relu2" else 1
    n_in = 2 + (side is not None)
    n_ci = len(comm.ins) if comm else 0
    n_co = len(comm.outs) if comm else 0

    def body(*refs):
        a_ref, b_ref = refs[0], refs[1]
        s_ref = refs[2] if side is not None else None
        o_refs = refs[n_in + n_ci:n_in + n_ci + n_out]
        if comm:
            c_args = (refs[n_in:n_in + n_ci], refs[n_in + n_ci + n_out:n_in + n_ci + n_out + n_co],
                      refs[-2], refs[-1])
            pj, pi, pk = pl.program_id(0), pl.program_id(1), pl.program_id(2)

            @pl.when(jnp.logical_and(jnp.logical_and(pj == 0, pi == 0), pk == 0))
            def _():
                comm.start(*c_args)

        def finish(r):
            if mode == "relu2":
                r = jnp.maximum(r, 0.0)
                o_refs[0][...] = _b(r)
                o_refs[1][...] = _b(r * r)
            elif mode == "mul2":
                o_refs[0][...] = _b(r * (2.0 * s_ref[...].astype(F32)))
            else:
                if acc is not None:
                    r = r + s_ref[...]
                o_refs[0][...] = r.astype(out_dtype)

        part = _dot(_b(a_ref[...]), _b(b_ref[...]))
        if nk == 1:
            finish(part)
        else:
            acc_ref = refs[n_in + n_ci + n_out + n_co]
            k = pl.program_id(2)

            @pl.when(k == 0)
            def _():
                acc_ref[...] = part

            @pl.when(jnp.logical_and(k > 0, k < nk - 1))
            def _():
                acc_ref[...] += part

            @pl.when(k == nk - 1)
            def _():
                finish(acc_ref[...] + part)

        if comm:
            @pl.when(jnp.logical_and(jnp.logical_and(pj == nj - 1, pi == ni - 1), pk == nk - 1))
            def _():
                comm.finish(*c_args)

    tile = pl.BlockSpec((tm, tn), lambda j, i, k: (i, j))
    in_specs = [pl.BlockSpec((tm, tk), lambda j, i, k: (i, k)),
                pl.BlockSpec((tk, tn), lambda j, i, k: (k, j))]
    args = [a, b]
    if side is not None:
        in_specs.append(tile)
        args.append(side)
    odt = BF16 if mode in ("relu2", "mul2") else out_dtype
    scratch = [pltpu.VMEM((tm, tn), F32)] if nk > 1 else []
    if comm:
        scratch += [pltpu.SemaphoreType.DMA((comm.n_sems,))] * 2
        params = pltpu.CompilerParams(dimension_semantics=("arbitrary",) * 3, vmem_limit_bytes=VMEM_LIMIT,
                                      has_side_effects=True)
    else:
        params = _params(("parallel", "parallel", "arbitrary"))
    outs = pl.pallas_call(
        body, name=name, grid=(nj, ni, nk),
        in_specs=in_specs + [ANY] * n_ci,
        out_specs=[tile] * n_out + [ANY] * n_co,
        out_shape=[jax.ShapeDtypeStruct((M, N), odt)] * n_out + list(comm.outs if comm else []),
        scratch_shapes=scratch,
        compiler_params=params,
    )(*args, *(comm.ins if comm else []))
    res = outs[:n_out] if n_out > 1 else outs[0]
    return (res, outs[n_out:]) if comm else res


def _mm_tn(a, b, name):
    S, Ka = a.shape
    _, N = b.shape
    tka = _pick(Ka, (1024, 768, 512))
    tn = _pick(N, (1024, 768, 512, 256, 128))
    ts = 1024 if S % 1024 == 0 else 512
    ns = S // ts

    def body(a_ref, b_ref, o_ref, acc_ref):
        s = pl.program_id(2)
        part = _dot_tn(_b(a_ref[...]), _b(b_ref[...]))

        @pl.when(s == 0)
        def _():
            acc_ref[...] = part

        @pl.when(s > 0)
        def _():
            acc_ref[...] += part

        @pl.when(s == ns - 1)
        def _():
            o_ref[...] = acc_ref[...]

    return pl.pallas_call(
        body, name=name, grid=(Ka // tka, N // tn, ns),
        in_specs=[pl.BlockSpec((ts, tka), lambda i, j, s: (s, i)),
                  pl.BlockSpec((ts, tn), lambda i, j, s: (s, j))],
        out_specs=pl.BlockSpec((tka, tn), lambda i, j, s: (i, j)),
        out_shape=jax.ShapeDtypeStruct((Ka, N), F32),
        scratch_shapes=[pltpu.VMEM((tka, tn), F32)],
        compiler_params=_params(("parallel", "parallel", "arbitrary")),
    )(a, b)


def _row_call(body, row_ins, full_ins, row_outs, acc_outs, bs, name):
    S = row_ins[0].shape[0]
    assert S % bs == 0
    in_specs = [pl.BlockSpec((bs, a.shape[1]), lambda i: (i, 0)) for a in row_ins]
    in_specs += [pl.BlockSpec(a.shape, lambda i: (0, 0)) for a in full_ins]
    out_specs = [pl.BlockSpec((bs, c), lambda i: (i, 0)) for c, _ in row_outs]
    out_specs += [pl.BlockSpec(s, lambda i: (0, 0)) for s in acc_outs]
    out_shape = [jax.ShapeDtypeStruct((S, c), dt) for c, dt in row_outs]
    out_shape += [jax.ShapeDtypeStruct(s, F32) for s in acc_outs]
    return pl.pallas_call(
        body, name=name, grid=(S // bs,), in_specs=in_specs, out_specs=out_specs, out_shape=out_shape,
        compiler_params=_params(("arbitrary",)),
    )(*row_ins, *full_ins)


def _rms_vals(x, w):
    r = lax.rsqrt(jnp.mean(x * x, axis=-1, keepdims=True) + RMS_EPS)
    return x * r * w


def _rms_bwd_vals(x, w, dy):
    r = lax.rsqrt(jnp.mean(x * x, axis=-1, keepdims=True) + RMS_EPS)
    xn = x * r
    g = dy * w
    dx = r * (g - xn * jnp.mean(g * xn, axis=-1, keepdims=True))
    dw = jnp.sum(dy * xn, axis=0, keepdims=True)
    return dx, dw


def _acc_add(ref, val):
    @pl.when(pl.program_id(0) == 0)
    def _():
        ref[...] = val

    @pl.when(pl.program_id(0) > 0)
    def _():
        ref[...] += val


def _rms_fwd(x, w):
    def body(x_ref, w_ref, o_ref):
        o_ref[...] = _b(_rms_vals(x_ref[...], w_ref[...]))
    return _row_call(body, [x], [w], [(x.shape[1], BF16)], [], 512, "rms_fwd")[0]


def _gate_fwd(att_o, ssm_o, gl, b_gate):
    def body(a_ref, s_ref, g_ref, b_ref, o_ref):
        g = _sigmoid(g_ref[...] + b_ref[...])
        o_ref[...] = _b(g[:, :D_MODEL] * a_ref[...] + g[:, D_MODEL:] * s_ref[...])
    return _row_call(body, [att_o, ssm_o, gl], [b_gate], [(D_MODEL, BF16)], [], 512, "gate_fwd")[0]


def _post_pre(x, mixed, w_post, w_pre):
    def body(x_ref, m_ref, wp_ref, wn_ref, h_ref, f_ref):
        h = x_ref[...] + _rms_vals(m_ref[...], wp_ref[...])
        h_ref[...] = h
        f_ref[...] = _b(_rms_vals(h, wn_ref[...]))
    return _row_call(body, [x, mixed], [w_post, w_pre], [(D_MODEL, F32), (D_MODEL, BF16)], [], 512,
                     "post_pre")


def _relu2(up):
    def body(u_ref, o_ref):
        r = jnp.maximum(u_ref[...], 0.0)
        o_ref[...] = _b(r * r)
    return _row_call(body, [up], [], [(up.shape[1], BF16)], [], 256, "relu2")[0]


def _final(h1, down, w_post, target):
    def body(h_ref, d_ref, t_ref, w_ref, dh_ref, dd_ref, loss_ref, dw_ref):
        dn = d_ref[...]
        w = w_ref[...]
        err = h_ref[...] + _rms_vals(dn, w) - t_ref[...]
        row = jnp.mean(err * err, axis=-1, keepdims=True)
        part = 0.5 * jnp.sum(row, axis=0, keepdims=True)
        dh = err * (1.0 / D_MODEL)
        dh_ref[...] = dh
        dx, dw = _rms_bwd_vals(dn, w, dh)
        dd_ref[...] = _b(dx)
        _acc_add(loss_ref, jnp.broadcast_to(part, (1, LANES)))
        _acc_add(dw_ref, dw)
    return _row_call(body, [h1, down, target], [w_post], [(D_MODEL, F32), (D_MODEL, BF16)],
                     [(1, LANES), (1, D_MODEL)], 512, "final_loss")


def _dup(da, up):
    def body(a_ref, u_ref, o_ref):
        o_ref[...] = _b(a_ref[...] * (2.0 * jnp.maximum(u_ref[...], 0.0)))
    return _row_call(body, [da, up], [], [(up.shape[1], BF16)], [], 256, "relu2_bwd")[0]


def _mid_bwd(dh2, df, h1, mixed, w_pre, w_post):
    def body(dh_ref, df_ref, h_ref, m_ref, wn_ref, wp_ref, dh1_ref, dm_ref, dwn_ref, dwp_ref):
        dx, dwn = _rms_bwd_vals(h_ref[...], wn_ref[...], df_ref[...])
        dh1 = dh_ref[...] + dx
        dh1_ref[...] = dh1
        dm, dwp = _rms_bwd_vals(m_ref[...], wp_ref[...], dh1)
        dm_ref[...] = _b(dm)
        _acc_add(dwn_ref, dwn)
        _acc_add(dwp_ref, dwp)
    return _row_call(body, [dh2, df, h1, mixed], [w_pre, w_post], [(D_MODEL, F32), (D_MODEL, BF16)],
                     [(1, D_MODEL), (1, D_MODEL)], 512, "mid_bwd")


def _gate_bwd(dmi, att_o, ssm_o, gl, b_gate):
    def body(d_ref, a_ref, s_ref, g_ref, b_ref, da_ref, ds_ref, dg_ref, db_ref):
        g = _sigmoid(g_ref[...] + b_ref[...])
        d = d_ref[...]
        ga, gs = g[:, :D_MODEL], g[:, D_MODEL:]
        da_ref[...] = _b(ga * d)
        ds_ref[...] = _b(gs * d)
        dga = d * a_ref[...] * ga * (1.0 - ga)
        dgs = d * s_ref[...] * gs * (1.0 - gs)
        dg_ref[:, :D_MODEL] = _b(dga)
        dg_ref[:, D_MODEL:] = _b(dgs)
        _acc_add(db_ref.at[:, pl.ds(0, D_MODEL)], jnp.sum(dga, axis=0, keepdims=True))
        _acc_add(db_ref.at[:, pl.ds(D_MODEL, D_MODEL)], jnp.sum(dgs, axis=0, keepdims=True))
    return _row_call(body, [dmi, att_o, ssm_o, gl], [b_gate],
                     [(D_MODEL, BF16), (D_MODEL, BF16), (2 * D_MODEL, BF16)], [(1, 2 * D_MODEL)], 256,
                     "gate_bwd")


def _first_bwd(dh1, du, x, w_pre):
    def body(dh_ref, du_ref, x_ref, w_ref, dx_ref, dw_ref):
        dx, dw = _rms_bwd_vals(x_ref[...], w_ref[...], du_ref[...])
        dx_ref[...] = dh_ref[...] + dx
        _acc_add(dw_ref, dw)
    return _row_call(body, [dh1, du, x], [w_pre], [(D_MODEL, F32)], [(1, D_MODEL)], 512, "first_bwd")


def _group_rms(t):
    gw = SSM_INNER // SSM_GROUPS
    out = []
    for g in range(SSM_GROUPS):
        tg = t[:, g * gw:(g + 1) * gw]
        out.append(lax.rsqrt(jnp.mean(tg * tg, axis=-1, keepdims=True) + RMS_EPS))
    return out


def _gnorm_fwd(y, z, w):
    gw = SSM_INNER // SSM_GROUPS

    def body(y_ref, z_ref, w_ref, o_ref):
        zz = z_ref[...]
        t = y_ref[...] * (zz * _sigmoid(zz))
        rs = _group_rms(t)
        for g in range(SSM_GROUPS):
            sl = slice(g * gw, (g + 1) * gw)
            o_ref[:, sl] = _b(t[:, sl] * rs[g] * w_ref[:, sl])
    return _row_call(body, [y, z], [w], [(SSM_INNER, BF16)], [], 256, "gnorm_fwd")[0]


def _gnorm_bwd(dout, y, z, w):
    gw = SSM_INNER // SSM_GROUPS

    def body(d_ref, y_ref, z_ref, w_ref, dy_ref, dz_ref, dw_ref):
        zz = z_ref[...]
        yy = y_ref[...]
        sg = _sigmoid(zz)
        sz = zz * sg
        t = yy * sz
        rs = _group_rms(t)
        for g in range(SSM_GROUPS):
            sl = slice(g * gw, (g + 1) * gw)
            tn = t[:, sl] * rs[g]
            d = d_ref[:, sl]
            gg = d * w_ref[:, sl]
            dt = rs[g] * (gg - tn * jnp.mean(gg * tn, axis=-1, keepdims=True))
            dy_ref[:, sl] = dt * sz[:, sl]
            dz_ref[:, sl] = _b(dt * yy[:, sl] * (sg[:, sl] * (1.0 + zz[:, sl] * (1.0 - sg[:, sl]))))
            _acc_add(dw_ref.at[:, pl.ds(g * gw, gw)], jnp.sum(d * tn, axis=0, keepdims=True))
    return _row_call(body, [dout, y, z], [w], [(SSM_INNER, F32), (SSM_INNER, BF16)], [(1, SSM_INNER)], 256,
                     "gnorm_bwd")


def _to_pat(a, d):
    if d == 1:
        return a
    S, C = a.shape
    return a.reshape(S // d, d, C).transpose(1, 0, 2).reshape(S, C)


def _from_pat(a, d):
    if d == 1:
        return a
    S, C = a.shape
    return a.reshape(d, S // d, C).transpose(1, 0, 2).reshape(S, C)


def _head_col(stat, h):
    return stat[:, h:h + 1]


def _attn_fwd(q, k, v, d):
    S = q.shape[0]
    blk = ATT_BLOCK
    nblk = S // blk
    nbs = nblk // d
    slopes = _alibi_slopes(N_ATT_HEADS)
    scale = HEAD_DIM ** -0.5

    def body(q_ref, kc_ref, kp_ref, vc_ref, vp_ref, o_ref, m_ref, l_ref):
        n = pl.program_id(0)
        has_prev = (n % nbs) != 0
        ii = lax.broadcasted_iota(jnp.int32, (blk, blk), 0)
        jj = lax.broadcasted_iota(jnp.int32, (blk, blk), 1)
        dist_c = (ii - jj).astype(F32)
        dist_p = dist_c + float(blk)
        ok_c = ii >= jj
        ok_p = jnp.logical_and(jj >= ii, has_prev)
        lane = lax.broadcasted_iota(jnp.int32, (blk, LANES), 1)
        m_all = jnp.zeros((blk, LANES), F32)
        l_all = jnp.zeros((blk, LANES), F32)
        for h in range(N_ATT_HEADS):
            sl = slice(h * HEAD_DIM, (h + 1) * HEAD_DIM)
            qh = q_ref[:, sl]
            bias = slopes[h] * float(d)
            sc = jnp.where(ok_c, _dot_nt(qh, kc_ref[:, sl]) * scale - bias * dist_c, NEG_BIG)
            sp = jnp.where(ok_p, _dot_nt(qh, kp_ref[:, sl]) * scale - bias * dist_p, NEG_BIG)
            m = jnp.maximum(jnp.max(sc, axis=-1, keepdims=True), jnp.max(sp, axis=-1, keepdims=True))
            pc = jnp.exp(sc - m)
            pp = jnp.exp(sp - m)
            l = jnp.sum(pc, axis=-1, keepdims=True) + jnp.sum(pp, axis=-1, keepdims=True)
            o_ref[:, sl] = _dot(_b(pc), vc_ref[:, sl]) + _dot(_b(pp), vp_ref[:, sl])
            m_all = jnp.where(lane == h, m, m_all)
            l_all = jnp.where(lane == h, l, l_all)
        m_ref[...] = m_all
        l_ref[...] = l_all

    cur = pl.BlockSpec((blk, ATT_WIDTH), lambda n: (n, 0))
    prev = pl.BlockSpec((blk, ATT_WIDTH), lambda n: (jnp.maximum(n - 1, 0), 0))
    stat = pl.BlockSpec((blk, LANES), lambda n: (n, 0))
    return pl.pallas_call(
        body, name=f"attn_fwd_d{d}", grid=(nblk,),
        in_specs=[cur, cur, prev, cur, prev],
        out_specs=[cur, stat, stat],
        out_shape=[jax.ShapeDtypeStruct((S, ATT_WIDTH), F32), jax.ShapeDtypeStruct((S, LANES), F32),
                   jax.ShapeDtypeStruct((S, LANES), F32)],
        compiler_params=_params(("parallel",)),
    )(q, k, k, v, v)


def _attn_combine(os, ms, ls):
    def body(o1, o2, o3, m1, m2, m3, l1, l2, l3, att_ref, lse_ref):
        mm = [m1[...], m2[...], m3[...]]
        big = jnp.maximum(jnp.maximum(mm[0], mm[1]), mm[2])
        es = [jnp.exp(m - big) for m in mm]
        den = es[0] * l1[...] + es[1] * l2[...] + es[2] * l3[...]
        lse_ref[...] = big + jnp.log(den)
        inv = 1.0 / den
        for h in range(N_ATT_HEADS):
            sl = slice(h * HEAD_DIM, (h + 1) * HEAD_DIM)
            num = (_head_col(es[0], h) * o1[:, sl] + _head_col(es[1], h) * o2[:, sl]
                   + _head_col(es[2], h) * o3[:, sl])
            att_ref[:, sl] = num * _head_col(inv, h)
    return _row_call(body, list(os) + list(ms) + list(ls), [], [(ATT_WIDTH, F32), (LANES, F32)], [], 256,
                     "attn_combine")


def _attn_delta(d_att, att):
    def body(d_ref, a_ref, dl_ref, db_ref):
        dd = d_ref[...]
        prod = dd * a_ref[...]
        lane = lax.broadcasted_iota(jnp.int32, (dd.shape[0], LANES), 1)
        acc = jnp.zeros((dd.shape[0], LANES), F32)
        for h in range(N_ATT_HEADS):
            s = jnp.sum(prod[:, h * HEAD_DIM:(h + 1) * HEAD_DIM], axis=-1, keepdims=True)
            acc = jnp.where(lane == h, s, acc)
        dl_ref[...] = acc
        db_ref[...] = _b(dd)
    return _row_call(body, [d_att, att], [], [(LANES, F32), (ATT_WIDTH, BF16)], [], 512, "attn_delta")


def _attn_bwd(q, k, v, do, lse, delta, d):
    S = q.shape[0]
    blk = ATT_BLOCK
    nblk = S // blk
    nbs = nblk // d
    slopes = _alibi_slopes(N_ATT_HEADS)
    scale = HEAD_DIM ** -0.5

    def body(qc_ref, qn_ref, k_ref, v_ref, doc_ref, don_ref, lc_ref, ln_ref, dc_ref, dn_ref,
             dq_ref, dk_ref, dv_ref, carry_ref):
        n = pl.program_id(0)
        has_next = ((n + 1) % nbs) != 0

        @pl.when(n == 0)
        def _():
            carry_ref[...] = jnp.zeros_like(carry_ref)

        ii = lax.broadcasted_iota(jnp.int32, (blk, blk), 0)
        jj = lax.broadcasted_iota(jnp.int32, (blk, blk), 1)
        dist_c = (ii - jj).astype(F32)
        dist_p = dist_c + float(blk)
        ok_c = ii >= jj
        ok_p = jnp.logical_and(jj >= ii, has_next)
        for h in range(N_ATT_HEADS):
            sl = slice(h * HEAD_DIM, (h + 1) * HEAD_DIM)
            bias = slopes[h] * float(d)
            kh = k_ref[:, sl]
            vh = v_ref[:, sl]
            qh = qc_ref[:, sl]
            doh = doc_ref[:, sl]
            s = jnp.where(ok_c, _dot_nt(qh, kh) * scale - bias * dist_c - _head_col(lc_ref[...], h), NEG_BIG)
            p = jnp.exp(s)
            ds = p * (_dot_nt(doh, vh) - _head_col(dc_ref[...], h)) * scale
            pb, dsb = _b(p), _b(ds)
            dv = _dot_tn(pb, doh)
            dk = _dot_tn(dsb, qh)
            dq_ref[:, sl] = _dot(dsb, kh) + carry_ref[:, sl]
            qh = qn_ref[:, sl]
            doh = don_ref[:, sl]
            s = jnp.where(ok_p, _dot_nt(qh, kh) * scale - bias * dist_p - _head_col(ln_ref[...], h), NEG_BIG)
            p = jnp.exp(s)
            ds = p * (_dot_nt(doh, vh) - _head_col(dn_ref[...], h)) * scale
            pb, dsb = _b(p), _b(ds)
            dv_ref[:, sl] = dv + _dot_tn(pb, doh)
            dk_ref[:, sl] = dk + _dot_tn(dsb, qh)
            carry_ref[:, sl] = _dot(dsb, kh)

    cur = pl.BlockSpec((blk, ATT_WIDTH), lambda n: (n, 0))
    nxt = pl.BlockSpec((blk, ATT_WIDTH), lambda n: (jnp.minimum(n + 1, nblk - 1), 0))
    scur = pl.BlockSpec((blk, LANES), lambda n: (n, 0))
    snxt = pl.BlockSpec((blk, LANES), lambda n: (jnp.minimum(n + 1, nblk - 1), 0))
    shp = jax.ShapeDtypeStruct((S, ATT_WIDTH), F32)
    return pl.pallas_call(
        body, name=f"attn_bwd_d{d}", grid=(nblk,),
        in_specs=[cur, nxt, cur, cur, cur, nxt, scur, snxt, scur, snxt],
        out_specs=[cur, cur, cur],
        out_shape=[shp, shp, shp],
        scratch_shapes=[pltpu.VMEM((blk, ATT_WIDTH), F32)],
        compiler_params=_params(("arbitrary",)),
    )(q, q, k, v, do, do, lse, lse, delta, delta)


def _head_pair_masks(x):
    lane = lax.broadcasted_iota(jnp.int32, x.shape, 1)
    zero = jnp.zeros_like(x)
    return jnp.where(lane < HEAD_DIM, x, zero), jnp.where(lane >= HEAD_DIM, x, zero)


def _attn_fwd2(qkv, d):
    S = qkv.shape[0]
    blk = ATT_BLOCK
    nblk = S // blk
    nbs = nblk // d
    slopes = _alibi_slopes(N_ATT_HEADS)
    scale = HEAD_DIM ** -0.5

    def body(q_ref, kc_ref, kp_ref, vc_ref, vp_ref, o_ref, m_ref, l_ref):
        n = pl.program_id(0)
        has_prev = (n % nbs) != 0
        ii = lax.broadcasted_iota(jnp.int32, (blk, 2 * blk), 0)
        jj = lax.broadcasted_iota(jnp.int32, (blk, 2 * blk), 1)
        dist_i = blk + ii - jj
        dist = dist_i.astype(F32)
        ok = jnp.logical_and(jnp.logical_and(dist_i >= 0, dist_i <= blk), jnp.logical_or(jj >= blk, has_prev))
        lane = lax.broadcasted_iota(jnp.int32, (blk, LANES), 1)
        m_all = jnp.zeros((blk, LANES), F32)
        l_all = jnp.zeros((blk, LANES), F32)
        for pr in range(N_ATT_HEADS // 2):
            sl = slice(pr * LANES, (pr + 1) * LANES)
            kcat = jnp.concatenate([kp_ref[:, sl], kc_ref[:, sl]], axis=0)
            vcat = jnp.concatenate([vp_ref[:, sl], vc_ref[:, sl]], axis=0)
            ps = []
            for h, qh in zip((2 * pr, 2 * pr + 1), _head_pair_masks(q_ref[:, sl])):
                s = jnp.where(ok, _dot_nt(qh, kcat) * scale - (slopes[h] * float(d)) * dist, NEG_BIG)
                m = jnp.max(s, axis=-1, keepdims=True)
                p = jnp.exp(s - m)
                l = jnp.sum(p, axis=-1, keepdims=True)
                m_all = jnp.where(lane == h, m, m_all)
                l_all = jnp.where(lane == h, l, l_all)
                ps.append(_b(p))
            o_ref[:, sl] = _dot(jnp.concatenate(ps, axis=1), jnp.concatenate(_head_pair_masks(vcat), axis=0))
        m_ref[...] = m_all
        l_ref[...] = l_all

    cur = lambda c: pl.BlockSpec((blk, ATT_WIDTH), lambda n: (n, c))
    prev = lambda c: pl.BlockSpec((blk, ATT_WIDTH), lambda n: (jnp.maximum(n - 1, 0), c))
    stat = pl.BlockSpec((blk, LANES), lambda n: (n, 0))
    return pl.pallas_call(
        body, name=f"attn_fwd_d{d}", grid=(nblk,),
        in_specs=[cur(0), cur(1), prev(1), cur(2), prev(2)],
        out_specs=[cur(0), stat, stat],
        out_shape=[jax.ShapeDtypeStruct((S, ATT_WIDTH), F32), jax.ShapeDtypeStruct((S, LANES), F32),
                   jax.ShapeDtypeStruct((S, LANES), F32)],
        compiler_params=_params(("parallel",)),
    )(qkv, qkv, qkv, qkv, qkv)


def _attn_bwd2(qkv, do, lse, delta, d, comm=None):
    S = qkv.shape[0]
    blk = ATT_BLOCK
    nblk = S // blk
    nbs = nblk // d
    slopes = _alibi_slopes(N_ATT_HEADS)
    scale = HEAD_DIM ** -0.5
    n_ci = len(comm.ins) if comm else 0
    n_co = len(comm.outs) if comm else 0

    def body(*refs):
        qc_ref, qn_ref, k_ref, v_ref, doc_ref, don_ref, lc_ref, ln_ref, dc_ref, dn_ref = refs[:10]
        dq_ref, dk_ref, dv_ref = refs[10 + n_ci:13 + n_ci]
        carry_ref = refs[13 + n_ci + n_co]
        n = pl.program_id(0)
        has_next = ((n + 1) % nbs) != 0
        if comm:
            c_args = (refs[10:10 + n_ci], refs[13 + n_ci:13 + n_ci + n_co], refs[-2], refs[-1])

        @pl.when(n == 0)
        def _():
            carry_ref[...] = jnp.zeros_like(carry_ref)
            if comm:
                comm.start(*c_args)

        rr = lax.broadcasted_iota(jnp.int32, (2 * blk, blk), 0)
        jj = lax.broadcasted_iota(jnp.int32, (2 * blk, blk), 1)
        dist_i = rr - jj
        dist = dist_i.astype(F32)
        ok = jnp.logical_or(jnp.logical_and(rr < blk, dist_i >= 0),
                            jnp.logical_and(jnp.logical_and(rr >= blk, dist_i <= blk), has_next))
        lcat = jnp.concatenate([lc_ref[...], ln_ref[...]], axis=0)
        dcat = jnp.concatenate([dc_ref[...], dn_ref[...]], axis=0)
        for pr in range(N_ATT_HEADS // 2):
            sl = slice(pr * LANES, (pr + 1) * LANES)
            qcat = jnp.concatenate([qc_ref[:, sl], qn_ref[:, sl]], axis=0)
            docat = jnp.concatenate([doc_ref[:, sl], don_ref[:, sl]], axis=0)
            k2 = k_ref[:, sl]
            v2 = v_ref[:, sl]
            qm = _head_pair_masks(qcat)
            dom = _head_pair_masks(docat)
            pbs, dsbs = [], []
            for h, qh, doh in zip((2 * pr, 2 * pr + 1), qm, dom):
                s = jnp.where(ok, _dot_nt(qh, k2) * scale - (slopes[h] * float(d)) * dist - lcat[:, h:h + 1],
                              NEG_BIG)
                p = jnp.exp(s)
                ds = p * (_dot_nt(doh, v2) - dcat[:, h:h + 1]) * scale
                pbs.append(_b(p))
                dsbs.append(_b(ds))
            dv_ref[:, sl] = _dot_tn(jnp.concatenate(pbs, axis=0), jnp.concatenate(dom, axis=0))
            dk_ref[:, sl] = _dot_tn(jnp.concatenate(dsbs, axis=0), jnp.concatenate(qm, axis=0))
            dq = _dot(jnp.concatenate(dsbs, axis=1), jnp.concatenate(_head_pair_masks(k2), axis=0))
            dq_ref[:, sl] = dq[:blk] + carry_ref[:, sl]
            carry_ref[:, sl] = dq[blk:]

        if comm:
            @pl.when(n == nblk - 1)
            def _():
                comm.finish(*c_args)

    cur = lambda c: pl.BlockSpec((blk, ATT_WIDTH), lambda n: (n, c))
    nxt = lambda c: pl.BlockSpec((blk, ATT_WIDTH), lambda n: (jnp.minimum(n + 1, nblk - 1), c))
    scur = pl.BlockSpec((blk, LANES), lambda n: (n, 0))
    snxt = pl.BlockSpec((blk, LANES), lambda n: (jnp.minimum(n + 1, nblk - 1), 0))
    shp = jax.ShapeDtypeStruct((S, ATT_WIDTH), F32)
    scratch = [pltpu.VMEM((blk, ATT_WIDTH), F32)]
    if comm:
        scratch += [pltpu.SemaphoreType.DMA((comm.n_sems,))] * 2
        params = pltpu.CompilerParams(dimension_semantics=("arbitrary",), vmem_limit_bytes=VMEM_LIMIT,
                                      has_side_effects=True)
    else:
        params = _params(("arbitrary",))
    outs = pl.pallas_call(
        body, name=f"attn_bwd_d{d}", grid=(nblk,),
        in_specs=[cur(0), nxt(0), cur(1), cur(2), cur(0), nxt(0), scur, snxt, scur, snxt] + [ANY] * n_ci,
        out_specs=[cur(0), cur(0), cur(0)] + [ANY] * n_co,
        out_shape=[shp, shp, shp] + list(comm.outs if comm else []),
        scratch_shapes=scratch,
        compiler_params=params,
    )(qkv, qkv, qkv, qkv, do, do, lse, lse, delta, delta, *(comm.ins if comm else []))
    return (outs[0], outs[1], outs[2], outs[3:]) if comm else outs


LAYOUT_TILE = 512
DILATED = tuple(d for d in DILATIONS if d > 1)


def _pat_spec(d, cols, col_block=0):
    return pl.BlockSpec((d, LAYOUT_TILE // d, cols), lambda i: (0, i, col_block))


def _pat_view(a, d):
    return a.reshape(d, a.shape[0] // d, a.shape[1])


def _qkv_layouts(qkv):
    S, C = qkv.shape
    t = LAYOUT_TILE

    def body(x_ref, nat_ref, *refs):
        pat_refs, slab = refs[:-1], refs[-1]
        nat_ref[...] = _b(x_ref[...])
        _to_slabs(slab, x_ref)
        for d, p_ref in zip(DILATED, pat_refs):
            _gather_pattern(p_ref, slab, d, BF16)

    outs = pl.pallas_call(
        body, name="qkv_layouts", grid=(S // t,),
        in_specs=[pl.BlockSpec((t, C), lambda i: (i, 0))],
        out_specs=[pl.BlockSpec((t, C), lambda i: (i, 0))] + [_pat_spec(d, C) for d in DILATED],
        out_shape=[jax.ShapeDtypeStruct((S, C), BF16)]
        + [jax.ShapeDtypeStruct((d, S // d, C), BF16) for d in DILATED],
        scratch_shapes=[pltpu.VMEM((C // LANES, t, LANES), F32)],
        compiler_params=_params(("parallel",)),
    )(qkv)
    return [outs[0]] + [o.reshape(S, C) for o in outs[1:]]


def _to_slabs(slab_ref, src_ref):
    for cb in range(slab_ref.shape[0]):
        slab_ref[cb] = src_ref[:, cb * LANES:(cb + 1) * LANES].astype(F32)


def _gather_pattern(dst_ref, slab_ref, d, dtype):
    t = slab_ref.shape[1]
    for cb in range(slab_ref.shape[0]):
        one = slab_ref.at[cb]
        for r in range(d):
            dst_ref[r, :, cb * LANES:(cb + 1) * LANES] = one[pl.ds(r, t // d, stride=d), :].astype(dtype)


def _scatter_pattern(slab_ref, src_ref, d, add=False):
    t = slab_ref.shape[1]
    for cb in range(slab_ref.shape[0]):
        one = slab_ref.at[cb]
        for r in range(d):
            idx = pl.ds(r, t // d, stride=d)
            val = src_ref[r, :, cb * LANES:(cb + 1) * LANES]
            if add:
                val = val + one[idx, :]
            one[idx, :] = val


def _attn_combine2(os, ms, ls):
    S = os[0].shape[0]
    t = LAYOUT_TILE

    def body(o1, o2, o3, m1, m2, m3, l1, l2, l3, att_ref, lse_ref, so2, so3, sm2, sm3, sl2, sl3):
        for d, src, dst in ((DILATED[0], o2, so2), (DILATED[1], o3, so3), (DILATED[0], m2, sm2),
                            (DILATED[1], m3, sm3), (DILATED[0], l2, sl2), (DILATED[1], l3, sl3)):
            _scatter_pattern(dst, src, d)
        mm = [m1[...], sm2[0], sm3[0]]
        big = jnp.maximum(jnp.maximum(mm[0], mm[1]), mm[2])
        es = [jnp.exp(m - big) for m in mm]
        den = es[0] * l1[...] + es[1] * sl2[0] + es[2] * sl3[0]
        lse_ref[...] = big + jnp.log(den)
        inv = 1.0 / den
        for h in range(N_ATT_HEADS):
            sl = slice(h * HEAD_DIM, (h + 1) * HEAD_DIM)
            cb, hl = divmod(h, 2)
            sll = slice(hl * HEAD_DIM, (hl + 1) * HEAD_DIM)
            num = (_head_col(es[0], h) * o1[:, sl] + _head_col(es[1], h) * so2[cb, :, sll]
                   + _head_col(es[2], h) * so3[cb, :, sll])
            att_ref[:, sl] = num * _head_col(inv, h)

    def specs(c):
        return [pl.BlockSpec((t, c), lambda i: (i, 0))] + [_pat_spec(d, c) for d in DILATED]

    args = [os[0]] + [_pat_view(o, d) for o, d in zip(os[1:], DILATED)]
    args += [ms[0]] + [_pat_view(m, d) for m, d in zip(ms[1:], DILATED)]
    args += [ls[0]] + [_pat_view(l, d) for l, d in zip(ls[1:], DILATED)]
    return pl.pallas_call(
        body, name="attn_combine", grid=(S // t,),
        in_specs=specs(ATT_WIDTH) + specs(LANES) + specs(LANES),
        out_specs=[pl.BlockSpec((t, ATT_WIDTH), lambda i: (i, 0)), pl.BlockSpec((t, LANES), lambda i: (i, 0))],
        out_shape=[jax.ShapeDtypeStruct((S, ATT_WIDTH), F32), jax.ShapeDtypeStruct((S, LANES), F32)],
        scratch_shapes=[pltpu.VMEM((ATT_WIDTH // LANES, t, LANES), F32)] * 2
        + [pltpu.VMEM((1, t, LANES), F32)] * 4,
        compiler_params=_params(("parallel",)),
    )(*args)


def _attn_delta2(d_att, att, lse):
    S = d_att.shape[0]
    t = LAYOUT_TILE

    def body(d_ref, a_ref, l_ref, *refs):
        out_refs, d_slab, l_slab, dl_slab = refs[:-3], refs[-3], refs[-2], refs[-1]
        dd = d_ref[...]
        prod = dd * a_ref[...]
        lane = lax.broadcasted_iota(jnp.int32, (t, LANES), 1)
        acc = jnp.zeros((t, LANES), F32)
        for h in range(N_ATT_HEADS):
            s = jnp.sum(prod[:, h * HEAD_DIM:(h + 1) * HEAD_DIM], axis=-1, keepdims=True)
            acc = jnp.where(lane == h, s, acc)
        out_refs[0][...] = _b(dd)
        out_refs[1][...] = acc
        _to_slabs(d_slab, d_ref)
        l_slab[0] = l_ref[...]
        dl_slab[0] = acc
        for k, d in enumerate(DILATED):
            db_ref, ls_ref, dl_ref = out_refs[2 + 3 * k:5 + 3 * k]
            _gather_pattern(db_ref, d_slab, d, BF16)
            _gather_pattern(ls_ref, l_slab, d, F32)
            _gather_pattern(dl_ref, dl_slab, d, F32)

    nat = lambda c: pl.BlockSpec((t, c), lambda i: (i, 0))
    out_specs = [nat(ATT_WIDTH), nat(LANES)]
    out_shape = [jax.ShapeDtypeStruct((S, ATT_WIDTH), BF16), jax.ShapeDtypeStruct((S, LANES), F32)]
    for d in DILATED:
        out_specs += [_pat_spec(d, ATT_WIDTH), _pat_spec(d, LANES), _pat_spec(d, LANES)]
        out_shape += [jax.ShapeDtypeStruct((d, S // d, ATT_WIDTH), BF16),
                      jax.ShapeDtypeStruct((d, S // d, LANES), F32),
                      jax.ShapeDtypeStruct((d, S // d, LANES), F32)]
    outs = pl.pallas_call(
        body, name="attn_delta", grid=(S // t,),
        in_specs=[nat(ATT_WIDTH), nat(ATT_WIDTH), nat(LANES)],
        out_specs=out_specs, out_shape=out_shape,
        scratch_shapes=[pltpu.VMEM((ATT_WIDTH // LANES, t, LANES), F32), pltpu.VMEM((1, t, LANES), F32),
                        pltpu.VMEM((1, t, LANES), F32)],
        compiler_params=_params(("parallel",)),
    )(d_att, att, lse)
    res = [(outs[0], lse, outs[1])]
    for k in range(len(DILATED)):
        db, ls, dl = outs[2 + 3 * k:5 + 3 * k]
        res.append((db.reshape(S, ATT_WIDTH), ls.reshape(S, LANES), dl.reshape(S, LANES)))
    return res


def _sum_qkv2(dqs, dks, dvs):
    S = dqs[0].shape[0]
    t = LAYOUT_TILE

    def body(*refs):
        o_ref, scr = refs[-2], refs[-1]
        for part in range(3):
            nat_ref, p_refs = refs[3 * part], refs[3 * part + 1:3 * part + 3]
            _to_slabs(scr, nat_ref)
            for d, p_ref in zip(DILATED, p_refs):
                _scatter_pattern(scr, p_ref, d, add=True)
            for cb in range(ATT_WIDTH // LANES):
                o_ref[:, part * ATT_WIDTH + cb * LANES:part * ATT_WIDTH + (cb + 1) * LANES] = _b(scr[cb])

    in_specs, args = [], []
    for group in (dqs, dks, dvs):
        in_specs += [pl.BlockSpec((t, ATT_WIDTH), lambda i: (i, 0))] + [_pat_spec(d, ATT_WIDTH) for d in DILATED]
        args += [group[0]] + [_pat_view(a, d) for a, d in zip(group[1:], DILATED)]
    return pl.pallas_call(
        body, name="sum_dqkv", grid=(S // t,),
        in_specs=in_specs,
        out_specs=pl.BlockSpec((t, 3 * ATT_WIDTH), lambda i: (i, 0)),
        out_shape=jax.ShapeDtypeStruct((S, 3 * ATT_WIDTH), BF16),
        scratch_shapes=[pltpu.VMEM((ATT_WIDTH // LANES, t, LANES), F32)],
        compiler_params=_params(("parallel",)),
    )(*args)


def _sum_qkv(dqs, dks, dvs):
    def body(q1, q2, q3, k1, k2, k3, v1, v2, v3, o_ref):
        o_ref[:, 0:ATT_WIDTH] = _b(q1[...] + q2[...] + q3[...])
        o_ref[:, ATT_WIDTH:2 * ATT_WIDTH] = _b(k1[...] + k2[...] + k3[...])
        o_ref[:, 2 * ATT_WIDTH:] = _b(v1[...] + v2[...] + v3[...])
    return _row_call(body, list(dqs) + list(dks) + list(dvs), [], [(3 * ATT_WIDTH, BF16)], [], 256,
                     "sum_dqkv")[0]


CONV_COLS = 1024
CONV_ROWS = 512
HALO = 8


def _conv_fwd(xbc, conv_w, conv_b):
    S, C = xbc.shape
    bs, bc = CONV_ROWS, CONV_COLS
    nr = S // bs

    def body(x_ref, halo_ref, w_ref, b_ref, o_ref, xs_ref):
        r = pl.program_id(1)
        xs_ref[pl.ds(HALO, bs), :] = x_ref[...]
        xs_ref[pl.ds(0, HALO), :] = jnp.where(r > 0, halo_ref[...], 0.0)
        pre = b_ref[...] + w_ref[3:4, :] * x_ref[...]
        for j in range(SSM_CONV - 1):
            pre = pre + w_ref[j:j + 1, :] * xs_ref[pl.ds(HALO - 3 + j, bs), :]
        o_ref[...] = pre * _sigmoid(pre)

    return pl.pallas_call(
        body, name="conv_fwd", grid=(C // bc, nr),
        in_specs=[pl.BlockSpec((bs, bc), lambda c, r: (r, c)),
                  pl.BlockSpec((HALO, bc), lambda c, r: (jnp.maximum(r * (bs // HALO) - 1, 0), c)),
                  pl.BlockSpec((SSM_CONV, bc), lambda c, r: (0, c)),
                  pl.BlockSpec((1, bc), lambda c, r: (0, c))],
        out_specs=pl.BlockSpec((bs, bc), lambda c, r: (r, c)),
        out_shape=jax.ShapeDtypeStruct((S, C), F32),
        scratch_shapes=[pltpu.VMEM((bs + HALO, bc), F32)],
        compiler_params=_params(("parallel", "arbitrary")),
    )(xbc, xbc, conv_w, conv_b)


def _conv_bwd(xbc, dact, conv_w, conv_b, col0):
    S, C = xbc.shape
    Cp = dact.shape[1]
    bs, bc = CONV_ROWS, min(CONV_COLS, Cp)
    nr = S // bs
    cb0 = col0 // bc
    last_halo = S // HALO - 1

    def body(x_ref, xp_ref, xn_ref, d_ref, dn_ref, w_ref, b_ref, dx_ref, dw_ref, db_ref,
             xs_ref, dp_ref):
        r = pl.program_id(1)
        xs_ref[pl.ds(0, HALO), :] = jnp.where(r > 0, xp_ref[...], 0.0)
        xs_ref[pl.ds(HALO, bs), :] = x_ref[...]
        xs_ref[pl.ds(HALO + bs, HALO), :] = xn_ref[...]
        ext = bs + HALO
        pre = b_ref[...] + jnp.zeros((ext, bc), F32)
        for j in range(SSM_CONV):
            pre = pre + w_ref[j:j + 1, :] * xs_ref[pl.ds(HALO - 3 + j, ext), :]
        sg = _sigmoid(pre)
        dsilu = sg * (1.0 + pre * (1.0 - sg))
        dp_ref[pl.ds(0, bs), :] = d_ref[...] * dsilu[:bs]
        dp_ref[pl.ds(bs, HALO), :] = jnp.where(r < nr - 1, dn_ref[...], 0.0) * dsilu[bs:]
        dx = jnp.zeros((bs, bc), F32)
        for j in range(SSM_CONV):
            dx = dx + w_ref[j:j + 1, :] * dp_ref[pl.ds(3 - j, bs), :]
        dx_ref[...] = _b(dx)
        dpre = dp_ref[pl.ds(0, bs), :]
        for j in range(SSM_CONV):
            part = jnp.sum(dpre * xs_ref[pl.ds(HALO - 3 + j, bs), :], axis=0, keepdims=True)

            @pl.when(r == 0)
            def _():
                dw_ref[j:j + 1, :] = part

            @pl.when(r > 0)
            def _():
                dw_ref[j:j + 1, :] += part
        part = jnp.sum(dpre, axis=0, keepdims=True)

        @pl.when(r == 0)
        def _():
            db_ref[...] = part

        @pl.when(r > 0)
        def _():
            db_ref[...] += part

    hb = bs // HALO
    return pl.pallas_call(
        body, name=f"conv_bwd_{col0}", grid=(Cp // bc, nr),
        in_specs=[pl.BlockSpec((bs, bc), lambda c, r: (r, cb0 + c)),
                  pl.BlockSpec((HALO, bc), lambda c, r: (jnp.maximum(r * hb - 1, 0), cb0 + c)),
                  pl.BlockSpec((HALO, bc), lambda c, r: (jnp.minimum((r + 1) * hb, last_halo), cb0 + c)),
                  pl.BlockSpec((bs, bc), lambda c, r: (r, c)),
                  pl.BlockSpec((HALO, bc), lambda c, r: (jnp.minimum((r + 1) * hb, last_halo), c)),
                  pl.BlockSpec((SSM_CONV, bc), lambda c, r: (0, cb0 + c)),
                  pl.BlockSpec((1, bc), lambda c, r: (0, cb0 + c))],
        out_specs=[pl.BlockSpec((bs, bc), lambda c, r: (r, c)),
                   pl.BlockSpec((SSM_CONV, bc), lambda c, r: (0, c)),
                   pl.BlockSpec((1, bc), lambda c, r: (0, c))],
        out_shape=[jax.ShapeDtypeStruct((S, Cp), BF16), jax.ShapeDtypeStruct((SSM_CONV, Cp), F32),
                   jax.ShapeDtypeStruct((1, Cp), F32)],
        scratch_shapes=[pltpu.VMEM((bs + 2 * HALO, bc), F32), pltpu.VMEM((bs + HALO, bc), F32)],
        compiler_params=_params(("parallel", "arbitrary")),
    )(xbc, xbc, xbc, dact, dact, conv_w, conv_b)


def _shift_down(x, k, top_src):
    r8 = lax.broadcasted_iota(jnp.int32, (HALO, x.shape[1]), 0)
    rolled = pltpu.roll(x, k, 0)
    top = jnp.where(r8 < k, pltpu.roll(top_src, k, 0), rolled[0:HALO])
    if x.shape[0] == HALO:
        return top
    return jnp.concatenate([top, rolled[HALO:]], axis=0)


def _shift_up(x, k, bottom_src):
    n = x.shape[0]
    r8 = lax.broadcasted_iota(jnp.int32, (HALO, x.shape[1]), 0)
    rolled = pltpu.roll(x, n - k, 0)
    bottom = jnp.where(r8 >= HALO - k, pltpu.roll(bottom_src, HALO - k, 0), rolled[n - HALO:n])
    return jnp.concatenate([rolled[:n - HALO], bottom], axis=0)


def _conv_pre(x, top_src, w_ref, b_ref):
    shifted = [x] + [_shift_down(x, k, top_src) for k in range(1, SSM_CONV)]
    pre = b_ref[...] + w_ref[SSM_CONV - 1:SSM_CONV, :] * x
    for k in range(1, SSM_CONV):
        pre = pre + w_ref[SSM_CONV - 1 - k:SSM_CONV - k, :] * shifted[k]
    return pre, shifted


def _conv_fwd2(xbc, conv_w, conv_b):
    S, C = xbc.shape
    bs, bc = CONV_ROWS, CONV_COLS
    nr = S // bs

    def body(x_ref, halo_ref, w_ref, b_ref, o_ref):
        r = pl.program_id(1)
        halo = jnp.where(r > 0, halo_ref[...], 0.0)
        pre, _ = _conv_pre(x_ref[...], halo, w_ref, b_ref)
        o_ref[...] = pre * _sigmoid(pre)

    return pl.pallas_call(
        body, name="conv_fwd", grid=(C // bc, nr),
        in_specs=[pl.BlockSpec((bs, bc), lambda c, r: (r, c)),
                  pl.BlockSpec((HALO, bc), lambda c, r: (jnp.maximum(r * (bs // HALO) - 1, 0), c)),
                  pl.BlockSpec((SSM_CONV, bc), lambda c, r: (0, c)),
                  pl.BlockSpec((1, bc), lambda c, r: (0, c))],
        out_specs=pl.BlockSpec((bs, bc), lambda c, r: (r, c)),
        out_shape=jax.ShapeDtypeStruct((S, C), F32),
        compiler_params=_params(("parallel", "arbitrary")),
    )(xbc, xbc, conv_w, conv_b)


def _conv_bwd2(xbc, dact, conv_w, conv_b):
    S, C = xbc.shape
    bs, bc = CONV_ROWS, CONV_COLS
    nr = S // bs
    hb = bs // HALO
    last_halo = S // HALO - 1

    def dsilu(pre):
        sg = _sigmoid(pre)
        return sg * (1.0 + pre * (1.0 - sg))

    def body(x_ref, xp_ref, xn_ref, d_ref, dn_ref, w_ref, b_ref, dx_ref, dw_ref, db_ref):
        r = pl.program_id(1)
        x = x_ref[...]
        pre, shifted = _conv_pre(x, jnp.where(r > 0, xp_ref[...], 0.0), w_ref, b_ref)
        dpre = d_ref[...] * dsilu(pre)
        pre_n, _ = _conv_pre(xn_ref[...], x[bs - HALO:bs], w_ref, b_ref)
        dpre_n = jnp.where(r < nr - 1, dn_ref[...], 0.0) * dsilu(pre_n)
        dx = w_ref[SSM_CONV - 1:SSM_CONV, :] * dpre
        for k in range(1, SSM_CONV):
            dx = dx + w_ref[SSM_CONV - 1 - k:SSM_CONV - k, :] * _shift_up(dpre, k, dpre_n)
        dx_ref[...] = _b(dx)
        parts = [jnp.sum(dpre * shifted[SSM_CONV - 1 - j], axis=0, keepdims=True) for j in range(SSM_CONV)]
        dbp = jnp.sum(dpre, axis=0, keepdims=True)

        @pl.when(r == 0)
        def _():
            for j in range(SSM_CONV):
                dw_ref[j:j + 1, :] = parts[j]
            db_ref[...] = dbp

        @pl.when(r > 0)
        def _():
            for j in range(SSM_CONV):
                dw_ref[j:j + 1, :] += parts[j]
            db_ref[...] += dbp

    return pl.pallas_call(
        body, name="conv_bwd", grid=(C // bc, nr),
        in_specs=[pl.BlockSpec((bs, bc), lambda c, r: (r, c)),
                  pl.BlockSpec((HALO, bc), lambda c, r: (jnp.maximum(r * hb - 1, 0), c)),
                  pl.BlockSpec((HALO, bc), lambda c, r: (jnp.minimum((r + 1) * hb, last_halo), c)),
                  pl.BlockSpec((bs, bc), lambda c, r: (r, c)),
                  pl.BlockSpec((HALO, bc), lambda c, r: (jnp.minimum((r + 1) * hb, last_halo), c)),
                  pl.BlockSpec((SSM_CONV, bc), lambda c, r: (0, c)),
                  pl.BlockSpec((1, bc), lambda c, r: (0, c))],
        out_specs=[pl.BlockSpec((bs, bc), lambda c, r: (r, c)),
                   pl.BlockSpec((SSM_CONV, bc), lambda c, r: (0, c)),
                   pl.BlockSpec((1, bc), lambda c, r: (0, c))],
        out_shape=[jax.ShapeDtypeStruct((S, C), BF16), jax.ShapeDtypeStruct((SSM_CONV, C), F32),
                   jax.ShapeDtypeStruct((1, C), F32)],
        compiler_params=_params(("parallel", "arbitrary")),
    )(xbc, xbc, xbc, dact, dact, conv_w, conv_b)


def _softplus(x):
    return jnp.maximum(x, 0.0) + jnp.log(1.0 + jnp.exp(-jnp.abs(x)))


def _ssd_common(dtr_ref, bias_ref, a_ref, g):
    ch = SSM_CHUNK
    x = dtr_ref[...] + bias_ref[...]
    dt_all = _softplus(x)
    r = lax.broadcasted_iota(jnp.int32, (LANES, LANES), 0)
    c = lax.broadcasted_iota(jnp.int32, (LANES, LANES), 1)
    sel = jnp.where(jnp.logical_and(r == HEADS_PER_GROUP * g + c, c < HEADS_PER_GROUP), 1.0, 0.0)
    dt4 = _dot_hi(dt_all, sel)
    la4 = _dot_hi(dt_all * a_ref[...], sel)
    ii = lax.broadcasted_iota(jnp.int32, (ch, ch), 0)
    jj = lax.broadcasted_iota(jnp.int32, (ch, ch), 1)
    tril = jnp.where(ii >= jj, 1.0, 0.0)
    acs = _dot_hi(tril, la4)
    return x, sel, dt4, acs, acs.T, ii >= jj


def _row8(v):
    return jnp.broadcast_to(v, (8, v.shape[1]))


def _ssd_fwd(xact, dt_raw, dt_bias, a_neg, d_skip):
    S = xact.shape[0]
    ch = SSM_CHUNK
    nch = S // ch
    hg = HEADS_PER_GROUP
    gw = hg * SSM_HEAD_DIM
    b_off = SSM_INNER // SSM_STATE
    c_off = b_off + SSM_GROUPS

    def body(x_ref, b_ref, c_ref, dtr_ref, bias_ref, a_ref, dsk_ref, y_ref, hs_ref, h_ref):
        c = pl.program_id(0)
        g = pl.program_id(1)

        @pl.when(jnp.logical_and(c == 0, g == 0))
        def _():
            h_ref[...] = jnp.zeros_like(h_ref)

        _, sel, dt4, acs, acs_t, low = _ssd_common(dtr_ref, bias_ref, a_ref, g)
        dsk4 = _dot_hi(_row8(dsk_ref[...]), sel)
        bb = _b(b_ref[...])
        cc = _b(c_ref[...])
        cb = _dot_nt(cc, bb)
        for j in range(hg):
            sl = slice(j * SSM_HEAD_DIM, (j + 1) * SSM_HEAD_DIM)
            acol = acs[:, j:j + 1]
            arow = acs_t[j:j + 1, :]
            alast = acs[ch - 1:ch, j:j + 1]
            decay = jnp.exp(jnp.where(low, acol - arow, -jnp.inf))
            xh = x_ref[:, sl]
            xd = xh * dt4[:, j:j + 1]
            hj = h_ref[hg * g + j]
            y = _dot(_b(cb * decay), _b(xd))
            y = y + _dot_nt(cc, _b(hj)) * jnp.exp(acol)
            y_ref[:, sl] = y + dsk4[0:1, j:j + 1] * xh
            hs_ref[0, j] = hj
            st = _dot_tn(_b(xd * jnp.exp(alast - acol)), bb)
            h_ref[hg * g + j] = hj * jnp.exp(alast) + st

    small = pl.BlockSpec((1, LANES), lambda c, g: (0, 0))
    return pl.pallas_call(
        body, name="ssd_fwd", grid=(nch, SSM_GROUPS),
        in_specs=[pl.BlockSpec((ch, gw), lambda c, g: (c, g)),
                  pl.BlockSpec((ch, SSM_STATE), lambda c, g: (c, b_off + g)),
                  pl.BlockSpec((ch, SSM_STATE), lambda c, g: (c, c_off + g)),
                  pl.BlockSpec((ch, LANES), lambda c, g: (c, 0)),
                  small, small, small],
        out_specs=[pl.BlockSpec((ch, gw), lambda c, g: (c, g)),
                   pl.BlockSpec((1, hg, SSM_HEAD_DIM, SSM_STATE), lambda c, g: (c, g, 0, 0))],
        out_shape=[jax.ShapeDtypeStruct((S, SSM_INNER), F32),
                   jax.ShapeDtypeStruct((nch, SSM_HEADS, SSM_HEAD_DIM, SSM_STATE), F32)],
        scratch_shapes=[pltpu.VMEM((SSM_HEADS, SSM_HEAD_DIM, SSM_STATE), F32)],
        compiler_params=_params(("arbitrary", "arbitrary")),
    )(xact, xact, xact, dt_raw, dt_bias, a_neg, d_skip)


def _ssd_bwd(xact, dt_raw, dt_bias, a_neg, d_skip, hs, dy):
    S = xact.shape[0]
    ch = SSM_CHUNK
    nch = S // ch
    hg = HEADS_PER_GROUP
    gw = hg * SSM_HEAD_DIM
    b_off = SSM_INNER // SSM_STATE
    c_off = b_off + SSM_GROUPS

    def body(x_ref, b_ref, c_ref, dtr_ref, bias_ref, a_ref, dsk_ref, hs_ref, dy_ref,
             dx_ref, db_ref, dc_ref, ddt_ref, st_ref, dh_ref, ddt_acc):
        step = pl.program_id(0)
        g = pl.program_id(1)

        @pl.when(jnp.logical_and(step == 0, g == 0))
        def _():
            dh_ref[...] = jnp.zeros_like(dh_ref)
            st_ref[...] = jnp.zeros_like(st_ref)

        @pl.when(g == 0)
        def _():
            ddt_acc[...] = jnp.zeros_like(ddt_acc)

        xraw, sel, dt4, acs, acs_t, low = _ssd_common(dtr_ref, bias_ref, a_ref, g)
        a4 = _dot_hi(_row8(a_ref[...]), sel)[0:1, :]
        dsk4 = _dot_hi(_row8(dsk_ref[...]), sel)
        bf = b_ref[...]
        cf = c_ref[...]
        bb = _b(bf)
        cc = _b(cf)
        cb = _dot_nt(cc, bb)
        lane = lax.broadcasted_iota(jnp.int32, (ch, LANES), 1)
        rowi = lax.broadcasted_iota(jnp.int32, (ch, 1), 0)
        ones = jnp.ones((ch, LANES), F32)
        dcb = jnp.zeros((ch, ch), F32)
        dc_acc = jnp.zeros((ch, SSM_STATE), F32)
        db_acc = jnp.zeros((ch, SSM_STATE), F32)
        dacs4 = jnp.zeros((ch, LANES), F32)
        ddt4 = jnp.zeros((ch, LANES), F32)
        dd4 = jnp.zeros((1, LANES), F32)
        lane1 = lax.broadcasted_iota(jnp.int32, (1, LANES), 1)
        for j in range(hg):
            sl = slice(j * SSM_HEAD_DIM, (j + 1) * SSM_HEAD_DIM)
            acol = acs[:, j:j + 1]
            arow = acs_t[j:j + 1, :]
            alast = acs[ch - 1:ch, j:j + 1]
            decay = jnp.exp(jnp.where(low, acol - arow, -jnp.inf))
            ea = jnp.exp(acol)
            dsd = jnp.exp(alast - acol)
            cd = jnp.exp(alast)
            dtc = dt4[:, j:j + 1]
            xh = x_ref[:, sl]
            xd = xh * dtc
            xdb = _b(xd)
            hj = hs_ref[0, j]
            hjb = _b(hj)
            dhn = dh_ref[hg * g + j]
            dyj = dy_ref[:, sl]
            dyb = _b(dyj)
            lm = cb * decay
            dxh = dsk4[0:1, j:j + 1] * dyj
            dd4 = jnp.where(lane1 == j, jnp.sum(jnp.sum(dyj * xh, axis=1, keepdims=True), axis=0,
                                                keepdims=True), dd4)
            dlm = _dot_nt(dyb, xdb)
            dxd = _dot_tn(_b(lm), dyb)
            gm = dlm * lm
            dcb = dcb + dlm * decay
            dac = jnp.sum(gm, axis=1, keepdims=True) - _dot_tn_hi(gm, ones)[:, 0:1]
            zz = _dot_nt(cc, hjb)
            dzb = _b(dyj * ea)
            dac = dac + jnp.sum(dyj * zz, axis=1, keepdims=True) * ea
            dc_acc = dc_acc + _dot(dzb, hjb)
            dh_in = _dot_tn(dzb, cc)
            dsb = _b(dhn)
            ww = _dot_nt(bb, dsb)
            dxd = dxd + ww * dsd
            dds = jnp.sum(ww * xd, axis=1, keepdims=True) * dsd
            db_acc = db_acc + _dot(_b(xd * dsd), dsb)
            dac = dac - dds
            dal = (jnp.sum(dds, axis=0, keepdims=True)
                   + jnp.sum(jnp.sum(dhn * hj, axis=1, keepdims=True), axis=0, keepdims=True) * cd)
            dh_ref[hg * g + j] = dh_in + dhn * cd
            dac = dac + jnp.where(rowi == ch - 1, dal, 0.0)
            dacs4 = jnp.where(lane == j, dac, dacs4)
            dx_ref[:, sl] = dxh + dxd * dtc
            ddt4 = jnp.where(lane == j, jnp.sum(dxd * xh, axis=1, keepdims=True), ddt4)
        dcbb = _b(dcb)
        dc_ref[...] = dc_acc + _dot(dcbb, bb)
        db_ref[...] = db_acc + _dot_tn(dcbb, cc)
        ii = lax.broadcasted_iota(jnp.int32, (ch, ch), 0)
        jj = lax.broadcasted_iota(jnp.int32, (ch, ch), 1)
        triu = jnp.where(ii <= jj, 1.0, 0.0)
        dla4 = _dot_hi(triu, dacs4)
        ddt4 = ddt4 + dla4 * a4
        da4 = jnp.sum(dla4 * dt4, axis=0, keepdims=True) * a4
        sel_t = sel.T
        ddt_raw = _dot_hi(ddt4, sel_t) * _sigmoid(xraw)
        ddt_acc[...] += ddt_raw
        st_ref[0:1, :] += _dot_hi(_row8(da4), sel_t)[0:1, :]
        st_ref[1:2, :] += _dot_hi(_row8(dd4), sel_t)[0:1, :]
        st_ref[2:3, :] += jnp.sum(ddt_raw, axis=0, keepdims=True)

        @pl.when(g == SSM_GROUPS - 1)
        def _():
            ddt_ref[...] = _b(ddt_acc[...])

    small = pl.BlockSpec((1, LANES), lambda s, g: (0, 0))
    rc = lambda s: nch - 1 - s
    return pl.pallas_call(
        body, name="ssd_bwd", grid=(nch, SSM_GROUPS),
        in_specs=[pl.BlockSpec((ch, gw), lambda s, g: (rc(s), g)),
                  pl.BlockSpec((ch, SSM_STATE), lambda s, g: (rc(s), b_off + g)),
                  pl.BlockSpec((ch, SSM_STATE), lambda s, g: (rc(s), c_off + g)),
                  pl.BlockSpec((ch, LANES), lambda s, g: (rc(s), 0)),
                  small, small, small,
                  pl.BlockSpec((1, hg, SSM_HEAD_DIM, SSM_STATE), lambda s, g: (rc(s), g, 0, 0)),
                  pl.BlockSpec((ch, gw), lambda s, g: (rc(s), g))],
        out_specs=[pl.BlockSpec((ch, gw), lambda s, g: (rc(s), g)),
                   pl.BlockSpec((ch, SSM_STATE), lambda s, g: (rc(s), g)),
                   pl.BlockSpec((ch, SSM_STATE), lambda s, g: (rc(s), g)),
                   pl.BlockSpec((ch, LANES), lambda s, g: (rc(s), 0)),
                   pl.BlockSpec((8, LANES), lambda s, g: (0, 0))],
        out_shape=[jax.ShapeDtypeStruct((S, SSM_INNER), F32),
                   jax.ShapeDtypeStruct((S, SSM_GROUPS * SSM_STATE), F32),
                   jax.ShapeDtypeStruct((S, SSM_GROUPS * SSM_STATE), F32),
                   jax.ShapeDtypeStruct((S, LANES), BF16),
                   jax.ShapeDtypeStruct((8, LANES), F32)],
        scratch_shapes=[pltpu.VMEM((SSM_HEADS, SSM_HEAD_DIM, SSM_STATE), F32),
                        pltpu.VMEM((ch, LANES), F32)],
        compiler_params=_params(("arbitrary", "arbitrary")),
    )(xact, xact, xact, dt_raw, dt_bias, a_neg, d_skip, hs, dy)


GROUP_W = HEADS_PER_GROUP * SSM_HEAD_DIM
B_COL0 = SSM_INNER
C_COL0 = SSM_INNER + SSM_GROUPS * SSM_STATE


def _ssd_prep(dt_raw, dt_bias, a_neg):
    S = dt_raw.shape[0]
    ch = SSM_CHUNK
    nch = S // ch

    def body(dtr_ref, bias_ref, a_ref, dt_ref, acs_ref, acst_ref, sig_ref):
        x = dtr_ref[...] + bias_ref[...]
        lane = lax.broadcasted_iota(jnp.int32, (ch, LANES), 1)
        dt = jnp.where(lane < SSM_HEADS, _softplus(x), 0.0)
        ii = lax.broadcasted_iota(jnp.int32, (ch, ch), 0)
        jj = lax.broadcasted_iota(jnp.int32, (ch, ch), 1)
        acs = _dot_hi(jnp.where(ii >= jj, 1.0, 0.0), dt * a_ref[...])
        dt_ref[...] = dt
        acs_ref[...] = acs
        acst_ref[0] = acs.T[0:SSM_HEADS, :]
        sig_ref[...] = _sigmoid(x)

    blk = pl.BlockSpec((ch, LANES), lambda c: (c, 0))
    small = pl.BlockSpec((1, LANES), lambda c: (0, 0))
    shp = jax.ShapeDtypeStruct((S, LANES), F32)
    return pl.pallas_call(
        body, name="ssd_prep", grid=(nch,),
        in_specs=[blk, small, small],
        out_specs=[blk, blk, pl.BlockSpec((1, SSM_HEADS, ch), lambda c: (c, 0, 0)), blk],
        out_shape=[shp, shp, jax.ShapeDtypeStruct((nch, SSM_HEADS, ch), F32), shp],
        compiler_params=_params(("parallel",)),
    )(dt_raw, dt_bias, a_neg)


def _expand_heads(arr, g, rows):
    lane = lax.broadcasted_iota(jnp.int32, (rows, GROUP_W), 1) // SSM_HEAD_DIM
    h0 = HEADS_PER_GROUP * g
    out = jnp.broadcast_to(arr[:, h0:h0 + 1], (rows, GROUP_W))
    for j in range(1, HEADS_PER_GROUP):
        out = jnp.where(lane == j, arr[:, h0 + j:h0 + j + 1], out)
    return out


def _seg_matrix(k, lanes_per_head, h0):
    r = lax.broadcasted_iota(jnp.int32, (k, LANES), 0)
    c = lax.broadcasted_iota(jnp.int32, (k, LANES), 1)
    return jnp.where(c == h0 + r // lanes_per_head, 1.0, 0.0).astype(BF16)


def _seg_dot(t, e):
    hi = _b(t)
    lo = _b(t - hi.astype(F32))
    return _dot(hi, e) + _dot(lo, e)


def _head_sums(t, e, rows):
    if rows >= 8:
        return _seg_dot(t, e)
    return _seg_dot(jnp.broadcast_to(t, (8, t.shape[1])), e)[0:rows]


def _pair_masks(x):
    lane = lax.broadcasted_iota(jnp.int32, x.shape, 1)
    zero = jnp.zeros_like(x)
    return jnp.where(lane < SSM_HEAD_DIM, x, zero), jnp.where(lane >= SSM_HEAD_DIM, x, zero)


def _ssd_fwd2(xact, dt, acs, acst, dsk_e):
    S = xact.shape[0]
    ch = SSM_CHUNK
    nch = S // ch

    def body(x_ref, dt_ref, acs_ref, acst_ref, dsk_ref, y_ref, hs_ref, h_ref):
        c = pl.program_id(0)

        @pl.when(c == 0)
        def _():
            h_ref[...] = jnp.zeros_like(h_ref)

        dt_all = dt_ref[...]
        acs_all = acs_ref[...]
        acst_all = acst_ref[0]
        alast = acs_all[ch - 1:ch, :]
        eacs = jnp.exp(acs_all)
        dsd_all = jnp.exp(alast - acs_all)
        cd_all = jnp.exp(alast)
        ii = lax.broadcasted_iota(jnp.int32, (ch, ch), 0)
        jj = lax.broadcasted_iota(jnp.int32, (ch, ch), 1)
        low = ii >= jj
        for g in range(SSM_GROUPS):
            xs = x_ref[:, g * GROUP_W:(g + 1) * GROUP_W]
            bb = _b(x_ref[:, B_COL0 + g * SSM_STATE:B_COL0 + (g + 1) * SSM_STATE])
            cc = _b(x_ref[:, C_COL0 + g * SSM_STATE:C_COL0 + (g + 1) * SSM_STATE])
            cb = _dot_nt(cc, bb)
            xd = xs * _expand_heads(dt_all, g, ch)
            xdb = _b(xd)
            ht = h_ref[g]
            rest = (_dot(cc, _b(ht)) * _expand_heads(eacs, g, ch)
                    + dsk_ref[:, g * GROUP_W:(g + 1) * GROUP_W] * xs)
            for p in range(HEADS_PER_GROUP // 2):
                lms = []
                for h in (HEADS_PER_GROUP * g + 2 * p, HEADS_PER_GROUP * g + 2 * p + 1):
                    diff = acs_all[:, h:h + 1] - acst_all[h:h + 1, :]
                    lms.append(_b(cb * jnp.exp(jnp.where(low, diff, -jnp.inf))))
                xa, xb = _pair_masks(xdb[:, p * LANES:(p + 1) * LANES])
                yp = _dot(jnp.concatenate(lms, axis=1), jnp.concatenate([xa, xb], axis=0))
                y_ref[:, g * GROUP_W + p * LANES:g * GROUP_W + (p + 1) * LANES] = (
                    yp + rest[:, p * LANES:(p + 1) * LANES])
            hs_ref[0, g] = ht
            st = _dot_tn(bb, _b(xd * _expand_heads(dsd_all, g, ch)))
            h_ref[g] = ht * _expand_heads(cd_all, g, 1) + st

    blk = pl.BlockSpec((ch, LANES), lambda c: (c, 0))
    return pl.pallas_call(
        body, name="ssd_fwd", grid=(nch,),
        in_specs=[pl.BlockSpec((ch, CONV_DIM), lambda c: (c, 0)), blk, blk,
                  pl.BlockSpec((1, SSM_HEADS, ch), lambda c: (c, 0, 0)),
                  pl.BlockSpec((1, SSM_INNER), lambda c: (0, 0))],
        out_specs=[pl.BlockSpec((ch, SSM_INNER), lambda c: (c, 0)),
                   pl.BlockSpec((1, SSM_GROUPS, SSM_STATE, GROUP_W), lambda c: (c, 0, 0, 0))],
        out_shape=[jax.ShapeDtypeStruct((S, SSM_INNER), F32),
                   jax.ShapeDtypeStruct((nch, SSM_GROUPS, SSM_STATE, GROUP_W), F32)],
        scratch_shapes=[pltpu.VMEM((SSM_GROUPS, SSM_STATE, GROUP_W), F32)],
        compiler_params=_params(("arbitrary",)),
    )(xact, dt, acs, acst, dsk_e)


def _ssd_bwd2(xact, dt, acs, acst, sig, a_neg, dsk_e, hs, dy):
    S = xact.shape[0]
    ch = SSM_CHUNK
    nch = S // ch

    def body(x_ref, dt_ref, acs_ref, acst_ref, sig_ref, a_ref, dsk_ref, hs_ref, dy_ref,
             dx_ref, ddt_ref, st_ref, dh_ref, rows_ref):
        step = pl.program_id(0)

        @pl.when(step == 0)
        def _():
            dh_ref[...] = jnp.zeros_like(dh_ref)
            st_ref[...] = jnp.zeros_like(st_ref)
            rows_ref[...] = jnp.zeros_like(rows_ref)

        dt_all = dt_ref[...]
        acs_all = acs_ref[...]
        acst_all = acst_ref[0]
        alast = acs_all[ch - 1:ch, :]
        eacs = jnp.exp(acs_all)
        dsd_all = jnp.exp(alast - acs_all)
        cd_all = jnp.exp(alast)
        ii = lax.broadcasted_iota(jnp.int32, (ch, ch), 0)
        jj = lax.broadcasted_iota(jnp.int32, (ch, ch), 1)
        low = ii >= jj
        lane = lax.broadcasted_iota(jnp.int32, (ch, LANES), 1)
        cols = jnp.zeros((ch, LANES), F32)
        ddt = jnp.zeros((ch, LANES), F32)
        dal = jnp.zeros((1, LANES), F32)
        ddsk = jnp.zeros((1, LANES), F32)
        for g in range(SSM_GROUPS):
            xs = x_ref[:, g * GROUP_W:(g + 1) * GROUP_W]
            bb = _b(x_ref[:, B_COL0 + g * SSM_STATE:B_COL0 + (g + 1) * SSM_STATE])
            cc = _b(x_ref[:, C_COL0 + g * SSM_STATE:C_COL0 + (g + 1) * SSM_STATE])
            cb = _dot_nt(cc, bb)
            dt_e = _expand_heads(dt_all, g, ch)
            ea_e = _expand_heads(eacs, g, ch)
            dsd_e = _expand_heads(dsd_all, g, ch)
            cd_e = _expand_heads(cd_all, g, 1)
            xd = xs * dt_e
            xdb = _b(xd)
            dyg = dy_ref[:, g * GROUP_W:(g + 1) * GROUP_W]
            dyb = _b(dyg)
            ht = hs_ref[0, g]
            htb = _b(ht)
            dhn = dh_ref[g]
            dhnb = _b(dhn)
            zz = _dot(cc, htb)
            dzb = _b(dyg * ea_e)
            d_c = _dot_nt(dzb, htb)
            dh_in = _dot_tn(cc, dzb)
            ww = _dot(bb, dhnb)
            xdd = xd * dsd_e
            d_b = _dot_nt(_b(xdd), dhnb)
            t2 = ww * xdd
            e_g = _seg_matrix(GROUP_W, SSM_HEAD_DIM, HEADS_PER_GROUP * g)
            cols = cols + _head_sums(dyg * zz * ea_e - t2, e_g, ch)
            dal = dal + _head_sums(jnp.sum(t2, axis=0, keepdims=True), e_g, 1) + cd_all * _head_sums(
                jnp.sum(dhn * ht, axis=0, keepdims=True), e_g, 1)
            dh_ref[g] = dh_in + dhn * cd_e
            ddsk = ddsk + _head_sums(jnp.sum(dyg * xs, axis=0, keepdims=True), e_g, 1)
            dxd_rest = ww * dsd_e
            dcb = jnp.zeros((ch, ch), F32)
            for p in range(HEADS_PER_GROUP // 2):
                dya, dyb2 = _pair_masks(dyb[:, p * LANES:(p + 1) * LANES])
                xp = xdb[:, p * LANES:(p + 1) * LANES]
                lms, gms = [], []
                for h, dyh in ((HEADS_PER_GROUP * g + 2 * p, dya), (HEADS_PER_GROUP * g + 2 * p + 1, dyb2)):
                    diff = acs_all[:, h:h + 1] - acst_all[h:h + 1, :]
                    decay = jnp.exp(jnp.where(low, diff, -jnp.inf))
                    lm = cb * decay
                    dlm = _dot_nt(dyh, xp)
                    gm = dlm * lm
                    dcb = dcb + dlm * decay
                    rows_ref[h:h + 1, :] = jnp.sum(gm, axis=0, keepdims=True)
                    lms.append(_b(lm))
                    gms.append(gm)
                h0 = HEADS_PER_GROUP * g + 2 * p
                cols = cols + _head_sums(jnp.concatenate(gms, axis=1), _seg_matrix(2 * ch, ch, h0), ch)
                dxd = _dot_tn(jnp.concatenate(lms, axis=0), jnp.concatenate([dya, dyb2], axis=0))
                dxd = dxd + dxd_rest[:, p * LANES:(p + 1) * LANES]
                sl = slice(g * GROUP_W + p * LANES, g * GROUP_W + (p + 1) * LANES)
                dx_ref[:, sl] = (dsk_ref[:, sl] * dyg[:, p * LANES:(p + 1) * LANES]
                                 + dxd * dt_e[:, p * LANES:(p + 1) * LANES])
                ddt = ddt + _head_sums(dxd * xs[:, p * LANES:(p + 1) * LANES],
                                       _seg_matrix(LANES, SSM_HEAD_DIM, h0), ch)
            dcbb = _b(dcb)
            dx_ref[:, C_COL0 + g * SSM_STATE:C_COL0 + (g + 1) * SSM_STATE] = d_c + _dot(dcbb, bb)
            dx_ref[:, B_COL0 + g * SSM_STATE:B_COL0 + (g + 1) * SSM_STATE] = d_b + _dot_tn(dcbb, cc)
        rowi = lax.broadcasted_iota(jnp.int32, (ch, 1), 0)
        dacs = cols - rows_ref[...].T + jnp.where(rowi == ch - 1, dal, 0.0)
        dla = _dot_hi(jnp.where(ii <= jj, 1.0, 0.0), dacs)
        a_row = a_ref[...]
        ddt_raw = (ddt + dla * a_row) * sig_ref[...]
        ddt_ref[...] = _b(ddt_raw)
        st_ref[0:1, :] += jnp.sum(dla * dt_all, axis=0, keepdims=True) * a_row
        st_ref[1:2, :] += ddsk
        st_ref[2:3, :] += jnp.sum(ddt_raw, axis=0, keepdims=True)

    rc = lambda s: nch - 1 - s
    blk = pl.BlockSpec((ch, LANES), lambda s: (rc(s), 0))
    return pl.pallas_call(
        body, name="ssd_bwd", grid=(nch,),
        in_specs=[pl.BlockSpec((ch, CONV_DIM), lambda s: (rc(s), 0)), blk, blk,
                  pl.BlockSpec((1, SSM_HEADS, ch), lambda s: (rc(s), 0, 0)), blk,
                  pl.BlockSpec((1, LANES), lambda s: (0, 0)),
                  pl.BlockSpec((1, SSM_INNER), lambda s: (0, 0)),
                  pl.BlockSpec((1, SSM_GROUPS, SSM_STATE, GROUP_W), lambda s: (rc(s), 0, 0, 0)),
                  pl.BlockSpec((ch, SSM_INNER), lambda s: (rc(s), 0))],
        out_specs=[pl.BlockSpec((ch, CONV_DIM), lambda s: (rc(s), 0)), blk,
                   pl.BlockSpec((8, LANES), lambda s: (0, 0))],
        out_shape=[jax.ShapeDtypeStruct((S, CONV_DIM), F32), jax.ShapeDtypeStruct((S, LANES), BF16),
                   jax.ShapeDtypeStruct((8, LANES), F32)],
        scratch_shapes=[pltpu.VMEM((SSM_GROUPS, SSM_STATE, GROUP_W), F32), pltpu.VMEM((LANES, ch), F32)],
        compiler_params=_params(("arbitrary",)),
    )(xact, dt, acs, acst, sig, a_neg, dsk_e, hs, dy)


def _pad_lanes(v, n=LANES):
    return jnp.pad(v, ((0, 0), (0, n - v.shape[1])))


def _local_step(x, target, w, ex=None):
    offs = np.cumsum((0,) + IN_SPLITS)
    w_in = w["w_in"]
    w_qkv = w_in[:, offs[0]:offs[3]]
    w_z = w_in[:, offs[3]:offs[4]]
    w_xbc = w_in[:, offs[4]:offs[5]]
    w_dt = _pad_lanes(w_in[:, offs[5]:offs[6]])
    w_g = w_in[:, offs[6]:offs[7]]
    dt_bias = _pad_lanes(w["dt_bias"])
    a_neg = _pad_lanes(-jnp.exp(w["a_log"]))
    d_skip = _pad_lanes(w["d_skip"])

    u = _rms_fwd(x, w["norm_mix_pre_w"])
    if ex is None:
        xbc = _mm_nn(u, w_xbc, F32, "proj_xbc")
    else:
        xbc, got = _mm_nn(u, w_xbc, F32, "proj_xbc", comm=_gather_comm([ex.rest_mine]))
        w = {**w, **ex.rest_weights(got[0])}
    qkv = _mm_nn(u, w_qkv, F32, "proj_qkv")
    z = _mm_nn(u, w_z, F32, "proj_z")
    dt_raw = _mm_nn(u, w_dt, F32, "proj_dt")
    gl = _mm_nn(u, w_g, F32, "proj_gate")

    pats = _qkv_layouts(qkv)
    os_, ms_, ls_ = [], [], []
    for d, qkv_p in zip(DILATIONS, pats):
        o, m, l = _attn_fwd2(qkv_p, d)
        os_.append(o)
        ms_.append(m)
        ls_.append(l)
    att, lse = _attn_combine2(os_, ms_, ls_)
    att_o = _mm_nn(att, w["w_att_proj"], F32, "att_proj")

    xact = _conv_fwd2(xbc, w["conv_w"], w["conv_b"])
    dsk_e = jnp.repeat(w["d_skip"], SSM_HEAD_DIM, axis=1)
    dt, acs, acst, sig = _ssd_prep(dt_raw, dt_bias, a_neg)
    y_ssd, hs = _ssd_fwd2(xact, dt, acs, acst, dsk_e)
    ssm_y = _gnorm_fwd(y_ssd, z, w["ssm_norm_w"])
    ssm_o = _mm_nn(ssm_y, w["w_ssm_proj"], F32, "ssm_proj")

    mi = _gate_fwd(att_o, ssm_o, gl, w["b_gate"])
    mixed = _mm_nn(mi, w["w_out"], F32, "out_proj")
    h1, f = _post_pre(x, mixed, w["norm_mix_post_w"], w["norm_ffn_pre_w"])
    r_up, act = _mm_nn(f, w["w_up"], BF16, "ffn_up", mode="relu2")
    down = _mm_nn(act, w["w_down"], F32, "ffn_down")
    dh2, d_down, loss, g_ffn_post = _final(h1, down, w["norm_ffn_post_w"], target)

    g = {"norm_ffn_post_w": g_ffn_post}
    g["w_down"] = _mm_tn(act, d_down, "dw_down")
    dup = _mm_nn(d_down, w["w_down"].T, BF16, "d_act", mode="mul2", extra=r_up)
    g["w_up"] = _mm_tn(f, dup, "dw_up")
    df = _mm_nn(dup, w["w_up"].T, F32, "d_f")
    dh1, d_mixed, g["norm_ffn_pre_w"], g["norm_mix_post_w"] = _mid_bwd(
        dh2, df, h1, mixed, w["norm_ffn_pre_w"], w["norm_mix_post_w"])
    g["w_out"] = _mm_tn(mi, d_mixed, "dw_out")
    dmi = _mm_nn(d_mixed, w["w_out"].T, F32, "d_mi")
    d_att_o, d_ssm_o, dgl, g["b_gate"] = _gate_bwd(dmi, att_o, ssm_o, gl, w["b_gate"])

    g["w_att_proj"] = _mm_tn(att, d_att_o, "dw_att_proj")
    g["w_ssm_proj"] = _mm_tn(ssm_y, d_ssm_o, "dw_ssm_proj")
    if ex is None:
        d_ssm_y = _mm_nn(d_ssm_o, w["w_ssm_proj"].T, F32, "d_ssm_y")
    else:
        gs_rest = jnp.concatenate(
            [_shards_from_full(n, g[n]).reshape(N_CHIPS, -1, PACK_COLS) for n in REST], axis=1)
        d_ssm_y, recv = _mm_nn(d_ssm_o, w["w_ssm_proj"].T, F32, "d_ssm_y", comm=_pair_comm([gs_rest]))
        p_rest = _pair_add2(gs_rest, recv[0], ex.c_arr, "rs_pair_add_rest")

    d_att = _mm_nn(d_att_o, w["w_att_proj"].T, F32, "d_att")
    dqs, dks, dvs = [], [], []
    for d, qkv_p, (do_p, lse_p, delta_p) in zip(DILATIONS, pats, _attn_delta2(d_att, att, lse)):
        if ex is not None and d == DILATIONS[0]:
            dq, dk, dv, recv3 = _attn_bwd2(qkv_p, do_p, lse_p, delta_p, d, comm=_chip_comm([p_rest]))
            q_rest = _chip_add2(p_rest, recv3[0], ex.chip_arr, "rs_chip_add_rest")
            ex.finish_reduce("rest", q_rest, _comm_call("rs_share_rest", _share_comm([q_rest]))[0])
        else:
            dq, dk, dv = _attn_bwd2(qkv_p, do_p, lse_p, delta_p, d)
        dqs.append(dq)
        dks.append(dk)
        dvs.append(dv)
    dqkv = _sum_qkv2(dqs, dks, dvs)

    dy_ssd, dz, g["ssm_norm_w"] = _gnorm_bwd(d_ssm_y, y_ssd, z, w["ssm_norm_w"])
    dxact, ddt_raw, stats = _ssd_bwd2(xact, dt, acs, acst, sig, a_neg, dsk_e, hs, dy_ssd)
    g["a_log"] = stats[0:1, :SSM_HEADS]
    g["d_skip"] = stats[1:2, :SSM_HEADS]
    g["dt_bias"] = stats[2:3, :SSM_HEADS]
    dxbc, g["conv_w"], g["conv_b"] = _conv_bwd2(xbc, dxact, w["conv_w"], w["conv_b"])

    pieces = [(dqkv, w_qkv), (dz, w_z), (dxbc, w_xbc), (ddt_raw, w_dt), (dgl, w_g)]
    gw = [_mm_tn(u, dp, f"dw_in_{i}") for i, (dp, _) in enumerate(pieces)]
    gw[3] = gw[3][:, :SSM_HEADS]
    g["w_in"] = jnp.concatenate(gw, axis=1)
    du = None
    for i, (dp, wp) in enumerate(pieces):
        if ex is not None and i == 0:
            gs_in = _shards_from_full("w_in", g["w_in"])
            du, recv = _mm_nn(dp, wp.T, F32, f"d_u_{i}", acc=du, comm=_pair_comm([gs_in]))
            p_in = _pair_add2(gs_in, recv[0], ex.c_arr, "rs_pair_add_in")
        elif ex is not None and i == 2:
            du, recv3 = _mm_nn(dp, wp.T, F32, f"d_u_{i}", acc=du, comm=_chip_comm([p_in]))
            q_in = _chip_add2(p_in, recv3[0], ex.chip_arr, "rs_chip_add_in")
            ex.finish_reduce("w_in", q_in, _comm_call("rs_share_in", _share_comm([q_in]))[0])
        else:
            du = _mm_nn(dp, wp.T, F32, f"d_u_{i}", acc=du)
    grad_x, g["norm_mix_pre_w"] = _first_bwd(dh1, du, x, w["norm_mix_pre_w"])
    return loss, grad_x, g


BIG = ("w_in", "w_att_proj", "w_ssm_proj", "w_out", "w_up", "w_down")
BIG_FULL_SHAPES = {"w_in": (D_MODEL, IN_PROJ_WIDTH), "w_att_proj": (ATT_WIDTH, D_MODEL),
                   "w_ssm_proj": (SSM_INNER, D_MODEL), "w_out": (D_MODEL, D_MODEL),
                   "w_up": (D_MODEL, FFN_HIDDEN), "w_down": (FFN_HIDDEN, D_MODEL)}
BIG_COL_SHARDED = {"w_in": True, "w_att_proj": True, "w_ssm_proj": False, "w_out": False, "w_up": True,
                   "w_down": False}
PACK_COLS = 1024
PACK_ROWS = 5760
PACK_HALF = PACK_ROWS // 2
PACK_BLOCK = 576
SMALL = ("norm_mix_pre_w", "b_gate", "conv_b", "dt_bias", "a_log", "d_skip", "ssm_norm_w",
         "norm_mix_post_w", "norm_ffn_pre_w", "norm_ffn_post_w")
SMALL_ROWS = 232


def _shard_shape(name):
    r, c = BIG_FULL_SHAPES[name]
    return (r, c // N_CHIPS) if BIG_COL_SHARDED[name] else (r // N_CHIPS, c)


def _pack(shards, dtype):
    flat = [shards[n].astype(dtype).reshape(-1, PACK_COLS) for n in BIG]
    rows = sum(f.shape[0] for f in flat)
    flat.append(jnp.zeros((PACK_ROWS - rows, PACK_COLS), dtype))
    return jnp.concatenate(flat, axis=0)


def _unpack(packed):
    out, r0 = {}, 0
    for n in BIG:
        shp = _shard_shape(n)
        rows = shp[0] * shp[1] // PACK_COLS
        out[n] = packed[r0:r0 + rows].reshape(shp)
        r0 += rows
    return out


def _unpack_full(gathered):
    out, r0 = {}, 0
    for n in BIG:
        shp = _shard_shape(n)
        rows = shp[0] * shp[1] // PACK_COLS
        sh = gathered[:, r0:r0 + rows].reshape((N_CHIPS,) + shp)
        if BIG_COL_SHARDED[n]:
            out[n] = sh.transpose(1, 0, 2).reshape(BIG_FULL_SHAPES[n])
        else:
            out[n] = sh.reshape(BIG_FULL_SHAPES[n])
        r0 += rows
    return out


def _pack_full(grads):
    parts = []
    rows_total = 0
    for n in BIG:
        shp = _shard_shape(n)
        gfull = grads[n]
        if BIG_COL_SHARDED[n]:
            sh = gfull.reshape(shp[0], N_CHIPS, shp[1]).transpose(1, 0, 2)
        else:
            sh = gfull.reshape((N_CHIPS,) + shp)
        parts.append(sh.reshape(N_CHIPS, -1, PACK_COLS))
        rows_total += parts[-1].shape[1]
    parts.append(jnp.zeros((N_CHIPS, PACK_ROWS - rows_total, PACK_COLS), F32))
    return jnp.concatenate(parts, axis=1)


def _mesh_pos():
    return lax.axis_index("x"), lax.axis_index("y"), lax.axis_index("c")


def _other_chips(x, y):
    return [(1 - x, y), (x, 1 - y), (1 - x, 1 - y)]


ANY = pl.BlockSpec(memory_space=pl.ANY)


def _allgather_packed(wpack):
    half = PACK_HALF

    def body(w_ref, out_ref, send_sems, recv_sems):
        x, y, c = _mesh_pos()
        me = 2 * x + y
        sibling = (x, y, 1 - c)
        chips = _other_chips(x, y)

        def rows(chip, h):
            return out_ref.at[chip, pl.ds(h * half, half), :]

        def copy(k, chip, h, to, src=None):
            return pltpu.make_async_remote_copy(
                src_ref=rows(chip, h) if src is None else src, dst_ref=rows(chip, h),
                send_sem=send_sems.at[k], recv_sem=recv_sems.at[k], device_id=to, device_id_type=MESH)

        mine_half = w_ref.at[pl.ds(c * half, half), :]
        first = [copy(j, me, c, (*chip, c), src=mine_half) for j, chip in enumerate(chips)]
        for cp in first:
            cp.start()
        passed = [copy(3 + j, 2 * chip[0] + chip[1], c, sibling) for j, chip in enumerate(chips)]
        for j, chip in enumerate(chips):
            copy(j, 2 * chip[0] + chip[1], c, (x, y, c)).wait_recv()
            passed[j].start()
        for j, chip in enumerate(chips):
            copy(3 + j, 2 * chip[0] + chip[1], 1 - c, (x, y, c)).wait_recv()
        for cp in first + passed:
            cp.wait_send()

    return pl.pallas_call(
        body, name="allgather_weights",
        out_shape=jax.ShapeDtypeStruct((N_CHIPS,) + wpack.shape, wpack.dtype),
        in_specs=[ANY], out_specs=ANY,
        scratch_shapes=[pltpu.SemaphoreType.DMA((6,)), pltpu.SemaphoreType.DMA((6,))],
        compiler_params=pltpu.CompilerParams(has_side_effects=True),
    )(wpack)


def _exchange_halves(gpack):
    half = PACK_HALF

    def body(g_ref, out_ref, send_sem, recv_sem):
        x, y, c = _mesh_pos()
        cp = pltpu.make_async_remote_copy(
            src_ref=g_ref.at[:, pl.ds((1 - c) * half, half), :], dst_ref=out_ref,
            send_sem=send_sem, recv_sem=recv_sem, device_id=(x, y, 1 - c), device_id_type=MESH)
        cp.start()
        cp.wait()

    return pl.pallas_call(
        body, name="rs_pair_exchange",
        out_shape=jax.ShapeDtypeStruct((N_CHIPS, half, PACK_COLS), F32),
        in_specs=[ANY], out_specs=ANY,
        scratch_shapes=[pltpu.SemaphoreType.DMA, pltpu.SemaphoreType.DMA],
        compiler_params=pltpu.CompilerParams(has_side_effects=True),
    )(gpack)


def _pair_add(gpack, recv, c_idx):
    nb = PACK_HALF // PACK_BLOCK

    def body(c_ref, g_ref, r_ref, o_ref):
        o_ref[...] = _b(g_ref[...] + r_ref[...])

    blk = (1, PACK_BLOCK, PACK_COLS)
    return pl.pallas_call(
        body, name="rs_pair_add",
        grid_spec=pltpu.PrefetchScalarGridSpec(
            num_scalar_prefetch=1, grid=(N_CHIPS, nb),
            in_specs=[pl.BlockSpec(blk, lambda s, i, c: (s, c[0] * nb + i, 0)),
                      pl.BlockSpec(blk, lambda s, i, c: (s, i, 0))],
            out_specs=pl.BlockSpec(blk, lambda s, i, c: (s, i, 0))),
        out_shape=jax.ShapeDtypeStruct((N_CHIPS, PACK_HALF, PACK_COLS), BF16),
        compiler_params=_params(("arbitrary", "arbitrary")),
    )(c_idx, gpack, recv)


def _exchange_chips(ppack):
    def body(p_ref, out_ref, send_sems, recv_sems):
        x, y, c = _mesh_pos()
        chips = _other_chips(x, y)
        cps = [pltpu.make_async_remote_copy(
            src_ref=p_ref.at[2 * chip[0] + chip[1]], dst_ref=out_ref.at[j],
            send_sem=send_sems.at[j], recv_sem=recv_sems.at[j], device_id=(*chip, c), device_id_type=MESH)
            for j, chip in enumerate(chips)]
        for cp in cps:
            cp.start()
        for cp in cps:
            cp.wait_recv()
        for cp in cps:
            cp.wait_send()

    return pl.pallas_call(
        body, name="rs_chip_exchange",
        out_shape=jax.ShapeDtypeStruct((N_CHIPS - 1, PACK_HALF, PACK_COLS), ppack.dtype),
        in_specs=[ANY], out_specs=ANY,
        scratch_shapes=[pltpu.SemaphoreType.DMA((3,)), pltpu.SemaphoreType.DMA((3,))],
        compiler_params=pltpu.CompilerParams(has_side_effects=True),
    )(ppack)


def _chip_add(ppack, recv, me_idx):
    nb = PACK_HALF // PACK_BLOCK

    def body(m_ref, p_ref, r0_ref, r1_ref, r2_ref, o_ref):
        o_ref[...] = ((p_ref[0].astype(F32) + r0_ref[0].astype(F32)) + r1_ref[0].astype(F32)) + r2_ref[0].astype(F32)

    blk = (1, PACK_BLOCK, PACK_COLS)
    return pl.pallas_call(
        body, name="rs_chip_add",
        grid_spec=pltpu.PrefetchScalarGridSpec(
            num_scalar_prefetch=1, grid=(nb,),
            in_specs=[pl.BlockSpec(blk, lambda i, m: (m[0], i, 0)),
                      pl.BlockSpec(blk, lambda i, m: (0, i, 0)),
                      pl.BlockSpec(blk, lambda i, m: (1, i, 0)),
                      pl.BlockSpec(blk, lambda i, m: (2, i, 0))],
            out_specs=pl.BlockSpec((PACK_BLOCK, PACK_COLS), lambda i, m: (i, 0))),
        out_shape=jax.ShapeDtypeStruct((PACK_HALF, PACK_COLS), F32),
        compiler_params=_params(("arbitrary",)),
    )(me_idx, ppack, recv, recv, recv)


def _share_halves(qhalf):
    def body(q_ref, out_ref, send_sem, recv_sem):
        x, y, c = _mesh_pos()
        cp = pltpu.make_async_remote_copy(
            src_ref=q_ref, dst_ref=out_ref, send_sem=send_sem, recv_sem=recv_sem,
            device_id=(x, y, 1 - c), device_id_type=MESH)
        cp.start()
        cp.wait()

    return pl.pallas_call(
        body, name="rs_share_halves",
        out_shape=jax.ShapeDtypeStruct(qhalf.shape, F32),
        in_specs=[ANY], out_specs=ANY,
        scratch_shapes=[pltpu.SemaphoreType.DMA, pltpu.SemaphoreType.DMA],
        compiler_params=pltpu.CompilerParams(has_side_effects=True),
    )(qhalf)


REST = ("w_att_proj", "w_ssm_proj", "w_out", "w_up", "w_down")
ADD_ROWS = {IN_PROJ_WIDTH // N_CHIPS: 256, PACK_COLS: 752}


def _stack_rest(shards, dtype):
    return jnp.concatenate([shards[n].astype(dtype).reshape(-1, PACK_COLS) for n in REST], axis=0)


def _unstack_rest(stacked, lead=()):
    out, r0 = {}, 0
    for n in REST:
        shp = _shard_shape(n)
        rows = shp[0] * shp[1] // PACK_COLS
        out[n] = stacked[..., r0:r0 + rows, :].reshape(lead + shp)
        r0 += rows
    return out


def _full_from_shards(name, sh):
    if BIG_COL_SHARDED[name]:
        return sh.transpose(1, 0, 2).reshape(BIG_FULL_SHAPES[name])
    return sh.reshape(BIG_FULL_SHAPES[name])


def _shards_from_full(name, full):
    shp = _shard_shape(name)
    if BIG_COL_SHARDED[name]:
        return full.reshape(shp[0], N_CHIPS, shp[1]).transpose(1, 0, 2)
    return full.reshape((N_CHIPS,) + shp)


def _allgather2(shards):
    n = len(shards)

    def body(*refs):
        w_refs, out_refs, send_sems, recv_sems = refs[:n], refs[n:2 * n], refs[2 * n], refs[2 * n + 1]
        x, y, c = _mesh_pos()
        me = 2 * x + y
        sibling = (x, y, 1 - c)
        chips = _other_chips(x, y)
        plans = []
        for a, (w_ref, out_ref) in enumerate(zip(w_refs, out_refs)):
            half = w_ref.shape[0] // 2

            def copy(k, chip, h, to, src=None, out_ref=out_ref, half=half, a=a):
                rows = out_ref.at[chip, pl.ds(h * half, half), :]
                return pltpu.make_async_remote_copy(
                    src_ref=rows if src is None else src, dst_ref=rows,
                    send_sem=send_sems.at[6 * a + k], recv_sem=recv_sems.at[6 * a + k],
                    device_id=to, device_id_type=MESH)

            mine_half = w_ref.at[pl.ds(c * half, half), :]
            idx = [2 * chip[0] + chip[1] for chip in chips]
            send = [copy(j, me, c, (*chip, c), src=mine_half) for j, chip in enumerate(chips)]
            land = [copy(j, idx[j], c, (x, y, c)) for j in range(N_CHIPS - 1)]
            forward = [copy(3 + j, idx[j], c, sibling) for j in range(N_CHIPS - 1)]
            land_fw = [copy(3 + j, idx[j], 1 - c, (x, y, c)) for j in range(N_CHIPS - 1)]
            plans.append((send, land, forward, land_fw))
        for send, _, _, _ in plans:
            for cp in send:
                cp.start()
        for _, land, forward, _ in plans:
            for j in range(N_CHIPS - 1):
                land[j].wait_recv()
                forward[j].start()
        for _, _, _, land_fw in plans:
            for cp in land_fw:
                cp.wait_recv()
        for send, _, forward, _ in plans:
            for cp in send + forward:
                cp.wait_send()

    return pl.pallas_call(
        body, name="allgather_weights",
        out_shape=[jax.ShapeDtypeStruct((N_CHIPS,) + s.shape, s.dtype) for s in shards],
        in_specs=[ANY] * n, out_specs=[ANY] * n,
        scratch_shapes=[pltpu.SemaphoreType.DMA((6 * n,)), pltpu.SemaphoreType.DMA((6 * n,))],
        compiler_params=pltpu.CompilerParams(has_side_effects=True),
    )(*shards)


def _exchange_halves2(gs):
    n = len(gs)

    def body(*refs):
        g_refs, out_refs, send_sems, recv_sems = refs[:n], refs[n:2 * n], refs[2 * n], refs[2 * n + 1]
        x, y, c = _mesh_pos()
        cps = []
        for a, (g_ref, out_ref) in enumerate(zip(g_refs, out_refs)):
            half = g_ref.shape[1] // 2
            cps.append(pltpu.make_async_remote_copy(
                src_ref=g_ref.at[:, pl.ds((1 - c) * half, half), :], dst_ref=out_ref,
                send_sem=send_sems.at[a], recv_sem=recv_sems.at[a], device_id=(x, y, 1 - c),
                device_id_type=MESH))
        for cp in cps:
            cp.start()
        for cp in cps:
            cp.wait()

    return pl.pallas_call(
        body, name="rs_pair_exchange",
        out_shape=[jax.ShapeDtypeStruct((N_CHIPS, g.shape[1] // 2, g.shape[2]), F32) for g in gs],
        in_specs=[ANY] * n, out_specs=[ANY] * n,
        scratch_shapes=[pltpu.SemaphoreType.DMA((n,)), pltpu.SemaphoreType.DMA((n,))],
        compiler_params=pltpu.CompilerParams(has_side_effects=True),
    )(*gs)


def _pair_add2(g, recv, c_idx, name):
    _, half, cols = recv.shape
    rb = ADD_ROWS[cols]
    nb = half // rb

    def body(c_ref, g_ref, r_ref, o_ref):
        o_ref[...] = _b(g_ref[...] + r_ref[...])

    blk = (1, rb, cols)
    return pl.pallas_call(
        body, name=name,
        grid_spec=pltpu.PrefetchScalarGridSpec(
            num_scalar_prefetch=1, grid=(N_CHIPS, nb),
            in_specs=[pl.BlockSpec(blk, lambda s, i, c: (s, c[0] * nb + i, 0)),
                      pl.BlockSpec(blk, lambda s, i, c: (s, i, 0))],
            out_specs=pl.BlockSpec(blk, lambda s, i, c: (s, i, 0))),
        out_shape=jax.ShapeDtypeStruct(recv.shape, BF16),
        compiler_params=_params(("arbitrary", "arbitrary")),
    )(c_idx, g, recv)


def _exchange_chips2(ps):
    n = len(ps)

    def body(*refs):
        p_refs, out_refs, send_sems, recv_sems = refs[:n], refs[n:2 * n], refs[2 * n], refs[2 * n + 1]
        x, y, c = _mesh_pos()
        chips = _other_chips(x, y)
        cps = [pltpu.make_async_remote_copy(
            src_ref=p_ref.at[2 * chip[0] + chip[1]], dst_ref=out_ref.at[j],
            send_sem=send_sems.at[3 * a + j], recv_sem=recv_sems.at[3 * a + j], device_id=(*chip, c),
            device_id_type=MESH)
            for a, (p_ref, out_ref) in enumerate(zip(p_refs, out_refs)) for j, chip in enumerate(chips)]
        for cp in cps:
            cp.start()
        for cp in cps:
            cp.wait_recv()
        for cp in cps:
            cp.wait_send()

    return pl.pallas_call(
        body, name="rs_chip_exchange",
        out_shape=[jax.ShapeDtypeStruct((N_CHIPS - 1,) + p.shape[1:], p.dtype) for p in ps],
        in_specs=[ANY] * n, out_specs=[ANY] * n,
        scratch_shapes=[pltpu.SemaphoreType.DMA((3 * n,)), pltpu.SemaphoreType.DMA((3 * n,))],
        compiler_params=pltpu.CompilerParams(has_side_effects=True),
    )(*ps)


def _chip_add2(p, recv, me_idx, name):
    _, half, cols = recv.shape
    rb = ADD_ROWS[cols]

    def body(m_ref, p_ref, r0_ref, r1_ref, r2_ref, o_ref):
        o_ref[...] = ((p_ref[0].astype(F32) + r0_ref[0].astype(F32)) + r1_ref[0].astype(F32)) + r2_ref[0].astype(F32)

    blk = (1, rb, cols)
    return pl.pallas_call(
        body, name=name,
        grid_spec=pltpu.PrefetchScalarGridSpec(
            num_scalar_prefetch=1, grid=(half // rb,),
            in_specs=[pl.BlockSpec(blk, lambda i, m: (m[0], i, 0)),
                      pl.BlockSpec(blk, lambda i, m: (0, i, 0)),
                      pl.BlockSpec(blk, lambda i, m: (1, i, 0)),
                      pl.BlockSpec(blk, lambda i, m: (2, i, 0))],
            out_specs=pl.BlockSpec((rb, cols), lambda i, m: (i, 0))),
        out_shape=jax.ShapeDtypeStruct((half, cols), F32),
        compiler_params=_params(("arbitrary",)),
    )(me_idx, p, recv, recv, recv)


def _share_halves2(qs):
    n = len(qs)

    def body(*refs):
        q_refs, out_refs, send_sems, recv_sems = refs[:n], refs[n:2 * n], refs[2 * n], refs[2 * n + 1]
        x, y, c = _mesh_pos()
        cps = [pltpu.make_async_remote_copy(
            src_ref=q_ref, dst_ref=out_ref, send_sem=send_sems.at[a], recv_sem=recv_sems.at[a],
            device_id=(x, y, 1 - c), device_id_type=MESH)
            for a, (q_ref, out_ref) in enumerate(zip(q_refs, out_refs))]
        for cp in cps:
            cp.start()
        for cp in cps:
            cp.wait()

    return pl.pallas_call(
        body, name="rs_share_halves",
        out_shape=[jax.ShapeDtypeStruct(q.shape, F32) for q in qs],
        in_specs=[ANY] * n, out_specs=[ANY] * n,
        scratch_shapes=[pltpu.SemaphoreType.DMA((n,)), pltpu.SemaphoreType.DMA((n,))],
        compiler_params=pltpu.CompilerParams(has_side_effects=True),
    )(*qs)


def _gather_plan():
    def copies(w_refs, out_refs, send_sems, recv_sems):
        x, y, c = _mesh_pos()
        me = 2 * x + y
        sibling = (x, y, 1 - c)
        chips = _other_chips(x, y)
        idx = [2 * chip[0] + chip[1] for chip in chips]
        plans = []
        for a, (w_ref, out_ref) in enumerate(zip(w_refs, out_refs)):
            half = w_ref.shape[0] // 2

            def copy(k, chip, h, to, src=None, out_ref=out_ref, half=half, a=a):
                rows = out_ref.at[chip, pl.ds(h * half, half), :]
                return pltpu.make_async_remote_copy(
                    src_ref=rows if src is None else src, dst_ref=rows,
                    send_sem=send_sems.at[6 * a + k], recv_sem=recv_sems.at[6 * a + k],
                    device_id=to, device_id_type=MESH)

            mine_half = w_ref.at[pl.ds(c * half, half), :]
            send = [copy(j, me, c, (*chip, c), src=mine_half) for j, chip in enumerate(chips)]
            land = [copy(j, idx[j], c, (x, y, c)) for j in range(N_CHIPS - 1)]
            forward = [copy(3 + j, idx[j], c, sibling) for j in range(N_CHIPS - 1)]
            land_fw = [copy(3 + j, idx[j], 1 - c, (x, y, c)) for j in range(N_CHIPS - 1)]
            plans.append((send, land, forward, land_fw))
        return plans

    def start(*refs):
        for send, _, _, _ in copies(*refs):
            for cp in send:
                cp.start()

    def finish(*refs):
        plans = copies(*refs)
        for _, land, forward, _ in plans:
            for j in range(N_CHIPS - 1):
                land[j].wait_recv()
                forward[j].start()
        for _, _, _, land_fw in plans:
            for cp in land_fw:
                cp.wait_recv()
        for send, _, forward, _ in plans:
            for cp in send + forward:
                cp.wait_send()

    return start, finish


def _pair_plan(halves):
    def copies(in_refs, out_refs, send_sems, recv_sems):
        x, y, c = _mesh_pos()
        cps = []
        for a, (g_ref, out_ref) in enumerate(zip(in_refs, out_refs)):
            if halves:
                half = g_ref.shape[1] // 2
                src = g_ref.at[:, pl.ds((1 - c) * half, half), :]
            else:
                src = g_ref
            cps.append(pltpu.make_async_remote_copy(
                src_ref=src, dst_ref=out_ref, send_sem=send_sems.at[a], recv_sem=recv_sems.at[a],
                device_id=(x, y, 1 - c), device_id_type=MESH))
        return cps

    def start(*refs):
        for cp in copies(*refs):
            cp.start()

    def finish(*refs):
        for cp in copies(*refs):
            cp.wait()

    return start, finish


def _chip_plan():
    def copies(in_refs, out_refs, send_sems, recv_sems):
        x, y, c = _mesh_pos()
        chips = _other_chips(x, y)
        return [pltpu.make_async_remote_copy(
            src_ref=p_ref.at[2 * chip[0] + chip[1]], dst_ref=out_ref.at[j],
            send_sem=send_sems.at[3 * a + j], recv_sem=recv_sems.at[3 * a + j], device_id=(*chip, c),
            device_id_type=MESH)
            for a, (p_ref, out_ref) in enumerate(zip(in_refs, out_refs)) for j, chip in enumerate(chips)]

    def start(*refs):
        for cp in copies(*refs):
            cp.start()

    def finish(*refs):
        cps = copies(*refs)
        for cp in cps:
            cp.wait_recv()
        for cp in cps:
            cp.wait_send()

    return start, finish


def _gather_comm(shards):
    return _Comm(_gather_plan(), shards, [jax.ShapeDtypeStruct((N_CHIPS,) + s.shape, s.dtype) for s in shards],
                 6 * len(shards))


def _pair_comm(gs):
    return _Comm(_pair_plan(True), gs,
                 [jax.ShapeDtypeStruct((N_CHIPS, g.shape[1] // 2, g.shape[2]), g.dtype) for g in gs], len(gs))


def _chip_comm(ps):
    return _Comm(_chip_plan(), ps, [jax.ShapeDtypeStruct((N_CHIPS - 1,) + p.shape[1:], p.dtype) for p in ps],
                 3 * len(ps))


def _share_comm(qs):
    return _Comm(_pair_plan(False), qs, [jax.ShapeDtypeStruct(q.shape, q.dtype) for q in qs], len(qs))


def _comm_call(name, comm):
    n, m = len(comm.ins), len(comm.outs)

    def body(*refs):
        args = (refs[:n], refs[n:n + m], refs[n + m], refs[n + m + 1])
        comm.start(*args)
        comm.finish(*args)

    return pl.pallas_call(
        body, name=name, out_shape=comm.outs, in_specs=[ANY] * n, out_specs=[ANY] * m,
        scratch_shapes=[pltpu.SemaphoreType.DMA((comm.n_sems,))] * 2,
        compiler_params=pltpu.CompilerParams(has_side_effects=True),
    )(*comm.ins)


class _Exchange:
    def __init__(self, chip, ci, rest_mine):
        self.chip, self.ci, self.rest_mine = chip, ci, rest_mine
        self.c_arr = ci.reshape(1).astype(jnp.int32)
        self.chip_arr = chip.reshape(1).astype(jnp.int32)
        self.reduced = {}

    def rest_weights(self, got):
        stacks = lax.dynamic_update_slice(got, self.rest_mine[None], (self.chip, 0, 0))
        return {n: _full_from_shards(n, sh) for n, sh in _unstack_rest(stacks, (N_CHIPS,)).items()}

    def finish_reduce(self, key, mine, other):
        south = self.ci == 0
        self.reduced[key] = jnp.concatenate([jnp.where(south, mine, other), jnp.where(south, other, mine)],
                                            axis=0)


def _allreduce_small(part, name):
    rows = part.shape[0]

    def body(p_ref, out_ref, buf, send_sems, recv_sems, local_sem):
        x, y, c = _mesh_pos()
        me, sibling = (x, y, c), (x, y, 1 - c)
        chips = _other_chips(x, y)

        def slot(px, py, pc):
            return buf.at[pl.ds((4 * px + 2 * py + pc) * rows, rows), :]

        def copy(k, block, to, src=None):
            return pltpu.make_async_remote_copy(
                src_ref=slot(*block) if src is None else src, dst_ref=slot(*block),
                send_sem=send_sems.at[k], recv_sem=recv_sems.at[k], device_id=to, device_id_type=MESH)

        mine = pltpu.make_async_copy(p_ref, slot(*me), local_sem)
        mine.start()
        first = [copy(0, me, sibling, src=p_ref)]
        first += [copy(1 + j, me, (*chip, c), src=p_ref) for j, chip in enumerate(chips)]
        for cp in first:
            cp.start()
        passed = [copy(4 + j, (*chip, c), sibling) for j, chip in enumerate(chips)]
        for j, chip in enumerate(chips):
            copy(1 + j, (*chip, c), me).wait_recv()
            passed[j].start()
        copy(0, sibling, me).wait_recv()
        for j, chip in enumerate(chips):
            copy(4 + j, (*chip, 1 - c), me).wait_recv()
        for cp in first + passed:
            cp.wait_send()
        mine.wait()
        acc = buf[pl.ds(0, rows), :]
        for k in range(1, N_DEV):
            acc = acc + buf[pl.ds(k * rows, rows), :]
        out_ref[...] = acc

    return pl.pallas_call(
        body, name=name,
        out_shape=jax.ShapeDtypeStruct(part.shape, F32),
        in_specs=[pl.BlockSpec(memory_space=pltpu.VMEM)],
        out_specs=pl.BlockSpec(memory_space=pltpu.VMEM),
        scratch_shapes=[pltpu.VMEM((N_DEV * rows, LANES), F32), pltpu.SemaphoreType.DMA((7,)),
                        pltpu.SemaphoreType.DMA((7,)), pltpu.SemaphoreType.DMA],
        compiler_params=pltpu.CompilerParams(has_side_effects=True),
    )(part)


def _adamw(w, g, m, v, name):
    R, C = w.shape
    bs = _pick(R, (128, 64, 32, 8)) if R % 8 == 0 else R
    c1 = 1.0 / (1.0 - ADAM_B1 ** ADAM_STEP)
    c2 = 1.0 / (1.0 - ADAM_B2 ** ADAM_STEP)

    def body(w_ref, g_ref, m_ref, v_ref, d_ref, nm_ref, nv_ref):
        gg = g_ref[...]
        nm = ADAM_B1 * m_ref[...] + (1.0 - ADAM_B1) * gg
        nv = ADAM_B2 * v_ref[...] + (1.0 - ADAM_B2) * (gg * gg)
        nm_ref[...] = nm
        nv_ref[...] = nv
        d_ref[...] = -ADAM_LR * ((nm * c1) / (jnp.sqrt(nv * c2) + ADAM_EPS) + ADAM_WD * w_ref[...])

    spec = pl.BlockSpec((bs, C), lambda i: (i, 0))
    shp = jax.ShapeDtypeStruct((R, C), F32)
    return pl.pallas_call(
        body, name=name, grid=(R // bs,), in_specs=[spec] * 4, out_specs=[spec] * 3, out_shape=[shp] * 3,
        compiler_params=_params(("parallel",)),
    )(w, g, m, v)


WEIGHTS = ("norm_mix_pre_w", "w_in", "b_gate", "conv_w", "conv_b", "dt_bias", "a_log", "d_skip",
           "ssm_norm_w", "w_att_proj", "w_ssm_proj", "w_out", "norm_mix_post_w", "norm_ffn_pre_w", "w_up",
           "w_down", "norm_ffn_post_w")


def _flat_small(vals, conv_w_full):
    flat = [vals[n].reshape(-1) for n in SMALL] + [conv_w_full.reshape(-1)]
    v = jnp.concatenate(flat)
    return jnp.pad(v, (0, SMALL_ROWS * LANES - v.shape[0])).reshape(SMALL_ROWS, LANES)


def kernel(x, norm_mix_pre_w, w_in, b_gate, conv_w, conv_b, dt_bias, a_log, d_skip, ssm_norm_w, w_att_proj, w_ssm_proj, w_out, norm_mix_post_w, norm_ffn_pre_w, w_up, w_down, norm_ffn_post_w, loss_target, m_norm_mix_pre_w, m_w_in, m_b_gate, m_conv_w, m_conv_b, m_dt_bias, m_a_log, m_d_skip, m_ssm_norm_w, m_w_att_proj, m_w_ssm_proj, m_w_out, m_norm_mix_post_w, m_norm_ffn_pre_w, m_w_up, m_w_down, m_norm_ffn_post_w, v_norm_mix_pre_w, v_w_in, v_b_gate, v_conv_w, v_conv_b, v_dt_bias, v_a_log, v_d_skip, v_ssm_norm_w, v_w_att_proj, v_w_ssm_proj, v_w_out, v_norm_mix_post_w, v_norm_ffn_pre_w, v_w_up, v_w_down, v_norm_ffn_post_w):
    args = locals()

    def strip(a):
        return a[0] if a.ndim == 3 else a

    wts = {n: strip(args[n]) for n in WEIGHTS}
    mom = {n: strip(args["m_" + n]) for n in WEIGHTS}
    var = {n: strip(args["v_" + n]) for n in WEIGHTS}
    xi, yi, ci = _mesh_pos()
    chip = 2 * xi + yi

    w_in_mine = wts["w_in"].astype(BF16)
    got_in = _comm_call("allgather_w_in", _gather_comm([w_in_mine]))[0]
    full = {"w_in": _full_from_shards("w_in", lax.dynamic_update_slice(got_in, w_in_mine[None], (chip, 0, 0)))}
    ex = _Exchange(chip, ci, _stack_rest(wts, BF16))
    cw_cols = CONV_DIM // N_CHIPS
    conv_slab = lax.dynamic_update_slice(jnp.zeros((SSM_CONV, CONV_DIM), F32),
                                         jnp.where(ci == 0, wts["conv_w"], 0.0), (0, chip * cw_cols))
    small_in = jnp.pad(conv_slab.reshape(-1), (0, SMALL_ROWS * LANES - SSM_CONV * CONV_DIM))
    conv_full = _allreduce_small(small_in.reshape(SMALL_ROWS, LANES), "gather_conv_w")
    full["conv_w"] = conv_full.reshape(-1)[:SSM_CONV * CONV_DIM].reshape(SSM_CONV, CONV_DIM)
    for n in SMALL:
        full[n] = wts[n]

    loss_part, grad_x, g = _local_step(x[0], loss_target[0], full, ex)
    loss = lax.psum(loss_part[0, 0], ("x", "y", "c"))

    gshard = {"w_in": ex.reduced["w_in"], **_unstack_rest(ex.reduced["rest"])}
    small_sum = _allreduce_small(_flat_small(g, g["conv_w"]), "allreduce_small_grads").reshape(-1)
    grads, off = {}, 0
    for n in SMALL:
        sz = wts[n].size
        grads[n] = small_sum[off:off + sz].reshape(wts[n].shape)
        off += sz
    conv_g = small_sum[off:off + SSM_CONV * CONV_DIM].reshape(SSM_CONV, CONV_DIM)
    grads["conv_w"] = lax.dynamic_slice(conv_g, (0, chip * cw_cols), (SSM_CONV, cw_cols))
    grads.update(gshard)

    delta, new_m, new_v = {}, {}, {}
    for n in BIG:
        delta[n], new_m[n], new_v[n] = _adamw(wts[n], grads[n], mom[n], var[n], f"adamw_{n}")
    small_names = SMALL + ("conv_w",)

    def pack_small(d):
        v = jnp.concatenate([d[n].reshape(-1) for n in small_names])
        rows = -(-v.shape[0] // (8 * LANES)) * 8
        return jnp.pad(v, (0, rows * LANES - v.shape[0])).reshape(rows, LANES)

    ds, ms, vs = _adamw(pack_small(wts), pack_small(grads), pack_small(mom), pack_small(var), "adamw_small")
    off = 0
    for n in small_names:
        sz = wts[n].size
        for dst, src in ((delta, ds), (new_m, ms), (new_v, vs)):
            dst[n] = src.reshape(-1)[off:off + sz].reshape(wts[n].shape)
        off += sz

    out = [loss, grad_x[None]]
    for d in (grads, delta, new_m, new_v):
        out += [d[n][None] if args[n].ndim == 3 else d[n] for n in WEIGHTS]
    return tuple(out)
```

```python
import functools
import math

import numpy as np
import jax
import jax.numpy as jnp
from jax import lax
from jax.experimental import pallas as pl
from jax.experimental.pallas import tpu as pltpu

F32 = jnp.float32
BF16 = jnp.bfloat16

D_MODEL = 1024
HEAD_DIM = 64
N_ATT_HEADS = 12
ATT_WIDTH = N_ATT_HEADS * HEAD_DIM
DILATIONS = (1, 4, 16)
ATT_BLOCK = 128
SSM_INNER = 2048
SSM_HEADS = 32
SSM_GROUPS = 8
HEADS_PER_GROUP = SSM_HEADS // SSM_GROUPS
SSM_HEAD_DIM = 64
SSM_STATE = 128
SSM_CONV = 4
SSM_CHUNK = 128
CONV_DIM = SSM_INNER + 2 * SSM_GROUPS * SSM_STATE
FFN_HIDDEN = 4 * D_MODEL
IN_SPLITS = (ATT_WIDTH, ATT_WIDTH, ATT_WIDTH, SSM_INNER, CONV_DIM, SSM_HEADS, 2 * D_MODEL)
IN_PROJ_WIDTH = sum(IN_SPLITS)
RMS_EPS = 1e-6
LANES = 128
NEG_BIG = -1e30

ADAM_LR = 0.001
ADAM_B1 = 0.9
ADAM_B2 = 0.999
ADAM_EPS = 1e-08
ADAM_WD = 0.01
ADAM_STEP = 10

N_CHIPS = 4
N_DEV = 8
VMEM_LIMIT = 56 * 1024 * 1024
MESH = pl.DeviceIdType.MESH


def _alibi_slopes(n):
    def pow2(m):
        start = 2.0 ** (-8.0 / m)
        return [start ** (i + 1) for i in range(m)]
    if (n & (n - 1)) == 0:
        s = pow2(n)
    else:
        c = 2 ** int(math.floor(math.log2(n)))
        s = pow2(c) + pow2(2 * c)[0::2][: n - c]
    return [float(v) for v in np.array(s, dtype=np.float32)]


def _params(sem):
    return pltpu.CompilerParams(dimension_semantics=sem, vmem_limit_bytes=VMEM_LIMIT)


def _dot(a, b):
    return lax.dot_general(a, b, (((1,), (0,)), ((), ())), preferred_element_type=F32)


def _dot_nt(a, b):
    return lax.dot_general(a, b, (((1,), (1,)), ((), ())), preferred_element_type=F32)


def _dot_tn(a, b):
    return lax.dot_general(a, b, (((0,), (0,)), ((), ())), preferred_element_type=F32)


def _dot_hi(a, b):
    return lax.dot_general(a, b, (((1,), (0,)), ((), ())), preferred_element_type=F32,
                           precision=lax.Precision.HIGHEST)


def _dot_tn_hi(a, b):
    return lax.dot_general(a, b, (((0,), (0,)), ((), ())), preferred_element_type=F32,
                           precision=lax.Precision.HIGHEST)


def _b(x):
    return x.astype(BF16)


def _sigmoid(x):
    return 1.0 / (1.0 + jnp.exp(-x))


def _pick(n, cands):
    for c in cands:
        if n % c == 0:
            return c
    raise ValueError(f"no tile for {n}")


class _Comm:
    def __init__(self, plan, ins, outs, n_sems):
        self.start, self.finish = plan
        self.ins, self.outs, self.n_sems = list(ins), list(outs), n_sems


def _mm_nn(a, b, out_dtype, name, acc=None, mode=None, extra=None, comm=None, tb=False):
    M, K = a.shape
    N = b.shape[0] if tb else b.shape[1]
    tm = 1024 if M % 1024 == 0 else 512
    tn = _pick(N, (1024, 768, 512, 256, 128))
    tk = K if K <= 2304 else _pick(K, (2048, 1024))
    nk = K // tk
    nj, ni = N // tn, M // tm
    side = acc if acc is not None else extra
    n_out = 2 if mode == "relu2" else 1
    n_in = 2 + (side is not None)
    n_ci = len(comm.ins) if comm else 0
    n_co = len(comm.outs) if comm else 0

    def body(*refs):
        a_ref, b_ref = refs[0], refs[1]
        s_ref = refs[2] if side is not None else None
        o_refs = refs[n_in + n_ci:n_in + n_ci + n_out]
        if comm:
            c_args = (refs[n_in:n_in + n_ci], refs[n_in + n_ci + n_out:n_in + n_ci + n_out + n_co],
                      refs[-2], refs[-1])
            pj, pi, pk = pl.program_id(0), pl.program_id(1), pl.program_id(2)

            @pl.when(jnp.logical_and(jnp.logical_and(pj == 0, pi == 0), pk == 0))
            def _():
                comm.start(*c_args)

        def finish(r):
            if mode == "relu2":
                r = jnp.maximum(r, 0.0)
                o_refs[0][...] = _b(r)
                o_refs[1][...] = _b(r * r)
            elif mode == "mul2":
                o_refs[0][...] = _b(r * (2.0 * s_ref[...].astype(F32)))
            else:
                if acc is not None:
                    r = r + s_ref[...]
                o_refs[0][...] = r.astype(out_dtype)

        part = (_dot_nt if tb else _dot)(_b(a_ref[...]), _b(b_ref[...]))
        if nk == 1:
            finish(part)
        else:
            acc_ref = refs[n_in + n_ci + n_out + n_co]
            k = pl.program_id(2)

            @pl.when(k == 0)
            def _():
                acc_ref[...] = part

            @pl.when(jnp.logical_and(k > 0, k < nk - 1))
            def _():
                acc_ref[...] += part

            @pl.when(k == nk - 1)
            def _():
                finish(acc_ref[...] + part)

        if comm:
            @pl.when(jnp.logical_and(jnp.logical_and(pj == nj - 1, pi == ni - 1), pk == nk - 1))
            def _():
                comm.finish(*c_args)

    tile = pl.BlockSpec((tm, tn), lambda j, i, k: (i, j))
    in_specs = [pl.BlockSpec((tm, tk), lambda j, i, k: (i, k)),
                pl.BlockSpec((tn, tk), lambda j, i, k: (j, k)) if tb else
                pl.BlockSpec((tk, tn), lambda j, i, k: (k, j))]
    args = [a, b]
    if side is not None:
        in_specs.append(tile)
        args.append(side)
    odt = BF16 if mode in ("relu2", "mul2") else out_dtype
    scratch = [pltpu.VMEM((tm, tn), F32)] if nk > 1 else []
    if comm:
        scratch += [pltpu.SemaphoreType.DMA((comm.n_sems,))] * 2
        params = pltpu.CompilerParams(dimension_semantics=("arbitrary",) * 3, vmem_limit_bytes=VMEM_LIMIT,
                                      has_side_effects=True)
    else:
        params = _params(("parallel", "parallel", "arbitrary"))
    outs = pl.pallas_call(
        body, name=name, grid=(nj, ni, nk),
        in_specs=in_specs + [ANY] * n_ci,
        out_specs=[tile] * n_out + [ANY] * n_co,
        out_shape=[jax.ShapeDtypeStruct((M, N), odt)] * n_out + list(comm.outs if comm else []),
        scratch_shapes=scratch,
        compiler_params=params,
    )(*args, *(comm.ins if comm else []))
    res = outs[:n_out] if n_out > 1 else outs[0]
    return (res, outs[n_out:]) if comm else res


def _mm_tn(a, b, name):
    S, Ka = a.shape
    _, N = b.shape
    tka = _pick(Ka, (1024, 768, 512))
    tn = _pick(N, (1024, 768, 512, 256, 128))
    ts = 1024 if S % 1024 == 0 else 512
    ns = S // ts

    def body(a_ref, b_ref, o_ref, acc_ref):
        s = pl.program_id(2)
        part = _dot_tn(_b(a_ref[...]), _b(b_ref[...]))

        @pl.when(s == 0)
        def _():
            acc_ref[...] = part

        @pl.when(s > 0)
        def _():
            acc_ref[...] += part

        @pl.when(s == ns - 1)
        def _():
            o_ref[...] = acc_ref[...]

    return pl.pallas_call(
        body, name=name, grid=(Ka // tka, N // tn, ns),
        in_specs=[pl.BlockSpec((ts, tka), lambda i, j, s: (s, i)),
                  pl.BlockSpec((ts, tn), lambda i, j, s: (s, j))],
        out_specs=pl.BlockSpec((tka, tn), lambda i, j, s: (i, j)),
        out_shape=jax.ShapeDtypeStruct((Ka, N), F32),
        scratch_shapes=[pltpu.VMEM((tka, tn), F32)],
        compiler_params=_params(("parallel", "parallel", "arbitrary")),
    )(a, b)


def _row_call(body, row_ins, full_ins, row_outs, acc_outs, bs, name):
    S = row_ins[0].shape[0]
    assert S % bs == 0
    in_specs = [pl.BlockSpec((bs, a.shape[1]), lambda i: (i, 0)) for a in row_ins]
    in_specs += [pl.BlockSpec(a.shape, lambda i: (0, 0)) for a in full_ins]
    out_specs = [pl.BlockSpec((bs, c), lambda i: (i, 0)) for c, _ in row_outs]
    out_specs += [pl.BlockSpec(s, lambda i: (0, 0)) for s in acc_outs]
    out_shape = [jax.ShapeDtypeStruct((S, c), dt) for c, dt in row_outs]
    out_shape += [jax.ShapeDtypeStruct(s, F32) for s in acc_outs]
    return pl.pallas_call(
        body, name=name, grid=(S // bs,), in_specs=in_specs, out_specs=out_specs, out_shape=out_shape,
        compiler_params=_params(("arbitrary",)),
    )(*row_ins, *full_ins)


def _rms_vals(x, w):
    r = lax.rsqrt(jnp.mean(x * x, axis=-1, keepdims=True) + RMS_EPS)
    return x * r * w


def _rms_bwd_vals(x, w, dy):
    r = lax.rsqrt(jnp.mean(x * x, axis=-1, keepdims=True) + RMS_EPS)
    xn = x * r
    g = dy * w
    dx = r * (g - xn * jnp.mean(g * xn, axis=-1, keepdims=True))
    dw = jnp.sum(dy * xn, axis=0, keepdims=True)
    return dx, dw


def _acc_add(ref, val):
    @pl.when(pl.program_id(0) == 0)
    def _():
        ref[...] = val

    @pl.when(pl.program_id(0) > 0)
    def _():
        ref[...] += val


def _rms_fwd(x, w):
    def body(x_ref, w_ref, o_ref):
        o_ref[...] = _b(_rms_vals(x_ref[...], w_ref[...]))
    return _row_call(body, [x], [w], [(x.shape[1], BF16)], [], 512, "rms_fwd")[0]


def _gate_fwd(att_o, ssm_o, gl, b_gate):
    def body(a_ref, s_ref, g_ref, b_ref, o_ref):
        g = _sigmoid(g_ref[...] + b_ref[...])
        o_ref[...] = _b(g[:, :D_MODEL] * a_ref[...] + g[:, D_MODEL:] * s_ref[...])
    return _row_call(body, [att_o, ssm_o, gl], [b_gate], [(D_MODEL, BF16)], [], 512, "gate_fwd")[0]


def _post_pre(x, mixed, w_post, w_pre):
    def body(x_ref, m_ref, wp_ref, wn_ref, h_ref, f_ref):
        h = x_ref[...] + _rms_vals(m_ref[...], wp_ref[...])
        h_ref[...] = h
        f_ref[...] = _b(_rms_vals(h, wn_ref[...]))
    return _row_call(body, [x, mixed], [w_post, w_pre], [(D_MODEL, F32), (D_MODEL, BF16)], [], 512,
                     "post_pre")


def _relu2(up):
    def body(u_ref, o_ref):
        r = jnp.maximum(u_ref[...], 0.0)
        o_ref[...] = _b(r * r)
    return _row_call(body, [up], [], [(up.shape[1], BF16)], [], 256, "relu2")[0]


def _final(h1, down, w_post, target):
    def body(h_ref, d_ref, t_ref, w_ref, dh_ref, dd_ref, loss_ref, dw_ref):
        dn = d_ref[...]
        w = w_ref[...]
        err = h_ref[...] + _rms_vals(dn, w) - t_ref[...]
        row = jnp.mean(err * err, axis=-1, keepdims=True)
        part = 0.5 * jnp.sum(row, axis=0, keepdims=True)
        dh = err * (1.0 / D_MODEL)
        dh_ref[...] = dh
        dx, dw = _rms_bwd_vals(dn, w, dh)
        dd_ref[...] = _b(dx)
        _acc_add(loss_ref, jnp.broadcast_to(part, (1, LANES)))
        _acc_add(dw_ref, dw)
    return _row_call(body, [h1, down, target], [w_post], [(D_MODEL, F32), (D_MODEL, BF16)],
                     [(1, LANES), (1, D_MODEL)], 512, "final_loss")


def _dup(da, up):
    def body(a_ref, u_ref, o_ref):
        o_ref[...] = _b(a_ref[...] * (2.0 * jnp.maximum(u_ref[...], 0.0)))
    return _row_call(body, [da, up], [], [(up.shape[1], BF16)], [], 256, "relu2_bwd")[0]


def _mid_bwd(dh2, df, h1, mixed, w_pre, w_post):
    def body(dh_ref, df_ref, h_ref, m_ref, wn_ref, wp_ref, dh1_ref, dm_ref, dwn_ref, dwp_ref):
        dx, dwn = _rms_bwd_vals(h_ref[...], wn_ref[...], df_ref[...])
        dh1 = dh_ref[...] + dx
        dh1_ref[...] = dh1
        dm, dwp = _rms_bwd_vals(m_ref[...], wp_ref[...], dh1)
        dm_ref[...] = _b(dm)
        _acc_add(dwn_ref, dwn)
        _acc_add(dwp_ref, dwp)
    return _row_call(body, [dh2, df, h1, mixed], [w_pre, w_post], [(D_MODEL, F32), (D_MODEL, BF16)],
                     [(1, D_MODEL), (1, D_MODEL)], 512, "mid_bwd")


def _gate_bwd(dmi, att_o, ssm_o, gl, b_gate):
    def body(d_ref, a_ref, s_ref, g_ref, b_ref, da_ref, ds_ref, dg_ref, db_ref):
        g = _sigmoid(g_ref[...] + b_ref[...])
        d = d_ref[...]
        ga, gs = g[:, :D_MODEL], g[:, D_MODEL:]
        da_ref[...] = _b(ga * d)
        ds_ref[...] = _b(gs * d)
        dga = d * a_ref[...] * ga * (1.0 - ga)
        dgs = d * s_ref[...] * gs * (1.0 - gs)
        dg_ref[:, :D_MODEL] = _b(dga)
        dg_ref[:, D_MODEL:] = _b(dgs)
        _acc_add(db_ref.at[:, pl.ds(0, D_MODEL)], jnp.sum(dga, axis=0, keepdims=True))
        _acc_add(db_ref.at[:, pl.ds(D_MODEL, D_MODEL)], jnp.sum(dgs, axis=0, keepdims=True))
    return _row_call(body, [dmi, att_o, ssm_o, gl], [b_gate],
                     [(D_MODEL, BF16), (D_MODEL, BF16), (2 * D_MODEL, BF16)], [(1, 2 * D_MODEL)], 256,
                     "gate_bwd")


def _first_bwd(dh1, du, x, w_pre):
    def body(dh_ref, du_ref, x_ref, w_ref, dx_ref, dw_ref):
        dx, dw = _rms_bwd_vals(x_ref[...], w_ref[...], du_ref[...])
        dx_ref[...] = dh_ref[...] + dx
        _acc_add(dw_ref, dw)
    return _row_call(body, [dh1, du, x], [w_pre], [(D_MODEL, F32)], [(1, D_MODEL)], 512, "first_bwd")


def _group_rms(t):
    gw = SSM_INNER // SSM_GROUPS
    out = []
    for g in range(SSM_GROUPS):
        tg = t[:, g * gw:(g + 1) * gw]
        out.append(lax.rsqrt(jnp.mean(tg * tg, axis=-1, keepdims=True) + RMS_EPS))
    return out


def _gnorm_fwd(y, z, w):
    gw = SSM_INNER // SSM_GROUPS

    def body(y_ref, z_ref, w_ref, o_ref):
        zz = z_ref[...]
        t = y_ref[...] * (zz * _sigmoid(zz))
        rs = _group_rms(t)
        for g in range(SSM_GROUPS):
            sl = slice(g * gw, (g + 1) * gw)
            o_ref[:, sl] = _b(t[:, sl] * rs[g] * w_ref[:, sl])
    return _row_call(body, [y, z], [w], [(SSM_INNER, BF16)], [], 256, "gnorm_fwd")[0]


def _gnorm_bwd(dout, y, z, w):
    gw = SSM_INNER // SSM_GROUPS

    def body(d_ref, y_ref, z_ref, w_ref, dy_ref, dz_ref, dw_ref):
        zz = z_ref[...]
        yy = y_ref[...]
        sg = _sigmoid(zz)
        sz = zz * sg
        t = yy * sz
        rs = _group_rms(t)
        for g in range(SSM_GROUPS):
            sl = slice(g * gw, (g + 1) * gw)
            tn = t[:, sl] * rs[g]
            d = d_ref[:, sl]
            gg = d * w_ref[:, sl]
            dt = rs[g] * (gg - tn * jnp.mean(gg * tn, axis=-1, keepdims=True))
            dy_ref[:, sl] = dt * sz[:, sl]
            dz_ref[:, sl] = _b(dt * yy[:, sl] * (sg[:, sl] * (1.0 + zz[:, sl] * (1.0 - sg[:, sl]))))
            _acc_add(dw_ref.at[:, pl.ds(g * gw, gw)], jnp.sum(d * tn, axis=0, keepdims=True))
    return _row_call(body, [dout, y, z], [w], [(SSM_INNER, F32), (SSM_INNER, BF16)], [(1, SSM_INNER)], 256,
                     "gnorm_bwd")


def _to_pat(a, d):
    if d == 1:
        return a
    S, C = a.shape
    return a.reshape(S // d, d, C).transpose(1, 0, 2).reshape(S, C)


def _from_pat(a, d):
    if d == 1:
        return a
    S, C = a.shape
    return a.reshape(d, S // d, C).transpose(1, 0, 2).reshape(S, C)


def _head_col(stat, h):
    return stat[:, h:h + 1]


def _attn_fwd(q, k, v, d):
    S = q.shape[0]
    blk = ATT_BLOCK
    nblk = S // blk
    nbs = nblk // d
    slopes = _alibi_slopes(N_ATT_HEADS)
    scale = HEAD_DIM ** -0.5

    def body(q_ref, kc_ref, kp_ref, vc_ref, vp_ref, o_ref, m_ref, l_ref):
        n = pl.program_id(0)
        has_prev = (n % nbs) != 0
        ii = lax.broadcasted_iota(jnp.int32, (blk, blk), 0)
        jj = lax.broadcasted_iota(jnp.int32, (blk, blk), 1)
        dist_c = (ii - jj).astype(F32)
        dist_p = dist_c + float(blk)
        ok_c = ii >= jj
        ok_p = jnp.logical_and(jj >= ii, has_prev)
        lane = lax.broadcasted_iota(jnp.int32, (blk, LANES), 1)
        m_all = jnp.zeros((blk, LANES), F32)
        l_all = jnp.zeros((blk, LANES), F32)
        for h in range(N_ATT_HEADS):
            sl = slice(h * HEAD_DIM, (h + 1) * HEAD_DIM)
            qh = q_ref[:, sl]
            bias = slopes[h] * float(d)
            sc = jnp.where(ok_c, _dot_nt(qh, kc_ref[:, sl]) * scale - bias * dist_c, NEG_BIG)
            sp = jnp.where(ok_p, _dot_nt(qh, kp_ref[:, sl]) * scale - bias * dist_p, NEG_BIG)
            m = jnp.maximum(jnp.max(sc, axis=-1, keepdims=True), jnp.max(sp, axis=-1, keepdims=True))
            pc = jnp.exp(sc - m)
            pp = jnp.exp(sp - m)
            l = jnp.sum(pc, axis=-1, keepdims=True) + jnp.sum(pp, axis=-1, keepdims=True)
            o_ref[:, sl] = _dot(_b(pc), vc_ref[:, sl]) + _dot(_b(pp), vp_ref[:, sl])
            m_all = jnp.where(lane == h, m, m_all)
            l_all = jnp.where(lane == h, l, l_all)
        m_ref[...] = m_all
        l_ref[...] = l_all

    cur = pl.BlockSpec((blk, ATT_WIDTH), lambda n: (n, 0))
    prev = pl.BlockSpec((blk, ATT_WIDTH), lambda n: (jnp.maximum(n - 1, 0), 0))
    stat = pl.BlockSpec((blk, LANES), lambda n: (n, 0))
    return pl.pallas_call(
        body, name=f"attn_fwd_d{d}", grid=(nblk,),
        in_specs=[cur, cur, prev, cur, prev],
        out_specs=[cur, stat, stat],
        out_shape=[jax.ShapeDtypeStruct((S, ATT_WIDTH), F32), jax.ShapeDtypeStruct((S, LANES), F32),
                   jax.ShapeDtypeStruct((S, LANES), F32)],
        compiler_params=_params(("parallel",)),
    )(q, k, k, v, v)


def _attn_combine(os, ms, ls):
    def body(o1, o2, o3, m1, m2, m3, l1, l2, l3, att_ref, lse_ref):
        mm = [m1[...], m2[...], m3[...]]
        big = jnp.maximum(jnp.maximum(mm[0], mm[1]), mm[2])
        es = [jnp.exp(m - big) for m in mm]
        den = es[0] * l1[...] + es[1] * l2[...] + es[2] * l3[...]
        lse_ref[...] = big + jnp.log(den)
        inv = 1.0 / den
        for h in range(N_ATT_HEADS):
            sl = slice(h * HEAD_DIM, (h + 1) * HEAD_DIM)
            num = (_head_col(es[0], h) * o1[:, sl] + _head_col(es[1], h) * o2[:, sl]
                   + _head_col(es[2], h) * o3[:, sl])
            att_ref[:, sl] = num * _head_col(inv, h)
    return _row_call(body, list(os) + list(ms) + list(ls), [], [(ATT_WIDTH, F32), (LANES, F32)], [], 256,
                     "attn_combine")


def _attn_delta(d_att, att):
    def body(d_ref, a_ref, dl_ref, db_ref):
        dd = d_ref[...]
        prod = dd * a_ref[...]
        lane = lax.broadcasted_iota(jnp.int32, (dd.shape[0], LANES), 1)
        acc = jnp.zeros((dd.shape[0], LANES), F32)
        for h in range(N_ATT_HEADS):
            s = jnp.sum(prod[:, h * HEAD_DIM:(h + 1) * HEAD_DIM], axis=-1, keepdims=True)
            acc = jnp.where(lane == h, s, acc)
        dl_ref[...] = acc
        db_ref[...] = _b(dd)
    return _row_call(body, [d_att, att], [], [(LANES, F32), (ATT_WIDTH, BF16)], [], 512, "attn_delta")


def _attn_bwd(q, k, v, do, lse, delta, d):
    S = q.shape[0]
    blk = ATT_BLOCK
    nblk = S // blk
    nbs = nblk // d
    slopes = _alibi_slopes(N_ATT_HEADS)
    scale = HEAD_DIM ** -0.5

    def body(qc_ref, qn_ref, k_ref, v_ref, doc_ref, don_ref, lc_ref, ln_ref, dc_ref, dn_ref,
             dq_ref, dk_ref, dv_ref, carry_ref):
        n = pl.program_id(0)
        has_next = ((n + 1) % nbs) != 0

        @pl.when(n == 0)
        def _():
            carry_ref[...] = jnp.zeros_like(carry_ref)

        ii = lax.broadcasted_iota(jnp.int32, (blk, blk), 0)
        jj = lax.broadcasted_iota(jnp.int32, (blk, blk), 1)
        dist_c = (ii - jj).astype(F32)
        dist_p = dist_c + float(blk)
        ok_c = ii >= jj
        ok_p = jnp.logical_and(jj >= ii, has_next)
        for h in range(N_ATT_HEADS):
            sl = slice(h * HEAD_DIM, (h + 1) * HEAD_DIM)
            bias = slopes[h] * float(d)
            kh = k_ref[:, sl]
            vh = v_ref[:, sl]
            qh = qc_ref[:, sl]
            doh = doc_ref[:, sl]
            s = jnp.where(ok_c, _dot_nt(qh, kh) * scale - bias * dist_c - _head_col(lc_ref[...], h), NEG_BIG)
            p = jnp.exp(s)
            ds = p * (_dot_nt(doh, vh) - _head_col(dc_ref[...], h)) * scale
            pb, dsb = _b(p), _b(ds)
            dv = _dot_tn(pb, doh)
            dk = _dot_tn(dsb, qh)
            dq_ref[:, sl] = _dot(dsb, kh) + carry_ref[:, sl]
            qh = qn_ref[:, sl]
            doh = don_ref[:, sl]
            s = jnp.where(ok_p, _dot_nt(qh, kh) * scale - bias * dist_p - _head_col(ln_ref[...], h), NEG_BIG)
            p = jnp.exp(s)
            ds = p * (_dot_nt(doh, vh) - _head_col(dn_ref[...], h)) * scale
            pb, dsb = _b(p), _b(ds)
            dv_ref[:, sl] = dv + _dot_tn(pb, doh)
            dk_ref[:, sl] = dk + _dot_tn(dsb, qh)
            carry_ref[:, sl] = _dot(dsb, kh)

    cur = pl.BlockSpec((blk, ATT_WIDTH), lambda n: (n, 0))
    nxt = pl.BlockSpec((blk, ATT_WIDTH), lambda n: (jnp.minimum(n + 1, nblk - 1), 0))
    scur = pl.BlockSpec((blk, LANES), lambda n: (n, 0))
    snxt = pl.BlockSpec((blk, LANES), lambda n: (jnp.minimum(n + 1, nblk - 1), 0))
    shp = jax.ShapeDtypeStruct((S, ATT_WIDTH), F32)
    return pl.pallas_call(
        body, name=f"attn_bwd_d{d}", grid=(nblk,),
        in_specs=[cur, nxt, cur, cur, cur, nxt, scur, snxt, scur, snxt],
        out_specs=[cur, cur, cur],
        out_shape=[shp, shp, shp],
        scratch_shapes=[pltpu.VMEM((blk, ATT_WIDTH), F32)],
        compiler_params=_params(("arbitrary",)),
    )(q, q, k, v, do, do, lse, lse, delta, delta)


def _head_pair_masks(x):
    lane = lax.broadcasted_iota(jnp.int32, x.shape, 1)
    zero = jnp.zeros_like(x)
    return jnp.where(lane < HEAD_DIM, x, zero), jnp.where(lane >= HEAD_DIM, x, zero)


def _attn_fwd2(qkv, d, comm=None):
    S = qkv.shape[0]
    blk = ATT_BLOCK
    nblk = S // blk
    nbs = nblk // d
    slopes = _alibi_slopes(N_ATT_HEADS)
    scale = HEAD_DIM ** -0.5
    n_ci = len(comm.ins) if comm else 0
    n_co = len(comm.outs) if comm else 0

    def body(*refs):
        q_ref, kc_ref, kp_ref, vc_ref, vp_ref = refs[:5]
        o_ref, m_ref, l_ref = refs[5 + n_ci:8 + n_ci]
        n = pl.program_id(0)
        if comm:
            c_args = (refs[5:5 + n_ci], refs[8 + n_ci:8 + n_ci + n_co], refs[-2], refs[-1])

            @pl.when(n == 0)
            def _():
                comm.start(*c_args)

        has_prev = (n % nbs) != 0
        ii = lax.broadcasted_iota(jnp.int32, (blk, 2 * blk), 0)
        jj = lax.broadcasted_iota(jnp.int32, (blk, 2 * blk), 1)
        dist_i = blk + ii - jj
        dist = dist_i.astype(F32)
        ok = jnp.logical_and(jnp.logical_and(dist_i >= 0, dist_i <= blk), jnp.logical_or(jj >= blk, has_prev))
        lane = lax.broadcasted_iota(jnp.int32, (blk, LANES), 1)
        m_all = jnp.zeros((blk, LANES), F32)
        l_all = jnp.zeros((blk, LANES), F32)
        for pr in range(N_ATT_HEADS // 2):
            sl = slice(pr * LANES, (pr + 1) * LANES)
            kcat = jnp.concatenate([kp_ref[:, sl], kc_ref[:, sl]], axis=0)
            vcat = jnp.concatenate([vp_ref[:, sl], vc_ref[:, sl]], axis=0)
            ps = []
            for h, qh in zip((2 * pr, 2 * pr + 1), _head_pair_masks(q_ref[:, sl])):
                s = jnp.where(ok, _dot_nt(qh, kcat) * scale - (slopes[h] * float(d)) * dist, NEG_BIG)
                m = jnp.max(s, axis=-1, keepdims=True)
                p = jnp.exp(s - m)
                l = jnp.sum(p, axis=-1, keepdims=True)
                m_all = jnp.where(lane == h, m, m_all)
                l_all = jnp.where(lane == h, l, l_all)
                ps.append(_b(p))
            o_ref[:, sl] = _dot(jnp.concatenate(ps, axis=1), jnp.concatenate(_head_pair_masks(vcat), axis=0))
        m_ref[...] = m_all
        l_ref[...] = l_all
        if comm:
            @pl.when(n == nblk - 1)
            def _():
                comm.finish(*c_args)

    cur = lambda c: pl.BlockSpec((blk, ATT_WIDTH), lambda n: (n, c))
    prev = lambda c: pl.BlockSpec((blk, ATT_WIDTH), lambda n: (jnp.maximum(n - 1, 0), c))
    stat = pl.BlockSpec((blk, LANES), lambda n: (n, 0))
    if comm:
        scratch = [pltpu.SemaphoreType.DMA((comm.n_sems,))] * 2
        params = pltpu.CompilerParams(dimension_semantics=("arbitrary",), vmem_limit_bytes=VMEM_LIMIT,
                                      has_side_effects=True)
    else:
        scratch, params = [], _params(("parallel",))
    outs = pl.pallas_call(
        body, name=f"attn_fwd_d{d}", grid=(nblk,),
        in_specs=[cur(0), cur(1), prev(1), cur(2), prev(2)] + [ANY] * n_ci,
        out_specs=[cur(0), stat, stat] + [ANY] * n_co,
        out_shape=[jax.ShapeDtypeStruct((S, ATT_WIDTH), F32), jax.ShapeDtypeStruct((S, LANES), F32),
                   jax.ShapeDtypeStruct((S, LANES), F32)] + list(comm.outs if comm else []),
        scratch_shapes=scratch,
        compiler_params=params,
    )(qkv, qkv, qkv, qkv, qkv, *(comm.ins if comm else []))
    return (outs[0], outs[1], outs[2], outs[3:]) if comm else outs


def _attn_bwd2(qkv, do, lse, delta, d, comm=None):
    S = qkv.shape[0]
    blk = ATT_BLOCK
    nblk = S // blk
    nbs = nblk // d
    slopes = _alibi_slopes(N_ATT_HEADS)
    scale = HEAD_DIM ** -0.5
    n_ci = len(comm.ins) if comm else 0
    n_co = len(comm.outs) if comm else 0

    def body(*refs):
        qc_ref, qn_ref, k_ref, v_ref, doc_ref, don_ref, lc_ref, ln_ref, dc_ref, dn_ref = refs[:10]
        dq_ref, dk_ref, dv_ref = refs[10 + n_ci:13 + n_ci]
        carry_ref = refs[13 + n_ci + n_co]
        n = pl.program_id(0)
        has_next = ((n + 1) % nbs) != 0
        if comm:
            c_args = (refs[10:10 + n_ci], refs[13 + n_ci:13 + n_ci + n_co], refs[-2], refs[-1])

        @pl.when(n == 0)
        def _():
            carry_ref[...] = jnp.zeros_like(carry_ref)
            if comm:
                comm.start(*c_args)

        rr = lax.broadcasted_iota(jnp.int32, (2 * blk, blk), 0)
        jj = lax.broadcasted_iota(jnp.int32, (2 * blk, blk), 1)
        dist_i = rr - jj
        dist = dist_i.astype(F32)
        ok = jnp.logical_or(jnp.logical_and(rr < blk, dist_i >= 0),
                            jnp.logical_and(jnp.logical_and(rr >= blk, dist_i <= blk), has_next))
        lcat = jnp.concatenate([lc_ref[...], ln_ref[...]], axis=0)
        dcat = jnp.concatenate([dc_ref[...], dn_ref[...]], axis=0)
        for pr in range(N_ATT_HEADS // 2):
            sl = slice(pr * LANES, (pr + 1) * LANES)
            qcat = jnp.concatenate([qc_ref[:, sl], qn_ref[:, sl]], axis=0)
            docat = jnp.concatenate([doc_ref[:, sl], don_ref[:, sl]], axis=0)
            k2 = k_ref[:, sl]
            v2 = v_ref[:, sl]
            qm = _head_pair_masks(qcat)
            dom = _head_pair_masks(docat)
            pbs, dsbs = [], []
            for h, qh, doh in zip((2 * pr, 2 * pr + 1), qm, dom):
                s = jnp.where(ok, _dot_nt(qh, k2) * scale - (slopes[h] * float(d)) * dist - lcat[:, h:h + 1],
                              NEG_BIG)
                p = jnp.exp(s)
                ds = p * (_dot_nt(doh, v2) - dcat[:, h:h + 1]) * scale
                pbs.append(_b(p))
                dsbs.append(_b(ds))
            dv_ref[:, sl] = _b(_dot_tn(jnp.concatenate(pbs, axis=0), jnp.concatenate(dom, axis=0)))
            dk_ref[:, sl] = _b(_dot_tn(jnp.concatenate(dsbs, axis=0), jnp.concatenate(qm, axis=0)))
            dq = _dot(jnp.concatenate(dsbs, axis=1), jnp.concatenate(_head_pair_masks(k2), axis=0))
            dq_ref[:, sl] = _b(dq[:blk] + carry_ref[:, sl])
            carry_ref[:, sl] = dq[blk:]

        if comm:
            @pl.when(n == nblk - 1)
            def _():
                comm.finish(*c_args)

    cur = lambda c: pl.BlockSpec((blk, ATT_WIDTH), lambda n: (n, c))
    nxt = lambda c: pl.BlockSpec((blk, ATT_WIDTH), lambda n: (jnp.minimum(n + 1, nblk - 1), c))
    scur = pl.BlockSpec((blk, LANES), lambda n: (n, 0))
    snxt = pl.BlockSpec((blk, LANES), lambda n: (jnp.minimum(n + 1, nblk - 1), 0))
    shp = jax.ShapeDtypeStruct((S, ATT_WIDTH), BF16)
    scratch = [pltpu.VMEM((blk, ATT_WIDTH), F32)]
    if comm:
        scratch += [pltpu.SemaphoreType.DMA((comm.n_sems,))] * 2
        params = pltpu.CompilerParams(dimension_semantics=("arbitrary",), vmem_limit_bytes=VMEM_LIMIT,
                                      has_side_effects=True)
    else:
        params = _params(("arbitrary",))
    outs = pl.pallas_call(
        body, name=f"attn_bwd_d{d}", grid=(nblk,),
        in_specs=[cur(0), nxt(0), cur(1), cur(2), cur(0), nxt(0), scur, snxt, scur, snxt] + [ANY] * n_ci,
        out_specs=[cur(0), cur(0), cur(0)] + [ANY] * n_co,
        out_shape=[shp, shp, shp] + list(comm.outs if comm else []),
        scratch_shapes=scratch,
        compiler_params=params,
    )(qkv, qkv, qkv, qkv, do, do, lse, lse, delta, delta, *(comm.ins if comm else []))
    return (outs[0], outs[1], outs[2], outs[3:]) if comm else outs


LAYOUT_TILE = 512
DILATED = tuple(d for d in DILATIONS if d > 1)


def _pat_spec(d, cols, col_block=0):
    return pl.BlockSpec((d, LAYOUT_TILE // d, cols), lambda i: (0, i, col_block))


def _pat_view(a, d):
    return a.reshape(d, a.shape[0] // d, a.shape[1])


def _qkv_layouts(qkv):
    S, C = qkv.shape
    t = LAYOUT_TILE

    def body(x_ref, nat_ref, *refs):
        pat_refs, slab = refs[:-1], refs[-1]
        nat_ref[...] = _b(x_ref[...])
        _to_slabs(slab, x_ref)
        for d, p_ref in zip(DILATED, pat_refs):
            _gather_pattern(p_ref, slab, d, BF16)

    outs = pl.pallas_call(
        body, name="qkv_layouts", grid=(S // t,),
        in_specs=[pl.BlockSpec((t, C), lambda i: (i, 0))],
        out_specs=[pl.BlockSpec((t, C), lambda i: (i, 0))] + [_pat_spec(d, C) for d in DILATED],
        out_shape=[jax.ShapeDtypeStruct((S, C), BF16)]
        + [jax.ShapeDtypeStruct((d, S // d, C), BF16) for d in DILATED],
        scratch_shapes=[pltpu.VMEM((C // LANES, t, LANES), F32)],
        compiler_params=_params(("parallel",)),
    )(qkv)
    return [outs[0]] + [o.reshape(S, C) for o in outs[1:]]


def _to_slabs(slab_ref, src_ref):
    for cb in range(slab_ref.shape[0]):
        slab_ref[cb] = src_ref[:, cb * LANES:(cb + 1) * LANES].astype(F32)


def _gather_pattern(dst_ref, slab_ref, d, dtype):
    t = slab_ref.shape[1]
    for cb in range(slab_ref.shape[0]):
        one = slab_ref.at[cb]
        for r in range(d):
            dst_ref[r, :, cb * LANES:(cb + 1) * LANES] = one[pl.ds(r, t // d, stride=d), :].astype(dtype)


def _scatter_pattern(slab_ref, src_ref, d, add=False):
    t = slab_ref.shape[1]
    for cb in range(slab_ref.shape[0]):
        one = slab_ref.at[cb]
        for r in range(d):
            idx = pl.ds(r, t // d, stride=d)
            val = src_ref[r, :, cb * LANES:(cb + 1) * LANES]
            if add:
                val = val + one[idx, :]
            one[idx, :] = val


def _attn_combine2(os, ms, ls):
    S = os[0].shape[0]
    t = LAYOUT_TILE

    def body(o1, o2, o3, m1, m2, m3, l1, l2, l3, att_ref, lse_ref, so2, so3, sm2, sm3, sl2, sl3):
        for d, src, dst in ((DILATED[0], o2, so2), (DILATED[1], o3, so3), (DILATED[0], m2, sm2),
                            (DILATED[1], m3, sm3), (DILATED[0], l2, sl2), (DILATED[1], l3, sl3)):
            _scatter_pattern(dst, src, d)
        mm = [m1[...], sm2[0], sm3[0]]
        big = jnp.maximum(jnp.maximum(mm[0], mm[1]), mm[2])
        es = [jnp.exp(m - big) for m in mm]
        den = es[0] * l1[...] + es[1] * sl2[0] + es[2] * sl3[0]
        lse_ref[...] = big + jnp.log(den)
        inv = 1.0 / den
        for h in range(N_ATT_HEADS):
            sl = slice(h * HEAD_DIM, (h + 1) * HEAD_DIM)
            cb, hl = divmod(h, 2)
            sll = slice(hl * HEAD_DIM, (hl + 1) * HEAD_DIM)
            num = (_head_col(es[0], h) * o1[:, sl] + _head_col(es[1], h) * so2[cb, :, sll]
                   + _head_col(es[2], h) * so3[cb, :, sll])
            att_ref[:, sl] = num * _head_col(inv, h)

    def specs(c):
        return [pl.BlockSpec((t, c), lambda i: (i, 0))] + [_pat_spec(d, c) for d in DILATED]

    args = [os[0]] + [_pat_view(o, d) for o, d in zip(os[1:], DILATED)]
    args += [ms[0]] + [_pat_view(m, d) for m, d in zip(ms[1:], DILATED)]
    args += [ls[0]] + [_pat_view(l, d) for l, d in zip(ls[1:], DILATED)]
    return pl.pallas_call(
        body, name="attn_combine", grid=(S // t,),
        in_specs=specs(ATT_WIDTH) + specs(LANES) + specs(LANES),
        out_specs=[pl.BlockSpec((t, ATT_WIDTH), lambda i: (i, 0)), pl.BlockSpec((t, LANES), lambda i: (i, 0))],
        out_shape=[jax.ShapeDtypeStruct((S, ATT_WIDTH), F32), jax.ShapeDtypeStruct((S, LANES), F32)],
        scratch_shapes=[pltpu.VMEM((ATT_WIDTH // LANES, t, LANES), F32)] * 2
        + [pltpu.VMEM((1, t, LANES), F32)] * 4,
        compiler_params=_params(("parallel",)),
    )(*args)


def _attn_delta2(d_att, att, lse):
    S = d_att.shape[0]
    t = LAYOUT_TILE

    def body(d_ref, a_ref, l_ref, *refs):
        out_refs, d_slab, l_slab, dl_slab = refs[:-3], refs[-3], refs[-2], refs[-1]
        dd = d_ref[...]
        prod = dd * a_ref[...]
        lane = lax.broadcasted_iota(jnp.int32, (t, LANES), 1)
        acc = jnp.zeros((t, LANES), F32)
        for h in range(N_ATT_HEADS):
            s = jnp.sum(prod[:, h * HEAD_DIM:(h + 1) * HEAD_DIM], axis=-1, keepdims=True)
            acc = jnp.where(lane == h, s, acc)
        out_refs[0][...] = _b(dd)
        out_refs[1][...] = acc
        _to_slabs(d_slab, d_ref)
        l_slab[0] = l_ref[...]
        dl_slab[0] = acc
        for k, d in enumerate(DILATED):
            db_ref, ls_ref, dl_ref = out_refs[2 + 3 * k:5 + 3 * k]
            _gather_pattern(db_ref, d_slab, d, BF16)
            _gather_pattern(ls_ref, l_slab, d, F32)
            _gather_pattern(dl_ref, dl_slab, d, F32)

    nat = lambda c: pl.BlockSpec((t, c), lambda i: (i, 0))
    out_specs = [nat(ATT_WIDTH), nat(LANES)]
    out_shape = [jax.ShapeDtypeStruct((S, ATT_WIDTH), BF16), jax.ShapeDtypeStruct((S, LANES), F32)]
    for d in DILATED:
        out_specs += [_pat_spec(d, ATT_WIDTH), _pat_spec(d, LANES), _pat_spec(d, LANES)]
        out_shape += [jax.ShapeDtypeStruct((d, S // d, ATT_WIDTH), BF16),
                      jax.ShapeDtypeStruct((d, S // d, LANES), F32),
                      jax.ShapeDtypeStruct((d, S // d, LANES), F32)]
    outs = pl.pallas_call(
        body, name="attn_delta", grid=(S // t,),
        in_specs=[nat(ATT_WIDTH), nat(ATT_WIDTH), nat(LANES)],
        out_specs=out_specs, out_shape=out_shape,
        scratch_shapes=[pltpu.VMEM((ATT_WIDTH // LANES, t, LANES), F32), pltpu.VMEM((1, t, LANES), F32),
                        pltpu.VMEM((1, t, LANES), F32)],
        compiler_params=_params(("parallel",)),
    )(d_att, att, lse)
    res = [(outs[0], lse, outs[1])]
    for k in range(len(DILATED)):
        db, ls, dl = outs[2 + 3 * k:5 + 3 * k]
        res.append((db.reshape(S, ATT_WIDTH), ls.reshape(S, LANES), dl.reshape(S, LANES)))
    return res


def _sum_qkv2(dqs, dks, dvs):
    S = dqs[0].shape[0]
    t = LAYOUT_TILE

    def body(*refs):
        o_ref, scr = refs[-2], refs[-1]
        for part in range(3):
            nat_ref, p_refs = refs[3 * part], refs[3 * part + 1:3 * part + 3]
            _to_slabs(scr, nat_ref)
            for d, p_ref in zip(DILATED, p_refs):
                _scatter_pattern(scr, p_ref, d, add=True)
            for cb in range(ATT_WIDTH // LANES):
                o_ref[:, part * ATT_WIDTH + cb * LANES:part * ATT_WIDTH + (cb + 1) * LANES] = _b(scr[cb])

    in_specs, args = [], []
    for group in (dqs, dks, dvs):
        in_specs += [pl.BlockSpec((t, ATT_WIDTH), lambda i: (i, 0))] + [_pat_spec(d, ATT_WIDTH) for d in DILATED]
        args += [group[0]] + [_pat_view(a, d) for a, d in zip(group[1:], DILATED)]
    return pl.pallas_call(
        body, name="sum_dqkv", grid=(S // t,),
        in_specs=in_specs,
        out_specs=pl.BlockSpec((t, 3 * ATT_WIDTH), lambda i: (i, 0)),
        out_shape=jax.ShapeDtypeStruct((S, 3 * ATT_WIDTH), BF16),
        scratch_shapes=[pltpu.VMEM((ATT_WIDTH // LANES, t, LANES), F32)],
        compiler_params=_params(("parallel",)),
    )(*args)


def _sum_qkv(dqs, dks, dvs):
    def body(q1, q2, q3, k1, k2, k3, v1, v2, v3, o_ref):
        o_ref[:, 0:ATT_WIDTH] = _b(q1[...] + q2[...] + q3[...])
        o_ref[:, ATT_WIDTH:2 * ATT_WIDTH] = _b(k1[...] + k2[...] + k3[...])
        o_ref[:, 2 * ATT_WIDTH:] = _b(v1[...] + v2[...] + v3[...])
    return _row_call(body, list(dqs) + list(dks) + list(dvs), [], [(3 * ATT_WIDTH, BF16)], [], 256,
                     "sum_dqkv")[0]


CONV_COLS = 1024
CONV_ROWS = 512
HALO = 8


def _conv_fwd(xbc, conv_w, conv_b):
    S, C = xbc.shape
    bs, bc = CONV_ROWS, CONV_COLS
    nr = S // bs

    def body(x_ref, halo_ref, w_ref, b_ref, o_ref, xs_ref):
        r = pl.program_id(1)
        xs_ref[pl.ds(HALO, bs), :] = x_ref[...]
        xs_ref[pl.ds(0, HALO), :] = jnp.where(r > 0, halo_ref[...], 0.0)
        pre = b_ref[...] + w_ref[3:4, :] * x_ref[...]
        for j in range(SSM_CONV - 1):
            pre = pre + w_ref[j:j + 1, :] * xs_ref[pl.ds(HALO - 3 + j, bs), :]
        o_ref[...] = pre * _sigmoid(pre)

    return pl.pallas_call(
        body, name="conv_fwd", grid=(C // bc, nr),
        in_specs=[pl.BlockSpec((bs, bc), lambda c, r: (r, c)),
                  pl.BlockSpec((HALO, bc), lambda c, r: (jnp.maximum(r * (bs // HALO) - 1, 0), c)),
                  pl.BlockSpec((SSM_CONV, bc), lambda c, r: (0, c)),
                  pl.BlockSpec((1, bc), lambda c, r: (0, c))],
        out_specs=pl.BlockSpec((bs, bc), lambda c, r: (r, c)),
        out_shape=jax.ShapeDtypeStruct((S, C), F32),
        scratch_shapes=[pltpu.VMEM((bs + HALO, bc), F32)],
        compiler_params=_params(("parallel", "arbitrary")),
    )(xbc, xbc, conv_w, conv_b)


def _conv_bwd(xbc, dact, conv_w, conv_b, col0):
    S, C = xbc.shape
    Cp = dact.shape[1]
    bs, bc = CONV_ROWS, min(CONV_COLS, Cp)
    nr = S // bs
    cb0 = col0 // bc
    last_halo = S // HALO - 1

    def body(x_ref, xp_ref, xn_ref, d_ref, dn_ref, w_ref, b_ref, dx_ref, dw_ref, db_ref,
             xs_ref, dp_ref):
        r = pl.program_id(1)
        xs_ref[pl.ds(0, HALO), :] = jnp.where(r > 0, xp_ref[...], 0.0)
        xs_ref[pl.ds(HALO, bs), :] = x_ref[...]
        xs_ref[pl.ds(HALO + bs, HALO), :] = xn_ref[...]
        ext = bs + HALO
        pre = b_ref[...] + jnp.zeros((ext, bc), F32)
        for j in range(SSM_CONV):
            pre = pre + w_ref[j:j + 1, :] * xs_ref[pl.ds(HALO - 3 + j, ext), :]
        sg = _sigmoid(pre)
        dsilu = sg * (1.0 + pre * (1.0 - sg))
        dp_ref[pl.ds(0, bs), :] = d_ref[...] * dsilu[:bs]
        dp_ref[pl.ds(bs, HALO), :] = jnp.where(r < nr - 1, dn_ref[...], 0.0) * dsilu[bs:]
        dx = jnp.zeros((bs, bc), F32)
        for j in range(SSM_CONV):
            dx = dx + w_ref[j:j + 1, :] * dp_ref[pl.ds(3 - j, bs), :]
        dx_ref[...] = _b(dx)
        dpre = dp_ref[pl.ds(0, bs), :]
        for j in range(SSM_CONV):
            part = jnp.sum(dpre * xs_ref[pl.ds(HALO - 3 + j, bs), :], axis=0, keepdims=True)

            @pl.when(r == 0)
            def _():
                dw_ref[j:j + 1, :] = part

            @pl.when(r > 0)
            def _():
                dw_ref[j:j + 1, :] += part
        part = jnp.sum(dpre, axis=0, keepdims=True)

        @pl.when(r == 0)
        def _():
            db_ref[...] = part

        @pl.when(r > 0)
        def _():
            db_ref[...] += part

    hb = bs // HALO
    return pl.pallas_call(
        body, name=f"conv_bwd_{col0}", grid=(Cp // bc, nr),
        in_specs=[pl.BlockSpec((bs, bc), lambda c, r: (r, cb0 + c)),
                  pl.BlockSpec((HALO, bc), lambda c, r: (jnp.maximum(r * hb - 1, 0), cb0 + c)),
                  pl.BlockSpec((HALO, bc), lambda c, r: (jnp.minimum((r + 1) * hb, last_halo), cb0 + c)),
                  pl.BlockSpec((bs, bc), lambda c, r: (r, c)),
                  pl.BlockSpec((HALO, bc), lambda c, r: (jnp.minimum((r + 1) * hb, last_halo), c)),
                  pl.BlockSpec((SSM_CONV, bc), lambda c, r: (0, cb0 + c)),
                  pl.BlockSpec((1, bc), lambda c, r: (0, cb0 + c))],
        out_specs=[pl.BlockSpec((bs, bc), lambda c, r: (r, c)),
                   pl.BlockSpec((SSM_CONV, bc), lambda c, r: (0, c)),
                   pl.BlockSpec((1, bc), lambda c, r: (0, c))],
        out_shape=[jax.ShapeDtypeStruct((S, Cp), BF16), jax.ShapeDtypeStruct((SSM_CONV, Cp), F32),
                   jax.ShapeDtypeStruct((1, Cp), F32)],
        scratch_shapes=[pltpu.VMEM((bs + 2 * HALO, bc), F32), pltpu.VMEM((bs + HALO, bc), F32)],
        compiler_params=_params(("parallel", "arbitrary")),
    )(xbc, xbc, xbc, dact, dact, conv_w, conv_b)


def _shift_down(x, k, top_src):
    r8 = lax.broadcasted_iota(jnp.int32, (HALO, x.shape[1]), 0)
    rolled = pltpu.roll(x, k, 0)
    top = jnp.where(r8 < k, pltpu.roll(top_src, k, 0), rolled[0:HALO])
    if x.shape[0] == HALO:
        return top
    return jnp.concatenate([top, rolled[HALO:]], axis=0)


def _shift_up(x, k, bottom_src):
    n = x.shape[0]
    r8 = lax.broadcasted_iota(jnp.int32, (HALO, x.shape[1]), 0)
    rolled = pltpu.roll(x, n - k, 0)
    bottom = jnp.where(r8 >= HALO - k, pltpu.roll(bottom_src, HALO - k, 0), rolled[n - HALO:n])
    return jnp.concatenate([rolled[:n - HALO], bottom], axis=0)


def _conv_pre(x, top_src, w_ref, b_ref):
    shifted = [x] + [_shift_down(x, k, top_src) for k in range(1, SSM_CONV)]
    pre = b_ref[...] + w_ref[SSM_CONV - 1:SSM_CONV, :] * x
    for k in range(1, SSM_CONV):
        pre = pre + w_ref[SSM_CONV - 1 - k:SSM_CONV - k, :] * shifted[k]
    return pre, shifted


def _conv_fwd2(xbc, conv_w, conv_b):
    S, C = xbc.shape
    bs, bc = CONV_ROWS, CONV_COLS
    nr = S // bs

    def body(x_ref, halo_ref, w_ref, b_ref, o_ref):
        r = pl.program_id(1)
        halo = jnp.where(r > 0, halo_ref[...], 0.0)
        pre, _ = _conv_pre(x_ref[...], halo, w_ref, b_ref)
        o_ref[...] = pre * _sigmoid(pre)

    return pl.pallas_call(
        body, name="conv_fwd", grid=(C // bc, nr),
        in_specs=[pl.BlockSpec((bs, bc), lambda c, r: (r, c)),
                  pl.BlockSpec((HALO, bc), lambda c, r: (jnp.maximum(r * (bs // HALO) - 1, 0), c)),
                  pl.BlockSpec((SSM_CONV, bc), lambda c, r: (0, c)),
                  pl.BlockSpec((1, bc), lambda c, r: (0, c))],
        out_specs=pl.BlockSpec((bs, bc), lambda c, r: (r, c)),
        out_shape=jax.ShapeDtypeStruct((S, C), F32),
        compiler_params=_params(("parallel", "arbitrary")),
    )(xbc, xbc, conv_w, conv_b)


def _conv_bwd2(xbc, dact, conv_w, conv_b):
    S, C = xbc.shape
    bs, bc = CONV_ROWS, CONV_COLS
    nr = S // bs
    hb = bs // HALO
    last_halo = S // HALO - 1

    def dsilu(pre):
        sg = _sigmoid(pre)
        return sg * (1.0 + pre * (1.0 - sg))

    def body(x_ref, xp_ref, xn_ref, d_ref, dn_ref, w_ref, b_ref, dx_ref, dw_ref, db_ref):
        r = pl.program_id(1)
        x = x_ref[...]
        pre, shifted = _conv_pre(x, jnp.where(r > 0, xp_ref[...], 0.0), w_ref, b_ref)
        dpre = d_ref[...] * dsilu(pre)
        pre_n, _ = _conv_pre(xn_ref[...], x[bs - HALO:bs], w_ref, b_ref)
        dpre_n = jnp.where(r < nr - 1, dn_ref[...], 0.0) * dsilu(pre_n)
        dx = w_ref[SSM_CONV - 1:SSM_CONV, :] * dpre
        for k in range(1, SSM_CONV):
            dx = dx + w_ref[SSM_CONV - 1 - k:SSM_CONV - k, :] * _shift_up(dpre, k, dpre_n)
        dx_ref[...] = _b(dx)
        parts = [jnp.sum(dpre * shifted[SSM_CONV - 1 - j], axis=0, keepdims=True) for j in range(SSM_CONV)]
        dbp = jnp.sum(dpre, axis=0, keepdims=True)

        @pl.when(r == 0)
        def _():
            for j in range(SSM_CONV):
                dw_ref[j:j + 1, :] = parts[j]
            db_ref[...] = dbp

        @pl.when(r > 0)
        def _():
            for j in range(SSM_CONV):
                dw_ref[j:j + 1, :] += parts[j]
            db_ref[...] += dbp

    return pl.pallas_call(
        body, name="conv_bwd", grid=(C // bc, nr),
        in_specs=[pl.BlockSpec((bs, bc), lambda c, r: (r, c)),
                  pl.BlockSpec((HALO, bc), lambda c, r: (jnp.maximum(r * hb - 1, 0), c)),
                  pl.BlockSpec((HALO, bc), lambda c, r: (jnp.minimum((r + 1) * hb, last_halo), c)),
                  pl.BlockSpec((bs, bc), lambda c, r: (r, c)),
                  pl.BlockSpec((HALO, bc), lambda c, r: (jnp.minimum((r + 1) * hb, last_halo), c)),
                  pl.BlockSpec((SSM_CONV, bc), lambda c, r: (0, c)),
                  pl.BlockSpec((1, bc), lambda c, r: (0, c))],
        out_specs=[pl.BlockSpec((bs, bc), lambda c, r: (r, c)),
                   pl.BlockSpec((SSM_CONV, bc), lambda c, r: (0, c)),
                   pl.BlockSpec((1, bc), lambda c, r: (0, c))],
        out_shape=[jax.ShapeDtypeStruct((S, C), BF16), jax.ShapeDtypeStruct((SSM_CONV, C), F32),
                   jax.ShapeDtypeStruct((1, C), F32)],
        compiler_params=_params(("parallel", "arbitrary")),
    )(xbc, xbc, xbc, dact, dact, conv_w, conv_b)


def _softplus(x):
    return jnp.maximum(x, 0.0) + jnp.log(1.0 + jnp.exp(-jnp.abs(x)))


def _ssd_common(dtr_ref, bias_ref, a_ref, g):
    ch = SSM_CHUNK
    x = dtr_ref[...] + bias_ref[...]
    dt_all = _softplus(x)
    r = lax.broadcasted_iota(jnp.int32, (LANES, LANES), 0)
    c = lax.broadcasted_iota(jnp.int32, (LANES, LANES), 1)
    sel = jnp.where(jnp.logical_and(r == HEADS_PER_GROUP * g + c, c < HEADS_PER_GROUP), 1.0, 0.0)
    dt4 = _dot_hi(dt_all, sel)
    la4 = _dot_hi(dt_all * a_ref[...], sel)
    ii = lax.broadcasted_iota(jnp.int32, (ch, ch), 0)
    jj = lax.broadcasted_iota(jnp.int32, (ch, ch), 1)
    tril = jnp.where(ii >= jj, 1.0, 0.0)
    acs = _dot_hi(tril, la4)
    return x, sel, dt4, acs, acs.T, ii >= jj


def _row8(v):
    return jnp.broadcast_to(v, (8, v.shape[1]))


def _ssd_fwd(xact, dt_raw, dt_bias, a_neg, d_skip):
    S = xact.shape[0]
    ch = SSM_CHUNK
    nch = S // ch
    hg = HEADS_PER_GROUP
    gw = hg * SSM_HEAD_DIM
    b_off = SSM_INNER // SSM_STATE
    c_off = b_off + SSM_GROUPS

    def body(x_ref, b_ref, c_ref, dtr_ref, bias_ref, a_ref, dsk_ref, y_ref, hs_ref, h_ref):
        c = pl.program_id(0)
        g = pl.program_id(1)

        @pl.when(jnp.logical_and(c == 0, g == 0))
        def _():
            h_ref[...] = jnp.zeros_like(h_ref)

        _, sel, dt4, acs, acs_t, low = _ssd_common(dtr_ref, bias_ref, a_ref, g)
        dsk4 = _dot_hi(_row8(dsk_ref[...]), sel)
        bb = _b(b_ref[...])
        cc = _b(c_ref[...])
        cb = _dot_nt(cc, bb)
        for j in range(hg):
            sl = slice(j * SSM_HEAD_DIM, (j + 1) * SSM_HEAD_DIM)
            acol = acs[:, j:j + 1]
            arow = acs_t[j:j + 1, :]
            alast = acs[ch - 1:ch, j:j + 1]
            decay = jnp.exp(jnp.where(low, acol - arow, -jnp.inf))
            xh = x_ref[:, sl]
            xd = xh * dt4[:, j:j + 1]
            hj = h_ref[hg * g + j]
            y = _dot(_b(cb * decay), _b(xd))
            y = y + _dot_nt(cc, _b(hj)) * jnp.exp(acol)
            y_ref[:, sl] = y + dsk4[0:1, j:j + 1] * xh
            hs_ref[0, j] = hj
            st = _dot_tn(_b(xd * jnp.exp(alast - acol)), bb)
            h_ref[hg * g + j] = hj * jnp.exp(alast) + st

    small = pl.BlockSpec((1, LANES), lambda c, g: (0, 0))
    return pl.pallas_call(
        body, name="ssd_fwd", grid=(nch, SSM_GROUPS),
        in_specs=[pl.BlockSpec((ch, gw), lambda c, g: (c, g)),
                  pl.BlockSpec((ch, SSM_STATE), lambda c, g: (c, b_off + g)),
                  pl.BlockSpec((ch, SSM_STATE), lambda c, g: (c, c_off + g)),
                  pl.BlockSpec((ch, LANES), lambda c, g: (c, 0)),
                  small, small, small],
        out_specs=[pl.BlockSpec((ch, gw), lambda c, g: (c, g)),
                   pl.BlockSpec((1, hg, SSM_HEAD_DIM, SSM_STATE), lambda c, g: (c, g, 0, 0))],
        out_shape=[jax.ShapeDtypeStruct((S, SSM_INNER), F32),
                   jax.ShapeDtypeStruct((nch, SSM_HEADS, SSM_HEAD_DIM, SSM_STATE), F32)],
        scratch_shapes=[pltpu.VMEM((SSM_HEADS, SSM_HEAD_DIM, SSM_STATE), F32)],
        compiler_params=_params(("arbitrary", "arbitrary")),
    )(xact, xact, xact, dt_raw, dt_bias, a_neg, d_skip)


def _ssd_bwd(xact, dt_raw, dt_bias, a_neg, d_skip, hs, dy):
    S = xact.shape[0]
    ch = SSM_CHUNK
    nch = S // ch
    hg = HEADS_PER_GROUP
    gw = hg * SSM_HEAD_DIM
    b_off = SSM_INNER // SSM_STATE
    c_off = b_off + SSM_GROUPS

    def body(x_ref, b_ref, c_ref, dtr_ref, bias_ref, a_ref, dsk_ref, hs_ref, dy_ref,
             dx_ref, db_ref, dc_ref, ddt_ref, st_ref, dh_ref, ddt_acc):
        step = pl.program_id(0)
        g = pl.program_id(1)

        @pl.when(jnp.logical_and(step == 0, g == 0))
        def _():
            dh_ref[...] = jnp.zeros_like(dh_ref)
            st_ref[...] = jnp.zeros_like(st_ref)

        @pl.when(g == 0)
        def _():
            ddt_acc[...] = jnp.zeros_like(ddt_acc)

        xraw, sel, dt4, acs, acs_t, low = _ssd_common(dtr_ref, bias_ref, a_ref, g)
        a4 = _dot_hi(_row8(a_ref[...]), sel)[0:1, :]
        dsk4 = _dot_hi(_row8(dsk_ref[...]), sel)
        bf = b_ref[...]
        cf = c_ref[...]
        bb = _b(bf)
        cc = _b(cf)
        cb = _dot_nt(cc, bb)
        lane = lax.broadcasted_iota(jnp.int32, (ch, LANES), 1)
        rowi = lax.broadcasted_iota(jnp.int32, (ch, 1), 0)
        ones = jnp.ones((ch, LANES), F32)
        dcb = jnp.zeros((ch, ch), F32)
        dc_acc = jnp.zeros((ch, SSM_STATE), F32)
        db_acc = jnp.zeros((ch, SSM_STATE), F32)
        dacs4 = jnp.zeros((ch, LANES), F32)
        ddt4 = jnp.zeros((ch, LANES), F32)
        dd4 = jnp.zeros((1, LANES), F32)
        lane1 = lax.broadcasted_iota(jnp.int32, (1, LANES), 1)
        for j in range(hg):
            sl = slice(j * SSM_HEAD_DIM, (j + 1) * SSM_HEAD_DIM)
            acol = acs[:, j:j + 1]
            arow = acs_t[j:j + 1, :]
            alast = acs[ch - 1:ch, j:j + 1]
            decay = jnp.exp(jnp.where(low, acol - arow, -jnp.inf))
            ea = jnp.exp(acol)
            dsd = jnp.exp(alast - acol)
            cd = jnp.exp(alast)
            dtc = dt4[:, j:j + 1]
            xh = x_ref[:, sl]
            xd = xh * dtc
            xdb = _b(xd)
            hj = hs_ref[0, j]
            hjb = _b(hj)
            dhn = dh_ref[hg * g + j]
            dyj = dy_ref[:, sl]
            dyb = _b(dyj)
            lm = cb * decay
            dxh = dsk4[0:1, j:j + 1] * dyj
            dd4 = jnp.where(lane1 == j, jnp.sum(jnp.sum(dyj * xh, axis=1, keepdims=True), axis=0,
                                                keepdims=True), dd4)
            dlm = _dot_nt(dyb, xdb)
            dxd = _dot_tn(_b(lm), dyb)
            gm = dlm * lm
            dcb = dcb + dlm * decay
            dac = jnp.sum(gm, axis=1, keepdims=True) - _dot_tn_hi(gm, ones)[:, 0:1]
            zz = _dot_nt(cc, hjb)
            dzb = _b(dyj * ea)
            dac = dac + jnp.sum(dyj * zz, axis=1, keepdims=True) * ea
            dc_acc = dc_acc + _dot(dzb, hjb)
            dh_in = _dot_tn(dzb, cc)
            dsb = _b(dhn)
            ww = _dot_nt(bb, dsb)
            dxd = dxd + ww * dsd
            dds = jnp.sum(ww * xd, axis=1, keepdims=True) * dsd
            db_acc = db_acc + _dot(_b(xd * dsd), dsb)
            dac = dac - dds
            dal = (jnp.sum(dds, axis=0, keepdims=True)
                   + jnp.sum(jnp.sum(dhn * hj, axis=1, keepdims=True), axis=0, keepdims=True) * cd)
            dh_ref[hg * g + j] = dh_in + dhn * cd
            dac = dac + jnp.where(rowi == ch - 1, dal, 0.0)
            dacs4 = jnp.where(lane == j, dac, dacs4)
            dx_ref[:, sl] = dxh + dxd * dtc
            ddt4 = jnp.where(lane == j, jnp.sum(dxd * xh, axis=1, keepdims=True), ddt4)
        dcbb = _b(dcb)
        dc_ref[...] = dc_acc + _dot(dcbb, bb)
        db_ref[...] = db_acc + _dot_tn(dcbb, cc)
        ii = lax.broadcasted_iota(jnp.int32, (ch, ch), 0)
        jj = lax.broadcasted_iota(jnp.int32, (ch, ch), 1)
        triu = jnp.where(ii <= jj, 1.0, 0.0)
        dla4 = _dot_hi(triu, dacs4)
        ddt4 = ddt4 + dla4 * a4
        da4 = jnp.sum(dla4 * dt4, axis=0, keepdims=True) * a4
        sel_t = sel.T
        ddt_raw = _dot_hi(ddt4, sel_t) * _sigmoid(xraw)
        ddt_acc[...] += ddt_raw
        st_ref[0:1, :] += _dot_hi(_row8(da4), sel_t)[0:1, :]
        st_ref[1:2, :] += _dot_hi(_row8(dd4), sel_t)[0:1, :]
        st_ref[2:3, :] += jnp.sum(ddt_raw, axis=0, keepdims=True)

        @pl.when(g == SSM_GROUPS - 1)
        def _():
            ddt_ref[...] = _b(ddt_acc[...])

    small = pl.BlockSpec((1, LANES), lambda s, g: (0, 0))
    rc = lambda s: nch - 1 - s
    return pl.pallas_call(
        body, name="ssd_bwd", grid=(nch, SSM_GROUPS),
        in_specs=[pl.BlockSpec((ch, gw), lambda s, g: (rc(s), g)),
                  pl.BlockSpec((ch, SSM_STATE), lambda s, g: (rc(s), b_off + g)),
                  pl.BlockSpec((ch, SSM_STATE), lambda s, g: (rc(s), c_off + g)),
                  pl.BlockSpec((ch, LANES), lambda s, g: (rc(s), 0)),
                  small, small, small,
                  pl.BlockSpec((1, hg, SSM_HEAD_DIM, SSM_STATE), lambda s, g: (rc(s), g, 0, 0)),
                  pl.BlockSpec((ch, gw), lambda s, g: (rc(s), g))],
        out_specs=[pl.BlockSpec((ch, gw), lambda s, g: (rc(s), g)),
                   pl.BlockSpec((ch, SSM_STATE), lambda s, g: (rc(s), g)),
                   pl.BlockSpec((ch, SSM_STATE), lambda s, g: (rc(s), g)),
                   pl.BlockSpec((ch, LANES), lambda s, g: (rc(s), 0)),
                   pl.BlockSpec((8, LANES), lambda s, g: (0, 0))],
        out_shape=[jax.ShapeDtypeStruct((S, SSM_INNER), F32),
                   jax.ShapeDtypeStruct((S, SSM_GROUPS * SSM_STATE), F32),
                   jax.ShapeDtypeStruct((S, SSM_GROUPS * SSM_STATE), F32),
                   jax.ShapeDtypeStruct((S, LANES), BF16),
                   jax.ShapeDtypeStruct((8, LANES), F32)],
        scratch_shapes=[pltpu.VMEM((SSM_HEADS, SSM_HEAD_DIM, SSM_STATE), F32),
                        pltpu.VMEM((ch, LANES), F32)],
        compiler_params=_params(("arbitrary", "arbitrary")),
    )(xact, xact, xact, dt_raw, dt_bias, a_neg, d_skip, hs, dy)


GROUP_W = HEADS_PER_GROUP * SSM_HEAD_DIM
B_COL0 = SSM_INNER
C_COL0 = SSM_INNER + SSM_GROUPS * SSM_STATE


def _ssd_prep(dt_raw, dt_bias, a_neg):
    S = dt_raw.shape[0]
    ch = SSM_CHUNK
    nch = S // ch

    def body(dtr_ref, bias_ref, a_ref, dt_ref, acs_ref, acst_ref, sig_ref):
        x = dtr_ref[...] + bias_ref[...]
        lane = lax.broadcasted_iota(jnp.int32, (ch, LANES), 1)
        dt = jnp.where(lane < SSM_HEADS, _softplus(x), 0.0)
        ii = lax.broadcasted_iota(jnp.int32, (ch, ch), 0)
        jj = lax.broadcasted_iota(jnp.int32, (ch, ch), 1)
        acs = _dot_hi(jnp.where(ii >= jj, 1.0, 0.0), dt * a_ref[...])
        dt_ref[...] = dt
        acs_ref[...] = acs
        acst_ref[0] = acs.T[0:SSM_HEADS, :]
        sig_ref[...] = _sigmoid(x)

    blk = pl.BlockSpec((ch, LANES), lambda c: (c, 0))
    small = pl.BlockSpec((1, LANES), lambda c: (0, 0))
    shp = jax.ShapeDtypeStruct((S, LANES), F32)
    return pl.pallas_call(
        body, name="ssd_prep", grid=(nch,),
        in_specs=[blk, small, small],
        out_specs=[blk, blk, pl.BlockSpec((1, SSM_HEADS, ch), lambda c: (c, 0, 0)), blk],
        out_shape=[shp, shp, jax.ShapeDtypeStruct((nch, SSM_HEADS, ch), F32), shp],
        compiler_params=_params(("parallel",)),
    )(dt_raw, dt_bias, a_neg)


def _expand_heads(arr, g, rows):
    lane = lax.broadcasted_iota(jnp.int32, (rows, GROUP_W), 1) // SSM_HEAD_DIM
    h0 = HEADS_PER_GROUP * g
    out = jnp.broadcast_to(arr[:, h0:h0 + 1], (rows, GROUP_W))
    for j in range(1, HEADS_PER_GROUP):
        out = jnp.where(lane == j, arr[:, h0 + j:h0 + j + 1], out)
    return out


def _seg_matrix(k, lanes_per_head, h0):
    r = lax.broadcasted_iota(jnp.int32, (k, LANES), 0)
    c = lax.broadcasted_iota(jnp.int32, (k, LANES), 1)
    return jnp.where(c == h0 + r // lanes_per_head, 1.0, 0.0).astype(BF16)


def _seg_dot(t, e):
    hi = _b(t)
    lo = _b(t - hi.astype(F32))
    return _dot(hi, e) + _dot(lo, e)


def _head_sums(t, e, rows):
    if rows >= 8:
        return _seg_dot(t, e)
    return _seg_dot(jnp.broadcast_to(t, (8, t.shape[1])), e)[0:rows]


def _pair_masks(x):
    lane = lax.broadcasted_iota(jnp.int32, x.shape, 1)
    zero = jnp.zeros_like(x)
    return jnp.where(lane < SSM_HEAD_DIM, x, zero), jnp.where(lane >= SSM_HEAD_DIM, x, zero)


def _ssd_fwd2(xact, dt, acs, acst, dsk_e):
    S = xact.shape[0]
    ch = SSM_CHUNK
    nch = S // ch

    def body(x_ref, dt_ref, acs_ref, acst_ref, dsk_ref, y_ref, hs_ref, h_ref):
        c = pl.program_id(0)

        @pl.when(c == 0)
        def _():
            h_ref[...] = jnp.zeros_like(h_ref)

        dt_all = dt_ref[...]
        acs_all = acs_ref[...]
        acst_all = acst_ref[0]
        alast = acs_all[ch - 1:ch, :]
        eacs = jnp.exp(acs_all)
        dsd_all = jnp.exp(alast - acs_all)
        cd_all = jnp.exp(alast)
        ii = lax.broadcasted_iota(jnp.int32, (ch, ch), 0)
        jj = lax.broadcasted_iota(jnp.int32, (ch, ch), 1)
        low = ii >= jj
        for g in range(SSM_GROUPS):
            xs = x_ref[:, g * GROUP_W:(g + 1) * GROUP_W]
            bb = _b(x_ref[:, B_COL0 + g * SSM_STATE:B_COL0 + (g + 1) * SSM_STATE])
            cc = _b(x_ref[:, C_COL0 + g * SSM_STATE:C_COL0 + (g + 1) * SSM_STATE])
            cb = _dot_nt(cc, bb)
            xd = xs * _expand_heads(dt_all, g, ch)
            xdb = _b(xd)
            ht = h_ref[g]
            rest = (_dot(cc, _b(ht)) * _expand_heads(eacs, g, ch)
                    + dsk_ref[:, g * GROUP_W:(g + 1) * GROUP_W] * xs)
            for p in range(HEADS_PER_GROUP // 2):
                lms = []
                for h in (HEADS_PER_GROUP * g + 2 * p, HEADS_PER_GROUP * g + 2 * p + 1):
                    diff = acs_all[:, h:h + 1] - acst_all[h:h + 1, :]
                    lms.append(_b(cb * jnp.exp(jnp.where(low, diff, -jnp.inf))))
                xa, xb = _pair_masks(xdb[:, p * LANES:(p + 1) * LANES])
                yp = _dot(jnp.concatenate(lms, axis=1), jnp.concatenate([xa, xb], axis=0))
                y_ref[:, g * GROUP_W + p * LANES:g * GROUP_W + (p + 1) * LANES] = (
                    yp + rest[:, p * LANES:(p + 1) * LANES])
            hs_ref[0, g] = ht
            st = _dot_tn(bb, _b(xd * _expand_heads(dsd_all, g, ch)))
            h_ref[g] = ht * _expand_heads(cd_all, g, 1) + st

    blk = pl.BlockSpec((ch, LANES), lambda c: (c, 0))
    return pl.pallas_call(
        body, name="ssd_fwd", grid=(nch,),
        in_specs=[pl.BlockSpec((ch, CONV_DIM), lambda c: (c, 0)), blk, blk,
                  pl.BlockSpec((1, SSM_HEADS, ch), lambda c: (c, 0, 0)),
                  pl.BlockSpec((1, SSM_INNER), lambda c: (0, 0))],
        out_specs=[pl.BlockSpec((ch, SSM_INNER), lambda c: (c, 0)),
                   pl.BlockSpec((1, SSM_GROUPS, SSM_STATE, GROUP_W), lambda c: (c, 0, 0, 0))],
        out_shape=[jax.ShapeDtypeStruct((S, SSM_INNER), F32),
                   jax.ShapeDtypeStruct((nch, SSM_GROUPS, SSM_STATE, GROUP_W), F32)],
        scratch_shapes=[pltpu.VMEM((SSM_GROUPS, SSM_STATE, GROUP_W), F32)],
        compiler_params=_params(("arbitrary",)),
    )(xact, dt, acs, acst, dsk_e)


def _ssd_bwd2(xact, dt, acs, acst, sig, a_neg, dsk_e, hs, dy):
    S = xact.shape[0]
    ch = SSM_CHUNK
    nch = S // ch

    def body(x_ref, dt_ref, acs_ref, acst_ref, sig_ref, a_ref, dsk_ref, hs_ref, dy_ref,
             dx_ref, ddt_ref, st_ref, dh_ref, rows_ref):
        step = pl.program_id(0)

        @pl.when(step == 0)
        def _():
            dh_ref[...] = jnp.zeros_like(dh_ref)
            st_ref[...] = jnp.zeros_like(st_ref)
            rows_ref[...] = jnp.zeros_like(rows_ref)

        dt_all = dt_ref[...]
        acs_all = acs_ref[...]
        acst_all = acst_ref[0]
        alast = acs_all[ch - 1:ch, :]
        eacs = jnp.exp(acs_all)
        dsd_all = jnp.exp(alast - acs_all)
        cd_all = jnp.exp(alast)
        ii = lax.broadcasted_iota(jnp.int32, (ch, ch), 0)
        jj = lax.broadcasted_iota(jnp.int32, (ch, ch), 1)
        low = ii >= jj
        lane = lax.broadcasted_iota(jnp.int32, (ch, LANES), 1)
        cols = jnp.zeros((ch, LANES), F32)
        ddt = jnp.zeros((ch, LANES), F32)
        dal = jnp.zeros((1, LANES), F32)
        ddsk = jnp.zeros((1, LANES), F32)
        for g in range(SSM_GROUPS):
            xs = x_ref[:, g * GROUP_W:(g + 1) * GROUP_W]
            bb = _b(x_ref[:, B_COL0 + g * SSM_STATE:B_COL0 + (g + 1) * SSM_STATE])
            cc = _b(x_ref[:, C_COL0 + g * SSM_STATE:C_COL0 + (g + 1) * SSM_STATE])
            cb = _dot_nt(cc, bb)
            dt_e = _expand_heads(dt_all, g, ch)
            ea_e = _expand_heads(eacs, g, ch)
            dsd_e = _expand_heads(dsd_all, g, ch)
            cd_e = _expand_heads(cd_all, g, 1)
            xd = xs * dt_e
            xdb = _b(xd)
            dyg = dy_ref[:, g * GROUP_W:(g + 1) * GROUP_W]
            dyb = _b(dyg)
            ht = hs_ref[0, g]
            htb = _b(ht)
            dhn = dh_ref[g]
            dhnb = _b(dhn)
            zz = _dot(cc, htb)
            dzb = _b(dyg * ea_e)
            d_c = _dot_nt(dzb, htb)
            dh_in = _dot_tn(cc, dzb)
            ww = _dot(bb, dhnb)
            xdd = xd * dsd_e
            d_b = _dot_nt(_b(xdd), dhnb)
            t2 = ww * xdd
            e_g = _seg_matrix(GROUP_W, SSM_HEAD_DIM, HEADS_PER_GROUP * g)
            cols = cols + _head_sums(dyg * zz * ea_e - t2, e_g, ch)
            dal = dal + _head_sums(jnp.sum(t2, axis=0, keepdims=True), e_g, 1) + cd_all * _head_sums(
                jnp.sum(dhn * ht, axis=0, keepdims=True), e_g, 1)
            dh_ref[g] = dh_in + dhn * cd_e
            ddsk = ddsk + _head_sums(jnp.sum(dyg * xs, axis=0, keepdims=True), e_g, 1)
            dxd_rest = ww * dsd_e
            dcb = jnp.zeros((ch, ch), F32)
            for p in range(HEADS_PER_GROUP // 2):
                dya, dyb2 = _pair_masks(dyb[:, p * LANES:(p + 1) * LANES])
                xp = xdb[:, p * LANES:(p + 1) * LANES]
                lms, gms = [], []
                for h, dyh in ((HEADS_PER_GROUP * g + 2 * p, dya), (HEADS_PER_GROUP * g + 2 * p + 1, dyb2)):
                    diff = acs_all[:, h:h + 1] - acst_all[h:h + 1, :]
                    decay = jnp.exp(jnp.where(low, diff, -jnp.inf))
                    lm = cb * decay
                    dlm = _dot_nt(dyh, xp)
                    gm = dlm * lm
                    dcb = dcb + dlm * decay
                    rows_ref[h:h + 1, :] = jnp.sum(gm, axis=0, keepdims=True)
                    lms.append(_b(lm))
                    gms.append(gm)
                h0 = HEADS_PER_GROUP * g + 2 * p
                cols = cols + _head_sums(jnp.concatenate(gms, axis=1), _seg_matrix(2 * ch, ch, h0), ch)
                dxd = _dot_tn(jnp.concatenate(lms, axis=0), jnp.concatenate([dya, dyb2], axis=0))
                dxd = dxd + dxd_rest[:, p * LANES:(p + 1) * LANES]
                sl = slice(g * GROUP_W + p * LANES, g * GROUP_W + (p + 1) * LANES)
                dx_ref[:, sl] = (dsk_ref[:, sl] * dyg[:, p * LANES:(p + 1) * LANES]
                                 + dxd * dt_e[:, p * LANES:(p + 1) * LANES])
                ddt = ddt + _head_sums(dxd * xs[:, p * LANES:(p + 1) * LANES],
                                       _seg_matrix(LANES, SSM_HEAD_DIM, h0), ch)
            dcbb = _b(dcb)
            dx_ref[:, C_COL0 + g * SSM_STATE:C_COL0 + (g + 1) * SSM_STATE] = d_c + _dot(dcbb, bb)
            dx_ref[:, B_COL0 + g * SSM_STATE:B_COL0 + (g + 1) * SSM_STATE] = d_b + _dot_tn(dcbb, cc)
        rowi = lax.broadcasted_iota(jnp.int32, (ch, 1), 0)
        dacs = cols - rows_ref[...].T + jnp.where(rowi == ch - 1, dal, 0.0)
        dla = _dot_hi(jnp.where(ii <= jj, 1.0, 0.0), dacs)
        a_row = a_ref[...]
        ddt_raw = (ddt + dla * a_row) * sig_ref[...]
        ddt_ref[...] = _b(ddt_raw)
        st_ref[0:1, :] += jnp.sum(dla * dt_all, axis=0, keepdims=True) * a_row
        st_ref[1:2, :] += ddsk
        st_ref[2:3, :] += jnp.sum(ddt_raw, axis=0, keepdims=True)

    rc = lambda s: nch - 1 - s
    blk = pl.BlockSpec((ch, LANES), lambda s: (rc(s), 0))
    return pl.pallas_call(
        body, name="ssd_bwd", grid=(nch,),
        in_specs=[pl.BlockSpec((ch, CONV_DIM), lambda s: (rc(s), 0)), blk, blk,
                  pl.BlockSpec((1, SSM_HEADS, ch), lambda s: (rc(s), 0, 0)), blk,
                  pl.BlockSpec((1, LANES), lambda s: (0, 0)),
                  pl.BlockSpec((1, SSM_INNER), lambda s: (0, 0)),
                  pl.BlockSpec((1, SSM_GROUPS, SSM_STATE, GROUP_W), lambda s: (rc(s), 0, 0, 0)),
                  pl.BlockSpec((ch, SSM_INNER), lambda s: (rc(s), 0))],
        out_specs=[pl.BlockSpec((ch, CONV_DIM), lambda s: (rc(s), 0)), blk,
                   pl.BlockSpec((8, LANES), lambda s: (0, 0))],
        out_shape=[jax.ShapeDtypeStruct((S, CONV_DIM), F32), jax.ShapeDtypeStruct((S, LANES), BF16),
                   jax.ShapeDtypeStruct((8, LANES), F32)],
        scratch_shapes=[pltpu.VMEM((SSM_GROUPS, SSM_STATE, GROUP_W), F32), pltpu.VMEM((LANES, ch), F32)],
        compiler_params=_params(("arbitrary",)),
    )(xact, dt, acs, acst, sig, a_neg, dsk_e, hs, dy)


def _pad_lanes(v, n=LANES):
    return jnp.pad(v, ((0, 0), (0, n - v.shape[1])))


def _local_step(x, target, w, ex=None):
    offs = np.cumsum((0,) + IN_SPLITS)
    w_in = w["w_in"]
    w_qkv = w_in[:, offs[0]:offs[3]]
    w_z = w_in[:, offs[3]:offs[4]]
    w_xbc = w_in[:, offs[4]:offs[5]]
    w_dt = _pad_lanes(w_in[:, offs[5]:offs[6]])
    w_g = w_in[:, offs[6]:offs[7]]
    dt_bias = _pad_lanes(w["dt_bias"])
    a_neg = _pad_lanes(-jnp.exp(w["a_log"]))
    d_skip = _pad_lanes(w["d_skip"])

    u = _rms_fwd(x, w["norm_mix_pre_w"])
    if ex is None:
        xbc = _mm_nn(u, w_xbc, F32, "proj_xbc")
    else:
        xbc, got = _mm_nn(u, w_xbc, F32, "proj_xbc", comm=_gather_comm([ex.mine[REST_EARLY]]))
        w = {**w, **ex.rest_weights(got[0], REST_EARLY)}
    qkv = _mm_nn(u, w_qkv, F32, "proj_qkv")
    z = _mm_nn(u, w_z, F32, "proj_z")
    dt_raw = _mm_nn(u, w_dt, F32, "proj_dt")
    gl = _mm_nn(u, w_g, F32, "proj_gate")

    pats = _qkv_layouts(qkv)
    os_, ms_, ls_ = [], [], []
    for d, qkv_p in zip(DILATIONS, pats):
        if ex is not None and d == DILATIONS[0]:
            o, m, l, got = _attn_fwd2(qkv_p, d, comm=_gather_comm([ex.mine[REST_LATE]]))
            w = {**w, **ex.rest_weights(got[0], REST_LATE)}
        else:
            o, m, l = _attn_fwd2(qkv_p, d)
        os_.append(o)
        ms_.append(m)
        ls_.append(l)
    att, lse = _attn_combine2(os_, ms_, ls_)
    att_o = _mm_nn(att, w["w_att_proj"], F32, "att_proj")

    xact = _conv_fwd2(xbc, w["conv_w"], w["conv_b"])
    dsk_e = jnp.repeat(w["d_skip"], SSM_HEAD_DIM, axis=1)
    dt, acs, acst, sig = _ssd_prep(dt_raw, dt_bias, a_neg)
    y_ssd, hs = _ssd_fwd2(xact, dt, acs, acst, dsk_e)
    ssm_y = _gnorm_fwd(y_ssd, z, w["ssm_norm_w"])
    ssm_o = _mm_nn(ssm_y, w["w_ssm_proj"], F32, "ssm_proj")

    mi = _gate_fwd(att_o, ssm_o, gl, w["b_gate"])
    mixed = _mm_nn(mi, w["w_out"], F32, "out_proj")
    h1, f = _post_pre(x, mixed, w["norm_mix_post_w"], w["norm_ffn_pre_w"])
    r_up, act = _mm_nn(f, w["w_up"], BF16, "ffn_up", mode="relu2")
    down = _mm_nn(act, w["w_down"], F32, "ffn_down")
    dh2, d_down, loss, g_ffn_post = _final(h1, down, w["norm_ffn_post_w"], target)

    g = {"norm_ffn_post_w": g_ffn_post}
    g["w_down"] = _mm_tn(act, d_down, "dw_down")
    dup = _mm_nn(d_down, w["w_down"], BF16, "d_act", mode="mul2", extra=r_up, tb=True)
    g["w_up"] = _mm_tn(f, dup, "dw_up")
    df = _mm_nn(dup, w["w_up"], F32, "d_f", tb=True)
    dh1, d_mixed, g["norm_ffn_pre_w"], g["norm_mix_post_w"] = _mid_bwd(
        dh2, df, h1, mixed, w["norm_ffn_pre_w"], w["norm_mix_post_w"])
    g["w_out"] = _mm_tn(mi, d_mixed, "dw_out")
    dmi = _mm_nn(d_mixed, w["w_out"], F32, "d_mi", tb=True)
    d_att_o, d_ssm_o, dgl, g["b_gate"] = _gate_bwd(dmi, att_o, ssm_o, gl, w["b_gate"])

    g["w_att_proj"] = _mm_tn(att, d_att_o, "dw_att_proj")
    g["w_ssm_proj"] = _mm_tn(ssm_y, d_ssm_o, "dw_ssm_proj")
    if ex is None:
        d_ssm_y = _mm_nn(d_ssm_o, w["w_ssm_proj"], F32, "d_ssm_y", tb=True)
    else:
        gs_rest = jnp.concatenate(
            [_shards_from_full(n, g[n]).reshape(N_CHIPS, -1, PACK_COLS) for n in REST], axis=1)
        d_ssm_y, recv = _mm_nn(d_ssm_o, w["w_ssm_proj"], F32, "d_ssm_y", comm=_pair_comm([gs_rest]), tb=True)
        p_rest = _pair_add2(gs_rest, recv[0], ex.c_arr, "rs_pair_add_rest")

    d_att = _mm_nn(d_att_o, w["w_att_proj"], F32, "d_att", tb=True)
    dqs, dks, dvs = [], [], []
    for d, qkv_p, (do_p, lse_p, delta_p) in zip(DILATIONS, pats, _attn_delta2(d_att, att, lse)):
        if ex is not None and d == DILATIONS[0]:
            dq, dk, dv, recv3 = _attn_bwd2(qkv_p, do_p, lse_p, delta_p, d, comm=_chip_comm([p_rest]))
            q_rest = _chip_add2(p_rest, recv3[0], ex.chip_arr, "rs_chip_add_rest")
            ex.finish_reduce("rest", q_rest, _comm_call("rs_share_rest", _share_comm([q_rest]))[0])
        else:
            dq, dk, dv = _attn_bwd2(qkv_p, do_p, lse_p, delta_p, d)
        dqs.append(dq)
        dks.append(dk)
        dvs.append(dv)
    dqkv = _sum_qkv2(dqs, dks, dvs)

    dy_ssd, dz, g["ssm_norm_w"] = _gnorm_bwd(d_ssm_y, y_ssd, z, w["ssm_norm_w"])
    dxact, ddt_raw, stats = _ssd_bwd2(xact, dt, acs, acst, sig, a_neg, dsk_e, hs, dy_ssd)
    g["a_log"] = stats[0:1, :SSM_HEADS]
    g["d_skip"] = stats[1:2, :SSM_HEADS]
    g["dt_bias"] = stats[2:3, :SSM_HEADS]
    dxbc, g["conv_w"], g["conv_b"] = _conv_bwd2(xbc, dxact, w["conv_w"], w["conv_b"])

    pieces = [(dqkv, w_qkv), (dz, w_z), (dxbc, w_xbc), (ddt_raw, w_dt), (dgl, w_g)]
    gw = [_mm_tn(u, dp, f"dw_in_{i}") for i, (dp, _) in enumerate(pieces)]
    gw[3] = gw[3][:, :SSM_HEADS]
    g["w_in"] = jnp.concatenate(gw, axis=1)
    du = None
    for i, (dp, wp) in enumerate(pieces):
        if ex is not None and i == 0:
            gs_in = _shards_from_full("w_in", g["w_in"])
            du, recv = _mm_nn(dp, wp, F32, f"d_u_{i}", acc=du, comm=_pair_comm([gs_in]), tb=True)
            p_in = _pair_add2(gs_in, recv[0], ex.c_arr, "rs_pair_add_in")
            rows = p_in.shape[1] // 2
            p_parts = [p_in[:, :rows], p_in[:, rows:]]
            q_parts = []
        elif ex is not None and i in (1, 2):
            p_part = p_parts[i - 1]
            du, recv3 = _mm_nn(dp, wp, F32, f"d_u_{i}", acc=du, comm=_chip_comm([p_part]), tb=True)
            q_parts.append(_chip_add2(p_part, recv3[0], ex.chip_arr, f"rs_chip_add_in_{i}"))
            if i == 2:
                others = _comm_call("rs_share_in", _share_comm(q_parts))
                ex.finish_reduce("w_in", jnp.concatenate(q_parts, axis=0), jnp.concatenate(others, axis=0))
        else:
            du = _mm_nn(dp, wp, F32, f"d_u_{i}", acc=du, tb=True)
    grad_x, g["norm_mix_pre_w"] = _first_bwd(dh1, du, x, w["norm_mix_pre_w"])
    return loss, grad_x, g


BIG = ("w_in", "w_att_proj", "w_ssm_proj", "w_out", "w_up", "w_down")
BIG_FULL_SHAPES = {"w_in": (D_MODEL, IN_PROJ_WIDTH), "w_att_proj": (ATT_WIDTH, D_MODEL),
                   "w_ssm_proj": (SSM_INNER, D_MODEL), "w_out": (D_MODEL, D_MODEL),
                   "w_up": (D_MODEL, FFN_HIDDEN), "w_down": (FFN_HIDDEN, D_MODEL)}
BIG_COL_SHARDED = {"w_in": True, "w_att_proj": True, "w_ssm_proj": False, "w_out": False, "w_up": True,
                   "w_down": False}
PACK_COLS = 1024
PACK_ROWS = 5760
PACK_HALF = PACK_ROWS // 2
PACK_BLOCK = 576
SMALL = ("norm_mix_pre_w", "b_gate", "conv_b", "dt_bias", "a_log", "d_skip", "ssm_norm_w",
         "norm_mix_post_w", "norm_ffn_pre_w", "norm_ffn_post_w")
SMALL_ROWS = 232


def _shard_shape(name):
    r, c = BIG_FULL_SHAPES[name]
    return (r, c // N_CHIPS) if BIG_COL_SHARDED[name] else (r // N_CHIPS, c)


def _pack(shards, dtype):
    flat = [shards[n].astype(dtype).reshape(-1, PACK_COLS) for n in BIG]
    rows = sum(f.shape[0] for f in flat)
    flat.append(jnp.zeros((PACK_ROWS - rows, PACK_COLS), dtype))
    return jnp.concatenate(flat, axis=0)


def _unpack(packed):
    out, r0 = {}, 0
    for n in BIG:
        shp = _shard_shape(n)
        rows = shp[0] * shp[1] // PACK_COLS
        out[n] = packed[r0:r0 + rows].reshape(shp)
        r0 += rows
    return out


def _unpack_full(gathered):
    out, r0 = {}, 0
    for n in BIG:
        shp = _shard_shape(n)
        rows = shp[0] * shp[1] // PACK_COLS
        sh = gathered[:, r0:r0 + rows].reshape((N_CHIPS,) + shp)
        if BIG_COL_SHARDED[n]:
            out[n] = sh.transpose(1, 0, 2).reshape(BIG_FULL_SHAPES[n])
        else:
            out[n] = sh.reshape(BIG_FULL_SHAPES[n])
        r0 += rows
    return out


def _pack_full(grads):
    parts = []
    rows_total = 0
    for n in BIG:
        shp = _shard_shape(n)
        gfull = grads[n]
        if BIG_COL_SHARDED[n]:
            sh = gfull.reshape(shp[0], N_CHIPS, shp[1]).transpose(1, 0, 2)
        else:
            sh = gfull.reshape((N_CHIPS,) + shp)
        parts.append(sh.reshape(N_CHIPS, -1, PACK_COLS))
        rows_total += parts[-1].shape[1]
    parts.append(jnp.zeros((N_CHIPS, PACK_ROWS - rows_total, PACK_COLS), F32))
    return jnp.concatenate(parts, axis=1)


def _mesh_pos():
    return lax.axis_index("x"), lax.axis_index("y"), lax.axis_index("c")


def _other_chips(x, y):
    return [(1 - x, y), (x, 1 - y), (1 - x, 1 - y)]


ANY = pl.BlockSpec(memory_space=pl.ANY)


def _allgather_packed(wpack):
    half = PACK_HALF

    def body(w_ref, out_ref, send_sems, recv_sems):
        x, y, c = _mesh_pos()
        me = 2 * x + y
        sibling = (x, y, 1 - c)
        chips = _other_chips(x, y)

        def rows(chip, h):
            return out_ref.at[chip, pl.ds(h * half, half), :]

        def copy(k, chip, h, to, src=None):
            return pltpu.make_async_remote_copy(
                src_ref=rows(chip, h) if src is None else src, dst_ref=rows(chip, h),
                send_sem=send_sems.at[k], recv_sem=recv_sems.at[k], device_id=to, device_id_type=MESH)

        mine_half = w_ref.at[pl.ds(c * half, half), :]
        first = [copy(j, me, c, (*chip, c), src=mine_half) for j, chip in enumerate(chips)]
        for cp in first:
            cp.start()
        passed = [copy(3 + j, 2 * chip[0] + chip[1], c, sibling) for j, chip in enumerate(chips)]
        for j, chip in enumerate(chips):
            copy(j, 2 * chip[0] + chip[1], c, (x, y, c)).wait_recv()
            passed[j].start()
        for j, chip in enumerate(chips):
            copy(3 + j, 2 * chip[0] + chip[1], 1 - c, (x, y, c)).wait_recv()
        for cp in first + passed:
            cp.wait_send()

    return pl.pallas_call(
        body, name="allgather_weights",
        out_shape=jax.ShapeDtypeStruct((N_CHIPS,) + wpack.shape, wpack.dtype),
        in_specs=[ANY], out_specs=ANY,
        scratch_shapes=[pltpu.SemaphoreType.DMA((6,)), pltpu.SemaphoreType.DMA((6,))],
        compiler_params=pltpu.CompilerParams(has_side_effects=True),
    )(wpack)


def _exchange_halves(gpack):
    half = PACK_HALF

    def body(g_ref, out_ref, send_sem, recv_sem):
        x, y, c = _mesh_pos()
        cp = pltpu.make_async_remote_copy(
            src_ref=g_ref.at[:, pl.ds((1 - c) * half, half), :], dst_ref=out_ref,
            send_sem=send_sem, recv_sem=recv_sem, device_id=(x, y, 1 - c), device_id_type=MESH)
        cp.start()
        cp.wait()

    return pl.pallas_call(
        body, name="rs_pair_exchange",
        out_shape=jax.ShapeDtypeStruct((N_CHIPS, half, PACK_COLS), F32),
        in_specs=[ANY], out_specs=ANY,
        scratch_shapes=[pltpu.SemaphoreType.DMA, pltpu.SemaphoreType.DMA],
        compiler_params=pltpu.CompilerParams(has_side_effects=True),
    )(gpack)


def _pair_add(gpack, recv, c_idx):
    nb = PACK_HALF // PACK_BLOCK

    def body(c_ref, g_ref, r_ref, o_ref):
        o_ref[...] = _b(g_ref[...] + r_ref[...])

    blk = (1, PACK_BLOCK, PACK_COLS)
    return pl.pallas_call(
        body, name="rs_pair_add",
        grid_spec=pltpu.PrefetchScalarGridSpec(
            num_scalar_prefetch=1, grid=(N_CHIPS, nb),
            in_specs=[pl.BlockSpec(blk, lambda s, i, c: (s, c[0] * nb + i, 0)),
                      pl.BlockSpec(blk, lambda s, i, c: (s, i, 0))],
            out_specs=pl.BlockSpec(blk, lambda s, i, c: (s, i, 0))),
        out_shape=jax.ShapeDtypeStruct((N_CHIPS, PACK_HALF, PACK_COLS), BF16),
        compiler_params=_params(("arbitrary", "arbitrary")),
    )(c_idx, gpack, recv)


def _exchange_chips(ppack):
    def body(p_ref, out_ref, send_sems, recv_sems):
        x, y, c = _mesh_pos()
        chips = _other_chips(x, y)
        cps = [pltpu.make_async_remote_copy(
            src_ref=p_ref.at[2 * chip[0] + chip[1]], dst_ref=out_ref.at[j],
            send_sem=send_sems.at[j], recv_sem=recv_sems.at[j], device_id=(*chip, c), device_id_type=MESH)
            for j, chip in enumerate(chips)]
        for cp in cps:
            cp.start()
        for cp in cps:
            cp.wait_recv()
        for cp in cps:
            cp.wait_send()

    return pl.pallas_call(
        body, name="rs_chip_exchange",
        out_shape=jax.ShapeDtypeStruct((N_CHIPS - 1, PACK_HALF, PACK_COLS), ppack.dtype),
        in_specs=[ANY], out_specs=ANY,
        scratch_shapes=[pltpu.SemaphoreType.DMA((3,)), pltpu.SemaphoreType.DMA((3,))],
        compiler_params=pltpu.CompilerParams(has_side_effects=True),
    )(ppack)


def _chip_add(ppack, recv, me_idx):
    nb = PACK_HALF // PACK_BLOCK

    def body(m_ref, p_ref, r0_ref, r1_ref, r2_ref, o_ref):
        o_ref[...] = ((p_ref[0].astype(F32) + r0_ref[0].astype(F32)) + r1_ref[0].astype(F32)) + r2_ref[0].astype(F32)

    blk = (1, PACK_BLOCK, PACK_COLS)
    return pl.pallas_call(
        body, name="rs_chip_add",
        grid_spec=pltpu.PrefetchScalarGridSpec(
            num_scalar_prefetch=1, grid=(nb,),
            in_specs=[pl.BlockSpec(blk, lambda i, m: (m[0], i, 0)),
                      pl.BlockSpec(blk, lambda i, m: (0, i, 0)),
                      pl.BlockSpec(blk, lambda i, m: (1, i, 0)),
                      pl.BlockSpec(blk, lambda i, m: (2, i, 0))],
            out_specs=pl.BlockSpec((PACK_BLOCK, PACK_COLS), lambda i, m: (i, 0))),
        out_shape=jax.ShapeDtypeStruct((PACK_HALF, PACK_COLS), F32),
        compiler_params=_params(("arbitrary",)),
    )(me_idx, ppack, recv, recv, recv)


def _share_halves(qhalf):
    def body(q_ref, out_ref, send_sem, recv_sem):
        x, y, c = _mesh_pos()
        cp = pltpu.make_async_remote_copy(
            src_ref=q_ref, dst_ref=out_ref, send_sem=send_sem, recv_sem=recv_sem,
            device_id=(x, y, 1 - c), device_id_type=MESH)
        cp.start()
        cp.wait()

    return pl.pallas_call(
        body, name="rs_share_halves",
        out_shape=jax.ShapeDtypeStruct(qhalf.shape, F32),
        in_specs=[ANY], out_specs=ANY,
        scratch_shapes=[pltpu.SemaphoreType.DMA, pltpu.SemaphoreType.DMA],
        compiler_params=pltpu.CompilerParams(has_side_effects=True),
    )(qhalf)


REST_EARLY = ("w_att_proj", "w_ssm_proj", "w_out")
REST_LATE = ("w_up", "w_down")
REST = REST_EARLY + REST_LATE
ADD_ROWS = {IN_PROJ_WIDTH // N_CHIPS: 256, PACK_COLS: 752}


def _stack_rest(shards, dtype, names=REST):
    return jnp.concatenate([shards[n].astype(dtype).reshape(-1, PACK_COLS) for n in names], axis=0)


def _unstack_rest(stacked, lead=(), names=REST):
    out, r0 = {}, 0
    for n in names:
        shp = _shard_shape(n)
        rows = shp[0] * shp[1] // PACK_COLS
        out[n] = stacked[..., r0:r0 + rows, :].reshape(lead + shp)
        r0 += rows
    return out


def _full_from_shards(name, sh):
    if BIG_COL_SHARDED[name]:
        return sh.transpose(1, 0, 2).reshape(BIG_FULL_SHAPES[name])
    return sh.reshape(BIG_FULL_SHAPES[name])


def _shards_from_full(name, full):
    shp = _shard_shape(name)
    if BIG_COL_SHARDED[name]:
        return full.reshape(shp[0], N_CHIPS, shp[1]).transpose(1, 0, 2)
    return full.reshape((N_CHIPS,) + shp)


def _allgather2(shards):
    n = len(shards)

    def body(*refs):
        w_refs, out_refs, send_sems, recv_sems = refs[:n], refs[n:2 * n], refs[2 * n], refs[2 * n + 1]
        x, y, c = _mesh_pos()
        me = 2 * x + y
        sibling = (x, y, 1 - c)
        chips = _other_chips(x, y)
        plans = []
        for a, (w_ref, out_ref) in enumerate(zip(w_refs, out_refs)):
            half = w_ref.shape[0] // 2

            def copy(k, chip, h, to, src=None, out_ref=out_ref, half=half, a=a):
                rows = out_ref.at[chip, pl.ds(h * half, half), :]
                return pltpu.make_async_remote_copy(
                    src_ref=rows if src is None else src, dst_ref=rows,
                    send_sem=send_sems.at[6 * a + k], recv_sem=recv_sems.at[6 * a + k],
                    device_id=to, device_id_type=MESH)

            mine_half = w_ref.at[pl.ds(c * half, half), :]
            idx = [2 * chip[0] + chip[1] for chip in chips]
            send = [copy(j, me, c, (*chip, c), src=mine_half) for j, chip in enumerate(chips)]
            land = [copy(j, idx[j], c, (x, y, c)) for j in range(N_CHIPS - 1)]
            forward = [copy(3 + j, idx[j], c, sibling) for j in range(N_CHIPS - 1)]
            land_fw = [copy(3 + j, idx[j], 1 - c, (x, y, c)) for j in range(N_CHIPS - 1)]
            plans.append((send, land, forward, land_fw))
        for send, _, _, _ in plans:
            for cp in send:
                cp.start()
        for _, land, forward, _ in plans:
            for j in range(N_CHIPS - 1):
                land[j].wait_recv()
                forward[j].start()
        for _, _, _, land_fw in plans:
            for cp in land_fw:
                cp.wait_recv()
        for send, _, forward, _ in plans:
            for cp in send + forward:
                cp.wait_send()

    return pl.pallas_call(
        body, name="allgather_weights",
        out_shape=[jax.ShapeDtypeStruct((N_CHIPS,) + s.shape, s.dtype) for s in shards],
        in_specs=[ANY] * n, out_specs=[ANY] * n,
        scratch_shapes=[pltpu.SemaphoreType.DMA((6 * n,)), pltpu.SemaphoreType.DMA((6 * n,))],
        compiler_params=pltpu.CompilerParams(has_side_effects=True),
    )(*shards)


def _exchange_halves2(gs):
    n = len(gs)

    def body(*refs):
        g_refs, out_refs, send_sems, recv_sems = refs[:n], refs[n:2 * n], refs[2 * n], refs[2 * n + 1]
        x, y, c = _mesh_pos()
        cps = []
        for a, (g_ref, out_ref) in enumerate(zip(g_refs, out_refs)):
            half = g_ref.shape[1] // 2
            cps.append(pltpu.make_async_remote_copy(
                src_ref=g_ref.at[:, pl.ds((1 - c) * half, half), :], dst_ref=out_ref,
                send_sem=send_sems.at[a], recv_sem=recv_sems.at[a], device_id=(x, y, 1 - c),
                device_id_type=MESH))
        for cp in cps:
            cp.start()
        for cp in cps:
            cp.wait()

    return pl.pallas_call(
        body, name="rs_pair_exchange",
        out_shape=[jax.ShapeDtypeStruct((N_CHIPS, g.shape[1] // 2, g.shape[2]), F32) for g in gs],
        in_specs=[ANY] * n, out_specs=[ANY] * n,
        scratch_shapes=[pltpu.SemaphoreType.DMA((n,)), pltpu.SemaphoreType.DMA((n,))],
        compiler_params=pltpu.CompilerParams(has_side_effects=True),
    )(*gs)


def _pair_add2(g, recv, c_idx, name):
    _, half, cols = recv.shape
    rb = ADD_ROWS[cols]
    nb = half // rb

    def body(c_ref, g_ref, r_ref, o_ref):
        o_ref[...] = _b(g_ref[...] + r_ref[...])

    blk = (1, rb, cols)
    return pl.pallas_call(
        body, name=name,
        grid_spec=pltpu.PrefetchScalarGridSpec(
            num_scalar_prefetch=1, grid=(N_CHIPS, nb),
            in_specs=[pl.BlockSpec(blk, lambda s, i, c: (s, c[0] * nb + i, 0)),
                      pl.BlockSpec(blk, lambda s, i, c: (s, i, 0))],
            out_specs=pl.BlockSpec(blk, lambda s, i, c: (s, i, 0))),
        out_shape=jax.ShapeDtypeStruct(recv.shape, BF16),
        compiler_params=_params(("arbitrary", "arbitrary")),
    )(c_idx, g, recv)


def _exchange_chips2(ps):
    n = len(ps)

    def body(*refs):
        p_refs, out_refs, send_sems, recv_sems = refs[:n], refs[n:2 * n], refs[2 * n], refs[2 * n + 1]
        x, y, c = _mesh_pos()
        chips = _other_chips(x, y)
        cps = [pltpu.make_async_remote_copy(
            src_ref=p_ref.at[2 * chip[0] + chip[1]], dst_ref=out_ref.at[j],
            send_sem=send_sems.at[3 * a + j], recv_sem=recv_sems.at[3 * a + j], device_id=(*chip, c),
            device_id_type=MESH)
            for a, (p_ref, out_ref) in enumerate(zip(p_refs, out_refs)) for j, chip in enumerate(chips)]
        for cp in cps:
            cp.start()
        for cp in cps:
            cp.wait_recv()
        for cp in cps:
            cp.wait_send()

    return pl.pallas_call(
        body, name="rs_chip_exchange",
        out_shape=[jax.ShapeDtypeStruct((N_CHIPS - 1,) + p.shape[1:], p.dtype) for p in ps],
        in_specs=[ANY] * n, out_specs=[ANY] * n,
        scratch_shapes=[pltpu.SemaphoreType.DMA((3 * n,)), pltpu.SemaphoreType.DMA((3 * n,))],
        compiler_params=pltpu.CompilerParams(has_side_effects=True),
    )(*ps)


def _chip_add2(p, recv, me_idx, name):
    _, half, cols = recv.shape
    rb = ADD_ROWS[cols]

    def body(m_ref, p_ref, r0_ref, r1_ref, r2_ref, o_ref):
        o_ref[...] = ((p_ref[0].astype(F32) + r0_ref[0].astype(F32)) + r1_ref[0].astype(F32)) + r2_ref[0].astype(F32)

    blk = (1, rb, cols)
    return pl.pallas_call(
        body, name=name,
        grid_spec=pltpu.PrefetchScalarGridSpec(
            num_scalar_prefetch=1, grid=(half // rb,),
            in_specs=[pl.BlockSpec(blk, lambda i, m: (m[0], i, 0)),
                      pl.BlockSpec(blk, lambda i, m: (0, i, 0)),
                      pl.BlockSpec(blk, lambda i, m: (1, i, 0)),
                      pl.BlockSpec(blk, lambda i, m: (2, i, 0))],
            out_specs=pl.BlockSpec((rb, cols), lambda i, m: (i, 0))),
        out_shape=jax.ShapeDtypeStruct((half, cols), F32),
        compiler_params=_params(("arbitrary",)),
    )(me_idx, p, recv, recv, recv)


def _share_halves2(qs):
    n = len(qs)

    def body(*refs):
        q_refs, out_refs, send_sems, recv_sems = refs[:n], refs[n:2 * n], refs[2 * n], refs[2 * n + 1]
        x, y, c = _mesh_pos()
        cps = [pltpu.make_async_remote_copy(
            src_ref=q_ref, dst_ref=out_ref, send_sem=send_sems.at[a], recv_sem=recv_sems.at[a],
            device_id=(x, y, 1 - c), device_id_type=MESH)
            for a, (q_ref, out_ref) in enumerate(zip(q_refs, out_refs))]
        for cp in cps:
            cp.start()
        for cp in cps:
            cp.wait()

    return pl.pallas_call(
        body, name="rs_share_halves",
        out_shape=[jax.ShapeDtypeStruct(q.shape, F32) for q in qs],
        in_specs=[ANY] * n, out_specs=[ANY] * n,
        scratch_shapes=[pltpu.SemaphoreType.DMA((n,)), pltpu.SemaphoreType.DMA((n,))],
        compiler_params=pltpu.CompilerParams(has_side_effects=True),
    )(*qs)


def _gather_plan():
    def copies(w_refs, out_refs, send_sems, recv_sems):
        x, y, c = _mesh_pos()
        me = 2 * x + y
        sibling = (x, y, 1 - c)
        chips = _other_chips(x, y)
        idx = [2 * chip[0] + chip[1] for chip in chips]
        plans = []
        for a, (w_ref, out_ref) in enumerate(zip(w_refs, out_refs)):
            half = w_ref.shape[0] // 2

            def copy(k, chip, h, to, src=None, out_ref=out_ref, half=half, a=a):
                rows = out_ref.at[chip, pl.ds(h * half, half), :]
                return pltpu.make_async_remote_copy(
                    src_ref=rows if src is None else src, dst_ref=rows,
                    send_sem=send_sems.at[6 * a + k], recv_sem=recv_sems.at[6 * a + k],
                    device_id=to, device_id_type=MESH)

            mine_half = w_ref.at[pl.ds(c * half, half), :]
            send = [copy(j, me, c, (*chip, c), src=mine_half) for j, chip in enumerate(chips)]
            land = [copy(j, idx[j], c, (x, y, c)) for j in range(N_CHIPS - 1)]
            forward = [copy(3 + j, idx[j], c, sibling) for j in range(N_CHIPS - 1)]
            land_fw = [copy(3 + j, idx[j], 1 - c, (x, y, c)) for j in range(N_CHIPS - 1)]
            plans.append((send, land, forward, land_fw))
        return plans

    def start(*refs):
        for send, _, _, _ in copies(*refs):
            for cp in send:
                cp.start()

    def finish(*refs):
        plans = copies(*refs)
        for _, land, forward, _ in plans:
            for j in range(N_CHIPS - 1):
                land[j].wait_recv()
                forward[j].start()
        for _, _, _, land_fw in plans:
            for cp in land_fw:
                cp.wait_recv()
        for send, _, forward, _ in plans:
            for cp in send + forward:
                cp.wait_send()

    return start, finish


def _pair_plan(halves):
    def copies(in_refs, out_refs, send_sems, recv_sems):
        x, y, c = _mesh_pos()
        cps = []
        for a, (g_ref, out_ref) in enumerate(zip(in_refs, out_refs)):
            if halves:
                half = g_ref.shape[1] // 2
                src = g_ref.at[:, pl.ds((1 - c) * half, half), :]
            else:
                src = g_ref
            cps.append(pltpu.make_async_remote_copy(
                src_ref=src, dst_ref=out_ref, send_sem=send_sems.at[a], recv_sem=recv_sems.at[a],
                device_id=(x, y, 1 - c), device_id_type=MESH))
        return cps

    def start(*refs):
        for cp in copies(*refs):
            cp.start()

    def finish(*refs):
        for cp in copies(*refs):
            cp.wait()

    return start, finish


def _chip_plan():
    def copies(in_refs, out_refs, send_sems, recv_sems):
        x, y, c = _mesh_pos()
        chips = _other_chips(x, y)
        return [pltpu.make_async_remote_copy(
            src_ref=p_ref.at[2 * chip[0] + chip[1]], dst_ref=out_ref.at[j],
            send_sem=send_sems.at[3 * a + j], recv_sem=recv_sems.at[3 * a + j], device_id=(*chip, c),
            device_id_type=MESH)
            for a, (p_ref, out_ref) in enumerate(zip(in_refs, out_refs)) for j, chip in enumerate(chips)]

    def start(*refs):
        for cp in copies(*refs):
            cp.start()

    def finish(*refs):
        cps = copies(*refs)
        for cp in cps:
            cp.wait_recv()
        for cp in cps:
            cp.wait_send()

    return start, finish


def _gather_comm(shards):
    return _Comm(_gather_plan(), shards, [jax.ShapeDtypeStruct((N_CHIPS,) + s.shape, s.dtype) for s in shards],
                 6 * len(shards))


def _pair_comm(gs):
    return _Comm(_pair_plan(True), gs,
                 [jax.ShapeDtypeStruct((N_CHIPS, g.shape[1] // 2, g.shape[2]), g.dtype) for g in gs], len(gs))


def _chip_comm(ps):
    return _Comm(_chip_plan(), ps, [jax.ShapeDtypeStruct((N_CHIPS - 1,) + p.shape[1:], p.dtype) for p in ps],
                 3 * len(ps))


def _share_comm(qs):
    return _Comm(_pair_plan(False), qs, [jax.ShapeDtypeStruct(q.shape, q.dtype) for q in qs], len(qs))


def _comm_call(name, comm):
    n, m = len(comm.ins), len(comm.outs)

    def body(*refs):
        args = (refs[:n], refs[n:n + m], refs[n + m], refs[n + m + 1])
        comm.start(*args)
        comm.finish(*args)

    return pl.pallas_call(
        body, name=name, out_shape=comm.outs, in_specs=[ANY] * n, out_specs=[ANY] * m,
        scratch_shapes=[pltpu.SemaphoreType.DMA((comm.n_sems,))] * 2,
        compiler_params=pltpu.CompilerParams(has_side_effects=True),
    )(*comm.ins)


class _Exchange:
    def __init__(self, chip, ci, early_mine, late_mine):
        self.chip, self.ci = chip, ci
        self.mine = {REST_EARLY: early_mine, REST_LATE: late_mine}
        self.c_arr = ci.reshape(1).astype(jnp.int32)
        self.chip_arr = chip.reshape(1).astype(jnp.int32)
        self.reduced = {}

    def rest_weights(self, got, names):
        stacks = lax.dynamic_update_slice(got, self.mine[names][None], (self.chip, 0, 0))
        return {n: _full_from_shards(n, sh) for n, sh in _unstack_rest(stacks, (N_CHIPS,), names).items()}

    def finish_reduce(self, key, mine, other):
        south = self.ci == 0
        self.reduced[key] = jnp.concatenate([jnp.where(south, mine, other), jnp.where(south, other, mine)],
                                            axis=0)


def _allreduce_small(part, name):
    rows = part.shape[0]

    def body(p_ref, out_ref, buf, send_sems, recv_sems, local_sem):
        x, y, c = _mesh_pos()
        me, sibling = (x, y, c), (x, y, 1 - c)
        chips = _other_chips(x, y)

        def slot(px, py, pc):
            return buf.at[pl.ds((4 * px + 2 * py + pc) * rows, rows), :]

        def copy(k, block, to, src=None):
            return pltpu.make_async_remote_copy(
                src_ref=slot(*block) if src is None else src, dst_ref=slot(*block),
                send_sem=send_sems.at[k], recv_sem=recv_sems.at[k], device_id=to, device_id_type=MESH)

        mine = pltpu.make_async_copy(p_ref, slot(*me), local_sem)
        mine.start()
        first = [copy(0, me, sibling, src=p_ref)]
        first += [copy(1 + j, me, (*chip, c), src=p_ref) for j, chip in enumerate(chips)]
        for cp in first:
            cp.start()
        passed = [copy(4 + j, (*chip, c), sibling) for j, chip in enumerate(chips)]
        for j, chip in enumerate(chips):
            copy(1 + j, (*chip, c), me).wait_recv()
            passed[j].start()
        copy(0, sibling, me).wait_recv()
        for j, chip in enumerate(chips):
            copy(4 + j, (*chip, 1 - c), me).wait_recv()
        for cp in first + passed:
            cp.wait_send()
        mine.wait()
        acc = buf[pl.ds(0, rows), :]
        for k in range(1, N_DEV):
            acc = acc + buf[pl.ds(k * rows, rows), :]
        out_ref[...] = acc

    return pl.pallas_call(
        body, name=name,
        out_shape=jax.ShapeDtypeStruct(part.shape, F32),
        in_specs=[pl.BlockSpec(memory_space=pltpu.VMEM)],
        out_specs=pl.BlockSpec(memory_space=pltpu.VMEM),
        scratch_shapes=[pltpu.VMEM((N_DEV * rows, LANES), F32), pltpu.SemaphoreType.DMA((7,)),
                        pltpu.SemaphoreType.DMA((7,)), pltpu.SemaphoreType.DMA],
        compiler_params=pltpu.CompilerParams(has_side_effects=True),
    )(part)


def _adamw(w, g, m, v, name):
    R, C = w.shape
    bs = _pick(R, (128, 64, 32, 8)) if R % 8 == 0 else R
    c1 = 1.0 / (1.0 - ADAM_B1 ** ADAM_STEP)
    c2 = 1.0 / (1.0 - ADAM_B2 ** ADAM_STEP)

    def body(w_ref, g_ref, m_ref, v_ref, d_ref, nm_ref, nv_ref):
        gg = g_ref[...]
        nm = ADAM_B1 * m_ref[...] + (1.0 - ADAM_B1) * gg
        nv = ADAM_B2 * v_ref[...] + (1.0 - ADAM_B2) * (gg * gg)
        nm_ref[...] = nm
        nv_ref[...] = nv
        d_ref[...] = -ADAM_LR * ((nm * c1) / (jnp.sqrt(nv * c2) + ADAM_EPS) + ADAM_WD * w_ref[...])

    spec = pl.BlockSpec((bs, C), lambda i: (i, 0))
    shp = jax.ShapeDtypeStruct((R, C), F32)
    return pl.pallas_call(
        body, name=name, grid=(R // bs,), in_specs=[spec] * 4, out_specs=[spec] * 3, out_shape=[shp] * 3,
        compiler_params=_params(("parallel",)),
    )(w, g, m, v)


WEIGHTS = ("norm_mix_pre_w", "w_in", "b_gate", "conv_w", "conv_b", "dt_bias", "a_log", "d_skip",
           "ssm_norm_w", "w_att_proj", "w_ssm_proj", "w_out", "norm_mix_post_w", "norm_ffn_pre_w", "w_up",
           "w_down", "norm_ffn_post_w")


def _flat_small(vals, conv_w_full):
    flat = [vals[n].reshape(-1) for n in SMALL] + [conv_w_full.reshape(-1)]
    v = jnp.concatenate(flat)
    return jnp.pad(v, (0, SMALL_ROWS * LANES - v.shape[0])).reshape(SMALL_ROWS, LANES)


def kernel(x, norm_mix_pre_w, w_in, b_gate, conv_w, conv_b, dt_bias, a_log, d_skip, ssm_norm_w, w_att_proj, w_ssm_proj, w_out, norm_mix_post_w, norm_ffn_pre_w, w_up, w_down, norm_ffn_post_w, loss_target, m_norm_mix_pre_w, m_w_in, m_b_gate, m_conv_w, m_conv_b, m_dt_bias, m_a_log, m_d_skip, m_ssm_norm_w, m_w_att_proj, m_w_ssm_proj, m_w_out, m_norm_mix_post_w, m_norm_ffn_pre_w, m_w_up, m_w_down, m_norm_ffn_post_w, v_norm_mix_pre_w, v_w_in, v_b_gate, v_conv_w, v_conv_b, v_dt_bias, v_a_log, v_d_skip, v_ssm_norm_w, v_w_att_proj, v_w_ssm_proj, v_w_out, v_norm_mix_post_w, v_norm_ffn_pre_w, v_w_up, v_w_down, v_norm_ffn_post_w):
    args = locals()

    def strip(a):
        return a[0] if a.ndim == 3 else a

    wts = {n: strip(args[n]) for n in WEIGHTS}
    mom = {n: strip(args["m_" + n]) for n in WEIGHTS}
    var = {n: strip(args["v_" + n]) for n in WEIGHTS}
    xi, yi, ci = _mesh_pos()
    chip = 2 * xi + yi

    w_in_mine = wts["w_in"].astype(BF16)
    got_in = _comm_call("allgather_w_in", _gather_comm([w_in_mine]))[0]
    full = {"w_in": _full_from_shards("w_in", lax.dynamic_update_slice(got_in, w_in_mine[None], (chip, 0, 0)))}
    ex = _Exchange(chip, ci, _stack_rest(wts, BF16, REST_EARLY), _stack_rest(wts, BF16, REST_LATE))
    cw_cols = CONV_DIM // N_CHIPS
    conv_slab = lax.dynamic_update_slice(jnp.zeros((SSM_CONV, CONV_DIM), F32),
                                         jnp.where(ci == 0, wts["conv_w"], 0.0), (0, chip * cw_cols))
    small_in = jnp.pad(conv_slab.reshape(-1), (0, SMALL_ROWS * LANES - SSM_CONV * CONV_DIM))
    conv_full = _allreduce_small(small_in.reshape(SMALL_ROWS, LANES), "gather_conv_w")
    full["conv_w"] = conv_full.reshape(-1)[:SSM_CONV * CONV_DIM].reshape(SSM_CONV, CONV_DIM)
    for n in SMALL:
        full[n] = wts[n]

    loss_part, grad_x, g = _local_step(x[0], loss_target[0], full, ex)
    loss = lax.psum(loss_part[0, 0], ("x", "y", "c"))

    gshard = {"w_in": ex.reduced["w_in"], **_unstack_rest(ex.reduced["rest"])}
    small_sum = _allreduce_small(_flat_small(g, g["conv_w"]), "allreduce_small_grads").reshape(-1)
    grads, off = {}, 0
    for n in SMALL:
        sz = wts[n].size
        grads[n] = small_sum[off:off + sz].reshape(wts[n].shape)
        off += sz
    conv_g = small_sum[off:off + SSM_CONV * CONV_DIM].reshape(SSM_CONV, CONV_DIM)
    grads["conv_w"] = lax.dynamic_slice(conv_g, (0, chip * cw_cols), (SSM_CONV, cw_cols))
    grads.update(gshard)

    delta, new_m, new_v = {}, {}, {}
    for n in BIG:
        delta[n], new_m[n], new_v[n] = _adamw(wts[n], grads[n], mom[n], var[n], f"adamw_{n}")
    small_names = SMALL + ("conv_w",)

    def pack_small(d):
        v = jnp.concatenate([d[n].reshape(-1) for n in small_names])
        rows = -(-v.shape[0] // (8 * LANES)) * 8
        return jnp.pad(v, (0, rows * LANES - v.shape[0])).reshape(rows, LANES)

    ds, ms, vs = _adamw(pack_small(wts), pack_small(grads), pack_small(mom), pack_small(var), "adamw_small")
    off = 0
    for n in small_names:
        sz = wts[n].size
        for dst, src in ((delta, ds), (new_m, ms), (new_v, vs)):
            dst[n] = src.reshape(-1)[off:off + sz].reshape(wts[n].shape)
        off += sz

    out = [loss, grad_x[None]]
    for d in (grads, delta, new_m, new_v):
        out += [d[n][None] if args[n].ndim == 3 else d[n] for n in WEIGHTS]
    return tuple(out)
```

```python
import functools
import math

import numpy as np
import jax
import jax.numpy as jnp
from jax import lax
from jax.experimental import pallas as pl
from jax.experimental.pallas import tpu as pltpu

F32 = jnp.float32
BF16 = jnp.bfloat16

D_MODEL = 1024
HEAD_DIM = 64
N_ATT_HEADS = 12
ATT_WIDTH = N_ATT_HEADS * HEAD_DIM
DILATIONS = (1, 4, 16)
ATT_BLOCK = 128
SSM_INNER = 2048
SSM_HEADS = 32
SSM_GROUPS = 8
HEADS_PER_GROUP = SSM_HEADS // SSM_GROUPS
SSM_HEAD_DIM = 64
SSM_STATE = 128
SSM_CONV = 4
SSM_CHUNK = 128
CONV_DIM = SSM_INNER + 2 * SSM_GROUPS * SSM_STATE
FFN_HIDDEN = 4 * D_MODEL
IN_SPLITS = (ATT_WIDTH, ATT_WIDTH, ATT_WIDTH, SSM_INNER, CONV_DIM, SSM_HEADS, 2 * D_MODEL)
IN_PROJ_WIDTH = sum(IN_SPLITS)
RMS_EPS = 1e-6
LANES = 128
NEG_BIG = -1e30

ADAM_LR = 0.001
ADAM_B1 = 0.9
ADAM_B2 = 0.999
ADAM_EPS = 1e-08
ADAM_WD = 0.01
ADAM_STEP = 10

N_CHIPS = 4
N_DEV = 8
VMEM_LIMIT = 56 * 1024 * 1024
MESH = pl.DeviceIdType.MESH


def _alibi_slopes(n):
    def pow2(m):
        start = 2.0 ** (-8.0 / m)
        return [start ** (i + 1) for i in range(m)]
    if (n & (n - 1)) == 0:
        s = pow2(n)
    else:
        c = 2 ** int(math.floor(math.log2(n)))
        s = pow2(c) + pow2(2 * c)[0::2][: n - c]
    return [float(v) for v in np.array(s, dtype=np.float32)]


def _params(sem):
    return pltpu.CompilerParams(dimension_semantics=sem, vmem_limit_bytes=VMEM_LIMIT)


def _dot(a, b):
    return lax.dot_general(a, b, (((1,), (0,)), ((), ())), preferred_element_type=F32)


def _dot_nt(a, b):
    return lax.dot_general(a, b, (((1,), (1,)), ((), ())), preferred_element_type=F32)


def _dot_tn(a, b):
    return lax.dot_general(a, b, (((0,), (0,)), ((), ())), preferred_element_type=F32)


def _dot_hi(a, b):
    return lax.dot_general(a, b, (((1,), (0,)), ((), ())), preferred_element_type=F32,
                           precision=lax.Precision.HIGHEST)


def _dot_tn_hi(a, b):
    return lax.dot_general(a, b, (((0,), (0,)), ((), ())), preferred_element_type=F32,
                           precision=lax.Precision.HIGHEST)


def _b(x):
    return x.astype(BF16)


def _sigmoid(x):
    return 1.0 / (1.0 + jnp.exp(-x))


def _pick(n, cands):
    for c in cands:
        if n % c == 0:
            return c
    raise ValueError(f"no tile for {n}")


def _row_block(rows, cap, mult):
    best = max(d for d in range(mult, cap + 1, mult) if rows % d == 0)
    return best


class _Comm:
    def __init__(self, plan, ins, outs, n_sems):
        self.start, self.finish = plan
        self.ins, self.outs, self.n_sems = list(ins), list(outs), n_sems


def _mm_nn(a, b, out_dtype, name, acc=None, mode=None, extra=None, comm=None, tb=False):
    M, K = a.shape
    N = b.shape[0] if tb else b.shape[1]
    tm = 1024 if M % 1024 == 0 else 512
    tn = _pick(N, (1024, 768, 512, 256, 128))
    tk = K if K <= 2304 else _pick(K, (2048, 1024))
    nk = K // tk
    nj, ni = N // tn, M // tm
    side = acc if acc is not None else extra
    n_out = 2 if mode == "relu2" else 1
    n_in = 2 + (side is not None)
    n_ci = len(comm.ins) if comm else 0
    n_co = len(comm.outs) if comm else 0

    def body(*refs):
        a_ref, b_ref = refs[0], refs[1]
        s_ref = refs[2] if side is not None else None
        o_refs = refs[n_in + n_ci:n_in + n_ci + n_out]
        if comm:
            c_args = (refs[n_in:n_in + n_ci], refs[n_in + n_ci + n_out:n_in + n_ci + n_out + n_co],
                      refs[-2], refs[-1])
            pj, pi, pk = pl.program_id(0), pl.program_id(1), pl.program_id(2)

            @pl.when(jnp.logical_and(jnp.logical_and(pj == 0, pi == 0), pk == 0))
            def _():
                comm.start(*c_args)

        def finish(r):
            if mode == "relu2":
                r = jnp.maximum(r, 0.0)
                o_refs[0][...] = _b(r)
                o_refs[1][...] = _b(r * r)
            elif mode == "mul2":
                o_refs[0][...] = _b(r * (2.0 * s_ref[...].astype(F32)))
            else:
                if acc is not None:
                    r = r + s_ref[...]
                o_refs[0][...] = r.astype(out_dtype)

        part = (_dot_nt if tb else _dot)(_b(a_ref[...]), _b(b_ref[...]))
        if nk == 1:
            finish(part)
        else:
            acc_ref = refs[n_in + n_ci + n_out + n_co]
            k = pl.program_id(2)

            @pl.when(k == 0)
            def _():
                acc_ref[...] = part

            @pl.when(jnp.logical_and(k > 0, k < nk - 1))
            def _():
                acc_ref[...] += part

            @pl.when(k == nk - 1)
            def _():
                finish(acc_ref[...] + part)

        if comm:
            @pl.when(jnp.logical_and(jnp.logical_and(pj == nj - 1, pi == ni - 1), pk == nk - 1))
            def _():
                comm.finish(*c_args)

    tile = pl.BlockSpec((tm, tn), lambda j, i, k: (i, j))
    in_specs = [pl.BlockSpec((tm, tk), lambda j, i, k: (i, k)),
                pl.BlockSpec((tn, tk), lambda j, i, k: (j, k)) if tb else
                pl.BlockSpec((tk, tn), lambda j, i, k: (k, j))]
    args = [a, b]
    if side is not None:
        in_specs.append(tile)
        args.append(side)
    odt = BF16 if mode in ("relu2", "mul2") else out_dtype
    scratch = [pltpu.VMEM((tm, tn), F32)] if nk > 1 else []
    if comm:
        scratch += [pltpu.SemaphoreType.DMA((comm.n_sems,))] * 2
        params = pltpu.CompilerParams(dimension_semantics=("arbitrary",) * 3, vmem_limit_bytes=VMEM_LIMIT,
                                      has_side_effects=True)
    else:
        params = _params(("parallel", "parallel", "arbitrary"))
    outs = pl.pallas_call(
        body, name=name, grid=(nj, ni, nk),
        in_specs=in_specs + [ANY] * n_ci,
        out_specs=[tile] * n_out + [ANY] * n_co,
        out_shape=[jax.ShapeDtypeStruct((M, N), odt)] * n_out + list(comm.outs if comm else []),
        scratch_shapes=scratch,
        compiler_params=params,
    )(*args, *(comm.ins if comm else []))
    res = outs[:n_out] if n_out > 1 else outs[0]
    return (res, outs[n_out:]) if comm else res


def _mm_tn(a, b, name):
    S, Ka = a.shape
    _, N = b.shape
    tka = _pick(Ka, (1024, 768, 512, 256, 128))
    tn = _pick(N, (1024, 768, 512, 256, 128))
    ts = 1024 if S % 1024 == 0 else 512
    ns = S // ts

    def body(a_ref, b_ref, o_ref, acc_ref):
        s = pl.program_id(2)
        part = _dot_tn(_b(a_ref[...]), _b(b_ref[...]))

        @pl.when(s == 0)
        def _():
            acc_ref[...] = part

        @pl.when(s > 0)
        def _():
            acc_ref[...] += part

        @pl.when(s == ns - 1)
        def _():
            o_ref[...] = acc_ref[...]

    return pl.pallas_call(
        body, name=name, grid=(Ka // tka, N // tn, ns),
        in_specs=[pl.BlockSpec((ts, tka), lambda i, j, s: (s, i)),
                  pl.BlockSpec((ts, tn), lambda i, j, s: (s, j))],
        out_specs=pl.BlockSpec((tka, tn), lambda i, j, s: (i, j)),
        out_shape=jax.ShapeDtypeStruct((Ka, N), F32),
        scratch_shapes=[pltpu.VMEM((tka, tn), F32)],
        compiler_params=_params(("parallel", "parallel", "arbitrary")),
    )(a, b)


def _row_call(body, row_ins, full_ins, row_outs, acc_outs, bs, name):
    S = row_ins[0].shape[0]
    assert S % bs == 0
    in_specs = [pl.BlockSpec((bs, a.shape[1]), lambda i: (i, 0)) for a in row_ins]
    in_specs += [pl.BlockSpec(a.shape, lambda i: (0, 0)) for a in full_ins]
    out_specs = [pl.BlockSpec((bs, c), lambda i: (i, 0)) for c, _ in row_outs]
    out_specs += [pl.BlockSpec(s, lambda i: (0, 0)) for s in acc_outs]
    out_shape = [jax.ShapeDtypeStruct((S, c), dt) for c, dt in row_outs]
    out_shape += [jax.ShapeDtypeStruct(s, F32) for s in acc_outs]
    return pl.pallas_call(
        body, name=name, grid=(S // bs,), in_specs=in_specs, out_specs=out_specs, out_shape=out_shape,
        compiler_params=_params(("arbitrary",)),
    )(*row_ins, *full_ins)


def _rms_vals(x, w):
    r = lax.rsqrt(jnp.mean(x * x, axis=-1, keepdims=True) + RMS_EPS)
    return x * r * w


def _rms_bwd_vals(x, w, dy):
    r = lax.rsqrt(jnp.mean(x * x, axis=-1, keepdims=True) + RMS_EPS)
    xn = x * r
    g = dy * w
    dx = r * (g - xn * jnp.mean(g * xn, axis=-1, keepdims=True))
    dw = jnp.sum(dy * xn, axis=0, keepdims=True)
    return dx, dw


def _acc_add(ref, val):
    @pl.when(pl.program_id(0) == 0)
    def _():
        ref[...] = val

    @pl.when(pl.program_id(0) > 0)
    def _():
        ref[...] += val


def _rms_fwd(x, w):
    def body(x_ref, w_ref, o_ref):
        o_ref[...] = _b(_rms_vals(x_ref[...], w_ref[...]))
    return _row_call(body, [x], [w], [(x.shape[1], BF16)], [], 512, "rms_fwd")[0]


def _gate_fwd(att_o, ssm_o, gl, b_gate):
    def body(a_ref, s_ref, g_ref, b_ref, o_ref):
        g = _sigmoid(g_ref[...] + b_ref[...])
        o_ref[...] = _b(g[:, :D_MODEL] * a_ref[...] + g[:, D_MODEL:] * s_ref[...])
    return _row_call(body, [att_o, ssm_o, gl], [b_gate], [(D_MODEL, BF16)], [], 512, "gate_fwd")[0]


def _post_pre(x, mixed, w_post, w_pre):
    def body(x_ref, m_ref, wp_ref, wn_ref, h_ref, f_ref):
        h = x_ref[...] + _rms_vals(m_ref[...], wp_ref[...])
        h_ref[...] = h
        f_ref[...] = _b(_rms_vals(h, wn_ref[...]))
    return _row_call(body, [x, mixed], [w_post, w_pre], [(D_MODEL, F32), (D_MODEL, BF16)], [], 512,
                     "post_pre")


def _relu2(up):
    def body(u_ref, o_ref):
        r = jnp.maximum(u_ref[...], 0.0)
        o_ref[...] = _b(r * r)
    return _row_call(body, [up], [], [(up.shape[1], BF16)], [], 256, "relu2")[0]


def _final(h1, down, w_post, target):
    def body(h_ref, d_ref, t_ref, w_ref, dh_ref, dd_ref, loss_ref, dw_ref):
        dn = d_ref[...]
        w = w_ref[...]
        err = h_ref[...] + _rms_vals(dn, w) - t_ref[...]
        row = jnp.mean(err * err, axis=-1, keepdims=True)
        part = 0.5 * jnp.sum(row, axis=0, keepdims=True)
        dh = err * (1.0 / D_MODEL)
        dh_ref[...] = dh
        dx, dw = _rms_bwd_vals(dn, w, dh)
        dd_ref[...] = _b(dx)
        _acc_add(loss_ref, jnp.broadcast_to(part, (1, LANES)))
        _acc_add(dw_ref, dw)
    return _row_call(body, [h1, down, target], [w_post], [(D_MODEL, F32), (D_MODEL, BF16)],
                     [(1, LANES), (1, D_MODEL)], 512, "final_loss")


def _dup(da, up):
    def body(a_ref, u_ref, o_ref):
        o_ref[...] = _b(a_ref[...] * (2.0 * jnp.maximum(u_ref[...], 0.0)))
    return _row_call(body, [da, up], [], [(up.shape[1], BF16)], [], 256, "relu2_bwd")[0]


def _mid_bwd(dh2, df, h1, mixed, w_pre, w_post):
    def body(dh_ref, df_ref, h_ref, m_ref, wn_ref, wp_ref, dh1_ref, dm_ref, dwn_ref, dwp_ref):
        dx, dwn = _rms_bwd_vals(h_ref[...], wn_ref[...], df_ref[...])
        dh1 = dh_ref[...] + dx
        dh1_ref[...] = dh1
        dm, dwp = _rms_bwd_vals(m_ref[...], wp_ref[...], dh1)
        dm_ref[...] = _b(dm)
        _acc_add(dwn_ref, dwn)
        _acc_add(dwp_ref, dwp)
    return _row_call(body, [dh2, df, h1, mixed], [w_pre, w_post], [(D_MODEL, F32), (D_MODEL, BF16)],
                     [(1, D_MODEL), (1, D_MODEL)], 512, "mid_bwd")


def _gate_bwd(dmi, att_o, ssm_o, gl, b_gate):
    def body(d_ref, a_ref, s_ref, g_ref, b_ref, da_ref, ds_ref, dg_ref, db_ref):
        g = _sigmoid(g_ref[...] + b_ref[...])
        d = d_ref[...]
        ga, gs = g[:, :D_MODEL], g[:, D_MODEL:]
        da_ref[...] = _b(ga * d)
        ds_ref[...] = _b(gs * d)
        dga = d * a_ref[...] * ga * (1.0 - ga)
        dgs = d * s_ref[...] * gs * (1.0 - gs)
        dg_ref[:, :D_MODEL] = _b(dga)
        dg_ref[:, D_MODEL:] = _b(dgs)
        _acc_add(db_ref.at[:, pl.ds(0, D_MODEL)], jnp.sum(dga, axis=0, keepdims=True))
        _acc_add(db_ref.at[:, pl.ds(D_MODEL, D_MODEL)], jnp.sum(dgs, axis=0, keepdims=True))
    return _row_call(body, [dmi, att_o, ssm_o, gl], [b_gate],
                     [(D_MODEL, BF16), (D_MODEL, BF16), (2 * D_MODEL, BF16)], [(1, 2 * D_MODEL)], 256,
                     "gate_bwd")


def _first_bwd(dh1, du, x, w_pre):
    def body(dh_ref, du_ref, x_ref, w_ref, dx_ref, dw_ref):
        dx, dw = _rms_bwd_vals(x_ref[...], w_ref[...], du_ref[...])
        dx_ref[...] = dh_ref[...] + dx
        _acc_add(dw_ref, dw)
    return _row_call(body, [dh1, du, x], [w_pre], [(D_MODEL, F32)], [(1, D_MODEL)], 512, "first_bwd")


def _group_rms(t):
    gw = SSM_INNER // SSM_GROUPS
    out = []
    for g in range(SSM_GROUPS):
        tg = t[:, g * gw:(g + 1) * gw]
        out.append(lax.rsqrt(jnp.mean(tg * tg, axis=-1, keepdims=True) + RMS_EPS))
    return out


def _gnorm_fwd(y, z, w):
    gw = SSM_INNER // SSM_GROUPS

    def body(y_ref, z_ref, w_ref, o_ref):
        zz = z_ref[...]
        t = y_ref[...] * (zz * _sigmoid(zz))
        rs = _group_rms(t)
        for g in range(SSM_GROUPS):
            sl = slice(g * gw, (g + 1) * gw)
            o_ref[:, sl] = _b(t[:, sl] * rs[g] * w_ref[:, sl])
    return _row_call(body, [y, z], [w], [(SSM_INNER, BF16)], [], 256, "gnorm_fwd")[0]


def _gnorm_bwd(dout, y, z, w):
    gw = SSM_INNER // SSM_GROUPS

    def body(d_ref, y_ref, z_ref, w_ref, dy_ref, dz_ref, dw_ref):
        zz = z_ref[...]
        yy = y_ref[...]
        sg = _sigmoid(zz)
        sz = zz * sg
        t = yy * sz
        rs = _group_rms(t)
        for g in range(SSM_GROUPS):
            sl = slice(g * gw, (g + 1) * gw)
            tn = t[:, sl] * rs[g]
            d = d_ref[:, sl]
            gg = d * w_ref[:, sl]
            dt = rs[g] * (gg - tn * jnp.mean(gg * tn, axis=-1, keepdims=True))
            dy_ref[:, sl] = dt * sz[:, sl]
            dz_ref[:, sl] = _b(dt * yy[:, sl] * (sg[:, sl] * (1.0 + zz[:, sl] * (1.0 - sg[:, sl]))))
            _acc_add(dw_ref.at[:, pl.ds(g * gw, gw)], jnp.sum(d * tn, axis=0, keepdims=True))
    return _row_call(body, [dout, y, z], [w], [(SSM_INNER, F32), (SSM_INNER, BF16)], [(1, SSM_INNER)], 256,
                     "gnorm_bwd")


def _to_pat(a, d):
    if d == 1:
        return a
    S, C = a.shape
    return a.reshape(S // d, d, C).transpose(1, 0, 2).reshape(S, C)


def _from_pat(a, d):
    if d == 1:
        return a
    S, C = a.shape
    return a.reshape(d, S // d, C).transpose(1, 0, 2).reshape(S, C)


def _head_col(stat, h):
    return stat[:, h:h + 1]


def _attn_fwd(q, k, v, d):
    S = q.shape[0]
    blk = ATT_BLOCK
    nblk = S // blk
    nbs = nblk // d
    slopes = _alibi_slopes(N_ATT_HEADS)
    scale = HEAD_DIM ** -0.5

    def body(q_ref, kc_ref, kp_ref, vc_ref, vp_ref, o_ref, m_ref, l_ref):
        n = pl.program_id(0)
        has_prev = (n % nbs) != 0
        ii = lax.broadcasted_iota(jnp.int32, (blk, blk), 0)
        jj = lax.broadcasted_iota(jnp.int32, (blk, blk), 1)
        dist_c = (ii - jj).astype(F32)
        dist_p = dist_c + float(blk)
        ok_c = ii >= jj
        ok_p = jnp.logical_and(jj >= ii, has_prev)
        lane = lax.broadcasted_iota(jnp.int32, (blk, LANES), 1)
        m_all = jnp.zeros((blk, LANES), F32)
        l_all = jnp.zeros((blk, LANES), F32)
        for h in range(N_ATT_HEADS):
            sl = slice(h * HEAD_DIM, (h + 1) * HEAD_DIM)
            qh = q_ref[:, sl]
            bias = slopes[h] * float(d)
            sc = jnp.where(ok_c, _dot_nt(qh, kc_ref[:, sl]) * scale - bias * dist_c, NEG_BIG)
            sp = jnp.where(ok_p, _dot_nt(qh, kp_ref[:, sl]) * scale - bias * dist_p, NEG_BIG)
            m = jnp.maximum(jnp.max(sc, axis=-1, keepdims=True), jnp.max(sp, axis=-1, keepdims=True))
            pc = jnp.exp(sc - m)
            pp = jnp.exp(sp - m)
            l = jnp.sum(pc, axis=-1, keepdims=True) + jnp.sum(pp, axis=-1, keepdims=True)
            o_ref[:, sl] = _dot(_b(pc), vc_ref[:, sl]) + _dot(_b(pp), vp_ref[:, sl])
            m_all = jnp.where(lane == h, m, m_all)
            l_all = jnp.where(lane == h, l, l_all)
        m_ref[...] = m_all
        l_ref[...] = l_all

    cur = pl.BlockSpec((blk, ATT_WIDTH), lambda n: (n, 0))
    prev = pl.BlockSpec((blk, ATT_WIDTH), lambda n: (jnp.maximum(n - 1, 0), 0))
    stat = pl.BlockSpec((blk, LANES), lambda n: (n, 0))
    return pl.pallas_call(
        body, name=f"attn_fwd_d{d}", grid=(nblk,),
        in_specs=[cur, cur, prev, cur, prev],
        out_specs=[cur, stat, stat],
        out_shape=[jax.ShapeDtypeStruct((S, ATT_WIDTH), F32), jax.ShapeDtypeStruct((S, LANES), F32),
                   jax.ShapeDtypeStruct((S, LANES), F32)],
        compiler_params=_params(("parallel",)),
    )(q, k, k, v, v)


def _attn_combine(os, ms, ls):
    def body(o1, o2, o3, m1, m2, m3, l1, l2, l3, att_ref, lse_ref):
        mm = [m1[...], m2[...], m3[...]]
        big = jnp.maximum(jnp.maximum(mm[0], mm[1]), mm[2])
        es = [jnp.exp(m - big) for m in mm]
        den = es[0] * l1[...] + es[1] * l2[...] + es[2] * l3[...]
        lse_ref[...] = big + jnp.log(den)
        inv = 1.0 / den
        for h in range(N_ATT_HEADS):
            sl = slice(h * HEAD_DIM, (h + 1) * HEAD_DIM)
            num = (_head_col(es[0], h) * o1[:, sl] + _head_col(es[1], h) * o2[:, sl]
                   + _head_col(es[2], h) * o3[:, sl])
            att_ref[:, sl] = num * _head_col(inv, h)
    return _row_call(body, list(os) + list(ms) + list(ls), [], [(ATT_WIDTH, F32), (LANES, F32)], [], 256,
                     "attn_combine")


def _attn_delta(d_att, att):
    def body(d_ref, a_ref, dl_ref, db_ref):
        dd = d_ref[...]
        prod = dd * a_ref[...]
        lane = lax.broadcasted_iota(jnp.int32, (dd.shape[0], LANES), 1)
        acc = jnp.zeros((dd.shape[0], LANES), F32)
        for h in range(N_ATT_HEADS):
            s = jnp.sum(prod[:, h * HEAD_DIM:(h + 1) * HEAD_DIM], axis=-1, keepdims=True)
            acc = jnp.where(lane == h, s, acc)
        dl_ref[...] = acc
        db_ref[...] = _b(dd)
    return _row_call(body, [d_att, att], [], [(LANES, F32), (ATT_WIDTH, BF16)], [], 512, "attn_delta")


def _attn_bwd(q, k, v, do, lse, delta, d):
    S = q.shape[0]
    blk = ATT_BLOCK
    nblk = S // blk
    nbs = nblk // d
    slopes = _alibi_slopes(N_ATT_HEADS)
    scale = HEAD_DIM ** -0.5

    def body(qc_ref, qn_ref, k_ref, v_ref, doc_ref, don_ref, lc_ref, ln_ref, dc_ref, dn_ref,
             dq_ref, dk_ref, dv_ref, carry_ref):
        n = pl.program_id(0)
        has_next = ((n + 1) % nbs) != 0

        @pl.when(n == 0)
        def _():
            carry_ref[...] = jnp.zeros_like(carry_ref)

        ii = lax.broadcasted_iota(jnp.int32, (blk, blk), 0)
        jj = lax.broadcasted_iota(jnp.int32, (blk, blk), 1)
        dist_c = (ii - jj).astype(F32)
        dist_p = dist_c + float(blk)
        ok_c = ii >= jj
        ok_p = jnp.logical_and(jj >= ii, has_next)
        for h in range(N_ATT_HEADS):
            sl = slice(h * HEAD_DIM, (h + 1) * HEAD_DIM)
            bias = slopes[h] * float(d)
            kh = k_ref[:, sl]
            vh = v_ref[:, sl]
            qh = qc_ref[:, sl]
            doh = doc_ref[:, sl]
            s = jnp.where(ok_c, _dot_nt(qh, kh) * scale - bias * dist_c - _head_col(lc_ref[...], h), NEG_BIG)
            p = jnp.exp(s)
            ds = p * (_dot_nt(doh, vh) - _head_col(dc_ref[...], h)) * scale
            pb, dsb = _b(p), _b(ds)
            dv = _dot_tn(pb, doh)
            dk = _dot_tn(dsb, qh)
            dq_ref[:, sl] = _dot(dsb, kh) + carry_ref[:, sl]
            qh = qn_ref[:, sl]
            doh = don_ref[:, sl]
            s = jnp.where(ok_p, _dot_nt(qh, kh) * scale - bias * dist_p - _head_col(ln_ref[...], h), NEG_BIG)
            p = jnp.exp(s)
            ds = p * (_dot_nt(doh, vh) - _head_col(dn_ref[...], h)) * scale
            pb, dsb = _b(p), _b(ds)
            dv_ref[:, sl] = dv + _dot_tn(pb, doh)
            dk_ref[:, sl] = dk + _dot_tn(dsb, qh)
            carry_ref[:, sl] = _dot(dsb, kh)

    cur = pl.BlockSpec((blk, ATT_WIDTH), lambda n: (n, 0))
    nxt = pl.BlockSpec((blk, ATT_WIDTH), lambda n: (jnp.minimum(n + 1, nblk - 1), 0))
    scur = pl.BlockSpec((blk, LANES), lambda n: (n, 0))
    snxt = pl.BlockSpec((blk, LANES), lambda n: (jnp.minimum(n + 1, nblk - 1), 0))
    shp = jax.ShapeDtypeStruct((S, ATT_WIDTH), F32)
    return pl.pallas_call(
        body, name=f"attn_bwd_d{d}", grid=(nblk,),
        in_specs=[cur, nxt, cur, cur, cur, nxt, scur, snxt, scur, snxt],
        out_specs=[cur, cur, cur],
        out_shape=[shp, shp, shp],
        scratch_shapes=[pltpu.VMEM((blk, ATT_WIDTH), F32)],
        compiler_params=_params(("arbitrary",)),
    )(q, q, k, v, do, do, lse, lse, delta, delta)


def _head_pair_masks(x):
    lane = lax.broadcasted_iota(jnp.int32, x.shape, 1)
    zero = jnp.zeros_like(x)
    return jnp.where(lane < HEAD_DIM, x, zero), jnp.where(lane >= HEAD_DIM, x, zero)


ATT_QUERY_ROWS = 32


def _attn_fwd2(qkv, d, comm=None):
    S = qkv.shape[0]
    blk = ATT_BLOCK
    nblk = S // blk
    nbs = nblk // d
    slopes = _alibi_slopes(N_ATT_HEADS)
    scale = HEAD_DIM ** -0.5
    n_ci = len(comm.ins) if comm else 0
    n_co = len(comm.outs) if comm else 0

    def body(*refs):
        q_ref, kc_ref, kp_ref, vc_ref, vp_ref = refs[:5]
        o_ref, m_ref, l_ref = refs[5 + n_ci:8 + n_ci]
        n = pl.program_id(0)
        if comm:
            c_args = (refs[5:5 + n_ci], refs[8 + n_ci:8 + n_ci + n_co], refs[-2], refs[-1])

            @pl.when(n == 0)
            def _():
                comm.start(*c_args)

        has_prev = (n % nbs) != 0
        ii = lax.broadcasted_iota(jnp.int32, (blk, 2 * blk), 0)
        jj = lax.broadcasted_iota(jnp.int32, (blk, 2 * blk), 1)
        dist_i = blk + ii - jj
        dist = dist_i.astype(F32)
        ok = jnp.logical_and(jnp.logical_and(dist_i >= 0, dist_i <= blk), jnp.logical_or(jj >= blk, has_prev))
        s_scr, p_scr = refs[8 + n_ci + n_co], refs[9 + n_ci + n_co]
        lane = lax.broadcasted_iota(jnp.int32, (blk, LANES), 1)
        for pr in range(N_ATT_HEADS // 2):
            sl = slice(pr * LANES, (pr + 1) * LANES)
            kcat = jnp.concatenate([kp_ref[:, sl], kc_ref[:, sl]], axis=0)
            for h, qh in zip((2 * pr, 2 * pr + 1), _head_pair_masks(q_ref[:, sl])):
                s_scr[h] = _dot_nt(qh, kcat)
        m_all = jnp.zeros((blk, LANES), F32)
        l_all = jnp.zeros((blk, LANES), F32)
        for h in range(N_ATT_HEADS):
            s = jnp.where(ok, s_scr[h] * scale - (slopes[h] * float(d)) * dist, NEG_BIG)
            m = jnp.max(s, axis=-1, keepdims=True)
            p = jnp.exp(s - m)
            l = jnp.sum(p, axis=-1, keepdims=True)
            m_all = jnp.where(lane == h, m, m_all)
            l_all = jnp.where(lane == h, l, l_all)
            p_scr[:, h * 2 * blk:(h + 1) * 2 * blk] = _b(p)
        for pr in range(N_ATT_HEADS // 2):
            sl = slice(pr * LANES, (pr + 1) * LANES)
            vmask = jnp.concatenate(
                _head_pair_masks(jnp.concatenate([vp_ref[:, sl], vc_ref[:, sl]], axis=0)), axis=0)
            o_ref[:, sl] = _dot(p_scr[:, pr * 4 * blk:(pr + 1) * 4 * blk], vmask)
        m_ref[...] = m_all
        l_ref[...] = l_all
        if comm:
            @pl.when(n == nblk - 1)
            def _():
                comm.finish(*c_args)

    cur = lambda c: pl.BlockSpec((blk, ATT_WIDTH), lambda n: (n, c))
    prev = lambda c: pl.BlockSpec((blk, ATT_WIDTH), lambda n: (jnp.maximum(n - 1, 0), c))
    stat = pl.BlockSpec((blk, LANES), lambda n: (n, 0))
    scratch = [pltpu.VMEM((N_ATT_HEADS, blk, 2 * blk), F32), pltpu.VMEM((blk, N_ATT_HEADS * 2 * blk), BF16)]
    if comm:
        scratch += [pltpu.SemaphoreType.DMA((comm.n_sems,))] * 2
        params = pltpu.CompilerParams(dimension_semantics=("arbitrary",), vmem_limit_bytes=VMEM_LIMIT,
                                      has_side_effects=True)
    else:
        params = _params(("parallel",))
    outs = pl.pallas_call(
        body, name=f"attn_fwd_d{d}", grid=(nblk,),
        in_specs=[cur(0), cur(1), prev(1), cur(2), prev(2)] + [ANY] * n_ci,
        out_specs=[cur(0), stat, stat] + [ANY] * n_co,
        out_shape=[jax.ShapeDtypeStruct((S, ATT_WIDTH), F32), jax.ShapeDtypeStruct((S, LANES), F32),
                   jax.ShapeDtypeStruct((S, LANES), F32)] + list(comm.outs if comm else []),
        scratch_shapes=scratch,
        compiler_params=params,
    )(qkv, qkv, qkv, qkv, qkv, *(comm.ins if comm else []))
    return (outs[0], outs[1], outs[2], outs[3:]) if comm else outs


def _attn_bwd2(qkv, do, lse, delta, d, comm=None):
    S = qkv.shape[0]
    blk = ATT_BLOCK
    nblk = S // blk
    nbs = nblk // d
    slopes = _alibi_slopes(N_ATT_HEADS)
    scale = HEAD_DIM ** -0.5
    n_ci = len(comm.ins) if comm else 0
    n_co = len(comm.outs) if comm else 0

    def body(*refs):
        qc_ref, qn_ref, k_ref, v_ref, doc_ref, don_ref, lc_ref, ln_ref, dc_ref, dn_ref = refs[:10]
        dq_ref, dk_ref, dv_ref = refs[10 + n_ci:13 + n_ci]
        carry_ref = refs[13 + n_ci + n_co]
        n = pl.program_id(0)
        has_next = ((n + 1) % nbs) != 0
        if comm:
            c_args = (refs[10:10 + n_ci], refs[13 + n_ci:13 + n_ci + n_co], refs[-2], refs[-1])

        @pl.when(n == 0)
        def _():
            carry_ref[...] = jnp.zeros_like(carry_ref)
            if comm:
                comm.start(*c_args)

        rr = lax.broadcasted_iota(jnp.int32, (2 * blk, blk), 0)
        jj = lax.broadcasted_iota(jnp.int32, (2 * blk, blk), 1)
        dist_i = rr - jj
        dist = dist_i.astype(F32)
        ok = jnp.logical_or(jnp.logical_and(rr < blk, dist_i >= 0),
                            jnp.logical_and(jnp.logical_and(rr >= blk, dist_i <= blk), has_next))
        s_scr, dp_scr, p_rows, ds_rows, ds_cols = refs[14 + n_ci + n_co:19 + n_ci + n_co]
        lcat = jnp.concatenate([lc_ref[...], ln_ref[...]], axis=0)
        dcat = jnp.concatenate([dc_ref[...], dn_ref[...]], axis=0)
        rows2 = 2 * blk

        def operands(pr):
            sl = slice(pr * LANES, (pr + 1) * LANES)
            qm = _head_pair_masks(jnp.concatenate([qc_ref[:, sl], qn_ref[:, sl]], axis=0))
            dom = _head_pair_masks(jnp.concatenate([doc_ref[:, sl], don_ref[:, sl]], axis=0))
            return sl, qm, dom

        for pr in range(N_ATT_HEADS // 2):
            sl, qm, dom = operands(pr)
            for h, qh, doh in zip((2 * pr, 2 * pr + 1), qm, dom):
                s_scr[h] = _dot_nt(qh, k_ref[:, sl])
                dp_scr[h] = _dot_nt(doh, v_ref[:, sl])
        for h in range(N_ATT_HEADS):
            s = jnp.where(ok, s_scr[h] * scale - (slopes[h] * float(d)) * dist - lcat[:, h:h + 1], NEG_BIG)
            p = jnp.exp(s)
            dsb = _b(p * (dp_scr[h] - dcat[:, h:h + 1]) * scale)
            p_rows[h * rows2:(h + 1) * rows2, :] = _b(p)
            ds_rows[h * rows2:(h + 1) * rows2, :] = dsb
            ds_cols[:, h * blk:(h + 1) * blk] = dsb
        for pr in range(N_ATT_HEADS // 2):
            sl, qm, dom = operands(pr)
            pair_rows = slice(pr * 2 * rows2, (pr + 1) * 2 * rows2)
            dv_ref[:, sl] = _b(_dot_tn(p_rows[pair_rows, :], jnp.concatenate(dom, axis=0)))
            dk_ref[:, sl] = _b(_dot_tn(ds_rows[pair_rows, :], jnp.concatenate(qm, axis=0)))
            dq = _dot(ds_cols[:, pr * 2 * blk:(pr + 1) * 2 * blk],
                      jnp.concatenate(_head_pair_masks(k_ref[:, sl]), axis=0))
            dq_ref[:, sl] = _b(dq[:blk] + carry_ref[:, sl])
            carry_ref[:, sl] = dq[blk:]

        if comm:
            @pl.when(n == nblk - 1)
            def _():
                comm.finish(*c_args)

    cur = lambda c: pl.BlockSpec((blk, ATT_WIDTH), lambda n: (n, c))
    nxt = lambda c: pl.BlockSpec((blk, ATT_WIDTH), lambda n: (jnp.minimum(n + 1, nblk - 1), c))
    scur = pl.BlockSpec((blk, LANES), lambda n: (n, 0))
    snxt = pl.BlockSpec((blk, LANES), lambda n: (jnp.minimum(n + 1, nblk - 1), 0))
    shp = jax.ShapeDtypeStruct((S, ATT_WIDTH), BF16)
    scratch = [pltpu.VMEM((blk, ATT_WIDTH), F32),
               pltpu.VMEM((N_ATT_HEADS, 2 * blk, blk), F32), pltpu.VMEM((N_ATT_HEADS, 2 * blk, blk), F32),
               pltpu.VMEM((N_ATT_HEADS * 2 * blk, blk), BF16), pltpu.VMEM((N_ATT_HEADS * 2 * blk, blk), BF16),
               pltpu.VMEM((2 * blk, N_ATT_HEADS * blk), BF16)]
    if comm:
        scratch += [pltpu.SemaphoreType.DMA((comm.n_sems,))] * 2
        params = pltpu.CompilerParams(dimension_semantics=("arbitrary",), vmem_limit_bytes=VMEM_LIMIT,
                                      has_side_effects=True)
    else:
        params = _params(("arbitrary",))
    outs = pl.pallas_call(
        body, name=f"attn_bwd_d{d}", grid=(nblk,),
        in_specs=[cur(0), nxt(0), cur(1), cur(2), cur(0), nxt(0), scur, snxt, scur, snxt] + [ANY] * n_ci,
        out_specs=[cur(0), cur(0), cur(0)] + [ANY] * n_co,
        out_shape=[shp, shp, shp] + list(comm.outs if comm else []),
        scratch_shapes=scratch,
        compiler_params=params,
    )(qkv, qkv, qkv, qkv, do, do, lse, lse, delta, delta, *(comm.ins if comm else []))
    return (outs[0], outs[1], outs[2], outs[3:]) if comm else outs


LAYOUT_TILE = 512
DILATED = tuple(d for d in DILATIONS if d > 1)


def _pat_spec(d, cols, col_block=0):
    return pl.BlockSpec((d, LAYOUT_TILE // d, cols), lambda i: (0, i, col_block))


def _pat_view(a, d):
    return a.reshape(d, a.shape[0] // d, a.shape[1])


def _qkv_layouts(qkv):
    S, C = qkv.shape
    t = LAYOUT_TILE

    def body(x_ref, nat_ref, *refs):
        pat_refs, slab = refs[:-1], refs[-1]
        nat_ref[...] = _b(x_ref[...])
        _to_slabs(slab, x_ref)
        for d, p_ref in zip(DILATED, pat_refs):
            _gather_pattern(p_ref, slab, d, BF16)

    outs = pl.pallas_call(
        body, name="qkv_layouts", grid=(S // t,),
        in_specs=[pl.BlockSpec((t, C), lambda i: (i, 0))],
        out_specs=[pl.BlockSpec((t, C), lambda i: (i, 0))] + [_pat_spec(d, C) for d in DILATED],
        out_shape=[jax.ShapeDtypeStruct((S, C), BF16)]
        + [jax.ShapeDtypeStruct((d, S // d, C), BF16) for d in DILATED],
        scratch_shapes=[pltpu.VMEM((C // LANES, t, LANES), F32)],
        compiler_params=_params(("parallel",)),
    )(qkv)
    return [outs[0]] + [o.reshape(S, C) for o in outs[1:]]


def _to_slabs(slab_ref, src_ref):
    for cb in range(slab_ref.shape[0]):
        slab_ref[cb] = src_ref[:, cb * LANES:(cb + 1) * LANES].astype(F32)


def _gather_pattern(dst_ref, slab_ref, d, dtype):
    t = slab_ref.shape[1]
    for cb in range(slab_ref.shape[0]):
        one = slab_ref.at[cb]
        for r in range(d):
            dst_ref[r, :, cb * LANES:(cb + 1) * LANES] = one[pl.ds(r, t // d, stride=d), :].astype(dtype)


def _scatter_pattern(slab_ref, src_ref, d, add=False):
    t = slab_ref.shape[1]
    for cb in range(slab_ref.shape[0]):
        one = slab_ref.at[cb]
        for r in range(d):
            idx = pl.ds(r, t // d, stride=d)
            val = src_ref[r, :, cb * LANES:(cb + 1) * LANES]
            if add:
                val = val + one[idx, :]
            one[idx, :] = val


def _attn_combine2(os, ms, ls):
    S = os[0].shape[0]
    t = LAYOUT_TILE

    def body(o1, o2, o3, m1, m2, m3, l1, l2, l3, att_ref, lse_ref, so2, so3, sm2, sm3, sl2, sl3):
        for d, src, dst in ((DILATED[0], o2, so2), (DILATED[1], o3, so3), (DILATED[0], m2, sm2),
                            (DILATED[1], m3, sm3), (DILATED[0], l2, sl2), (DILATED[1], l3, sl3)):
            _scatter_pattern(dst, src, d)
        mm = [m1[...], sm2[0], sm3[0]]
        big = jnp.maximum(jnp.maximum(mm[0], mm[1]), mm[2])
        es = [jnp.exp(m - big) for m in mm]
        den = es[0] * l1[...] + es[1] * sl2[0] + es[2] * sl3[0]
        lse_ref[...] = big + jnp.log(den)
        inv = 1.0 / den
        for h in range(N_ATT_HEADS):
            sl = slice(h * HEAD_DIM, (h + 1) * HEAD_DIM)
            cb, hl = divmod(h, 2)
            sll = slice(hl * HEAD_DIM, (hl + 1) * HEAD_DIM)
            num = (_head_col(es[0], h) * o1[:, sl] + _head_col(es[1], h) * so2[cb, :, sll]
                   + _head_col(es[2], h) * so3[cb, :, sll])
            att_ref[:, sl] = num * _head_col(inv, h)

    def specs(c):
        return [pl.BlockSpec((t, c), lambda i: (i, 0))] + [_pat_spec(d, c) for d in DILATED]

    args = [os[0]] + [_pat_view(o, d) for o, d in zip(os[1:], DILATED)]
    args += [ms[0]] + [_pat_view(m, d) for m, d in zip(ms[1:], DILATED)]
    args += [ls[0]] + [_pat_view(l, d) for l, d in zip(ls[1:], DILATED)]
    return pl.pallas_call(
        body, name="attn_combine", grid=(S // t,),
        in_specs=specs(ATT_WIDTH) + specs(LANES) + specs(LANES),
        out_specs=[pl.BlockSpec((t, ATT_WIDTH), lambda i: (i, 0)), pl.BlockSpec((t, LANES), lambda i: (i, 0))],
        out_shape=[jax.ShapeDtypeStruct((S, ATT_WIDTH), F32), jax.ShapeDtypeStruct((S, LANES), F32)],
        scratch_shapes=[pltpu.VMEM((ATT_WIDTH // LANES, t, LANES), F32)] * 2
        + [pltpu.VMEM((1, t, LANES), F32)] * 4,
        compiler_params=_params(("parallel",)),
    )(*args)


def _attn_delta2(d_att, att, lse):
    S = d_att.shape[0]
    t = LAYOUT_TILE

    def body(d_ref, a_ref, l_ref, *refs):
        out_refs, d_slab, l_slab, dl_slab = refs[:-3], refs[-3], refs[-2], refs[-1]
        dd = d_ref[...]
        prod = dd * a_ref[...]
        lane = lax.broadcasted_iota(jnp.int32, (t, LANES), 1)
        acc = jnp.zeros((t, LANES), F32)
        for h in range(N_ATT_HEADS):
            s = jnp.sum(prod[:, h * HEAD_DIM:(h + 1) * HEAD_DIM], axis=-1, keepdims=True)
            acc = jnp.where(lane == h, s, acc)
        out_refs[0][...] = _b(dd)
        out_refs[1][...] = acc
        _to_slabs(d_slab, d_ref)
        l_slab[0] = l_ref[...]
        dl_slab[0] = acc
        for k, d in enumerate(DILATED):
            db_ref, ls_ref, dl_ref = out_refs[2 + 3 * k:5 + 3 * k]
            _gather_pattern(db_ref, d_slab, d, BF16)
            _gather_pattern(ls_ref, l_slab, d, F32)
            _gather_pattern(dl_ref, dl_slab, d, F32)

    nat = lambda c: pl.BlockSpec((t, c), lambda i: (i, 0))
    out_specs = [nat(ATT_WIDTH), nat(LANES)]
    out_shape = [jax.ShapeDtypeStruct((S, ATT_WIDTH), BF16), jax.ShapeDtypeStruct((S, LANES), F32)]
    for d in DILATED:
        out_specs += [_pat_spec(d, ATT_WIDTH), _pat_spec(d, LANES), _pat_spec(d, LANES)]
        out_shape += [jax.ShapeDtypeStruct((d, S // d, ATT_WIDTH), BF16),
                      jax.ShapeDtypeStruct((d, S // d, LANES), F32),
                      jax.ShapeDtypeStruct((d, S // d, LANES), F32)]
    outs = pl.pallas_call(
        body, name="attn_delta", grid=(S // t,),
        in_specs=[nat(ATT_WIDTH), nat(ATT_WIDTH), nat(LANES)],
        out_specs=out_specs, out_shape=out_shape,
        scratch_shapes=[pltpu.VMEM((ATT_WIDTH // LANES, t, LANES), F32), pltpu.VMEM((1, t, LANES), F32),
                        pltpu.VMEM((1, t, LANES), F32)],
        compiler_params=_params(("parallel",)),
    )(d_att, att, lse)
    res = [(outs[0], lse, outs[1])]
    for k in range(len(DILATED)):
        db, ls, dl = outs[2 + 3 * k:5 + 3 * k]
        res.append((db.reshape(S, ATT_WIDTH), ls.reshape(S, LANES), dl.reshape(S, LANES)))
    return res


def _sum_qkv2(dqs, dks, dvs):
    S = dqs[0].shape[0]
    t = LAYOUT_TILE

    def body(*refs):
        o_ref, scr = refs[-2], refs[-1]
        for part in range(3):
            nat_ref, p_refs = refs[3 * part], refs[3 * part + 1:3 * part + 3]
            _to_slabs(scr, nat_ref)
            for d, p_ref in zip(DILATED, p_refs):
                _scatter_pattern(scr, p_ref, d, add=True)
            for cb in range(ATT_WIDTH // LANES):
                o_ref[:, part * ATT_WIDTH + cb * LANES:part * ATT_WIDTH + (cb + 1) * LANES] = _b(scr[cb])

    in_specs, args = [], []
    for group in (dqs, dks, dvs):
        in_specs += [pl.BlockSpec((t, ATT_WIDTH), lambda i: (i, 0))] + [_pat_spec(d, ATT_WIDTH) for d in DILATED]
        args += [group[0]] + [_pat_view(a, d) for a, d in zip(group[1:], DILATED)]
    return pl.pallas_call(
        body, name="sum_dqkv", grid=(S // t,),
        in_specs=in_specs,
        out_specs=pl.BlockSpec((t, 3 * ATT_WIDTH), lambda i: (i, 0)),
        out_shape=jax.ShapeDtypeStruct((S, 3 * ATT_WIDTH), BF16),
        scratch_shapes=[pltpu.VMEM((ATT_WIDTH // LANES, t, LANES), F32)],
        compiler_params=_params(("parallel",)),
    )(*args)


def _sum_qkv(dqs, dks, dvs):
    def body(q1, q2, q3, k1, k2, k3, v1, v2, v3, o_ref):
        o_ref[:, 0:ATT_WIDTH] = _b(q1[...] + q2[...] + q3[...])
        o_ref[:, ATT_WIDTH:2 * ATT_WIDTH] = _b(k1[...] + k2[...] + k3[...])
        o_ref[:, 2 * ATT_WIDTH:] = _b(v1[...] + v2[...] + v3[...])
    return _row_call(body, list(dqs) + list(dks) + list(dvs), [], [(3 * ATT_WIDTH, BF16)], [], 256,
                     "sum_dqkv")[0]


CONV_COLS = 1024
CONV_ROWS = 512
HALO = 8


def _conv_fwd(xbc, conv_w, conv_b):
    S, C = xbc.shape
    bs, bc = CONV_ROWS, CONV_COLS
    nr = S // bs

    def body(x_ref, halo_ref, w_ref, b_ref, o_ref, xs_ref):
        r = pl.program_id(1)
        xs_ref[pl.ds(HALO, bs), :] = x_ref[...]
        xs_ref[pl.ds(0, HALO), :] = jnp.where(r > 0, halo_ref[...], 0.0)
        pre = b_ref[...] + w_ref[3:4, :] * x_ref[...]
        for j in range(SSM_CONV - 1):
            pre = pre + w_ref[j:j + 1, :] * xs_ref[pl.ds(HALO - 3 + j, bs), :]
        o_ref[...] = pre * _sigmoid(pre)

    return pl.pallas_call(
        body, name="conv_fwd", grid=(C // bc, nr),
        in_specs=[pl.BlockSpec((bs, bc), lambda c, r: (r, c)),
                  pl.BlockSpec((HALO, bc), lambda c, r: (jnp.maximum(r * (bs // HALO) - 1, 0), c)),
                  pl.BlockSpec((SSM_CONV, bc), lambda c, r: (0, c)),
                  pl.BlockSpec((1, bc), lambda c, r: (0, c))],
        out_specs=pl.BlockSpec((bs, bc), lambda c, r: (r, c)),
        out_shape=jax.ShapeDtypeStruct((S, C), F32),
        scratch_shapes=[pltpu.VMEM((bs + HALO, bc), F32)],
        compiler_params=_params(("parallel", "arbitrary")),
    )(xbc, xbc, conv_w, conv_b)


def _conv_bwd(xbc, dact, conv_w, conv_b, col0):
    S, C = xbc.shape
    Cp = dact.shape[1]
    bs, bc = CONV_ROWS, min(CONV_COLS, Cp)
    nr = S // bs
    cb0 = col0 // bc
    last_halo = S // HALO - 1

    def body(x_ref, xp_ref, xn_ref, d_ref, dn_ref, w_ref, b_ref, dx_ref, dw_ref, db_ref,
             xs_ref, dp_ref):
        r = pl.program_id(1)
        xs_ref[pl.ds(0, HALO), :] = jnp.where(r > 0, xp_ref[...], 0.0)
        xs_ref[pl.ds(HALO, bs), :] = x_ref[...]
        xs_ref[pl.ds(HALO + bs, HALO), :] = xn_ref[...]
        ext = bs + HALO
        pre = b_ref[...] + jnp.zeros((ext, bc), F32)
        for j in range(SSM_CONV):
            pre = pre + w_ref[j:j + 1, :] * xs_ref[pl.ds(HALO - 3 + j, ext), :]
        sg = _sigmoid(pre)
        dsilu = sg * (1.0 + pre * (1.0 - sg))
        dp_ref[pl.ds(0, bs), :] = d_ref[...] * dsilu[:bs]
        dp_ref[pl.ds(bs, HALO), :] = jnp.where(r < nr - 1, dn_ref[...], 0.0) * dsilu[bs:]
        dx = jnp.zeros((bs, bc), F32)
        for j in range(SSM_CONV):
            dx = dx + w_ref[j:j + 1, :] * dp_ref[pl.ds(3 - j, bs), :]
        dx_ref[...] = _b(dx)
        dpre = dp_ref[pl.ds(0, bs), :]
        for j in range(SSM_CONV):
            part = jnp.sum(dpre * xs_ref[pl.ds(HALO - 3 + j, bs), :], axis=0, keepdims=True)

            @pl.when(r == 0)
            def _():
                dw_ref[j:j + 1, :] = part

            @pl.when(r > 0)
            def _():
                dw_ref[j:j + 1, :] += part
        part = jnp.sum(dpre, axis=0, keepdims=True)

        @pl.when(r == 0)
        def _():
            db_ref[...] = part

        @pl.when(r > 0)
        def _():
            db_ref[...] += part

    hb = bs // HALO
    return pl.pallas_call(
        body, name=f"conv_bwd_{col0}", grid=(Cp // bc, nr),
        in_specs=[pl.BlockSpec((bs, bc), lambda c, r: (r, cb0 + c)),
                  pl.BlockSpec((HALO, bc), lambda c, r: (jnp.maximum(r * hb - 1, 0), cb0 + c)),
                  pl.BlockSpec((HALO, bc), lambda c, r: (jnp.minimum((r + 1) * hb, last_halo), cb0 + c)),
                  pl.BlockSpec((bs, bc), lambda c, r: (r, c)),
                  pl.BlockSpec((HALO, bc), lambda c, r: (jnp.minimum((r + 1) * hb, last_halo), c)),
                  pl.BlockSpec((SSM_CONV, bc), lambda c, r: (0, cb0 + c)),
                  pl.BlockSpec((1, bc), lambda c, r: (0, cb0 + c))],
        out_specs=[pl.BlockSpec((bs, bc), lambda c, r: (r, c)),
                   pl.BlockSpec((SSM_CONV, bc), lambda c, r: (0, c)),
                   pl.BlockSpec((1, bc), lambda c, r: (0, c))],
        out_shape=[jax.ShapeDtypeStruct((S, Cp), BF16), jax.ShapeDtypeStruct((SSM_CONV, Cp), F32),
                   jax.ShapeDtypeStruct((1, Cp), F32)],
        scratch_shapes=[pltpu.VMEM((bs + 2 * HALO, bc), F32), pltpu.VMEM((bs + HALO, bc), F32)],
        compiler_params=_params(("parallel", "arbitrary")),
    )(xbc, xbc, xbc, dact, dact, conv_w, conv_b)


def _shift_down(x, k, top_src):
    r8 = lax.broadcasted_iota(jnp.int32, (HALO, x.shape[1]), 0)
    rolled = pltpu.roll(x, k, 0)
    top = jnp.where(r8 < k, pltpu.roll(top_src, k, 0), rolled[0:HALO])
    if x.shape[0] == HALO:
        return top
    return jnp.concatenate([top, rolled[HALO:]], axis=0)


def _shift_up(x, k, bottom_src):
    n = x.shape[0]
    r8 = lax.broadcasted_iota(jnp.int32, (HALO, x.shape[1]), 0)
    rolled = pltpu.roll(x, n - k, 0)
    bottom = jnp.where(r8 >= HALO - k, pltpu.roll(bottom_src, HALO - k, 0), rolled[n - HALO:n])
    return jnp.concatenate([rolled[:n - HALO], bottom], axis=0)


def _conv_pre(x, top_src, w_ref, b_ref):
    shifted = [x] + [_shift_down(x, k, top_src) for k in range(1, SSM_CONV)]
    pre = b_ref[...] + w_ref[SSM_CONV - 1:SSM_CONV, :] * x
    for k in range(1, SSM_CONV):
        pre = pre + w_ref[SSM_CONV - 1 - k:SSM_CONV - k, :] * shifted[k]
    return pre, shifted


def _conv_fwd2(xbc, conv_w, conv_b):
    S, C = xbc.shape
    bs, bc = CONV_ROWS, CONV_COLS
    nr = S // bs

    def body(x_ref, halo_ref, w_ref, b_ref, o_ref):
        r = pl.program_id(1)
        halo = jnp.where(r > 0, halo_ref[...], 0.0)
        pre, _ = _conv_pre(x_ref[...], halo, w_ref, b_ref)
        o_ref[...] = pre * _sigmoid(pre)

    return pl.pallas_call(
        body, name="conv_fwd", grid=(C // bc, nr),
        in_specs=[pl.BlockSpec((bs, bc), lambda c, r: (r, c)),
                  pl.BlockSpec((HALO, bc), lambda c, r: (jnp.maximum(r * (bs // HALO) - 1, 0), c)),
                  pl.BlockSpec((SSM_CONV, bc), lambda c, r: (0, c)),
                  pl.BlockSpec((1, bc), lambda c, r: (0, c))],
        out_specs=pl.BlockSpec((bs, bc), lambda c, r: (r, c)),
        out_shape=jax.ShapeDtypeStruct((S, C), F32),
        compiler_params=_params(("parallel", "arbitrary")),
    )(xbc, xbc, conv_w, conv_b)


def _conv_bwd2(xbc, dact, conv_w, conv_b):
    S, C = xbc.shape
    bs, bc = CONV_ROWS, CONV_COLS
    nr = S // bs
    hb = bs // HALO
    last_halo = S // HALO - 1

    def dsilu(pre):
        sg = _sigmoid(pre)
        return sg * (1.0 + pre * (1.0 - sg))

    def body(x_ref, xp_ref, xn_ref, d_ref, dn_ref, w_ref, b_ref, dx_ref, dw_ref, db_ref):
        r = pl.program_id(1)
        x = x_ref[...]
        pre, shifted = _conv_pre(x, jnp.where(r > 0, xp_ref[...], 0.0), w_ref, b_ref)
        dpre = d_ref[...] * dsilu(pre)
        pre_n, _ = _conv_pre(xn_ref[...], x[bs - HALO:bs], w_ref, b_ref)
        dpre_n = jnp.where(r < nr - 1, dn_ref[...], 0.0) * dsilu(pre_n)
        dx = w_ref[SSM_CONV - 1:SSM_CONV, :] * dpre
        for k in range(1, SSM_CONV):
            dx = dx + w_ref[SSM_CONV - 1 - k:SSM_CONV - k, :] * _shift_up(dpre, k, dpre_n)
        dx_ref[...] = _b(dx)
        parts = [jnp.sum(dpre * shifted[SSM_CONV - 1 - j], axis=0, keepdims=True) for j in range(SSM_CONV)]
        dbp = jnp.sum(dpre, axis=0, keepdims=True)

        @pl.when(r == 0)
        def _():
            for j in range(SSM_CONV):
                dw_ref[j:j + 1, :] = parts[j]
            db_ref[...] = dbp

        @pl.when(r > 0)
        def _():
            for j in range(SSM_CONV):
                dw_ref[j:j + 1, :] += parts[j]
            db_ref[...] += dbp

    return pl.pallas_call(
        body, name="conv_bwd", grid=(C // bc, nr),
        in_specs=[pl.BlockSpec((bs, bc), lambda c, r: (r, c)),
                  pl.BlockSpec((HALO, bc), lambda c, r: (jnp.maximum(r * hb - 1, 0), c)),
                  pl.BlockSpec((HALO, bc), lambda c, r: (jnp.minimum((r + 1) * hb, last_halo), c)),
                  pl.BlockSpec((bs, bc), lambda c, r: (r, c)),
                  pl.BlockSpec((HALO, bc), lambda c, r: (jnp.minimum((r + 1) * hb, last_halo), c)),
                  pl.BlockSpec((SSM_CONV, bc), lambda c, r: (0, c)),
                  pl.BlockSpec((1, bc), lambda c, r: (0, c))],
        out_specs=[pl.BlockSpec((bs, bc), lambda c, r: (r, c)),
                   pl.BlockSpec((SSM_CONV, bc), lambda c, r: (0, c)),
                   pl.BlockSpec((1, bc), lambda c, r: (0, c))],
        out_shape=[jax.ShapeDtypeStruct((S, C), BF16), jax.ShapeDtypeStruct((SSM_CONV, C), F32),
                   jax.ShapeDtypeStruct((1, C), F32)],
        compiler_params=_params(("parallel", "arbitrary")),
    )(xbc, xbc, xbc, dact, dact, conv_w, conv_b)


def _softplus(x):
    return jnp.maximum(x, 0.0) + jnp.log(1.0 + jnp.exp(-jnp.abs(x)))


def _ssd_common(dtr_ref, bias_ref, a_ref, g):
    ch = SSM_CHUNK
    x = dtr_ref[...] + bias_ref[...]
    dt_all = _softplus(x)
    r = lax.broadcasted_iota(jnp.int32, (LANES, LANES), 0)
    c = lax.broadcasted_iota(jnp.int32, (LANES, LANES), 1)
    sel = jnp.where(jnp.logical_and(r == HEADS_PER_GROUP * g + c, c < HEADS_PER_GROUP), 1.0, 0.0)
    dt4 = _dot_hi(dt_all, sel)
    la4 = _dot_hi(dt_all * a_ref[...], sel)
    ii = lax.broadcasted_iota(jnp.int32, (ch, ch), 0)
    jj = lax.broadcasted_iota(jnp.int32, (ch, ch), 1)
    tril = jnp.where(ii >= jj, 1.0, 0.0)
    acs = _dot_hi(tril, la4)
    return x, sel, dt4, acs, acs.T, ii >= jj


def _row8(v):
    return jnp.broadcast_to(v, (8, v.shape[1]))


def _ssd_fwd(xact, dt_raw, dt_bias, a_neg, d_skip):
    S = xact.shape[0]
    ch = SSM_CHUNK
    nch = S // ch
    hg = HEADS_PER_GROUP
    gw = hg * SSM_HEAD_DIM
    b_off = SSM_INNER // SSM_STATE
    c_off = b_off + SSM_GROUPS

    def body(x_ref, b_ref, c_ref, dtr_ref, bias_ref, a_ref, dsk_ref, y_ref, hs_ref, h_ref):
        c = pl.program_id(0)
        g = pl.program_id(1)

        @pl.when(jnp.logical_and(c == 0, g == 0))
        def _():
            h_ref[...] = jnp.zeros_like(h_ref)

        _, sel, dt4, acs, acs_t, low = _ssd_common(dtr_ref, bias_ref, a_ref, g)
        dsk4 = _dot_hi(_row8(dsk_ref[...]), sel)
        bb = _b(b_ref[...])
        cc = _b(c_ref[...])
        cb = _dot_nt(cc, bb)
        for j in range(hg):
            sl = slice(j * SSM_HEAD_DIM, (j + 1) * SSM_HEAD_DIM)
            acol = acs[:, j:j + 1]
            arow = acs_t[j:j + 1, :]
            alast = acs[ch - 1:ch, j:j + 1]
            decay = jnp.exp(jnp.where(low, acol - arow, -jnp.inf))
            xh = x_ref[:, sl]
            xd = xh * dt4[:, j:j + 1]
            hj = h_ref[hg * g + j]
            y = _dot(_b(cb * decay), _b(xd))
            y = y + _dot_nt(cc, _b(hj)) * jnp.exp(acol)
            y_ref[:, sl] = y + dsk4[0:1, j:j + 1] * xh
            hs_ref[0, j] = hj
            st = _dot_tn(_b(xd * jnp.exp(alast - acol)), bb)
            h_ref[hg * g + j] = hj * jnp.exp(alast) + st

    small = pl.BlockSpec((1, LANES), lambda c, g: (0, 0))
    return pl.pallas_call(
        body, name="ssd_fwd", grid=(nch, SSM_GROUPS),
        in_specs=[pl.BlockSpec((ch, gw), lambda c, g: (c, g)),
                  pl.BlockSpec((ch, SSM_STATE), lambda c, g: (c, b_off + g)),
                  pl.BlockSpec((ch, SSM_STATE), lambda c, g: (c, c_off + g)),
                  pl.BlockSpec((ch, LANES), lambda c, g: (c, 0)),
                  small, small, small],
        out_specs=[pl.BlockSpec((ch, gw), lambda c, g: (c, g)),
                   pl.BlockSpec((1, hg, SSM_HEAD_DIM, SSM_STATE), lambda c, g: (c, g, 0, 0))],
        out_shape=[jax.ShapeDtypeStruct((S, SSM_INNER), F32),
                   jax.ShapeDtypeStruct((nch, SSM_HEADS, SSM_HEAD_DIM, SSM_STATE), F32)],
        scratch_shapes=[pltpu.VMEM((SSM_HEADS, SSM_HEAD_DIM, SSM_STATE), F32)],
        compiler_params=_params(("arbitrary", "arbitrary")),
    )(xact, xact, xact, dt_raw, dt_bias, a_neg, d_skip)


def _ssd_bwd(xact, dt_raw, dt_bias, a_neg, d_skip, hs, dy):
    S = xact.shape[0]
    ch = SSM_CHUNK
    nch = S // ch
    hg = HEADS_PER_GROUP
    gw = hg * SSM_HEAD_DIM
    b_off = SSM_INNER // SSM_STATE
    c_off = b_off + SSM_GROUPS

    def body(x_ref, b_ref, c_ref, dtr_ref, bias_ref, a_ref, dsk_ref, hs_ref, dy_ref,
             dx_ref, db_ref, dc_ref, ddt_ref, st_ref, dh_ref, ddt_acc):
        step = pl.program_id(0)
        g = pl.program_id(1)

        @pl.when(jnp.logical_and(step == 0, g == 0))
        def _():
            dh_ref[...] = jnp.zeros_like(dh_ref)
            st_ref[...] = jnp.zeros_like(st_ref)

        @pl.when(g == 0)
        def _():
            ddt_acc[...] = jnp.zeros_like(ddt_acc)

        xraw, sel, dt4, acs, acs_t, low = _ssd_common(dtr_ref, bias_ref, a_ref, g)
        a4 = _dot_hi(_row8(a_ref[...]), sel)[0:1, :]
        dsk4 = _dot_hi(_row8(dsk_ref[...]), sel)
        bf = b_ref[...]
        cf = c_ref[...]
        bb = _b(bf)
        cc = _b(cf)
        cb = _dot_nt(cc, bb)
        lane = lax.broadcasted_iota(jnp.int32, (ch, LANES), 1)
        rowi = lax.broadcasted_iota(jnp.int32, (ch, 1), 0)
        ones = jnp.ones((ch, LANES), F32)
        dcb = jnp.zeros((ch, ch), F32)
        dc_acc = jnp.zeros((ch, SSM_STATE), F32)
        db_acc = jnp.zeros((ch, SSM_STATE), F32)
        dacs4 = jnp.zeros((ch, LANES), F32)
        ddt4 = jnp.zeros((ch, LANES), F32)
        dd4 = jnp.zeros((1, LANES), F32)
        lane1 = lax.broadcasted_iota(jnp.int32, (1, LANES), 1)
        for j in range(hg):
            sl = slice(j * SSM_HEAD_DIM, (j + 1) * SSM_HEAD_DIM)
            acol = acs[:, j:j + 1]
            arow = acs_t[j:j + 1, :]
            alast = acs[ch - 1:ch, j:j + 1]
            decay = jnp.exp(jnp.where(low, acol - arow, -jnp.inf))
            ea = jnp.exp(acol)
            dsd = jnp.exp(alast - acol)
            cd = jnp.exp(alast)
            dtc = dt4[:, j:j + 1]
            xh = x_ref[:, sl]
            xd = xh * dtc
            xdb = _b(xd)
            hj = hs_ref[0, j]
            hjb = _b(hj)
            dhn = dh_ref[hg * g + j]
            dyj = dy_ref[:, sl]
            dyb = _b(dyj)
            lm = cb * decay
            dxh = dsk4[0:1, j:j + 1] * dyj
            dd4 = jnp.where(lane1 == j, jnp.sum(jnp.sum(dyj * xh, axis=1, keepdims=True), axis=0,
                                                keepdims=True), dd4)
            dlm = _dot_nt(dyb, xdb)
            dxd = _dot_tn(_b(lm), dyb)
            gm = dlm * lm
            dcb = dcb + dlm * decay
            dac = jnp.sum(gm, axis=1, keepdims=True) - _dot_tn_hi(gm, ones)[:, 0:1]
            zz = _dot_nt(cc, hjb)
            dzb = _b(dyj * ea)
            dac = dac + jnp.sum(dyj * zz, axis=1, keepdims=True) * ea
            dc_acc = dc_acc + _dot(dzb, hjb)
            dh_in = _dot_tn(dzb, cc)
            dsb = _b(dhn)
            ww = _dot_nt(bb, dsb)
            dxd = dxd + ww * dsd
            dds = jnp.sum(ww * xd, axis=1, keepdims=True) * dsd
            db_acc = db_acc + _dot(_b(xd * dsd), dsb)
            dac = dac - dds
            dal = (jnp.sum(dds, axis=0, keepdims=True)
                   + jnp.sum(jnp.sum(dhn * hj, axis=1, keepdims=True), axis=0, keepdims=True) * cd)
            dh_ref[hg * g + j] = dh_in + dhn * cd
            dac = dac + jnp.where(rowi == ch - 1, dal, 0.0)
            dacs4 = jnp.where(lane == j, dac, dacs4)
            dx_ref[:, sl] = dxh + dxd * dtc
            ddt4 = jnp.where(lane == j, jnp.sum(dxd * xh, axis=1, keepdims=True), ddt4)
        dcbb = _b(dcb)
        dc_ref[...] = dc_acc + _dot(dcbb, bb)
        db_ref[...] = db_acc + _dot_tn(dcbb, cc)
        ii = lax.broadcasted_iota(jnp.int32, (ch, ch), 0)
        jj = lax.broadcasted_iota(jnp.int32, (ch, ch), 1)
        triu = jnp.where(ii <= jj, 1.0, 0.0)
        dla4 = _dot_hi(triu, dacs4)
        ddt4 = ddt4 + dla4 * a4
        da4 = jnp.sum(dla4 * dt4, axis=0, keepdims=True) * a4
        sel_t = sel.T
        ddt_raw = _dot_hi(ddt4, sel_t) * _sigmoid(xraw)
        ddt_acc[...] += ddt_raw
        st_ref[0:1, :] += _dot_hi(_row8(da4), sel_t)[0:1, :]
        st_ref[1:2, :] += _dot_hi(_row8(dd4), sel_t)[0:1, :]
        st_ref[2:3, :] += jnp.sum(ddt_raw, axis=0, keepdims=True)

        @pl.when(g == SSM_GROUPS - 1)
        def _():
            ddt_ref[...] = _b(ddt_acc[...])

    small = pl.BlockSpec((1, LANES), lambda s, g: (0, 0))
    rc = lambda s: nch - 1 - s
    return pl.pallas_call(
        body, name="ssd_bwd", grid=(nch, SSM_GROUPS),
        in_specs=[pl.BlockSpec((ch, gw), lambda s, g: (rc(s), g)),
                  pl.BlockSpec((ch, SSM_STATE), lambda s, g: (rc(s), b_off + g)),
                  pl.BlockSpec((ch, SSM_STATE), lambda s, g: (rc(s), c_off + g)),
                  pl.BlockSpec((ch, LANES), lambda s, g: (rc(s), 0)),
                  small, small, small,
                  pl.BlockSpec((1, hg, SSM_HEAD_DIM, SSM_STATE), lambda s, g: (rc(s), g, 0, 0)),
                  pl.BlockSpec((ch, gw), lambda s, g: (rc(s), g))],
        out_specs=[pl.BlockSpec((ch, gw), lambda s, g: (rc(s), g)),
                   pl.BlockSpec((ch, SSM_STATE), lambda s, g: (rc(s), g)),
                   pl.BlockSpec((ch, SSM_STATE), lambda s, g: (rc(s), g)),
                   pl.BlockSpec((ch, LANES), lambda s, g: (rc(s), 0)),
                   pl.BlockSpec((8, LANES), lambda s, g: (0, 0))],
        out_shape=[jax.ShapeDtypeStruct((S, SSM_INNER), F32),
                   jax.ShapeDtypeStruct((S, SSM_GROUPS * SSM_STATE), F32),
                   jax.ShapeDtypeStruct((S, SSM_GROUPS * SSM_STATE), F32),
                   jax.ShapeDtypeStruct((S, LANES), BF16),
                   jax.ShapeDtypeStruct((8, LANES), F32)],
        scratch_shapes=[pltpu.VMEM((SSM_HEADS, SSM_HEAD_DIM, SSM_STATE), F32),
                        pltpu.VMEM((ch, LANES), F32)],
        compiler_params=_params(("arbitrary", "arbitrary")),
    )(xact, xact, xact, dt_raw, dt_bias, a_neg, d_skip, hs, dy)


GROUP_W = HEADS_PER_GROUP * SSM_HEAD_DIM
B_COL0 = SSM_INNER
C_COL0 = SSM_INNER + SSM_GROUPS * SSM_STATE


def _ssd_prep(dt_raw, dt_bias, a_neg):
    S = dt_raw.shape[0]
    ch = SSM_CHUNK
    nch = S // ch

    def body(dtr_ref, bias_ref, a_ref, dt_ref, acs_ref, acst_ref, sig_ref):
        x = dtr_ref[...] + bias_ref[...]
        lane = lax.broadcasted_iota(jnp.int32, (ch, LANES), 1)
        dt = jnp.where(lane < SSM_HEADS, _softplus(x), 0.0)
        ii = lax.broadcasted_iota(jnp.int32, (ch, ch), 0)
        jj = lax.broadcasted_iota(jnp.int32, (ch, ch), 1)
        acs = _dot_hi(jnp.where(ii >= jj, 1.0, 0.0), dt * a_ref[...])
        dt_ref[...] = dt
        acs_ref[...] = acs
        acst_ref[0] = acs.T[0:SSM_HEADS, :]
        sig_ref[...] = _sigmoid(x)

    blk = pl.BlockSpec((ch, LANES), lambda c: (c, 0))
    small = pl.BlockSpec((1, LANES), lambda c: (0, 0))
    shp = jax.ShapeDtypeStruct((S, LANES), F32)
    return pl.pallas_call(
        body, name="ssd_prep", grid=(nch,),
        in_specs=[blk, small, small],
        out_specs=[blk, blk, pl.BlockSpec((1, SSM_HEADS, ch), lambda c: (c, 0, 0)), blk],
        out_shape=[shp, shp, jax.ShapeDtypeStruct((nch, SSM_HEADS, ch), F32), shp],
        compiler_params=_params(("parallel",)),
    )(dt_raw, dt_bias, a_neg)


def _expand_heads(arr, g, rows):
    lane = lax.broadcasted_iota(jnp.int32, (rows, GROUP_W), 1) // SSM_HEAD_DIM
    h0 = HEADS_PER_GROUP * g
    out = jnp.broadcast_to(arr[:, h0:h0 + 1], (rows, GROUP_W))
    for j in range(1, HEADS_PER_GROUP):
        out = jnp.where(lane == j, arr[:, h0 + j:h0 + j + 1], out)
    return out


def _seg_matrix(k, lanes_per_head, h0):
    r = lax.broadcasted_iota(jnp.int32, (k, LANES), 0)
    c = lax.broadcasted_iota(jnp.int32, (k, LANES), 1)
    return jnp.where(c == h0 + r // lanes_per_head, 1.0, 0.0).astype(BF16)


def _seg_dot(t, e):
    hi = _b(t)
    lo = _b(t - hi.astype(F32))
    return _dot(hi, e) + _dot(lo, e)


def _head_sums(t, e, rows):
    if rows >= 8:
        return _seg_dot(t, e)
    return _seg_dot(jnp.broadcast_to(t, (8, t.shape[1])), e)[0:rows]


def _pair_masks(x):
    lane = lax.broadcasted_iota(jnp.int32, x.shape, 1)
    zero = jnp.zeros_like(x)
    return jnp.where(lane < SSM_HEAD_DIM, x, zero), jnp.where(lane >= SSM_HEAD_DIM, x, zero)


def _ssd_fwd2(xact, dt, acs, acst, dsk_e):
    S = xact.shape[0]
    ch = SSM_CHUNK
    nch = S // ch

    def body(x_ref, dt_ref, acs_ref, acst_ref, dsk_ref, y_ref, hs_ref, h_ref):
        c = pl.program_id(0)

        @pl.when(c == 0)
        def _():
            h_ref[...] = jnp.zeros_like(h_ref)

        dt_all = dt_ref[...]
        acs_all = acs_ref[...]
        acst_all = acst_ref[0]
        alast = acs_all[ch - 1:ch, :]
        eacs = jnp.exp(acs_all)
        dsd_all = jnp.exp(alast - acs_all)
        cd_all = jnp.exp(alast)
        ii = lax.broadcasted_iota(jnp.int32, (ch, ch), 0)
        jj = lax.broadcasted_iota(jnp.int32, (ch, ch), 1)
        low = ii >= jj
        for g in range(SSM_GROUPS):
            xs = x_ref[:, g * GROUP_W:(g + 1) * GROUP_W]
            bb = _b(x_ref[:, B_COL0 + g * SSM_STATE:B_COL0 + (g + 1) * SSM_STATE])
            cc = _b(x_ref[:, C_COL0 + g * SSM_STATE:C_COL0 + (g + 1) * SSM_STATE])
            cb = _dot_nt(cc, bb)
            xd = xs * _expand_heads(dt_all, g, ch)
            xdb = _b(xd)
            ht = h_ref[g]
            rest = (_dot(cc, _b(ht)) * _expand_heads(eacs, g, ch)
                    + dsk_ref[:, g * GROUP_W:(g + 1) * GROUP_W] * xs)
            for p in range(HEADS_PER_GROUP // 2):
                lms = []
                for h in (HEADS_PER_GROUP * g + 2 * p, HEADS_PER_GROUP * g + 2 * p + 1):
                    diff = acs_all[:, h:h + 1] - acst_all[h:h + 1, :]
                    lms.append(_b(cb * jnp.exp(jnp.where(low, diff, -jnp.inf))))
                xa, xb = _pair_masks(xdb[:, p * LANES:(p + 1) * LANES])
                yp = _dot(jnp.concatenate(lms, axis=1), jnp.concatenate([xa, xb], axis=0))
                y_ref[:, g * GROUP_W + p * LANES:g * GROUP_W + (p + 1) * LANES] = (
                    yp + rest[:, p * LANES:(p + 1) * LANES])
            hs_ref[0, g] = ht
            st = _dot_tn(bb, _b(xd * _expand_heads(dsd_all, g, ch)))
            h_ref[g] = ht * _expand_heads(cd_all, g, 1) + st

    blk = pl.BlockSpec((ch, LANES), lambda c: (c, 0))
    return pl.pallas_call(
        body, name="ssd_fwd", grid=(nch,),
        in_specs=[pl.BlockSpec((ch, CONV_DIM), lambda c: (c, 0)), blk, blk,
                  pl.BlockSpec((1, SSM_HEADS, ch), lambda c: (c, 0, 0)),
                  pl.BlockSpec((1, SSM_INNER), lambda c: (0, 0))],
        out_specs=[pl.BlockSpec((ch, SSM_INNER), lambda c: (c, 0)),
                   pl.BlockSpec((1, SSM_GROUPS, SSM_STATE, GROUP_W), lambda c: (c, 0, 0, 0))],
        out_shape=[jax.ShapeDtypeStruct((S, SSM_INNER), F32),
                   jax.ShapeDtypeStruct((nch, SSM_GROUPS, SSM_STATE, GROUP_W), F32)],
        scratch_shapes=[pltpu.VMEM((SSM_GROUPS, SSM_STATE, GROUP_W), F32)],
        compiler_params=_params(("arbitrary",)),
    )(xact, dt, acs, acst, dsk_e)


def _ssd_bwd2(xact, dt, acs, acst, sig, a_neg, dsk_e, hs, dy):
    S = xact.shape[0]
    ch = SSM_CHUNK
    nch = S // ch

    def body(x_ref, dt_ref, acs_ref, acst_ref, sig_ref, a_ref, dsk_ref, hs_ref, dy_ref,
             dx_ref, ddt_ref, st_ref, dh_ref, rows_ref):
        step = pl.program_id(0)

        @pl.when(step == 0)
        def _():
            dh_ref[...] = jnp.zeros_like(dh_ref)
            st_ref[...] = jnp.zeros_like(st_ref)
            rows_ref[...] = jnp.zeros_like(rows_ref)

        dt_all = dt_ref[...]
        acs_all = acs_ref[...]
        acst_all = acst_ref[0]
        alast = acs_all[ch - 1:ch, :]
        eacs = jnp.exp(acs_all)
        dsd_all = jnp.exp(alast - acs_all)
        cd_all = jnp.exp(alast)
        ii = lax.broadcasted_iota(jnp.int32, (ch, ch), 0)
        jj = lax.broadcasted_iota(jnp.int32, (ch, ch), 1)
        low = ii >= jj
        lane = lax.broadcasted_iota(jnp.int32, (ch, LANES), 1)
        cols = jnp.zeros((ch, LANES), F32)
        ddt = jnp.zeros((ch, LANES), F32)
        dal = jnp.zeros((1, LANES), F32)
        ddsk = jnp.zeros((1, LANES), F32)
        for g in range(SSM_GROUPS):
            xs = x_ref[:, g * GROUP_W:(g + 1) * GROUP_W]
            bb = _b(x_ref[:, B_COL0 + g * SSM_STATE:B_COL0 + (g + 1) * SSM_STATE])
            cc = _b(x_ref[:, C_COL0 + g * SSM_STATE:C_COL0 + (g + 1) * SSM_STATE])
            cb = _dot_nt(cc, bb)
            dt_e = _expand_heads(dt_all, g, ch)
            ea_e = _expand_heads(eacs, g, ch)
            dsd_e = _expand_heads(dsd_all, g, ch)
            cd_e = _expand_heads(cd_all, g, 1)
            xd = xs * dt_e
            xdb = _b(xd)
            dyg = dy_ref[:, g * GROUP_W:(g + 1) * GROUP_W]
            dyb = _b(dyg)
            ht = hs_ref[0, g]
            htb = _b(ht)
            dhn = dh_ref[g]
            dhnb = _b(dhn)
            zz = _dot(cc, htb)
            dzb = _b(dyg * ea_e)
            d_c = _dot_nt(dzb, htb)
            dh_in = _dot_tn(cc, dzb)
            ww = _dot(bb, dhnb)
            xdd = xd * dsd_e
            d_b = _dot_nt(_b(xdd), dhnb)
            t2 = ww * xdd
            e_g = _seg_matrix(GROUP_W, SSM_HEAD_DIM, HEADS_PER_GROUP * g)
            cols = cols + _head_sums(dyg * zz * ea_e - t2, e_g, ch)
            dal = dal + _head_sums(jnp.sum(t2, axis=0, keepdims=True), e_g, 1) + cd_all * _head_sums(
                jnp.sum(dhn * ht, axis=0, keepdims=True), e_g, 1)
            dh_ref[g] = dh_in + dhn * cd_e
            ddsk = ddsk + _head_sums(jnp.sum(dyg * xs, axis=0, keepdims=True), e_g, 1)
            dxd_rest = ww * dsd_e
            dcb = jnp.zeros((ch, ch), F32)
            for p in range(HEADS_PER_GROUP // 2):
                dya, dyb2 = _pair_masks(dyb[:, p * LANES:(p + 1) * LANES])
                xp = xdb[:, p * LANES:(p + 1) * LANES]
                lms, gms = [], []
                for h, dyh in ((HEADS_PER_GROUP * g + 2 * p, dya), (HEADS_PER_GROUP * g + 2 * p + 1, dyb2)):
                    diff = acs_all[:, h:h + 1] - acst_all[h:h + 1, :]
                    decay = jnp.exp(jnp.where(low, diff, -jnp.inf))
                    lm = cb * decay
                    dlm = _dot_nt(dyh, xp)
                    gm = dlm * lm
                    dcb = dcb + dlm * decay
                    rows_ref[h:h + 1, :] = jnp.sum(gm, axis=0, keepdims=True)
                    lms.append(_b(lm))
                    gms.append(gm)
                h0 = HEADS_PER_GROUP * g + 2 * p
                cols = cols + _head_sums(jnp.concatenate(gms, axis=1), _seg_matrix(2 * ch, ch, h0), ch)
                dxd = _dot_tn(jnp.concatenate(lms, axis=0), jnp.concatenate([dya, dyb2], axis=0))
                dxd = dxd + dxd_rest[:, p * LANES:(p + 1) * LANES]
                sl = slice(g * GROUP_W + p * LANES, g * GROUP_W + (p + 1) * LANES)
                dx_ref[:, sl] = (dsk_ref[:, sl] * dyg[:, p * LANES:(p + 1) * LANES]
                                 + dxd * dt_e[:, p * LANES:(p + 1) * LANES])
                ddt = ddt + _head_sums(dxd * xs[:, p * LANES:(p + 1) * LANES],
                                       _seg_matrix(LANES, SSM_HEAD_DIM, h0), ch)
            dcbb = _b(dcb)
            dx_ref[:, C_COL0 + g * SSM_STATE:C_COL0 + (g + 1) * SSM_STATE] = d_c + _dot(dcbb, bb)
            dx_ref[:, B_COL0 + g * SSM_STATE:B_COL0 + (g + 1) * SSM_STATE] = d_b + _dot_tn(dcbb, cc)
        rowi = lax.broadcasted_iota(jnp.int32, (ch, 1), 0)
        dacs = cols - rows_ref[...].T + jnp.where(rowi == ch - 1, dal, 0.0)
        dla = _dot_hi(jnp.where(ii <= jj, 1.0, 0.0), dacs)
        a_row = a_ref[...]
        ddt_raw = (ddt + dla * a_row) * sig_ref[...]
        ddt_ref[...] = _b(ddt_raw)
        st_ref[0:1, :] += jnp.sum(dla * dt_all, axis=0, keepdims=True) * a_row
        st_ref[1:2, :] += ddsk
        st_ref[2:3, :] += jnp.sum(ddt_raw, axis=0, keepdims=True)

    rc = lambda s: nch - 1 - s
    blk = pl.BlockSpec((ch, LANES), lambda s: (rc(s), 0))
    return pl.pallas_call(
        body, name="ssd_bwd", grid=(nch,),
        in_specs=[pl.BlockSpec((ch, CONV_DIM), lambda s: (rc(s), 0)), blk, blk,
                  pl.BlockSpec((1, SSM_HEADS, ch), lambda s: (rc(s), 0, 0)), blk,
                  pl.BlockSpec((1, LANES), lambda s: (0, 0)),
                  pl.BlockSpec((1, SSM_INNER), lambda s: (0, 0)),
                  pl.BlockSpec((1, SSM_GROUPS, SSM_STATE, GROUP_W), lambda s: (rc(s), 0, 0, 0)),
                  pl.BlockSpec((ch, SSM_INNER), lambda s: (rc(s), 0))],
        out_specs=[pl.BlockSpec((ch, CONV_DIM), lambda s: (rc(s), 0)), blk,
                   pl.BlockSpec((8, LANES), lambda s: (0, 0))],
        out_shape=[jax.ShapeDtypeStruct((S, CONV_DIM), F32), jax.ShapeDtypeStruct((S, LANES), BF16),
                   jax.ShapeDtypeStruct((8, LANES), F32)],
        scratch_shapes=[pltpu.VMEM((SSM_GROUPS, SSM_STATE, GROUP_W), F32), pltpu.VMEM((LANES, ch), F32)],
        compiler_params=_params(("arbitrary",)),
    )(xact, dt, acs, acst, sig, a_neg, dsk_e, hs, dy)


def _pad_lanes(v, n=LANES):
    return jnp.pad(v, ((0, 0), (0, n - v.shape[1])))


def _local_step(x, target, w, ex=None):
    offs = np.cumsum((0,) + IN_SPLITS)
    wt_in = w["w_in_t"]
    w_qkv = wt_in[offs[0]:offs[3]]
    w_z = wt_in[offs[3]:offs[4]]
    w_xbc = wt_in[offs[4]:offs[5]]
    w_dt = jnp.pad(wt_in[offs[5]:offs[6]], ((0, LANES - SSM_HEADS), (0, 0)))
    w_g = wt_in[offs[6]:offs[7]]
    dt_bias = _pad_lanes(w["dt_bias"])
    a_neg = _pad_lanes(-jnp.exp(w["a_log"]))
    d_skip = _pad_lanes(w["d_skip"])

    u = _rms_fwd(x, w["norm_mix_pre_w"])
    if ex is None:
        xbc = _mm_nn(u, w_xbc, F32, "proj_xbc", tb=True)
    else:
        xbc, got = _mm_nn(u, w_xbc, F32, "proj_xbc", comm=_gather_comm([ex.mine[REST_EARLY]]), tb=True)
        w = {**w, **ex.rest_weights(got[0], REST_EARLY)}
    qkv = _mm_nn(u, w_qkv, F32, "proj_qkv", tb=True)
    z = _mm_nn(u, w_z, F32, "proj_z", tb=True)
    dt_raw = _mm_nn(u, w_dt, F32, "proj_dt", tb=True)
    gl = _mm_nn(u, w_g, F32, "proj_gate", tb=True)

    pats = _qkv_layouts(qkv)
    os_, ms_, ls_ = [], [], []
    for d, qkv_p in zip(DILATIONS, pats):
        if ex is not None and d == DILATIONS[0]:
            o, m, l, got = _attn_fwd2(qkv_p, d, comm=_gather_comm([ex.mine[REST_LATE]]))
            w = {**w, **ex.rest_weights(got[0], REST_LATE)}
        else:
            o, m, l = _attn_fwd2(qkv_p, d)
        os_.append(o)
        ms_.append(m)
        ls_.append(l)
    att, lse = _attn_combine2(os_, ms_, ls_)
    att_o = _mm_nn(att, w["w_att_proj"], F32, "att_proj")

    xact = _conv_fwd2(xbc, w["conv_w"], w["conv_b"])
    dsk_e = jnp.repeat(w["d_skip"], SSM_HEAD_DIM, axis=1)
    dt, acs, acst, sig = _ssd_prep(dt_raw, dt_bias, a_neg)
    y_ssd, hs = _ssd_fwd2(xact, dt, acs, acst, dsk_e)
    ssm_y = _gnorm_fwd(y_ssd, z, w["ssm_norm_w"])
    ssm_o = _mm_nn(ssm_y, w["w_ssm_proj"], F32, "ssm_proj")

    mi = _gate_fwd(att_o, ssm_o, gl, w["b_gate"])
    mixed = _mm_nn(mi, w["w_out"], F32, "out_proj")
    h1, f = _post_pre(x, mixed, w["norm_mix_post_w"], w["norm_ffn_pre_w"])
    r_up, act = _mm_nn(f, w["w_up"], BF16, "ffn_up", mode="relu2")
    down = _mm_nn(act, w["w_down"], F32, "ffn_down")
    dh2, d_down, loss, g_ffn_post = _final(h1, down, w["norm_ffn_post_w"], target)

    g = {"norm_ffn_post_w": g_ffn_post}
    g["w_down"] = _mm_tn(act, d_down, "dw_down")
    dup = _mm_nn(d_down, w["w_down"], BF16, "d_act", mode="mul2", extra=r_up, tb=True)
    g["w_up"] = _mm_tn(f, dup, "dw_up")
    df = _mm_nn(dup, w["w_up"], F32, "d_f", tb=True)
    dh1, d_mixed, g["norm_ffn_pre_w"], g["norm_mix_post_w"] = _mid_bwd(
        dh2, df, h1, mixed, w["norm_ffn_pre_w"], w["norm_mix_post_w"])
    g["w_out"] = _mm_tn(mi, d_mixed, "dw_out")
    dmi = _mm_nn(d_mixed, w["w_out"], F32, "d_mi", tb=True)
    d_att_o, d_ssm_o, dgl, g["b_gate"] = _gate_bwd(dmi, att_o, ssm_o, gl, w["b_gate"])

    g["w_att_proj"] = _mm_tn(att, d_att_o, "dw_att_proj")
    g["w_ssm_proj"] = _mm_tn(ssm_y, d_ssm_o, "dw_ssm_proj")
    if ex is None:
        d_ssm_y = _mm_nn(d_ssm_o, w["w_ssm_proj"], F32, "d_ssm_y", tb=True)
    else:
        gs_rest = jnp.concatenate(
            [_shards_from_full(n, g[n]).reshape(N_CHIPS, -1, PACK_COLS) for n in REST], axis=1)
        d_ssm_y, recv = _mm_nn(d_ssm_o, w["w_ssm_proj"], F32, "d_ssm_y", comm=_pair_comm([gs_rest]), tb=True)
        p_rest = _pair_add2(gs_rest, recv[0], ex.c_arr, "rs_pair_add_rest")

    d_att = _mm_nn(d_att_o, w["w_att_proj"], F32, "d_att", tb=True)
    dqs, dks, dvs = [], [], []
    for d, qkv_p, (do_p, lse_p, delta_p) in zip(DILATIONS, pats, _attn_delta2(d_att, att, lse)):
        if ex is not None and d == DILATIONS[0]:
            dq, dk, dv, recv3 = _attn_bwd2(qkv_p, do_p, lse_p, delta_p, d, comm=_chip_comm([p_rest]))
            q_rest = _chip_add2(p_rest, recv3[0], ex.chip_arr, "rs_chip_add_rest")
            ex.finish_reduce("rest", q_rest, _comm_call("rs_share_rest", _share_comm([q_rest]))[0])
        else:
            dq, dk, dv = _attn_bwd2(qkv_p, do_p, lse_p, delta_p, d)
        dqs.append(dq)
        dks.append(dk)
        dvs.append(dv)
    dqkv = _sum_qkv2(dqs, dks, dvs)

    dy_ssd, dz, g["ssm_norm_w"] = _gnorm_bwd(d_ssm_y, y_ssd, z, w["ssm_norm_w"])
    dxact, ddt_raw, stats = _ssd_bwd2(xact, dt, acs, acst, sig, a_neg, dsk_e, hs, dy_ssd)
    g["a_log"] = stats[0:1, :SSM_HEADS]
    g["d_skip"] = stats[1:2, :SSM_HEADS]
    g["dt_bias"] = stats[2:3, :SSM_HEADS]
    dxbc, g["conv_w"], g["conv_b"] = _conv_bwd2(xbc, dxact, w["conv_w"], w["conv_b"])

    pieces = [(dqkv, w_qkv), (dz, w_z), (dxbc, w_xbc), (ddt_raw, w_dt), (dgl, w_g)]
    gw = [_mm_tn(dp, u, f"dw_in_{i}") for i, (dp, _) in enumerate(pieces)]
    gw[3] = gw[3][:SSM_HEADS]
    if ex is None:
        g["w_in_t"] = jnp.concatenate(gw, axis=0)
    du = None
    for i, (dp, wp) in enumerate(pieces):
        if ex is not None and i == 0:
            gs_in = _rows_to_shards(gw, IN_SHARD_ROWS, IN_SHARD_PAD)
            du, recv = _mm_nn(dp, wp, F32, f"d_u_{i}", acc=du, comm=_pair_comm([gs_in]))
            p_in = _pair_add2(gs_in, recv[0], ex.c_arr, "rs_pair_add_in")
            rows = p_in.shape[1] // 2
            p_parts = [p_in[:, :rows], p_in[:, rows:]]
            q_parts = []
        elif ex is not None and i in (1, 2):
            p_part = p_parts[i - 1]
            du, recv3 = _mm_nn(dp, wp, F32, f"d_u_{i}", acc=du, comm=_chip_comm([p_part]))
            q_parts.append(_chip_add2(p_part, recv3[0], ex.chip_arr, f"rs_chip_add_in_{i}"))
            if i == 2:
                others = _comm_call("rs_share_in", _share_comm(q_parts))
                ex.finish_reduce("w_in", jnp.concatenate(q_parts, axis=0), jnp.concatenate(others, axis=0))
        else:
            du = _mm_nn(dp, wp, F32, f"d_u_{i}", acc=du)
    grad_x, g["norm_mix_pre_w"] = _first_bwd(dh1, du, x, w["norm_mix_pre_w"])
    return loss, grad_x, g


def _rows_to_shards(pieces, shard_rows, pad_rows):
    cols = pieces[0].shape[1]
    shards = []
    for s in range(N_CHIPS):
        lo, hi = s * shard_rows, (s + 1) * shard_rows
        parts, r0 = [], 0
        for p in pieces:
            a, b = max(lo, r0), min(hi, r0 + p.shape[0])
            if a < b:
                parts.append(p[a - r0:b - r0])
            r0 += p.shape[0]
        parts.append(jnp.zeros((pad_rows - shard_rows, cols), pieces[0].dtype))
        shards.append(jnp.concatenate(parts, axis=0))
    return jnp.stack(shards)


BIG = ("w_in", "w_att_proj", "w_ssm_proj", "w_out", "w_up", "w_down")
BIG_FULL_SHAPES = {"w_in": (D_MODEL, IN_PROJ_WIDTH), "w_att_proj": (ATT_WIDTH, D_MODEL),
                   "w_ssm_proj": (SSM_INNER, D_MODEL), "w_out": (D_MODEL, D_MODEL),
                   "w_up": (D_MODEL, FFN_HIDDEN), "w_down": (FFN_HIDDEN, D_MODEL)}
BIG_COL_SHARDED = {"w_in": True, "w_att_proj": True, "w_ssm_proj": False, "w_out": False, "w_up": True,
                   "w_down": False}
PACK_COLS = 1024
PACK_ROWS = 5760
PACK_HALF = PACK_ROWS // 2
PACK_BLOCK = 576
SMALL = ("norm_mix_pre_w", "b_gate", "conv_b", "dt_bias", "a_log", "d_skip", "ssm_norm_w",
         "norm_mix_post_w", "norm_ffn_pre_w", "norm_ffn_post_w")
SMALL_ROWS = 232


def _shard_shape(name):
    r, c = BIG_FULL_SHAPES[name]
    return (r, c // N_CHIPS) if BIG_COL_SHARDED[name] else (r // N_CHIPS, c)


def _pack(shards, dtype):
    flat = [shards[n].astype(dtype).reshape(-1, PACK_COLS) for n in BIG]
    rows = sum(f.shape[0] for f in flat)
    flat.append(jnp.zeros((PACK_ROWS - rows, PACK_COLS), dtype))
    return jnp.concatenate(flat, axis=0)


def _unpack(packed):
    out, r0 = {}, 0
    for n in BIG:
        shp = _shard_shape(n)
        rows = shp[0] * shp[1] // PACK_COLS
        out[n] = packed[r0:r0 + rows].reshape(shp)
        r0 += rows
    return out


def _unpack_full(gathered):
    out, r0 = {}, 0
    for n in BIG:
        shp = _shard_shape(n)
        rows = shp[0] * shp[1] // PACK_COLS
        sh = gathered[:, r0:r0 + rows].reshape((N_CHIPS,) + shp)
        if BIG_COL_SHARDED[n]:
            out[n] = sh.transpose(1, 0, 2).reshape(BIG_FULL_SHAPES[n])
        else:
            out[n] = sh.reshape(BIG_FULL_SHAPES[n])
        r0 += rows
    return out


def _pack_full(grads):
    parts = []
    rows_total = 0
    for n in BIG:
        shp = _shard_shape(n)
        gfull = grads[n]
        if BIG_COL_SHARDED[n]:
            sh = gfull.reshape(shp[0], N_CHIPS, shp[1]).transpose(1, 0, 2)
        else:
            sh = gfull.reshape((N_CHIPS,) + shp)
        parts.append(sh.reshape(N_CHIPS, -1, PACK_COLS))
        rows_total += parts[-1].shape[1]
    parts.append(jnp.zeros((N_CHIPS, PACK_ROWS - rows_total, PACK_COLS), F32))
    return jnp.concatenate(parts, axis=1)


def _mesh_pos():
    return lax.axis_index("x"), lax.axis_index("y"), lax.axis_index("c")


def _other_chips(x, y):
    return [(1 - x, y), (x, 1 - y), (1 - x, 1 - y)]


ANY = pl.BlockSpec(memory_space=pl.ANY)


def _allgather_packed(wpack):
    half = PACK_HALF

    def body(w_ref, out_ref, send_sems, recv_sems):
        x, y, c = _mesh_pos()
        me = 2 * x + y
        sibling = (x, y, 1 - c)
        chips = _other_chips(x, y)

        def rows(chip, h):
            return out_ref.at[chip, pl.ds(h * half, half), :]

        def copy(k, chip, h, to, src=None):
            return pltpu.make_async_remote_copy(
                src_ref=rows(chip, h) if src is None else src, dst_ref=rows(chip, h),
                send_sem=send_sems.at[k], recv_sem=recv_sems.at[k], device_id=to, device_id_type=MESH)

        mine_half = w_ref.at[pl.ds(c * half, half), :]
        first = [copy(j, me, c, (*chip, c), src=mine_half) for j, chip in enumerate(chips)]
        for cp in first:
            cp.start()
        passed = [copy(3 + j, 2 * chip[0] + chip[1], c, sibling) for j, chip in enumerate(chips)]
        for j, chip in enumerate(chips):
            copy(j, 2 * chip[0] + chip[1], c, (x, y, c)).wait_recv()
            passed[j].start()
        for j, chip in enumerate(chips):
            copy(3 + j, 2 * chip[0] + chip[1], 1 - c, (x, y, c)).wait_recv()
        for cp in first + passed:
            cp.wait_send()

    return pl.pallas_call(
        body, name="allgather_weights",
        out_shape=jax.ShapeDtypeStruct((N_CHIPS,) + wpack.shape, wpack.dtype),
        in_specs=[ANY], out_specs=ANY,
        scratch_shapes=[pltpu.SemaphoreType.DMA((6,)), pltpu.SemaphoreType.DMA((6,))],
        compiler_params=pltpu.CompilerParams(has_side_effects=True),
    )(wpack)


def _exchange_halves(gpack):
    half = PACK_HALF

    def body(g_ref, out_ref, send_sem, recv_sem):
        x, y, c = _mesh_pos()
        cp = pltpu.make_async_remote_copy(
            src_ref=g_ref.at[:, pl.ds((1 - c) * half, half), :], dst_ref=out_ref,
            send_sem=send_sem, recv_sem=recv_sem, device_id=(x, y, 1 - c), device_id_type=MESH)
        cp.start()
        cp.wait()

    return pl.pallas_call(
        body, name="rs_pair_exchange",
        out_shape=jax.ShapeDtypeStruct((N_CHIPS, half, PACK_COLS), F32),
        in_specs=[ANY], out_specs=ANY,
        scratch_shapes=[pltpu.SemaphoreType.DMA, pltpu.SemaphoreType.DMA],
        compiler_params=pltpu.CompilerParams(has_side_effects=True),
    )(gpack)


def _pair_add(gpack, recv, c_idx):
    nb = PACK_HALF // PACK_BLOCK

    def body(c_ref, g_ref, r_ref, o_ref):
        o_ref[...] = _b(g_ref[...] + r_ref[...])

    blk = (1, PACK_BLOCK, PACK_COLS)
    return pl.pallas_call(
        body, name="rs_pair_add",
        grid_spec=pltpu.PrefetchScalarGridSpec(
            num_scalar_prefetch=1, grid=(N_CHIPS, nb),
            in_specs=[pl.BlockSpec(blk, lambda s, i, c: (s, c[0] * nb + i, 0)),
                      pl.BlockSpec(blk, lambda s, i, c: (s, i, 0))],
            out_specs=pl.BlockSpec(blk, lambda s, i, c: (s, i, 0))),
        out_shape=jax.ShapeDtypeStruct((N_CHIPS, PACK_HALF, PACK_COLS), BF16),
        compiler_params=_params(("arbitrary", "arbitrary")),
    )(c_idx, gpack, recv)


def _exchange_chips(ppack):
    def body(p_ref, out_ref, send_sems, recv_sems):
        x, y, c = _mesh_pos()
        chips = _other_chips(x, y)
        cps = [pltpu.make_async_remote_copy(
            src_ref=p_ref.at[2 * chip[0] + chip[1]], dst_ref=out_ref.at[j],
            send_sem=send_sems.at[j], recv_sem=recv_sems.at[j], device_id=(*chip, c), device_id_type=MESH)
            for j, chip in enumerate(chips)]
        for cp in cps:
            cp.start()
        for cp in cps:
            cp.wait_recv()
        for cp in cps:
            cp.wait_send()

    return pl.pallas_call(
        body, name="rs_chip_exchange",
        out_shape=jax.ShapeDtypeStruct((N_CHIPS - 1, PACK_HALF, PACK_COLS), ppack.dtype),
        in_specs=[ANY], out_specs=ANY,
        scratch_shapes=[pltpu.SemaphoreType.DMA((3,)), pltpu.SemaphoreType.DMA((3,))],
        compiler_params=pltpu.CompilerParams(has_side_effects=True),
    )(ppack)


def _chip_add(ppack, recv, me_idx):
    nb = PACK_HALF // PACK_BLOCK

    def body(m_ref, p_ref, r0_ref, r1_ref, r2_ref, o_ref):
        o_ref[...] = ((p_ref[0].astype(F32) + r0_ref[0].astype(F32)) + r1_ref[0].astype(F32)) + r2_ref[0].astype(F32)

    blk = (1, PACK_BLOCK, PACK_COLS)
    return pl.pallas_call(
        body, name="rs_chip_add",
        grid_spec=pltpu.PrefetchScalarGridSpec(
            num_scalar_prefetch=1, grid=(nb,),
            in_specs=[pl.BlockSpec(blk, lambda i, m: (m[0], i, 0)),
                      pl.BlockSpec(blk, lambda i, m: (0, i, 0)),
                      pl.BlockSpec(blk, lambda i, m: (1, i, 0)),
                      pl.BlockSpec(blk, lambda i, m: (2, i, 0))],
            out_specs=pl.BlockSpec((PACK_BLOCK, PACK_COLS), lambda i, m: (i, 0))),
        out_shape=jax.ShapeDtypeStruct((PACK_HALF, PACK_COLS), F32),
        compiler_params=_params(("arbitrary",)),
    )(me_idx, ppack, recv, recv, recv)


def _share_halves(qhalf):
    def body(q_ref, out_ref, send_sem, recv_sem):
        x, y, c = _mesh_pos()
        cp = pltpu.make_async_remote_copy(
            src_ref=q_ref, dst_ref=out_ref, send_sem=send_sem, recv_sem=recv_sem,
            device_id=(x, y, 1 - c), device_id_type=MESH)
        cp.start()
        cp.wait()

    return pl.pallas_call(
        body, name="rs_share_halves",
        out_shape=jax.ShapeDtypeStruct(qhalf.shape, F32),
        in_specs=[ANY], out_specs=ANY,
        scratch_shapes=[pltpu.SemaphoreType.DMA, pltpu.SemaphoreType.DMA],
        compiler_params=pltpu.CompilerParams(has_side_effects=True),
    )(qhalf)


REST_EARLY = ("w_att_proj", "w_ssm_proj", "w_out")
REST_LATE = ("w_up", "w_down")
REST = REST_EARLY + REST_LATE
ADD_ROWS_CAP = 800
BF16_ROWS = 16
IN_SHARD_ROWS = IN_PROJ_WIDTH // N_CHIPS
IN_SHARD_PAD = 2688


def _stack_rest(shards, dtype, names=REST):
    return jnp.concatenate([shards[n].astype(dtype).reshape(-1, PACK_COLS) for n in names], axis=0)


def _unstack_rest(stacked, lead=(), names=REST):
    out, r0 = {}, 0
    for n in names:
        shp = _shard_shape(n)
        rows = shp[0] * shp[1] // PACK_COLS
        out[n] = stacked[..., r0:r0 + rows, :].reshape(lead + shp)
        r0 += rows
    return out


def _full_from_shards(name, sh):
    if BIG_COL_SHARDED[name]:
        return sh.transpose(1, 0, 2).reshape(BIG_FULL_SHAPES[name])
    return sh.reshape(BIG_FULL_SHAPES[name])


def _shards_from_full(name, full):
    shp = _shard_shape(name)
    if BIG_COL_SHARDED[name]:
        return full.reshape(shp[0], N_CHIPS, shp[1]).transpose(1, 0, 2)
    return full.reshape((N_CHIPS,) + shp)


def _allgather2(shards):
    n = len(shards)

    def body(*refs):
        w_refs, out_refs, send_sems, recv_sems = refs[:n], refs[n:2 * n], refs[2 * n], refs[2 * n + 1]
        x, y, c = _mesh_pos()
        me = 2 * x + y
        sibling = (x, y, 1 - c)
        chips = _other_chips(x, y)
        plans = []
        for a, (w_ref, out_ref) in enumerate(zip(w_refs, out_refs)):
            half = w_ref.shape[0] // 2

            def copy(k, chip, h, to, src=None, out_ref=out_ref, half=half, a=a):
                rows = out_ref.at[chip, pl.ds(h * half, half), :]
                return pltpu.make_async_remote_copy(
                    src_ref=rows if src is None else src, dst_ref=rows,
                    send_sem=send_sems.at[6 * a + k], recv_sem=recv_sems.at[6 * a + k],
                    device_id=to, device_id_type=MESH)

            mine_half = w_ref.at[pl.ds(c * half, half), :]
            idx = [2 * chip[0] + chip[1] for chip in chips]
            send = [copy(j, me, c, (*chip, c), src=mine_half) for j, chip in enumerate(chips)]
            land = [copy(j, idx[j], c, (x, y, c)) for j in range(N_CHIPS - 1)]
            forward = [copy(3 + j, idx[j], c, sibling) for j in range(N_CHIPS - 1)]
            land_fw = [copy(3 + j, idx[j], 1 - c, (x, y, c)) for j in range(N_CHIPS - 1)]
            plans.append((send, land, forward, land_fw))
        for send, _, _, _ in plans:
            for cp in send:
                cp.start()
        for _, land, forward, _ in plans:
            for j in range(N_CHIPS - 1):
                land[j].wait_recv()
                forward[j].start()
        for _, _, _, land_fw in plans:
            for cp in land_fw:
                cp.wait_recv()
        for send, _, forward, _ in plans:
            for cp in send + forward:
                cp.wait_send()

    return pl.pallas_call(
        body, name="allgather_weights",
        out_shape=[jax.ShapeDtypeStruct((N_CHIPS,) + s.shape, s.dtype) for s in shards],
        in_specs=[ANY] * n, out_specs=[ANY] * n,
        scratch_shapes=[pltpu.SemaphoreType.DMA((6 * n,)), pltpu.SemaphoreType.DMA((6 * n,))],
        compiler_params=pltpu.CompilerParams(has_side_effects=True),
    )(*shards)


def _exchange_halves2(gs):
    n = len(gs)

    def body(*refs):
        g_refs, out_refs, send_sems, recv_sems = refs[:n], refs[n:2 * n], refs[2 * n], refs[2 * n + 1]
        x, y, c = _mesh_pos()
        cps = []
        for a, (g_ref, out_ref) in enumerate(zip(g_refs, out_refs)):
            half = g_ref.shape[1] // 2
            cps.append(pltpu.make_async_remote_copy(
                src_ref=g_ref.at[:, pl.ds((1 - c) * half, half), :], dst_ref=out_ref,
                send_sem=send_sems.at[a], recv_sem=recv_sems.at[a], device_id=(x, y, 1 - c),
                device_id_type=MESH))
        for cp in cps:
            cp.start()
        for cp in cps:
            cp.wait()

    return pl.pallas_call(
        body, name="rs_pair_exchange",
        out_shape=[jax.ShapeDtypeStruct((N_CHIPS, g.shape[1] // 2, g.shape[2]), F32) for g in gs],
        in_specs=[ANY] * n, out_specs=[ANY] * n,
        scratch_shapes=[pltpu.SemaphoreType.DMA((n,)), pltpu.SemaphoreType.DMA((n,))],
        compiler_params=pltpu.CompilerParams(has_side_effects=True),
    )(*gs)


def _pair_add2(g, recv, c_idx, name):
    _, half, cols = recv.shape
    rb = _row_block(half, ADD_ROWS_CAP, BF16_ROWS)
    nb = half // rb

    def body(c_ref, g_ref, r_ref, o_ref):
        o_ref[...] = _b(g_ref[...] + r_ref[...])

    blk = (1, rb, cols)
    return pl.pallas_call(
        body, name=name,
        grid_spec=pltpu.PrefetchScalarGridSpec(
            num_scalar_prefetch=1, grid=(N_CHIPS, nb),
            in_specs=[pl.BlockSpec(blk, lambda s, i, c: (s, c[0] * nb + i, 0)),
                      pl.BlockSpec(blk, lambda s, i, c: (s, i, 0))],
            out_specs=pl.BlockSpec(blk, lambda s, i, c: (s, i, 0))),
        out_shape=jax.ShapeDtypeStruct(recv.shape, BF16),
        compiler_params=_params(("arbitrary", "arbitrary")),
    )(c_idx, g, recv)


def _exchange_chips2(ps):
    n = len(ps)

    def body(*refs):
        p_refs, out_refs, send_sems, recv_sems = refs[:n], refs[n:2 * n], refs[2 * n], refs[2 * n + 1]
        x, y, c = _mesh_pos()
        chips = _other_chips(x, y)
        cps = [pltpu.make_async_remote_copy(
            src_ref=p_ref.at[2 * chip[0] + chip[1]], dst_ref=out_ref.at[j],
            send_sem=send_sems.at[3 * a + j], recv_sem=recv_sems.at[3 * a + j], device_id=(*chip, c),
            device_id_type=MESH)
            for a, (p_ref, out_ref) in enumerate(zip(p_refs, out_refs)) for j, chip in enumerate(chips)]
        for cp in cps:
            cp.start()
        for cp in cps:
            cp.wait_recv()
        for cp in cps:
            cp.wait_send()

    return pl.pallas_call(
        body, name="rs_chip_exchange",
        out_shape=[jax.ShapeDtypeStruct((N_CHIPS - 1,) + p.shape[1:], p.dtype) for p in ps],
        in_specs=[ANY] * n, out_specs=[ANY] * n,
        scratch_shapes=[pltpu.SemaphoreType.DMA((3 * n,)), pltpu.SemaphoreType.DMA((3 * n,))],
        compiler_params=pltpu.CompilerParams(has_side_effects=True),
    )(*ps)


def _chip_add2(p, recv, me_idx, name):
    _, half, cols = recv.shape
    rb = _row_block(half, ADD_ROWS_CAP, BF16_ROWS)

    def body(m_ref, p_ref, r0_ref, r1_ref, r2_ref, o_ref):
        o_ref[...] = ((p_ref[0].astype(F32) + r0_ref[0].astype(F32)) + r1_ref[0].astype(F32)) + r2_ref[0].astype(F32)

    blk = (1, rb, cols)
    return pl.pallas_call(
        body, name=name,
        grid_spec=pltpu.PrefetchScalarGridSpec(
            num_scalar_prefetch=1, grid=(half // rb,),
            in_specs=[pl.BlockSpec(blk, lambda i, m: (m[0], i, 0)),
                      pl.BlockSpec(blk, lambda i, m: (0, i, 0)),
                      pl.BlockSpec(blk, lambda i, m: (1, i, 0)),
                      pl.BlockSpec(blk, lambda i, m: (2, i, 0))],
            out_specs=pl.BlockSpec((rb, cols), lambda i, m: (i, 0))),
        out_shape=jax.ShapeDtypeStruct((half, cols), F32),
        compiler_params=_params(("arbitrary",)),
    )(me_idx, p, recv, recv, recv)


def _share_halves2(qs):
    n = len(qs)

    def body(*refs):
        q_refs, out_refs, send_sems, recv_sems = refs[:n], refs[n:2 * n], refs[2 * n], refs[2 * n + 1]
        x, y, c = _mesh_pos()
        cps = [pltpu.make_async_remote_copy(
            src_ref=q_ref, dst_ref=out_ref, send_sem=send_sems.at[a], recv_sem=recv_sems.at[a],
            device_id=(x, y, 1 - c), device_id_type=MESH)
            for a, (q_ref, out_ref) in enumerate(zip(q_refs, out_refs))]
        for cp in cps:
            cp.start()
        for cp in cps:
            cp.wait()

    return pl.pallas_call(
        body, name="rs_share_halves",
        out_shape=[jax.ShapeDtypeStruct(q.shape, F32) for q in qs],
        in_specs=[ANY] * n, out_specs=[ANY] * n,
        scratch_shapes=[pltpu.SemaphoreType.DMA((n,)), pltpu.SemaphoreType.DMA((n,))],
        compiler_params=pltpu.CompilerParams(has_side_effects=True),
    )(*qs)


def _gather_plan():
    def copies(w_refs, out_refs, send_sems, recv_sems):
        x, y, c = _mesh_pos()
        me = 2 * x + y
        sibling = (x, y, 1 - c)
        chips = _other_chips(x, y)
        idx = [2 * chip[0] + chip[1] for chip in chips]
        plans = []
        for a, (w_ref, out_ref) in enumerate(zip(w_refs, out_refs)):
            half = w_ref.shape[0] // 2

            def copy(k, chip, h, to, src=None, out_ref=out_ref, half=half, a=a):
                rows = out_ref.at[chip, pl.ds(h * half, half), :]
                return pltpu.make_async_remote_copy(
                    src_ref=rows if src is None else src, dst_ref=rows,
                    send_sem=send_sems.at[6 * a + k], recv_sem=recv_sems.at[6 * a + k],
                    device_id=to, device_id_type=MESH)

            mine_half = w_ref.at[pl.ds(c * half, half), :]
            send = [copy(j, me, c, (*chip, c), src=mine_half) for j, chip in enumerate(chips)]
            land = [copy(j, idx[j], c, (x, y, c)) for j in range(N_CHIPS - 1)]
            forward = [copy(3 + j, idx[j], c, sibling) for j in range(N_CHIPS - 1)]
            land_fw = [copy(3 + j, idx[j], 1 - c, (x, y, c)) for j in range(N_CHIPS - 1)]
            plans.append((send, land, forward, land_fw))
        return plans

    def start(*refs):
        for send, _, _, _ in copies(*refs):
            for cp in send:
                cp.start()

    def finish(*refs):
        plans = copies(*refs)
        for _, land, forward, _ in plans:
            for j in range(N_CHIPS - 1):
                land[j].wait_recv()
                forward[j].start()
        for _, _, _, land_fw in plans:
            for cp in land_fw:
                cp.wait_recv()
        for send, _, forward, _ in plans:
            for cp in send + forward:
                cp.wait_send()

    return start, finish


def _pair_plan(halves):
    def copies(in_refs, out_refs, send_sems, recv_sems):
        x, y, c = _mesh_pos()
        cps = []
        for a, (g_ref, out_ref) in enumerate(zip(in_refs, out_refs)):
            if halves:
                half = g_ref.shape[1] // 2
                src = g_ref.at[:, pl.ds((1 - c) * half, half), :]
            else:
                src = g_ref
            cps.append(pltpu.make_async_remote_copy(
                src_ref=src, dst_ref=out_ref, send_sem=send_sems.at[a], recv_sem=recv_sems.at[a],
                device_id=(x, y, 1 - c), device_id_type=MESH))
        return cps

    def start(*refs):
        for cp in copies(*refs):
            cp.start()

    def finish(*refs):
        for cp in copies(*refs):
            cp.wait()

    return start, finish


def _chip_plan():
    def copies(in_refs, out_refs, send_sems, recv_sems):
        x, y, c = _mesh_pos()
        chips = _other_chips(x, y)
        return [pltpu.make_async_remote_copy(
            src_ref=p_ref.at[2 * chip[0] + chip[1]], dst_ref=out_ref.at[j],
            send_sem=send_sems.at[3 * a + j], recv_sem=recv_sems.at[3 * a + j], device_id=(*chip, c),
            device_id_type=MESH)
            for a, (p_ref, out_ref) in enumerate(zip(in_refs, out_refs)) for j, chip in enumerate(chips)]

    def start(*refs):
        for cp in copies(*refs):
            cp.start()

    def finish(*refs):
        cps = copies(*refs)
        for cp in cps:
            cp.wait_recv()
        for cp in cps:
            cp.wait_send()

    return start, finish


def _gather_comm(shards):
    return _Comm(_gather_plan(), shards, [jax.ShapeDtypeStruct((N_CHIPS,) + s.shape, s.dtype) for s in shards],
                 6 * len(shards))


def _pair_comm(gs):
    return _Comm(_pair_plan(True), gs,
                 [jax.ShapeDtypeStruct((N_CHIPS, g.shape[1] // 2, g.shape[2]), g.dtype) for g in gs], len(gs))


def _chip_comm(ps):
    return _Comm(_chip_plan(), ps, [jax.ShapeDtypeStruct((N_CHIPS - 1,) + p.shape[1:], p.dtype) for p in ps],
                 3 * len(ps))


def _share_comm(qs):
    return _Comm(_pair_plan(False), qs, [jax.ShapeDtypeStruct(q.shape, q.dtype) for q in qs], len(qs))


def _comm_call(name, comm):
    n, m = len(comm.ins), len(comm.outs)

    def body(*refs):
        args = (refs[:n], refs[n:n + m], refs[n + m], refs[n + m + 1])
        comm.start(*args)
        comm.finish(*args)

    return pl.pallas_call(
        body, name=name, out_shape=comm.outs, in_specs=[ANY] * n, out_specs=[ANY] * m,
        scratch_shapes=[pltpu.SemaphoreType.DMA((comm.n_sems,))] * 2,
        compiler_params=pltpu.CompilerParams(has_side_effects=True),
    )(*comm.ins)


class _Exchange:
    def __init__(self, chip, ci, early_mine, late_mine):
        self.chip, self.ci = chip, ci
        self.mine = {REST_EARLY: early_mine, REST_LATE: late_mine}
        self.c_arr = ci.reshape(1).astype(jnp.int32)
        self.chip_arr = chip.reshape(1).astype(jnp.int32)
        self.reduced = {}

    def rest_weights(self, got, names):
        stacks = lax.dynamic_update_slice(got, self.mine[names][None], (self.chip, 0, 0))
        return {n: _full_from_shards(n, sh) for n, sh in _unstack_rest(stacks, (N_CHIPS,), names).items()}

    def finish_reduce(self, key, mine, other):
        south = self.ci == 0
        self.reduced[key] = jnp.concatenate([jnp.where(south, mine, other), jnp.where(south, other, mine)],
                                            axis=0)


def _allreduce_small(part, name):
    rows = part.shape[0]

    def body(p_ref, out_ref, buf, send_sems, recv_sems, local_sem):
        x, y, c = _mesh_pos()
        me, sibling = (x, y, c), (x, y, 1 - c)
        chips = _other_chips(x, y)

        def slot(px, py, pc):
            return buf.at[pl.ds((4 * px + 2 * py + pc) * rows, rows), :]

        def copy(k, block, to, src=None):
            return pltpu.make_async_remote_copy(
                src_ref=slot(*block) if src is None else src, dst_ref=slot(*block),
                send_sem=send_sems.at[k], recv_sem=recv_sems.at[k], device_id=to, device_id_type=MESH)

        mine = pltpu.make_async_copy(p_ref, slot(*me), local_sem)
        mine.start()
        first = [copy(0, me, sibling, src=p_ref)]
        first += [copy(1 + j, me, (*chip, c), src=p_ref) for j, chip in enumerate(chips)]
        for cp in first:
            cp.start()
        passed = [copy(4 + j, (*chip, c), sibling) for j, chip in enumerate(chips)]
        for j, chip in enumerate(chips):
            copy(1 + j, (*chip, c), me).wait_recv()
            passed[j].start()
        copy(0, sibling, me).wait_recv()
        for j, chip in enumerate(chips):
            copy(4 + j, (*chip, 1 - c), me).wait_recv()
        for cp in first + passed:
            cp.wait_send()
        mine.wait()
        acc = buf[pl.ds(0, rows), :]
        for k in range(1, N_DEV):
            acc = acc + buf[pl.ds(k * rows, rows), :]
        out_ref[...] = acc

    return pl.pallas_call(
        body, name=name,
        out_shape=jax.ShapeDtypeStruct(part.shape, F32),
        in_specs=[pl.BlockSpec(memory_space=pltpu.VMEM)],
        out_specs=pl.BlockSpec(memory_space=pltpu.VMEM),
        scratch_shapes=[pltpu.VMEM((N_DEV * rows, LANES), F32), pltpu.SemaphoreType.DMA((7,)),
                        pltpu.SemaphoreType.DMA((7,)), pltpu.SemaphoreType.DMA],
        compiler_params=pltpu.CompilerParams(has_side_effects=True),
    )(part)


def _adamw(w, g, m, v, name):
    R, C = w.shape
    bs = _row_block(R, 512, 8) if R % 8 == 0 else R
    c1 = 1.0 / (1.0 - ADAM_B1 ** ADAM_STEP)
    c2 = 1.0 / (1.0 - ADAM_B2 ** ADAM_STEP)

    def body(w_ref, g_ref, m_ref, v_ref, d_ref, nm_ref, nv_ref):
        gg = g_ref[...]
        nm = ADAM_B1 * m_ref[...] + (1.0 - ADAM_B1) * gg
        nv = ADAM_B2 * v_ref[...] + (1.0 - ADAM_B2) * (gg * gg)
        nm_ref[...] = nm
        nv_ref[...] = nv
        d_ref[...] = -ADAM_LR * ((nm * c1) / (jnp.sqrt(nv * c2) + ADAM_EPS) + ADAM_WD * w_ref[...])

    spec = pl.BlockSpec((bs, C), lambda i: (i, 0))
    shp = jax.ShapeDtypeStruct((R, C), F32)
    return pl.pallas_call(
        body, name=name, grid=(R // bs,), in_specs=[spec] * 4, out_specs=[spec] * 3, out_shape=[shp] * 3,
        compiler_params=_params(("parallel",)),
    )(w, g, m, v)


WEIGHTS = ("norm_mix_pre_w", "w_in", "b_gate", "conv_w", "conv_b", "dt_bias", "a_log", "d_skip",
           "ssm_norm_w", "w_att_proj", "w_ssm_proj", "w_out", "norm_mix_post_w", "norm_ffn_pre_w", "w_up",
           "w_down", "norm_ffn_post_w")


def _flat_small(vals, conv_w_full):
    flat = [vals[n].reshape(-1) for n in SMALL] + [conv_w_full.reshape(-1)]
    v = jnp.concatenate(flat)
    return jnp.pad(v, (0, SMALL_ROWS * LANES - v.shape[0])).reshape(SMALL_ROWS, LANES)


def kernel(x, norm_mix_pre_w, w_in, b_gate, conv_w, conv_b, dt_bias, a_log, d_skip, ssm_norm_w, w_att_proj, w_ssm_proj, w_out, norm_mix_post_w, norm_ffn_pre_w, w_up, w_down, norm_ffn_post_w, loss_target, m_norm_mix_pre_w, m_w_in, m_b_gate, m_conv_w, m_conv_b, m_dt_bias, m_a_log, m_d_skip, m_ssm_norm_w, m_w_att_proj, m_w_ssm_proj, m_w_out, m_norm_mix_post_w, m_norm_ffn_pre_w, m_w_up, m_w_down, m_norm_ffn_post_w, v_norm_mix_pre_w, v_w_in, v_b_gate, v_conv_w, v_conv_b, v_dt_bias, v_a_log, v_d_skip, v_ssm_norm_w, v_w_att_proj, v_w_ssm_proj, v_w_out, v_norm_mix_post_w, v_norm_ffn_pre_w, v_w_up, v_w_down, v_norm_ffn_post_w):
    args = locals()

    def strip(a):
        return a[0] if a.ndim == 3 else a

    wts = {n: strip(args[n]) for n in WEIGHTS}
    mom = {n: strip(args["m_" + n]) for n in WEIGHTS}
    var = {n: strip(args["v_" + n]) for n in WEIGHTS}
    xi, yi, ci = _mesh_pos()
    chip = 2 * xi + yi

    tr = lambda a: jnp.swapaxes(a, 0, 1)
    w_in_mine = jnp.pad(tr(wts["w_in"]).astype(BF16), ((0, IN_SHARD_PAD - IN_SHARD_ROWS), (0, 0)))
    got_in = _comm_call("allgather_w_in", _gather_comm([w_in_mine]))[0]
    stacks_in = lax.dynamic_update_slice(got_in, w_in_mine[None], (chip, 0, 0))
    full = {"w_in_t": stacks_in[:, :IN_SHARD_ROWS].reshape(IN_PROJ_WIDTH, D_MODEL)}
    ex = _Exchange(chip, ci, _stack_rest(wts, BF16, REST_EARLY), _stack_rest(wts, BF16, REST_LATE))
    cw_cols = CONV_DIM // N_CHIPS
    conv_slab = lax.dynamic_update_slice(jnp.zeros((SSM_CONV, CONV_DIM), F32),
                                         jnp.where(ci == 0, wts["conv_w"], 0.0), (0, chip * cw_cols))
    small_in = jnp.pad(conv_slab.reshape(-1), (0, SMALL_ROWS * LANES - SSM_CONV * CONV_DIM))
    conv_full = _allreduce_small(small_in.reshape(SMALL_ROWS, LANES), "gather_conv_w")
    full["conv_w"] = conv_full.reshape(-1)[:SSM_CONV * CONV_DIM].reshape(SSM_CONV, CONV_DIM)
    for n in SMALL:
        full[n] = wts[n]

    loss_part, grad_x, g = _local_step(x[0], loss_target[0], full, ex)
    loss = lax.psum(loss_part[0, 0], ("x", "y", "c"))

    gshard = _unstack_rest(ex.reduced["rest"])
    g_in_t = ex.reduced["w_in"][:IN_SHARD_ROWS]
    small_sum = _allreduce_small(_flat_small(g, g["conv_w"]), "allreduce_small_grads").reshape(-1)
    grads, off = {}, 0
    for n in SMALL:
        sz = wts[n].size
        grads[n] = small_sum[off:off + sz].reshape(wts[n].shape)
        off += sz
    conv_g = small_sum[off:off + SSM_CONV * CONV_DIM].reshape(SSM_CONV, CONV_DIM)
    grads["conv_w"] = lax.dynamic_slice(conv_g, (0, chip * cw_cols), (SSM_CONV, cw_cols))
    grads.update(gshard)

    delta, new_m, new_v = {}, {}, {}
    for n in REST:
        delta[n], new_m[n], new_v[n] = _adamw(wts[n], grads[n], mom[n], var[n], f"adamw_{n}")
    in_t = _adamw(tr(wts["w_in"]), g_in_t, tr(mom["w_in"]), tr(var["w_in"]), "adamw_w_in")
    grads["w_in"] = tr(g_in_t)
    delta["w_in"], new_m["w_in"], new_v["w_in"] = (tr(a) for a in in_t)
    small_names = SMALL + ("conv_w",)

    def pack_small(d):
        v = jnp.concatenate([d[n].reshape(-1) for n in small_names])
        rows = -(-v.shape[0] // (8 * LANES)) * 8
        return jnp.pad(v, (0, rows * LANES - v.shape[0])).reshape(rows, LANES)

    ds, ms, vs = _adamw(pack_small(wts), pack_small(grads), pack_small(mom), pack_small(var), "adamw_small")
    off = 0
    for n in small_names:
        sz = wts[n].size
        for dst, src in ((delta, ds), (new_m, ms), (new_v, vs)):
            dst[n] = src.reshape(-1)[off:off + sz].reshape(wts[n].shape)
        off += sz

    out = [loss, grad_x[None]]
    for d in (grads, delta, new_m, new_v):
        out += [d[n][None] if args[n].ndim == 3 else d[n] for n in WEIGHTS]
    return tuple(out)
```

```python
import functools
import math

import numpy as np
import jax
import jax.numpy as jnp
from jax import lax
from jax.experimental import pallas as pl
from jax.experimental.pallas import tpu as pltpu

F32 = jnp.float32
BF16 = jnp.bfloat16

D_MODEL = 1024
HEAD_DIM = 64
N_ATT_HEADS = 12
ATT_WIDTH = N_ATT_HEADS * HEAD_DIM
DILATIONS = (1, 4, 16)
ATT_BLOCK = 128
SSM_INNER = 2048
SSM_HEADS = 32
SSM_GROUPS = 8
HEADS_PER_GROUP = SSM_HEADS // SSM_GROUPS
SSM_HEAD_DIM = 64
SSM_STATE = 128
SSM_CONV = 4
SSM_CHUNK = 128
CONV_DIM = SSM_INNER + 2 * SSM_GROUPS * SSM_STATE
FFN_HIDDEN = 4 * D_MODEL
IN_SPLITS = (ATT_WIDTH, ATT_WIDTH, ATT_WIDTH, SSM_INNER, CONV_DIM, SSM_HEADS, 2 * D_MODEL)
IN_PROJ_WIDTH = sum(IN_SPLITS)
RMS_EPS = 1e-6
LANES = 128
NEG_BIG = -1e30

ADAM_LR = 0.001
ADAM_B1 = 0.9
ADAM_B2 = 0.999
ADAM_EPS = 1e-08
ADAM_WD = 0.01
ADAM_STEP = 10

N_CHIPS = 4
N_DEV = 8
VMEM_LIMIT = 56 * 1024 * 1024
MESH = pl.DeviceIdType.MESH


def _alibi_slopes(n):
    def pow2(m):
        start = 2.0 ** (-8.0 / m)
        return [start ** (i + 1) for i in range(m)]
    if (n & (n - 1)) == 0:
        s = pow2(n)
    else:
        c = 2 ** int(math.floor(math.log2(n)))
        s = pow2(c) + pow2(2 * c)[0::2][: n - c]
    return [float(v) for v in np.array(s, dtype=np.float32)]


def _params(sem):
    return pltpu.CompilerParams(dimension_semantics=sem, vmem_limit_bytes=VMEM_LIMIT)


def _dot(a, b):
    return lax.dot_general(a, b, (((1,), (0,)), ((), ())), preferred_element_type=F32)


def _dot_nt(a, b):
    return lax.dot_general(a, b, (((1,), (1,)), ((), ())), preferred_element_type=F32)


def _dot_tn(a, b):
    return lax.dot_general(a, b, (((0,), (0,)), ((), ())), preferred_element_type=F32)


def _dot_hi(a, b):
    return lax.dot_general(a, b, (((1,), (0,)), ((), ())), preferred_element_type=F32,
                           precision=lax.Precision.HIGHEST)


def _dot_tn_hi(a, b):
    return lax.dot_general(a, b, (((0,), (0,)), ((), ())), preferred_element_type=F32,
                           precision=lax.Precision.HIGHEST)


def _b(x):
    return x.astype(BF16)


def _sigmoid(x):
    return 1.0 / (1.0 + jnp.exp(-x))


def _pick(n, cands):
    for c in cands:
        if n % c == 0:
            return c
    raise ValueError(f"no tile for {n}")


def _row_block(rows, cap, mult):
    best = max(d for d in range(mult, cap + 1, mult) if rows % d == 0)
    return best


class _Comm:
    def __init__(self, plan, ins, outs, n_sems):
        self.start, self.finish = plan
        self.ins, self.outs, self.n_sems = list(ins), list(outs), n_sems


def _mm_nn(a, b, out_dtype, name, acc=None, mode=None, extra=None, comm=None, tb=False):
    M, K = a.shape
    N = b.shape[0] if tb else b.shape[1]
    tm = 1024 if M % 1024 == 0 else 512
    tn = _pick(N, (1024, 768, 512, 256, 128))
    tk = K if K <= 2304 else _pick(K, (2048, 1024))
    nk = K // tk
    nj, ni = N // tn, M // tm
    side = acc if acc is not None else extra
    n_out = 2 if mode == "relu2" else 1
    n_in = 2 + (side is not None)
    n_ci = len(comm.ins) if comm else 0
    n_co = len(comm.outs) if comm else 0

    def body(*refs):
        a_ref, b_ref = refs[0], refs[1]
        s_ref = refs[2] if side is not None else None
        o_refs = refs[n_in + n_ci:n_in + n_ci + n_out]
        if comm:
            c_args = (refs[n_in:n_in + n_ci], refs[n_in + n_ci + n_out:n_in + n_ci + n_out + n_co],
                      refs[-2], refs[-1])
            pj, pi, pk = pl.program_id(0), pl.program_id(1), pl.program_id(2)

            @pl.when(jnp.logical_and(jnp.logical_and(pj == 0, pi == 0), pk == 0))
            def _():
                comm.start(*c_args)

        def finish(r):
            if mode == "relu2":
                r = jnp.maximum(r, 0.0)
                o_refs[0][...] = _b(r)
                o_refs[1][...] = _b(r * r)
            elif mode == "mul2":
                o_refs[0][...] = _b(r * (2.0 * s_ref[...].astype(F32)))
            else:
                if acc is not None:
                    r = r + s_ref[...]
                o_refs[0][...] = r.astype(out_dtype)

        part = (_dot_nt if tb else _dot)(_b(a_ref[...]), _b(b_ref[...]))
        if nk == 1:
            finish(part)
        else:
            acc_ref = refs[n_in + n_ci + n_out + n_co]
            k = pl.program_id(2)

            @pl.when(k == 0)
            def _():
                acc_ref[...] = part

            @pl.when(jnp.logical_and(k > 0, k < nk - 1))
            def _():
                acc_ref[...] += part

            @pl.when(k == nk - 1)
            def _():
                finish(acc_ref[...] + part)

        if comm:
            @pl.when(jnp.logical_and(jnp.logical_and(pj == nj - 1, pi == ni - 1), pk == nk - 1))
            def _():
                comm.finish(*c_args)

    tile = pl.BlockSpec((tm, tn), lambda j, i, k: (i, j))
    in_specs = [pl.BlockSpec((tm, tk), lambda j, i, k: (i, k)),
                pl.BlockSpec((tn, tk), lambda j, i, k: (j, k)) if tb else
                pl.BlockSpec((tk, tn), lambda j, i, k: (k, j))]
    args = [a, b]
    if side is not None:
        in_specs.append(tile)
        args.append(side)
    odt = BF16 if mode in ("relu2", "mul2") else out_dtype
    scratch = [pltpu.VMEM((tm, tn), F32)] if nk > 1 else []
    if comm:
        scratch += [pltpu.SemaphoreType.DMA((comm.n_sems,))] * 2
        params = pltpu.CompilerParams(dimension_semantics=("arbitrary",) * 3, vmem_limit_bytes=VMEM_LIMIT,
                                      has_side_effects=True)
    else:
        params = _params(("parallel", "parallel", "arbitrary"))
    outs = pl.pallas_call(
        body, name=name, grid=(nj, ni, nk),
        in_specs=in_specs + [ANY] * n_ci,
        out_specs=[tile] * n_out + [ANY] * n_co,
        out_shape=[jax.ShapeDtypeStruct((M, N), odt)] * n_out + list(comm.outs if comm else []),
        scratch_shapes=scratch,
        compiler_params=params,
    )(*args, *(comm.ins if comm else []))
    res = outs[:n_out] if n_out > 1 else outs[0]
    return (res, outs[n_out:]) if comm else res


class _Epi:
    def __init__(self, fn, row_ins=(), full_ins=(), row_outs=(), acc_outs=(), tiled=False):
        self.fn, self.row_ins, self.full_ins = fn, list(row_ins), list(full_ins)
        self.row_outs, self.acc_outs, self.tiled = list(row_outs), list(acc_outs), tiled


def _acc_into(ref, val, first):
    @pl.when(first)
    def _():
        ref[...] = val

    @pl.when(jnp.logical_not(first))
    def _():
        ref[...] += val


def _mm_epi(a, b, epi, name, tb=False, comm=None, tm=512, tn=None):
    M, K = a.shape
    N = b.shape[0] if tb else b.shape[1]
    tn = tn or N
    assert epi.tiled or tn == N
    tk = K if K <= 2304 else _pick(K, (2048, 1024))
    nk = K // tk
    nj, ni = N // tn, M // tm
    n_ri, n_fi, n_ro, n_ao = len(epi.row_ins), len(epi.full_ins), len(epi.row_outs), len(epi.acc_outs)
    n_ci = len(comm.ins) if comm else 0
    n_co = len(comm.outs) if comm else 0
    o0 = 2 + n_ci + n_ri + n_fi

    def body(*refs):
        a_ref, b_ref = refs[0], refs[1]
        ri = refs[2 + n_ci:2 + n_ci + n_ri]
        fi = refs[2 + n_ci + n_ri:o0]
        ro = refs[o0 + n_co:o0 + n_co + n_ro]
        ao = refs[o0 + n_co + n_ro:o0 + n_co + n_ro + n_ao]
        pj, pi, pk = pl.program_id(0), pl.program_id(1), pl.program_id(2)
        if comm:
            c_args = (refs[2:2 + n_ci], refs[o0:o0 + n_co], refs[-2], refs[-1])

            @pl.when(jnp.logical_and(jnp.logical_and(pj == 0, pi == 0), pk == 0))
            def _():
                comm.start(*c_args)

        part = (_dot_nt if tb else _dot)(_b(a_ref[...]), _b(b_ref[...]))
        if nk == 1:
            epi.fn(part, ri, fi, ro, ao, pi == 0)
        else:
            acc_ref = refs[o0 + n_co + n_ro + n_ao]

            @pl.when(pk == 0)
            def _():
                acc_ref[...] = part

            @pl.when(jnp.logical_and(pk > 0, pk < nk - 1))
            def _():
                acc_ref[...] += part

            @pl.when(pk == nk - 1)
            def _():
                epi.fn(acc_ref[...] + part, ri, fi, ro, ao, pi == 0)

        if comm:
            @pl.when(jnp.logical_and(jnp.logical_and(pj == nj - 1, pi == ni - 1), pk == nk - 1))
            def _():
                comm.finish(*c_args)

    def row_spec(width, cb):
        if epi.tiled:
            return pl.BlockSpec((tm, tn), lambda j, i, k: (i, j + cb))
        return pl.BlockSpec((tm, width), lambda j, i, k: (i, cb))

    in_specs = [pl.BlockSpec((tm, tk), lambda j, i, k: (i, k)),
                pl.BlockSpec((tn, tk), lambda j, i, k: (j, k)) if tb else
                pl.BlockSpec((tk, tn), lambda j, i, k: (k, j))]
    in_specs += [ANY] * n_ci
    in_specs += [row_spec(w, cb) for (_, w, cb) in epi.row_ins]
    in_specs += [pl.BlockSpec((1, tn), lambda j, i, k: (0, j)) if epi.tiled else
                 pl.BlockSpec(f.shape, lambda j, i, k: (0, 0)) for f in epi.full_ins]
    out_specs = [ANY] * n_co + [row_spec(c, 0) for c, _ in epi.row_outs]
    out_specs += [pl.BlockSpec((1, tn), lambda j, i, k: (0, j)) if epi.tiled else
                  pl.BlockSpec((1, c), lambda j, i, k: (0, 0)) for c in epi.acc_outs]
    out_shape = list(comm.outs if comm else [])
    out_shape += [jax.ShapeDtypeStruct((M, c), dt_) for c, dt_ in epi.row_outs]
    out_shape += [jax.ShapeDtypeStruct((1, c), F32) for c in epi.acc_outs]
    scratch = [pltpu.VMEM((tm, tn), F32)] if nk > 1 else []
    if comm:
        scratch += [pltpu.SemaphoreType.DMA((comm.n_sems,))] * 2
    params = pltpu.CompilerParams(dimension_semantics=("arbitrary",) * 3, vmem_limit_bytes=VMEM_LIMIT,
                                  has_side_effects=comm is not None)
    outs = pl.pallas_call(
        body, name=name, grid=(nj, ni, nk), in_specs=in_specs, out_specs=out_specs, out_shape=out_shape,
        scratch_shapes=scratch, compiler_params=params,
    )(a, b, *(comm.ins if comm else []), *[arr for arr, _, _ in epi.row_ins], *epi.full_ins)
    return outs[n_co:], (outs[:n_co] if comm else None)


def _mm_tn(a, b, name):
    S, Ka = a.shape
    _, N = b.shape
    tka = _pick(Ka, (1024, 768, 512, 256, 128))
    tn = _pick(N, (1024, 768, 512, 256, 128))
    ts = 1024 if S % 1024 == 0 else 512
    ns = S // ts

    def body(a_ref, b_ref, o_ref, acc_ref):
        s = pl.program_id(2)
        part = _dot_tn(_b(a_ref[...]), _b(b_ref[...]))

        @pl.when(s == 0)
        def _():
            acc_ref[...] = part

        @pl.when(s > 0)
        def _():
            acc_ref[...] += part

        @pl.when(s == ns - 1)
        def _():
            o_ref[...] = acc_ref[...]

    return pl.pallas_call(
        body, name=name, grid=(Ka // tka, N // tn, ns),
        in_specs=[pl.BlockSpec((ts, tka), lambda i, j, s: (s, i)),
                  pl.BlockSpec((ts, tn), lambda i, j, s: (s, j))],
        out_specs=pl.BlockSpec((tka, tn), lambda i, j, s: (i, j)),
        out_shape=jax.ShapeDtypeStruct((Ka, N), F32),
        scratch_shapes=[pltpu.VMEM((tka, tn), F32)],
        compiler_params=_params(("parallel", "parallel", "arbitrary")),
    )(a, b)


def _row_call(body, row_ins, full_ins, row_outs, acc_outs, bs, name):
    S = row_ins[0].shape[0]
    assert S % bs == 0
    in_specs = [pl.BlockSpec((bs, a.shape[1]), lambda i: (i, 0)) for a in row_ins]
    in_specs += [pl.BlockSpec(a.shape, lambda i: (0, 0)) for a in full_ins]
    out_specs = [pl.BlockSpec((bs, c), lambda i: (i, 0)) for c, _ in row_outs]
    out_specs += [pl.BlockSpec(s, lambda i: (0, 0)) for s in acc_outs]
    out_shape = [jax.ShapeDtypeStruct((S, c), dt) for c, dt in row_outs]
    out_shape += [jax.ShapeDtypeStruct(s, F32) for s in acc_outs]
    return pl.pallas_call(
        body, name=name, grid=(S // bs,), in_specs=in_specs, out_specs=out_specs, out_shape=out_shape,
        compiler_params=_params(("arbitrary",)),
    )(*row_ins, *full_ins)


def _rms_vals(x, w):
    r = lax.rsqrt(jnp.mean(x * x, axis=-1, keepdims=True) + RMS_EPS)
    return x * r * w


def _rms_bwd_vals(x, w, dy):
    r = lax.rsqrt(jnp.mean(x * x, axis=-1, keepdims=True) + RMS_EPS)
    xn = x * r
    g = dy * w
    dx = r * (g - xn * jnp.mean(g * xn, axis=-1, keepdims=True))
    dw = jnp.sum(dy * xn, axis=0, keepdims=True)
    return dx, dw


def _acc_add(ref, val):
    @pl.when(pl.program_id(0) == 0)
    def _():
        ref[...] = val

    @pl.when(pl.program_id(0) > 0)
    def _():
        ref[...] += val


def _rms_fwd(x, w):
    def body(x_ref, w_ref, o_ref):
        o_ref[...] = _b(_rms_vals(x_ref[...], w_ref[...]))
    return _row_call(body, [x], [w], [(x.shape[1], BF16)], [], 512, "rms_fwd")[0]


def _gate_fwd(att_o, ssm_o, gl, b_gate):
    def body(a_ref, s_ref, g_ref, b_ref, o_ref):
        g = _sigmoid(g_ref[...] + b_ref[...])
        o_ref[...] = _b(g[:, :D_MODEL] * a_ref[...] + g[:, D_MODEL:] * s_ref[...])
    return _row_call(body, [att_o, ssm_o, gl], [b_gate], [(D_MODEL, BF16)], [], 512, "gate_fwd")[0]


def _post_pre(x, mixed, w_post, w_pre):
    def body(x_ref, m_ref, wp_ref, wn_ref, h_ref, f_ref):
        h = x_ref[...] + _rms_vals(m_ref[...], wp_ref[...])
        h_ref[...] = h
        f_ref[...] = _b(_rms_vals(h, wn_ref[...]))
    return _row_call(body, [x, mixed], [w_post, w_pre], [(D_MODEL, F32), (D_MODEL, BF16)], [], 512,
                     "post_pre")


def _relu2(up):
    def body(u_ref, o_ref):
        r = jnp.maximum(u_ref[...], 0.0)
        o_ref[...] = _b(r * r)
    return _row_call(body, [up], [], [(up.shape[1], BF16)], [], 256, "relu2")[0]


def _final(h1, down, w_post, target):
    def body(h_ref, d_ref, t_ref, w_ref, dh_ref, dd_ref, loss_ref, dw_ref):
        dn = d_ref[...]
        w = w_ref[...]
        err = h_ref[...] + _rms_vals(dn, w) - t_ref[...]
        row = jnp.mean(err * err, axis=-1, keepdims=True)
        part = 0.5 * jnp.sum(row, axis=0, keepdims=True)
        dh = err * (1.0 / D_MODEL)
        dh_ref[...] = dh
        dx, dw = _rms_bwd_vals(dn, w, dh)
        dd_ref[...] = _b(dx)
        _acc_add(loss_ref, jnp.broadcast_to(part, (1, LANES)))
        _acc_add(dw_ref, dw)
    return _row_call(body, [h1, down, target], [w_post], [(D_MODEL, F32), (D_MODEL, BF16)],
                     [(1, LANES), (1, D_MODEL)], 512, "final_loss")


def _dup(da, up):
    def body(a_ref, u_ref, o_ref):
        o_ref[...] = _b(a_ref[...] * (2.0 * jnp.maximum(u_ref[...], 0.0)))
    return _row_call(body, [da, up], [], [(up.shape[1], BF16)], [], 256, "relu2_bwd")[0]


def _mid_bwd(dh2, df, h1, mixed, w_pre, w_post):
    def body(dh_ref, df_ref, h_ref, m_ref, wn_ref, wp_ref, dh1_ref, dm_ref, dwn_ref, dwp_ref):
        dx, dwn = _rms_bwd_vals(h_ref[...], wn_ref[...], df_ref[...])
        dh1 = dh_ref[...] + dx
        dh1_ref[...] = dh1
        dm, dwp = _rms_bwd_vals(m_ref[...], wp_ref[...], dh1)
        dm_ref[...] = _b(dm)
        _acc_add(dwn_ref, dwn)
        _acc_add(dwp_ref, dwp)
    return _row_call(body, [dh2, df, h1, mixed], [w_pre, w_post], [(D_MODEL, F32), (D_MODEL, BF16)],
                     [(1, D_MODEL), (1, D_MODEL)], 512, "mid_bwd")


def _gate_bwd(dmi, att_o, ssm_o, gl, b_gate):
    def body(d_ref, a_ref, s_ref, g_ref, b_ref, da_ref, ds_ref, dg_ref, db_ref):
        g = _sigmoid(g_ref[...] + b_ref[...])
        d = d_ref[...]
        ga, gs = g[:, :D_MODEL], g[:, D_MODEL:]
        da_ref[...] = _b(ga * d)
        ds_ref[...] = _b(gs * d)
        dga = d * a_ref[...] * ga * (1.0 - ga)
        dgs = d * s_ref[...] * gs * (1.0 - gs)
        dg_ref[:, :D_MODEL] = _b(dga)
        dg_ref[:, D_MODEL:] = _b(dgs)
        _acc_add(db_ref.at[:, pl.ds(0, D_MODEL)], jnp.sum(dga, axis=0, keepdims=True))
        _acc_add(db_ref.at[:, pl.ds(D_MODEL, D_MODEL)], jnp.sum(dgs, axis=0, keepdims=True))
    return _row_call(body, [dmi, att_o, ssm_o, gl], [b_gate],
                     [(D_MODEL, BF16), (D_MODEL, BF16), (2 * D_MODEL, BF16)], [(1, 2 * D_MODEL)], 256,
                     "gate_bwd")


def _first_bwd(dh1, du, x, w_pre):
    def body(dh_ref, du_ref, x_ref, w_ref, dx_ref, dw_ref):
        dx, dw = _rms_bwd_vals(x_ref[...], w_ref[...], du_ref[...])
        dx_ref[...] = dh_ref[...] + dx
        _acc_add(dw_ref, dw)
    return _row_call(body, [dh1, du, x], [w_pre], [(D_MODEL, F32)], [(1, D_MODEL)], 512, "first_bwd")


def _group_rms(t):
    gw = SSM_INNER // SSM_GROUPS
    out = []
    for g in range(SSM_GROUPS):
        tg = t[:, g * gw:(g + 1) * gw]
        out.append(lax.rsqrt(jnp.mean(tg * tg, axis=-1, keepdims=True) + RMS_EPS))
    return out


def _gnorm_fwd(y, z, w):
    gw = SSM_INNER // SSM_GROUPS

    def body(y_ref, z_ref, w_ref, o_ref):
        zz = z_ref[...]
        t = y_ref[...] * (zz * _sigmoid(zz))
        rs = _group_rms(t)
        for g in range(SSM_GROUPS):
            sl = slice(g * gw, (g + 1) * gw)
            o_ref[:, sl] = _b(t[:, sl] * rs[g] * w_ref[:, sl])
    return _row_call(body, [y, z], [w], [(SSM_INNER, BF16)], [], 256, "gnorm_fwd")[0]


def _gnorm_bwd(dout, y, z, w):
    gw = SSM_INNER // SSM_GROUPS

    def body(d_ref, y_ref, z_ref, w_ref, dy_ref, dz_ref, dw_ref):
        zz = z_ref[...]
        yy = y_ref[...]
        sg = _sigmoid(zz)
        sz = zz * sg
        t = yy * sz
        rs = _group_rms(t)
        for g in range(SSM_GROUPS):
            sl = slice(g * gw, (g + 1) * gw)
            tn = t[:, sl] * rs[g]
            d = d_ref[:, sl]
            gg = d * w_ref[:, sl]
            dt = rs[g] * (gg - tn * jnp.mean(gg * tn, axis=-1, keepdims=True))
            dy_ref[:, sl] = dt * sz[:, sl]
            dz_ref[:, sl] = _b(dt * yy[:, sl] * (sg[:, sl] * (1.0 + zz[:, sl] * (1.0 - sg[:, sl]))))
            _acc_add(dw_ref.at[:, pl.ds(g * gw, gw)], jnp.sum(d * tn, axis=0, keepdims=True))
    return _row_call(body, [dout, y, z], [w], [(SSM_INNER, F32), (SSM_INNER, BF16)], [(1, SSM_INNER)], 256,
                     "gnorm_bwd")


def _final_epi(h1, target, w_post):
    def fn(dn, ri, fi, ro, ao, first):
        w = fi[0][...]
        err = ri[0][...] + _rms_vals(dn, w) - ri[1][...]
        row = jnp.mean(err * err, axis=-1, keepdims=True)
        part = 0.5 * jnp.sum(row, axis=0, keepdims=True)
        dh = err * (1.0 / D_MODEL)
        ro[0][...] = dh
        dx, dw = _rms_bwd_vals(dn, w, dh)
        ro[1][...] = _b(dx)
        _acc_into(ao[0], jnp.broadcast_to(part, (1, LANES)), first)
        _acc_into(ao[1], dw, first)
    return _Epi(fn, [(h1, D_MODEL, 0), (target, D_MODEL, 0)], [w_post], [(D_MODEL, F32), (D_MODEL, BF16)],
                [LANES, D_MODEL])


def _mid_epi(dh2, h1, mixed, w_pre, w_post):
    def fn(df, ri, fi, ro, ao, first):
        dx, dwn = _rms_bwd_vals(ri[1][...], fi[0][...], df)
        dh1 = ri[0][...] + dx
        ro[0][...] = dh1
        dm, dwp = _rms_bwd_vals(ri[2][...], fi[1][...], dh1)
        ro[1][...] = _b(dm)
        _acc_into(ao[0], dwn, first)
        _acc_into(ao[1], dwp, first)
    return _Epi(fn, [(dh2, D_MODEL, 0), (h1, D_MODEL, 0), (mixed, D_MODEL, 0)], [w_pre, w_post],
                [(D_MODEL, F32), (D_MODEL, BF16)], [D_MODEL, D_MODEL])


def _gate_epi(att_o, ssm_o, gl, b_gate):
    def fn(d, ri, fi, ro, ao, first):
        g = _sigmoid(ri[2][...] + fi[0][...])
        ga, gs = g[:, :D_MODEL], g[:, D_MODEL:]
        ro[0][...] = _b(ga * d)
        ro[1][...] = _b(gs * d)
        dga = d * ri[0][...] * ga * (1.0 - ga)
        dgs = d * ri[1][...] * gs * (1.0 - gs)
        ro[2][:, :D_MODEL] = _b(dga)
        ro[2][:, D_MODEL:] = _b(dgs)
        _acc_into(ao[0].at[:, pl.ds(0, D_MODEL)], jnp.sum(dga, axis=0, keepdims=True), first)
        _acc_into(ao[0].at[:, pl.ds(D_MODEL, D_MODEL)], jnp.sum(dgs, axis=0, keepdims=True), first)
    return _Epi(fn, [(att_o, D_MODEL, 0), (ssm_o, D_MODEL, 0), (gl, 2 * D_MODEL, 0)], [b_gate],
                [(D_MODEL, BF16), (D_MODEL, BF16), (2 * D_MODEL, BF16)], [2 * D_MODEL])


def _first_epi(du, dh1, x, w_pre):
    def fn(r, ri, fi, ro, ao, first):
        dx, dw = _rms_bwd_vals(ri[2][...], fi[0][...], ri[0][...] + r)
        ro[0][...] = ri[1][...] + dx
        _acc_into(ao[0], dw, first)
    return _Epi(fn, [(du, D_MODEL, 0), (dh1, D_MODEL, 0), (x, D_MODEL, 0)], [w_pre], [(D_MODEL, F32)],
                [D_MODEL])


def _gnorm_epi(y, z, w):
    gw = SSM_INNER // SSM_GROUPS

    def fn(d_all, ri, fi, ro, ao, first):
        zz = ri[1][...]
        yy = ri[0][...]
        sg = _sigmoid(zz)
        sz = zz * sg
        t = yy * sz
        dws = []
        for g in range(d_all.shape[1] // gw):
            sl = slice(g * gw, (g + 1) * gw)
            tg = t[:, sl]
            r = lax.rsqrt(jnp.mean(tg * tg, axis=-1, keepdims=True) + RMS_EPS)
            tn = tg * r
            d = d_all[:, sl]
            gg = d * fi[0][:, sl]
            dt = r * (gg - tn * jnp.mean(gg * tn, axis=-1, keepdims=True))
            ro[0][:, sl] = dt * sz[:, sl]
            ro[1][:, sl] = _b(dt * yy[:, sl] * (sg[:, sl] * (1.0 + zz[:, sl] * (1.0 - sg[:, sl]))))
            dws.append(jnp.sum(d * tn, axis=0, keepdims=True))
        _acc_into(ao[0], jnp.concatenate(dws, axis=1), first)
    return _Epi(fn, [(y, SSM_INNER, 0), (z, SSM_INNER, 0)], [w], [(SSM_INNER, F32), (SSM_INNER, BF16)],
                [SSM_INNER], tiled=True)


def _to_pat(a, d):
    if d == 1:
        return a
    S, C = a.shape
    return a.reshape(S // d, d, C).transpose(1, 0, 2).reshape(S, C)


def _from_pat(a, d):
    if d == 1:
        return a
    S, C = a.shape
    return a.reshape(d, S // d, C).transpose(1, 0, 2).reshape(S, C)


def _head_col(stat, h):
    return stat[:, h:h + 1]


def _attn_fwd(q, k, v, d):
    S = q.shape[0]
    blk = ATT_BLOCK
    nblk = S // blk
    nbs = nblk // d
    slopes = _alibi_slopes(N_ATT_HEADS)
    scale = HEAD_DIM ** -0.5

    def body(q_ref, kc_ref, kp_ref, vc_ref, vp_ref, o_ref, m_ref, l_ref):
        n = pl.program_id(0)
        has_prev = (n % nbs) != 0
        ii = lax.broadcasted_iota(jnp.int32, (blk, blk), 0)
        jj = lax.broadcasted_iota(jnp.int32, (blk, blk), 1)
        dist_c = (ii - jj).astype(F32)
        dist_p = dist_c + float(blk)
        ok_c = ii >= jj
        ok_p = jnp.logical_and(jj >= ii, has_prev)
        lane = lax.broadcasted_iota(jnp.int32, (blk, LANES), 1)
        m_all = jnp.zeros((blk, LANES), F32)
        l_all = jnp.zeros((blk, LANES), F32)
        for h in range(N_ATT_HEADS):
            sl = slice(h * HEAD_DIM, (h + 1) * HEAD_DIM)
            qh = q_ref[:, sl]
            bias = slopes[h] * float(d)
            sc = jnp.where(ok_c, _dot_nt(qh, kc_ref[:, sl]) * scale - bias * dist_c, NEG_BIG)
            sp = jnp.where(ok_p, _dot_nt(qh, kp_ref[:, sl]) * scale - bias * dist_p, NEG_BIG)
            m = jnp.maximum(jnp.max(sc, axis=-1, keepdims=True), jnp.max(sp, axis=-1, keepdims=True))
            pc = jnp.exp(sc - m)
            pp = jnp.exp(sp - m)
            l = jnp.sum(pc, axis=-1, keepdims=True) + jnp.sum(pp, axis=-1, keepdims=True)
            o_ref[:, sl] = _dot(_b(pc), vc_ref[:, sl]) + _dot(_b(pp), vp_ref[:, sl])
            m_all = jnp.where(lane == h, m, m_all)
            l_all = jnp.where(lane == h, l, l_all)
        m_ref[...] = m_all
        l_ref[...] = l_all

    cur = pl.BlockSpec((blk, ATT_WIDTH), lambda n: (n, 0))
    prev = pl.BlockSpec((blk, ATT_WIDTH), lambda n: (jnp.maximum(n - 1, 0), 0))
    stat = pl.BlockSpec((blk, LANES), lambda n: (n, 0))
    return pl.pallas_call(
        body, name=f"attn_fwd_d{d}", grid=(nblk,),
        in_specs=[cur, cur, prev, cur, prev],
        out_specs=[cur, stat, stat],
        out_shape=[jax.ShapeDtypeStruct((S, ATT_WIDTH), F32), jax.ShapeDtypeStruct((S, LANES), F32),
                   jax.ShapeDtypeStruct((S, LANES), F32)],
        compiler_params=_params(("parallel",)),
    )(q, k, k, v, v)


def _attn_combine(os, ms, ls):
    def body(o1, o2, o3, m1, m2, m3, l1, l2, l3, att_ref, lse_ref):
        mm = [m1[...], m2[...], m3[...]]
        big = jnp.maximum(jnp.maximum(mm[0], mm[1]), mm[2])
        es = [jnp.exp(m - big) for m in mm]
        den = es[0] * l1[...] + es[1] * l2[...] + es[2] * l3[...]
        lse_ref[...] = big + jnp.log(den)
        inv = 1.0 / den
        for h in range(N_ATT_HEADS):
            sl = slice(h * HEAD_DIM, (h + 1) * HEAD_DIM)
            num = (_head_col(es[0], h) * o1[:, sl] + _head_col(es[1], h) * o2[:, sl]
                   + _head_col(es[2], h) * o3[:, sl])
            att_ref[:, sl] = num * _head_col(inv, h)
    return _row_call(body, list(os) + list(ms) + list(ls), [], [(ATT_WIDTH, F32), (LANES, F32)], [], 256,
                     "attn_combine")


def _attn_delta(d_att, att):
    def body(d_ref, a_ref, dl_ref, db_ref):
        dd = d_ref[...]
        prod = dd * a_ref[...]
        lane = lax.broadcasted_iota(jnp.int32, (dd.shape[0], LANES), 1)
        acc = jnp.zeros((dd.shape[0], LANES), F32)
        for h in range(N_ATT_HEADS):
            s = jnp.sum(prod[:, h * HEAD_DIM:(h + 1) * HEAD_DIM], axis=-1, keepdims=True)
            acc = jnp.where(lane == h, s, acc)
        dl_ref[...] = acc
        db_ref[...] = _b(dd)
    return _row_call(body, [d_att, att], [], [(LANES, F32), (ATT_WIDTH, BF16)], [], 512, "attn_delta")


def _attn_bwd(q, k, v, do, lse, delta, d):
    S = q.shape[0]
    blk = ATT_BLOCK
    nblk = S // blk
    nbs = nblk // d
    slopes = _alibi_slopes(N_ATT_HEADS)
    scale = HEAD_DIM ** -0.5

    def body(qc_ref, qn_ref, k_ref, v_ref, doc_ref, don_ref, lc_ref, ln_ref, dc_ref, dn_ref,
             dq_ref, dk_ref, dv_ref, carry_ref):
        n = pl.program_id(0)
        has_next = ((n + 1) % nbs) != 0

        @pl.when(n == 0)
        def _():
            carry_ref[...] = jnp.zeros_like(carry_ref)

        ii = lax.broadcasted_iota(jnp.int32, (blk, blk), 0)
        jj = lax.broadcasted_iota(jnp.int32, (blk, blk), 1)
        dist_c = (ii - jj).astype(F32)
        dist_p = dist_c + float(blk)
        ok_c = ii >= jj
        ok_p = jnp.logical_and(jj >= ii, has_next)
        for h in range(N_ATT_HEADS):
            sl = slice(h * HEAD_DIM, (h + 1) * HEAD_DIM)
            bias = slopes[h] * float(d)
            kh = k_ref[:, sl]
            vh = v_ref[:, sl]
            qh = qc_ref[:, sl]
            doh = doc_ref[:, sl]
            s = jnp.where(ok_c, _dot_nt(qh, kh) * scale - bias * dist_c - _head_col(lc_ref[...], h), NEG_BIG)
            p = jnp.exp(s)
            ds = p * (_dot_nt(doh, vh) - _head_col(dc_ref[...], h)) * scale
            pb, dsb = _b(p), _b(ds)
            dv = _dot_tn(pb, doh)
            dk = _dot_tn(dsb, qh)
            dq_ref[:, sl] = _dot(dsb, kh) + carry_ref[:, sl]
            qh = qn_ref[:, sl]
            doh = don_ref[:, sl]
            s = jnp.where(ok_p, _dot_nt(qh, kh) * scale - bias * dist_p - _head_col(ln_ref[...], h), NEG_BIG)
            p = jnp.exp(s)
            ds = p * (_dot_nt(doh, vh) - _head_col(dn_ref[...], h)) * scale
            pb, dsb = _b(p), _b(ds)
            dv_ref[:, sl] = dv + _dot_tn(pb, doh)
            dk_ref[:, sl] = dk + _dot_tn(dsb, qh)
            carry_ref[:, sl] = _dot(dsb, kh)

    cur = pl.BlockSpec((blk, ATT_WIDTH), lambda n: (n, 0))
    nxt = pl.BlockSpec((blk, ATT_WIDTH), lambda n: (jnp.minimum(n + 1, nblk - 1), 0))
    scur = pl.BlockSpec((blk, LANES), lambda n: (n, 0))
    snxt = pl.BlockSpec((blk, LANES), lambda n: (jnp.minimum(n + 1, nblk - 1), 0))
    shp = jax.ShapeDtypeStruct((S, ATT_WIDTH), F32)
    return pl.pallas_call(
        body, name=f"attn_bwd_d{d}", grid=(nblk,),
        in_specs=[cur, nxt, cur, cur, cur, nxt, scur, snxt, scur, snxt],
        out_specs=[cur, cur, cur],
        out_shape=[shp, shp, shp],
        scratch_shapes=[pltpu.VMEM((blk, ATT_WIDTH), F32)],
        compiler_params=_params(("arbitrary",)),
    )(q, q, k, v, do, do, lse, lse, delta, delta)


def _head_pair_masks(x):
    lane = lax.broadcasted_iota(jnp.int32, x.shape, 1)
    zero = jnp.zeros_like(x)
    return jnp.where(lane < HEAD_DIM, x, zero), jnp.where(lane >= HEAD_DIM, x, zero)


ATT_QUERY_ROWS = 32


def _attn_fwd2(qkv, d, comm=None):
    S = qkv.shape[0]
    blk = ATT_BLOCK
    nblk = S // blk
    nbs = nblk // d
    slopes = _alibi_slopes(N_ATT_HEADS)
    scale = HEAD_DIM ** -0.5
    n_ci = len(comm.ins) if comm else 0
    n_co = len(comm.outs) if comm else 0

    def body(*refs):
        q_ref, kc_ref, kp_ref, vc_ref, vp_ref = refs[:5]
        o_ref, m_ref, l_ref = refs[5 + n_ci:8 + n_ci]
        n = pl.program_id(0)
        if comm:
            c_args = (refs[5:5 + n_ci], refs[8 + n_ci:8 + n_ci + n_co], refs[-2], refs[-1])

            @pl.when(n == 0)
            def _():
                comm.start(*c_args)

        has_prev = (n % nbs) != 0
        ii = lax.broadcasted_iota(jnp.int32, (blk, 2 * blk), 0)
        jj = lax.broadcasted_iota(jnp.int32, (blk, 2 * blk), 1)
        dist_i = blk + ii - jj
        dist = dist_i.astype(F32)
        ok = jnp.logical_and(jnp.logical_and(dist_i >= 0, dist_i <= blk), jnp.logical_or(jj >= blk, has_prev))
        s_scr, p_scr = refs[8 + n_ci + n_co], refs[9 + n_ci + n_co]
        lane = lax.broadcasted_iota(jnp.int32, (blk, LANES), 1)
        for pr in range(N_ATT_HEADS // 2):
            sl = slice(pr * LANES, (pr + 1) * LANES)
            kcat = jnp.concatenate([kp_ref[:, sl], kc_ref[:, sl]], axis=0)
            for h, qh in zip((2 * pr, 2 * pr + 1), _head_pair_masks(q_ref[:, sl])):
                s_scr[h] = _dot_nt(qh, kcat)
        m_all = jnp.zeros((blk, LANES), F32)
        l_all = jnp.zeros((blk, LANES), F32)
        for h in range(N_ATT_HEADS):
            s = jnp.where(ok, s_scr[h] * scale - (slopes[h] * float(d)) * dist, NEG_BIG)
            m = jnp.max(s, axis=-1, keepdims=True)
            p = jnp.exp(s - m)
            l = jnp.sum(p, axis=-1, keepdims=True)
            m_all = jnp.where(lane == h, m, m_all)
            l_all = jnp.where(lane == h, l, l_all)
            p_scr[:, h * 2 * blk:(h + 1) * 2 * blk] = _b(p)
        for pr in range(N_ATT_HEADS // 2):
            sl = slice(pr * LANES, (pr + 1) * LANES)
            vmask = jnp.concatenate(
                _head_pair_masks(jnp.concatenate([vp_ref[:, sl], vc_ref[:, sl]], axis=0)), axis=0)
            o_ref[:, sl] = _dot(p_scr[:, pr * 4 * blk:(pr + 1) * 4 * blk], vmask)
        m_ref[...] = m_all
        l_ref[...] = l_all
        if comm:
            @pl.when(n == nblk - 1)
            def _():
                comm.finish(*c_args)

    cur = lambda c: pl.BlockSpec((blk, ATT_WIDTH), lambda n: (n, c))
    prev = lambda c: pl.BlockSpec((blk, ATT_WIDTH), lambda n: (jnp.maximum(n - 1, 0), c))
    stat = pl.BlockSpec((blk, LANES), lambda n: (n, 0))
    scratch = [pltpu.VMEM((N_ATT_HEADS, blk, 2 * blk), F32), pltpu.VMEM((blk, N_ATT_HEADS * 2 * blk), BF16)]
    if comm:
        scratch += [pltpu.SemaphoreType.DMA((comm.n_sems,))] * 2
        params = pltpu.CompilerParams(dimension_semantics=("arbitrary",), vmem_limit_bytes=VMEM_LIMIT,
                                      has_side_effects=True)
    else:
        params = _params(("parallel",))
    outs = pl.pallas_call(
        body, name=f"attn_fwd_d{d}", grid=(nblk,),
        in_specs=[cur(0), cur(1), prev(1), cur(2), prev(2)] + [ANY] * n_ci,
        out_specs=[cur(0), stat, stat] + [ANY] * n_co,
        out_shape=[jax.ShapeDtypeStruct((S, ATT_WIDTH), F32), jax.ShapeDtypeStruct((S, LANES), F32),
                   jax.ShapeDtypeStruct((S, LANES), F32)] + list(comm.outs if comm else []),
        scratch_shapes=scratch,
        compiler_params=params,
    )(qkv, qkv, qkv, qkv, qkv, *(comm.ins if comm else []))
    return (outs[0], outs[1], outs[2], outs[3:]) if comm else outs


def _attn_bwd2(qkv, do, lse, delta, d, comm=None):
    S = qkv.shape[0]
    blk = ATT_BLOCK
    nblk = S // blk
    nbs = nblk // d
    slopes = _alibi_slopes(N_ATT_HEADS)
    scale = HEAD_DIM ** -0.5
    n_ci = len(comm.ins) if comm else 0
    n_co = len(comm.outs) if comm else 0

    def body(*refs):
        qc_ref, qn_ref, k_ref, v_ref, doc_ref, don_ref, lc_ref, ln_ref, dc_ref, dn_ref = refs[:10]
        dq_ref, dk_ref, dv_ref = refs[10 + n_ci:13 + n_ci]
        carry_ref = refs[13 + n_ci + n_co]
        n = pl.program_id(0)
        has_next = ((n + 1) % nbs) != 0
        if comm:
            c_args = (refs[10:10 + n_ci], refs[13 + n_ci:13 + n_ci + n_co], refs[-2], refs[-1])

        @pl.when(n == 0)
        def _():
            carry_ref[...] = jnp.zeros_like(carry_ref)
            if comm:
                comm.start(*c_args)

        rr = lax.broadcasted_iota(jnp.int32, (2 * blk, blk), 0)
        jj = lax.broadcasted_iota(jnp.int32, (2 * blk, blk), 1)
        dist_i = rr - jj
        dist = dist_i.astype(F32)
        ok = jnp.logical_or(jnp.logical_and(rr < blk, dist_i >= 0),
                            jnp.logical_and(jnp.logical_and(rr >= blk, dist_i <= blk), has_next))
        s_scr, dp_scr, p_rows, ds_rows, ds_cols = refs[14 + n_ci + n_co:19 + n_ci + n_co]
        lcat = jnp.concatenate([lc_ref[...], ln_ref[...]], axis=0)
        dcat = jnp.concatenate([dc_ref[...], dn_ref[...]], axis=0)
        rows2 = 2 * blk

        def operands(pr):
            sl = slice(pr * LANES, (pr + 1) * LANES)
            qm = _head_pair_masks(jnp.concatenate([qc_ref[:, sl], qn_ref[:, sl]], axis=0))
            dom = _head_pair_masks(jnp.concatenate([doc_ref[:, sl], don_ref[:, sl]], axis=0))
            return sl, qm, dom

        for pr in range(N_ATT_HEADS // 2):
            sl, qm, dom = operands(pr)
            for h, qh, doh in zip((2 * pr, 2 * pr + 1), qm, dom):
                s_scr[h] = _dot_nt(qh, k_ref[:, sl])
                dp_scr[h] = _dot_nt(doh, v_ref[:, sl])
        for h in range(N_ATT_HEADS):
            s = jnp.where(ok, s_scr[h] * scale - (slopes[h] * float(d)) * dist - lcat[:, h:h + 1], NEG_BIG)
            p = jnp.exp(s)
            dsb = _b(p * (dp_scr[h] - dcat[:, h:h + 1]) * scale)
            p_rows[h * rows2:(h + 1) * rows2, :] = _b(p)
            ds_rows[h * rows2:(h + 1) * rows2, :] = dsb
            ds_cols[:, h * blk:(h + 1) * blk] = dsb
        for pr in range(N_ATT_HEADS // 2):
            sl, qm, dom = operands(pr)
            pair_rows = slice(pr * 2 * rows2, (pr + 1) * 2 * rows2)
            dv_ref[:, sl] = _b(_dot_tn(p_rows[pair_rows, :], jnp.concatenate(dom, axis=0)))
            dk_ref[:, sl] = _b(_dot_tn(ds_rows[pair_rows, :], jnp.concatenate(qm, axis=0)))
            dq = _dot(ds_cols[:, pr * 2 * blk:(pr + 1) * 2 * blk],
                      jnp.concatenate(_head_pair_masks(k_ref[:, sl]), axis=0))
            dq_ref[:, sl] = _b(dq[:blk] + carry_ref[:, sl])
            carry_ref[:, sl] = dq[blk:]

        if comm:
            @pl.when(n == nblk - 1)
            def _():
                comm.finish(*c_args)

    cur = lambda c: pl.BlockSpec((blk, ATT_WIDTH), lambda n: (n, c))
    nxt = lambda c: pl.BlockSpec((blk, ATT_WIDTH), lambda n: (jnp.minimum(n + 1, nblk - 1), c))
    scur = pl.BlockSpec((blk, LANES), lambda n: (n, 0))
    snxt = pl.BlockSpec((blk, LANES), lambda n: (jnp.minimum(n + 1, nblk - 1), 0))
    shp = jax.ShapeDtypeStruct((S, ATT_WIDTH), BF16)
    scratch = [pltpu.VMEM((blk, ATT_WIDTH), F32),
               pltpu.VMEM((N_ATT_HEADS, 2 * blk, blk), F32), pltpu.VMEM((N_ATT_HEADS, 2 * blk, blk), F32),
               pltpu.VMEM((N_ATT_HEADS * 2 * blk, blk), BF16), pltpu.VMEM((N_ATT_HEADS * 2 * blk, blk), BF16),
               pltpu.VMEM((2 * blk, N_ATT_HEADS * blk), BF16)]
    if comm:
        scratch += [pltpu.SemaphoreType.DMA((comm.n_sems,))] * 2
        params = pltpu.CompilerParams(dimension_semantics=("arbitrary",), vmem_limit_bytes=VMEM_LIMIT,
                                      has_side_effects=True)
    else:
        params = _params(("arbitrary",))
    outs = pl.pallas_call(
        body, name=f"attn_bwd_d{d}", grid=(nblk,),
        in_specs=[cur(0), nxt(0), cur(1), cur(2), cur(0), nxt(0), scur, snxt, scur, snxt] + [ANY] * n_ci,
        out_specs=[cur(0), cur(0), cur(0)] + [ANY] * n_co,
        out_shape=[shp, shp, shp] + list(comm.outs if comm else []),
        scratch_shapes=scratch,
        compiler_params=params,
    )(qkv, qkv, qkv, qkv, do, do, lse, lse, delta, delta, *(comm.ins if comm else []))
    return (outs[0], outs[1], outs[2], outs[3:]) if comm else outs


LAYOUT_TILE = 512
DILATED = tuple(d for d in DILATIONS if d > 1)


def _pat_spec(d, cols, col_block=0):
    return pl.BlockSpec((d, LAYOUT_TILE // d, cols), lambda i: (0, i, col_block))


def _pat_view(a, d):
    return a.reshape(d, a.shape[0] // d, a.shape[1])


def _qkv_layouts(qkv):
    S, C = qkv.shape
    t = LAYOUT_TILE

    def body(x_ref, nat_ref, *refs):
        pat_refs, slab = refs[:-1], refs[-1]
        nat_ref[...] = _b(x_ref[...])
        _to_slabs(slab, x_ref)
        for d, p_ref in zip(DILATED, pat_refs):
            _gather_pattern(p_ref, slab, d, BF16)

    outs = pl.pallas_call(
        body, name="qkv_layouts", grid=(S // t,),
        in_specs=[pl.BlockSpec((t, C), lambda i: (i, 0))],
        out_specs=[pl.BlockSpec((t, C), lambda i: (i, 0))] + [_pat_spec(d, C) for d in DILATED],
        out_shape=[jax.ShapeDtypeStruct((S, C), BF16)]
        + [jax.ShapeDtypeStruct((d, S // d, C), BF16) for d in DILATED],
        scratch_shapes=[pltpu.VMEM((C // LANES, t, LANES), F32)],
        compiler_params=_params(("parallel",)),
    )(qkv)
    return [outs[0]] + [o.reshape(S, C) for o in outs[1:]]


def _to_slabs(slab_ref, src_ref):
    for cb in range(slab_ref.shape[0]):
        slab_ref[cb] = src_ref[:, cb * LANES:(cb + 1) * LANES].astype(F32)


def _gather_pattern(dst_ref, slab_ref, d, dtype):
    t = slab_ref.shape[1]
    for cb in range(slab_ref.shape[0]):
        one = slab_ref.at[cb]
        for r in range(d):
            dst_ref[r, :, cb * LANES:(cb + 1) * LANES] = one[pl.ds(r, t // d, stride=d), :].astype(dtype)


def _scatter_pattern(slab_ref, src_ref, d, add=False):
    t = slab_ref.shape[1]
    for cb in range(slab_ref.shape[0]):
        one = slab_ref.at[cb]
        for r in range(d):
            idx = pl.ds(r, t // d, stride=d)
            val = src_ref[r, :, cb * LANES:(cb + 1) * LANES]
            if add:
                val = val + one[idx, :]
            one[idx, :] = val


def _attn_combine2(os, ms, ls):
    S = os[0].shape[0]
    t = LAYOUT_TILE

    def body(o1, o2, o3, m1, m2, m3, l1, l2, l3, att_ref, lse_ref, so2, so3, sm2, sm3, sl2, sl3):
        for d, src, dst in ((DILATED[0], o2, so2), (DILATED[1], o3, so3), (DILATED[0], m2, sm2),
                            (DILATED[1], m3, sm3), (DILATED[0], l2, sl2), (DILATED[1], l3, sl3)):
            _scatter_pattern(dst, src, d)
        mm = [m1[...], sm2[0], sm3[0]]
        big = jnp.maximum(jnp.maximum(mm[0], mm[1]), mm[2])
        es = [jnp.exp(m - big) for m in mm]
        den = es[0] * l1[...] + es[1] * sl2[0] + es[2] * sl3[0]
        lse_ref[...] = big + jnp.log(den)
        inv = 1.0 / den
        for h in range(N_ATT_HEADS):
            sl = slice(h * HEAD_DIM, (h + 1) * HEAD_DIM)
            cb, hl = divmod(h, 2)
            sll = slice(hl * HEAD_DIM, (hl + 1) * HEAD_DIM)
            num = (_head_col(es[0], h) * o1[:, sl] + _head_col(es[1], h) * so2[cb, :, sll]
                   + _head_col(es[2], h) * so3[cb, :, sll])
            att_ref[:, sl] = num * _head_col(inv, h)

    def specs(c):
        return [pl.BlockSpec((t, c), lambda i: (i, 0))] + [_pat_spec(d, c) for d in DILATED]

    args = [os[0]] + [_pat_view(o, d) for o, d in zip(os[1:], DILATED)]
    args += [ms[0]] + [_pat_view(m, d) for m, d in zip(ms[1:], DILATED)]
    args += [ls[0]] + [_pat_view(l, d) for l, d in zip(ls[1:], DILATED)]
    return pl.pallas_call(
        body, name="attn_combine", grid=(S // t,),
        in_specs=specs(ATT_WIDTH) + specs(LANES) + specs(LANES),
        out_specs=[pl.BlockSpec((t, ATT_WIDTH), lambda i: (i, 0)), pl.BlockSpec((t, LANES), lambda i: (i, 0))],
        out_shape=[jax.ShapeDtypeStruct((S, ATT_WIDTH), F32), jax.ShapeDtypeStruct((S, LANES), F32)],
        scratch_shapes=[pltpu.VMEM((ATT_WIDTH // LANES, t, LANES), F32)] * 2
        + [pltpu.VMEM((1, t, LANES), F32)] * 4,
        compiler_params=_params(("parallel",)),
    )(*args)


def _attn_delta2(d_att, att, lse):
    S = d_att.shape[0]
    t = LAYOUT_TILE

    def body(d_ref, a_ref, l_ref, *refs):
        out_refs, d_slab, l_slab, dl_slab = refs[:-3], refs[-3], refs[-2], refs[-1]
        dd = d_ref[...]
        prod = dd * a_ref[...]
        lane = lax.broadcasted_iota(jnp.int32, (t, LANES), 1)
        acc = jnp.zeros((t, LANES), F32)
        for h in range(N_ATT_HEADS):
            s = jnp.sum(prod[:, h * HEAD_DIM:(h + 1) * HEAD_DIM], axis=-1, keepdims=True)
            acc = jnp.where(lane == h, s, acc)
        out_refs[0][...] = _b(dd)
        out_refs[1][...] = acc
        _to_slabs(d_slab, d_ref)
        l_slab[0] = l_ref[...]
        dl_slab[0] = acc
        for k, d in enumerate(DILATED):
            db_ref, ls_ref, dl_ref = out_refs[2 + 3 * k:5 + 3 * k]
            _gather_pattern(db_ref, d_slab, d, BF16)
            _gather_pattern(ls_ref, l_slab, d, F32)
            _gather_pattern(dl_ref, dl_slab, d, F32)

    nat = lambda c: pl.BlockSpec((t, c), lambda i: (i, 0))
    out_specs = [nat(ATT_WIDTH), nat(LANES)]
    out_shape = [jax.ShapeDtypeStruct((S, ATT_WIDTH), BF16), jax.ShapeDtypeStruct((S, LANES), F32)]
    for d in DILATED:
        out_specs += [_pat_spec(d, ATT_WIDTH), _pat_spec(d, LANES), _pat_spec(d, LANES)]
        out_shape += [jax.ShapeDtypeStruct((d, S // d, ATT_WIDTH), BF16),
                      jax.ShapeDtypeStruct((d, S // d, LANES), F32),
                      jax.ShapeDtypeStruct((d, S // d, LANES), F32)]
    outs = pl.pallas_call(
        body, name="attn_delta", grid=(S // t,),
        in_specs=[nat(ATT_WIDTH), nat(ATT_WIDTH), nat(LANES)],
        out_specs=out_specs, out_shape=out_shape,
        scratch_shapes=[pltpu.VMEM((ATT_WIDTH // LANES, t, LANES), F32), pltpu.VMEM((1, t, LANES), F32),
                        pltpu.VMEM((1, t, LANES), F32)],
        compiler_params=_params(("parallel",)),
    )(d_att, att, lse)
    res = [(outs[0], lse, outs[1])]
    for k in range(len(DILATED)):
        db, ls, dl = outs[2 + 3 * k:5 + 3 * k]
        res.append((db.reshape(S, ATT_WIDTH), ls.reshape(S, LANES), dl.reshape(S, LANES)))
    return res


def _sum_qkv2(dqs, dks, dvs):
    S = dqs[0].shape[0]
    t = LAYOUT_TILE

    def body(*refs):
        o_ref, scr = refs[-2], refs[-1]
        for part in range(3):
            nat_ref, p_refs = refs[3 * part], refs[3 * part + 1:3 * part + 3]
            _to_slabs(scr, nat_ref)
            for d, p_ref in zip(DILATED, p_refs):
                _scatter_pattern(scr, p_ref, d, add=True)
            for cb in range(ATT_WIDTH // LANES):
                o_ref[:, part * ATT_WIDTH + cb * LANES:part * ATT_WIDTH + (cb + 1) * LANES] = _b(scr[cb])

    in_specs, args = [], []
    for group in (dqs, dks, dvs):
        in_specs += [pl.BlockSpec((t, ATT_WIDTH), lambda i: (i, 0))] + [_pat_spec(d, ATT_WIDTH) for d in DILATED]
        args += [group[0]] + [_pat_view(a, d) for a, d in zip(group[1:], DILATED)]
    return pl.pallas_call(
        body, name="sum_dqkv", grid=(S // t,),
        in_specs=in_specs,
        out_specs=pl.BlockSpec((t, 3 * ATT_WIDTH), lambda i: (i, 0)),
        out_shape=jax.ShapeDtypeStruct((S, 3 * ATT_WIDTH), BF16),
        scratch_shapes=[pltpu.VMEM((ATT_WIDTH // LANES, t, LANES), F32)],
        compiler_params=_params(("parallel",)),
    )(*args)


def _sum_qkv(dqs, dks, dvs):
    def body(q1, q2, q3, k1, k2, k3, v1, v2, v3, o_ref):
        o_ref[:, 0:ATT_WIDTH] = _b(q1[...] + q2[...] + q3[...])
        o_ref[:, ATT_WIDTH:2 * ATT_WIDTH] = _b(k1[...] + k2[...] + k3[...])
        o_ref[:, 2 * ATT_WIDTH:] = _b(v1[...] + v2[...] + v3[...])
    return _row_call(body, list(dqs) + list(dks) + list(dvs), [], [(3 * ATT_WIDTH, BF16)], [], 256,
                     "sum_dqkv")[0]


CONV_COLS = 1024
CONV_ROWS = 512
HALO = 8


def _conv_fwd(xbc, conv_w, conv_b):
    S, C = xbc.shape
    bs, bc = CONV_ROWS, CONV_COLS
    nr = S // bs

    def body(x_ref, halo_ref, w_ref, b_ref, o_ref, xs_ref):
        r = pl.program_id(1)
        xs_ref[pl.ds(HALO, bs), :] = x_ref[...]
        xs_ref[pl.ds(0, HALO), :] = jnp.where(r > 0, halo_ref[...], 0.0)
        pre = b_ref[...] + w_ref[3:4, :] * x_ref[...]
        for j in range(SSM_CONV - 1):
            pre = pre + w_ref[j:j + 1, :] * xs_ref[pl.ds(HALO - 3 + j, bs), :]
        o_ref[...] = pre * _sigmoid(pre)

    return pl.pallas_call(
        body, name="conv_fwd", grid=(C // bc, nr),
        in_specs=[pl.BlockSpec((bs, bc), lambda c, r: (r, c)),
                  pl.BlockSpec((HALO, bc), lambda c, r: (jnp.maximum(r * (bs // HALO) - 1, 0), c)),
                  pl.BlockSpec((SSM_CONV, bc), lambda c, r: (0, c)),
                  pl.BlockSpec((1, bc), lambda c, r: (0, c))],
        out_specs=pl.BlockSpec((bs, bc), lambda c, r: (r, c)),
        out_shape=jax.ShapeDtypeStruct((S, C), F32),
        scratch_shapes=[pltpu.VMEM((bs + HALO, bc), F32)],
        compiler_params=_params(("parallel", "arbitrary")),
    )(xbc, xbc, conv_w, conv_b)


def _conv_bwd(xbc, dact, conv_w, conv_b, col0):
    S, C = xbc.shape
    Cp = dact.shape[1]
    bs, bc = CONV_ROWS, min(CONV_COLS, Cp)
    nr = S // bs
    cb0 = col0 // bc
    last_halo = S // HALO - 1

    def body(x_ref, xp_ref, xn_ref, d_ref, dn_ref, w_ref, b_ref, dx_ref, dw_ref, db_ref,
             xs_ref, dp_ref):
        r = pl.program_id(1)
        xs_ref[pl.ds(0, HALO), :] = jnp.where(r > 0, xp_ref[...], 0.0)
        xs_ref[pl.ds(HALO, bs), :] = x_ref[...]
        xs_ref[pl.ds(HALO + bs, HALO), :] = xn_ref[...]
        ext = bs + HALO
        pre = b_ref[...] + jnp.zeros((ext, bc), F32)
        for j in range(SSM_CONV):
            pre = pre + w_ref[j:j + 1, :] * xs_ref[pl.ds(HALO - 3 + j, ext), :]
        sg = _sigmoid(pre)
        dsilu = sg * (1.0 + pre * (1.0 - sg))
        dp_ref[pl.ds(0, bs), :] = d_ref[...] * dsilu[:bs]
        dp_ref[pl.ds(bs, HALO), :] = jnp.where(r < nr - 1, dn_ref[...], 0.0) * dsilu[bs:]
        dx = jnp.zeros((bs, bc), F32)
        for j in range(SSM_CONV):
            dx = dx + w_ref[j:j + 1, :] * dp_ref[pl.ds(3 - j, bs), :]
        dx_ref[...] = _b(dx)
        dpre = dp_ref[pl.ds(0, bs), :]
        for j in range(SSM_CONV):
            part = jnp.sum(dpre * xs_ref[pl.ds(HALO - 3 + j, bs), :], axis=0, keepdims=True)

            @pl.when(r == 0)
            def _():
                dw_ref[j:j + 1, :] = part

            @pl.when(r > 0)
            def _():
                dw_ref[j:j + 1, :] += part
        part = jnp.sum(dpre, axis=0, keepdims=True)

        @pl.when(r == 0)
        def _():
            db_ref[...] = part

        @pl.when(r > 0)
        def _():
            db_ref[...] += part

    hb = bs // HALO
    return pl.pallas_call(
        body, name=f"conv_bwd_{col0}", grid=(Cp // bc, nr),
        in_specs=[pl.BlockSpec((bs, bc), lambda c, r: (r, cb0 + c)),
                  pl.BlockSpec((HALO, bc), lambda c, r: (jnp.maximum(r * hb - 1, 0), cb0 + c)),
                  pl.BlockSpec((HALO, bc), lambda c, r: (jnp.minimum((r + 1) * hb, last_halo), cb0 + c)),
                  pl.BlockSpec((bs, bc), lambda c, r: (r, c)),
                  pl.BlockSpec((HALO, bc), lambda c, r: (jnp.minimum((r + 1) * hb, last_halo), c)),
                  pl.BlockSpec((SSM_CONV, bc), lambda c, r: (0, cb0 + c)),
                  pl.BlockSpec((1, bc), lambda c, r: (0, cb0 + c))],
        out_specs=[pl.BlockSpec((bs, bc), lambda c, r: (r, c)),
                   pl.BlockSpec((SSM_CONV, bc), lambda c, r: (0, c)),
                   pl.BlockSpec((1, bc), lambda c, r: (0, c))],
        out_shape=[jax.ShapeDtypeStruct((S, Cp), BF16), jax.ShapeDtypeStruct((SSM_CONV, Cp), F32),
                   jax.ShapeDtypeStruct((1, Cp), F32)],
        scratch_shapes=[pltpu.VMEM((bs + 2 * HALO, bc), F32), pltpu.VMEM((bs + HALO, bc), F32)],
        compiler_params=_params(("parallel", "arbitrary")),
    )(xbc, xbc, xbc, dact, dact, conv_w, conv_b)


def _shift_down(x, k, top_src):
    r8 = lax.broadcasted_iota(jnp.int32, (HALO, x.shape[1]), 0)
    rolled = pltpu.roll(x, k, 0)
    top = jnp.where(r8 < k, pltpu.roll(top_src, k, 0), rolled[0:HALO])
    if x.shape[0] == HALO:
        return top
    return jnp.concatenate([top, rolled[HALO:]], axis=0)


def _shift_up(x, k, bottom_src):
    n = x.shape[0]
    r8 = lax.broadcasted_iota(jnp.int32, (HALO, x.shape[1]), 0)
    rolled = pltpu.roll(x, n - k, 0)
    bottom = jnp.where(r8 >= HALO - k, pltpu.roll(bottom_src, HALO - k, 0), rolled[n - HALO:n])
    return jnp.concatenate([rolled[:n - HALO], bottom], axis=0)


def _conv_pre(x, top_src, w_ref, b_ref):
    shifted = [x] + [_shift_down(x, k, top_src) for k in range(1, SSM_CONV)]
    pre = b_ref[...] + w_ref[SSM_CONV - 1:SSM_CONV, :] * x
    for k in range(1, SSM_CONV):
        pre = pre + w_ref[SSM_CONV - 1 - k:SSM_CONV - k, :] * shifted[k]
    return pre, shifted


def _conv_fwd2(xbc, conv_w, conv_b):
    S, C = xbc.shape
    bs, bc = CONV_ROWS, CONV_COLS
    nr = S // bs

    def body(x_ref, halo_ref, w_ref, b_ref, o_ref):
        r = pl.program_id(1)
        halo = jnp.where(r > 0, halo_ref[...], 0.0)
        pre, _ = _conv_pre(x_ref[...], halo, w_ref, b_ref)
        o_ref[...] = pre * _sigmoid(pre)

    return pl.pallas_call(
        body, name="conv_fwd", grid=(C // bc, nr),
        in_specs=[pl.BlockSpec((bs, bc), lambda c, r: (r, c)),
                  pl.BlockSpec((HALO, bc), lambda c, r: (jnp.maximum(r * (bs // HALO) - 1, 0), c)),
                  pl.BlockSpec((SSM_CONV, bc), lambda c, r: (0, c)),
                  pl.BlockSpec((1, bc), lambda c, r: (0, c))],
        out_specs=pl.BlockSpec((bs, bc), lambda c, r: (r, c)),
        out_shape=jax.ShapeDtypeStruct((S, C), F32),
        compiler_params=_params(("parallel", "arbitrary")),
    )(xbc, xbc, conv_w, conv_b)


def _conv_bwd2(xbc, dact, conv_w, conv_b):
    S, C = xbc.shape
    bs, bc = CONV_ROWS, CONV_COLS
    nr = S // bs
    hb = bs // HALO
    last_halo = S // HALO - 1

    def dsilu(pre):
        sg = _sigmoid(pre)
        return sg * (1.0 + pre * (1.0 - sg))

    def body(x_ref, xp_ref, xn_ref, d_ref, dn_ref, w_ref, b_ref, dx_ref, dw_ref, db_ref):
        r = pl.program_id(1)
        x = x_ref[...]
        pre, shifted = _conv_pre(x, jnp.where(r > 0, xp_ref[...], 0.0), w_ref, b_ref)
        dpre = d_ref[...] * dsilu(pre)
        pre_n, _ = _conv_pre(xn_ref[...], x[bs - HALO:bs], w_ref, b_ref)
        dpre_n = jnp.where(r < nr - 1, dn_ref[...], 0.0) * dsilu(pre_n)
        dx = w_ref[SSM_CONV - 1:SSM_CONV, :] * dpre
        for k in range(1, SSM_CONV):
            dx = dx + w_ref[SSM_CONV - 1 - k:SSM_CONV - k, :] * _shift_up(dpre, k, dpre_n)
        dx_ref[...] = _b(dx)
        parts = [jnp.sum(dpre * shifted[SSM_CONV - 1 - j], axis=0, keepdims=True) for j in range(SSM_CONV)]
        dbp = jnp.sum(dpre, axis=0, keepdims=True)

        @pl.when(r == 0)
        def _():
            for j in range(SSM_CONV):
                dw_ref[j:j + 1, :] = parts[j]
            db_ref[...] = dbp

        @pl.when(r > 0)
        def _():
            for j in range(SSM_CONV):
                dw_ref[j:j + 1, :] += parts[j]
            db_ref[...] += dbp

    return pl.pallas_call(
        body, name="conv_bwd", grid=(C // bc, nr),
        in_specs=[pl.BlockSpec((bs, bc), lambda c, r: (r, c)),
                  pl.BlockSpec((HALO, bc), lambda c, r: (jnp.maximum(r * hb - 1, 0), c)),
                  pl.BlockSpec((HALO, bc), lambda c, r: (jnp.minimum((r + 1) * hb, last_halo), c)),
                  pl.BlockSpec((bs, bc), lambda c, r: (r, c)),
                  pl.BlockSpec((HALO, bc), lambda c, r: (jnp.minimum((r + 1) * hb, last_halo), c)),
                  pl.BlockSpec((SSM_CONV, bc), lambda c, r: (0, c)),
                  pl.BlockSpec((1, bc), lambda c, r: (0, c))],
        out_specs=[pl.BlockSpec((bs, bc), lambda c, r: (r, c)),
                   pl.BlockSpec((SSM_CONV, bc), lambda c, r: (0, c)),
                   pl.BlockSpec((1, bc), lambda c, r: (0, c))],
        out_shape=[jax.ShapeDtypeStruct((S, C), BF16), jax.ShapeDtypeStruct((SSM_CONV, C), F32),
                   jax.ShapeDtypeStruct((1, C), F32)],
        compiler_params=_params(("parallel", "arbitrary")),
    )(xbc, xbc, xbc, dact, dact, conv_w, conv_b)


def _softplus(x):
    return jnp.maximum(x, 0.0) + jnp.log(1.0 + jnp.exp(-jnp.abs(x)))


def _ssd_common(dtr_ref, bias_ref, a_ref, g):
    ch = SSM_CHUNK
    x = dtr_ref[...] + bias_ref[...]
    dt_all = _softplus(x)
    r = lax.broadcasted_iota(jnp.int32, (LANES, LANES), 0)
    c = lax.broadcasted_iota(jnp.int32, (LANES, LANES), 1)
    sel = jnp.where(jnp.logical_and(r == HEADS_PER_GROUP * g + c, c < HEADS_PER_GROUP), 1.0, 0.0)
    dt4 = _dot_hi(dt_all, sel)
    la4 = _dot_hi(dt_all * a_ref[...], sel)
    ii = lax.broadcasted_iota(jnp.int32, (ch, ch), 0)
    jj = lax.broadcasted_iota(jnp.int32, (ch, ch), 1)
    tril = jnp.where(ii >= jj, 1.0, 0.0)
    acs = _dot_hi(tril, la4)
    return x, sel, dt4, acs, acs.T, ii >= jj


def _row8(v):
    return jnp.broadcast_to(v, (8, v.shape[1]))


def _ssd_fwd(xact, dt_raw, dt_bias, a_neg, d_skip):
    S = xact.shape[0]
    ch = SSM_CHUNK
    nch = S // ch
    hg = HEADS_PER_GROUP
    gw = hg * SSM_HEAD_DIM
    b_off = SSM_INNER // SSM_STATE
    c_off = b_off + SSM_GROUPS

    def body(x_ref, b_ref, c_ref, dtr_ref, bias_ref, a_ref, dsk_ref, y_ref, hs_ref, h_ref):
        c = pl.program_id(0)
        g = pl.program_id(1)

        @pl.when(jnp.logical_and(c == 0, g == 0))
        def _():
            h_ref[...] = jnp.zeros_like(h_ref)

        _, sel, dt4, acs, acs_t, low = _ssd_common(dtr_ref, bias_ref, a_ref, g)
        dsk4 = _dot_hi(_row8(dsk_ref[...]), sel)
        bb = _b(b_ref[...])
        cc = _b(c_ref[...])
        cb = _dot_nt(cc, bb)
        for j in range(hg):
            sl = slice(j * SSM_HEAD_DIM, (j + 1) * SSM_HEAD_DIM)
            acol = acs[:, j:j + 1]
            arow = acs_t[j:j + 1, :]
            alast = acs[ch - 1:ch, j:j + 1]
            decay = jnp.exp(jnp.where(low, acol - arow, -jnp.inf))
            xh = x_ref[:, sl]
            xd = xh * dt4[:, j:j + 1]
            hj = h_ref[hg * g + j]
            y = _dot(_b(cb * decay), _b(xd))
            y = y + _dot_nt(cc, _b(hj)) * jnp.exp(acol)
            y_ref[:, sl] = y + dsk4[0:1, j:j + 1] * xh
            hs_ref[0, j] = hj
            st = _dot_tn(_b(xd * jnp.exp(alast - acol)), bb)
            h_ref[hg * g + j] = hj * jnp.exp(alast) + st

    small = pl.BlockSpec((1, LANES), lambda c, g: (0, 0))
    return pl.pallas_call(
        body, name="ssd_fwd", grid=(nch, SSM_GROUPS),
        in_specs=[pl.BlockSpec((ch, gw), lambda c, g: (c, g)),
                  pl.BlockSpec((ch, SSM_STATE), lambda c, g: (c, b_off + g)),
                  pl.BlockSpec((ch, SSM_STATE), lambda c, g: (c, c_off + g)),
                  pl.BlockSpec((ch, LANES), lambda c, g: (c, 0)),
                  small, small, small],
        out_specs=[pl.BlockSpec((ch, gw), lambda c, g: (c, g)),
                   pl.BlockSpec((1, hg, SSM_HEAD_DIM, SSM_STATE), lambda c, g: (c, g, 0, 0))],
        out_shape=[jax.ShapeDtypeStruct((S, SSM_INNER), F32),
                   jax.ShapeDtypeStruct((nch, SSM_HEADS, SSM_HEAD_DIM, SSM_STATE), F32)],
        scratch_shapes=[pltpu.VMEM((SSM_HEADS, SSM_HEAD_DIM, SSM_STATE), F32)],
        compiler_params=_params(("arbitrary", "arbitrary")),
    )(xact, xact, xact, dt_raw, dt_bias, a_neg, d_skip)


def _ssd_bwd(xact, dt_raw, dt_bias, a_neg, d_skip, hs, dy):
    S = xact.shape[0]
    ch = SSM_CHUNK
    nch = S // ch
    hg = HEADS_PER_GROUP
    gw = hg * SSM_HEAD_DIM
    b_off = SSM_INNER // SSM_STATE
    c_off = b_off + SSM_GROUPS

    def body(x_ref, b_ref, c_ref, dtr_ref, bias_ref, a_ref, dsk_ref, hs_ref, dy_ref,
             dx_ref, db_ref, dc_ref, ddt_ref, st_ref, dh_ref, ddt_acc):
        step = pl.program_id(0)
        g = pl.program_id(1)

        @pl.when(jnp.logical_and(step == 0, g == 0))
        def _():
            dh_ref[...] = jnp.zeros_like(dh_ref)
            st_ref[...] = jnp.zeros_like(st_ref)

        @pl.when(g == 0)
        def _():
            ddt_acc[...] = jnp.zeros_like(ddt_acc)

        xraw, sel, dt4, acs, acs_t, low = _ssd_common(dtr_ref, bias_ref, a_ref, g)
        a4 = _dot_hi(_row8(a_ref[...]), sel)[0:1, :]
        dsk4 = _dot_hi(_row8(dsk_ref[...]), sel)
        bf = b_ref[...]
        cf = c_ref[...]
        bb = _b(bf)
        cc = _b(cf)
        cb = _dot_nt(cc, bb)
        lane = lax.broadcasted_iota(jnp.int32, (ch, LANES), 1)
        rowi = lax.broadcasted_iota(jnp.int32, (ch, 1), 0)
        ones = jnp.ones((ch, LANES), F32)
        dcb = jnp.zeros((ch, ch), F32)
        dc_acc = jnp.zeros((ch, SSM_STATE), F32)
        db_acc = jnp.zeros((ch, SSM_STATE), F32)
        dacs4 = jnp.zeros((ch, LANES), F32)
        ddt4 = jnp.zeros((ch, LANES), F32)
        dd4 = jnp.zeros((1, LANES), F32)
        lane1 = lax.broadcasted_iota(jnp.int32, (1, LANES), 1)
        for j in range(hg):
            sl = slice(j * SSM_HEAD_DIM, (j + 1) * SSM_HEAD_DIM)
            acol = acs[:, j:j + 1]
            arow = acs_t[j:j + 1, :]
            alast = acs[ch - 1:ch, j:j + 1]
            decay = jnp.exp(jnp.where(low, acol - arow, -jnp.inf))
            ea = jnp.exp(acol)
            dsd = jnp.exp(alast - acol)
            cd = jnp.exp(alast)
            dtc = dt4[:, j:j + 1]
            xh = x_ref[:, sl]
            xd = xh * dtc
            xdb = _b(xd)
            hj = hs_ref[0, j]
            hjb = _b(hj)
            dhn = dh_ref[hg * g + j]
            dyj = dy_ref[:, sl]
            dyb = _b(dyj)
            lm = cb * decay
            dxh = dsk4[0:1, j:j + 1] * dyj
            dd4 = jnp.where(lane1 == j, jnp.sum(jnp.sum(dyj * xh, axis=1, keepdims=True), axis=0,
                                                keepdims=True), dd4)
            dlm = _dot_nt(dyb, xdb)
            dxd = _dot_tn(_b(lm), dyb)
            gm = dlm * lm
            dcb = dcb + dlm * decay
            dac = jnp.sum(gm, axis=1, keepdims=True) - _dot_tn_hi(gm, ones)[:, 0:1]
            zz = _dot_nt(cc, hjb)
            dzb = _b(dyj * ea)
            dac = dac + jnp.sum(dyj * zz, axis=1, keepdims=True) * ea
            dc_acc = dc_acc + _dot(dzb, hjb)
            dh_in = _dot_tn(dzb, cc)
            dsb = _b(dhn)
            ww = _dot_nt(bb, dsb)
            dxd = dxd + ww * dsd
            dds = jnp.sum(ww * xd, axis=1, keepdims=True) * dsd
            db_acc = db_acc + _dot(_b(xd * dsd), dsb)
            dac = dac - dds
            dal = (jnp.sum(dds, axis=0, keepdims=True)
                   + jnp.sum(jnp.sum(dhn * hj, axis=1, keepdims=True), axis=0, keepdims=True) * cd)
            dh_ref[hg * g + j] = dh_in + dhn * cd
            dac = dac + jnp.where(rowi == ch - 1, dal, 0.0)
            dacs4 = jnp.where(lane == j, dac, dacs4)
            dx_ref[:, sl] = dxh + dxd * dtc
            ddt4 = jnp.where(lane == j, jnp.sum(dxd * xh, axis=1, keepdims=True), ddt4)
        dcbb = _b(dcb)
        dc_ref[...] = dc_acc + _dot(dcbb, bb)
        db_ref[...] = db_acc + _dot_tn(dcbb, cc)
        ii = lax.broadcasted_iota(jnp.int32, (ch, ch), 0)
        jj = lax.broadcasted_iota(jnp.int32, (ch, ch), 1)
        triu = jnp.where(ii <= jj, 1.0, 0.0)
        dla4 = _dot_hi(triu, dacs4)
        ddt4 = ddt4 + dla4 * a4
        da4 = jnp.sum(dla4 * dt4, axis=0, keepdims=True) * a4
        sel_t = sel.T
        ddt_raw = _dot_hi(ddt4, sel_t) * _sigmoid(xraw)
        ddt_acc[...] += ddt_raw
        st_ref[0:1, :] += _dot_hi(_row8(da4), sel_t)[0:1, :]
        st_ref[1:2, :] += _dot_hi(_row8(dd4), sel_t)[0:1, :]
        st_ref[2:3, :] += jnp.sum(ddt_raw, axis=0, keepdims=True)

        @pl.when(g == SSM_GROUPS - 1)
        def _():
            ddt_ref[...] = _b(ddt_acc[...])

    small = pl.BlockSpec((1, LANES), lambda s, g: (0, 0))
    rc = lambda s: nch - 1 - s
    return pl.pallas_call(
        body, name="ssd_bwd", grid=(nch, SSM_GROUPS),
        in_specs=[pl.BlockSpec((ch, gw), lambda s, g: (rc(s), g)),
                  pl.BlockSpec((ch, SSM_STATE), lambda s, g: (rc(s), b_off + g)),
                  pl.BlockSpec((ch, SSM_STATE), lambda s, g: (rc(s), c_off + g)),
                  pl.BlockSpec((ch, LANES), lambda s, g: (rc(s), 0)),
                  small, small, small,
                  pl.BlockSpec((1, hg, SSM_HEAD_DIM, SSM_STATE), lambda s, g: (rc(s), g, 0, 0)),
                  pl.BlockSpec((ch, gw), lambda s, g: (rc(s), g))],
        out_specs=[pl.BlockSpec((ch, gw), lambda s, g: (rc(s), g)),
                   pl.BlockSpec((ch, SSM_STATE), lambda s, g: (rc(s), g)),
                   pl.BlockSpec((ch, SSM_STATE), lambda s, g: (rc(s), g)),
                   pl.BlockSpec((ch, LANES), lambda s, g: (rc(s), 0)),
                   pl.BlockSpec((8, LANES), lambda s, g: (0, 0))],
        out_shape=[jax.ShapeDtypeStruct((S, SSM_INNER), F32),
                   jax.ShapeDtypeStruct((S, SSM_GROUPS * SSM_STATE), F32),
                   jax.ShapeDtypeStruct((S, SSM_GROUPS * SSM_STATE), F32),
                   jax.ShapeDtypeStruct((S, LANES), BF16),
                   jax.ShapeDtypeStruct((8, LANES), F32)],
        scratch_shapes=[pltpu.VMEM((SSM_HEADS, SSM_HEAD_DIM, SSM_STATE), F32),
                        pltpu.VMEM((ch, LANES), F32)],
        compiler_params=_params(("arbitrary", "arbitrary")),
    )(xact, xact, xact, dt_raw, dt_bias, a_neg, d_skip, hs, dy)


GROUP_W = HEADS_PER_GROUP * SSM_HEAD_DIM
B_COL0 = SSM_INNER
C_COL0 = SSM_INNER + SSM_GROUPS * SSM_STATE


def _ssd_prep(dt_raw, dt_bias, a_neg):
    S = dt_raw.shape[0]
    ch = SSM_CHUNK
    nch = S // ch

    def body(dtr_ref, bias_ref, a_ref, dt_ref, acs_ref, acst_ref, sig_ref):
        x = dtr_ref[...] + bias_ref[...]
        lane = lax.broadcasted_iota(jnp.int32, (ch, LANES), 1)
        dt = jnp.where(lane < SSM_HEADS, _softplus(x), 0.0)
        ii = lax.broadcasted_iota(jnp.int32, (ch, ch), 0)
        jj = lax.broadcasted_iota(jnp.int32, (ch, ch), 1)
        acs = _dot_hi(jnp.where(ii >= jj, 1.0, 0.0), dt * a_ref[...])
        dt_ref[...] = dt
        acs_ref[...] = acs
        acst_ref[0] = acs.T[0:SSM_HEADS, :]
        sig_ref[...] = _sigmoid(x)

    blk = pl.BlockSpec((ch, LANES), lambda c: (c, 0))
    small = pl.BlockSpec((1, LANES), lambda c: (0, 0))
    shp = jax.ShapeDtypeStruct((S, LANES), F32)
    return pl.pallas_call(
        body, name="ssd_prep", grid=(nch,),
        in_specs=[blk, small, small],
        out_specs=[blk, blk, pl.BlockSpec((1, SSM_HEADS, ch), lambda c: (c, 0, 0)), blk],
        out_shape=[shp, shp, jax.ShapeDtypeStruct((nch, SSM_HEADS, ch), F32), shp],
        compiler_params=_params(("parallel",)),
    )(dt_raw, dt_bias, a_neg)


def _expand_heads(arr, g, rows):
    lane = lax.broadcasted_iota(jnp.int32, (rows, GROUP_W), 1) // SSM_HEAD_DIM
    h0 = HEADS_PER_GROUP * g
    out = jnp.broadcast_to(arr[:, h0:h0 + 1], (rows, GROUP_W))
    for j in range(1, HEADS_PER_GROUP):
        out = jnp.where(lane == j, arr[:, h0 + j:h0 + j + 1], out)
    return out


def _seg_matrix(k, lanes_per_head, h0):
    r = lax.broadcasted_iota(jnp.int32, (k, LANES), 0)
    c = lax.broadcasted_iota(jnp.int32, (k, LANES), 1)
    return jnp.where(c == h0 + r // lanes_per_head, 1.0, 0.0).astype(BF16)


def _seg_dot(t, e):
    hi = _b(t)
    lo = _b(t - hi.astype(F32))
    return _dot(hi, e) + _dot(lo, e)


def _head_sums(t, e, rows):
    if rows >= 8:
        return _seg_dot(t, e)
    return _seg_dot(jnp.broadcast_to(t, (8, t.shape[1])), e)[0:rows]


def _pair_masks(x):
    lane = lax.broadcasted_iota(jnp.int32, x.shape, 1)
    zero = jnp.zeros_like(x)
    return jnp.where(lane < SSM_HEAD_DIM, x, zero), jnp.where(lane >= SSM_HEAD_DIM, x, zero)


def _ssd_fwd2(xact, dt, acs, acst, dsk_e):
    S = xact.shape[0]
    ch = SSM_CHUNK
    nch = S // ch

    def body(x_ref, dt_ref, acs_ref, acst_ref, dsk_ref, y_ref, hs_ref, h_ref):
        c = pl.program_id(0)

        @pl.when(c == 0)
        def _():
            h_ref[...] = jnp.zeros_like(h_ref)

        dt_all = dt_ref[...]
        acs_all = acs_ref[...]
        acst_all = acst_ref[0]
        alast = acs_all[ch - 1:ch, :]
        eacs = jnp.exp(acs_all)
        dsd_all = jnp.exp(alast - acs_all)
        cd_all = jnp.exp(alast)
        ii = lax.broadcasted_iota(jnp.int32, (ch, ch), 0)
        jj = lax.broadcasted_iota(jnp.int32, (ch, ch), 1)
        low = ii >= jj
        for g in range(SSM_GROUPS):
            xs = x_ref[:, g * GROUP_W:(g + 1) * GROUP_W]
            bb = _b(x_ref[:, B_COL0 + g * SSM_STATE:B_COL0 + (g + 1) * SSM_STATE])
            cc = _b(x_ref[:, C_COL0 + g * SSM_STATE:C_COL0 + (g + 1) * SSM_STATE])
            cb = _dot_nt(cc, bb)
            xd = xs * _expand_heads(dt_all, g, ch)
            xdb = _b(xd)
            ht = h_ref[g]
            rest = (_dot(cc, _b(ht)) * _expand_heads(eacs, g, ch)
                    + dsk_ref[:, g * GROUP_W:(g + 1) * GROUP_W] * xs)
            for p in range(HEADS_PER_GROUP // 2):
                lms = []
                for h in (HEADS_PER_GROUP * g + 2 * p, HEADS_PER_GROUP * g + 2 * p + 1):
                    diff = acs_all[:, h:h + 1] - acst_all[h:h + 1, :]
                    lms.append(_b(cb * jnp.exp(jnp.where(low, diff, -jnp.inf))))
                xa, xb = _pair_masks(xdb[:, p * LANES:(p + 1) * LANES])
                yp = _dot(jnp.concatenate(lms, axis=1), jnp.concatenate([xa, xb], axis=0))
                y_ref[:, g * GROUP_W + p * LANES:g * GROUP_W + (p + 1) * LANES] = (
                    yp + rest[:, p * LANES:(p + 1) * LANES])
            hs_ref[0, g] = ht
            st = _dot_tn(bb, _b(xd * _expand_heads(dsd_all, g, ch)))
            h_ref[g] = ht * _expand_heads(cd_all, g, 1) + st

    blk = pl.BlockSpec((ch, LANES), lambda c: (c, 0))
    return pl.pallas_call(
        body, name="ssd_fwd", grid=(nch,),
        in_specs=[pl.BlockSpec((ch, CONV_DIM), lambda c: (c, 0)), blk, blk,
                  pl.BlockSpec((1, SSM_HEADS, ch), lambda c: (c, 0, 0)),
                  pl.BlockSpec((1, SSM_INNER), lambda c: (0, 0))],
        out_specs=[pl.BlockSpec((ch, SSM_INNER), lambda c: (c, 0)),
                   pl.BlockSpec((1, SSM_GROUPS, SSM_STATE, GROUP_W), lambda c: (c, 0, 0, 0))],
        out_shape=[jax.ShapeDtypeStruct((S, SSM_INNER), F32),
                   jax.ShapeDtypeStruct((nch, SSM_GROUPS, SSM_STATE, GROUP_W), F32)],
        scratch_shapes=[pltpu.VMEM((SSM_GROUPS, SSM_STATE, GROUP_W), F32)],
        compiler_params=_params(("arbitrary",)),
    )(xact, dt, acs, acst, dsk_e)


def _ssd_bwd2(xact, dt, acs, acst, sig, a_neg, dsk_e, hs, dy):
    S = xact.shape[0]
    ch = SSM_CHUNK
    nch = S // ch

    def body(x_ref, dt_ref, acs_ref, acst_ref, sig_ref, a_ref, dsk_ref, hs_ref, dy_ref,
             dx_ref, ddt_ref, st_ref, dh_ref, rows_ref):
        step = pl.program_id(0)

        @pl.when(step == 0)
        def _():
            dh_ref[...] = jnp.zeros_like(dh_ref)
            st_ref[...] = jnp.zeros_like(st_ref)
            rows_ref[...] = jnp.zeros_like(rows_ref)

        dt_all = dt_ref[...]
        acs_all = acs_ref[...]
        acst_all = acst_ref[0]
        alast = acs_all[ch - 1:ch, :]
        eacs = jnp.exp(acs_all)
        dsd_all = jnp.exp(alast - acs_all)
        cd_all = jnp.exp(alast)
        ii = lax.broadcasted_iota(jnp.int32, (ch, ch), 0)
        jj = lax.broadcasted_iota(jnp.int32, (ch, ch), 1)
        low = ii >= jj
        lane = lax.broadcasted_iota(jnp.int32, (ch, LANES), 1)
        cols = jnp.zeros((ch, LANES), F32)
        ddt = jnp.zeros((ch, LANES), F32)
        dal = jnp.zeros((1, LANES), F32)
        ddsk = jnp.zeros((1, LANES), F32)
        for g in range(SSM_GROUPS):
            xs = x_ref[:, g * GROUP_W:(g + 1) * GROUP_W]
            bb = _b(x_ref[:, B_COL0 + g * SSM_STATE:B_COL0 + (g + 1) * SSM_STATE])
            cc = _b(x_ref[:, C_COL0 + g * SSM_STATE:C_COL0 + (g + 1) * SSM_STATE])
            cb = _dot_nt(cc, bb)
            dt_e = _expand_heads(dt_all, g, ch)
            ea_e = _expand_heads(eacs, g, ch)
            dsd_e = _expand_heads(dsd_all, g, ch)
            cd_e = _expand_heads(cd_all, g, 1)
            xd = xs * dt_e
            xdb = _b(xd)
            dyg = dy_ref[:, g * GROUP_W:(g + 1) * GROUP_W]
            dyb = _b(dyg)
            ht = hs_ref[0, g]
            htb = _b(ht)
            dhn = dh_ref[g]
            dhnb = _b(dhn)
            zz = _dot(cc, htb)
            dzb = _b(dyg * ea_e)
            d_c = _dot_nt(dzb, htb)
            dh_in = _dot_tn(cc, dzb)
            ww = _dot(bb, dhnb)
            xdd = xd * dsd_e
            d_b = _dot_nt(_b(xdd), dhnb)
            t2 = ww * xdd
            e_g = _seg_matrix(GROUP_W, SSM_HEAD_DIM, HEADS_PER_GROUP * g)
            cols = cols + _head_sums(dyg * zz * ea_e - t2, e_g, ch)
            dal = dal + _head_sums(jnp.sum(t2, axis=0, keepdims=True), e_g, 1) + cd_all * _head_sums(
                jnp.sum(dhn * ht, axis=0, keepdims=True), e_g, 1)
            dh_ref[g] = dh_in + dhn * cd_e
            ddsk = ddsk + _head_sums(jnp.sum(dyg * xs, axis=0, keepdims=True), e_g, 1)
            dxd_rest = ww * dsd_e
            dcb = jnp.zeros((ch, ch), F32)
            for p in range(HEADS_PER_GROUP // 2):
                dya, dyb2 = _pair_masks(dyb[:, p * LANES:(p + 1) * LANES])
                xp = xdb[:, p * LANES:(p + 1) * LANES]
                lms, gms = [], []
                for h, dyh in ((HEADS_PER_GROUP * g + 2 * p, dya), (HEADS_PER_GROUP * g + 2 * p + 1, dyb2)):
                    diff = acs_all[:, h:h + 1] - acst_all[h:h + 1, :]
                    decay = jnp.exp(jnp.where(low, diff, -jnp.inf))
                    lm = cb * decay
                    dlm = _dot_nt(dyh, xp)
                    gm = dlm * lm
                    dcb = dcb + dlm * decay
                    rows_ref[h:h + 1, :] = jnp.sum(gm, axis=0, keepdims=True)
                    lms.append(_b(lm))
                    gms.append(gm)
                h0 = HEADS_PER_GROUP * g + 2 * p
                cols = cols + _head_sums(jnp.concatenate(gms, axis=1), _seg_matrix(2 * ch, ch, h0), ch)
                dxd = _dot_tn(jnp.concatenate(lms, axis=0), jnp.concatenate([dya, dyb2], axis=0))
                dxd = dxd + dxd_rest[:, p * LANES:(p + 1) * LANES]
                sl = slice(g * GROUP_W + p * LANES, g * GROUP_W + (p + 1) * LANES)
                dx_ref[:, sl] = (dsk_ref[:, sl] * dyg[:, p * LANES:(p + 1) * LANES]
                                 + dxd * dt_e[:, p * LANES:(p + 1) * LANES])
                ddt = ddt + _head_sums(dxd * xs[:, p * LANES:(p + 1) * LANES],
                                       _seg_matrix(LANES, SSM_HEAD_DIM, h0), ch)
            dcbb = _b(dcb)
            dx_ref[:, C_COL0 + g * SSM_STATE:C_COL0 + (g + 1) * SSM_STATE] = d_c + _dot(dcbb, bb)
            dx_ref[:, B_COL0 + g * SSM_STATE:B_COL0 + (g + 1) * SSM_STATE] = d_b + _dot_tn(dcbb, cc)
        rowi = lax.broadcasted_iota(jnp.int32, (ch, 1), 0)
        dacs = cols - rows_ref[...].T + jnp.where(rowi == ch - 1, dal, 0.0)
        dla = _dot_hi(jnp.where(ii <= jj, 1.0, 0.0), dacs)
        a_row = a_ref[...]
        ddt_raw = (ddt + dla * a_row) * sig_ref[...]
        ddt_ref[...] = _b(ddt_raw)
        st_ref[0:1, :] += jnp.sum(dla * dt_all, axis=0, keepdims=True) * a_row
        st_ref[1:2, :] += ddsk
        st_ref[2:3, :] += jnp.sum(ddt_raw, axis=0, keepdims=True)

    rc = lambda s: nch - 1 - s
    blk = pl.BlockSpec((ch, LANES), lambda s: (rc(s), 0))
    return pl.pallas_call(
        body, name="ssd_bwd", grid=(nch,),
        in_specs=[pl.BlockSpec((ch, CONV_DIM), lambda s: (rc(s), 0)), blk, blk,
                  pl.BlockSpec((1, SSM_HEADS, ch), lambda s: (rc(s), 0, 0)), blk,
                  pl.BlockSpec((1, LANES), lambda s: (0, 0)),
                  pl.BlockSpec((1, SSM_INNER), lambda s: (0, 0)),
                  pl.BlockSpec((1, SSM_GROUPS, SSM_STATE, GROUP_W), lambda s: (rc(s), 0, 0, 0)),
                  pl.BlockSpec((ch, SSM_INNER), lambda s: (rc(s), 0))],
        out_specs=[pl.BlockSpec((ch, CONV_DIM), lambda s: (rc(s), 0)), blk,
                   pl.BlockSpec((8, LANES), lambda s: (0, 0))],
        out_shape=[jax.ShapeDtypeStruct((S, CONV_DIM), F32), jax.ShapeDtypeStruct((S, LANES), BF16),
                   jax.ShapeDtypeStruct((8, LANES), F32)],
        scratch_shapes=[pltpu.VMEM((SSM_GROUPS, SSM_STATE, GROUP_W), F32), pltpu.VMEM((LANES, ch), F32)],
        compiler_params=_params(("arbitrary",)),
    )(xact, dt, acs, acst, sig, a_neg, dsk_e, hs, dy)


def _ssd_bwd3(xact, dt, acs, acst, sig, a_neg, dsk_e, hs, dy):
    S = xact.shape[0]
    ch = SSM_CHUNK
    nch = S // ch
    ng, hg = SSM_GROUPS, HEADS_PER_GROUP
    nbc = SSM_GROUPS * SSM_STATE

    def body(x_ref, dt_ref, acs_ref, acst_ref, sig_ref, a_ref, dsk_ref, hs_ref, dy_ref,
             dx_ref, ddt_ref, st_ref,
             dh_ref, rows_ref, e_dt, e_ea, e_dsd, xdb_s, xddb_s, dzb_s, bcb_s, cb_s, zz_s, ww_s, dc1_s, db1_s,
             dhin_s, dlm_s, lmb_s, gm_s, dcbb_s, t_s, dxd_s, prod_s, csum_s):
        step = pl.program_id(0)

        @pl.when(step == 0)
        def _():
            dh_ref[...] = jnp.zeros_like(dh_ref)
            st_ref[...] = jnp.zeros_like(st_ref)
            rows_ref[...] = jnp.zeros_like(rows_ref)

        dt_all = dt_ref[...]
        acs_all = acs_ref[...]
        alast = acs_all[ch - 1:ch, :]
        eacs = jnp.exp(acs_all)
        dsd_all = jnp.exp(alast - acs_all)
        cd_all = jnp.exp(alast)
        ii = lax.broadcasted_iota(jnp.int32, (ch, ch), 0)
        jj = lax.broadcasted_iota(jnp.int32, (ch, ch), 1)
        low = ii >= jj
        gsl = [slice(g * GROUP_W, (g + 1) * GROUP_W) for g in range(ng)]
        psl = [[slice(g * GROUP_W + p * LANES, g * GROUP_W + (p + 1) * LANES) for p in range(hg // 2)]
               for g in range(ng)]
        seg = [_seg_matrix(GROUP_W, SSM_HEAD_DIM, hg * g) for g in range(ng)]

        def bc(g):
            return (bcb_s[:, g * SSM_STATE:(g + 1) * SSM_STATE],
                    bcb_s[:, nbc + g * SSM_STATE:nbc + (g + 1) * SSM_STATE])

        def dy_pair(g, p):
            return _pair_masks(_b(dy_ref[:, psl[g][p]]))

        bcb_s[...] = _b(x_ref[:, B_COL0:])
        for g in range(ng):
            dt_e = _expand_heads(dt_all, g, ch)
            ea_e = _expand_heads(eacs, g, ch)
            dsd_e = _expand_heads(dsd_all, g, ch)
            e_dt[:, gsl[g]] = dt_e
            e_ea[:, gsl[g]] = ea_e
            e_dsd[:, gsl[g]] = dsd_e
            xd = x_ref[:, gsl[g]] * dt_e
            xdb_s[:, gsl[g]] = _b(xd)
            xddb_s[:, gsl[g]] = _b(xd * dsd_e)
            dzb_s[:, gsl[g]] = _b(dy_ref[:, gsl[g]] * ea_e)
        for g in range(ng):
            bb, cc = bc(g)
            htb = _b(hs_ref[0, g])
            dhnb = _b(dh_ref[g])
            cb_s[g] = _dot_nt(cc, bb)
            zz_s[:, gsl[g]] = _dot(cc, htb)
            ww_s[:, gsl[g]] = _dot(bb, dhnb)
            dc1_s[g] = _dot_nt(dzb_s[:, gsl[g]], htb)
            db1_s[g] = _dot_nt(xddb_s[:, gsl[g]], dhnb)
            dhin_s[g] = _dot_tn(cc, dzb_s[:, gsl[g]])
            for p in range(hg // 2):
                xp = xdb_s[:, psl[g][p]]
                for q, dyh in enumerate(dy_pair(g, p)):
                    dlm_s[hg * g + 2 * p + q] = _dot_nt(dyh, xp)
        for g in range(ng):
            cb = cb_s[g]
            dcb = jnp.zeros((ch, ch), F32)
            for j in range(hg):
                h = hg * g + j
                diff = acs_all[:, h:h + 1] - acst_ref[0, h:h + 1, :]
                decay = jnp.exp(jnp.where(low, diff, -jnp.inf))
                lm = cb * decay
                dlm = dlm_s[h]
                gm = dlm * lm
                dcb = dcb + dlm * decay
                rows_ref[h:h + 1, :] = jnp.sum(gm, axis=0, keepdims=True)
                lmb_s[h * ch:(h + 1) * ch, :] = _b(lm)
                gm_s[:, h * ch:(h + 1) * ch] = gm
            dcbb_s[g] = _b(dcb)
            xs = x_ref[:, gsl[g]]
            dyg = dy_ref[:, gsl[g]]
            ww = ww_s[:, gsl[g]]
            dsd_e = e_dsd[:, gsl[g]]
            t2 = ww * (xs * e_dt[:, gsl[g]] * dsd_e)
            t_s[:, gsl[g]] = dyg * zz_s[:, gsl[g]] * e_ea[:, gsl[g]] - t2
            dhn = dh_ref[g]
            csum_s[0:1, gsl[g]] = jnp.sum(t2, axis=0, keepdims=True)
            csum_s[1:2, gsl[g]] = jnp.sum(dhn * hs_ref[0, g], axis=0, keepdims=True)
            csum_s[2:3, gsl[g]] = jnp.sum(dyg * xs, axis=0, keepdims=True)
            dh_ref[g] = dhin_s[g] + dhn * _expand_heads(cd_all, g, 1)
            dxd_s[:, gsl[g]] = ww * dsd_e
        cols = jnp.zeros((ch, LANES), F32)
        for g in range(ng):
            bb, cc = bc(g)
            dcbb = dcbb_s[g]
            dx_ref[:, C_COL0 + g * SSM_STATE:C_COL0 + (g + 1) * SSM_STATE] = dc1_s[g] + _dot(dcbb, bb)
            dx_ref[:, B_COL0 + g * SSM_STATE:B_COL0 + (g + 1) * SSM_STATE] = db1_s[g] + _dot_tn(dcbb, cc)
            cols = cols + _head_sums(t_s[:, gsl[g]], seg[g], ch)
            for p in range(hg // 2):
                h0 = hg * g + 2 * p
                dxd_s[:, psl[g][p]] += _dot_tn(lmb_s[h0 * ch:(h0 + 2) * ch, :],
                                               jnp.concatenate(dy_pair(g, p), axis=0))
                cols = cols + _head_sums(gm_s[:, h0 * ch:(h0 + 2) * ch], _seg_matrix(2 * ch, ch, h0), ch)
        for g in range(ng):
            dxd = dxd_s[:, gsl[g]]
            xs = x_ref[:, gsl[g]]
            dx_ref[:, gsl[g]] = dsk_ref[:, gsl[g]] * dy_ref[:, gsl[g]] + dxd * e_dt[:, gsl[g]]
            prod_s[:, gsl[g]] = dxd * xs
        ddt = jnp.zeros((ch, LANES), F32)
        dal = jnp.zeros((1, LANES), F32)
        ddsk = jnp.zeros((1, LANES), F32)
        for g in range(ng):
            ddt = ddt + _head_sums(prod_s[:, gsl[g]], seg[g], ch)
            dal = (dal + _head_sums(csum_s[0:1, gsl[g]], seg[g], 1)
                   + cd_all * _head_sums(csum_s[1:2, gsl[g]], seg[g], 1))
            ddsk = ddsk + _head_sums(csum_s[2:3, gsl[g]], seg[g], 1)
        rowi = lax.broadcasted_iota(jnp.int32, (ch, 1), 0)
        dacs = cols - rows_ref[...].T + jnp.where(rowi == ch - 1, dal, 0.0)
        dla = _dot_hi(jnp.where(ii <= jj, 1.0, 0.0), dacs)
        a_row = a_ref[...]
        ddt_raw = (ddt + dla * a_row) * sig_ref[...]
        ddt_ref[...] = _b(ddt_raw)
        st_ref[0:1, :] += jnp.sum(dla * dt_all, axis=0, keepdims=True) * a_row
        st_ref[1:2, :] += ddsk
        st_ref[2:3, :] += jnp.sum(ddt_raw, axis=0, keepdims=True)

    rc = lambda s: nch - 1 - s
    blk = pl.BlockSpec((ch, LANES), lambda s: (rc(s), 0))
    wide = lambda dt_: pltpu.VMEM((ch, SSM_INNER), dt_)
    sq = lambda n, dt_: pltpu.VMEM((n, ch, ch), dt_)
    scratch = [pltpu.VMEM((ng, SSM_STATE, GROUP_W), F32), pltpu.VMEM((LANES, ch), F32),
               wide(F32), wide(F32), wide(F32),
               wide(BF16), wide(BF16), wide(BF16), wide(BF16),
               sq(ng, F32), wide(F32), wide(F32), sq(ng, F32), sq(ng, F32),
               pltpu.VMEM((ng, SSM_STATE, GROUP_W), F32),
               sq(SSM_HEADS, F32),
               pltpu.VMEM((SSM_HEADS * ch, ch), BF16),
               pltpu.VMEM((ch, SSM_HEADS * ch), F32),
               sq(ng, BF16), wide(F32), wide(F32), wide(F32),
               pltpu.VMEM((8, SSM_INNER), F32)]
    return pl.pallas_call(
        body, name="ssd_bwd", grid=(nch,),
        in_specs=[pl.BlockSpec((ch, CONV_DIM), lambda s: (rc(s), 0)), blk, blk,
                  pl.BlockSpec((1, SSM_HEADS, ch), lambda s: (rc(s), 0, 0)), blk,
                  pl.BlockSpec((1, LANES), lambda s: (0, 0)),
                  pl.BlockSpec((1, SSM_INNER), lambda s: (0, 0)),
                  pl.BlockSpec((1, SSM_GROUPS, SSM_STATE, GROUP_W), lambda s: (rc(s), 0, 0, 0)),
                  pl.BlockSpec((ch, SSM_INNER), lambda s: (rc(s), 0))],
        out_specs=[pl.BlockSpec((ch, CONV_DIM), lambda s: (rc(s), 0)), blk,
                   pl.BlockSpec((8, LANES), lambda s: (0, 0))],
        out_shape=[jax.ShapeDtypeStruct((S, CONV_DIM), F32), jax.ShapeDtypeStruct((S, LANES), BF16),
                   jax.ShapeDtypeStruct((8, LANES), F32)],
        scratch_shapes=scratch,
        compiler_params=_params(("arbitrary",)),
    )(xact, dt, acs, acst, sig, a_neg, dsk_e, hs, dy)


def _pad_lanes(v, n=LANES):
    return jnp.pad(v, ((0, 0), (0, n - v.shape[1])))


def _local_step(x, target, w, ex=None):
    offs = np.cumsum((0,) + IN_SPLITS)
    wt_in = w["w_in_t"]
    w_qkv = wt_in[offs[0]:offs[3]]
    w_z = wt_in[offs[3]:offs[4]]
    w_xbc = wt_in[offs[4]:offs[5]]
    w_dt = jnp.pad(wt_in[offs[5]:offs[6]], ((0, LANES - SSM_HEADS), (0, 0)))
    w_g = wt_in[offs[6]:offs[7]]
    dt_bias = _pad_lanes(w["dt_bias"])
    a_neg = _pad_lanes(-jnp.exp(w["a_log"]))
    d_skip = _pad_lanes(w["d_skip"])

    u = _rms_fwd(x, w["norm_mix_pre_w"])
    if ex is None:
        xbc = _mm_nn(u, w_xbc, F32, "proj_xbc", tb=True)
    else:
        xbc, got = _mm_nn(u, w_xbc, F32, "proj_xbc", comm=_gather_comm([ex.mine[REST_EARLY]]), tb=True)
        w = {**w, **ex.rest_weights(got[0], REST_EARLY)}
    qkv = _mm_nn(u, w_qkv, F32, "proj_qkv", tb=True)
    z = _mm_nn(u, w_z, F32, "proj_z", tb=True)
    dt_raw = _mm_nn(u, w_dt, F32, "proj_dt", tb=True)
    gl = _mm_nn(u, w_g, F32, "proj_gate", tb=True)

    pats = _qkv_layouts(qkv)
    os_, ms_, ls_ = [], [], []
    for d, qkv_p in zip(DILATIONS, pats):
        if ex is not None and d == DILATIONS[0]:
            o, m, l, got = _attn_fwd2(qkv_p, d, comm=_gather_comm([ex.mine[REST_LATE]]))
            w = {**w, **ex.rest_weights(got[0], REST_LATE)}
        else:
            o, m, l = _attn_fwd2(qkv_p, d)
        os_.append(o)
        ms_.append(m)
        ls_.append(l)
    att, lse = _attn_combine2(os_, ms_, ls_)
    att_o = _mm_nn(att, w["w_att_proj"], F32, "att_proj")

    xact = _conv_fwd2(xbc, w["conv_w"], w["conv_b"])
    dsk_e = jnp.repeat(w["d_skip"], SSM_HEAD_DIM, axis=1)
    dt, acs, acst, sig = _ssd_prep(dt_raw, dt_bias, a_neg)
    y_ssd, hs = _ssd_fwd2(xact, dt, acs, acst, dsk_e)
    ssm_y = _gnorm_fwd(y_ssd, z, w["ssm_norm_w"])
    ssm_o = _mm_nn(ssm_y, w["w_ssm_proj"], F32, "ssm_proj")

    mi = _gate_fwd(att_o, ssm_o, gl, w["b_gate"])
    mixed = _mm_nn(mi, w["w_out"], F32, "out_proj")
    h1, f = _post_pre(x, mixed, w["norm_mix_post_w"], w["norm_ffn_pre_w"])
    r_up, act = _mm_nn(f, w["w_up"], BF16, "ffn_up", mode="relu2")
    (dh2, d_down, loss, g_ffn_post), _ = _mm_epi(
        act, w["w_down"], _final_epi(h1, target, w["norm_ffn_post_w"]), "ffn_down")

    g = {"norm_ffn_post_w": g_ffn_post}
    g["w_down"] = _mm_tn(act, d_down, "dw_down")
    dup = _mm_nn(d_down, w["w_down"], BF16, "d_act", mode="mul2", extra=r_up, tb=True)
    g["w_up"] = _mm_tn(f, dup, "dw_up")
    (dh1, d_mixed, g["norm_ffn_pre_w"], g["norm_mix_post_w"]), _ = _mm_epi(
        dup, w["w_up"], _mid_epi(dh2, h1, mixed, w["norm_ffn_pre_w"], w["norm_mix_post_w"]), "d_f", tb=True)
    g["w_out"] = _mm_tn(mi, d_mixed, "dw_out")
    (d_att_o, d_ssm_o, dgl, g["b_gate"]), _ = _mm_epi(
        d_mixed, w["w_out"], _gate_epi(att_o, ssm_o, gl, w["b_gate"]), "d_mi", tb=True)

    g["w_att_proj"] = _mm_tn(att, d_att_o, "dw_att_proj")
    g["w_ssm_proj"] = _mm_tn(ssm_y, d_ssm_o, "dw_ssm_proj")
    gn_epi = _gnorm_epi(y_ssd, z, w["ssm_norm_w"])
    if ex is None:
        (dy_ssd, dz, g["ssm_norm_w"]), _ = _mm_epi(d_ssm_o, w["w_ssm_proj"], gn_epi, "d_ssm_y", tb=True,
                                                    tn=PACK_COLS)
    else:
        gs_rest = jnp.concatenate(
            [_shards_from_full(n, g[n]).reshape(N_CHIPS, -1, PACK_COLS) for n in REST], axis=1)
        (dy_ssd, dz, g["ssm_norm_w"]), recv = _mm_epi(d_ssm_o, w["w_ssm_proj"], gn_epi, "d_ssm_y", tb=True,
                                                       tn=PACK_COLS, comm=_pair_comm([gs_rest]))
        p_rest = _pair_add2(gs_rest, recv[0], ex.c_arr, "rs_pair_add_rest")

    d_att = _mm_nn(d_att_o, w["w_att_proj"], F32, "d_att", tb=True)
    dqs, dks, dvs = [], [], []
    for d, qkv_p, (do_p, lse_p, delta_p) in zip(DILATIONS, pats, _attn_delta2(d_att, att, lse)):
        if ex is not None and d == DILATIONS[0]:
            dq, dk, dv, recv3 = _attn_bwd2(qkv_p, do_p, lse_p, delta_p, d, comm=_chip_comm([p_rest]))
            q_rest = _chip_add2(p_rest, recv3[0], ex.chip_arr, "rs_chip_add_rest")
            ex.finish_reduce("rest", q_rest, _comm_call("rs_share_rest", _share_comm([q_rest]))[0])
        else:
            dq, dk, dv = _attn_bwd2(qkv_p, do_p, lse_p, delta_p, d)
        dqs.append(dq)
        dks.append(dk)
        dvs.append(dv)
    dqkv = _sum_qkv2(dqs, dks, dvs)

    dxact, ddt_raw, stats = _ssd_bwd3(xact, dt, acs, acst, sig, a_neg, dsk_e, hs, dy_ssd)
    g["a_log"] = stats[0:1, :SSM_HEADS]
    g["d_skip"] = stats[1:2, :SSM_HEADS]
    g["dt_bias"] = stats[2:3, :SSM_HEADS]
    dxbc, g["conv_w"], g["conv_b"] = _conv_bwd2(xbc, dxact, w["conv_w"], w["conv_b"])

    pieces = [(dqkv, w_qkv), (dz, w_z), (dxbc, w_xbc), (ddt_raw, w_dt), (dgl, w_g)]
    gw = [_mm_tn(dp, u, f"dw_in_{i}") for i, (dp, _) in enumerate(pieces)]
    gw[3] = gw[3][:SSM_HEADS]
    if ex is None:
        g["w_in_t"] = jnp.concatenate(gw, axis=0)
    du = None
    for i, (dp, wp) in enumerate(pieces):
        if ex is not None and i == 0:
            gs_in = _rows_to_shards(gw, IN_SHARD_ROWS, IN_SHARD_PAD)
            du, recv = _mm_nn(dp, wp, F32, f"d_u_{i}", acc=du, comm=_pair_comm([gs_in]))
            p_in = _pair_add2(gs_in, recv[0], ex.c_arr, "rs_pair_add_in")
            rows = p_in.shape[1] // 2
            p_parts = [p_in[:, :rows], p_in[:, rows:]]
            q_parts = []
        elif ex is not None and i in (1, 2):
            p_part = p_parts[i - 1]
            du, recv3 = _mm_nn(dp, wp, F32, f"d_u_{i}", acc=du, comm=_chip_comm([p_part]))
            q_parts.append(_chip_add2(p_part, recv3[0], ex.chip_arr, f"rs_chip_add_in_{i}"))
            if i == 2:
                others = _comm_call("rs_share_in", _share_comm(q_parts))
                ex.finish_reduce("w_in", jnp.concatenate(q_parts, axis=0), jnp.concatenate(others, axis=0))
        elif i == len(pieces) - 1:
            (grad_x, g["norm_mix_pre_w"]), _ = _mm_epi(
                dp, wp, _first_epi(du, dh1, x, w["norm_mix_pre_w"]), f"d_u_{i}")
        else:
            du = _mm_nn(dp, wp, F32, f"d_u_{i}", acc=du)
    return loss, grad_x, g


def _rows_to_shards(pieces, shard_rows, pad_rows):
    cols = pieces[0].shape[1]
    shards = []
    for s in range(N_CHIPS):
        lo, hi = s * shard_rows, (s + 1) * shard_rows
        parts, r0 = [], 0
        for p in pieces:
            a, b = max(lo, r0), min(hi, r0 + p.shape[0])
            if a < b:
                parts.append(p[a - r0:b - r0])
            r0 += p.shape[0]
        parts.append(jnp.zeros((pad_rows - shard_rows, cols), pieces[0].dtype))
        shards.append(jnp.concatenate(parts, axis=0))
    return jnp.stack(shards)


BIG = ("w_in", "w_att_proj", "w_ssm_proj", "w_out", "w_up", "w_down")
BIG_FULL_SHAPES = {"w_in": (D_MODEL, IN_PROJ_WIDTH), "w_att_proj": (ATT_WIDTH, D_MODEL),
                   "w_ssm_proj": (SSM_INNER, D_MODEL), "w_out": (D_MODEL, D_MODEL),
                   "w_up": (D_MODEL, FFN_HIDDEN), "w_down": (FFN_HIDDEN, D_MODEL)}
BIG_COL_SHARDED = {"w_in": True, "w_att_proj": True, "w_ssm_proj": False, "w_out": False, "w_up": True,
                   "w_down": False}
PACK_COLS = 1024
PACK_ROWS = 5760
PACK_HALF = PACK_ROWS // 2
PACK_BLOCK = 576
SMALL = ("norm_mix_pre_w", "b_gate", "conv_b", "dt_bias", "a_log", "d_skip", "ssm_norm_w",
         "norm_mix_post_w", "norm_ffn_pre_w", "norm_ffn_post_w")
SMALL_ROWS = 232


def _shard_shape(name):
    r, c = BIG_FULL_SHAPES[name]
    return (r, c // N_CHIPS) if BIG_COL_SHARDED[name] else (r // N_CHIPS, c)


def _pack(shards, dtype):
    flat = [shards[n].astype(dtype).reshape(-1, PACK_COLS) for n in BIG]
    rows = sum(f.shape[0] for f in flat)
    flat.append(jnp.zeros((PACK_ROWS - rows, PACK_COLS), dtype))
    return jnp.concatenate(flat, axis=0)


def _unpack(packed):
    out, r0 = {}, 0
    for n in BIG:
        shp = _shard_shape(n)
        rows = shp[0] * shp[1] // PACK_COLS
        out[n] = packed[r0:r0 + rows].reshape(shp)
        r0 += rows
    return out


def _unpack_full(gathered):
    out, r0 = {}, 0
    for n in BIG:
        shp = _shard_shape(n)
        rows = shp[0] * shp[1] // PACK_COLS
        sh = gathered[:, r0:r0 + rows].reshape((N_CHIPS,) + shp)
        if BIG_COL_SHARDED[n]:
            out[n] = sh.transpose(1, 0, 2).reshape(BIG_FULL_SHAPES[n])
        else:
            out[n] = sh.reshape(BIG_FULL_SHAPES[n])
        r0 += rows
    return out


def _pack_full(grads):
    parts = []
    rows_total = 0
    for n in BIG:
        shp = _shard_shape(n)
        gfull = grads[n]
        if BIG_COL_SHARDED[n]:
            sh = gfull.reshape(shp[0], N_CHIPS, shp[1]).transpose(1, 0, 2)
        else:
            sh = gfull.reshape((N_CHIPS,) + shp)
        parts.append(sh.reshape(N_CHIPS, -1, PACK_COLS))
        rows_total += parts[-1].shape[1]
    parts.append(jnp.zeros((N_CHIPS, PACK_ROWS - rows_total, PACK_COLS), F32))
    return jnp.concatenate(parts, axis=1)


def _mesh_pos():
    return lax.axis_index("x"), lax.axis_index("y"), lax.axis_index("c")


def _other_chips(x, y):
    return [(1 - x, y), (x, 1 - y), (1 - x, 1 - y)]


ANY = pl.BlockSpec(memory_space=pl.ANY)


def _allgather_packed(wpack):
    half = PACK_HALF

    def body(w_ref, out_ref, send_sems, recv_sems):
        x, y, c = _mesh_pos()
        me = 2 * x + y
        sibling = (x, y, 1 - c)
        chips = _other_chips(x, y)

        def rows(chip, h):
            return out_ref.at[chip, pl.ds(h * half, half), :]

        def copy(k, chip, h, to, src=None):
            return pltpu.make_async_remote_copy(
                src_ref=rows(chip, h) if src is None else src, dst_ref=rows(chip, h),
                send_sem=send_sems.at[k], recv_sem=recv_sems.at[k], device_id=to, device_id_type=MESH)

        mine_half = w_ref.at[pl.ds(c * half, half), :]
        first = [copy(j, me, c, (*chip, c), src=mine_half) for j, chip in enumerate(chips)]
        for cp in first:
            cp.start()
        passed = [copy(3 + j, 2 * chip[0] + chip[1], c, sibling) for j, chip in enumerate(chips)]
        for j, chip in enumerate(chips):
            copy(j, 2 * chip[0] + chip[1], c, (x, y, c)).wait_recv()
            passed[j].start()
        for j, chip in enumerate(chips):
            copy(3 + j, 2 * chip[0] + chip[1], 1 - c, (x, y, c)).wait_recv()
        for cp in first + passed:
            cp.wait_send()

    return pl.pallas_call(
        body, name="allgather_weights",
        out_shape=jax.ShapeDtypeStruct((N_CHIPS,) + wpack.shape, wpack.dtype),
        in_specs=[ANY], out_specs=ANY,
        scratch_shapes=[pltpu.SemaphoreType.DMA((6,)), pltpu.SemaphoreType.DMA((6,))],
        compiler_params=pltpu.CompilerParams(has_side_effects=True),
    )(wpack)


def _exchange_halves(gpack):
    half = PACK_HALF

    def body(g_ref, out_ref, send_sem, recv_sem):
        x, y, c = _mesh_pos()
        cp = pltpu.make_async_remote_copy(
            src_ref=g_ref.at[:, pl.ds((1 - c) * half, half), :], dst_ref=out_ref,
            send_sem=send_sem, recv_sem=recv_sem, device_id=(x, y, 1 - c), device_id_type=MESH)
        cp.start()
        cp.wait()

    return pl.pallas_call(
        body, name="rs_pair_exchange",
        out_shape=jax.ShapeDtypeStruct((N_CHIPS, half, PACK_COLS), F32),
        in_specs=[ANY], out_specs=ANY,
        scratch_shapes=[pltpu.SemaphoreType.DMA, pltpu.SemaphoreType.DMA],
        compiler_params=pltpu.CompilerParams(has_side_effects=True),
    )(gpack)


def _pair_add(gpack, recv, c_idx):
    nb = PACK_HALF // PACK_BLOCK

    def body(c_ref, g_ref, r_ref, o_ref):
        o_ref[...] = _b(g_ref[...] + r_ref[...])

    blk = (1, PACK_BLOCK, PACK_COLS)
    return pl.pallas_call(
        body, name="rs_pair_add",
        grid_spec=pltpu.PrefetchScalarGridSpec(
            num_scalar_prefetch=1, grid=(N_CHIPS, nb),
            in_specs=[pl.BlockSpec(blk, lambda s, i, c: (s, c[0] * nb + i, 0)),
                      pl.BlockSpec(blk, lambda s, i, c: (s, i, 0))],
            out_specs=pl.BlockSpec(blk, lambda s, i, c: (s, i, 0))),
        out_shape=jax.ShapeDtypeStruct((N_CHIPS, PACK_HALF, PACK_COLS), BF16),
        compiler_params=_params(("arbitrary", "arbitrary")),
    )(c_idx, gpack, recv)


def _exchange_chips(ppack):
    def body(p_ref, out_ref, send_sems, recv_sems):
        x, y, c = _mesh_pos()
        chips = _other_chips(x, y)
        cps = [pltpu.make_async_remote_copy(
            src_ref=p_ref.at[2 * chip[0] + chip[1]], dst_ref=out_ref.at[j],
            send_sem=send_sems.at[j], recv_sem=recv_sems.at[j], device_id=(*chip, c), device_id_type=MESH)
            for j, chip in enumerate(chips)]
        for cp in cps:
            cp.start()
        for cp in cps:
            cp.wait_recv()
        for cp in cps:
            cp.wait_send()

    return pl.pallas_call(
        body, name="rs_chip_exchange",
        out_shape=jax.ShapeDtypeStruct((N_CHIPS - 1, PACK_HALF, PACK_COLS), ppack.dtype),
        in_specs=[ANY], out_specs=ANY,
        scratch_shapes=[pltpu.SemaphoreType.DMA((3,)), pltpu.SemaphoreType.DMA((3,))],
        compiler_params=pltpu.CompilerParams(has_side_effects=True),
    )(ppack)


def _chip_add(ppack, recv, me_idx):
    nb = PACK_HALF // PACK_BLOCK

    def body(m_ref, p_ref, r0_ref, r1_ref, r2_ref, o_ref):
        o_ref[...] = ((p_ref[0].astype(F32) + r0_ref[0].astype(F32)) + r1_ref[0].astype(F32)) + r2_ref[0].astype(F32)

    blk = (1, PACK_BLOCK, PACK_COLS)
    return pl.pallas_call(
        body, name="rs_chip_add",
        grid_spec=pltpu.PrefetchScalarGridSpec(
            num_scalar_prefetch=1, grid=(nb,),
            in_specs=[pl.BlockSpec(blk, lambda i, m: (m[0], i, 0)),
                      pl.BlockSpec(blk, lambda i, m: (0, i, 0)),
                      pl.BlockSpec(blk, lambda i, m: (1, i, 0)),
                      pl.BlockSpec(blk, lambda i, m: (2, i, 0))],
            out_specs=pl.BlockSpec((PACK_BLOCK, PACK_COLS), lambda i, m: (i, 0))),
        out_shape=jax.ShapeDtypeStruct((PACK_HALF, PACK_COLS), F32),
        compiler_params=_params(("arbitrary",)),
    )(me_idx, ppack, recv, recv, recv)


def _share_halves(qhalf):
    def body(q_ref, out_ref, send_sem, recv_sem):
        x, y, c = _mesh_pos()
        cp = pltpu.make_async_remote_copy(
            src_ref=q_ref, dst_ref=out_ref, send_sem=send_sem, recv_sem=recv_sem,
            device_id=(x, y, 1 - c), device_id_type=MESH)
        cp.start()
        cp.wait()

    return pl.pallas_call(
        body, name="rs_share_halves",
        out_shape=jax.ShapeDtypeStruct(qhalf.shape, F32),
        in_specs=[ANY], out_specs=ANY,
        scratch_shapes=[pltpu.SemaphoreType.DMA, pltpu.SemaphoreType.DMA],
        compiler_params=pltpu.CompilerParams(has_side_effects=True),
    )(qhalf)


REST_EARLY = ("w_att_proj", "w_ssm_proj", "w_out")
REST_LATE = ("w_up", "w_down")
REST = REST_EARLY + REST_LATE
ADD_ROWS_CAP = 800
BF16_ROWS = 16
IN_SHARD_ROWS = IN_PROJ_WIDTH // N_CHIPS
IN_SHARD_PAD = 2688


def _stack_rest(shards, dtype, names=REST):
    return jnp.concatenate([shards[n].astype(dtype).reshape(-1, PACK_COLS) for n in names], axis=0)


def _unstack_rest(stacked, lead=(), names=REST):
    out, r0 = {}, 0
    for n in names:
        shp = _shard_shape(n)
        rows = shp[0] * shp[1] // PACK_COLS
        out[n] = stacked[..., r0:r0 + rows, :].reshape(lead + shp)
        r0 += rows
    return out


def _full_from_shards(name, sh):
    if BIG_COL_SHARDED[name]:
        return sh.transpose(1, 0, 2).reshape(BIG_FULL_SHAPES[name])
    return sh.reshape(BIG_FULL_SHAPES[name])


def _shards_from_full(name, full):
    shp = _shard_shape(name)
    if BIG_COL_SHARDED[name]:
        return full.reshape(shp[0], N_CHIPS, shp[1]).transpose(1, 0, 2)
    return full.reshape((N_CHIPS,) + shp)


def _allgather2(shards):
    n = len(shards)

    def body(*refs):
        w_refs, out_refs, send_sems, recv_sems = refs[:n], refs[n:2 * n], refs[2 * n], refs[2 * n + 1]
        x, y, c = _mesh_pos()
        me = 2 * x + y
        sibling = (x, y, 1 - c)
        chips = _other_chips(x, y)
        plans = []
        for a, (w_ref, out_ref) in enumerate(zip(w_refs, out_refs)):
            half = w_ref.shape[0] // 2

            def copy(k, chip, h, to, src=None, out_ref=out_ref, half=half, a=a):
                rows = out_ref.at[chip, pl.ds(h * half, half), :]
                return pltpu.make_async_remote_copy(
                    src_ref=rows if src is None else src, dst_ref=rows,
                    send_sem=send_sems.at[6 * a + k], recv_sem=recv_sems.at[6 * a + k],
                    device_id=to, device_id_type=MESH)

            mine_half = w_ref.at[pl.ds(c * half, half), :]
            idx = [2 * chip[0] + chip[1] for chip in chips]
            send = [copy(j, me, c, (*chip, c), src=mine_half) for j, chip in enumerate(chips)]
            land = [copy(j, idx[j], c, (x, y, c)) for j in range(N_CHIPS - 1)]
            forward = [copy(3 + j, idx[j], c, sibling) for j in range(N_CHIPS - 1)]
            land_fw = [copy(3 + j, idx[j], 1 - c, (x, y, c)) for j in range(N_CHIPS - 1)]
            plans.append((send, land, forward, land_fw))
        for send, _, _, _ in plans:
            for cp in send:
                cp.start()
        for _, land, forward, _ in plans:
            for j in range(N_CHIPS - 1):
                land[j].wait_recv()
                forward[j].start()
        for _, _, _, land_fw in plans:
            for cp in land_fw:
                cp.wait_recv()
        for send, _, forward, _ in plans:
            for cp in send + forward:
                cp.wait_send()

    return pl.pallas_call(
        body, name="allgather_weights",
        out_shape=[jax.ShapeDtypeStruct((N_CHIPS,) + s.shape, s.dtype) for s in shards],
        in_specs=[ANY] * n, out_specs=[ANY] * n,
        scratch_shapes=[pltpu.SemaphoreType.DMA((6 * n,)), pltpu.SemaphoreType.DMA((6 * n,))],
        compiler_params=pltpu.CompilerParams(has_side_effects=True),
    )(*shards)


def _exchange_halves2(gs):
    n = len(gs)

    def body(*refs):
        g_refs, out_refs, send_sems, recv_sems = refs[:n], refs[n:2 * n], refs[2 * n], refs[2 * n + 1]
        x, y, c = _mesh_pos()
        cps = []
        for a, (g_ref, out_ref) in enumerate(zip(g_refs, out_refs)):
            half = g_ref.shape[1] // 2
            cps.append(pltpu.make_async_remote_copy(
                src_ref=g_ref.at[:, pl.ds((1 - c) * half, half), :], dst_ref=out_ref,
                send_sem=send_sems.at[a], recv_sem=recv_sems.at[a], device_id=(x, y, 1 - c),
                device_id_type=MESH))
        for cp in cps:
            cp.start()
        for cp in cps:
            cp.wait()

    return pl.pallas_call(
        body, name="rs_pair_exchange",
        out_shape=[jax.ShapeDtypeStruct((N_CHIPS, g.shape[1] // 2, g.shape[2]), F32) for g in gs],
        in_specs=[ANY] * n, out_specs=[ANY] * n,
        scratch_shapes=[pltpu.SemaphoreType.DMA((n,)), pltpu.SemaphoreType.DMA((n,))],
        compiler_params=pltpu.CompilerParams(has_side_effects=True),
    )(*gs)


def _pair_add2(g, recv, c_idx, name):
    _, half, cols = recv.shape
    rb = _row_block(half, ADD_ROWS_CAP, BF16_ROWS)
    nb = half // rb

    def body(c_ref, g_ref, r_ref, o_ref):
        o_ref[...] = _b(g_ref[...] + r_ref[...])

    blk = (1, rb, cols)
    return pl.pallas_call(
        body, name=name,
        grid_spec=pltpu.PrefetchScalarGridSpec(
            num_scalar_prefetch=1, grid=(N_CHIPS, nb),
            in_specs=[pl.BlockSpec(blk, lambda s, i, c: (s, c[0] * nb + i, 0)),
                      pl.BlockSpec(blk, lambda s, i, c: (s, i, 0))],
            out_specs=pl.BlockSpec(blk, lambda s, i, c: (s, i, 0))),
        out_shape=jax.ShapeDtypeStruct(recv.shape, BF16),
        compiler_params=_params(("arbitrary", "arbitrary")),
    )(c_idx, g, recv)


def _exchange_chips2(ps):
    n = len(ps)

    def body(*refs):
        p_refs, out_refs, send_sems, recv_sems = refs[:n], refs[n:2 * n], refs[2 * n], refs[2 * n + 1]
        x, y, c = _mesh_pos()
        chips = _other_chips(x, y)
        cps = [pltpu.make_async_remote_copy(
            src_ref=p_ref.at[2 * chip[0] + chip[1]], dst_ref=out_ref.at[j],
            send_sem=send_sems.at[3 * a + j], recv_sem=recv_sems.at[3 * a + j], device_id=(*chip, c),
            device_id_type=MESH)
            for a, (p_ref, out_ref) in enumerate(zip(p_refs, out_refs)) for j, chip in enumerate(chips)]
        for cp in cps:
            cp.start()
        for cp in cps:
            cp.wait_recv()
        for cp in cps:
            cp.wait_send()

    return pl.pallas_call(
        body, name="rs_chip_exchange",
        out_shape=[jax.ShapeDtypeStruct((N_CHIPS - 1,) + p.shape[1:], p.dtype) for p in ps],
        in_specs=[ANY] * n, out_specs=[ANY] * n,
        scratch_shapes=[pltpu.SemaphoreType.DMA((3 * n,)), pltpu.SemaphoreType.DMA((3 * n,))],
        compiler_params=pltpu.CompilerParams(has_side_effects=True),
    )(*ps)


def _chip_add2(p, recv, me_idx, name):
    _, half, cols = recv.shape
    rb = _row_block(half, ADD_ROWS_CAP, BF16_ROWS)

    def body(m_ref, p_ref, r0_ref, r1_ref, r2_ref, o_ref):
        o_ref[...] = ((p_ref[0].astype(F32) + r0_ref[0].astype(F32)) + r1_ref[0].astype(F32)) + r2_ref[0].astype(F32)

    blk = (1, rb, cols)
    return pl.pallas_call(
        body, name=name,
        grid_spec=pltpu.PrefetchScalarGridSpec(
            num_scalar_prefetch=1, grid=(half // rb,),
            in_specs=[pl.BlockSpec(blk, lambda i, m: (m[0], i, 0)),
                      pl.BlockSpec(blk, lambda i, m: (0, i, 0)),
                      pl.BlockSpec(blk, lambda i, m: (1, i, 0)),
                      pl.BlockSpec(blk, lambda i, m: (2, i, 0))],
            out_specs=pl.BlockSpec((rb, cols), lambda i, m: (i, 0))),
        out_shape=jax.ShapeDtypeStruct((half, cols), F32),
        compiler_params=_params(("arbitrary",)),
    )(me_idx, p, recv, recv, recv)


def _share_halves2(qs):
    n = len(qs)

    def body(*refs):
        q_refs, out_refs, send_sems, recv_sems = refs[:n], refs[n:2 * n], refs[2 * n], refs[2 * n + 1]
        x, y, c = _mesh_pos()
        cps = [pltpu.make_async_remote_copy(
            src_ref=q_ref, dst_ref=out_ref, send_sem=send_sems.at[a], recv_sem=recv_sems.at[a],
            device_id=(x, y, 1 - c), device_id_type=MESH)
            for a, (q_ref, out_ref) in enumerate(zip(q_refs, out_refs))]
        for cp in cps:
            cp.start()
        for cp in cps:
            cp.wait()

    return pl.pallas_call(
        body, name="rs_share_halves",
        out_shape=[jax.ShapeDtypeStruct(q.shape, F32) for q in qs],
        in_specs=[ANY] * n, out_specs=[ANY] * n,
        scratch_shapes=[pltpu.SemaphoreType.DMA((n,)), pltpu.SemaphoreType.DMA((n,))],
        compiler_params=pltpu.CompilerParams(has_side_effects=True),
    )(*qs)


def _gather_plan():
    def copies(w_refs, out_refs, send_sems, recv_sems):
        x, y, c = _mesh_pos()
        me = 2 * x + y
        sibling = (x, y, 1 - c)
        chips = _other_chips(x, y)
        idx = [2 * chip[0] + chip[1] for chip in chips]
        plans = []
        for a, (w_ref, out_ref) in enumerate(zip(w_refs, out_refs)):
            half = w_ref.shape[0] // 2

            def copy(k, chip, h, to, src=None, out_ref=out_ref, half=half, a=a):
                rows = out_ref.at[chip, pl.ds(h * half, half), :]
                return pltpu.make_async_remote_copy(
                    src_ref=rows if src is None else src, dst_ref=rows,
                    send_sem=send_sems.at[6 * a + k], recv_sem=recv_sems.at[6 * a + k],
                    device_id=to, device_id_type=MESH)

            mine_half = w_ref.at[pl.ds(c * half, half), :]
            send = [copy(j, me, c, (*chip, c), src=mine_half) for j, chip in enumerate(chips)]
            land = [copy(j, idx[j], c, (x, y, c)) for j in range(N_CHIPS - 1)]
            forward = [copy(3 + j, idx[j], c, sibling) for j in range(N_CHIPS - 1)]
            land_fw = [copy(3 + j, idx[j], 1 - c, (x, y, c)) for j in range(N_CHIPS - 1)]
            plans.append((send, land, forward, land_fw))
        return plans

    def start(*refs):
        for send, _, _, _ in copies(*refs):
            for cp in send:
                cp.start()

    def finish(*refs):
        plans = copies(*refs)
        for _, land, forward, _ in plans:
            for j in range(N_CHIPS - 1):
                land[j].wait_recv()
                forward[j].start()
        for _, _, _, land_fw in plans:
            for cp in land_fw:
                cp.wait_recv()
        for send, _, forward, _ in plans:
            for cp in send + forward:
                cp.wait_send()

    return start, finish


def _pair_plan(halves):
    def copies(in_refs, out_refs, send_sems, recv_sems):
        x, y, c = _mesh_pos()
        cps = []
        for a, (g_ref, out_ref) in enumerate(zip(in_refs, out_refs)):
            if halves:
                half = g_ref.shape[1] // 2
                src = g_ref.at[:, pl.ds((1 - c) * half, half), :]
            else:
                src = g_ref
            cps.append(pltpu.make_async_remote_copy(
                src_ref=src, dst_ref=out_ref, send_sem=send_sems.at[a], recv_sem=recv_sems.at[a],
                device_id=(x, y, 1 - c), device_id_type=MESH))
        return cps

    def start(*refs):
        for cp in copies(*refs):
            cp.start()

    def finish(*refs):
        for cp in copies(*refs):
            cp.wait()

    return start, finish


def _chip_plan():
    def copies(in_refs, out_refs, send_sems, recv_sems):
        x, y, c = _mesh_pos()
        chips = _other_chips(x, y)
        return [pltpu.make_async_remote_copy(
            src_ref=p_ref.at[2 * chip[0] + chip[1]], dst_ref=out_ref.at[j],
            send_sem=send_sems.at[3 * a + j], recv_sem=recv_sems.at[3 * a + j], device_id=(*chip, c),
            device_id_type=MESH)
            for a, (p_ref, out_ref) in enumerate(zip(in_refs, out_refs)) for j, chip in enumerate(chips)]

    def start(*refs):
        for cp in copies(*refs):
            cp.start()

    def finish(*refs):
        cps = copies(*refs)
        for cp in cps:
            cp.wait_recv()
        for cp in cps:
            cp.wait_send()

    return start, finish


def _gather_comm(shards):
    return _Comm(_gather_plan(), shards, [jax.ShapeDtypeStruct((N_CHIPS,) + s.shape, s.dtype) for s in shards],
                 6 * len(shards))


def _pair_comm(gs):
    return _Comm(_pair_plan(True), gs,
                 [jax.ShapeDtypeStruct((N_CHIPS, g.shape[1] // 2, g.shape[2]), g.dtype) for g in gs], len(gs))


def _chip_comm(ps):
    return _Comm(_chip_plan(), ps, [jax.ShapeDtypeStruct((N_CHIPS - 1,) + p.shape[1:], p.dtype) for p in ps],
                 3 * len(ps))


def _share_comm(qs):
    return _Comm(_pair_plan(False), qs, [jax.ShapeDtypeStruct(q.shape, q.dtype) for q in qs], len(qs))


def _comm_call(name, comm):
    n, m = len(comm.ins), len(comm.outs)

    def body(*refs):
        args = (refs[:n], refs[n:n + m], refs[n + m], refs[n + m + 1])
        comm.start(*args)
        comm.finish(*args)

    return pl.pallas_call(
        body, name=name, out_shape=comm.outs, in_specs=[ANY] * n, out_specs=[ANY] * m,
        scratch_shapes=[pltpu.SemaphoreType.DMA((comm.n_sems,))] * 2,
        compiler_params=pltpu.CompilerParams(has_side_effects=True),
    )(*comm.ins)


class _Exchange:
    def __init__(self, chip, ci, early_mine, late_mine):
        self.chip, self.ci = chip, ci
        self.mine = {REST_EARLY: early_mine, REST_LATE: late_mine}
        self.c_arr = ci.reshape(1).astype(jnp.int32)
        self.chip_arr = chip.reshape(1).astype(jnp.int32)
        self.reduced = {}

    def rest_weights(self, got, names):
        stacks = lax.dynamic_update_slice(got, self.mine[names][None], (self.chip, 0, 0))
        return {n: _full_from_shards(n, sh) for n, sh in _unstack_rest(stacks, (N_CHIPS,), names).items()}

    def finish_reduce(self, key, mine, other):
        south = self.ci == 0
        self.reduced[key] = jnp.concatenate([jnp.where(south, mine, other), jnp.where(south, other, mine)],
                                            axis=0)


def _allreduce_small(part, name):
    rows = part.shape[0]

    def body(p_ref, out_ref, buf, send_sems, recv_sems, local_sem):
        x, y, c = _mesh_pos()
        me, sibling = (x, y, c), (x, y, 1 - c)
        chips = _other_chips(x, y)

        def slot(px, py, pc):
            return buf.at[pl.ds((4 * px + 2 * py + pc) * rows, rows), :]

        def copy(k, block, to, src=None):
            return pltpu.make_async_remote_copy(
                src_ref=slot(*block) if src is None else src, dst_ref=slot(*block),
                send_sem=send_sems.at[k], recv_sem=recv_sems.at[k], device_id=to, device_id_type=MESH)

        mine = pltpu.make_async_copy(p_ref, slot(*me), local_sem)
        mine.start()
        first = [copy(0, me, sibling, src=p_ref)]
        first += [copy(1 + j, me, (*chip, c), src=p_ref) for j, chip in enumerate(chips)]
        for cp in first:
            cp.start()
        passed = [copy(4 + j, (*chip, c), sibling) for j, chip in enumerate(chips)]
        for j, chip in enumerate(chips):
            copy(1 + j, (*chip, c), me).wait_recv()
            passed[j].start()
        copy(0, sibling, me).wait_recv()
        for j, chip in enumerate(chips):
            copy(4 + j, (*chip, 1 - c), me).wait_recv()
        for cp in first + passed:
            cp.wait_send()
        mine.wait()
        acc = buf[pl.ds(0, rows), :]
        for k in range(1, N_DEV):
            acc = acc + buf[pl.ds(k * rows, rows), :]
        out_ref[...] = acc

    return pl.pallas_call(
        body, name=name,
        out_shape=jax.ShapeDtypeStruct(part.shape, F32),
        in_specs=[pl.BlockSpec(memory_space=pltpu.VMEM)],
        out_specs=pl.BlockSpec(memory_space=pltpu.VMEM),
        scratch_shapes=[pltpu.VMEM((N_DEV * rows, LANES), F32), pltpu.SemaphoreType.DMA((7,)),
                        pltpu.SemaphoreType.DMA((7,)), pltpu.SemaphoreType.DMA],
        compiler_params=pltpu.CompilerParams(has_side_effects=True),
    )(part)


def _adamw(w, g, m, v, name):
    R, C = w.shape
    bs = _row_block(R, 512, 8) if R % 8 == 0 else R
    c1 = 1.0 / (1.0 - ADAM_B1 ** ADAM_STEP)
    c2 = 1.0 / (1.0 - ADAM_B2 ** ADAM_STEP)

    def body(w_ref, g_ref, m_ref, v_ref, d_ref, nm_ref, nv_ref):
        gg = g_ref[...]
        nm = ADAM_B1 * m_ref[...] + (1.0 - ADAM_B1) * gg
        nv = ADAM_B2 * v_ref[...] + (1.0 - ADAM_B2) * (gg * gg)
        nm_ref[...] = nm
        nv_ref[...] = nv
        d_ref[...] = -ADAM_LR * ((nm * c1) / (jnp.sqrt(nv * c2) + ADAM_EPS) + ADAM_WD * w_ref[...])

    spec = pl.BlockSpec((bs, C), lambda i: (i, 0))
    shp = jax.ShapeDtypeStruct((R, C), F32)
    return pl.pallas_call(
        body, name=name, grid=(R // bs,), in_specs=[spec] * 4, out_specs=[spec] * 3, out_shape=[shp] * 3,
        compiler_params=_params(("parallel",)),
    )(w, g, m, v)


WEIGHTS = ("norm_mix_pre_w", "w_in", "b_gate", "conv_w", "conv_b", "dt_bias", "a_log", "d_skip",
           "ssm_norm_w", "w_att_proj", "w_ssm_proj", "w_out", "norm_mix_post_w", "norm_ffn_pre_w", "w_up",
           "w_down", "norm_ffn_post_w")


def _flat_small(vals, conv_w_full):
    flat = [vals[n].reshape(-1) for n in SMALL] + [conv_w_full.reshape(-1)]
    v = jnp.concatenate(flat)
    return jnp.pad(v, (0, SMALL_ROWS * LANES - v.shape[0])).reshape(SMALL_ROWS, LANES)


def kernel(x, norm_mix_pre_w, w_in, b_gate, conv_w, conv_b, dt_bias, a_log, d_skip, ssm_norm_w, w_att_proj, w_ssm_proj, w_out, norm_mix_post_w, norm_ffn_pre_w, w_up, w_down, norm_ffn_post_w, loss_target, m_norm_mix_pre_w, m_w_in, m_b_gate, m_conv_w, m_conv_b, m_dt_bias, m_a_log, m_d_skip, m_ssm_norm_w, m_w_att_proj, m_w_ssm_proj, m_w_out, m_norm_mix_post_w, m_norm_ffn_pre_w, m_w_up, m_w_down, m_norm_ffn_post_w, v_norm_mix_pre_w, v_w_in, v_b_gate, v_conv_w, v_conv_b, v_dt_bias, v_a_log, v_d_skip, v_ssm_norm_w, v_w_att_proj, v_w_ssm_proj, v_w_out, v_norm_mix_post_w, v_norm_ffn_pre_w, v_w_up, v_w_down, v_norm_ffn_post_w):
    args = locals()

    def strip(a):
        return a[0] if a.ndim == 3 else a

    wts = {n: strip(args[n]) for n in WEIGHTS}
    mom = {n: strip(args["m_" + n]) for n in WEIGHTS}
    var = {n: strip(args["v_" + n]) for n in WEIGHTS}
    xi, yi, ci = _mesh_pos()
    chip = 2 * xi + yi

    tr = lambda a: jnp.swapaxes(a, 0, 1)
    w_in_mine = jnp.pad(tr(wts["w_in"]).astype(BF16), ((0, IN_SHARD_PAD - IN_SHARD_ROWS), (0, 0)))
    got_in = _comm_call("allgather_w_in", _gather_comm([w_in_mine]))[0]
    stacks_in = lax.dynamic_update_slice(got_in, w_in_mine[None], (chip, 0, 0))
    full = {"w_in_t": stacks_in[:, :IN_SHARD_ROWS].reshape(IN_PROJ_WIDTH, D_MODEL)}
    ex = _Exchange(chip, ci, _stack_rest(wts, BF16, REST_EARLY), _stack_rest(wts, BF16, REST_LATE))
    cw_cols = CONV_DIM // N_CHIPS
    conv_slab = lax.dynamic_update_slice(jnp.zeros((SSM_CONV, CONV_DIM), F32),
                                         jnp.where(ci == 0, wts["conv_w"], 0.0), (0, chip * cw_cols))
    small_in = jnp.pad(conv_slab.reshape(-1), (0, SMALL_ROWS * LANES - SSM_CONV * CONV_DIM))
    conv_full = _allreduce_small(small_in.reshape(SMALL_ROWS, LANES), "gather_conv_w")
    full["conv_w"] = conv_full.reshape(-1)[:SSM_CONV * CONV_DIM].reshape(SSM_CONV, CONV_DIM)
    for n in SMALL:
        full[n] = wts[n]

    loss_part, grad_x, g = _local_step(x[0], loss_target[0], full, ex)
    loss = lax.psum(loss_part[0, 0], ("x", "y", "c"))

    gshard = _unstack_rest(ex.reduced["rest"])
    g_in_t = ex.reduced["w_in"][:IN_SHARD_ROWS]
    small_sum = _allreduce_small(_flat_small(g, g["conv_w"]), "allreduce_small_grads").reshape(-1)
    grads, off = {}, 0
    for n in SMALL:
        sz = wts[n].size
        grads[n] = small_sum[off:off + sz].reshape(wts[n].shape)
        off += sz
    conv_g = small_sum[off:off + SSM_CONV * CONV_DIM].reshape(SSM_CONV, CONV_DIM)
    grads["conv_w"] = lax.dynamic_slice(conv_g, (0, chip * cw_cols), (SSM_CONV, cw_cols))
    grads.update(gshard)

    delta, new_m, new_v = {}, {}, {}
    for n in REST:
        delta[n], new_m[n], new_v[n] = _adamw(wts[n], grads[n], mom[n], var[n], f"adamw_{n}")
    in_t = _adamw(tr(wts["w_in"]), g_in_t, tr(mom["w_in"]), tr(var["w_in"]), "adamw_w_in")
    grads["w_in"] = tr(g_in_t)
    delta["w_in"], new_m["w_in"], new_v["w_in"] = (tr(a) for a in in_t)
    small_names = SMALL + ("conv_w",)

    def pack_small(d):
        v = jnp.concatenate([d[n].reshape(-1) for n in small_names])
        rows = -(-v.shape[0] // (8 * LANES)) * 8
        return jnp.pad(v, (0, rows * LANES - v.shape[0])).reshape(rows, LANES)

    ds, ms, vs = _adamw(pack_small(wts), pack_small(grads), pack_small(mom), pack_small(var), "adamw_small")
    off = 0
    for n in small_names:
        sz = wts[n].size
        for dst, src in ((delta, ds), (new_m, ms), (new_v, vs)):
            dst[n] = src.reshape(-1)[off:off + sz].reshape(wts[n].shape)
        off += sz

    out = [loss, grad_x[None]]
    for d in (grads, delta, new_m, new_v):
        out += [d[n][None] if args[n].ndim == 3 else d[n] for n in WEIGHTS]
    return tuple(out)
```

```python
import functools
import math

import numpy as np
import jax
import jax.numpy as jnp
from jax import lax
from jax.experimental import pallas as pl
from jax.experimental.pallas import tpu as pltpu

F32 = jnp.float32
BF16 = jnp.bfloat16

D_MODEL = 1024
HEAD_DIM = 64
N_ATT_HEADS = 12
ATT_WIDTH = N_ATT_HEADS * HEAD_DIM
DILATIONS = (1, 4, 16)
ATT_BLOCK = 128
SSM_INNER = 2048
SSM_HEADS = 32
SSM_GROUPS = 8
HEADS_PER_GROUP = SSM_HEADS // SSM_GROUPS
SSM_HEAD_DIM = 64
SSM_STATE = 128
SSM_CONV = 4
SSM_CHUNK = 128
CONV_DIM = SSM_INNER + 2 * SSM_GROUPS * SSM_STATE
FFN_HIDDEN = 4 * D_MODEL
IN_SPLITS = (ATT_WIDTH, ATT_WIDTH, ATT_WIDTH, SSM_INNER, CONV_DIM, SSM_HEADS, 2 * D_MODEL)
IN_PROJ_WIDTH = sum(IN_SPLITS)
RMS_EPS = 1e-6
LANES = 128
NEG_BIG = -1e30

ADAM_LR = 0.001
ADAM_B1 = 0.9
ADAM_B2 = 0.999
ADAM_EPS = 1e-08
ADAM_WD = 0.01
ADAM_STEP = 10

N_CHIPS = 4
N_DEV = 8
VMEM_LIMIT = 56 * 1024 * 1024
MESH = pl.DeviceIdType.MESH


def _alibi_slopes(n):
    def pow2(m):
        start = 2.0 ** (-8.0 / m)
        return [start ** (i + 1) for i in range(m)]
    if (n & (n - 1)) == 0:
        s = pow2(n)
    else:
        c = 2 ** int(math.floor(math.log2(n)))
        s = pow2(c) + pow2(2 * c)[0::2][: n - c]
    return [float(v) for v in np.array(s, dtype=np.float32)]


def _params(sem):
    return pltpu.CompilerParams(dimension_semantics=sem, vmem_limit_bytes=VMEM_LIMIT)


def _dot(a, b):
    return lax.dot_general(a, b, (((1,), (0,)), ((), ())), preferred_element_type=F32)


def _dot_nt(a, b):
    return lax.dot_general(a, b, (((1,), (1,)), ((), ())), preferred_element_type=F32)


def _dot_tn(a, b):
    return lax.dot_general(a, b, (((0,), (0,)), ((), ())), preferred_element_type=F32)


def _dot_hi(a, b):
    return lax.dot_general(a, b, (((1,), (0,)), ((), ())), preferred_element_type=F32,
                           precision=lax.Precision.HIGHEST)


def _dot_tn_hi(a, b):
    return lax.dot_general(a, b, (((0,), (0,)), ((), ())), preferred_element_type=F32,
                           precision=lax.Precision.HIGHEST)


def _b(x):
    return x.astype(BF16)


def _sigmoid(x):
    return 1.0 / (1.0 + jnp.exp(-x))


def _pick(n, cands):
    for c in cands:
        if n % c == 0:
            return c
    raise ValueError(f"no tile for {n}")


def _row_block(rows, cap, mult):
    best = max(d for d in range(mult, cap + 1, mult) if rows % d == 0)
    return best


class _Comm:
    def __init__(self, plan, ins, outs, n_sems):
        self.start, self.finish = plan
        self.ins, self.outs, self.n_sems = list(ins), list(outs), n_sems


def _mm_nn(a, b, out_dtype, name, acc=None, mode=None, extra=None, comm=None, tb=False):
    M, K = a.shape
    N = b.shape[0] if tb else b.shape[1]
    tm = 1024 if M % 1024 == 0 else 512
    tn = _pick(N, (1024, 768, 512, 256, 128))
    tk = K if K <= 2304 else _pick(K, (2048, 1024))
    nk = K // tk
    nj, ni = N // tn, M // tm
    side = acc if acc is not None else extra
    n_out = 2 if mode == "relu2" else 1
    n_in = 2 + (side is not None)
    n_ci = len(comm.ins) if comm else 0
    n_co = len(comm.outs) if comm else 0

    def body(*refs):
        a_ref, b_ref = refs[0], refs[1]
        s_ref = refs[2] if side is not None else None
        o_refs = refs[n_in + n_ci:n_in + n_ci + n_out]
        if comm:
            c_args = (refs[n_in:n_in + n_ci], refs[n_in + n_ci + n_out:n_in + n_ci + n_out + n_co],
                      refs[-2], refs[-1])
            pj, pi, pk = pl.program_id(0), pl.program_id(1), pl.program_id(2)

            @pl.when(jnp.logical_and(jnp.logical_and(pj == 0, pi == 0), pk == 0))
            def _():
                comm.start(*c_args)

        def finish(r):
            if mode == "relu2":
                r = jnp.maximum(r, 0.0)
                o_refs[0][...] = _b(r)
                o_refs[1][...] = _b(r * r)
            elif mode == "mul2":
                o_refs[0][...] = _b(r * (2.0 * s_ref[...].astype(F32)))
            else:
                if acc is not None:
                    r = r + s_ref[...]
                o_refs[0][...] = r.astype(out_dtype)

        part = (_dot_nt if tb else _dot)(_b(a_ref[...]), _b(b_ref[...]))
        if nk == 1:
            finish(part)
        else:
            acc_ref = refs[n_in + n_ci + n_out + n_co]
            k = pl.program_id(2)

            @pl.when(k == 0)
            def _():
                acc_ref[...] = part

            @pl.when(jnp.logical_and(k > 0, k < nk - 1))
            def _():
                acc_ref[...] += part

            @pl.when(k == nk - 1)
            def _():
                finish(acc_ref[...] + part)

        if comm:
            @pl.when(jnp.logical_and(jnp.logical_and(pj == nj - 1, pi == ni - 1), pk == nk - 1))
            def _():
                comm.finish(*c_args)

    tile = pl.BlockSpec((tm, tn), lambda j, i, k: (i, j))
    in_specs = [pl.BlockSpec((tm, tk), lambda j, i, k: (i, k)),
                pl.BlockSpec((tn, tk), lambda j, i, k: (j, k)) if tb else
                pl.BlockSpec((tk, tn), lambda j, i, k: (k, j))]
    args = [a, b]
    if side is not None:
        in_specs.append(tile)
        args.append(side)
    odt = BF16 if mode in ("relu2", "mul2") else out_dtype
    scratch = [pltpu.VMEM((tm, tn), F32)] if nk > 1 else []
    if comm:
        scratch += [pltpu.SemaphoreType.DMA((comm.n_sems,))] * 2
        params = pltpu.CompilerParams(dimension_semantics=("arbitrary",) * 3, vmem_limit_bytes=VMEM_LIMIT,
                                      has_side_effects=True)
    else:
        params = _params(("parallel", "parallel", "arbitrary"))
    outs = pl.pallas_call(
        body, name=name, grid=(nj, ni, nk),
        in_specs=in_specs + [ANY] * n_ci,
        out_specs=[tile] * n_out + [ANY] * n_co,
        out_shape=[jax.ShapeDtypeStruct((M, N), odt)] * n_out + list(comm.outs if comm else []),
        scratch_shapes=scratch,
        compiler_params=params,
    )(*args, *(comm.ins if comm else []))
    res = outs[:n_out] if n_out > 1 else outs[0]
    return (res, outs[n_out:]) if comm else res


class _Epi:
    def __init__(self, fn, row_ins=(), full_ins=(), row_outs=(), acc_outs=(), tiled=False):
        self.fn, self.row_ins, self.full_ins = fn, list(row_ins), list(full_ins)
        self.row_outs, self.acc_outs, self.tiled = list(row_outs), list(acc_outs), tiled


def _acc_into(ref, val, first):
    @pl.when(first)
    def _():
        ref[...] = val

    @pl.when(jnp.logical_not(first))
    def _():
        ref[...] += val


def _mm_epi(a, b, epi, name, tb=False, comm=None, tm=512, tn=None):
    M, K = a.shape
    N = b.shape[0] if tb else b.shape[1]
    tn = tn or N
    assert epi.tiled or tn == N
    tk = K if K <= 2304 else _pick(K, (2048, 1024))
    nk = K // tk
    nj, ni = N // tn, M // tm
    n_ri, n_fi, n_ro, n_ao = len(epi.row_ins), len(epi.full_ins), len(epi.row_outs), len(epi.acc_outs)
    n_ci = len(comm.ins) if comm else 0
    n_co = len(comm.outs) if comm else 0
    o0 = 2 + n_ci + n_ri + n_fi

    def body(*refs):
        a_ref, b_ref = refs[0], refs[1]
        ri = refs[2 + n_ci:2 + n_ci + n_ri]
        fi = refs[2 + n_ci + n_ri:o0]
        ro = refs[o0 + n_co:o0 + n_co + n_ro]
        ao = refs[o0 + n_co + n_ro:o0 + n_co + n_ro + n_ao]
        pj, pi, pk = pl.program_id(0), pl.program_id(1), pl.program_id(2)
        if comm:
            c_args = (refs[2:2 + n_ci], refs[o0:o0 + n_co], refs[-2], refs[-1])

            @pl.when(jnp.logical_and(jnp.logical_and(pj == 0, pi == 0), pk == 0))
            def _():
                comm.start(*c_args)

        part = (_dot_nt if tb else _dot)(_b(a_ref[...]), _b(b_ref[...]))
        if nk == 1:
            epi.fn(part, ri, fi, ro, ao, pi == 0)
        else:
            acc_ref = refs[o0 + n_co + n_ro + n_ao]

            @pl.when(pk == 0)
            def _():
                acc_ref[...] = part

            @pl.when(jnp.logical_and(pk > 0, pk < nk - 1))
            def _():
                acc_ref[...] += part

            @pl.when(pk == nk - 1)
            def _():
                epi.fn(acc_ref[...] + part, ri, fi, ro, ao, pi == 0)

        if comm:
            @pl.when(jnp.logical_and(jnp.logical_and(pj == nj - 1, pi == ni - 1), pk == nk - 1))
            def _():
                comm.finish(*c_args)

    def row_spec(width, cb):
        if epi.tiled:
            return pl.BlockSpec((tm, tn), lambda j, i, k: (i, j + cb))
        return pl.BlockSpec((tm, width), lambda j, i, k: (i, cb))

    in_specs = [pl.BlockSpec((tm, tk), lambda j, i, k: (i, k)),
                pl.BlockSpec((tn, tk), lambda j, i, k: (j, k)) if tb else
                pl.BlockSpec((tk, tn), lambda j, i, k: (k, j))]
    in_specs += [ANY] * n_ci
    in_specs += [row_spec(w, cb) for (_, w, cb) in epi.row_ins]
    in_specs += [pl.BlockSpec((1, tn), lambda j, i, k: (0, j)) if epi.tiled else
                 pl.BlockSpec(f.shape, lambda j, i, k: (0, 0)) for f in epi.full_ins]
    out_specs = [ANY] * n_co + [row_spec(c, 0) for c, _ in epi.row_outs]
    out_specs += [pl.BlockSpec((1, tn), lambda j, i, k: (0, j)) if epi.tiled else
                  pl.BlockSpec((1, c), lambda j, i, k: (0, 0)) for c in epi.acc_outs]
    out_shape = list(comm.outs if comm else [])
    out_shape += [jax.ShapeDtypeStruct((M, c), dt_) for c, dt_ in epi.row_outs]
    out_shape += [jax.ShapeDtypeStruct((1, c), F32) for c in epi.acc_outs]
    scratch = [pltpu.VMEM((tm, tn), F32)] if nk > 1 else []
    if comm:
        scratch += [pltpu.SemaphoreType.DMA((comm.n_sems,))] * 2
    params = pltpu.CompilerParams(dimension_semantics=("arbitrary",) * 3, vmem_limit_bytes=VMEM_LIMIT,
                                  has_side_effects=comm is not None)
    outs = pl.pallas_call(
        body, name=name, grid=(nj, ni, nk), in_specs=in_specs, out_specs=out_specs, out_shape=out_shape,
        scratch_shapes=scratch, compiler_params=params,
    )(a, b, *(comm.ins if comm else []), *[arr for arr, _, _ in epi.row_ins], *epi.full_ins)
    return outs[n_co:], (outs[:n_co] if comm else None)


def _mm_tn(a, b, name):
    S, Ka = a.shape
    _, N = b.shape
    tka = _pick(Ka, (1024, 768, 512, 256, 128))
    tn = _pick(N, (1024, 768, 512, 256, 128))
    ts = 1024 if S % 1024 == 0 else 512
    ns = S // ts

    def body(a_ref, b_ref, o_ref, acc_ref):
        s = pl.program_id(2)
        part = _dot_tn(_b(a_ref[...]), _b(b_ref[...]))

        @pl.when(s == 0)
        def _():
            acc_ref[...] = part

        @pl.when(s > 0)
        def _():
            acc_ref[...] += part

        @pl.when(s == ns - 1)
        def _():
            o_ref[...] = acc_ref[...]

    return pl.pallas_call(
        body, name=name, grid=(Ka // tka, N // tn, ns),
        in_specs=[pl.BlockSpec((ts, tka), lambda i, j, s: (s, i)),
                  pl.BlockSpec((ts, tn), lambda i, j, s: (s, j))],
        out_specs=pl.BlockSpec((tka, tn), lambda i, j, s: (i, j)),
        out_shape=jax.ShapeDtypeStruct((Ka, N), F32),
        scratch_shapes=[pltpu.VMEM((tka, tn), F32)],
        compiler_params=_params(("parallel", "parallel", "arbitrary")),
    )(a, b)


def _row_call(body, row_ins, full_ins, row_outs, acc_outs, bs, name):
    S = row_ins[0].shape[0]
    assert S % bs == 0
    in_specs = [pl.BlockSpec((bs, a.shape[1]), lambda i: (i, 0)) for a in row_ins]
    in_specs += [pl.BlockSpec(a.shape, lambda i: (0, 0)) for a in full_ins]
    out_specs = [pl.BlockSpec((bs, c), lambda i: (i, 0)) for c, _ in row_outs]
    out_specs += [pl.BlockSpec(s, lambda i: (0, 0)) for s in acc_outs]
    out_shape = [jax.ShapeDtypeStruct((S, c), dt) for c, dt in row_outs]
    out_shape += [jax.ShapeDtypeStruct(s, F32) for s in acc_outs]
    return pl.pallas_call(
        body, name=name, grid=(S // bs,), in_specs=in_specs, out_specs=out_specs, out_shape=out_shape,
        compiler_params=_params(("arbitrary",)),
    )(*row_ins, *full_ins)


def _rms_vals(x, w):
    r = lax.rsqrt(jnp.mean(x * x, axis=-1, keepdims=True) + RMS_EPS)
    return x * r * w


def _rms_bwd_vals(x, w, dy):
    r = lax.rsqrt(jnp.mean(x * x, axis=-1, keepdims=True) + RMS_EPS)
    xn = x * r
    g = dy * w
    dx = r * (g - xn * jnp.mean(g * xn, axis=-1, keepdims=True))
    dw = jnp.sum(dy * xn, axis=0, keepdims=True)
    return dx, dw


def _acc_add(ref, val):
    @pl.when(pl.program_id(0) == 0)
    def _():
        ref[...] = val

    @pl.when(pl.program_id(0) > 0)
    def _():
        ref[...] += val


def _rms_fwd(x, w):
    def body(x_ref, w_ref, o_ref):
        o_ref[...] = _b(_rms_vals(x_ref[...], w_ref[...]))
    return _row_call(body, [x], [w], [(x.shape[1], BF16)], [], 512, "rms_fwd")[0]


def _gate_fwd(att_o, ssm_o, gl, b_gate):
    def body(a_ref, s_ref, g_ref, b_ref, o_ref):
        g = _sigmoid(g_ref[...] + b_ref[...])
        o_ref[...] = _b(g[:, :D_MODEL] * a_ref[...] + g[:, D_MODEL:] * s_ref[...])
    return _row_call(body, [att_o, ssm_o, gl], [b_gate], [(D_MODEL, BF16)], [], 512, "gate_fwd")[0]


def _post_pre(x, mixed, w_post, w_pre):
    def body(x_ref, m_ref, wp_ref, wn_ref, h_ref, f_ref):
        h = x_ref[...] + _rms_vals(m_ref[...], wp_ref[...])
        h_ref[...] = h
        f_ref[...] = _b(_rms_vals(h, wn_ref[...]))
    return _row_call(body, [x, mixed], [w_post, w_pre], [(D_MODEL, F32), (D_MODEL, BF16)], [], 512,
                     "post_pre")


def _relu2(up):
    def body(u_ref, o_ref):
        r = jnp.maximum(u_ref[...], 0.0)
        o_ref[...] = _b(r * r)
    return _row_call(body, [up], [], [(up.shape[1], BF16)], [], 256, "relu2")[0]


def _final(h1, down, w_post, target):
    def body(h_ref, d_ref, t_ref, w_ref, dh_ref, dd_ref, loss_ref, dw_ref):
        dn = d_ref[...]
        w = w_ref[...]
        err = h_ref[...] + _rms_vals(dn, w) - t_ref[...]
        row = jnp.mean(err * err, axis=-1, keepdims=True)
        part = 0.5 * jnp.sum(row, axis=0, keepdims=True)
        dh = err * (1.0 / D_MODEL)
        dh_ref[...] = dh
        dx, dw = _rms_bwd_vals(dn, w, dh)
        dd_ref[...] = _b(dx)
        _acc_add(loss_ref, jnp.broadcast_to(part, (1, LANES)))
        _acc_add(dw_ref, dw)
    return _row_call(body, [h1, down, target], [w_post], [(D_MODEL, F32), (D_MODEL, BF16)],
                     [(1, LANES), (1, D_MODEL)], 512, "final_loss")


def _dup(da, up):
    def body(a_ref, u_ref, o_ref):
        o_ref[...] = _b(a_ref[...] * (2.0 * jnp.maximum(u_ref[...], 0.0)))
    return _row_call(body, [da, up], [], [(up.shape[1], BF16)], [], 256, "relu2_bwd")[0]


def _mid_bwd(dh2, df, h1, mixed, w_pre, w_post):
    def body(dh_ref, df_ref, h_ref, m_ref, wn_ref, wp_ref, dh1_ref, dm_ref, dwn_ref, dwp_ref):
        dx, dwn = _rms_bwd_vals(h_ref[...], wn_ref[...], df_ref[...])
        dh1 = dh_ref[...] + dx
        dh1_ref[...] = dh1
        dm, dwp = _rms_bwd_vals(m_ref[...], wp_ref[...], dh1)
        dm_ref[...] = _b(dm)
        _acc_add(dwn_ref, dwn)
        _acc_add(dwp_ref, dwp)
    return _row_call(body, [dh2, df, h1, mixed], [w_pre, w_post], [(D_MODEL, F32), (D_MODEL, BF16)],
                     [(1, D_MODEL), (1, D_MODEL)], 512, "mid_bwd")


def _gate_bwd(dmi, att_o, ssm_o, gl, b_gate):
    def body(d_ref, a_ref, s_ref, g_ref, b_ref, da_ref, ds_ref, dg_ref, db_ref):
        g = _sigmoid(g_ref[...] + b_ref[...])
        d = d_ref[...]
        ga, gs = g[:, :D_MODEL], g[:, D_MODEL:]
        da_ref[...] = _b(ga * d)
        ds_ref[...] = _b(gs * d)
        dga = d * a_ref[...] * ga * (1.0 - ga)
        dgs = d * s_ref[...] * gs * (1.0 - gs)
        dg_ref[:, :D_MODEL] = _b(dga)
        dg_ref[:, D_MODEL:] = _b(dgs)
        _acc_add(db_ref.at[:, pl.ds(0, D_MODEL)], jnp.sum(dga, axis=0, keepdims=True))
        _acc_add(db_ref.at[:, pl.ds(D_MODEL, D_MODEL)], jnp.sum(dgs, axis=0, keepdims=True))
    return _row_call(body, [dmi, att_o, ssm_o, gl], [b_gate],
                     [(D_MODEL, BF16), (D_MODEL, BF16), (2 * D_MODEL, BF16)], [(1, 2 * D_MODEL)], 256,
                     "gate_bwd")


def _first_bwd(dh1, du, x, w_pre):
    def body(dh_ref, du_ref, x_ref, w_ref, dx_ref, dw_ref):
        dx, dw = _rms_bwd_vals(x_ref[...], w_ref[...], du_ref[...])
        dx_ref[...] = dh_ref[...] + dx
        _acc_add(dw_ref, dw)
    return _row_call(body, [dh1, du, x], [w_pre], [(D_MODEL, F32)], [(1, D_MODEL)], 512, "first_bwd")


def _group_rms(t):
    gw = SSM_INNER // SSM_GROUPS
    out = []
    for g in range(SSM_GROUPS):
        tg = t[:, g * gw:(g + 1) * gw]
        out.append(lax.rsqrt(jnp.mean(tg * tg, axis=-1, keepdims=True) + RMS_EPS))
    return out


def _gnorm_fwd(y, z, w):
    gw = SSM_INNER // SSM_GROUPS

    def body(y_ref, z_ref, w_ref, o_ref):
        zz = z_ref[...]
        t = y_ref[...] * (zz * _sigmoid(zz))
        rs = _group_rms(t)
        for g in range(SSM_GROUPS):
            sl = slice(g * gw, (g + 1) * gw)
            o_ref[:, sl] = _b(t[:, sl] * rs[g] * w_ref[:, sl])
    return _row_call(body, [y, z], [w], [(SSM_INNER, BF16)], [], 256, "gnorm_fwd")[0]


def _gnorm_bwd(dout, y, z, w):
    gw = SSM_INNER // SSM_GROUPS

    def body(d_ref, y_ref, z_ref, w_ref, dy_ref, dz_ref, dw_ref):
        zz = z_ref[...]
        yy = y_ref[...]
        sg = _sigmoid(zz)
        sz = zz * sg
        t = yy * sz
        rs = _group_rms(t)
        for g in range(SSM_GROUPS):
            sl = slice(g * gw, (g + 1) * gw)
            tn = t[:, sl] * rs[g]
            d = d_ref[:, sl]
            gg = d * w_ref[:, sl]
            dt = rs[g] * (gg - tn * jnp.mean(gg * tn, axis=-1, keepdims=True))
            dy_ref[:, sl] = dt * sz[:, sl]
            dz_ref[:, sl] = _b(dt * yy[:, sl] * (sg[:, sl] * (1.0 + zz[:, sl] * (1.0 - sg[:, sl]))))
            _acc_add(dw_ref.at[:, pl.ds(g * gw, gw)], jnp.sum(d * tn, axis=0, keepdims=True))
    return _row_call(body, [dout, y, z], [w], [(SSM_INNER, F32), (SSM_INNER, BF16)], [(1, SSM_INNER)], 256,
                     "gnorm_bwd")


def _final_epi(h1, target, w_post):
    def fn(dn, ri, fi, ro, ao, first):
        w = fi[0][...]
        err = ri[0][...] + _rms_vals(dn, w) - ri[1][...]
        row = jnp.mean(err * err, axis=-1, keepdims=True)
        part = 0.5 * jnp.sum(row, axis=0, keepdims=True)
        dh = err * (1.0 / D_MODEL)
        ro[0][...] = dh
        dx, dw = _rms_bwd_vals(dn, w, dh)
        ro[1][...] = _b(dx)
        _acc_into(ao[0], jnp.broadcast_to(part, (1, LANES)), first)
        _acc_into(ao[1], dw, first)
    return _Epi(fn, [(h1, D_MODEL, 0), (target, D_MODEL, 0)], [w_post], [(D_MODEL, F32), (D_MODEL, BF16)],
                [LANES, D_MODEL])


def _mid_epi(dh2, h1, mixed, w_pre, w_post):
    def fn(df, ri, fi, ro, ao, first):
        dx, dwn = _rms_bwd_vals(ri[1][...], fi[0][...], df)
        dh1 = ri[0][...] + dx
        ro[0][...] = dh1
        dm, dwp = _rms_bwd_vals(ri[2][...], fi[1][...], dh1)
        ro[1][...] = _b(dm)
        _acc_into(ao[0], dwn, first)
        _acc_into(ao[1], dwp, first)
    return _Epi(fn, [(dh2, D_MODEL, 0), (h1, D_MODEL, 0), (mixed, D_MODEL, 0)], [w_pre, w_post],
                [(D_MODEL, F32), (D_MODEL, BF16)], [D_MODEL, D_MODEL])


def _gate_epi(att_o, ssm_o, gl, b_gate):
    def fn(d, ri, fi, ro, ao, first):
        g = _sigmoid(ri[2][...] + fi[0][...])
        ga, gs = g[:, :D_MODEL], g[:, D_MODEL:]
        ro[0][...] = _b(ga * d)
        ro[1][...] = _b(gs * d)
        dga = d * ri[0][...] * ga * (1.0 - ga)
        dgs = d * ri[1][...] * gs * (1.0 - gs)
        ro[2][:, :D_MODEL] = _b(dga)
        ro[2][:, D_MODEL:] = _b(dgs)
        _acc_into(ao[0].at[:, pl.ds(0, D_MODEL)], jnp.sum(dga, axis=0, keepdims=True), first)
        _acc_into(ao[0].at[:, pl.ds(D_MODEL, D_MODEL)], jnp.sum(dgs, axis=0, keepdims=True), first)
    return _Epi(fn, [(att_o, D_MODEL, 0), (ssm_o, D_MODEL, 0), (gl, 2 * D_MODEL, 0)], [b_gate],
                [(D_MODEL, BF16), (D_MODEL, BF16), (2 * D_MODEL, BF16)], [2 * D_MODEL])


def _first_epi(du, dh1, x, w_pre):
    def fn(r, ri, fi, ro, ao, first):
        dx, dw = _rms_bwd_vals(ri[2][...], fi[0][...], ri[0][...] + r)
        ro[0][...] = ri[1][...] + dx
        _acc_into(ao[0], dw, first)
    return _Epi(fn, [(du, D_MODEL, 0), (dh1, D_MODEL, 0), (x, D_MODEL, 0)], [w_pre], [(D_MODEL, F32)],
                [D_MODEL])


def _gnorm_epi(y, z, w):
    gw = SSM_INNER // SSM_GROUPS

    def fn(d_all, ri, fi, ro, ao, first):
        zz = ri[1][...]
        yy = ri[0][...]
        sg = _sigmoid(zz)
        sz = zz * sg
        t = yy * sz
        dws = []
        for g in range(d_all.shape[1] // gw):
            sl = slice(g * gw, (g + 1) * gw)
            tg = t[:, sl]
            r = lax.rsqrt(jnp.mean(tg * tg, axis=-1, keepdims=True) + RMS_EPS)
            tn = tg * r
            d = d_all[:, sl]
            gg = d * fi[0][:, sl]
            dt = r * (gg - tn * jnp.mean(gg * tn, axis=-1, keepdims=True))
            ro[0][:, sl] = dt * sz[:, sl]
            ro[1][:, sl] = _b(dt * yy[:, sl] * (sg[:, sl] * (1.0 + zz[:, sl] * (1.0 - sg[:, sl]))))
            dws.append(jnp.sum(d * tn, axis=0, keepdims=True))
        _acc_into(ao[0], jnp.concatenate(dws, axis=1), first)
    return _Epi(fn, [(y, SSM_INNER, 0), (z, SSM_INNER, 0)], [w], [(SSM_INNER, F32), (SSM_INNER, BF16)],
                [SSM_INNER], tiled=True)


def _to_pat(a, d):
    if d == 1:
        return a
    S, C = a.shape
    return a.reshape(S // d, d, C).transpose(1, 0, 2).reshape(S, C)


def _from_pat(a, d):
    if d == 1:
        return a
    S, C = a.shape
    return a.reshape(d, S // d, C).transpose(1, 0, 2).reshape(S, C)


def _head_col(stat, h):
    return stat[:, h:h + 1]


def _attn_fwd(q, k, v, d):
    S = q.shape[0]
    blk = ATT_BLOCK
    nblk = S // blk
    nbs = nblk // d
    slopes = _alibi_slopes(N_ATT_HEADS)
    scale = HEAD_DIM ** -0.5

    def body(q_ref, kc_ref, kp_ref, vc_ref, vp_ref, o_ref, m_ref, l_ref):
        n = pl.program_id(0)
        has_prev = (n % nbs) != 0
        ii = lax.broadcasted_iota(jnp.int32, (blk, blk), 0)
        jj = lax.broadcasted_iota(jnp.int32, (blk, blk), 1)
        dist_c = (ii - jj).astype(F32)
        dist_p = dist_c + float(blk)
        ok_c = ii >= jj
        ok_p = jnp.logical_and(jj >= ii, has_prev)
        lane = lax.broadcasted_iota(jnp.int32, (blk, LANES), 1)
        m_all = jnp.zeros((blk, LANES), F32)
        l_all = jnp.zeros((blk, LANES), F32)
        for h in range(N_ATT_HEADS):
            sl = slice(h * HEAD_DIM, (h + 1) * HEAD_DIM)
            qh = q_ref[:, sl]
            bias = slopes[h] * float(d)
            sc = jnp.where(ok_c, _dot_nt(qh, kc_ref[:, sl]) * scale - bias * dist_c, NEG_BIG)
            sp = jnp.where(ok_p, _dot_nt(qh, kp_ref[:, sl]) * scale - bias * dist_p, NEG_BIG)
            m = jnp.maximum(jnp.max(sc, axis=-1, keepdims=True), jnp.max(sp, axis=-1, keepdims=True))
            pc = jnp.exp(sc - m)
            pp = jnp.exp(sp - m)
            l = jnp.sum(pc, axis=-1, keepdims=True) + jnp.sum(pp, axis=-1, keepdims=True)
            o_ref[:, sl] = _dot(_b(pc), vc_ref[:, sl]) + _dot(_b(pp), vp_ref[:, sl])
            m_all = jnp.where(lane == h, m, m_all)
            l_all = jnp.where(lane == h, l, l_all)
        m_ref[...] = m_all
        l_ref[...] = l_all

    cur = pl.BlockSpec((blk, ATT_WIDTH), lambda n: (n, 0))
    prev = pl.BlockSpec((blk, ATT_WIDTH), lambda n: (jnp.maximum(n - 1, 0), 0))
    stat = pl.BlockSpec((blk, LANES), lambda n: (n, 0))
    return pl.pallas_call(
        body, name=f"attn_fwd_d{d}", grid=(nblk,),
        in_specs=[cur, cur, prev, cur, prev],
        out_specs=[cur, stat, stat],
        out_shape=[jax.ShapeDtypeStruct((S, ATT_WIDTH), F32), jax.ShapeDtypeStruct((S, LANES), F32),
                   jax.ShapeDtypeStruct((S, LANES), F32)],
        compiler_params=_params(("parallel",)),
    )(q, k, k, v, v)


def _attn_combine(os, ms, ls):
    def body(o1, o2, o3, m1, m2, m3, l1, l2, l3, att_ref, lse_ref):
        mm = [m1[...], m2[...], m3[...]]
        big = jnp.maximum(jnp.maximum(mm[0], mm[1]), mm[2])
        es = [jnp.exp(m - big) for m in mm]
        den = es[0] * l1[...] + es[1] * l2[...] + es[2] * l3[...]
        lse_ref[...] = big + jnp.log(den)
        inv = 1.0 / den
        for h in range(N_ATT_HEADS):
            sl = slice(h * HEAD_DIM, (h + 1) * HEAD_DIM)
            num = (_head_col(es[0], h) * o1[:, sl] + _head_col(es[1], h) * o2[:, sl]
                   + _head_col(es[2], h) * o3[:, sl])
            att_ref[:, sl] = num * _head_col(inv, h)
    return _row_call(body, list(os) + list(ms) + list(ls), [], [(ATT_WIDTH, F32), (LANES, F32)], [], 256,
                     "attn_combine")


def _attn_delta(d_att, att):
    def body(d_ref, a_ref, dl_ref, db_ref):
        dd = d_ref[...]
        prod = dd * a_ref[...]
        lane = lax.broadcasted_iota(jnp.int32, (dd.shape[0], LANES), 1)
        acc = jnp.zeros((dd.shape[0], LANES), F32)
        for h in range(N_ATT_HEADS):
            s = jnp.sum(prod[:, h * HEAD_DIM:(h + 1) * HEAD_DIM], axis=-1, keepdims=True)
            acc = jnp.where(lane == h, s, acc)
        dl_ref[...] = acc
        db_ref[...] = _b(dd)
    return _row_call(body, [d_att, att], [], [(LANES, F32), (ATT_WIDTH, BF16)], [], 512, "attn_delta")


def _attn_bwd(q, k, v, do, lse, delta, d):
    S = q.shape[0]
    blk = ATT_BLOCK
    nblk = S // blk
    nbs = nblk // d
    slopes = _alibi_slopes(N_ATT_HEADS)
    scale = HEAD_DIM ** -0.5

    def body(qc_ref, qn_ref, k_ref, v_ref, doc_ref, don_ref, lc_ref, ln_ref, dc_ref, dn_ref,
             dq_ref, dk_ref, dv_ref, carry_ref):
        n = pl.program_id(0)
        has_next = ((n + 1) % nbs) != 0

        @pl.when(n == 0)
        def _():
            carry_ref[...] = jnp.zeros_like(carry_ref)

        ii = lax.broadcasted_iota(jnp.int32, (blk, blk), 0)
        jj = lax.broadcasted_iota(jnp.int32, (blk, blk), 1)
        dist_c = (ii - jj).astype(F32)
        dist_p = dist_c + float(blk)
        ok_c = ii >= jj
        ok_p = jnp.logical_and(jj >= ii, has_next)
        for h in range(N_ATT_HEADS):
            sl = slice(h * HEAD_DIM, (h + 1) * HEAD_DIM)
            bias = slopes[h] * float(d)
            kh = k_ref[:, sl]
            vh = v_ref[:, sl]
            qh = qc_ref[:, sl]
            doh = doc_ref[:, sl]
            s = jnp.where(ok_c, _dot_nt(qh, kh) * scale - bias * dist_c - _head_col(lc_ref[...], h), NEG_BIG)
            p = jnp.exp(s)
            ds = p * (_dot_nt(doh, vh) - _head_col(dc_ref[...], h)) * scale
            pb, dsb = _b(p), _b(ds)
            dv = _dot_tn(pb, doh)
            dk = _dot_tn(dsb, qh)
            dq_ref[:, sl] = _dot(dsb, kh) + carry_ref[:, sl]
            qh = qn_ref[:, sl]
            doh = don_ref[:, sl]
            s = jnp.where(ok_p, _dot_nt(qh, kh) * scale - bias * dist_p - _head_col(ln_ref[...], h), NEG_BIG)
            p = jnp.exp(s)
            ds = p * (_dot_nt(doh, vh) - _head_col(dn_ref[...], h)) * scale
            pb, dsb = _b(p), _b(ds)
            dv_ref[:, sl] = dv + _dot_tn(pb, doh)
            dk_ref[:, sl] = dk + _dot_tn(dsb, qh)
            carry_ref[:, sl] = _dot(dsb, kh)

    cur = pl.BlockSpec((blk, ATT_WIDTH), lambda n: (n, 0))
    nxt = pl.BlockSpec((blk, ATT_WIDTH), lambda n: (jnp.minimum(n + 1, nblk - 1), 0))
    scur = pl.BlockSpec((blk, LANES), lambda n: (n, 0))
    snxt = pl.BlockSpec((blk, LANES), lambda n: (jnp.minimum(n + 1, nblk - 1), 0))
    shp = jax.ShapeDtypeStruct((S, ATT_WIDTH), F32)
    return pl.pallas_call(
        body, name=f"attn_bwd_d{d}", grid=(nblk,),
        in_specs=[cur, nxt, cur, cur, cur, nxt, scur, snxt, scur, snxt],
        out_specs=[cur, cur, cur],
        out_shape=[shp, shp, shp],
        scratch_shapes=[pltpu.VMEM((blk, ATT_WIDTH), F32)],
        compiler_params=_params(("arbitrary",)),
    )(q, q, k, v, do, do, lse, lse, delta, delta)


def _head_pair_masks(x):
    lane = lax.broadcasted_iota(jnp.int32, x.shape, 1)
    zero = jnp.zeros_like(x)
    return jnp.where(lane < HEAD_DIM, x, zero), jnp.where(lane >= HEAD_DIM, x, zero)


ATT_QUERY_ROWS = 32


def _attn_fwd2(qkv, d, comm=None):
    S = qkv.shape[0]
    blk = ATT_BLOCK
    nblk = S // blk
    nbs = nblk // d
    slopes = _alibi_slopes(N_ATT_HEADS)
    scale = HEAD_DIM ** -0.5
    n_ci = len(comm.ins) if comm else 0
    n_co = len(comm.outs) if comm else 0

    def body(*refs):
        q_ref, kc_ref, kp_ref, vc_ref, vp_ref = refs[:5]
        o_ref, m_ref, l_ref = refs[5 + n_ci:8 + n_ci]
        n = pl.program_id(0)
        if comm:
            c_args = (refs[5:5 + n_ci], refs[8 + n_ci:8 + n_ci + n_co], refs[-2], refs[-1])

            @pl.when(n == 0)
            def _():
                comm.start(*c_args)

        has_prev = (n % nbs) != 0
        ii = lax.broadcasted_iota(jnp.int32, (blk, 2 * blk), 0)
        jj = lax.broadcasted_iota(jnp.int32, (blk, 2 * blk), 1)
        dist_i = blk + ii - jj
        dist = dist_i.astype(F32)
        ok = jnp.logical_and(jnp.logical_and(dist_i >= 0, dist_i <= blk), jnp.logical_or(jj >= blk, has_prev))
        s_scr, p_scr = refs[8 + n_ci + n_co], refs[9 + n_ci + n_co]
        lane = lax.broadcasted_iota(jnp.int32, (blk, LANES), 1)
        for pr in range(N_ATT_HEADS // 2):
            sl = slice(pr * LANES, (pr + 1) * LANES)
            kcat = jnp.concatenate([kp_ref[:, sl], kc_ref[:, sl]], axis=0)
            for h, qh in zip((2 * pr, 2 * pr + 1), _head_pair_masks(q_ref[:, sl])):
                s_scr[h] = _dot_nt(qh, kcat)
        m_all = jnp.zeros((blk, LANES), F32)
        l_all = jnp.zeros((blk, LANES), F32)
        for h in range(N_ATT_HEADS):
            s = jnp.where(ok, s_scr[h] * scale - (slopes[h] * float(d)) * dist, NEG_BIG)
            m = jnp.max(s, axis=-1, keepdims=True)
            p = jnp.exp(s - m)
            l = jnp.sum(p, axis=-1, keepdims=True)
            m_all = jnp.where(lane == h, m, m_all)
            l_all = jnp.where(lane == h, l, l_all)
            p_scr[:, h * 2 * blk:(h + 1) * 2 * blk] = _b(p)
        for pr in range(N_ATT_HEADS // 2):
            sl = slice(pr * LANES, (pr + 1) * LANES)
            vmask = jnp.concatenate(
                _head_pair_masks(jnp.concatenate([vp_ref[:, sl], vc_ref[:, sl]], axis=0)), axis=0)
            o_ref[:, sl] = _dot(p_scr[:, pr * 4 * blk:(pr + 1) * 4 * blk], vmask)
        m_ref[...] = m_all
        l_ref[...] = l_all
        if comm:
            @pl.when(n == nblk - 1)
            def _():
                comm.finish(*c_args)

    cur = lambda c: pl.BlockSpec((blk, ATT_WIDTH), lambda n: (n, c))
    prev = lambda c: pl.BlockSpec((blk, ATT_WIDTH), lambda n: (jnp.maximum(n - 1, 0), c))
    stat = pl.BlockSpec((blk, LANES), lambda n: (n, 0))
    scratch = [pltpu.VMEM((N_ATT_HEADS, blk, 2 * blk), F32), pltpu.VMEM((blk, N_ATT_HEADS * 2 * blk), BF16)]
    if comm:
        scratch += [pltpu.SemaphoreType.DMA((comm.n_sems,))] * 2
        params = pltpu.CompilerParams(dimension_semantics=("arbitrary",), vmem_limit_bytes=VMEM_LIMIT,
                                      has_side_effects=True)
    else:
        params = _params(("parallel",))
    outs = pl.pallas_call(
        body, name=f"attn_fwd_d{d}", grid=(nblk,),
        in_specs=[cur(0), cur(1), prev(1), cur(2), prev(2)] + [ANY] * n_ci,
        out_specs=[cur(0), stat, stat] + [ANY] * n_co,
        out_shape=[jax.ShapeDtypeStruct((S, ATT_WIDTH), F32), jax.ShapeDtypeStruct((S, LANES), F32),
                   jax.ShapeDtypeStruct((S, LANES), F32)] + list(comm.outs if comm else []),
        scratch_shapes=scratch,
        compiler_params=params,
    )(qkv, qkv, qkv, qkv, qkv, *(comm.ins if comm else []))
    return (outs[0], outs[1], outs[2], outs[3:]) if comm else outs


def _attn_bwd2(qkv, do, lse, delta, d, comm=None):
    S = qkv.shape[0]
    blk = ATT_BLOCK
    nblk = S // blk
    nbs = nblk // d
    slopes = _alibi_slopes(N_ATT_HEADS)
    scale = HEAD_DIM ** -0.5
    n_ci = len(comm.ins) if comm else 0
    n_co = len(comm.outs) if comm else 0

    def body(*refs):
        qc_ref, qn_ref, k_ref, v_ref, doc_ref, don_ref, lc_ref, ln_ref, dc_ref, dn_ref = refs[:10]
        dq_ref, dk_ref, dv_ref = refs[10 + n_ci:13 + n_ci]
        carry_ref = refs[13 + n_ci + n_co]
        n = pl.program_id(0)
        has_next = ((n + 1) % nbs) != 0
        if comm:
            c_args = (refs[10:10 + n_ci], refs[13 + n_ci:13 + n_ci + n_co], refs[-2], refs[-1])

        @pl.when(n == 0)
        def _():
            carry_ref[...] = jnp.zeros_like(carry_ref)
            if comm:
                comm.start(*c_args)

        rr = lax.broadcasted_iota(jnp.int32, (2 * blk, blk), 0)
        jj = lax.broadcasted_iota(jnp.int32, (2 * blk, blk), 1)
        dist_i = rr - jj
        dist = dist_i.astype(F32)
        ok = jnp.logical_or(jnp.logical_and(rr < blk, dist_i >= 0),
                            jnp.logical_and(jnp.logical_and(rr >= blk, dist_i <= blk), has_next))
        s_scr, dp_scr, p_rows, ds_rows, ds_cols = refs[14 + n_ci + n_co:19 + n_ci + n_co]
        lcat = jnp.concatenate([lc_ref[...], ln_ref[...]], axis=0)
        dcat = jnp.concatenate([dc_ref[...], dn_ref[...]], axis=0)
        rows2 = 2 * blk

        def operands(pr):
            sl = slice(pr * LANES, (pr + 1) * LANES)
            qm = _head_pair_masks(jnp.concatenate([qc_ref[:, sl], qn_ref[:, sl]], axis=0))
            dom = _head_pair_masks(jnp.concatenate([doc_ref[:, sl], don_ref[:, sl]], axis=0))
            return sl, qm, dom

        for pr in range(N_ATT_HEADS // 2):
            sl, qm, dom = operands(pr)
            for h, qh, doh in zip((2 * pr, 2 * pr + 1), qm, dom):
                s_scr[h] = _dot_nt(qh, k_ref[:, sl])
                dp_scr[h] = _dot_nt(doh, v_ref[:, sl])
        for h in range(N_ATT_HEADS):
            s = jnp.where(ok, s_scr[h] * scale - (slopes[h] * float(d)) * dist - lcat[:, h:h + 1], NEG_BIG)
            p = jnp.exp(s)
            dsb = _b(p * (dp_scr[h] - dcat[:, h:h + 1]) * scale)
            p_rows[h * rows2:(h + 1) * rows2, :] = _b(p)
            ds_rows[h * rows2:(h + 1) * rows2, :] = dsb
            ds_cols[:, h * blk:(h + 1) * blk] = dsb
        for pr in range(N_ATT_HEADS // 2):
            sl, qm, dom = operands(pr)
            pair_rows = slice(pr * 2 * rows2, (pr + 1) * 2 * rows2)
            dv_ref[:, sl] = _b(_dot_tn(p_rows[pair_rows, :], jnp.concatenate(dom, axis=0)))
            dk_ref[:, sl] = _b(_dot_tn(ds_rows[pair_rows, :], jnp.concatenate(qm, axis=0)))
            dq = _dot(ds_cols[:, pr * 2 * blk:(pr + 1) * 2 * blk],
                      jnp.concatenate(_head_pair_masks(k_ref[:, sl]), axis=0))
            dq_ref[:, sl] = _b(dq[:blk] + carry_ref[:, sl])
            carry_ref[:, sl] = dq[blk:]

        if comm:
            @pl.when(n == nblk - 1)
            def _():
                comm.finish(*c_args)

    cur = lambda c: pl.BlockSpec((blk, ATT_WIDTH), lambda n: (n, c))
    nxt = lambda c: pl.BlockSpec((blk, ATT_WIDTH), lambda n: (jnp.minimum(n + 1, nblk - 1), c))
    scur = pl.BlockSpec((blk, LANES), lambda n: (n, 0))
    snxt = pl.BlockSpec((blk, LANES), lambda n: (jnp.minimum(n + 1, nblk - 1), 0))
    shp = jax.ShapeDtypeStruct((S, ATT_WIDTH), BF16)
    scratch = [pltpu.VMEM((blk, ATT_WIDTH), F32),
               pltpu.VMEM((N_ATT_HEADS, 2 * blk, blk), F32), pltpu.VMEM((N_ATT_HEADS, 2 * blk, blk), F32),
               pltpu.VMEM((N_ATT_HEADS * 2 * blk, blk), BF16), pltpu.VMEM((N_ATT_HEADS * 2 * blk, blk), BF16),
               pltpu.VMEM((2 * blk, N_ATT_HEADS * blk), BF16)]
    if comm:
        scratch += [pltpu.SemaphoreType.DMA((comm.n_sems,))] * 2
        params = pltpu.CompilerParams(dimension_semantics=("arbitrary",), vmem_limit_bytes=VMEM_LIMIT,
                                      has_side_effects=True)
    else:
        params = _params(("arbitrary",))
    outs = pl.pallas_call(
        body, name=f"attn_bwd_d{d}", grid=(nblk,),
        in_specs=[cur(0), nxt(0), cur(1), cur(2), cur(0), nxt(0), scur, snxt, scur, snxt] + [ANY] * n_ci,
        out_specs=[cur(0), cur(0), cur(0)] + [ANY] * n_co,
        out_shape=[shp, shp, shp] + list(comm.outs if comm else []),
        scratch_shapes=scratch,
        compiler_params=params,
    )(qkv, qkv, qkv, qkv, do, do, lse, lse, delta, delta, *(comm.ins if comm else []))
    return (outs[0], outs[1], outs[2], outs[3:]) if comm else outs


LAYOUT_TILE = 512
DILATED = tuple(d for d in DILATIONS if d > 1)


def _pat_spec(d, cols, col_block=0):
    return pl.BlockSpec((d, LAYOUT_TILE // d, cols), lambda i: (0, i, col_block))


def _pat_view(a, d):
    return a.reshape(d, a.shape[0] // d, a.shape[1])


def _qkv_layouts(qkv):
    S, C = qkv.shape
    t = LAYOUT_TILE

    def body(x_ref, nat_ref, *refs):
        pat_refs, slab = refs[:-1], refs[-1]
        nat_ref[...] = _b(x_ref[...])
        _to_slabs(slab, x_ref)
        for d, p_ref in zip(DILATED, pat_refs):
            _gather_pattern(p_ref, slab, d, BF16)

    outs = pl.pallas_call(
        body, name="qkv_layouts", grid=(S // t,),
        in_specs=[pl.BlockSpec((t, C), lambda i: (i, 0))],
        out_specs=[pl.BlockSpec((t, C), lambda i: (i, 0))] + [_pat_spec(d, C) for d in DILATED],
        out_shape=[jax.ShapeDtypeStruct((S, C), BF16)]
        + [jax.ShapeDtypeStruct((d, S // d, C), BF16) for d in DILATED],
        scratch_shapes=[pltpu.VMEM((C // LANES, t, LANES), F32)],
        compiler_params=_params(("parallel",)),
    )(qkv)
    return [outs[0]] + [o.reshape(S, C) for o in outs[1:]]


def _to_slabs(slab_ref, src_ref):
    for cb in range(slab_ref.shape[0]):
        slab_ref[cb] = src_ref[:, cb * LANES:(cb + 1) * LANES].astype(F32)


def _gather_pattern(dst_ref, slab_ref, d, dtype):
    t = slab_ref.shape[1]
    for cb in range(slab_ref.shape[0]):
        one = slab_ref.at[cb]
        for r in range(d):
            dst_ref[r, :, cb * LANES:(cb + 1) * LANES] = one[pl.ds(r, t // d, stride=d), :].astype(dtype)


def _scatter_pattern(slab_ref, src_ref, d, add=False):
    t = slab_ref.shape[1]
    for cb in range(slab_ref.shape[0]):
        one = slab_ref.at[cb]
        for r in range(d):
            idx = pl.ds(r, t // d, stride=d)
            val = src_ref[r, :, cb * LANES:(cb + 1) * LANES]
            if add:
                val = val + one[idx, :]
            one[idx, :] = val


def _attn_combine2(os, ms, ls):
    S = os[0].shape[0]
    t = LAYOUT_TILE

    def body(o1, o2, o3, m1, m2, m3, l1, l2, l3, att_ref, lse_ref, so2, so3, sm2, sm3, sl2, sl3):
        for d, src, dst in ((DILATED[0], o2, so2), (DILATED[1], o3, so3), (DILATED[0], m2, sm2),
                            (DILATED[1], m3, sm3), (DILATED[0], l2, sl2), (DILATED[1], l3, sl3)):
            _scatter_pattern(dst, src, d)
        mm = [m1[...], sm2[0], sm3[0]]
        big = jnp.maximum(jnp.maximum(mm[0], mm[1]), mm[2])
        es = [jnp.exp(m - big) for m in mm]
        den = es[0] * l1[...] + es[1] * sl2[0] + es[2] * sl3[0]
        lse_ref[...] = big + jnp.log(den)
        inv = 1.0 / den
        for h in range(N_ATT_HEADS):
            sl = slice(h * HEAD_DIM, (h + 1) * HEAD_DIM)
            cb, hl = divmod(h, 2)
            sll = slice(hl * HEAD_DIM, (hl + 1) * HEAD_DIM)
            num = (_head_col(es[0], h) * o1[:, sl] + _head_col(es[1], h) * so2[cb, :, sll]
                   + _head_col(es[2], h) * so3[cb, :, sll])
            att_ref[:, sl] = num * _head_col(inv, h)

    def specs(c):
        return [pl.BlockSpec((t, c), lambda i: (i, 0))] + [_pat_spec(d, c) for d in DILATED]

    args = [os[0]] + [_pat_view(o, d) for o, d in zip(os[1:], DILATED)]
    args += [ms[0]] + [_pat_view(m, d) for m, d in zip(ms[1:], DILATED)]
    args += [ls[0]] + [_pat_view(l, d) for l, d in zip(ls[1:], DILATED)]
    return pl.pallas_call(
        body, name="attn_combine", grid=(S // t,),
        in_specs=specs(ATT_WIDTH) + specs(LANES) + specs(LANES),
        out_specs=[pl.BlockSpec((t, ATT_WIDTH), lambda i: (i, 0)), pl.BlockSpec((t, LANES), lambda i: (i, 0))],
        out_shape=[jax.ShapeDtypeStruct((S, ATT_WIDTH), F32), jax.ShapeDtypeStruct((S, LANES), F32)],
        scratch_shapes=[pltpu.VMEM((ATT_WIDTH // LANES, t, LANES), F32)] * 2
        + [pltpu.VMEM((1, t, LANES), F32)] * 4,
        compiler_params=_params(("parallel",)),
    )(*args)


def _attn_delta2(d_att, att, lse):
    S = d_att.shape[0]
    t = LAYOUT_TILE

    def body(d_ref, a_ref, l_ref, *refs):
        out_refs, d_slab, l_slab, dl_slab = refs[:-3], refs[-3], refs[-2], refs[-1]
        dd = d_ref[...]
        prod = dd * a_ref[...]
        lane = lax.broadcasted_iota(jnp.int32, (t, LANES), 1)
        acc = jnp.zeros((t, LANES), F32)
        for h in range(N_ATT_HEADS):
            s = jnp.sum(prod[:, h * HEAD_DIM:(h + 1) * HEAD_DIM], axis=-1, keepdims=True)
            acc = jnp.where(lane == h, s, acc)
        out_refs[0][...] = _b(dd)
        out_refs[1][...] = acc
        _to_slabs(d_slab, d_ref)
        l_slab[0] = l_ref[...]
        dl_slab[0] = acc
        for k, d in enumerate(DILATED):
            db_ref, ls_ref, dl_ref = out_refs[2 + 3 * k:5 + 3 * k]
            _gather_pattern(db_ref, d_slab, d, BF16)
            _gather_pattern(ls_ref, l_slab, d, F32)
            _gather_pattern(dl_ref, dl_slab, d, F32)

    nat = lambda c: pl.BlockSpec((t, c), lambda i: (i, 0))
    out_specs = [nat(ATT_WIDTH), nat(LANES)]
    out_shape = [jax.ShapeDtypeStruct((S, ATT_WIDTH), BF16), jax.ShapeDtypeStruct((S, LANES), F32)]
    for d in DILATED:
        out_specs += [_pat_spec(d, ATT_WIDTH), _pat_spec(d, LANES), _pat_spec(d, LANES)]
        out_shape += [jax.ShapeDtypeStruct((d, S // d, ATT_WIDTH), BF16),
                      jax.ShapeDtypeStruct((d, S // d, LANES), F32),
                      jax.ShapeDtypeStruct((d, S // d, LANES), F32)]
    outs = pl.pallas_call(
        body, name="attn_delta", grid=(S // t,),
        in_specs=[nat(ATT_WIDTH), nat(ATT_WIDTH), nat(LANES)],
        out_specs=out_specs, out_shape=out_shape,
        scratch_shapes=[pltpu.VMEM((ATT_WIDTH // LANES, t, LANES), F32), pltpu.VMEM((1, t, LANES), F32),
                        pltpu.VMEM((1, t, LANES), F32)],
        compiler_params=_params(("parallel",)),
    )(d_att, att, lse)
    res = [(outs[0], lse, outs[1])]
    for k in range(len(DILATED)):
        db, ls, dl = outs[2 + 3 * k:5 + 3 * k]
        res.append((db.reshape(S, ATT_WIDTH), ls.reshape(S, LANES), dl.reshape(S, LANES)))
    return res


def _sum_qkv2(dqs, dks, dvs):
    S = dqs[0].shape[0]
    t = LAYOUT_TILE

    def body(*refs):
        o_ref, scr = refs[-2], refs[-1]
        for part in range(3):
            nat_ref, p_refs = refs[3 * part], refs[3 * part + 1:3 * part + 3]
            _to_slabs(scr, nat_ref)
            for d, p_ref in zip(DILATED, p_refs):
                _scatter_pattern(scr, p_ref, d, add=True)
            for cb in range(ATT_WIDTH // LANES):
                o_ref[:, part * ATT_WIDTH + cb * LANES:part * ATT_WIDTH + (cb + 1) * LANES] = _b(scr[cb])

    in_specs, args = [], []
    for group in (dqs, dks, dvs):
        in_specs += [pl.BlockSpec((t, ATT_WIDTH), lambda i: (i, 0))] + [_pat_spec(d, ATT_WIDTH) for d in DILATED]
        args += [group[0]] + [_pat_view(a, d) for a, d in zip(group[1:], DILATED)]
    return pl.pallas_call(
        body, name="sum_dqkv", grid=(S // t,),
        in_specs=in_specs,
        out_specs=pl.BlockSpec((t, 3 * ATT_WIDTH), lambda i: (i, 0)),
        out_shape=jax.ShapeDtypeStruct((S, 3 * ATT_WIDTH), BF16),
        scratch_shapes=[pltpu.VMEM((ATT_WIDTH // LANES, t, LANES), F32)],
        compiler_params=_params(("parallel",)),
    )(*args)


def _sum_qkv(dqs, dks, dvs):
    def body(q1, q2, q3, k1, k2, k3, v1, v2, v3, o_ref):
        o_ref[:, 0:ATT_WIDTH] = _b(q1[...] + q2[...] + q3[...])
        o_ref[:, ATT_WIDTH:2 * ATT_WIDTH] = _b(k1[...] + k2[...] + k3[...])
        o_ref[:, 2 * ATT_WIDTH:] = _b(v1[...] + v2[...] + v3[...])
    return _row_call(body, list(dqs) + list(dks) + list(dvs), [], [(3 * ATT_WIDTH, BF16)], [], 256,
                     "sum_dqkv")[0]


CONV_COLS = 1024
CONV_ROWS = 512
HALO = 8


def _conv_fwd(xbc, conv_w, conv_b):
    S, C = xbc.shape
    bs, bc = CONV_ROWS, CONV_COLS
    nr = S // bs

    def body(x_ref, halo_ref, w_ref, b_ref, o_ref, xs_ref):
        r = pl.program_id(1)
        xs_ref[pl.ds(HALO, bs), :] = x_ref[...]
        xs_ref[pl.ds(0, HALO), :] = jnp.where(r > 0, halo_ref[...], 0.0)
        pre = b_ref[...] + w_ref[3:4, :] * x_ref[...]
        for j in range(SSM_CONV - 1):
            pre = pre + w_ref[j:j + 1, :] * xs_ref[pl.ds(HALO - 3 + j, bs), :]
        o_ref[...] = pre * _sigmoid(pre)

    return pl.pallas_call(
        body, name="conv_fwd", grid=(C // bc, nr),
        in_specs=[pl.BlockSpec((bs, bc), lambda c, r: (r, c)),
                  pl.BlockSpec((HALO, bc), lambda c, r: (jnp.maximum(r * (bs // HALO) - 1, 0), c)),
                  pl.BlockSpec((SSM_CONV, bc), lambda c, r: (0, c)),
                  pl.BlockSpec((1, bc), lambda c, r: (0, c))],
        out_specs=pl.BlockSpec((bs, bc), lambda c, r: (r, c)),
        out_shape=jax.ShapeDtypeStruct((S, C), F32),
        scratch_shapes=[pltpu.VMEM((bs + HALO, bc), F32)],
        compiler_params=_params(("parallel", "arbitrary")),
    )(xbc, xbc, conv_w, conv_b)


def _conv_bwd(xbc, dact, conv_w, conv_b, col0):
    S, C = xbc.shape
    Cp = dact.shape[1]
    bs, bc = CONV_ROWS, min(CONV_COLS, Cp)
    nr = S // bs
    cb0 = col0 // bc
    last_halo = S // HALO - 1

    def body(x_ref, xp_ref, xn_ref, d_ref, dn_ref, w_ref, b_ref, dx_ref, dw_ref, db_ref,
             xs_ref, dp_ref):
        r = pl.program_id(1)
        xs_ref[pl.ds(0, HALO), :] = jnp.where(r > 0, xp_ref[...], 0.0)
        xs_ref[pl.ds(HALO, bs), :] = x_ref[...]
        xs_ref[pl.ds(HALO + bs, HALO), :] = xn_ref[...]
        ext = bs + HALO
        pre = b_ref[...] + jnp.zeros((ext, bc), F32)
        for j in range(SSM_CONV):
            pre = pre + w_ref[j:j + 1, :] * xs_ref[pl.ds(HALO - 3 + j, ext), :]
        sg = _sigmoid(pre)
        dsilu = sg * (1.0 + pre * (1.0 - sg))
        dp_ref[pl.ds(0, bs), :] = d_ref[...] * dsilu[:bs]
        dp_ref[pl.ds(bs, HALO), :] = jnp.where(r < nr - 1, dn_ref[...], 0.0) * dsilu[bs:]
        dx = jnp.zeros((bs, bc), F32)
        for j in range(SSM_CONV):
            dx = dx + w_ref[j:j + 1, :] * dp_ref[pl.ds(3 - j, bs), :]
        dx_ref[...] = _b(dx)
        dpre = dp_ref[pl.ds(0, bs), :]
        for j in range(SSM_CONV):
            part = jnp.sum(dpre * xs_ref[pl.ds(HALO - 3 + j, bs), :], axis=0, keepdims=True)

            @pl.when(r == 0)
            def _():
                dw_ref[j:j + 1, :] = part

            @pl.when(r > 0)
            def _():
                dw_ref[j:j + 1, :] += part
        part = jnp.sum(dpre, axis=0, keepdims=True)

        @pl.when(r == 0)
        def _():
            db_ref[...] = part

        @pl.when(r > 0)
        def _():
            db_ref[...] += part

    hb = bs // HALO
    return pl.pallas_call(
        body, name=f"conv_bwd_{col0}", grid=(Cp // bc, nr),
        in_specs=[pl.BlockSpec((bs, bc), lambda c, r: (r, cb0 + c)),
                  pl.BlockSpec((HALO, bc), lambda c, r: (jnp.maximum(r * hb - 1, 0), cb0 + c)),
                  pl.BlockSpec((HALO, bc), lambda c, r: (jnp.minimum((r + 1) * hb, last_halo), cb0 + c)),
                  pl.BlockSpec((bs, bc), lambda c, r: (r, c)),
                  pl.BlockSpec((HALO, bc), lambda c, r: (jnp.minimum((r + 1) * hb, last_halo), c)),
                  pl.BlockSpec((SSM_CONV, bc), lambda c, r: (0, cb0 + c)),
                  pl.BlockSpec((1, bc), lambda c, r: (0, cb0 + c))],
        out_specs=[pl.BlockSpec((bs, bc), lambda c, r: (r, c)),
                   pl.BlockSpec((SSM_CONV, bc), lambda c, r: (0, c)),
                   pl.BlockSpec((1, bc), lambda c, r: (0, c))],
        out_shape=[jax.ShapeDtypeStruct((S, Cp), BF16), jax.ShapeDtypeStruct((SSM_CONV, Cp), F32),
                   jax.ShapeDtypeStruct((1, Cp), F32)],
        scratch_shapes=[pltpu.VMEM((bs + 2 * HALO, bc), F32), pltpu.VMEM((bs + HALO, bc), F32)],
        compiler_params=_params(("parallel", "arbitrary")),
    )(xbc, xbc, xbc, dact, dact, conv_w, conv_b)


def _shift_down(x, k, top_src):
    r8 = lax.broadcasted_iota(jnp.int32, (HALO, x.shape[1]), 0)
    rolled = pltpu.roll(x, k, 0)
    top = jnp.where(r8 < k, pltpu.roll(top_src, k, 0), rolled[0:HALO])
    if x.shape[0] == HALO:
        return top
    return jnp.concatenate([top, rolled[HALO:]], axis=0)


def _shift_up(x, k, bottom_src):
    n = x.shape[0]
    r8 = lax.broadcasted_iota(jnp.int32, (HALO, x.shape[1]), 0)
    rolled = pltpu.roll(x, n - k, 0)
    bottom = jnp.where(r8 >= HALO - k, pltpu.roll(bottom_src, HALO - k, 0), rolled[n - HALO:n])
    return jnp.concatenate([rolled[:n - HALO], bottom], axis=0)


def _conv_pre(x, top_src, w_ref, b_ref):
    shifted = [x] + [_shift_down(x, k, top_src) for k in range(1, SSM_CONV)]
    pre = b_ref[...] + w_ref[SSM_CONV - 1:SSM_CONV, :] * x
    for k in range(1, SSM_CONV):
        pre = pre + w_ref[SSM_CONV - 1 - k:SSM_CONV - k, :] * shifted[k]
    return pre, shifted


def _conv_fwd2(xbc, conv_w, conv_b):
    S, C = xbc.shape
    bs, bc = CONV_ROWS, CONV_COLS
    nr = S // bs

    def body(x_ref, halo_ref, w_ref, b_ref, o_ref, pre_ref):
        r = pl.program_id(1)
        halo = jnp.where(r > 0, halo_ref[...], 0.0)
        pre, _ = _conv_pre(x_ref[...], halo, w_ref, b_ref)
        pre_ref[...] = pre
        o_ref[...] = pre * _sigmoid(pre)

    tile = pl.BlockSpec((bs, bc), lambda c, r: (r, c))
    return pl.pallas_call(
        body, name="conv_fwd", grid=(C // bc, nr),
        in_specs=[tile,
                  pl.BlockSpec((HALO, bc), lambda c, r: (jnp.maximum(r * (bs // HALO) - 1, 0), c)),
                  pl.BlockSpec((SSM_CONV, bc), lambda c, r: (0, c)),
                  pl.BlockSpec((1, bc), lambda c, r: (0, c))],
        out_specs=[tile, tile],
        out_shape=[jax.ShapeDtypeStruct((S, C), F32), jax.ShapeDtypeStruct((S, C), F32)],
        compiler_params=_params(("parallel", "arbitrary")),
    )(xbc, xbc, conv_w, conv_b)


def _conv_bwd2(xbc, pre_all, dact, conv_w):
    S, C = xbc.shape
    bs, bc = CONV_ROWS, CONV_COLS
    nr = S // bs
    hb = bs // HALO
    last_halo = S // HALO - 1

    def dsilu(pre):
        sg = _sigmoid(pre)
        return sg * (1.0 + pre * (1.0 - sg))

    def body(x_ref, p_ref, pn_ref, d_ref, dn_ref, w_ref, dx_ref, dw_ref, db_ref):
        r = pl.program_id(1)
        x = x_ref[...]
        dpre = d_ref[...] * dsilu(p_ref[...])
        dpre_n = jnp.where(r < nr - 1, dn_ref[...], 0.0) * dsilu(pn_ref[...])
        ups = [dpre] + [_shift_up(dpre, k, dpre_n) for k in range(1, SSM_CONV)]
        dx = w_ref[SSM_CONV - 1:SSM_CONV, :] * dpre
        for k in range(1, SSM_CONV):
            dx = dx + w_ref[SSM_CONV - 1 - k:SSM_CONV - k, :] * ups[k]
        dx_ref[...] = _b(dx)
        parts = [jnp.sum(x * ups[SSM_CONV - 1 - j], axis=0, keepdims=True) for j in range(SSM_CONV)]
        dbp = jnp.sum(dpre, axis=0, keepdims=True)

        @pl.when(r == 0)
        def _():
            for j in range(SSM_CONV):
                dw_ref[j:j + 1, :] = parts[j]
            db_ref[...] = dbp

        @pl.when(r > 0)
        def _():
            for j in range(SSM_CONV):
                dw_ref[j:j + 1, :] += parts[j]
            db_ref[...] += dbp

    tile = pl.BlockSpec((bs, bc), lambda c, r: (r, c))
    nxt = pl.BlockSpec((HALO, bc), lambda c, r: (jnp.minimum((r + 1) * hb, last_halo), c))
    return pl.pallas_call(
        body, name="conv_bwd", grid=(C // bc, nr),
        in_specs=[tile, tile, nxt, tile, nxt, pl.BlockSpec((SSM_CONV, bc), lambda c, r: (0, c))],
        out_specs=[tile,
                   pl.BlockSpec((SSM_CONV, bc), lambda c, r: (0, c)),
                   pl.BlockSpec((1, bc), lambda c, r: (0, c))],
        out_shape=[jax.ShapeDtypeStruct((S, C), BF16), jax.ShapeDtypeStruct((SSM_CONV, C), F32),
                   jax.ShapeDtypeStruct((1, C), F32)],
        compiler_params=_params(("parallel", "arbitrary")),
    )(xbc, pre_all, pre_all, dact, dact, conv_w)


def _softplus(x):
    return jnp.maximum(x, 0.0) + jnp.log(1.0 + jnp.exp(-jnp.abs(x)))


def _ssd_common(dtr_ref, bias_ref, a_ref, g):
    ch = SSM_CHUNK
    x = dtr_ref[...] + bias_ref[...]
    dt_all = _softplus(x)
    r = lax.broadcasted_iota(jnp.int32, (LANES, LANES), 0)
    c = lax.broadcasted_iota(jnp.int32, (LANES, LANES), 1)
    sel = jnp.where(jnp.logical_and(r == HEADS_PER_GROUP * g + c, c < HEADS_PER_GROUP), 1.0, 0.0)
    dt4 = _dot_hi(dt_all, sel)
    la4 = _dot_hi(dt_all * a_ref[...], sel)
    ii = lax.broadcasted_iota(jnp.int32, (ch, ch), 0)
    jj = lax.broadcasted_iota(jnp.int32, (ch, ch), 1)
    tril = jnp.where(ii >= jj, 1.0, 0.0)
    acs = _dot_hi(tril, la4)
    return x, sel, dt4, acs, acs.T, ii >= jj


def _row8(v):
    return jnp.broadcast_to(v, (8, v.shape[1]))


def _ssd_fwd(xact, dt_raw, dt_bias, a_neg, d_skip):
    S = xact.shape[0]
    ch = SSM_CHUNK
    nch = S // ch
    hg = HEADS_PER_GROUP
    gw = hg * SSM_HEAD_DIM
    b_off = SSM_INNER // SSM_STATE
    c_off = b_off + SSM_GROUPS

    def body(x_ref, b_ref, c_ref, dtr_ref, bias_ref, a_ref, dsk_ref, y_ref, hs_ref, h_ref):
        c = pl.program_id(0)
        g = pl.program_id(1)

        @pl.when(jnp.logical_and(c == 0, g == 0))
        def _():
            h_ref[...] = jnp.zeros_like(h_ref)

        _, sel, dt4, acs, acs_t, low = _ssd_common(dtr_ref, bias_ref, a_ref, g)
        dsk4 = _dot_hi(_row8(dsk_ref[...]), sel)
        bb = _b(b_ref[...])
        cc = _b(c_ref[...])
        cb = _dot_nt(cc, bb)
        for j in range(hg):
            sl = slice(j * SSM_HEAD_DIM, (j + 1) * SSM_HEAD_DIM)
            acol = acs[:, j:j + 1]
            arow = acs_t[j:j + 1, :]
            alast = acs[ch - 1:ch, j:j + 1]
            decay = jnp.exp(jnp.where(low, acol - arow, -jnp.inf))
            xh = x_ref[:, sl]
            xd = xh * dt4[:, j:j + 1]
            hj = h_ref[hg * g + j]
            y = _dot(_b(cb * decay), _b(xd))
            y = y + _dot_nt(cc, _b(hj)) * jnp.exp(acol)
            y_ref[:, sl] = y + dsk4[0:1, j:j + 1] * xh
            hs_ref[0, j] = hj
            st = _dot_tn(_b(xd * jnp.exp(alast - acol)), bb)
            h_ref[hg * g + j] = hj * jnp.exp(alast) + st

    small = pl.BlockSpec((1, LANES), lambda c, g: (0, 0))
    return pl.pallas_call(
        body, name="ssd_fwd", grid=(nch, SSM_GROUPS),
        in_specs=[pl.BlockSpec((ch, gw), lambda c, g: (c, g)),
                  pl.BlockSpec((ch, SSM_STATE), lambda c, g: (c, b_off + g)),
                  pl.BlockSpec((ch, SSM_STATE), lambda c, g: (c, c_off + g)),
                  pl.BlockSpec((ch, LANES), lambda c, g: (c, 0)),
                  small, small, small],
        out_specs=[pl.BlockSpec((ch, gw), lambda c, g: (c, g)),
                   pl.BlockSpec((1, hg, SSM_HEAD_DIM, SSM_STATE), lambda c, g: (c, g, 0, 0))],
        out_shape=[jax.ShapeDtypeStruct((S, SSM_INNER), F32),
                   jax.ShapeDtypeStruct((nch, SSM_HEADS, SSM_HEAD_DIM, SSM_STATE), F32)],
        scratch_shapes=[pltpu.VMEM((SSM_HEADS, SSM_HEAD_DIM, SSM_STATE), F32)],
        compiler_params=_params(("arbitrary", "arbitrary")),
    )(xact, xact, xact, dt_raw, dt_bias, a_neg, d_skip)


def _ssd_bwd(xact, dt_raw, dt_bias, a_neg, d_skip, hs, dy):
    S = xact.shape[0]
    ch = SSM_CHUNK
    nch = S // ch
    hg = HEADS_PER_GROUP
    gw = hg * SSM_HEAD_DIM
    b_off = SSM_INNER // SSM_STATE
    c_off = b_off + SSM_GROUPS

    def body(x_ref, b_ref, c_ref, dtr_ref, bias_ref, a_ref, dsk_ref, hs_ref, dy_ref,
             dx_ref, db_ref, dc_ref, ddt_ref, st_ref, dh_ref, ddt_acc):
        step = pl.program_id(0)
        g = pl.program_id(1)

        @pl.when(jnp.logical_and(step == 0, g == 0))
        def _():
            dh_ref[...] = jnp.zeros_like(dh_ref)
            st_ref[...] = jnp.zeros_like(st_ref)

        @pl.when(g == 0)
        def _():
            ddt_acc[...] = jnp.zeros_like(ddt_acc)

        xraw, sel, dt4, acs, acs_t, low = _ssd_common(dtr_ref, bias_ref, a_ref, g)
        a4 = _dot_hi(_row8(a_ref[...]), sel)[0:1, :]
        dsk4 = _dot_hi(_row8(dsk_ref[...]), sel)
        bf = b_ref[...]
        cf = c_ref[...]
        bb = _b(bf)
        cc = _b(cf)
        cb = _dot_nt(cc, bb)
        lane = lax.broadcasted_iota(jnp.int32, (ch, LANES), 1)
        rowi = lax.broadcasted_iota(jnp.int32, (ch, 1), 0)
        ones = jnp.ones((ch, LANES), F32)
        dcb = jnp.zeros((ch, ch), F32)
        dc_acc = jnp.zeros((ch, SSM_STATE), F32)
        db_acc = jnp.zeros((ch, SSM_STATE), F32)
        dacs4 = jnp.zeros((ch, LANES), F32)
        ddt4 = jnp.zeros((ch, LANES), F32)
        dd4 = jnp.zeros((1, LANES), F32)
        lane1 = lax.broadcasted_iota(jnp.int32, (1, LANES), 1)
        for j in range(hg):
            sl = slice(j * SSM_HEAD_DIM, (j + 1) * SSM_HEAD_DIM)
            acol = acs[:, j:j + 1]
            arow = acs_t[j:j + 1, :]
            alast = acs[ch - 1:ch, j:j + 1]
            decay = jnp.exp(jnp.where(low, acol - arow, -jnp.inf))
            ea = jnp.exp(acol)
            dsd = jnp.exp(alast - acol)
            cd = jnp.exp(alast)
            dtc = dt4[:, j:j + 1]
            xh = x_ref[:, sl]
            xd = xh * dtc
            xdb = _b(xd)
            hj = hs_ref[0, j]
            hjb = _b(hj)
            dhn = dh_ref[hg * g + j]
            dyj = dy_ref[:, sl]
            dyb = _b(dyj)
            lm = cb * decay
            dxh = dsk4[0:1, j:j + 1] * dyj
            dd4 = jnp.where(lane1 == j, jnp.sum(jnp.sum(dyj * xh, axis=1, keepdims=True), axis=0,
                                                keepdims=True), dd4)
            dlm = _dot_nt(dyb, xdb)
            dxd = _dot_tn(_b(lm), dyb)
            gm = dlm * lm
            dcb = dcb + dlm * decay
            dac = jnp.sum(gm, axis=1, keepdims=True) - _dot_tn_hi(gm, ones)[:, 0:1]
            zz = _dot_nt(cc, hjb)
            dzb = _b(dyj * ea)
            dac = dac + jnp.sum(dyj * zz, axis=1, keepdims=True) * ea
            dc_acc = dc_acc + _dot(dzb, hjb)
            dh_in = _dot_tn(dzb, cc)
            dsb = _b(dhn)
            ww = _dot_nt(bb, dsb)
            dxd = dxd + ww * dsd
            dds = jnp.sum(ww * xd, axis=1, keepdims=True) * dsd
            db_acc = db_acc + _dot(_b(xd * dsd), dsb)
            dac = dac - dds
            dal = (jnp.sum(dds, axis=0, keepdims=True)
                   + jnp.sum(jnp.sum(dhn * hj, axis=1, keepdims=True), axis=0, keepdims=True) * cd)
            dh_ref[hg * g + j] = dh_in + dhn * cd
            dac = dac + jnp.where(rowi == ch - 1, dal, 0.0)
            dacs4 = jnp.where(lane == j, dac, dacs4)
            dx_ref[:, sl] = dxh + dxd * dtc
            ddt4 = jnp.where(lane == j, jnp.sum(dxd * xh, axis=1, keepdims=True), ddt4)
        dcbb = _b(dcb)
        dc_ref[...] = dc_acc + _dot(dcbb, bb)
        db_ref[...] = db_acc + _dot_tn(dcbb, cc)
        ii = lax.broadcasted_iota(jnp.int32, (ch, ch), 0)
        jj = lax.broadcasted_iota(jnp.int32, (ch, ch), 1)
        triu = jnp.where(ii <= jj, 1.0, 0.0)
        dla4 = _dot_hi(triu, dacs4)
        ddt4 = ddt4 + dla4 * a4
        da4 = jnp.sum(dla4 * dt4, axis=0, keepdims=True) * a4
        sel_t = sel.T
        ddt_raw = _dot_hi(ddt4, sel_t) * _sigmoid(xraw)
        ddt_acc[...] += ddt_raw
        st_ref[0:1, :] += _dot_hi(_row8(da4), sel_t)[0:1, :]
        st_ref[1:2, :] += _dot_hi(_row8(dd4), sel_t)[0:1, :]
        st_ref[2:3, :] += jnp.sum(ddt_raw, axis=0, keepdims=True)

        @pl.when(g == SSM_GROUPS - 1)
        def _():
            ddt_ref[...] = _b(ddt_acc[...])

    small = pl.BlockSpec((1, LANES), lambda s, g: (0, 0))
    rc = lambda s: nch - 1 - s
    return pl.pallas_call(
        body, name="ssd_bwd", grid=(nch, SSM_GROUPS),
        in_specs=[pl.BlockSpec((ch, gw), lambda s, g: (rc(s), g)),
                  pl.BlockSpec((ch, SSM_STATE), lambda s, g: (rc(s), b_off + g)),
                  pl.BlockSpec((ch, SSM_STATE), lambda s, g: (rc(s), c_off + g)),
                  pl.BlockSpec((ch, LANES), lambda s, g: (rc(s), 0)),
                  small, small, small,
                  pl.BlockSpec((1, hg, SSM_HEAD_DIM, SSM_STATE), lambda s, g: (rc(s), g, 0, 0)),
                  pl.BlockSpec((ch, gw), lambda s, g: (rc(s), g))],
        out_specs=[pl.BlockSpec((ch, gw), lambda s, g: (rc(s), g)),
                   pl.BlockSpec((ch, SSM_STATE), lambda s, g: (rc(s), g)),
                   pl.BlockSpec((ch, SSM_STATE), lambda s, g: (rc(s), g)),
                   pl.BlockSpec((ch, LANES), lambda s, g: (rc(s), 0)),
                   pl.BlockSpec((8, LANES), lambda s, g: (0, 0))],
        out_shape=[jax.ShapeDtypeStruct((S, SSM_INNER), F32),
                   jax.ShapeDtypeStruct((S, SSM_GROUPS * SSM_STATE), F32),
                   jax.ShapeDtypeStruct((S, SSM_GROUPS * SSM_STATE), F32),
                   jax.ShapeDtypeStruct((S, LANES), BF16),
                   jax.ShapeDtypeStruct((8, LANES), F32)],
        scratch_shapes=[pltpu.VMEM((SSM_HEADS, SSM_HEAD_DIM, SSM_STATE), F32),
                        pltpu.VMEM((ch, LANES), F32)],
        compiler_params=_params(("arbitrary", "arbitrary")),
    )(xact, xact, xact, dt_raw, dt_bias, a_neg, d_skip, hs, dy)


GROUP_W = HEADS_PER_GROUP * SSM_HEAD_DIM
B_COL0 = SSM_INNER
C_COL0 = SSM_INNER + SSM_GROUPS * SSM_STATE


def _ssd_prep(dt_raw, dt_bias, a_neg):
    S = dt_raw.shape[0]
    ch = SSM_CHUNK
    nch = S // ch

    def body(dtr_ref, bias_ref, a_ref, dt_ref, acs_ref, acst_ref, sig_ref):
        x = dtr_ref[...] + bias_ref[...]
        lane = lax.broadcasted_iota(jnp.int32, (ch, LANES), 1)
        dt = jnp.where(lane < SSM_HEADS, _softplus(x), 0.0)
        ii = lax.broadcasted_iota(jnp.int32, (ch, ch), 0)
        jj = lax.broadcasted_iota(jnp.int32, (ch, ch), 1)
        acs = _dot_hi(jnp.where(ii >= jj, 1.0, 0.0), dt * a_ref[...])
        dt_ref[...] = dt
        acs_ref[...] = acs
        acst_ref[0] = acs.T[0:SSM_HEADS, :]
        sig_ref[...] = _sigmoid(x)

    blk = pl.BlockSpec((ch, LANES), lambda c: (c, 0))
    small = pl.BlockSpec((1, LANES), lambda c: (0, 0))
    shp = jax.ShapeDtypeStruct((S, LANES), F32)
    return pl.pallas_call(
        body, name="ssd_prep", grid=(nch,),
        in_specs=[blk, small, small],
        out_specs=[blk, blk, pl.BlockSpec((1, SSM_HEADS, ch), lambda c: (c, 0, 0)), blk],
        out_shape=[shp, shp, jax.ShapeDtypeStruct((nch, SSM_HEADS, ch), F32), shp],
        compiler_params=_params(("parallel",)),
    )(dt_raw, dt_bias, a_neg)


def _expand_heads(arr, g, rows):
    lane = lax.broadcasted_iota(jnp.int32, (rows, GROUP_W), 1) // SSM_HEAD_DIM
    h0 = HEADS_PER_GROUP * g
    out = jnp.broadcast_to(arr[:, h0:h0 + 1], (rows, GROUP_W))
    for j in range(1, HEADS_PER_GROUP):
        out = jnp.where(lane == j, arr[:, h0 + j:h0 + j + 1], out)
    return out


def _seg_matrix(k, lanes_per_head, h0):
    r = lax.broadcasted_iota(jnp.int32, (k, LANES), 0)
    c = lax.broadcasted_iota(jnp.int32, (k, LANES), 1)
    return jnp.where(c == h0 + r // lanes_per_head, 1.0, 0.0).astype(BF16)


def _seg_dot(t, e):
    hi = _b(t)
    lo = _b(t - hi.astype(F32))
    return _dot(hi, e) + _dot(lo, e)


def _head_sums(t, e, rows):
    if rows >= 8:
        return _seg_dot(t, e)
    return _seg_dot(jnp.broadcast_to(t, (8, t.shape[1])), e)[0:rows]


def _pair_masks(x):
    lane = lax.broadcasted_iota(jnp.int32, x.shape, 1)
    zero = jnp.zeros_like(x)
    return jnp.where(lane < SSM_HEAD_DIM, x, zero), jnp.where(lane >= SSM_HEAD_DIM, x, zero)


def _ssd_fwd2(xact, dt, acs, acst, dsk_e):
    S = xact.shape[0]
    ch = SSM_CHUNK
    nch = S // ch

    def body(x_ref, dt_ref, acs_ref, acst_ref, dsk_ref, y_ref, hs_ref, h_ref):
        c = pl.program_id(0)

        @pl.when(c == 0)
        def _():
            h_ref[...] = jnp.zeros_like(h_ref)

        dt_all = dt_ref[...]
        acs_all = acs_ref[...]
        acst_all = acst_ref[0]
        alast = acs_all[ch - 1:ch, :]
        eacs = jnp.exp(acs_all)
        wd_all = dt_all * jnp.exp(alast - acs_all)
        dtt = dt_all.T
        cd_all = jnp.exp(alast)
        ii = lax.broadcasted_iota(jnp.int32, (ch, ch), 0)
        jj = lax.broadcasted_iota(jnp.int32, (ch, ch), 1)
        low = ii >= jj
        for g in range(SSM_GROUPS):
            xs = x_ref[:, g * GROUP_W:(g + 1) * GROUP_W]
            bb = _b(x_ref[:, B_COL0 + g * SSM_STATE:B_COL0 + (g + 1) * SSM_STATE])
            cc = _b(x_ref[:, C_COL0 + g * SSM_STATE:C_COL0 + (g + 1) * SSM_STATE])
            cb = _dot_nt(cc, bb)
            xsb = _b(xs)
            ht = h_ref[g]
            rest = (_dot(cc, _b(ht)) * _expand_heads(eacs, g, ch)
                    + dsk_ref[:, g * GROUP_W:(g + 1) * GROUP_W] * xs)
            for p in range(HEADS_PER_GROUP // 2):
                lms = []
                for h in (HEADS_PER_GROUP * g + 2 * p, HEADS_PER_GROUP * g + 2 * p + 1):
                    diff = acs_all[:, h:h + 1] - acst_all[h:h + 1, :]
                    lms.append(_b(cb * jnp.exp(jnp.where(low, diff, -jnp.inf)) * dtt[h:h + 1, :]))
                xa, xb = _pair_masks(xsb[:, p * LANES:(p + 1) * LANES])
                yp = _dot(jnp.concatenate(lms, axis=1), jnp.concatenate([xa, xb], axis=0))
                y_ref[:, g * GROUP_W + p * LANES:g * GROUP_W + (p + 1) * LANES] = (
                    yp + rest[:, p * LANES:(p + 1) * LANES])
            hs_ref[0, g] = ht
            st = _dot_tn(bb, _b(xs * _expand_heads(wd_all, g, ch)))
            h_ref[g] = ht * _expand_heads(cd_all, g, 1) + st

    blk = pl.BlockSpec((ch, LANES), lambda c: (c, 0))
    return pl.pallas_call(
        body, name="ssd_fwd", grid=(nch,),
        in_specs=[pl.BlockSpec((ch, CONV_DIM), lambda c: (c, 0)), blk, blk,
                  pl.BlockSpec((1, SSM_HEADS, ch), lambda c: (c, 0, 0)),
                  pl.BlockSpec((1, SSM_INNER), lambda c: (0, 0))],
        out_specs=[pl.BlockSpec((ch, SSM_INNER), lambda c: (c, 0)),
                   pl.BlockSpec((1, SSM_GROUPS, SSM_STATE, GROUP_W), lambda c: (c, 0, 0, 0))],
        out_shape=[jax.ShapeDtypeStruct((S, SSM_INNER), F32),
                   jax.ShapeDtypeStruct((nch, SSM_GROUPS, SSM_STATE, GROUP_W), F32)],
        scratch_shapes=[pltpu.VMEM((SSM_GROUPS, SSM_STATE, GROUP_W), F32)],
        compiler_params=_params(("arbitrary",)),
    )(xact, dt, acs, acst, dsk_e)


def _ssd_fwd3(xact, dt, acs, acst, dsk_e):
    S = xact.shape[0]
    ch = SSM_CHUNK
    nch = S // ch
    ng, hg = SSM_GROUPS, HEADS_PER_GROUP
    nbc = SSM_GROUPS * SSM_STATE

    def body(x_ref, dt_ref, acs_ref, acst_ref, dsk_ref, y_ref, hs_ref,
             h_ref, e_ea, xdb_s, xddb_s, bcb_s, cb_s, zz_s, st_s, lmb_s):
        c = pl.program_id(0)

        @pl.when(c == 0)
        def _():
            h_ref[...] = jnp.zeros_like(h_ref)

        dt_all = dt_ref[...]
        acs_all = acs_ref[...]
        alast = acs_all[ch - 1:ch, :]
        eacs = jnp.exp(acs_all)
        dsd_all = jnp.exp(alast - acs_all)
        cd_all = jnp.exp(alast)
        ii = lax.broadcasted_iota(jnp.int32, (ch, ch), 0)
        jj = lax.broadcasted_iota(jnp.int32, (ch, ch), 1)
        low = ii >= jj
        gsl = [slice(g * GROUP_W, (g + 1) * GROUP_W) for g in range(ng)]
        bcb_s[...] = _b(x_ref[:, B_COL0:])
        for g in range(ng):
            xd = x_ref[:, gsl[g]] * _expand_heads(dt_all, g, ch)
            xdb_s[:, gsl[g]] = _b(xd)
            xddb_s[:, gsl[g]] = _b(xd * _expand_heads(dsd_all, g, ch))
            e_ea[:, gsl[g]] = _expand_heads(eacs, g, ch)
        for g in range(ng):
            bb = bcb_s[:, g * SSM_STATE:(g + 1) * SSM_STATE]
            cc = bcb_s[:, nbc + g * SSM_STATE:nbc + (g + 1) * SSM_STATE]
            cb_s[g] = _dot_nt(cc, bb)
            zz_s[:, gsl[g]] = _dot(cc, _b(h_ref[g]))
            st_s[g] = _dot_tn(bb, xddb_s[:, gsl[g]])
        for g in range(ng):
            cb = cb_s[g]
            for j in range(hg):
                h = hg * g + j
                diff = acs_all[:, h:h + 1] - acst_ref[0, h:h + 1, :]
                lmb_s[:, h * ch:(h + 1) * ch] = _b(cb * jnp.exp(jnp.where(low, diff, -jnp.inf)))
            ht = h_ref[g]
            hs_ref[0, g] = ht
            h_ref[g] = ht * _expand_heads(cd_all, g, 1) + st_s[g]
        for g in range(ng):
            for p in range(hg // 2):
                h0 = hg * g + 2 * p
                sl = slice(g * GROUP_W + p * LANES, g * GROUP_W + (p + 1) * LANES)
                yp = _dot(lmb_s[:, h0 * ch:(h0 + 2) * ch], jnp.concatenate(_pair_masks(xdb_s[:, sl]), axis=0))
                y_ref[:, sl] = yp + zz_s[:, sl] * e_ea[:, sl] + dsk_ref[:, sl] * x_ref[:, sl]

    blk = pl.BlockSpec((ch, LANES), lambda c: (c, 0))
    wide = lambda dt_: pltpu.VMEM((ch, SSM_INNER), dt_)
    return pl.pallas_call(
        body, name="ssd_fwd", grid=(nch,),
        in_specs=[pl.BlockSpec((ch, CONV_DIM), lambda c: (c, 0)), blk, blk,
                  pl.BlockSpec((1, SSM_HEADS, ch), lambda c: (c, 0, 0)),
                  pl.BlockSpec((1, SSM_INNER), lambda c: (0, 0))],
        out_specs=[pl.BlockSpec((ch, SSM_INNER), lambda c: (c, 0)),
                   pl.BlockSpec((1, SSM_GROUPS, SSM_STATE, GROUP_W), lambda c: (c, 0, 0, 0))],
        out_shape=[jax.ShapeDtypeStruct((S, SSM_INNER), F32),
                   jax.ShapeDtypeStruct((nch, SSM_GROUPS, SSM_STATE, GROUP_W), F32)],
        scratch_shapes=[pltpu.VMEM((ng, SSM_STATE, GROUP_W), F32), wide(F32), wide(BF16), wide(BF16), wide(BF16),
                        pltpu.VMEM((ng, ch, ch), F32), wide(F32), pltpu.VMEM((ng, SSM_STATE, GROUP_W), F32),
                        pltpu.VMEM((ch, SSM_HEADS * ch), BF16)],
        compiler_params=_params(("arbitrary",)),
    )(xact, dt, acs, acst, dsk_e)


def _ssd_bwd2(xact, dt, acs, acst, sig, a_neg, dsk_e, hs, dy):
    S = xact.shape[0]
    ch = SSM_CHUNK
    nch = S // ch

    def body(x_ref, dt_ref, acs_ref, acst_ref, sig_ref, a_ref, dsk_ref, hs_ref, dy_ref,
             dx_ref, ddt_ref, st_ref, dh_ref, rows_ref):
        step = pl.program_id(0)

        @pl.when(step == 0)
        def _():
            dh_ref[...] = jnp.zeros_like(dh_ref)
            st_ref[...] = jnp.zeros_like(st_ref)
            rows_ref[...] = jnp.zeros_like(rows_ref)

        dt_all = dt_ref[...]
        acs_all = acs_ref[...]
        acst_all = acst_ref[0]
        alast = acs_all[ch - 1:ch, :]
        eacs = jnp.exp(acs_all)
        dsd_all = jnp.exp(alast - acs_all)
        cd_all = jnp.exp(alast)
        ii = lax.broadcasted_iota(jnp.int32, (ch, ch), 0)
        jj = lax.broadcasted_iota(jnp.int32, (ch, ch), 1)
        low = ii >= jj
        lane = lax.broadcasted_iota(jnp.int32, (ch, LANES), 1)
        cols = jnp.zeros((ch, LANES), F32)
        ddt = jnp.zeros((ch, LANES), F32)
        dal = jnp.zeros((1, LANES), F32)
        ddsk = jnp.zeros((1, LANES), F32)
        for g in range(SSM_GROUPS):
            xs = x_ref[:, g * GROUP_W:(g + 1) * GROUP_W]
            bb = _b(x_ref[:, B_COL0 + g * SSM_STATE:B_COL0 + (g + 1) * SSM_STATE])
            cc = _b(x_ref[:, C_COL0 + g * SSM_STATE:C_COL0 + (g + 1) * SSM_STATE])
            cb = _dot_nt(cc, bb)
            dt_e = _expand_heads(dt_all, g, ch)
            ea_e = _expand_heads(eacs, g, ch)
            dsd_e = _expand_heads(dsd_all, g, ch)
            cd_e = _expand_heads(cd_all, g, 1)
            xd = xs * dt_e
            xdb = _b(xd)
            dyg = dy_ref[:, g * GROUP_W:(g + 1) * GROUP_W]
            dyb = _b(dyg)
            ht = hs_ref[0, g]
            htb = _b(ht)
            dhn = dh_ref[g]
            dhnb = _b(dhn)
            zz = _dot(cc, htb)
            dzb = _b(dyg * ea_e)
            d_c = _dot_nt(dzb, htb)
            dh_in = _dot_tn(cc, dzb)
            ww = _dot(bb, dhnb)
            xdd = xd * dsd_e
            d_b = _dot_nt(_b(xdd), dhnb)
            t2 = ww * xdd
            e_g = _seg_matrix(GROUP_W, SSM_HEAD_DIM, HEADS_PER_GROUP * g)
            cols = cols + _head_sums(dyg * zz * ea_e - t2, e_g, ch)
            dal = dal + _head_sums(jnp.sum(t2, axis=0, keepdims=True), e_g, 1) + cd_all * _head_sums(
                jnp.sum(dhn * ht, axis=0, keepdims=True), e_g, 1)
            dh_ref[g] = dh_in + dhn * cd_e
            ddsk = ddsk + _head_sums(jnp.sum(dyg * xs, axis=0, keepdims=True), e_g, 1)
            dxd_rest = ww * dsd_e
            dcb = jnp.zeros((ch, ch), F32)
            for p in range(HEADS_PER_GROUP // 2):
                dya, dyb2 = _pair_masks(dyb[:, p * LANES:(p + 1) * LANES])
                xp = xdb[:, p * LANES:(p + 1) * LANES]
                lms, gms = [], []
                for h, dyh in ((HEADS_PER_GROUP * g + 2 * p, dya), (HEADS_PER_GROUP * g + 2 * p + 1, dyb2)):
                    diff = acs_all[:, h:h + 1] - acst_all[h:h + 1, :]
                    decay = jnp.exp(jnp.where(low, diff, -jnp.inf))
                    lm = cb * decay
                    dlm = _dot_nt(dyh, xp)
                    gm = dlm * lm
                    dcb = dcb + dlm * decay
                    rows_ref[h:h + 1, :] = jnp.sum(gm, axis=0, keepdims=True)
                    lms.append(_b(lm))
                    gms.append(gm)
                h0 = HEADS_PER_GROUP * g + 2 * p
                cols = cols + _head_sums(jnp.concatenate(gms, axis=1), _seg_matrix(2 * ch, ch, h0), ch)
                dxd = _dot_tn(jnp.concatenate(lms, axis=0), jnp.concatenate([dya, dyb2], axis=0))
                dxd = dxd + dxd_rest[:, p * LANES:(p + 1) * LANES]
                sl = slice(g * GROUP_W + p * LANES, g * GROUP_W + (p + 1) * LANES)
                dx_ref[:, sl] = (dsk_ref[:, sl] * dyg[:, p * LANES:(p + 1) * LANES]
                                 + dxd * dt_e[:, p * LANES:(p + 1) * LANES])
                ddt = ddt + _head_sums(dxd * xs[:, p * LANES:(p + 1) * LANES],
                                       _seg_matrix(LANES, SSM_HEAD_DIM, h0), ch)
            dcbb = _b(dcb)
            dx_ref[:, C_COL0 + g * SSM_STATE:C_COL0 + (g + 1) * SSM_STATE] = d_c + _dot(dcbb, bb)
            dx_ref[:, B_COL0 + g * SSM_STATE:B_COL0 + (g + 1) * SSM_STATE] = d_b + _dot_tn(dcbb, cc)
        rowi = lax.broadcasted_iota(jnp.int32, (ch, 1), 0)
        dacs = cols - rows_ref[...].T + jnp.where(rowi == ch - 1, dal, 0.0)
        dla = _dot_hi(jnp.where(ii <= jj, 1.0, 0.0), dacs)
        a_row = a_ref[...]
        ddt_raw = (ddt + dla * a_row) * sig_ref[...]
        ddt_ref[...] = _b(ddt_raw)
        st_ref[0:1, :] += jnp.sum(dla * dt_all, axis=0, keepdims=True) * a_row
        st_ref[1:2, :] += ddsk
        st_ref[2:3, :] += jnp.sum(ddt_raw, axis=0, keepdims=True)

    rc = lambda s: nch - 1 - s
    blk = pl.BlockSpec((ch, LANES), lambda s: (rc(s), 0))
    return pl.pallas_call(
        body, name="ssd_bwd", grid=(nch,),
        in_specs=[pl.BlockSpec((ch, CONV_DIM), lambda s: (rc(s), 0)), blk, blk,
                  pl.BlockSpec((1, SSM_HEADS, ch), lambda s: (rc(s), 0, 0)), blk,
                  pl.BlockSpec((1, LANES), lambda s: (0, 0)),
                  pl.BlockSpec((1, SSM_INNER), lambda s: (0, 0)),
                  pl.BlockSpec((1, SSM_GROUPS, SSM_STATE, GROUP_W), lambda s: (rc(s), 0, 0, 0)),
                  pl.BlockSpec((ch, SSM_INNER), lambda s: (rc(s), 0))],
        out_specs=[pl.BlockSpec((ch, CONV_DIM), lambda s: (rc(s), 0)), blk,
                   pl.BlockSpec((8, LANES), lambda s: (0, 0))],
        out_shape=[jax.ShapeDtypeStruct((S, CONV_DIM), F32), jax.ShapeDtypeStruct((S, LANES), BF16),
                   jax.ShapeDtypeStruct((8, LANES), F32)],
        scratch_shapes=[pltpu.VMEM((SSM_GROUPS, SSM_STATE, GROUP_W), F32), pltpu.VMEM((LANES, ch), F32)],
        compiler_params=_params(("arbitrary",)),
    )(xact, dt, acs, acst, sig, a_neg, dsk_e, hs, dy)


def _ssd_bwd3(xact, dt, acs, acst, sig, a_neg, dsk_e, hs, dy):
    S = xact.shape[0]
    ch = SSM_CHUNK
    nch = S // ch
    ng, hg = SSM_GROUPS, HEADS_PER_GROUP
    nbc = SSM_GROUPS * SSM_STATE

    def body(x_ref, dt_ref, acs_ref, acst_ref, sig_ref, a_ref, dsk_ref, hs_ref, dy_ref,
             dx_ref, ddt_ref, st_ref,
             dh_ref, rows_ref, e_dt, e_ea, e_dsd, xdb_s, xddb_s, dzb_s, bcb_s, cb_s, zz_s, ww_s, dc1_s, db1_s,
             dhin_s, dlm_s, lmb_s, gm_s, dcbb_s, t_s, dxd_s, prod_s, csum_s):
        step = pl.program_id(0)

        @pl.when(step == 0)
        def _():
            dh_ref[...] = jnp.zeros_like(dh_ref)
            st_ref[...] = jnp.zeros_like(st_ref)
            rows_ref[...] = jnp.zeros_like(rows_ref)

        dt_all = dt_ref[...]
        acs_all = acs_ref[...]
        alast = acs_all[ch - 1:ch, :]
        eacs = jnp.exp(acs_all)
        dsd_all = jnp.exp(alast - acs_all)
        cd_all = jnp.exp(alast)
        ii = lax.broadcasted_iota(jnp.int32, (ch, ch), 0)
        jj = lax.broadcasted_iota(jnp.int32, (ch, ch), 1)
        low = ii >= jj
        gsl = [slice(g * GROUP_W, (g + 1) * GROUP_W) for g in range(ng)]
        psl = [[slice(g * GROUP_W + p * LANES, g * GROUP_W + (p + 1) * LANES) for p in range(hg // 2)]
               for g in range(ng)]
        seg = [_seg_matrix(GROUP_W, SSM_HEAD_DIM, hg * g) for g in range(ng)]

        def bc(g):
            return (bcb_s[:, g * SSM_STATE:(g + 1) * SSM_STATE],
                    bcb_s[:, nbc + g * SSM_STATE:nbc + (g + 1) * SSM_STATE])

        def dy_pair(g, p):
            return _pair_masks(_b(dy_ref[:, psl[g][p]]))

        bcb_s[...] = _b(x_ref[:, B_COL0:])
        for g in range(ng):
            dt_e = _expand_heads(dt_all, g, ch)
            ea_e = _expand_heads(eacs, g, ch)
            dsd_e = _expand_heads(dsd_all, g, ch)
            e_dt[:, gsl[g]] = dt_e
            e_ea[:, gsl[g]] = ea_e
            e_dsd[:, gsl[g]] = dsd_e
            xd = x_ref[:, gsl[g]] * dt_e
            xdb_s[:, gsl[g]] = _b(xd)
            xddb_s[:, gsl[g]] = _b(xd * dsd_e)
            dzb_s[:, gsl[g]] = _b(dy_ref[:, gsl[g]] * ea_e)
        for g in range(ng):
            bb, cc = bc(g)
            htb = _b(hs_ref[0, g])
            dhnb = _b(dh_ref[g])
            cb_s[g] = _dot_nt(cc, bb)
            zz_s[:, gsl[g]] = _dot(cc, htb)
            ww_s[:, gsl[g]] = _dot(bb, dhnb)
            dc1_s[g] = _dot_nt(dzb_s[:, gsl[g]], htb)
            db1_s[g] = _dot_nt(xddb_s[:, gsl[g]], dhnb)
            dhin_s[g] = _dot_tn(cc, dzb_s[:, gsl[g]])
            for p in range(hg // 2):
                xp = xdb_s[:, psl[g][p]]
                for q, dyh in enumerate(dy_pair(g, p)):
                    dlm_s[hg * g + 2 * p + q] = _dot_nt(dyh, xp)
        for g in range(ng):
            cb = cb_s[g]
            dcb = jnp.zeros((ch, ch), F32)
            for j in range(hg):
                h = hg * g + j
                diff = acs_all[:, h:h + 1] - acst_ref[0, h:h + 1, :]
                decay = jnp.exp(jnp.where(low, diff, -jnp.inf))
                lm = cb * decay
                dlm = dlm_s[h]
                gm = dlm * lm
                dcb = dcb + dlm * decay
                rows_ref[h:h + 1, :] = jnp.sum(gm, axis=0, keepdims=True)
                lmb_s[h * ch:(h + 1) * ch, :] = _b(lm)
                gm_s[:, h * ch:(h + 1) * ch] = gm
            dcbb_s[g] = _b(dcb)
            xs = x_ref[:, gsl[g]]
            dyg = dy_ref[:, gsl[g]]
            ww = ww_s[:, gsl[g]]
            dsd_e = e_dsd[:, gsl[g]]
            t2 = ww * (xs * e_dt[:, gsl[g]] * dsd_e)
            t_s[:, gsl[g]] = dyg * zz_s[:, gsl[g]] * e_ea[:, gsl[g]] - t2
            dhn = dh_ref[g]
            csum_s[0:1, gsl[g]] = jnp.sum(t2, axis=0, keepdims=True)
            csum_s[1:2, gsl[g]] = jnp.sum(dhn * hs_ref[0, g], axis=0, keepdims=True)
            csum_s[2:3, gsl[g]] = jnp.sum(dyg * xs, axis=0, keepdims=True)
            dh_ref[g] = dhin_s[g] + dhn * _expand_heads(cd_all, g, 1)
            dxd_s[:, gsl[g]] = ww * dsd_e
        cols = jnp.zeros((ch, LANES), F32)
        for g in range(ng):
            bb, cc = bc(g)
            dcbb = dcbb_s[g]
            dx_ref[:, C_COL0 + g * SSM_STATE:C_COL0 + (g + 1) * SSM_STATE] = dc1_s[g] + _dot(dcbb, bb)
            dx_ref[:, B_COL0 + g * SSM_STATE:B_COL0 + (g + 1) * SSM_STATE] = db1_s[g] + _dot_tn(dcbb, cc)
            cols = cols + _head_sums(t_s[:, gsl[g]], seg[g], ch)
            for p in range(hg // 2):
                h0 = hg * g + 2 * p
                dxd_s[:, psl[g][p]] += _dot_tn(lmb_s[h0 * ch:(h0 + 2) * ch, :],
                                               jnp.concatenate(dy_pair(g, p), axis=0))
                cols = cols + _head_sums(gm_s[:, h0 * ch:(h0 + 2) * ch], _seg_matrix(2 * ch, ch, h0), ch)
        for g in range(ng):
            dxd = dxd_s[:, gsl[g]]
            xs = x_ref[:, gsl[g]]
            dx_ref[:, gsl[g]] = dsk_ref[:, gsl[g]] * dy_ref[:, gsl[g]] + dxd * e_dt[:, gsl[g]]
            prod_s[:, gsl[g]] = dxd * xs
        ddt = jnp.zeros((ch, LANES), F32)
        dal = jnp.zeros((1, LANES), F32)
        ddsk = jnp.zeros((1, LANES), F32)
        for g in range(ng):
            ddt = ddt + _head_sums(prod_s[:, gsl[g]], seg[g], ch)
            dal = (dal + _head_sums(csum_s[0:1, gsl[g]], seg[g], 1)
                   + cd_all * _head_sums(csum_s[1:2, gsl[g]], seg[g], 1))
            ddsk = ddsk + _head_sums(csum_s[2:3, gsl[g]], seg[g], 1)
        rowi = lax.broadcasted_iota(jnp.int32, (ch, 1), 0)
        dacs = cols - rows_ref[...].T + jnp.where(rowi == ch - 1, dal, 0.0)
        dla = _dot_hi(jnp.where(ii <= jj, 1.0, 0.0), dacs)
        a_row = a_ref[...]
        ddt_raw = (ddt + dla * a_row) * sig_ref[...]
        ddt_ref[...] = _b(ddt_raw)
        st_ref[0:1, :] += jnp.sum(dla * dt_all, axis=0, keepdims=True) * a_row
        st_ref[1:2, :] += ddsk
        st_ref[2:3, :] += jnp.sum(ddt_raw, axis=0, keepdims=True)

    rc = lambda s: nch - 1 - s
    blk = pl.BlockSpec((ch, LANES), lambda s: (rc(s), 0))
    wide = lambda dt_: pltpu.VMEM((ch, SSM_INNER), dt_)
    sq = lambda n, dt_: pltpu.VMEM((n, ch, ch), dt_)
    scratch = [pltpu.VMEM((ng, SSM_STATE, GROUP_W), F32), pltpu.VMEM((LANES, ch), F32),
               wide(F32), wide(F32), wide(F32),
               wide(BF16), wide(BF16), wide(BF16), wide(BF16),
               sq(ng, F32), wide(F32), wide(F32), sq(ng, F32), sq(ng, F32),
               pltpu.VMEM((ng, SSM_STATE, GROUP_W), F32),
               sq(SSM_HEADS, F32),
               pltpu.VMEM((SSM_HEADS * ch, ch), BF16),
               pltpu.VMEM((ch, SSM_HEADS * ch), F32),
               sq(ng, BF16), wide(F32), wide(F32), wide(F32),
               pltpu.VMEM((8, SSM_INNER), F32)]
    return pl.pallas_call(
        body, name="ssd_bwd", grid=(nch,),
        in_specs=[pl.BlockSpec((ch, CONV_DIM), lambda s: (rc(s), 0)), blk, blk,
                  pl.BlockSpec((1, SSM_HEADS, ch), lambda s: (rc(s), 0, 0)), blk,
                  pl.BlockSpec((1, LANES), lambda s: (0, 0)),
                  pl.BlockSpec((1, SSM_INNER), lambda s: (0, 0)),
                  pl.BlockSpec((1, SSM_GROUPS, SSM_STATE, GROUP_W), lambda s: (rc(s), 0, 0, 0)),
                  pl.BlockSpec((ch, SSM_INNER), lambda s: (rc(s), 0))],
        out_specs=[pl.BlockSpec((ch, CONV_DIM), lambda s: (rc(s), 0)), blk,
                   pl.BlockSpec((8, LANES), lambda s: (0, 0))],
        out_shape=[jax.ShapeDtypeStruct((S, CONV_DIM), F32), jax.ShapeDtypeStruct((S, LANES), BF16),
                   jax.ShapeDtypeStruct((8, LANES), F32)],
        scratch_shapes=scratch,
        compiler_params=_params(("arbitrary",)),
    )(xact, dt, acs, acst, sig, a_neg, dsk_e, hs, dy)


def _pad_lanes(v, n=LANES):
    return jnp.pad(v, ((0, 0), (0, n - v.shape[1])))


def _local_step(x, target, w, ex=None):
    offs = np.cumsum((0,) + IN_SPLITS)
    wt_in = w["w_in_t"]
    w_qkv = wt_in[offs[0]:offs[3]]
    w_z = wt_in[offs[3]:offs[4]]
    w_xbc = wt_in[offs[4]:offs[5]]
    w_dt = jnp.pad(wt_in[offs[5]:offs[6]], ((0, LANES - SSM_HEADS), (0, 0)))
    w_g = wt_in[offs[6]:offs[7]]
    dt_bias = _pad_lanes(w["dt_bias"])
    a_neg = _pad_lanes(-jnp.exp(w["a_log"]))
    d_skip = _pad_lanes(w["d_skip"])

    u = _rms_fwd(x, w["norm_mix_pre_w"])
    if ex is None:
        xbc = _mm_nn(u, w_xbc, F32, "proj_xbc", tb=True)
    else:
        xbc, got = _mm_nn(u, w_xbc, F32, "proj_xbc", comm=_gather_comm([ex.mine[REST_EARLY]]), tb=True)
        w = {**w, **ex.rest_weights(got[0], REST_EARLY)}
    qkv = _mm_nn(u, w_qkv, F32, "proj_qkv", tb=True)
    z = _mm_nn(u, w_z, F32, "proj_z", tb=True)
    dt_raw = _mm_nn(u, w_dt, F32, "proj_dt", tb=True)
    gl = _mm_nn(u, w_g, F32, "proj_gate", tb=True)

    pats = _qkv_layouts(qkv)
    os_, ms_, ls_ = [], [], []
    for d, qkv_p in zip(DILATIONS, pats):
        if ex is not None and d == DILATIONS[0]:
            o, m, l, got = _attn_fwd2(qkv_p, d, comm=_gather_comm([ex.mine[REST_LATE]]))
            w = {**w, **ex.rest_weights(got[0], REST_LATE)}
        else:
            o, m, l = _attn_fwd2(qkv_p, d)
        os_.append(o)
        ms_.append(m)
        ls_.append(l)
    att, lse = _attn_combine2(os_, ms_, ls_)
    att_o = _mm_nn(att, w["w_att_proj"], F32, "att_proj")

    xact, conv_pre = _conv_fwd2(xbc, w["conv_w"], w["conv_b"])
    dsk_e = jnp.repeat(w["d_skip"], SSM_HEAD_DIM, axis=1)
    dt, acs, acst, sig = _ssd_prep(dt_raw, dt_bias, a_neg)
    y_ssd, hs = _ssd_fwd2(xact, dt, acs, acst, dsk_e)
    ssm_y = _gnorm_fwd(y_ssd, z, w["ssm_norm_w"])
    ssm_o = _mm_nn(ssm_y, w["w_ssm_proj"], F32, "ssm_proj")

    mi = _gate_fwd(att_o, ssm_o, gl, w["b_gate"])
    mixed = _mm_nn(mi, w["w_out"], F32, "out_proj")
    h1, f = _post_pre(x, mixed, w["norm_mix_post_w"], w["norm_ffn_pre_w"])
    r_up, act = _mm_nn(f, w["w_up"], BF16, "ffn_up", mode="relu2")
    (dh2, d_down, loss, g_ffn_post), _ = _mm_epi(
        act, w["w_down"], _final_epi(h1, target, w["norm_ffn_post_w"]), "ffn_down")

    g = {"norm_ffn_post_w": g_ffn_post}
    g["w_down"] = _mm_tn(act, d_down, "dw_down")
    dup = _mm_nn(d_down, w["w_down"], BF16, "d_act", mode="mul2", extra=r_up, tb=True)
    g["w_up"] = _mm_tn(f, dup, "dw_up")
    (dh1, d_mixed, g["norm_ffn_pre_w"], g["norm_mix_post_w"]), _ = _mm_epi(
        dup, w["w_up"], _mid_epi(dh2, h1, mixed, w["norm_ffn_pre_w"], w["norm_mix_post_w"]), "d_f", tb=True)
    g["w_out"] = _mm_tn(mi, d_mixed, "dw_out")
    (d_att_o, d_ssm_o, dgl, g["b_gate"]), _ = _mm_epi(
        d_mixed, w["w_out"], _gate_epi(att_o, ssm_o, gl, w["b_gate"]), "d_mi", tb=True)

    g["w_att_proj"] = _mm_tn(att, d_att_o, "dw_att_proj")
    g["w_ssm_proj"] = _mm_tn(ssm_y, d_ssm_o, "dw_ssm_proj")
    gn_epi = _gnorm_epi(y_ssd, z, w["ssm_norm_w"])
    if ex is None:
        (dy_ssd, dz, g["ssm_norm_w"]), _ = _mm_epi(d_ssm_o, w["w_ssm_proj"], gn_epi, "d_ssm_y", tb=True,
                                                    tn=PACK_COLS)
    else:
        gs_rest = jnp.concatenate(
            [_shards_from_full(n, g[n]).reshape(N_CHIPS, -1, PACK_COLS) for n in REST], axis=1)
        (dy_ssd, dz, g["ssm_norm_w"]), recv = _mm_epi(d_ssm_o, w["w_ssm_proj"], gn_epi, "d_ssm_y", tb=True,
                                                       tn=PACK_COLS, comm=_pair_comm([gs_rest]))
        p_rest = _pair_add2(gs_rest, recv[0], ex.c_arr, "rs_pair_add_rest")

    d_att = _mm_nn(d_att_o, w["w_att_proj"], F32, "d_att", tb=True)
    dqs, dks, dvs = [], [], []
    for d, qkv_p, (do_p, lse_p, delta_p) in zip(DILATIONS, pats, _attn_delta2(d_att, att, lse)):
        if ex is not None and d == DILATIONS[0]:
            dq, dk, dv, recv3 = _attn_bwd2(qkv_p, do_p, lse_p, delta_p, d, comm=_chip_comm([p_rest]))
            q_rest = _chip_add2(p_rest, recv3[0], ex.chip_arr, "rs_chip_add_rest")
            ex.finish_reduce("rest", q_rest, _comm_call("rs_share_rest", _share_comm([q_rest]))[0])
        else:
            dq, dk, dv = _attn_bwd2(qkv_p, do_p, lse_p, delta_p, d)
        dqs.append(dq)
        dks.append(dk)
        dvs.append(dv)
    dqkv = _sum_qkv2(dqs, dks, dvs)

    dxact, ddt_raw, stats = _ssd_bwd3(xact, dt, acs, acst, sig, a_neg, dsk_e, hs, dy_ssd)
    g["a_log"] = stats[0:1, :SSM_HEADS]
    g["d_skip"] = stats[1:2, :SSM_HEADS]
    g["dt_bias"] = stats[2:3, :SSM_HEADS]
    dxbc, g["conv_w"], g["conv_b"] = _conv_bwd2(xbc, conv_pre, dxact, w["conv_w"])

    pieces = [(dqkv, w_qkv), (dz, w_z), (dxbc, w_xbc), (ddt_raw, w_dt), (dgl, w_g)]
    gw = [_mm_tn(dp, u, f"dw_in_{i}") for i, (dp, _) in enumerate(pieces)]
    gw[3] = gw[3][:SSM_HEADS]
    if ex is None:
        g["w_in_t"] = jnp.concatenate(gw, axis=0)
    du = None
    for i, (dp, wp) in enumerate(pieces):
        if ex is not None and i == 0:
            gs_in = _rows_to_shards(gw, IN_SHARD_ROWS, IN_SHARD_PAD)
            du, recv = _mm_nn(dp, wp, F32, f"d_u_{i}", acc=du, comm=_pair_comm([gs_in]))
            p_in = _pair_add2(gs_in, recv[0], ex.c_arr, "rs_pair_add_in")
            rows = p_in.shape[1] // 2
            p_parts = [p_in[:, :rows], p_in[:, rows:]]
            q_parts = []
        elif ex is not None and i in (1, 2):
            p_part = p_parts[i - 1]
            du, recv3 = _mm_nn(dp, wp, F32, f"d_u_{i}", acc=du, comm=_chip_comm([p_part]))
            q_parts.append(_chip_add2(p_part, recv3[0], ex.chip_arr, f"rs_chip_add_in_{i}"))
            if i == 2:
                others = _comm_call("rs_share_in", _share_comm(q_parts))
                ex.finish_reduce("w_in", jnp.concatenate(q_parts, axis=0), jnp.concatenate(others, axis=0))
        elif i == len(pieces) - 1:
            (grad_x, g["norm_mix_pre_w"]), _ = _mm_epi(
                dp, wp, _first_epi(du, dh1, x, w["norm_mix_pre_w"]), f"d_u_{i}")
        else:
            du = _mm_nn(dp, wp, F32, f"d_u_{i}", acc=du)
    return loss, grad_x, g


def _rows_to_shards(pieces, shard_rows, pad_rows):
    cols = pieces[0].shape[1]
    shards = []
    for s in range(N_CHIPS):
        lo, hi = s * shard_rows, (s + 1) * shard_rows
        parts, r0 = [], 0
        for p in pieces:
            a, b = max(lo, r0), min(hi, r0 + p.shape[0])
            if a < b:
                parts.append(p[a - r0:b - r0])
            r0 += p.shape[0]
        parts.append(jnp.zeros((pad_rows - shard_rows, cols), pieces[0].dtype))
        shards.append(jnp.concatenate(parts, axis=0))
    return jnp.stack(shards)


BIG = ("w_in", "w_att_proj", "w_ssm_proj", "w_out", "w_up", "w_down")
BIG_FULL_SHAPES = {"w_in": (D_MODEL, IN_PROJ_WIDTH), "w_att_proj": (ATT_WIDTH, D_MODEL),
                   "w_ssm_proj": (SSM_INNER, D_MODEL), "w_out": (D_MODEL, D_MODEL),
                   "w_up": (D_MODEL, FFN_HIDDEN), "w_down": (FFN_HIDDEN, D_MODEL)}
BIG_COL_SHARDED = {"w_in": True, "w_att_proj": True, "w_ssm_proj": False, "w_out": False, "w_up": True,
                   "w_down": False}
PACK_COLS = 1024
PACK_ROWS = 5760
PACK_HALF = PACK_ROWS // 2
PACK_BLOCK = 576
SMALL = ("norm_mix_pre_w", "b_gate", "conv_b", "dt_bias", "a_log", "d_skip", "ssm_norm_w",
         "norm_mix_post_w", "norm_ffn_pre_w", "norm_ffn_post_w")
SMALL_ROWS = 232


def _shard_shape(name):
    r, c = BIG_FULL_SHAPES[name]
    return (r, c // N_CHIPS) if BIG_COL_SHARDED[name] else (r // N_CHIPS, c)


def _pack(shards, dtype):
    flat = [shards[n].astype(dtype).reshape(-1, PACK_COLS) for n in BIG]
    rows = sum(f.shape[0] for f in flat)
    flat.append(jnp.zeros((PACK_ROWS - rows, PACK_COLS), dtype))
    return jnp.concatenate(flat, axis=0)


def _unpack(packed):
    out, r0 = {}, 0
    for n in BIG:
        shp = _shard_shape(n)
        rows = shp[0] * shp[1] // PACK_COLS
        out[n] = packed[r0:r0 + rows].reshape(shp)
        r0 += rows
    return out


def _unpack_full(gathered):
    out, r0 = {}, 0
    for n in BIG:
        shp = _shard_shape(n)
        rows = shp[0] * shp[1] // PACK_COLS
        sh = gathered[:, r0:r0 + rows].reshape((N_CHIPS,) + shp)
        if BIG_COL_SHARDED[n]:
            out[n] = sh.transpose(1, 0, 2).reshape(BIG_FULL_SHAPES[n])
        else:
            out[n] = sh.reshape(BIG_FULL_SHAPES[n])
        r0 += rows
    return out


def _pack_full(grads):
    parts = []
    rows_total = 0
    for n in BIG:
        shp = _shard_shape(n)
        gfull = grads[n]
        if BIG_COL_SHARDED[n]:
            sh = gfull.reshape(shp[0], N_CHIPS, shp[1]).transpose(1, 0, 2)
        else:
            sh = gfull.reshape((N_CHIPS,) + shp)
        parts.append(sh.reshape(N_CHIPS, -1, PACK_COLS))
        rows_total += parts[-1].shape[1]
    parts.append(jnp.zeros((N_CHIPS, PACK_ROWS - rows_total, PACK_COLS), F32))
    return jnp.concatenate(parts, axis=1)


def _mesh_pos():
    return lax.axis_index("x"), lax.axis_index("y"), lax.axis_index("c")


def _other_chips(x, y):
    return [(1 - x, y), (x, 1 - y), (1 - x, 1 - y)]


ANY = pl.BlockSpec(memory_space=pl.ANY)


def _allgather_packed(wpack):
    half = PACK_HALF

    def body(w_ref, out_ref, send_sems, recv_sems):
        x, y, c = _mesh_pos()
        me = 2 * x + y
        sibling = (x, y, 1 - c)
        chips = _other_chips(x, y)

        def rows(chip, h):
            return out_ref.at[chip, pl.ds(h * half, half), :]

        def copy(k, chip, h, to, src=None):
            return pltpu.make_async_remote_copy(
                src_ref=rows(chip, h) if src is None else src, dst_ref=rows(chip, h),
                send_sem=send_sems.at[k], recv_sem=recv_sems.at[k], device_id=to, device_id_type=MESH)

        mine_half = w_ref.at[pl.ds(c * half, half), :]
        first = [copy(j, me, c, (*chip, c), src=mine_half) for j, chip in enumerate(chips)]
        for cp in first:
            cp.start()
        passed = [copy(3 + j, 2 * chip[0] + chip[1], c, sibling) for j, chip in enumerate(chips)]
        for j, chip in enumerate(chips):
            copy(j, 2 * chip[0] + chip[1], c, (x, y, c)).wait_recv()
            passed[j].start()
        for j, chip in enumerate(chips):
            copy(3 + j, 2 * chip[0] + chip[1], 1 - c, (x, y, c)).wait_recv()
        for cp in first + passed:
            cp.wait_send()

    return pl.pallas_call(
        body, name="allgather_weights",
        out_shape=jax.ShapeDtypeStruct((N_CHIPS,) + wpack.shape, wpack.dtype),
        in_specs=[ANY], out_specs=ANY,
        scratch_shapes=[pltpu.SemaphoreType.DMA((6,)), pltpu.SemaphoreType.DMA((6,))],
        compiler_params=pltpu.CompilerParams(has_side_effects=True),
    )(wpack)


def _exchange_halves(gpack):
    half = PACK_HALF

    def body(g_ref, out_ref, send_sem, recv_sem):
        x, y, c = _mesh_pos()
        cp = pltpu.make_async_remote_copy(
            src_ref=g_ref.at[:, pl.ds((1 - c) * half, half), :], dst_ref=out_ref,
            send_sem=send_sem, recv_sem=recv_sem, device_id=(x, y, 1 - c), device_id_type=MESH)
        cp.start()
        cp.wait()

    return pl.pallas_call(
        body, name="rs_pair_exchange",
        out_shape=jax.ShapeDtypeStruct((N_CHIPS, half, PACK_COLS), F32),
        in_specs=[ANY], out_specs=ANY,
        scratch_shapes=[pltpu.SemaphoreType.DMA, pltpu.SemaphoreType.DMA],
        compiler_params=pltpu.CompilerParams(has_side_effects=True),
    )(gpack)


def _pair_add(gpack, recv, c_idx):
    nb = PACK_HALF // PACK_BLOCK

    def body(c_ref, g_ref, r_ref, o_ref):
        o_ref[...] = _b(g_ref[...] + r_ref[...])

    blk = (1, PACK_BLOCK, PACK_COLS)
    return pl.pallas_call(
        body, name="rs_pair_add",
        grid_spec=pltpu.PrefetchScalarGridSpec(
            num_scalar_prefetch=1, grid=(N_CHIPS, nb),
            in_specs=[pl.BlockSpec(blk, lambda s, i, c: (s, c[0] * nb + i, 0)),
                      pl.BlockSpec(blk, lambda s, i, c: (s, i, 0))],
            out_specs=pl.BlockSpec(blk, lambda s, i, c: (s, i, 0))),
        out_shape=jax.ShapeDtypeStruct((N_CHIPS, PACK_HALF, PACK_COLS), BF16),
        compiler_params=_params(("arbitrary", "arbitrary")),
    )(c_idx, gpack, recv)


def _exchange_chips(ppack):
    def body(p_ref, out_ref, send_sems, recv_sems):
        x, y, c = _mesh_pos()
        chips = _other_chips(x, y)
        cps = [pltpu.make_async_remote_copy(
            src_ref=p_ref.at[2 * chip[0] + chip[1]], dst_ref=out_ref.at[j],
            send_sem=send_sems.at[j], recv_sem=recv_sems.at[j], device_id=(*chip, c), device_id_type=MESH)
            for j, chip in enumerate(chips)]
        for cp in cps:
            cp.start()
        for cp in cps:
            cp.wait_recv()
        for cp in cps:
            cp.wait_send()

    return pl.pallas_call(
        body, name="rs_chip_exchange",
        out_shape=jax.ShapeDtypeStruct((N_CHIPS - 1, PACK_HALF, PACK_COLS), ppack.dtype),
        in_specs=[ANY], out_specs=ANY,
        scratch_shapes=[pltpu.SemaphoreType.DMA((3,)), pltpu.SemaphoreType.DMA((3,))],
        compiler_params=pltpu.CompilerParams(has_side_effects=True),
    )(ppack)


def _chip_add(ppack, recv, me_idx):
    nb = PACK_HALF // PACK_BLOCK

    def body(m_ref, p_ref, r0_ref, r1_ref, r2_ref, o_ref):
        o_ref[...] = ((p_ref[0].astype(F32) + r0_ref[0].astype(F32)) + r1_ref[0].astype(F32)) + r2_ref[0].astype(F32)

    blk = (1, PACK_BLOCK, PACK_COLS)
    return pl.pallas_call(
        body, name="rs_chip_add",
        grid_spec=pltpu.PrefetchScalarGridSpec(
            num_scalar_prefetch=1, grid=(nb,),
            in_specs=[pl.BlockSpec(blk, lambda i, m: (m[0], i, 0)),
                      pl.BlockSpec(blk, lambda i, m: (0, i, 0)),
                      pl.BlockSpec(blk, lambda i, m: (1, i, 0)),
                      pl.BlockSpec(blk, lambda i, m: (2, i, 0))],
            out_specs=pl.BlockSpec((PACK_BLOCK, PACK_COLS), lambda i, m: (i, 0))),
        out_shape=jax.ShapeDtypeStruct((PACK_HALF, PACK_COLS), F32),
        compiler_params=_params(("arbitrary",)),
    )(me_idx, ppack, recv, recv, recv)


def _share_halves(qhalf):
    def body(q_ref, out_ref, send_sem, recv_sem):
        x, y, c = _mesh_pos()
        cp = pltpu.make_async_remote_copy(
            src_ref=q_ref, dst_ref=out_ref, send_sem=send_sem, recv_sem=recv_sem,
            device_id=(x, y, 1 - c), device_id_type=MESH)
        cp.start()
        cp.wait()

    return pl.pallas_call(
        body, name="rs_share_halves",
        out_shape=jax.ShapeDtypeStruct(qhalf.shape, F32),
        in_specs=[ANY], out_specs=ANY,
        scratch_shapes=[pltpu.SemaphoreType.DMA, pltpu.SemaphoreType.DMA],
        compiler_params=pltpu.CompilerParams(has_side_effects=True),
    )(qhalf)


REST_EARLY = ("w_att_proj", "w_ssm_proj", "w_out")
REST_LATE = ("w_up", "w_down")
REST = REST_EARLY + REST_LATE
ADD_ROWS_CAP = 800
BF16_ROWS = 16
IN_SHARD_ROWS = IN_PROJ_WIDTH // N_CHIPS
IN_SHARD_PAD = 2688


def _stack_rest(shards, dtype, names=REST):
    return jnp.concatenate([shards[n].astype(dtype).reshape(-1, PACK_COLS) for n in names], axis=0)


def _unstack_rest(stacked, lead=(), names=REST):
    out, r0 = {}, 0
    for n in names:
        shp = _shard_shape(n)
        rows = shp[0] * shp[1] // PACK_COLS
        out[n] = stacked[..., r0:r0 + rows, :].reshape(lead + shp)
        r0 += rows
    return out


def _full_from_shards(name, sh):
    if BIG_COL_SHARDED[name]:
        return sh.transpose(1, 0, 2).reshape(BIG_FULL_SHAPES[name])
    return sh.reshape(BIG_FULL_SHAPES[name])


def _shards_from_full(name, full):
    shp = _shard_shape(name)
    if BIG_COL_SHARDED[name]:
        return full.reshape(shp[0], N_CHIPS, shp[1]).transpose(1, 0, 2)
    return full.reshape((N_CHIPS,) + shp)


def _allgather2(shards):
    n = len(shards)

    def body(*refs):
        w_refs, out_refs, send_sems, recv_sems = refs[:n], refs[n:2 * n], refs[2 * n], refs[2 * n + 1]
        x, y, c = _mesh_pos()
        me = 2 * x + y
        sibling = (x, y, 1 - c)
        chips = _other_chips(x, y)
        plans = []
        for a, (w_ref, out_ref) in enumerate(zip(w_refs, out_refs)):
            half = w_ref.shape[0] // 2

            def copy(k, chip, h, to, src=None, out_ref=out_ref, half=half, a=a):
                rows = out_ref.at[chip, pl.ds(h * half, half), :]
                return pltpu.make_async_remote_copy(
                    src_ref=rows if src is None else src, dst_ref=rows,
                    send_sem=send_sems.at[6 * a + k], recv_sem=recv_sems.at[6 * a + k],
                    device_id=to, device_id_type=MESH)

            mine_half = w_ref.at[pl.ds(c * half, half), :]
            idx = [2 * chip[0] + chip[1] for chip in chips]
            send = [copy(j, me, c, (*chip, c), src=mine_half) for j, chip in enumerate(chips)]
            land = [copy(j, idx[j], c, (x, y, c)) for j in range(N_CHIPS - 1)]
            forward = [copy(3 + j, idx[j], c, sibling) for j in range(N_CHIPS - 1)]
            land_fw = [copy(3 + j, idx[j], 1 - c, (x, y, c)) for j in range(N_CHIPS - 1)]
            plans.append((send, land, forward, land_fw))
        for send, _, _, _ in plans:
            for cp in send:
                cp.start()
        for _, land, forward, _ in plans:
            for j in range(N_CHIPS - 1):
                land[j].wait_recv()
                forward[j].start()
        for _, _, _, land_fw in plans:
            for cp in land_fw:
                cp.wait_recv()
        for send, _, forward, _ in plans:
            for cp in send + forward:
                cp.wait_send()

    return pl.pallas_call(
        body, name="allgather_weights",
        out_shape=[jax.ShapeDtypeStruct((N_CHIPS,) + s.shape, s.dtype) for s in shards],
        in_specs=[ANY] * n, out_specs=[ANY] * n,
        scratch_shapes=[pltpu.SemaphoreType.DMA((6 * n,)), pltpu.SemaphoreType.DMA((6 * n,))],
        compiler_params=pltpu.CompilerParams(has_side_effects=True),
    )(*shards)


def _exchange_halves2(gs):
    n = len(gs)

    def body(*refs):
        g_refs, out_refs, send_sems, recv_sems = refs[:n], refs[n:2 * n], refs[2 * n], refs[2 * n + 1]
        x, y, c = _mesh_pos()
        cps = []
        for a, (g_ref, out_ref) in enumerate(zip(g_refs, out_refs)):
            half = g_ref.shape[1] // 2
            cps.append(pltpu.make_async_remote_copy(
                src_ref=g_ref.at[:, pl.ds((1 - c) * half, half), :], dst_ref=out_ref,
                send_sem=send_sems.at[a], recv_sem=recv_sems.at[a], device_id=(x, y, 1 - c),
                device_id_type=MESH))
        for cp in cps:
            cp.start()
        for cp in cps:
            cp.wait()

    return pl.pallas_call(
        body, name="rs_pair_exchange",
        out_shape=[jax.ShapeDtypeStruct((N_CHIPS, g.shape[1] // 2, g.shape[2]), F32) for g in gs],
        in_specs=[ANY] * n, out_specs=[ANY] * n,
        scratch_shapes=[pltpu.SemaphoreType.DMA((n,)), pltpu.SemaphoreType.DMA((n,))],
        compiler_params=pltpu.CompilerParams(has_side_effects=True),
    )(*gs)


def _pair_add2(g, recv, c_idx, name):
    _, half, cols = recv.shape
    rb = _row_block(half, ADD_ROWS_CAP, BF16_ROWS)
    nb = half // rb

    def body(c_ref, g_ref, r_ref, o_ref):
        o_ref[...] = _b(g_ref[...] + r_ref[...])

    blk = (1, rb, cols)
    return pl.pallas_call(
        body, name=name,
        grid_spec=pltpu.PrefetchScalarGridSpec(
            num_scalar_prefetch=1, grid=(N_CHIPS, nb),
            in_specs=[pl.BlockSpec(blk, lambda s, i, c: (s, c[0] * nb + i, 0)),
                      pl.BlockSpec(blk, lambda s, i, c: (s, i, 0))],
            out_specs=pl.BlockSpec(blk, lambda s, i, c: (s, i, 0))),
        out_shape=jax.ShapeDtypeStruct(recv.shape, BF16),
        compiler_params=_params(("arbitrary", "arbitrary")),
    )(c_idx, g, recv)


def _exchange_chips2(ps):
    n = len(ps)

    def body(*refs):
        p_refs, out_refs, send_sems, recv_sems = refs[:n], refs[n:2 * n], refs[2 * n], refs[2 * n + 1]
        x, y, c = _mesh_pos()
        chips = _other_chips(x, y)
        cps = [pltpu.make_async_remote_copy(
            src_ref=p_ref.at[2 * chip[0] + chip[1]], dst_ref=out_ref.at[j],
            send_sem=send_sems.at[3 * a + j], recv_sem=recv_sems.at[3 * a + j], device_id=(*chip, c),
            device_id_type=MESH)
            for a, (p_ref, out_ref) in enumerate(zip(p_refs, out_refs)) for j, chip in enumerate(chips)]
        for cp in cps:
            cp.start()
        for cp in cps:
            cp.wait_recv()
        for cp in cps:
            cp.wait_send()

    return pl.pallas_call(
        body, name="rs_chip_exchange",
        out_shape=[jax.ShapeDtypeStruct((N_CHIPS - 1,) + p.shape[1:], p.dtype) for p in ps],
        in_specs=[ANY] * n, out_specs=[ANY] * n,
        scratch_shapes=[pltpu.SemaphoreType.DMA((3 * n,)), pltpu.SemaphoreType.DMA((3 * n,))],
        compiler_params=pltpu.CompilerParams(has_side_effects=True),
    )(*ps)


def _chip_add2(p, recv, me_idx, name):
    _, half, cols = recv.shape
    rb = _row_block(half, ADD_ROWS_CAP, BF16_ROWS)

    def body(m_ref, p_ref, r0_ref, r1_ref, r2_ref, o_ref):
        o_ref[...] = ((p_ref[0].astype(F32) + r0_ref[0].astype(F32)) + r1_ref[0].astype(F32)) + r2_ref[0].astype(F32)

    blk = (1, rb, cols)
    return pl.pallas_call(
        body, name=name,
        grid_spec=pltpu.PrefetchScalarGridSpec(
            num_scalar_prefetch=1, grid=(half // rb,),
            in_specs=[pl.BlockSpec(blk, lambda i, m: (m[0], i, 0)),
                      pl.BlockSpec(blk, lambda i, m: (0, i, 0)),
                      pl.BlockSpec(blk, lambda i, m: (1, i, 0)),
                      pl.BlockSpec(blk, lambda i, m: (2, i, 0))],
            out_specs=pl.BlockSpec((rb, cols), lambda i, m: (i, 0))),
        out_shape=jax.ShapeDtypeStruct((half, cols), F32),
        compiler_params=_params(("arbitrary",)),
    )(me_idx, p, recv, recv, recv)


def _share_halves2(qs):
    n = len(qs)

    def body(*refs):
        q_refs, out_refs, send_sems, recv_sems = refs[:n], refs[n:2 * n], refs[2 * n], refs[2 * n + 1]
        x, y, c = _mesh_pos()
        cps = [pltpu.make_async_remote_copy(
            src_ref=q_ref, dst_ref=out_ref, send_sem=send_sems.at[a], recv_sem=recv_sems.at[a],
            device_id=(x, y, 1 - c), device_id_type=MESH)
            for a, (q_ref, out_ref) in enumerate(zip(q_refs, out_refs))]
        for cp in cps:
            cp.start()
        for cp in cps:
            cp.wait()

    return pl.pallas_call(
        body, name="rs_share_halves",
        out_shape=[jax.ShapeDtypeStruct(q.shape, F32) for q in qs],
        in_specs=[ANY] * n, out_specs=[ANY] * n,
        scratch_shapes=[pltpu.SemaphoreType.DMA((n,)), pltpu.SemaphoreType.DMA((n,))],
        compiler_params=pltpu.CompilerParams(has_side_effects=True),
    )(*qs)


def _gather_plan():
    def copies(w_refs, out_refs, send_sems, recv_sems):
        x, y, c = _mesh_pos()
        me = 2 * x + y
        sibling = (x, y, 1 - c)
        chips = _other_chips(x, y)
        idx = [2 * chip[0] + chip[1] for chip in chips]
        plans = []
        for a, (w_ref, out_ref) in enumerate(zip(w_refs, out_refs)):
            half = w_ref.shape[0] // 2

            def copy(k, chip, h, to, src=None, out_ref=out_ref, half=half, a=a):
                rows = out_ref.at[chip, pl.ds(h * half, half), :]
                return pltpu.make_async_remote_copy(
                    src_ref=rows if src is None else src, dst_ref=rows,
                    send_sem=send_sems.at[6 * a + k], recv_sem=recv_sems.at[6 * a + k],
                    device_id=to, device_id_type=MESH)

            mine_half = w_ref.at[pl.ds(c * half, half), :]
            send = [copy(j, me, c, (*chip, c), src=mine_half) for j, chip in enumerate(chips)]
            land = [copy(j, idx[j], c, (x, y, c)) for j in range(N_CHIPS - 1)]
            forward = [copy(3 + j, idx[j], c, sibling) for j in range(N_CHIPS - 1)]
            land_fw = [copy(3 + j, idx[j], 1 - c, (x, y, c)) for j in range(N_CHIPS - 1)]
            plans.append((send, land, forward, land_fw))
        return plans

    def start(*refs):
        for send, _, _, _ in copies(*refs):
            for cp in send:
                cp.start()

    def finish(*refs):
        plans = copies(*refs)
        for _, land, forward, _ in plans:
            for j in range(N_CHIPS - 1):
                land[j].wait_recv()
                forward[j].start()
        for _, _, _, land_fw in plans:
            for cp in land_fw:
                cp.wait_recv()
        for send, _, forward, _ in plans:
            for cp in send + forward:
                cp.wait_send()

    return start, finish


def _pair_plan(halves):
    def copies(in_refs, out_refs, send_sems, recv_sems):
        x, y, c = _mesh_pos()
        cps = []
        for a, (g_ref, out_ref) in enumerate(zip(in_refs, out_refs)):
            if halves:
                half = g_ref.shape[1] // 2
                src = g_ref.at[:, pl.ds((1 - c) * half, half), :]
            else:
                src = g_ref
            cps.append(pltpu.make_async_remote_copy(
                src_ref=src, dst_ref=out_ref, send_sem=send_sems.at[a], recv_sem=recv_sems.at[a],
                device_id=(x, y, 1 - c), device_id_type=MESH))
        return cps

    def start(*refs):
        for cp in copies(*refs):
            cp.start()

    def finish(*refs):
        for cp in copies(*refs):
            cp.wait()

    return start, finish


def _chip_plan():
    def copies(in_refs, out_refs, send_sems, recv_sems):
        x, y, c = _mesh_pos()
        chips = _other_chips(x, y)
        return [pltpu.make_async_remote_copy(
            src_ref=p_ref.at[2 * chip[0] + chip[1]], dst_ref=out_ref.at[j],
            send_sem=send_sems.at[3 * a + j], recv_sem=recv_sems.at[3 * a + j], device_id=(*chip, c),
            device_id_type=MESH)
            for a, (p_ref, out_ref) in enumerate(zip(in_refs, out_refs)) for j, chip in enumerate(chips)]

    def start(*refs):
        for cp in copies(*refs):
            cp.start()

    def finish(*refs):
        cps = copies(*refs)
        for cp in cps:
            cp.wait_recv()
        for cp in cps:
            cp.wait_send()

    return start, finish


def _gather_comm(shards):
    return _Comm(_gather_plan(), shards, [jax.ShapeDtypeStruct((N_CHIPS,) + s.shape, s.dtype) for s in shards],
                 6 * len(shards))


def _pair_comm(gs):
    return _Comm(_pair_plan(True), gs,
                 [jax.ShapeDtypeStruct((N_CHIPS, g.shape[1] // 2, g.shape[2]), g.dtype) for g in gs], len(gs))


def _chip_comm(ps):
    return _Comm(_chip_plan(), ps, [jax.ShapeDtypeStruct((N_CHIPS - 1,) + p.shape[1:], p.dtype) for p in ps],
                 3 * len(ps))


def _share_comm(qs):
    return _Comm(_pair_plan(False), qs, [jax.ShapeDtypeStruct(q.shape, q.dtype) for q in qs], len(qs))


def _comm_call(name, comm):
    n, m = len(comm.ins), len(comm.outs)

    def body(*refs):
        args = (refs[:n], refs[n:n + m], refs[n + m], refs[n + m + 1])
        comm.start(*args)
        comm.finish(*args)

    return pl.pallas_call(
        body, name=name, out_shape=comm.outs, in_specs=[ANY] * n, out_specs=[ANY] * m,
        scratch_shapes=[pltpu.SemaphoreType.DMA((comm.n_sems,))] * 2,
        compiler_params=pltpu.CompilerParams(has_side_effects=True),
    )(*comm.ins)


class _Exchange:
    def __init__(self, chip, ci, early_mine, late_mine):
        self.chip, self.ci = chip, ci
        self.mine = {REST_EARLY: early_mine, REST_LATE: late_mine}
        self.c_arr = ci.reshape(1).astype(jnp.int32)
        self.chip_arr = chip.reshape(1).astype(jnp.int32)
        self.reduced = {}

    def rest_weights(self, got, names):
        stacks = lax.dynamic_update_slice(got, self.mine[names][None], (self.chip, 0, 0))
        return {n: _full_from_shards(n, sh) for n, sh in _unstack_rest(stacks, (N_CHIPS,), names).items()}

    def finish_reduce(self, key, mine, other):
        south = self.ci == 0
        self.reduced[key] = jnp.concatenate([jnp.where(south, mine, other), jnp.where(south, other, mine)],
                                            axis=0)


def _allreduce_small(part, name):
    rows = part.shape[0]

    def body(p_ref, out_ref, buf, send_sems, recv_sems, local_sem):
        x, y, c = _mesh_pos()
        me, sibling = (x, y, c), (x, y, 1 - c)
        chips = _other_chips(x, y)

        def slot(px, py, pc):
            return buf.at[pl.ds((4 * px + 2 * py + pc) * rows, rows), :]

        def copy(k, block, to, src=None):
            return pltpu.make_async_remote_copy(
                src_ref=slot(*block) if src is None else src, dst_ref=slot(*block),
                send_sem=send_sems.at[k], recv_sem=recv_sems.at[k], device_id=to, device_id_type=MESH)

        mine = pltpu.make_async_copy(p_ref, slot(*me), local_sem)
        mine.start()
        first = [copy(0, me, sibling, src=p_ref)]
        first += [copy(1 + j, me, (*chip, c), src=p_ref) for j, chip in enumerate(chips)]
        for cp in first:
            cp.start()
        passed = [copy(4 + j, (*chip, c), sibling) for j, chip in enumerate(chips)]
        for j, chip in enumerate(chips):
            copy(1 + j, (*chip, c), me).wait_recv()
            passed[j].start()
        copy(0, sibling, me).wait_recv()
        for j, chip in enumerate(chips):
            copy(4 + j, (*chip, 1 - c), me).wait_recv()
        for cp in first + passed:
            cp.wait_send()
        mine.wait()
        acc = buf[pl.ds(0, rows), :]
        for k in range(1, N_DEV):
            acc = acc + buf[pl.ds(k * rows, rows), :]
        out_ref[...] = acc

    return pl.pallas_call(
        body, name=name,
        out_shape=jax.ShapeDtypeStruct(part.shape, F32),
        in_specs=[pl.BlockSpec(memory_space=pltpu.VMEM)],
        out_specs=pl.BlockSpec(memory_space=pltpu.VMEM),
        scratch_shapes=[pltpu.VMEM((N_DEV * rows, LANES), F32), pltpu.SemaphoreType.DMA((7,)),
                        pltpu.SemaphoreType.DMA((7,)), pltpu.SemaphoreType.DMA],
        compiler_params=pltpu.CompilerParams(has_side_effects=True),
    )(part)


def _adamw(w, g, m, v, name):
    R, C = w.shape
    bs = _row_block(R, 512, 8) if R % 8 == 0 else R
    c1 = 1.0 / (1.0 - ADAM_B1 ** ADAM_STEP)
    c2 = 1.0 / (1.0 - ADAM_B2 ** ADAM_STEP)

    def body(w_ref, g_ref, m_ref, v_ref, d_ref, nm_ref, nv_ref):
        gg = g_ref[...]
        nm = ADAM_B1 * m_ref[...] + (1.0 - ADAM_B1) * gg
        nv = ADAM_B2 * v_ref[...] + (1.0 - ADAM_B2) * (gg * gg)
        nm_ref[...] = nm
        nv_ref[...] = nv
        d_ref[...] = -ADAM_LR * ((nm * c1) / (jnp.sqrt(nv * c2) + ADAM_EPS) + ADAM_WD * w_ref[...])

    spec = pl.BlockSpec((bs, C), lambda i: (i, 0))
    shp = jax.ShapeDtypeStruct((R, C), F32)
    return pl.pallas_call(
        body, name=name, grid=(R // bs,), in_specs=[spec] * 4, out_specs=[spec] * 3, out_shape=[shp] * 3,
        compiler_params=_params(("parallel",)),
    )(w, g, m, v)


WEIGHTS = ("norm_mix_pre_w", "w_in", "b_gate", "conv_w", "conv_b", "dt_bias", "a_log", "d_skip",
           "ssm_norm_w", "w_att_proj", "w_ssm_proj", "w_out", "norm_mix_post_w", "norm_ffn_pre_w", "w_up",
           "w_down", "norm_ffn_post_w")


def _flat_small(vals, conv_w_full):
    flat = [vals[n].reshape(-1) for n in SMALL] + [conv_w_full.reshape(-1)]
    v = jnp.concatenate(flat)
    return jnp.pad(v, (0, SMALL_ROWS * LANES - v.shape[0])).reshape(SMALL_ROWS, LANES)


def kernel(x, norm_mix_pre_w, w_in, b_gate, conv_w, conv_b, dt_bias, a_log, d_skip, ssm_norm_w, w_att_proj, w_ssm_proj, w_out, norm_mix_post_w, norm_ffn_pre_w, w_up, w_down, norm_ffn_post_w, loss_target, m_norm_mix_pre_w, m_w_in, m_b_gate, m_conv_w, m_conv_b, m_dt_bias, m_a_log, m_d_skip, m_ssm_norm_w, m_w_att_proj, m_w_ssm_proj, m_w_out, m_norm_mix_post_w, m_norm_ffn_pre_w, m_w_up, m_w_down, m_norm_ffn_post_w, v_norm_mix_pre_w, v_w_in, v_b_gate, v_conv_w, v_conv_b, v_dt_bias, v_a_log, v_d_skip, v_ssm_norm_w, v_w_att_proj, v_w_ssm_proj, v_w_out, v_norm_mix_post_w, v_norm_ffn_pre_w, v_w_up, v_w_down, v_norm_ffn_post_w):
    args = locals()

    def strip(a):
        return a[0] if a.ndim == 3 else a

    wts = {n: strip(args[n]) for n in WEIGHTS}
    mom = {n: strip(args["m_" + n]) for n in WEIGHTS}
    var = {n: strip(args["v_" + n]) for n in WEIGHTS}
    xi, yi, ci = _mesh_pos()
    chip = 2 * xi + yi

    tr = lambda a: jnp.swapaxes(a, 0, 1)
    w_in_mine = jnp.pad(tr(wts["w_in"]).astype(BF16), ((0, IN_SHARD_PAD - IN_SHARD_ROWS), (0, 0)))
    got_in = _comm_call("allgather_w_in", _gather_comm([w_in_mine]))[0]
    stacks_in = lax.dynamic_update_slice(got_in, w_in_mine[None], (chip, 0, 0))
    full = {"w_in_t": stacks_in[:, :IN_SHARD_ROWS].reshape(IN_PROJ_WIDTH, D_MODEL)}
    ex = _Exchange(chip, ci, _stack_rest(wts, BF16, REST_EARLY), _stack_rest(wts, BF16, REST_LATE))
    cw_cols = CONV_DIM // N_CHIPS
    conv_slab = lax.dynamic_update_slice(jnp.zeros((SSM_CONV, CONV_DIM), F32),
                                         jnp.where(ci == 0, wts["conv_w"], 0.0), (0, chip * cw_cols))
    small_in = jnp.pad(conv_slab.reshape(-1), (0, SMALL_ROWS * LANES - SSM_CONV * CONV_DIM))
    conv_full = _allreduce_small(small_in.reshape(SMALL_ROWS, LANES), "gather_conv_w")
    full["conv_w"] = conv_full.reshape(-1)[:SSM_CONV * CONV_DIM].reshape(SSM_CONV, CONV_DIM)
    for n in SMALL:
        full[n] = wts[n]

    loss_part, grad_x, g = _local_step(x[0], loss_target[0], full, ex)
    loss = lax.psum(loss_part[0, 0], ("x", "y", "c"))

    gshard = _unstack_rest(ex.reduced["rest"])
    g_in_t = ex.reduced["w_in"][:IN_SHARD_ROWS]
    small_sum = _allreduce_small(_flat_small(g, g["conv_w"]), "allreduce_small_grads").reshape(-1)
    grads, off = {}, 0
    for n in SMALL:
        sz = wts[n].size
        grads[n] = small_sum[off:off + sz].reshape(wts[n].shape)
        off += sz
    conv_g = small_sum[off:off + SSM_CONV * CONV_DIM].reshape(SSM_CONV, CONV_DIM)
    grads["conv_w"] = lax.dynamic_slice(conv_g, (0, chip * cw_cols), (SSM_CONV, cw_cols))
    grads.update(gshard)

    delta, new_m, new_v = {}, {}, {}
    for n in REST:
        delta[n], new_m[n], new_v[n] = _adamw(wts[n], grads[n], mom[n], var[n], f"adamw_{n}")
    in_t = _adamw(tr(wts["w_in"]), g_in_t, tr(mom["w_in"]), tr(var["w_in"]), "adamw_w_in")
    grads["w_in"] = tr(g_in_t)
    delta["w_in"], new_m["w_in"], new_v["w_in"] = (tr(a) for a in in_t)
    small_names = SMALL + ("conv_w",)

    def pack_small(d):
        v = jnp.concatenate([d[n].reshape(-1) for n in small_names])
        rows = -(-v.shape[0] // (8 * LANES)) * 8
        return jnp.pad(v, (0, rows * LANES - v.shape[0])).reshape(rows, LANES)

    ds, ms, vs = _adamw(pack_small(wts), pack_small(grads), pack_small(mom), pack_small(var), "adamw_small")
    off = 0
    for n in small_names:
        sz = wts[n].size
        for dst, src in ((delta, ds), (new_m, ms), (new_v, vs)):
            dst[n] = src.reshape(-1)[off:off + sz].reshape(wts[n].shape)
        off += sz

    out = [loss, grad_x[None]]
    for d in (grads, delta, new_m, new_v):
        out += [d[n][None] if args[n].ndim == 3 else d[n] for n in WEIGHTS]
    return tuple(out)
```

```python
import functools
import math

import numpy as np
import jax
import jax.numpy as jnp
from jax import lax
from jax.experimental import pallas as pl
from jax.experimental.pallas import tpu as pltpu

F32 = jnp.float32
BF16 = jnp.bfloat16

D_MODEL = 1024
HEAD_DIM = 64
N_ATT_HEADS = 12
ATT_WIDTH = N_ATT_HEADS * HEAD_DIM
DILATIONS = (1, 4, 16)
ATT_BLOCK = 128
SSM_INNER = 2048
SSM_HEADS = 32
SSM_GROUPS = 8
HEADS_PER_GROUP = SSM_HEADS // SSM_GROUPS
SSM_HEAD_DIM = 64
SSM_STATE = 128
SSM_CONV = 4
SSM_CHUNK = 128
CONV_DIM = SSM_INNER + 2 * SSM_GROUPS * SSM_STATE
FFN_HIDDEN = 4 * D_MODEL
IN_SPLITS = (ATT_WIDTH, ATT_WIDTH, ATT_WIDTH, SSM_INNER, CONV_DIM, SSM_HEADS, 2 * D_MODEL)
IN_PROJ_WIDTH = sum(IN_SPLITS)
RMS_EPS = 1e-6
LANES = 128
NEG_BIG = -1e30

ADAM_LR = 0.001
ADAM_B1 = 0.9
ADAM_B2 = 0.999
ADAM_EPS = 1e-08
ADAM_WD = 0.01
ADAM_STEP = 10

N_CHIPS = 4
N_DEV = 8
VMEM_LIMIT = 56 * 1024 * 1024
MESH = pl.DeviceIdType.MESH


def _alibi_slopes(n):
    def pow2(m):
        start = 2.0 ** (-8.0 / m)
        return [start ** (i + 1) for i in range(m)]
    if (n & (n - 1)) == 0:
        s = pow2(n)
    else:
        c = 2 ** int(math.floor(math.log2(n)))
        s = pow2(c) + pow2(2 * c)[0::2][: n - c]
    return [float(v) for v in np.array(s, dtype=np.float32)]


def _params(sem):
    return pltpu.CompilerParams(dimension_semantics=sem, vmem_limit_bytes=VMEM_LIMIT)


def _dot(a, b):
    return lax.dot_general(a, b, (((1,), (0,)), ((), ())), preferred_element_type=F32)


def _dot_nt(a, b):
    return lax.dot_general(a, b, (((1,), (1,)), ((), ())), preferred_element_type=F32)


def _dot_tn(a, b):
    return lax.dot_general(a, b, (((0,), (0,)), ((), ())), preferred_element_type=F32)


def _dot_hi(a, b):
    return lax.dot_general(a, b, (((1,), (0,)), ((), ())), preferred_element_type=F32,
                           precision=lax.Precision.HIGHEST)


def _dot_tn_hi(a, b):
    return lax.dot_general(a, b, (((0,), (0,)), ((), ())), preferred_element_type=F32,
                           precision=lax.Precision.HIGHEST)


def _b(x):
    return x.astype(BF16)


def _sigmoid(x):
    return 1.0 / (1.0 + jnp.exp(-x))


def _pick(n, cands):
    for c in cands:
        if n % c == 0:
            return c
    raise ValueError(f"no tile for {n}")


def _row_block(rows, cap, mult):
    best = max(d for d in range(mult, cap + 1, mult) if rows % d == 0)
    return best


class _Comm:
    def __init__(self, plan, ins, outs, n_sems):
        self.start, self.finish = plan
        self.ins, self.outs, self.n_sems = list(ins), list(outs), n_sems


def _mm_nn(a, b, out_dtype, name, acc=None, mode=None, extra=None, comm=None, tb=False):
    M, K = a.shape
    N = b.shape[0] if tb else b.shape[1]
    tn = _pick(N, (1024, 768, 512, 256, 128))
    tk = K if K <= 4096 else _pick(K, (2048, 1024))
    tm = 1024 if M % 1024 == 0 and K <= 2304 else 512
    nk = K // tk
    nj, ni = N // tn, M // tm
    side = acc if acc is not None else extra
    n_out = 2 if mode == "relu2" else 1
    n_in = 2 + (side is not None)
    n_ci = len(comm.ins) if comm else 0
    n_co = len(comm.outs) if comm else 0

    def body(*refs):
        a_ref, b_ref = refs[0], refs[1]
        s_ref = refs[2] if side is not None else None
        o_refs = refs[n_in + n_ci:n_in + n_ci + n_out]
        if comm:
            c_args = (refs[n_in:n_in + n_ci], refs[n_in + n_ci + n_out:n_in + n_ci + n_out + n_co],
                      refs[-2], refs[-1])
            pj, pi, pk = pl.program_id(0), pl.program_id(1), pl.program_id(2)

            @pl.when(jnp.logical_and(jnp.logical_and(pj == 0, pi == 0), pk == 0))
            def _():
                comm.start(*c_args)

        def finish(r):
            if mode == "relu2":
                r = jnp.maximum(r, 0.0)
                o_refs[0][...] = _b(r)
                o_refs[1][...] = _b(r * r)
            elif mode == "mul2":
                o_refs[0][...] = _b(r * (2.0 * s_ref[...].astype(F32)))
            else:
                if acc is not None:
                    r = r + s_ref[...]
                o_refs[0][...] = r.astype(out_dtype)

        part = (_dot_nt if tb else _dot)(_b(a_ref[...]), _b(b_ref[...]))
        if nk == 1:
            finish(part)
        else:
            acc_ref = refs[n_in + n_ci + n_out + n_co]
            k = pl.program_id(2)

            @pl.when(k == 0)
            def _():
                acc_ref[...] = part

            @pl.when(jnp.logical_and(k > 0, k < nk - 1))
            def _():
                acc_ref[...] += part

            @pl.when(k == nk - 1)
            def _():
                finish(acc_ref[...] + part)

        if comm:
            @pl.when(jnp.logical_and(jnp.logical_and(pj == nj - 1, pi == ni - 1), pk == nk - 1))
            def _():
                comm.finish(*c_args)

    tile = pl.BlockSpec((tm, tn), lambda j, i, k: (i, j))
    in_specs = [pl.BlockSpec((tm, tk), lambda j, i, k: (i, k)),
                pl.BlockSpec((tn, tk), lambda j, i, k: (j, k)) if tb else
                pl.BlockSpec((tk, tn), lambda j, i, k: (k, j))]
    args = [a, b]
    if side is not None:
        in_specs.append(tile)
        args.append(side)
    odt = BF16 if mode in ("relu2", "mul2") else out_dtype
    scratch = [pltpu.VMEM((tm, tn), F32)] if nk > 1 else []
    if comm:
        scratch += [pltpu.SemaphoreType.DMA((comm.n_sems,))] * 2
        params = pltpu.CompilerParams(dimension_semantics=("arbitrary",) * 3, vmem_limit_bytes=VMEM_LIMIT,
                                      has_side_effects=True)
    else:
        params = _params(("parallel", "parallel", "arbitrary"))
    outs = pl.pallas_call(
        body, name=name, grid=(nj, ni, nk),
        in_specs=in_specs + [ANY] * n_ci,
        out_specs=[tile] * n_out + [ANY] * n_co,
        out_shape=[jax.ShapeDtypeStruct((M, N), odt)] * n_out + list(comm.outs if comm else []),
        scratch_shapes=scratch,
        compiler_params=params,
    )(*args, *(comm.ins if comm else []))
    res = outs[:n_out] if n_out > 1 else outs[0]
    return (res, outs[n_out:]) if comm else res


class _Epi:
    def __init__(self, fn, row_ins=(), full_ins=(), row_outs=(), acc_outs=(), tiled=False, a_fn=None):
        self.fn, self.row_ins, self.full_ins = fn, list(row_ins), list(full_ins)
        self.row_outs, self.acc_outs, self.tiled = list(row_outs), list(acc_outs), tiled
        self.a_fn = a_fn


def _acc_into(ref, val, first):
    @pl.when(first)
    def _():
        ref[...] = val

    @pl.when(jnp.logical_not(first))
    def _():
        ref[...] += val


def _mm_epi(a, b, epi, name, tb=False, comm=None, tm=512, tn=None):
    N, K = b.shape if tb else b.shape[::-1]
    M = epi.row_ins[0][0].shape[0] if a is None else a.shape[0]
    tn = tn or N
    assert epi.tiled or tn == N
    tk = K if K <= 4096 else _pick(K, (2048, 1024))
    nk = K // tk
    nj, ni = N // tn, M // tm
    assert a is not None or (nk == 1 and nj == 1)
    n_a = 0 if a is None else 1
    n_ri, n_fi, n_ro, n_ao = len(epi.row_ins), len(epi.full_ins), len(epi.row_outs), len(epi.acc_outs)
    n_ci = len(comm.ins) if comm else 0
    n_co = len(comm.outs) if comm else 0
    i0 = n_a + 1
    o0 = i0 + n_ci + n_ri + n_fi

    def body(*refs):
        b_ref = refs[n_a]
        ri = refs[i0 + n_ci:i0 + n_ci + n_ri]
        fi = refs[i0 + n_ci + n_ri:o0]
        ro = refs[o0 + n_co:o0 + n_co + n_ro]
        ao = refs[o0 + n_co + n_ro:o0 + n_co + n_ro + n_ao]
        pj, pi, pk = pl.program_id(0), pl.program_id(1), pl.program_id(2)
        if comm:
            c_args = (refs[i0:i0 + n_ci], refs[o0:o0 + n_co], refs[-2], refs[-1])

            @pl.when(jnp.logical_and(jnp.logical_and(pj == 0, pi == 0), pk == 0))
            def _():
                comm.start(*c_args)

        a_val = epi.a_fn(ri, fi, ro) if a is None else _b(refs[0][...])
        part = (_dot_nt if tb else _dot)(a_val, _b(b_ref[...]))
        if nk == 1:
            epi.fn(part, ri, fi, ro, ao, pi == 0)
        else:
            acc_ref = refs[o0 + n_co + n_ro + n_ao]

            @pl.when(pk == 0)
            def _():
                acc_ref[...] = part

            @pl.when(jnp.logical_and(pk > 0, pk < nk - 1))
            def _():
                acc_ref[...] += part

            @pl.when(pk == nk - 1)
            def _():
                epi.fn(acc_ref[...] + part, ri, fi, ro, ao, pi == 0)

        if comm:
            @pl.when(jnp.logical_and(jnp.logical_and(pj == nj - 1, pi == ni - 1), pk == nk - 1))
            def _():
                comm.finish(*c_args)

    def row_spec(width, cb):
        if epi.tiled:
            return pl.BlockSpec((tm, tn), lambda j, i, k: (i, j + cb))
        return pl.BlockSpec((tm, width), lambda j, i, k: (i, cb))

    in_specs = [pl.BlockSpec((tm, tk), lambda j, i, k: (i, k))] * n_a
    in_specs += [pl.BlockSpec((tn, tk), lambda j, i, k: (j, k)) if tb else
                 pl.BlockSpec((tk, tn), lambda j, i, k: (k, j))]
    in_specs += [ANY] * n_ci
    in_specs += [row_spec(w, cb) for (_, w, cb) in epi.row_ins]
    in_specs += [pl.BlockSpec((1, tn), lambda j, i, k: (0, j)) if epi.tiled else
                 pl.BlockSpec(f.shape, lambda j, i, k: (0, 0)) for f in epi.full_ins]
    out_specs = [ANY] * n_co + [row_spec(c, 0) for c, _ in epi.row_outs]
    out_specs += [pl.BlockSpec((1, tn), lambda j, i, k: (0, j)) if epi.tiled else
                  pl.BlockSpec((1, c), lambda j, i, k: (0, 0)) for c in epi.acc_outs]
    out_shape = list(comm.outs if comm else [])
    out_shape += [jax.ShapeDtypeStruct((M, c), dt_) for c, dt_ in epi.row_outs]
    out_shape += [jax.ShapeDtypeStruct((1, c), F32) for c in epi.acc_outs]
    scratch = [pltpu.VMEM((tm, tn), F32)] if nk > 1 else []
    if comm:
        scratch += [pltpu.SemaphoreType.DMA((comm.n_sems,))] * 2
    params = pltpu.CompilerParams(dimension_semantics=("arbitrary",) * 3, vmem_limit_bytes=VMEM_LIMIT,
                                  has_side_effects=comm is not None)
    outs = pl.pallas_call(
        body, name=name, grid=(nj, ni, nk), in_specs=in_specs, out_specs=out_specs, out_shape=out_shape,
        scratch_shapes=scratch, compiler_params=params,
    )(*([a] * n_a), b, *(comm.ins if comm else []), *[arr for arr, _, _ in epi.row_ins], *epi.full_ins)
    return outs[n_co:], (outs[:n_co] if comm else None)


def _mm_tn(a, b, name):
    S, Ka = a.shape
    _, N = b.shape
    tka = _pick(Ka, (1024, 768, 512, 256, 128))
    tn = _pick(N, (1024, 768, 512, 256, 128))
    ts = 1024 if S % 1024 == 0 else 512
    ns = S // ts

    def body(a_ref, b_ref, o_ref, acc_ref):
        s = pl.program_id(2)
        part = _dot_tn(_b(a_ref[...]), _b(b_ref[...]))

        @pl.when(s == 0)
        def _():
            acc_ref[...] = part

        @pl.when(s > 0)
        def _():
            acc_ref[...] += part

        @pl.when(s == ns - 1)
        def _():
            o_ref[...] = acc_ref[...]

    return pl.pallas_call(
        body, name=name, grid=(Ka // tka, N // tn, ns),
        in_specs=[pl.BlockSpec((ts, tka), lambda i, j, s: (s, i)),
                  pl.BlockSpec((ts, tn), lambda i, j, s: (s, j))],
        out_specs=pl.BlockSpec((tka, tn), lambda i, j, s: (i, j)),
        out_shape=jax.ShapeDtypeStruct((Ka, N), F32),
        scratch_shapes=[pltpu.VMEM((tka, tn), F32)],
        compiler_params=_params(("parallel", "parallel", "arbitrary")),
    )(a, b)


def _row_call(body, row_ins, full_ins, row_outs, acc_outs, bs, name):
    S = row_ins[0].shape[0]
    assert S % bs == 0
    in_specs = [pl.BlockSpec((bs, a.shape[1]), lambda i: (i, 0)) for a in row_ins]
    in_specs += [pl.BlockSpec(a.shape, lambda i: (0, 0)) for a in full_ins]
    out_specs = [pl.BlockSpec((bs, c), lambda i: (i, 0)) for c, _ in row_outs]
    out_specs += [pl.BlockSpec(s, lambda i: (0, 0)) for s in acc_outs]
    out_shape = [jax.ShapeDtypeStruct((S, c), dt) for c, dt in row_outs]
    out_shape += [jax.ShapeDtypeStruct(s, F32) for s in acc_outs]
    return pl.pallas_call(
        body, name=name, grid=(S // bs,), in_specs=in_specs, out_specs=out_specs, out_shape=out_shape,
        compiler_params=_params(("arbitrary",)),
    )(*row_ins, *full_ins)


def _rms_vals(x, w):
    r = lax.rsqrt(jnp.mean(x * x, axis=-1, keepdims=True) + RMS_EPS)
    return x * r * w


def _rms_bwd_vals(x, w, dy):
    r = lax.rsqrt(jnp.mean(x * x, axis=-1, keepdims=True) + RMS_EPS)
    xn = x * r
    g = dy * w
    dx = r * (g - xn * jnp.mean(g * xn, axis=-1, keepdims=True))
    dw = jnp.sum(dy * xn, axis=0, keepdims=True)
    return dx, dw


def _acc_add(ref, val):
    @pl.when(pl.program_id(0) == 0)
    def _():
        ref[...] = val

    @pl.when(pl.program_id(0) > 0)
    def _():
        ref[...] += val


def _rms_fwd(x, w):
    def body(x_ref, w_ref, o_ref):
        o_ref[...] = _b(_rms_vals(x_ref[...], w_ref[...]))
    return _row_call(body, [x], [w], [(x.shape[1], BF16)], [], 512, "rms_fwd")[0]


def _gate_fwd(att_o, ssm_o, gl, b_gate):
    def body(a_ref, s_ref, g_ref, b_ref, o_ref):
        g = _sigmoid(g_ref[...] + b_ref[...])
        o_ref[...] = _b(g[:, :D_MODEL] * a_ref[...] + g[:, D_MODEL:] * s_ref[...])
    return _row_call(body, [att_o, ssm_o, gl], [b_gate], [(D_MODEL, BF16)], [], 512, "gate_fwd")[0]


def _post_pre(x, mixed, w_post, w_pre):
    def body(x_ref, m_ref, wp_ref, wn_ref, h_ref, f_ref):
        h = x_ref[...] + _rms_vals(m_ref[...], wp_ref[...])
        h_ref[...] = h
        f_ref[...] = _b(_rms_vals(h, wn_ref[...]))
    return _row_call(body, [x, mixed], [w_post, w_pre], [(D_MODEL, F32), (D_MODEL, BF16)], [], 512,
                     "post_pre")


def _relu2(up):
    def body(u_ref, o_ref):
        r = jnp.maximum(u_ref[...], 0.0)
        o_ref[...] = _b(r * r)
    return _row_call(body, [up], [], [(up.shape[1], BF16)], [], 256, "relu2")[0]


def _final(h1, down, w_post, target):
    def body(h_ref, d_ref, t_ref, w_ref, dh_ref, dd_ref, loss_ref, dw_ref):
        dn = d_ref[...]
        w = w_ref[...]
        err = h_ref[...] + _rms_vals(dn, w) - t_ref[...]
        row = jnp.mean(err * err, axis=-1, keepdims=True)
        part = 0.5 * jnp.sum(row, axis=0, keepdims=True)
        dh = err * (1.0 / D_MODEL)
        dh_ref[...] = dh
        dx, dw = _rms_bwd_vals(dn, w, dh)
        dd_ref[...] = _b(dx)
        _acc_add(loss_ref, jnp.broadcast_to(part, (1, LANES)))
        _acc_add(dw_ref, dw)
    return _row_call(body, [h1, down, target], [w_post], [(D_MODEL, F32), (D_MODEL, BF16)],
                     [(1, LANES), (1, D_MODEL)], 512, "final_loss")


def _dup(da, up):
    def body(a_ref, u_ref, o_ref):
        o_ref[...] = _b(a_ref[...] * (2.0 * jnp.maximum(u_ref[...], 0.0)))
    return _row_call(body, [da, up], [], [(up.shape[1], BF16)], [], 256, "relu2_bwd")[0]


def _mid_bwd(dh2, df, h1, mixed, w_pre, w_post):
    def body(dh_ref, df_ref, h_ref, m_ref, wn_ref, wp_ref, dh1_ref, dm_ref, dwn_ref, dwp_ref):
        dx, dwn = _rms_bwd_vals(h_ref[...], wn_ref[...], df_ref[...])
        dh1 = dh_ref[...] + dx
        dh1_ref[...] = dh1
        dm, dwp = _rms_bwd_vals(m_ref[...], wp_ref[...], dh1)
        dm_ref[...] = _b(dm)
        _acc_add(dwn_ref, dwn)
        _acc_add(dwp_ref, dwp)
    return _row_call(body, [dh2, df, h1, mixed], [w_pre, w_post], [(D_MODEL, F32), (D_MODEL, BF16)],
                     [(1, D_MODEL), (1, D_MODEL)], 512, "mid_bwd")


def _gate_bwd(dmi, att_o, ssm_o, gl, b_gate):
    def body(d_ref, a_ref, s_ref, g_ref, b_ref, da_ref, ds_ref, dg_ref, db_ref):
        g = _sigmoid(g_ref[...] + b_ref[...])
        d = d_ref[...]
        ga, gs = g[:, :D_MODEL], g[:, D_MODEL:]
        da_ref[...] = _b(ga * d)
        ds_ref[...] = _b(gs * d)
        dga = d * a_ref[...] * ga * (1.0 - ga)
        dgs = d * s_ref[...] * gs * (1.0 - gs)
        dg_ref[:, :D_MODEL] = _b(dga)
        dg_ref[:, D_MODEL:] = _b(dgs)
        _acc_add(db_ref.at[:, pl.ds(0, D_MODEL)], jnp.sum(dga, axis=0, keepdims=True))
        _acc_add(db_ref.at[:, pl.ds(D_MODEL, D_MODEL)], jnp.sum(dgs, axis=0, keepdims=True))
    return _row_call(body, [dmi, att_o, ssm_o, gl], [b_gate],
                     [(D_MODEL, BF16), (D_MODEL, BF16), (2 * D_MODEL, BF16)], [(1, 2 * D_MODEL)], 256,
                     "gate_bwd")


def _first_bwd(dh1, du, x, w_pre):
    def body(dh_ref, du_ref, x_ref, w_ref, dx_ref, dw_ref):
        dx, dw = _rms_bwd_vals(x_ref[...], w_ref[...], du_ref[...])
        dx_ref[...] = dh_ref[...] + dx
        _acc_add(dw_ref, dw)
    return _row_call(body, [dh1, du, x], [w_pre], [(D_MODEL, F32)], [(1, D_MODEL)], 512, "first_bwd")


def _group_rms(t):
    gw = SSM_INNER // SSM_GROUPS
    out = []
    for g in range(SSM_GROUPS):
        tg = t[:, g * gw:(g + 1) * gw]
        out.append(lax.rsqrt(jnp.mean(tg * tg, axis=-1, keepdims=True) + RMS_EPS))
    return out


def _gnorm_fwd(y, z, w):
    gw = SSM_INNER // SSM_GROUPS

    def body(y_ref, z_ref, w_ref, o_ref):
        zz = z_ref[...]
        t = y_ref[...] * (zz * _sigmoid(zz))
        rs = _group_rms(t)
        for g in range(SSM_GROUPS):
            sl = slice(g * gw, (g + 1) * gw)
            o_ref[:, sl] = _b(t[:, sl] * rs[g] * w_ref[:, sl])
    return _row_call(body, [y, z], [w], [(SSM_INNER, BF16)], [], 256, "gnorm_fwd")[0]


def _gnorm_bwd(dout, y, z, w):
    gw = SSM_INNER // SSM_GROUPS

    def body(d_ref, y_ref, z_ref, w_ref, dy_ref, dz_ref, dw_ref):
        zz = z_ref[...]
        yy = y_ref[...]
        sg = _sigmoid(zz)
        sz = zz * sg
        t = yy * sz
        rs = _group_rms(t)
        for g in range(SSM_GROUPS):
            sl = slice(g * gw, (g + 1) * gw)
            tn = t[:, sl] * rs[g]
            d = d_ref[:, sl]
            gg = d * w_ref[:, sl]
            dt = rs[g] * (gg - tn * jnp.mean(gg * tn, axis=-1, keepdims=True))
            dy_ref[:, sl] = dt * sz[:, sl]
            dz_ref[:, sl] = _b(dt * yy[:, sl] * (sg[:, sl] * (1.0 + zz[:, sl] * (1.0 - sg[:, sl]))))
            _acc_add(dw_ref.at[:, pl.ds(g * gw, gw)], jnp.sum(d * tn, axis=0, keepdims=True))
    return _row_call(body, [dout, y, z], [w], [(SSM_INNER, F32), (SSM_INNER, BF16)], [(1, SSM_INNER)], 256,
                     "gnorm_bwd")


def _ssm_out_epi(y, z, w):
    gw = SSM_INNER // SSM_GROUPS

    def a_fn(ri, fi, ro):
        zz = ri[1][...]
        t = ri[0][...] * (zz * _sigmoid(zz))
        rs = _group_rms(t)
        for g in range(SSM_GROUPS):
            sl = slice(g * gw, (g + 1) * gw)
            ro[0][:, sl] = _b(t[:, sl] * rs[g] * fi[0][:, sl])
        return ro[0][...]

    def fn(r, ri, fi, ro, ao, first):
        ro[1][...] = r
    return _Epi(fn, [(y, SSM_INNER, 0), (z, SSM_INNER, 0)], [w], [(SSM_INNER, BF16), (D_MODEL, F32)], a_fn=a_fn)


def _mix_out_epi(att_o, ssm_o, gl, x, b_gate, w_post, w_pre):
    def a_fn(ri, fi, ro):
        g = _sigmoid(ri[2][...] + fi[0][...])
        mi = _b(g[:, :D_MODEL] * ri[0][...] + g[:, D_MODEL:] * ri[1][...])
        ro[0][...] = mi
        return mi

    def fn(r, ri, fi, ro, ao, first):
        ro[1][...] = r
        h = ri[3][...] + _rms_vals(r, fi[1][...])
        ro[2][...] = h
        ro[3][...] = _b(_rms_vals(h, fi[2][...]))
    return _Epi(fn, [(att_o, D_MODEL, 0), (ssm_o, D_MODEL, 0), (gl, 2 * D_MODEL, 0), (x, D_MODEL, 0)],
                [b_gate, w_post, w_pre],
                [(D_MODEL, BF16), (D_MODEL, F32), (D_MODEL, F32), (D_MODEL, BF16)], a_fn=a_fn)


def _final_epi(h1, target, w_post):
    def fn(dn, ri, fi, ro, ao, first):
        w = fi[0][...]
        err = ri[0][...] + _rms_vals(dn, w) - ri[1][...]
        row = jnp.mean(err * err, axis=-1, keepdims=True)
        part = 0.5 * jnp.sum(row, axis=0, keepdims=True)
        dh = err * (1.0 / D_MODEL)
        ro[0][...] = dh
        dx, dw = _rms_bwd_vals(dn, w, dh)
        ro[1][...] = _b(dx)
        _acc_into(ao[0], jnp.broadcast_to(part, (1, LANES)), first)
        _acc_into(ao[1], dw, first)
    return _Epi(fn, [(h1, D_MODEL, 0), (target, D_MODEL, 0)], [w_post], [(D_MODEL, F32), (D_MODEL, BF16)],
                [LANES, D_MODEL])


def _mid_epi(dh2, h1, mixed, w_pre, w_post):
    def fn(df, ri, fi, ro, ao, first):
        dx, dwn = _rms_bwd_vals(ri[1][...], fi[0][...], df)
        dh1 = ri[0][...] + dx
        ro[0][...] = dh1
        dm, dwp = _rms_bwd_vals(ri[2][...], fi[1][...], dh1)
        ro[1][...] = _b(dm)
        _acc_into(ao[0], dwn, first)
        _acc_into(ao[1], dwp, first)
    return _Epi(fn, [(dh2, D_MODEL, 0), (h1, D_MODEL, 0), (mixed, D_MODEL, 0)], [w_pre, w_post],
                [(D_MODEL, F32), (D_MODEL, BF16)], [D_MODEL, D_MODEL])


def _gate_epi(att_o, ssm_o, gl, b_gate):
    def fn(d, ri, fi, ro, ao, first):
        g = _sigmoid(ri[2][...] + fi[0][...])
        ga, gs = g[:, :D_MODEL], g[:, D_MODEL:]
        ro[0][...] = _b(ga * d)
        ro[1][...] = _b(gs * d)
        dga = d * ri[0][...] * ga * (1.0 - ga)
        dgs = d * ri[1][...] * gs * (1.0 - gs)
        ro[2][:, :D_MODEL] = _b(dga)
        ro[2][:, D_MODEL:] = _b(dgs)
        _acc_into(ao[0].at[:, pl.ds(0, D_MODEL)], jnp.sum(dga, axis=0, keepdims=True), first)
        _acc_into(ao[0].at[:, pl.ds(D_MODEL, D_MODEL)], jnp.sum(dgs, axis=0, keepdims=True), first)
    return _Epi(fn, [(att_o, D_MODEL, 0), (ssm_o, D_MODEL, 0), (gl, 2 * D_MODEL, 0)], [b_gate],
                [(D_MODEL, BF16), (D_MODEL, BF16), (2 * D_MODEL, BF16)], [2 * D_MODEL])


def _first_epi(du, dh1, x, w_pre):
    def fn(r, ri, fi, ro, ao, first):
        dx, dw = _rms_bwd_vals(ri[2][...], fi[0][...], ri[0][...] + r)
        ro[0][...] = ri[1][...] + dx
        _acc_into(ao[0], dw, first)
    return _Epi(fn, [(du, D_MODEL, 0), (dh1, D_MODEL, 0), (x, D_MODEL, 0)], [w_pre], [(D_MODEL, F32)],
                [D_MODEL])


def _gnorm_epi(y, z, w):
    gw = SSM_INNER // SSM_GROUPS

    def fn(d_all, ri, fi, ro, ao, first):
        zz = ri[1][...]
        yy = ri[0][...]
        sg = _sigmoid(zz)
        sz = zz * sg
        t = yy * sz
        dws = []
        for g in range(d_all.shape[1] // gw):
            sl = slice(g * gw, (g + 1) * gw)
            tg = t[:, sl]
            r = lax.rsqrt(jnp.mean(tg * tg, axis=-1, keepdims=True) + RMS_EPS)
            tn = tg * r
            d = d_all[:, sl]
            gg = d * fi[0][:, sl]
            dt = r * (gg - tn * jnp.mean(gg * tn, axis=-1, keepdims=True))
            ro[0][:, sl] = dt * sz[:, sl]
            ro[1][:, sl] = _b(dt * yy[:, sl] * (sg[:, sl] * (1.0 + zz[:, sl] * (1.0 - sg[:, sl]))))
            dws.append(jnp.sum(d * tn, axis=0, keepdims=True))
        _acc_into(ao[0], jnp.concatenate(dws, axis=1), first)
    return _Epi(fn, [(y, SSM_INNER, 0), (z, SSM_INNER, 0)], [w], [(SSM_INNER, F32), (SSM_INNER, BF16)],
                [SSM_INNER], tiled=True)


def _to_pat(a, d):
    if d == 1:
        return a
    S, C = a.shape
    return a.reshape(S // d, d, C).transpose(1, 0, 2).reshape(S, C)


def _from_pat(a, d):
    if d == 1:
        return a
    S, C = a.shape
    return a.reshape(d, S // d, C).transpose(1, 0, 2).reshape(S, C)


def _head_col(stat, h):
    return stat[:, h:h + 1]


def _attn_fwd(q, k, v, d):
    S = q.shape[0]
    blk = ATT_BLOCK
    nblk = S // blk
    nbs = nblk // d
    slopes = _alibi_slopes(N_ATT_HEADS)
    scale = HEAD_DIM ** -0.5

    def body(q_ref, kc_ref, kp_ref, vc_ref, vp_ref, o_ref, m_ref, l_ref):
        n = pl.program_id(0)
        has_prev = (n % nbs) != 0
        ii = lax.broadcasted_iota(jnp.int32, (blk, blk), 0)
        jj = lax.broadcasted_iota(jnp.int32, (blk, blk), 1)
        dist_c = (ii - jj).astype(F32)
        dist_p = dist_c + float(blk)
        ok_c = ii >= jj
        ok_p = jnp.logical_and(jj >= ii, has_prev)
        lane = lax.broadcasted_iota(jnp.int32, (blk, LANES), 1)
        m_all = jnp.zeros((blk, LANES), F32)
        l_all = jnp.zeros((blk, LANES), F32)
        for h in range(N_ATT_HEADS):
            sl = slice(h * HEAD_DIM, (h + 1) * HEAD_DIM)
            qh = q_ref[:, sl]
            bias = slopes[h] * float(d)
            sc = jnp.where(ok_c, _dot_nt(qh, kc_ref[:, sl]) * scale - bias * dist_c, NEG_BIG)
            sp = jnp.where(ok_p, _dot_nt(qh, kp_ref[:, sl]) * scale - bias * dist_p, NEG_BIG)
            m = jnp.maximum(jnp.max(sc, axis=-1, keepdims=True), jnp.max(sp, axis=-1, keepdims=True))
            pc = jnp.exp(sc - m)
            pp = jnp.exp(sp - m)
            l = jnp.sum(pc, axis=-1, keepdims=True) + jnp.sum(pp, axis=-1, keepdims=True)
            o_ref[:, sl] = _dot(_b(pc), vc_ref[:, sl]) + _dot(_b(pp), vp_ref[:, sl])
            m_all = jnp.where(lane == h, m, m_all)
            l_all = jnp.where(lane == h, l, l_all)
        m_ref[...] = m_all
        l_ref[...] = l_all

    cur = pl.BlockSpec((blk, ATT_WIDTH), lambda n: (n, 0))
    prev = pl.BlockSpec((blk, ATT_WIDTH), lambda n: (jnp.maximum(n - 1, 0), 0))
    stat = pl.BlockSpec((blk, LANES), lambda n: (n, 0))
    return pl.pallas_call(
        body, name=f"attn_fwd_d{d}", grid=(nblk,),
        in_specs=[cur, cur, prev, cur, prev],
        out_specs=[cur, stat, stat],
        out_shape=[jax.ShapeDtypeStruct((S, ATT_WIDTH), F32), jax.ShapeDtypeStruct((S, LANES), F32),
                   jax.ShapeDtypeStruct((S, LANES), F32)],
        compiler_params=_params(("parallel",)),
    )(q, k, k, v, v)


def _attn_combine(os, ms, ls):
    def body(o1, o2, o3, m1, m2, m3, l1, l2, l3, att_ref, lse_ref):
        mm = [m1[...], m2[...], m3[...]]
        big = jnp.maximum(jnp.maximum(mm[0], mm[1]), mm[2])
        es = [jnp.exp(m - big) for m in mm]
        den = es[0] * l1[...] + es[1] * l2[...] + es[2] * l3[...]
        lse_ref[...] = big + jnp.log(den)
        inv = 1.0 / den
        for h in range(N_ATT_HEADS):
            sl = slice(h * HEAD_DIM, (h + 1) * HEAD_DIM)
            num = (_head_col(es[0], h) * o1[:, sl] + _head_col(es[1], h) * o2[:, sl]
                   + _head_col(es[2], h) * o3[:, sl])
            att_ref[:, sl] = num * _head_col(inv, h)
    return _row_call(body, list(os) + list(ms) + list(ls), [], [(ATT_WIDTH, F32), (LANES, F32)], [], 256,
                     "attn_combine")


def _attn_delta(d_att, att):
    def body(d_ref, a_ref, dl_ref, db_ref):
        dd = d_ref[...]
        prod = dd * a_ref[...]
        lane = lax.broadcasted_iota(jnp.int32, (dd.shape[0], LANES), 1)
        acc = jnp.zeros((dd.shape[0], LANES), F32)
        for h in range(N_ATT_HEADS):
            s = jnp.sum(prod[:, h * HEAD_DIM:(h + 1) * HEAD_DIM], axis=-1, keepdims=True)
            acc = jnp.where(lane == h, s, acc)
        dl_ref[...] = acc
        db_ref[...] = _b(dd)
    return _row_call(body, [d_att, att], [], [(LANES, F32), (ATT_WIDTH, BF16)], [], 512, "attn_delta")


def _attn_bwd(q, k, v, do, lse, delta, d):
    S = q.shape[0]
    blk = ATT_BLOCK
    nblk = S // blk
    nbs = nblk // d
    slopes = _alibi_slopes(N_ATT_HEADS)
    scale = HEAD_DIM ** -0.5

    def body(qc_ref, qn_ref, k_ref, v_ref, doc_ref, don_ref, lc_ref, ln_ref, dc_ref, dn_ref,
             dq_ref, dk_ref, dv_ref, carry_ref):
        n = pl.program_id(0)
        has_next = ((n + 1) % nbs) != 0

        @pl.when(n == 0)
        def _():
            carry_ref[...] = jnp.zeros_like(carry_ref)

        ii = lax.broadcasted_iota(jnp.int32, (blk, blk), 0)
        jj = lax.broadcasted_iota(jnp.int32, (blk, blk), 1)
        dist_c = (ii - jj).astype(F32)
        dist_p = dist_c + float(blk)
        ok_c = ii >= jj
        ok_p = jnp.logical_and(jj >= ii, has_next)
        for h in range(N_ATT_HEADS):
            sl = slice(h * HEAD_DIM, (h + 1) * HEAD_DIM)
            bias = slopes[h] * float(d)
            kh = k_ref[:, sl]
            vh = v_ref[:, sl]
            qh = qc_ref[:, sl]
            doh = doc_ref[:, sl]
            s = jnp.where(ok_c, _dot_nt(qh, kh) * scale - bias * dist_c - _head_col(lc_ref[...], h), NEG_BIG)
            p = jnp.exp(s)
            ds = p * (_dot_nt(doh, vh) - _head_col(dc_ref[...], h)) * scale
            pb, dsb = _b(p), _b(ds)
            dv = _dot_tn(pb, doh)
            dk = _dot_tn(dsb, qh)
            dq_ref[:, sl] = _dot(dsb, kh) + carry_ref[:, sl]
            qh = qn_ref[:, sl]
            doh = don_ref[:, sl]
            s = jnp.where(ok_p, _dot_nt(qh, kh) * scale - bias * dist_p - _head_col(ln_ref[...], h), NEG_BIG)
            p = jnp.exp(s)
            ds = p * (_dot_nt(doh, vh) - _head_col(dn_ref[...], h)) * scale
            pb, dsb = _b(p), _b(ds)
            dv_ref[:, sl] = dv + _dot_tn(pb, doh)
            dk_ref[:, sl] = dk + _dot_tn(dsb, qh)
            carry_ref[:, sl] = _dot(dsb, kh)

    cur = pl.BlockSpec((blk, ATT_WIDTH), lambda n: (n, 0))
    nxt = pl.BlockSpec((blk, ATT_WIDTH), lambda n: (jnp.minimum(n + 1, nblk - 1), 0))
    scur = pl.BlockSpec((blk, LANES), lambda n: (n, 0))
    snxt = pl.BlockSpec((blk, LANES), lambda n: (jnp.minimum(n + 1, nblk - 1), 0))
    shp = jax.ShapeDtypeStruct((S, ATT_WIDTH), F32)
    return pl.pallas_call(
        body, name=f"attn_bwd_d{d}", grid=(nblk,),
        in_specs=[cur, nxt, cur, cur, cur, nxt, scur, snxt, scur, snxt],
        out_specs=[cur, cur, cur],
        out_shape=[shp, shp, shp],
        scratch_shapes=[pltpu.VMEM((blk, ATT_WIDTH), F32)],
        compiler_params=_params(("arbitrary",)),
    )(q, q, k, v, do, do, lse, lse, delta, delta)


def _head_pair_masks(x):
    lane = lax.broadcasted_iota(jnp.int32, x.shape, 1)
    zero = jnp.zeros_like(x)
    return jnp.where(lane < HEAD_DIM, x, zero), jnp.where(lane >= HEAD_DIM, x, zero)


ATT_QUERY_ROWS = 32


def _attn_fwd2(qkv, d, comm=None):
    S = qkv.shape[0]
    blk = ATT_BLOCK
    nblk = S // blk
    nbs = nblk // d
    slopes = _alibi_slopes(N_ATT_HEADS)
    scale = HEAD_DIM ** -0.5
    n_ci = len(comm.ins) if comm else 0
    n_co = len(comm.outs) if comm else 0

    def body(*refs):
        q_ref, kc_ref, kp_ref, vc_ref, vp_ref = refs[:5]
        o_ref, m_ref, l_ref = refs[5 + n_ci:8 + n_ci]
        n = pl.program_id(0)
        if comm:
            c_args = (refs[5:5 + n_ci], refs[8 + n_ci:8 + n_ci + n_co], refs[-2], refs[-1])

            @pl.when(n == 0)
            def _():
                comm.start(*c_args)

        has_prev = (n % nbs) != 0
        ii = lax.broadcasted_iota(jnp.int32, (blk, 2 * blk), 0)
        jj = lax.broadcasted_iota(jnp.int32, (blk, 2 * blk), 1)
        dist_i = blk + ii - jj
        dist = dist_i.astype(F32)
        ok = jnp.logical_and(jnp.logical_and(dist_i >= 0, dist_i <= blk), jnp.logical_or(jj >= blk, has_prev))
        s_scr, p_scr = refs[8 + n_ci + n_co], refs[9 + n_ci + n_co]
        lane = lax.broadcasted_iota(jnp.int32, (blk, LANES), 1)
        for pr in range(N_ATT_HEADS // 2):
            sl = slice(pr * LANES, (pr + 1) * LANES)
            kcat = jnp.concatenate([kp_ref[:, sl], kc_ref[:, sl]], axis=0)
            for h, qh in zip((2 * pr, 2 * pr + 1), _head_pair_masks(q_ref[:, sl])):
                s_scr[h] = _dot_nt(qh, kcat)
        m_all = jnp.zeros((blk, LANES), F32)
        l_all = jnp.zeros((blk, LANES), F32)
        for h in range(N_ATT_HEADS):
            s = jnp.where(ok, s_scr[h] * scale - (slopes[h] * float(d)) * dist, NEG_BIG)
            m = jnp.max(s, axis=-1, keepdims=True)
            p = jnp.exp(s - m)
            l = jnp.sum(p, axis=-1, keepdims=True)
            m_all = jnp.where(lane == h, m, m_all)
            l_all = jnp.where(lane == h, l, l_all)
            p_scr[:, h * 2 * blk:(h + 1) * 2 * blk] = _b(p)
        for pr in range(N_ATT_HEADS // 2):
            sl = slice(pr * LANES, (pr + 1) * LANES)
            vmask = jnp.concatenate(
                _head_pair_masks(jnp.concatenate([vp_ref[:, sl], vc_ref[:, sl]], axis=0)), axis=0)
            o_ref[:, sl] = _dot(p_scr[:, pr * 4 * blk:(pr + 1) * 4 * blk], vmask)
        m_ref[...] = m_all
        l_ref[...] = l_all
        if comm:
            @pl.when(n == nblk - 1)
            def _():
                comm.finish(*c_args)

    cur = lambda c: pl.BlockSpec((blk, ATT_WIDTH), lambda n: (n, c))
    prev = lambda c: pl.BlockSpec((blk, ATT_WIDTH), lambda n: (jnp.maximum(n - 1, 0), c))
    stat = pl.BlockSpec((blk, LANES), lambda n: (n, 0))
    scratch = [pltpu.VMEM((N_ATT_HEADS, blk, 2 * blk), F32), pltpu.VMEM((blk, N_ATT_HEADS * 2 * blk), BF16)]
    if comm:
        scratch += [pltpu.SemaphoreType.DMA((comm.n_sems,))] * 2
        params = pltpu.CompilerParams(dimension_semantics=("arbitrary",), vmem_limit_bytes=VMEM_LIMIT,
                                      has_side_effects=True)
    else:
        params = _params(("parallel",))
    outs = pl.pallas_call(
        body, name=f"attn_fwd_d{d}", grid=(nblk,),
        in_specs=[cur(0), cur(1), prev(1), cur(2), prev(2)] + [ANY] * n_ci,
        out_specs=[cur(0), stat, stat] + [ANY] * n_co,
        out_shape=[jax.ShapeDtypeStruct((S, ATT_WIDTH), F32), jax.ShapeDtypeStruct((S, LANES), F32),
                   jax.ShapeDtypeStruct((S, LANES), F32)] + list(comm.outs if comm else []),
        scratch_shapes=scratch,
        compiler_params=params,
    )(qkv, qkv, qkv, qkv, qkv, *(comm.ins if comm else []))
    return (outs[0], outs[1], outs[2], outs[3:]) if comm else outs


def _attn_bwd2(qkv, do, lse, delta, d, comm=None):
    S = qkv.shape[0]
    blk = ATT_BLOCK
    nblk = S // blk
    nbs = nblk // d
    slopes = _alibi_slopes(N_ATT_HEADS)
    scale = HEAD_DIM ** -0.5
    n_ci = len(comm.ins) if comm else 0
    n_co = len(comm.outs) if comm else 0

    def body(*refs):
        qc_ref, qn_ref, k_ref, v_ref, doc_ref, don_ref, lc_ref, ln_ref, dc_ref, dn_ref = refs[:10]
        dq_ref, dk_ref, dv_ref = refs[10 + n_ci:13 + n_ci]
        carry_ref = refs[13 + n_ci + n_co]
        n = pl.program_id(0)
        has_next = ((n + 1) % nbs) != 0
        if comm:
            c_args = (refs[10:10 + n_ci], refs[13 + n_ci:13 + n_ci + n_co], refs[-2], refs[-1])

        @pl.when(n == 0)
        def _():
            carry_ref[...] = jnp.zeros_like(carry_ref)
            if comm:
                comm.start(*c_args)

        rr = lax.broadcasted_iota(jnp.int32, (2 * blk, blk), 0)
        jj = lax.broadcasted_iota(jnp.int32, (2 * blk, blk), 1)
        dist_i = rr - jj
        dist = dist_i.astype(F32)
        ok = jnp.logical_or(jnp.logical_and(rr < blk, dist_i >= 0),
                            jnp.logical_and(jnp.logical_and(rr >= blk, dist_i <= blk), has_next))
        s_scr, dp_scr, p_rows, ds_rows, ds_cols = refs[14 + n_ci + n_co:19 + n_ci + n_co]
        lcat = jnp.concatenate([lc_ref[...], ln_ref[...]], axis=0)
        dcat = jnp.concatenate([dc_ref[...], dn_ref[...]], axis=0)
        rows2 = 2 * blk

        def operands(pr):
            sl = slice(pr * LANES, (pr + 1) * LANES)
            qm = _head_pair_masks(jnp.concatenate([qc_ref[:, sl], qn_ref[:, sl]], axis=0))
            dom = _head_pair_masks(jnp.concatenate([doc_ref[:, sl], don_ref[:, sl]], axis=0))
            return sl, qm, dom

        for pr in range(N_ATT_HEADS // 2):
            sl, qm, dom = operands(pr)
            for h, qh, doh in zip((2 * pr, 2 * pr + 1), qm, dom):
                s_scr[h] = _dot_nt(qh, k_ref[:, sl])
                dp_scr[h] = _dot_nt(doh, v_ref[:, sl])
        for h in range(N_ATT_HEADS):
            s = jnp.where(ok, s_scr[h] * scale - (slopes[h] * float(d)) * dist - lcat[:, h:h + 1], NEG_BIG)
            p = jnp.exp(s)
            dsb = _b(p * (dp_scr[h] - dcat[:, h:h + 1]) * scale)
            p_rows[h * rows2:(h + 1) * rows2, :] = _b(p)
            ds_rows[h * rows2:(h + 1) * rows2, :] = dsb
            ds_cols[:, h * blk:(h + 1) * blk] = dsb
        for pr in range(N_ATT_HEADS // 2):
            sl, qm, dom = operands(pr)
            pair_rows = slice(pr * 2 * rows2, (pr + 1) * 2 * rows2)
            dv_ref[:, sl] = _b(_dot_tn(p_rows[pair_rows, :], jnp.concatenate(dom, axis=0)))
            dk_ref[:, sl] = _b(_dot_tn(ds_rows[pair_rows, :], jnp.concatenate(qm, axis=0)))
            dq = _dot(ds_cols[:, pr * 2 * blk:(pr + 1) * 2 * blk],
                      jnp.concatenate(_head_pair_masks(k_ref[:, sl]), axis=0))
            dq_ref[:, sl] = _b(dq[:blk] + carry_ref[:, sl])
            carry_ref[:, sl] = dq[blk:]

        if comm:
            @pl.when(n == nblk - 1)
            def _():
                comm.finish(*c_args)

    cur = lambda c: pl.BlockSpec((blk, ATT_WIDTH), lambda n: (n, c))
    nxt = lambda c: pl.BlockSpec((blk, ATT_WIDTH), lambda n: (jnp.minimum(n + 1, nblk - 1), c))
    scur = pl.BlockSpec((blk, LANES), lambda n: (n, 0))
    snxt = pl.BlockSpec((blk, LANES), lambda n: (jnp.minimum(n + 1, nblk - 1), 0))
    shp = jax.ShapeDtypeStruct((S, ATT_WIDTH), BF16)
    scratch = [pltpu.VMEM((blk, ATT_WIDTH), F32),
               pltpu.VMEM((N_ATT_HEADS, 2 * blk, blk), F32), pltpu.VMEM((N_ATT_HEADS, 2 * blk, blk), F32),
               pltpu.VMEM((N_ATT_HEADS * 2 * blk, blk), BF16), pltpu.VMEM((N_ATT_HEADS * 2 * blk, blk), BF16),
               pltpu.VMEM((2 * blk, N_ATT_HEADS * blk), BF16)]
    if comm:
        scratch += [pltpu.SemaphoreType.DMA((comm.n_sems,))] * 2
        params = pltpu.CompilerParams(dimension_semantics=("arbitrary",), vmem_limit_bytes=VMEM_LIMIT,
                                      has_side_effects=True)
    else:
        params = _params(("arbitrary",))
    outs = pl.pallas_call(
        body, name=f"attn_bwd_d{d}", grid=(nblk,),
        in_specs=[cur(0), nxt(0), cur(1), cur(2), cur(0), nxt(0), scur, snxt, scur, snxt] + [ANY] * n_ci,
        out_specs=[cur(0), cur(0), cur(0)] + [ANY] * n_co,
        out_shape=[shp, shp, shp] + list(comm.outs if comm else []),
        scratch_shapes=scratch,
        compiler_params=params,
    )(qkv, qkv, qkv, qkv, do, do, lse, lse, delta, delta, *(comm.ins if comm else []))
    return (outs[0], outs[1], outs[2], outs[3:]) if comm else outs


LAYOUT_TILE = 512
DILATED = tuple(d for d in DILATIONS if d > 1)


def _pat_spec(d, cols, col_block=0):
    return pl.BlockSpec((d, LAYOUT_TILE // d, cols), lambda i: (0, i, col_block))


def _pat_view(a, d):
    return a.reshape(d, a.shape[0] // d, a.shape[1])


def _qkv_layouts(qkv):
    S, C = qkv.shape
    t = LAYOUT_TILE

    def body(x_ref, nat_ref, *refs):
        pat_refs, slab = refs[:-1], refs[-1]
        nat_ref[...] = _b(x_ref[...])
        _to_slabs(slab, x_ref)
        for d, p_ref in zip(DILATED, pat_refs):
            _gather_pattern(p_ref, slab, d, BF16)

    outs = pl.pallas_call(
        body, name="qkv_layouts", grid=(S // t,),
        in_specs=[pl.BlockSpec((t, C), lambda i: (i, 0))],
        out_specs=[pl.BlockSpec((t, C), lambda i: (i, 0))] + [_pat_spec(d, C) for d in DILATED],
        out_shape=[jax.ShapeDtypeStruct((S, C), BF16)]
        + [jax.ShapeDtypeStruct((d, S // d, C), BF16) for d in DILATED],
        scratch_shapes=[pltpu.VMEM((C // LANES, t, LANES), F32)],
        compiler_params=_params(("parallel",)),
    )(qkv)
    return [outs[0]] + [o.reshape(S, C) for o in outs[1:]]


def _to_slabs(slab_ref, src_ref):
    for cb in range(slab_ref.shape[0]):
        slab_ref[cb] = src_ref[:, cb * LANES:(cb + 1) * LANES].astype(F32)


def _gather_pattern(dst_ref, slab_ref, d, dtype):
    t = slab_ref.shape[1]
    for cb in range(slab_ref.shape[0]):
        one = slab_ref.at[cb]
        for r in range(d):
            dst_ref[r, :, cb * LANES:(cb + 1) * LANES] = one[pl.ds(r, t // d, stride=d), :].astype(dtype)


def _scatter_pattern(slab_ref, src_ref, d, add=False):
    t = slab_ref.shape[1]
    for cb in range(slab_ref.shape[0]):
        one = slab_ref.at[cb]
        for r in range(d):
            idx = pl.ds(r, t // d, stride=d)
            val = src_ref[r, :, cb * LANES:(cb + 1) * LANES]
            if add:
                val = val + one[idx, :]
            one[idx, :] = val


def _attn_combine2(os, ms, ls):
    S = os[0].shape[0]
    t = LAYOUT_TILE

    def body(o1, o2, o3, m1, m2, m3, l1, l2, l3, att_ref, lse_ref, so2, so3, sm2, sm3, sl2, sl3):
        for d, src, dst in ((DILATED[0], o2, so2), (DILATED[1], o3, so3), (DILATED[0], m2, sm2),
                            (DILATED[1], m3, sm3), (DILATED[0], l2, sl2), (DILATED[1], l3, sl3)):
            _scatter_pattern(dst, src, d)
        mm = [m1[...], sm2[0], sm3[0]]
        big = jnp.maximum(jnp.maximum(mm[0], mm[1]), mm[2])
        es = [jnp.exp(m - big) for m in mm]
        den = es[0] * l1[...] + es[1] * sl2[0] + es[2] * sl3[0]
        lse_ref[...] = big + jnp.log(den)
        inv = 1.0 / den
        for h in range(N_ATT_HEADS):
            sl = slice(h * HEAD_DIM, (h + 1) * HEAD_DIM)
            cb, hl = divmod(h, 2)
            sll = slice(hl * HEAD_DIM, (hl + 1) * HEAD_DIM)
            num = (_head_col(es[0], h) * o1[:, sl] + _head_col(es[1], h) * so2[cb, :, sll]
                   + _head_col(es[2], h) * so3[cb, :, sll])
            att_ref[:, sl] = num * _head_col(inv, h)

    def specs(c):
        return [pl.BlockSpec((t, c), lambda i: (i, 0))] + [_pat_spec(d, c) for d in DILATED]

    args = [os[0]] + [_pat_view(o, d) for o, d in zip(os[1:], DILATED)]
    args += [ms[0]] + [_pat_view(m, d) for m, d in zip(ms[1:], DILATED)]
    args += [ls[0]] + [_pat_view(l, d) for l, d in zip(ls[1:], DILATED)]
    return pl.pallas_call(
        body, name="attn_combine", grid=(S // t,),
        in_specs=specs(ATT_WIDTH) + specs(LANES) + specs(LANES),
        out_specs=[pl.BlockSpec((t, ATT_WIDTH), lambda i: (i, 0)), pl.BlockSpec((t, LANES), lambda i: (i, 0))],
        out_shape=[jax.ShapeDtypeStruct((S, ATT_WIDTH), F32), jax.ShapeDtypeStruct((S, LANES), F32)],
        scratch_shapes=[pltpu.VMEM((ATT_WIDTH // LANES, t, LANES), F32)] * 2
        + [pltpu.VMEM((1, t, LANES), F32)] * 4,
        compiler_params=_params(("parallel",)),
    )(*args)


def _attn_delta2(d_att, att, lse):
    S = d_att.shape[0]
    t = LAYOUT_TILE

    def body(d_ref, a_ref, l_ref, *refs):
        out_refs, d_slab, l_slab, dl_slab = refs[:-3], refs[-3], refs[-2], refs[-1]
        dd = d_ref[...]
        prod = dd * a_ref[...]
        lane = lax.broadcasted_iota(jnp.int32, (t, LANES), 1)
        acc = jnp.zeros((t, LANES), F32)
        for h in range(N_ATT_HEADS):
            s = jnp.sum(prod[:, h * HEAD_DIM:(h + 1) * HEAD_DIM], axis=-1, keepdims=True)
            acc = jnp.where(lane == h, s, acc)
        out_refs[0][...] = _b(dd)
        out_refs[1][...] = acc
        _to_slabs(d_slab, d_ref)
        l_slab[0] = l_ref[...]
        dl_slab[0] = acc
        for k, d in enumerate(DILATED):
            db_ref, ls_ref, dl_ref = out_refs[2 + 3 * k:5 + 3 * k]
            _gather_pattern(db_ref, d_slab, d, BF16)
            _gather_pattern(ls_ref, l_slab, d, F32)
            _gather_pattern(dl_ref, dl_slab, d, F32)

    nat = lambda c: pl.BlockSpec((t, c), lambda i: (i, 0))
    out_specs = [nat(ATT_WIDTH), nat(LANES)]
    out_shape = [jax.ShapeDtypeStruct((S, ATT_WIDTH), BF16), jax.ShapeDtypeStruct((S, LANES), F32)]
    for d in DILATED:
        out_specs += [_pat_spec(d, ATT_WIDTH), _pat_spec(d, LANES), _pat_spec(d, LANES)]
        out_shape += [jax.ShapeDtypeStruct((d, S // d, ATT_WIDTH), BF16),
                      jax.ShapeDtypeStruct((d, S // d, LANES), F32),
                      jax.ShapeDtypeStruct((d, S // d, LANES), F32)]
    outs = pl.pallas_call(
        body, name="attn_delta", grid=(S // t,),
        in_specs=[nat(ATT_WIDTH), nat(ATT_WIDTH), nat(LANES)],
        out_specs=out_specs, out_shape=out_shape,
        scratch_shapes=[pltpu.VMEM((ATT_WIDTH // LANES, t, LANES), F32), pltpu.VMEM((1, t, LANES), F32),
                        pltpu.VMEM((1, t, LANES), F32)],
        compiler_params=_params(("parallel",)),
    )(d_att, att, lse)
    res = [(outs[0], lse, outs[1])]
    for k in range(len(DILATED)):
        db, ls, dl = outs[2 + 3 * k:5 + 3 * k]
        res.append((db.reshape(S, ATT_WIDTH), ls.reshape(S, LANES), dl.reshape(S, LANES)))
    return res


def _sum_qkv2(dqs, dks, dvs):
    S = dqs[0].shape[0]
    t = LAYOUT_TILE

    def body(*refs):
        o_ref, scr = refs[-2], refs[-1]
        for part in range(3):
            nat_ref, p_refs = refs[3 * part], refs[3 * part + 1:3 * part + 3]
            _to_slabs(scr, nat_ref)
            for d, p_ref in zip(DILATED, p_refs):
                _scatter_pattern(scr, p_ref, d, add=True)
            for cb in range(ATT_WIDTH // LANES):
                o_ref[:, part * ATT_WIDTH + cb * LANES:part * ATT_WIDTH + (cb + 1) * LANES] = _b(scr[cb])

    in_specs, args = [], []
    for group in (dqs, dks, dvs):
        in_specs += [pl.BlockSpec((t, ATT_WIDTH), lambda i: (i, 0))] + [_pat_spec(d, ATT_WIDTH) for d in DILATED]
        args += [group[0]] + [_pat_view(a, d) for a, d in zip(group[1:], DILATED)]
    return pl.pallas_call(
        body, name="sum_dqkv", grid=(S // t,),
        in_specs=in_specs,
        out_specs=pl.BlockSpec((t, 3 * ATT_WIDTH), lambda i: (i, 0)),
        out_shape=jax.ShapeDtypeStruct((S, 3 * ATT_WIDTH), BF16),
        scratch_shapes=[pltpu.VMEM((ATT_WIDTH // LANES, t, LANES), F32)],
        compiler_params=_params(("parallel",)),
    )(*args)


def _sum_qkv(dqs, dks, dvs):
    def body(q1, q2, q3, k1, k2, k3, v1, v2, v3, o_ref):
        o_ref[:, 0:ATT_WIDTH] = _b(q1[...] + q2[...] + q3[...])
        o_ref[:, ATT_WIDTH:2 * ATT_WIDTH] = _b(k1[...] + k2[...] + k3[...])
        o_ref[:, 2 * ATT_WIDTH:] = _b(v1[...] + v2[...] + v3[...])
    return _row_call(body, list(dqs) + list(dks) + list(dvs), [], [(3 * ATT_WIDTH, BF16)], [], 256,
                     "sum_dqkv")[0]


CONV_COLS = 1024
CONV_ROWS = 512
HALO = 8


def _conv_fwd(xbc, conv_w, conv_b):
    S, C = xbc.shape
    bs, bc = CONV_ROWS, CONV_COLS
    nr = S // bs

    def body(x_ref, halo_ref, w_ref, b_ref, o_ref, xs_ref):
        r = pl.program_id(1)
        xs_ref[pl.ds(HALO, bs), :] = x_ref[...]
        xs_ref[pl.ds(0, HALO), :] = jnp.where(r > 0, halo_ref[...], 0.0)
        pre = b_ref[...] + w_ref[3:4, :] * x_ref[...]
        for j in range(SSM_CONV - 1):
            pre = pre + w_ref[j:j + 1, :] * xs_ref[pl.ds(HALO - 3 + j, bs), :]
        o_ref[...] = pre * _sigmoid(pre)

    return pl.pallas_call(
        body, name="conv_fwd", grid=(C // bc, nr),
        in_specs=[pl.BlockSpec((bs, bc), lambda c, r: (r, c)),
                  pl.BlockSpec((HALO, bc), lambda c, r: (jnp.maximum(r * (bs // HALO) - 1, 0), c)),
                  pl.BlockSpec((SSM_CONV, bc), lambda c, r: (0, c)),
                  pl.BlockSpec((1, bc), lambda c, r: (0, c))],
        out_specs=pl.BlockSpec((bs, bc), lambda c, r: (r, c)),
        out_shape=jax.ShapeDtypeStruct((S, C), F32),
        scratch_shapes=[pltpu.VMEM((bs + HALO, bc), F32)],
        compiler_params=_params(("parallel", "arbitrary")),
    )(xbc, xbc, conv_w, conv_b)


def _conv_bwd(xbc, dact, conv_w, conv_b, col0):
    S, C = xbc.shape
    Cp = dact.shape[1]
    bs, bc = CONV_ROWS, min(CONV_COLS, Cp)
    nr = S // bs
    cb0 = col0 // bc
    last_halo = S // HALO - 1

    def body(x_ref, xp_ref, xn_ref, d_ref, dn_ref, w_ref, b_ref, dx_ref, dw_ref, db_ref,
             xs_ref, dp_ref):
        r = pl.program_id(1)
        xs_ref[pl.ds(0, HALO), :] = jnp.where(r > 0, xp_ref[...], 0.0)
        xs_ref[pl.ds(HALO, bs), :] = x_ref[...]
        xs_ref[pl.ds(HALO + bs, HALO), :] = xn_ref[...]
        ext = bs + HALO
        pre = b_ref[...] + jnp.zeros((ext, bc), F32)
        for j in range(SSM_CONV):
            pre = pre + w_ref[j:j + 1, :] * xs_ref[pl.ds(HALO - 3 + j, ext), :]
        sg = _sigmoid(pre)
        dsilu = sg * (1.0 + pre * (1.0 - sg))
        dp_ref[pl.ds(0, bs), :] = d_ref[...] * dsilu[:bs]
        dp_ref[pl.ds(bs, HALO), :] = jnp.where(r < nr - 1, dn_ref[...], 0.0) * dsilu[bs:]
        dx = jnp.zeros((bs, bc), F32)
        for j in range(SSM_CONV):
            dx = dx + w_ref[j:j + 1, :] * dp_ref[pl.ds(3 - j, bs), :]
        dx_ref[...] = _b(dx)
        dpre = dp_ref[pl.ds(0, bs), :]
        for j in range(SSM_CONV):
            part = jnp.sum(dpre * xs_ref[pl.ds(HALO - 3 + j, bs), :], axis=0, keepdims=True)

            @pl.when(r == 0)
            def _():
                dw_ref[j:j + 1, :] = part

            @pl.when(r > 0)
            def _():
                dw_ref[j:j + 1, :] += part
        part = jnp.sum(dpre, axis=0, keepdims=True)

        @pl.when(r == 0)
        def _():
            db_ref[...] = part

        @pl.when(r > 0)
        def _():
            db_ref[...] += part

    hb = bs // HALO
    return pl.pallas_call(
        body, name=f"conv_bwd_{col0}", grid=(Cp // bc, nr),
        in_specs=[pl.BlockSpec((bs, bc), lambda c, r: (r, cb0 + c)),
                  pl.BlockSpec((HALO, bc), lambda c, r: (jnp.maximum(r * hb - 1, 0), cb0 + c)),
                  pl.BlockSpec((HALO, bc), lambda c, r: (jnp.minimum((r + 1) * hb, last_halo), cb0 + c)),
                  pl.BlockSpec((bs, bc), lambda c, r: (r, c)),
                  pl.BlockSpec((HALO, bc), lambda c, r: (jnp.minimum((r + 1) * hb, last_halo), c)),
                  pl.BlockSpec((SSM_CONV, bc), lambda c, r: (0, cb0 + c)),
                  pl.BlockSpec((1, bc), lambda c, r: (0, cb0 + c))],
        out_specs=[pl.BlockSpec((bs, bc), lambda c, r: (r, c)),
                   pl.BlockSpec((SSM_CONV, bc), lambda c, r: (0, c)),
                   pl.BlockSpec((1, bc), lambda c, r: (0, c))],
        out_shape=[jax.ShapeDtypeStruct((S, Cp), BF16), jax.ShapeDtypeStruct((SSM_CONV, Cp), F32),
                   jax.ShapeDtypeStruct((1, Cp), F32)],
        scratch_shapes=[pltpu.VMEM((bs + 2 * HALO, bc), F32), pltpu.VMEM((bs + HALO, bc), F32)],
        compiler_params=_params(("parallel", "arbitrary")),
    )(xbc, xbc, xbc, dact, dact, conv_w, conv_b)


def _shift_down(x, k, top_src):
    r8 = lax.broadcasted_iota(jnp.int32, (HALO, x.shape[1]), 0)
    rolled = pltpu.roll(x, k, 0)
    top = jnp.where(r8 < k, pltpu.roll(top_src, k, 0), rolled[0:HALO])
    if x.shape[0] == HALO:
        return top
    return jnp.concatenate([top, rolled[HALO:]], axis=0)


def _shift_up(x, k, bottom_src):
    n = x.shape[0]
    r8 = lax.broadcasted_iota(jnp.int32, (HALO, x.shape[1]), 0)
    rolled = pltpu.roll(x, n - k, 0)
    bottom = jnp.where(r8 >= HALO - k, pltpu.roll(bottom_src, HALO - k, 0), rolled[n - HALO:n])
    return jnp.concatenate([rolled[:n - HALO], bottom], axis=0)


def _conv_pre(x, top_src, w_ref, b_ref):
    shifted = [x] + [_shift_down(x, k, top_src) for k in range(1, SSM_CONV)]
    pre = b_ref[...] + w_ref[SSM_CONV - 1:SSM_CONV, :] * x
    for k in range(1, SSM_CONV):
        pre = pre + w_ref[SSM_CONV - 1 - k:SSM_CONV - k, :] * shifted[k]
    return pre, shifted


def _conv_fwd2(xbc, conv_w, conv_b):
    S, C = xbc.shape
    bs, bc = CONV_ROWS, CONV_COLS
    nr = S // bs

    def body(x_ref, halo_ref, w_ref, b_ref, o_ref, pre_ref):
        r = pl.program_id(1)
        halo = jnp.where(r > 0, halo_ref[...], 0.0)
        pre, _ = _conv_pre(x_ref[...], halo, w_ref, b_ref)
        pre_ref[...] = _b(pre)
        o_ref[...] = pre * _sigmoid(pre)

    tile = pl.BlockSpec((bs, bc), lambda c, r: (r, c))
    return pl.pallas_call(
        body, name="conv_fwd", grid=(C // bc, nr),
        in_specs=[tile,
                  pl.BlockSpec((HALO, bc), lambda c, r: (jnp.maximum(r * (bs // HALO) - 1, 0), c)),
                  pl.BlockSpec((SSM_CONV, bc), lambda c, r: (0, c)),
                  pl.BlockSpec((1, bc), lambda c, r: (0, c))],
        out_specs=[tile, tile],
        out_shape=[jax.ShapeDtypeStruct((S, C), F32), jax.ShapeDtypeStruct((S, C), BF16)],
        compiler_params=_params(("parallel", "arbitrary")),
    )(xbc, xbc, conv_w, conv_b)


def _conv_bwd2(xbc, pre_all, dact, conv_w):
    S, C = xbc.shape
    bs, bc = CONV_ROWS, CONV_COLS
    nr = S // bs
    hb = bs // HALO
    last_halo = S // HALO - 1

    def dsilu(pre):
        sg = _sigmoid(pre)
        return sg * (1.0 + pre * (1.0 - sg))

    def body(x_ref, p_ref, pn_ref, d_ref, dn_ref, w_ref, dx_ref, dw_ref, db_ref):
        r = pl.program_id(1)
        x = x_ref[...]
        dpre = d_ref[...] * dsilu(p_ref[...].astype(F32))
        dpre_n = jnp.where(r < nr - 1, dn_ref[...], 0.0) * dsilu(pn_ref[...].astype(F32)[0:HALO])
        ups = [dpre] + [_shift_up(dpre, k, dpre_n) for k in range(1, SSM_CONV)]
        dx = w_ref[SSM_CONV - 1:SSM_CONV, :] * dpre
        for k in range(1, SSM_CONV):
            dx = dx + w_ref[SSM_CONV - 1 - k:SSM_CONV - k, :] * ups[k]
        dx_ref[...] = _b(dx)
        parts = [jnp.sum(x * ups[SSM_CONV - 1 - j], axis=0, keepdims=True) for j in range(SSM_CONV)]
        dbp = jnp.sum(dpre, axis=0, keepdims=True)

        @pl.when(r == 0)
        def _():
            for j in range(SSM_CONV):
                dw_ref[j:j + 1, :] = parts[j]
            db_ref[...] = dbp

        @pl.when(r > 0)
        def _():
            for j in range(SSM_CONV):
                dw_ref[j:j + 1, :] += parts[j]
            db_ref[...] += dbp

    tile = pl.BlockSpec((bs, bc), lambda c, r: (r, c))
    nxt = pl.BlockSpec((HALO, bc), lambda c, r: (jnp.minimum((r + 1) * hb, last_halo), c))
    nxt16 = pl.BlockSpec((BF16_ROWS, bc), lambda c, r: (
        jnp.minimum((r + 1) * (bs // BF16_ROWS), S // BF16_ROWS - 1), c))
    return pl.pallas_call(
        body, name="conv_bwd", grid=(C // bc, nr),
        in_specs=[tile, tile, nxt16, tile, nxt, pl.BlockSpec((SSM_CONV, bc), lambda c, r: (0, c))],
        out_specs=[tile,
                   pl.BlockSpec((SSM_CONV, bc), lambda c, r: (0, c)),
                   pl.BlockSpec((1, bc), lambda c, r: (0, c))],
        out_shape=[jax.ShapeDtypeStruct((S, C), BF16), jax.ShapeDtypeStruct((SSM_CONV, C), F32),
                   jax.ShapeDtypeStruct((1, C), F32)],
        compiler_params=_params(("parallel", "arbitrary")),
    )(xbc, pre_all, pre_all, dact, dact, conv_w)


def _softplus(x):
    return jnp.maximum(x, 0.0) + jnp.log(1.0 + jnp.exp(-jnp.abs(x)))


def _ssd_common(dtr_ref, bias_ref, a_ref, g):
    ch = SSM_CHUNK
    x = dtr_ref[...] + bias_ref[...]
    dt_all = _softplus(x)
    r = lax.broadcasted_iota(jnp.int32, (LANES, LANES), 0)
    c = lax.broadcasted_iota(jnp.int32, (LANES, LANES), 1)
    sel = jnp.where(jnp.logical_and(r == HEADS_PER_GROUP * g + c, c < HEADS_PER_GROUP), 1.0, 0.0)
    dt4 = _dot_hi(dt_all, sel)
    la4 = _dot_hi(dt_all * a_ref[...], sel)
    ii = lax.broadcasted_iota(jnp.int32, (ch, ch), 0)
    jj = lax.broadcasted_iota(jnp.int32, (ch, ch), 1)
    tril = jnp.where(ii >= jj, 1.0, 0.0)
    acs = _dot_hi(tril, la4)
    return x, sel, dt4, acs, acs.T, ii >= jj


def _row8(v):
    return jnp.broadcast_to(v, (8, v.shape[1]))


def _ssd_fwd(xact, dt_raw, dt_bias, a_neg, d_skip):
    S = xact.shape[0]
    ch = SSM_CHUNK
    nch = S // ch
    hg = HEADS_PER_GROUP
    gw = hg * SSM_HEAD_DIM
    b_off = SSM_INNER // SSM_STATE
    c_off = b_off + SSM_GROUPS

    def body(x_ref, b_ref, c_ref, dtr_ref, bias_ref, a_ref, dsk_ref, y_ref, hs_ref, h_ref):
        c = pl.program_id(0)
        g = pl.program_id(1)

        @pl.when(jnp.logical_and(c == 0, g == 0))
        def _():
            h_ref[...] = jnp.zeros_like(h_ref)

        _, sel, dt4, acs, acs_t, low = _ssd_common(dtr_ref, bias_ref, a_ref, g)
        dsk4 = _dot_hi(_row8(dsk_ref[...]), sel)
        bb = _b(b_ref[...])
        cc = _b(c_ref[...])
        cb = _dot_nt(cc, bb)
        for j in range(hg):
            sl = slice(j * SSM_HEAD_DIM, (j + 1) * SSM_HEAD_DIM)
            acol = acs[:, j:j + 1]
            arow = acs_t[j:j + 1, :]
            alast = acs[ch - 1:ch, j:j + 1]
            decay = jnp.exp(jnp.where(low, acol - arow, -jnp.inf))
            xh = x_ref[:, sl]
            xd = xh * dt4[:, j:j + 1]
            hj = h_ref[hg * g + j]
            y = _dot(_b(cb * decay), _b(xd))
            y = y + _dot_nt(cc, _b(hj)) * jnp.exp(acol)
            y_ref[:, sl] = y + dsk4[0:1, j:j + 1] * xh
            hs_ref[0, j] = hj
            st = _dot_tn(_b(xd * jnp.exp(alast - acol)), bb)
            h_ref[hg * g + j] = hj * jnp.exp(alast) + st

    small = pl.BlockSpec((1, LANES), lambda c, g: (0, 0))
    return pl.pallas_call(
        body, name="ssd_fwd", grid=(nch, SSM_GROUPS),
        in_specs=[pl.BlockSpec((ch, gw), lambda c, g: (c, g)),
                  pl.BlockSpec((ch, SSM_STATE), lambda c, g: (c, b_off + g)),
                  pl.BlockSpec((ch, SSM_STATE), lambda c, g: (c, c_off + g)),
                  pl.BlockSpec((ch, LANES), lambda c, g: (c, 0)),
                  small, small, small],
        out_specs=[pl.BlockSpec((ch, gw), lambda c, g: (c, g)),
                   pl.BlockSpec((1, hg, SSM_HEAD_DIM, SSM_STATE), lambda c, g: (c, g, 0, 0))],
        out_shape=[jax.ShapeDtypeStruct((S, SSM_INNER), F32),
                   jax.ShapeDtypeStruct((nch, SSM_HEADS, SSM_HEAD_DIM, SSM_STATE), F32)],
        scratch_shapes=[pltpu.VMEM((SSM_HEADS, SSM_HEAD_DIM, SSM_STATE), F32)],
        compiler_params=_params(("arbitrary", "arbitrary")),
    )(xact, xact, xact, dt_raw, dt_bias, a_neg, d_skip)


def _ssd_bwd(xact, dt_raw, dt_bias, a_neg, d_skip, hs, dy):
    S = xact.shape[0]
    ch = SSM_CHUNK
    nch = S // ch
    hg = HEADS_PER_GROUP
    gw = hg * SSM_HEAD_DIM
    b_off = SSM_INNER // SSM_STATE
    c_off = b_off + SSM_GROUPS

    def body(x_ref, b_ref, c_ref, dtr_ref, bias_ref, a_ref, dsk_ref, hs_ref, dy_ref,
             dx_ref, db_ref, dc_ref, ddt_ref, st_ref, dh_ref, ddt_acc):
        step = pl.program_id(0)
        g = pl.program_id(1)

        @pl.when(jnp.logical_and(step == 0, g == 0))
        def _():
            dh_ref[...] = jnp.zeros_like(dh_ref)
            st_ref[...] = jnp.zeros_like(st_ref)

        @pl.when(g == 0)
        def _():
            ddt_acc[...] = jnp.zeros_like(ddt_acc)

        xraw, sel, dt4, acs, acs_t, low = _ssd_common(dtr_ref, bias_ref, a_ref, g)
        a4 = _dot_hi(_row8(a_ref[...]), sel)[0:1, :]
        dsk4 = _dot_hi(_row8(dsk_ref[...]), sel)
        bf = b_ref[...]
        cf = c_ref[...]
        bb = _b(bf)
        cc = _b(cf)
        cb = _dot_nt(cc, bb)
        lane = lax.broadcasted_iota(jnp.int32, (ch, LANES), 1)
        rowi = lax.broadcasted_iota(jnp.int32, (ch, 1), 0)
        ones = jnp.ones((ch, LANES), F32)
        dcb = jnp.zeros((ch, ch), F32)
        dc_acc = jnp.zeros((ch, SSM_STATE), F32)
        db_acc = jnp.zeros((ch, SSM_STATE), F32)
        dacs4 = jnp.zeros((ch, LANES), F32)
        ddt4 = jnp.zeros((ch, LANES), F32)
        dd4 = jnp.zeros((1, LANES), F32)
        lane1 = lax.broadcasted_iota(jnp.int32, (1, LANES), 1)
        for j in range(hg):
            sl = slice(j * SSM_HEAD_DIM, (j + 1) * SSM_HEAD_DIM)
            acol = acs[:, j:j + 1]
            arow = acs_t[j:j + 1, :]
            alast = acs[ch - 1:ch, j:j + 1]
            decay = jnp.exp(jnp.where(low, acol - arow, -jnp.inf))
            ea = jnp.exp(acol)
            dsd = jnp.exp(alast - acol)
            cd = jnp.exp(alast)
            dtc = dt4[:, j:j + 1]
            xh = x_ref[:, sl]
            xd = xh * dtc
            xdb = _b(xd)
            hj = hs_ref[0, j]
            hjb = _b(hj)
            dhn = dh_ref[hg * g + j]
            dyj = dy_ref[:, sl]
            dyb = _b(dyj)
            lm = cb * decay
            dxh = dsk4[0:1, j:j + 1] * dyj
            dd4 = jnp.where(lane1 == j, jnp.sum(jnp.sum(dyj * xh, axis=1, keepdims=True), axis=0,
                                                keepdims=True), dd4)
            dlm = _dot_nt(dyb, xdb)
            dxd = _dot_tn(_b(lm), dyb)
            gm = dlm * lm
            dcb = dcb + dlm * decay
            dac = jnp.sum(gm, axis=1, keepdims=True) - _dot_tn_hi(gm, ones)[:, 0:1]
            zz = _dot_nt(cc, hjb)
            dzb = _b(dyj * ea)
            dac = dac + jnp.sum(dyj * zz, axis=1, keepdims=True) * ea
            dc_acc = dc_acc + _dot(dzb, hjb)
            dh_in = _dot_tn(dzb, cc)
            dsb = _b(dhn)
            ww = _dot_nt(bb, dsb)
            dxd = dxd + ww * dsd
            dds = jnp.sum(ww * xd, axis=1, keepdims=True) * dsd
            db_acc = db_acc + _dot(_b(xd * dsd), dsb)
            dac = dac - dds
            dal = (jnp.sum(dds, axis=0, keepdims=True)
                   + jnp.sum(jnp.sum(dhn * hj, axis=1, keepdims=True), axis=0, keepdims=True) * cd)
            dh_ref[hg * g + j] = dh_in + dhn * cd
            dac = dac + jnp.where(rowi == ch - 1, dal, 0.0)
            dacs4 = jnp.where(lane == j, dac, dacs4)
            dx_ref[:, sl] = dxh + dxd * dtc
            ddt4 = jnp.where(lane == j, jnp.sum(dxd * xh, axis=1, keepdims=True), ddt4)
        dcbb = _b(dcb)
        dc_ref[...] = dc_acc + _dot(dcbb, bb)
        db_ref[...] = db_acc + _dot_tn(dcbb, cc)
        ii = lax.broadcasted_iota(jnp.int32, (ch, ch), 0)
        jj = lax.broadcasted_iota(jnp.int32, (ch, ch), 1)
        triu = jnp.where(ii <= jj, 1.0, 0.0)
        dla4 = _dot_hi(triu, dacs4)
        ddt4 = ddt4 + dla4 * a4
        da4 = jnp.sum(dla4 * dt4, axis=0, keepdims=True) * a4
        sel_t = sel.T
        ddt_raw = _dot_hi(ddt4, sel_t) * _sigmoid(xraw)
        ddt_acc[...] += ddt_raw
        st_ref[0:1, :] += _dot_hi(_row8(da4), sel_t)[0:1, :]
        st_ref[1:2, :] += _dot_hi(_row8(dd4), sel_t)[0:1, :]
        st_ref[2:3, :] += jnp.sum(ddt_raw, axis=0, keepdims=True)

        @pl.when(g == SSM_GROUPS - 1)
        def _():
            ddt_ref[...] = _b(ddt_acc[...])

    small = pl.BlockSpec((1, LANES), lambda s, g: (0, 0))
    rc = lambda s: nch - 1 - s
    return pl.pallas_call(
        body, name="ssd_bwd", grid=(nch, SSM_GROUPS),
        in_specs=[pl.BlockSpec((ch, gw), lambda s, g: (rc(s), g)),
                  pl.BlockSpec((ch, SSM_STATE), lambda s, g: (rc(s), b_off + g)),
                  pl.BlockSpec((ch, SSM_STATE), lambda s, g: (rc(s), c_off + g)),
                  pl.BlockSpec((ch, LANES), lambda s, g: (rc(s), 0)),
                  small, small, small,
                  pl.BlockSpec((1, hg, SSM_HEAD_DIM, SSM_STATE), lambda s, g: (rc(s), g, 0, 0)),
                  pl.BlockSpec((ch, gw), lambda s, g: (rc(s), g))],
        out_specs=[pl.BlockSpec((ch, gw), lambda s, g: (rc(s), g)),
                   pl.BlockSpec((ch, SSM_STATE), lambda s, g: (rc(s), g)),
                   pl.BlockSpec((ch, SSM_STATE), lambda s, g: (rc(s), g)),
                   pl.BlockSpec((ch, LANES), lambda s, g: (rc(s), 0)),
                   pl.BlockSpec((8, LANES), lambda s, g: (0, 0))],
        out_shape=[jax.ShapeDtypeStruct((S, SSM_INNER), F32),
                   jax.ShapeDtypeStruct((S, SSM_GROUPS * SSM_STATE), F32),
                   jax.ShapeDtypeStruct((S, SSM_GROUPS * SSM_STATE), F32),
                   jax.ShapeDtypeStruct((S, LANES), BF16),
                   jax.ShapeDtypeStruct((8, LANES), F32)],
        scratch_shapes=[pltpu.VMEM((SSM_HEADS, SSM_HEAD_DIM, SSM_STATE), F32),
                        pltpu.VMEM((ch, LANES), F32)],
        compiler_params=_params(("arbitrary", "arbitrary")),
    )(xact, xact, xact, dt_raw, dt_bias, a_neg, d_skip, hs, dy)


GROUP_W = HEADS_PER_GROUP * SSM_HEAD_DIM
B_COL0 = SSM_INNER
C_COL0 = SSM_INNER + SSM_GROUPS * SSM_STATE


def _ssd_prep(dt_raw, dt_bias, a_neg):
    S = dt_raw.shape[0]
    ch = SSM_CHUNK
    nch = S // ch

    def body(dtr_ref, bias_ref, a_ref, dt_ref, acs_ref, acst_ref, sig_ref):
        x = dtr_ref[...] + bias_ref[...]
        lane = lax.broadcasted_iota(jnp.int32, (ch, LANES), 1)
        dt = jnp.where(lane < SSM_HEADS, _softplus(x), 0.0)
        ii = lax.broadcasted_iota(jnp.int32, (ch, ch), 0)
        jj = lax.broadcasted_iota(jnp.int32, (ch, ch), 1)
        acs = _dot_hi(jnp.where(ii >= jj, 1.0, 0.0), dt * a_ref[...])
        dt_ref[...] = dt
        acs_ref[...] = acs
        acst_ref[0] = acs.T[0:SSM_HEADS, :]
        sig_ref[...] = _sigmoid(x)

    blk = pl.BlockSpec((ch, LANES), lambda c: (c, 0))
    small = pl.BlockSpec((1, LANES), lambda c: (0, 0))
    shp = jax.ShapeDtypeStruct((S, LANES), F32)
    return pl.pallas_call(
        body, name="ssd_prep", grid=(nch,),
        in_specs=[blk, small, small],
        out_specs=[blk, blk, pl.BlockSpec((1, SSM_HEADS, ch), lambda c: (c, 0, 0)), blk],
        out_shape=[shp, shp, jax.ShapeDtypeStruct((nch, SSM_HEADS, ch), F32), shp],
        compiler_params=_params(("parallel",)),
    )(dt_raw, dt_bias, a_neg)


def _expand_heads(arr, g, rows):
    lane = lax.broadcasted_iota(jnp.int32, (rows, GROUP_W), 1) // SSM_HEAD_DIM
    h0 = HEADS_PER_GROUP * g
    out = jnp.broadcast_to(arr[:, h0:h0 + 1], (rows, GROUP_W))
    for j in range(1, HEADS_PER_GROUP):
        out = jnp.where(lane == j, arr[:, h0 + j:h0 + j + 1], out)
    return out


def _seg_matrix(k, lanes_per_head, h0):
    r = lax.broadcasted_iota(jnp.int32, (k, LANES), 0)
    c = lax.broadcasted_iota(jnp.int32, (k, LANES), 1)
    return jnp.where(c == h0 + r // lanes_per_head, 1.0, 0.0).astype(BF16)


def _seg_dot(t, e):
    hi = _b(t)
    lo = _b(t - hi.astype(F32))
    return _dot(hi, e) + _dot(lo, e)


def _head_sums(t, e, rows):
    if rows >= 8:
        return _seg_dot(t, e)
    return _seg_dot(jnp.broadcast_to(t, (8, t.shape[1])), e)[0:rows]


def _pair_masks(x):
    lane = lax.broadcasted_iota(jnp.int32, x.shape, 1)
    zero = jnp.zeros_like(x)
    return jnp.where(lane < SSM_HEAD_DIM, x, zero), jnp.where(lane >= SSM_HEAD_DIM, x, zero)


def _ssd_fwd2(xact, dt, acs, acst, dsk_e):
    S = xact.shape[0]
    ch = SSM_CHUNK
    nch = S // ch

    def body(x_ref, dt_ref, acs_ref, acst_ref, dsk_ref, y_ref, hs_ref, h_ref):
        c = pl.program_id(0)

        @pl.when(c == 0)
        def _():
            h_ref[...] = jnp.zeros_like(h_ref)

        dt_all = dt_ref[...]
        acs_all = acs_ref[...]
        acst_all = acst_ref[0]
        alast = acs_all[ch - 1:ch, :]
        eacs = jnp.exp(acs_all)
        wd_all = dt_all * jnp.exp(alast - acs_all)
        dtt = dt_all.T
        cd_all = jnp.exp(alast)
        ii = lax.broadcasted_iota(jnp.int32, (ch, ch), 0)
        jj = lax.broadcasted_iota(jnp.int32, (ch, ch), 1)
        low = ii >= jj
        for g in range(SSM_GROUPS):
            xs = x_ref[:, g * GROUP_W:(g + 1) * GROUP_W]
            bb = _b(x_ref[:, B_COL0 + g * SSM_STATE:B_COL0 + (g + 1) * SSM_STATE])
            cc = _b(x_ref[:, C_COL0 + g * SSM_STATE:C_COL0 + (g + 1) * SSM_STATE])
            cb = _dot_nt(cc, bb)
            xsb = _b(xs)
            ht = h_ref[g]
            rest = (_dot(cc, _b(ht)) * _expand_heads(eacs, g, ch)
                    + dsk_ref[:, g * GROUP_W:(g + 1) * GROUP_W] * xs)
            for p in range(HEADS_PER_GROUP // 2):
                lms = []
                for h in (HEADS_PER_GROUP * g + 2 * p, HEADS_PER_GROUP * g + 2 * p + 1):
                    diff = acs_all[:, h:h + 1] - acst_all[h:h + 1, :]
                    lms.append(_b(cb * jnp.exp(jnp.where(low, diff, -jnp.inf)) * dtt[h:h + 1, :]))
                xa, xb = _pair_masks(xsb[:, p * LANES:(p + 1) * LANES])
                yp = _dot(jnp.concatenate(lms, axis=1), jnp.concatenate([xa, xb], axis=0))
                y_ref[:, g * GROUP_W + p * LANES:g * GROUP_W + (p + 1) * LANES] = (
                    yp + rest[:, p * LANES:(p + 1) * LANES])
            hs_ref[0, g] = ht
            st = _dot_tn(bb, _b(xs * _expand_heads(wd_all, g, ch)))
            h_ref[g] = ht * _expand_heads(cd_all, g, 1) + st

    blk = pl.BlockSpec((ch, LANES), lambda c: (c, 0))
    return pl.pallas_call(
        body, name="ssd_fwd", grid=(nch,),
        in_specs=[pl.BlockSpec((ch, CONV_DIM), lambda c: (c, 0)), blk, blk,
                  pl.BlockSpec((1, SSM_HEADS, ch), lambda c: (c, 0, 0)),
                  pl.BlockSpec((1, SSM_INNER), lambda c: (0, 0))],
        out_specs=[pl.BlockSpec((ch, SSM_INNER), lambda c: (c, 0)),
                   pl.BlockSpec((1, SSM_GROUPS, SSM_STATE, GROUP_W), lambda c: (c, 0, 0, 0))],
        out_shape=[jax.ShapeDtypeStruct((S, SSM_INNER), F32),
                   jax.ShapeDtypeStruct((nch, SSM_GROUPS, SSM_STATE, GROUP_W), F32)],
        scratch_shapes=[pltpu.VMEM((SSM_GROUPS, SSM_STATE, GROUP_W), F32)],
        compiler_params=_params(("arbitrary",)),
    )(xact, dt, acs, acst, dsk_e)


def _ssd_fwd3(xact, dt, acs, acst, dsk_e):
    S = xact.shape[0]
    ch = SSM_CHUNK
    nch = S // ch
    ng, hg = SSM_GROUPS, HEADS_PER_GROUP
    nbc = SSM_GROUPS * SSM_STATE

    def body(x_ref, dt_ref, acs_ref, acst_ref, dsk_ref, y_ref, hs_ref,
             h_ref, e_ea, xdb_s, xddb_s, bcb_s, cb_s, zz_s, st_s, lmb_s):
        c = pl.program_id(0)

        @pl.when(c == 0)
        def _():
            h_ref[...] = jnp.zeros_like(h_ref)

        dt_all = dt_ref[...]
        acs_all = acs_ref[...]
        alast = acs_all[ch - 1:ch, :]
        eacs = jnp.exp(acs_all)
        dsd_all = jnp.exp(alast - acs_all)
        cd_all = jnp.exp(alast)
        ii = lax.broadcasted_iota(jnp.int32, (ch, ch), 0)
        jj = lax.broadcasted_iota(jnp.int32, (ch, ch), 1)
        low = ii >= jj
        gsl = [slice(g * GROUP_W, (g + 1) * GROUP_W) for g in range(ng)]
        bcb_s[...] = _b(x_ref[:, B_COL0:])
        for g in range(ng):
            xd = x_ref[:, gsl[g]] * _expand_heads(dt_all, g, ch)
            xdb_s[:, gsl[g]] = _b(xd)
            xddb_s[:, gsl[g]] = _b(xd * _expand_heads(dsd_all, g, ch))
            e_ea[:, gsl[g]] = _expand_heads(eacs, g, ch)
        for g in range(ng):
            bb = bcb_s[:, g * SSM_STATE:(g + 1) * SSM_STATE]
            cc = bcb_s[:, nbc + g * SSM_STATE:nbc + (g + 1) * SSM_STATE]
            cb_s[g] = _dot_nt(cc, bb)
            zz_s[:, gsl[g]] = _dot(cc, _b(h_ref[g]))
            st_s[g] = _dot_tn(bb, xddb_s[:, gsl[g]])
        for g in range(ng):
            cb = cb_s[g]
            for j in range(hg):
                h = hg * g + j
                diff = acs_all[:, h:h + 1] - acst_ref[0, h:h + 1, :]
                lmb_s[:, h * ch:(h + 1) * ch] = _b(cb * jnp.exp(jnp.where(low, diff, -jnp.inf)))
            ht = h_ref[g]
            hs_ref[0, g] = ht
            h_ref[g] = ht * _expand_heads(cd_all, g, 1) + st_s[g]
        for g in range(ng):
            for p in range(hg // 2):
                h0 = hg * g + 2 * p
                sl = slice(g * GROUP_W + p * LANES, g * GROUP_W + (p + 1) * LANES)
                yp = _dot(lmb_s[:, h0 * ch:(h0 + 2) * ch], jnp.concatenate(_pair_masks(xdb_s[:, sl]), axis=0))
                y_ref[:, sl] = yp + zz_s[:, sl] * e_ea[:, sl] + dsk_ref[:, sl] * x_ref[:, sl]

    blk = pl.BlockSpec((ch, LANES), lambda c: (c, 0))
    wide = lambda dt_: pltpu.VMEM((ch, SSM_INNER), dt_)
    return pl.pallas_call(
        body, name="ssd_fwd", grid=(nch,),
        in_specs=[pl.BlockSpec((ch, CONV_DIM), lambda c: (c, 0)), blk, blk,
                  pl.BlockSpec((1, SSM_HEADS, ch), lambda c: (c, 0, 0)),
                  pl.BlockSpec((1, SSM_INNER), lambda c: (0, 0))],
        out_specs=[pl.BlockSpec((ch, SSM_INNER), lambda c: (c, 0)),
                   pl.BlockSpec((1, SSM_GROUPS, SSM_STATE, GROUP_W), lambda c: (c, 0, 0, 0))],
        out_shape=[jax.ShapeDtypeStruct((S, SSM_INNER), F32),
                   jax.ShapeDtypeStruct((nch, SSM_GROUPS, SSM_STATE, GROUP_W), F32)],
        scratch_shapes=[pltpu.VMEM((ng, SSM_STATE, GROUP_W), F32), wide(F32), wide(BF16), wide(BF16), wide(BF16),
                        pltpu.VMEM((ng, ch, ch), F32), wide(F32), pltpu.VMEM((ng, SSM_STATE, GROUP_W), F32),
                        pltpu.VMEM((ch, SSM_HEADS * ch), BF16)],
        compiler_params=_params(("arbitrary",)),
    )(xact, dt, acs, acst, dsk_e)


def _ssd_bwd2(xact, dt, acs, acst, sig, a_neg, dsk_e, hs, dy):
    S = xact.shape[0]
    ch = SSM_CHUNK
    nch = S // ch

    def body(x_ref, dt_ref, acs_ref, acst_ref, sig_ref, a_ref, dsk_ref, hs_ref, dy_ref,
             dx_ref, ddt_ref, st_ref, dh_ref, rows_ref):
        step = pl.program_id(0)

        @pl.when(step == 0)
        def _():
            dh_ref[...] = jnp.zeros_like(dh_ref)
            st_ref[...] = jnp.zeros_like(st_ref)
            rows_ref[...] = jnp.zeros_like(rows_ref)

        dt_all = dt_ref[...]
        acs_all = acs_ref[...]
        acst_all = acst_ref[0]
        alast = acs_all[ch - 1:ch, :]
        eacs = jnp.exp(acs_all)
        dsd_all = jnp.exp(alast - acs_all)
        cd_all = jnp.exp(alast)
        ii = lax.broadcasted_iota(jnp.int32, (ch, ch), 0)
        jj = lax.broadcasted_iota(jnp.int32, (ch, ch), 1)
        low = ii >= jj
        lane = lax.broadcasted_iota(jnp.int32, (ch, LANES), 1)
        cols = jnp.zeros((ch, LANES), F32)
        ddt = jnp.zeros((ch, LANES), F32)
        dal = jnp.zeros((1, LANES), F32)
        ddsk = jnp.zeros((1, LANES), F32)
        for g in range(SSM_GROUPS):
            xs = x_ref[:, g * GROUP_W:(g + 1) * GROUP_W]
            bb = _b(x_ref[:, B_COL0 + g * SSM_STATE:B_COL0 + (g + 1) * SSM_STATE])
            cc = _b(x_ref[:, C_COL0 + g * SSM_STATE:C_COL0 + (g + 1) * SSM_STATE])
            cb = _dot_nt(cc, bb)
            dt_e = _expand_heads(dt_all, g, ch)
            ea_e = _expand_heads(eacs, g, ch)
            dsd_e = _expand_heads(dsd_all, g, ch)
            cd_e = _expand_heads(cd_all, g, 1)
            xd = xs * dt_e
            xdb = _b(xd)
            dyg = dy_ref[:, g * GROUP_W:(g + 1) * GROUP_W]
            dyb = _b(dyg)
            ht = hs_ref[0, g]
            htb = _b(ht)
            dhn = dh_ref[g]
            dhnb = _b(dhn)
            zz = _dot(cc, htb)
            dzb = _b(dyg * ea_e)
            d_c = _dot_nt(dzb, htb)
            dh_in = _dot_tn(cc, dzb)
            ww = _dot(bb, dhnb)
            xdd = xd * dsd_e
            d_b = _dot_nt(_b(xdd), dhnb)
            t2 = ww * xdd
            e_g = _seg_matrix(GROUP_W, SSM_HEAD_DIM, HEADS_PER_GROUP * g)
            cols = cols + _head_sums(dyg * zz * ea_e - t2, e_g, ch)
            dal = dal + _head_sums(jnp.sum(t2, axis=0, keepdims=True), e_g, 1) + cd_all * _head_sums(
                jnp.sum(dhn * ht, axis=0, keepdims=True), e_g, 1)
            dh_ref[g] = dh_in + dhn * cd_e
            ddsk = ddsk + _head_sums(jnp.sum(dyg * xs, axis=0, keepdims=True), e_g, 1)
            dxd_rest = ww * dsd_e
            dcb = jnp.zeros((ch, ch), F32)
            for p in range(HEADS_PER_GROUP // 2):
                dya, dyb2 = _pair_masks(dyb[:, p * LANES:(p + 1) * LANES])
                xp = xdb[:, p * LANES:(p + 1) * LANES]
                lms, gms = [], []
                for h, dyh in ((HEADS_PER_GROUP * g + 2 * p, dya), (HEADS_PER_GROUP * g + 2 * p + 1, dyb2)):
                    diff = acs_all[:, h:h + 1] - acst_all[h:h + 1, :]
                    decay = jnp.exp(jnp.where(low, diff, -jnp.inf))
                    lm = cb * decay
                    dlm = _dot_nt(dyh, xp)
                    gm = dlm * lm
                    dcb = dcb + dlm * decay
                    rows_ref[h:h + 1, :] = jnp.sum(gm, axis=0, keepdims=True)
                    lms.append(_b(lm))
                    gms.append(gm)
                h0 = HEADS_PER_GROUP * g + 2 * p
                cols = cols + _head_sums(jnp.concatenate(gms, axis=1), _seg_matrix(2 * ch, ch, h0), ch)
                dxd = _dot_tn(jnp.concatenate(lms, axis=0), jnp.concatenate([dya, dyb2], axis=0))
                dxd = dxd + dxd_rest[:, p * LANES:(p + 1) * LANES]
                sl = slice(g * GROUP_W + p * LANES, g * GROUP_W + (p + 1) * LANES)
                dx_ref[:, sl] = (dsk_ref[:, sl] * dyg[:, p * LANES:(p + 1) * LANES]
                                 + dxd * dt_e[:, p * LANES:(p + 1) * LANES])
                ddt = ddt + _head_sums(dxd * xs[:, p * LANES:(p + 1) * LANES],
                                       _seg_matrix(LANES, SSM_HEAD_DIM, h0), ch)
            dcbb = _b(dcb)
            dx_ref[:, C_COL0 + g * SSM_STATE:C_COL0 + (g + 1) * SSM_STATE] = d_c + _dot(dcbb, bb)
            dx_ref[:, B_COL0 + g * SSM_STATE:B_COL0 + (g + 1) * SSM_STATE] = d_b + _dot_tn(dcbb, cc)
        rowi = lax.broadcasted_iota(jnp.int32, (ch, 1), 0)
        dacs = cols - rows_ref[...].T + jnp.where(rowi == ch - 1, dal, 0.0)
        dla = _dot_hi(jnp.where(ii <= jj, 1.0, 0.0), dacs)
        a_row = a_ref[...]
        ddt_raw = (ddt + dla * a_row) * sig_ref[...]
        ddt_ref[...] = _b(ddt_raw)
        st_ref[0:1, :] += jnp.sum(dla * dt_all, axis=0, keepdims=True) * a_row
        st_ref[1:2, :] += ddsk
        st_ref[2:3, :] += jnp.sum(ddt_raw, axis=0, keepdims=True)

    rc = lambda s: nch - 1 - s
    blk = pl.BlockSpec((ch, LANES), lambda s: (rc(s), 0))
    return pl.pallas_call(
        body, name="ssd_bwd", grid=(nch,),
        in_specs=[pl.BlockSpec((ch, CONV_DIM), lambda s: (rc(s), 0)), blk, blk,
                  pl.BlockSpec((1, SSM_HEADS, ch), lambda s: (rc(s), 0, 0)), blk,
                  pl.BlockSpec((1, LANES), lambda s: (0, 0)),
                  pl.BlockSpec((1, SSM_INNER), lambda s: (0, 0)),
                  pl.BlockSpec((1, SSM_GROUPS, SSM_STATE, GROUP_W), lambda s: (rc(s), 0, 0, 0)),
                  pl.BlockSpec((ch, SSM_INNER), lambda s: (rc(s), 0))],
        out_specs=[pl.BlockSpec((ch, CONV_DIM), lambda s: (rc(s), 0)), blk,
                   pl.BlockSpec((8, LANES), lambda s: (0, 0))],
        out_shape=[jax.ShapeDtypeStruct((S, CONV_DIM), F32), jax.ShapeDtypeStruct((S, LANES), BF16),
                   jax.ShapeDtypeStruct((8, LANES), F32)],
        scratch_shapes=[pltpu.VMEM((SSM_GROUPS, SSM_STATE, GROUP_W), F32), pltpu.VMEM((LANES, ch), F32)],
        compiler_params=_params(("arbitrary",)),
    )(xact, dt, acs, acst, sig, a_neg, dsk_e, hs, dy)


def _ssd_bwd3(xact, dt, acs, acst, sig, a_neg, dsk_e, hs, dy):
    S = xact.shape[0]
    ch = SSM_CHUNK
    nch = S // ch
    ng, hg = SSM_GROUPS, HEADS_PER_GROUP
    nbc = SSM_GROUPS * SSM_STATE

    def body(x_ref, dt_ref, acs_ref, acst_ref, sig_ref, a_ref, dsk_ref, hs_ref, dy_ref,
             dx_ref, ddt_ref, st_ref,
             dh_ref, rows_ref, e_dt, e_ea, e_dsd, xdb_s, xddb_s, dzb_s, bcb_s, cb_s, zz_s, ww_s, dc1_s, db1_s,
             dhin_s, dlm_s, lmb_s, gm_s, dcbb_s, t_s, dxd_s, prod_s, csum_s):
        step = pl.program_id(0)

        @pl.when(step == 0)
        def _():
            dh_ref[...] = jnp.zeros_like(dh_ref)
            st_ref[...] = jnp.zeros_like(st_ref)
            rows_ref[...] = jnp.zeros_like(rows_ref)

        dt_all = dt_ref[...]
        acs_all = acs_ref[...]
        alast = acs_all[ch - 1:ch, :]
        eacs = jnp.exp(acs_all)
        dsd_all = jnp.exp(alast - acs_all)
        cd_all = jnp.exp(alast)
        ii = lax.broadcasted_iota(jnp.int32, (ch, ch), 0)
        jj = lax.broadcasted_iota(jnp.int32, (ch, ch), 1)
        low = ii >= jj
        gsl = [slice(g * GROUP_W, (g + 1) * GROUP_W) for g in range(ng)]
        psl = [[slice(g * GROUP_W + p * LANES, g * GROUP_W + (p + 1) * LANES) for p in range(hg // 2)]
               for g in range(ng)]
        seg = [_seg_matrix(GROUP_W, SSM_HEAD_DIM, hg * g) for g in range(ng)]

        def bc(g):
            return (bcb_s[:, g * SSM_STATE:(g + 1) * SSM_STATE],
                    bcb_s[:, nbc + g * SSM_STATE:nbc + (g + 1) * SSM_STATE])

        def dy_pair(g, p):
            return _pair_masks(_b(dy_ref[:, psl[g][p]]))

        bcb_s[...] = _b(x_ref[:, B_COL0:])
        for g in range(ng):
            dt_e = _expand_heads(dt_all, g, ch)
            ea_e = _expand_heads(eacs, g, ch)
            dsd_e = _expand_heads(dsd_all, g, ch)
            e_dt[:, gsl[g]] = dt_e
            e_ea[:, gsl[g]] = ea_e
            e_dsd[:, gsl[g]] = dsd_e
            xd = x_ref[:, gsl[g]] * dt_e
            xdb_s[:, gsl[g]] = _b(xd)
            xddb_s[:, gsl[g]] = _b(xd * dsd_e)
            dzb_s[:, gsl[g]] = _b(dy_ref[:, gsl[g]] * ea_e)
        for g in range(ng):
            bb, cc = bc(g)
            htb = _b(hs_ref[0, g])
            dhnb = _b(dh_ref[g])
            cb_s[g] = _dot_nt(cc, bb)
            zz_s[:, gsl[g]] = _dot(cc, htb)
            ww_s[:, gsl[g]] = _dot(bb, dhnb)
            dc1_s[g] = _dot_nt(dzb_s[:, gsl[g]], htb)
            db1_s[g] = _dot_nt(xddb_s[:, gsl[g]], dhnb)
            dhin_s[g] = _dot_tn(cc, dzb_s[:, gsl[g]])
            for p in range(hg // 2):
                xp = xdb_s[:, psl[g][p]]
                for q, dyh in enumerate(dy_pair(g, p)):
                    dlm_s[hg * g + 2 * p + q] = _dot_nt(dyh, xp)
        for g in range(ng):
            cb = cb_s[g]
            dcb = jnp.zeros((ch, ch), F32)
            for j in range(hg):
                h = hg * g + j
                diff = acs_all[:, h:h + 1] - acst_ref[0, h:h + 1, :]
                decay = jnp.exp(jnp.where(low, diff, -jnp.inf))
                lm = cb * decay
                dlm = dlm_s[h]
                gm = dlm * lm
                dcb = dcb + dlm * decay
                rows_ref[h:h + 1, :] = jnp.sum(gm, axis=0, keepdims=True)
                lmb_s[h * ch:(h + 1) * ch, :] = _b(lm)
                gm_s[:, h * ch:(h + 1) * ch] = gm
            dcbb_s[g] = _b(dcb)
            xs = x_ref[:, gsl[g]]
            dyg = dy_ref[:, gsl[g]]
            ww = ww_s[:, gsl[g]]
            dsd_e = e_dsd[:, gsl[g]]
            t2 = ww * (xs * e_dt[:, gsl[g]] * dsd_e)
            t_s[:, gsl[g]] = dyg * zz_s[:, gsl[g]] * e_ea[:, gsl[g]] - t2
            dhn = dh_ref[g]
            csum_s[0:1, gsl[g]] = jnp.sum(t2, axis=0, keepdims=True)
            csum_s[1:2, gsl[g]] = jnp.sum(dhn * hs_ref[0, g], axis=0, keepdims=True)
            csum_s[2:3, gsl[g]] = jnp.sum(dyg * xs, axis=0, keepdims=True)
            dh_ref[g] = dhin_s[g] + dhn * _expand_heads(cd_all, g, 1)
            dxd_s[:, gsl[g]] = ww * dsd_e
        cols = jnp.zeros((ch, LANES), F32)
        for g in range(ng):
            bb, cc = bc(g)
            dcbb = dcbb_s[g]
            dx_ref[:, C_COL0 + g * SSM_STATE:C_COL0 + (g + 1) * SSM_STATE] = dc1_s[g] + _dot(dcbb, bb)
            dx_ref[:, B_COL0 + g * SSM_STATE:B_COL0 + (g + 1) * SSM_STATE] = db1_s[g] + _dot_tn(dcbb, cc)
            cols = cols + _head_sums(t_s[:, gsl[g]], seg[g], ch)
            for p in range(hg // 2):
                h0 = hg * g + 2 * p
                dxd_s[:, psl[g][p]] += _dot_tn(lmb_s[h0 * ch:(h0 + 2) * ch, :],
                                               jnp.concatenate(dy_pair(g, p), axis=0))
                cols = cols + _head_sums(gm_s[:, h0 * ch:(h0 + 2) * ch], _seg_matrix(2 * ch, ch, h0), ch)
        for g in range(ng):
            dxd = dxd_s[:, gsl[g]]
            xs = x_ref[:, gsl[g]]
            dx_ref[:, gsl[g]] = dsk_ref[:, gsl[g]] * dy_ref[:, gsl[g]] + dxd * e_dt[:, gsl[g]]
            prod_s[:, gsl[g]] = dxd * xs
        ddt = jnp.zeros((ch, LANES), F32)
        dal = jnp.zeros((1, LANES), F32)
        ddsk = jnp.zeros((1, LANES), F32)
        for g in range(ng):
            ddt = ddt + _head_sums(prod_s[:, gsl[g]], seg[g], ch)
            dal = (dal + _head_sums(csum_s[0:1, gsl[g]], seg[g], 1)
                   + cd_all * _head_sums(csum_s[1:2, gsl[g]], seg[g], 1))
            ddsk = ddsk + _head_sums(csum_s[2:3, gsl[g]], seg[g], 1)
        rowi = lax.broadcasted_iota(jnp.int32, (ch, 1), 0)
        dacs = cols - rows_ref[...].T + jnp.where(rowi == ch - 1, dal, 0.0)
        dla = _dot_hi(jnp.where(ii <= jj, 1.0, 0.0), dacs)
        a_row = a_ref[...]
        ddt_raw = (ddt + dla * a_row) * sig_ref[...]
        ddt_ref[...] = _b(ddt_raw)
        st_ref[0:1, :] += jnp.sum(dla * dt_all, axis=0, keepdims=True) * a_row
        st_ref[1:2, :] += ddsk
        st_ref[2:3, :] += jnp.sum(ddt_raw, axis=0, keepdims=True)

    rc = lambda s: nch - 1 - s
    blk = pl.BlockSpec((ch, LANES), lambda s: (rc(s), 0))
    wide = lambda dt_: pltpu.VMEM((ch, SSM_INNER), dt_)
    sq = lambda n, dt_: pltpu.VMEM((n, ch, ch), dt_)
    scratch = [pltpu.VMEM((ng, SSM_STATE, GROUP_W), F32), pltpu.VMEM((LANES, ch), F32),
               wide(F32), wide(F32), wide(F32),
               wide(BF16), wide(BF16), wide(BF16), wide(BF16),
               sq(ng, F32), wide(F32), wide(F32), sq(ng, F32), sq(ng, F32),
               pltpu.VMEM((ng, SSM_STATE, GROUP_W), F32),
               sq(SSM_HEADS, F32),
               pltpu.VMEM((SSM_HEADS * ch, ch), BF16),
               pltpu.VMEM((ch, SSM_HEADS * ch), F32),
               sq(ng, BF16), wide(F32), wide(F32), wide(F32),
               pltpu.VMEM((8, SSM_INNER), F32)]
    return pl.pallas_call(
        body, name="ssd_bwd", grid=(nch,),
        in_specs=[pl.BlockSpec((ch, CONV_DIM), lambda s: (rc(s), 0)), blk, blk,
                  pl.BlockSpec((1, SSM_HEADS, ch), lambda s: (rc(s), 0, 0)), blk,
                  pl.BlockSpec((1, LANES), lambda s: (0, 0)),
                  pl.BlockSpec((1, SSM_INNER), lambda s: (0, 0)),
                  pl.BlockSpec((1, SSM_GROUPS, SSM_STATE, GROUP_W), lambda s: (rc(s), 0, 0, 0)),
                  pl.BlockSpec((ch, SSM_INNER), lambda s: (rc(s), 0))],
        out_specs=[pl.BlockSpec((ch, CONV_DIM), lambda s: (rc(s), 0)), blk,
                   pl.BlockSpec((8, LANES), lambda s: (0, 0))],
        out_shape=[jax.ShapeDtypeStruct((S, CONV_DIM), F32), jax.ShapeDtypeStruct((S, LANES), BF16),
                   jax.ShapeDtypeStruct((8, LANES), F32)],
        scratch_shapes=scratch,
        compiler_params=_params(("arbitrary",)),
    )(xact, dt, acs, acst, sig, a_neg, dsk_e, hs, dy)


def _pad_lanes(v, n=LANES):
    return jnp.pad(v, ((0, 0), (0, n - v.shape[1])))


def _local_step(x, target, w, ex=None):
    offs = np.cumsum((0,) + IN_SPLITS)
    wt_in = w["w_in_t"]
    w_qkv = wt_in[offs[0]:offs[3]]
    w_z = wt_in[offs[3]:offs[4]]
    w_xbc = wt_in[offs[4]:offs[5]]
    w_dt = jnp.pad(wt_in[offs[5]:offs[6]], ((0, LANES - SSM_HEADS), (0, 0)))
    w_g = wt_in[offs[6]:offs[7]]
    dt_bias = _pad_lanes(w["dt_bias"])
    a_neg = _pad_lanes(-jnp.exp(w["a_log"]))
    d_skip = _pad_lanes(w["d_skip"])

    u = _rms_fwd(x, w["norm_mix_pre_w"])
    if ex is None:
        xbc = _mm_nn(u, w_xbc, F32, "proj_xbc", tb=True)
    else:
        xbc, got = _mm_nn(u, w_xbc, F32, "proj_xbc", comm=_gather_comm([ex.mine[REST_EARLY]]), tb=True)
        w = {**w, **ex.rest_weights(got[0], REST_EARLY)}
    qkv = _mm_nn(u, w_qkv, F32, "proj_qkv", tb=True)
    z = _mm_nn(u, w_z, F32, "proj_z", tb=True)
    dt_raw = _mm_nn(u, w_dt, F32, "proj_dt", tb=True)
    gl = _mm_nn(u, w_g, F32, "proj_gate", tb=True)

    pats = _qkv_layouts(qkv)
    os_, ms_, ls_ = [], [], []
    for d, qkv_p in zip(DILATIONS, pats):
        if ex is not None and d == DILATIONS[0]:
            o, m, l, got = _attn_fwd2(qkv_p, d, comm=_gather_comm([ex.mine[REST_LATE]]))
            w = {**w, **ex.rest_weights(got[0], REST_LATE)}
        else:
            o, m, l = _attn_fwd2(qkv_p, d)
        os_.append(o)
        ms_.append(m)
        ls_.append(l)
    att, lse = _attn_combine2(os_, ms_, ls_)
    att_o = _mm_nn(att, w["w_att_proj"], F32, "att_proj")

    xact, conv_pre = _conv_fwd2(xbc, w["conv_w"], w["conv_b"])
    dsk_e = jnp.repeat(w["d_skip"], SSM_HEAD_DIM, axis=1)
    dt, acs, acst, sig = _ssd_prep(dt_raw, dt_bias, a_neg)
    y_ssd, hs = _ssd_fwd2(xact, dt, acs, acst, dsk_e)
    (ssm_y, ssm_o), _ = _mm_epi(None, w["w_ssm_proj"], _ssm_out_epi(y_ssd, z, w["ssm_norm_w"]), "ssm_proj")

    (mi, mixed, h1, f), _ = _mm_epi(None, w["w_out"], _mix_out_epi(
        att_o, ssm_o, gl, x, w["b_gate"], w["norm_mix_post_w"], w["norm_ffn_pre_w"]), "out_proj")
    r_up, act = _mm_nn(f, w["w_up"], BF16, "ffn_up", mode="relu2")
    (dh2, d_down, loss, g_ffn_post), _ = _mm_epi(
        act, w["w_down"], _final_epi(h1, target, w["norm_ffn_post_w"]), "ffn_down")

    g = {"norm_ffn_post_w": g_ffn_post}
    g["w_down"] = _mm_tn(act, d_down, "dw_down")
    dup = _mm_nn(d_down, w["w_down"], BF16, "d_act", mode="mul2", extra=r_up, tb=True)
    g["w_up"] = _mm_tn(f, dup, "dw_up")
    (dh1, d_mixed, g["norm_ffn_pre_w"], g["norm_mix_post_w"]), _ = _mm_epi(
        dup, w["w_up"], _mid_epi(dh2, h1, mixed, w["norm_ffn_pre_w"], w["norm_mix_post_w"]), "d_f", tb=True)
    g["w_out"] = _mm_tn(mi, d_mixed, "dw_out")
    (d_att_o, d_ssm_o, dgl, g["b_gate"]), _ = _mm_epi(
        d_mixed, w["w_out"], _gate_epi(att_o, ssm_o, gl, w["b_gate"]), "d_mi", tb=True)

    g["w_att_proj"] = _mm_tn(att, d_att_o, "dw_att_proj")
    g["w_ssm_proj"] = _mm_tn(ssm_y, d_ssm_o, "dw_ssm_proj")
    gn_epi = _gnorm_epi(y_ssd, z, w["ssm_norm_w"])
    if ex is None:
        (dy_ssd, dz, g["ssm_norm_w"]), _ = _mm_epi(d_ssm_o, w["w_ssm_proj"], gn_epi, "d_ssm_y", tb=True,
                                                    tn=PACK_COLS)
    else:
        gs_rest = jnp.concatenate(
            [_shards_from_full(n, g[n]).reshape(N_CHIPS, -1, PACK_COLS) for n in REST], axis=1)
        (dy_ssd, dz, g["ssm_norm_w"]), recv = _mm_epi(d_ssm_o, w["w_ssm_proj"], gn_epi, "d_ssm_y", tb=True,
                                                       tn=PACK_COLS, comm=_pair_comm([gs_rest]))
        p_rest = _pair_add2(gs_rest, recv[0], ex.c_arr, "rs_pair_add_rest")

    d_att = _mm_nn(d_att_o, w["w_att_proj"], F32, "d_att", tb=True)
    dqs, dks, dvs = [], [], []
    for d, qkv_p, (do_p, lse_p, delta_p) in zip(DILATIONS, pats, _attn_delta2(d_att, att, lse)):
        if ex is not None and d == DILATIONS[0]:
            dq, dk, dv, recv3 = _attn_bwd2(qkv_p, do_p, lse_p, delta_p, d, comm=_chip_comm([p_rest]))
            q_rest = _chip_add2(p_rest, recv3[0], ex.chip_arr, "rs_chip_add_rest")
            ex.finish_reduce("rest", q_rest, _comm_call("rs_share_rest", _share_comm([q_rest]))[0])
        else:
            dq, dk, dv = _attn_bwd2(qkv_p, do_p, lse_p, delta_p, d)
        dqs.append(dq)
        dks.append(dk)
        dvs.append(dv)
    dqkv = _sum_qkv2(dqs, dks, dvs)

    dxact, ddt_raw, stats = _ssd_bwd3(xact, dt, acs, acst, sig, a_neg, dsk_e, hs, dy_ssd)
    g["a_log"] = stats[0:1, :SSM_HEADS]
    g["d_skip"] = stats[1:2, :SSM_HEADS]
    g["dt_bias"] = stats[2:3, :SSM_HEADS]
    dxbc, g["conv_w"], g["conv_b"] = _conv_bwd2(xbc, conv_pre, dxact, w["conv_w"])

    pieces = [(dqkv, w_qkv), (dz, w_z), (dxbc, w_xbc), (ddt_raw, w_dt), (dgl, w_g)]
    gw = [_mm_tn(dp, u, f"dw_in_{i}") for i, (dp, _) in enumerate(pieces)]
    gw[3] = gw[3][:SSM_HEADS]
    if ex is None:
        g["w_in_t"] = jnp.concatenate(gw, axis=0)
    du = None
    for i, (dp, wp) in enumerate(pieces):
        if ex is not None and i == 0:
            gs_in = _rows_to_shards(gw, IN_SHARD_ROWS, IN_SHARD_PAD)
            du, recv = _mm_nn(dp, wp, F32, f"d_u_{i}", acc=du, comm=_pair_comm([gs_in]))
            p_in = _pair_add2(gs_in, recv[0], ex.c_arr, "rs_pair_add_in")
            rows = p_in.shape[1] // 2
            p_parts = [p_in[:, :rows], p_in[:, rows:]]
            q_parts = []
        elif ex is not None and i in (1, 2):
            p_part = p_parts[i - 1]
            du, recv3 = _mm_nn(dp, wp, F32, f"d_u_{i}", acc=du, comm=_chip_comm([p_part]))
            q_parts.append(_chip_add2(p_part, recv3[0], ex.chip_arr, f"rs_chip_add_in_{i}"))
            if i == 2:
                others = _comm_call("rs_share_in", _share_comm(q_parts))
                ex.finish_reduce("w_in", jnp.concatenate(q_parts, axis=0), jnp.concatenate(others, axis=0))
        elif i == len(pieces) - 1:
            (grad_x, g["norm_mix_pre_w"]), _ = _mm_epi(
                dp, wp, _first_epi(du, dh1, x, w["norm_mix_pre_w"]), f"d_u_{i}")
        else:
            du = _mm_nn(dp, wp, F32, f"d_u_{i}", acc=du)
    return loss, grad_x, g


def _rows_to_shards(pieces, shard_rows, pad_rows):
    cols = pieces[0].shape[1]
    shards = []
    for s in range(N_CHIPS):
        lo, hi = s * shard_rows, (s + 1) * shard_rows
        parts, r0 = [], 0
        for p in pieces:
            a, b = max(lo, r0), min(hi, r0 + p.shape[0])
            if a < b:
                parts.append(p[a - r0:b - r0])
            r0 += p.shape[0]
        parts.append(jnp.zeros((pad_rows - shard_rows, cols), pieces[0].dtype))
        shards.append(jnp.concatenate(parts, axis=0))
    return jnp.stack(shards)


BIG = ("w_in", "w_att_proj", "w_ssm_proj", "w_out", "w_up", "w_down")
BIG_FULL_SHAPES = {"w_in": (D_MODEL, IN_PROJ_WIDTH), "w_att_proj": (ATT_WIDTH, D_MODEL),
                   "w_ssm_proj": (SSM_INNER, D_MODEL), "w_out": (D_MODEL, D_MODEL),
                   "w_up": (D_MODEL, FFN_HIDDEN), "w_down": (FFN_HIDDEN, D_MODEL)}
BIG_COL_SHARDED = {"w_in": True, "w_att_proj": True, "w_ssm_proj": False, "w_out": False, "w_up": True,
                   "w_down": False}
PACK_COLS = 1024
PACK_ROWS = 5760
PACK_HALF = PACK_ROWS // 2
PACK_BLOCK = 576
SMALL = ("norm_mix_pre_w", "b_gate", "conv_b", "dt_bias", "a_log", "d_skip", "ssm_norm_w",
         "norm_mix_post_w", "norm_ffn_pre_w", "norm_ffn_post_w")
SMALL_ROWS = 232


def _shard_shape(name):
    r, c = BIG_FULL_SHAPES[name]
    return (r, c // N_CHIPS) if BIG_COL_SHARDED[name] else (r // N_CHIPS, c)


def _pack(shards, dtype):
    flat = [shards[n].astype(dtype).reshape(-1, PACK_COLS) for n in BIG]
    rows = sum(f.shape[0] for f in flat)
    flat.append(jnp.zeros((PACK_ROWS - rows, PACK_COLS), dtype))
    return jnp.concatenate(flat, axis=0)


def _unpack(packed):
    out, r0 = {}, 0
    for n in BIG:
        shp = _shard_shape(n)
        rows = shp[0] * shp[1] // PACK_COLS
        out[n] = packed[r0:r0 + rows].reshape(shp)
        r0 += rows
    return out


def _unpack_full(gathered):
    out, r0 = {}, 0
    for n in BIG:
        shp = _shard_shape(n)
        rows = shp[0] * shp[1] // PACK_COLS
        sh = gathered[:, r0:r0 + rows].reshape((N_CHIPS,) + shp)
        if BIG_COL_SHARDED[n]:
            out[n] = sh.transpose(1, 0, 2).reshape(BIG_FULL_SHAPES[n])
        else:
            out[n] = sh.reshape(BIG_FULL_SHAPES[n])
        r0 += rows
    return out


def _pack_full(grads):
    parts = []
    rows_total = 0
    for n in BIG:
        shp = _shard_shape(n)
        gfull = grads[n]
        if BIG_COL_SHARDED[n]:
            sh = gfull.reshape(shp[0], N_CHIPS, shp[1]).transpose(1, 0, 2)
        else:
            sh = gfull.reshape((N_CHIPS,) + shp)
        parts.append(sh.reshape(N_CHIPS, -1, PACK_COLS))
        rows_total += parts[-1].shape[1]
    parts.append(jnp.zeros((N_CHIPS, PACK_ROWS - rows_total, PACK_COLS), F32))
    return jnp.concatenate(parts, axis=1)


def _mesh_pos():
    return lax.axis_index("x"), lax.axis_index("y"), lax.axis_index("c")


def _other_chips(x, y):
    return [(1 - x, y), (x, 1 - y), (1 - x, 1 - y)]


ANY = pl.BlockSpec(memory_space=pl.ANY)


def _allgather_packed(wpack):
    half = PACK_HALF

    def body(w_ref, out_ref, send_sems, recv_sems):
        x, y, c = _mesh_pos()
        me = 2 * x + y
        sibling = (x, y, 1 - c)
        chips = _other_chips(x, y)

        def rows(chip, h):
            return out_ref.at[chip, pl.ds(h * half, half), :]

        def copy(k, chip, h, to, src=None):
            return pltpu.make_async_remote_copy(
                src_ref=rows(chip, h) if src is None else src, dst_ref=rows(chip, h),
                send_sem=send_sems.at[k], recv_sem=recv_sems.at[k], device_id=to, device_id_type=MESH)

        mine_half = w_ref.at[pl.ds(c * half, half), :]
        first = [copy(j, me, c, (*chip, c), src=mine_half) for j, chip in enumerate(chips)]
        for cp in first:
            cp.start()
        passed = [copy(3 + j, 2 * chip[0] + chip[1], c, sibling) for j, chip in enumerate(chips)]
        for j, chip in enumerate(chips):
            copy(j, 2 * chip[0] + chip[1], c, (x, y, c)).wait_recv()
            passed[j].start()
        for j, chip in enumerate(chips):
            copy(3 + j, 2 * chip[0] + chip[1], 1 - c, (x, y, c)).wait_recv()
        for cp in first + passed:
            cp.wait_send()

    return pl.pallas_call(
        body, name="allgather_weights",
        out_shape=jax.ShapeDtypeStruct((N_CHIPS,) + wpack.shape, wpack.dtype),
        in_specs=[ANY], out_specs=ANY,
        scratch_shapes=[pltpu.SemaphoreType.DMA((6,)), pltpu.SemaphoreType.DMA((6,))],
        compiler_params=pltpu.CompilerParams(has_side_effects=True),
    )(wpack)


def _exchange_halves(gpack):
    half = PACK_HALF

    def body(g_ref, out_ref, send_sem, recv_sem):
        x, y, c = _mesh_pos()
        cp = pltpu.make_async_remote_copy(
            src_ref=g_ref.at[:, pl.ds((1 - c) * half, half), :], dst_ref=out_ref,
            send_sem=send_sem, recv_sem=recv_sem, device_id=(x, y, 1 - c), device_id_type=MESH)
        cp.start()
        cp.wait()

    return pl.pallas_call(
        body, name="rs_pair_exchange",
        out_shape=jax.ShapeDtypeStruct((N_CHIPS, half, PACK_COLS), F32),
        in_specs=[ANY], out_specs=ANY,
        scratch_shapes=[pltpu.SemaphoreType.DMA, pltpu.SemaphoreType.DMA],
        compiler_params=pltpu.CompilerParams(has_side_effects=True),
    )(gpack)


def _pair_add(gpack, recv, c_idx):
    nb = PACK_HALF // PACK_BLOCK

    def body(c_ref, g_ref, r_ref, o_ref):
        o_ref[...] = _b(g_ref[...] + r_ref[...])

    blk = (1, PACK_BLOCK, PACK_COLS)
    return pl.pallas_call(
        body, name="rs_pair_add",
        grid_spec=pltpu.PrefetchScalarGridSpec(
            num_scalar_prefetch=1, grid=(N_CHIPS, nb),
            in_specs=[pl.BlockSpec(blk, lambda s, i, c: (s, c[0] * nb + i, 0)),
                      pl.BlockSpec(blk, lambda s, i, c: (s, i, 0))],
            out_specs=pl.BlockSpec(blk, lambda s, i, c: (s, i, 0))),
        out_shape=jax.ShapeDtypeStruct((N_CHIPS, PACK_HALF, PACK_COLS), BF16),
        compiler_params=_params(("arbitrary", "arbitrary")),
    )(c_idx, gpack, recv)


def _exchange_chips(ppack):
    def body(p_ref, out_ref, send_sems, recv_sems):
        x, y, c = _mesh_pos()
        chips = _other_chips(x, y)
        cps = [pltpu.make_async_remote_copy(
            src_ref=p_ref.at[2 * chip[0] + chip[1]], dst_ref=out_ref.at[j],
            send_sem=send_sems.at[j], recv_sem=recv_sems.at[j], device_id=(*chip, c), device_id_type=MESH)
            for j, chip in enumerate(chips)]
        for cp in cps:
            cp.start()
        for cp in cps:
            cp.wait_recv()
        for cp in cps:
            cp.wait_send()

    return pl.pallas_call(
        body, name="rs_chip_exchange",
        out_shape=jax.ShapeDtypeStruct((N_CHIPS - 1, PACK_HALF, PACK_COLS), ppack.dtype),
        in_specs=[ANY], out_specs=ANY,
        scratch_shapes=[pltpu.SemaphoreType.DMA((3,)), pltpu.SemaphoreType.DMA((3,))],
        compiler_params=pltpu.CompilerParams(has_side_effects=True),
    )(ppack)


def _chip_add(ppack, recv, me_idx):
    nb = PACK_HALF // PACK_BLOCK

    def body(m_ref, p_ref, r0_ref, r1_ref, r2_ref, o_ref):
        o_ref[...] = ((p_ref[0].astype(F32) + r0_ref[0].astype(F32)) + r1_ref[0].astype(F32)) + r2_ref[0].astype(F32)

    blk = (1, PACK_BLOCK, PACK_COLS)
    return pl.pallas_call(
        body, name="rs_chip_add",
        grid_spec=pltpu.PrefetchScalarGridSpec(
            num_scalar_prefetch=1, grid=(nb,),
            in_specs=[pl.BlockSpec(blk, lambda i, m: (m[0], i, 0)),
                      pl.BlockSpec(blk, lambda i, m: (0, i, 0)),
                      pl.BlockSpec(blk, lambda i, m: (1, i, 0)),
                      pl.BlockSpec(blk, lambda i, m: (2, i, 0))],
            out_specs=pl.BlockSpec((PACK_BLOCK, PACK_COLS), lambda i, m: (i, 0))),
        out_shape=jax.ShapeDtypeStruct((PACK_HALF, PACK_COLS), F32),
        compiler_params=_params(("arbitrary",)),
    )(me_idx, ppack, recv, recv, recv)


def _share_halves(qhalf):
    def body(q_ref, out_ref, send_sem, recv_sem):
        x, y, c = _mesh_pos()
        cp = pltpu.make_async_remote_copy(
            src_ref=q_ref, dst_ref=out_ref, send_sem=send_sem, recv_sem=recv_sem,
            device_id=(x, y, 1 - c), device_id_type=MESH)
        cp.start()
        cp.wait()

    return pl.pallas_call(
        body, name="rs_share_halves",
        out_shape=jax.ShapeDtypeStruct(qhalf.shape, F32),
        in_specs=[ANY], out_specs=ANY,
        scratch_shapes=[pltpu.SemaphoreType.DMA, pltpu.SemaphoreType.DMA],
        compiler_params=pltpu.CompilerParams(has_side_effects=True),
    )(qhalf)


REST_EARLY = ("w_att_proj", "w_ssm_proj", "w_out")
REST_LATE = ("w_up", "w_down")
REST = REST_EARLY + REST_LATE
ADD_ROWS_CAP = 800
BF16_ROWS = 16
IN_SHARD_ROWS = IN_PROJ_WIDTH // N_CHIPS
IN_SHARD_PAD = 2688


def _stack_rest(shards, dtype, names=REST):
    return jnp.concatenate([shards[n].astype(dtype).reshape(-1, PACK_COLS) for n in names], axis=0)


def _unstack_rest(stacked, lead=(), names=REST):
    out, r0 = {}, 0
    for n in names:
        shp = _shard_shape(n)
        rows = shp[0] * shp[1] // PACK_COLS
        out[n] = stacked[..., r0:r0 + rows, :].reshape(lead + shp)
        r0 += rows
    return out


def _full_from_shards(name, sh):
    if BIG_COL_SHARDED[name]:
        return sh.transpose(1, 0, 2).reshape(BIG_FULL_SHAPES[name])
    return sh.reshape(BIG_FULL_SHAPES[name])


def _shards_from_full(name, full):
    shp = _shard_shape(name)
    if BIG_COL_SHARDED[name]:
        return full.reshape(shp[0], N_CHIPS, shp[1]).transpose(1, 0, 2)
    return full.reshape((N_CHIPS,) + shp)


def _allgather2(shards):
    n = len(shards)

    def body(*refs):
        w_refs, out_refs, send_sems, recv_sems = refs[:n], refs[n:2 * n], refs[2 * n], refs[2 * n + 1]
        x, y, c = _mesh_pos()
        me = 2 * x + y
        sibling = (x, y, 1 - c)
        chips = _other_chips(x, y)
        plans = []
        for a, (w_ref, out_ref) in enumerate(zip(w_refs, out_refs)):
            half = w_ref.shape[0] // 2

            def copy(k, chip, h, to, src=None, out_ref=out_ref, half=half, a=a):
                rows = out_ref.at[chip, pl.ds(h * half, half), :]
                return pltpu.make_async_remote_copy(
                    src_ref=rows if src is None else src, dst_ref=rows,
                    send_sem=send_sems.at[6 * a + k], recv_sem=recv_sems.at[6 * a + k],
                    device_id=to, device_id_type=MESH)

            mine_half = w_ref.at[pl.ds(c * half, half), :]
            idx = [2 * chip[0] + chip[1] for chip in chips]
            send = [copy(j, me, c, (*chip, c), src=mine_half) for j, chip in enumerate(chips)]
            land = [copy(j, idx[j], c, (x, y, c)) for j in range(N_CHIPS - 1)]
            forward = [copy(3 + j, idx[j], c, sibling) for j in range(N_CHIPS - 1)]
            land_fw = [copy(3 + j, idx[j], 1 - c, (x, y, c)) for j in range(N_CHIPS - 1)]
            plans.append((send, land, forward, land_fw))
        for send, _, _, _ in plans:
            for cp in send:
                cp.start()
        for _, land, forward, _ in plans:
            for j in range(N_CHIPS - 1):
                land[j].wait_recv()
                forward[j].start()
        for _, _, _, land_fw in plans:
            for cp in land_fw:
                cp.wait_recv()
        for send, _, forward, _ in plans:
            for cp in send + forward:
                cp.wait_send()

    return pl.pallas_call(
        body, name="allgather_weights",
        out_shape=[jax.ShapeDtypeStruct((N_CHIPS,) + s.shape, s.dtype) for s in shards],
        in_specs=[ANY] * n, out_specs=[ANY] * n,
        scratch_shapes=[pltpu.SemaphoreType.DMA((6 * n,)), pltpu.SemaphoreType.DMA((6 * n,))],
        compiler_params=pltpu.CompilerParams(has_side_effects=True),
    )(*shards)


def _exchange_halves2(gs):
    n = len(gs)

    def body(*refs):
        g_refs, out_refs, send_sems, recv_sems = refs[:n], refs[n:2 * n], refs[2 * n], refs[2 * n + 1]
        x, y, c = _mesh_pos()
        cps = []
        for a, (g_ref, out_ref) in enumerate(zip(g_refs, out_refs)):
            half = g_ref.shape[1] // 2
            cps.append(pltpu.make_async_remote_copy(
                src_ref=g_ref.at[:, pl.ds((1 - c) * half, half), :], dst_ref=out_ref,
                send_sem=send_sems.at[a], recv_sem=recv_sems.at[a], device_id=(x, y, 1 - c),
                device_id_type=MESH))
        for cp in cps:
            cp.start()
        for cp in cps:
            cp.wait()

    return pl.pallas_call(
        body, name="rs_pair_exchange",
        out_shape=[jax.ShapeDtypeStruct((N_CHIPS, g.shape[1] // 2, g.shape[2]), F32) for g in gs],
        in_specs=[ANY] * n, out_specs=[ANY] * n,
        scratch_shapes=[pltpu.SemaphoreType.DMA((n,)), pltpu.SemaphoreType.DMA((n,))],
        compiler_params=pltpu.CompilerParams(has_side_effects=True),
    )(*gs)


def _pair_add2(g, recv, c_idx, name):
    _, half, cols = recv.shape
    rb = _row_block(half, ADD_ROWS_CAP, BF16_ROWS)
    nb = half // rb

    def body(c_ref, g_ref, r_ref, o_ref):
        o_ref[...] = _b(g_ref[...] + r_ref[...])

    blk = (1, rb, cols)
    return pl.pallas_call(
        body, name=name,
        grid_spec=pltpu.PrefetchScalarGridSpec(
            num_scalar_prefetch=1, grid=(N_CHIPS, nb),
            in_specs=[pl.BlockSpec(blk, lambda s, i, c: (s, c[0] * nb + i, 0)),
                      pl.BlockSpec(blk, lambda s, i, c: (s, i, 0))],
            out_specs=pl.BlockSpec(blk, lambda s, i, c: (s, i, 0))),
        out_shape=jax.ShapeDtypeStruct(recv.shape, BF16),
        compiler_params=_params(("arbitrary", "arbitrary")),
    )(c_idx, g, recv)


def _exchange_chips2(ps):
    n = len(ps)

    def body(*refs):
        p_refs, out_refs, send_sems, recv_sems = refs[:n], refs[n:2 * n], refs[2 * n], refs[2 * n + 1]
        x, y, c = _mesh_pos()
        chips = _other_chips(x, y)
        cps = [pltpu.make_async_remote_copy(
            src_ref=p_ref.at[2 * chip[0] + chip[1]], dst_ref=out_ref.at[j],
            send_sem=send_sems.at[3 * a + j], recv_sem=recv_sems.at[3 * a + j], device_id=(*chip, c),
            device_id_type=MESH)
            for a, (p_ref, out_ref) in enumerate(zip(p_refs, out_refs)) for j, chip in enumerate(chips)]
        for cp in cps:
            cp.start()
        for cp in cps:
            cp.wait_recv()
        for cp in cps:
            cp.wait_send()

    return pl.pallas_call(
        body, name="rs_chip_exchange",
        out_shape=[jax.ShapeDtypeStruct((N_CHIPS - 1,) + p.shape[1:], p.dtype) for p in ps],
        in_specs=[ANY] * n, out_specs=[ANY] * n,
        scratch_shapes=[pltpu.SemaphoreType.DMA((3 * n,)), pltpu.SemaphoreType.DMA((3 * n,))],
        compiler_params=pltpu.CompilerParams(has_side_effects=True),
    )(*ps)


def _chip_add2(p, recv, me_idx, name):
    _, half, cols = recv.shape
    rb = _row_block(half, ADD_ROWS_CAP, BF16_ROWS)

    def body(m_ref, p_ref, r0_ref, r1_ref, r2_ref, o_ref):
        o_ref[...] = ((p_ref[0].astype(F32) + r0_ref[0].astype(F32)) + r1_ref[0].astype(F32)) + r2_ref[0].astype(F32)

    blk = (1, rb, cols)
    return pl.pallas_call(
        body, name=name,
        grid_spec=pltpu.PrefetchScalarGridSpec(
            num_scalar_prefetch=1, grid=(half // rb,),
            in_specs=[pl.BlockSpec(blk, lambda i, m: (m[0], i, 0)),
                      pl.BlockSpec(blk, lambda i, m: (0, i, 0)),
                      pl.BlockSpec(blk, lambda i, m: (1, i, 0)),
                      pl.BlockSpec(blk, lambda i, m: (2, i, 0))],
            out_specs=pl.BlockSpec((rb, cols), lambda i, m: (i, 0))),
        out_shape=jax.ShapeDtypeStruct((half, cols), F32),
        compiler_params=_params(("arbitrary",)),
    )(me_idx, p, recv, recv, recv)


def _share_halves2(qs):
    n = len(qs)

    def body(*refs):
        q_refs, out_refs, send_sems, recv_sems = refs[:n], refs[n:2 * n], refs[2 * n], refs[2 * n + 1]
        x, y, c = _mesh_pos()
        cps = [pltpu.make_async_remote_copy(
            src_ref=q_ref, dst_ref=out_ref, send_sem=send_sems.at[a], recv_sem=recv_sems.at[a],
            device_id=(x, y, 1 - c), device_id_type=MESH)
            for a, (q_ref, out_ref) in enumerate(zip(q_refs, out_refs))]
        for cp in cps:
            cp.start()
        for cp in cps:
            cp.wait()

    return pl.pallas_call(
        body, name="rs_share_halves",
        out_shape=[jax.ShapeDtypeStruct(q.shape, F32) for q in qs],
        in_specs=[ANY] * n, out_specs=[ANY] * n,
        scratch_shapes=[pltpu.SemaphoreType.DMA((n,)), pltpu.SemaphoreType.DMA((n,))],
        compiler_params=pltpu.CompilerParams(has_side_effects=True),
    )(*qs)


def _gather_plan():
    def copies(w_refs, out_refs, send_sems, recv_sems):
        x, y, c = _mesh_pos()
        me = 2 * x + y
        sibling = (x, y, 1 - c)
        chips = _other_chips(x, y)
        idx = [2 * chip[0] + chip[1] for chip in chips]
        plans = []
        for a, (w_ref, out_ref) in enumerate(zip(w_refs, out_refs)):
            half = w_ref.shape[0] // 2

            def copy(k, chip, h, to, src=None, out_ref=out_ref, half=half, a=a):
                rows = out_ref.at[chip, pl.ds(h * half, half), :]
                return pltpu.make_async_remote_copy(
                    src_ref=rows if src is None else src, dst_ref=rows,
                    send_sem=send_sems.at[6 * a + k], recv_sem=recv_sems.at[6 * a + k],
                    device_id=to, device_id_type=MESH)

            mine_half = w_ref.at[pl.ds(c * half, half), :]
            send = [copy(j, me, c, (*chip, c), src=mine_half) for j, chip in enumerate(chips)]
            land = [copy(j, idx[j], c, (x, y, c)) for j in range(N_CHIPS - 1)]
            forward = [copy(3 + j, idx[j], c, sibling) for j in range(N_CHIPS - 1)]
            land_fw = [copy(3 + j, idx[j], 1 - c, (x, y, c)) for j in range(N_CHIPS - 1)]
            plans.append((send, land, forward, land_fw))
        return plans

    def start(*refs):
        for send, _, _, _ in copies(*refs):
            for cp in send:
                cp.start()

    def finish(*refs):
        plans = copies(*refs)
        for _, land, forward, _ in plans:
            for j in range(N_CHIPS - 1):
                land[j].wait_recv()
                forward[j].start()
        for _, _, _, land_fw in plans:
            for cp in land_fw:
                cp.wait_recv()
        for send, _, forward, _ in plans:
            for cp in send + forward:
                cp.wait_send()

    return start, finish


def _pair_plan(halves):
    def copies(in_refs, out_refs, send_sems, recv_sems):
        x, y, c = _mesh_pos()
        cps = []
        for a, (g_ref, out_ref) in enumerate(zip(in_refs, out_refs)):
            if halves:
                half = g_ref.shape[1] // 2
                src = g_ref.at[:, pl.ds((1 - c) * half, half), :]
            else:
                src = g_ref
            cps.append(pltpu.make_async_remote_copy(
                src_ref=src, dst_ref=out_ref, send_sem=send_sems.at[a], recv_sem=recv_sems.at[a],
                device_id=(x, y, 1 - c), device_id_type=MESH))
        return cps

    def start(*refs):
        for cp in copies(*refs):
            cp.start()

    def finish(*refs):
        for cp in copies(*refs):
            cp.wait()

    return start, finish


def _chip_plan():
    def copies(in_refs, out_refs, send_sems, recv_sems):
        x, y, c = _mesh_pos()
        chips = _other_chips(x, y)
        return [pltpu.make_async_remote_copy(
            src_ref=p_ref.at[2 * chip[0] + chip[1]], dst_ref=out_ref.at[j],
            send_sem=send_sems.at[3 * a + j], recv_sem=recv_sems.at[3 * a + j], device_id=(*chip, c),
            device_id_type=MESH)
            for a, (p_ref, out_ref) in enumerate(zip(in_refs, out_refs)) for j, chip in enumerate(chips)]

    def start(*refs):
        for cp in copies(*refs):
            cp.start()

    def finish(*refs):
        cps = copies(*refs)
        for cp in cps:
            cp.wait_recv()
        for cp in cps:
            cp.wait_send()

    return start, finish


def _gather_comm(shards):
    return _Comm(_gather_plan(), shards, [jax.ShapeDtypeStruct((N_CHIPS,) + s.shape, s.dtype) for s in shards],
                 6 * len(shards))


def _pair_comm(gs):
    return _Comm(_pair_plan(True), gs,
                 [jax.ShapeDtypeStruct((N_CHIPS, g.shape[1] // 2, g.shape[2]), g.dtype) for g in gs], len(gs))


def _chip_comm(ps):
    return _Comm(_chip_plan(), ps, [jax.ShapeDtypeStruct((N_CHIPS - 1,) + p.shape[1:], p.dtype) for p in ps],
                 3 * len(ps))


def _share_comm(qs):
    return _Comm(_pair_plan(False), qs, [jax.ShapeDtypeStruct(q.shape, q.dtype) for q in qs], len(qs))


def _comm_call(name, comm):
    n, m = len(comm.ins), len(comm.outs)

    def body(*refs):
        args = (refs[:n], refs[n:n + m], refs[n + m], refs[n + m + 1])
        comm.start(*args)
        comm.finish(*args)

    return pl.pallas_call(
        body, name=name, out_shape=comm.outs, in_specs=[ANY] * n, out_specs=[ANY] * m,
        scratch_shapes=[pltpu.SemaphoreType.DMA((comm.n_sems,))] * 2,
        compiler_params=pltpu.CompilerParams(has_side_effects=True),
    )(*comm.ins)


class _Exchange:
    def __init__(self, chip, ci, early_mine, late_mine):
        self.chip, self.ci = chip, ci
        self.mine = {REST_EARLY: early_mine, REST_LATE: late_mine}
        self.c_arr = ci.reshape(1).astype(jnp.int32)
        self.chip_arr = chip.reshape(1).astype(jnp.int32)
        self.reduced = {}

    def rest_weights(self, got, names):
        stacks = lax.dynamic_update_slice(got, self.mine[names][None], (self.chip, 0, 0))
        return {n: _full_from_shards(n, sh) for n, sh in _unstack_rest(stacks, (N_CHIPS,), names).items()}

    def finish_reduce(self, key, mine, other):
        south = self.ci == 0
        self.reduced[key] = jnp.concatenate([jnp.where(south, mine, other), jnp.where(south, other, mine)],
                                            axis=0)


def _allreduce_small(part, name):
    rows = part.shape[0]

    def body(p_ref, out_ref, buf, send_sems, recv_sems, local_sem):
        x, y, c = _mesh_pos()
        me, sibling = (x, y, c), (x, y, 1 - c)
        chips = _other_chips(x, y)

        def slot(px, py, pc):
            return buf.at[pl.ds((4 * px + 2 * py + pc) * rows, rows), :]

        def copy(k, block, to, src=None):
            return pltpu.make_async_remote_copy(
                src_ref=slot(*block) if src is None else src, dst_ref=slot(*block),
                send_sem=send_sems.at[k], recv_sem=recv_sems.at[k], device_id=to, device_id_type=MESH)

        mine = pltpu.make_async_copy(p_ref, slot(*me), local_sem)
        mine.start()
        first = [copy(0, me, sibling, src=p_ref)]
        first += [copy(1 + j, me, (*chip, c), src=p_ref) for j, chip in enumerate(chips)]
        for cp in first:
            cp.start()
        passed = [copy(4 + j, (*chip, c), sibling) for j, chip in enumerate(chips)]
        for j, chip in enumerate(chips):
            copy(1 + j, (*chip, c), me).wait_recv()
            passed[j].start()
        copy(0, sibling, me).wait_recv()
        for j, chip in enumerate(chips):
            copy(4 + j, (*chip, 1 - c), me).wait_recv()
        for cp in first + passed:
            cp.wait_send()
        mine.wait()
        acc = buf[pl.ds(0, rows), :]
        for k in range(1, N_DEV):
            acc = acc + buf[pl.ds(k * rows, rows), :]
        out_ref[...] = acc

    return pl.pallas_call(
        body, name=name,
        out_shape=jax.ShapeDtypeStruct(part.shape, F32),
        in_specs=[pl.BlockSpec(memory_space=pltpu.VMEM)],
        out_specs=pl.BlockSpec(memory_space=pltpu.VMEM),
        scratch_shapes=[pltpu.VMEM((N_DEV * rows, LANES), F32), pltpu.SemaphoreType.DMA((7,)),
                        pltpu.SemaphoreType.DMA((7,)), pltpu.SemaphoreType.DMA],
        compiler_params=pltpu.CompilerParams(has_side_effects=True),
    )(part)


def _adamw(w, g, m, v, name):
    R, C = w.shape
    bs = _row_block(R, 512, 8) if R % 8 == 0 else R
    c1 = 1.0 / (1.0 - ADAM_B1 ** ADAM_STEP)
    c2 = 1.0 / (1.0 - ADAM_B2 ** ADAM_STEP)

    def body(w_ref, g_ref, m_ref, v_ref, d_ref, nm_ref, nv_ref):
        gg = g_ref[...]
        nm = ADAM_B1 * m_ref[...] + (1.0 - ADAM_B1) * gg
        nv = ADAM_B2 * v_ref[...] + (1.0 - ADAM_B2) * (gg * gg)
        nm_ref[...] = nm
        nv_ref[...] = nv
        d_ref[...] = -ADAM_LR * ((nm * c1) / (jnp.sqrt(nv * c2) + ADAM_EPS) + ADAM_WD * w_ref[...])

    spec = pl.BlockSpec((bs, C), lambda i: (i, 0))
    shp = jax.ShapeDtypeStruct((R, C), F32)
    return pl.pallas_call(
        body, name=name, grid=(R // bs,), in_specs=[spec] * 4, out_specs=[spec] * 3, out_shape=[shp] * 3,
        compiler_params=_params(("parallel",)),
    )(w, g, m, v)


WEIGHTS = ("norm_mix_pre_w", "w_in", "b_gate", "conv_w", "conv_b", "dt_bias", "a_log", "d_skip",
           "ssm_norm_w", "w_att_proj", "w_ssm_proj", "w_out", "norm_mix_post_w", "norm_ffn_pre_w", "w_up",
           "w_down", "norm_ffn_post_w")


def _flat_small(vals, conv_w_full):
    flat = [vals[n].reshape(-1) for n in SMALL] + [conv_w_full.reshape(-1)]
    v = jnp.concatenate(flat)
    return jnp.pad(v, (0, SMALL_ROWS * LANES - v.shape[0])).reshape(SMALL_ROWS, LANES)


def kernel(x, norm_mix_pre_w, w_in, b_gate, conv_w, conv_b, dt_bias, a_log, d_skip, ssm_norm_w, w_att_proj, w_ssm_proj, w_out, norm_mix_post_w, norm_ffn_pre_w, w_up, w_down, norm_ffn_post_w, loss_target, m_norm_mix_pre_w, m_w_in, m_b_gate, m_conv_w, m_conv_b, m_dt_bias, m_a_log, m_d_skip, m_ssm_norm_w, m_w_att_proj, m_w_ssm_proj, m_w_out, m_norm_mix_post_w, m_norm_ffn_pre_w, m_w_up, m_w_down, m_norm_ffn_post_w, v_norm_mix_pre_w, v_w_in, v_b_gate, v_conv_w, v_conv_b, v_dt_bias, v_a_log, v_d_skip, v_ssm_norm_w, v_w_att_proj, v_w_ssm_proj, v_w_out, v_norm_mix_post_w, v_norm_ffn_pre_w, v_w_up, v_w_down, v_norm_ffn_post_w):
    args = locals()

    def strip(a):
        return a[0] if a.ndim == 3 else a

    wts = {n: strip(args[n]) for n in WEIGHTS}
    mom = {n: strip(args["m_" + n]) for n in WEIGHTS}
    var = {n: strip(args["v_" + n]) for n in WEIGHTS}
    xi, yi, ci = _mesh_pos()
    chip = 2 * xi + yi

    tr = lambda a: jnp.swapaxes(a, 0, 1)
    w_in_mine = jnp.pad(tr(wts["w_in"]).astype(BF16), ((0, IN_SHARD_PAD - IN_SHARD_ROWS), (0, 0)))
    got_in = _comm_call("allgather_w_in", _gather_comm([w_in_mine]))[0]
    stacks_in = lax.dynamic_update_slice(got_in, w_in_mine[None], (chip, 0, 0))
    full = {"w_in_t": stacks_in[:, :IN_SHARD_ROWS].reshape(IN_PROJ_WIDTH, D_MODEL)}
    ex = _Exchange(chip, ci, _stack_rest(wts, BF16, REST_EARLY), _stack_rest(wts, BF16, REST_LATE))
    cw_cols = CONV_DIM // N_CHIPS
    conv_slab = lax.dynamic_update_slice(jnp.zeros((SSM_CONV, CONV_DIM), F32),
                                         jnp.where(ci == 0, wts["conv_w"], 0.0), (0, chip * cw_cols))
    small_in = jnp.pad(conv_slab.reshape(-1), (0, SMALL_ROWS * LANES - SSM_CONV * CONV_DIM))
    conv_full = _allreduce_small(small_in.reshape(SMALL_ROWS, LANES), "gather_conv_w")
    full["conv_w"] = conv_full.reshape(-1)[:SSM_CONV * CONV_DIM].reshape(SSM_CONV, CONV_DIM)
    for n in SMALL:
        full[n] = wts[n]

    loss_part, grad_x, g = _local_step(x[0], loss_target[0], full, ex)
    loss = lax.psum(loss_part[0, 0], ("x", "y", "c"))

    gshard = _unstack_rest(ex.reduced["rest"])
    g_in_t = ex.reduced["w_in"][:IN_SHARD_ROWS]
    small_sum = _allreduce_small(_flat_small(g, g["conv_w"]), "allreduce_small_grads").reshape(-1)
    grads, off = {}, 0
    for n in SMALL:
        sz = wts[n].size
        grads[n] = small_sum[off:off + sz].reshape(wts[n].shape)
        off += sz
    conv_g = small_sum[off:off + SSM_CONV * CONV_DIM].reshape(SSM_CONV, CONV_DIM)
    grads["conv_w"] = lax.dynamic_slice(conv_g, (0, chip * cw_cols), (SSM_CONV, cw_cols))
    grads.update(gshard)

    delta, new_m, new_v = {}, {}, {}
    for n in REST:
        delta[n], new_m[n], new_v[n] = _adamw(wts[n], grads[n], mom[n], var[n], f"adamw_{n}")
    in_t = _adamw(tr(wts["w_in"]), g_in_t, tr(mom["w_in"]), tr(var["w_in"]), "adamw_w_in")
    grads["w_in"] = tr(g_in_t)
    delta["w_in"], new_m["w_in"], new_v["w_in"] = (tr(a) for a in in_t)
    small_names = SMALL + ("conv_w",)

    def pack_small(d):
        v = jnp.concatenate([d[n].reshape(-1) for n in small_names])
        rows = -(-v.shape[0] // (8 * LANES)) * 8
        return jnp.pad(v, (0, rows * LANES - v.shape[0])).reshape(rows, LANES)

    ds, ms, vs = _adamw(pack_small(wts), pack_small(grads), pack_small(mom), pack_small(var), "adamw_small")
    off = 0
    for n in small_names:
        sz = wts[n].size
        for dst, src in ((delta, ds), (new_m, ms), (new_v, vs)):
            dst[n] = src.reshape(-1)[off:off + sz].reshape(wts[n].shape)
        off += sz

    out = [loss, grad_x[None]]
    for d in (grads, delta, new_m, new_v):
        out += [d[n][None] if args[n].ndim == 3 else d[n] for n in WEIGHTS]
    return tuple(out)
```

```python
import functools
import math

import numpy as np
import jax
import jax.numpy as jnp
from jax import lax
from jax.experimental import pallas as pl
from jax.experimental.pallas import tpu as pltpu

F32 = jnp.float32
BF16 = jnp.bfloat16

D_MODEL = 1024
HEAD_DIM = 64
N_ATT_HEADS = 12
ATT_WIDTH = N_ATT_HEADS * HEAD_DIM
DILATIONS = (1, 4, 16)
ATT_BLOCK = 128
SSM_INNER = 2048
SSM_HEADS = 32
SSM_GROUPS = 8
HEADS_PER_GROUP = SSM_HEADS // SSM_GROUPS
SSM_HEAD_DIM = 64
SSM_STATE = 128
SSM_CONV = 4
SSM_CHUNK = 128
CONV_DIM = SSM_INNER + 2 * SSM_GROUPS * SSM_STATE
FFN_HIDDEN = 4 * D_MODEL
IN_SPLITS = (ATT_WIDTH, ATT_WIDTH, ATT_WIDTH, SSM_INNER, CONV_DIM, SSM_HEADS, 2 * D_MODEL)
IN_PROJ_WIDTH = sum(IN_SPLITS)
RMS_EPS = 1e-6
LANES = 128
NEG_BIG = -1e30

ADAM_LR = 0.001
ADAM_B1 = 0.9
ADAM_B2 = 0.999
ADAM_EPS = 1e-08
ADAM_WD = 0.01
ADAM_STEP = 10

N_CHIPS = 4
N_DEV = 8
VMEM_LIMIT = 56 * 1024 * 1024
MESH = pl.DeviceIdType.MESH


def _alibi_slopes(n):
    def pow2(m):
        start = 2.0 ** (-8.0 / m)
        return [start ** (i + 1) for i in range(m)]
    if (n & (n - 1)) == 0:
        s = pow2(n)
    else:
        c = 2 ** int(math.floor(math.log2(n)))
        s = pow2(c) + pow2(2 * c)[0::2][: n - c]
    return [float(v) for v in np.array(s, dtype=np.float32)]


def _params(sem):
    return pltpu.CompilerParams(dimension_semantics=sem, vmem_limit_bytes=VMEM_LIMIT)


def _dot(a, b):
    return lax.dot_general(a, b, (((1,), (0,)), ((), ())), preferred_element_type=F32)


def _dot_nt(a, b):
    return lax.dot_general(a, b, (((1,), (1,)), ((), ())), preferred_element_type=F32)


def _dot_tn(a, b):
    return lax.dot_general(a, b, (((0,), (0,)), ((), ())), preferred_element_type=F32)


def _dot_hi(a, b):
    return lax.dot_general(a, b, (((1,), (0,)), ((), ())), preferred_element_type=F32,
                           precision=lax.Precision.HIGHEST)


def _dot_tn_hi(a, b):
    return lax.dot_general(a, b, (((0,), (0,)), ((), ())), preferred_element_type=F32,
                           precision=lax.Precision.HIGHEST)


def _b(x):
    return x.astype(BF16)


def _sigmoid(x):
    return 1.0 / (1.0 + jnp.exp(-x))


def _pick(n, cands):
    for c in cands:
        if n % c == 0:
            return c
    raise ValueError(f"no tile for {n}")


def _row_block(rows, cap, mult):
    best = max(d for d in range(mult, cap + 1, mult) if rows % d == 0)
    return best


class _Comm:
    def __init__(self, plan, ins, outs, n_sems):
        self.start, self.finish = plan
        self.ins, self.outs, self.n_sems = list(ins), list(outs), n_sems


def _mm_nn(a, b, out_dtype, name, acc=None, mode=None, extra=None, comm=None, tb=False):
    M, K = a.shape
    N = b.shape[0] if tb else b.shape[1]
    tn = _pick(N, (1024, 768, 512, 256, 128))
    tk = K if K <= 4096 else _pick(K, (2048, 1024))
    tm = 1024 if M % 1024 == 0 and K <= 2304 else 512
    nk = K // tk
    nj, ni = N // tn, M // tm
    side = acc if acc is not None else extra
    n_out = 2 if mode == "relu2" else 1
    n_in = 2 + (side is not None)
    n_ci = len(comm.ins) if comm else 0
    n_co = len(comm.outs) if comm else 0

    def body(*refs):
        a_ref, b_ref = refs[0], refs[1]
        s_ref = refs[2] if side is not None else None
        o_refs = refs[n_in + n_ci:n_in + n_ci + n_out]
        if comm:
            c_args = (refs[n_in:n_in + n_ci], refs[n_in + n_ci + n_out:n_in + n_ci + n_out + n_co],
                      refs[-2], refs[-1])
            pj, pi, pk = pl.program_id(0), pl.program_id(1), pl.program_id(2)

            @pl.when(jnp.logical_and(jnp.logical_and(pj == 0, pi == 0), pk == 0))
            def _():
                comm.start(*c_args)

        def finish(r):
            if mode == "relu2":
                r = jnp.maximum(r, 0.0)
                o_refs[0][...] = _b(r)
                o_refs[1][...] = _b(r * r)
            elif mode == "mul2":
                o_refs[0][...] = _b(r * (2.0 * s_ref[...].astype(F32)))
            else:
                if acc is not None:
                    r = r + s_ref[...]
                o_refs[0][...] = r.astype(out_dtype)

        part = (_dot_nt if tb else _dot)(_b(a_ref[...]), _b(b_ref[...]))
        if nk == 1:
            finish(part)
        else:
            acc_ref = refs[n_in + n_ci + n_out + n_co]
            k = pl.program_id(2)

            @pl.when(k == 0)
            def _():
                acc_ref[...] = part

            @pl.when(jnp.logical_and(k > 0, k < nk - 1))
            def _():
                acc_ref[...] += part

            @pl.when(k == nk - 1)
            def _():
                finish(acc_ref[...] + part)

        if comm:
            @pl.when(jnp.logical_and(jnp.logical_and(pj == nj - 1, pi == ni - 1), pk == nk - 1))
            def _():
                comm.finish(*c_args)

    tile = pl.BlockSpec((tm, tn), lambda j, i, k: (i, j))
    in_specs = [pl.BlockSpec((tm, tk), lambda j, i, k: (i, k)),
                pl.BlockSpec((tn, tk), lambda j, i, k: (j, k)) if tb else
                pl.BlockSpec((tk, tn), lambda j, i, k: (k, j))]
    args = [a, b]
    if side is not None:
        in_specs.append(tile)
        args.append(side)
    odt = BF16 if mode in ("relu2", "mul2") else out_dtype
    scratch = [pltpu.VMEM((tm, tn), F32)] if nk > 1 else []
    if comm:
        scratch += [pltpu.SemaphoreType.DMA((comm.n_sems,))] * 2
        params = pltpu.CompilerParams(dimension_semantics=("arbitrary",) * 3, vmem_limit_bytes=VMEM_LIMIT,
                                      has_side_effects=True)
    else:
        params = _params(("parallel", "parallel", "arbitrary"))
    outs = pl.pallas_call(
        body, name=name, grid=(nj, ni, nk),
        in_specs=in_specs + [ANY] * n_ci,
        out_specs=[tile] * n_out + [ANY] * n_co,
        out_shape=[jax.ShapeDtypeStruct((M, N), odt)] * n_out + list(comm.outs if comm else []),
        scratch_shapes=scratch,
        compiler_params=params,
    )(*args, *(comm.ins if comm else []))
    res = outs[:n_out] if n_out > 1 else outs[0]
    return (res, outs[n_out:]) if comm else res


class _Epi:
    def __init__(self, fn, row_ins=(), full_ins=(), row_outs=(), acc_outs=(), tiled=False, a_fn=None):
        self.fn, self.row_ins, self.full_ins = fn, list(row_ins), list(full_ins)
        self.row_outs, self.acc_outs, self.tiled = list(row_outs), list(acc_outs), tiled
        self.a_fn = a_fn


def _acc_into(ref, val, first):
    @pl.when(first)
    def _():
        ref[...] = val

    @pl.when(jnp.logical_not(first))
    def _():
        ref[...] += val


def _mm_epi(a, b, epi, name, tb=False, comm=None, tm=512, tn=None):
    N, K = b.shape if tb else b.shape[::-1]
    M = epi.row_ins[0][0].shape[0] if a is None else a.shape[0]
    tn = tn or N
    assert epi.tiled or tn == N
    tk = K if K <= 4096 else _pick(K, (2048, 1024))
    nk = K // tk
    nj, ni = N // tn, M // tm
    assert a is not None or (nk == 1 and nj == 1)
    n_a = 0 if a is None else 1
    n_ri, n_fi, n_ro, n_ao = len(epi.row_ins), len(epi.full_ins), len(epi.row_outs), len(epi.acc_outs)
    n_ci = len(comm.ins) if comm else 0
    n_co = len(comm.outs) if comm else 0
    i0 = n_a + 1
    o0 = i0 + n_ci + n_ri + n_fi

    def body(*refs):
        b_ref = refs[n_a]
        ri = refs[i0 + n_ci:i0 + n_ci + n_ri]
        fi = refs[i0 + n_ci + n_ri:o0]
        ro = refs[o0 + n_co:o0 + n_co + n_ro]
        ao = refs[o0 + n_co + n_ro:o0 + n_co + n_ro + n_ao]
        pj, pi, pk = pl.program_id(0), pl.program_id(1), pl.program_id(2)
        if comm:
            c_args = (refs[i0:i0 + n_ci], refs[o0:o0 + n_co], refs[-2], refs[-1])

            @pl.when(jnp.logical_and(jnp.logical_and(pj == 0, pi == 0), pk == 0))
            def _():
                comm.start(*c_args)

        a_val = epi.a_fn(ri, fi, ro) if a is None else _b(refs[0][...])
        part = (_dot_nt if tb else _dot)(a_val, _b(b_ref[...]))
        if nk == 1:
            epi.fn(part, ri, fi, ro, ao, pi == 0)
        else:
            acc_ref = refs[o0 + n_co + n_ro + n_ao]

            @pl.when(pk == 0)
            def _():
                acc_ref[...] = part

            @pl.when(jnp.logical_and(pk > 0, pk < nk - 1))
            def _():
                acc_ref[...] += part

            @pl.when(pk == nk - 1)
            def _():
                epi.fn(acc_ref[...] + part, ri, fi, ro, ao, pi == 0)

        if comm:
            @pl.when(jnp.logical_and(jnp.logical_and(pj == nj - 1, pi == ni - 1), pk == nk - 1))
            def _():
                comm.finish(*c_args)

    def row_spec(width, cb):
        if epi.tiled:
            return pl.BlockSpec((tm, tn), lambda j, i, k: (i, j + cb))
        return pl.BlockSpec((tm, width), lambda j, i, k: (i, cb))

    in_specs = [pl.BlockSpec((tm, tk), lambda j, i, k: (i, k))] * n_a
    in_specs += [pl.BlockSpec((tn, tk), lambda j, i, k: (j, k)) if tb else
                 pl.BlockSpec((tk, tn), lambda j, i, k: (k, j))]
    in_specs += [ANY] * n_ci
    in_specs += [row_spec(w, cb) for (_, w, cb) in epi.row_ins]
    in_specs += [pl.BlockSpec((1, tn), lambda j, i, k: (0, j)) if epi.tiled else
                 pl.BlockSpec(f.shape, lambda j, i, k: (0, 0)) for f in epi.full_ins]
    out_specs = [ANY] * n_co + [row_spec(c, 0) for c, _ in epi.row_outs]
    out_specs += [pl.BlockSpec((1, tn), lambda j, i, k: (0, j)) if epi.tiled else
                  pl.BlockSpec((1, c), lambda j, i, k: (0, 0)) for c in epi.acc_outs]
    out_shape = list(comm.outs if comm else [])
    out_shape += [jax.ShapeDtypeStruct((M, c), dt_) for c, dt_ in epi.row_outs]
    out_shape += [jax.ShapeDtypeStruct((1, c), F32) for c in epi.acc_outs]
    scratch = [pltpu.VMEM((tm, tn), F32)] if nk > 1 else []
    if comm:
        scratch += [pltpu.SemaphoreType.DMA((comm.n_sems,))] * 2
    params = pltpu.CompilerParams(dimension_semantics=("arbitrary",) * 3, vmem_limit_bytes=VMEM_LIMIT,
                                  has_side_effects=comm is not None)
    outs = pl.pallas_call(
        body, name=name, grid=(nj, ni, nk), in_specs=in_specs, out_specs=out_specs, out_shape=out_shape,
        scratch_shapes=scratch, compiler_params=params,
    )(*([a] * n_a), b, *(comm.ins if comm else []), *[arr for arr, _, _ in epi.row_ins], *epi.full_ins)
    return outs[n_co:], (outs[:n_co] if comm else None)


def _mm_tn(a, b, name):
    S, Ka = a.shape
    _, N = b.shape
    tka = _pick(Ka, (1024, 768, 512, 256, 128))
    tn = _pick(N, (1024, 768, 512, 256, 128))
    ts = 1024 if S % 1024 == 0 else 512
    ns = S // ts

    def body(a_ref, b_ref, o_ref, acc_ref):
        s = pl.program_id(2)
        part = _dot_tn(_b(a_ref[...]), _b(b_ref[...]))

        @pl.when(s == 0)
        def _():
            acc_ref[...] = part

        @pl.when(s > 0)
        def _():
            acc_ref[...] += part

        @pl.when(s == ns - 1)
        def _():
            o_ref[...] = acc_ref[...]

    return pl.pallas_call(
        body, name=name, grid=(Ka // tka, N // tn, ns),
        in_specs=[pl.BlockSpec((ts, tka), lambda i, j, s: (s, i)),
                  pl.BlockSpec((ts, tn), lambda i, j, s: (s, j))],
        out_specs=pl.BlockSpec((tka, tn), lambda i, j, s: (i, j)),
        out_shape=jax.ShapeDtypeStruct((Ka, N), F32),
        scratch_shapes=[pltpu.VMEM((tka, tn), F32)],
        compiler_params=_params(("parallel", "parallel", "arbitrary")),
    )(a, b)


def _row_call(body, row_ins, full_ins, row_outs, acc_outs, bs, name):
    S = row_ins[0].shape[0]
    assert S % bs == 0
    in_specs = [pl.BlockSpec((bs, a.shape[1]), lambda i: (i, 0)) for a in row_ins]
    in_specs += [pl.BlockSpec(a.shape, lambda i: (0, 0)) for a in full_ins]
    out_specs = [pl.BlockSpec((bs, c), lambda i: (i, 0)) for c, _ in row_outs]
    out_specs += [pl.BlockSpec(s, lambda i: (0, 0)) for s in acc_outs]
    out_shape = [jax.ShapeDtypeStruct((S, c), dt) for c, dt in row_outs]
    out_shape += [jax.ShapeDtypeStruct(s, F32) for s in acc_outs]
    return pl.pallas_call(
        body, name=name, grid=(S // bs,), in_specs=in_specs, out_specs=out_specs, out_shape=out_shape,
        compiler_params=_params(("arbitrary",)),
    )(*row_ins, *full_ins)


def _rms_vals(x, w):
    r = lax.rsqrt(jnp.mean(x * x, axis=-1, keepdims=True) + RMS_EPS)
    return x * r * w


def _rms_bwd_vals(x, w, dy):
    r = lax.rsqrt(jnp.mean(x * x, axis=-1, keepdims=True) + RMS_EPS)
    xn = x * r
    g = dy * w
    dx = r * (g - xn * jnp.mean(g * xn, axis=-1, keepdims=True))
    dw = jnp.sum(dy * xn, axis=0, keepdims=True)
    return dx, dw


def _acc_add(ref, val):
    @pl.when(pl.program_id(0) == 0)
    def _():
        ref[...] = val

    @pl.when(pl.program_id(0) > 0)
    def _():
        ref[...] += val


def _rms_fwd(x, w):
    def body(x_ref, w_ref, o_ref):
        o_ref[...] = _b(_rms_vals(x_ref[...], w_ref[...]))
    return _row_call(body, [x], [w], [(x.shape[1], BF16)], [], 512, "rms_fwd")[0]


def _gate_fwd(att_o, ssm_o, gl, b_gate):
    def body(a_ref, s_ref, g_ref, b_ref, o_ref):
        g = _sigmoid(g_ref[...] + b_ref[...])
        o_ref[...] = _b(g[:, :D_MODEL] * a_ref[...] + g[:, D_MODEL:] * s_ref[...])
    return _row_call(body, [att_o, ssm_o, gl], [b_gate], [(D_MODEL, BF16)], [], 512, "gate_fwd")[0]


def _post_pre(x, mixed, w_post, w_pre):
    def body(x_ref, m_ref, wp_ref, wn_ref, h_ref, f_ref):
        h = x_ref[...] + _rms_vals(m_ref[...], wp_ref[...])
        h_ref[...] = h
        f_ref[...] = _b(_rms_vals(h, wn_ref[...]))
    return _row_call(body, [x, mixed], [w_post, w_pre], [(D_MODEL, F32), (D_MODEL, BF16)], [], 512,
                     "post_pre")


def _relu2(up):
    def body(u_ref, o_ref):
        r = jnp.maximum(u_ref[...], 0.0)
        o_ref[...] = _b(r * r)
    return _row_call(body, [up], [], [(up.shape[1], BF16)], [], 256, "relu2")[0]


def _final(h1, down, w_post, target):
    def body(h_ref, d_ref, t_ref, w_ref, dh_ref, dd_ref, loss_ref, dw_ref):
        dn = d_ref[...]
        w = w_ref[...]
        err = h_ref[...] + _rms_vals(dn, w) - t_ref[...]
        row = jnp.mean(err * err, axis=-1, keepdims=True)
        part = 0.5 * jnp.sum(row, axis=0, keepdims=True)
        dh = err * (1.0 / D_MODEL)
        dh_ref[...] = dh
        dx, dw = _rms_bwd_vals(dn, w, dh)
        dd_ref[...] = _b(dx)
        _acc_add(loss_ref, jnp.broadcast_to(part, (1, LANES)))
        _acc_add(dw_ref, dw)
    return _row_call(body, [h1, down, target], [w_post], [(D_MODEL, F32), (D_MODEL, BF16)],
                     [(1, LANES), (1, D_MODEL)], 512, "final_loss")


def _dup(da, up):
    def body(a_ref, u_ref, o_ref):
        o_ref[...] = _b(a_ref[...] * (2.0 * jnp.maximum(u_ref[...], 0.0)))
    return _row_call(body, [da, up], [], [(up.shape[1], BF16)], [], 256, "relu2_bwd")[0]


def _mid_bwd(dh2, df, h1, mixed, w_pre, w_post):
    def body(dh_ref, df_ref, h_ref, m_ref, wn_ref, wp_ref, dh1_ref, dm_ref, dwn_ref, dwp_ref):
        dx, dwn = _rms_bwd_vals(h_ref[...], wn_ref[...], df_ref[...])
        dh1 = dh_ref[...] + dx
        dh1_ref[...] = dh1
        dm, dwp = _rms_bwd_vals(m_ref[...], wp_ref[...], dh1)
        dm_ref[...] = _b(dm)
        _acc_add(dwn_ref, dwn)
        _acc_add(dwp_ref, dwp)
    return _row_call(body, [dh2, df, h1, mixed], [w_pre, w_post], [(D_MODEL, F32), (D_MODEL, BF16)],
                     [(1, D_MODEL), (1, D_MODEL)], 512, "mid_bwd")


def _gate_bwd(dmi, att_o, ssm_o, gl, b_gate):
    def body(d_ref, a_ref, s_ref, g_ref, b_ref, da_ref, ds_ref, dg_ref, db_ref):
        g = _sigmoid(g_ref[...] + b_ref[...])
        d = d_ref[...]
        ga, gs = g[:, :D_MODEL], g[:, D_MODEL:]
        da_ref[...] = _b(ga * d)
        ds_ref[...] = _b(gs * d)
        dga = d * a_ref[...] * ga * (1.0 - ga)
        dgs = d * s_ref[...] * gs * (1.0 - gs)
        dg_ref[:, :D_MODEL] = _b(dga)
        dg_ref[:, D_MODEL:] = _b(dgs)
        _acc_add(db_ref.at[:, pl.ds(0, D_MODEL)], jnp.sum(dga, axis=0, keepdims=True))
        _acc_add(db_ref.at[:, pl.ds(D_MODEL, D_MODEL)], jnp.sum(dgs, axis=0, keepdims=True))
    return _row_call(body, [dmi, att_o, ssm_o, gl], [b_gate],
                     [(D_MODEL, BF16), (D_MODEL, BF16), (2 * D_MODEL, BF16)], [(1, 2 * D_MODEL)], 256,
                     "gate_bwd")


def _first_bwd(dh1, du, x, w_pre):
    def body(dh_ref, du_ref, x_ref, w_ref, dx_ref, dw_ref):
        dx, dw = _rms_bwd_vals(x_ref[...], w_ref[...], du_ref[...])
        dx_ref[...] = dh_ref[...] + dx
        _acc_add(dw_ref, dw)
    return _row_call(body, [dh1, du, x], [w_pre], [(D_MODEL, F32)], [(1, D_MODEL)], 512, "first_bwd")


def _group_rms(t):
    gw = SSM_INNER // SSM_GROUPS
    out = []
    for g in range(SSM_GROUPS):
        tg = t[:, g * gw:(g + 1) * gw]
        out.append(lax.rsqrt(jnp.mean(tg * tg, axis=-1, keepdims=True) + RMS_EPS))
    return out


def _gnorm_fwd(y, z, w):
    gw = SSM_INNER // SSM_GROUPS

    def body(y_ref, z_ref, w_ref, o_ref):
        zz = z_ref[...]
        t = y_ref[...] * (zz * _sigmoid(zz))
        rs = _group_rms(t)
        for g in range(SSM_GROUPS):
            sl = slice(g * gw, (g + 1) * gw)
            o_ref[:, sl] = _b(t[:, sl] * rs[g] * w_ref[:, sl])
    return _row_call(body, [y, z], [w], [(SSM_INNER, BF16)], [], 256, "gnorm_fwd")[0]


def _gnorm_bwd(dout, y, z, w):
    gw = SSM_INNER // SSM_GROUPS

    def body(d_ref, y_ref, z_ref, w_ref, dy_ref, dz_ref, dw_ref):
        zz = z_ref[...]
        yy = y_ref[...]
        sg = _sigmoid(zz)
        sz = zz * sg
        t = yy * sz
        rs = _group_rms(t)
        for g in range(SSM_GROUPS):
            sl = slice(g * gw, (g + 1) * gw)
            tn = t[:, sl] * rs[g]
            d = d_ref[:, sl]
            gg = d * w_ref[:, sl]
            dt = rs[g] * (gg - tn * jnp.mean(gg * tn, axis=-1, keepdims=True))
            dy_ref[:, sl] = dt * sz[:, sl]
            dz_ref[:, sl] = _b(dt * yy[:, sl] * (sg[:, sl] * (1.0 + zz[:, sl] * (1.0 - sg[:, sl]))))
            _acc_add(dw_ref.at[:, pl.ds(g * gw, gw)], jnp.sum(d * tn, axis=0, keepdims=True))
    return _row_call(body, [dout, y, z], [w], [(SSM_INNER, F32), (SSM_INNER, BF16)], [(1, SSM_INNER)], 256,
                     "gnorm_bwd")


def _ssm_out_epi(y, z, w):
    gw = SSM_INNER // SSM_GROUPS

    def a_fn(ri, fi, ro):
        zz = ri[1][...]
        t = ri[0][...] * (zz * _sigmoid(zz))
        rs = _group_rms(t)
        for g in range(SSM_GROUPS):
            sl = slice(g * gw, (g + 1) * gw)
            ro[0][:, sl] = _b(t[:, sl] * rs[g] * fi[0][:, sl])
        return ro[0][...]

    def fn(r, ri, fi, ro, ao, first):
        ro[1][...] = r
    return _Epi(fn, [(y, SSM_INNER, 0), (z, SSM_INNER, 0)], [w], [(SSM_INNER, BF16), (D_MODEL, F32)], a_fn=a_fn)


def _mix_out_epi(att_o, ssm_o, gl, x, b_gate, w_post, w_pre):
    def a_fn(ri, fi, ro):
        g = _sigmoid(ri[2][...] + fi[0][...])
        mi = _b(g[:, :D_MODEL] * ri[0][...] + g[:, D_MODEL:] * ri[1][...])
        ro[0][...] = mi
        return mi

    def fn(r, ri, fi, ro, ao, first):
        ro[1][...] = r
        h = ri[3][...] + _rms_vals(r, fi[1][...])
        ro[2][...] = h
        ro[3][...] = _b(_rms_vals(h, fi[2][...]))
    return _Epi(fn, [(att_o, D_MODEL, 0), (ssm_o, D_MODEL, 0), (gl, 2 * D_MODEL, 0), (x, D_MODEL, 0)],
                [b_gate, w_post, w_pre],
                [(D_MODEL, BF16), (D_MODEL, F32), (D_MODEL, F32), (D_MODEL, BF16)], a_fn=a_fn)


def _final_epi(h1, target, w_post):
    def fn(dn, ri, fi, ro, ao, first):
        w = fi[0][...]
        err = ri[0][...] + _rms_vals(dn, w) - ri[1][...]
        row = jnp.mean(err * err, axis=-1, keepdims=True)
        part = 0.5 * jnp.sum(row, axis=0, keepdims=True)
        dh = err * (1.0 / D_MODEL)
        ro[0][...] = dh
        dx, dw = _rms_bwd_vals(dn, w, dh)
        ro[1][...] = _b(dx)
        _acc_into(ao[0], jnp.broadcast_to(part, (1, LANES)), first)
        _acc_into(ao[1], dw, first)
    return _Epi(fn, [(h1, D_MODEL, 0), (target, D_MODEL, 0)], [w_post], [(D_MODEL, F32), (D_MODEL, BF16)],
                [LANES, D_MODEL])


def _mid_epi(dh2, h1, mixed, w_pre, w_post):
    def fn(df, ri, fi, ro, ao, first):
        dx, dwn = _rms_bwd_vals(ri[1][...], fi[0][...], df)
        dh1 = ri[0][...] + dx
        ro[0][...] = dh1
        dm, dwp = _rms_bwd_vals(ri[2][...], fi[1][...], dh1)
        ro[1][...] = _b(dm)
        _acc_into(ao[0], dwn, first)
        _acc_into(ao[1], dwp, first)
    return _Epi(fn, [(dh2, D_MODEL, 0), (h1, D_MODEL, 0), (mixed, D_MODEL, 0)], [w_pre, w_post],
                [(D_MODEL, F32), (D_MODEL, BF16)], [D_MODEL, D_MODEL])


def _gate_epi(att_o, ssm_o, gl, b_gate):
    def fn(d, ri, fi, ro, ao, first):
        g = _sigmoid(ri[2][...] + fi[0][...])
        ga, gs = g[:, :D_MODEL], g[:, D_MODEL:]
        ro[0][...] = _b(ga * d)
        ro[1][...] = _b(gs * d)
        dga = d * ri[0][...] * ga * (1.0 - ga)
        dgs = d * ri[1][...] * gs * (1.0 - gs)
        ro[2][:, :D_MODEL] = _b(dga)
        ro[2][:, D_MODEL:] = _b(dgs)
        _acc_into(ao[0].at[:, pl.ds(0, D_MODEL)], jnp.sum(dga, axis=0, keepdims=True), first)
        _acc_into(ao[0].at[:, pl.ds(D_MODEL, D_MODEL)], jnp.sum(dgs, axis=0, keepdims=True), first)
    return _Epi(fn, [(att_o, D_MODEL, 0), (ssm_o, D_MODEL, 0), (gl, 2 * D_MODEL, 0)], [b_gate],
                [(D_MODEL, BF16), (D_MODEL, BF16), (2 * D_MODEL, BF16)], [2 * D_MODEL])


def _first_epi(du, dh1, x, w_pre):
    def fn(r, ri, fi, ro, ao, first):
        dx, dw = _rms_bwd_vals(ri[2][...], fi[0][...], ri[0][...] + r)
        ro[0][...] = ri[1][...] + dx
        _acc_into(ao[0], dw, first)
    return _Epi(fn, [(du, D_MODEL, 0), (dh1, D_MODEL, 0), (x, D_MODEL, 0)], [w_pre], [(D_MODEL, F32)],
                [D_MODEL])


def _gnorm_epi(y, z, w):
    gw = SSM_INNER // SSM_GROUPS

    def fn(d_all, ri, fi, ro, ao, first):
        zz = ri[1][...]
        yy = ri[0][...]
        sg = _sigmoid(zz)
        sz = zz * sg
        t = yy * sz
        dws = []
        for g in range(d_all.shape[1] // gw):
            sl = slice(g * gw, (g + 1) * gw)
            tg = t[:, sl]
            r = lax.rsqrt(jnp.mean(tg * tg, axis=-1, keepdims=True) + RMS_EPS)
            tn = tg * r
            d = d_all[:, sl]
            gg = d * fi[0][:, sl]
            dt = r * (gg - tn * jnp.mean(gg * tn, axis=-1, keepdims=True))
            ro[0][:, sl] = dt * sz[:, sl]
            ro[1][:, sl] = _b(dt * yy[:, sl] * (sg[:, sl] * (1.0 + zz[:, sl] * (1.0 - sg[:, sl]))))
            dws.append(jnp.sum(d * tn, axis=0, keepdims=True))
        _acc_into(ao[0], jnp.concatenate(dws, axis=1), first)
    return _Epi(fn, [(y, SSM_INNER, 0), (z, SSM_INNER, 0)], [w], [(SSM_INNER, F32), (SSM_INNER, BF16)],
                [SSM_INNER], tiled=True)


def _to_pat(a, d):
    if d == 1:
        return a
    S, C = a.shape
    return a.reshape(S // d, d, C).transpose(1, 0, 2).reshape(S, C)


def _from_pat(a, d):
    if d == 1:
        return a
    S, C = a.shape
    return a.reshape(d, S // d, C).transpose(1, 0, 2).reshape(S, C)


def _head_col(stat, h):
    return stat[:, h:h + 1]


def _attn_fwd(q, k, v, d):
    S = q.shape[0]
    blk = ATT_BLOCK
    nblk = S // blk
    nbs = nblk // d
    slopes = _alibi_slopes(N_ATT_HEADS)
    scale = HEAD_DIM ** -0.5

    def body(q_ref, kc_ref, kp_ref, vc_ref, vp_ref, o_ref, m_ref, l_ref):
        n = pl.program_id(0)
        has_prev = (n % nbs) != 0
        ii = lax.broadcasted_iota(jnp.int32, (blk, blk), 0)
        jj = lax.broadcasted_iota(jnp.int32, (blk, blk), 1)
        dist_c = (ii - jj).astype(F32)
        dist_p = dist_c + float(blk)
        ok_c = ii >= jj
        ok_p = jnp.logical_and(jj >= ii, has_prev)
        lane = lax.broadcasted_iota(jnp.int32, (blk, LANES), 1)
        m_all = jnp.zeros((blk, LANES), F32)
        l_all = jnp.zeros((blk, LANES), F32)
        for h in range(N_ATT_HEADS):
            sl = slice(h * HEAD_DIM, (h + 1) * HEAD_DIM)
            qh = q_ref[:, sl]
            bias = slopes[h] * float(d)
            sc = jnp.where(ok_c, _dot_nt(qh, kc_ref[:, sl]) * scale - bias * dist_c, NEG_BIG)
            sp = jnp.where(ok_p, _dot_nt(qh, kp_ref[:, sl]) * scale - bias * dist_p, NEG_BIG)
            m = jnp.maximum(jnp.max(sc, axis=-1, keepdims=True), jnp.max(sp, axis=-1, keepdims=True))
            pc = jnp.exp(sc - m)
            pp = jnp.exp(sp - m)
            l = jnp.sum(pc, axis=-1, keepdims=True) + jnp.sum(pp, axis=-1, keepdims=True)
            o_ref[:, sl] = _dot(_b(pc), vc_ref[:, sl]) + _dot(_b(pp), vp_ref[:, sl])
            m_all = jnp.where(lane == h, m, m_all)
            l_all = jnp.where(lane == h, l, l_all)
        m_ref[...] = m_all
        l_ref[...] = l_all

    cur = pl.BlockSpec((blk, ATT_WIDTH), lambda n: (n, 0))
    prev = pl.BlockSpec((blk, ATT_WIDTH), lambda n: (jnp.maximum(n - 1, 0), 0))
    stat = pl.BlockSpec((blk, LANES), lambda n: (n, 0))
    return pl.pallas_call(
        body, name=f"attn_fwd_d{d}", grid=(nblk,),
        in_specs=[cur, cur, prev, cur, prev],
        out_specs=[cur, stat, stat],
        out_shape=[jax.ShapeDtypeStruct((S, ATT_WIDTH), F32), jax.ShapeDtypeStruct((S, LANES), F32),
                   jax.ShapeDtypeStruct((S, LANES), F32)],
        compiler_params=_params(("parallel",)),
    )(q, k, k, v, v)


def _attn_combine(os, ms, ls):
    def body(o1, o2, o3, m1, m2, m3, l1, l2, l3, att_ref, lse_ref):
        mm = [m1[...], m2[...], m3[...]]
        big = jnp.maximum(jnp.maximum(mm[0], mm[1]), mm[2])
        es = [jnp.exp(m - big) for m in mm]
        den = es[0] * l1[...] + es[1] * l2[...] + es[2] * l3[...]
        lse_ref[...] = big + jnp.log(den)
        inv = 1.0 / den
        for h in range(N_ATT_HEADS):
            sl = slice(h * HEAD_DIM, (h + 1) * HEAD_DIM)
            num = (_head_col(es[0], h) * o1[:, sl] + _head_col(es[1], h) * o2[:, sl]
                   + _head_col(es[2], h) * o3[:, sl])
            att_ref[:, sl] = num * _head_col(inv, h)
    return _row_call(body, list(os) + list(ms) + list(ls), [], [(ATT_WIDTH, F32), (LANES, F32)], [], 256,
                     "attn_combine")


def _attn_delta(d_att, att):
    def body(d_ref, a_ref, dl_ref, db_ref):
        dd = d_ref[...]
        prod = dd * a_ref[...]
        lane = lax.broadcasted_iota(jnp.int32, (dd.shape[0], LANES), 1)
        acc = jnp.zeros((dd.shape[0], LANES), F32)
        for h in range(N_ATT_HEADS):
            s = jnp.sum(prod[:, h * HEAD_DIM:(h + 1) * HEAD_DIM], axis=-1, keepdims=True)
            acc = jnp.where(lane == h, s, acc)
        dl_ref[...] = acc
        db_ref[...] = _b(dd)
    return _row_call(body, [d_att, att], [], [(LANES, F32), (ATT_WIDTH, BF16)], [], 512, "attn_delta")


def _attn_bwd(q, k, v, do, lse, delta, d):
    S = q.shape[0]
    blk = ATT_BLOCK
    nblk = S // blk
    nbs = nblk // d
    slopes = _alibi_slopes(N_ATT_HEADS)
    scale = HEAD_DIM ** -0.5

    def body(qc_ref, qn_ref, k_ref, v_ref, doc_ref, don_ref, lc_ref, ln_ref, dc_ref, dn_ref,
             dq_ref, dk_ref, dv_ref, carry_ref):
        n = pl.program_id(0)
        has_next = ((n + 1) % nbs) != 0

        @pl.when(n == 0)
        def _():
            carry_ref[...] = jnp.zeros_like(carry_ref)

        ii = lax.broadcasted_iota(jnp.int32, (blk, blk), 0)
        jj = lax.broadcasted_iota(jnp.int32, (blk, blk), 1)
        dist_c = (ii - jj).astype(F32)
        dist_p = dist_c + float(blk)
        ok_c = ii >= jj
        ok_p = jnp.logical_and(jj >= ii, has_next)
        for h in range(N_ATT_HEADS):
            sl = slice(h * HEAD_DIM, (h + 1) * HEAD_DIM)
            bias = slopes[h] * float(d)
            kh = k_ref[:, sl]
            vh = v_ref[:, sl]
            qh = qc_ref[:, sl]
            doh = doc_ref[:, sl]
            s = jnp.where(ok_c, _dot_nt(qh, kh) * scale - bias * dist_c - _head_col(lc_ref[...], h), NEG_BIG)
            p = jnp.exp(s)
            ds = p * (_dot_nt(doh, vh) - _head_col(dc_ref[...], h)) * scale
            pb, dsb = _b(p), _b(ds)
            dv = _dot_tn(pb, doh)
            dk = _dot_tn(dsb, qh)
            dq_ref[:, sl] = _dot(dsb, kh) + carry_ref[:, sl]
            qh = qn_ref[:, sl]
            doh = don_ref[:, sl]
            s = jnp.where(ok_p, _dot_nt(qh, kh) * scale - bias * dist_p - _head_col(ln_ref[...], h), NEG_BIG)
            p = jnp.exp(s)
            ds = p * (_dot_nt(doh, vh) - _head_col(dn_ref[...], h)) * scale
            pb, dsb = _b(p), _b(ds)
            dv_ref[:, sl] = dv + _dot_tn(pb, doh)
            dk_ref[:, sl] = dk + _dot_tn(dsb, qh)
            carry_ref[:, sl] = _dot(dsb, kh)

    cur = pl.BlockSpec((blk, ATT_WIDTH), lambda n: (n, 0))
    nxt = pl.BlockSpec((blk, ATT_WIDTH), lambda n: (jnp.minimum(n + 1, nblk - 1), 0))
    scur = pl.BlockSpec((blk, LANES), lambda n: (n, 0))
    snxt = pl.BlockSpec((blk, LANES), lambda n: (jnp.minimum(n + 1, nblk - 1), 0))
    shp = jax.ShapeDtypeStruct((S, ATT_WIDTH), F32)
    return pl.pallas_call(
        body, name=f"attn_bwd_d{d}", grid=(nblk,),
        in_specs=[cur, nxt, cur, cur, cur, nxt, scur, snxt, scur, snxt],
        out_specs=[cur, cur, cur],
        out_shape=[shp, shp, shp],
        scratch_shapes=[pltpu.VMEM((blk, ATT_WIDTH), F32)],
        compiler_params=_params(("arbitrary",)),
    )(q, q, k, v, do, do, lse, lse, delta, delta)


def _head_pair_masks(x):
    lane = lax.broadcasted_iota(jnp.int32, x.shape, 1)
    zero = jnp.zeros_like(x)
    return jnp.where(lane < HEAD_DIM, x, zero), jnp.where(lane >= HEAD_DIM, x, zero)


ATT_QUERY_ROWS = 32


def _attn_fwd2(qkv, d, comm=None):
    S = qkv.shape[0]
    blk = ATT_BLOCK
    nblk = S // blk
    nbs = nblk // d
    slopes = _alibi_slopes(N_ATT_HEADS)
    scale = HEAD_DIM ** -0.5
    n_ci = len(comm.ins) if comm else 0
    n_co = len(comm.outs) if comm else 0

    def body(*refs):
        q_ref, kc_ref, kp_ref, vc_ref, vp_ref = refs[:5]
        o_ref, m_ref, l_ref = refs[5 + n_ci:8 + n_ci]
        n = pl.program_id(0)
        if comm:
            c_args = (refs[5:5 + n_ci], refs[8 + n_ci:8 + n_ci + n_co], refs[-2], refs[-1])

            @pl.when(n == 0)
            def _():
                comm.start(*c_args)

        has_prev = (n % nbs) != 0
        ii = lax.broadcasted_iota(jnp.int32, (blk, 2 * blk), 0)
        jj = lax.broadcasted_iota(jnp.int32, (blk, 2 * blk), 1)
        dist_i = blk + ii - jj
        dist = dist_i.astype(F32)
        ok = jnp.logical_and(jnp.logical_and(dist_i >= 0, dist_i <= blk), jnp.logical_or(jj >= blk, has_prev))
        s_scr, p_scr = refs[8 + n_ci + n_co], refs[9 + n_ci + n_co]
        lane = lax.broadcasted_iota(jnp.int32, (blk, LANES), 1)
        for pr in range(N_ATT_HEADS // 2):
            sl = slice(pr * LANES, (pr + 1) * LANES)
            kcat = jnp.concatenate([kp_ref[:, sl], kc_ref[:, sl]], axis=0)
            for h, qh in zip((2 * pr, 2 * pr + 1), _head_pair_masks(q_ref[:, sl])):
                s_scr[h] = _dot_nt(qh, kcat)
        m_all = jnp.zeros((blk, LANES), F32)
        l_all = jnp.zeros((blk, LANES), F32)
        for h in range(N_ATT_HEADS):
            s = jnp.where(ok, s_scr[h] * scale - (slopes[h] * float(d)) * dist, NEG_BIG)
            m = jnp.max(s, axis=-1, keepdims=True)
            p = jnp.exp(s - m)
            l = jnp.sum(p, axis=-1, keepdims=True)
            m_all = jnp.where(lane == h, m, m_all)
            l_all = jnp.where(lane == h, l, l_all)
            p_scr[:, h * 2 * blk:(h + 1) * 2 * blk] = _b(p)
        for pr in range(N_ATT_HEADS // 2):
            sl = slice(pr * LANES, (pr + 1) * LANES)
            vmask = jnp.concatenate(
                _head_pair_masks(jnp.concatenate([vp_ref[:, sl], vc_ref[:, sl]], axis=0)), axis=0)
            o_ref[:, sl] = _dot(p_scr[:, pr * 4 * blk:(pr + 1) * 4 * blk], vmask)
        m_ref[...] = m_all
        l_ref[...] = l_all
        if comm:
            @pl.when(n == nblk - 1)
            def _():
                comm.finish(*c_args)

    cur = lambda c: pl.BlockSpec((blk, ATT_WIDTH), lambda n: (n, c))
    prev = lambda c: pl.BlockSpec((blk, ATT_WIDTH), lambda n: (jnp.maximum(n - 1, 0), c))
    stat = pl.BlockSpec((blk, LANES), lambda n: (n, 0))
    scratch = [pltpu.VMEM((N_ATT_HEADS, blk, 2 * blk), F32), pltpu.VMEM((blk, N_ATT_HEADS * 2 * blk), BF16)]
    if comm:
        scratch += [pltpu.SemaphoreType.DMA((comm.n_sems,))] * 2
        params = pltpu.CompilerParams(dimension_semantics=("arbitrary",), vmem_limit_bytes=VMEM_LIMIT,
                                      has_side_effects=True)
    else:
        params = _params(("parallel",))
    outs = pl.pallas_call(
        body, name=f"attn_fwd_d{d}", grid=(nblk,),
        in_specs=[cur(0), cur(1), prev(1), cur(2), prev(2)] + [ANY] * n_ci,
        out_specs=[cur(0), stat, stat] + [ANY] * n_co,
        out_shape=[jax.ShapeDtypeStruct((S, ATT_WIDTH), F32), jax.ShapeDtypeStruct((S, LANES), F32),
                   jax.ShapeDtypeStruct((S, LANES), F32)] + list(comm.outs if comm else []),
        scratch_shapes=scratch,
        compiler_params=params,
    )(qkv, qkv, qkv, qkv, qkv, *(comm.ins if comm else []))
    return (outs[0], outs[1], outs[2], outs[3:]) if comm else outs


def _attn_bwd2(qkv, do, lse, delta, d, comm=None):
    S = qkv.shape[0]
    blk = ATT_BLOCK
    nblk = S // blk
    nbs = nblk // d
    slopes = _alibi_slopes(N_ATT_HEADS)
    scale = HEAD_DIM ** -0.5
    n_ci = len(comm.ins) if comm else 0
    n_co = len(comm.outs) if comm else 0

    def body(*refs):
        qc_ref, qn_ref, k_ref, v_ref, doc_ref, don_ref, lc_ref, ln_ref, dc_ref, dn_ref = refs[:10]
        dq_ref, dk_ref, dv_ref = refs[10 + n_ci:13 + n_ci]
        carry_ref = refs[13 + n_ci + n_co]
        n = pl.program_id(0)
        has_next = ((n + 1) % nbs) != 0
        if comm:
            c_args = (refs[10:10 + n_ci], refs[13 + n_ci:13 + n_ci + n_co], refs[-2], refs[-1])

        @pl.when(n == 0)
        def _():
            carry_ref[...] = jnp.zeros_like(carry_ref)
            if comm:
                comm.start(*c_args)

        rr = lax.broadcasted_iota(jnp.int32, (2 * blk, blk), 0)
        jj = lax.broadcasted_iota(jnp.int32, (2 * blk, blk), 1)
        dist_i = rr - jj
        dist = dist_i.astype(F32)
        ok = jnp.logical_or(jnp.logical_and(rr < blk, dist_i >= 0),
                            jnp.logical_and(jnp.logical_and(rr >= blk, dist_i <= blk), has_next))
        s_scr, dp_scr, p_rows, ds_rows, ds_cols = refs[14 + n_ci + n_co:19 + n_ci + n_co]
        lcat = jnp.concatenate([lc_ref[...], ln_ref[...]], axis=0)
        dcat = jnp.concatenate([dc_ref[...], dn_ref[...]], axis=0)
        rows2 = 2 * blk

        def operands(pr):
            sl = slice(pr * LANES, (pr + 1) * LANES)
            qm = _head_pair_masks(jnp.concatenate([qc_ref[:, sl], qn_ref[:, sl]], axis=0))
            dom = _head_pair_masks(jnp.concatenate([doc_ref[:, sl], don_ref[:, sl]], axis=0))
            return sl, qm, dom

        for pr in range(N_ATT_HEADS // 2):
            sl, qm, dom = operands(pr)
            for h, qh, doh in zip((2 * pr, 2 * pr + 1), qm, dom):
                s_scr[h] = _dot_nt(qh, k_ref[:, sl])
                dp_scr[h] = _dot_nt(doh, v_ref[:, sl])
        for h in range(N_ATT_HEADS):
            s = jnp.where(ok, s_scr[h] * scale - (slopes[h] * float(d)) * dist - lcat[:, h:h + 1], NEG_BIG)
            p = jnp.exp(s)
            dsb = _b(p * (dp_scr[h] - dcat[:, h:h + 1]) * scale)
            p_rows[h * rows2:(h + 1) * rows2, :] = _b(p)
            ds_rows[h * rows2:(h + 1) * rows2, :] = dsb
            ds_cols[:, h * blk:(h + 1) * blk] = dsb
        for pr in range(N_ATT_HEADS // 2):
            sl, qm, dom = operands(pr)
            pair_rows = slice(pr * 2 * rows2, (pr + 1) * 2 * rows2)
            dv_ref[:, sl] = _b(_dot_tn(p_rows[pair_rows, :], jnp.concatenate(dom, axis=0)))
            dk_ref[:, sl] = _b(_dot_tn(ds_rows[pair_rows, :], jnp.concatenate(qm, axis=0)))
            dq = _dot(ds_cols[:, pr * 2 * blk:(pr + 1) * 2 * blk],
                      jnp.concatenate(_head_pair_masks(k_ref[:, sl]), axis=0))
            dq_ref[:, sl] = _b(dq[:blk] + carry_ref[:, sl])
            carry_ref[:, sl] = dq[blk:]

        if comm:
            @pl.when(n == nblk - 1)
            def _():
                comm.finish(*c_args)

    cur = lambda c: pl.BlockSpec((blk, ATT_WIDTH), lambda n: (n, c))
    nxt = lambda c: pl.BlockSpec((blk, ATT_WIDTH), lambda n: (jnp.minimum(n + 1, nblk - 1), c))
    scur = pl.BlockSpec((blk, LANES), lambda n: (n, 0))
    snxt = pl.BlockSpec((blk, LANES), lambda n: (jnp.minimum(n + 1, nblk - 1), 0))
    shp = jax.ShapeDtypeStruct((S, ATT_WIDTH), BF16)
    scratch = [pltpu.VMEM((blk, ATT_WIDTH), F32),
               pltpu.VMEM((N_ATT_HEADS, 2 * blk, blk), F32), pltpu.VMEM((N_ATT_HEADS, 2 * blk, blk), F32),
               pltpu.VMEM((N_ATT_HEADS * 2 * blk, blk), BF16), pltpu.VMEM((N_ATT_HEADS * 2 * blk, blk), BF16),
               pltpu.VMEM((2 * blk, N_ATT_HEADS * blk), BF16)]
    if comm:
        scratch += [pltpu.SemaphoreType.DMA((comm.n_sems,))] * 2
        params = pltpu.CompilerParams(dimension_semantics=("arbitrary",), vmem_limit_bytes=VMEM_LIMIT,
                                      has_side_effects=True)
    else:
        params = _params(("arbitrary",))
    outs = pl.pallas_call(
        body, name=f"attn_bwd_d{d}", grid=(nblk,),
        in_specs=[cur(0), nxt(0), cur(1), cur(2), cur(0), nxt(0), scur, snxt, scur, snxt] + [ANY] * n_ci,
        out_specs=[cur(0), cur(0), cur(0)] + [ANY] * n_co,
        out_shape=[shp, shp, shp] + list(comm.outs if comm else []),
        scratch_shapes=scratch,
        compiler_params=params,
    )(qkv, qkv, qkv, qkv, do, do, lse, lse, delta, delta, *(comm.ins if comm else []))
    return (outs[0], outs[1], outs[2], outs[3:]) if comm else outs


LAYOUT_TILE = 512
DILATED = tuple(d for d in DILATIONS if d > 1)


def _pat_spec(d, cols, col_block=0):
    return pl.BlockSpec((d, LAYOUT_TILE // d, cols), lambda i: (0, i, col_block))


def _pat_view(a, d):
    return a.reshape(d, a.shape[0] // d, a.shape[1])


def _qkv_layouts(qkv):
    S, C = qkv.shape
    t = LAYOUT_TILE

    def body(x_ref, nat_ref, *refs):
        pat_refs, slab = refs[:-1], refs[-1]
        nat_ref[...] = _b(x_ref[...])
        _to_slabs(slab, x_ref)
        for d, p_ref in zip(DILATED, pat_refs):
            _gather_pattern(p_ref, slab, d, BF16)

    outs = pl.pallas_call(
        body, name="qkv_layouts", grid=(S // t,),
        in_specs=[pl.BlockSpec((t, C), lambda i: (i, 0))],
        out_specs=[pl.BlockSpec((t, C), lambda i: (i, 0))] + [_pat_spec(d, C) for d in DILATED],
        out_shape=[jax.ShapeDtypeStruct((S, C), BF16)]
        + [jax.ShapeDtypeStruct((d, S // d, C), BF16) for d in DILATED],
        scratch_shapes=[pltpu.VMEM((C // LANES, t, LANES), F32)],
        compiler_params=_params(("parallel",)),
    )(qkv)
    return [outs[0]] + [o.reshape(S, C) for o in outs[1:]]


def _to_slabs(slab_ref, src_ref):
    for cb in range(slab_ref.shape[0]):
        slab_ref[cb] = src_ref[:, cb * LANES:(cb + 1) * LANES].astype(F32)


def _gather_pattern(dst_ref, slab_ref, d, dtype):
    t = slab_ref.shape[1]
    for cb in range(slab_ref.shape[0]):
        one = slab_ref.at[cb]
        for r in range(d):
            dst_ref[r, :, cb * LANES:(cb + 1) * LANES] = one[pl.ds(r, t // d, stride=d), :].astype(dtype)


def _scatter_pattern(slab_ref, src_ref, d, add=False):
    t = slab_ref.shape[1]
    for cb in range(slab_ref.shape[0]):
        one = slab_ref.at[cb]
        for r in range(d):
            idx = pl.ds(r, t // d, stride=d)
            val = src_ref[r, :, cb * LANES:(cb + 1) * LANES]
            if add:
                val = val + one[idx, :]
            one[idx, :] = val


def _attn_combine2(os, ms, ls):
    S = os[0].shape[0]
    t = LAYOUT_TILE

    def body(o1, o2, o3, m1, m2, m3, l1, l2, l3, att_ref, lse_ref, so2, so3, sm2, sm3, sl2, sl3):
        for d, src, dst in ((DILATED[0], o2, so2), (DILATED[1], o3, so3), (DILATED[0], m2, sm2),
                            (DILATED[1], m3, sm3), (DILATED[0], l2, sl2), (DILATED[1], l3, sl3)):
            _scatter_pattern(dst, src, d)
        mm = [m1[...], sm2[0], sm3[0]]
        big = jnp.maximum(jnp.maximum(mm[0], mm[1]), mm[2])
        es = [jnp.exp(m - big) for m in mm]
        den = es[0] * l1[...] + es[1] * sl2[0] + es[2] * sl3[0]
        lse_ref[...] = big + jnp.log(den)
        inv = 1.0 / den
        for h in range(N_ATT_HEADS):
            sl = slice(h * HEAD_DIM, (h + 1) * HEAD_DIM)
            cb, hl = divmod(h, 2)
            sll = slice(hl * HEAD_DIM, (hl + 1) * HEAD_DIM)
            num = (_head_col(es[0], h) * o1[:, sl] + _head_col(es[1], h) * so2[cb, :, sll]
                   + _head_col(es[2], h) * so3[cb, :, sll])
            att_ref[:, sl] = num * _head_col(inv, h)

    def specs(c):
        return [pl.BlockSpec((t, c), lambda i: (i, 0))] + [_pat_spec(d, c) for d in DILATED]

    args = [os[0]] + [_pat_view(o, d) for o, d in zip(os[1:], DILATED)]
    args += [ms[0]] + [_pat_view(m, d) for m, d in zip(ms[1:], DILATED)]
    args += [ls[0]] + [_pat_view(l, d) for l, d in zip(ls[1:], DILATED)]
    return pl.pallas_call(
        body, name="attn_combine", grid=(S // t,),
        in_specs=specs(ATT_WIDTH) + specs(LANES) + specs(LANES),
        out_specs=[pl.BlockSpec((t, ATT_WIDTH), lambda i: (i, 0)), pl.BlockSpec((t, LANES), lambda i: (i, 0))],
        out_shape=[jax.ShapeDtypeStruct((S, ATT_WIDTH), F32), jax.ShapeDtypeStruct((S, LANES), F32)],
        scratch_shapes=[pltpu.VMEM((ATT_WIDTH // LANES, t, LANES), F32)] * 2
        + [pltpu.VMEM((1, t, LANES), F32)] * 4,
        compiler_params=_params(("parallel",)),
    )(*args)


def _attn_delta2(d_att, att, lse):
    S = d_att.shape[0]
    t = LAYOUT_TILE

    def body(d_ref, a_ref, l_ref, *refs):
        out_refs, d_slab, l_slab, dl_slab = refs[:-3], refs[-3], refs[-2], refs[-1]
        dd = d_ref[...]
        prod = dd * a_ref[...]
        lane = lax.broadcasted_iota(jnp.int32, (t, LANES), 1)
        acc = jnp.zeros((t, LANES), F32)
        for h in range(N_ATT_HEADS):
            s = jnp.sum(prod[:, h * HEAD_DIM:(h + 1) * HEAD_DIM], axis=-1, keepdims=True)
            acc = jnp.where(lane == h, s, acc)
        out_refs[0][...] = _b(dd)
        out_refs[1][...] = acc
        _to_slabs(d_slab, d_ref)
        l_slab[0] = l_ref[...]
        dl_slab[0] = acc
        for k, d in enumerate(DILATED):
            db_ref, ls_ref, dl_ref = out_refs[2 + 3 * k:5 + 3 * k]
            _gather_pattern(db_ref, d_slab, d, BF16)
            _gather_pattern(ls_ref, l_slab, d, F32)
            _gather_pattern(dl_ref, dl_slab, d, F32)

    nat = lambda c: pl.BlockSpec((t, c), lambda i: (i, 0))
    out_specs = [nat(ATT_WIDTH), nat(LANES)]
    out_shape = [jax.ShapeDtypeStruct((S, ATT_WIDTH), BF16), jax.ShapeDtypeStruct((S, LANES), F32)]
    for d in DILATED:
        out_specs += [_pat_spec(d, ATT_WIDTH), _pat_spec(d, LANES), _pat_spec(d, LANES)]
        out_shape += [jax.ShapeDtypeStruct((d, S // d, ATT_WIDTH), BF16),
                      jax.ShapeDtypeStruct((d, S // d, LANES), F32),
                      jax.ShapeDtypeStruct((d, S // d, LANES), F32)]
    outs = pl.pallas_call(
        body, name="attn_delta", grid=(S // t,),
        in_specs=[nat(ATT_WIDTH), nat(ATT_WIDTH), nat(LANES)],
        out_specs=out_specs, out_shape=out_shape,
        scratch_shapes=[pltpu.VMEM((ATT_WIDTH // LANES, t, LANES), F32), pltpu.VMEM((1, t, LANES), F32),
                        pltpu.VMEM((1, t, LANES), F32)],
        compiler_params=_params(("parallel",)),
    )(d_att, att, lse)
    res = [(outs[0], lse, outs[1])]
    for k in range(len(DILATED)):
        db, ls, dl = outs[2 + 3 * k:5 + 3 * k]
        res.append((db.reshape(S, ATT_WIDTH), ls.reshape(S, LANES), dl.reshape(S, LANES)))
    return res


def _sum_qkv2(dqs, dks, dvs):
    S = dqs[0].shape[0]
    t = LAYOUT_TILE

    def body(*refs):
        o_ref, scr = refs[-2], refs[-1]
        for part in range(3):
            nat_ref, p_refs = refs[3 * part], refs[3 * part + 1:3 * part + 3]
            _to_slabs(scr, nat_ref)
            for d, p_ref in zip(DILATED, p_refs):
                _scatter_pattern(scr, p_ref, d, add=True)
            for cb in range(ATT_WIDTH // LANES):
                o_ref[:, part * ATT_WIDTH + cb * LANES:part * ATT_WIDTH + (cb + 1) * LANES] = _b(scr[cb])

    in_specs, args = [], []
    for group in (dqs, dks, dvs):
        in_specs += [pl.BlockSpec((t, ATT_WIDTH), lambda i: (i, 0))] + [_pat_spec(d, ATT_WIDTH) for d in DILATED]
        args += [group[0]] + [_pat_view(a, d) for a, d in zip(group[1:], DILATED)]
    return pl.pallas_call(
        body, name="sum_dqkv", grid=(S // t,),
        in_specs=in_specs,
        out_specs=pl.BlockSpec((t, 3 * ATT_WIDTH), lambda i: (i, 0)),
        out_shape=jax.ShapeDtypeStruct((S, 3 * ATT_WIDTH), BF16),
        scratch_shapes=[pltpu.VMEM((ATT_WIDTH // LANES, t, LANES), F32)],
        compiler_params=_params(("parallel",)),
    )(*args)


def _sum_qkv(dqs, dks, dvs):
    def body(q1, q2, q3, k1, k2, k3, v1, v2, v3, o_ref):
        o_ref[:, 0:ATT_WIDTH] = _b(q1[...] + q2[...] + q3[...])
        o_ref[:, ATT_WIDTH:2 * ATT_WIDTH] = _b(k1[...] + k2[...] + k3[...])
        o_ref[:, 2 * ATT_WIDTH:] = _b(v1[...] + v2[...] + v3[...])
    return _row_call(body, list(dqs) + list(dks) + list(dvs), [], [(3 * ATT_WIDTH, BF16)], [], 256,
                     "sum_dqkv")[0]


CONV_COLS = 1024
CONV_ROWS = 512
HALO = 8


def _conv_fwd(xbc, conv_w, conv_b):
    S, C = xbc.shape
    bs, bc = CONV_ROWS, CONV_COLS
    nr = S // bs

    def body(x_ref, halo_ref, w_ref, b_ref, o_ref, xs_ref):
        r = pl.program_id(1)
        xs_ref[pl.ds(HALO, bs), :] = x_ref[...]
        xs_ref[pl.ds(0, HALO), :] = jnp.where(r > 0, halo_ref[...], 0.0)
        pre = b_ref[...] + w_ref[3:4, :] * x_ref[...]
        for j in range(SSM_CONV - 1):
            pre = pre + w_ref[j:j + 1, :] * xs_ref[pl.ds(HALO - 3 + j, bs), :]
        o_ref[...] = pre * _sigmoid(pre)

    return pl.pallas_call(
        body, name="conv_fwd", grid=(C // bc, nr),
        in_specs=[pl.BlockSpec((bs, bc), lambda c, r: (r, c)),
                  pl.BlockSpec((HALO, bc), lambda c, r: (jnp.maximum(r * (bs // HALO) - 1, 0), c)),
                  pl.BlockSpec((SSM_CONV, bc), lambda c, r: (0, c)),
                  pl.BlockSpec((1, bc), lambda c, r: (0, c))],
        out_specs=pl.BlockSpec((bs, bc), lambda c, r: (r, c)),
        out_shape=jax.ShapeDtypeStruct((S, C), F32),
        scratch_shapes=[pltpu.VMEM((bs + HALO, bc), F32)],
        compiler_params=_params(("parallel", "arbitrary")),
    )(xbc, xbc, conv_w, conv_b)


def _conv_bwd(xbc, dact, conv_w, conv_b, col0):
    S, C = xbc.shape
    Cp = dact.shape[1]
    bs, bc = CONV_ROWS, min(CONV_COLS, Cp)
    nr = S // bs
    cb0 = col0 // bc
    last_halo = S // HALO - 1

    def body(x_ref, xp_ref, xn_ref, d_ref, dn_ref, w_ref, b_ref, dx_ref, dw_ref, db_ref,
             xs_ref, dp_ref):
        r = pl.program_id(1)
        xs_ref[pl.ds(0, HALO), :] = jnp.where(r > 0, xp_ref[...], 0.0)
        xs_ref[pl.ds(HALO, bs), :] = x_ref[...]
        xs_ref[pl.ds(HALO + bs, HALO), :] = xn_ref[...]
        ext = bs + HALO
        pre = b_ref[...] + jnp.zeros((ext, bc), F32)
        for j in range(SSM_CONV):
            pre = pre + w_ref[j:j + 1, :] * xs_ref[pl.ds(HALO - 3 + j, ext), :]
        sg = _sigmoid(pre)
        dsilu = sg * (1.0 + pre * (1.0 - sg))
        dp_ref[pl.ds(0, bs), :] = d_ref[...] * dsilu[:bs]
        dp_ref[pl.ds(bs, HALO), :] = jnp.where(r < nr - 1, dn_ref[...], 0.0) * dsilu[bs:]
        dx = jnp.zeros((bs, bc), F32)
        for j in range(SSM_CONV):
            dx = dx + w_ref[j:j + 1, :] * dp_ref[pl.ds(3 - j, bs), :]
        dx_ref[...] = _b(dx)
        dpre = dp_ref[pl.ds(0, bs), :]
        for j in range(SSM_CONV):
            part = jnp.sum(dpre * xs_ref[pl.ds(HALO - 3 + j, bs), :], axis=0, keepdims=True)

            @pl.when(r == 0)
            def _():
                dw_ref[j:j + 1, :] = part

            @pl.when(r > 0)
            def _():
                dw_ref[j:j + 1, :] += part
        part = jnp.sum(dpre, axis=0, keepdims=True)

        @pl.when(r == 0)
        def _():
            db_ref[...] = part

        @pl.when(r > 0)
        def _():
            db_ref[...] += part

    hb = bs // HALO
    return pl.pallas_call(
        body, name=f"conv_bwd_{col0}", grid=(Cp // bc, nr),
        in_specs=[pl.BlockSpec((bs, bc), lambda c, r: (r, cb0 + c)),
                  pl.BlockSpec((HALO, bc), lambda c, r: (jnp.maximum(r * hb - 1, 0), cb0 + c)),
                  pl.BlockSpec((HALO, bc), lambda c, r: (jnp.minimum((r + 1) * hb, last_halo), cb0 + c)),
                  pl.BlockSpec((bs, bc), lambda c, r: (r, c)),
                  pl.BlockSpec((HALO, bc), lambda c, r: (jnp.minimum((r + 1) * hb, last_halo), c)),
                  pl.BlockSpec((SSM_CONV, bc), lambda c, r: (0, cb0 + c)),
                  pl.BlockSpec((1, bc), lambda c, r: (0, cb0 + c))],
        out_specs=[pl.BlockSpec((bs, bc), lambda c, r: (r, c)),
                   pl.BlockSpec((SSM_CONV, bc), lambda c, r: (0, c)),
                   pl.BlockSpec((1, bc), lambda c, r: (0, c))],
        out_shape=[jax.ShapeDtypeStruct((S, Cp), BF16), jax.ShapeDtypeStruct((SSM_CONV, Cp), F32),
                   jax.ShapeDtypeStruct((1, Cp), F32)],
        scratch_shapes=[pltpu.VMEM((bs + 2 * HALO, bc), F32), pltpu.VMEM((bs + HALO, bc), F32)],
        compiler_params=_params(("parallel", "arbitrary")),
    )(xbc, xbc, xbc, dact, dact, conv_w, conv_b)


def _shift_down(x, k, top_src):
    r8 = lax.broadcasted_iota(jnp.int32, (HALO, x.shape[1]), 0)
    rolled = pltpu.roll(x, k, 0)
    top = jnp.where(r8 < k, pltpu.roll(top_src, k, 0), rolled[0:HALO])
    if x.shape[0] == HALO:
        return top
    return jnp.concatenate([top, rolled[HALO:]], axis=0)


def _shift_up(x, k, bottom_src):
    n = x.shape[0]
    r8 = lax.broadcasted_iota(jnp.int32, (HALO, x.shape[1]), 0)
    rolled = pltpu.roll(x, n - k, 0)
    bottom = jnp.where(r8 >= HALO - k, pltpu.roll(bottom_src, HALO - k, 0), rolled[n - HALO:n])
    return jnp.concatenate([rolled[:n - HALO], bottom], axis=0)


def _conv_pre(x, top_src, w_ref, b_ref):
    shifted = [x] + [_shift_down(x, k, top_src) for k in range(1, SSM_CONV)]
    pre = b_ref[...] + w_ref[SSM_CONV - 1:SSM_CONV, :] * x
    for k in range(1, SSM_CONV):
        pre = pre + w_ref[SSM_CONV - 1 - k:SSM_CONV - k, :] * shifted[k]
    return pre, shifted


def _conv_fwd2(xbc, conv_w, conv_b):
    S, C = xbc.shape
    bs, bc = CONV_ROWS, CONV_COLS
    nr = S // bs

    def body(x_ref, halo_ref, w_ref, b_ref, o_ref, pre_ref):
        r = pl.program_id(1)
        halo = jnp.where(r > 0, halo_ref[...], 0.0)
        pre, _ = _conv_pre(x_ref[...], halo, w_ref, b_ref)
        pre_ref[...] = _b(pre)
        o_ref[...] = pre * _sigmoid(pre)

    tile = pl.BlockSpec((bs, bc), lambda c, r: (r, c))
    return pl.pallas_call(
        body, name="conv_fwd", grid=(C // bc, nr),
        in_specs=[tile,
                  pl.BlockSpec((HALO, bc), lambda c, r: (jnp.maximum(r * (bs // HALO) - 1, 0), c)),
                  pl.BlockSpec((SSM_CONV, bc), lambda c, r: (0, c)),
                  pl.BlockSpec((1, bc), lambda c, r: (0, c))],
        out_specs=[tile, tile],
        out_shape=[jax.ShapeDtypeStruct((S, C), F32), jax.ShapeDtypeStruct((S, C), BF16)],
        compiler_params=_params(("parallel", "arbitrary")),
    )(xbc, xbc, conv_w, conv_b)


def _conv_bwd2(xbc, pre_all, dact, conv_w):
    S, C = xbc.shape
    bs, bc = CONV_ROWS, CONV_COLS
    nr = S // bs
    hb = bs // HALO
    last_halo = S // HALO - 1

    def dsilu(pre):
        sg = _sigmoid(pre)
        return sg * (1.0 + pre * (1.0 - sg))

    def body(x_ref, p_ref, pn_ref, d_ref, dn_ref, w_ref, dx_ref, dw_ref, db_ref):
        r = pl.program_id(1)
        x = x_ref[...]
        dpre = d_ref[...] * dsilu(p_ref[...].astype(F32))
        dpre_n = jnp.where(r < nr - 1, dn_ref[...], 0.0) * dsilu(pn_ref[...].astype(F32)[0:HALO])
        ups = [dpre] + [_shift_up(dpre, k, dpre_n) for k in range(1, SSM_CONV)]
        dx = w_ref[SSM_CONV - 1:SSM_CONV, :] * dpre
        for k in range(1, SSM_CONV):
            dx = dx + w_ref[SSM_CONV - 1 - k:SSM_CONV - k, :] * ups[k]
        dx_ref[...] = _b(dx)
        parts = [jnp.sum(x * ups[SSM_CONV - 1 - j], axis=0, keepdims=True) for j in range(SSM_CONV)]
        dbp = jnp.sum(dpre, axis=0, keepdims=True)

        @pl.when(r == 0)
        def _():
            for j in range(SSM_CONV):
                dw_ref[j:j + 1, :] = parts[j]
            db_ref[...] = dbp

        @pl.when(r > 0)
        def _():
            for j in range(SSM_CONV):
                dw_ref[j:j + 1, :] += parts[j]
            db_ref[...] += dbp

    tile = pl.BlockSpec((bs, bc), lambda c, r: (r, c))
    nxt = pl.BlockSpec((HALO, bc), lambda c, r: (jnp.minimum((r + 1) * hb, last_halo), c))
    nxt16 = pl.BlockSpec((BF16_ROWS, bc), lambda c, r: (
        jnp.minimum((r + 1) * (bs // BF16_ROWS), S // BF16_ROWS - 1), c))
    return pl.pallas_call(
        body, name="conv_bwd", grid=(C // bc, nr),
        in_specs=[tile, tile, nxt16, tile, nxt, pl.BlockSpec((SSM_CONV, bc), lambda c, r: (0, c))],
        out_specs=[tile,
                   pl.BlockSpec((SSM_CONV, bc), lambda c, r: (0, c)),
                   pl.BlockSpec((1, bc), lambda c, r: (0, c))],
        out_shape=[jax.ShapeDtypeStruct((S, C), BF16), jax.ShapeDtypeStruct((SSM_CONV, C), F32),
                   jax.ShapeDtypeStruct((1, C), F32)],
        compiler_params=_params(("parallel", "arbitrary")),
    )(xbc, pre_all, pre_all, dact, dact, conv_w)


def _softplus(x):
    return jnp.maximum(x, 0.0) + jnp.log(1.0 + jnp.exp(-jnp.abs(x)))


def _ssd_common(dtr_ref, bias_ref, a_ref, g):
    ch = SSM_CHUNK
    x = dtr_ref[...] + bias_ref[...]
    dt_all = _softplus(x)
    r = lax.broadcasted_iota(jnp.int32, (LANES, LANES), 0)
    c = lax.broadcasted_iota(jnp.int32, (LANES, LANES), 1)
    sel = jnp.where(jnp.logical_and(r == HEADS_PER_GROUP * g + c, c < HEADS_PER_GROUP), 1.0, 0.0)
    dt4 = _dot_hi(dt_all, sel)
    la4 = _dot_hi(dt_all * a_ref[...], sel)
    ii = lax.broadcasted_iota(jnp.int32, (ch, ch), 0)
    jj = lax.broadcasted_iota(jnp.int32, (ch, ch), 1)
    tril = jnp.where(ii >= jj, 1.0, 0.0)
    acs = _dot_hi(tril, la4)
    return x, sel, dt4, acs, acs.T, ii >= jj


def _row8(v):
    return jnp.broadcast_to(v, (8, v.shape[1]))


def _ssd_fwd(xact, dt_raw, dt_bias, a_neg, d_skip):
    S = xact.shape[0]
    ch = SSM_CHUNK
    nch = S // ch
    hg = HEADS_PER_GROUP
    gw = hg * SSM_HEAD_DIM
    b_off = SSM_INNER // SSM_STATE
    c_off = b_off + SSM_GROUPS

    def body(x_ref, b_ref, c_ref, dtr_ref, bias_ref, a_ref, dsk_ref, y_ref, hs_ref, h_ref):
        c = pl.program_id(0)
        g = pl.program_id(1)

        @pl.when(jnp.logical_and(c == 0, g == 0))
        def _():
            h_ref[...] = jnp.zeros_like(h_ref)

        _, sel, dt4, acs, acs_t, low = _ssd_common(dtr_ref, bias_ref, a_ref, g)
        dsk4 = _dot_hi(_row8(dsk_ref[...]), sel)
        bb = _b(b_ref[...])
        cc = _b(c_ref[...])
        cb = _dot_nt(cc, bb)
        for j in range(hg):
            sl = slice(j * SSM_HEAD_DIM, (j + 1) * SSM_HEAD_DIM)
            acol = acs[:, j:j + 1]
            arow = acs_t[j:j + 1, :]
            alast = acs[ch - 1:ch, j:j + 1]
            decay = jnp.exp(jnp.where(low, acol - arow, -jnp.inf))
            xh = x_ref[:, sl]
            xd = xh * dt4[:, j:j + 1]
            hj = h_ref[hg * g + j]
            y = _dot(_b(cb * decay), _b(xd))
            y = y + _dot_nt(cc, _b(hj)) * jnp.exp(acol)
            y_ref[:, sl] = y + dsk4[0:1, j:j + 1] * xh
            hs_ref[0, j] = hj
            st = _dot_tn(_b(xd * jnp.exp(alast - acol)), bb)
            h_ref[hg * g + j] = hj * jnp.exp(alast) + st

    small = pl.BlockSpec((1, LANES), lambda c, g: (0, 0))
    return pl.pallas_call(
        body, name="ssd_fwd", grid=(nch, SSM_GROUPS),
        in_specs=[pl.BlockSpec((ch, gw), lambda c, g: (c, g)),
                  pl.BlockSpec((ch, SSM_STATE), lambda c, g: (c, b_off + g)),
                  pl.BlockSpec((ch, SSM_STATE), lambda c, g: (c, c_off + g)),
                  pl.BlockSpec((ch, LANES), lambda c, g: (c, 0)),
                  small, small, small],
        out_specs=[pl.BlockSpec((ch, gw), lambda c, g: (c, g)),
                   pl.BlockSpec((1, hg, SSM_HEAD_DIM, SSM_STATE), lambda c, g: (c, g, 0, 0))],
        out_shape=[jax.ShapeDtypeStruct((S, SSM_INNER), F32),
                   jax.ShapeDtypeStruct((nch, SSM_HEADS, SSM_HEAD_DIM, SSM_STATE), F32)],
        scratch_shapes=[pltpu.VMEM((SSM_HEADS, SSM_HEAD_DIM, SSM_STATE), F32)],
        compiler_params=_params(("arbitrary", "arbitrary")),
    )(xact, xact, xact, dt_raw, dt_bias, a_neg, d_skip)


def _ssd_bwd(xact, dt_raw, dt_bias, a_neg, d_skip, hs, dy):
    S = xact.shape[0]
    ch = SSM_CHUNK
    nch = S // ch
    hg = HEADS_PER_GROUP
    gw = hg * SSM_HEAD_DIM
    b_off = SSM_INNER // SSM_STATE
    c_off = b_off + SSM_GROUPS

    def body(x_ref, b_ref, c_ref, dtr_ref, bias_ref, a_ref, dsk_ref, hs_ref, dy_ref,
             dx_ref, db_ref, dc_ref, ddt_ref, st_ref, dh_ref, ddt_acc):
        step = pl.program_id(0)
        g = pl.program_id(1)

        @pl.when(jnp.logical_and(step == 0, g == 0))
        def _():
            dh_ref[...] = jnp.zeros_like(dh_ref)
            st_ref[...] = jnp.zeros_like(st_ref)

        @pl.when(g == 0)
        def _():
            ddt_acc[...] = jnp.zeros_like(ddt_acc)

        xraw, sel, dt4, acs, acs_t, low = _ssd_common(dtr_ref, bias_ref, a_ref, g)
        a4 = _dot_hi(_row8(a_ref[...]), sel)[0:1, :]
        dsk4 = _dot_hi(_row8(dsk_ref[...]), sel)
        bf = b_ref[...]
        cf = c_ref[...]
        bb = _b(bf)
        cc = _b(cf)
        cb = _dot_nt(cc, bb)
        lane = lax.broadcasted_iota(jnp.int32, (ch, LANES), 1)
        rowi = lax.broadcasted_iota(jnp.int32, (ch, 1), 0)
        ones = jnp.ones((ch, LANES), F32)
        dcb = jnp.zeros((ch, ch), F32)
        dc_acc = jnp.zeros((ch, SSM_STATE), F32)
        db_acc = jnp.zeros((ch, SSM_STATE), F32)
        dacs4 = jnp.zeros((ch, LANES), F32)
        ddt4 = jnp.zeros((ch, LANES), F32)
        dd4 = jnp.zeros((1, LANES), F32)
        lane1 = lax.broadcasted_iota(jnp.int32, (1, LANES), 1)
        for j in range(hg):
            sl = slice(j * SSM_HEAD_DIM, (j + 1) * SSM_HEAD_DIM)
            acol = acs[:, j:j + 1]
            arow = acs_t[j:j + 1, :]
            alast = acs[ch - 1:ch, j:j + 1]
            decay = jnp.exp(jnp.where(low, acol - arow, -jnp.inf))
            ea = jnp.exp(acol)
            dsd = jnp.exp(alast - acol)
            cd = jnp.exp(alast)
            dtc = dt4[:, j:j + 1]
            xh = x_ref[:, sl]
            xd = xh * dtc
            xdb = _b(xd)
            hj = hs_ref[0, j]
            hjb = _b(hj)
            dhn = dh_ref[hg * g + j]
            dyj = dy_ref[:, sl]
            dyb = _b(dyj)
            lm = cb * decay
            dxh = dsk4[0:1, j:j + 1] * dyj
            dd4 = jnp.where(lane1 == j, jnp.sum(jnp.sum(dyj * xh, axis=1, keepdims=True), axis=0,
                                                keepdims=True), dd4)
            dlm = _dot_nt(dyb, xdb)
            dxd = _dot_tn(_b(lm), dyb)
            gm = dlm * lm
            dcb = dcb + dlm * decay
            dac = jnp.sum(gm, axis=1, keepdims=True) - _dot_tn_hi(gm, ones)[:, 0:1]
            zz = _dot_nt(cc, hjb)
            dzb = _b(dyj * ea)
            dac = dac + jnp.sum(dyj * zz, axis=1, keepdims=True) * ea
            dc_acc = dc_acc + _dot(dzb, hjb)
            dh_in = _dot_tn(dzb, cc)
            dsb = _b(dhn)
            ww = _dot_nt(bb, dsb)
            dxd = dxd + ww * dsd
            dds = jnp.sum(ww * xd, axis=1, keepdims=True) * dsd
            db_acc = db_acc + _dot(_b(xd * dsd), dsb)
            dac = dac - dds
            dal = (jnp.sum(dds, axis=0, keepdims=True)
                   + jnp.sum(jnp.sum(dhn * hj, axis=1, keepdims=True), axis=0, keepdims=True) * cd)
            dh_ref[hg * g + j] = dh_in + dhn * cd
            dac = dac + jnp.where(rowi == ch - 1, dal, 0.0)
            dacs4 = jnp.where(lane == j, dac, dacs4)
            dx_ref[:, sl] = dxh + dxd * dtc
            ddt4 = jnp.where(lane == j, jnp.sum(dxd * xh, axis=1, keepdims=True), ddt4)
        dcbb = _b(dcb)
        dc_ref[...] = dc_acc + _dot(dcbb, bb)
        db_ref[...] = db_acc + _dot_tn(dcbb, cc)
        ii = lax.broadcasted_iota(jnp.int32, (ch, ch), 0)
        jj = lax.broadcasted_iota(jnp.int32, (ch, ch), 1)
        triu = jnp.where(ii <= jj, 1.0, 0.0)
        dla4 = _dot_hi(triu, dacs4)
        ddt4 = ddt4 + dla4 * a4
        da4 = jnp.sum(dla4 * dt4, axis=0, keepdims=True) * a4
        sel_t = sel.T
        ddt_raw = _dot_hi(ddt4, sel_t) * _sigmoid(xraw)
        ddt_acc[...] += ddt_raw
        st_ref[0:1, :] += _dot_hi(_row8(da4), sel_t)[0:1, :]
        st_ref[1:2, :] += _dot_hi(_row8(dd4), sel_t)[0:1, :]
        st_ref[2:3, :] += jnp.sum(ddt_raw, axis=0, keepdims=True)

        @pl.when(g == SSM_GROUPS - 1)
        def _():
            ddt_ref[...] = _b(ddt_acc[...])

    small = pl.BlockSpec((1, LANES), lambda s, g: (0, 0))
    rc = lambda s: nch - 1 - s
    return pl.pallas_call(
        body, name="ssd_bwd", grid=(nch, SSM_GROUPS),
        in_specs=[pl.BlockSpec((ch, gw), lambda s, g: (rc(s), g)),
                  pl.BlockSpec((ch, SSM_STATE), lambda s, g: (rc(s), b_off + g)),
                  pl.BlockSpec((ch, SSM_STATE), lambda s, g: (rc(s), c_off + g)),
                  pl.BlockSpec((ch, LANES), lambda s, g: (rc(s), 0)),
                  small, small, small,
                  pl.BlockSpec((1, hg, SSM_HEAD_DIM, SSM_STATE), lambda s, g: (rc(s), g, 0, 0)),
                  pl.BlockSpec((ch, gw), lambda s, g: (rc(s), g))],
        out_specs=[pl.BlockSpec((ch, gw), lambda s, g: (rc(s), g)),
                   pl.BlockSpec((ch, SSM_STATE), lambda s, g: (rc(s), g)),
                   pl.BlockSpec((ch, SSM_STATE), lambda s, g: (rc(s), g)),
                   pl.BlockSpec((ch, LANES), lambda s, g: (rc(s), 0)),
                   pl.BlockSpec((8, LANES), lambda s, g: (0, 0))],
        out_shape=[jax.ShapeDtypeStruct((S, SSM_INNER), F32),
                   jax.ShapeDtypeStruct((S, SSM_GROUPS * SSM_STATE), F32),
                   jax.ShapeDtypeStruct((S, SSM_GROUPS * SSM_STATE), F32),
                   jax.ShapeDtypeStruct((S, LANES), BF16),
                   jax.ShapeDtypeStruct((8, LANES), F32)],
        scratch_shapes=[pltpu.VMEM((SSM_HEADS, SSM_HEAD_DIM, SSM_STATE), F32),
                        pltpu.VMEM((ch, LANES), F32)],
        compiler_params=_params(("arbitrary", "arbitrary")),
    )(xact, xact, xact, dt_raw, dt_bias, a_neg, d_skip, hs, dy)


GROUP_W = HEADS_PER_GROUP * SSM_HEAD_DIM
B_COL0 = SSM_INNER
C_COL0 = SSM_INNER + SSM_GROUPS * SSM_STATE


def _ssd_prep(dt_raw, dt_bias, a_neg):
    S = dt_raw.shape[0]
    ch = SSM_CHUNK
    nch = S // ch

    def body(dtr_ref, bias_ref, a_ref, dt_ref, acs_ref, acst_ref, sig_ref):
        x = dtr_ref[...] + bias_ref[...]
        lane = lax.broadcasted_iota(jnp.int32, (ch, LANES), 1)
        dt = jnp.where(lane < SSM_HEADS, _softplus(x), 0.0)
        ii = lax.broadcasted_iota(jnp.int32, (ch, ch), 0)
        jj = lax.broadcasted_iota(jnp.int32, (ch, ch), 1)
        acs = _dot_hi(jnp.where(ii >= jj, 1.0, 0.0), dt * a_ref[...])
        dt_ref[...] = dt
        acs_ref[...] = acs
        acst_ref[0] = acs.T[0:SSM_HEADS, :]
        sig_ref[...] = _sigmoid(x)

    blk = pl.BlockSpec((ch, LANES), lambda c: (c, 0))
    small = pl.BlockSpec((1, LANES), lambda c: (0, 0))
    shp = jax.ShapeDtypeStruct((S, LANES), F32)
    return pl.pallas_call(
        body, name="ssd_prep", grid=(nch,),
        in_specs=[blk, small, small],
        out_specs=[blk, blk, pl.BlockSpec((1, SSM_HEADS, ch), lambda c: (c, 0, 0)), blk],
        out_shape=[shp, shp, jax.ShapeDtypeStruct((nch, SSM_HEADS, ch), F32), shp],
        compiler_params=_params(("parallel",)),
    )(dt_raw, dt_bias, a_neg)


def _expand_heads(arr, g, rows):
    lane = lax.broadcasted_iota(jnp.int32, (rows, GROUP_W), 1) // SSM_HEAD_DIM
    h0 = HEADS_PER_GROUP * g
    out = jnp.broadcast_to(arr[:, h0:h0 + 1], (rows, GROUP_W))
    for j in range(1, HEADS_PER_GROUP):
        out = jnp.where(lane == j, arr[:, h0 + j:h0 + j + 1], out)
    return out


def _seg_matrix(k, lanes_per_head, h0):
    r = lax.broadcasted_iota(jnp.int32, (k, LANES), 0)
    c = lax.broadcasted_iota(jnp.int32, (k, LANES), 1)
    return jnp.where(c == h0 + r // lanes_per_head, 1.0, 0.0).astype(BF16)


def _seg_dot(t, e):
    hi = _b(t)
    lo = _b(t - hi.astype(F32))
    return _dot(hi, e) + _dot(lo, e)


def _head_sums(t, e, rows):
    if rows >= 8:
        return _seg_dot(t, e)
    return _seg_dot(jnp.broadcast_to(t, (8, t.shape[1])), e)[0:rows]


def _pair_masks(x):
    lane = lax.broadcasted_iota(jnp.int32, x.shape, 1)
    zero = jnp.zeros_like(x)
    return jnp.where(lane < SSM_HEAD_DIM, x, zero), jnp.where(lane >= SSM_HEAD_DIM, x, zero)


def _ssd_fwd2(xact, dt, acs, acst, dsk_e):
    S = xact.shape[0]
    ch = SSM_CHUNK
    nch = S // ch

    def body(x_ref, dt_ref, acs_ref, acst_ref, dsk_ref, y_ref, hs_ref, h_ref):
        c = pl.program_id(0)

        @pl.when(c == 0)
        def _():
            h_ref[...] = jnp.zeros_like(h_ref)

        dt_all = dt_ref[...]
        acs_all = acs_ref[...]
        acst_all = acst_ref[0]
        alast = acs_all[ch - 1:ch, :]
        eacs = jnp.exp(acs_all)
        wd_all = dt_all * jnp.exp(alast - acs_all)
        dtt = dt_all.T
        cd_all = jnp.exp(alast)
        ii = lax.broadcasted_iota(jnp.int32, (ch, ch), 0)
        jj = lax.broadcasted_iota(jnp.int32, (ch, ch), 1)
        low = ii >= jj
        for g in range(SSM_GROUPS):
            xs = x_ref[:, g * GROUP_W:(g + 1) * GROUP_W]
            bb = _b(x_ref[:, B_COL0 + g * SSM_STATE:B_COL0 + (g + 1) * SSM_STATE])
            cc = _b(x_ref[:, C_COL0 + g * SSM_STATE:C_COL0 + (g + 1) * SSM_STATE])
            cb = _dot_nt(cc, bb)
            xsb = _b(xs)
            ht = h_ref[g]
            rest = (_dot(cc, _b(ht)) * _expand_heads(eacs, g, ch)
                    + dsk_ref[:, g * GROUP_W:(g + 1) * GROUP_W] * xs)
            for p in range(HEADS_PER_GROUP // 2):
                lms = []
                for h in (HEADS_PER_GROUP * g + 2 * p, HEADS_PER_GROUP * g + 2 * p + 1):
                    diff = acs_all[:, h:h + 1] - acst_all[h:h + 1, :]
                    lms.append(_b(cb * jnp.exp(jnp.where(low, diff, -jnp.inf)) * dtt[h:h + 1, :]))
                xa, xb = _pair_masks(xsb[:, p * LANES:(p + 1) * LANES])
                yp = _dot(jnp.concatenate(lms, axis=1), jnp.concatenate([xa, xb], axis=0))
                y_ref[:, g * GROUP_W + p * LANES:g * GROUP_W + (p + 1) * LANES] = (
                    yp + rest[:, p * LANES:(p + 1) * LANES])
            hs_ref[0, g] = ht
            st = _dot_tn(bb, _b(xs * _expand_heads(wd_all, g, ch)))
            h_ref[g] = ht * _expand_heads(cd_all, g, 1) + st

    blk = pl.BlockSpec((ch, LANES), lambda c: (c, 0))
    return pl.pallas_call(
        body, name="ssd_fwd", grid=(nch,),
        in_specs=[pl.BlockSpec((ch, CONV_DIM), lambda c: (c, 0)), blk, blk,
                  pl.BlockSpec((1, SSM_HEADS, ch), lambda c: (c, 0, 0)),
                  pl.BlockSpec((1, SSM_INNER), lambda c: (0, 0))],
        out_specs=[pl.BlockSpec((ch, SSM_INNER), lambda c: (c, 0)),
                   pl.BlockSpec((1, SSM_GROUPS, SSM_STATE, GROUP_W), lambda c: (c, 0, 0, 0))],
        out_shape=[jax.ShapeDtypeStruct((S, SSM_INNER), F32),
                   jax.ShapeDtypeStruct((nch, SSM_GROUPS, SSM_STATE, GROUP_W), F32)],
        scratch_shapes=[pltpu.VMEM((SSM_GROUPS, SSM_STATE, GROUP_W), F32)],
        compiler_params=_params(("arbitrary",)),
    )(xact, dt, acs, acst, dsk_e)


def _ssd_fwd3(xact, dt, acs, acst, dsk_e):
    S = xact.shape[0]
    ch = SSM_CHUNK
    nch = S // ch
    ng, hg = SSM_GROUPS, HEADS_PER_GROUP
    nbc = SSM_GROUPS * SSM_STATE

    def body(x_ref, dt_ref, acs_ref, acst_ref, dsk_ref, y_ref, hs_ref,
             h_ref, e_ea, xdb_s, xddb_s, bcb_s, cb_s, zz_s, st_s, lmb_s):
        c = pl.program_id(0)

        @pl.when(c == 0)
        def _():
            h_ref[...] = jnp.zeros_like(h_ref)

        dt_all = dt_ref[...]
        acs_all = acs_ref[...]
        alast = acs_all[ch - 1:ch, :]
        eacs = jnp.exp(acs_all)
        dsd_all = jnp.exp(alast - acs_all)
        cd_all = jnp.exp(alast)
        ii = lax.broadcasted_iota(jnp.int32, (ch, ch), 0)
        jj = lax.broadcasted_iota(jnp.int32, (ch, ch), 1)
        low = ii >= jj
        gsl = [slice(g * GROUP_W, (g + 1) * GROUP_W) for g in range(ng)]
        bcb_s[...] = _b(x_ref[:, B_COL0:])
        for g in range(ng):
            xd = x_ref[:, gsl[g]] * _expand_heads(dt_all, g, ch)
            xdb_s[:, gsl[g]] = _b(xd)
            xddb_s[:, gsl[g]] = _b(xd * _expand_heads(dsd_all, g, ch))
            e_ea[:, gsl[g]] = _expand_heads(eacs, g, ch)
        for g in range(ng):
            bb = bcb_s[:, g * SSM_STATE:(g + 1) * SSM_STATE]
            cc = bcb_s[:, nbc + g * SSM_STATE:nbc + (g + 1) * SSM_STATE]
            cb_s[g] = _dot_nt(cc, bb)
            zz_s[:, gsl[g]] = _dot(cc, _b(h_ref[g]))
            st_s[g] = _dot_tn(bb, xddb_s[:, gsl[g]])
        for g in range(ng):
            cb = cb_s[g]
            for j in range(hg):
                h = hg * g + j
                diff = acs_all[:, h:h + 1] - acst_ref[0, h:h + 1, :]
                lmb_s[:, h * ch:(h + 1) * ch] = _b(cb * jnp.exp(jnp.where(low, diff, -jnp.inf)))
            ht = h_ref[g]
            hs_ref[0, g] = ht
            h_ref[g] = ht * _expand_heads(cd_all, g, 1) + st_s[g]
        for g in range(ng):
            for p in range(hg // 2):
                h0 = hg * g + 2 * p
                sl = slice(g * GROUP_W + p * LANES, g * GROUP_W + (p + 1) * LANES)
                yp = _dot(lmb_s[:, h0 * ch:(h0 + 2) * ch], jnp.concatenate(_pair_masks(xdb_s[:, sl]), axis=0))
                y_ref[:, sl] = yp + zz_s[:, sl] * e_ea[:, sl] + dsk_ref[:, sl] * x_ref[:, sl]

    blk = pl.BlockSpec((ch, LANES), lambda c: (c, 0))
    wide = lambda dt_: pltpu.VMEM((ch, SSM_INNER), dt_)
    return pl.pallas_call(
        body, name="ssd_fwd", grid=(nch,),
        in_specs=[pl.BlockSpec((ch, CONV_DIM), lambda c: (c, 0)), blk, blk,
                  pl.BlockSpec((1, SSM_HEADS, ch), lambda c: (c, 0, 0)),
                  pl.BlockSpec((1, SSM_INNER), lambda c: (0, 0))],
        out_specs=[pl.BlockSpec((ch, SSM_INNER), lambda c: (c, 0)),
                   pl.BlockSpec((1, SSM_GROUPS, SSM_STATE, GROUP_W), lambda c: (c, 0, 0, 0))],
        out_shape=[jax.ShapeDtypeStruct((S, SSM_INNER), F32),
                   jax.ShapeDtypeStruct((nch, SSM_GROUPS, SSM_STATE, GROUP_W), F32)],
        scratch_shapes=[pltpu.VMEM((ng, SSM_STATE, GROUP_W), F32), wide(F32), wide(BF16), wide(BF16), wide(BF16),
                        pltpu.VMEM((ng, ch, ch), F32), wide(F32), pltpu.VMEM((ng, SSM_STATE, GROUP_W), F32),
                        pltpu.VMEM((ch, SSM_HEADS * ch), BF16)],
        compiler_params=_params(("arbitrary",)),
    )(xact, dt, acs, acst, dsk_e)


def _ssd_bwd2(xact, dt, acs, acst, sig, a_neg, dsk_e, hs, dy):
    S = xact.shape[0]
    ch = SSM_CHUNK
    nch = S // ch

    def body(x_ref, dt_ref, acs_ref, acst_ref, sig_ref, a_ref, dsk_ref, hs_ref, dy_ref,
             dx_ref, ddt_ref, st_ref, dh_ref, rows_ref):
        step = pl.program_id(0)

        @pl.when(step == 0)
        def _():
            dh_ref[...] = jnp.zeros_like(dh_ref)
            st_ref[...] = jnp.zeros_like(st_ref)
            rows_ref[...] = jnp.zeros_like(rows_ref)

        dt_all = dt_ref[...]
        acs_all = acs_ref[...]
        acst_all = acst_ref[0]
        alast = acs_all[ch - 1:ch, :]
        eacs = jnp.exp(acs_all)
        dsd_all = jnp.exp(alast - acs_all)
        cd_all = jnp.exp(alast)
        ii = lax.broadcasted_iota(jnp.int32, (ch, ch), 0)
        jj = lax.broadcasted_iota(jnp.int32, (ch, ch), 1)
        low = ii >= jj
        lane = lax.broadcasted_iota(jnp.int32, (ch, LANES), 1)
        cols = jnp.zeros((ch, LANES), F32)
        ddt = jnp.zeros((ch, LANES), F32)
        dal = jnp.zeros((1, LANES), F32)
        ddsk = jnp.zeros((1, LANES), F32)
        for g in range(SSM_GROUPS):
            xs = x_ref[:, g * GROUP_W:(g + 1) * GROUP_W]
            bb = _b(x_ref[:, B_COL0 + g * SSM_STATE:B_COL0 + (g + 1) * SSM_STATE])
            cc = _b(x_ref[:, C_COL0 + g * SSM_STATE:C_COL0 + (g + 1) * SSM_STATE])
            cb = _dot_nt(cc, bb)
            dt_e = _expand_heads(dt_all, g, ch)
            ea_e = _expand_heads(eacs, g, ch)
            dsd_e = _expand_heads(dsd_all, g, ch)
            cd_e = _expand_heads(cd_all, g, 1)
            xd = xs * dt_e
            xdb = _b(xd)
            dyg = dy_ref[:, g * GROUP_W:(g + 1) * GROUP_W]
            dyb = _b(dyg)
            ht = hs_ref[0, g]
            htb = _b(ht)
            dhn = dh_ref[g]
            dhnb = _b(dhn)
            zz = _dot(cc, htb)
            dzb = _b(dyg * ea_e)
            d_c = _dot_nt(dzb, htb)
            dh_in = _dot_tn(cc, dzb)
            ww = _dot(bb, dhnb)
            xdd = xd * dsd_e
            d_b = _dot_nt(_b(xdd), dhnb)
            t2 = ww * xdd
            e_g = _seg_matrix(GROUP_W, SSM_HEAD_DIM, HEADS_PER_GROUP * g)
            cols = cols + _head_sums(dyg * zz * ea_e - t2, e_g, ch)
            dal = dal + _head_sums(jnp.sum(t2, axis=0, keepdims=True), e_g, 1) + cd_all * _head_sums(
                jnp.sum(dhn * ht, axis=0, keepdims=True), e_g, 1)
            dh_ref[g] = dh_in + dhn * cd_e
            ddsk = ddsk + _head_sums(jnp.sum(dyg * xs, axis=0, keepdims=True), e_g, 1)
            dxd_rest = ww * dsd_e
            dcb = jnp.zeros((ch, ch), F32)
            for p in range(HEADS_PER_GROUP // 2):
                dya, dyb2 = _pair_masks(dyb[:, p * LANES:(p + 1) * LANES])
                xp = xdb[:, p * LANES:(p + 1) * LANES]
                lms, gms = [], []
                for h, dyh in ((HEADS_PER_GROUP * g + 2 * p, dya), (HEADS_PER_GROUP * g + 2 * p + 1, dyb2)):
                    diff = acs_all[:, h:h + 1] - acst_all[h:h + 1, :]
                    decay = jnp.exp(jnp.where(low, diff, -jnp.inf))
                    lm = cb * decay
                    dlm = _dot_nt(dyh, xp)
                    gm = dlm * lm
                    dcb = dcb + dlm * decay
                    rows_ref[h:h + 1, :] = jnp.sum(gm, axis=0, keepdims=True)
                    lms.append(_b(lm))
                    gms.append(gm)
                h0 = HEADS_PER_GROUP * g + 2 * p
                cols = cols + _head_sums(jnp.concatenate(gms, axis=1), _seg_matrix(2 * ch, ch, h0), ch)
                dxd = _dot_tn(jnp.concatenate(lms, axis=0), jnp.concatenate([dya, dyb2], axis=0))
                dxd = dxd + dxd_rest[:, p * LANES:(p + 1) * LANES]
                sl = slice(g * GROUP_W + p * LANES, g * GROUP_W + (p + 1) * LANES)
                dx_ref[:, sl] = (dsk_ref[:, sl] * dyg[:, p * LANES:(p + 1) * LANES]
                                 + dxd * dt_e[:, p * LANES:(p + 1) * LANES])
                ddt = ddt + _head_sums(dxd * xs[:, p * LANES:(p + 1) * LANES],
                                       _seg_matrix(LANES, SSM_HEAD_DIM, h0), ch)
            dcbb = _b(dcb)
            dx_ref[:, C_COL0 + g * SSM_STATE:C_COL0 + (g + 1) * SSM_STATE] = d_c + _dot(dcbb, bb)
            dx_ref[:, B_COL0 + g * SSM_STATE:B_COL0 + (g + 1) * SSM_STATE] = d_b + _dot_tn(dcbb, cc)
        rowi = lax.broadcasted_iota(jnp.int32, (ch, 1), 0)
        dacs = cols - rows_ref[...].T + jnp.where(rowi == ch - 1, dal, 0.0)
        dla = _dot_hi(jnp.where(ii <= jj, 1.0, 0.0), dacs)
        a_row = a_ref[...]
        ddt_raw = (ddt + dla * a_row) * sig_ref[...]
        ddt_ref[...] = _b(ddt_raw)
        st_ref[0:1, :] += jnp.sum(dla * dt_all, axis=0, keepdims=True) * a_row
        st_ref[1:2, :] += ddsk
        st_ref[2:3, :] += jnp.sum(ddt_raw, axis=0, keepdims=True)

    rc = lambda s: nch - 1 - s
    blk = pl.BlockSpec((ch, LANES), lambda s: (rc(s), 0))
    return pl.pallas_call(
        body, name="ssd_bwd", grid=(nch,),
        in_specs=[pl.BlockSpec((ch, CONV_DIM), lambda s: (rc(s), 0)), blk, blk,
                  pl.BlockSpec((1, SSM_HEADS, ch), lambda s: (rc(s), 0, 0)), blk,
                  pl.BlockSpec((1, LANES), lambda s: (0, 0)),
                  pl.BlockSpec((1, SSM_INNER), lambda s: (0, 0)),
                  pl.BlockSpec((1, SSM_GROUPS, SSM_STATE, GROUP_W), lambda s: (rc(s), 0, 0, 0)),
                  pl.BlockSpec((ch, SSM_INNER), lambda s: (rc(s), 0))],
        out_specs=[pl.BlockSpec((ch, CONV_DIM), lambda s: (rc(s), 0)), blk,
                   pl.BlockSpec((8, LANES), lambda s: (0, 0))],
        out_shape=[jax.ShapeDtypeStruct((S, CONV_DIM), F32), jax.ShapeDtypeStruct((S, LANES), BF16),
                   jax.ShapeDtypeStruct((8, LANES), F32)],
        scratch_shapes=[pltpu.VMEM((SSM_GROUPS, SSM_STATE, GROUP_W), F32), pltpu.VMEM((LANES, ch), F32)],
        compiler_params=_params(("arbitrary",)),
    )(xact, dt, acs, acst, sig, a_neg, dsk_e, hs, dy)


def _ssd_bwd3(xact, dt, acs, acst, sig, a_neg, dsk_e, hs, dy):
    S = xact.shape[0]
    ch = SSM_CHUNK
    nch = S // ch
    ng, hg = SSM_GROUPS, HEADS_PER_GROUP
    nbc = SSM_GROUPS * SSM_STATE

    def body(x_ref, dt_ref, acs_ref, acst_ref, sig_ref, a_ref, dsk_ref, hs_ref, dy_ref,
             dx_ref, ddt_ref, st_ref,
             dh_ref, rows_ref, e_dt, e_ea, e_dsd, xdb_s, xddb_s, dzb_s, bcb_s, cb_s, zz_s, ww_s, dc1_s, db1_s,
             dhin_s, dlm_s, lmb_s, gm_s, dcbb_s, t_s, dxd_s, prod_s, csum_s):
        step = pl.program_id(0)

        @pl.when(step == 0)
        def _():
            dh_ref[...] = jnp.zeros_like(dh_ref)
            st_ref[...] = jnp.zeros_like(st_ref)
            rows_ref[...] = jnp.zeros_like(rows_ref)

        dt_all = dt_ref[...]
        acs_all = acs_ref[...]
        alast = acs_all[ch - 1:ch, :]
        eacs = jnp.exp(acs_all)
        dsd_all = jnp.exp(alast - acs_all)
        cd_all = jnp.exp(alast)
        ii = lax.broadcasted_iota(jnp.int32, (ch, ch), 0)
        jj = lax.broadcasted_iota(jnp.int32, (ch, ch), 1)
        low = ii >= jj
        gsl = [slice(g * GROUP_W, (g + 1) * GROUP_W) for g in range(ng)]
        psl = [[slice(g * GROUP_W + p * LANES, g * GROUP_W + (p + 1) * LANES) for p in range(hg // 2)]
               for g in range(ng)]
        seg = [_seg_matrix(GROUP_W, SSM_HEAD_DIM, hg * g) for g in range(ng)]

        def bc(g):
            return (bcb_s[:, g * SSM_STATE:(g + 1) * SSM_STATE],
                    bcb_s[:, nbc + g * SSM_STATE:nbc + (g + 1) * SSM_STATE])

        def dy_pair(g, p):
            return _pair_masks(_b(dy_ref[:, psl[g][p]]))

        bcb_s[...] = _b(x_ref[:, B_COL0:])
        for g in range(ng):
            dt_e = _expand_heads(dt_all, g, ch)
            ea_e = _expand_heads(eacs, g, ch)
            dsd_e = _expand_heads(dsd_all, g, ch)
            e_dt[:, gsl[g]] = dt_e
            e_ea[:, gsl[g]] = ea_e
            e_dsd[:, gsl[g]] = dsd_e
            xd = x_ref[:, gsl[g]] * dt_e
            xdb_s[:, gsl[g]] = _b(xd)
            xddb_s[:, gsl[g]] = _b(xd * dsd_e)
            dzb_s[:, gsl[g]] = _b(dy_ref[:, gsl[g]] * ea_e)
        for g in range(ng):
            bb, cc = bc(g)
            htb = _b(hs_ref[0, g])
            dhnb = _b(dh_ref[g])
            cb_s[g] = _dot_nt(cc, bb)
            zz_s[:, gsl[g]] = _dot(cc, htb)
            ww_s[:, gsl[g]] = _dot(bb, dhnb)
            dc1_s[g] = _dot_nt(dzb_s[:, gsl[g]], htb)
            db1_s[g] = _dot_nt(xddb_s[:, gsl[g]], dhnb)
            dhin_s[g] = _dot_tn(cc, dzb_s[:, gsl[g]])
            for p in range(hg // 2):
                xp = xdb_s[:, psl[g][p]]
                for q, dyh in enumerate(dy_pair(g, p)):
                    dlm_s[hg * g + 2 * p + q] = _dot_nt(dyh, xp)
        for g in range(ng):
            cb = cb_s[g]
            dcb = jnp.zeros((ch, ch), F32)
            for j in range(hg):
                h = hg * g + j
                diff = acs_all[:, h:h + 1] - acst_ref[0, h:h + 1, :]
                decay = jnp.exp(jnp.where(low, diff, -jnp.inf))
                lm = cb * decay
                dlm = dlm_s[h]
                gm = dlm * lm
                dcb = dcb + dlm * decay
                rows_ref[h:h + 1, :] = jnp.sum(gm, axis=0, keepdims=True)
                lmb_s[h * ch:(h + 1) * ch, :] = _b(lm)
                gm_s[:, h * ch:(h + 1) * ch] = gm
            dcbb_s[g] = _b(dcb)
            xs = x_ref[:, gsl[g]]
            dyg = dy_ref[:, gsl[g]]
            ww = ww_s[:, gsl[g]]
            dsd_e = e_dsd[:, gsl[g]]
            t2 = ww * (xs * e_dt[:, gsl[g]] * dsd_e)
            t_s[:, gsl[g]] = dyg * zz_s[:, gsl[g]] * e_ea[:, gsl[g]] - t2
            dhn = dh_ref[g]
            csum_s[0:1, gsl[g]] = jnp.sum(t2, axis=0, keepdims=True)
            csum_s[1:2, gsl[g]] = jnp.sum(dhn * hs_ref[0, g], axis=0, keepdims=True)
            csum_s[2:3, gsl[g]] = jnp.sum(dyg * xs, axis=0, keepdims=True)
            dh_ref[g] = dhin_s[g] + dhn * _expand_heads(cd_all, g, 1)
            dxd_s[:, gsl[g]] = ww * dsd_e
        cols = jnp.zeros((ch, LANES), F32)
        for g in range(ng):
            bb, cc = bc(g)
            dcbb = dcbb_s[g]
            dx_ref[:, C_COL0 + g * SSM_STATE:C_COL0 + (g + 1) * SSM_STATE] = dc1_s[g] + _dot(dcbb, bb)
            dx_ref[:, B_COL0 + g * SSM_STATE:B_COL0 + (g + 1) * SSM_STATE] = db1_s[g] + _dot_tn(dcbb, cc)
            cols = cols + _head_sums(t_s[:, gsl[g]], seg[g], ch)
            for p in range(hg // 2):
                h0 = hg * g + 2 * p
                dxd_s[:, psl[g][p]] += _dot_tn(lmb_s[h0 * ch:(h0 + 2) * ch, :],
                                               jnp.concatenate(dy_pair(g, p), axis=0))
                cols = cols + _head_sums(gm_s[:, h0 * ch:(h0 + 2) * ch], _seg_matrix(2 * ch, ch, h0), ch)
        for g in range(ng):
            dxd = dxd_s[:, gsl[g]]
            xs = x_ref[:, gsl[g]]
            dx_ref[:, gsl[g]] = dsk_ref[:, gsl[g]] * dy_ref[:, gsl[g]] + dxd * e_dt[:, gsl[g]]
            prod_s[:, gsl[g]] = dxd * xs
        ddt = jnp.zeros((ch, LANES), F32)
        dal = jnp.zeros((1, LANES), F32)
        ddsk = jnp.zeros((1, LANES), F32)
        for g in range(ng):
            ddt = ddt + _head_sums(prod_s[:, gsl[g]], seg[g], ch)
            dal = (dal + _head_sums(csum_s[0:1, gsl[g]], seg[g], 1)
                   + cd_all * _head_sums(csum_s[1:2, gsl[g]], seg[g], 1))
            ddsk = ddsk + _head_sums(csum_s[2:3, gsl[g]], seg[g], 1)
        rowi = lax.broadcasted_iota(jnp.int32, (ch, 1), 0)
        dacs = cols - rows_ref[...].T + jnp.where(rowi == ch - 1, dal, 0.0)
        dla = _dot_hi(jnp.where(ii <= jj, 1.0, 0.0), dacs)
        a_row = a_ref[...]
        ddt_raw = (ddt + dla * a_row) * sig_ref[...]
        ddt_ref[...] = _b(ddt_raw)
        st_ref[0:1, :] += jnp.sum(dla * dt_all, axis=0, keepdims=True) * a_row
        st_ref[1:2, :] += ddsk
        st_ref[2:3, :] += jnp.sum(ddt_raw, axis=0, keepdims=True)

    rc = lambda s: nch - 1 - s
    blk = pl.BlockSpec((ch, LANES), lambda s: (rc(s), 0))
    wide = lambda dt_: pltpu.VMEM((ch, SSM_INNER), dt_)
    sq = lambda n, dt_: pltpu.VMEM((n, ch, ch), dt_)
    scratch = [pltpu.VMEM((ng, SSM_STATE, GROUP_W), F32), pltpu.VMEM((LANES, ch), F32),
               wide(F32), wide(F32), wide(F32),
               wide(BF16), wide(BF16), wide(BF16), wide(BF16),
               sq(ng, F32), wide(F32), wide(F32), sq(ng, F32), sq(ng, F32),
               pltpu.VMEM((ng, SSM_STATE, GROUP_W), F32),
               sq(SSM_HEADS, F32),
               pltpu.VMEM((SSM_HEADS * ch, ch), BF16),
               pltpu.VMEM((ch, SSM_HEADS * ch), F32),
               sq(ng, BF16), wide(F32), wide(F32), wide(F32),
               pltpu.VMEM((8, SSM_INNER), F32)]
    return pl.pallas_call(
        body, name="ssd_bwd", grid=(nch,),
        in_specs=[pl.BlockSpec((ch, CONV_DIM), lambda s: (rc(s), 0)), blk, blk,
                  pl.BlockSpec((1, SSM_HEADS, ch), lambda s: (rc(s), 0, 0)), blk,
                  pl.BlockSpec((1, LANES), lambda s: (0, 0)),
                  pl.BlockSpec((1, SSM_INNER), lambda s: (0, 0)),
                  pl.BlockSpec((1, SSM_GROUPS, SSM_STATE, GROUP_W), lambda s: (rc(s), 0, 0, 0)),
                  pl.BlockSpec((ch, SSM_INNER), lambda s: (rc(s), 0))],
        out_specs=[pl.BlockSpec((ch, CONV_DIM), lambda s: (rc(s), 0)), blk,
                   pl.BlockSpec((8, LANES), lambda s: (0, 0))],
        out_shape=[jax.ShapeDtypeStruct((S, CONV_DIM), F32), jax.ShapeDtypeStruct((S, LANES), BF16),
                   jax.ShapeDtypeStruct((8, LANES), F32)],
        scratch_shapes=scratch,
        compiler_params=_params(("arbitrary",)),
    )(xact, dt, acs, acst, sig, a_neg, dsk_e, hs, dy)


def _pad_lanes(v, n=LANES):
    return jnp.pad(v, ((0, 0), (0, n - v.shape[1])))


def _local_step(x, target, w, ex=None):
    offs = np.cumsum((0,) + IN_SPLITS)
    wt_in = w["w_in_t"]
    w_qkv = wt_in[offs[0]:offs[3]]
    w_z = wt_in[offs[3]:offs[4]]
    w_xbc = wt_in[offs[4]:offs[5]]
    w_dt = jnp.pad(wt_in[offs[5]:offs[6]], ((0, LANES - SSM_HEADS), (0, 0)))
    w_g = wt_in[offs[6]:offs[7]]
    dt_bias = _pad_lanes(w["dt_bias"])
    a_neg = _pad_lanes(-jnp.exp(w["a_log"]))
    d_skip = _pad_lanes(w["d_skip"])

    u = _rms_fwd(x, w["norm_mix_pre_w"])
    if ex is None:
        xbc = _mm_nn(u, w_xbc, F32, "proj_xbc", tb=True)
    else:
        xbc, got = _mm_nn(u, w_xbc, F32, "proj_xbc", comm=_gather_comm([ex.mine[REST_EARLY]]), tb=True)
        w = {**w, **ex.rest_weights(got[0], REST_EARLY)}
    qkv = _mm_nn(u, w_qkv, F32, "proj_qkv", tb=True)
    z = _mm_nn(u, w_z, F32, "proj_z", tb=True)
    dt_raw = _mm_nn(u, w_dt, F32, "proj_dt", tb=True)
    gl = _mm_nn(u, w_g, F32, "proj_gate", tb=True)

    pats = _qkv_layouts(qkv)
    os_, ms_, ls_ = [], [], []
    for i, (d, qkv_p) in enumerate(zip(DILATIONS, pats)):
        if ex is not None and i < len(REST_LATE):
            o, m, l, got = _attn_fwd2(qkv_p, d, comm=_gather_comm([ex.mine[REST_LATE[i]]]))
            w = {**w, **ex.rest_weights(got[0], REST_LATE[i])}
        else:
            o, m, l = _attn_fwd2(qkv_p, d)
        os_.append(o)
        ms_.append(m)
        ls_.append(l)
    att, lse = _attn_combine2(os_, ms_, ls_)
    att_o = _mm_nn(att, w["w_att_proj"], F32, "att_proj")

    xact, conv_pre = _conv_fwd2(xbc, w["conv_w"], w["conv_b"])
    dsk_e = jnp.repeat(w["d_skip"], SSM_HEAD_DIM, axis=1)
    dt, acs, acst, sig = _ssd_prep(dt_raw, dt_bias, a_neg)
    y_ssd, hs = _ssd_fwd2(xact, dt, acs, acst, dsk_e)
    (ssm_y, ssm_o), _ = _mm_epi(None, w["w_ssm_proj"], _ssm_out_epi(y_ssd, z, w["ssm_norm_w"]), "ssm_proj")

    (mi, mixed, h1, f), _ = _mm_epi(None, w["w_out"], _mix_out_epi(
        att_o, ssm_o, gl, x, w["b_gate"], w["norm_mix_post_w"], w["norm_ffn_pre_w"]), "out_proj")
    r_up, act = _mm_nn(f, w["w_up"], BF16, "ffn_up", mode="relu2")
    (dh2, d_down, loss, g_ffn_post), _ = _mm_epi(
        act, w["w_down"], _final_epi(h1, target, w["norm_ffn_post_w"]), "ffn_down")

    g = {"norm_ffn_post_w": g_ffn_post}
    g["w_down"] = _mm_tn(act, d_down, "dw_down")
    dup = _mm_nn(d_down, w["w_down"], BF16, "d_act", mode="mul2", extra=r_up, tb=True)
    g["w_up"] = _mm_tn(f, dup, "dw_up")
    (dh1, d_mixed, g["norm_ffn_pre_w"], g["norm_mix_post_w"]), _ = _mm_epi(
        dup, w["w_up"], _mid_epi(dh2, h1, mixed, w["norm_ffn_pre_w"], w["norm_mix_post_w"]), "d_f", tb=True)
    g["w_out"] = _mm_tn(mi, d_mixed, "dw_out")
    (d_att_o, d_ssm_o, dgl, g["b_gate"]), _ = _mm_epi(
        d_mixed, w["w_out"], _gate_epi(att_o, ssm_o, gl, w["b_gate"]), "d_mi", tb=True)

    g["w_att_proj"] = _mm_tn(att, d_att_o, "dw_att_proj")
    g["w_ssm_proj"] = _mm_tn(ssm_y, d_ssm_o, "dw_ssm_proj")
    gn_epi = _gnorm_epi(y_ssd, z, w["ssm_norm_w"])
    if ex is None:
        (dy_ssd, dz, g["ssm_norm_w"]), _ = _mm_epi(d_ssm_o, w["w_ssm_proj"], gn_epi, "d_ssm_y", tb=True,
                                                    tn=PACK_COLS)
    else:
        gs_rest = jnp.concatenate(
            [_shards_from_full(n, g[n]).reshape(N_CHIPS, -1, PACK_COLS) for n in REST], axis=1)
        (dy_ssd, dz, g["ssm_norm_w"]), recv = _mm_epi(d_ssm_o, w["w_ssm_proj"], gn_epi, "d_ssm_y", tb=True,
                                                       tn=PACK_COLS, comm=_pair_comm([gs_rest]))
        p_rest = _pair_add2(gs_rest, recv[0], ex.c_arr, "rs_pair_add_rest")

    d_att = _mm_nn(d_att_o, w["w_att_proj"], F32, "d_att", tb=True)
    dqs, dks, dvs = [], [], []
    for d, qkv_p, (do_p, lse_p, delta_p) in zip(DILATIONS, pats, _attn_delta2(d_att, att, lse)):
        if ex is not None and d == DILATIONS[0]:
            dq, dk, dv, recv3 = _attn_bwd2(qkv_p, do_p, lse_p, delta_p, d, comm=_chip_comm([p_rest]))
            q_rest = _chip_add2(p_rest, recv3[0], ex.chip_arr, "rs_chip_add_rest")
            ex.finish_reduce("rest", q_rest, _comm_call("rs_share_rest", _share_comm([q_rest]))[0])
        else:
            dq, dk, dv = _attn_bwd2(qkv_p, do_p, lse_p, delta_p, d)
        dqs.append(dq)
        dks.append(dk)
        dvs.append(dv)
    dqkv = _sum_qkv2(dqs, dks, dvs)

    dxact, ddt_raw, stats = _ssd_bwd3(xact, dt, acs, acst, sig, a_neg, dsk_e, hs, dy_ssd)
    g["a_log"] = stats[0:1, :SSM_HEADS]
    g["d_skip"] = stats[1:2, :SSM_HEADS]
    g["dt_bias"] = stats[2:3, :SSM_HEADS]
    dxbc, g["conv_w"], g["conv_b"] = _conv_bwd2(xbc, conv_pre, dxact, w["conv_w"])

    pieces = [(dqkv, w_qkv), (dz, w_z), (dxbc, w_xbc), (ddt_raw, w_dt), (dgl, w_g)]
    gw = [_mm_tn(dp, u, f"dw_in_{i}") for i, (dp, _) in enumerate(pieces)]
    gw[3] = gw[3][:SSM_HEADS]
    if ex is None:
        g["w_in_t"] = jnp.concatenate(gw, axis=0)
    du = None
    for i, (dp, wp) in enumerate([pieces[k] for k in (1, 2, 0, 3, 4)]):
        if ex is not None and i == 0:
            gs_in = _rows_to_shards(gw, IN_SHARD_ROWS, IN_SHARD_PAD)
            du, recv = _mm_nn(dp, wp, F32, f"d_u_{i}", acc=du, comm=_pair_comm([gs_in]))
            p_in = _pair_add2(gs_in, recv[0], ex.c_arr, "rs_pair_add_in")
            rows = p_in.shape[1] // 2
            p_parts = [p_in[:, :rows], p_in[:, rows:]]
            q_parts = []
        elif ex is not None and i in (1, 2):
            p_part = p_parts[i - 1]
            du, recv3 = _mm_nn(dp, wp, F32, f"d_u_{i}", acc=du, comm=_chip_comm([p_part]))
            q_parts.append(_chip_add2(p_part, recv3[0], ex.chip_arr, f"rs_chip_add_in_{i}"))
            if i == 2:
                others = _comm_call("rs_share_in", _share_comm(q_parts))
                ex.finish_reduce("w_in", jnp.concatenate(q_parts, axis=0), jnp.concatenate(others, axis=0))
        elif i == len(pieces) - 1:
            (grad_x, g["norm_mix_pre_w"]), _ = _mm_epi(
                dp, wp, _first_epi(du, dh1, x, w["norm_mix_pre_w"]), f"d_u_{i}")
        else:
            du = _mm_nn(dp, wp, F32, f"d_u_{i}", acc=du)
    return loss, grad_x, g


def _rows_to_shards(pieces, shard_rows, pad_rows):
    cols = pieces[0].shape[1]
    shards = []
    for s in range(N_CHIPS):
        lo, hi = s * shard_rows, (s + 1) * shard_rows
        parts, r0 = [], 0
        for p in pieces:
            a, b = max(lo, r0), min(hi, r0 + p.shape[0])
            if a < b:
                parts.append(p[a - r0:b - r0])
            r0 += p.shape[0]
        parts.append(jnp.zeros((pad_rows - shard_rows, cols), pieces[0].dtype))
        shards.append(jnp.concatenate(parts, axis=0))
    return jnp.stack(shards)


BIG = ("w_in", "w_att_proj", "w_ssm_proj", "w_out", "w_up", "w_down")
BIG_FULL_SHAPES = {"w_in": (D_MODEL, IN_PROJ_WIDTH), "w_att_proj": (ATT_WIDTH, D_MODEL),
                   "w_ssm_proj": (SSM_INNER, D_MODEL), "w_out": (D_MODEL, D_MODEL),
                   "w_up": (D_MODEL, FFN_HIDDEN), "w_down": (FFN_HIDDEN, D_MODEL)}
BIG_COL_SHARDED = {"w_in": True, "w_att_proj": True, "w_ssm_proj": False, "w_out": False, "w_up": True,
                   "w_down": False}
PACK_COLS = 1024
PACK_ROWS = 5760
PACK_HALF = PACK_ROWS // 2
PACK_BLOCK = 576
SMALL = ("norm_mix_pre_w", "b_gate", "conv_b", "dt_bias", "a_log", "d_skip", "ssm_norm_w",
         "norm_mix_post_w", "norm_ffn_pre_w", "norm_ffn_post_w")
SMALL_ROWS = 232


def _shard_shape(name):
    r, c = BIG_FULL_SHAPES[name]
    return (r, c // N_CHIPS) if BIG_COL_SHARDED[name] else (r // N_CHIPS, c)


def _pack(shards, dtype):
    flat = [shards[n].astype(dtype).reshape(-1, PACK_COLS) for n in BIG]
    rows = sum(f.shape[0] for f in flat)
    flat.append(jnp.zeros((PACK_ROWS - rows, PACK_COLS), dtype))
    return jnp.concatenate(flat, axis=0)


def _unpack(packed):
    out, r0 = {}, 0
    for n in BIG:
        shp = _shard_shape(n)
        rows = shp[0] * shp[1] // PACK_COLS
        out[n] = packed[r0:r0 + rows].reshape(shp)
        r0 += rows
    return out


def _unpack_full(gathered):
    out, r0 = {}, 0
    for n in BIG:
        shp = _shard_shape(n)
        rows = shp[0] * shp[1] // PACK_COLS
        sh = gathered[:, r0:r0 + rows].reshape((N_CHIPS,) + shp)
        if BIG_COL_SHARDED[n]:
            out[n] = sh.transpose(1, 0, 2).reshape(BIG_FULL_SHAPES[n])
        else:
            out[n] = sh.reshape(BIG_FULL_SHAPES[n])
        r0 += rows
    return out


def _pack_full(grads):
    parts = []
    rows_total = 0
    for n in BIG:
        shp = _shard_shape(n)
        gfull = grads[n]
        if BIG_COL_SHARDED[n]:
            sh = gfull.reshape(shp[0], N_CHIPS, shp[1]).transpose(1, 0, 2)
        else:
            sh = gfull.reshape((N_CHIPS,) + shp)
        parts.append(sh.reshape(N_CHIPS, -1, PACK_COLS))
        rows_total += parts[-1].shape[1]
    parts.append(jnp.zeros((N_CHIPS, PACK_ROWS - rows_total, PACK_COLS), F32))
    return jnp.concatenate(parts, axis=1)


def _mesh_pos():
    return lax.axis_index("x"), lax.axis_index("y"), lax.axis_index("c")


def _other_chips(x, y):
    return [(1 - x, y), (x, 1 - y), (1 - x, 1 - y)]


ANY = pl.BlockSpec(memory_space=pl.ANY)


def _allgather_packed(wpack):
    half = PACK_HALF

    def body(w_ref, out_ref, send_sems, recv_sems):
        x, y, c = _mesh_pos()
        me = 2 * x + y
        sibling = (x, y, 1 - c)
        chips = _other_chips(x, y)

        def rows(chip, h):
            return out_ref.at[chip, pl.ds(h * half, half), :]

        def copy(k, chip, h, to, src=None):
            return pltpu.make_async_remote_copy(
                src_ref=rows(chip, h) if src is None else src, dst_ref=rows(chip, h),
                send_sem=send_sems.at[k], recv_sem=recv_sems.at[k], device_id=to, device_id_type=MESH)

        mine_half = w_ref.at[pl.ds(c * half, half), :]
        first = [copy(j, me, c, (*chip, c), src=mine_half) for j, chip in enumerate(chips)]
        for cp in first:
            cp.start()
        passed = [copy(3 + j, 2 * chip[0] + chip[1], c, sibling) for j, chip in enumerate(chips)]
        for j, chip in enumerate(chips):
            copy(j, 2 * chip[0] + chip[1], c, (x, y, c)).wait_recv()
            passed[j].start()
        for j, chip in enumerate(chips):
            copy(3 + j, 2 * chip[0] + chip[1], 1 - c, (x, y, c)).wait_recv()
        for cp in first + passed:
            cp.wait_send()

    return pl.pallas_call(
        body, name="allgather_weights",
        out_shape=jax.ShapeDtypeStruct((N_CHIPS,) + wpack.shape, wpack.dtype),
        in_specs=[ANY], out_specs=ANY,
        scratch_shapes=[pltpu.SemaphoreType.DMA((6,)), pltpu.SemaphoreType.DMA((6,))],
        compiler_params=pltpu.CompilerParams(has_side_effects=True),
    )(wpack)


def _exchange_halves(gpack):
    half = PACK_HALF

    def body(g_ref, out_ref, send_sem, recv_sem):
        x, y, c = _mesh_pos()
        cp = pltpu.make_async_remote_copy(
            src_ref=g_ref.at[:, pl.ds((1 - c) * half, half), :], dst_ref=out_ref,
            send_sem=send_sem, recv_sem=recv_sem, device_id=(x, y, 1 - c), device_id_type=MESH)
        cp.start()
        cp.wait()

    return pl.pallas_call(
        body, name="rs_pair_exchange",
        out_shape=jax.ShapeDtypeStruct((N_CHIPS, half, PACK_COLS), F32),
        in_specs=[ANY], out_specs=ANY,
        scratch_shapes=[pltpu.SemaphoreType.DMA, pltpu.SemaphoreType.DMA],
        compiler_params=pltpu.CompilerParams(has_side_effects=True),
    )(gpack)


def _pair_add(gpack, recv, c_idx):
    nb = PACK_HALF // PACK_BLOCK

    def body(c_ref, g_ref, r_ref, o_ref):
        o_ref[...] = _b(g_ref[...] + r_ref[...])

    blk = (1, PACK_BLOCK, PACK_COLS)
    return pl.pallas_call(
        body, name="rs_pair_add",
        grid_spec=pltpu.PrefetchScalarGridSpec(
            num_scalar_prefetch=1, grid=(N_CHIPS, nb),
            in_specs=[pl.BlockSpec(blk, lambda s, i, c: (s, c[0] * nb + i, 0)),
                      pl.BlockSpec(blk, lambda s, i, c: (s, i, 0))],
            out_specs=pl.BlockSpec(blk, lambda s, i, c: (s, i, 0))),
        out_shape=jax.ShapeDtypeStruct((N_CHIPS, PACK_HALF, PACK_COLS), BF16),
        compiler_params=_params(("arbitrary", "arbitrary")),
    )(c_idx, gpack, recv)


def _exchange_chips(ppack):
    def body(p_ref, out_ref, send_sems, recv_sems):
        x, y, c = _mesh_pos()
        chips = _other_chips(x, y)
        cps = [pltpu.make_async_remote_copy(
            src_ref=p_ref.at[2 * chip[0] + chip[1]], dst_ref=out_ref.at[j],
            send_sem=send_sems.at[j], recv_sem=recv_sems.at[j], device_id=(*chip, c), device_id_type=MESH)
            for j, chip in enumerate(chips)]
        for cp in cps:
            cp.start()
        for cp in cps:
            cp.wait_recv()
        for cp in cps:
            cp.wait_send()

    return pl.pallas_call(
        body, name="rs_chip_exchange",
        out_shape=jax.ShapeDtypeStruct((N_CHIPS - 1, PACK_HALF, PACK_COLS), ppack.dtype),
        in_specs=[ANY], out_specs=ANY,
        scratch_shapes=[pltpu.SemaphoreType.DMA((3,)), pltpu.SemaphoreType.DMA((3,))],
        compiler_params=pltpu.CompilerParams(has_side_effects=True),
    )(ppack)


def _chip_add(ppack, recv, me_idx):
    nb = PACK_HALF // PACK_BLOCK

    def body(m_ref, p_ref, r0_ref, r1_ref, r2_ref, o_ref):
        o_ref[...] = ((p_ref[0].astype(F32) + r0_ref[0].astype(F32)) + r1_ref[0].astype(F32)) + r2_ref[0].astype(F32)

    blk = (1, PACK_BLOCK, PACK_COLS)
    return pl.pallas_call(
        body, name="rs_chip_add",
        grid_spec=pltpu.PrefetchScalarGridSpec(
            num_scalar_prefetch=1, grid=(nb,),
            in_specs=[pl.BlockSpec(blk, lambda i, m: (m[0], i, 0)),
                      pl.BlockSpec(blk, lambda i, m: (0, i, 0)),
                      pl.BlockSpec(blk, lambda i, m: (1, i, 0)),
                      pl.BlockSpec(blk, lambda i, m: (2, i, 0))],
            out_specs=pl.BlockSpec((PACK_BLOCK, PACK_COLS), lambda i, m: (i, 0))),
        out_shape=jax.ShapeDtypeStruct((PACK_HALF, PACK_COLS), F32),
        compiler_params=_params(("arbitrary",)),
    )(me_idx, ppack, recv, recv, recv)


def _share_halves(qhalf):
    def body(q_ref, out_ref, send_sem, recv_sem):
        x, y, c = _mesh_pos()
        cp = pltpu.make_async_remote_copy(
            src_ref=q_ref, dst_ref=out_ref, send_sem=send_sem, recv_sem=recv_sem,
            device_id=(x, y, 1 - c), device_id_type=MESH)
        cp.start()
        cp.wait()

    return pl.pallas_call(
        body, name="rs_share_halves",
        out_shape=jax.ShapeDtypeStruct(qhalf.shape, F32),
        in_specs=[ANY], out_specs=ANY,
        scratch_shapes=[pltpu.SemaphoreType.DMA, pltpu.SemaphoreType.DMA],
        compiler_params=pltpu.CompilerParams(has_side_effects=True),
    )(qhalf)


REST_EARLY = ("w_att_proj", "w_ssm_proj", "w_out")
REST_LATE = (("w_up",), ("w_down",))
REST = REST_EARLY + REST_LATE[0] + REST_LATE[1]
ADD_ROWS_CAP = 800
BF16_ROWS = 16
IN_SHARD_ROWS = IN_PROJ_WIDTH // N_CHIPS
IN_SHARD_PAD = 2688


def _stack_rest(shards, dtype, names=REST):
    return jnp.concatenate([shards[n].astype(dtype).reshape(-1, PACK_COLS) for n in names], axis=0)


def _unstack_rest(stacked, lead=(), names=REST):
    out, r0 = {}, 0
    for n in names:
        shp = _shard_shape(n)
        rows = shp[0] * shp[1] // PACK_COLS
        out[n] = stacked[..., r0:r0 + rows, :].reshape(lead + shp)
        r0 += rows
    return out


def _full_from_shards(name, sh):
    if BIG_COL_SHARDED[name]:
        return sh.transpose(1, 0, 2).reshape(BIG_FULL_SHAPES[name])
    return sh.reshape(BIG_FULL_SHAPES[name])


def _shards_from_full(name, full):
    shp = _shard_shape(name)
    if BIG_COL_SHARDED[name]:
        return full.reshape(shp[0], N_CHIPS, shp[1]).transpose(1, 0, 2)
    return full.reshape((N_CHIPS,) + shp)


def _allgather2(shards):
    n = len(shards)

    def body(*refs):
        w_refs, out_refs, send_sems, recv_sems = refs[:n], refs[n:2 * n], refs[2 * n], refs[2 * n + 1]
        x, y, c = _mesh_pos()
        me = 2 * x + y
        sibling = (x, y, 1 - c)
        chips = _other_chips(x, y)
        plans = []
        for a, (w_ref, out_ref) in enumerate(zip(w_refs, out_refs)):
            half = w_ref.shape[0] // 2

            def copy(k, chip, h, to, src=None, out_ref=out_ref, half=half, a=a):
                rows = out_ref.at[chip, pl.ds(h * half, half), :]
                return pltpu.make_async_remote_copy(
                    src_ref=rows if src is None else src, dst_ref=rows,
                    send_sem=send_sems.at[6 * a + k], recv_sem=recv_sems.at[6 * a + k],
                    device_id=to, device_id_type=MESH)

            mine_half = w_ref.at[pl.ds(c * half, half), :]
            idx = [2 * chip[0] + chip[1] for chip in chips]
            send = [copy(j, me, c, (*chip, c), src=mine_half) for j, chip in enumerate(chips)]
            land = [copy(j, idx[j], c, (x, y, c)) for j in range(N_CHIPS - 1)]
            forward = [copy(3 + j, idx[j], c, sibling) for j in range(N_CHIPS - 1)]
            land_fw = [copy(3 + j, idx[j], 1 - c, (x, y, c)) for j in range(N_CHIPS - 1)]
            plans.append((send, land, forward, land_fw))
        for send, _, _, _ in plans:
            for cp in send:
                cp.start()
        for _, land, forward, _ in plans:
            for j in range(N_CHIPS - 1):
                land[j].wait_recv()
                forward[j].start()
        for _, _, _, land_fw in plans:
            for cp in land_fw:
                cp.wait_recv()
        for send, _, forward, _ in plans:
            for cp in send + forward:
                cp.wait_send()

    return pl.pallas_call(
        body, name="allgather_weights",
        out_shape=[jax.ShapeDtypeStruct((N_CHIPS,) + s.shape, s.dtype) for s in shards],
        in_specs=[ANY] * n, out_specs=[ANY] * n,
        scratch_shapes=[pltpu.SemaphoreType.DMA((6 * n,)), pltpu.SemaphoreType.DMA((6 * n,))],
        compiler_params=pltpu.CompilerParams(has_side_effects=True),
    )(*shards)


def _exchange_halves2(gs):
    n = len(gs)

    def body(*refs):
        g_refs, out_refs, send_sems, recv_sems = refs[:n], refs[n:2 * n], refs[2 * n], refs[2 * n + 1]
        x, y, c = _mesh_pos()
        cps = []
        for a, (g_ref, out_ref) in enumerate(zip(g_refs, out_refs)):
            half = g_ref.shape[1] // 2
            cps.append(pltpu.make_async_remote_copy(
                src_ref=g_ref.at[:, pl.ds((1 - c) * half, half), :], dst_ref=out_ref,
                send_sem=send_sems.at[a], recv_sem=recv_sems.at[a], device_id=(x, y, 1 - c),
                device_id_type=MESH))
        for cp in cps:
            cp.start()
        for cp in cps:
            cp.wait()

    return pl.pallas_call(
        body, name="rs_pair_exchange",
        out_shape=[jax.ShapeDtypeStruct((N_CHIPS, g.shape[1] // 2, g.shape[2]), F32) for g in gs],
        in_specs=[ANY] * n, out_specs=[ANY] * n,
        scratch_shapes=[pltpu.SemaphoreType.DMA((n,)), pltpu.SemaphoreType.DMA((n,))],
        compiler_params=pltpu.CompilerParams(has_side_effects=True),
    )(*gs)


def _pair_add2(g, recv, c_idx, name):
    _, half, cols = recv.shape
    rb = _row_block(half, ADD_ROWS_CAP, BF16_ROWS)
    nb = half // rb

    def body(c_ref, g_ref, r_ref, o_ref):
        o_ref[...] = _b(g_ref[...] + r_ref[...])

    blk = (1, rb, cols)
    return pl.pallas_call(
        body, name=name,
        grid_spec=pltpu.PrefetchScalarGridSpec(
            num_scalar_prefetch=1, grid=(N_CHIPS, nb),
            in_specs=[pl.BlockSpec(blk, lambda s, i, c: (s, c[0] * nb + i, 0)),
                      pl.BlockSpec(blk, lambda s, i, c: (s, i, 0))],
            out_specs=pl.BlockSpec(blk, lambda s, i, c: (s, i, 0))),
        out_shape=jax.ShapeDtypeStruct(recv.shape, BF16),
        compiler_params=_params(("arbitrary", "arbitrary")),
    )(c_idx, g, recv)


def _exchange_chips2(ps):
    n = len(ps)

    def body(*refs):
        p_refs, out_refs, send_sems, recv_sems = refs[:n], refs[n:2 * n], refs[2 * n], refs[2 * n + 1]
        x, y, c = _mesh_pos()
        chips = _other_chips(x, y)
        cps = [pltpu.make_async_remote_copy(
            src_ref=p_ref.at[2 * chip[0] + chip[1]], dst_ref=out_ref.at[j],
            send_sem=send_sems.at[3 * a + j], recv_sem=recv_sems.at[3 * a + j], device_id=(*chip, c),
            device_id_type=MESH)
            for a, (p_ref, out_ref) in enumerate(zip(p_refs, out_refs)) for j, chip in enumerate(chips)]
        for cp in cps:
            cp.start()
        for cp in cps:
            cp.wait_recv()
        for cp in cps:
            cp.wait_send()

    return pl.pallas_call(
        body, name="rs_chip_exchange",
        out_shape=[jax.ShapeDtypeStruct((N_CHIPS - 1,) + p.shape[1:], p.dtype) for p in ps],
        in_specs=[ANY] * n, out_specs=[ANY] * n,
        scratch_shapes=[pltpu.SemaphoreType.DMA((3 * n,)), pltpu.SemaphoreType.DMA((3 * n,))],
        compiler_params=pltpu.CompilerParams(has_side_effects=True),
    )(*ps)


def _chip_add2(p, recv, me_idx, name):
    _, half, cols = recv.shape
    rb = _row_block(half, ADD_ROWS_CAP, BF16_ROWS)

    def body(m_ref, p_ref, r0_ref, r1_ref, r2_ref, o_ref):
        o_ref[...] = ((p_ref[0].astype(F32) + r0_ref[0].astype(F32)) + r1_ref[0].astype(F32)) + r2_ref[0].astype(F32)

    blk = (1, rb, cols)
    return pl.pallas_call(
        body, name=name,
        grid_spec=pltpu.PrefetchScalarGridSpec(
            num_scalar_prefetch=1, grid=(half // rb,),
            in_specs=[pl.BlockSpec(blk, lambda i, m: (m[0], i, 0)),
                      pl.BlockSpec(blk, lambda i, m: (0, i, 0)),
                      pl.BlockSpec(blk, lambda i, m: (1, i, 0)),
                      pl.BlockSpec(blk, lambda i, m: (2, i, 0))],
            out_specs=pl.BlockSpec((rb, cols), lambda i, m: (i, 0))),
        out_shape=jax.ShapeDtypeStruct((half, cols), F32),
        compiler_params=_params(("arbitrary",)),
    )(me_idx, p, recv, recv, recv)


def _share_halves2(qs):
    n = len(qs)

    def body(*refs):
        q_refs, out_refs, send_sems, recv_sems = refs[:n], refs[n:2 * n], refs[2 * n], refs[2 * n + 1]
        x, y, c = _mesh_pos()
        cps = [pltpu.make_async_remote_copy(
            src_ref=q_ref, dst_ref=out_ref, send_sem=send_sems.at[a], recv_sem=recv_sems.at[a],
            device_id=(x, y, 1 - c), device_id_type=MESH)
            for a, (q_ref, out_ref) in enumerate(zip(q_refs, out_refs))]
        for cp in cps:
            cp.start()
        for cp in cps:
            cp.wait()

    return pl.pallas_call(
        body, name="rs_share_halves",
        out_shape=[jax.ShapeDtypeStruct(q.shape, F32) for q in qs],
        in_specs=[ANY] * n, out_specs=[ANY] * n,
        scratch_shapes=[pltpu.SemaphoreType.DMA((n,)), pltpu.SemaphoreType.DMA((n,))],
        compiler_params=pltpu.CompilerParams(has_side_effects=True),
    )(*qs)


def _gather_plan():
    def copies(w_refs, out_refs, send_sems, recv_sems):
        x, y, c = _mesh_pos()
        me = 2 * x + y
        sibling = (x, y, 1 - c)
        chips = _other_chips(x, y)
        idx = [2 * chip[0] + chip[1] for chip in chips]
        plans = []
        for a, (w_ref, out_ref) in enumerate(zip(w_refs, out_refs)):
            half = w_ref.shape[0] // 2

            def copy(k, chip, h, to, src=None, out_ref=out_ref, half=half, a=a):
                rows = out_ref.at[chip, pl.ds(h * half, half), :]
                return pltpu.make_async_remote_copy(
                    src_ref=rows if src is None else src, dst_ref=rows,
                    send_sem=send_sems.at[6 * a + k], recv_sem=recv_sems.at[6 * a + k],
                    device_id=to, device_id_type=MESH)

            mine_half = w_ref.at[pl.ds(c * half, half), :]
            send = [copy(j, me, c, (*chip, c), src=mine_half) for j, chip in enumerate(chips)]
            land = [copy(j, idx[j], c, (x, y, c)) for j in range(N_CHIPS - 1)]
            forward = [copy(3 + j, idx[j], c, sibling) for j in range(N_CHIPS - 1)]
            land_fw = [copy(3 + j, idx[j], 1 - c, (x, y, c)) for j in range(N_CHIPS - 1)]
            plans.append((send, land, forward, land_fw))
        return plans

    def start(*refs):
        for send, _, _, _ in copies(*refs):
            for cp in send:
                cp.start()

    def finish(*refs):
        plans = copies(*refs)
        for _, land, forward, _ in plans:
            for j in range(N_CHIPS - 1):
                land[j].wait_recv()
                forward[j].start()
        for _, _, _, land_fw in plans:
            for cp in land_fw:
                cp.wait_recv()
        for send, _, forward, _ in plans:
            for cp in send + forward:
                cp.wait_send()

    return start, finish


def _pair_plan(halves):
    def copies(in_refs, out_refs, send_sems, recv_sems):
        x, y, c = _mesh_pos()
        cps = []
        for a, (g_ref, out_ref) in enumerate(zip(in_refs, out_refs)):
            if halves:
                half = g_ref.shape[1] // 2
                src = g_ref.at[:, pl.ds((1 - c) * half, half), :]
            else:
                src = g_ref
            cps.append(pltpu.make_async_remote_copy(
                src_ref=src, dst_ref=out_ref, send_sem=send_sems.at[a], recv_sem=recv_sems.at[a],
                device_id=(x, y, 1 - c), device_id_type=MESH))
        return cps

    def start(*refs):
        for cp in copies(*refs):
            cp.start()

    def finish(*refs):
        for cp in copies(*refs):
            cp.wait()

    return start, finish


def _chip_plan():
    def copies(in_refs, out_refs, send_sems, recv_sems):
        x, y, c = _mesh_pos()
        chips = _other_chips(x, y)
        return [pltpu.make_async_remote_copy(
            src_ref=p_ref.at[2 * chip[0] + chip[1]], dst_ref=out_ref.at[j],
            send_sem=send_sems.at[3 * a + j], recv_sem=recv_sems.at[3 * a + j], device_id=(*chip, c),
            device_id_type=MESH)
            for a, (p_ref, out_ref) in enumerate(zip(in_refs, out_refs)) for j, chip in enumerate(chips)]

    def start(*refs):
        for cp in copies(*refs):
            cp.start()

    def finish(*refs):
        cps = copies(*refs)
        for cp in cps:
            cp.wait_recv()
        for cp in cps:
            cp.wait_send()

    return start, finish


def _gather_comm(shards):
    return _Comm(_gather_plan(), shards, [jax.ShapeDtypeStruct((N_CHIPS,) + s.shape, s.dtype) for s in shards],
                 6 * len(shards))


def _pair_comm(gs):
    return _Comm(_pair_plan(True), gs,
                 [jax.ShapeDtypeStruct((N_CHIPS, g.shape[1] // 2, g.shape[2]), g.dtype) for g in gs], len(gs))


def _chip_comm(ps):
    return _Comm(_chip_plan(), ps, [jax.ShapeDtypeStruct((N_CHIPS - 1,) + p.shape[1:], p.dtype) for p in ps],
                 3 * len(ps))


def _share_comm(qs):
    return _Comm(_pair_plan(False), qs, [jax.ShapeDtypeStruct(q.shape, q.dtype) for q in qs], len(qs))


def _comm_call(name, comm):
    n, m = len(comm.ins), len(comm.outs)

    def body(*refs):
        args = (refs[:n], refs[n:n + m], refs[n + m], refs[n + m + 1])
        comm.start(*args)
        comm.finish(*args)

    return pl.pallas_call(
        body, name=name, out_shape=comm.outs, in_specs=[ANY] * n, out_specs=[ANY] * m,
        scratch_shapes=[pltpu.SemaphoreType.DMA((comm.n_sems,))] * 2,
        compiler_params=pltpu.CompilerParams(has_side_effects=True),
    )(*comm.ins)


class _Exchange:
    def __init__(self, chip, ci, mine):
        self.chip, self.ci = chip, ci
        self.mine = mine
        self.c_arr = ci.reshape(1).astype(jnp.int32)
        self.chip_arr = chip.reshape(1).astype(jnp.int32)
        self.reduced = {}

    def rest_weights(self, got, names):
        stacks = lax.dynamic_update_slice(got, self.mine[names][None], (self.chip, 0, 0))
        return {n: _full_from_shards(n, sh) for n, sh in _unstack_rest(stacks, (N_CHIPS,), names).items()}

    def finish_reduce(self, key, mine, other):
        south = self.ci == 0
        self.reduced[key] = jnp.concatenate([jnp.where(south, mine, other), jnp.where(south, other, mine)],
                                            axis=0)


def _allreduce_small(part, name):
    rows = part.shape[0]

    def body(p_ref, out_ref, buf, send_sems, recv_sems, local_sem):
        x, y, c = _mesh_pos()
        me, sibling = (x, y, c), (x, y, 1 - c)
        chips = _other_chips(x, y)

        def slot(px, py, pc):
            return buf.at[pl.ds((4 * px + 2 * py + pc) * rows, rows), :]

        def copy(k, block, to, src=None):
            return pltpu.make_async_remote_copy(
                src_ref=slot(*block) if src is None else src, dst_ref=slot(*block),
                send_sem=send_sems.at[k], recv_sem=recv_sems.at[k], device_id=to, device_id_type=MESH)

        mine = pltpu.make_async_copy(p_ref, slot(*me), local_sem)
        mine.start()
        first = [copy(0, me, sibling, src=p_ref)]
        first += [copy(1 + j, me, (*chip, c), src=p_ref) for j, chip in enumerate(chips)]
        for cp in first:
            cp.start()
        passed = [copy(4 + j, (*chip, c), sibling) for j, chip in enumerate(chips)]
        for j, chip in enumerate(chips):
            copy(1 + j, (*chip, c), me).wait_recv()
            passed[j].start()
        copy(0, sibling, me).wait_recv()
        for j, chip in enumerate(chips):
            copy(4 + j, (*chip, 1 - c), me).wait_recv()
        for cp in first + passed:
            cp.wait_send()
        mine.wait()
        acc = buf[pl.ds(0, rows), :]
        for k in range(1, N_DEV):
            acc = acc + buf[pl.ds(k * rows, rows), :]
        out_ref[...] = acc

    return pl.pallas_call(
        body, name=name,
        out_shape=jax.ShapeDtypeStruct(part.shape, F32),
        in_specs=[pl.BlockSpec(memory_space=pltpu.VMEM)],
        out_specs=pl.BlockSpec(memory_space=pltpu.VMEM),
        scratch_shapes=[pltpu.VMEM((N_DEV * rows, LANES), F32), pltpu.SemaphoreType.DMA((7,)),
                        pltpu.SemaphoreType.DMA((7,)), pltpu.SemaphoreType.DMA],
        compiler_params=pltpu.CompilerParams(has_side_effects=True),
    )(part)


def _adamw(w, g, m, v, name):
    R, C = w.shape
    bs = _row_block(R, 512, 8) if R % 8 == 0 else R
    c1 = 1.0 / (1.0 - ADAM_B1 ** ADAM_STEP)
    c2 = 1.0 / (1.0 - ADAM_B2 ** ADAM_STEP)

    def body(w_ref, g_ref, m_ref, v_ref, d_ref, nm_ref, nv_ref):
        gg = g_ref[...]
        nm = ADAM_B1 * m_ref[...] + (1.0 - ADAM_B1) * gg
        nv = ADAM_B2 * v_ref[...] + (1.0 - ADAM_B2) * (gg * gg)
        nm_ref[...] = nm
        nv_ref[...] = nv
        d_ref[...] = -ADAM_LR * ((nm * c1) / (jnp.sqrt(nv * c2) + ADAM_EPS) + ADAM_WD * w_ref[...])

    spec = pl.BlockSpec((bs, C), lambda i: (i, 0))
    shp = jax.ShapeDtypeStruct((R, C), F32)
    return pl.pallas_call(
        body, name=name, grid=(R // bs,), in_specs=[spec] * 4, out_specs=[spec] * 3, out_shape=[shp] * 3,
        compiler_params=_params(("parallel",)),
    )(w, g, m, v)


WEIGHTS = ("norm_mix_pre_w", "w_in", "b_gate", "conv_w", "conv_b", "dt_bias", "a_log", "d_skip",
           "ssm_norm_w", "w_att_proj", "w_ssm_proj", "w_out", "norm_mix_post_w", "norm_ffn_pre_w", "w_up",
           "w_down", "norm_ffn_post_w")


def _flat_small(vals, conv_w_full):
    flat = [vals[n].reshape(-1) for n in SMALL] + [conv_w_full.reshape(-1)]
    v = jnp.concatenate(flat)
    return jnp.pad(v, (0, SMALL_ROWS * LANES - v.shape[0])).reshape(SMALL_ROWS, LANES)


def kernel(x, norm_mix_pre_w, w_in, b_gate, conv_w, conv_b, dt_bias, a_log, d_skip, ssm_norm_w, w_att_proj, w_ssm_proj, w_out, norm_mix_post_w, norm_ffn_pre_w, w_up, w_down, norm_ffn_post_w, loss_target, m_norm_mix_pre_w, m_w_in, m_b_gate, m_conv_w, m_conv_b, m_dt_bias, m_a_log, m_d_skip, m_ssm_norm_w, m_w_att_proj, m_w_ssm_proj, m_w_out, m_norm_mix_post_w, m_norm_ffn_pre_w, m_w_up, m_w_down, m_norm_ffn_post_w, v_norm_mix_pre_w, v_w_in, v_b_gate, v_conv_w, v_conv_b, v_dt_bias, v_a_log, v_d_skip, v_ssm_norm_w, v_w_att_proj, v_w_ssm_proj, v_w_out, v_norm_mix_post_w, v_norm_ffn_pre_w, v_w_up, v_w_down, v_norm_ffn_post_w):
    args = locals()

    def strip(a):
        return a[0] if a.ndim == 3 else a

    wts = {n: strip(args[n]) for n in WEIGHTS}
    mom = {n: strip(args["m_" + n]) for n in WEIGHTS}
    var = {n: strip(args["v_" + n]) for n in WEIGHTS}
    xi, yi, ci = _mesh_pos()
    chip = 2 * xi + yi

    tr = lambda a: jnp.swapaxes(a, 0, 1)
    w_in_mine = jnp.pad(tr(wts["w_in"]).astype(BF16), ((0, IN_SHARD_PAD - IN_SHARD_ROWS), (0, 0)))
    got_in = _comm_call("allgather_w_in", _gather_comm([w_in_mine]))[0]
    stacks_in = lax.dynamic_update_slice(got_in, w_in_mine[None], (chip, 0, 0))
    full = {"w_in_t": stacks_in[:, :IN_SHARD_ROWS].reshape(IN_PROJ_WIDTH, D_MODEL)}
    ex = _Exchange(chip, ci, {names: _stack_rest(wts, BF16, names) for names in (REST_EARLY,) + REST_LATE})
    cw_cols = CONV_DIM // N_CHIPS
    conv_slab = lax.dynamic_update_slice(jnp.zeros((SSM_CONV, CONV_DIM), F32),
                                         jnp.where(ci == 0, wts["conv_w"], 0.0), (0, chip * cw_cols))
    small_in = jnp.pad(conv_slab.reshape(-1), (0, SMALL_ROWS * LANES - SSM_CONV * CONV_DIM))
    conv_full = _allreduce_small(small_in.reshape(SMALL_ROWS, LANES), "gather_conv_w")
    full["conv_w"] = conv_full.reshape(-1)[:SSM_CONV * CONV_DIM].reshape(SSM_CONV, CONV_DIM)
    for n in SMALL:
        full[n] = wts[n]

    loss_part, grad_x, g = _local_step(x[0], loss_target[0], full, ex)
    loss = lax.psum(loss_part[0, 0], ("x", "y", "c"))

    gshard = _unstack_rest(ex.reduced["rest"])
    g_in_t = ex.reduced["w_in"][:IN_SHARD_ROWS]
    small_sum = _allreduce_small(_flat_small(g, g["conv_w"]), "allreduce_small_grads").reshape(-1)
    grads, off = {}, 0
    for n in SMALL:
        sz = wts[n].size
        grads[n] = small_sum[off:off + sz].reshape(wts[n].shape)
        off += sz
    conv_g = small_sum[off:off + SSM_CONV * CONV_DIM].reshape(SSM_CONV, CONV_DIM)
    grads["conv_w"] = lax.dynamic_slice(conv_g, (0, chip * cw_cols), (SSM_CONV, cw_cols))
    grads.update(gshard)

    delta, new_m, new_v = {}, {}, {}
    for n in REST:
        delta[n], new_m[n], new_v[n] = _adamw(wts[n], grads[n], mom[n], var[n], f"adamw_{n}")
    in_t = _adamw(tr(wts["w_in"]), g_in_t, tr(mom["w_in"]), tr(var["w_in"]), "adamw_w_in")
    grads["w_in"] = tr(g_in_t)
    delta["w_in"], new_m["w_in"], new_v["w_in"] = (tr(a) for a in in_t)
    small_names = SMALL + ("conv_w",)

    def pack_small(d):
        v = jnp.concatenate([d[n].reshape(-1) for n in small_names])
        rows = -(-v.shape[0] // (8 * LANES)) * 8
        return jnp.pad(v, (0, rows * LANES - v.shape[0])).reshape(rows, LANES)

    ds, ms, vs = _adamw(pack_small(wts), pack_small(grads), pack_small(mom), pack_small(var), "adamw_small")
    off = 0
    for n in small_names:
        sz = wts[n].size
        for dst, src in ((delta, ds), (new_m, ms), (new_v, vs)):
            dst[n] = src.reshape(-1)[off:off + sz].reshape(wts[n].shape)
        off += sz

    out = [loss, grad_x[None]]
    for d in (grads, delta, new_m, new_v):
        out += [d[n][None] if args[n].ndim == 3 else d[n] for n in WEIGHTS]
    return tuple(out)
```

```python
import functools
import math

import numpy as np
import jax
import jax.numpy as jnp
from jax import lax
from jax.experimental import pallas as pl
from jax.experimental.pallas import tpu as pltpu

F32 = jnp.float32
BF16 = jnp.bfloat16

D_MODEL = 1024
HEAD_DIM = 64
N_ATT_HEADS = 12
ATT_WIDTH = N_ATT_HEADS * HEAD_DIM
DILATIONS = (1, 4, 16)
ATT_BLOCK = 128
SSM_INNER = 2048
SSM_HEADS = 32
SSM_GROUPS = 8
HEADS_PER_GROUP = SSM_HEADS // SSM_GROUPS
SSM_HEAD_DIM = 64
SSM_STATE = 128
SSM_CONV = 4
SSM_CHUNK = 128
CONV_DIM = SSM_INNER + 2 * SSM_GROUPS * SSM_STATE
FFN_HIDDEN = 4 * D_MODEL
IN_SPLITS = (ATT_WIDTH, ATT_WIDTH, ATT_WIDTH, SSM_INNER, CONV_DIM, SSM_HEADS, 2 * D_MODEL)
IN_PROJ_WIDTH = sum(IN_SPLITS)
RMS_EPS = 1e-6
LANES = 128
NEG_BIG = -1e30

ADAM_LR = 0.001
ADAM_B1 = 0.9
ADAM_B2 = 0.999
ADAM_EPS = 1e-08
ADAM_WD = 0.01
ADAM_STEP = 10

N_CHIPS = 4
N_DEV = 8
VMEM_LIMIT = 56 * 1024 * 1024
MESH = pl.DeviceIdType.MESH


def _alibi_slopes(n):
    def pow2(m):
        start = 2.0 ** (-8.0 / m)
        return [start ** (i + 1) for i in range(m)]
    if (n & (n - 1)) == 0:
        s = pow2(n)
    else:
        c = 2 ** int(math.floor(math.log2(n)))
        s = pow2(c) + pow2(2 * c)[0::2][: n - c]
    return [float(v) for v in np.array(s, dtype=np.float32)]


def _params(sem):
    return pltpu.CompilerParams(dimension_semantics=sem, vmem_limit_bytes=VMEM_LIMIT)


def _dot(a, b):
    return lax.dot_general(a, b, (((1,), (0,)), ((), ())), preferred_element_type=F32)


def _dot_nt(a, b):
    return lax.dot_general(a, b, (((1,), (1,)), ((), ())), preferred_element_type=F32)


def _dot_tn(a, b):
    return lax.dot_general(a, b, (((0,), (0,)), ((), ())), preferred_element_type=F32)


def _dot_hi(a, b):
    return lax.dot_general(a, b, (((1,), (0,)), ((), ())), preferred_element_type=F32,
                           precision=lax.Precision.HIGHEST)


def _dot_tn_hi(a, b):
    return lax.dot_general(a, b, (((0,), (0,)), ((), ())), preferred_element_type=F32,
                           precision=lax.Precision.HIGHEST)


def _b(x):
    return x.astype(BF16)


def _sigmoid(x):
    return 1.0 / (1.0 + jnp.exp(-x))


def _pick(n, cands):
    for c in cands:
        if n % c == 0:
            return c
    raise ValueError(f"no tile for {n}")


def _row_block(rows, cap, mult):
    best = max(d for d in range(mult, cap + 1, mult) if rows % d == 0)
    return best


class _Comm:
    def __init__(self, plan, ins, outs, n_sems):
        self.start, self.finish = plan
        self.ins, self.outs, self.n_sems = list(ins), list(outs), n_sems


def _mm_nn(a, b, out_dtype, name, acc=None, mode=None, extra=None, comm=None, tb=False):
    M, K = a.shape
    N = b.shape[0] if tb else b.shape[1]
    tn = _pick(N, (1024, 768, 512, 256, 128))
    tk = K if K <= 4096 else _pick(K, (2048, 1024))
    tm = 1024 if M % 1024 == 0 and K <= 2304 else 512
    nk = K // tk
    nj, ni = N // tn, M // tm
    side = acc if acc is not None else extra
    n_out = 2 if mode == "relu2" else 1
    n_in = 2 + (side is not None)
    n_ci = len(comm.ins) if comm else 0
    n_co = len(comm.outs) if comm else 0

    def body(*refs):
        a_ref, b_ref = refs[0], refs[1]
        s_ref = refs[2] if side is not None else None
        o_refs = refs[n_in + n_ci:n_in + n_ci + n_out]
        if comm:
            c_args = (refs[n_in:n_in + n_ci], refs[n_in + n_ci + n_out:n_in + n_ci + n_out + n_co],
                      refs[-2], refs[-1])
            pj, pi, pk = pl.program_id(0), pl.program_id(1), pl.program_id(2)

            @pl.when(jnp.logical_and(jnp.logical_and(pj == 0, pi == 0), pk == 0))
            def _():
                comm.start(*c_args)

        def finish(r):
            if mode == "relu2":
                r = jnp.maximum(r, 0.0)
                o_refs[0][...] = _b(r)
                o_refs[1][...] = _b(r * r)
            elif mode == "mul2":
                o_refs[0][...] = _b(r * (2.0 * s_ref[...].astype(F32)))
            else:
                if acc is not None:
                    r = r + s_ref[...]
                o_refs[0][...] = r.astype(out_dtype)

        part = (_dot_nt if tb else _dot)(_b(a_ref[...]), _b(b_ref[...]))
        if nk == 1:
            finish(part)
        else:
            acc_ref = refs[n_in + n_ci + n_out + n_co]
            k = pl.program_id(2)

            @pl.when(k == 0)
            def _():
                acc_ref[...] = part

            @pl.when(jnp.logical_and(k > 0, k < nk - 1))
            def _():
                acc_ref[...] += part

            @pl.when(k == nk - 1)
            def _():
                finish(acc_ref[...] + part)

        if comm:
            @pl.when(jnp.logical_and(jnp.logical_and(pj == nj - 1, pi == ni - 1), pk == nk - 1))
            def _():
                comm.finish(*c_args)

    tile = pl.BlockSpec((tm, tn), lambda j, i, k: (i, j))
    in_specs = [pl.BlockSpec((tm, tk), lambda j, i, k: (i, k)),
                pl.BlockSpec((tn, tk), lambda j, i, k: (j, k)) if tb else
                pl.BlockSpec((tk, tn), lambda j, i, k: (k, j))]
    args = [a, b]
    if side is not None:
        in_specs.append(tile)
        args.append(side)
    odt = BF16 if mode in ("relu2", "mul2") else out_dtype
    scratch = [pltpu.VMEM((tm, tn), F32)] if nk > 1 else []
    if comm:
        scratch += [pltpu.SemaphoreType.DMA((comm.n_sems,))] * 2
        params = pltpu.CompilerParams(dimension_semantics=("arbitrary",) * 3, vmem_limit_bytes=VMEM_LIMIT,
                                      has_side_effects=True)
    else:
        params = _params(("parallel", "parallel", "arbitrary"))
    outs = pl.pallas_call(
        body, name=name, grid=(nj, ni, nk),
        in_specs=in_specs + [ANY] * n_ci,
        out_specs=[tile] * n_out + [ANY] * n_co,
        out_shape=[jax.ShapeDtypeStruct((M, N), odt)] * n_out + list(comm.outs if comm else []),
        scratch_shapes=scratch,
        compiler_params=params,
    )(*args, *(comm.ins if comm else []))
    res = outs[:n_out] if n_out > 1 else outs[0]
    return (res, outs[n_out:]) if comm else res


class _Epi:
    def __init__(self, fn, row_ins=(), full_ins=(), row_outs=(), acc_outs=(), tiled=False, a_fn=None,
                 scratch=()):
        self.fn, self.row_ins, self.full_ins = fn, list(row_ins), list(full_ins)
        self.row_outs, self.acc_outs, self.tiled = list(row_outs), list(acc_outs), tiled
        self.a_fn = a_fn
        self.scratch = list(scratch)


def _acc_into(ref, val, first):
    @pl.when(first)
    def _():
        ref[...] = val

    @pl.when(jnp.logical_not(first))
    def _():
        ref[...] += val


def _mm_epi(a, b, epi, name, tb=False, comm=None, tm=512, tn=None):
    N, K = b.shape if tb else b.shape[::-1]
    M = epi.row_ins[0][0].shape[0] if a is None else a.shape[0]
    tn = tn or N
    assert epi.tiled or tn == N
    tk = K if K <= 4096 else _pick(K, (2048, 1024))
    nk = K // tk
    nj, ni = N // tn, M // tm
    assert a is not None or (nk == 1 and nj == 1)
    n_a = 0 if a is None else 1
    n_ri, n_fi, n_ro, n_ao = len(epi.row_ins), len(epi.full_ins), len(epi.row_outs), len(epi.acc_outs)
    n_ci = len(comm.ins) if comm else 0
    n_co = len(comm.outs) if comm else 0
    i0 = n_a + 1
    o0 = i0 + n_ci + n_ri + n_fi

    def body(*refs):
        b_ref = refs[n_a]
        ri = refs[i0 + n_ci:i0 + n_ci + n_ri]
        fi = refs[i0 + n_ci + n_ri:o0]
        ro = refs[o0 + n_co:o0 + n_co + n_ro]
        ao = refs[o0 + n_co + n_ro:o0 + n_co + n_ro + n_ao]
        pj, pi, pk = pl.program_id(0), pl.program_id(1), pl.program_id(2)
        if comm:
            c_args = (refs[i0:i0 + n_ci], refs[o0:o0 + n_co], refs[-2], refs[-1])

            @pl.when(jnp.logical_and(jnp.logical_and(pj == 0, pi == 0), pk == 0))
            def _():
                comm.start(*c_args)

        s0 = o0 + n_co + n_ro + n_ao + (nk > 1)
        extra = (refs[s0:s0 + len(epi.scratch)],) if epi.scratch else ()
        a_val = epi.a_fn(ri, fi, ro) if a is None else _b(refs[0][...])
        part = (_dot_nt if tb else _dot)(a_val, _b(b_ref[...]))
        if nk == 1:
            epi.fn(part, ri, fi, ro, ao, pi == 0, *extra)
        else:
            acc_ref = refs[o0 + n_co + n_ro + n_ao]

            @pl.when(pk == 0)
            def _():
                acc_ref[...] = part

            @pl.when(jnp.logical_and(pk > 0, pk < nk - 1))
            def _():
                acc_ref[...] += part

            @pl.when(pk == nk - 1)
            def _():
                epi.fn(acc_ref[...] + part, ri, fi, ro, ao, pi == 0, *extra)

        if comm:
            @pl.when(jnp.logical_and(jnp.logical_and(pj == nj - 1, pi == ni - 1), pk == nk - 1))
            def _():
                comm.finish(*c_args)

    def row_spec(width, cb):
        if epi.tiled:
            return pl.BlockSpec((tm, tn), lambda j, i, k: (i, j + cb))
        return pl.BlockSpec((tm, width), lambda j, i, k: (i, cb))

    in_specs = [pl.BlockSpec((tm, tk), lambda j, i, k: (i, k))] * n_a
    in_specs += [pl.BlockSpec((tn, tk), lambda j, i, k: (j, k)) if tb else
                 pl.BlockSpec((tk, tn), lambda j, i, k: (k, j))]
    in_specs += [ANY] * n_ci
    in_specs += [row_spec(w, cb) for (_, w, cb) in epi.row_ins]
    in_specs += [pl.BlockSpec((1, tn), lambda j, i, k: (0, j)) if epi.tiled else
                 pl.BlockSpec(f.shape, lambda j, i, k: (0, 0)) for f in epi.full_ins]
    out_specs = [ANY] * n_co + [row_spec(c, 0) if isinstance(c, int) else spec(tm, tn) for c, spec in epi.row_outs]
    out_specs += [pl.BlockSpec((1, tn), lambda j, i, k: (0, j)) if epi.tiled else
                  pl.BlockSpec((1, c), lambda j, i, k: (0, 0)) for c in epi.acc_outs]
    out_shape = list(comm.outs if comm else [])
    out_shape += [jax.ShapeDtypeStruct((M, c), dt_) if isinstance(c, int) else c for c, dt_ in epi.row_outs]
    out_shape += [jax.ShapeDtypeStruct((1, c), F32) for c in epi.acc_outs]
    scratch = ([pltpu.VMEM((tm, tn), F32)] if nk > 1 else []) + epi.scratch
    if comm:
        scratch += [pltpu.SemaphoreType.DMA((comm.n_sems,))] * 2
    params = pltpu.CompilerParams(dimension_semantics=("arbitrary",) * 3, vmem_limit_bytes=VMEM_LIMIT,
                                  has_side_effects=comm is not None)
    outs = pl.pallas_call(
        body, name=name, grid=(nj, ni, nk), in_specs=in_specs, out_specs=out_specs, out_shape=out_shape,
        scratch_shapes=scratch, compiler_params=params,
    )(*([a] * n_a), b, *(comm.ins if comm else []), *[arr for arr, _, _ in epi.row_ins], *epi.full_ins)
    return outs[n_co:], (outs[:n_co] if comm else None)


def _mm_tn(a, b, name):
    S, Ka = a.shape
    _, N = b.shape
    tka = _pick(Ka, (1024, 768, 512, 256, 128))
    tn = _pick(N, (1024, 768, 512, 256, 128))
    ts = 1024 if S % 1024 == 0 else 512
    ns = S // ts

    def body(a_ref, b_ref, o_ref, acc_ref):
        s = pl.program_id(2)
        part = _dot_tn(_b(a_ref[...]), _b(b_ref[...]))

        @pl.when(s == 0)
        def _():
            acc_ref[...] = part

        @pl.when(s > 0)
        def _():
            acc_ref[...] += part

        @pl.when(s == ns - 1)
        def _():
            o_ref[...] = acc_ref[...]

    return pl.pallas_call(
        body, name=name, grid=(Ka // tka, N // tn, ns),
        in_specs=[pl.BlockSpec((ts, tka), lambda i, j, s: (s, i)),
                  pl.BlockSpec((ts, tn), lambda i, j, s: (s, j))],
        out_specs=pl.BlockSpec((tka, tn), lambda i, j, s: (i, j)),
        out_shape=jax.ShapeDtypeStruct((Ka, N), F32),
        scratch_shapes=[pltpu.VMEM((tka, tn), F32)],
        compiler_params=_params(("parallel", "parallel", "arbitrary")),
    )(a, b)


def _row_call(body, row_ins, full_ins, row_outs, acc_outs, bs, name):
    S = row_ins[0].shape[0]
    assert S % bs == 0
    in_specs = [pl.BlockSpec((bs, a.shape[1]), lambda i: (i, 0)) for a in row_ins]
    in_specs += [pl.BlockSpec(a.shape, lambda i: (0, 0)) for a in full_ins]
    out_specs = [pl.BlockSpec((bs, c), lambda i: (i, 0)) for c, _ in row_outs]
    out_specs += [pl.BlockSpec(s, lambda i: (0, 0)) for s in acc_outs]
    out_shape = [jax.ShapeDtypeStruct((S, c), dt) for c, dt in row_outs]
    out_shape += [jax.ShapeDtypeStruct(s, F32) for s in acc_outs]
    return pl.pallas_call(
        body, name=name, grid=(S // bs,), in_specs=in_specs, out_specs=out_specs, out_shape=out_shape,
        compiler_params=_params(("arbitrary",)),
    )(*row_ins, *full_ins)


def _rms_vals(x, w):
    r = lax.rsqrt(jnp.mean(x * x, axis=-1, keepdims=True) + RMS_EPS)
    return x * r * w


def _rms_bwd_vals(x, w, dy):
    r = lax.rsqrt(jnp.mean(x * x, axis=-1, keepdims=True) + RMS_EPS)
    xn = x * r
    g = dy * w
    dx = r * (g - xn * jnp.mean(g * xn, axis=-1, keepdims=True))
    dw = jnp.sum(dy * xn, axis=0, keepdims=True)
    return dx, dw


def _acc_add(ref, val):
    @pl.when(pl.program_id(0) == 0)
    def _():
        ref[...] = val

    @pl.when(pl.program_id(0) > 0)
    def _():
        ref[...] += val


def _rms_fwd(x, w):
    def body(x_ref, w_ref, o_ref):
        o_ref[...] = _b(_rms_vals(x_ref[...], w_ref[...]))
    return _row_call(body, [x], [w], [(x.shape[1], BF16)], [], 512, "rms_fwd")[0]


def _gate_fwd(att_o, ssm_o, gl, b_gate):
    def body(a_ref, s_ref, g_ref, b_ref, o_ref):
        g = _sigmoid(g_ref[...] + b_ref[...])
        o_ref[...] = _b(g[:, :D_MODEL] * a_ref[...] + g[:, D_MODEL:] * s_ref[...])
    return _row_call(body, [att_o, ssm_o, gl], [b_gate], [(D_MODEL, BF16)], [], 512, "gate_fwd")[0]


def _post_pre(x, mixed, w_post, w_pre):
    def body(x_ref, m_ref, wp_ref, wn_ref, h_ref, f_ref):
        h = x_ref[...] + _rms_vals(m_ref[...], wp_ref[...])
        h_ref[...] = h
        f_ref[...] = _b(_rms_vals(h, wn_ref[...]))
    return _row_call(body, [x, mixed], [w_post, w_pre], [(D_MODEL, F32), (D_MODEL, BF16)], [], 512,
                     "post_pre")


def _relu2(up):
    def body(u_ref, o_ref):
        r = jnp.maximum(u_ref[...], 0.0)
        o_ref[...] = _b(r * r)
    return _row_call(body, [up], [], [(up.shape[1], BF16)], [], 256, "relu2")[0]


def _final(h1, down, w_post, target):
    def body(h_ref, d_ref, t_ref, w_ref, dh_ref, dd_ref, loss_ref, dw_ref):
        dn = d_ref[...]
        w = w_ref[...]
        err = h_ref[...] + _rms_vals(dn, w) - t_ref[...]
        row = jnp.mean(err * err, axis=-1, keepdims=True)
        part = 0.5 * jnp.sum(row, axis=0, keepdims=True)
        dh = err * (1.0 / D_MODEL)
        dh_ref[...] = dh
        dx, dw = _rms_bwd_vals(dn, w, dh)
        dd_ref[...] = _b(dx)
        _acc_add(loss_ref, jnp.broadcast_to(part, (1, LANES)))
        _acc_add(dw_ref, dw)
    return _row_call(body, [h1, down, target], [w_post], [(D_MODEL, F32), (D_MODEL, BF16)],
                     [(1, LANES), (1, D_MODEL)], 512, "final_loss")


def _dup(da, up):
    def body(a_ref, u_ref, o_ref):
        o_ref[...] = _b(a_ref[...] * (2.0 * jnp.maximum(u_ref[...], 0.0)))
    return _row_call(body, [da, up], [], [(up.shape[1], BF16)], [], 256, "relu2_bwd")[0]


def _mid_bwd(dh2, df, h1, mixed, w_pre, w_post):
    def body(dh_ref, df_ref, h_ref, m_ref, wn_ref, wp_ref, dh1_ref, dm_ref, dwn_ref, dwp_ref):
        dx, dwn = _rms_bwd_vals(h_ref[...], wn_ref[...], df_ref[...])
        dh1 = dh_ref[...] + dx
        dh1_ref[...] = dh1
        dm, dwp = _rms_bwd_vals(m_ref[...], wp_ref[...], dh1)
        dm_ref[...] = _b(dm)
        _acc_add(dwn_ref, dwn)
        _acc_add(dwp_ref, dwp)
    return _row_call(body, [dh2, df, h1, mixed], [w_pre, w_post], [(D_MODEL, F32), (D_MODEL, BF16)],
                     [(1, D_MODEL), (1, D_MODEL)], 512, "mid_bwd")


def _gate_bwd(dmi, att_o, ssm_o, gl, b_gate):
    def body(d_ref, a_ref, s_ref, g_ref, b_ref, da_ref, ds_ref, dg_ref, db_ref):
        g = _sigmoid(g_ref[...] + b_ref[...])
        d = d_ref[...]
        ga, gs = g[:, :D_MODEL], g[:, D_MODEL:]
        da_ref[...] = _b(ga * d)
        ds_ref[...] = _b(gs * d)
        dga = d * a_ref[...] * ga * (1.0 - ga)
        dgs = d * s_ref[...] * gs * (1.0 - gs)
        dg_ref[:, :D_MODEL] = _b(dga)
        dg_ref[:, D_MODEL:] = _b(dgs)
        _acc_add(db_ref.at[:, pl.ds(0, D_MODEL)], jnp.sum(dga, axis=0, keepdims=True))
        _acc_add(db_ref.at[:, pl.ds(D_MODEL, D_MODEL)], jnp.sum(dgs, axis=0, keepdims=True))
    return _row_call(body, [dmi, att_o, ssm_o, gl], [b_gate],
                     [(D_MODEL, BF16), (D_MODEL, BF16), (2 * D_MODEL, BF16)], [(1, 2 * D_MODEL)], 256,
                     "gate_bwd")


def _first_bwd(dh1, du, x, w_pre):
    def body(dh_ref, du_ref, x_ref, w_ref, dx_ref, dw_ref):
        dx, dw = _rms_bwd_vals(x_ref[...], w_ref[...], du_ref[...])
        dx_ref[...] = dh_ref[...] + dx
        _acc_add(dw_ref, dw)
    return _row_call(body, [dh1, du, x], [w_pre], [(D_MODEL, F32)], [(1, D_MODEL)], 512, "first_bwd")


def _group_rms(t):
    gw = SSM_INNER // SSM_GROUPS
    out = []
    for g in range(SSM_GROUPS):
        tg = t[:, g * gw:(g + 1) * gw]
        out.append(lax.rsqrt(jnp.mean(tg * tg, axis=-1, keepdims=True) + RMS_EPS))
    return out


def _gnorm_fwd(y, z, w):
    gw = SSM_INNER // SSM_GROUPS

    def body(y_ref, z_ref, w_ref, o_ref):
        zz = z_ref[...]
        t = y_ref[...] * (zz * _sigmoid(zz))
        rs = _group_rms(t)
        for g in range(SSM_GROUPS):
            sl = slice(g * gw, (g + 1) * gw)
            o_ref[:, sl] = _b(t[:, sl] * rs[g] * w_ref[:, sl])
    return _row_call(body, [y, z], [w], [(SSM_INNER, BF16)], [], 256, "gnorm_fwd")[0]


def _gnorm_bwd(dout, y, z, w):
    gw = SSM_INNER // SSM_GROUPS

    def body(d_ref, y_ref, z_ref, w_ref, dy_ref, dz_ref, dw_ref):
        zz = z_ref[...]
        yy = y_ref[...]
        sg = _sigmoid(zz)
        sz = zz * sg
        t = yy * sz
        rs = _group_rms(t)
        for g in range(SSM_GROUPS):
            sl = slice(g * gw, (g + 1) * gw)
            tn = t[:, sl] * rs[g]
            d = d_ref[:, sl]
            gg = d * w_ref[:, sl]
            dt = rs[g] * (gg - tn * jnp.mean(gg * tn, axis=-1, keepdims=True))
            dy_ref[:, sl] = dt * sz[:, sl]
            dz_ref[:, sl] = _b(dt * yy[:, sl] * (sg[:, sl] * (1.0 + zz[:, sl] * (1.0 - sg[:, sl]))))
            _acc_add(dw_ref.at[:, pl.ds(g * gw, gw)], jnp.sum(d * tn, axis=0, keepdims=True))
    return _row_call(body, [dout, y, z], [w], [(SSM_INNER, F32), (SSM_INNER, BF16)], [(1, SSM_INNER)], 256,
                     "gnorm_bwd")


def _ssm_out_epi(y, z, w):
    gw = SSM_INNER // SSM_GROUPS

    def a_fn(ri, fi, ro):
        zz = ri[1][...]
        t = ri[0][...] * (zz * _sigmoid(zz))
        rs = _group_rms(t)
        for g in range(SSM_GROUPS):
            sl = slice(g * gw, (g + 1) * gw)
            ro[0][:, sl] = _b(t[:, sl] * rs[g] * fi[0][:, sl])
        return ro[0][...]

    def fn(r, ri, fi, ro, ao, first):
        ro[1][...] = r
    return _Epi(fn, [(y, SSM_INNER, 0), (z, SSM_INNER, 0)], [w], [(SSM_INNER, BF16), (D_MODEL, F32)], a_fn=a_fn)


def _mix_out_epi(att_o, ssm_o, gl, x, b_gate, w_post, w_pre):
    def a_fn(ri, fi, ro):
        g = _sigmoid(ri[2][...] + fi[0][...])
        mi = _b(g[:, :D_MODEL] * ri[0][...] + g[:, D_MODEL:] * ri[1][...])
        ro[0][...] = mi
        return mi

    def fn(r, ri, fi, ro, ao, first):
        ro[1][...] = r
        h = ri[3][...] + _rms_vals(r, fi[1][...])
        ro[2][...] = h
        ro[3][...] = _b(_rms_vals(h, fi[2][...]))
    return _Epi(fn, [(att_o, D_MODEL, 0), (ssm_o, D_MODEL, 0), (gl, 2 * D_MODEL, 0), (x, D_MODEL, 0)],
                [b_gate, w_post, w_pre],
                [(D_MODEL, BF16), (D_MODEL, F32), (D_MODEL, F32), (D_MODEL, BF16)], a_fn=a_fn)


def _final_epi(h1, target, w_post):
    def fn(dn, ri, fi, ro, ao, first):
        w = fi[0][...]
        err = ri[0][...] + _rms_vals(dn, w) - ri[1][...]
        row = jnp.mean(err * err, axis=-1, keepdims=True)
        part = 0.5 * jnp.sum(row, axis=0, keepdims=True)
        dh = err * (1.0 / D_MODEL)
        ro[0][...] = dh
        dx, dw = _rms_bwd_vals(dn, w, dh)
        ro[1][...] = _b(dx)
        _acc_into(ao[0], jnp.broadcast_to(part, (1, LANES)), first)
        _acc_into(ao[1], dw, first)
    return _Epi(fn, [(h1, D_MODEL, 0), (target, D_MODEL, 0)], [w_post], [(D_MODEL, F32), (D_MODEL, BF16)],
                [LANES, D_MODEL])


def _mid_epi(dh2, h1, mixed, w_pre, w_post):
    def fn(df, ri, fi, ro, ao, first):
        dx, dwn = _rms_bwd_vals(ri[1][...], fi[0][...], df)
        dh1 = ri[0][...] + dx
        ro[0][...] = dh1
        dm, dwp = _rms_bwd_vals(ri[2][...], fi[1][...], dh1)
        ro[1][...] = _b(dm)
        _acc_into(ao[0], dwn, first)
        _acc_into(ao[1], dwp, first)
    return _Epi(fn, [(dh2, D_MODEL, 0), (h1, D_MODEL, 0), (mixed, D_MODEL, 0)], [w_pre, w_post],
                [(D_MODEL, F32), (D_MODEL, BF16)], [D_MODEL, D_MODEL])


def _gate_epi(att_o, ssm_o, gl, b_gate):
    def fn(d, ri, fi, ro, ao, first):
        g = _sigmoid(ri[2][...] + fi[0][...])
        ga, gs = g[:, :D_MODEL], g[:, D_MODEL:]
        ro[0][...] = _b(ga * d)
        ro[1][...] = _b(gs * d)
        dga = d * ri[0][...] * ga * (1.0 - ga)
        dgs = d * ri[1][...] * gs * (1.0 - gs)
        ro[2][:, :D_MODEL] = _b(dga)
        ro[2][:, D_MODEL:] = _b(dgs)
        _acc_into(ao[0].at[:, pl.ds(0, D_MODEL)], jnp.sum(dga, axis=0, keepdims=True), first)
        _acc_into(ao[0].at[:, pl.ds(D_MODEL, D_MODEL)], jnp.sum(dgs, axis=0, keepdims=True), first)
    return _Epi(fn, [(att_o, D_MODEL, 0), (ssm_o, D_MODEL, 0), (gl, 2 * D_MODEL, 0)], [b_gate],
                [(D_MODEL, BF16), (D_MODEL, BF16), (2 * D_MODEL, BF16)], [2 * D_MODEL])


def _first_epi(du, dh1, x, w_pre):
    def fn(r, ri, fi, ro, ao, first):
        dx, dw = _rms_bwd_vals(ri[2][...], fi[0][...], ri[0][...] + r)
        ro[0][...] = ri[1][...] + dx
        _acc_into(ao[0], dw, first)
    return _Epi(fn, [(du, D_MODEL, 0), (dh1, D_MODEL, 0), (x, D_MODEL, 0)], [w_pre], [(D_MODEL, F32)],
                [D_MODEL])


def _gnorm_epi(y, z, w):
    gw = SSM_INNER // SSM_GROUPS

    def fn(d_all, ri, fi, ro, ao, first):
        zz = ri[1][...]
        yy = ri[0][...]
        sg = _sigmoid(zz)
        sz = zz * sg
        t = yy * sz
        dws = []
        for g in range(d_all.shape[1] // gw):
            sl = slice(g * gw, (g + 1) * gw)
            tg = t[:, sl]
            r = lax.rsqrt(jnp.mean(tg * tg, axis=-1, keepdims=True) + RMS_EPS)
            tn = tg * r
            d = d_all[:, sl]
            gg = d * fi[0][:, sl]
            dt = r * (gg - tn * jnp.mean(gg * tn, axis=-1, keepdims=True))
            ro[0][:, sl] = dt * sz[:, sl]
            ro[1][:, sl] = _b(dt * yy[:, sl] * (sg[:, sl] * (1.0 + zz[:, sl] * (1.0 - sg[:, sl]))))
            dws.append(jnp.sum(d * tn, axis=0, keepdims=True))
        _acc_into(ao[0], jnp.concatenate(dws, axis=1), first)
    return _Epi(fn, [(y, SSM_INNER, 0), (z, SSM_INNER, 0)], [w], [(SSM_INNER, F32), (SSM_INNER, BF16)],
                [SSM_INNER], tiled=True)


def _to_pat(a, d):
    if d == 1:
        return a
    S, C = a.shape
    return a.reshape(S // d, d, C).transpose(1, 0, 2).reshape(S, C)


def _from_pat(a, d):
    if d == 1:
        return a
    S, C = a.shape
    return a.reshape(d, S // d, C).transpose(1, 0, 2).reshape(S, C)


def _head_col(stat, h):
    return stat[:, h:h + 1]


def _attn_fwd(q, k, v, d):
    S = q.shape[0]
    blk = ATT_BLOCK
    nblk = S // blk
    nbs = nblk // d
    slopes = _alibi_slopes(N_ATT_HEADS)
    scale = HEAD_DIM ** -0.5

    def body(q_ref, kc_ref, kp_ref, vc_ref, vp_ref, o_ref, m_ref, l_ref):
        n = pl.program_id(0)
        has_prev = (n % nbs) != 0
        ii = lax.broadcasted_iota(jnp.int32, (blk, blk), 0)
        jj = lax.broadcasted_iota(jnp.int32, (blk, blk), 1)
        dist_c = (ii - jj).astype(F32)
        dist_p = dist_c + float(blk)
        ok_c = ii >= jj
        ok_p = jnp.logical_and(jj >= ii, has_prev)
        lane = lax.broadcasted_iota(jnp.int32, (blk, LANES), 1)
        m_all = jnp.zeros((blk, LANES), F32)
        l_all = jnp.zeros((blk, LANES), F32)
        for h in range(N_ATT_HEADS):
            sl = slice(h * HEAD_DIM, (h + 1) * HEAD_DIM)
            qh = q_ref[:, sl]
            bias = slopes[h] * float(d)
            sc = jnp.where(ok_c, _dot_nt(qh, kc_ref[:, sl]) * scale - bias * dist_c, NEG_BIG)
            sp = jnp.where(ok_p, _dot_nt(qh, kp_ref[:, sl]) * scale - bias * dist_p, NEG_BIG)
            m = jnp.maximum(jnp.max(sc, axis=-1, keepdims=True), jnp.max(sp, axis=-1, keepdims=True))
            pc = jnp.exp(sc - m)
            pp = jnp.exp(sp - m)
            l = jnp.sum(pc, axis=-1, keepdims=True) + jnp.sum(pp, axis=-1, keepdims=True)
            o_ref[:, sl] = _dot(_b(pc), vc_ref[:, sl]) + _dot(_b(pp), vp_ref[:, sl])
            m_all = jnp.where(lane == h, m, m_all)
            l_all = jnp.where(lane == h, l, l_all)
        m_ref[...] = m_all
        l_ref[...] = l_all

    cur = pl.BlockSpec((blk, ATT_WIDTH), lambda n: (n, 0))
    prev = pl.BlockSpec((blk, ATT_WIDTH), lambda n: (jnp.maximum(n - 1, 0), 0))
    stat = pl.BlockSpec((blk, LANES), lambda n: (n, 0))
    return pl.pallas_call(
        body, name=f"attn_fwd_d{d}", grid=(nblk,),
        in_specs=[cur, cur, prev, cur, prev],
        out_specs=[cur, stat, stat],
        out_shape=[jax.ShapeDtypeStruct((S, ATT_WIDTH), F32), jax.ShapeDtypeStruct((S, LANES), F32),
                   jax.ShapeDtypeStruct((S, LANES), F32)],
        compiler_params=_params(("parallel",)),
    )(q, k, k, v, v)


def _attn_combine(os, ms, ls):
    def body(o1, o2, o3, m1, m2, m3, l1, l2, l3, att_ref, lse_ref):
        mm = [m1[...], m2[...], m3[...]]
        big = jnp.maximum(jnp.maximum(mm[0], mm[1]), mm[2])
        es = [jnp.exp(m - big) for m in mm]
        den = es[0] * l1[...] + es[1] * l2[...] + es[2] * l3[...]
        lse_ref[...] = big + jnp.log(den)
        inv = 1.0 / den
        for h in range(N_ATT_HEADS):
            sl = slice(h * HEAD_DIM, (h + 1) * HEAD_DIM)
            num = (_head_col(es[0], h) * o1[:, sl] + _head_col(es[1], h) * o2[:, sl]
                   + _head_col(es[2], h) * o3[:, sl])
            att_ref[:, sl] = num * _head_col(inv, h)
    return _row_call(body, list(os) + list(ms) + list(ls), [], [(ATT_WIDTH, F32), (LANES, F32)], [], 256,
                     "attn_combine")


def _attn_delta(d_att, att):
    def body(d_ref, a_ref, dl_ref, db_ref):
        dd = d_ref[...]
        prod = dd * a_ref[...]
        lane = lax.broadcasted_iota(jnp.int32, (dd.shape[0], LANES), 1)
        acc = jnp.zeros((dd.shape[0], LANES), F32)
        for h in range(N_ATT_HEADS):
            s = jnp.sum(prod[:, h * HEAD_DIM:(h + 1) * HEAD_DIM], axis=-1, keepdims=True)
            acc = jnp.where(lane == h, s, acc)
        dl_ref[...] = acc
        db_ref[...] = _b(dd)
    return _row_call(body, [d_att, att], [], [(LANES, F32), (ATT_WIDTH, BF16)], [], 512, "attn_delta")


def _attn_bwd(q, k, v, do, lse, delta, d):
    S = q.shape[0]
    blk = ATT_BLOCK
    nblk = S // blk
    nbs = nblk // d
    slopes = _alibi_slopes(N_ATT_HEADS)
    scale = HEAD_DIM ** -0.5

    def body(qc_ref, qn_ref, k_ref, v_ref, doc_ref, don_ref, lc_ref, ln_ref, dc_ref, dn_ref,
             dq_ref, dk_ref, dv_ref, carry_ref):
        n = pl.program_id(0)
        has_next = ((n + 1) % nbs) != 0

        @pl.when(n == 0)
        def _():
            carry_ref[...] = jnp.zeros_like(carry_ref)

        ii = lax.broadcasted_iota(jnp.int32, (blk, blk), 0)
        jj = lax.broadcasted_iota(jnp.int32, (blk, blk), 1)
        dist_c = (ii - jj).astype(F32)
        dist_p = dist_c + float(blk)
        ok_c = ii >= jj
        ok_p = jnp.logical_and(jj >= ii, has_next)
        for h in range(N_ATT_HEADS):
            sl = slice(h * HEAD_DIM, (h + 1) * HEAD_DIM)
            bias = slopes[h] * float(d)
            kh = k_ref[:, sl]
            vh = v_ref[:, sl]
            qh = qc_ref[:, sl]
            doh = doc_ref[:, sl]
            s = jnp.where(ok_c, _dot_nt(qh, kh) * scale - bias * dist_c - _head_col(lc_ref[...], h), NEG_BIG)
            p = jnp.exp(s)
            ds = p * (_dot_nt(doh, vh) - _head_col(dc_ref[...], h)) * scale
            pb, dsb = _b(p), _b(ds)
            dv = _dot_tn(pb, doh)
            dk = _dot_tn(dsb, qh)
            dq_ref[:, sl] = _dot(dsb, kh) + carry_ref[:, sl]
            qh = qn_ref[:, sl]
            doh = don_ref[:, sl]
            s = jnp.where(ok_p, _dot_nt(qh, kh) * scale - bias * dist_p - _head_col(ln_ref[...], h), NEG_BIG)
            p = jnp.exp(s)
            ds = p * (_dot_nt(doh, vh) - _head_col(dn_ref[...], h)) * scale
            pb, dsb = _b(p), _b(ds)
            dv_ref[:, sl] = dv + _dot_tn(pb, doh)
            dk_ref[:, sl] = dk + _dot_tn(dsb, qh)
            carry_ref[:, sl] = _dot(dsb, kh)

    cur = pl.BlockSpec((blk, ATT_WIDTH), lambda n: (n, 0))
    nxt = pl.BlockSpec((blk, ATT_WIDTH), lambda n: (jnp.minimum(n + 1, nblk - 1), 0))
    scur = pl.BlockSpec((blk, LANES), lambda n: (n, 0))
    snxt = pl.BlockSpec((blk, LANES), lambda n: (jnp.minimum(n + 1, nblk - 1), 0))
    shp = jax.ShapeDtypeStruct((S, ATT_WIDTH), F32)
    return pl.pallas_call(
        body, name=f"attn_bwd_d{d}", grid=(nblk,),
        in_specs=[cur, nxt, cur, cur, cur, nxt, scur, snxt, scur, snxt],
        out_specs=[cur, cur, cur],
        out_shape=[shp, shp, shp],
        scratch_shapes=[pltpu.VMEM((blk, ATT_WIDTH), F32)],
        compiler_params=_params(("arbitrary",)),
    )(q, q, k, v, do, do, lse, lse, delta, delta)


def _head_pair_masks(x):
    lane = lax.broadcasted_iota(jnp.int32, x.shape, 1)
    zero = jnp.zeros_like(x)
    return jnp.where(lane < HEAD_DIM, x, zero), jnp.where(lane >= HEAD_DIM, x, zero)


ATT_QUERY_ROWS = 32


def _attn_fwd2(qkv, d, comm=None):
    S = qkv.shape[0]
    blk = ATT_BLOCK
    nblk = S // blk
    nbs = nblk // d
    slopes = _alibi_slopes(N_ATT_HEADS)
    scale = HEAD_DIM ** -0.5
    n_ci = len(comm.ins) if comm else 0
    n_co = len(comm.outs) if comm else 0

    def body(*refs):
        q_ref, kc_ref, kp_ref, vc_ref, vp_ref = refs[:5]
        o_ref, m_ref, l_ref = refs[5 + n_ci:8 + n_ci]
        n = pl.program_id(0)
        if comm:
            c_args = (refs[5:5 + n_ci], refs[8 + n_ci:8 + n_ci + n_co], refs[-2], refs[-1])

            @pl.when(n == 0)
            def _():
                comm.start(*c_args)

        has_prev = (n % nbs) != 0
        ii = lax.broadcasted_iota(jnp.int32, (blk, 2 * blk), 0)
        jj = lax.broadcasted_iota(jnp.int32, (blk, 2 * blk), 1)
        dist_i = blk + ii - jj
        dist = dist_i.astype(F32)
        ok = jnp.logical_and(jnp.logical_and(dist_i >= 0, dist_i <= blk), jnp.logical_or(jj >= blk, has_prev))
        s_scr, p_scr = refs[8 + n_ci + n_co], refs[9 + n_ci + n_co]
        lane = lax.broadcasted_iota(jnp.int32, (blk, LANES), 1)
        for pr in range(N_ATT_HEADS // 2):
            sl = slice(pr * LANES, (pr + 1) * LANES)
            kcat = jnp.concatenate([kp_ref[:, sl], kc_ref[:, sl]], axis=0)
            for h, qh in zip((2 * pr, 2 * pr + 1), _head_pair_masks(q_ref[:, sl])):
                s_scr[h] = _dot_nt(qh, kcat)
        m_all = jnp.zeros((blk, LANES), F32)
        l_all = jnp.zeros((blk, LANES), F32)
        for h in range(N_ATT_HEADS):
            s = jnp.where(ok, s_scr[h] * scale - (slopes[h] * float(d)) * dist, NEG_BIG)
            m = jnp.max(s, axis=-1, keepdims=True)
            p = jnp.exp(s - m)
            l = jnp.sum(p, axis=-1, keepdims=True)
            m_all = jnp.where(lane == h, m, m_all)
            l_all = jnp.where(lane == h, l, l_all)
            p_scr[:, h * 2 * blk:(h + 1) * 2 * blk] = _b(p)
        for pr in range(N_ATT_HEADS // 2):
            sl = slice(pr * LANES, (pr + 1) * LANES)
            vmask = jnp.concatenate(
                _head_pair_masks(jnp.concatenate([vp_ref[:, sl], vc_ref[:, sl]], axis=0)), axis=0)
            o_ref[:, sl] = _dot(p_scr[:, pr * 4 * blk:(pr + 1) * 4 * blk], vmask)
        m_ref[...] = m_all
        l_ref[...] = l_all
        if comm:
            @pl.when(n == nblk - 1)
            def _():
                comm.finish(*c_args)

    cur = lambda c: pl.BlockSpec((blk, ATT_WIDTH), lambda n: (n, c))
    prev = lambda c: pl.BlockSpec((blk, ATT_WIDTH), lambda n: (jnp.maximum(n - 1, 0), c))
    stat = pl.BlockSpec((blk, LANES), lambda n: (n, 0))
    scratch = [pltpu.VMEM((N_ATT_HEADS, blk, 2 * blk), F32), pltpu.VMEM((blk, N_ATT_HEADS * 2 * blk), BF16)]
    if comm:
        scratch += [pltpu.SemaphoreType.DMA((comm.n_sems,))] * 2
        params = pltpu.CompilerParams(dimension_semantics=("arbitrary",), vmem_limit_bytes=VMEM_LIMIT,
                                      has_side_effects=True)
    else:
        params = _params(("parallel",))
    outs = pl.pallas_call(
        body, name=f"attn_fwd_d{d}", grid=(nblk,),
        in_specs=[cur(0), cur(1), prev(1), cur(2), prev(2)] + [ANY] * n_ci,
        out_specs=[cur(0), stat, stat] + [ANY] * n_co,
        out_shape=[jax.ShapeDtypeStruct((S, ATT_WIDTH), F32), jax.ShapeDtypeStruct((S, LANES), F32),
                   jax.ShapeDtypeStruct((S, LANES), F32)] + list(comm.outs if comm else []),
        scratch_shapes=scratch,
        compiler_params=params,
    )(qkv, qkv, qkv, qkv, qkv, *(comm.ins if comm else []))
    return (outs[0], outs[1], outs[2], outs[3:]) if comm else outs


def _attn_bwd2(qkv, do, lse, delta, d, comm=None):
    S = qkv.shape[0]
    blk = ATT_BLOCK
    nblk = S // blk
    nbs = nblk // d
    slopes = _alibi_slopes(N_ATT_HEADS)
    scale = HEAD_DIM ** -0.5
    n_ci = len(comm.ins) if comm else 0
    n_co = len(comm.outs) if comm else 0

    def body(*refs):
        qc_ref, qn_ref, k_ref, v_ref, doc_ref, don_ref, lc_ref, ln_ref, dc_ref, dn_ref = refs[:10]
        dq_ref, dk_ref, dv_ref = refs[10 + n_ci:13 + n_ci]
        carry_ref = refs[13 + n_ci + n_co]
        n = pl.program_id(0)
        has_next = ((n + 1) % nbs) != 0
        if comm:
            c_args = (refs[10:10 + n_ci], refs[13 + n_ci:13 + n_ci + n_co], refs[-2], refs[-1])

        @pl.when(n == 0)
        def _():
            carry_ref[...] = jnp.zeros_like(carry_ref)
            if comm:
                comm.start(*c_args)

        rr = lax.broadcasted_iota(jnp.int32, (2 * blk, blk), 0)
        jj = lax.broadcasted_iota(jnp.int32, (2 * blk, blk), 1)
        dist_i = rr - jj
        dist = dist_i.astype(F32)
        ok = jnp.logical_or(jnp.logical_and(rr < blk, dist_i >= 0),
                            jnp.logical_and(jnp.logical_and(rr >= blk, dist_i <= blk), has_next))
        s_scr, dp_scr, p_rows, ds_rows, ds_cols = refs[14 + n_ci + n_co:19 + n_ci + n_co]
        lcat = jnp.concatenate([lc_ref[...], ln_ref[...]], axis=0)
        dcat = jnp.concatenate([dc_ref[...], dn_ref[...]], axis=0)
        rows2 = 2 * blk

        def operands(pr):
            sl = slice(pr * LANES, (pr + 1) * LANES)
            qm = _head_pair_masks(jnp.concatenate([qc_ref[:, sl], qn_ref[:, sl]], axis=0))
            dom = _head_pair_masks(jnp.concatenate([doc_ref[:, sl], don_ref[:, sl]], axis=0))
            return sl, qm, dom

        for pr in range(N_ATT_HEADS // 2):
            sl, qm, dom = operands(pr)
            for h, qh, doh in zip((2 * pr, 2 * pr + 1), qm, dom):
                s_scr[h] = _dot_nt(qh, k_ref[:, sl])
                dp_scr[h] = _dot_nt(doh, v_ref[:, sl])
        for h in range(N_ATT_HEADS):
            s = jnp.where(ok, s_scr[h] * scale - (slopes[h] * float(d)) * dist - lcat[:, h:h + 1], NEG_BIG)
            p = jnp.exp(s)
            dsb = _b(p * (dp_scr[h] - dcat[:, h:h + 1]) * scale)
            p_rows[h * rows2:(h + 1) * rows2, :] = _b(p)
            ds_rows[h * rows2:(h + 1) * rows2, :] = dsb
            ds_cols[:, h * blk:(h + 1) * blk] = dsb
        for pr in range(N_ATT_HEADS // 2):
            sl, qm, dom = operands(pr)
            pair_rows = slice(pr * 2 * rows2, (pr + 1) * 2 * rows2)
            dv_ref[:, sl] = _b(_dot_tn(p_rows[pair_rows, :], jnp.concatenate(dom, axis=0)))
            dk_ref[:, sl] = _b(_dot_tn(ds_rows[pair_rows, :], jnp.concatenate(qm, axis=0)))
            dq = _dot(ds_cols[:, pr * 2 * blk:(pr + 1) * 2 * blk],
                      jnp.concatenate(_head_pair_masks(k_ref[:, sl]), axis=0))
            dq_ref[:, sl] = _b(dq[:blk] + carry_ref[:, sl])
            carry_ref[:, sl] = dq[blk:]

        if comm:
            @pl.when(n == nblk - 1)
            def _():
                comm.finish(*c_args)

    cur = lambda c: pl.BlockSpec((blk, ATT_WIDTH), lambda n: (n, c))
    nxt = lambda c: pl.BlockSpec((blk, ATT_WIDTH), lambda n: (jnp.minimum(n + 1, nblk - 1), c))
    scur = pl.BlockSpec((blk, LANES), lambda n: (n, 0))
    snxt = pl.BlockSpec((blk, LANES), lambda n: (jnp.minimum(n + 1, nblk - 1), 0))
    shp = jax.ShapeDtypeStruct((S, ATT_WIDTH), BF16)
    scratch = [pltpu.VMEM((blk, ATT_WIDTH), F32),
               pltpu.VMEM((N_ATT_HEADS, 2 * blk, blk), F32), pltpu.VMEM((N_ATT_HEADS, 2 * blk, blk), F32),
               pltpu.VMEM((N_ATT_HEADS * 2 * blk, blk), BF16), pltpu.VMEM((N_ATT_HEADS * 2 * blk, blk), BF16),
               pltpu.VMEM((2 * blk, N_ATT_HEADS * blk), BF16)]
    if comm:
        scratch += [pltpu.SemaphoreType.DMA((comm.n_sems,))] * 2
        params = pltpu.CompilerParams(dimension_semantics=("arbitrary",), vmem_limit_bytes=VMEM_LIMIT,
                                      has_side_effects=True)
    else:
        params = _params(("arbitrary",))
    outs = pl.pallas_call(
        body, name=f"attn_bwd_d{d}", grid=(nblk,),
        in_specs=[cur(0), nxt(0), cur(1), cur(2), cur(0), nxt(0), scur, snxt, scur, snxt] + [ANY] * n_ci,
        out_specs=[cur(0), cur(0), cur(0)] + [ANY] * n_co,
        out_shape=[shp, shp, shp] + list(comm.outs if comm else []),
        scratch_shapes=scratch,
        compiler_params=params,
    )(qkv, qkv, qkv, qkv, do, do, lse, lse, delta, delta, *(comm.ins if comm else []))
    return (outs[0], outs[1], outs[2], outs[3:]) if comm else outs


LAYOUT_TILE = 512
DILATED = tuple(d for d in DILATIONS if d > 1)


def _pat_spec(d, cols, col_block=0):
    return pl.BlockSpec((d, LAYOUT_TILE // d, cols), lambda i: (0, i, col_block))


def _pat_view(a, d):
    return a.reshape(d, a.shape[0] // d, a.shape[1])


def _qkv_layouts(qkv):
    S, C = qkv.shape
    t = LAYOUT_TILE

    def body(x_ref, nat_ref, *refs):
        pat_refs, slab = refs[:-1], refs[-1]
        nat_ref[...] = _b(x_ref[...])
        _to_slabs(slab, x_ref)
        for d, p_ref in zip(DILATED, pat_refs):
            _gather_pattern(p_ref, slab, d, BF16)

    outs = pl.pallas_call(
        body, name="qkv_layouts", grid=(S // t,),
        in_specs=[pl.BlockSpec((t, C), lambda i: (i, 0))],
        out_specs=[pl.BlockSpec((t, C), lambda i: (i, 0))] + [_pat_spec(d, C) for d in DILATED],
        out_shape=[jax.ShapeDtypeStruct((S, C), BF16)]
        + [jax.ShapeDtypeStruct((d, S // d, C), BF16) for d in DILATED],
        scratch_shapes=[pltpu.VMEM((C // LANES, t, LANES), F32)],
        compiler_params=_params(("parallel",)),
    )(qkv)
    return [outs[0]] + [o.reshape(S, C) for o in outs[1:]]


def _to_slabs(slab_ref, src_ref):
    for cb in range(slab_ref.shape[0]):
        slab_ref[cb] = src_ref[:, cb * LANES:(cb + 1) * LANES].astype(F32)


def _gather_pattern(dst_ref, slab_ref, d, dtype):
    t = slab_ref.shape[1]
    for cb in range(slab_ref.shape[0]):
        one = slab_ref.at[cb]
        for r in range(d):
            dst_ref[r, :, cb * LANES:(cb + 1) * LANES] = one[pl.ds(r, t // d, stride=d), :].astype(dtype)


def _scatter_pattern(slab_ref, src_ref, d, add=False):
    t = slab_ref.shape[1]
    for cb in range(slab_ref.shape[0]):
        one = slab_ref.at[cb]
        for r in range(d):
            idx = pl.ds(r, t // d, stride=d)
            val = src_ref[r, :, cb * LANES:(cb + 1) * LANES]
            if add:
                val = val + one[idx, :]
            one[idx, :] = val


def _pat_out(d, S, cols, dtype, col_tiled):
    def spec(tm, tn):
        if col_tiled:
            return pl.BlockSpec((d, tm // d, tn), lambda j, i, k: (0, i, j))
        return pl.BlockSpec((d, tm // d, cols), lambda j, i, k: (0, i, 0))
    return jax.ShapeDtypeStruct((d, S // d, cols), dtype), spec


def _qkv_epi(S):
    def fn(r, ri, fi, ro, ao, first, scr):
        ro[0][...] = _b(r)
        slab = scr[0]
        for cb in range(slab.shape[0]):
            slab[cb] = r[:, cb * LANES:(cb + 1) * LANES]
        for d, p_ref in zip(DILATED, ro[1:]):
            _gather_pattern(p_ref, slab, d, BF16)
    return _Epi(fn, row_outs=[(3 * ATT_WIDTH, BF16)] + [_pat_out(d, S, 3 * ATT_WIDTH, BF16, True) for d in DILATED],
                tiled=True, scratch=[pltpu.VMEM((ATT_WIDTH // LANES, LAYOUT_TILE, LANES), F32)])


def _attn_combine2(os, ms, ls):
    S = os[0].shape[0]
    t = LAYOUT_TILE

    def body(o1, o2, o3, m1, m2, m3, l1, l2, l3, att_ref, lse_ref, so2, so3, sm2, sm3, sl2, sl3):
        for d, src, dst in ((DILATED[0], o2, so2), (DILATED[1], o3, so3), (DILATED[0], m2, sm2),
                            (DILATED[1], m3, sm3), (DILATED[0], l2, sl2), (DILATED[1], l3, sl3)):
            _scatter_pattern(dst, src, d)
        mm = [m1[...], sm2[0], sm3[0]]
        big = jnp.maximum(jnp.maximum(mm[0], mm[1]), mm[2])
        es = [jnp.exp(m - big) for m in mm]
        den = es[0] * l1[...] + es[1] * sl2[0] + es[2] * sl3[0]
        lse_ref[...] = big + jnp.log(den)
        inv = 1.0 / den
        for h in range(N_ATT_HEADS):
            sl = slice(h * HEAD_DIM, (h + 1) * HEAD_DIM)
            cb, hl = divmod(h, 2)
            sll = slice(hl * HEAD_DIM, (hl + 1) * HEAD_DIM)
            num = (_head_col(es[0], h) * o1[:, sl] + _head_col(es[1], h) * so2[cb, :, sll]
                   + _head_col(es[2], h) * so3[cb, :, sll])
            att_ref[:, sl] = num * _head_col(inv, h)

    def specs(c):
        return [pl.BlockSpec((t, c), lambda i: (i, 0))] + [_pat_spec(d, c) for d in DILATED]

    args = [os[0]] + [_pat_view(o, d) for o, d in zip(os[1:], DILATED)]
    args += [ms[0]] + [_pat_view(m, d) for m, d in zip(ms[1:], DILATED)]
    args += [ls[0]] + [_pat_view(l, d) for l, d in zip(ls[1:], DILATED)]
    return pl.pallas_call(
        body, name="attn_combine", grid=(S // t,),
        in_specs=specs(ATT_WIDTH) + specs(LANES) + specs(LANES),
        out_specs=[pl.BlockSpec((t, ATT_WIDTH), lambda i: (i, 0)), pl.BlockSpec((t, LANES), lambda i: (i, 0))],
        out_shape=[jax.ShapeDtypeStruct((S, ATT_WIDTH), F32), jax.ShapeDtypeStruct((S, LANES), F32)],
        scratch_shapes=[pltpu.VMEM((ATT_WIDTH // LANES, t, LANES), F32)] * 2
        + [pltpu.VMEM((1, t, LANES), F32)] * 4,
        compiler_params=_params(("parallel",)),
    )(*args)


def _attn_delta2(d_att, att, lse):
    S = d_att.shape[0]
    t = LAYOUT_TILE

    def body(d_ref, a_ref, l_ref, *refs):
        out_refs, d_slab, l_slab, dl_slab = refs[:-3], refs[-3], refs[-2], refs[-1]
        dd = d_ref[...]
        prod = dd * a_ref[...]
        lane = lax.broadcasted_iota(jnp.int32, (t, LANES), 1)
        acc = jnp.zeros((t, LANES), F32)
        for h in range(N_ATT_HEADS):
            s = jnp.sum(prod[:, h * HEAD_DIM:(h + 1) * HEAD_DIM], axis=-1, keepdims=True)
            acc = jnp.where(lane == h, s, acc)
        out_refs[0][...] = _b(dd)
        out_refs[1][...] = acc
        _to_slabs(d_slab, d_ref)
        l_slab[0] = l_ref[...]
        dl_slab[0] = acc
        for k, d in enumerate(DILATED):
            db_ref, ls_ref, dl_ref = out_refs[2 + 3 * k:5 + 3 * k]
            _gather_pattern(db_ref, d_slab, d, BF16)
            _gather_pattern(ls_ref, l_slab, d, F32)
            _gather_pattern(dl_ref, dl_slab, d, F32)

    nat = lambda c: pl.BlockSpec((t, c), lambda i: (i, 0))
    out_specs = [nat(ATT_WIDTH), nat(LANES)]
    out_shape = [jax.ShapeDtypeStruct((S, ATT_WIDTH), BF16), jax.ShapeDtypeStruct((S, LANES), F32)]
    for d in DILATED:
        out_specs += [_pat_spec(d, ATT_WIDTH), _pat_spec(d, LANES), _pat_spec(d, LANES)]
        out_shape += [jax.ShapeDtypeStruct((d, S // d, ATT_WIDTH), BF16),
                      jax.ShapeDtypeStruct((d, S // d, LANES), F32),
                      jax.ShapeDtypeStruct((d, S // d, LANES), F32)]
    outs = pl.pallas_call(
        body, name="attn_delta", grid=(S // t,),
        in_specs=[nat(ATT_WIDTH), nat(ATT_WIDTH), nat(LANES)],
        out_specs=out_specs, out_shape=out_shape,
        scratch_shapes=[pltpu.VMEM((ATT_WIDTH // LANES, t, LANES), F32), pltpu.VMEM((1, t, LANES), F32),
                        pltpu.VMEM((1, t, LANES), F32)],
        compiler_params=_params(("parallel",)),
    )(d_att, att, lse)
    res = [(outs[0], lse, outs[1])]
    for k in range(len(DILATED)):
        db, ls, dl = outs[2 + 3 * k:5 + 3 * k]
        res.append((db.reshape(S, ATT_WIDTH), ls.reshape(S, LANES), dl.reshape(S, LANES)))
    return res


def _sum_qkv2(dqs, dks, dvs):
    S = dqs[0].shape[0]
    t = LAYOUT_TILE

    def body(*refs):
        o_ref, scr = refs[-2], refs[-1]
        for part in range(3):
            nat_ref, p_refs = refs[3 * part], refs[3 * part + 1:3 * part + 3]
            _to_slabs(scr, nat_ref)
            for d, p_ref in zip(DILATED, p_refs):
                _scatter_pattern(scr, p_ref, d, add=True)
            for cb in range(ATT_WIDTH // LANES):
                o_ref[:, part * ATT_WIDTH + cb * LANES:part * ATT_WIDTH + (cb + 1) * LANES] = _b(scr[cb])

    in_specs, args = [], []
    for group in (dqs, dks, dvs):
        in_specs += [pl.BlockSpec((t, ATT_WIDTH), lambda i: (i, 0))] + [_pat_spec(d, ATT_WIDTH) for d in DILATED]
        args += [group[0]] + [_pat_view(a, d) for a, d in zip(group[1:], DILATED)]
    return pl.pallas_call(
        body, name="sum_dqkv", grid=(S // t,),
        in_specs=in_specs,
        out_specs=pl.BlockSpec((t, 3 * ATT_WIDTH), lambda i: (i, 0)),
        out_shape=jax.ShapeDtypeStruct((S, 3 * ATT_WIDTH), BF16),
        scratch_shapes=[pltpu.VMEM((ATT_WIDTH // LANES, t, LANES), F32)],
        compiler_params=_params(("parallel",)),
    )(*args)


def _sum_qkv(dqs, dks, dvs):
    def body(q1, q2, q3, k1, k2, k3, v1, v2, v3, o_ref):
        o_ref[:, 0:ATT_WIDTH] = _b(q1[...] + q2[...] + q3[...])
        o_ref[:, ATT_WIDTH:2 * ATT_WIDTH] = _b(k1[...] + k2[...] + k3[...])
        o_ref[:, 2 * ATT_WIDTH:] = _b(v1[...] + v2[...] + v3[...])
    return _row_call(body, list(dqs) + list(dks) + list(dvs), [], [(3 * ATT_WIDTH, BF16)], [], 256,
                     "sum_dqkv")[0]


CONV_COLS = 1024
CONV_ROWS = 512
HALO = 8


def _conv_fwd(xbc, conv_w, conv_b):
    S, C = xbc.shape
    bs, bc = CONV_ROWS, CONV_COLS
    nr = S // bs

    def body(x_ref, halo_ref, w_ref, b_ref, o_ref, xs_ref):
        r = pl.program_id(1)
        xs_ref[pl.ds(HALO, bs), :] = x_ref[...]
        xs_ref[pl.ds(0, HALO), :] = jnp.where(r > 0, halo_ref[...], 0.0)
        pre = b_ref[...] + w_ref[3:4, :] * x_ref[...]
        for j in range(SSM_CONV - 1):
            pre = pre + w_ref[j:j + 1, :] * xs_ref[pl.ds(HALO - 3 + j, bs), :]
        o_ref[...] = pre * _sigmoid(pre)

    return pl.pallas_call(
        body, name="conv_fwd", grid=(C // bc, nr),
        in_specs=[pl.BlockSpec((bs, bc), lambda c, r: (r, c)),
                  pl.BlockSpec((HALO, bc), lambda c, r: (jnp.maximum(r * (bs // HALO) - 1, 0), c)),
                  pl.BlockSpec((SSM_CONV, bc), lambda c, r: (0, c)),
                  pl.BlockSpec((1, bc), lambda c, r: (0, c))],
        out_specs=pl.BlockSpec((bs, bc), lambda c, r: (r, c)),
        out_shape=jax.ShapeDtypeStruct((S, C), F32),
        scratch_shapes=[pltpu.VMEM((bs + HALO, bc), F32)],
        compiler_params=_params(("parallel", "arbitrary")),
    )(xbc, xbc, conv_w, conv_b)


def _conv_bwd(xbc, dact, conv_w, conv_b, col0):
    S, C = xbc.shape
    Cp = dact.shape[1]
    bs, bc = CONV_ROWS, min(CONV_COLS, Cp)
    nr = S // bs
    cb0 = col0 // bc
    last_halo = S // HALO - 1

    def body(x_ref, xp_ref, xn_ref, d_ref, dn_ref, w_ref, b_ref, dx_ref, dw_ref, db_ref,
             xs_ref, dp_ref):
        r = pl.program_id(1)
        xs_ref[pl.ds(0, HALO), :] = jnp.where(r > 0, xp_ref[...], 0.0)
        xs_ref[pl.ds(HALO, bs), :] = x_ref[...]
        xs_ref[pl.ds(HALO + bs, HALO), :] = xn_ref[...]
        ext = bs + HALO
        pre = b_ref[...] + jnp.zeros((ext, bc), F32)
        for j in range(SSM_CONV):
            pre = pre + w_ref[j:j + 1, :] * xs_ref[pl.ds(HALO - 3 + j, ext), :]
        sg = _sigmoid(pre)
        dsilu = sg * (1.0 + pre * (1.0 - sg))
        dp_ref[pl.ds(0, bs), :] = d_ref[...] * dsilu[:bs]
        dp_ref[pl.ds(bs, HALO), :] = jnp.where(r < nr - 1, dn_ref[...], 0.0) * dsilu[bs:]
        dx = jnp.zeros((bs, bc), F32)
        for j in range(SSM_CONV):
            dx = dx + w_ref[j:j + 1, :] * dp_ref[pl.ds(3 - j, bs), :]
        dx_ref[...] = _b(dx)
        dpre = dp_ref[pl.ds(0, bs), :]
        for j in range(SSM_CONV):
            part = jnp.sum(dpre * xs_ref[pl.ds(HALO - 3 + j, bs), :], axis=0, keepdims=True)

            @pl.when(r == 0)
            def _():
                dw_ref[j:j + 1, :] = part

            @pl.when(r > 0)
            def _():
                dw_ref[j:j + 1, :] += part
        part = jnp.sum(dpre, axis=0, keepdims=True)

        @pl.when(r == 0)
        def _():
            db_ref[...] = part

        @pl.when(r > 0)
        def _():
            db_ref[...] += part

    hb = bs // HALO
    return pl.pallas_call(
        body, name=f"conv_bwd_{col0}", grid=(Cp // bc, nr),
        in_specs=[pl.BlockSpec((bs, bc), lambda c, r: (r, cb0 + c)),
                  pl.BlockSpec((HALO, bc), lambda c, r: (jnp.maximum(r * hb - 1, 0), cb0 + c)),
                  pl.BlockSpec((HALO, bc), lambda c, r: (jnp.minimum((r + 1) * hb, last_halo), cb0 + c)),
                  pl.BlockSpec((bs, bc), lambda c, r: (r, c)),
                  pl.BlockSpec((HALO, bc), lambda c, r: (jnp.minimum((r + 1) * hb, last_halo), c)),
                  pl.BlockSpec((SSM_CONV, bc), lambda c, r: (0, cb0 + c)),
                  pl.BlockSpec((1, bc), lambda c, r: (0, cb0 + c))],
        out_specs=[pl.BlockSpec((bs, bc), lambda c, r: (r, c)),
                   pl.BlockSpec((SSM_CONV, bc), lambda c, r: (0, c)),
                   pl.BlockSpec((1, bc), lambda c, r: (0, c))],
        out_shape=[jax.ShapeDtypeStruct((S, Cp), BF16), jax.ShapeDtypeStruct((SSM_CONV, Cp), F32),
                   jax.ShapeDtypeStruct((1, Cp), F32)],
        scratch_shapes=[pltpu.VMEM((bs + 2 * HALO, bc), F32), pltpu.VMEM((bs + HALO, bc), F32)],
        compiler_params=_params(("parallel", "arbitrary")),
    )(xbc, xbc, xbc, dact, dact, conv_w, conv_b)


def _shift_down(x, k, top_src):
    r8 = lax.broadcasted_iota(jnp.int32, (HALO, x.shape[1]), 0)
    rolled = pltpu.roll(x, k, 0)
    top = jnp.where(r8 < k, pltpu.roll(top_src, k, 0), rolled[0:HALO])
    if x.shape[0] == HALO:
        return top
    return jnp.concatenate([top, rolled[HALO:]], axis=0)


def _shift_up(x, k, bottom_src):
    n = x.shape[0]
    r8 = lax.broadcasted_iota(jnp.int32, (HALO, x.shape[1]), 0)
    rolled = pltpu.roll(x, n - k, 0)
    bottom = jnp.where(r8 >= HALO - k, pltpu.roll(bottom_src, HALO - k, 0), rolled[n - HALO:n])
    return jnp.concatenate([rolled[:n - HALO], bottom], axis=0)


def _conv_pre(x, top_src, w_ref, b_ref):
    shifted = [x] + [_shift_down(x, k, top_src) for k in range(1, SSM_CONV)]
    pre = b_ref[...] + w_ref[SSM_CONV - 1:SSM_CONV, :] * x
    for k in range(1, SSM_CONV):
        pre = pre + w_ref[SSM_CONV - 1 - k:SSM_CONV - k, :] * shifted[k]
    return pre, shifted


def _conv_fwd2(xbc, conv_w, conv_b):
    S, C = xbc.shape
    bs, bc = CONV_ROWS, CONV_COLS
    nr = S // bs

    def body(x_ref, halo_ref, w_ref, b_ref, o_ref, pre_ref):
        r = pl.program_id(1)
        halo = jnp.where(r > 0, halo_ref[...], 0.0)
        pre, _ = _conv_pre(x_ref[...], halo, w_ref, b_ref)
        pre_ref[...] = _b(pre)
        o_ref[...] = pre * _sigmoid(pre)

    tile = pl.BlockSpec((bs, bc), lambda c, r: (r, c))
    return pl.pallas_call(
        body, name="conv_fwd", grid=(C // bc, nr),
        in_specs=[tile,
                  pl.BlockSpec((HALO, bc), lambda c, r: (jnp.maximum(r * (bs // HALO) - 1, 0), c)),
                  pl.BlockSpec((SSM_CONV, bc), lambda c, r: (0, c)),
                  pl.BlockSpec((1, bc), lambda c, r: (0, c))],
        out_specs=[tile, tile],
        out_shape=[jax.ShapeDtypeStruct((S, C), F32), jax.ShapeDtypeStruct((S, C), BF16)],
        compiler_params=_params(("parallel", "arbitrary")),
    )(xbc, xbc, conv_w, conv_b)


def _conv_bwd2(xbc, pre_all, dact, conv_w):
    S, C = xbc.shape
    bs, bc = CONV_ROWS, CONV_COLS
    nr = S // bs
    hb = bs // HALO
    last_halo = S // HALO - 1

    def dsilu(pre):
        sg = _sigmoid(pre)
        return sg * (1.0 + pre * (1.0 - sg))

    def body(x_ref, p_ref, pn_ref, d_ref, dn_ref, w_ref, dx_ref, dw_ref, db_ref):
        r = pl.program_id(1)
        x = x_ref[...]
        dpre = d_ref[...] * dsilu(p_ref[...].astype(F32))
        dpre_n = jnp.where(r < nr - 1, dn_ref[...], 0.0) * dsilu(pn_ref[...].astype(F32)[0:HALO])
        ups = [dpre] + [_shift_up(dpre, k, dpre_n) for k in range(1, SSM_CONV)]
        dx = w_ref[SSM_CONV - 1:SSM_CONV, :] * dpre
        for k in range(1, SSM_CONV):
            dx = dx + w_ref[SSM_CONV - 1 - k:SSM_CONV - k, :] * ups[k]
        dx_ref[...] = _b(dx)
        parts = [jnp.sum(x * ups[SSM_CONV - 1 - j], axis=0, keepdims=True) for j in range(SSM_CONV)]
        dbp = jnp.sum(dpre, axis=0, keepdims=True)

        @pl.when(r == 0)
        def _():
            for j in range(SSM_CONV):
                dw_ref[j:j + 1, :] = parts[j]
            db_ref[...] = dbp

        @pl.when(r > 0)
        def _():
            for j in range(SSM_CONV):
                dw_ref[j:j + 1, :] += parts[j]
            db_ref[...] += dbp

    tile = pl.BlockSpec((bs, bc), lambda c, r: (r, c))
    nxt = pl.BlockSpec((HALO, bc), lambda c, r: (jnp.minimum((r + 1) * hb, last_halo), c))
    nxt16 = pl.BlockSpec((BF16_ROWS, bc), lambda c, r: (
        jnp.minimum((r + 1) * (bs // BF16_ROWS), S // BF16_ROWS - 1), c))
    return pl.pallas_call(
        body, name="conv_bwd", grid=(C // bc, nr),
        in_specs=[tile, tile, nxt16, tile, nxt, pl.BlockSpec((SSM_CONV, bc), lambda c, r: (0, c))],
        out_specs=[tile,
                   pl.BlockSpec((SSM_CONV, bc), lambda c, r: (0, c)),
                   pl.BlockSpec((1, bc), lambda c, r: (0, c))],
        out_shape=[jax.ShapeDtypeStruct((S, C), BF16), jax.ShapeDtypeStruct((SSM_CONV, C), F32),
                   jax.ShapeDtypeStruct((1, C), F32)],
        compiler_params=_params(("parallel", "arbitrary")),
    )(xbc, pre_all, pre_all, dact, dact, conv_w)


def _softplus(x):
    return jnp.maximum(x, 0.0) + jnp.log(1.0 + jnp.exp(-jnp.abs(x)))


def _ssd_common(dtr_ref, bias_ref, a_ref, g):
    ch = SSM_CHUNK
    x = dtr_ref[...] + bias_ref[...]
    dt_all = _softplus(x)
    r = lax.broadcasted_iota(jnp.int32, (LANES, LANES), 0)
    c = lax.broadcasted_iota(jnp.int32, (LANES, LANES), 1)
    sel = jnp.where(jnp.logical_and(r == HEADS_PER_GROUP * g + c, c < HEADS_PER_GROUP), 1.0, 0.0)
    dt4 = _dot_hi(dt_all, sel)
    la4 = _dot_hi(dt_all * a_ref[...], sel)
    ii = lax.broadcasted_iota(jnp.int32, (ch, ch), 0)
    jj = lax.broadcasted_iota(jnp.int32, (ch, ch), 1)
    tril = jnp.where(ii >= jj, 1.0, 0.0)
    acs = _dot_hi(tril, la4)
    return x, sel, dt4, acs, acs.T, ii >= jj


def _row8(v):
    return jnp.broadcast_to(v, (8, v.shape[1]))


def _ssd_fwd(xact, dt_raw, dt_bias, a_neg, d_skip):
    S = xact.shape[0]
    ch = SSM_CHUNK
    nch = S // ch
    hg = HEADS_PER_GROUP
    gw = hg * SSM_HEAD_DIM
    b_off = SSM_INNER // SSM_STATE
    c_off = b_off + SSM_GROUPS

    def body(x_ref, b_ref, c_ref, dtr_ref, bias_ref, a_ref, dsk_ref, y_ref, hs_ref, h_ref):
        c = pl.program_id(0)
        g = pl.program_id(1)

        @pl.when(jnp.logical_and(c == 0, g == 0))
        def _():
            h_ref[...] = jnp.zeros_like(h_ref)

        _, sel, dt4, acs, acs_t, low = _ssd_common(dtr_ref, bias_ref, a_ref, g)
        dsk4 = _dot_hi(_row8(dsk_ref[...]), sel)
        bb = _b(b_ref[...])
        cc = _b(c_ref[...])
        cb = _dot_nt(cc, bb)
        for j in range(hg):
            sl = slice(j * SSM_HEAD_DIM, (j + 1) * SSM_HEAD_DIM)
            acol = acs[:, j:j + 1]
            arow = acs_t[j:j + 1, :]
            alast = acs[ch - 1:ch, j:j + 1]
            decay = jnp.exp(jnp.where(low, acol - arow, -jnp.inf))
            xh = x_ref[:, sl]
            xd = xh * dt4[:, j:j + 1]
            hj = h_ref[hg * g + j]
            y = _dot(_b(cb * decay), _b(xd))
            y = y + _dot_nt(cc, _b(hj)) * jnp.exp(acol)
            y_ref[:, sl] = y + dsk4[0:1, j:j + 1] * xh
            hs_ref[0, j] = hj
            st = _dot_tn(_b(xd * jnp.exp(alast - acol)), bb)
            h_ref[hg * g + j] = hj * jnp.exp(alast) + st

    small = pl.BlockSpec((1, LANES), lambda c, g: (0, 0))
    return pl.pallas_call(
        body, name="ssd_fwd", grid=(nch, SSM_GROUPS),
        in_specs=[pl.BlockSpec((ch, gw), lambda c, g: (c, g)),
                  pl.BlockSpec((ch, SSM_STATE), lambda c, g: (c, b_off + g)),
                  pl.BlockSpec((ch, SSM_STATE), lambda c, g: (c, c_off + g)),
                  pl.BlockSpec((ch, LANES), lambda c, g: (c, 0)),
                  small, small, small],
        out_specs=[pl.BlockSpec((ch, gw), lambda c, g: (c, g)),
                   pl.BlockSpec((1, hg, SSM_HEAD_DIM, SSM_STATE), lambda c, g: (c, g, 0, 0))],
        out_shape=[jax.ShapeDtypeStruct((S, SSM_INNER), F32),
                   jax.ShapeDtypeStruct((nch, SSM_HEADS, SSM_HEAD_DIM, SSM_STATE), F32)],
        scratch_shapes=[pltpu.VMEM((SSM_HEADS, SSM_HEAD_DIM, SSM_STATE), F32)],
        compiler_params=_params(("arbitrary", "arbitrary")),
    )(xact, xact, xact, dt_raw, dt_bias, a_neg, d_skip)


def _ssd_bwd(xact, dt_raw, dt_bias, a_neg, d_skip, hs, dy):
    S = xact.shape[0]
    ch = SSM_CHUNK
    nch = S // ch
    hg = HEADS_PER_GROUP
    gw = hg * SSM_HEAD_DIM
    b_off = SSM_INNER // SSM_STATE
    c_off = b_off + SSM_GROUPS

    def body(x_ref, b_ref, c_ref, dtr_ref, bias_ref, a_ref, dsk_ref, hs_ref, dy_ref,
             dx_ref, db_ref, dc_ref, ddt_ref, st_ref, dh_ref, ddt_acc):
        step = pl.program_id(0)
        g = pl.program_id(1)

        @pl.when(jnp.logical_and(step == 0, g == 0))
        def _():
            dh_ref[...] = jnp.zeros_like(dh_ref)
            st_ref[...] = jnp.zeros_like(st_ref)

        @pl.when(g == 0)
        def _():
            ddt_acc[...] = jnp.zeros_like(ddt_acc)

        xraw, sel, dt4, acs, acs_t, low = _ssd_common(dtr_ref, bias_ref, a_ref, g)
        a4 = _dot_hi(_row8(a_ref[...]), sel)[0:1, :]
        dsk4 = _dot_hi(_row8(dsk_ref[...]), sel)
        bf = b_ref[...]
        cf = c_ref[...]
        bb = _b(bf)
        cc = _b(cf)
        cb = _dot_nt(cc, bb)
        lane = lax.broadcasted_iota(jnp.int32, (ch, LANES), 1)
        rowi = lax.broadcasted_iota(jnp.int32, (ch, 1), 0)
        ones = jnp.ones((ch, LANES), F32)
        dcb = jnp.zeros((ch, ch), F32)
        dc_acc = jnp.zeros((ch, SSM_STATE), F32)
        db_acc = jnp.zeros((ch, SSM_STATE), F32)
        dacs4 = jnp.zeros((ch, LANES), F32)
        ddt4 = jnp.zeros((ch, LANES), F32)
        dd4 = jnp.zeros((1, LANES), F32)
        lane1 = lax.broadcasted_iota(jnp.int32, (1, LANES), 1)
        for j in range(hg):
            sl = slice(j * SSM_HEAD_DIM, (j + 1) * SSM_HEAD_DIM)
            acol = acs[:, j:j + 1]
            arow = acs_t[j:j + 1, :]
            alast = acs[ch - 1:ch, j:j + 1]
            decay = jnp.exp(jnp.where(low, acol - arow, -jnp.inf))
            ea = jnp.exp(acol)
            dsd = jnp.exp(alast - acol)
            cd = jnp.exp(alast)
            dtc = dt4[:, j:j + 1]
            xh = x_ref[:, sl]
            xd = xh * dtc
            xdb = _b(xd)
            hj = hs_ref[0, j]
            hjb = _b(hj)
            dhn = dh_ref[hg * g + j]
            dyj = dy_ref[:, sl]
            dyb = _b(dyj)
            lm = cb * decay
            dxh = dsk4[0:1, j:j + 1] * dyj
            dd4 = jnp.where(lane1 == j, jnp.sum(jnp.sum(dyj * xh, axis=1, keepdims=True), axis=0,
                                                keepdims=True), dd4)
            dlm = _dot_nt(dyb, xdb)
            dxd = _dot_tn(_b(lm), dyb)
            gm = dlm * lm
            dcb = dcb + dlm * decay
            dac = jnp.sum(gm, axis=1, keepdims=True) - _dot_tn_hi(gm, ones)[:, 0:1]
            zz = _dot_nt(cc, hjb)
            dzb = _b(dyj * ea)
            dac = dac + jnp.sum(dyj * zz, axis=1, keepdims=True) * ea
            dc_acc = dc_acc + _dot(dzb, hjb)
            dh_in = _dot_tn(dzb, cc)
            dsb = _b(dhn)
            ww = _dot_nt(bb, dsb)
            dxd = dxd + ww * dsd
            dds = jnp.sum(ww * xd, axis=1, keepdims=True) * dsd
            db_acc = db_acc + _dot(_b(xd * dsd), dsb)
            dac = dac - dds
            dal = (jnp.sum(dds, axis=0, keepdims=True)
                   + jnp.sum(jnp.sum(dhn * hj, axis=1, keepdims=True), axis=0, keepdims=True) * cd)
            dh_ref[hg * g + j] = dh_in + dhn * cd
            dac = dac + jnp.where(rowi == ch - 1, dal, 0.0)
            dacs4 = jnp.where(lane == j, dac, dacs4)
            dx_ref[:, sl] = dxh + dxd * dtc
            ddt4 = jnp.where(lane == j, jnp.sum(dxd * xh, axis=1, keepdims=True), ddt4)
        dcbb = _b(dcb)
        dc_ref[...] = dc_acc + _dot(dcbb, bb)
        db_ref[...] = db_acc + _dot_tn(dcbb, cc)
        ii = lax.broadcasted_iota(jnp.int32, (ch, ch), 0)
        jj = lax.broadcasted_iota(jnp.int32, (ch, ch), 1)
        triu = jnp.where(ii <= jj, 1.0, 0.0)
        dla4 = _dot_hi(triu, dacs4)
        ddt4 = ddt4 + dla4 * a4
        da4 = jnp.sum(dla4 * dt4, axis=0, keepdims=True) * a4
        sel_t = sel.T
        ddt_raw = _dot_hi(ddt4, sel_t) * _sigmoid(xraw)
        ddt_acc[...] += ddt_raw
        st_ref[0:1, :] += _dot_hi(_row8(da4), sel_t)[0:1, :]
        st_ref[1:2, :] += _dot_hi(_row8(dd4), sel_t)[0:1, :]
        st_ref[2:3, :] += jnp.sum(ddt_raw, axis=0, keepdims=True)

        @pl.when(g == SSM_GROUPS - 1)
        def _():
            ddt_ref[...] = _b(ddt_acc[...])

    small = pl.BlockSpec((1, LANES), lambda s, g: (0, 0))
    rc = lambda s: nch - 1 - s
    return pl.pallas_call(
        body, name="ssd_bwd", grid=(nch, SSM_GROUPS),
        in_specs=[pl.BlockSpec((ch, gw), lambda s, g: (rc(s), g)),
                  pl.BlockSpec((ch, SSM_STATE), lambda s, g: (rc(s), b_off + g)),
                  pl.BlockSpec((ch, SSM_STATE), lambda s, g: (rc(s), c_off + g)),
                  pl.BlockSpec((ch, LANES), lambda s, g: (rc(s), 0)),
                  small, small, small,
                  pl.BlockSpec((1, hg, SSM_HEAD_DIM, SSM_STATE), lambda s, g: (rc(s), g, 0, 0)),
                  pl.BlockSpec((ch, gw), lambda s, g: (rc(s), g))],
        out_specs=[pl.BlockSpec((ch, gw), lambda s, g: (rc(s), g)),
                   pl.BlockSpec((ch, SSM_STATE), lambda s, g: (rc(s), g)),
                   pl.BlockSpec((ch, SSM_STATE), lambda s, g: (rc(s), g)),
                   pl.BlockSpec((ch, LANES), lambda s, g: (rc(s), 0)),
                   pl.BlockSpec((8, LANES), lambda s, g: (0, 0))],
        out_shape=[jax.ShapeDtypeStruct((S, SSM_INNER), F32),
                   jax.ShapeDtypeStruct((S, SSM_GROUPS * SSM_STATE), F32),
                   jax.ShapeDtypeStruct((S, SSM_GROUPS * SSM_STATE), F32),
                   jax.ShapeDtypeStruct((S, LANES), BF16),
                   jax.ShapeDtypeStruct((8, LANES), F32)],
        scratch_shapes=[pltpu.VMEM((SSM_HEADS, SSM_HEAD_DIM, SSM_STATE), F32),
                        pltpu.VMEM((ch, LANES), F32)],
        compiler_params=_params(("arbitrary", "arbitrary")),
    )(xact, xact, xact, dt_raw, dt_bias, a_neg, d_skip, hs, dy)


GROUP_W = HEADS_PER_GROUP * SSM_HEAD_DIM
B_COL0 = SSM_INNER
C_COL0 = SSM_INNER + SSM_GROUPS * SSM_STATE


def _ssd_prep(dt_raw, dt_bias, a_neg):
    S = dt_raw.shape[0]
    ch = SSM_CHUNK
    nch = S // ch

    def body(dtr_ref, bias_ref, a_ref, dt_ref, acs_ref, acst_ref, sig_ref):
        x = dtr_ref[...] + bias_ref[...]
        lane = lax.broadcasted_iota(jnp.int32, (ch, LANES), 1)
        dt = jnp.where(lane < SSM_HEADS, _softplus(x), 0.0)
        ii = lax.broadcasted_iota(jnp.int32, (ch, ch), 0)
        jj = lax.broadcasted_iota(jnp.int32, (ch, ch), 1)
        acs = _dot_hi(jnp.where(ii >= jj, 1.0, 0.0), dt * a_ref[...])
        dt_ref[...] = dt
        acs_ref[...] = acs
        acst_ref[0] = acs.T[0:SSM_HEADS, :]
        sig_ref[...] = _sigmoid(x)

    blk = pl.BlockSpec((ch, LANES), lambda c: (c, 0))
    small = pl.BlockSpec((1, LANES), lambda c: (0, 0))
    shp = jax.ShapeDtypeStruct((S, LANES), F32)
    return pl.pallas_call(
        body, name="ssd_prep", grid=(nch,),
        in_specs=[blk, small, small],
        out_specs=[blk, blk, pl.BlockSpec((1, SSM_HEADS, ch), lambda c: (c, 0, 0)), blk],
        out_shape=[shp, shp, jax.ShapeDtypeStruct((nch, SSM_HEADS, ch), F32), shp],
        compiler_params=_params(("parallel",)),
    )(dt_raw, dt_bias, a_neg)


def _expand_heads(arr, g, rows):
    lane = lax.broadcasted_iota(jnp.int32, (rows, GROUP_W), 1) // SSM_HEAD_DIM
    h0 = HEADS_PER_GROUP * g
    out = jnp.broadcast_to(arr[:, h0:h0 + 1], (rows, GROUP_W))
    for j in range(1, HEADS_PER_GROUP):
        out = jnp.where(lane == j, arr[:, h0 + j:h0 + j + 1], out)
    return out


def _seg_matrix(k, lanes_per_head, h0):
    r = lax.broadcasted_iota(jnp.int32, (k, LANES), 0)
    c = lax.broadcasted_iota(jnp.int32, (k, LANES), 1)
    return jnp.where(c == h0 + r // lanes_per_head, 1.0, 0.0).astype(BF16)


def _seg_dot(t, e):
    hi = _b(t)
    lo = _b(t - hi.astype(F32))
    return _dot(hi, e) + _dot(lo, e)


def _head_sums(t, e, rows):
    if rows >= 8:
        return _seg_dot(t, e)
    return _seg_dot(jnp.broadcast_to(t, (8, t.shape[1])), e)[0:rows]


def _pair_masks(x):
    lane = lax.broadcasted_iota(jnp.int32, x.shape, 1)
    zero = jnp.zeros_like(x)
    return jnp.where(lane < SSM_HEAD_DIM, x, zero), jnp.where(lane >= SSM_HEAD_DIM, x, zero)


def _ssd_fwd2(xact, dt, acs, acst, dsk_e):
    S = xact.shape[0]
    ch = SSM_CHUNK
    nch = S // ch

    def body(x_ref, dt_ref, acs_ref, acst_ref, dsk_ref, y_ref, hs_ref, h_ref):
        c = pl.program_id(0)

        @pl.when(c == 0)
        def _():
            h_ref[...] = jnp.zeros_like(h_ref)

        dt_all = dt_ref[...]
        acs_all = acs_ref[...]
        acst_all = acst_ref[0]
        alast = acs_all[ch - 1:ch, :]
        eacs = jnp.exp(acs_all)
        wd_all = dt_all * jnp.exp(alast - acs_all)
        dtt = dt_all.T
        cd_all = jnp.exp(alast)
        ii = lax.broadcasted_iota(jnp.int32, (ch, ch), 0)
        jj = lax.broadcasted_iota(jnp.int32, (ch, ch), 1)
        low = ii >= jj
        for g in range(SSM_GROUPS):
            xs = x_ref[:, g * GROUP_W:(g + 1) * GROUP_W]
            bb = _b(x_ref[:, B_COL0 + g * SSM_STATE:B_COL0 + (g + 1) * SSM_STATE])
            cc = _b(x_ref[:, C_COL0 + g * SSM_STATE:C_COL0 + (g + 1) * SSM_STATE])
            cb = _dot_nt(cc, bb)
            xsb = _b(xs)
            ht = h_ref[g]
            rest = (_dot(cc, _b(ht)) * _expand_heads(eacs, g, ch)
                    + dsk_ref[:, g * GROUP_W:(g + 1) * GROUP_W] * xs)
            for p in range(HEADS_PER_GROUP // 2):
                lms = []
                for h in (HEADS_PER_GROUP * g + 2 * p, HEADS_PER_GROUP * g + 2 * p + 1):
                    diff = acs_all[:, h:h + 1] - acst_all[h:h + 1, :]
                    lms.append(_b(cb * jnp.exp(jnp.where(low, diff, -jnp.inf)) * dtt[h:h + 1, :]))
                xa, xb = _pair_masks(xsb[:, p * LANES:(p + 1) * LANES])
                yp = _dot(jnp.concatenate(lms, axis=1), jnp.concatenate([xa, xb], axis=0))
                y_ref[:, g * GROUP_W + p * LANES:g * GROUP_W + (p + 1) * LANES] = (
                    yp + rest[:, p * LANES:(p + 1) * LANES])
            hs_ref[0, g] = ht
            st = _dot_tn(bb, _b(xs * _expand_heads(wd_all, g, ch)))
            h_ref[g] = ht * _expand_heads(cd_all, g, 1) + st

    blk = pl.BlockSpec((ch, LANES), lambda c: (c, 0))
    return pl.pallas_call(
        body, name="ssd_fwd", grid=(nch,),
        in_specs=[pl.BlockSpec((ch, CONV_DIM), lambda c: (c, 0)), blk, blk,
                  pl.BlockSpec((1, SSM_HEADS, ch), lambda c: (c, 0, 0)),
                  pl.BlockSpec((1, SSM_INNER), lambda c: (0, 0))],
        out_specs=[pl.BlockSpec((ch, SSM_INNER), lambda c: (c, 0)),
                   pl.BlockSpec((1, SSM_GROUPS, SSM_STATE, GROUP_W), lambda c: (c, 0, 0, 0))],
        out_shape=[jax.ShapeDtypeStruct((S, SSM_INNER), F32),
                   jax.ShapeDtypeStruct((nch, SSM_GROUPS, SSM_STATE, GROUP_W), F32)],
        scratch_shapes=[pltpu.VMEM((SSM_GROUPS, SSM_STATE, GROUP_W), F32)],
        compiler_params=_params(("arbitrary",)),
    )(xact, dt, acs, acst, dsk_e)


def _ssd_fwd3(xact, dt, acs, acst, dsk_e):
    S = xact.shape[0]
    ch = SSM_CHUNK
    nch = S // ch
    ng, hg = SSM_GROUPS, HEADS_PER_GROUP
    nbc = SSM_GROUPS * SSM_STATE

    def body(x_ref, dt_ref, acs_ref, acst_ref, dsk_ref, y_ref, hs_ref,
             h_ref, e_ea, xdb_s, xddb_s, bcb_s, cb_s, zz_s, st_s, lmb_s):
        c = pl.program_id(0)

        @pl.when(c == 0)
        def _():
            h_ref[...] = jnp.zeros_like(h_ref)

        dt_all = dt_ref[...]
        acs_all = acs_ref[...]
        alast = acs_all[ch - 1:ch, :]
        eacs = jnp.exp(acs_all)
        dsd_all = jnp.exp(alast - acs_all)
        cd_all = jnp.exp(alast)
        ii = lax.broadcasted_iota(jnp.int32, (ch, ch), 0)
        jj = lax.broadcasted_iota(jnp.int32, (ch, ch), 1)
        low = ii >= jj
        gsl = [slice(g * GROUP_W, (g + 1) * GROUP_W) for g in range(ng)]
        bcb_s[...] = _b(x_ref[:, B_COL0:])
        for g in range(ng):
            xd = x_ref[:, gsl[g]] * _expand_heads(dt_all, g, ch)
            xdb_s[:, gsl[g]] = _b(xd)
            xddb_s[:, gsl[g]] = _b(xd * _expand_heads(dsd_all, g, ch))
            e_ea[:, gsl[g]] = _expand_heads(eacs, g, ch)
        for g in range(ng):
            bb = bcb_s[:, g * SSM_STATE:(g + 1) * SSM_STATE]
            cc = bcb_s[:, nbc + g * SSM_STATE:nbc + (g + 1) * SSM_STATE]
            cb_s[g] = _dot_nt(cc, bb)
            zz_s[:, gsl[g]] = _dot(cc, _b(h_ref[g]))
            st_s[g] = _dot_tn(bb, xddb_s[:, gsl[g]])
        for g in range(ng):
            cb = cb_s[g]
            for j in range(hg):
                h = hg * g + j
                diff = acs_all[:, h:h + 1] - acst_ref[0, h:h + 1, :]
                lmb_s[:, h * ch:(h + 1) * ch] = _b(cb * jnp.exp(jnp.where(low, diff, -jnp.inf)))
            ht = h_ref[g]
            hs_ref[0, g] = ht
            h_ref[g] = ht * _expand_heads(cd_all, g, 1) + st_s[g]
        for g in range(ng):
            for p in range(hg // 2):
                h0 = hg * g + 2 * p
                sl = slice(g * GROUP_W + p * LANES, g * GROUP_W + (p + 1) * LANES)
                yp = _dot(lmb_s[:, h0 * ch:(h0 + 2) * ch], jnp.concatenate(_pair_masks(xdb_s[:, sl]), axis=0))
                y_ref[:, sl] = yp + zz_s[:, sl] * e_ea[:, sl] + dsk_ref[:, sl] * x_ref[:, sl]

    blk = pl.BlockSpec((ch, LANES), lambda c: (c, 0))
    wide = lambda dt_: pltpu.VMEM((ch, SSM_INNER), dt_)
    return pl.pallas_call(
        body, name="ssd_fwd", grid=(nch,),
        in_specs=[pl.BlockSpec((ch, CONV_DIM), lambda c: (c, 0)), blk, blk,
                  pl.BlockSpec((1, SSM_HEADS, ch), lambda c: (c, 0, 0)),
                  pl.BlockSpec((1, SSM_INNER), lambda c: (0, 0))],
        out_specs=[pl.BlockSpec((ch, SSM_INNER), lambda c: (c, 0)),
                   pl.BlockSpec((1, SSM_GROUPS, SSM_STATE, GROUP_W), lambda c: (c, 0, 0, 0))],
        out_shape=[jax.ShapeDtypeStruct((S, SSM_INNER), F32),
                   jax.ShapeDtypeStruct((nch, SSM_GROUPS, SSM_STATE, GROUP_W), F32)],
        scratch_shapes=[pltpu.VMEM((ng, SSM_STATE, GROUP_W), F32), wide(F32), wide(BF16), wide(BF16), wide(BF16),
                        pltpu.VMEM((ng, ch, ch), F32), wide(F32), pltpu.VMEM((ng, SSM_STATE, GROUP_W), F32),
                        pltpu.VMEM((ch, SSM_HEADS * ch), BF16)],
        compiler_params=_params(("arbitrary",)),
    )(xact, dt, acs, acst, dsk_e)


def _ssd_bwd2(xact, dt, acs, acst, sig, a_neg, dsk_e, hs, dy):
    S = xact.shape[0]
    ch = SSM_CHUNK
    nch = S // ch

    def body(x_ref, dt_ref, acs_ref, acst_ref, sig_ref, a_ref, dsk_ref, hs_ref, dy_ref,
             dx_ref, ddt_ref, st_ref, dh_ref, rows_ref):
        step = pl.program_id(0)

        @pl.when(step == 0)
        def _():
            dh_ref[...] = jnp.zeros_like(dh_ref)
            st_ref[...] = jnp.zeros_like(st_ref)
            rows_ref[...] = jnp.zeros_like(rows_ref)

        dt_all = dt_ref[...]
        acs_all = acs_ref[...]
        acst_all = acst_ref[0]
        alast = acs_all[ch - 1:ch, :]
        eacs = jnp.exp(acs_all)
        dsd_all = jnp.exp(alast - acs_all)
        cd_all = jnp.exp(alast)
        ii = lax.broadcasted_iota(jnp.int32, (ch, ch), 0)
        jj = lax.broadcasted_iota(jnp.int32, (ch, ch), 1)
        low = ii >= jj
        lane = lax.broadcasted_iota(jnp.int32, (ch, LANES), 1)
        cols = jnp.zeros((ch, LANES), F32)
        ddt = jnp.zeros((ch, LANES), F32)
        dal = jnp.zeros((1, LANES), F32)
        ddsk = jnp.zeros((1, LANES), F32)
        for g in range(SSM_GROUPS):
            xs = x_ref[:, g * GROUP_W:(g + 1) * GROUP_W]
            bb = _b(x_ref[:, B_COL0 + g * SSM_STATE:B_COL0 + (g + 1) * SSM_STATE])
            cc = _b(x_ref[:, C_COL0 + g * SSM_STATE:C_COL0 + (g + 1) * SSM_STATE])
            cb = _dot_nt(cc, bb)
            dt_e = _expand_heads(dt_all, g, ch)
            ea_e = _expand_heads(eacs, g, ch)
            dsd_e = _expand_heads(dsd_all, g, ch)
            cd_e = _expand_heads(cd_all, g, 1)
            xd = xs * dt_e
            xdb = _b(xd)
            dyg = dy_ref[:, g * GROUP_W:(g + 1) * GROUP_W]
            dyb = _b(dyg)
            ht = hs_ref[0, g]
            htb = _b(ht)
            dhn = dh_ref[g]
            dhnb = _b(dhn)
            zz = _dot(cc, htb)
            dzb = _b(dyg * ea_e)
            d_c = _dot_nt(dzb, htb)
            dh_in = _dot_tn(cc, dzb)
            ww = _dot(bb, dhnb)
            xdd = xd * dsd_e
            d_b = _dot_nt(_b(xdd), dhnb)
            t2 = ww * xdd
            e_g = _seg_matrix(GROUP_W, SSM_HEAD_DIM, HEADS_PER_GROUP * g)
            cols = cols + _head_sums(dyg * zz * ea_e - t2, e_g, ch)
            dal = dal + _head_sums(jnp.sum(t2, axis=0, keepdims=True), e_g, 1) + cd_all * _head_sums(
                jnp.sum(dhn * ht, axis=0, keepdims=True), e_g, 1)
            dh_ref[g] = dh_in + dhn * cd_e
            ddsk = ddsk + _head_sums(jnp.sum(dyg * xs, axis=0, keepdims=True), e_g, 1)
            dxd_rest = ww * dsd_e
            dcb = jnp.zeros((ch, ch), F32)
            for p in range(HEADS_PER_GROUP // 2):
                dya, dyb2 = _pair_masks(dyb[:, p * LANES:(p + 1) * LANES])
                xp = xdb[:, p * LANES:(p + 1) * LANES]
                lms, gms = [], []
                for h, dyh in ((HEADS_PER_GROUP * g + 2 * p, dya), (HEADS_PER_GROUP * g + 2 * p + 1, dyb2)):
                    diff = acs_all[:, h:h + 1] - acst_all[h:h + 1, :]
                    decay = jnp.exp(jnp.where(low, diff, -jnp.inf))
                    lm = cb * decay
                    dlm = _dot_nt(dyh, xp)
                    gm = dlm * lm
                    dcb = dcb + dlm * decay
                    rows_ref[h:h + 1, :] = jnp.sum(gm, axis=0, keepdims=True)
                    lms.append(_b(lm))
                    gms.append(gm)
                h0 = HEADS_PER_GROUP * g + 2 * p
                cols = cols + _head_sums(jnp.concatenate(gms, axis=1), _seg_matrix(2 * ch, ch, h0), ch)
                dxd = _dot_tn(jnp.concatenate(lms, axis=0), jnp.concatenate([dya, dyb2], axis=0))
                dxd = dxd + dxd_rest[:, p * LANES:(p + 1) * LANES]
                sl = slice(g * GROUP_W + p * LANES, g * GROUP_W + (p + 1) * LANES)
                dx_ref[:, sl] = (dsk_ref[:, sl] * dyg[:, p * LANES:(p + 1) * LANES]
                                 + dxd * dt_e[:, p * LANES:(p + 1) * LANES])
                ddt = ddt + _head_sums(dxd * xs[:, p * LANES:(p + 1) * LANES],
                                       _seg_matrix(LANES, SSM_HEAD_DIM, h0), ch)
            dcbb = _b(dcb)
            dx_ref[:, C_COL0 + g * SSM_STATE:C_COL0 + (g + 1) * SSM_STATE] = d_c + _dot(dcbb, bb)
            dx_ref[:, B_COL0 + g * SSM_STATE:B_COL0 + (g + 1) * SSM_STATE] = d_b + _dot_tn(dcbb, cc)
        rowi = lax.broadcasted_iota(jnp.int32, (ch, 1), 0)
        dacs = cols - rows_ref[...].T + jnp.where(rowi == ch - 1, dal, 0.0)
        dla = _dot_hi(jnp.where(ii <= jj, 1.0, 0.0), dacs)
        a_row = a_ref[...]
        ddt_raw = (ddt + dla * a_row) * sig_ref[...]
        ddt_ref[...] = _b(ddt_raw)
        st_ref[0:1, :] += jnp.sum(dla * dt_all, axis=0, keepdims=True) * a_row
        st_ref[1:2, :] += ddsk
        st_ref[2:3, :] += jnp.sum(ddt_raw, axis=0, keepdims=True)

    rc = lambda s: nch - 1 - s
    blk = pl.BlockSpec((ch, LANES), lambda s: (rc(s), 0))
    return pl.pallas_call(
        body, name="ssd_bwd", grid=(nch,),
        in_specs=[pl.BlockSpec((ch, CONV_DIM), lambda s: (rc(s), 0)), blk, blk,
                  pl.BlockSpec((1, SSM_HEADS, ch), lambda s: (rc(s), 0, 0)), blk,
                  pl.BlockSpec((1, LANES), lambda s: (0, 0)),
                  pl.BlockSpec((1, SSM_INNER), lambda s: (0, 0)),
                  pl.BlockSpec((1, SSM_GROUPS, SSM_STATE, GROUP_W), lambda s: (rc(s), 0, 0, 0)),
                  pl.BlockSpec((ch, SSM_INNER), lambda s: (rc(s), 0))],
        out_specs=[pl.BlockSpec((ch, CONV_DIM), lambda s: (rc(s), 0)), blk,
                   pl.BlockSpec((8, LANES), lambda s: (0, 0))],
        out_shape=[jax.ShapeDtypeStruct((S, CONV_DIM), F32), jax.ShapeDtypeStruct((S, LANES), BF16),
                   jax.ShapeDtypeStruct((8, LANES), F32)],
        scratch_shapes=[pltpu.VMEM((SSM_GROUPS, SSM_STATE, GROUP_W), F32), pltpu.VMEM((LANES, ch), F32)],
        compiler_params=_params(("arbitrary",)),
    )(xact, dt, acs, acst, sig, a_neg, dsk_e, hs, dy)


def _ssd_bwd3(xact, dt, acs, acst, sig, a_neg, dsk_e, hs, dy):
    S = xact.shape[0]
    ch = SSM_CHUNK
    nch = S // ch
    ng, hg = SSM_GROUPS, HEADS_PER_GROUP
    nbc = SSM_GROUPS * SSM_STATE

    def body(x_ref, dt_ref, acs_ref, acst_ref, sig_ref, a_ref, dsk_ref, hs_ref, dy_ref,
             dx_ref, ddt_ref, st_ref,
             dh_ref, rows_ref, e_dt, e_ea, e_dsd, xdb_s, xddb_s, dzb_s, bcb_s, cb_s, zz_s, ww_s, dc1_s, db1_s,
             dhin_s, dlm_s, lmb_s, gm_s, dcbb_s, t_s, dxd_s, prod_s, csum_s):
        step = pl.program_id(0)

        @pl.when(step == 0)
        def _():
            dh_ref[...] = jnp.zeros_like(dh_ref)
            st_ref[...] = jnp.zeros_like(st_ref)
            rows_ref[...] = jnp.zeros_like(rows_ref)

        dt_all = dt_ref[...]
        acs_all = acs_ref[...]
        alast = acs_all[ch - 1:ch, :]
        eacs = jnp.exp(acs_all)
        dsd_all = jnp.exp(alast - acs_all)
        cd_all = jnp.exp(alast)
        ii = lax.broadcasted_iota(jnp.int32, (ch, ch), 0)
        jj = lax.broadcasted_iota(jnp.int32, (ch, ch), 1)
        low = ii >= jj
        gsl = [slice(g * GROUP_W, (g + 1) * GROUP_W) for g in range(ng)]
        psl = [[slice(g * GROUP_W + p * LANES, g * GROUP_W + (p + 1) * LANES) for p in range(hg // 2)]
               for g in range(ng)]
        seg = [_seg_matrix(GROUP_W, SSM_HEAD_DIM, hg * g) for g in range(ng)]

        def bc(g):
            return (bcb_s[:, g * SSM_STATE:(g + 1) * SSM_STATE],
                    bcb_s[:, nbc + g * SSM_STATE:nbc + (g + 1) * SSM_STATE])

        def dy_pair(g, p):
            return _pair_masks(_b(dy_ref[:, psl[g][p]]))

        bcb_s[...] = _b(x_ref[:, B_COL0:])
        for g in range(ng):
            dt_e = _expand_heads(dt_all, g, ch)
            ea_e = _expand_heads(eacs, g, ch)
            dsd_e = _expand_heads(dsd_all, g, ch)
            e_dt[:, gsl[g]] = dt_e
            e_ea[:, gsl[g]] = ea_e
            e_dsd[:, gsl[g]] = dsd_e
            xd = x_ref[:, gsl[g]] * dt_e
            xdb_s[:, gsl[g]] = _b(xd)
            xddb_s[:, gsl[g]] = _b(xd * dsd_e)
            dzb_s[:, gsl[g]] = _b(dy_ref[:, gsl[g]] * ea_e)
        for g in range(ng):
            bb, cc = bc(g)
            htb = _b(hs_ref[0, g])
            dhnb = _b(dh_ref[g])
            cb_s[g] = _dot_nt(cc, bb)
            zz_s[:, gsl[g]] = _dot(cc, htb)
            ww_s[:, gsl[g]] = _dot(bb, dhnb)
            dc1_s[g] = _dot_nt(dzb_s[:, gsl[g]], htb)
            db1_s[g] = _dot_nt(xddb_s[:, gsl[g]], dhnb)
            dhin_s[g] = _dot_tn(cc, dzb_s[:, gsl[g]])
            for p in range(hg // 2):
                xp = xdb_s[:, psl[g][p]]
                for q, dyh in enumerate(dy_pair(g, p)):
                    dlm_s[hg * g + 2 * p + q] = _dot_nt(dyh, xp)
        for g in range(ng):
            cb = cb_s[g]
            dcb = jnp.zeros((ch, ch), F32)
            for j in range(hg):
                h = hg * g + j
                diff = acs_all[:, h:h + 1] - acst_ref[0, h:h + 1, :]
                decay = jnp.exp(jnp.where(low, diff, -jnp.inf))
                lm = cb * decay
                dlm = dlm_s[h]
                gm = dlm * lm
                dcb = dcb + dlm * decay
                rows_ref[h:h + 1, :] = jnp.sum(gm, axis=0, keepdims=True)
                lmb_s[h * ch:(h + 1) * ch, :] = _b(lm)
                gm_s[:, h * ch:(h + 1) * ch] = gm
            dcbb_s[g] = _b(dcb)
            xs = x_ref[:, gsl[g]]
            dyg = dy_ref[:, gsl[g]]
            ww = ww_s[:, gsl[g]]
            dsd_e = e_dsd[:, gsl[g]]
            t2 = ww * (xs * e_dt[:, gsl[g]] * dsd_e)
            t_s[:, gsl[g]] = dyg * zz_s[:, gsl[g]] * e_ea[:, gsl[g]] - t2
            dhn = dh_ref[g]
            csum_s[0:1, gsl[g]] = jnp.sum(t2, axis=0, keepdims=True)
            csum_s[1:2, gsl[g]] = jnp.sum(dhn * hs_ref[0, g], axis=0, keepdims=True)
            csum_s[2:3, gsl[g]] = jnp.sum(dyg * xs, axis=0, keepdims=True)
            dh_ref[g] = dhin_s[g] + dhn * _expand_heads(cd_all, g, 1)
            dxd_s[:, gsl[g]] = ww * dsd_e
        cols = jnp.zeros((ch, LANES), F32)
        for g in range(ng):
            bb, cc = bc(g)
            dcbb = dcbb_s[g]
            dx_ref[:, C_COL0 + g * SSM_STATE:C_COL0 + (g + 1) * SSM_STATE] = dc1_s[g] + _dot(dcbb, bb)
            dx_ref[:, B_COL0 + g * SSM_STATE:B_COL0 + (g + 1) * SSM_STATE] = db1_s[g] + _dot_tn(dcbb, cc)
            cols = cols + _head_sums(t_s[:, gsl[g]], seg[g], ch)
            for p in range(hg // 2):
                h0 = hg * g + 2 * p
                dxd_s[:, psl[g][p]] += _dot_tn(lmb_s[h0 * ch:(h0 + 2) * ch, :],
                                               jnp.concatenate(dy_pair(g, p), axis=0))
                cols = cols + _head_sums(gm_s[:, h0 * ch:(h0 + 2) * ch], _seg_matrix(2 * ch, ch, h0), ch)
        for g in range(ng):
            dxd = dxd_s[:, gsl[g]]
            xs = x_ref[:, gsl[g]]
            dx_ref[:, gsl[g]] = dsk_ref[:, gsl[g]] * dy_ref[:, gsl[g]] + dxd * e_dt[:, gsl[g]]
            prod_s[:, gsl[g]] = dxd * xs
        ddt = jnp.zeros((ch, LANES), F32)
        dal = jnp.zeros((1, LANES), F32)
        ddsk = jnp.zeros((1, LANES), F32)
        for g in range(ng):
            ddt = ddt + _head_sums(prod_s[:, gsl[g]], seg[g], ch)
            dal = (dal + _head_sums(csum_s[0:1, gsl[g]], seg[g], 1)
                   + cd_all * _head_sums(csum_s[1:2, gsl[g]], seg[g], 1))
            ddsk = ddsk + _head_sums(csum_s[2:3, gsl[g]], seg[g], 1)
        rowi = lax.broadcasted_iota(jnp.int32, (ch, 1), 0)
        dacs = cols - rows_ref[...].T + jnp.where(rowi == ch - 1, dal, 0.0)
        dla = _dot_hi(jnp.where(ii <= jj, 1.0, 0.0), dacs)
        a_row = a_ref[...]
        ddt_raw = (ddt + dla * a_row) * sig_ref[...]
        ddt_ref[...] = _b(ddt_raw)
        st_ref[0:1, :] += jnp.sum(dla * dt_all, axis=0, keepdims=True) * a_row
        st_ref[1:2, :] += ddsk
        st_ref[2:3, :] += jnp.sum(ddt_raw, axis=0, keepdims=True)

    rc = lambda s: nch - 1 - s
    blk = pl.BlockSpec((ch, LANES), lambda s: (rc(s), 0))
    wide = lambda dt_: pltpu.VMEM((ch, SSM_INNER), dt_)
    sq = lambda n, dt_: pltpu.VMEM((n, ch, ch), dt_)
    scratch = [pltpu.VMEM((ng, SSM_STATE, GROUP_W), F32), pltpu.VMEM((LANES, ch), F32),
               wide(F32), wide(F32), wide(F32),
               wide(BF16), wide(BF16), wide(BF16), wide(BF16),
               sq(ng, F32), wide(F32), wide(F32), sq(ng, F32), sq(ng, F32),
               pltpu.VMEM((ng, SSM_STATE, GROUP_W), F32),
               sq(SSM_HEADS, F32),
               pltpu.VMEM((SSM_HEADS * ch, ch), BF16),
               pltpu.VMEM((ch, SSM_HEADS * ch), F32),
               sq(ng, BF16), wide(F32), wide(F32), wide(F32),
               pltpu.VMEM((8, SSM_INNER), F32)]
    return pl.pallas_call(
        body, name="ssd_bwd", grid=(nch,),
        in_specs=[pl.BlockSpec((ch, CONV_DIM), lambda s: (rc(s), 0)), blk, blk,
                  pl.BlockSpec((1, SSM_HEADS, ch), lambda s: (rc(s), 0, 0)), blk,
                  pl.BlockSpec((1, LANES), lambda s: (0, 0)),
                  pl.BlockSpec((1, SSM_INNER), lambda s: (0, 0)),
                  pl.BlockSpec((1, SSM_GROUPS, SSM_STATE, GROUP_W), lambda s: (rc(s), 0, 0, 0)),
                  pl.BlockSpec((ch, SSM_INNER), lambda s: (rc(s), 0))],
        out_specs=[pl.BlockSpec((ch, CONV_DIM), lambda s: (rc(s), 0)), blk,
                   pl.BlockSpec((8, LANES), lambda s: (0, 0))],
        out_shape=[jax.ShapeDtypeStruct((S, CONV_DIM), F32), jax.ShapeDtypeStruct((S, LANES), BF16),
                   jax.ShapeDtypeStruct((8, LANES), F32)],
        scratch_shapes=scratch,
        compiler_params=_params(("arbitrary",)),
    )(xact, dt, acs, acst, sig, a_neg, dsk_e, hs, dy)


def _pad_lanes(v, n=LANES):
    return jnp.pad(v, ((0, 0), (0, n - v.shape[1])))


def _local_step(x, target, w, ex=None):
    offs = np.cumsum((0,) + IN_SPLITS)
    wt_in = w["w_in_t"]
    w_qkv = wt_in[offs[0]:offs[3]]
    w_z = wt_in[offs[3]:offs[4]]
    w_xbc = wt_in[offs[4]:offs[5]]
    w_dt = jnp.pad(wt_in[offs[5]:offs[6]], ((0, LANES - SSM_HEADS), (0, 0)))
    w_g = wt_in[offs[6]:offs[7]]
    dt_bias = _pad_lanes(w["dt_bias"])
    a_neg = _pad_lanes(-jnp.exp(w["a_log"]))
    d_skip = _pad_lanes(w["d_skip"])

    u = _rms_fwd(x, w["norm_mix_pre_w"])
    if ex is None:
        xbc = _mm_nn(u, w_xbc, F32, "proj_xbc", tb=True)
    else:
        xbc, got = _mm_nn(u, w_xbc, F32, "proj_xbc", comm=_gather_comm([ex.mine[REST_EARLY]]), tb=True)
        w = {**w, **ex.rest_weights(got[0], REST_EARLY)}
    n_tok = x.shape[0]
    qkv_outs, _ = _mm_epi(u, w_qkv, _qkv_epi(n_tok), "proj_qkv", tb=True, tn=ATT_WIDTH)
    z = _mm_nn(u, w_z, F32, "proj_z", tb=True)
    dt_raw = _mm_nn(u, w_dt, F32, "proj_dt", tb=True)
    gl = _mm_nn(u, w_g, F32, "proj_gate", tb=True)

    pats = [qkv_outs[0]] + [o.reshape(n_tok, 3 * ATT_WIDTH) for o in qkv_outs[1:]]
    os_, ms_, ls_ = [], [], []
    for i, (d, qkv_p) in enumerate(zip(DILATIONS, pats)):
        if ex is not None and i < len(REST_LATE):
            o, m, l, got = _attn_fwd2(qkv_p, d, comm=_gather_comm([ex.mine[REST_LATE[i]]]))
            w = {**w, **ex.rest_weights(got[0], REST_LATE[i])}
        else:
            o, m, l = _attn_fwd2(qkv_p, d)
        os_.append(o)
        ms_.append(m)
        ls_.append(l)
    att, lse = _attn_combine2(os_, ms_, ls_)
    att_o = _mm_nn(att, w["w_att_proj"], F32, "att_proj")

    xact, conv_pre = _conv_fwd2(xbc, w["conv_w"], w["conv_b"])
    dsk_e = jnp.repeat(w["d_skip"], SSM_HEAD_DIM, axis=1)
    dt, acs, acst, sig = _ssd_prep(dt_raw, dt_bias, a_neg)
    y_ssd, hs = _ssd_fwd2(xact, dt, acs, acst, dsk_e)
    (ssm_y, ssm_o), _ = _mm_epi(None, w["w_ssm_proj"], _ssm_out_epi(y_ssd, z, w["ssm_norm_w"]), "ssm_proj")

    (mi, mixed, h1, f), _ = _mm_epi(None, w["w_out"], _mix_out_epi(
        att_o, ssm_o, gl, x, w["b_gate"], w["norm_mix_post_w"], w["norm_ffn_pre_w"]), "out_proj")
    r_up, act = _mm_nn(f, w["w_up"], BF16, "ffn_up", mode="relu2")
    (dh2, d_down, loss, g_ffn_post), _ = _mm_epi(
        act, w["w_down"], _final_epi(h1, target, w["norm_ffn_post_w"]), "ffn_down")

    g = {"norm_ffn_post_w": g_ffn_post}
    g["w_down"] = _mm_tn(act, d_down, "dw_down")
    dup = _mm_nn(d_down, w["w_down"], BF16, "d_act", mode="mul2", extra=r_up, tb=True)
    g["w_up"] = _mm_tn(f, dup, "dw_up")
    (dh1, d_mixed, g["norm_ffn_pre_w"], g["norm_mix_post_w"]), _ = _mm_epi(
        dup, w["w_up"], _mid_epi(dh2, h1, mixed, w["norm_ffn_pre_w"], w["norm_mix_post_w"]), "d_f", tb=True)
    g["w_out"] = _mm_tn(mi, d_mixed, "dw_out")
    (d_att_o, d_ssm_o, dgl, g["b_gate"]), _ = _mm_epi(
        d_mixed, w["w_out"], _gate_epi(att_o, ssm_o, gl, w["b_gate"]), "d_mi", tb=True)

    g["w_att_proj"] = _mm_tn(att, d_att_o, "dw_att_proj")
    g["w_ssm_proj"] = _mm_tn(ssm_y, d_ssm_o, "dw_ssm_proj")
    gn_epi = _gnorm_epi(y_ssd, z, w["ssm_norm_w"])
    if ex is None:
        (dy_ssd, dz, g["ssm_norm_w"]), _ = _mm_epi(d_ssm_o, w["w_ssm_proj"], gn_epi, "d_ssm_y", tb=True,
                                                    tn=PACK_COLS)
    else:
        gs_rest = jnp.concatenate(
            [_shards_from_full(n, g[n]).reshape(N_CHIPS, -1, PACK_COLS) for n in REST], axis=1)
        (dy_ssd, dz, g["ssm_norm_w"]), recv = _mm_epi(d_ssm_o, w["w_ssm_proj"], gn_epi, "d_ssm_y", tb=True,
                                                       tn=PACK_COLS, tm=1024, comm=_pair_comm([gs_rest]))
        p_rest = _pair_add2(gs_rest, recv[0], ex.c_arr, "rs_pair_add_rest")

    d_att = _mm_nn(d_att_o, w["w_att_proj"], F32, "d_att", tb=True)
    bwd_ins = _attn_delta2(d_att, att, lse)
    dqs, dks, dvs = [], [], []
    for d, qkv_p, (do_p, lse_p, delta_p) in zip(DILATIONS, pats, bwd_ins):
        if ex is not None and d == DILATIONS[0]:
            dq, dk, dv, recv3 = _attn_bwd2(qkv_p, do_p, lse_p, delta_p, d, comm=_chip_comm([p_rest]))
            q_rest = _chip_add2(p_rest, recv3[0], ex.chip_arr, "rs_chip_add_rest")
            ex.finish_reduce("rest", q_rest, _comm_call("rs_share_rest", _share_comm([q_rest]))[0])
        else:
            dq, dk, dv = _attn_bwd2(qkv_p, do_p, lse_p, delta_p, d)
        dqs.append(dq)
        dks.append(dk)
        dvs.append(dv)
    dqkv = _sum_qkv2(dqs, dks, dvs)

    dxact, ddt_raw, stats = _ssd_bwd3(xact, dt, acs, acst, sig, a_neg, dsk_e, hs, dy_ssd)
    g["a_log"] = stats[0:1, :SSM_HEADS]
    g["d_skip"] = stats[1:2, :SSM_HEADS]
    g["dt_bias"] = stats[2:3, :SSM_HEADS]
    dxbc, g["conv_w"], g["conv_b"] = _conv_bwd2(xbc, conv_pre, dxact, w["conv_w"])

    pieces = [(dqkv, w_qkv), (dz, w_z), (dxbc, w_xbc), (ddt_raw, w_dt), (dgl, w_g)]
    gw = [_mm_tn(dp, u, f"dw_in_{i}") for i, (dp, _) in enumerate(pieces)]
    gw[3] = gw[3][:SSM_HEADS]
    if ex is None:
        g["w_in_t"] = jnp.concatenate(gw, axis=0)
    du = None
    for i, (dp, wp) in enumerate([pieces[k] for k in (1, 2, 0, 3, 4)]):
        if ex is not None and i == 0:
            gs_in = _rows_to_shards(gw, IN_SHARD_ROWS, IN_SHARD_PAD)
            du, recv = _mm_nn(dp, wp, F32, f"d_u_{i}", acc=du, comm=_pair_comm([gs_in]))
            p_in = _pair_add2(gs_in, recv[0], ex.c_arr, "rs_pair_add_in")
            rows = p_in.shape[1] // 2
            p_parts = [p_in[:, :rows], p_in[:, rows:]]
            q_parts = []
        elif ex is not None and i in (1, 2):
            p_part = p_parts[i - 1]
            du, recv3 = _mm_nn(dp, wp, F32, f"d_u_{i}", acc=du, comm=_chip_comm([p_part]))
            q_parts.append(_chip_add2(p_part, recv3[0], ex.chip_arr, f"rs_chip_add_in_{i}"))
            if i == 2:
                others = _comm_call("rs_share_in", _share_comm(q_parts))
                ex.finish_reduce("w_in", jnp.concatenate(q_parts, axis=0), jnp.concatenate(others, axis=0))
        elif i == len(pieces) - 1:
            (grad_x, g["norm_mix_pre_w"]), _ = _mm_epi(
                dp, wp, _first_epi(du, dh1, x, w["norm_mix_pre_w"]), f"d_u_{i}")
        else:
            du = _mm_nn(dp, wp, F32, f"d_u_{i}", acc=du)
    return loss, grad_x, g


def _rows_to_shards(pieces, shard_rows, pad_rows):
    cols = pieces[0].shape[1]
    shards = []
    for s in range(N_CHIPS):
        lo, hi = s * shard_rows, (s + 1) * shard_rows
        parts, r0 = [], 0
        for p in pieces:
            a, b = max(lo, r0), min(hi, r0 + p.shape[0])
            if a < b:
                parts.append(p[a - r0:b - r0])
            r0 += p.shape[0]
        parts.append(jnp.zeros((pad_rows - shard_rows, cols), pieces[0].dtype))
        shards.append(jnp.concatenate(parts, axis=0))
    return jnp.stack(shards)


BIG = ("w_in", "w_att_proj", "w_ssm_proj", "w_out", "w_up", "w_down")
BIG_FULL_SHAPES = {"w_in": (D_MODEL, IN_PROJ_WIDTH), "w_att_proj": (ATT_WIDTH, D_MODEL),
                   "w_ssm_proj": (SSM_INNER, D_MODEL), "w_out": (D_MODEL, D_MODEL),
                   "w_up": (D_MODEL, FFN_HIDDEN), "w_down": (FFN_HIDDEN, D_MODEL)}
BIG_COL_SHARDED = {"w_in": True, "w_att_proj": True, "w_ssm_proj": False, "w_out": False, "w_up": True,
                   "w_down": False}
PACK_COLS = 1024
PACK_ROWS = 5760
PACK_HALF = PACK_ROWS // 2
PACK_BLOCK = 576
SMALL = ("norm_mix_pre_w", "b_gate", "conv_b", "dt_bias", "a_log", "d_skip", "ssm_norm_w",
         "norm_mix_post_w", "norm_ffn_pre_w", "norm_ffn_post_w")
SMALL_ROWS = 232


def _shard_shape(name):
    r, c = BIG_FULL_SHAPES[name]
    return (r, c // N_CHIPS) if BIG_COL_SHARDED[name] else (r // N_CHIPS, c)


def _pack(shards, dtype):
    flat = [shards[n].astype(dtype).reshape(-1, PACK_COLS) for n in BIG]
    rows = sum(f.shape[0] for f in flat)
    flat.append(jnp.zeros((PACK_ROWS - rows, PACK_COLS), dtype))
    return jnp.concatenate(flat, axis=0)


def _unpack(packed):
    out, r0 = {}, 0
    for n in BIG:
        shp = _shard_shape(n)
        rows = shp[0] * shp[1] // PACK_COLS
        out[n] = packed[r0:r0 + rows].reshape(shp)
        r0 += rows
    return out


def _unpack_full(gathered):
    out, r0 = {}, 0
    for n in BIG:
        shp = _shard_shape(n)
        rows = shp[0] * shp[1] // PACK_COLS
        sh = gathered[:, r0:r0 + rows].reshape((N_CHIPS,) + shp)
        if BIG_COL_SHARDED[n]:
            out[n] = sh.transpose(1, 0, 2).reshape(BIG_FULL_SHAPES[n])
        else:
            out[n] = sh.reshape(BIG_FULL_SHAPES[n])
        r0 += rows
    return out


def _pack_full(grads):
    parts = []
    rows_total = 0
    for n in BIG:
        shp = _shard_shape(n)
        gfull = grads[n]
        if BIG_COL_SHARDED[n]:
            sh = gfull.reshape(shp[0], N_CHIPS, shp[1]).transpose(1, 0, 2)
        else:
            sh = gfull.reshape((N_CHIPS,) + shp)
        parts.append(sh.reshape(N_CHIPS, -1, PACK_COLS))
        rows_total += parts[-1].shape[1]
    parts.append(jnp.zeros((N_CHIPS, PACK_ROWS - rows_total, PACK_COLS), F32))
    return jnp.concatenate(parts, axis=1)


def _mesh_pos():
    return lax.axis_index("x"), lax.axis_index("y"), lax.axis_index("c")


def _other_chips(x, y):
    return [(1 - x, y), (x, 1 - y), (1 - x, 1 - y)]


ANY = pl.BlockSpec(memory_space=pl.ANY)


def _allgather_packed(wpack):
    half = PACK_HALF

    def body(w_ref, out_ref, send_sems, recv_sems):
        x, y, c = _mesh_pos()
        me = 2 * x + y
        sibling = (x, y, 1 - c)
        chips = _other_chips(x, y)

        def rows(chip, h):
            return out_ref.at[chip, pl.ds(h * half, half), :]

        def copy(k, chip, h, to, src=None):
            return pltpu.make_async_remote_copy(
                src_ref=rows(chip, h) if src is None else src, dst_ref=rows(chip, h),
                send_sem=send_sems.at[k], recv_sem=recv_sems.at[k], device_id=to, device_id_type=MESH)

        mine_half = w_ref.at[pl.ds(c * half, half), :]
        first = [copy(j, me, c, (*chip, c), src=mine_half) for j, chip in enumerate(chips)]
        for cp in first:
            cp.start()
        passed = [copy(3 + j, 2 * chip[0] + chip[1], c, sibling) for j, chip in enumerate(chips)]
        for j, chip in enumerate(chips):
            copy(j, 2 * chip[0] + chip[1], c, (x, y, c)).wait_recv()
            passed[j].start()
        for j, chip in enumerate(chips):
            copy(3 + j, 2 * chip[0] + chip[1], 1 - c, (x, y, c)).wait_recv()
        for cp in first + passed:
            cp.wait_send()

    return pl.pallas_call(
        body, name="allgather_weights",
        out_shape=jax.ShapeDtypeStruct((N_CHIPS,) + wpack.shape, wpack.dtype),
        in_specs=[ANY], out_specs=ANY,
        scratch_shapes=[pltpu.SemaphoreType.DMA((6,)), pltpu.SemaphoreType.DMA((6,))],
        compiler_params=pltpu.CompilerParams(has_side_effects=True),
    )(wpack)


def _exchange_halves(gpack):
    half = PACK_HALF

    def body(g_ref, out_ref, send_sem, recv_sem):
        x, y, c = _mesh_pos()
        cp = pltpu.make_async_remote_copy(
            src_ref=g_ref.at[:, pl.ds((1 - c) * half, half), :], dst_ref=out_ref,
            send_sem=send_sem, recv_sem=recv_sem, device_id=(x, y, 1 - c), device_id_type=MESH)
        cp.start()
        cp.wait()

    return pl.pallas_call(
        body, name="rs_pair_exchange",
        out_shape=jax.ShapeDtypeStruct((N_CHIPS, half, PACK_COLS), F32),
        in_specs=[ANY], out_specs=ANY,
        scratch_shapes=[pltpu.SemaphoreType.DMA, pltpu.SemaphoreType.DMA],
        compiler_params=pltpu.CompilerParams(has_side_effects=True),
    )(gpack)


def _pair_add(gpack, recv, c_idx):
    nb = PACK_HALF // PACK_BLOCK

    def body(c_ref, g_ref, r_ref, o_ref):
        o_ref[...] = _b(g_ref[...] + r_ref[...])

    blk = (1, PACK_BLOCK, PACK_COLS)
    return pl.pallas_call(
        body, name="rs_pair_add",
        grid_spec=pltpu.PrefetchScalarGridSpec(
            num_scalar_prefetch=1, grid=(N_CHIPS, nb),
            in_specs=[pl.BlockSpec(blk, lambda s, i, c: (s, c[0] * nb + i, 0)),
                      pl.BlockSpec(blk, lambda s, i, c: (s, i, 0))],
            out_specs=pl.BlockSpec(blk, lambda s, i, c: (s, i, 0))),
        out_shape=jax.ShapeDtypeStruct((N_CHIPS, PACK_HALF, PACK_COLS), BF16),
        compiler_params=_params(("arbitrary", "arbitrary")),
    )(c_idx, gpack, recv)


def _exchange_chips(ppack):
    def body(p_ref, out_ref, send_sems, recv_sems):
        x, y, c = _mesh_pos()
        chips = _other_chips(x, y)
        cps = [pltpu.make_async_remote_copy(
            src_ref=p_ref.at[2 * chip[0] + chip[1]], dst_ref=out_ref.at[j],
            send_sem=send_sems.at[j], recv_sem=recv_sems.at[j], device_id=(*chip, c), device_id_type=MESH)
            for j, chip in enumerate(chips)]
        for cp in cps:
            cp.start()
        for cp in cps:
            cp.wait_recv()
        for cp in cps:
            cp.wait_send()

    return pl.pallas_call(
        body, name="rs_chip_exchange",
        out_shape=jax.ShapeDtypeStruct((N_CHIPS - 1, PACK_HALF, PACK_COLS), ppack.dtype),
        in_specs=[ANY], out_specs=ANY,
        scratch_shapes=[pltpu.SemaphoreType.DMA((3,)), pltpu.SemaphoreType.DMA((3,))],
        compiler_params=pltpu.CompilerParams(has_side_effects=True),
    )(ppack)


def _chip_add(ppack, recv, me_idx):
    nb = PACK_HALF // PACK_BLOCK

    def body(m_ref, p_ref, r0_ref, r1_ref, r2_ref, o_ref):
        o_ref[...] = ((p_ref[0].astype(F32) + r0_ref[0].astype(F32)) + r1_ref[0].astype(F32)) + r2_ref[0].astype(F32)

    blk = (1, PACK_BLOCK, PACK_COLS)
    return pl.pallas_call(
        body, name="rs_chip_add",
        grid_spec=pltpu.PrefetchScalarGridSpec(
            num_scalar_prefetch=1, grid=(nb,),
            in_specs=[pl.BlockSpec(blk, lambda i, m: (m[0], i, 0)),
                      pl.BlockSpec(blk, lambda i, m: (0, i, 0)),
                      pl.BlockSpec(blk, lambda i, m: (1, i, 0)),
                      pl.BlockSpec(blk, lambda i, m: (2, i, 0))],
            out_specs=pl.BlockSpec((PACK_BLOCK, PACK_COLS), lambda i, m: (i, 0))),
        out_shape=jax.ShapeDtypeStruct((PACK_HALF, PACK_COLS), F32),
        compiler_params=_params(("arbitrary",)),
    )(me_idx, ppack, recv, recv, recv)


def _share_halves(qhalf):
    def body(q_ref, out_ref, send_sem, recv_sem):
        x, y, c = _mesh_pos()
        cp = pltpu.make_async_remote_copy(
            src_ref=q_ref, dst_ref=out_ref, send_sem=send_sem, recv_sem=recv_sem,
            device_id=(x, y, 1 - c), device_id_type=MESH)
        cp.start()
        cp.wait()

    return pl.pallas_call(
        body, name="rs_share_halves",
        out_shape=jax.ShapeDtypeStruct(qhalf.shape, F32),
        in_specs=[ANY], out_specs=ANY,
        scratch_shapes=[pltpu.SemaphoreType.DMA, pltpu.SemaphoreType.DMA],
        compiler_params=pltpu.CompilerParams(has_side_effects=True),
    )(qhalf)


REST_EARLY = ("w_att_proj", "w_ssm_proj", "w_out")
REST_LATE = (("w_up",), ("w_down",))
REST = REST_EARLY + REST_LATE[0] + REST_LATE[1]
ADD_ROWS_CAP = 800
BF16_ROWS = 16
IN_SHARD_ROWS = IN_PROJ_WIDTH // N_CHIPS
IN_SHARD_PAD = 2688


def _stack_rest(shards, dtype, names=REST):
    return jnp.concatenate([shards[n].astype(dtype).reshape(-1, PACK_COLS) for n in names], axis=0)


def _unstack_rest(stacked, lead=(), names=REST):
    out, r0 = {}, 0
    for n in names:
        shp = _shard_shape(n)
        rows = shp[0] * shp[1] // PACK_COLS
        out[n] = stacked[..., r0:r0 + rows, :].reshape(lead + shp)
        r0 += rows
    return out


def _full_from_shards(name, sh):
    if BIG_COL_SHARDED[name]:
        return sh.transpose(1, 0, 2).reshape(BIG_FULL_SHAPES[name])
    return sh.reshape(BIG_FULL_SHAPES[name])


def _shards_from_full(name, full):
    shp = _shard_shape(name)
    if BIG_COL_SHARDED[name]:
        return full.reshape(shp[0], N_CHIPS, shp[1]).transpose(1, 0, 2)
    return full.reshape((N_CHIPS,) + shp)


def _allgather2(shards):
    n = len(shards)

    def body(*refs):
        w_refs, out_refs, send_sems, recv_sems = refs[:n], refs[n:2 * n], refs[2 * n], refs[2 * n + 1]
        x, y, c = _mesh_pos()
        me = 2 * x + y
        sibling = (x, y, 1 - c)
        chips = _other_chips(x, y)
        plans = []
        for a, (w_ref, out_ref) in enumerate(zip(w_refs, out_refs)):
            half = w_ref.shape[0] // 2

            def copy(k, chip, h, to, src=None, out_ref=out_ref, half=half, a=a):
                rows = out_ref.at[chip, pl.ds(h * half, half), :]
                return pltpu.make_async_remote_copy(
                    src_ref=rows if src is None else src, dst_ref=rows,
                    send_sem=send_sems.at[6 * a + k], recv_sem=recv_sems.at[6 * a + k],
                    device_id=to, device_id_type=MESH)

            mine_half = w_ref.at[pl.ds(c * half, half), :]
            idx = [2 * chip[0] + chip[1] for chip in chips]
            send = [copy(j, me, c, (*chip, c), src=mine_half) for j, chip in enumerate(chips)]
            land = [copy(j, idx[j], c, (x, y, c)) for j in range(N_CHIPS - 1)]
            forward = [copy(3 + j, idx[j], c, sibling) for j in range(N_CHIPS - 1)]
            land_fw = [copy(3 + j, idx[j], 1 - c, (x, y, c)) for j in range(N_CHIPS - 1)]
            plans.append((send, land, forward, land_fw))
        for send, _, _, _ in plans:
            for cp in send:
                cp.start()
        for _, land, forward, _ in plans:
            for j in range(N_CHIPS - 1):
                land[j].wait_recv()
                forward[j].start()
        for _, _, _, land_fw in plans:
            for cp in land_fw:
                cp.wait_recv()
        for send, _, forward, _ in plans:
            for cp in send + forward:
                cp.wait_send()

    return pl.pallas_call(
        body, name="allgather_weights",
        out_shape=[jax.ShapeDtypeStruct((N_CHIPS,) + s.shape, s.dtype) for s in shards],
        in_specs=[ANY] * n, out_specs=[ANY] * n,
        scratch_shapes=[pltpu.SemaphoreType.DMA((6 * n,)), pltpu.SemaphoreType.DMA((6 * n,))],
        compiler_params=pltpu.CompilerParams(has_side_effects=True),
    )(*shards)


def _exchange_halves2(gs):
    n = len(gs)

    def body(*refs):
        g_refs, out_refs, send_sems, recv_sems = refs[:n], refs[n:2 * n], refs[2 * n], refs[2 * n + 1]
        x, y, c = _mesh_pos()
        cps = []
        for a, (g_ref, out_ref) in enumerate(zip(g_refs, out_refs)):
            half = g_ref.shape[1] // 2
            cps.append(pltpu.make_async_remote_copy(
                src_ref=g_ref.at[:, pl.ds((1 - c) * half, half), :], dst_ref=out_ref,
                send_sem=send_sems.at[a], recv_sem=recv_sems.at[a], device_id=(x, y, 1 - c),
                device_id_type=MESH))
        for cp in cps:
            cp.start()
        for cp in cps:
            cp.wait()

    return pl.pallas_call(
        body, name="rs_pair_exchange",
        out_shape=[jax.ShapeDtypeStruct((N_CHIPS, g.shape[1] // 2, g.shape[2]), F32) for g in gs],
        in_specs=[ANY] * n, out_specs=[ANY] * n,
        scratch_shapes=[pltpu.SemaphoreType.DMA((n,)), pltpu.SemaphoreType.DMA((n,))],
        compiler_params=pltpu.CompilerParams(has_side_effects=True),
    )(*gs)


def _pair_add2(g, recv, c_idx, name):
    _, half, cols = recv.shape
    rb = _row_block(half, ADD_ROWS_CAP, BF16_ROWS)
    nb = half // rb

    def body(c_ref, g_ref, r_ref, o_ref):
        o_ref[...] = _b(g_ref[...] + r_ref[...])

    blk = (1, rb, cols)
    return pl.pallas_call(
        body, name=name,
        grid_spec=pltpu.PrefetchScalarGridSpec(
            num_scalar_prefetch=1, grid=(N_CHIPS, nb),
            in_specs=[pl.BlockSpec(blk, lambda s, i, c: (s, c[0] * nb + i, 0)),
                      pl.BlockSpec(blk, lambda s, i, c: (s, i, 0))],
            out_specs=pl.BlockSpec(blk, lambda s, i, c: (s, i, 0))),
        out_shape=jax.ShapeDtypeStruct(recv.shape, BF16),
        compiler_params=_params(("arbitrary", "arbitrary")),
    )(c_idx, g, recv)


def _exchange_chips2(ps):
    n = len(ps)

    def body(*refs):
        p_refs, out_refs, send_sems, recv_sems = refs[:n], refs[n:2 * n], refs[2 * n], refs[2 * n + 1]
        x, y, c = _mesh_pos()
        chips = _other_chips(x, y)
        cps = [pltpu.make_async_remote_copy(
            src_ref=p_ref.at[2 * chip[0] + chip[1]], dst_ref=out_ref.at[j],
            send_sem=send_sems.at[3 * a + j], recv_sem=recv_sems.at[3 * a + j], device_id=(*chip, c),
            device_id_type=MESH)
            for a, (p_ref, out_ref) in enumerate(zip(p_refs, out_refs)) for j, chip in enumerate(chips)]
        for cp in cps:
            cp.start()
        for cp in cps:
            cp.wait_recv()
        for cp in cps:
            cp.wait_send()

    return pl.pallas_call(
        body, name="rs_chip_exchange",
        out_shape=[jax.ShapeDtypeStruct((N_CHIPS - 1,) + p.shape[1:], p.dtype) for p in ps],
        in_specs=[ANY] * n, out_specs=[ANY] * n,
        scratch_shapes=[pltpu.SemaphoreType.DMA((3 * n,)), pltpu.SemaphoreType.DMA((3 * n,))],
        compiler_params=pltpu.CompilerParams(has_side_effects=True),
    )(*ps)


def _chip_add2(p, recv, me_idx, name):
    _, half, cols = recv.shape
    rb = _row_block(half, ADD_ROWS_CAP, BF16_ROWS)

    def body(m_ref, p_ref, r0_ref, r1_ref, r2_ref, o_ref):
        o_ref[...] = ((p_ref[0].astype(F32) + r0_ref[0].astype(F32)) + r1_ref[0].astype(F32)) + r2_ref[0].astype(F32)

    blk = (1, rb, cols)
    return pl.pallas_call(
        body, name=name,
        grid_spec=pltpu.PrefetchScalarGridSpec(
            num_scalar_prefetch=1, grid=(half // rb,),
            in_specs=[pl.BlockSpec(blk, lambda i, m: (m[0], i, 0)),
                      pl.BlockSpec(blk, lambda i, m: (0, i, 0)),
                      pl.BlockSpec(blk, lambda i, m: (1, i, 0)),
                      pl.BlockSpec(blk, lambda i, m: (2, i, 0))],
            out_specs=pl.BlockSpec((rb, cols), lambda i, m: (i, 0))),
        out_shape=jax.ShapeDtypeStruct((half, cols), F32),
        compiler_params=_params(("arbitrary",)),
    )(me_idx, p, recv, recv, recv)


def _share_halves2(qs):
    n = len(qs)

    def body(*refs):
        q_refs, out_refs, send_sems, recv_sems = refs[:n], refs[n:2 * n], refs[2 * n], refs[2 * n + 1]
        x, y, c = _mesh_pos()
        cps = [pltpu.make_async_remote_copy(
            src_ref=q_ref, dst_ref=out_ref, send_sem=send_sems.at[a], recv_sem=recv_sems.at[a],
            device_id=(x, y, 1 - c), device_id_type=MESH)
            for a, (q_ref, out_ref) in enumerate(zip(q_refs, out_refs))]
        for cp in cps:
            cp.start()
        for cp in cps:
            cp.wait()

    return pl.pallas_call(
        body, name="rs_share_halves",
        out_shape=[jax.ShapeDtypeStruct(q.shape, F32) for q in qs],
        in_specs=[ANY] * n, out_specs=[ANY] * n,
        scratch_shapes=[pltpu.SemaphoreType.DMA((n,)), pltpu.SemaphoreType.DMA((n,))],
        compiler_params=pltpu.CompilerParams(has_side_effects=True),
    )(*qs)


def _gather_plan():
    def copies(w_refs, out_refs, send_sems, recv_sems):
        x, y, c = _mesh_pos()
        me = 2 * x + y
        sibling = (x, y, 1 - c)
        chips = _other_chips(x, y)
        idx = [2 * chip[0] + chip[1] for chip in chips]
        plans = []
        for a, (w_ref, out_ref) in enumerate(zip(w_refs, out_refs)):
            half = w_ref.shape[0] // 2

            def copy(k, chip, h, to, src=None, out_ref=out_ref, half=half, a=a):
                rows = out_ref.at[chip, pl.ds(h * half, half), :]
                return pltpu.make_async_remote_copy(
                    src_ref=rows if src is None else src, dst_ref=rows,
                    send_sem=send_sems.at[6 * a + k], recv_sem=recv_sems.at[6 * a + k],
                    device_id=to, device_id_type=MESH)

            mine_half = w_ref.at[pl.ds(c * half, half), :]
            send = [copy(j, me, c, (*chip, c), src=mine_half) for j, chip in enumerate(chips)]
            land = [copy(j, idx[j], c, (x, y, c)) for j in range(N_CHIPS - 1)]
            forward = [copy(3 + j, idx[j], c, sibling) for j in range(N_CHIPS - 1)]
            land_fw = [copy(3 + j, idx[j], 1 - c, (x, y, c)) for j in range(N_CHIPS - 1)]
            plans.append((send, land, forward, land_fw))
        return plans

    def start(*refs):
        for send, _, _, _ in copies(*refs):
            for cp in send:
                cp.start()

    def finish(*refs):
        plans = copies(*refs)
        for _, land, forward, _ in plans:
            for j in range(N_CHIPS - 1):
                land[j].wait_recv()
                forward[j].start()
        for _, _, _, land_fw in plans:
            for cp in land_fw:
                cp.wait_recv()
        for send, _, forward, _ in plans:
            for cp in send + forward:
                cp.wait_send()

    return start, finish


def _pair_plan(halves):
    def copies(in_refs, out_refs, send_sems, recv_sems):
        x, y, c = _mesh_pos()
        cps = []
        for a, (g_ref, out_ref) in enumerate(zip(in_refs, out_refs)):
            if halves:
                half = g_ref.shape[1] // 2
                src = g_ref.at[:, pl.ds((1 - c) * half, half), :]
            else:
                src = g_ref
            cps.append(pltpu.make_async_remote_copy(
                src_ref=src, dst_ref=out_ref, send_sem=send_sems.at[a], recv_sem=recv_sems.at[a],
                device_id=(x, y, 1 - c), device_id_type=MESH))
        return cps

    def start(*refs):
        for cp in copies(*refs):
            cp.start()

    def finish(*refs):
        for cp in copies(*refs):
            cp.wait()

    return start, finish


def _chip_plan():
    def copies(in_refs, out_refs, send_sems, recv_sems):
        x, y, c = _mesh_pos()
        chips = _other_chips(x, y)
        return [pltpu.make_async_remote_copy(
            src_ref=p_ref.at[2 * chip[0] + chip[1]], dst_ref=out_ref.at[j],
            send_sem=send_sems.at[3 * a + j], recv_sem=recv_sems.at[3 * a + j], device_id=(*chip, c),
            device_id_type=MESH)
            for a, (p_ref, out_ref) in enumerate(zip(in_refs, out_refs)) for j, chip in enumerate(chips)]

    def start(*refs):
        for cp in copies(*refs):
            cp.start()

    def finish(*refs):
        cps = copies(*refs)
        for cp in cps:
            cp.wait_recv()
        for cp in cps:
            cp.wait_send()

    return start, finish


def _gather_comm(shards):
    return _Comm(_gather_plan(), shards, [jax.ShapeDtypeStruct((N_CHIPS,) + s.shape, s.dtype) for s in shards],
                 6 * len(shards))


def _pair_comm(gs):
    return _Comm(_pair_plan(True), gs,
                 [jax.ShapeDtypeStruct((N_CHIPS, g.shape[1] // 2, g.shape[2]), g.dtype) for g in gs], len(gs))


def _chip_comm(ps):
    return _Comm(_chip_plan(), ps, [jax.ShapeDtypeStruct((N_CHIPS - 1,) + p.shape[1:], p.dtype) for p in ps],
                 3 * len(ps))


def _share_comm(qs):
    return _Comm(_pair_plan(False), qs, [jax.ShapeDtypeStruct(q.shape, q.dtype) for q in qs], len(qs))


def _comm_call(name, comm):
    n, m = len(comm.ins), len(comm.outs)

    def body(*refs):
        args = (refs[:n], refs[n:n + m], refs[n + m], refs[n + m + 1])
        comm.start(*args)
        comm.finish(*args)

    return pl.pallas_call(
        body, name=name, out_shape=comm.outs, in_specs=[ANY] * n, out_specs=[ANY] * m,
        scratch_shapes=[pltpu.SemaphoreType.DMA((comm.n_sems,))] * 2,
        compiler_params=pltpu.CompilerParams(has_side_effects=True),
    )(*comm.ins)


class _Exchange:
    def __init__(self, chip, ci, mine):
        self.chip, self.ci = chip, ci
        self.mine = mine
        self.c_arr = ci.reshape(1).astype(jnp.int32)
        self.chip_arr = chip.reshape(1).astype(jnp.int32)
        self.reduced = {}

    def rest_weights(self, got, names):
        stacks = lax.dynamic_update_slice(got, self.mine[names][None], (self.chip, 0, 0))
        return {n: _full_from_shards(n, sh) for n, sh in _unstack_rest(stacks, (N_CHIPS,), names).items()}

    def finish_reduce(self, key, mine, other):
        south = self.ci == 0
        self.reduced[key] = jnp.concatenate([jnp.where(south, mine, other), jnp.where(south, other, mine)],
                                            axis=0)


def _allreduce_small(part, name):
    rows = part.shape[0]

    def body(p_ref, out_ref, buf, send_sems, recv_sems, local_sem):
        x, y, c = _mesh_pos()
        me, sibling = (x, y, c), (x, y, 1 - c)
        chips = _other_chips(x, y)

        def slot(px, py, pc):
            return buf.at[pl.ds((4 * px + 2 * py + pc) * rows, rows), :]

        def copy(k, block, to, src=None):
            return pltpu.make_async_remote_copy(
                src_ref=slot(*block) if src is None else src, dst_ref=slot(*block),
                send_sem=send_sems.at[k], recv_sem=recv_sems.at[k], device_id=to, device_id_type=MESH)

        mine = pltpu.make_async_copy(p_ref, slot(*me), local_sem)
        mine.start()
        first = [copy(0, me, sibling, src=p_ref)]
        first += [copy(1 + j, me, (*chip, c), src=p_ref) for j, chip in enumerate(chips)]
        for cp in first:
            cp.start()
        passed = [copy(4 + j, (*chip, c), sibling) for j, chip in enumerate(chips)]
        for j, chip in enumerate(chips):
            copy(1 + j, (*chip, c), me).wait_recv()
            passed[j].start()
        copy(0, sibling, me).wait_recv()
        for j, chip in enumerate(chips):
            copy(4 + j, (*chip, 1 - c), me).wait_recv()
        for cp in first + passed:
            cp.wait_send()
        mine.wait()
        acc = buf[pl.ds(0, rows), :]
        for k in range(1, N_DEV):
            acc = acc + buf[pl.ds(k * rows, rows), :]
        out_ref[...] = acc

    return pl.pallas_call(
        body, name=name,
        out_shape=jax.ShapeDtypeStruct(part.shape, F32),
        in_specs=[pl.BlockSpec(memory_space=pltpu.VMEM)],
        out_specs=pl.BlockSpec(memory_space=pltpu.VMEM),
        scratch_shapes=[pltpu.VMEM((N_DEV * rows, LANES), F32), pltpu.SemaphoreType.DMA((7,)),
                        pltpu.SemaphoreType.DMA((7,)), pltpu.SemaphoreType.DMA],
        compiler_params=pltpu.CompilerParams(has_side_effects=True),
    )(part)


def _adamw(w, g, m, v, name):
    R, C = w.shape
    bs = _row_block(R, 512, 8) if R % 8 == 0 else R
    c1 = 1.0 / (1.0 - ADAM_B1 ** ADAM_STEP)
    c2 = 1.0 / (1.0 - ADAM_B2 ** ADAM_STEP)

    def body(w_ref, g_ref, m_ref, v_ref, d_ref, nm_ref, nv_ref):
        gg = g_ref[...]
        nm = ADAM_B1 * m_ref[...] + (1.0 - ADAM_B1) * gg
        nv = ADAM_B2 * v_ref[...] + (1.0 - ADAM_B2) * (gg * gg)
        nm_ref[...] = nm
        nv_ref[...] = nv
        d_ref[...] = -ADAM_LR * ((nm * c1) / (jnp.sqrt(nv * c2) + ADAM_EPS) + ADAM_WD * w_ref[...])

    spec = pl.BlockSpec((bs, C), lambda i: (i, 0))
    shp = jax.ShapeDtypeStruct((R, C), F32)
    return pl.pallas_call(
        body, name=name, grid=(R // bs,), in_specs=[spec] * 4, out_specs=[spec] * 3, out_shape=[shp] * 3,
        compiler_params=_params(("parallel",)),
    )(w, g, m, v)


WEIGHTS = ("norm_mix_pre_w", "w_in", "b_gate", "conv_w", "conv_b", "dt_bias", "a_log", "d_skip",
           "ssm_norm_w", "w_att_proj", "w_ssm_proj", "w_out", "norm_mix_post_w", "norm_ffn_pre_w", "w_up",
           "w_down", "norm_ffn_post_w")


def _flat_small(vals, conv_w_full):
    flat = [vals[n].reshape(-1) for n in SMALL] + [conv_w_full.reshape(-1)]
    v = jnp.concatenate(flat)
    return jnp.pad(v, (0, SMALL_ROWS * LANES - v.shape[0])).reshape(SMALL_ROWS, LANES)


def kernel(x, norm_mix_pre_w, w_in, b_gate, conv_w, conv_b, dt_bias, a_log, d_skip, ssm_norm_w, w_att_proj, w_ssm_proj, w_out, norm_mix_post_w, norm_ffn_pre_w, w_up, w_down, norm_ffn_post_w, loss_target, m_norm_mix_pre_w, m_w_in, m_b_gate, m_conv_w, m_conv_b, m_dt_bias, m_a_log, m_d_skip, m_ssm_norm_w, m_w_att_proj, m_w_ssm_proj, m_w_out, m_norm_mix_post_w, m_norm_ffn_pre_w, m_w_up, m_w_down, m_norm_ffn_post_w, v_norm_mix_pre_w, v_w_in, v_b_gate, v_conv_w, v_conv_b, v_dt_bias, v_a_log, v_d_skip, v_ssm_norm_w, v_w_att_proj, v_w_ssm_proj, v_w_out, v_norm_mix_post_w, v_norm_ffn_pre_w, v_w_up, v_w_down, v_norm_ffn_post_w):
    args = locals()

    def strip(a):
        return a[0] if a.ndim == 3 else a

    wts = {n: strip(args[n]) for n in WEIGHTS}
    mom = {n: strip(args["m_" + n]) for n in WEIGHTS}
    var = {n: strip(args["v_" + n]) for n in WEIGHTS}
    xi, yi, ci = _mesh_pos()
    chip = 2 * xi + yi

    tr = lambda a: jnp.swapaxes(a, 0, 1)
    w_in_mine = jnp.pad(tr(wts["w_in"]).astype(BF16), ((0, IN_SHARD_PAD - IN_SHARD_ROWS), (0, 0)))
    got_in = _comm_call("allgather_w_in", _gather_comm([w_in_mine]))[0]
    stacks_in = lax.dynamic_update_slice(got_in, w_in_mine[None], (chip, 0, 0))
    full = {"w_in_t": stacks_in[:, :IN_SHARD_ROWS].reshape(IN_PROJ_WIDTH, D_MODEL)}
    ex = _Exchange(chip, ci, {names: _stack_rest(wts, BF16, names) for names in (REST_EARLY,) + REST_LATE})
    cw_cols = CONV_DIM // N_CHIPS
    conv_slab = lax.dynamic_update_slice(jnp.zeros((SSM_CONV, CONV_DIM), F32),
                                         jnp.where(ci == 0, wts["conv_w"], 0.0), (0, chip * cw_cols))
    small_in = jnp.pad(conv_slab.reshape(-1), (0, SMALL_ROWS * LANES - SSM_CONV * CONV_DIM))
    conv_full = _allreduce_small(small_in.reshape(SMALL_ROWS, LANES), "gather_conv_w")
    full["conv_w"] = conv_full.reshape(-1)[:SSM_CONV * CONV_DIM].reshape(SSM_CONV, CONV_DIM)
    for n in SMALL:
        full[n] = wts[n]

    loss_part, grad_x, g = _local_step(x[0], loss_target[0], full, ex)
    loss = lax.psum(loss_part[0, 0], ("x", "y", "c"))

    gshard = _unstack_rest(ex.reduced["rest"])
    g_in_t = ex.reduced["w_in"][:IN_SHARD_ROWS]
    small_sum = _allreduce_small(_flat_small(g, g["conv_w"]), "allreduce_small_grads").reshape(-1)
    grads, off = {}, 0
    for n in SMALL:
        sz = wts[n].size
        grads[n] = small_sum[off:off + sz].reshape(wts[n].shape)
        off += sz
    conv_g = small_sum[off:off + SSM_CONV * CONV_DIM].reshape(SSM_CONV, CONV_DIM)
    grads["conv_w"] = lax.dynamic_slice(conv_g, (0, chip * cw_cols), (SSM_CONV, cw_cols))
    grads.update(gshard)

    delta, new_m, new_v = {}, {}, {}
    for n in REST:
        delta[n], new_m[n], new_v[n] = _adamw(wts[n], grads[n], mom[n], var[n], f"adamw_{n}")
    in_t = _adamw(tr(wts["w_in"]), g_in_t, tr(mom["w_in"]), tr(var["w_in"]), "adamw_w_in")
    grads["w_in"] = tr(g_in_t)
    delta["w_in"], new_m["w_in"], new_v["w_in"] = (tr(a) for a in in_t)
    small_names = SMALL + ("conv_w",)

    def pack_small(d):
        v = jnp.concatenate([d[n].reshape(-1) for n in small_names])
        rows = -(-v.shape[0] // (8 * LANES)) * 8
        return jnp.pad(v, (0, rows * LANES - v.shape[0])).reshape(rows, LANES)

    ds, ms, vs = _adamw(pack_small(wts), pack_small(grads), pack_small(mom), pack_small(var), "adamw_small")
    off = 0
    for n in small_names:
        sz = wts[n].size
        for dst, src in ((delta, ds), (new_m, ms), (new_v, vs)):
            dst[n] = src.reshape(-1)[off:off + sz].reshape(wts[n].shape)
        off += sz

    out = [loss, grad_x[None]]
    for d in (grads, delta, new_m, new_v):
        out += [d[n][None] if args[n].ndim == 3 else d[n] for n in WEIGHTS]
    return tuple(out)
```

```python
import math

import numpy as np
import jax
import jax.numpy as jnp
from jax import lax
from jax.experimental import pallas as pl
from jax.experimental.pallas import tpu as pltpu

F32 = jnp.float32
BF16 = jnp.bfloat16

D_MODEL = 1024
HEAD_DIM = 64
N_ATT_HEADS = 12
ATT_WIDTH = N_ATT_HEADS * HEAD_DIM
DILATIONS = (1, 4, 16)
ATT_BLOCK = 128
SSM_INNER = 2048
SSM_HEADS = 32
SSM_GROUPS = 8
HEADS_PER_GROUP = SSM_HEADS // SSM_GROUPS
SSM_HEAD_DIM = 64
SSM_STATE = 128
SSM_CONV = 4
SSM_CHUNK = 128
CONV_DIM = SSM_INNER + 2 * SSM_GROUPS * SSM_STATE
FFN_HIDDEN = 4 * D_MODEL
IN_SPLITS = (ATT_WIDTH, ATT_WIDTH, ATT_WIDTH, SSM_INNER, CONV_DIM, SSM_HEADS, 2 * D_MODEL)
IN_PROJ_WIDTH = sum(IN_SPLITS)
RMS_EPS = 1e-6
LANES = 128
NEG_BIG = -1e30

ADAM_LR = 0.001
ADAM_B1 = 0.9
ADAM_B2 = 0.999
ADAM_EPS = 1e-08
ADAM_WD = 0.01
ADAM_STEP = 10

N_CHIPS = 4
N_DEV = 8
VMEM_LIMIT = 56 * 1024 * 1024
MESH = pl.DeviceIdType.MESH


def _alibi_slopes(n):
    def pow2(m):
        start = 2.0 ** (-8.0 / m)
        return [start ** (i + 1) for i in range(m)]
    if (n & (n - 1)) == 0:
        s = pow2(n)
    else:
        c = 2 ** int(math.floor(math.log2(n)))
        s = pow2(c) + pow2(2 * c)[0::2][: n - c]
    return [float(v) for v in np.array(s, dtype=np.float32)]


def _params(sem):
    return pltpu.CompilerParams(dimension_semantics=sem, vmem_limit_bytes=VMEM_LIMIT)


def _dot(a, b):
    return lax.dot_general(a, b, (((1,), (0,)), ((), ())), preferred_element_type=F32)


def _dot_nt(a, b):
    return lax.dot_general(a, b, (((1,), (1,)), ((), ())), preferred_element_type=F32)


def _dot_tn(a, b):
    return lax.dot_general(a, b, (((0,), (0,)), ((), ())), preferred_element_type=F32)


def _dot_hi(a, b):
    return lax.dot_general(a, b, (((1,), (0,)), ((), ())), preferred_element_type=F32,
                           precision=lax.Precision.HIGHEST)


def _b(x):
    return x.astype(BF16)


def _sigmoid(x):
    return 1.0 / (1.0 + jnp.exp(-x))


def _pick(n, cands):
    for c in cands:
        if n % c == 0:
            return c
    raise ValueError(f"no tile for {n}")


def _row_block(rows, cap, mult):
    best = max(d for d in range(mult, cap + 1, mult) if rows % d == 0)
    return best


class _Comm:
    def __init__(self, plan, ins, outs, n_sems):
        self.start, self.finish = plan
        self.ins, self.outs, self.n_sems = list(ins), list(outs), n_sems


def _mm_nn(a, b, out_dtype, name, acc=None, mode=None, extra=None, comm=None, tb=False):
    M, K = a.shape
    N = b.shape[0] if tb else b.shape[1]
    tn = _pick(N, (1024, 768, 512, 256, 128))
    tk = K if K <= 4096 else _pick(K, (2048, 1024))
    tm = 1024 if M % 1024 == 0 and K <= 2304 else 512
    nk = K // tk
    nj, ni = N // tn, M // tm
    side = acc if acc is not None else extra
    n_out = 2 if mode == "relu2" else 1
    n_in = 2 + (side is not None)
    n_ci = len(comm.ins) if comm else 0
    n_co = len(comm.outs) if comm else 0

    def body(*refs):
        a_ref, b_ref = refs[0], refs[1]
        s_ref = refs[2] if side is not None else None
        o_refs = refs[n_in + n_ci:n_in + n_ci + n_out]
        if comm:
            c_args = (refs[n_in:n_in + n_ci], refs[n_in + n_ci + n_out:n_in + n_ci + n_out + n_co],
                      refs[-2], refs[-1])
            pj, pi, pk = pl.program_id(0), pl.program_id(1), pl.program_id(2)

            @pl.when(jnp.logical_and(jnp.logical_and(pj == 0, pi == 0), pk == 0))
            def _():
                comm.start(*c_args)

        def finish(r):
            if mode == "relu2":
                r = jnp.maximum(r, 0.0)
                o_refs[0][...] = _b(r)
                o_refs[1][...] = _b(r * r)
            elif mode == "mul2":
                o_refs[0][...] = _b(r * (2.0 * s_ref[...].astype(F32)))
            else:
                if acc is not None:
                    r = r + s_ref[...]
                o_refs[0][...] = r.astype(out_dtype)

        part = (_dot_nt if tb else _dot)(_b(a_ref[...]), _b(b_ref[...]))
        if nk == 1:
            finish(part)
        else:
            acc_ref = refs[n_in + n_ci + n_out + n_co]
            k = pl.program_id(2)

            @pl.when(k == 0)
            def _():
                acc_ref[...] = part

            @pl.when(jnp.logical_and(k > 0, k < nk - 1))
            def _():
                acc_ref[...] += part

            @pl.when(k == nk - 1)
            def _():
                finish(acc_ref[...] + part)

        if comm:
            @pl.when(jnp.logical_and(jnp.logical_and(pj == nj - 1, pi == ni - 1), pk == nk - 1))
            def _():
                comm.finish(*c_args)

    tile = pl.BlockSpec((tm, tn), lambda j, i, k: (i, j))
    in_specs = [pl.BlockSpec((tm, tk), lambda j, i, k: (i, k)),
                pl.BlockSpec((tn, tk), lambda j, i, k: (j, k)) if tb else
                pl.BlockSpec((tk, tn), lambda j, i, k: (k, j))]
    args = [a, b]
    if side is not None:
        in_specs.append(tile)
        args.append(side)
    odt = BF16 if mode in ("relu2", "mul2") else out_dtype
    scratch = [pltpu.VMEM((tm, tn), F32)] if nk > 1 else []
    if comm:
        scratch += [pltpu.SemaphoreType.DMA((comm.n_sems,))] * 2
        params = pltpu.CompilerParams(dimension_semantics=("arbitrary",) * 3, vmem_limit_bytes=VMEM_LIMIT,
                                      has_side_effects=True)
    else:
        params = _params(("parallel", "parallel", "arbitrary"))
    outs = pl.pallas_call(
        body, name=name, grid=(nj, ni, nk),
        in_specs=in_specs + [ANY] * n_ci,
        out_specs=[tile] * n_out + [ANY] * n_co,
        out_shape=[jax.ShapeDtypeStruct((M, N), odt)] * n_out + list(comm.outs if comm else []),
        scratch_shapes=scratch,
        compiler_params=params,
    )(*args, *(comm.ins if comm else []))
    res = outs[:n_out] if n_out > 1 else outs[0]
    return (res, outs[n_out:]) if comm else res


class _Epi:
    def __init__(self, fn, row_ins=(), full_ins=(), row_outs=(), acc_outs=(), tiled=False, a_fn=None,
                 scratch=()):
        self.fn, self.row_ins, self.full_ins = fn, list(row_ins), list(full_ins)
        self.row_outs, self.acc_outs, self.tiled = list(row_outs), list(acc_outs), tiled
        self.a_fn = a_fn
        self.scratch = list(scratch)


def _acc_into(ref, val, first):
    @pl.when(first)
    def _():
        ref[...] = val

    @pl.when(jnp.logical_not(first))
    def _():
        ref[...] += val


def _mm_epi(a, b, epi, name, tb=False, comm=None, tm=512, tn=None):
    N, K = b.shape if tb else b.shape[::-1]
    M = epi.row_ins[0][0].shape[0] if a is None else a.shape[0]
    tn = tn or N
    assert epi.tiled or tn == N
    tk = K if K <= 4096 else _pick(K, (2048, 1024))
    nk = K // tk
    nj, ni = N // tn, M // tm
    assert a is not None or (nk == 1 and nj == 1)
    n_a = 0 if a is None else 1
    n_ri, n_fi, n_ro, n_ao = len(epi.row_ins), len(epi.full_ins), len(epi.row_outs), len(epi.acc_outs)
    n_ci = len(comm.ins) if comm else 0
    n_co = len(comm.outs) if comm else 0
    i0 = n_a + 1
    o0 = i0 + n_ci + n_ri + n_fi

    def body(*refs):
        b_ref = refs[n_a]
        ri = refs[i0 + n_ci:i0 + n_ci + n_ri]
        fi = refs[i0 + n_ci + n_ri:o0]
        ro = refs[o0 + n_co:o0 + n_co + n_ro]
        ao = refs[o0 + n_co + n_ro:o0 + n_co + n_ro + n_ao]
        pj, pi, pk = pl.program_id(0), pl.program_id(1), pl.program_id(2)
        if comm:
            c_args = (refs[i0:i0 + n_ci], refs[o0:o0 + n_co], refs[-2], refs[-1])

            @pl.when(jnp.logical_and(jnp.logical_and(pj == 0, pi == 0), pk == 0))
            def _():
                comm.start(*c_args)

        s0 = o0 + n_co + n_ro + n_ao + (nk > 1)
        extra = (refs[s0:s0 + len(epi.scratch)],) if epi.scratch else ()
        a_val = epi.a_fn(ri, fi, ro) if a is None else _b(refs[0][...])
        part = (_dot_nt if tb else _dot)(a_val, _b(b_ref[...]))
        if nk == 1:
            epi.fn(part, ri, fi, ro, ao, pi == 0, *extra)
        else:
            acc_ref = refs[o0 + n_co + n_ro + n_ao]

            @pl.when(pk == 0)
            def _():
                acc_ref[...] = part

            @pl.when(jnp.logical_and(pk > 0, pk < nk - 1))
            def _():
                acc_ref[...] += part

            @pl.when(pk == nk - 1)
            def _():
                epi.fn(acc_ref[...] + part, ri, fi, ro, ao, pi == 0, *extra)

        if comm:
            @pl.when(jnp.logical_and(jnp.logical_and(pj == nj - 1, pi == ni - 1), pk == nk - 1))
            def _():
                comm.finish(*c_args)

    def row_spec(width, cb):
        if epi.tiled:
            return pl.BlockSpec((tm, tn), lambda j, i, k: (i, j + cb))
        return pl.BlockSpec((tm, width), lambda j, i, k: (i, cb))

    in_specs = [pl.BlockSpec((tm, tk), lambda j, i, k: (i, k))] * n_a
    in_specs += [pl.BlockSpec((tn, tk), lambda j, i, k: (j, k)) if tb else
                 pl.BlockSpec((tk, tn), lambda j, i, k: (k, j))]
    in_specs += [ANY] * n_ci
    in_specs += [row_spec(w, cb) for (_, w, cb) in epi.row_ins]
    in_specs += [pl.BlockSpec((1, tn), lambda j, i, k: (0, j)) if epi.tiled else
                 pl.BlockSpec(f.shape, lambda j, i, k: (0, 0)) for f in epi.full_ins]
    out_specs = [ANY] * n_co + [row_spec(c, 0) if isinstance(c, int) else spec(tm, tn) for c, spec in epi.row_outs]
    out_specs += [pl.BlockSpec((1, tn), lambda j, i, k: (0, j)) if epi.tiled else
                  pl.BlockSpec((1, c), lambda j, i, k: (0, 0)) for c in epi.acc_outs]
    out_shape = list(comm.outs if comm else [])
    out_shape += [jax.ShapeDtypeStruct((M, c), dt_) if isinstance(c, int) else c for c, dt_ in epi.row_outs]
    out_shape += [jax.ShapeDtypeStruct((1, c), F32) for c in epi.acc_outs]
    scratch = ([pltpu.VMEM((tm, tn), F32)] if nk > 1 else []) + epi.scratch
    if comm:
        scratch += [pltpu.SemaphoreType.DMA((comm.n_sems,))] * 2
    params = pltpu.CompilerParams(dimension_semantics=("arbitrary",) * 3, vmem_limit_bytes=VMEM_LIMIT,
                                  has_side_effects=comm is not None)
    outs = pl.pallas_call(
        body, name=name, grid=(nj, ni, nk), in_specs=in_specs, out_specs=out_specs, out_shape=out_shape,
        scratch_shapes=scratch, compiler_params=params,
    )(*([a] * n_a), b, *(comm.ins if comm else []), *[arr for arr, _, _ in epi.row_ins], *epi.full_ins)
    return outs[n_co:], (outs[:n_co] if comm else None)


def _mm_tn(a, b, name):
    S, Ka = a.shape
    _, N = b.shape
    tka = _pick(Ka, (1024, 768, 512, 256, 128))
    tn = _pick(N, (1024, 768, 512, 256, 128))
    ts = 1024 if S % 1024 == 0 else 512
    ns = S // ts

    def body(a_ref, b_ref, o_ref, acc_ref):
        s = pl.program_id(2)
        part = _dot_tn(_b(a_ref[...]), _b(b_ref[...]))

        @pl.when(s == 0)
        def _():
            acc_ref[...] = part

        @pl.when(s > 0)
        def _():
            acc_ref[...] += part

        @pl.when(s == ns - 1)
        def _():
            o_ref[...] = acc_ref[...]

    return pl.pallas_call(
        body, name=name, grid=(Ka // tka, N // tn, ns),
        in_specs=[pl.BlockSpec((ts, tka), lambda i, j, s: (s, i)),
                  pl.BlockSpec((ts, tn), lambda i, j, s: (s, j))],
        out_specs=pl.BlockSpec((tka, tn), lambda i, j, s: (i, j)),
        out_shape=jax.ShapeDtypeStruct((Ka, N), F32),
        scratch_shapes=[pltpu.VMEM((tka, tn), F32)],
        compiler_params=_params(("parallel", "parallel", "arbitrary")),
    )(a, b)


def _row_call(body, row_ins, full_ins, row_outs, acc_outs, bs, name):
    S = row_ins[0].shape[0]
    assert S % bs == 0
    in_specs = [pl.BlockSpec((bs, a.shape[1]), lambda i: (i, 0)) for a in row_ins]
    in_specs += [pl.BlockSpec(a.shape, lambda i: (0, 0)) for a in full_ins]
    out_specs = [pl.BlockSpec((bs, c), lambda i: (i, 0)) for c, _ in row_outs]
    out_specs += [pl.BlockSpec(s, lambda i: (0, 0)) for s in acc_outs]
    out_shape = [jax.ShapeDtypeStruct((S, c), dt) for c, dt in row_outs]
    out_shape += [jax.ShapeDtypeStruct(s, F32) for s in acc_outs]
    return pl.pallas_call(
        body, name=name, grid=(S // bs,), in_specs=in_specs, out_specs=out_specs, out_shape=out_shape,
        compiler_params=_params(("arbitrary",)),
    )(*row_ins, *full_ins)


def _rms_vals(x, w):
    r = lax.rsqrt(jnp.mean(x * x, axis=-1, keepdims=True) + RMS_EPS)
    return x * r * w


def _rms_bwd_vals(x, w, dy):
    r = lax.rsqrt(jnp.mean(x * x, axis=-1, keepdims=True) + RMS_EPS)
    xn = x * r
    g = dy * w
    dx = r * (g - xn * jnp.mean(g * xn, axis=-1, keepdims=True))
    dw = jnp.sum(dy * xn, axis=0, keepdims=True)
    return dx, dw


def _rms_fwd(x, w, comm=None):
    S, D = x.shape
    bs = 512
    n_ci = len(comm.ins) if comm else 0
    n_co = len(comm.outs) if comm else 0

    def body(*refs):
        x_ref, w_ref, o_ref = refs[0], refs[1], refs[2 + n_ci]
        i = pl.program_id(0)
        if comm:
            c_args = (refs[2:2 + n_ci], refs[3 + n_ci:3 + n_ci + n_co], refs[-2], refs[-1])

            @pl.when(i == 0)
            def _():
                comm.start(*c_args)

        o_ref[...] = _b(_rms_vals(x_ref[...], w_ref[...]))
        if comm:
            @pl.when(i == S // bs - 1)
            def _():
                comm.finish(*c_args)

    if comm:
        scratch = [pltpu.SemaphoreType.DMA((comm.n_sems,))] * 2
        params = pltpu.CompilerParams(dimension_semantics=("arbitrary",), vmem_limit_bytes=VMEM_LIMIT,
                                      has_side_effects=True)
    else:
        scratch, params = [], _params(("arbitrary",))
    tile = pl.BlockSpec((bs, D), lambda i: (i, 0))
    outs = pl.pallas_call(
        body, name="rms_fwd", grid=(S // bs,),
        in_specs=[tile, pl.BlockSpec(w.shape, lambda i: (0, 0))] + [ANY] * n_ci,
        out_specs=[tile] + [ANY] * n_co,
        out_shape=[jax.ShapeDtypeStruct((S, D), BF16)] + list(comm.outs if comm else []),
        scratch_shapes=scratch, compiler_params=params,
    )(x, w, *(comm.ins if comm else []))
    return (outs[0], outs[1:]) if comm else outs[0]


def _group_rms(t):
    gw = SSM_INNER // SSM_GROUPS
    out = []
    for g in range(SSM_GROUPS):
        tg = t[:, g * gw:(g + 1) * gw]
        out.append(lax.rsqrt(jnp.mean(tg * tg, axis=-1, keepdims=True) + RMS_EPS))
    return out


def _ssm_out_epi(y, z, w):
    gw = SSM_INNER // SSM_GROUPS

    def a_fn(ri, fi, ro):
        zz = ri[1][...]
        t = ri[0][...] * (zz * _sigmoid(zz))
        rs = _group_rms(t)
        for g in range(SSM_GROUPS):
            sl = slice(g * gw, (g + 1) * gw)
            ro[0][:, sl] = _b(t[:, sl] * rs[g] * fi[0][:, sl])
        return ro[0][...]

    def fn(r, ri, fi, ro, ao, first):
        ro[1][...] = r
    return _Epi(fn, [(y, SSM_INNER, 0), (z, SSM_INNER, 0)], [w], [(SSM_INNER, BF16), (D_MODEL, F32)], a_fn=a_fn)


def _mix_out_epi(att_o, ssm_o, gl, x, b_gate, w_post, w_pre):
    def a_fn(ri, fi, ro):
        g = _sigmoid(ri[2][...] + fi[0][...])
        mi = _b(g[:, :D_MODEL] * ri[0][...] + g[:, D_MODEL:] * ri[1][...])
        ro[0][...] = mi
        return mi

    def fn(r, ri, fi, ro, ao, first):
        ro[1][...] = r
        h = ri[3][...] + _rms_vals(r, fi[1][...])
        ro[2][...] = h
        ro[3][...] = _b(_rms_vals(h, fi[2][...]))
    return _Epi(fn, [(att_o, D_MODEL, 0), (ssm_o, D_MODEL, 0), (gl, 2 * D_MODEL, 0), (x, D_MODEL, 0)],
                [b_gate, w_post, w_pre],
                [(D_MODEL, BF16), (D_MODEL, F32), (D_MODEL, F32), (D_MODEL, BF16)], a_fn=a_fn)


def _final_epi(h1, target, w_post):
    def fn(dn, ri, fi, ro, ao, first):
        w = fi[0][...]
        err = ri[0][...] + _rms_vals(dn, w) - ri[1][...]
        row = jnp.mean(err * err, axis=-1, keepdims=True)
        part = 0.5 * jnp.sum(row, axis=0, keepdims=True)
        dh = err * (1.0 / D_MODEL)
        ro[0][...] = dh
        dx, dw = _rms_bwd_vals(dn, w, dh)
        ro[1][...] = _b(dx)
        _acc_into(ao[0], jnp.broadcast_to(part, (1, LANES)), first)
        _acc_into(ao[1], dw, first)
    return _Epi(fn, [(h1, D_MODEL, 0), (target, D_MODEL, 0)], [w_post], [(D_MODEL, F32), (D_MODEL, BF16)],
                [LANES, D_MODEL])


def _mid_epi(dh2, h1, mixed, w_pre, w_post):
    def fn(df, ri, fi, ro, ao, first):
        dx, dwn = _rms_bwd_vals(ri[1][...], fi[0][...], df)
        dh1 = ri[0][...] + dx
        ro[0][...] = dh1
        dm, dwp = _rms_bwd_vals(ri[2][...], fi[1][...], dh1)
        ro[1][...] = _b(dm)
        _acc_into(ao[0], dwn, first)
        _acc_into(ao[1], dwp, first)
    return _Epi(fn, [(dh2, D_MODEL, 0), (h1, D_MODEL, 0), (mixed, D_MODEL, 0)], [w_pre, w_post],
                [(D_MODEL, F32), (D_MODEL, BF16)], [D_MODEL, D_MODEL])


def _gate_epi(att_o, ssm_o, gl, b_gate):
    def fn(d, ri, fi, ro, ao, first):
        g = _sigmoid(ri[2][...] + fi[0][...])
        ga, gs = g[:, :D_MODEL], g[:, D_MODEL:]
        ro[0][...] = _b(ga * d)
        ro[1][...] = _b(gs * d)
        dga = d * ri[0][...] * ga * (1.0 - ga)
        dgs = d * ri[1][...] * gs * (1.0 - gs)
        ro[2][:, :D_MODEL] = _b(dga)
        ro[2][:, D_MODEL:] = _b(dgs)
        _acc_into(ao[0].at[:, pl.ds(0, D_MODEL)], jnp.sum(dga, axis=0, keepdims=True), first)
        _acc_into(ao[0].at[:, pl.ds(D_MODEL, D_MODEL)], jnp.sum(dgs, axis=0, keepdims=True), first)
    return _Epi(fn, [(att_o, D_MODEL, 0), (ssm_o, D_MODEL, 0), (gl, 2 * D_MODEL, 0)], [b_gate],
                [(D_MODEL, BF16), (D_MODEL, BF16), (2 * D_MODEL, BF16)], [2 * D_MODEL])


def _first_epi(du, dh1, x, w_pre):
    def fn(r, ri, fi, ro, ao, first):
        dx, dw = _rms_bwd_vals(ri[2][...], fi[0][...], ri[0][...] + r)
        ro[0][...] = ri[1][...] + dx
        _acc_into(ao[0], dw, first)
    return _Epi(fn, [(du, D_MODEL, 0), (dh1, D_MODEL, 0), (x, D_MODEL, 0)], [w_pre], [(D_MODEL, F32)],
                [D_MODEL])


def _gnorm_epi(y, z, w):
    gw = SSM_INNER // SSM_GROUPS

    def fn(d_all, ri, fi, ro, ao, first):
        zz = ri[1][...]
        yy = ri[0][...]
        sg = _sigmoid(zz)
        sz = zz * sg
        t = yy * sz
        dws = []
        for g in range(d_all.shape[1] // gw):
            sl = slice(g * gw, (g + 1) * gw)
            tg = t[:, sl]
            r = lax.rsqrt(jnp.mean(tg * tg, axis=-1, keepdims=True) + RMS_EPS)
            tn = tg * r
            d = d_all[:, sl]
            gg = d * fi[0][:, sl]
            dt = r * (gg - tn * jnp.mean(gg * tn, axis=-1, keepdims=True))
            ro[0][:, sl] = dt * sz[:, sl]
            ro[1][:, sl] = _b(dt * yy[:, sl] * (sg[:, sl] * (1.0 + zz[:, sl] * (1.0 - sg[:, sl]))))
            dws.append(jnp.sum(d * tn, axis=0, keepdims=True))
        _acc_into(ao[0], jnp.concatenate(dws, axis=1), first)
    return _Epi(fn, [(y, SSM_INNER, 0), (z, SSM_INNER, 0)], [w], [(SSM_INNER, F32), (SSM_INNER, BF16)],
                [SSM_INNER], tiled=True)


def _head_col(stat, h):
    return stat[:, h:h + 1]


def _head_pair_masks(x):
    lane = lax.broadcasted_iota(jnp.int32, x.shape, 1)
    zero = jnp.zeros_like(x)
    return jnp.where(lane < HEAD_DIM, x, zero), jnp.where(lane >= HEAD_DIM, x, zero)


def _attn_fwd(qkv, d, comm=None):
    S = qkv.shape[0]
    blk = ATT_BLOCK
    nblk = S // blk
    nbs = nblk // d
    slopes = _alibi_slopes(N_ATT_HEADS)
    scale = HEAD_DIM ** -0.5
    n_ci = len(comm.ins) if comm else 0
    n_co = len(comm.outs) if comm else 0

    def body(*refs):
        q_ref, kc_ref, kp_ref, vc_ref, vp_ref = refs[:5]
        o_ref, m_ref, l_ref = refs[5 + n_ci:8 + n_ci]
        n = pl.program_id(0)
        if comm:
            c_args = (refs[5:5 + n_ci], refs[8 + n_ci:8 + n_ci + n_co], refs[-2], refs[-1])

            @pl.when(n == 0)
            def _():
                comm.start(*c_args)

        has_prev = (n % nbs) != 0
        ii = lax.broadcasted_iota(jnp.int32, (blk, 2 * blk), 0)
        jj = lax.broadcasted_iota(jnp.int32, (blk, 2 * blk), 1)
        dist_i = blk + ii - jj
        dist = dist_i.astype(F32)
        ok = jnp.logical_and(jnp.logical_and(dist_i >= 0, dist_i <= blk), jnp.logical_or(jj >= blk, has_prev))
        s_scr, p_scr = refs[8 + n_ci + n_co], refs[9 + n_ci + n_co]
        lane = lax.broadcasted_iota(jnp.int32, (blk, LANES), 1)
        for pr in range(N_ATT_HEADS // 2):
            sl = slice(pr * LANES, (pr + 1) * LANES)
            kcat = jnp.concatenate([kp_ref[:, sl], kc_ref[:, sl]], axis=0)
            for h, qh in zip((2 * pr, 2 * pr + 1), _head_pair_masks(q_ref[:, sl])):
                s_scr[h] = _dot_nt(qh, kcat)
        m_all = jnp.zeros((blk, LANES), F32)
        l_all = jnp.zeros((blk, LANES), F32)
        for h in range(N_ATT_HEADS):
            s = jnp.where(ok, s_scr[h] * scale - (slopes[h] * float(d)) * dist, NEG_BIG)
            m = jnp.max(s, axis=-1, keepdims=True)
            p = jnp.exp(s - m)
            l = jnp.sum(p, axis=-1, keepdims=True)
            m_all = jnp.where(lane == h, m, m_all)
            l_all = jnp.where(lane == h, l, l_all)
            p_scr[:, h * 2 * blk:(h + 1) * 2 * blk] = _b(p)
        for pr in range(N_ATT_HEADS // 2):
            sl = slice(pr * LANES, (pr + 1) * LANES)
            vmask = jnp.concatenate(
                _head_pair_masks(jnp.concatenate([vp_ref[:, sl], vc_ref[:, sl]], axis=0)), axis=0)
            o_ref[:, sl] = _dot(p_scr[:, pr * 4 * blk:(pr + 1) * 4 * blk], vmask)
        m_ref[...] = m_all
        l_ref[...] = l_all
        if comm:
            @pl.when(n == nblk - 1)
            def _():
                comm.finish(*c_args)

    cur = lambda c: pl.BlockSpec((blk, ATT_WIDTH), lambda n: (n, c))
    prev = lambda c: pl.BlockSpec((blk, ATT_WIDTH), lambda n: (jnp.maximum(n - 1, 0), c))
    stat = pl.BlockSpec((blk, LANES), lambda n: (n, 0))
    scratch = [pltpu.VMEM((N_ATT_HEADS, blk, 2 * blk), F32), pltpu.VMEM((blk, N_ATT_HEADS * 2 * blk), BF16)]
    if comm:
        scratch += [pltpu.SemaphoreType.DMA((comm.n_sems,))] * 2
        params = pltpu.CompilerParams(dimension_semantics=("arbitrary",), vmem_limit_bytes=VMEM_LIMIT,
                                      has_side_effects=True)
    else:
        params = _params(("parallel",))
    outs = pl.pallas_call(
        body, name=f"attn_fwd_d{d}", grid=(nblk,),
        in_specs=[cur(0), cur(1), prev(1), cur(2), prev(2)] + [ANY] * n_ci,
        out_specs=[cur(0), stat, stat] + [ANY] * n_co,
        out_shape=[jax.ShapeDtypeStruct((S, ATT_WIDTH), F32), jax.ShapeDtypeStruct((S, LANES), F32),
                   jax.ShapeDtypeStruct((S, LANES), F32)] + list(comm.outs if comm else []),
        scratch_shapes=scratch,
        compiler_params=params,
    )(qkv, qkv, qkv, qkv, qkv, *(comm.ins if comm else []))
    return (outs[0], outs[1], outs[2], outs[3:]) if comm else outs


def _attn_bwd(qkv, do, lse, delta, d, comm=None):
    S = qkv.shape[0]
    blk = ATT_BLOCK
    nblk = S // blk
    nbs = nblk // d
    slopes = _alibi_slopes(N_ATT_HEADS)
    scale = HEAD_DIM ** -0.5
    n_ci = len(comm.ins) if comm else 0
    n_co = len(comm.outs) if comm else 0

    def body(*refs):
        qc_ref, qn_ref, k_ref, v_ref, doc_ref, don_ref, lc_ref, ln_ref, dc_ref, dn_ref = refs[:10]
        dq_ref, dk_ref, dv_ref = refs[10 + n_ci:13 + n_ci]
        carry_ref = refs[13 + n_ci + n_co]
        n = pl.program_id(0)
        has_next = ((n + 1) % nbs) != 0
        if comm:
            c_args = (refs[10:10 + n_ci], refs[13 + n_ci:13 + n_ci + n_co], refs[-2], refs[-1])

        @pl.when(n == 0)
        def _():
            carry_ref[...] = jnp.zeros_like(carry_ref)
            if comm:
                comm.start(*c_args)

        rr = lax.broadcasted_iota(jnp.int32, (2 * blk, blk), 0)
        jj = lax.broadcasted_iota(jnp.int32, (2 * blk, blk), 1)
        dist_i = rr - jj
        dist = dist_i.astype(F32)
        ok = jnp.logical_or(jnp.logical_and(rr < blk, dist_i >= 0),
                            jnp.logical_and(jnp.logical_and(rr >= blk, dist_i <= blk), has_next))
        s_scr, dp_scr, p_rows, ds_rows, ds_cols = refs[14 + n_ci + n_co:19 + n_ci + n_co]
        lcat = jnp.concatenate([lc_ref[...], ln_ref[...]], axis=0)
        dcat = jnp.concatenate([dc_ref[...], dn_ref[...]], axis=0)
        rows2 = 2 * blk

        def operands(pr):
            sl = slice(pr * LANES, (pr + 1) * LANES)
            qm = _head_pair_masks(jnp.concatenate([qc_ref[:, sl], qn_ref[:, sl]], axis=0))
            dom = _head_pair_masks(jnp.concatenate([doc_ref[:, sl], don_ref[:, sl]], axis=0))
            return sl, qm, dom

        for pr in range(N_ATT_HEADS // 2):
            sl, qm, dom = operands(pr)
            for h, qh, doh in zip((2 * pr, 2 * pr + 1), qm, dom):
                s_scr[h] = _dot_nt(qh, k_ref[:, sl])
                dp_scr[h] = _dot_nt(doh, v_ref[:, sl])
        for h in range(N_ATT_HEADS):
            s = jnp.where(ok, s_scr[h] * scale - (slopes[h] * float(d)) * dist - lcat[:, h:h + 1], NEG_BIG)
            p = jnp.exp(s)
            dsb = _b(p * (dp_scr[h] - dcat[:, h:h + 1]) * scale)
            p_rows[h * rows2:(h + 1) * rows2, :] = _b(p)
            ds_rows[h * rows2:(h + 1) * rows2, :] = dsb
            ds_cols[:, h * blk:(h + 1) * blk] = dsb
        for pr in range(N_ATT_HEADS // 2):
            sl, qm, dom = operands(pr)
            pair_rows = slice(pr * 2 * rows2, (pr + 1) * 2 * rows2)
            dv_ref[:, sl] = _b(_dot_tn(p_rows[pair_rows, :], jnp.concatenate(dom, axis=0)))
            dk_ref[:, sl] = _b(_dot_tn(ds_rows[pair_rows, :], jnp.concatenate(qm, axis=0)))
            dq = _dot(ds_cols[:, pr * 2 * blk:(pr + 1) * 2 * blk],
                      jnp.concatenate(_head_pair_masks(k_ref[:, sl]), axis=0))
            dq_ref[:, sl] = _b(dq[:blk] + carry_ref[:, sl])
            carry_ref[:, sl] = dq[blk:]

        if comm:
            @pl.when(n == nblk - 1)
            def _():
                comm.finish(*c_args)

    cur = lambda c: pl.BlockSpec((blk, ATT_WIDTH), lambda n: (n, c))
    nxt = lambda c: pl.BlockSpec((blk, ATT_WIDTH), lambda n: (jnp.minimum(n + 1, nblk - 1), c))
    scur = pl.BlockSpec((blk, LANES), lambda n: (n, 0))
    snxt = pl.BlockSpec((blk, LANES), lambda n: (jnp.minimum(n + 1, nblk - 1), 0))
    shp = jax.ShapeDtypeStruct((S, ATT_WIDTH), BF16)
    scratch = [pltpu.VMEM((blk, ATT_WIDTH), F32),
               pltpu.VMEM((N_ATT_HEADS, 2 * blk, blk), F32), pltpu.VMEM((N_ATT_HEADS, 2 * blk, blk), F32),
               pltpu.VMEM((N_ATT_HEADS * 2 * blk, blk), BF16), pltpu.VMEM((N_ATT_HEADS * 2 * blk, blk), BF16),
               pltpu.VMEM((2 * blk, N_ATT_HEADS * blk), BF16)]
    if comm:
        scratch += [pltpu.SemaphoreType.DMA((comm.n_sems,))] * 2
        params = pltpu.CompilerParams(dimension_semantics=("arbitrary",), vmem_limit_bytes=VMEM_LIMIT,
                                      has_side_effects=True)
    else:
        params = _params(("arbitrary",))
    outs = pl.pallas_call(
        body, name=f"attn_bwd_d{d}", grid=(nblk,),
        in_specs=[cur(0), nxt(0), cur(1), cur(2), cur(0), nxt(0), scur, snxt, scur, snxt] + [ANY] * n_ci,
        out_specs=[cur(0), cur(0), cur(0)] + [ANY] * n_co,
        out_shape=[shp, shp, shp] + list(comm.outs if comm else []),
        scratch_shapes=scratch,
        compiler_params=params,
    )(qkv, qkv, qkv, qkv, do, do, lse, lse, delta, delta, *(comm.ins if comm else []))
    return (outs[0], outs[1], outs[2], outs[3:]) if comm else outs


LAYOUT_TILE = 512
DILATED = tuple(d for d in DILATIONS if d > 1)


def _pat_spec(d, cols, col_block=0):
    return pl.BlockSpec((d, LAYOUT_TILE // d, cols), lambda i: (0, i, col_block))


def _pat_view(a, d):
    return a.reshape(d, a.shape[0] // d, a.shape[1])


def _to_slabs(slab_ref, src_ref):
    for cb in range(slab_ref.shape[0]):
        slab_ref[cb] = src_ref[:, cb * LANES:(cb + 1) * LANES].astype(F32)


def _gather_pattern(dst_ref, slab_ref, d, dtype):
    t = slab_ref.shape[1]
    for cb in range(slab_ref.shape[0]):
        one = slab_ref.at[cb]
        for r in range(d):
            dst_ref[r, :, cb * LANES:(cb + 1) * LANES] = one[pl.ds(r, t // d, stride=d), :].astype(dtype)


def _scatter_pattern(slab_ref, src_ref, d, add=False):
    t = slab_ref.shape[1]
    for cb in range(slab_ref.shape[0]):
        one = slab_ref.at[cb]
        for r in range(d):
            idx = pl.ds(r, t // d, stride=d)
            val = src_ref[r, :, cb * LANES:(cb + 1) * LANES]
            if add:
                val = val + one[idx, :]
            one[idx, :] = val


def _pat_out(d, S, cols, dtype, col_tiled):
    def spec(tm, tn):
        if col_tiled:
            return pl.BlockSpec((d, tm // d, tn), lambda j, i, k: (0, i, j))
        return pl.BlockSpec((d, tm // d, cols), lambda j, i, k: (0, i, 0))
    return jax.ShapeDtypeStruct((d, S // d, cols), dtype), spec


def _qkv_epi(S):
    def fn(r, ri, fi, ro, ao, first, scr):
        ro[0][...] = _b(r)
        slab = scr[0]
        for cb in range(slab.shape[0]):
            slab[cb] = r[:, cb * LANES:(cb + 1) * LANES]
        for d, p_ref in zip(DILATED, ro[1:]):
            _gather_pattern(p_ref, slab, d, BF16)
    return _Epi(fn, row_outs=[(3 * ATT_WIDTH, BF16)] + [_pat_out(d, S, 3 * ATT_WIDTH, BF16, True) for d in DILATED],
                tiled=True, scratch=[pltpu.VMEM((ATT_WIDTH // LANES, LAYOUT_TILE, LANES), F32)])


def _attn_combine(os, ms, ls):
    S = os[0].shape[0]
    t = LAYOUT_TILE

    def body(o1, o2, o3, m1, m2, m3, l1, l2, l3, att_ref, lse_ref, so2, so3, sm2, sm3, sl2, sl3):
        for d, src, dst in ((DILATED[0], o2, so2), (DILATED[1], o3, so3), (DILATED[0], m2, sm2),
                            (DILATED[1], m3, sm3), (DILATED[0], l2, sl2), (DILATED[1], l3, sl3)):
            _scatter_pattern(dst, src, d)
        mm = [m1[...], sm2[0], sm3[0]]
        big = jnp.maximum(jnp.maximum(mm[0], mm[1]), mm[2])
        es = [jnp.exp(m - big) for m in mm]
        den = es[0] * l1[...] + es[1] * sl2[0] + es[2] * sl3[0]
        lse_ref[...] = big + jnp.log(den)
        inv = 1.0 / den
        for h in range(N_ATT_HEADS):
            sl = slice(h * HEAD_DIM, (h + 1) * HEAD_DIM)
            cb, hl = divmod(h, 2)
            sll = slice(hl * HEAD_DIM, (hl + 1) * HEAD_DIM)
            num = (_head_col(es[0], h) * o1[:, sl] + _head_col(es[1], h) * so2[cb, :, sll]
                   + _head_col(es[2], h) * so3[cb, :, sll])
            att_ref[:, sl] = num * _head_col(inv, h)

    def specs(c):
        return [pl.BlockSpec((t, c), lambda i: (i, 0))] + [_pat_spec(d, c) for d in DILATED]

    args = [os[0]] + [_pat_view(o, d) for o, d in zip(os[1:], DILATED)]
    args += [ms[0]] + [_pat_view(m, d) for m, d in zip(ms[1:], DILATED)]
    args += [ls[0]] + [_pat_view(l, d) for l, d in zip(ls[1:], DILATED)]
    return pl.pallas_call(
        body, name="attn_combine", grid=(S // t,),
        in_specs=specs(ATT_WIDTH) + specs(LANES) + specs(LANES),
        out_specs=[pl.BlockSpec((t, ATT_WIDTH), lambda i: (i, 0)), pl.BlockSpec((t, LANES), lambda i: (i, 0))],
        out_shape=[jax.ShapeDtypeStruct((S, ATT_WIDTH), F32), jax.ShapeDtypeStruct((S, LANES), F32)],
        scratch_shapes=[pltpu.VMEM((ATT_WIDTH // LANES, t, LANES), F32)] * 2
        + [pltpu.VMEM((1, t, LANES), F32)] * 4,
        compiler_params=_params(("parallel",)),
    )(*args)


def _attn_delta(d_att, att, lse):
    S = d_att.shape[0]
    t = LAYOUT_TILE

    def body(d_ref, a_ref, l_ref, *refs):
        out_refs, d_slab, l_slab, dl_slab = refs[:-3], refs[-3], refs[-2], refs[-1]
        dd = d_ref[...]
        prod = dd * a_ref[...]
        lane = lax.broadcasted_iota(jnp.int32, (t, LANES), 1)
        acc = jnp.zeros((t, LANES), F32)
        for h in range(N_ATT_HEADS):
            s = jnp.sum(prod[:, h * HEAD_DIM:(h + 1) * HEAD_DIM], axis=-1, keepdims=True)
            acc = jnp.where(lane == h, s, acc)
        out_refs[0][...] = _b(dd)
        out_refs[1][...] = acc
        _to_slabs(d_slab, d_ref)
        l_slab[0] = l_ref[...]
        dl_slab[0] = acc
        for k, d in enumerate(DILATED):
            db_ref, ls_ref, dl_ref = out_refs[2 + 3 * k:5 + 3 * k]
            _gather_pattern(db_ref, d_slab, d, BF16)
            _gather_pattern(ls_ref, l_slab, d, F32)
            _gather_pattern(dl_ref, dl_slab, d, F32)

    nat = lambda c: pl.BlockSpec((t, c), lambda i: (i, 0))
    out_specs = [nat(ATT_WIDTH), nat(LANES)]
    out_shape = [jax.ShapeDtypeStruct((S, ATT_WIDTH), BF16), jax.ShapeDtypeStruct((S, LANES), F32)]
    for d in DILATED:
        out_specs += [_pat_spec(d, ATT_WIDTH), _pat_spec(d, LANES), _pat_spec(d, LANES)]
        out_shape += [jax.ShapeDtypeStruct((d, S // d, ATT_WIDTH), BF16),
                      jax.ShapeDtypeStruct((d, S // d, LANES), F32),
                      jax.ShapeDtypeStruct((d, S // d, LANES), F32)]
    outs = pl.pallas_call(
        body, name="attn_delta", grid=(S // t,),
        in_specs=[nat(ATT_WIDTH), nat(ATT_WIDTH), nat(LANES)],
        out_specs=out_specs, out_shape=out_shape,
        scratch_shapes=[pltpu.VMEM((ATT_WIDTH // LANES, t, LANES), F32), pltpu.VMEM((1, t, LANES), F32),
                        pltpu.VMEM((1, t, LANES), F32)],
        compiler_params=_params(("parallel",)),
    )(d_att, att, lse)
    res = [(outs[0], lse, outs[1])]
    for k in range(len(DILATED)):
        db, ls, dl = outs[2 + 3 * k:5 + 3 * k]
        res.append((db.reshape(S, ATT_WIDTH), ls.reshape(S, LANES), dl.reshape(S, LANES)))
    return res


def _sum_qkv(dqs, dks, dvs):
    S = dqs[0].shape[0]
    t = LAYOUT_TILE

    def body(*refs):
        o_ref, scr = refs[-2], refs[-1]
        for part in range(3):
            nat_ref, p_refs = refs[3 * part], refs[3 * part + 1:3 * part + 3]
            _to_slabs(scr, nat_ref)
            for d, p_ref in zip(DILATED, p_refs):
                _scatter_pattern(scr, p_ref, d, add=True)
            for cb in range(ATT_WIDTH // LANES):
                o_ref[:, part * ATT_WIDTH + cb * LANES:part * ATT_WIDTH + (cb + 1) * LANES] = _b(scr[cb])

    in_specs, args = [], []
    for group in (dqs, dks, dvs):
        in_specs += [pl.BlockSpec((t, ATT_WIDTH), lambda i: (i, 0))] + [_pat_spec(d, ATT_WIDTH) for d in DILATED]
        args += [group[0]] + [_pat_view(a, d) for a, d in zip(group[1:], DILATED)]
    return pl.pallas_call(
        body, name="sum_dqkv", grid=(S // t,),
        in_specs=in_specs,
        out_specs=pl.BlockSpec((t, 3 * ATT_WIDTH), lambda i: (i, 0)),
        out_shape=jax.ShapeDtypeStruct((S, 3 * ATT_WIDTH), BF16),
        scratch_shapes=[pltpu.VMEM((ATT_WIDTH // LANES, t, LANES), F32)],
        compiler_params=_params(("parallel",)),
    )(*args)


CONV_COLS = 1024
CONV_ROWS = 512
HALO = 8


def _shift_down(x, k, top_src):
    r8 = lax.broadcasted_iota(jnp.int32, (HALO, x.shape[1]), 0)
    rolled = pltpu.roll(x, k, 0)
    top = jnp.where(r8 < k, pltpu.roll(top_src, k, 0), rolled[0:HALO])
    if x.shape[0] == HALO:
        return top
    return jnp.concatenate([top, rolled[HALO:]], axis=0)


def _shift_up(x, k, bottom_src):
    n = x.shape[0]
    r8 = lax.broadcasted_iota(jnp.int32, (HALO, x.shape[1]), 0)
    rolled = pltpu.roll(x, n - k, 0)
    bottom = jnp.where(r8 >= HALO - k, pltpu.roll(bottom_src, HALO - k, 0), rolled[n - HALO:n])
    return jnp.concatenate([rolled[:n - HALO], bottom], axis=0)


def _conv_pre(x, top_src, w_ref, b_ref):
    shifted = [x] + [_shift_down(x, k, top_src) for k in range(1, SSM_CONV)]
    pre = b_ref[...] + w_ref[SSM_CONV - 1:SSM_CONV, :] * x
    for k in range(1, SSM_CONV):
        pre = pre + w_ref[SSM_CONV - 1 - k:SSM_CONV - k, :] * shifted[k]
    return pre, shifted


def _conv_fwd(xbc, conv_w, conv_b):
    S, C = xbc.shape
    bs, bc = CONV_ROWS, CONV_COLS
    nr = S // bs

    def body(x_ref, halo_ref, w_ref, b_ref, o_ref, pre_ref):
        r = pl.program_id(1)
        halo = jnp.where(r > 0, halo_ref[...], 0.0)
        pre, _ = _conv_pre(x_ref[...], halo, w_ref, b_ref)
        pre_ref[...] = _b(pre)
        o_ref[...] = pre * _sigmoid(pre)

    tile = pl.BlockSpec((bs, bc), lambda c, r: (r, c))
    return pl.pallas_call(
        body, name="conv_fwd", grid=(C // bc, nr),
        in_specs=[tile,
                  pl.BlockSpec((HALO, bc), lambda c, r: (jnp.maximum(r * (bs // HALO) - 1, 0), c)),
                  pl.BlockSpec((SSM_CONV, bc), lambda c, r: (0, c)),
                  pl.BlockSpec((1, bc), lambda c, r: (0, c))],
        out_specs=[tile, tile],
        out_shape=[jax.ShapeDtypeStruct((S, C), F32), jax.ShapeDtypeStruct((S, C), BF16)],
        compiler_params=_params(("parallel", "arbitrary")),
    )(xbc, xbc, conv_w, conv_b)


def _conv_bwd(xbc, pre_all, dact, conv_w):
    S, C = xbc.shape
    bs, bc = CONV_ROWS, CONV_COLS
    nr = S // bs
    hb = bs // HALO
    last_halo = S // HALO - 1

    def dsilu(pre):
        sg = _sigmoid(pre)
        return sg * (1.0 + pre * (1.0 - sg))

    def body(x_ref, p_ref, pn_ref, d_ref, dn_ref, w_ref, dx_ref, dw_ref, db_ref):
        r = pl.program_id(1)
        x = x_ref[...]
        dpre = d_ref[...] * dsilu(p_ref[...].astype(F32))
        dpre_n = jnp.where(r < nr - 1, dn_ref[...], 0.0) * dsilu(pn_ref[...].astype(F32)[0:HALO])
        ups = [dpre] + [_shift_up(dpre, k, dpre_n) for k in range(1, SSM_CONV)]
        dx = w_ref[SSM_CONV - 1:SSM_CONV, :] * dpre
        for k in range(1, SSM_CONV):
            dx = dx + w_ref[SSM_CONV - 1 - k:SSM_CONV - k, :] * ups[k]
        dx_ref[...] = _b(dx)
        parts = [jnp.sum(x * ups[SSM_CONV - 1 - j], axis=0, keepdims=True) for j in range(SSM_CONV)]
        dbp = jnp.sum(dpre, axis=0, keepdims=True)

        @pl.when(r == 0)
        def _():
            for j in range(SSM_CONV):
                dw_ref[j:j + 1, :] = parts[j]
            db_ref[...] = dbp

        @pl.when(r > 0)
        def _():
            for j in range(SSM_CONV):
                dw_ref[j:j + 1, :] += parts[j]
            db_ref[...] += dbp

    tile = pl.BlockSpec((bs, bc), lambda c, r: (r, c))
    nxt = pl.BlockSpec((HALO, bc), lambda c, r: (jnp.minimum((r + 1) * hb, last_halo), c))
    nxt16 = pl.BlockSpec((BF16_ROWS, bc), lambda c, r: (
        jnp.minimum((r + 1) * (bs // BF16_ROWS), S // BF16_ROWS - 1), c))
    return pl.pallas_call(
        body, name="conv_bwd", grid=(C // bc, nr),
        in_specs=[tile, tile, nxt16, tile, nxt, pl.BlockSpec((SSM_CONV, bc), lambda c, r: (0, c))],
        out_specs=[tile,
                   pl.BlockSpec((SSM_CONV, bc), lambda c, r: (0, c)),
                   pl.BlockSpec((1, bc), lambda c, r: (0, c))],
        out_shape=[jax.ShapeDtypeStruct((S, C), BF16), jax.ShapeDtypeStruct((SSM_CONV, C), F32),
                   jax.ShapeDtypeStruct((1, C), F32)],
        compiler_params=_params(("parallel", "arbitrary")),
    )(xbc, pre_all, pre_all, dact, dact, conv_w)


def _softplus(x):
    return jnp.maximum(x, 0.0) + jnp.log(1.0 + jnp.exp(-jnp.abs(x)))


GROUP_W = HEADS_PER_GROUP * SSM_HEAD_DIM
B_COL0 = SSM_INNER
C_COL0 = SSM_INNER + SSM_GROUPS * SSM_STATE


def _ssd_prep(dt_raw, dt_bias, a_neg):
    S = dt_raw.shape[0]
    ch = SSM_CHUNK
    nch = S // ch

    def body(dtr_ref, bias_ref, a_ref, dt_ref, acs_ref, acst_ref, sig_ref):
        x = dtr_ref[...] + bias_ref[...]
        lane = lax.broadcasted_iota(jnp.int32, (ch, LANES), 1)
        dt = jnp.where(lane < SSM_HEADS, _softplus(x), 0.0)
        ii = lax.broadcasted_iota(jnp.int32, (ch, ch), 0)
        jj = lax.broadcasted_iota(jnp.int32, (ch, ch), 1)
        acs = _dot_hi(jnp.where(ii >= jj, 1.0, 0.0), dt * a_ref[...])
        dt_ref[...] = dt
        acs_ref[...] = acs
        acst_ref[0] = acs.T[0:SSM_HEADS, :]
        sig_ref[...] = _sigmoid(x)

    blk = pl.BlockSpec((ch, LANES), lambda c: (c, 0))
    small = pl.BlockSpec((1, LANES), lambda c: (0, 0))
    shp = jax.ShapeDtypeStruct((S, LANES), F32)
    return pl.pallas_call(
        body, name="ssd_prep", grid=(nch,),
        in_specs=[blk, small, small],
        out_specs=[blk, blk, pl.BlockSpec((1, SSM_HEADS, ch), lambda c: (c, 0, 0)), blk],
        out_shape=[shp, shp, jax.ShapeDtypeStruct((nch, SSM_HEADS, ch), F32), shp],
        compiler_params=_params(("parallel",)),
    )(dt_raw, dt_bias, a_neg)


def _expand_heads(arr, g, rows):
    lane = lax.broadcasted_iota(jnp.int32, (rows, GROUP_W), 1) // SSM_HEAD_DIM
    h0 = HEADS_PER_GROUP * g
    out = jnp.broadcast_to(arr[:, h0:h0 + 1], (rows, GROUP_W))
    for j in range(1, HEADS_PER_GROUP):
        out = jnp.where(lane == j, arr[:, h0 + j:h0 + j + 1], out)
    return out


def _seg_matrix(k, lanes_per_head, h0):
    r = lax.broadcasted_iota(jnp.int32, (k, LANES), 0)
    c = lax.broadcasted_iota(jnp.int32, (k, LANES), 1)
    return jnp.where(c == h0 + r // lanes_per_head, 1.0, 0.0).astype(BF16)


def _seg_dot(t, e):
    hi = _b(t)
    lo = _b(t - hi.astype(F32))
    return _dot(hi, e) + _dot(lo, e)


def _head_sums(t, e, rows):
    if rows >= 8:
        return _seg_dot(t, e)
    return _seg_dot(jnp.broadcast_to(t, (8, t.shape[1])), e)[0:rows]


def _pair_masks(x):
    lane = lax.broadcasted_iota(jnp.int32, x.shape, 1)
    zero = jnp.zeros_like(x)
    return jnp.where(lane < SSM_HEAD_DIM, x, zero), jnp.where(lane >= SSM_HEAD_DIM, x, zero)


def _ssd_fwd(xact, dt, acs, acst, dsk_e):
    S = xact.shape[0]
    ch = SSM_CHUNK
    nch = S // ch

    def body(x_ref, dt_ref, acs_ref, acst_ref, dsk_ref, y_ref, hs_ref, h_ref):
        c = pl.program_id(0)

        @pl.when(c == 0)
        def _():
            h_ref[...] = jnp.zeros_like(h_ref)

        dt_all = dt_ref[...]
        acs_all = acs_ref[...]
        acst_all = acst_ref[0]
        alast = acs_all[ch - 1:ch, :]
        eacs = jnp.exp(acs_all)
        wd_all = dt_all * jnp.exp(alast - acs_all)
        dtt = dt_all.T
        cd_all = jnp.exp(alast)
        ii = lax.broadcasted_iota(jnp.int32, (ch, ch), 0)
        jj = lax.broadcasted_iota(jnp.int32, (ch, ch), 1)
        low = ii >= jj
        for g in range(SSM_GROUPS):
            xs = x_ref[:, g * GROUP_W:(g + 1) * GROUP_W]
            bb = _b(x_ref[:, B_COL0 + g * SSM_STATE:B_COL0 + (g + 1) * SSM_STATE])
            cc = _b(x_ref[:, C_COL0 + g * SSM_STATE:C_COL0 + (g + 1) * SSM_STATE])
            cb = _dot_nt(cc, bb)
            xsb = _b(xs)
            ht = h_ref[g]
            rest = (_dot(cc, _b(ht)) * _expand_heads(eacs, g, ch)
                    + dsk_ref[:, g * GROUP_W:(g + 1) * GROUP_W] * xs)
            for p in range(HEADS_PER_GROUP // 2):
                lms = []
                for h in (HEADS_PER_GROUP * g + 2 * p, HEADS_PER_GROUP * g + 2 * p + 1):
                    diff = acs_all[:, h:h + 1] - acst_all[h:h + 1, :]
                    lms.append(_b(cb * jnp.exp(jnp.where(low, diff, -jnp.inf)) * dtt[h:h + 1, :]))
                xa, xb = _pair_masks(xsb[:, p * LANES:(p + 1) * LANES])
                yp = _dot(jnp.concatenate(lms, axis=1), jnp.concatenate([xa, xb], axis=0))
                y_ref[:, g * GROUP_W + p * LANES:g * GROUP_W + (p + 1) * LANES] = (
                    yp + rest[:, p * LANES:(p + 1) * LANES])
            hs_ref[0, g] = ht
            st = _dot_tn(bb, _b(xs * _expand_heads(wd_all, g, ch)))
            h_ref[g] = ht * _expand_heads(cd_all, g, 1) + st

    blk = pl.BlockSpec((ch, LANES), lambda c: (c, 0))
    return pl.pallas_call(
        body, name="ssd_fwd", grid=(nch,),
        in_specs=[pl.BlockSpec((ch, CONV_DIM), lambda c: (c, 0)), blk, blk,
                  pl.BlockSpec((1, SSM_HEADS, ch), lambda c: (c, 0, 0)),
                  pl.BlockSpec((1, SSM_INNER), lambda c: (0, 0))],
        out_specs=[pl.BlockSpec((ch, SSM_INNER), lambda c: (c, 0)),
                   pl.BlockSpec((1, SSM_GROUPS, SSM_STATE, GROUP_W), lambda c: (c, 0, 0, 0))],
        out_shape=[jax.ShapeDtypeStruct((S, SSM_INNER), F32),
                   jax.ShapeDtypeStruct((nch, SSM_GROUPS, SSM_STATE, GROUP_W), F32)],
        scratch_shapes=[pltpu.VMEM((SSM_GROUPS, SSM_STATE, GROUP_W), F32)],
        compiler_params=_params(("arbitrary",)),
    )(xact, dt, acs, acst, dsk_e)


def _ssd_bwd(xact, dt, acs, acst, sig, a_neg, dsk_e, hs, dy):
    S = xact.shape[0]
    ch = SSM_CHUNK
    nch = S // ch
    ng, hg = SSM_GROUPS, HEADS_PER_GROUP
    nbc = SSM_GROUPS * SSM_STATE

    def body(x_ref, dt_ref, acs_ref, acst_ref, sig_ref, a_ref, dsk_ref, hs_ref, dy_ref,
             dx_ref, ddt_ref, st_ref,
             dh_ref, rows_ref, e_dt, e_ea, e_dsd, xdb_s, xddb_s, dzb_s, bcb_s, cb_s, zz_s, ww_s, dc1_s, db1_s,
             dhin_s, dlm_s, lmb_s, gm_s, dcbb_s, t_s, dxd_s, prod_s, csum_s):
        step = pl.program_id(0)

        @pl.when(step == 0)
        def _():
            dh_ref[...] = jnp.zeros_like(dh_ref)
            st_ref[...] = jnp.zeros_like(st_ref)
            rows_ref[...] = jnp.zeros_like(rows_ref)

        dt_all = dt_ref[...]
        acs_all = acs_ref[...]
        alast = acs_all[ch - 1:ch, :]
        eacs = jnp.exp(acs_all)
        dsd_all = jnp.exp(alast - acs_all)
        cd_all = jnp.exp(alast)
        ii = lax.broadcasted_iota(jnp.int32, (ch, ch), 0)
        jj = lax.broadcasted_iota(jnp.int32, (ch, ch), 1)
        low = ii >= jj
        gsl = [slice(g * GROUP_W, (g + 1) * GROUP_W) for g in range(ng)]
        psl = [[slice(g * GROUP_W + p * LANES, g * GROUP_W + (p + 1) * LANES) for p in range(hg // 2)]
               for g in range(ng)]
        seg = [_seg_matrix(GROUP_W, SSM_HEAD_DIM, hg * g) for g in range(ng)]

        def bc(g):
            return (bcb_s[:, g * SSM_STATE:(g + 1) * SSM_STATE],
                    bcb_s[:, nbc + g * SSM_STATE:nbc + (g + 1) * SSM_STATE])

        def dy_pair(g, p):
            return _pair_masks(_b(dy_ref[:, psl[g][p]]))

        bcb_s[...] = _b(x_ref[:, B_COL0:])
        for g in range(ng):
            dt_e = _expand_heads(dt_all, g, ch)
            ea_e = _expand_heads(eacs, g, ch)
            dsd_e = _expand_heads(dsd_all, g, ch)
            e_dt[:, gsl[g]] = dt_e
            e_ea[:, gsl[g]] = ea_e
            e_dsd[:, gsl[g]] = dsd_e
            xd = x_ref[:, gsl[g]] * dt_e
            xdb_s[:, gsl[g]] = _b(xd)
            xddb_s[:, gsl[g]] = _b(xd * dsd_e)
            dzb_s[:, gsl[g]] = _b(dy_ref[:, gsl[g]] * ea_e)
        for g in range(ng):
            bb, cc = bc(g)
            htb = _b(hs_ref[0, g])
            dhnb = _b(dh_ref[g])
            cb_s[g] = _dot_nt(cc, bb)
            zz_s[:, gsl[g]] = _dot(cc, htb)
            ww_s[:, gsl[g]] = _dot(bb, dhnb)
            dc1_s[g] = _dot_nt(dzb_s[:, gsl[g]], htb)
            db1_s[g] = _dot_nt(xddb_s[:, gsl[g]], dhnb)
            dhin_s[g] = _dot_tn(cc, dzb_s[:, gsl[g]])
            for p in range(hg // 2):
                xp = xdb_s[:, psl[g][p]]
                for q, dyh in enumerate(dy_pair(g, p)):
                    dlm_s[hg * g + 2 * p + q] = _dot_nt(dyh, xp)
        for g in range(ng):
            cb = cb_s[g]
            dcb = jnp.zeros((ch, ch), F32)
            for j in range(hg):
                h = hg * g + j
                diff = acs_all[:, h:h + 1] - acst_ref[0, h:h + 1, :]
                decay = jnp.exp(jnp.where(low, diff, -jnp.inf))
                lm = cb * decay
                dlm = dlm_s[h]
                gm = dlm * lm
                dcb = dcb + dlm * decay
                rows_ref[h:h + 1, :] = jnp.sum(gm, axis=0, keepdims=True)
                lmb_s[h * ch:(h + 1) * ch, :] = _b(lm)
                gm_s[:, h * ch:(h + 1) * ch] = gm
            dcbb_s[g] = _b(dcb)
            xs = x_ref[:, gsl[g]]
            dyg = dy_ref[:, gsl[g]]
            ww = ww_s[:, gsl[g]]
            dsd_e = e_dsd[:, gsl[g]]
            t2 = ww * (xs * e_dt[:, gsl[g]] * dsd_e)
            t_s[:, gsl[g]] = dyg * zz_s[:, gsl[g]] * e_ea[:, gsl[g]] - t2
            dhn = dh_ref[g]
            csum_s[0:1, gsl[g]] = jnp.sum(t2, axis=0, keepdims=True)
            csum_s[1:2, gsl[g]] = jnp.sum(dhn * hs_ref[0, g], axis=0, keepdims=True)
            csum_s[2:3, gsl[g]] = jnp.sum(dyg * xs, axis=0, keepdims=True)
            dh_ref[g] = dhin_s[g] + dhn * _expand_heads(cd_all, g, 1)
            dxd_s[:, gsl[g]] = ww * dsd_e
        cols = jnp.zeros((ch, LANES), F32)
        for g in range(ng):
            bb, cc = bc(g)
            dcbb = dcbb_s[g]
            dx_ref[:, C_COL0 + g * SSM_STATE:C_COL0 + (g + 1) * SSM_STATE] = dc1_s[g] + _dot(dcbb, bb)
            dx_ref[:, B_COL0 + g * SSM_STATE:B_COL0 + (g + 1) * SSM_STATE] = db1_s[g] + _dot_tn(dcbb, cc)
            cols = cols + _head_sums(t_s[:, gsl[g]], seg[g], ch)
            for p in range(hg // 2):
                h0 = hg * g + 2 * p
                dxd_s[:, psl[g][p]] += _dot_tn(lmb_s[h0 * ch:(h0 + 2) * ch, :],
                                               jnp.concatenate(dy_pair(g, p), axis=0))
                cols = cols + _head_sums(gm_s[:, h0 * ch:(h0 + 2) * ch], _seg_matrix(2 * ch, ch, h0), ch)
        for g in range(ng):
            dxd = dxd_s[:, gsl[g]]
            xs = x_ref[:, gsl[g]]
            dx_ref[:, gsl[g]] = dsk_ref[:, gsl[g]] * dy_ref[:, gsl[g]] + dxd * e_dt[:, gsl[g]]
            prod_s[:, gsl[g]] = dxd * xs
        ddt = jnp.zeros((ch, LANES), F32)
        dal = jnp.zeros((1, LANES), F32)
        ddsk = jnp.zeros((1, LANES), F32)
        for g in range(ng):
            ddt = ddt + _head_sums(prod_s[:, gsl[g]], seg[g], ch)
            dal = (dal + _head_sums(csum_s[0:1, gsl[g]], seg[g], 1)
                   + cd_all * _head_sums(csum_s[1:2, gsl[g]], seg[g], 1))
            ddsk = ddsk + _head_sums(csum_s[2:3, gsl[g]], seg[g], 1)
        rowi = lax.broadcasted_iota(jnp.int32, (ch, 1), 0)
        dacs = cols - rows_ref[...].T + jnp.where(rowi == ch - 1, dal, 0.0)
        dla = _dot_hi(jnp.where(ii <= jj, 1.0, 0.0), dacs)
        a_row = a_ref[...]
        ddt_raw = (ddt + dla * a_row) * sig_ref[...]
        ddt_ref[...] = _b(ddt_raw)
        st_ref[0:1, :] += jnp.sum(dla * dt_all, axis=0, keepdims=True) * a_row
        st_ref[1:2, :] += ddsk
        st_ref[2:3, :] += jnp.sum(ddt_raw, axis=0, keepdims=True)

    rc = lambda s: nch - 1 - s
    blk = pl.BlockSpec((ch, LANES), lambda s: (rc(s), 0))
    wide = lambda dt_: pltpu.VMEM((ch, SSM_INNER), dt_)
    sq = lambda n, dt_: pltpu.VMEM((n, ch, ch), dt_)
    scratch = [pltpu.VMEM((ng, SSM_STATE, GROUP_W), F32), pltpu.VMEM((LANES, ch), F32),
               wide(F32), wide(F32), wide(F32),
               wide(BF16), wide(BF16), wide(BF16), wide(BF16),
               sq(ng, F32), wide(F32), wide(F32), sq(ng, F32), sq(ng, F32),
               pltpu.VMEM((ng, SSM_STATE, GROUP_W), F32),
               sq(SSM_HEADS, F32),
               pltpu.VMEM((SSM_HEADS * ch, ch), BF16),
               pltpu.VMEM((ch, SSM_HEADS * ch), F32),
               sq(ng, BF16), wide(F32), wide(F32), wide(F32),
               pltpu.VMEM((8, SSM_INNER), F32)]
    return pl.pallas_call(
        body, name="ssd_bwd", grid=(nch,),
        in_specs=[pl.BlockSpec((ch, CONV_DIM), lambda s: (rc(s), 0)), blk, blk,
                  pl.BlockSpec((1, SSM_HEADS, ch), lambda s: (rc(s), 0, 0)), blk,
                  pl.BlockSpec((1, LANES), lambda s: (0, 0)),
                  pl.BlockSpec((1, SSM_INNER), lambda s: (0, 0)),
                  pl.BlockSpec((1, SSM_GROUPS, SSM_STATE, GROUP_W), lambda s: (rc(s), 0, 0, 0)),
                  pl.BlockSpec((ch, SSM_INNER), lambda s: (rc(s), 0))],
        out_specs=[pl.BlockSpec((ch, CONV_DIM), lambda s: (rc(s), 0)), blk,
                   pl.BlockSpec((8, LANES), lambda s: (0, 0))],
        out_shape=[jax.ShapeDtypeStruct((S, CONV_DIM), F32), jax.ShapeDtypeStruct((S, LANES), BF16),
                   jax.ShapeDtypeStruct((8, LANES), F32)],
        scratch_shapes=scratch,
        compiler_params=_params(("arbitrary",)),
    )(xact, dt, acs, acst, sig, a_neg, dsk_e, hs, dy)


def _pad_lanes(v, n=LANES):
    return jnp.pad(v, ((0, 0), (0, n - v.shape[1])))


def _local_step(x, target, w, ex=None, u=None):
    offs = np.cumsum((0,) + IN_SPLITS)
    wt_in = w["w_in_t"]
    w_qkv = wt_in[offs[0]:offs[3]]
    w_z = wt_in[offs[3]:offs[4]]
    w_xbc = wt_in[offs[4]:offs[5]]
    w_dt = jnp.pad(wt_in[offs[5]:offs[6]], ((0, LANES - SSM_HEADS), (0, 0)))
    w_g = wt_in[offs[6]:offs[7]]
    dt_bias = _pad_lanes(w["dt_bias"])
    a_neg = _pad_lanes(-jnp.exp(w["a_log"]))

    if u is None:
        u = _rms_fwd(x, w["norm_mix_pre_w"])
    if ex is None:
        xbc = _mm_nn(u, w_xbc, F32, "proj_xbc", tb=True)
    else:
        xbc, got = _mm_nn(u, w_xbc, F32, "proj_xbc", comm=_gather_comm([ex.mine[REST_EARLY]]), tb=True)
        w = {**w, **ex.rest_weights(got[0], REST_EARLY)}
    n_tok = x.shape[0]
    qkv_outs, _ = _mm_epi(u, w_qkv, _qkv_epi(n_tok), "proj_qkv", tb=True, tn=ATT_WIDTH)
    z = _mm_nn(u, w_z, F32, "proj_z", tb=True)
    dt_raw = _mm_nn(u, w_dt, F32, "proj_dt", tb=True)
    gl = _mm_nn(u, w_g, F32, "proj_gate", tb=True)

    pats = [qkv_outs[0]] + [o.reshape(n_tok, 3 * ATT_WIDTH) for o in qkv_outs[1:]]
    os_, ms_, ls_ = [], [], []
    for i, (d, qkv_p) in enumerate(zip(DILATIONS, pats)):
        if ex is not None and i < len(REST_LATE):
            o, m, l, got = _attn_fwd(qkv_p, d, comm=_gather_comm([ex.mine[REST_LATE[i]]]))
            w = {**w, **ex.rest_weights(got[0], REST_LATE[i])}
        else:
            o, m, l = _attn_fwd(qkv_p, d)
        os_.append(o)
        ms_.append(m)
        ls_.append(l)
    att, lse = _attn_combine(os_, ms_, ls_)
    att_o = _mm_nn(att, w["w_att_proj"], F32, "att_proj")

    xact, conv_pre = _conv_fwd(xbc, w["conv_w"], w["conv_b"])
    dsk_e = jnp.repeat(w["d_skip"], SSM_HEAD_DIM, axis=1)
    dt, acs, acst, sig = _ssd_prep(dt_raw, dt_bias, a_neg)
    y_ssd, hs = _ssd_fwd(xact, dt, acs, acst, dsk_e)
    (ssm_y, ssm_o), _ = _mm_epi(None, w["w_ssm_proj"], _ssm_out_epi(y_ssd, z, w["ssm_norm_w"]), "ssm_proj")

    (mi, mixed, h1, f), _ = _mm_epi(None, w["w_out"], _mix_out_epi(
        att_o, ssm_o, gl, x, w["b_gate"], w["norm_mix_post_w"], w["norm_ffn_pre_w"]), "out_proj")
    r_up, act = _mm_nn(f, w["w_up"], BF16, "ffn_up", mode="relu2")
    (dh2, d_down, loss, g_ffn_post), _ = _mm_epi(
        act, w["w_down"], _final_epi(h1, target, w["norm_ffn_post_w"]), "ffn_down")

    g = {"norm_ffn_post_w": g_ffn_post}
    g["w_down"] = _mm_tn(act, d_down, "dw_down")
    dup = _mm_nn(d_down, w["w_down"], BF16, "d_act", mode="mul2", extra=r_up, tb=True)
    g["w_up"] = _mm_tn(f, dup, "dw_up")
    (dh1, d_mixed, g["norm_ffn_pre_w"], g["norm_mix_post_w"]), _ = _mm_epi(
        dup, w["w_up"], _mid_epi(dh2, h1, mixed, w["norm_ffn_pre_w"], w["norm_mix_post_w"]), "d_f", tb=True)
    g["w_out"] = _mm_tn(mi, d_mixed, "dw_out")
    (d_att_o, d_ssm_o, dgl, g["b_gate"]), _ = _mm_epi(
        d_mixed, w["w_out"], _gate_epi(att_o, ssm_o, gl, w["b_gate"]), "d_mi", tb=True)

    g["w_att_proj"] = _mm_tn(att, d_att_o, "dw_att_proj")
    g["w_ssm_proj"] = _mm_tn(ssm_y, d_ssm_o, "dw_ssm_proj")
    gn_epi = _gnorm_epi(y_ssd, z, w["ssm_norm_w"])
    if ex is None:
        (dy_ssd, dz, g["ssm_norm_w"]), _ = _mm_epi(d_ssm_o, w["w_ssm_proj"], gn_epi, "d_ssm_y", tb=True,
                                                    tn=PACK_COLS)
    else:
        gs_rest = jnp.concatenate(
            [_shards_from_full(n, g[n]).reshape(N_CHIPS, -1, PACK_COLS) for n in REST], axis=1)
        (dy_ssd, dz, g["ssm_norm_w"]), recv = _mm_epi(d_ssm_o, w["w_ssm_proj"], gn_epi, "d_ssm_y", tb=True,
                                                       tn=PACK_COLS, tm=1024, comm=_pair_comm([gs_rest]))
        p_rest = _pair_add(gs_rest, recv[0], ex.c_arr, "rs_pair_add_rest")

    d_att = _mm_nn(d_att_o, w["w_att_proj"], F32, "d_att", tb=True)
    bwd_ins = _attn_delta(d_att, att, lse)
    dqs, dks, dvs = [], [], []
    for d, qkv_p, (do_p, lse_p, delta_p) in zip(DILATIONS, pats, bwd_ins):
        if ex is not None and d == DILATIONS[0]:
            dq, dk, dv, recv3 = _attn_bwd(qkv_p, do_p, lse_p, delta_p, d, comm=_chip_comm([p_rest]))
            q_rest = _chip_add(p_rest, recv3[0], ex.chip_arr, "rs_chip_add_rest")
            ex.finish_reduce("rest", q_rest, _comm_call("rs_share_rest", _share_comm([q_rest]))[0])
        else:
            dq, dk, dv = _attn_bwd(qkv_p, do_p, lse_p, delta_p, d)
        dqs.append(dq)
        dks.append(dk)
        dvs.append(dv)
    dqkv = _sum_qkv(dqs, dks, dvs)

    dxact, ddt_raw, stats = _ssd_bwd(xact, dt, acs, acst, sig, a_neg, dsk_e, hs, dy_ssd)
    g["a_log"] = stats[0:1, :SSM_HEADS]
    g["d_skip"] = stats[1:2, :SSM_HEADS]
    g["dt_bias"] = stats[2:3, :SSM_HEADS]
    dxbc, g["conv_w"], g["conv_b"] = _conv_bwd(xbc, conv_pre, dxact, w["conv_w"])

    pieces = [(dqkv, w_qkv), (dz, w_z), (dxbc, w_xbc), (ddt_raw, w_dt), (dgl, w_g)]
    gw = [_mm_tn(dp, u, f"dw_in_{i}") for i, (dp, _) in enumerate(pieces)]
    gw[3] = gw[3][:SSM_HEADS]
    if ex is None:
        g["w_in_t"] = jnp.concatenate(gw, axis=0)
    du = None
    for i, (dp, wp) in enumerate([pieces[k] for k in (1, 2, 0, 3, 4)]):
        if ex is not None and i == 0:
            gs_in = _rows_to_shards(gw, IN_SHARD_ROWS, IN_SHARD_PAD)
            du, recv = _mm_nn(dp, wp, F32, f"d_u_{i}", acc=du, comm=_pair_comm([gs_in]))
            p_in = _pair_add(gs_in, recv[0], ex.c_arr, "rs_pair_add_in")
            rows = p_in.shape[1] // 2
            p_parts = [p_in[:, :rows], p_in[:, rows:]]
            q_parts = []
        elif ex is not None and i in (1, 2):
            p_part = p_parts[i - 1]
            du, recv3 = _mm_nn(dp, wp, F32, f"d_u_{i}", acc=du, comm=_chip_comm([p_part]))
            q_parts.append(_chip_add(p_part, recv3[0], ex.chip_arr, f"rs_chip_add_in_{i}"))
            if i == 2:
                others = _comm_call("rs_share_in", _share_comm(q_parts))
                ex.finish_reduce("w_in", jnp.concatenate(q_parts, axis=0), jnp.concatenate(others, axis=0))
        elif i == len(pieces) - 1:
            (grad_x, g["norm_mix_pre_w"]), _ = _mm_epi(
                dp, wp, _first_epi(du, dh1, x, w["norm_mix_pre_w"]), f"d_u_{i}")
        else:
            du = _mm_nn(dp, wp, F32, f"d_u_{i}", acc=du)
    return loss, grad_x, g


def _rows_to_shards(pieces, shard_rows, pad_rows):
    cols = pieces[0].shape[1]
    shards = []
    for s in range(N_CHIPS):
        lo, hi = s * shard_rows, (s + 1) * shard_rows
        parts, r0 = [], 0
        for p in pieces:
            a, b = max(lo, r0), min(hi, r0 + p.shape[0])
            if a < b:
                parts.append(p[a - r0:b - r0])
            r0 += p.shape[0]
        parts.append(jnp.zeros((pad_rows - shard_rows, cols), pieces[0].dtype))
        shards.append(jnp.concatenate(parts, axis=0))
    return jnp.stack(shards)


BIG = ("w_in", "w_att_proj", "w_ssm_proj", "w_out", "w_up", "w_down")
BIG_FULL_SHAPES = {"w_in": (D_MODEL, IN_PROJ_WIDTH), "w_att_proj": (ATT_WIDTH, D_MODEL),
                   "w_ssm_proj": (SSM_INNER, D_MODEL), "w_out": (D_MODEL, D_MODEL),
                   "w_up": (D_MODEL, FFN_HIDDEN), "w_down": (FFN_HIDDEN, D_MODEL)}
BIG_COL_SHARDED = {"w_in": True, "w_att_proj": True, "w_ssm_proj": False, "w_out": False, "w_up": True,
                   "w_down": False}
PACK_COLS = 1024
SMALL = ("norm_mix_pre_w", "b_gate", "conv_b", "dt_bias", "a_log", "d_skip", "ssm_norm_w",
         "norm_mix_post_w", "norm_ffn_pre_w", "norm_ffn_post_w")
SMALL_ROWS = 232


def _shard_shape(name):
    r, c = BIG_FULL_SHAPES[name]
    return (r, c // N_CHIPS) if BIG_COL_SHARDED[name] else (r // N_CHIPS, c)


def _mesh_pos():
    return lax.axis_index("x"), lax.axis_index("y"), lax.axis_index("c")


def _other_chips(x, y):
    return [(1 - x, y), (x, 1 - y), (1 - x, 1 - y)]


ANY = pl.BlockSpec(memory_space=pl.ANY)


REST_EARLY = ("w_att_proj", "w_ssm_proj")
REST_LATE = (("w_up",), ("w_down",), ("w_out",))
REST = REST_EARLY + REST_LATE[0] + REST_LATE[1] + REST_LATE[2]
ADD_ROWS_CAP = 800
BF16_ROWS = 16
IN_SHARD_ROWS = IN_PROJ_WIDTH // N_CHIPS
IN_SHARD_PAD = 2688


def _stack_rest(shards, dtype, names=REST):
    return jnp.concatenate([shards[n].astype(dtype).reshape(-1, PACK_COLS) for n in names], axis=0)


def _unstack_rest(stacked, lead=(), names=REST):
    out, r0 = {}, 0
    for n in names:
        shp = _shard_shape(n)
        rows = shp[0] * shp[1] // PACK_COLS
        out[n] = stacked[..., r0:r0 + rows, :].reshape(lead + shp)
        r0 += rows
    return out


def _full_from_shards(name, sh):
    if BIG_COL_SHARDED[name]:
        return sh.transpose(1, 0, 2).reshape(BIG_FULL_SHAPES[name])
    return sh.reshape(BIG_FULL_SHAPES[name])


def _shards_from_full(name, full):
    shp = _shard_shape(name)
    if BIG_COL_SHARDED[name]:
        return full.reshape(shp[0], N_CHIPS, shp[1]).transpose(1, 0, 2)
    return full.reshape((N_CHIPS,) + shp)


def _pair_add(g, recv, c_idx, name):
    _, half, cols = recv.shape
    rb = _row_block(half, ADD_ROWS_CAP, BF16_ROWS)
    nb = half // rb

    def body(c_ref, g_ref, r_ref, o_ref):
        o_ref[...] = _b(g_ref[...] + r_ref[...])

    blk = (1, rb, cols)
    return pl.pallas_call(
        body, name=name,
        grid_spec=pltpu.PrefetchScalarGridSpec(
            num_scalar_prefetch=1, grid=(N_CHIPS, nb),
            in_specs=[pl.BlockSpec(blk, lambda s, i, c: (s, c[0] * nb + i, 0)),
                      pl.BlockSpec(blk, lambda s, i, c: (s, i, 0))],
            out_specs=pl.BlockSpec(blk, lambda s, i, c: (s, i, 0))),
        out_shape=jax.ShapeDtypeStruct(recv.shape, BF16),
        compiler_params=_params(("arbitrary", "arbitrary")),
    )(c_idx, g, recv)


def _chip_add(p, recv, me_idx, name):
    _, half, cols = recv.shape
    rb = _row_block(half, ADD_ROWS_CAP, BF16_ROWS)

    def body(m_ref, p_ref, r0_ref, r1_ref, r2_ref, o_ref):
        o_ref[...] = ((p_ref[0].astype(F32) + r0_ref[0].astype(F32)) + r1_ref[0].astype(F32)) + r2_ref[0].astype(F32)

    blk = (1, rb, cols)
    return pl.pallas_call(
        body, name=name,
        grid_spec=pltpu.PrefetchScalarGridSpec(
            num_scalar_prefetch=1, grid=(half // rb,),
            in_specs=[pl.BlockSpec(blk, lambda i, m: (m[0], i, 0)),
                      pl.BlockSpec(blk, lambda i, m: (0, i, 0)),
                      pl.BlockSpec(blk, lambda i, m: (1, i, 0)),
                      pl.BlockSpec(blk, lambda i, m: (2, i, 0))],
            out_specs=pl.BlockSpec((rb, cols), lambda i, m: (i, 0))),
        out_shape=jax.ShapeDtypeStruct((half, cols), F32),
        compiler_params=_params(("arbitrary",)),
    )(me_idx, p, recv, recv, recv)


def _gather_plan():
    def copies(w_refs, out_refs, send_sems, recv_sems):
        x, y, c = _mesh_pos()
        me = 2 * x + y
        sibling = (x, y, 1 - c)
        chips = _other_chips(x, y)
        idx = [2 * chip[0] + chip[1] for chip in chips]
        plans = []
        for a, (w_ref, out_ref) in enumerate(zip(w_refs, out_refs)):
            half = w_ref.shape[0] // 2

            def copy(k, chip, h, to, src=None, out_ref=out_ref, half=half, a=a):
                rows = out_ref.at[chip, pl.ds(h * half, half), :]
                return pltpu.make_async_remote_copy(
                    src_ref=rows if src is None else src, dst_ref=rows,
                    send_sem=send_sems.at[6 * a + k], recv_sem=recv_sems.at[6 * a + k],
                    device_id=to, device_id_type=MESH)

            mine_half = w_ref.at[pl.ds(c * half, half), :]
            send = [copy(j, me, c, (*chip, c), src=mine_half) for j, chip in enumerate(chips)]
            land = [copy(j, idx[j], c, (x, y, c)) for j in range(N_CHIPS - 1)]
            forward = [copy(3 + j, idx[j], c, sibling) for j in range(N_CHIPS - 1)]
            land_fw = [copy(3 + j, idx[j], 1 - c, (x, y, c)) for j in range(N_CHIPS - 1)]
            plans.append((send, land, forward, land_fw))
        return plans

    def start(*refs):
        for send, _, _, _ in copies(*refs):
            for cp in send:
                cp.start()

    def finish(*refs):
        plans = copies(*refs)
        for _, land, forward, _ in plans:
            for j in range(N_CHIPS - 1):
                land[j].wait_recv()
                forward[j].start()
        for _, _, _, land_fw in plans:
            for cp in land_fw:
                cp.wait_recv()
        for send, _, forward, _ in plans:
            for cp in send + forward:
                cp.wait_send()

    return start, finish


def _pair_plan(halves):
    def copies(in_refs, out_refs, send_sems, recv_sems):
        x, y, c = _mesh_pos()
        cps = []
        for a, (g_ref, out_ref) in enumerate(zip(in_refs, out_refs)):
            if halves:
                half = g_ref.shape[1] // 2
                src = g_ref.at[:, pl.ds((1 - c) * half, half), :]
            else:
                src = g_ref
            cps.append(pltpu.make_async_remote_copy(
                src_ref=src, dst_ref=out_ref, send_sem=send_sems.at[a], recv_sem=recv_sems.at[a],
                device_id=(x, y, 1 - c), device_id_type=MESH))
        return cps

    def start(*refs):
        for cp in copies(*refs):
            cp.start()

    def finish(*refs):
        for cp in copies(*refs):
            cp.wait()

    return start, finish


def _chip_plan():
    def copies(in_refs, out_refs, send_sems, recv_sems):
        x, y, c = _mesh_pos()
        chips = _other_chips(x, y)
        return [pltpu.make_async_remote_copy(
            src_ref=p_ref.at[2 * chip[0] + chip[1]], dst_ref=out_ref.at[j],
            send_sem=send_sems.at[3 * a + j], recv_sem=recv_sems.at[3 * a + j], device_id=(*chip, c),
            device_id_type=MESH)
            for a, (p_ref, out_ref) in enumerate(zip(in_refs, out_refs)) for j, chip in enumerate(chips)]

    def start(*refs):
        for cp in copies(*refs):
            cp.start()

    def finish(*refs):
        cps = copies(*refs)
        for cp in cps:
            cp.wait_recv()
        for cp in cps:
            cp.wait_send()

    return start, finish


def _gather_comm(shards):
    return _Comm(_gather_plan(), shards, [jax.ShapeDtypeStruct((N_CHIPS,) + s.shape, s.dtype) for s in shards],
                 6 * len(shards))


def _pair_comm(gs):
    return _Comm(_pair_plan(True), gs,
                 [jax.ShapeDtypeStruct((N_CHIPS, g.shape[1] // 2, g.shape[2]), g.dtype) for g in gs], len(gs))


def _chip_comm(ps):
    return _Comm(_chip_plan(), ps, [jax.ShapeDtypeStruct((N_CHIPS - 1,) + p.shape[1:], p.dtype) for p in ps],
                 3 * len(ps))


def _share_comm(qs):
    return _Comm(_pair_plan(False), qs, [jax.ShapeDtypeStruct(q.shape, q.dtype) for q in qs], len(qs))


def _comm_call(name, comm):
    n, m = len(comm.ins), len(comm.outs)

    def body(*refs):
        args = (refs[:n], refs[n:n + m], refs[n + m], refs[n + m + 1])
        comm.start(*args)
        comm.finish(*args)

    return pl.pallas_call(
        body, name=name, out_shape=comm.outs, in_specs=[ANY] * n, out_specs=[ANY] * m,
        scratch_shapes=[pltpu.SemaphoreType.DMA((comm.n_sems,))] * 2,
        compiler_params=pltpu.CompilerParams(has_side_effects=True),
    )(*comm.ins)


class _Exchange:
    def __init__(self, chip, ci, mine):
        self.chip, self.ci = chip, ci
        self.mine = mine
        self.c_arr = ci.reshape(1).astype(jnp.int32)
        self.chip_arr = chip.reshape(1).astype(jnp.int32)
        self.reduced = {}

    def rest_weights(self, got, names):
        stacks = lax.dynamic_update_slice(got, self.mine[names][None], (self.chip, 0, 0))
        return {n: _full_from_shards(n, sh) for n, sh in _unstack_rest(stacks, (N_CHIPS,), names).items()}

    def finish_reduce(self, key, mine, other):
        south = self.ci == 0
        self.reduced[key] = jnp.concatenate([jnp.where(south, mine, other), jnp.where(south, other, mine)],
                                            axis=0)


def _allreduce_small(part, name):
    rows = part.shape[0]

    def body(p_ref, out_ref, buf, send_sems, recv_sems, local_sem):
        x, y, c = _mesh_pos()
        me, sibling = (x, y, c), (x, y, 1 - c)
        chips = _other_chips(x, y)

        def slot(px, py, pc):
            return buf.at[pl.ds((4 * px + 2 * py + pc) * rows, rows), :]

        def copy(k, block, to, src=None):
            return pltpu.make_async_remote_copy(
                src_ref=slot(*block) if src is None else src, dst_ref=slot(*block),
                send_sem=send_sems.at[k], recv_sem=recv_sems.at[k], device_id=to, device_id_type=MESH)

        mine = pltpu.make_async_copy(p_ref, slot(*me), local_sem)
        mine.start()
        first = [copy(0, me, sibling, src=p_ref)]
        first += [copy(1 + j, me, (*chip, c), src=p_ref) for j, chip in enumerate(chips)]
        for cp in first:
            cp.start()
        passed = [copy(4 + j, (*chip, c), sibling) for j, chip in enumerate(chips)]
        for j, chip in enumerate(chips):
            copy(1 + j, (*chip, c), me).wait_recv()
            passed[j].start()
        copy(0, sibling, me).wait_recv()
        for j, chip in enumerate(chips):
            copy(4 + j, (*chip, 1 - c), me).wait_recv()
        for cp in first + passed:
            cp.wait_send()
        mine.wait()
        acc = buf[pl.ds(0, rows), :]
        for k in range(1, N_DEV):
            acc = acc + buf[pl.ds(k * rows, rows), :]
        out_ref[...] = acc

    return pl.pallas_call(
        body, name=name,
        out_shape=jax.ShapeDtypeStruct(part.shape, F32),
        in_specs=[pl.BlockSpec(memory_space=pltpu.VMEM)],
        out_specs=pl.BlockSpec(memory_space=pltpu.VMEM),
        scratch_shapes=[pltpu.VMEM((N_DEV * rows, LANES), F32), pltpu.SemaphoreType.DMA((7,)),
                        pltpu.SemaphoreType.DMA((7,)), pltpu.SemaphoreType.DMA],
        compiler_params=pltpu.CompilerParams(has_side_effects=True),
    )(part)


def _adamw(w, g, m, v, name):
    R, C = w.shape
    bs = _row_block(R, 512, 8) if R % 8 == 0 else R
    c1 = 1.0 / (1.0 - ADAM_B1 ** ADAM_STEP)
    c2 = 1.0 / (1.0 - ADAM_B2 ** ADAM_STEP)

    def body(w_ref, g_ref, m_ref, v_ref, d_ref, nm_ref, nv_ref):
        gg = g_ref[...]
        nm = ADAM_B1 * m_ref[...] + (1.0 - ADAM_B1) * gg
        nv = ADAM_B2 * v_ref[...] + (1.0 - ADAM_B2) * (gg * gg)
        nm_ref[...] = nm
        nv_ref[...] = nv
        d_ref[...] = -ADAM_LR * ((nm * c1) / (jnp.sqrt(nv * c2) + ADAM_EPS) + ADAM_WD * w_ref[...])

    spec = pl.BlockSpec((bs, C), lambda i: (i, 0))
    shp = jax.ShapeDtypeStruct((R, C), F32)
    return pl.pallas_call(
        body, name=name, grid=(R // bs,), in_specs=[spec] * 4, out_specs=[spec] * 3, out_shape=[shp] * 3,
        compiler_params=_params(("parallel",)),
    )(w, g, m, v)


WEIGHTS = ("norm_mix_pre_w", "w_in", "b_gate", "conv_w", "conv_b", "dt_bias", "a_log", "d_skip",
           "ssm_norm_w", "w_att_proj", "w_ssm_proj", "w_out", "norm_mix_post_w", "norm_ffn_pre_w", "w_up",
           "w_down", "norm_ffn_post_w")


def _flat_small(vals, conv_w_full):
    flat = [vals[n].reshape(-1) for n in SMALL] + [conv_w_full.reshape(-1)]
    v = jnp.concatenate(flat)
    return jnp.pad(v, (0, SMALL_ROWS * LANES - v.shape[0])).reshape(SMALL_ROWS, LANES)


def kernel(x, norm_mix_pre_w, w_in, b_gate, conv_w, conv_b, dt_bias, a_log, d_skip, ssm_norm_w, w_att_proj, w_ssm_proj, w_out, norm_mix_post_w, norm_ffn_pre_w, w_up, w_down, norm_ffn_post_w, loss_target, m_norm_mix_pre_w, m_w_in, m_b_gate, m_conv_w, m_conv_b, m_dt_bias, m_a_log, m_d_skip, m_ssm_norm_w, m_w_att_proj, m_w_ssm_proj, m_w_out, m_norm_mix_post_w, m_norm_ffn_pre_w, m_w_up, m_w_down, m_norm_ffn_post_w, v_norm_mix_pre_w, v_w_in, v_b_gate, v_conv_w, v_conv_b, v_dt_bias, v_a_log, v_d_skip, v_ssm_norm_w, v_w_att_proj, v_w_ssm_proj, v_w_out, v_norm_mix_post_w, v_norm_ffn_pre_w, v_w_up, v_w_down, v_norm_ffn_post_w):
    args = locals()

    def strip(a):
        return a[0] if a.ndim == 3 else a

    wts = {n: strip(args[n]) for n in WEIGHTS}
    mom = {n: strip(args["m_" + n]) for n in WEIGHTS}
    var = {n: strip(args["v_" + n]) for n in WEIGHTS}
    xi, yi, ci = _mesh_pos()
    chip = 2 * xi + yi

    tr = lambda a: jnp.swapaxes(a, 0, 1)
    w_in_mine = jnp.pad(tr(wts["w_in"]).astype(BF16), ((0, IN_SHARD_PAD - IN_SHARD_ROWS), (0, 0)))
    u, (got_in,) = _rms_fwd(x[0], wts["norm_mix_pre_w"], comm=_gather_comm([w_in_mine]))
    stacks_in = lax.dynamic_update_slice(got_in, w_in_mine[None], (chip, 0, 0))
    full = {"w_in_t": stacks_in[:, :IN_SHARD_ROWS].reshape(IN_PROJ_WIDTH, D_MODEL)}
    ex = _Exchange(chip, ci, {names: _stack_rest(wts, BF16, names) for names in (REST_EARLY,) + REST_LATE})
    cw_cols = CONV_DIM // N_CHIPS
    conv_slab = lax.dynamic_update_slice(jnp.zeros((SSM_CONV, CONV_DIM), F32),
                                         jnp.where(ci == 0, wts["conv_w"], 0.0), (0, chip * cw_cols))
    small_in = jnp.pad(conv_slab.reshape(-1), (0, SMALL_ROWS * LANES - SSM_CONV * CONV_DIM))
    conv_full = _allreduce_small(small_in.reshape(SMALL_ROWS, LANES), "gather_conv_w")
    full["conv_w"] = conv_full.reshape(-1)[:SSM_CONV * CONV_DIM].reshape(SSM_CONV, CONV_DIM)
    for n in SMALL:
        full[n] = wts[n]

    loss_part, grad_x, g = _local_step(x[0], loss_target[0], full, ex, u)
    loss = lax.psum(loss_part[0, 0], ("x", "y", "c"))

    gshard = _unstack_rest(ex.reduced["rest"])
    g_in_t = ex.reduced["w_in"][:IN_SHARD_ROWS]
    small_sum = _allreduce_small(_flat_small(g, g["conv_w"]), "allreduce_small_grads").reshape(-1)
    grads, off = {}, 0
    for n in SMALL:
        sz = wts[n].size
        grads[n] = small_sum[off:off + sz].reshape(wts[n].shape)
        off += sz
    conv_g = small_sum[off:off + SSM_CONV * CONV_DIM].reshape(SSM_CONV, CONV_DIM)
    grads["conv_w"] = lax.dynamic_slice(conv_g, (0, chip * cw_cols), (SSM_CONV, cw_cols))
    grads.update(gshard)

    delta, new_m, new_v = {}, {}, {}
    for n in REST:
        delta[n], new_m[n], new_v[n] = _adamw(wts[n], grads[n], mom[n], var[n], f"adamw_{n}")
    in_t = _adamw(tr(wts["w_in"]), g_in_t, tr(mom["w_in"]), tr(var["w_in"]), "adamw_w_in")
    grads["w_in"] = tr(g_in_t)
    delta["w_in"], new_m["w_in"], new_v["w_in"] = (tr(a) for a in in_t)
    small_names = SMALL + ("conv_w",)

    def pack_small(d):
        v = jnp.concatenate([d[n].reshape(-1) for n in small_names])
        rows = -(-v.shape[0] // (8 * LANES)) * 8
        return jnp.pad(v, (0, rows * LANES - v.shape[0])).reshape(rows, LANES)

    ds, ms, vs = _adamw(pack_small(wts), pack_small(grads), pack_small(mom), pack_small(var), "adamw_small")
    off = 0
    for n in small_names:
        sz = wts[n].size
        for dst, src in ((delta, ds), (new_m, ms), (new_v, vs)):
            dst[n] = src.reshape(-1)[off:off + sz].reshape(wts[n].shape)
        off += sz

    out = [loss, grad_x[None]]
    for d in (grads, delta, new_m, new_v):
        out += [d[n][None] if args[n].ndim == 3 else d[n] for n in WEIGHTS]
    return tuple(out)
```

```python
import math

import numpy as np
import jax
import jax.numpy as jnp
from jax import lax
from jax.experimental import pallas as pl
from jax.experimental.pallas import tpu as pltpu

F32 = jnp.float32
BF16 = jnp.bfloat16

D_MODEL = 1024
HEAD_DIM = 64
N_ATT_HEADS = 12
ATT_WIDTH = N_ATT_HEADS * HEAD_DIM
DILATIONS = (1, 4, 16)
ATT_BLOCK = 128
SSM_INNER = 2048
SSM_HEADS = 32
SSM_GROUPS = 8
HEADS_PER_GROUP = SSM_HEADS // SSM_GROUPS
SSM_HEAD_DIM = 64
SSM_STATE = 128
SSM_CONV = 4
SSM_CHUNK = 128
CONV_DIM = SSM_INNER + 2 * SSM_GROUPS * SSM_STATE
FFN_HIDDEN = 4 * D_MODEL
IN_SPLITS = (ATT_WIDTH, ATT_WIDTH, ATT_WIDTH, SSM_INNER, CONV_DIM, SSM_HEADS, 2 * D_MODEL)
IN_PROJ_WIDTH = sum(IN_SPLITS)
RMS_EPS = 1e-6
LANES = 128
NEG_BIG = -1e30

ADAM_LR = 0.001
ADAM_B1 = 0.9
ADAM_B2 = 0.999
ADAM_EPS = 1e-08
ADAM_WD = 0.01
ADAM_STEP = 10

N_CHIPS = 4
N_DEV = 8
VMEM_LIMIT = 56 * 1024 * 1024
MESH = pl.DeviceIdType.MESH


def _alibi_slopes(n):
    def pow2(m):
        start = 2.0 ** (-8.0 / m)
        return [start ** (i + 1) for i in range(m)]
    if (n & (n - 1)) == 0:
        s = pow2(n)
    else:
        c = 2 ** int(math.floor(math.log2(n)))
        s = pow2(c) + pow2(2 * c)[0::2][: n - c]
    return [float(v) for v in np.array(s, dtype=np.float32)]


def _params(sem):
    return pltpu.CompilerParams(dimension_semantics=sem, vmem_limit_bytes=VMEM_LIMIT)


def _dot(a, b):
    return lax.dot_general(a, b, (((1,), (0,)), ((), ())), preferred_element_type=F32)


def _dot_nt(a, b):
    return lax.dot_general(a, b, (((1,), (1,)), ((), ())), preferred_element_type=F32)


def _dot_tn(a, b):
    return lax.dot_general(a, b, (((0,), (0,)), ((), ())), preferred_element_type=F32)


def _dot_hi(a, b):
    return lax.dot_general(a, b, (((1,), (0,)), ((), ())), preferred_element_type=F32,
                           precision=lax.Precision.HIGHEST)


def _b(x):
    return x.astype(BF16)


def _sigmoid(x):
    return 1.0 / (1.0 + jnp.exp(-x))


def _pick(n, cands):
    for c in cands:
        if n % c == 0:
            return c
    raise ValueError(f"no tile for {n}")


def _row_block(rows, cap, mult):
    best = max(d for d in range(mult, cap + 1, mult) if rows % d == 0)
    return best


class _Comm:
    def __init__(self, plan, ins, outs, n_sems):
        self.start, self.finish = plan
        self.ins, self.outs, self.n_sems = list(ins), list(outs), n_sems


def _mm_nn(a, b, out_dtype, name, acc=None, mode=None, extra=None, comm=None, tb=False):
    M, K = a.shape
    N = b.shape[0] if tb else b.shape[1]
    tn = _pick(N, ((2048,) if K <= 1024 else ()) + (1024, 768, 512, 256, 128))
    tk = K if K <= 4096 else _pick(K, (2048, 1024))
    tm = 1024 if M % 1024 == 0 and K <= 2304 else 512
    nk = K // tk
    nj, ni = N // tn, M // tm
    side = acc if acc is not None else extra
    n_out = 2 if mode == "relu2" else 1
    n_in = 2 + (side is not None)
    n_ci = len(comm.ins) if comm else 0
    n_co = len(comm.outs) if comm else 0

    def body(*refs):
        a_ref, b_ref = refs[0], refs[1]
        s_ref = refs[2] if side is not None else None
        o_refs = refs[n_in + n_ci:n_in + n_ci + n_out]
        if comm:
            c_args = (refs[n_in:n_in + n_ci], refs[n_in + n_ci + n_out:n_in + n_ci + n_out + n_co],
                      refs[-2], refs[-1])
            pj, pi, pk = pl.program_id(0), pl.program_id(1), pl.program_id(2)

            @pl.when(jnp.logical_and(jnp.logical_and(pj == 0, pi == 0), pk == 0))
            def _():
                comm.start(*c_args)

        def finish(r):
            if mode == "relu2":
                r = jnp.maximum(r, 0.0)
                o_refs[0][...] = _b(r)
                o_refs[1][...] = _b(r * r)
            elif mode == "mul2":
                o_refs[0][...] = _b(r * (2.0 * s_ref[...].astype(F32)))
            else:
                if acc is not None:
                    r = r + s_ref[...]
                o_refs[0][...] = r.astype(out_dtype)

        part = (_dot_nt if tb else _dot)(_b(a_ref[...]), _b(b_ref[...]))
        if nk == 1:
            finish(part)
        else:
            acc_ref = refs[n_in + n_ci + n_out + n_co]
            k = pl.program_id(2)

            @pl.when(k == 0)
            def _():
                acc_ref[...] = part

            @pl.when(jnp.logical_and(k > 0, k < nk - 1))
            def _():
                acc_ref[...] += part

            @pl.when(k == nk - 1)
            def _():
                finish(acc_ref[...] + part)

        if comm:
            @pl.when(jnp.logical_and(jnp.logical_and(pj == nj - 1, pi == ni - 1), pk == nk - 1))
            def _():
                comm.finish(*c_args)

    tile = pl.BlockSpec((tm, tn), lambda j, i, k: (i, j))
    in_specs = [pl.BlockSpec((tm, tk), lambda j, i, k: (i, k)),
                pl.BlockSpec((tn, tk), lambda j, i, k: (j, k)) if tb else
                pl.BlockSpec((tk, tn), lambda j, i, k: (k, j))]
    args = [a, b]
    if side is not None:
        in_specs.append(tile)
        args.append(side)
    odt = BF16 if mode in ("relu2", "mul2") else out_dtype
    scratch = [pltpu.VMEM((tm, tn), F32)] if nk > 1 else []
    if comm:
        scratch += [pltpu.SemaphoreType.DMA((comm.n_sems,))] * 2
        params = pltpu.CompilerParams(dimension_semantics=("arbitrary",) * 3, vmem_limit_bytes=VMEM_LIMIT,
                                      has_side_effects=True)
    else:
        params = _params(("parallel", "parallel", "arbitrary"))
    outs = pl.pallas_call(
        body, name=name, grid=(nj, ni, nk),
        in_specs=in_specs + [ANY] * n_ci,
        out_specs=[tile] * n_out + [ANY] * n_co,
        out_shape=[jax.ShapeDtypeStruct((M, N), odt)] * n_out + list(comm.outs if comm else []),
        scratch_shapes=scratch,
        compiler_params=params,
    )(*args, *(comm.ins if comm else []))
    res = outs[:n_out] if n_out > 1 else outs[0]
    return (res, outs[n_out:]) if comm else res


class _Epi:
    def __init__(self, fn, row_ins=(), full_ins=(), row_outs=(), acc_outs=(), tiled=False, a_fn=None,
                 scratch=()):
        self.fn, self.row_ins, self.full_ins = fn, list(row_ins), list(full_ins)
        self.row_outs, self.acc_outs, self.tiled = list(row_outs), list(acc_outs), tiled
        self.a_fn = a_fn
        self.scratch = list(scratch)


def _acc_into(ref, val, first):
    @pl.when(first)
    def _():
        ref[...] = val

    @pl.when(jnp.logical_not(first))
    def _():
        ref[...] += val


def _mm_epi(a, b, epi, name, tb=False, comm=None, tm=512, tn=None):
    N, K = b.shape if tb else b.shape[::-1]
    M = epi.row_ins[0][0].shape[0] if a is None else a.shape[0]
    tn = tn or N
    assert epi.tiled or tn == N
    tk = K if K <= 4096 else _pick(K, (2048, 1024))
    nk = K // tk
    nj, ni = N // tn, M // tm
    assert a is not None or (nk == 1 and nj == 1)
    n_a = 0 if a is None else 1
    n_ri, n_fi, n_ro, n_ao = len(epi.row_ins), len(epi.full_ins), len(epi.row_outs), len(epi.acc_outs)
    n_ci = len(comm.ins) if comm else 0
    n_co = len(comm.outs) if comm else 0
    i0 = n_a + 1
    o0 = i0 + n_ci + n_ri + n_fi

    def body(*refs):
        b_ref = refs[n_a]
        ri = refs[i0 + n_ci:i0 + n_ci + n_ri]
        fi = refs[i0 + n_ci + n_ri:o0]
        ro = refs[o0 + n_co:o0 + n_co + n_ro]
        ao = refs[o0 + n_co + n_ro:o0 + n_co + n_ro + n_ao]
        pj, pi, pk = pl.program_id(0), pl.program_id(1), pl.program_id(2)
        if comm:
            c_args = (refs[i0:i0 + n_ci], refs[o0:o0 + n_co], refs[-2], refs[-1])

            @pl.when(jnp.logical_and(jnp.logical_and(pj == 0, pi == 0), pk == 0))
            def _():
                comm.start(*c_args)

        s0 = o0 + n_co + n_ro + n_ao + (nk > 1)
        extra = (refs[s0:s0 + len(epi.scratch)],) if epi.scratch else ()
        a_val = epi.a_fn(ri, fi, ro) if a is None else _b(refs[0][...])
        part = (_dot_nt if tb else _dot)(a_val, _b(b_ref[...]))
        if nk == 1:
            epi.fn(part, ri, fi, ro, ao, pi == 0, *extra)
        else:
            acc_ref = refs[o0 + n_co + n_ro + n_ao]

            @pl.when(pk == 0)
            def _():
                acc_ref[...] = part

            @pl.when(jnp.logical_and(pk > 0, pk < nk - 1))
            def _():
                acc_ref[...] += part

            @pl.when(pk == nk - 1)
            def _():
                epi.fn(acc_ref[...] + part, ri, fi, ro, ao, pi == 0, *extra)

        if comm:
            @pl.when(jnp.logical_and(jnp.logical_and(pj == nj - 1, pi == ni - 1), pk == nk - 1))
            def _():
                comm.finish(*c_args)

    def row_spec(width, cb):
        if epi.tiled:
            return pl.BlockSpec((tm, tn), lambda j, i, k: (i, j + cb))
        return pl.BlockSpec((tm, width), lambda j, i, k: (i, cb))

    in_specs = [pl.BlockSpec((tm, tk), lambda j, i, k: (i, k))] * n_a
    in_specs += [pl.BlockSpec((tn, tk), lambda j, i, k: (j, k)) if tb else
                 pl.BlockSpec((tk, tn), lambda j, i, k: (k, j))]
    in_specs += [ANY] * n_ci
    in_specs += [row_spec(w, cb) for (_, w, cb) in epi.row_ins]
    in_specs += [pl.BlockSpec((1, tn), lambda j, i, k: (0, j)) if epi.tiled else
                 pl.BlockSpec(f.shape, lambda j, i, k: (0, 0)) for f in epi.full_ins]
    out_specs = [ANY] * n_co + [row_spec(c, 0) if isinstance(c, int) else spec(tm, tn) for c, spec in epi.row_outs]
    out_specs += [pl.BlockSpec((1, tn), lambda j, i, k: (0, j)) if epi.tiled else
                  pl.BlockSpec((1, c), lambda j, i, k: (0, 0)) for c in epi.acc_outs]
    out_shape = list(comm.outs if comm else [])
    out_shape += [jax.ShapeDtypeStruct((M, c), dt_) if isinstance(c, int) else c for c, dt_ in epi.row_outs]
    out_shape += [jax.ShapeDtypeStruct((1, c), F32) for c in epi.acc_outs]
    scratch = ([pltpu.VMEM((tm, tn), F32)] if nk > 1 else []) + epi.scratch
    if comm:
        scratch += [pltpu.SemaphoreType.DMA((comm.n_sems,))] * 2
    params = pltpu.CompilerParams(dimension_semantics=("arbitrary",) * 3, vmem_limit_bytes=VMEM_LIMIT,
                                  has_side_effects=comm is not None)
    outs = pl.pallas_call(
        body, name=name, grid=(nj, ni, nk), in_specs=in_specs, out_specs=out_specs, out_shape=out_shape,
        scratch_shapes=scratch, compiler_params=params,
    )(*([a] * n_a), b, *(comm.ins if comm else []), *[arr for arr, _, _ in epi.row_ins], *epi.full_ins)
    return outs[n_co:], (outs[:n_co] if comm else None)


def _mm_tn(a, b, name):
    S, Ka = a.shape
    _, N = b.shape
    tka = _pick(Ka, (1024, 768, 512, 256, 128))
    tn = _pick(N, (1024, 768, 512, 256, 128))
    ts = 1024 if S % 1024 == 0 else 512
    ns = S // ts

    def body(a_ref, b_ref, o_ref, acc_ref):
        s = pl.program_id(2)
        part = _dot_tn(_b(a_ref[...]), _b(b_ref[...]))

        @pl.when(s == 0)
        def _():
            acc_ref[...] = part

        @pl.when(s > 0)
        def _():
            acc_ref[...] += part

        @pl.when(s == ns - 1)
        def _():
            o_ref[...] = acc_ref[...]

    return pl.pallas_call(
        body, name=name, grid=(Ka // tka, N // tn, ns),
        in_specs=[pl.BlockSpec((ts, tka), lambda i, j, s: (s, i)),
                  pl.BlockSpec((ts, tn), lambda i, j, s: (s, j))],
        out_specs=pl.BlockSpec((tka, tn), lambda i, j, s: (i, j)),
        out_shape=jax.ShapeDtypeStruct((Ka, N), F32),
        scratch_shapes=[pltpu.VMEM((tka, tn), F32)],
        compiler_params=_params(("parallel", "parallel", "arbitrary")),
    )(a, b)


def _row_call(body, row_ins, full_ins, row_outs, acc_outs, bs, name):
    S = row_ins[0].shape[0]
    assert S % bs == 0
    in_specs = [pl.BlockSpec((bs, a.shape[1]), lambda i: (i, 0)) for a in row_ins]
    in_specs += [pl.BlockSpec(a.shape, lambda i: (0, 0)) for a in full_ins]
    out_specs = [pl.BlockSpec((bs, c), lambda i: (i, 0)) for c, _ in row_outs]
    out_specs += [pl.BlockSpec(s, lambda i: (0, 0)) for s in acc_outs]
    out_shape = [jax.ShapeDtypeStruct((S, c), dt) for c, dt in row_outs]
    out_shape += [jax.ShapeDtypeStruct(s, F32) for s in acc_outs]
    return pl.pallas_call(
        body, name=name, grid=(S // bs,), in_specs=in_specs, out_specs=out_specs, out_shape=out_shape,
        compiler_params=_params(("arbitrary",)),
    )(*row_ins, *full_ins)


def _rms_vals(x, w):
    r = lax.rsqrt(jnp.mean(x * x, axis=-1, keepdims=True) + RMS_EPS)
    return x * r * w


def _rms_bwd_vals(x, w, dy):
    r = lax.rsqrt(jnp.mean(x * x, axis=-1, keepdims=True) + RMS_EPS)
    xn = x * r
    g = dy * w
    dx = r * (g - xn * jnp.mean(g * xn, axis=-1, keepdims=True))
    dw = jnp.sum(dy * xn, axis=0, keepdims=True)
    return dx, dw


def _rms_fwd(x, w):
    def body(x_ref, w_ref, o_ref):
        o_ref[...] = _b(_rms_vals(x_ref[...], w_ref[...]))
    return _row_call(body, [x], [w], [(x.shape[1], BF16)], [], 512, "rms_fwd")[0]


def _group_rms(t):
    gw = SSM_INNER // SSM_GROUPS
    out = []
    for g in range(SSM_GROUPS):
        tg = t[:, g * gw:(g + 1) * gw]
        out.append(lax.rsqrt(jnp.mean(tg * tg, axis=-1, keepdims=True) + RMS_EPS))
    return out


def _ssm_out_epi(y, z, w):
    gw = SSM_INNER // SSM_GROUPS

    def a_fn(ri, fi, ro):
        zz = ri[1][...]
        t = ri[0][...] * (zz * _sigmoid(zz))
        rs = _group_rms(t)
        for g in range(SSM_GROUPS):
            sl = slice(g * gw, (g + 1) * gw)
            ro[0][:, sl] = _b(t[:, sl] * rs[g] * fi[0][:, sl])
        return ro[0][...]

    def fn(r, ri, fi, ro, ao, first):
        ro[1][...] = r
    return _Epi(fn, [(y, SSM_INNER, 0), (z, SSM_INNER, 0)], [w], [(SSM_INNER, BF16), (D_MODEL, F32)], a_fn=a_fn)


def _mix_out_epi(att_o, ssm_o, gl, x, b_gate, w_post, w_pre):
    def a_fn(ri, fi, ro):
        g = _sigmoid(ri[2][...] + fi[0][...])
        mi = _b(g[:, :D_MODEL] * ri[0][...] + g[:, D_MODEL:] * ri[1][...])
        ro[0][...] = mi
        return mi

    def fn(r, ri, fi, ro, ao, first):
        ro[1][...] = r
        h = ri[3][...] + _rms_vals(r, fi[1][...])
        ro[2][...] = h
        ro[3][...] = _b(_rms_vals(h, fi[2][...]))
    return _Epi(fn, [(att_o, D_MODEL, 0), (ssm_o, D_MODEL, 0), (gl, 2 * D_MODEL, 0), (x, D_MODEL, 0)],
                [b_gate, w_post, w_pre],
                [(D_MODEL, BF16), (D_MODEL, F32), (D_MODEL, F32), (D_MODEL, BF16)], a_fn=a_fn)


def _final_epi(h1, target, w_post):
    def fn(dn, ri, fi, ro, ao, first):
        w = fi[0][...]
        err = ri[0][...] + _rms_vals(dn, w) - ri[1][...]
        row = jnp.mean(err * err, axis=-1, keepdims=True)
        part = 0.5 * jnp.sum(row, axis=0, keepdims=True)
        dh = err * (1.0 / D_MODEL)
        ro[0][...] = dh
        dx, dw = _rms_bwd_vals(dn, w, dh)
        ro[1][...] = _b(dx)
        _acc_into(ao[0], jnp.broadcast_to(part, (1, LANES)), first)
        _acc_into(ao[1], dw, first)
    return _Epi(fn, [(h1, D_MODEL, 0), (target, D_MODEL, 0)], [w_post], [(D_MODEL, F32), (D_MODEL, BF16)],
                [LANES, D_MODEL])


def _mid_epi(dh2, h1, mixed, w_pre, w_post):
    def fn(df, ri, fi, ro, ao, first):
        dx, dwn = _rms_bwd_vals(ri[1][...], fi[0][...], df)
        dh1 = ri[0][...] + dx
        ro[0][...] = dh1
        dm, dwp = _rms_bwd_vals(ri[2][...], fi[1][...], dh1)
        ro[1][...] = _b(dm)
        _acc_into(ao[0], dwn, first)
        _acc_into(ao[1], dwp, first)
    return _Epi(fn, [(dh2, D_MODEL, 0), (h1, D_MODEL, 0), (mixed, D_MODEL, 0)], [w_pre, w_post],
                [(D_MODEL, F32), (D_MODEL, BF16)], [D_MODEL, D_MODEL])


def _gate_epi(att_o, ssm_o, gl, b_gate):
    def fn(d, ri, fi, ro, ao, first):
        g = _sigmoid(ri[2][...] + fi[0][...])
        ga, gs = g[:, :D_MODEL], g[:, D_MODEL:]
        ro[0][...] = _b(ga * d)
        ro[1][...] = _b(gs * d)
        dga = d * ri[0][...] * ga * (1.0 - ga)
        dgs = d * ri[1][...] * gs * (1.0 - gs)
        ro[2][:, :D_MODEL] = _b(dga)
        ro[2][:, D_MODEL:] = _b(dgs)
        _acc_into(ao[0].at[:, pl.ds(0, D_MODEL)], jnp.sum(dga, axis=0, keepdims=True), first)
        _acc_into(ao[0].at[:, pl.ds(D_MODEL, D_MODEL)], jnp.sum(dgs, axis=0, keepdims=True), first)
    return _Epi(fn, [(att_o, D_MODEL, 0), (ssm_o, D_MODEL, 0), (gl, 2 * D_MODEL, 0)], [b_gate],
                [(D_MODEL, BF16), (D_MODEL, BF16), (2 * D_MODEL, BF16)], [2 * D_MODEL])


def _first_epi(du, dh1, x, w_pre):
    def fn(r, ri, fi, ro, ao, first):
        dx, dw = _rms_bwd_vals(ri[2][...], fi[0][...], ri[0][...] + r)
        ro[0][...] = ri[1][...] + dx
        _acc_into(ao[0], dw, first)
    return _Epi(fn, [(du, D_MODEL, 0), (dh1, D_MODEL, 0), (x, D_MODEL, 0)], [w_pre], [(D_MODEL, F32)],
                [D_MODEL])


def _gnorm_epi(y, z, w):
    gw = SSM_INNER // SSM_GROUPS

    def fn(d_all, ri, fi, ro, ao, first):
        zz = ri[1][...]
        yy = ri[0][...]
        sg = _sigmoid(zz)
        sz = zz * sg
        t = yy * sz
        dws = []
        for g in range(d_all.shape[1] // gw):
            sl = slice(g * gw, (g + 1) * gw)
            tg = t[:, sl]
            r = lax.rsqrt(jnp.mean(tg * tg, axis=-1, keepdims=True) + RMS_EPS)
            tn = tg * r
            d = d_all[:, sl]
            gg = d * fi[0][:, sl]
            dt = r * (gg - tn * jnp.mean(gg * tn, axis=-1, keepdims=True))
            ro[0][:, sl] = dt * sz[:, sl]
            ro[1][:, sl] = _b(dt * yy[:, sl] * (sg[:, sl] * (1.0 + zz[:, sl] * (1.0 - sg[:, sl]))))
            dws.append(jnp.sum(d * tn, axis=0, keepdims=True))
        _acc_into(ao[0], jnp.concatenate(dws, axis=1), first)
    return _Epi(fn, [(y, SSM_INNER, 0), (z, SSM_INNER, 0)], [w], [(SSM_INNER, F32), (SSM_INNER, BF16)],
                [SSM_INNER], tiled=True)


def _head_col(stat, h):
    return stat[:, h:h + 1]


def _head_pair_masks(x):
    lane = lax.broadcasted_iota(jnp.int32, x.shape, 1)
    zero = jnp.zeros_like(x)
    return jnp.where(lane < HEAD_DIM, x, zero), jnp.where(lane >= HEAD_DIM, x, zero)


def _attn_fwd(qkv, d, comm=None):
    S = qkv.shape[0]
    blk = ATT_BLOCK
    nblk = S // blk
    nbs = nblk // d
    slopes = _alibi_slopes(N_ATT_HEADS)
    scale = HEAD_DIM ** -0.5
    n_ci = len(comm.ins) if comm else 0
    n_co = len(comm.outs) if comm else 0

    def body(*refs):
        q_ref, kc_ref, kp_ref, vc_ref, vp_ref = refs[:5]
        o_ref, m_ref, l_ref = refs[5 + n_ci:8 + n_ci]
        n = pl.program_id(0)
        if comm:
            c_args = (refs[5:5 + n_ci], refs[8 + n_ci:8 + n_ci + n_co], refs[-2], refs[-1])

            @pl.when(n == 0)
            def _():
                comm.start(*c_args)

        has_prev = (n % nbs) != 0
        ii = lax.broadcasted_iota(jnp.int32, (blk, 2 * blk), 0)
        jj = lax.broadcasted_iota(jnp.int32, (blk, 2 * blk), 1)
        dist_i = blk + ii - jj
        dist = dist_i.astype(F32)
        ok = jnp.logical_and(jnp.logical_and(dist_i >= 0, dist_i <= blk), jnp.logical_or(jj >= blk, has_prev))
        s_scr, p_scr = refs[8 + n_ci + n_co], refs[9 + n_ci + n_co]
        lane = lax.broadcasted_iota(jnp.int32, (blk, LANES), 1)
        for pr in range(N_ATT_HEADS // 2):
            sl = slice(pr * LANES, (pr + 1) * LANES)
            kcat = jnp.concatenate([kp_ref[:, sl], kc_ref[:, sl]], axis=0)
            for h, qh in zip((2 * pr, 2 * pr + 1), _head_pair_masks(q_ref[:, sl])):
                s_scr[h] = _dot_nt(qh, kcat)
        m_all = jnp.zeros((blk, LANES), F32)
        l_all = jnp.zeros((blk, LANES), F32)
        for h in range(N_ATT_HEADS):
            s = jnp.where(ok, s_scr[h] * scale - (slopes[h] * float(d)) * dist, NEG_BIG)
            m = jnp.max(s, axis=-1, keepdims=True)
            p = jnp.exp(s - m)
            l = jnp.sum(p, axis=-1, keepdims=True)
            m_all = jnp.where(lane == h, m, m_all)
            l_all = jnp.where(lane == h, l, l_all)
            p_scr[:, h * 2 * blk:(h + 1) * 2 * blk] = _b(p)
        for pr in range(N_ATT_HEADS // 2):
            sl = slice(pr * LANES, (pr + 1) * LANES)
            vmask = jnp.concatenate(
                _head_pair_masks(jnp.concatenate([vp_ref[:, sl], vc_ref[:, sl]], axis=0)), axis=0)
            o_ref[:, sl] = _dot(p_scr[:, pr * 4 * blk:(pr + 1) * 4 * blk], vmask)
        m_ref[...] = m_all
        l_ref[...] = l_all
        if comm:
            @pl.when(n == nblk - 1)
            def _():
                comm.finish(*c_args)

    cur = lambda c: pl.BlockSpec((blk, ATT_WIDTH), lambda n: (n, c))
    prev = lambda c: pl.BlockSpec((blk, ATT_WIDTH), lambda n: (jnp.maximum(n - 1, 0), c))
    stat = pl.BlockSpec((blk, LANES), lambda n: (n, 0))
    scratch = [pltpu.VMEM((N_ATT_HEADS, blk, 2 * blk), F32), pltpu.VMEM((blk, N_ATT_HEADS * 2 * blk), BF16)]
    if comm:
        scratch += [pltpu.SemaphoreType.DMA((comm.n_sems,))] * 2
        params = pltpu.CompilerParams(dimension_semantics=("arbitrary",), vmem_limit_bytes=VMEM_LIMIT,
                                      has_side_effects=True)
    else:
        params = _params(("parallel",))
    outs = pl.pallas_call(
        body, name=f"attn_fwd_d{d}", grid=(nblk,),
        in_specs=[cur(0), cur(1), prev(1), cur(2), prev(2)] + [ANY] * n_ci,
        out_specs=[cur(0), stat, stat] + [ANY] * n_co,
        out_shape=[jax.ShapeDtypeStruct((S, ATT_WIDTH), F32), jax.ShapeDtypeStruct((S, LANES), F32),
                   jax.ShapeDtypeStruct((S, LANES), F32)] + list(comm.outs if comm else []),
        scratch_shapes=scratch,
        compiler_params=params,
    )(qkv, qkv, qkv, qkv, qkv, *(comm.ins if comm else []))
    return (outs[0], outs[1], outs[2], outs[3:]) if comm else outs


def _attn_bwd(qkv, do, lse, delta, d, comm=None):
    S = qkv.shape[0]
    blk = ATT_BLOCK
    nblk = S // blk
    nbs = nblk // d
    slopes = _alibi_slopes(N_ATT_HEADS)
    scale = HEAD_DIM ** -0.5
    n_ci = len(comm.ins) if comm else 0
    n_co = len(comm.outs) if comm else 0

    def body(*refs):
        qc_ref, qn_ref, k_ref, v_ref, doc_ref, don_ref, lc_ref, ln_ref, dc_ref, dn_ref = refs[:10]
        dq_ref, dk_ref, dv_ref = refs[10 + n_ci:13 + n_ci]
        carry_ref = refs[13 + n_ci + n_co]
        n = pl.program_id(0)
        has_next = ((n + 1) % nbs) != 0
        if comm:
            c_args = (refs[10:10 + n_ci], refs[13 + n_ci:13 + n_ci + n_co], refs[-2], refs[-1])

        @pl.when(n == 0)
        def _():
            carry_ref[...] = jnp.zeros_like(carry_ref)
            if comm:
                comm.start(*c_args)

        rr = lax.broadcasted_iota(jnp.int32, (2 * blk, blk), 0)
        jj = lax.broadcasted_iota(jnp.int32, (2 * blk, blk), 1)
        dist_i = rr - jj
        dist = dist_i.astype(F32)
        ok = jnp.logical_or(jnp.logical_and(rr < blk, dist_i >= 0),
                            jnp.logical_and(jnp.logical_and(rr >= blk, dist_i <= blk), has_next))
        s_scr, dp_scr, p_rows, ds_rows, ds_cols = refs[14 + n_ci + n_co:19 + n_ci + n_co]
        lcat = jnp.concatenate([lc_ref[...], ln_ref[...]], axis=0)
        dcat = jnp.concatenate([dc_ref[...], dn_ref[...]], axis=0)
        rows2 = 2 * blk

        def operands(pr):
            sl = slice(pr * LANES, (pr + 1) * LANES)
            qm = _head_pair_masks(jnp.concatenate([qc_ref[:, sl], qn_ref[:, sl]], axis=0))
            dom = _head_pair_masks(jnp.concatenate([doc_ref[:, sl], don_ref[:, sl]], axis=0))
            return sl, qm, dom

        for pr in range(N_ATT_HEADS // 2):
            sl, qm, dom = operands(pr)
            for h, qh, doh in zip((2 * pr, 2 * pr + 1), qm, dom):
                s_scr[h] = _dot_nt(qh, k_ref[:, sl])
                dp_scr[h] = _dot_nt(doh, v_ref[:, sl])
        for h in range(N_ATT_HEADS):
            s = jnp.where(ok, s_scr[h] * scale - (slopes[h] * float(d)) * dist - lcat[:, h:h + 1], NEG_BIG)
            p = jnp.exp(s)
            dsb = _b(p * (dp_scr[h] - dcat[:, h:h + 1]) * scale)
            p_rows[h * rows2:(h + 1) * rows2, :] = _b(p)
            ds_rows[h * rows2:(h + 1) * rows2, :] = dsb
            ds_cols[:, h * blk:(h + 1) * blk] = dsb
        for pr in range(N_ATT_HEADS // 2):
            sl, qm, dom = operands(pr)
            pair_rows = slice(pr * 2 * rows2, (pr + 1) * 2 * rows2)
            dv_ref[:, sl] = _b(_dot_tn(p_rows[pair_rows, :], jnp.concatenate(dom, axis=0)))
            dk_ref[:, sl] = _b(_dot_tn(ds_rows[pair_rows, :], jnp.concatenate(qm, axis=0)))
            dq = _dot(ds_cols[:, pr * 2 * blk:(pr + 1) * 2 * blk],
                      jnp.concatenate(_head_pair_masks(k_ref[:, sl]), axis=0))
            dq_ref[:, sl] = _b(dq[:blk] + carry_ref[:, sl])
            carry_ref[:, sl] = dq[blk:]

        if comm:
            @pl.when(n == nblk - 1)
            def _():
                comm.finish(*c_args)

    cur = lambda c: pl.BlockSpec((blk, ATT_WIDTH), lambda n: (n, c))
    nxt = lambda c: pl.BlockSpec((blk, ATT_WIDTH), lambda n: (jnp.minimum(n + 1, nblk - 1), c))
    scur = pl.BlockSpec((blk, LANES), lambda n: (n, 0))
    snxt = pl.BlockSpec((blk, LANES), lambda n: (jnp.minimum(n + 1, nblk - 1), 0))
    shp = jax.ShapeDtypeStruct((S, ATT_WIDTH), BF16)
    scratch = [pltpu.VMEM((blk, ATT_WIDTH), F32),
               pltpu.VMEM((N_ATT_HEADS, 2 * blk, blk), F32), pltpu.VMEM((N_ATT_HEADS, 2 * blk, blk), F32),
               pltpu.VMEM((N_ATT_HEADS * 2 * blk, blk), BF16), pltpu.VMEM((N_ATT_HEADS * 2 * blk, blk), BF16),
               pltpu.VMEM((2 * blk, N_ATT_HEADS * blk), BF16)]
    if comm:
        scratch += [pltpu.SemaphoreType.DMA((comm.n_sems,))] * 2
        params = pltpu.CompilerParams(dimension_semantics=("arbitrary",), vmem_limit_bytes=VMEM_LIMIT,
                                      has_side_effects=True)
    else:
        params = _params(("arbitrary",))
    outs = pl.pallas_call(
        body, name=f"attn_bwd_d{d}", grid=(nblk,),
        in_specs=[cur(0), nxt(0), cur(1), cur(2), cur(0), nxt(0), scur, snxt, scur, snxt] + [ANY] * n_ci,
        out_specs=[cur(0), cur(0), cur(0)] + [ANY] * n_co,
        out_shape=[shp, shp, shp] + list(comm.outs if comm else []),
        scratch_shapes=scratch,
        compiler_params=params,
    )(qkv, qkv, qkv, qkv, do, do, lse, lse, delta, delta, *(comm.ins if comm else []))
    return (outs[0], outs[1], outs[2], outs[3:]) if comm else outs


LAYOUT_TILE = 512
DILATED = tuple(d for d in DILATIONS if d > 1)


def _pat_spec(d, cols, col_block=0):
    return pl.BlockSpec((d, LAYOUT_TILE // d, cols), lambda i: (0, i, col_block))


def _pat_view(a, d):
    return a.reshape(d, a.shape[0] // d, a.shape[1])


def _to_slabs(slab_ref, src_ref):
    for cb in range(slab_ref.shape[0]):
        slab_ref[cb] = src_ref[:, cb * LANES:(cb + 1) * LANES].astype(F32)


def _gather_pattern(dst_ref, slab_ref, d, dtype):
    t = slab_ref.shape[1]
    for cb in range(slab_ref.shape[0]):
        one = slab_ref.at[cb]
        for r in range(d):
            dst_ref[r, :, cb * LANES:(cb + 1) * LANES] = one[pl.ds(r, t // d, stride=d), :].astype(dtype)


def _scatter_pattern(slab_ref, src_ref, d, add=False):
    t = slab_ref.shape[1]
    for cb in range(slab_ref.shape[0]):
        one = slab_ref.at[cb]
        for r in range(d):
            idx = pl.ds(r, t // d, stride=d)
            val = src_ref[r, :, cb * LANES:(cb + 1) * LANES]
            if add:
                val = val + one[idx, :]
            one[idx, :] = val


def _pat_out(d, S, cols, dtype, col_tiled):
    def spec(tm, tn):
        if col_tiled:
            return pl.BlockSpec((d, tm // d, tn), lambda j, i, k: (0, i, j))
        return pl.BlockSpec((d, tm // d, cols), lambda j, i, k: (0, i, 0))
    return jax.ShapeDtypeStruct((d, S // d, cols), dtype), spec


def _qkv_epi(S):
    def fn(r, ri, fi, ro, ao, first, scr):
        ro[0][...] = _b(r)
        slab = scr[0]
        for cb in range(slab.shape[0]):
            slab[cb] = r[:, cb * LANES:(cb + 1) * LANES]
        for d, p_ref in zip(DILATED, ro[1:]):
            _gather_pattern(p_ref, slab, d, BF16)
    return _Epi(fn, row_outs=[(3 * ATT_WIDTH, BF16)] + [_pat_out(d, S, 3 * ATT_WIDTH, BF16, True) for d in DILATED],
                tiled=True, scratch=[pltpu.VMEM((ATT_WIDTH // LANES, LAYOUT_TILE, LANES), F32)])


def _attn_combine(os, ms, ls):
    S = os[0].shape[0]
    t = LAYOUT_TILE

    def body(o1, o2, o3, m1, m2, m3, l1, l2, l3, att_ref, lse_ref, so2, so3, sm2, sm3, sl2, sl3):
        for d, src, dst in ((DILATED[0], o2, so2), (DILATED[1], o3, so3), (DILATED[0], m2, sm2),
                            (DILATED[1], m3, sm3), (DILATED[0], l2, sl2), (DILATED[1], l3, sl3)):
            _scatter_pattern(dst, src, d)
        mm = [m1[...], sm2[0], sm3[0]]
        big = jnp.maximum(jnp.maximum(mm[0], mm[1]), mm[2])
        es = [jnp.exp(m - big) for m in mm]
        den = es[0] * l1[...] + es[1] * sl2[0] + es[2] * sl3[0]
        lse_ref[...] = big + jnp.log(den)
        inv = 1.0 / den
        for h in range(N_ATT_HEADS):
            sl = slice(h * HEAD_DIM, (h + 1) * HEAD_DIM)
            cb, hl = divmod(h, 2)
            sll = slice(hl * HEAD_DIM, (hl + 1) * HEAD_DIM)
            num = (_head_col(es[0], h) * o1[:, sl] + _head_col(es[1], h) * so2[cb, :, sll]
                   + _head_col(es[2], h) * so3[cb, :, sll])
            att_ref[:, sl] = num * _head_col(inv, h)

    def specs(c):
        return [pl.BlockSpec((t, c), lambda i: (i, 0))] + [_pat_spec(d, c) for d in DILATED]

    args = [os[0]] + [_pat_view(o, d) for o, d in zip(os[1:], DILATED)]
    args += [ms[0]] + [_pat_view(m, d) for m, d in zip(ms[1:], DILATED)]
    args += [ls[0]] + [_pat_view(l, d) for l, d in zip(ls[1:], DILATED)]
    return pl.pallas_call(
        body, name="attn_combine", grid=(S // t,),
        in_specs=specs(ATT_WIDTH) + specs(LANES) + specs(LANES),
        out_specs=[pl.BlockSpec((t, ATT_WIDTH), lambda i: (i, 0)), pl.BlockSpec((t, LANES), lambda i: (i, 0))],
        out_shape=[jax.ShapeDtypeStruct((S, ATT_WIDTH), F32), jax.ShapeDtypeStruct((S, LANES), F32)],
        scratch_shapes=[pltpu.VMEM((ATT_WIDTH // LANES, t, LANES), F32)] * 2
        + [pltpu.VMEM((1, t, LANES), F32)] * 4,
        compiler_params=_params(("parallel",)),
    )(*args)


def _attn_delta(d_att, att, lse):
    S = d_att.shape[0]
    t = LAYOUT_TILE

    def body(d_ref, a_ref, l_ref, *refs):
        out_refs, d_slab, l_slab, dl_slab = refs[:-3], refs[-3], refs[-2], refs[-1]
        dd = d_ref[...]
        prod = dd * a_ref[...]
        lane = lax.broadcasted_iota(jnp.int32, (t, LANES), 1)
        acc = jnp.zeros((t, LANES), F32)
        for h in range(N_ATT_HEADS):
            s = jnp.sum(prod[:, h * HEAD_DIM:(h + 1) * HEAD_DIM], axis=-1, keepdims=True)
            acc = jnp.where(lane == h, s, acc)
        out_refs[0][...] = _b(dd)
        out_refs[1][...] = acc
        _to_slabs(d_slab, d_ref)
        l_slab[0] = l_ref[...]
        dl_slab[0] = acc
        for k, d in enumerate(DILATED):
            db_ref, ls_ref, dl_ref = out_refs[2 + 3 * k:5 + 3 * k]
            _gather_pattern(db_ref, d_slab, d, BF16)
            _gather_pattern(ls_ref, l_slab, d, F32)
            _gather_pattern(dl_ref, dl_slab, d, F32)

    nat = lambda c: pl.BlockSpec((t, c), lambda i: (i, 0))
    out_specs = [nat(ATT_WIDTH), nat(LANES)]
    out_shape = [jax.ShapeDtypeStruct((S, ATT_WIDTH), BF16), jax.ShapeDtypeStruct((S, LANES), F32)]
    for d in DILATED:
        out_specs += [_pat_spec(d, ATT_WIDTH), _pat_spec(d, LANES), _pat_spec(d, LANES)]
        out_shape += [jax.ShapeDtypeStruct((d, S // d, ATT_WIDTH), BF16),
                      jax.ShapeDtypeStruct((d, S // d, LANES), F32),
                      jax.ShapeDtypeStruct((d, S // d, LANES), F32)]
    outs = pl.pallas_call(
        body, name="attn_delta", grid=(S // t,),
        in_specs=[nat(ATT_WIDTH), nat(ATT_WIDTH), nat(LANES)],
        out_specs=out_specs, out_shape=out_shape,
        scratch_shapes=[pltpu.VMEM((ATT_WIDTH // LANES, t, LANES), F32), pltpu.VMEM((1, t, LANES), F32),
                        pltpu.VMEM((1, t, LANES), F32)],
        compiler_params=_params(("parallel",)),
    )(d_att, att, lse)
    res = [(outs[0], lse, outs[1])]
    for k in range(len(DILATED)):
        db, ls, dl = outs[2 + 3 * k:5 + 3 * k]
        res.append((db.reshape(S, ATT_WIDTH), ls.reshape(S, LANES), dl.reshape(S, LANES)))
    return res


def _sum_qkv(dqs, dks, dvs):
    S = dqs[0].shape[0]
    t = LAYOUT_TILE

    def body(*refs):
        o_ref, scr = refs[-2], refs[-1]
        for part in range(3):
            nat_ref, p_refs = refs[3 * part], refs[3 * part + 1:3 * part + 3]
            _to_slabs(scr, nat_ref)
            for d, p_ref in zip(DILATED, p_refs):
                _scatter_pattern(scr, p_ref, d, add=True)
            for cb in range(ATT_WIDTH // LANES):
                o_ref[:, part * ATT_WIDTH + cb * LANES:part * ATT_WIDTH + (cb + 1) * LANES] = _b(scr[cb])

    in_specs, args = [], []
    for group in (dqs, dks, dvs):
        in_specs += [pl.BlockSpec((t, ATT_WIDTH), lambda i: (i, 0))] + [_pat_spec(d, ATT_WIDTH) for d in DILATED]
        args += [group[0]] + [_pat_view(a, d) for a, d in zip(group[1:], DILATED)]
    return pl.pallas_call(
        body, name="sum_dqkv", grid=(S // t,),
        in_specs=in_specs,
        out_specs=pl.BlockSpec((t, 3 * ATT_WIDTH), lambda i: (i, 0)),
        out_shape=jax.ShapeDtypeStruct((S, 3 * ATT_WIDTH), BF16),
        scratch_shapes=[pltpu.VMEM((ATT_WIDTH // LANES, t, LANES), F32)],
        compiler_params=_params(("parallel",)),
    )(*args)


CONV_COLS = 1024
CONV_ROWS = 512
HALO = 8


def _shift_down(x, k, top_src):
    r8 = lax.broadcasted_iota(jnp.int32, (HALO, x.shape[1]), 0)
    rolled = pltpu.roll(x, k, 0)
    top = jnp.where(r8 < k, pltpu.roll(top_src, k, 0), rolled[0:HALO])
    if x.shape[0] == HALO:
        return top
    return jnp.concatenate([top, rolled[HALO:]], axis=0)


def _shift_up(x, k, bottom_src):
    n = x.shape[0]
    r8 = lax.broadcasted_iota(jnp.int32, (HALO, x.shape[1]), 0)
    rolled = pltpu.roll(x, n - k, 0)
    bottom = jnp.where(r8 >= HALO - k, pltpu.roll(bottom_src, HALO - k, 0), rolled[n - HALO:n])
    return jnp.concatenate([rolled[:n - HALO], bottom], axis=0)


def _conv_pre(x, top_src, w_ref, b_ref):
    shifted = [x] + [_shift_down(x, k, top_src) for k in range(1, SSM_CONV)]
    pre = b_ref[...] + w_ref[SSM_CONV - 1:SSM_CONV, :] * x
    for k in range(1, SSM_CONV):
        pre = pre + w_ref[SSM_CONV - 1 - k:SSM_CONV - k, :] * shifted[k]
    return pre, shifted


def _conv_fwd(xbc, conv_w, conv_b):
    S, C = xbc.shape
    bs, bc = CONV_ROWS, CONV_COLS
    nr = S // bs

    def body(x_ref, halo_ref, w_ref, b_ref, o_ref, pre_ref):
        r = pl.program_id(1)
        halo = jnp.where(r > 0, halo_ref[...], 0.0)
        pre, _ = _conv_pre(x_ref[...], halo, w_ref, b_ref)
        pre_ref[...] = _b(pre)
        o_ref[...] = pre * _sigmoid(pre)

    tile = pl.BlockSpec((bs, bc), lambda c, r: (r, c))
    return pl.pallas_call(
        body, name="conv_fwd", grid=(C // bc, nr),
        in_specs=[tile,
                  pl.BlockSpec((HALO, bc), lambda c, r: (jnp.maximum(r * (bs // HALO) - 1, 0), c)),
                  pl.BlockSpec((SSM_CONV, bc), lambda c, r: (0, c)),
                  pl.BlockSpec((1, bc), lambda c, r: (0, c))],
        out_specs=[tile, tile],
        out_shape=[jax.ShapeDtypeStruct((S, C), F32), jax.ShapeDtypeStruct((S, C), BF16)],
        compiler_params=_params(("parallel", "arbitrary")),
    )(xbc, xbc, conv_w, conv_b)


def _conv_bwd(xbc, pre_all, dact, conv_w):
    S, C = xbc.shape
    bs, bc = CONV_ROWS, CONV_COLS
    nr = S // bs
    hb = bs // HALO
    last_halo = S // HALO - 1

    def dsilu(pre):
        sg = _sigmoid(pre)
        return sg * (1.0 + pre * (1.0 - sg))

    def body(x_ref, p_ref, pn_ref, d_ref, dn_ref, w_ref, dx_ref, dw_ref, db_ref):
        r = pl.program_id(1)
        x = x_ref[...]
        dpre = d_ref[...] * dsilu(p_ref[...].astype(F32))
        dpre_n = jnp.where(r < nr - 1, dn_ref[...], 0.0) * dsilu(pn_ref[...].astype(F32)[0:HALO])
        ups = [dpre] + [_shift_up(dpre, k, dpre_n) for k in range(1, SSM_CONV)]
        dx = w_ref[SSM_CONV - 1:SSM_CONV, :] * dpre
        for k in range(1, SSM_CONV):
            dx = dx + w_ref[SSM_CONV - 1 - k:SSM_CONV - k, :] * ups[k]
        dx_ref[...] = _b(dx)
        parts = [jnp.sum(x * ups[SSM_CONV - 1 - j], axis=0, keepdims=True) for j in range(SSM_CONV)]
        dbp = jnp.sum(dpre, axis=0, keepdims=True)

        @pl.when(r == 0)
        def _():
            for j in range(SSM_CONV):
                dw_ref[j:j + 1, :] = parts[j]
            db_ref[...] = dbp

        @pl.when(r > 0)
        def _():
            for j in range(SSM_CONV):
                dw_ref[j:j + 1, :] += parts[j]
            db_ref[...] += dbp

    tile = pl.BlockSpec((bs, bc), lambda c, r: (r, c))
    nxt = pl.BlockSpec((HALO, bc), lambda c, r: (jnp.minimum((r + 1) * hb, last_halo), c))
    nxt16 = pl.BlockSpec((BF16_ROWS, bc), lambda c, r: (
        jnp.minimum((r + 1) * (bs // BF16_ROWS), S // BF16_ROWS - 1), c))
    return pl.pallas_call(
        body, name="conv_bwd", grid=(C // bc, nr),
        in_specs=[tile, tile, nxt16, tile, nxt, pl.BlockSpec((SSM_CONV, bc), lambda c, r: (0, c))],
        out_specs=[tile,
                   pl.BlockSpec((SSM_CONV, bc), lambda c, r: (0, c)),
                   pl.BlockSpec((1, bc), lambda c, r: (0, c))],
        out_shape=[jax.ShapeDtypeStruct((S, C), BF16), jax.ShapeDtypeStruct((SSM_CONV, C), F32),
                   jax.ShapeDtypeStruct((1, C), F32)],
        compiler_params=_params(("parallel", "arbitrary")),
    )(xbc, pre_all, pre_all, dact, dact, conv_w)


def _softplus(x):
    return jnp.maximum(x, 0.0) + jnp.log(1.0 + jnp.exp(-jnp.abs(x)))


GROUP_W = HEADS_PER_GROUP * SSM_HEAD_DIM
B_COL0 = SSM_INNER
C_COL0 = SSM_INNER + SSM_GROUPS * SSM_STATE


def _ssd_prep(dt_raw, dt_bias, a_neg):
    S = dt_raw.shape[0]
    ch = SSM_CHUNK
    nch = S // ch

    def body(dtr_ref, bias_ref, a_ref, dt_ref, acs_ref, acst_ref, sig_ref):
        x = dtr_ref[...] + bias_ref[...]
        lane = lax.broadcasted_iota(jnp.int32, (ch, LANES), 1)
        dt = jnp.where(lane < SSM_HEADS, _softplus(x), 0.0)
        ii = lax.broadcasted_iota(jnp.int32, (ch, ch), 0)
        jj = lax.broadcasted_iota(jnp.int32, (ch, ch), 1)
        acs = _dot_hi(jnp.where(ii >= jj, 1.0, 0.0), dt * a_ref[...])
        dt_ref[...] = dt
        acs_ref[...] = acs
        acst_ref[0] = acs.T[0:SSM_HEADS, :]
        sig_ref[...] = _sigmoid(x)

    blk = pl.BlockSpec((ch, LANES), lambda c: (c, 0))
    small = pl.BlockSpec((1, LANES), lambda c: (0, 0))
    shp = jax.ShapeDtypeStruct((S, LANES), F32)
    return pl.pallas_call(
        body, name="ssd_prep", grid=(nch,),
        in_specs=[blk, small, small],
        out_specs=[blk, blk, pl.BlockSpec((1, SSM_HEADS, ch), lambda c: (c, 0, 0)), blk],
        out_shape=[shp, shp, jax.ShapeDtypeStruct((nch, SSM_HEADS, ch), F32), shp],
        compiler_params=_params(("parallel",)),
    )(dt_raw, dt_bias, a_neg)


def _expand_heads(arr, g, rows):
    lane = lax.broadcasted_iota(jnp.int32, (rows, GROUP_W), 1) // SSM_HEAD_DIM
    h0 = HEADS_PER_GROUP * g
    out = jnp.broadcast_to(arr[:, h0:h0 + 1], (rows, GROUP_W))
    for j in range(1, HEADS_PER_GROUP):
        out = jnp.where(lane == j, arr[:, h0 + j:h0 + j + 1], out)
    return out


def _seg_matrix(k, lanes_per_head, h0):
    r = lax.broadcasted_iota(jnp.int32, (k, LANES), 0)
    c = lax.broadcasted_iota(jnp.int32, (k, LANES), 1)
    return jnp.where(c == h0 + r // lanes_per_head, 1.0, 0.0).astype(BF16)


def _seg_dot(t, e):
    hi = _b(t)
    lo = _b(t - hi.astype(F32))
    return _dot(hi, e) + _dot(lo, e)


def _head_sums(t, e, rows):
    if rows >= 8:
        return _seg_dot(t, e)
    return _seg_dot(jnp.broadcast_to(t, (8, t.shape[1])), e)[0:rows]


def _pair_masks(x):
    lane = lax.broadcasted_iota(jnp.int32, x.shape, 1)
    zero = jnp.zeros_like(x)
    return jnp.where(lane < SSM_HEAD_DIM, x, zero), jnp.where(lane >= SSM_HEAD_DIM, x, zero)


def _ssd_fwd(xact, dt, acs, acst, dsk_e):
    S = xact.shape[0]
    ch = SSM_CHUNK
    nch = S // ch

    def body(x_ref, dt_ref, acs_ref, acst_ref, dsk_ref, y_ref, hs_ref, h_ref):
        c = pl.program_id(0)

        @pl.when(c == 0)
        def _():
            h_ref[...] = jnp.zeros_like(h_ref)

        dt_all = dt_ref[...]
        acs_all = acs_ref[...]
        acst_all = acst_ref[0]
        alast = acs_all[ch - 1:ch, :]
        eacs = jnp.exp(acs_all)
        wd_all = dt_all * jnp.exp(alast - acs_all)
        dtt = dt_all.T
        cd_all = jnp.exp(alast)
        ii = lax.broadcasted_iota(jnp.int32, (ch, ch), 0)
        jj = lax.broadcasted_iota(jnp.int32, (ch, ch), 1)
        low = ii >= jj
        for g in range(SSM_GROUPS):
            xs = x_ref[:, g * GROUP_W:(g + 1) * GROUP_W]
            bb = _b(x_ref[:, B_COL0 + g * SSM_STATE:B_COL0 + (g + 1) * SSM_STATE])
            cc = _b(x_ref[:, C_COL0 + g * SSM_STATE:C_COL0 + (g + 1) * SSM_STATE])
            cb = _dot_nt(cc, bb)
            xsb = _b(xs)
            ht = h_ref[g]
            rest = (_dot(cc, _b(ht)) * _expand_heads(eacs, g, ch)
                    + dsk_ref[:, g * GROUP_W:(g + 1) * GROUP_W] * xs)
            for p in range(HEADS_PER_GROUP // 2):
                lms = []
                for h in (HEADS_PER_GROUP * g + 2 * p, HEADS_PER_GROUP * g + 2 * p + 1):
                    diff = acs_all[:, h:h + 1] - acst_all[h:h + 1, :]
                    lms.append(_b(cb * jnp.exp(jnp.where(low, diff, -jnp.inf)) * dtt[h:h + 1, :]))
                xa, xb = _pair_masks(xsb[:, p * LANES:(p + 1) * LANES])
                yp = _dot(jnp.concatenate(lms, axis=1), jnp.concatenate([xa, xb], axis=0))
                y_ref[:, g * GROUP_W + p * LANES:g * GROUP_W + (p + 1) * LANES] = (
                    yp + rest[:, p * LANES:(p + 1) * LANES])
            hs_ref[0, g] = ht
            st = _dot_tn(bb, _b(xs * _expand_heads(wd_all, g, ch)))
            h_ref[g] = ht * _expand_heads(cd_all, g, 1) + st

    blk = pl.BlockSpec((ch, LANES), lambda c: (c, 0))
    return pl.pallas_call(
        body, name="ssd_fwd", grid=(nch,),
        in_specs=[pl.BlockSpec((ch, CONV_DIM), lambda c: (c, 0)), blk, blk,
                  pl.BlockSpec((1, SSM_HEADS, ch), lambda c: (c, 0, 0)),
                  pl.BlockSpec((1, SSM_INNER), lambda c: (0, 0))],
        out_specs=[pl.BlockSpec((ch, SSM_INNER), lambda c: (c, 0)),
                   pl.BlockSpec((1, SSM_GROUPS, SSM_STATE, GROUP_W), lambda c: (c, 0, 0, 0))],
        out_shape=[jax.ShapeDtypeStruct((S, SSM_INNER), F32),
                   jax.ShapeDtypeStruct((nch, SSM_GROUPS, SSM_STATE, GROUP_W), F32)],
        scratch_shapes=[pltpu.VMEM((SSM_GROUPS, SSM_STATE, GROUP_W), F32)],
        compiler_params=_params(("arbitrary",)),
    )(xact, dt, acs, acst, dsk_e)


def _ssd_bwd(xact, dt, acs, acst, sig, a_neg, dsk_e, hs, dy):
    S = xact.shape[0]
    ch = SSM_CHUNK
    nch = S // ch
    ng, hg = SSM_GROUPS, HEADS_PER_GROUP
    nbc = SSM_GROUPS * SSM_STATE

    def body(x_ref, dt_ref, acs_ref, acst_ref, sig_ref, a_ref, dsk_ref, hs_ref, dy_ref,
             dx_ref, ddt_ref, st_ref,
             dh_ref, rows_ref, e_dt, e_ea, e_dsd, xdb_s, xddb_s, dzb_s, bcb_s, cb_s, zz_s, ww_s, dc1_s, db1_s,
             dhin_s, dlm_s, lmb_s, gm_s, dcbb_s, t_s, dxd_s, prod_s, csum_s):
        step = pl.program_id(0)

        @pl.when(step == 0)
        def _():
            dh_ref[...] = jnp.zeros_like(dh_ref)
            st_ref[...] = jnp.zeros_like(st_ref)
            rows_ref[...] = jnp.zeros_like(rows_ref)

        dt_all = dt_ref[...]
        acs_all = acs_ref[...]
        alast = acs_all[ch - 1:ch, :]
        eacs = jnp.exp(acs_all)
        dsd_all = jnp.exp(alast - acs_all)
        cd_all = jnp.exp(alast)
        ii = lax.broadcasted_iota(jnp.int32, (ch, ch), 0)
        jj = lax.broadcasted_iota(jnp.int32, (ch, ch), 1)
        low = ii >= jj
        gsl = [slice(g * GROUP_W, (g + 1) * GROUP_W) for g in range(ng)]
        psl = [[slice(g * GROUP_W + p * LANES, g * GROUP_W + (p + 1) * LANES) for p in range(hg // 2)]
               for g in range(ng)]
        seg = [_seg_matrix(GROUP_W, SSM_HEAD_DIM, hg * g) for g in range(ng)]

        def bc(g):
            return (bcb_s[:, g * SSM_STATE:(g + 1) * SSM_STATE],
                    bcb_s[:, nbc + g * SSM_STATE:nbc + (g + 1) * SSM_STATE])

        def dy_pair(g, p):
            return _pair_masks(_b(dy_ref[:, psl[g][p]]))

        bcb_s[...] = _b(x_ref[:, B_COL0:])
        for g in range(ng):
            dt_e = _expand_heads(dt_all, g, ch)
            ea_e = _expand_heads(eacs, g, ch)
            dsd_e = _expand_heads(dsd_all, g, ch)
            e_dt[:, gsl[g]] = dt_e
            e_ea[:, gsl[g]] = ea_e
            e_dsd[:, gsl[g]] = dsd_e
            xd = x_ref[:, gsl[g]] * dt_e
            xdb_s[:, gsl[g]] = _b(xd)
            xddb_s[:, gsl[g]] = _b(xd * dsd_e)
            dzb_s[:, gsl[g]] = _b(dy_ref[:, gsl[g]] * ea_e)
        for g in range(ng):
            bb, cc = bc(g)
            htb = _b(hs_ref[0, g])
            dhnb = _b(dh_ref[g])
            cb_s[g] = _dot_nt(cc, bb)
            zz_s[:, gsl[g]] = _dot(cc, htb)
            ww_s[:, gsl[g]] = _dot(bb, dhnb)
            dc1_s[g] = _dot_nt(dzb_s[:, gsl[g]], htb)
            db1_s[g] = _dot_nt(xddb_s[:, gsl[g]], dhnb)
            dhin_s[g] = _dot_tn(cc, dzb_s[:, gsl[g]])
            for p in range(hg // 2):
                xp = xdb_s[:, psl[g][p]]
                for q, dyh in enumerate(dy_pair(g, p)):
                    dlm_s[hg * g + 2 * p + q] = _dot_nt(dyh, xp)
        for g in range(ng):
            cb = cb_s[g]
            dcb = jnp.zeros((ch, ch), F32)
            for j in range(hg):
                h = hg * g + j
                diff = acs_all[:, h:h + 1] - acst_ref[0, h:h + 1, :]
                decay = jnp.exp(jnp.where(low, diff, -jnp.inf))
                lm = cb * decay
                dlm = dlm_s[h]
                gm = dlm * lm
                dcb = dcb + dlm * decay
                rows_ref[h:h + 1, :] = jnp.sum(gm, axis=0, keepdims=True)
                lmb_s[h * ch:(h + 1) * ch, :] = _b(lm)
                gm_s[:, h * ch:(h + 1) * ch] = gm
            dcbb_s[g] = _b(dcb)
            xs = x_ref[:, gsl[g]]
            dyg = dy_ref[:, gsl[g]]
            ww = ww_s[:, gsl[g]]
            dsd_e = e_dsd[:, gsl[g]]
            t2 = ww * (xs * e_dt[:, gsl[g]] * dsd_e)
            t_s[:, gsl[g]] = dyg * zz_s[:, gsl[g]] * e_ea[:, gsl[g]] - t2
            dhn = dh_ref[g]
            csum_s[0:1, gsl[g]] = jnp.sum(t2, axis=0, keepdims=True)
            csum_s[1:2, gsl[g]] = jnp.sum(dhn * hs_ref[0, g], axis=0, keepdims=True)
            csum_s[2:3, gsl[g]] = jnp.sum(dyg * xs, axis=0, keepdims=True)
            dh_ref[g] = dhin_s[g] + dhn * _expand_heads(cd_all, g, 1)
            dxd_s[:, gsl[g]] = ww * dsd_e
        cols = jnp.zeros((ch, LANES), F32)
        for g in range(ng):
            bb, cc = bc(g)
            dcbb = dcbb_s[g]
            dx_ref[:, C_COL0 + g * SSM_STATE:C_COL0 + (g + 1) * SSM_STATE] = dc1_s[g] + _dot(dcbb, bb)
            dx_ref[:, B_COL0 + g * SSM_STATE:B_COL0 + (g + 1) * SSM_STATE] = db1_s[g] + _dot_tn(dcbb, cc)
            cols = cols + _head_sums(t_s[:, gsl[g]], seg[g], ch)
            for p in range(hg // 2):
                h0 = hg * g + 2 * p
                dxd_s[:, psl[g][p]] += _dot_tn(lmb_s[h0 * ch:(h0 + 2) * ch, :],
                                               jnp.concatenate(dy_pair(g, p), axis=0))
                cols = cols + _head_sums(gm_s[:, h0 * ch:(h0 + 2) * ch], _seg_matrix(2 * ch, ch, h0), ch)
        for g in range(ng):
            dxd = dxd_s[:, gsl[g]]
            xs = x_ref[:, gsl[g]]
            dx_ref[:, gsl[g]] = dsk_ref[:, gsl[g]] * dy_ref[:, gsl[g]] + dxd * e_dt[:, gsl[g]]
            prod_s[:, gsl[g]] = dxd * xs
        ddt = jnp.zeros((ch, LANES), F32)
        dal = jnp.zeros((1, LANES), F32)
        ddsk = jnp.zeros((1, LANES), F32)
        for g in range(ng):
            ddt = ddt + _head_sums(prod_s[:, gsl[g]], seg[g], ch)
            dal = (dal + _head_sums(csum_s[0:1, gsl[g]], seg[g], 1)
                   + cd_all * _head_sums(csum_s[1:2, gsl[g]], seg[g], 1))
            ddsk = ddsk + _head_sums(csum_s[2:3, gsl[g]], seg[g], 1)
        rowi = lax.broadcasted_iota(jnp.int32, (ch, 1), 0)
        dacs = cols - rows_ref[...].T + jnp.where(rowi == ch - 1, dal, 0.0)
        dla = _dot_hi(jnp.where(ii <= jj, 1.0, 0.0), dacs)
        a_row = a_ref[...]
        ddt_raw = (ddt + dla * a_row) * sig_ref[...]
        ddt_ref[...] = _b(ddt_raw)
        st_ref[0:1, :] += jnp.sum(dla * dt_all, axis=0, keepdims=True) * a_row
        st_ref[1:2, :] += ddsk
        st_ref[2:3, :] += jnp.sum(ddt_raw, axis=0, keepdims=True)

    rc = lambda s: nch - 1 - s
    blk = pl.BlockSpec((ch, LANES), lambda s: (rc(s), 0))
    wide = lambda dt_: pltpu.VMEM((ch, SSM_INNER), dt_)
    sq = lambda n, dt_: pltpu.VMEM((n, ch, ch), dt_)
    scratch = [pltpu.VMEM((ng, SSM_STATE, GROUP_W), F32), pltpu.VMEM((LANES, ch), F32),
               wide(F32), wide(F32), wide(F32),
               wide(BF16), wide(BF16), wide(BF16), wide(BF16),
               sq(ng, F32), wide(F32), wide(F32), sq(ng, F32), sq(ng, F32),
               pltpu.VMEM((ng, SSM_STATE, GROUP_W), F32),
               sq(SSM_HEADS, F32),
               pltpu.VMEM((SSM_HEADS * ch, ch), BF16),
               pltpu.VMEM((ch, SSM_HEADS * ch), F32),
               sq(ng, BF16), wide(F32), wide(F32), wide(F32),
               pltpu.VMEM((8, SSM_INNER), F32)]
    return pl.pallas_call(
        body, name="ssd_bwd", grid=(nch,),
        in_specs=[pl.BlockSpec((ch, CONV_DIM), lambda s: (rc(s), 0)), blk, blk,
                  pl.BlockSpec((1, SSM_HEADS, ch), lambda s: (rc(s), 0, 0)), blk,
                  pl.BlockSpec((1, LANES), lambda s: (0, 0)),
                  pl.BlockSpec((1, SSM_INNER), lambda s: (0, 0)),
                  pl.BlockSpec((1, SSM_GROUPS, SSM_STATE, GROUP_W), lambda s: (rc(s), 0, 0, 0)),
                  pl.BlockSpec((ch, SSM_INNER), lambda s: (rc(s), 0))],
        out_specs=[pl.BlockSpec((ch, CONV_DIM), lambda s: (rc(s), 0)), blk,
                   pl.BlockSpec((8, LANES), lambda s: (0, 0))],
        out_shape=[jax.ShapeDtypeStruct((S, CONV_DIM), F32), jax.ShapeDtypeStruct((S, LANES), BF16),
                   jax.ShapeDtypeStruct((8, LANES), F32)],
        scratch_shapes=scratch,
        compiler_params=_params(("arbitrary",)),
    )(xact, dt, acs, acst, sig, a_neg, dsk_e, hs, dy)


def _pad_lanes(v, n=LANES):
    return jnp.pad(v, ((0, 0), (0, n - v.shape[1])))


def _local_step(x, target, w, ex=None):
    offs = np.cumsum((0,) + IN_SPLITS)
    wt_in = w["w_in_t"]
    w_qkv = wt_in[offs[0]:offs[3]]
    w_z = wt_in[offs[3]:offs[4]]
    w_xbc = wt_in[offs[4]:offs[5]]
    w_dt = jnp.pad(wt_in[offs[5]:offs[6]], ((0, LANES - SSM_HEADS), (0, 0)))
    w_g = wt_in[offs[6]:offs[7]]
    dt_bias = _pad_lanes(w["dt_bias"])
    a_neg = _pad_lanes(-jnp.exp(w["a_log"]))

    u = _rms_fwd(x, w["norm_mix_pre_w"])
    if ex is None:
        xbc = _mm_nn(u, w_xbc, F32, "proj_xbc", tb=True)
    else:
        xbc, got = _mm_nn(u, w_xbc, F32, "proj_xbc", comm=_gather_comm([ex.mine[REST_EARLY]]), tb=True)
        w = {**w, **ex.rest_weights(got[0], REST_EARLY)}
    n_tok = x.shape[0]
    qkv_outs, _ = _mm_epi(u, w_qkv, _qkv_epi(n_tok), "proj_qkv", tb=True, tn=ATT_WIDTH)
    z = _mm_nn(u, w_z, F32, "proj_z", tb=True)
    dt_raw = _mm_nn(u, w_dt, F32, "proj_dt", tb=True)
    gl = _mm_nn(u, w_g, F32, "proj_gate", tb=True)

    pats = [qkv_outs[0]] + [o.reshape(n_tok, 3 * ATT_WIDTH) for o in qkv_outs[1:]]
    os_, ms_, ls_ = [], [], []
    for i, (d, qkv_p) in enumerate(zip(DILATIONS, pats)):
        if ex is not None and i < len(REST_LATE):
            o, m, l, got = _attn_fwd(qkv_p, d, comm=_gather_comm([ex.mine[REST_LATE[i]]]))
            w = {**w, **ex.rest_weights(got[0], REST_LATE[i])}
        else:
            o, m, l = _attn_fwd(qkv_p, d)
        os_.append(o)
        ms_.append(m)
        ls_.append(l)
    att, lse = _attn_combine(os_, ms_, ls_)
    att_o = _mm_nn(att, w["w_att_proj"], F32, "att_proj")

    xact, conv_pre = _conv_fwd(xbc, w["conv_w"], w["conv_b"])
    dsk_e = jnp.repeat(w["d_skip"], SSM_HEAD_DIM, axis=1)
    dt, acs, acst, sig = _ssd_prep(dt_raw, dt_bias, a_neg)
    y_ssd, hs = _ssd_fwd(xact, dt, acs, acst, dsk_e)
    (ssm_y, ssm_o), _ = _mm_epi(None, w["w_ssm_proj"], _ssm_out_epi(y_ssd, z, w["ssm_norm_w"]), "ssm_proj")

    (mi, mixed, h1, f), _ = _mm_epi(None, w["w_out"], _mix_out_epi(
        att_o, ssm_o, gl, x, w["b_gate"], w["norm_mix_post_w"], w["norm_ffn_pre_w"]), "out_proj")
    r_up, act = _mm_nn(f, w["w_up"], BF16, "ffn_up", mode="relu2")
    (dh2, d_down, loss, g_ffn_post), _ = _mm_epi(
        act, w["w_down"], _final_epi(h1, target, w["norm_ffn_post_w"]), "ffn_down")

    g = {"norm_ffn_post_w": g_ffn_post}
    g["w_down"] = _mm_tn(act, d_down, "dw_down")
    dup = _mm_nn(d_down, w["w_down"], BF16, "d_act", mode="mul2", extra=r_up, tb=True)
    g["w_up"] = _mm_tn(f, dup, "dw_up")
    (dh1, d_mixed, g["norm_ffn_pre_w"], g["norm_mix_post_w"]), _ = _mm_epi(
        dup, w["w_up"], _mid_epi(dh2, h1, mixed, w["norm_ffn_pre_w"], w["norm_mix_post_w"]), "d_f", tb=True)
    g["w_out"] = _mm_tn(mi, d_mixed, "dw_out")
    (d_att_o, d_ssm_o, dgl, g["b_gate"]), _ = _mm_epi(
        d_mixed, w["w_out"], _gate_epi(att_o, ssm_o, gl, w["b_gate"]), "d_mi", tb=True)

    g["w_att_proj"] = _mm_tn(att, d_att_o, "dw_att_proj")
    g["w_ssm_proj"] = _mm_tn(ssm_y, d_ssm_o, "dw_ssm_proj")
    gn_epi = _gnorm_epi(y_ssd, z, w["ssm_norm_w"])
    if ex is None:
        (dy_ssd, dz, g["ssm_norm_w"]), _ = _mm_epi(d_ssm_o, w["w_ssm_proj"], gn_epi, "d_ssm_y", tb=True,
                                                    tn=PACK_COLS)
    else:
        gs_rest = jnp.concatenate(
            [_shards_from_full(n, g[n]).reshape(N_CHIPS, -1, PACK_COLS) for n in REST], axis=1)
        (dy_ssd, dz, g["ssm_norm_w"]), recv = _mm_epi(d_ssm_o, w["w_ssm_proj"], gn_epi, "d_ssm_y", tb=True,
                                                       tn=PACK_COLS, tm=1024, comm=_pair_comm([gs_rest]))
        p_rest = _pair_add(gs_rest, recv[0], ex.c_arr, "rs_pair_add_rest")

    d_att = _mm_nn(d_att_o, w["w_att_proj"], F32, "d_att", tb=True)
    bwd_ins = _attn_delta(d_att, att, lse)
    dqs, dks, dvs = [], [], []
    for d, qkv_p, (do_p, lse_p, delta_p) in zip(DILATIONS, pats, bwd_ins):
        if ex is not None and d == DILATIONS[0]:
            dq, dk, dv, recv3 = _attn_bwd(qkv_p, do_p, lse_p, delta_p, d, comm=_chip_comm([p_rest]))
            q_rest = _chip_add(p_rest, recv3[0], ex.chip_arr, "rs_chip_add_rest")
            ex.finish_reduce("rest", q_rest, _comm_call("rs_share_rest", _share_comm([q_rest]))[0])
        else:
            dq, dk, dv = _attn_bwd(qkv_p, do_p, lse_p, delta_p, d)
        dqs.append(dq)
        dks.append(dk)
        dvs.append(dv)
    dqkv = _sum_qkv(dqs, dks, dvs)

    dxact, ddt_raw, stats = _ssd_bwd(xact, dt, acs, acst, sig, a_neg, dsk_e, hs, dy_ssd)
    g["a_log"] = stats[0:1, :SSM_HEADS]
    g["d_skip"] = stats[1:2, :SSM_HEADS]
    g["dt_bias"] = stats[2:3, :SSM_HEADS]
    dxbc, g["conv_w"], g["conv_b"] = _conv_bwd(xbc, conv_pre, dxact, w["conv_w"])

    pieces = [(dqkv, w_qkv), (dz, w_z), (dxbc, w_xbc), (ddt_raw, w_dt), (dgl, w_g)]
    gw = [_mm_tn(dp, u, f"dw_in_{i}") for i, (dp, _) in enumerate(pieces)]
    gw[3] = gw[3][:SSM_HEADS]
    if ex is None:
        g["w_in_t"] = jnp.concatenate(gw, axis=0)
    du = None
    for i, (dp, wp) in enumerate([pieces[k] for k in (1, 2, 0, 3, 4)]):
        if ex is not None and i == 0:
            gs_in = _rows_to_shards(gw, IN_SHARD_ROWS, IN_SHARD_PAD)
            du, recv = _mm_nn(dp, wp, F32, f"d_u_{i}", acc=du, comm=_pair_comm([gs_in]))
            p_in = _pair_add(gs_in, recv[0], ex.c_arr, "rs_pair_add_in")
            rows = p_in.shape[1] // 2
            p_parts = [p_in[:, :rows], p_in[:, rows:]]
            q_parts = []
        elif ex is not None and i in (1, 2):
            p_part = p_parts[i - 1]
            du, recv3 = _mm_nn(dp, wp, F32, f"d_u_{i}", acc=du, comm=_chip_comm([p_part]))
            q_parts.append(_chip_add(p_part, recv3[0], ex.chip_arr, f"rs_chip_add_in_{i}"))
            if i == 2:
                others = _comm_call("rs_share_in", _share_comm(q_parts))
                ex.finish_reduce("w_in", jnp.concatenate(q_parts, axis=0), jnp.concatenate(others, axis=0))
        elif i == len(pieces) - 1:
            (grad_x, g["norm_mix_pre_w"]), _ = _mm_epi(
                dp, wp, _first_epi(du, dh1, x, w["norm_mix_pre_w"]), f"d_u_{i}")
        else:
            du = _mm_nn(dp, wp, F32, f"d_u_{i}", acc=du)
    return loss, grad_x, g


def _rows_to_shards(pieces, shard_rows, pad_rows):
    cols = pieces[0].shape[1]
    shards = []
    for s in range(N_CHIPS):
        lo, hi = s * shard_rows, (s + 1) * shard_rows
        parts, r0 = [], 0
        for p in pieces:
            a, b = max(lo, r0), min(hi, r0 + p.shape[0])
            if a < b:
                parts.append(p[a - r0:b - r0])
            r0 += p.shape[0]
        parts.append(jnp.zeros((pad_rows - shard_rows, cols), pieces[0].dtype))
        shards.append(jnp.concatenate(parts, axis=0))
    return jnp.stack(shards)


BIG = ("w_in", "w_att_proj", "w_ssm_proj", "w_out", "w_up", "w_down")
BIG_FULL_SHAPES = {"w_in": (D_MODEL, IN_PROJ_WIDTH), "w_att_proj": (ATT_WIDTH, D_MODEL),
                   "w_ssm_proj": (SSM_INNER, D_MODEL), "w_out": (D_MODEL, D_MODEL),
                   "w_up": (D_MODEL, FFN_HIDDEN), "w_down": (FFN_HIDDEN, D_MODEL)}
BIG_COL_SHARDED = {"w_in": True, "w_att_proj": True, "w_ssm_proj": False, "w_out": False, "w_up": True,
                   "w_down": False}
PACK_COLS = 1024
SMALL = ("norm_mix_pre_w", "b_gate", "conv_b", "dt_bias", "a_log", "d_skip", "ssm_norm_w",
         "norm_mix_post_w", "norm_ffn_pre_w", "norm_ffn_post_w")
SMALL_ROWS = 232


def _shard_shape(name):
    r, c = BIG_FULL_SHAPES[name]
    return (r, c // N_CHIPS) if BIG_COL_SHARDED[name] else (r // N_CHIPS, c)


def _mesh_pos():
    return lax.axis_index("x"), lax.axis_index("y"), lax.axis_index("c")


def _other_chips(x, y):
    return [(1 - x, y), (x, 1 - y), (1 - x, 1 - y)]


ANY = pl.BlockSpec(memory_space=pl.ANY)


REST_EARLY = ("w_att_proj", "w_ssm_proj", "w_out")
REST_LATE = (("w_up",), ("w_down",))
REST = REST_EARLY + REST_LATE[0] + REST_LATE[1]
ADD_ROWS_CAP = 800
BF16_ROWS = 16
IN_SHARD_ROWS = IN_PROJ_WIDTH // N_CHIPS
IN_SHARD_PAD = 2688


def _stack_rest(shards, dtype, names=REST):
    return jnp.concatenate([shards[n].astype(dtype).reshape(-1, PACK_COLS) for n in names], axis=0)


def _unstack_rest(stacked, lead=(), names=REST):
    out, r0 = {}, 0
    for n in names:
        shp = _shard_shape(n)
        rows = shp[0] * shp[1] // PACK_COLS
        out[n] = stacked[..., r0:r0 + rows, :].reshape(lead + shp)
        r0 += rows
    return out


def _full_from_shards(name, sh):
    if BIG_COL_SHARDED[name]:
        return sh.transpose(1, 0, 2).reshape(BIG_FULL_SHAPES[name])
    return sh.reshape(BIG_FULL_SHAPES[name])


def _shards_from_full(name, full):
    shp = _shard_shape(name)
    if BIG_COL_SHARDED[name]:
        return full.reshape(shp[0], N_CHIPS, shp[1]).transpose(1, 0, 2)
    return full.reshape((N_CHIPS,) + shp)


def _pair_add(g, recv, c_idx, name):
    _, half, cols = recv.shape
    rb = _row_block(half, ADD_ROWS_CAP, BF16_ROWS)
    nb = half // rb

    def body(c_ref, g_ref, r_ref, o_ref):
        o_ref[...] = _b(g_ref[...] + r_ref[...])

    blk = (1, rb, cols)
    return pl.pallas_call(
        body, name=name,
        grid_spec=pltpu.PrefetchScalarGridSpec(
            num_scalar_prefetch=1, grid=(N_CHIPS, nb),
            in_specs=[pl.BlockSpec(blk, lambda s, i, c: (s, c[0] * nb + i, 0)),
                      pl.BlockSpec(blk, lambda s, i, c: (s, i, 0))],
            out_specs=pl.BlockSpec(blk, lambda s, i, c: (s, i, 0))),
        out_shape=jax.ShapeDtypeStruct(recv.shape, BF16),
        compiler_params=_params(("arbitrary", "arbitrary")),
    )(c_idx, g, recv)


def _chip_add(p, recv, me_idx, name):
    _, half, cols = recv.shape
    rb = _row_block(half, ADD_ROWS_CAP, BF16_ROWS)

    def body(m_ref, p_ref, r0_ref, r1_ref, r2_ref, o_ref):
        o_ref[...] = ((p_ref[0].astype(F32) + r0_ref[0].astype(F32)) + r1_ref[0].astype(F32)) + r2_ref[0].astype(F32)

    blk = (1, rb, cols)
    return pl.pallas_call(
        body, name=name,
        grid_spec=pltpu.PrefetchScalarGridSpec(
            num_scalar_prefetch=1, grid=(half // rb,),
            in_specs=[pl.BlockSpec(blk, lambda i, m: (m[0], i, 0)),
                      pl.BlockSpec(blk, lambda i, m: (0, i, 0)),
                      pl.BlockSpec(blk, lambda i, m: (1, i, 0)),
                      pl.BlockSpec(blk, lambda i, m: (2, i, 0))],
            out_specs=pl.BlockSpec((rb, cols), lambda i, m: (i, 0))),
        out_shape=jax.ShapeDtypeStruct((half, cols), F32),
        compiler_params=_params(("arbitrary",)),
    )(me_idx, p, recv, recv, recv)


def _gather_plan():
    def copies(w_refs, out_refs, send_sems, recv_sems):
        x, y, c = _mesh_pos()
        me = 2 * x + y
        sibling = (x, y, 1 - c)
        chips = _other_chips(x, y)
        idx = [2 * chip[0] + chip[1] for chip in chips]
        plans = []
        for a, (w_ref, out_ref) in enumerate(zip(w_refs, out_refs)):
            half = w_ref.shape[0] // 2

            def copy(k, chip, h, to, src=None, out_ref=out_ref, half=half, a=a):
                rows = out_ref.at[chip, pl.ds(h * half, half), :]
                return pltpu.make_async_remote_copy(
                    src_ref=rows if src is None else src, dst_ref=rows,
                    send_sem=send_sems.at[6 * a + k], recv_sem=recv_sems.at[6 * a + k],
                    device_id=to, device_id_type=MESH)

            mine_half = w_ref.at[pl.ds(c * half, half), :]
            send = [copy(j, me, c, (*chip, c), src=mine_half) for j, chip in enumerate(chips)]
            land = [copy(j, idx[j], c, (x, y, c)) for j in range(N_CHIPS - 1)]
            forward = [copy(3 + j, idx[j], c, sibling) for j in range(N_CHIPS - 1)]
            land_fw = [copy(3 + j, idx[j], 1 - c, (x, y, c)) for j in range(N_CHIPS - 1)]
            plans.append((send, land, forward, land_fw))
        return plans

    def start(*refs):
        for send, _, _, _ in copies(*refs):
            for cp in send:
                cp.start()

    def finish(*refs):
        plans = copies(*refs)
        for _, land, forward, _ in plans:
            for j in range(N_CHIPS - 1):
                land[j].wait_recv()
                forward[j].start()
        for _, _, _, land_fw in plans:
            for cp in land_fw:
                cp.wait_recv()
        for send, _, forward, _ in plans:
            for cp in send + forward:
                cp.wait_send()

    return start, finish


def _pair_plan(halves):
    def copies(in_refs, out_refs, send_sems, recv_sems):
        x, y, c = _mesh_pos()
        cps = []
        for a, (g_ref, out_ref) in enumerate(zip(in_refs, out_refs)):
            if halves:
                half = g_ref.shape[1] // 2
                src = g_ref.at[:, pl.ds((1 - c) * half, half), :]
            else:
                src = g_ref
            cps.append(pltpu.make_async_remote_copy(
                src_ref=src, dst_ref=out_ref, send_sem=send_sems.at[a], recv_sem=recv_sems.at[a],
                device_id=(x, y, 1 - c), device_id_type=MESH))
        return cps

    def start(*refs):
        for cp in copies(*refs):
            cp.start()

    def finish(*refs):
        for cp in copies(*refs):
            cp.wait()

    return start, finish


def _chip_plan():
    def copies(in_refs, out_refs, send_sems, recv_sems):
        x, y, c = _mesh_pos()
        chips = _other_chips(x, y)
        return [pltpu.make_async_remote_copy(
            src_ref=p_ref.at[2 * chip[0] + chip[1]], dst_ref=out_ref.at[j],
            send_sem=send_sems.at[3 * a + j], recv_sem=recv_sems.at[3 * a + j], device_id=(*chip, c),
            device_id_type=MESH)
            for a, (p_ref, out_ref) in enumerate(zip(in_refs, out_refs)) for j, chip in enumerate(chips)]

    def start(*refs):
        for cp in copies(*refs):
            cp.start()

    def finish(*refs):
        cps = copies(*refs)
        for cp in cps:
            cp.wait_recv()
        for cp in cps:
            cp.wait_send()

    return start, finish


def _gather_comm(shards):
    return _Comm(_gather_plan(), shards, [jax.ShapeDtypeStruct((N_CHIPS,) + s.shape, s.dtype) for s in shards],
                 6 * len(shards))


def _pair_comm(gs):
    return _Comm(_pair_plan(True), gs,
                 [jax.ShapeDtypeStruct((N_CHIPS, g.shape[1] // 2, g.shape[2]), g.dtype) for g in gs], len(gs))


def _chip_comm(ps):
    return _Comm(_chip_plan(), ps, [jax.ShapeDtypeStruct((N_CHIPS - 1,) + p.shape[1:], p.dtype) for p in ps],
                 3 * len(ps))


def _share_comm(qs):
    return _Comm(_pair_plan(False), qs, [jax.ShapeDtypeStruct(q.shape, q.dtype) for q in qs], len(qs))


def _comm_call(name, comm):
    n, m = len(comm.ins), len(comm.outs)

    def body(*refs):
        args = (refs[:n], refs[n:n + m], refs[n + m], refs[n + m + 1])
        comm.start(*args)
        comm.finish(*args)

    return pl.pallas_call(
        body, name=name, out_shape=comm.outs, in_specs=[ANY] * n, out_specs=[ANY] * m,
        scratch_shapes=[pltpu.SemaphoreType.DMA((comm.n_sems,))] * 2,
        compiler_params=pltpu.CompilerParams(has_side_effects=True),
    )(*comm.ins)


class _Exchange:
    def __init__(self, chip, ci, mine):
        self.chip, self.ci = chip, ci
        self.mine = mine
        self.c_arr = ci.reshape(1).astype(jnp.int32)
        self.chip_arr = chip.reshape(1).astype(jnp.int32)
        self.reduced = {}

    def rest_weights(self, got, names):
        stacks = lax.dynamic_update_slice(got, self.mine[names][None], (self.chip, 0, 0))
        return {n: _full_from_shards(n, sh) for n, sh in _unstack_rest(stacks, (N_CHIPS,), names).items()}

    def finish_reduce(self, key, mine, other):
        south = self.ci == 0
        self.reduced[key] = jnp.concatenate([jnp.where(south, mine, other), jnp.where(south, other, mine)],
                                            axis=0)


def _allreduce_small(part, name):
    rows = part.shape[0]

    def body(p_ref, out_ref, buf, send_sems, recv_sems, local_sem):
        x, y, c = _mesh_pos()
        me, sibling = (x, y, c), (x, y, 1 - c)
        chips = _other_chips(x, y)

        def slot(px, py, pc):
            return buf.at[pl.ds((4 * px + 2 * py + pc) * rows, rows), :]

        def copy(k, block, to, src=None):
            return pltpu.make_async_remote_copy(
                src_ref=slot(*block) if src is None else src, dst_ref=slot(*block),
                send_sem=send_sems.at[k], recv_sem=recv_sems.at[k], device_id=to, device_id_type=MESH)

        mine = pltpu.make_async_copy(p_ref, slot(*me), local_sem)
        mine.start()
        first = [copy(0, me, sibling, src=p_ref)]
        first += [copy(1 + j, me, (*chip, c), src=p_ref) for j, chip in enumerate(chips)]
        for cp in first:
            cp.start()
        passed = [copy(4 + j, (*chip, c), sibling) for j, chip in enumerate(chips)]
        for j, chip in enumerate(chips):
            copy(1 + j, (*chip, c), me).wait_recv()
            passed[j].start()
        copy(0, sibling, me).wait_recv()
        for j, chip in enumerate(chips):
            copy(4 + j, (*chip, 1 - c), me).wait_recv()
        for cp in first + passed:
            cp.wait_send()
        mine.wait()
        acc = buf[pl.ds(0, rows), :]
        for k in range(1, N_DEV):
            acc = acc + buf[pl.ds(k * rows, rows), :]
        out_ref[...] = acc

    return pl.pallas_call(
        body, name=name,
        out_shape=jax.ShapeDtypeStruct(part.shape, F32),
        in_specs=[pl.BlockSpec(memory_space=pltpu.VMEM)],
        out_specs=pl.BlockSpec(memory_space=pltpu.VMEM),
        scratch_shapes=[pltpu.VMEM((N_DEV * rows, LANES), F32), pltpu.SemaphoreType.DMA((7,)),
                        pltpu.SemaphoreType.DMA((7,)), pltpu.SemaphoreType.DMA],
        compiler_params=pltpu.CompilerParams(has_side_effects=True),
    )(part)


def _adamw(w, g, m, v, name):
    R, C = w.shape
    bs = _row_block(R, 512, 8) if R % 8 == 0 else R
    c1 = 1.0 / (1.0 - ADAM_B1 ** ADAM_STEP)
    c2 = 1.0 / (1.0 - ADAM_B2 ** ADAM_STEP)

    def body(w_ref, g_ref, m_ref, v_ref, d_ref, nm_ref, nv_ref):
        gg = g_ref[...]
        nm = ADAM_B1 * m_ref[...] + (1.0 - ADAM_B1) * gg
        nv = ADAM_B2 * v_ref[...] + (1.0 - ADAM_B2) * (gg * gg)
        nm_ref[...] = nm
        nv_ref[...] = nv
        d_ref[...] = -ADAM_LR * ((nm * c1) / (jnp.sqrt(nv * c2) + ADAM_EPS) + ADAM_WD * w_ref[...])

    spec = pl.BlockSpec((bs, C), lambda i: (i, 0))
    shp = jax.ShapeDtypeStruct((R, C), F32)
    return pl.pallas_call(
        body, name=name, grid=(R // bs,), in_specs=[spec] * 4, out_specs=[spec] * 3, out_shape=[shp] * 3,
        compiler_params=_params(("parallel",)),
    )(w, g, m, v)


WEIGHTS = ("norm_mix_pre_w", "w_in", "b_gate", "conv_w", "conv_b", "dt_bias", "a_log", "d_skip",
           "ssm_norm_w", "w_att_proj", "w_ssm_proj", "w_out", "norm_mix_post_w", "norm_ffn_pre_w", "w_up",
           "w_down", "norm_ffn_post_w")


def _flat_small(vals, conv_w_full):
    flat = [vals[n].reshape(-1) for n in SMALL] + [conv_w_full.reshape(-1)]
    v = jnp.concatenate(flat)
    return jnp.pad(v, (0, SMALL_ROWS * LANES - v.shape[0])).reshape(SMALL_ROWS, LANES)


def kernel(x, norm_mix_pre_w, w_in, b_gate, conv_w, conv_b, dt_bias, a_log, d_skip, ssm_norm_w, w_att_proj, w_ssm_proj, w_out, norm_mix_post_w, norm_ffn_pre_w, w_up, w_down, norm_ffn_post_w, loss_target, m_norm_mix_pre_w, m_w_in, m_b_gate, m_conv_w, m_conv_b, m_dt_bias, m_a_log, m_d_skip, m_ssm_norm_w, m_w_att_proj, m_w_ssm_proj, m_w_out, m_norm_mix_post_w, m_norm_ffn_pre_w, m_w_up, m_w_down, m_norm_ffn_post_w, v_norm_mix_pre_w, v_w_in, v_b_gate, v_conv_w, v_conv_b, v_dt_bias, v_a_log, v_d_skip, v_ssm_norm_w, v_w_att_proj, v_w_ssm_proj, v_w_out, v_norm_mix_post_w, v_norm_ffn_pre_w, v_w_up, v_w_down, v_norm_ffn_post_w):
    args = locals()

    def strip(a):
        return a[0] if a.ndim == 3 else a

    wts = {n: strip(args[n]) for n in WEIGHTS}
    mom = {n: strip(args["m_" + n]) for n in WEIGHTS}
    var = {n: strip(args["v_" + n]) for n in WEIGHTS}
    xi, yi, ci = _mesh_pos()
    chip = 2 * xi + yi

    tr = lambda a: jnp.swapaxes(a, 0, 1)
    w_in_mine = jnp.pad(tr(wts["w_in"]).astype(BF16), ((0, IN_SHARD_PAD - IN_SHARD_ROWS), (0, 0)))
    got_in = _comm_call("allgather_w_in", _gather_comm([w_in_mine]))[0]
    stacks_in = lax.dynamic_update_slice(got_in, w_in_mine[None], (chip, 0, 0))
    full = {"w_in_t": stacks_in[:, :IN_SHARD_ROWS].reshape(IN_PROJ_WIDTH, D_MODEL)}
    ex = _Exchange(chip, ci, {names: _stack_rest(wts, BF16, names) for names in (REST_EARLY,) + REST_LATE})
    cw_cols = CONV_DIM // N_CHIPS
    conv_slab = lax.dynamic_update_slice(jnp.zeros((SSM_CONV, CONV_DIM), F32),
                                         jnp.where(ci == 0, wts["conv_w"], 0.0), (0, chip * cw_cols))
    small_in = jnp.pad(conv_slab.reshape(-1), (0, SMALL_ROWS * LANES - SSM_CONV * CONV_DIM))
    conv_full = _allreduce_small(small_in.reshape(SMALL_ROWS, LANES), "gather_conv_w")
    full["conv_w"] = conv_full.reshape(-1)[:SSM_CONV * CONV_DIM].reshape(SSM_CONV, CONV_DIM)
    for n in SMALL:
        full[n] = wts[n]

    loss_part, grad_x, g = _local_step(x[0], loss_target[0], full, ex)
    loss = lax.psum(loss_part[0, 0], ("x", "y", "c"))

    gshard = _unstack_rest(ex.reduced["rest"])
    g_in_t = ex.reduced["w_in"][:IN_SHARD_ROWS]
    small_sum = _allreduce_small(_flat_small(g, g["conv_w"]), "allreduce_small_grads").reshape(-1)
    grads, off = {}, 0
    for n in SMALL:
        sz = wts[n].size
        grads[n] = small_sum[off:off + sz].reshape(wts[n].shape)
        off += sz
    conv_g = small_sum[off:off + SSM_CONV * CONV_DIM].reshape(SSM_CONV, CONV_DIM)
    grads["conv_w"] = lax.dynamic_slice(conv_g, (0, chip * cw_cols), (SSM_CONV, cw_cols))
    grads.update(gshard)

    delta, new_m, new_v = {}, {}, {}
    for n in REST:
        delta[n], new_m[n], new_v[n] = _adamw(wts[n], grads[n], mom[n], var[n], f"adamw_{n}")
    in_t = _adamw(tr(wts["w_in"]), g_in_t, tr(mom["w_in"]), tr(var["w_in"]), "adamw_w_in")
    grads["w_in"] = tr(g_in_t)
    delta["w_in"], new_m["w_in"], new_v["w_in"] = (tr(a) for a in in_t)
    small_names = SMALL + ("conv_w",)

    def pack_small(d):
        v = jnp.concatenate([d[n].reshape(-1) for n in small_names])
        rows = -(-v.shape[0] // (8 * LANES)) * 8
        return jnp.pad(v, (0, rows * LANES - v.shape[0])).reshape(rows, LANES)

    ds, ms, vs = _adamw(pack_small(wts), pack_small(grads), pack_small(mom), pack_small(var), "adamw_small")
    off = 0
    for n in small_names:
        sz = wts[n].size
        for dst, src in ((delta, ds), (new_m, ms), (new_v, vs)):
            dst[n] = src.reshape(-1)[off:off + sz].reshape(wts[n].shape)
        off += sz

    out = [loss, grad_x[None]]
    for d in (grads, delta, new_m, new_v):
        out += [d[n][None] if args[n].ndim == 3 else d[n] for n in WEIGHTS]
    return tuple(out)
```

```python
import math

import numpy as np
import jax
import jax.numpy as jnp
from jax import lax
from jax.experimental import pallas as pl
from jax.experimental.pallas import tpu as pltpu

F32 = jnp.float32
BF16 = jnp.bfloat16

D_MODEL = 1024
HEAD_DIM = 64
N_ATT_HEADS = 12
ATT_WIDTH = N_ATT_HEADS * HEAD_DIM
DILATIONS = (1, 4, 16)
ATT_BLOCK = 128
SSM_INNER = 2048
SSM_HEADS = 32
SSM_GROUPS = 8
HEADS_PER_GROUP = SSM_HEADS // SSM_GROUPS
SSM_HEAD_DIM = 64
SSM_STATE = 128
SSM_CONV = 4
SSM_CHUNK = 128
CONV_DIM = SSM_INNER + 2 * SSM_GROUPS * SSM_STATE
FFN_HIDDEN = 4 * D_MODEL
IN_SPLITS = (ATT_WIDTH, ATT_WIDTH, ATT_WIDTH, SSM_INNER, CONV_DIM, SSM_HEADS, 2 * D_MODEL)
IN_PROJ_WIDTH = sum(IN_SPLITS)
RMS_EPS = 1e-6
LANES = 128
NEG_BIG = -1e30

ADAM_LR = 0.001
ADAM_B1 = 0.9
ADAM_B2 = 0.999
ADAM_EPS = 1e-08
ADAM_WD = 0.01
ADAM_STEP = 10

N_CHIPS = 4
N_DEV = 8
VMEM_LIMIT = 56 * 1024 * 1024
MESH = pl.DeviceIdType.MESH


def _alibi_slopes(n):
    def pow2(m):
        start = 2.0 ** (-8.0 / m)
        return [start ** (i + 1) for i in range(m)]
    if (n & (n - 1)) == 0:
        s = pow2(n)
    else:
        c = 2 ** int(math.floor(math.log2(n)))
        s = pow2(c) + pow2(2 * c)[0::2][: n - c]
    return [float(v) for v in np.array(s, dtype=np.float32)]


def _params(sem):
    return pltpu.CompilerParams(dimension_semantics=sem, vmem_limit_bytes=VMEM_LIMIT)


def _dot(a, b):
    return lax.dot_general(a, b, (((1,), (0,)), ((), ())), preferred_element_type=F32)


def _dot_nt(a, b):
    return lax.dot_general(a, b, (((1,), (1,)), ((), ())), preferred_element_type=F32)


def _dot_tn(a, b):
    return lax.dot_general(a, b, (((0,), (0,)), ((), ())), preferred_element_type=F32)


def _dot_hi(a, b):
    return lax.dot_general(a, b, (((1,), (0,)), ((), ())), preferred_element_type=F32,
                           precision=lax.Precision.HIGHEST)


def _b(x):
    return x.astype(BF16)


def _sigmoid(x):
    return 1.0 / (1.0 + jnp.exp(-x))


def _pick(n, cands):
    for c in cands:
        if n % c == 0:
            return c
    raise ValueError(f"no tile for {n}")


def _row_block(rows, cap, mult):
    best = max(d for d in range(mult, cap + 1, mult) if rows % d == 0)
    return best


class _Comm:
    def __init__(self, plan, ins, outs, n_sems):
        self.start, self.finish = plan
        self.ins, self.outs, self.n_sems = list(ins), list(outs), n_sems


def _mm_nn(a, b, out_dtype, name, acc=None, mode=None, extra=None, comm=None, tb=False):
    M, K = a.shape
    N = b.shape[0] if tb else b.shape[1]
    tn = _pick(N, ((2048,) if K <= 1024 else ()) + (1024, 768, 512, 256, 128))
    tk = K if K <= 4096 else _pick(K, (2048, 1024))
    tm = 1024 if M % 1024 == 0 and K <= 2304 else 512
    nk = K // tk
    nj, ni = N // tn, M // tm
    side = acc if acc is not None else extra
    n_out = 2 if mode == "relu2" else 1
    n_in = 2 + (side is not None)
    n_ci = len(comm.ins) if comm else 0
    n_co = len(comm.outs) if comm else 0

    def body(*refs):
        a_ref, b_ref = refs[0], refs[1]
        s_ref = refs[2] if side is not None else None
        o_refs = refs[n_in + n_ci:n_in + n_ci + n_out]
        if comm:
            c_args = (refs[n_in:n_in + n_ci], refs[n_in + n_ci + n_out:n_in + n_ci + n_out + n_co],
                      refs[-2], refs[-1])
            pj, pi, pk = pl.program_id(0), pl.program_id(1), pl.program_id(2)

            @pl.when(jnp.logical_and(jnp.logical_and(pj == 0, pi == 0), pk == 0))
            def _():
                comm.start(*c_args)

        def finish(r):
            if mode == "relu2":
                r = jnp.maximum(r, 0.0)
                o_refs[0][...] = _b(r)
                o_refs[1][...] = _b(r * r)
            elif mode == "mul2":
                o_refs[0][...] = _b(r * (2.0 * s_ref[...].astype(F32)))
            else:
                if acc is not None:
                    r = r + s_ref[...]
                o_refs[0][...] = r.astype(out_dtype)

        part = (_dot_nt if tb else _dot)(_b(a_ref[...]), _b(b_ref[...]))
        if nk == 1:
            finish(part)
        else:
            acc_ref = refs[n_in + n_ci + n_out + n_co]
            k = pl.program_id(2)

            @pl.when(k == 0)
            def _():
                acc_ref[...] = part

            @pl.when(jnp.logical_and(k > 0, k < nk - 1))
            def _():
                acc_ref[...] += part

            @pl.when(k == nk - 1)
            def _():
                finish(acc_ref[...] + part)

        if comm:
            @pl.when(jnp.logical_and(jnp.logical_and(pj == nj - 1, pi == ni - 1), pk == nk - 1))
            def _():
                comm.finish(*c_args)

    tile = pl.BlockSpec((tm, tn), lambda j, i, k: (i, j))
    in_specs = [pl.BlockSpec((tm, tk), lambda j, i, k: (i, k)),
                pl.BlockSpec((tn, tk), lambda j, i, k: (j, k)) if tb else
                pl.BlockSpec((tk, tn), lambda j, i, k: (k, j))]
    args = [a, b]
    if side is not None:
        in_specs.append(tile)
        args.append(side)
    odt = BF16 if mode in ("relu2", "mul2") else out_dtype
    scratch = [pltpu.VMEM((tm, tn), F32)] if nk > 1 else []
    if comm:
        scratch += [pltpu.SemaphoreType.DMA((comm.n_sems,))] * 2
        params = pltpu.CompilerParams(dimension_semantics=("arbitrary",) * 3, vmem_limit_bytes=VMEM_LIMIT,
                                      has_side_effects=True)
    else:
        params = _params(("parallel", "parallel", "arbitrary"))
    outs = pl.pallas_call(
        body, name=name, grid=(nj, ni, nk),
        in_specs=in_specs + [ANY] * n_ci,
        out_specs=[tile] * n_out + [ANY] * n_co,
        out_shape=[jax.ShapeDtypeStruct((M, N), odt)] * n_out + list(comm.outs if comm else []),
        scratch_shapes=scratch,
        compiler_params=params,
    )(*args, *(comm.ins if comm else []))
    res = outs[:n_out] if n_out > 1 else outs[0]
    return (res, outs[n_out:]) if comm else res


class _Epi:
    def __init__(self, fn, row_ins=(), full_ins=(), row_outs=(), acc_outs=(), tiled=False, a_fn=None,
                 scratch=()):
        self.fn, self.row_ins, self.full_ins = fn, list(row_ins), list(full_ins)
        self.row_outs, self.acc_outs, self.tiled = list(row_outs), list(acc_outs), tiled
        self.a_fn = a_fn
        self.scratch = list(scratch)


def _acc_into(ref, val, first):
    @pl.when(first)
    def _():
        ref[...] = val

    @pl.when(jnp.logical_not(first))
    def _():
        ref[...] += val


def _mm_epi(a, b, epi, name, tb=False, comm=None, tm=512, tn=None):
    N, K = b.shape if tb else b.shape[::-1]
    M = epi.row_ins[0][0].shape[0] if a is None else a.shape[0]
    tn = tn or N
    assert epi.tiled or tn == N
    tk = K if K <= 4096 else _pick(K, (2048, 1024))
    nk = K // tk
    nj, ni = N // tn, M // tm
    assert a is not None or (nk == 1 and nj == 1)
    n_a = 0 if a is None else 1
    n_ri, n_fi, n_ro, n_ao = len(epi.row_ins), len(epi.full_ins), len(epi.row_outs), len(epi.acc_outs)
    n_ci = len(comm.ins) if comm else 0
    n_co = len(comm.outs) if comm else 0
    i0 = n_a + 1
    o0 = i0 + n_ci + n_ri + n_fi

    def body(*refs):
        b_ref = refs[n_a]
        ri = refs[i0 + n_ci:i0 + n_ci + n_ri]
        fi = refs[i0 + n_ci + n_ri:o0]
        ro = refs[o0 + n_co:o0 + n_co + n_ro]
        ao = refs[o0 + n_co + n_ro:o0 + n_co + n_ro + n_ao]
        pj, pi, pk = pl.program_id(0), pl.program_id(1), pl.program_id(2)
        if comm:
            c_args = (refs[i0:i0 + n_ci], refs[o0:o0 + n_co], refs[-2], refs[-1])

            @pl.when(jnp.logical_and(jnp.logical_and(pj == 0, pi == 0), pk == 0))
            def _():
                comm.start(*c_args)

        s0 = o0 + n_co + n_ro + n_ao + (nk > 1)
        extra = (refs[s0:s0 + len(epi.scratch)],) if epi.scratch else ()
        a_val = epi.a_fn(ri, fi, ro) if a is None else _b(refs[0][...])
        part = (_dot_nt if tb else _dot)(a_val, _b(b_ref[...]))
        if nk == 1:
            epi.fn(part, ri, fi, ro, ao, pi == 0, *extra)
        else:
            acc_ref = refs[o0 + n_co + n_ro + n_ao]

            @pl.when(pk == 0)
            def _():
                acc_ref[...] = part

            @pl.when(jnp.logical_and(pk > 0, pk < nk - 1))
            def _():
                acc_ref[...] += part

            @pl.when(pk == nk - 1)
            def _():
                epi.fn(acc_ref[...] + part, ri, fi, ro, ao, pi == 0, *extra)

        if comm:
            @pl.when(jnp.logical_and(jnp.logical_and(pj == nj - 1, pi == ni - 1), pk == nk - 1))
            def _():
                comm.finish(*c_args)

    def row_spec(width, cb):
        if epi.tiled:
            return pl.BlockSpec((tm, tn), lambda j, i, k: (i, j + cb))
        return pl.BlockSpec((tm, width), lambda j, i, k: (i, cb))

    in_specs = [pl.BlockSpec((tm, tk), lambda j, i, k: (i, k))] * n_a
    in_specs += [pl.BlockSpec((tn, tk), lambda j, i, k: (j, k)) if tb else
                 pl.BlockSpec((tk, tn), lambda j, i, k: (k, j))]
    in_specs += [ANY] * n_ci
    in_specs += [row_spec(w, cb) for (_, w, cb) in epi.row_ins]
    in_specs += [pl.BlockSpec((1, tn), lambda j, i, k: (0, j)) if epi.tiled else
                 pl.BlockSpec(f.shape, lambda j, i, k: (0, 0)) for f in epi.full_ins]
    out_specs = [ANY] * n_co + [row_spec(c, 0) if isinstance(c, int) else spec(tm, tn) for c, spec in epi.row_outs]
    out_specs += [pl.BlockSpec((1, tn), lambda j, i, k: (0, j)) if epi.tiled else
                  pl.BlockSpec((1, c), lambda j, i, k: (0, 0)) for c in epi.acc_outs]
    out_shape = list(comm.outs if comm else [])
    out_shape += [jax.ShapeDtypeStruct((M, c), dt_) if isinstance(c, int) else c for c, dt_ in epi.row_outs]
    out_shape += [jax.ShapeDtypeStruct((1, c), F32) for c in epi.acc_outs]
    scratch = ([pltpu.VMEM((tm, tn), F32)] if nk > 1 else []) + epi.scratch
    if comm:
        scratch += [pltpu.SemaphoreType.DMA((comm.n_sems,))] * 2
    params = pltpu.CompilerParams(dimension_semantics=("arbitrary",) * 3, vmem_limit_bytes=VMEM_LIMIT,
                                  has_side_effects=comm is not None)
    outs = pl.pallas_call(
        body, name=name, grid=(nj, ni, nk), in_specs=in_specs, out_specs=out_specs, out_shape=out_shape,
        scratch_shapes=scratch, compiler_params=params,
    )(*([a] * n_a), b, *(comm.ins if comm else []), *[arr for arr, _, _ in epi.row_ins], *epi.full_ins)
    return outs[n_co:], (outs[:n_co] if comm else None)


def _mm_tn(a, b, name):
    S, Ka = a.shape
    _, N = b.shape
    tka = _pick(Ka, ((2048,) if N <= 1024 else ()) + (1024, 768, 512, 256, 128))
    tn = _pick(N, ((2048,) if Ka <= 1024 else ()) + (1024, 768, 512, 256, 128))
    ts = 1024 if S % 1024 == 0 else 512
    ns = S // ts

    def body(a_ref, b_ref, o_ref, acc_ref):
        s = pl.program_id(2)
        part = _dot_tn(_b(a_ref[...]), _b(b_ref[...]))

        @pl.when(s == 0)
        def _():
            acc_ref[...] = part

        @pl.when(s > 0)
        def _():
            acc_ref[...] += part

        @pl.when(s == ns - 1)
        def _():
            o_ref[...] = acc_ref[...]

    return pl.pallas_call(
        body, name=name, grid=(Ka // tka, N // tn, ns),
        in_specs=[pl.BlockSpec((ts, tka), lambda i, j, s: (s, i)),
                  pl.BlockSpec((ts, tn), lambda i, j, s: (s, j))],
        out_specs=pl.BlockSpec((tka, tn), lambda i, j, s: (i, j)),
        out_shape=jax.ShapeDtypeStruct((Ka, N), F32),
        scratch_shapes=[pltpu.VMEM((tka, tn), F32)],
        compiler_params=_params(("parallel", "parallel", "arbitrary")),
    )(a, b)


def _row_call(body, row_ins, full_ins, row_outs, acc_outs, bs, name):
    S = row_ins[0].shape[0]
    assert S % bs == 0
    in_specs = [pl.BlockSpec((bs, a.shape[1]), lambda i: (i, 0)) for a in row_ins]
    in_specs += [pl.BlockSpec(a.shape, lambda i: (0, 0)) for a in full_ins]
    out_specs = [pl.BlockSpec((bs, c), lambda i: (i, 0)) for c, _ in row_outs]
    out_specs += [pl.BlockSpec(s, lambda i: (0, 0)) for s in acc_outs]
    out_shape = [jax.ShapeDtypeStruct((S, c), dt) for c, dt in row_outs]
    out_shape += [jax.ShapeDtypeStruct(s, F32) for s in acc_outs]
    return pl.pallas_call(
        body, name=name, grid=(S // bs,), in_specs=in_specs, out_specs=out_specs, out_shape=out_shape,
        compiler_params=_params(("arbitrary",)),
    )(*row_ins, *full_ins)


def _rms_vals(x, w):
    r = lax.rsqrt(jnp.mean(x * x, axis=-1, keepdims=True) + RMS_EPS)
    return x * r * w


def _rms_bwd_vals(x, w, dy):
    r = lax.rsqrt(jnp.mean(x * x, axis=-1, keepdims=True) + RMS_EPS)
    xn = x * r
    g = dy * w
    dx = r * (g - xn * jnp.mean(g * xn, axis=-1, keepdims=True))
    dw = jnp.sum(dy * xn, axis=0, keepdims=True)
    return dx, dw


def _rms_fwd(x, w):
    def body(x_ref, w_ref, o_ref):
        o_ref[...] = _b(_rms_vals(x_ref[...], w_ref[...]))
    return _row_call(body, [x], [w], [(x.shape[1], BF16)], [], 512, "rms_fwd")[0]


def _group_rms(t):
    gw = SSM_INNER // SSM_GROUPS
    out = []
    for g in range(SSM_GROUPS):
        tg = t[:, g * gw:(g + 1) * gw]
        out.append(lax.rsqrt(jnp.mean(tg * tg, axis=-1, keepdims=True) + RMS_EPS))
    return out


def _ssm_out_epi(y, z, w):
    gw = SSM_INNER // SSM_GROUPS

    def a_fn(ri, fi, ro):
        zz = ri[1][...]
        t = ri[0][...] * (zz * _sigmoid(zz))
        rs = _group_rms(t)
        for g in range(SSM_GROUPS):
            sl = slice(g * gw, (g + 1) * gw)
            ro[0][:, sl] = _b(t[:, sl] * rs[g] * fi[0][:, sl])
        return ro[0][...]

    def fn(r, ri, fi, ro, ao, first):
        ro[1][...] = r
    return _Epi(fn, [(y, SSM_INNER, 0), (z, SSM_INNER, 0)], [w], [(SSM_INNER, BF16), (D_MODEL, F32)], a_fn=a_fn)


def _mix_out_epi(att_o, ssm_o, gl, x, b_gate, w_post, w_pre):
    def a_fn(ri, fi, ro):
        g = _sigmoid(ri[2][...] + fi[0][...])
        mi = _b(g[:, :D_MODEL] * ri[0][...] + g[:, D_MODEL:] * ri[1][...])
        ro[0][...] = mi
        return mi

    def fn(r, ri, fi, ro, ao, first):
        ro[1][...] = r
        h = ri[3][...] + _rms_vals(r, fi[1][...])
        ro[2][...] = h
        ro[3][...] = _b(_rms_vals(h, fi[2][...]))
    return _Epi(fn, [(att_o, D_MODEL, 0), (ssm_o, D_MODEL, 0), (gl, 2 * D_MODEL, 0), (x, D_MODEL, 0)],
                [b_gate, w_post, w_pre],
                [(D_MODEL, BF16), (D_MODEL, F32), (D_MODEL, F32), (D_MODEL, BF16)], a_fn=a_fn)


def _final_epi(h1, target, w_post):
    def fn(dn, ri, fi, ro, ao, first):
        w = fi[0][...]
        err = ri[0][...] + _rms_vals(dn, w) - ri[1][...]
        row = jnp.mean(err * err, axis=-1, keepdims=True)
        part = 0.5 * jnp.sum(row, axis=0, keepdims=True)
        dh = err * (1.0 / D_MODEL)
        ro[0][...] = dh
        dx, dw = _rms_bwd_vals(dn, w, dh)
        ro[1][...] = _b(dx)
        _acc_into(ao[0], jnp.broadcast_to(part, (1, LANES)), first)
        _acc_into(ao[1], dw, first)
    return _Epi(fn, [(h1, D_MODEL, 0), (target, D_MODEL, 0)], [w_post], [(D_MODEL, F32), (D_MODEL, BF16)],
                [LANES, D_MODEL])


def _mid_epi(dh2, h1, mixed, w_pre, w_post):
    def fn(df, ri, fi, ro, ao, first):
        dx, dwn = _rms_bwd_vals(ri[1][...], fi[0][...], df)
        dh1 = ri[0][...] + dx
        ro[0][...] = dh1
        dm, dwp = _rms_bwd_vals(ri[2][...], fi[1][...], dh1)
        ro[1][...] = _b(dm)
        _acc_into(ao[0], dwn, first)
        _acc_into(ao[1], dwp, first)
    return _Epi(fn, [(dh2, D_MODEL, 0), (h1, D_MODEL, 0), (mixed, D_MODEL, 0)], [w_pre, w_post],
                [(D_MODEL, F32), (D_MODEL, BF16)], [D_MODEL, D_MODEL])


def _gate_epi(att_o, ssm_o, gl, b_gate):
    def fn(d, ri, fi, ro, ao, first):
        g = _sigmoid(ri[2][...] + fi[0][...])
        ga, gs = g[:, :D_MODEL], g[:, D_MODEL:]
        ro[0][...] = _b(ga * d)
        ro[1][...] = _b(gs * d)
        dga = d * ri[0][...] * ga * (1.0 - ga)
        dgs = d * ri[1][...] * gs * (1.0 - gs)
        ro[2][:, :D_MODEL] = _b(dga)
        ro[2][:, D_MODEL:] = _b(dgs)
        _acc_into(ao[0].at[:, pl.ds(0, D_MODEL)], jnp.sum(dga, axis=0, keepdims=True), first)
        _acc_into(ao[0].at[:, pl.ds(D_MODEL, D_MODEL)], jnp.sum(dgs, axis=0, keepdims=True), first)
    return _Epi(fn, [(att_o, D_MODEL, 0), (ssm_o, D_MODEL, 0), (gl, 2 * D_MODEL, 0)], [b_gate],
                [(D_MODEL, BF16), (D_MODEL, BF16), (2 * D_MODEL, BF16)], [2 * D_MODEL])


def _first_epi(du, dh1, x, w_pre):
    def fn(r, ri, fi, ro, ao, first):
        dx, dw = _rms_bwd_vals(ri[2][...], fi[0][...], ri[0][...] + r)
        ro[0][...] = ri[1][...] + dx
        _acc_into(ao[0], dw, first)
    return _Epi(fn, [(du, D_MODEL, 0), (dh1, D_MODEL, 0), (x, D_MODEL, 0)], [w_pre], [(D_MODEL, F32)],
                [D_MODEL])


def _gnorm_epi(y, z, w):
    gw = SSM_INNER // SSM_GROUPS

    def fn(d_all, ri, fi, ro, ao, first):
        zz = ri[1][...]
        yy = ri[0][...]
        sg = _sigmoid(zz)
        sz = zz * sg
        t = yy * sz
        dws = []
        for g in range(d_all.shape[1] // gw):
            sl = slice(g * gw, (g + 1) * gw)
            tg = t[:, sl]
            r = lax.rsqrt(jnp.mean(tg * tg, axis=-1, keepdims=True) + RMS_EPS)
            tn = tg * r
            d = d_all[:, sl]
            gg = d * fi[0][:, sl]
            dt = r * (gg - tn * jnp.mean(gg * tn, axis=-1, keepdims=True))
            ro[0][:, sl] = dt * sz[:, sl]
            ro[1][:, sl] = _b(dt * yy[:, sl] * (sg[:, sl] * (1.0 + zz[:, sl] * (1.0 - sg[:, sl]))))
            dws.append(jnp.sum(d * tn, axis=0, keepdims=True))
        _acc_into(ao[0], jnp.concatenate(dws, axis=1), first)
    return _Epi(fn, [(y, SSM_INNER, 0), (z, SSM_INNER, 0)], [w], [(SSM_INNER, F32), (SSM_INNER, BF16)],
                [SSM_INNER], tiled=True)


def _head_col(stat, h):
    return stat[:, h:h + 1]


def _head_pair_masks(x):
    lane = lax.broadcasted_iota(jnp.int32, x.shape, 1)
    zero = jnp.zeros_like(x)
    return jnp.where(lane < HEAD_DIM, x, zero), jnp.where(lane >= HEAD_DIM, x, zero)


def _attn_fwd(qkv, d, comm=None):
    S = qkv.shape[0]
    blk = ATT_BLOCK
    nblk = S // blk
    nbs = nblk // d
    slopes = _alibi_slopes(N_ATT_HEADS)
    scale = HEAD_DIM ** -0.5
    n_ci = len(comm.ins) if comm else 0
    n_co = len(comm.outs) if comm else 0

    def body(*refs):
        q_ref, kc_ref, kp_ref, vc_ref, vp_ref = refs[:5]
        o_ref, m_ref, l_ref = refs[5 + n_ci:8 + n_ci]
        n = pl.program_id(0)
        if comm:
            c_args = (refs[5:5 + n_ci], refs[8 + n_ci:8 + n_ci + n_co], refs[-2], refs[-1])

            @pl.when(n == 0)
            def _():
                comm.start(*c_args)

        has_prev = (n % nbs) != 0
        ii = lax.broadcasted_iota(jnp.int32, (blk, 2 * blk), 0)
        jj = lax.broadcasted_iota(jnp.int32, (blk, 2 * blk), 1)
        dist_i = blk + ii - jj
        dist = dist_i.astype(F32)
        ok = jnp.logical_and(jnp.logical_and(dist_i >= 0, dist_i <= blk), jnp.logical_or(jj >= blk, has_prev))
        s_scr, p_scr = refs[8 + n_ci + n_co], refs[9 + n_ci + n_co]
        lane = lax.broadcasted_iota(jnp.int32, (blk, LANES), 1)
        for pr in range(N_ATT_HEADS // 2):
            sl = slice(pr * LANES, (pr + 1) * LANES)
            kcat = jnp.concatenate([kp_ref[:, sl], kc_ref[:, sl]], axis=0)
            for h, qh in zip((2 * pr, 2 * pr + 1), _head_pair_masks(q_ref[:, sl])):
                s_scr[h] = _dot_nt(qh, kcat)
        m_all = jnp.zeros((blk, LANES), F32)
        l_all = jnp.zeros((blk, LANES), F32)
        for h in range(N_ATT_HEADS):
            s = jnp.where(ok, s_scr[h] * scale - (slopes[h] * float(d)) * dist, NEG_BIG)
            m = jnp.max(s, axis=-1, keepdims=True)
            p = jnp.exp(s - m)
            l = jnp.sum(p, axis=-1, keepdims=True)
            m_all = jnp.where(lane == h, m, m_all)
            l_all = jnp.where(lane == h, l, l_all)
            p_scr[:, h * 2 * blk:(h + 1) * 2 * blk] = _b(p)
        for pr in range(N_ATT_HEADS // 2):
            sl = slice(pr * LANES, (pr + 1) * LANES)
            vmask = jnp.concatenate(
                _head_pair_masks(jnp.concatenate([vp_ref[:, sl], vc_ref[:, sl]], axis=0)), axis=0)
            o_ref[:, sl] = _dot(p_scr[:, pr * 4 * blk:(pr + 1) * 4 * blk], vmask)
        m_ref[...] = m_all
        l_ref[...] = l_all
        if comm:
            @pl.when(n == nblk - 1)
            def _():
                comm.finish(*c_args)

    cur = lambda c: pl.BlockSpec((blk, ATT_WIDTH), lambda n: (n, c))
    prev = lambda c: pl.BlockSpec((blk, ATT_WIDTH), lambda n: (jnp.maximum(n - 1, 0), c))
    stat = pl.BlockSpec((blk, LANES), lambda n: (n, 0))
    scratch = [pltpu.VMEM((N_ATT_HEADS, blk, 2 * blk), F32), pltpu.VMEM((blk, N_ATT_HEADS * 2 * blk), BF16)]
    if comm:
        scratch += [pltpu.SemaphoreType.DMA((comm.n_sems,))] * 2
        params = pltpu.CompilerParams(dimension_semantics=("arbitrary",), vmem_limit_bytes=VMEM_LIMIT,
                                      has_side_effects=True)
    else:
        params = _params(("parallel",))
    outs = pl.pallas_call(
        body, name=f"attn_fwd_d{d}", grid=(nblk,),
        in_specs=[cur(0), cur(1), prev(1), cur(2), prev(2)] + [ANY] * n_ci,
        out_specs=[cur(0), stat, stat] + [ANY] * n_co,
        out_shape=[jax.ShapeDtypeStruct((S, ATT_WIDTH), F32), jax.ShapeDtypeStruct((S, LANES), F32),
                   jax.ShapeDtypeStruct((S, LANES), F32)] + list(comm.outs if comm else []),
        scratch_shapes=scratch,
        compiler_params=params,
    )(qkv, qkv, qkv, qkv, qkv, *(comm.ins if comm else []))
    return (outs[0], outs[1], outs[2], outs[3:]) if comm else outs


def _attn_bwd(qkv, do, lse, delta, d, comm=None):
    S = qkv.shape[0]
    blk = ATT_BLOCK
    nblk = S // blk
    nbs = nblk // d
    slopes = _alibi_slopes(N_ATT_HEADS)
    scale = HEAD_DIM ** -0.5
    n_ci = len(comm.ins) if comm else 0
    n_co = len(comm.outs) if comm else 0

    def body(*refs):
        qc_ref, qn_ref, k_ref, v_ref, doc_ref, don_ref, lc_ref, ln_ref, dc_ref, dn_ref = refs[:10]
        dq_ref, dk_ref, dv_ref = refs[10 + n_ci:13 + n_ci]
        carry_ref = refs[13 + n_ci + n_co]
        n = pl.program_id(0)
        has_next = ((n + 1) % nbs) != 0
        if comm:
            c_args = (refs[10:10 + n_ci], refs[13 + n_ci:13 + n_ci + n_co], refs[-2], refs[-1])

        @pl.when(n == 0)
        def _():
            carry_ref[...] = jnp.zeros_like(carry_ref)
            if comm:
                comm.start(*c_args)

        rr = lax.broadcasted_iota(jnp.int32, (2 * blk, blk), 0)
        jj = lax.broadcasted_iota(jnp.int32, (2 * blk, blk), 1)
        dist_i = rr - jj
        dist = dist_i.astype(F32)
        ok = jnp.logical_or(jnp.logical_and(rr < blk, dist_i >= 0),
                            jnp.logical_and(jnp.logical_and(rr >= blk, dist_i <= blk), has_next))
        s_scr, dp_scr, p_rows, ds_rows, ds_cols = refs[14 + n_ci + n_co:19 + n_ci + n_co]
        lcat = jnp.concatenate([lc_ref[...], ln_ref[...]], axis=0)
        dcat = jnp.concatenate([dc_ref[...], dn_ref[...]], axis=0)
        rows2 = 2 * blk

        def operands(pr):
            sl = slice(pr * LANES, (pr + 1) * LANES)
            qm = _head_pair_masks(jnp.concatenate([qc_ref[:, sl], qn_ref[:, sl]], axis=0))
            dom = _head_pair_masks(jnp.concatenate([doc_ref[:, sl], don_ref[:, sl]], axis=0))
            return sl, qm, dom

        for pr in range(N_ATT_HEADS // 2):
            sl, qm, dom = operands(pr)
            for h, qh, doh in zip((2 * pr, 2 * pr + 1), qm, dom):
                s_scr[h] = _dot_nt(qh, k_ref[:, sl])
                dp_scr[h] = _dot_nt(doh, v_ref[:, sl])
        for h in range(N_ATT_HEADS):
            s = jnp.where(ok, s_scr[h] * scale - (slopes[h] * float(d)) * dist - lcat[:, h:h + 1], NEG_BIG)
            p = jnp.exp(s)
            dsb = _b(p * (dp_scr[h] - dcat[:, h:h + 1]) * scale)
            p_rows[h * rows2:(h + 1) * rows2, :] = _b(p)
            ds_rows[h * rows2:(h + 1) * rows2, :] = dsb
            ds_cols[:, h * blk:(h + 1) * blk] = dsb
        for pr in range(N_ATT_HEADS // 2):
            sl, qm, dom = operands(pr)
            pair_rows = slice(pr * 2 * rows2, (pr + 1) * 2 * rows2)
            dv_ref[:, sl] = _b(_dot_tn(p_rows[pair_rows, :], jnp.concatenate(dom, axis=0)))
            dk_ref[:, sl] = _b(_dot_tn(ds_rows[pair_rows, :], jnp.concatenate(qm, axis=0)))
            dq = _dot(ds_cols[:, pr * 2 * blk:(pr + 1) * 2 * blk],
                      jnp.concatenate(_head_pair_masks(k_ref[:, sl]), axis=0))
            dq_ref[:, sl] = _b(dq[:blk] + carry_ref[:, sl])
            carry_ref[:, sl] = dq[blk:]

        if comm:
            @pl.when(n == nblk - 1)
            def _():
                comm.finish(*c_args)

    cur = lambda c: pl.BlockSpec((blk, ATT_WIDTH), lambda n: (n, c))
    nxt = lambda c: pl.BlockSpec((blk, ATT_WIDTH), lambda n: (jnp.minimum(n + 1, nblk - 1), c))
    scur = pl.BlockSpec((blk, LANES), lambda n: (n, 0))
    snxt = pl.BlockSpec((blk, LANES), lambda n: (jnp.minimum(n + 1, nblk - 1), 0))
    shp = jax.ShapeDtypeStruct((S, ATT_WIDTH), BF16)
    scratch = [pltpu.VMEM((blk, ATT_WIDTH), F32),
               pltpu.VMEM((N_ATT_HEADS, 2 * blk, blk), F32), pltpu.VMEM((N_ATT_HEADS, 2 * blk, blk), F32),
               pltpu.VMEM((N_ATT_HEADS * 2 * blk, blk), BF16), pltpu.VMEM((N_ATT_HEADS * 2 * blk, blk), BF16),
               pltpu.VMEM((2 * blk, N_ATT_HEADS * blk), BF16)]
    if comm:
        scratch += [pltpu.SemaphoreType.DMA((comm.n_sems,))] * 2
        params = pltpu.CompilerParams(dimension_semantics=("arbitrary",), vmem_limit_bytes=VMEM_LIMIT,
                                      has_side_effects=True)
    else:
        params = _params(("arbitrary",))
    outs = pl.pallas_call(
        body, name=f"attn_bwd_d{d}", grid=(nblk,),
        in_specs=[cur(0), nxt(0), cur(1), cur(2), cur(0), nxt(0), scur, snxt, scur, snxt] + [ANY] * n_ci,
        out_specs=[cur(0), cur(0), cur(0)] + [ANY] * n_co,
        out_shape=[shp, shp, shp] + list(comm.outs if comm else []),
        scratch_shapes=scratch,
        compiler_params=params,
    )(qkv, qkv, qkv, qkv, do, do, lse, lse, delta, delta, *(comm.ins if comm else []))
    return (outs[0], outs[1], outs[2], outs[3:]) if comm else outs


LAYOUT_TILE = 512
DILATED = tuple(d for d in DILATIONS if d > 1)


def _pat_spec(d, cols, col_block=0):
    return pl.BlockSpec((d, LAYOUT_TILE // d, cols), lambda i: (0, i, col_block))


def _pat_view(a, d):
    return a.reshape(d, a.shape[0] // d, a.shape[1])


def _to_slabs(slab_ref, src_ref):
    for cb in range(slab_ref.shape[0]):
        slab_ref[cb] = src_ref[:, cb * LANES:(cb + 1) * LANES].astype(F32)


def _gather_pattern(dst_ref, slab_ref, d, dtype):
    t = slab_ref.shape[1]
    for cb in range(slab_ref.shape[0]):
        one = slab_ref.at[cb]
        for r in range(d):
            dst_ref[r, :, cb * LANES:(cb + 1) * LANES] = one[pl.ds(r, t // d, stride=d), :].astype(dtype)


def _scatter_pattern(slab_ref, src_ref, d, add=False):
    t = slab_ref.shape[1]
    for cb in range(slab_ref.shape[0]):
        one = slab_ref.at[cb]
        for r in range(d):
            idx = pl.ds(r, t // d, stride=d)
            val = src_ref[r, :, cb * LANES:(cb + 1) * LANES]
            if add:
                val = val + one[idx, :]
            one[idx, :] = val


def _pat_out(d, S, cols, dtype, col_tiled):
    def spec(tm, tn):
        if col_tiled:
            return pl.BlockSpec((d, tm // d, tn), lambda j, i, k: (0, i, j))
        return pl.BlockSpec((d, tm // d, cols), lambda j, i, k: (0, i, 0))
    return jax.ShapeDtypeStruct((d, S // d, cols), dtype), spec


def _qkv_epi(S):
    def fn(r, ri, fi, ro, ao, first, scr):
        ro[0][...] = _b(r)
        slab = scr[0]
        for cb in range(slab.shape[0]):
            slab[cb] = r[:, cb * LANES:(cb + 1) * LANES]
        for d, p_ref in zip(DILATED, ro[1:]):
            _gather_pattern(p_ref, slab, d, BF16)
    return _Epi(fn, row_outs=[(3 * ATT_WIDTH, BF16)] + [_pat_out(d, S, 3 * ATT_WIDTH, BF16, True) for d in DILATED],
                tiled=True, scratch=[pltpu.VMEM((ATT_WIDTH // LANES, LAYOUT_TILE, LANES), F32)])


def _attn_combine(os, ms, ls):
    S = os[0].shape[0]
    t = LAYOUT_TILE

    def body(o1, o2, o3, m1, m2, m3, l1, l2, l3, att_ref, lse_ref, so2, so3, sm2, sm3, sl2, sl3):
        for d, src, dst in ((DILATED[0], o2, so2), (DILATED[1], o3, so3), (DILATED[0], m2, sm2),
                            (DILATED[1], m3, sm3), (DILATED[0], l2, sl2), (DILATED[1], l3, sl3)):
            _scatter_pattern(dst, src, d)
        mm = [m1[...], sm2[0], sm3[0]]
        big = jnp.maximum(jnp.maximum(mm[0], mm[1]), mm[2])
        es = [jnp.exp(m - big) for m in mm]
        den = es[0] * l1[...] + es[1] * sl2[0] + es[2] * sl3[0]
        lse_ref[...] = big + jnp.log(den)
        inv = 1.0 / den
        for h in range(N_ATT_HEADS):
            sl = slice(h * HEAD_DIM, (h + 1) * HEAD_DIM)
            cb, hl = divmod(h, 2)
            sll = slice(hl * HEAD_DIM, (hl + 1) * HEAD_DIM)
            num = (_head_col(es[0], h) * o1[:, sl] + _head_col(es[1], h) * so2[cb, :, sll]
                   + _head_col(es[2], h) * so3[cb, :, sll])
            att_ref[:, sl] = num * _head_col(inv, h)

    def specs(c):
        return [pl.BlockSpec((t, c), lambda i: (i, 0))] + [_pat_spec(d, c) for d in DILATED]

    args = [os[0]] + [_pat_view(o, d) for o, d in zip(os[1:], DILATED)]
    args += [ms[0]] + [_pat_view(m, d) for m, d in zip(ms[1:], DILATED)]
    args += [ls[0]] + [_pat_view(l, d) for l, d in zip(ls[1:], DILATED)]
    return pl.pallas_call(
        body, name="attn_combine", grid=(S // t,),
        in_specs=specs(ATT_WIDTH) + specs(LANES) + specs(LANES),
        out_specs=[pl.BlockSpec((t, ATT_WIDTH), lambda i: (i, 0)), pl.BlockSpec((t, LANES), lambda i: (i, 0))],
        out_shape=[jax.ShapeDtypeStruct((S, ATT_WIDTH), F32), jax.ShapeDtypeStruct((S, LANES), F32)],
        scratch_shapes=[pltpu.VMEM((ATT_WIDTH // LANES, t, LANES), F32)] * 2
        + [pltpu.VMEM((1, t, LANES), F32)] * 4,
        compiler_params=_params(("parallel",)),
    )(*args)


def _attn_delta(d_att, att, lse):
    S = d_att.shape[0]
    t = LAYOUT_TILE

    def body(d_ref, a_ref, l_ref, *refs):
        out_refs, d_slab, l_slab, dl_slab = refs[:-3], refs[-3], refs[-2], refs[-1]
        dd = d_ref[...]
        prod = dd * a_ref[...]
        lane = lax.broadcasted_iota(jnp.int32, (t, LANES), 1)
        acc = jnp.zeros((t, LANES), F32)
        for h in range(N_ATT_HEADS):
            s = jnp.sum(prod[:, h * HEAD_DIM:(h + 1) * HEAD_DIM], axis=-1, keepdims=True)
            acc = jnp.where(lane == h, s, acc)
        out_refs[0][...] = _b(dd)
        out_refs[1][...] = acc
        _to_slabs(d_slab, d_ref)
        l_slab[0] = l_ref[...]
        dl_slab[0] = acc
        for k, d in enumerate(DILATED):
            db_ref, ls_ref, dl_ref = out_refs[2 + 3 * k:5 + 3 * k]
            _gather_pattern(db_ref, d_slab, d, BF16)
            _gather_pattern(ls_ref, l_slab, d, F32)
            _gather_pattern(dl_ref, dl_slab, d, F32)

    nat = lambda c: pl.BlockSpec((t, c), lambda i: (i, 0))
    out_specs = [nat(ATT_WIDTH), nat(LANES)]
    out_shape = [jax.ShapeDtypeStruct((S, ATT_WIDTH), BF16), jax.ShapeDtypeStruct((S, LANES), F32)]
    for d in DILATED:
        out_specs += [_pat_spec(d, ATT_WIDTH), _pat_spec(d, LANES), _pat_spec(d, LANES)]
        out_shape += [jax.ShapeDtypeStruct((d, S // d, ATT_WIDTH), BF16),
                      jax.ShapeDtypeStruct((d, S // d, LANES), F32),
                      jax.ShapeDtypeStruct((d, S // d, LANES), F32)]
    outs = pl.pallas_call(
        body, name="attn_delta", grid=(S // t,),
        in_specs=[nat(ATT_WIDTH), nat(ATT_WIDTH), nat(LANES)],
        out_specs=out_specs, out_shape=out_shape,
        scratch_shapes=[pltpu.VMEM((ATT_WIDTH // LANES, t, LANES), F32), pltpu.VMEM((1, t, LANES), F32),
                        pltpu.VMEM((1, t, LANES), F32)],
        compiler_params=_params(("parallel",)),
    )(d_att, att, lse)
    res = [(outs[0], lse, outs[1])]
    for k in range(len(DILATED)):
        db, ls, dl = outs[2 + 3 * k:5 + 3 * k]
        res.append((db.reshape(S, ATT_WIDTH), ls.reshape(S, LANES), dl.reshape(S, LANES)))
    return res


def _sum_qkv(dqs, dks, dvs):
    S = dqs[0].shape[0]
    t = LAYOUT_TILE

    def body(*refs):
        o_ref, scr = refs[-2], refs[-1]
        for part in range(3):
            nat_ref, p_refs = refs[3 * part], refs[3 * part + 1:3 * part + 3]
            _to_slabs(scr, nat_ref)
            for d, p_ref in zip(DILATED, p_refs):
                _scatter_pattern(scr, p_ref, d, add=True)
            for cb in range(ATT_WIDTH // LANES):
                o_ref[:, part * ATT_WIDTH + cb * LANES:part * ATT_WIDTH + (cb + 1) * LANES] = _b(scr[cb])

    in_specs, args = [], []
    for group in (dqs, dks, dvs):
        in_specs += [pl.BlockSpec((t, ATT_WIDTH), lambda i: (i, 0))] + [_pat_spec(d, ATT_WIDTH) for d in DILATED]
        args += [group[0]] + [_pat_view(a, d) for a, d in zip(group[1:], DILATED)]
    return pl.pallas_call(
        body, name="sum_dqkv", grid=(S // t,),
        in_specs=in_specs,
        out_specs=pl.BlockSpec((t, 3 * ATT_WIDTH), lambda i: (i, 0)),
        out_shape=jax.ShapeDtypeStruct((S, 3 * ATT_WIDTH), BF16),
        scratch_shapes=[pltpu.VMEM((ATT_WIDTH // LANES, t, LANES), F32)],
        compiler_params=_params(("parallel",)),
    )(*args)


CONV_COLS = 1024
CONV_ROWS = 512
HALO = 8


def _shift_down(x, k, top_src):
    r8 = lax.broadcasted_iota(jnp.int32, (HALO, x.shape[1]), 0)
    rolled = pltpu.roll(x, k, 0)
    top = jnp.where(r8 < k, pltpu.roll(top_src, k, 0), rolled[0:HALO])
    if x.shape[0] == HALO:
        return top
    return jnp.concatenate([top, rolled[HALO:]], axis=0)


def _shift_up(x, k, bottom_src):
    n = x.shape[0]
    r8 = lax.broadcasted_iota(jnp.int32, (HALO, x.shape[1]), 0)
    rolled = pltpu.roll(x, n - k, 0)
    bottom = jnp.where(r8 >= HALO - k, pltpu.roll(bottom_src, HALO - k, 0), rolled[n - HALO:n])
    return jnp.concatenate([rolled[:n - HALO], bottom], axis=0)


def _conv_pre(x, top_src, w_ref, b_ref):
    shifted = [x] + [_shift_down(x, k, top_src) for k in range(1, SSM_CONV)]
    pre = b_ref[...] + w_ref[SSM_CONV - 1:SSM_CONV, :] * x
    for k in range(1, SSM_CONV):
        pre = pre + w_ref[SSM_CONV - 1 - k:SSM_CONV - k, :] * shifted[k]
    return pre, shifted


def _conv_fwd(xbc, conv_w, conv_b):
    S, C = xbc.shape
    bs, bc = CONV_ROWS, CONV_COLS
    nr = S // bs

    def body(x_ref, halo_ref, w_ref, b_ref, o_ref, pre_ref):
        r = pl.program_id(1)
        halo = jnp.where(r > 0, halo_ref[...], 0.0)
        pre, _ = _conv_pre(x_ref[...], halo, w_ref, b_ref)
        pre_ref[...] = _b(pre)
        o_ref[...] = pre * _sigmoid(pre)

    tile = pl.BlockSpec((bs, bc), lambda c, r: (r, c))
    return pl.pallas_call(
        body, name="conv_fwd", grid=(C // bc, nr),
        in_specs=[tile,
                  pl.BlockSpec((HALO, bc), lambda c, r: (jnp.maximum(r * (bs // HALO) - 1, 0), c)),
                  pl.BlockSpec((SSM_CONV, bc), lambda c, r: (0, c)),
                  pl.BlockSpec((1, bc), lambda c, r: (0, c))],
        out_specs=[tile, tile],
        out_shape=[jax.ShapeDtypeStruct((S, C), F32), jax.ShapeDtypeStruct((S, C), BF16)],
        compiler_params=_params(("parallel", "arbitrary")),
    )(xbc, xbc, conv_w, conv_b)


def _conv_bwd(xbc, pre_all, dact, conv_w):
    S, C = xbc.shape
    bs, bc = CONV_ROWS, CONV_COLS
    nr = S // bs
    hb = bs // HALO
    last_halo = S // HALO - 1

    def dsilu(pre):
        sg = _sigmoid(pre)
        return sg * (1.0 + pre * (1.0 - sg))

    def body(x_ref, p_ref, pn_ref, d_ref, dn_ref, w_ref, dx_ref, dw_ref, db_ref):
        r = pl.program_id(1)
        x = x_ref[...]
        dpre = d_ref[...] * dsilu(p_ref[...].astype(F32))
        dpre_n = jnp.where(r < nr - 1, dn_ref[...], 0.0) * dsilu(pn_ref[...].astype(F32)[0:HALO])
        ups = [dpre] + [_shift_up(dpre, k, dpre_n) for k in range(1, SSM_CONV)]
        dx = w_ref[SSM_CONV - 1:SSM_CONV, :] * dpre
        for k in range(1, SSM_CONV):
            dx = dx + w_ref[SSM_CONV - 1 - k:SSM_CONV - k, :] * ups[k]
        dx_ref[...] = _b(dx)
        parts = [jnp.sum(x * ups[SSM_CONV - 1 - j], axis=0, keepdims=True) for j in range(SSM_CONV)]
        dbp = jnp.sum(dpre, axis=0, keepdims=True)

        @pl.when(r == 0)
        def _():
            for j in range(SSM_CONV):
                dw_ref[j:j + 1, :] = parts[j]
            db_ref[...] = dbp

        @pl.when(r > 0)
        def _():
            for j in range(SSM_CONV):
                dw_ref[j:j + 1, :] += parts[j]
            db_ref[...] += dbp

    tile = pl.BlockSpec((bs, bc), lambda c, r: (r, c))
    nxt = pl.BlockSpec((HALO, bc), lambda c, r: (jnp.minimum((r + 1) * hb, last_halo), c))
    nxt16 = pl.BlockSpec((BF16_ROWS, bc), lambda c, r: (
        jnp.minimum((r + 1) * (bs // BF16_ROWS), S // BF16_ROWS - 1), c))
    return pl.pallas_call(
        body, name="conv_bwd", grid=(C // bc, nr),
        in_specs=[tile, tile, nxt16, tile, nxt, pl.BlockSpec((SSM_CONV, bc), lambda c, r: (0, c))],
        out_specs=[tile,
                   pl.BlockSpec((SSM_CONV, bc), lambda c, r: (0, c)),
                   pl.BlockSpec((1, bc), lambda c, r: (0, c))],
        out_shape=[jax.ShapeDtypeStruct((S, C), BF16), jax.ShapeDtypeStruct((SSM_CONV, C), F32),
                   jax.ShapeDtypeStruct((1, C), F32)],
        compiler_params=_params(("parallel", "arbitrary")),
    )(xbc, pre_all, pre_all, dact, dact, conv_w)


def _softplus(x):
    return jnp.maximum(x, 0.0) + jnp.log(1.0 + jnp.exp(-jnp.abs(x)))


GROUP_W = HEADS_PER_GROUP * SSM_HEAD_DIM
B_COL0 = SSM_INNER
C_COL0 = SSM_INNER + SSM_GROUPS * SSM_STATE


def _ssd_prep(dt_raw, dt_bias, a_neg):
    S = dt_raw.shape[0]
    ch = SSM_CHUNK
    nch = S // ch

    def body(dtr_ref, bias_ref, a_ref, dt_ref, acs_ref, acst_ref, sig_ref):
        x = dtr_ref[...] + bias_ref[...]
        lane = lax.broadcasted_iota(jnp.int32, (ch, LANES), 1)
        dt = jnp.where(lane < SSM_HEADS, _softplus(x), 0.0)
        ii = lax.broadcasted_iota(jnp.int32, (ch, ch), 0)
        jj = lax.broadcasted_iota(jnp.int32, (ch, ch), 1)
        acs = _dot_hi(jnp.where(ii >= jj, 1.0, 0.0), dt * a_ref[...])
        dt_ref[...] = dt
        acs_ref[...] = acs
        acst_ref[0] = acs.T[0:SSM_HEADS, :]
        sig_ref[...] = _sigmoid(x)

    blk = pl.BlockSpec((ch, LANES), lambda c: (c, 0))
    small = pl.BlockSpec((1, LANES), lambda c: (0, 0))
    shp = jax.ShapeDtypeStruct((S, LANES), F32)
    return pl.pallas_call(
        body, name="ssd_prep", grid=(nch,),
        in_specs=[blk, small, small],
        out_specs=[blk, blk, pl.BlockSpec((1, SSM_HEADS, ch), lambda c: (c, 0, 0)), blk],
        out_shape=[shp, shp, jax.ShapeDtypeStruct((nch, SSM_HEADS, ch), F32), shp],
        compiler_params=_params(("parallel",)),
    )(dt_raw, dt_bias, a_neg)


def _expand_heads(arr, g, rows):
    lane = lax.broadcasted_iota(jnp.int32, (rows, GROUP_W), 1) // SSM_HEAD_DIM
    h0 = HEADS_PER_GROUP * g
    out = jnp.broadcast_to(arr[:, h0:h0 + 1], (rows, GROUP_W))
    for j in range(1, HEADS_PER_GROUP):
        out = jnp.where(lane == j, arr[:, h0 + j:h0 + j + 1], out)
    return out


def _seg_matrix(k, lanes_per_head, h0):
    r = lax.broadcasted_iota(jnp.int32, (k, LANES), 0)
    c = lax.broadcasted_iota(jnp.int32, (k, LANES), 1)
    return jnp.where(c == h0 + r // lanes_per_head, 1.0, 0.0).astype(BF16)


def _seg_dot(t, e):
    hi = _b(t)
    lo = _b(t - hi.astype(F32))
    return _dot(hi, e) + _dot(lo, e)


def _head_sums(t, e, rows):
    if rows >= 8:
        return _seg_dot(t, e)
    return _seg_dot(jnp.broadcast_to(t, (8, t.shape[1])), e)[0:rows]


def _pair_masks(x):
    lane = lax.broadcasted_iota(jnp.int32, x.shape, 1)
    zero = jnp.zeros_like(x)
    return jnp.where(lane < SSM_HEAD_DIM, x, zero), jnp.where(lane >= SSM_HEAD_DIM, x, zero)


def _ssd_fwd(xact, dt, acs, acst, dsk_e):
    S = xact.shape[0]
    ch = SSM_CHUNK
    nch = S // ch

    def body(x_ref, dt_ref, acs_ref, acst_ref, dsk_ref, y_ref, hs_ref, h_ref):
        c = pl.program_id(0)

        @pl.when(c == 0)
        def _():
            h_ref[...] = jnp.zeros_like(h_ref)

        dt_all = dt_ref[...]
        acs_all = acs_ref[...]
        acst_all = acst_ref[0]
        alast = acs_all[ch - 1:ch, :]
        eacs = jnp.exp(acs_all)
        wd_all = dt_all * jnp.exp(alast - acs_all)
        dtt = dt_all.T
        cd_all = jnp.exp(alast)
        ii = lax.broadcasted_iota(jnp.int32, (ch, ch), 0)
        jj = lax.broadcasted_iota(jnp.int32, (ch, ch), 1)
        low = ii >= jj
        for g in range(SSM_GROUPS):
            xs = x_ref[:, g * GROUP_W:(g + 1) * GROUP_W]
            bb = _b(x_ref[:, B_COL0 + g * SSM_STATE:B_COL0 + (g + 1) * SSM_STATE])
            cc = _b(x_ref[:, C_COL0 + g * SSM_STATE:C_COL0 + (g + 1) * SSM_STATE])
            cb = _dot_nt(cc, bb)
            xsb = _b(xs)
            ht = h_ref[g]
            rest = (_dot(cc, _b(ht)) * _expand_heads(eacs, g, ch)
                    + dsk_ref[:, g * GROUP_W:(g + 1) * GROUP_W] * xs)
            for p in range(HEADS_PER_GROUP // 2):
                lms = []
                for h in (HEADS_PER_GROUP * g + 2 * p, HEADS_PER_GROUP * g + 2 * p + 1):
                    diff = acs_all[:, h:h + 1] - acst_all[h:h + 1, :]
                    lms.append(_b(cb * jnp.exp(jnp.where(low, diff, -jnp.inf)) * dtt[h:h + 1, :]))
                xa, xb = _pair_masks(xsb[:, p * LANES:(p + 1) * LANES])
                yp = _dot(jnp.concatenate(lms, axis=1), jnp.concatenate([xa, xb], axis=0))
                y_ref[:, g * GROUP_W + p * LANES:g * GROUP_W + (p + 1) * LANES] = (
                    yp + rest[:, p * LANES:(p + 1) * LANES])
            hs_ref[0, g] = ht
            st = _dot_tn(bb, _b(xs * _expand_heads(wd_all, g, ch)))
            h_ref[g] = ht * _expand_heads(cd_all, g, 1) + st

    blk = pl.BlockSpec((ch, LANES), lambda c: (c, 0))
    return pl.pallas_call(
        body, name="ssd_fwd", grid=(nch,),
        in_specs=[pl.BlockSpec((ch, CONV_DIM), lambda c: (c, 0)), blk, blk,
                  pl.BlockSpec((1, SSM_HEADS, ch), lambda c: (c, 0, 0)),
                  pl.BlockSpec((1, SSM_INNER), lambda c: (0, 0))],
        out_specs=[pl.BlockSpec((ch, SSM_INNER), lambda c: (c, 0)),
                   pl.BlockSpec((1, SSM_GROUPS, SSM_STATE, GROUP_W), lambda c: (c, 0, 0, 0))],
        out_shape=[jax.ShapeDtypeStruct((S, SSM_INNER), F32),
                   jax.ShapeDtypeStruct((nch, SSM_GROUPS, SSM_STATE, GROUP_W), F32)],
        scratch_shapes=[pltpu.VMEM((SSM_GROUPS, SSM_STATE, GROUP_W), F32)],
        compiler_params=_params(("arbitrary",)),
    )(xact, dt, acs, acst, dsk_e)


def _ssd_bwd(xact, dt, acs, acst, sig, a_neg, dsk_e, hs, dy):
    S = xact.shape[0]
    ch = SSM_CHUNK
    nch = S // ch
    ng, hg = SSM_GROUPS, HEADS_PER_GROUP
    nbc = SSM_GROUPS * SSM_STATE

    def body(x_ref, dt_ref, acs_ref, acst_ref, sig_ref, a_ref, dsk_ref, hs_ref, dy_ref,
             dx_ref, ddt_ref, st_ref,
             dh_ref, rows_ref, e_dt, e_ea, e_dsd, xdb_s, xddb_s, dzb_s, bcb_s, cb_s, zz_s, ww_s, dc1_s, db1_s,
             dhin_s, dlm_s, lmb_s, gm_s, dcbb_s, t_s, dxd_s, prod_s, csum_s):
        step = pl.program_id(0)

        @pl.when(step == 0)
        def _():
            dh_ref[...] = jnp.zeros_like(dh_ref)
            st_ref[...] = jnp.zeros_like(st_ref)
            rows_ref[...] = jnp.zeros_like(rows_ref)

        dt_all = dt_ref[...]
        acs_all = acs_ref[...]
        alast = acs_all[ch - 1:ch, :]
        eacs = jnp.exp(acs_all)
        dsd_all = jnp.exp(alast - acs_all)
        cd_all = jnp.exp(alast)
        ii = lax.broadcasted_iota(jnp.int32, (ch, ch), 0)
        jj = lax.broadcasted_iota(jnp.int32, (ch, ch), 1)
        low = ii >= jj
        gsl = [slice(g * GROUP_W, (g + 1) * GROUP_W) for g in range(ng)]
        psl = [[slice(g * GROUP_W + p * LANES, g * GROUP_W + (p + 1) * LANES) for p in range(hg // 2)]
               for g in range(ng)]
        seg = [_seg_matrix(GROUP_W, SSM_HEAD_DIM, hg * g) for g in range(ng)]

        def bc(g):
            return (bcb_s[:, g * SSM_STATE:(g + 1) * SSM_STATE],
                    bcb_s[:, nbc + g * SSM_STATE:nbc + (g + 1) * SSM_STATE])

        def dy_pair(g, p):
            return _pair_masks(_b(dy_ref[:, psl[g][p]]))

        bcb_s[...] = _b(x_ref[:, B_COL0:])
        for g in range(ng):
            dt_e = _expand_heads(dt_all, g, ch)
            ea_e = _expand_heads(eacs, g, ch)
            dsd_e = _expand_heads(dsd_all, g, ch)
            e_dt[:, gsl[g]] = dt_e
            e_ea[:, gsl[g]] = ea_e
            e_dsd[:, gsl[g]] = dsd_e
            xd = x_ref[:, gsl[g]] * dt_e
            xdb_s[:, gsl[g]] = _b(xd)
            xddb_s[:, gsl[g]] = _b(xd * dsd_e)
            dzb_s[:, gsl[g]] = _b(dy_ref[:, gsl[g]] * ea_e)
        for g in range(ng):
            bb, cc = bc(g)
            htb = _b(hs_ref[0, g])
            dhnb = _b(dh_ref[g])
            cb_s[g] = _dot_nt(cc, bb)
            zz_s[:, gsl[g]] = _dot(cc, htb)
            ww_s[:, gsl[g]] = _dot(bb, dhnb)
            dc1_s[g] = _dot_nt(dzb_s[:, gsl[g]], htb)
            db1_s[g] = _dot_nt(xddb_s[:, gsl[g]], dhnb)
            dhin_s[g] = _dot_tn(cc, dzb_s[:, gsl[g]])
            for p in range(hg // 2):
                xp = xdb_s[:, psl[g][p]]
                for q, dyh in enumerate(dy_pair(g, p)):
                    dlm_s[hg * g + 2 * p + q] = _dot_nt(dyh, xp)
        for g in range(ng):
            cb = cb_s[g]
            dcb = jnp.zeros((ch, ch), F32)
            for j in range(hg):
                h = hg * g + j
                diff = acs_all[:, h:h + 1] - acst_ref[0, h:h + 1, :]
                decay = jnp.exp(jnp.where(low, diff, -jnp.inf))
                lm = cb * decay
                dlm = dlm_s[h]
                gm = dlm * lm
                dcb = dcb + dlm * decay
                rows_ref[h:h + 1, :] = jnp.sum(gm, axis=0, keepdims=True)
                lmb_s[h * ch:(h + 1) * ch, :] = _b(lm)
                gm_s[:, h * ch:(h + 1) * ch] = gm
            dcbb_s[g] = _b(dcb)
            xs = x_ref[:, gsl[g]]
            dyg = dy_ref[:, gsl[g]]
            ww = ww_s[:, gsl[g]]
            dsd_e = e_dsd[:, gsl[g]]
            t2 = ww * (xs * e_dt[:, gsl[g]] * dsd_e)
            t_s[:, gsl[g]] = dyg * zz_s[:, gsl[g]] * e_ea[:, gsl[g]] - t2
            dhn = dh_ref[g]
            csum_s[0:1, gsl[g]] = jnp.sum(t2, axis=0, keepdims=True)
            csum_s[1:2, gsl[g]] = jnp.sum(dhn * hs_ref[0, g], axis=0, keepdims=True)
            csum_s[2:3, gsl[g]] = jnp.sum(dyg * xs, axis=0, keepdims=True)
            dh_ref[g] = dhin_s[g] + dhn * _expand_heads(cd_all, g, 1)
            dxd_s[:, gsl[g]] = ww * dsd_e
        cols = jnp.zeros((ch, LANES), F32)
        for g in range(ng):
            bb, cc = bc(g)
            dcbb = dcbb_s[g]
            dx_ref[:, C_COL0 + g * SSM_STATE:C_COL0 + (g + 1) * SSM_STATE] = dc1_s[g] + _dot(dcbb, bb)
            dx_ref[:, B_COL0 + g * SSM_STATE:B_COL0 + (g + 1) * SSM_STATE] = db1_s[g] + _dot_tn(dcbb, cc)
            cols = cols + _head_sums(t_s[:, gsl[g]], seg[g], ch)
            for p in range(hg // 2):
                h0 = hg * g + 2 * p
                dxd_s[:, psl[g][p]] += _dot_tn(lmb_s[h0 * ch:(h0 + 2) * ch, :],
                                               jnp.concatenate(dy_pair(g, p), axis=0))
                cols = cols + _head_sums(gm_s[:, h0 * ch:(h0 + 2) * ch], _seg_matrix(2 * ch, ch, h0), ch)
        for g in range(ng):
            dxd = dxd_s[:, gsl[g]]
            xs = x_ref[:, gsl[g]]
            dx_ref[:, gsl[g]] = dsk_ref[:, gsl[g]] * dy_ref[:, gsl[g]] + dxd * e_dt[:, gsl[g]]
            prod_s[:, gsl[g]] = dxd * xs
        ddt = jnp.zeros((ch, LANES), F32)
        dal = jnp.zeros((1, LANES), F32)
        ddsk = jnp.zeros((1, LANES), F32)
        for g in range(ng):
            ddt = ddt + _head_sums(prod_s[:, gsl[g]], seg[g], ch)
            dal = (dal + _head_sums(csum_s[0:1, gsl[g]], seg[g], 1)
                   + cd_all * _head_sums(csum_s[1:2, gsl[g]], seg[g], 1))
            ddsk = ddsk + _head_sums(csum_s[2:3, gsl[g]], seg[g], 1)
        rowi = lax.broadcasted_iota(jnp.int32, (ch, 1), 0)
        dacs = cols - rows_ref[...].T + jnp.where(rowi == ch - 1, dal, 0.0)
        dla = _dot_hi(jnp.where(ii <= jj, 1.0, 0.0), dacs)
        a_row = a_ref[...]
        ddt_raw = (ddt + dla * a_row) * sig_ref[...]
        ddt_ref[...] = _b(ddt_raw)
        st_ref[0:1, :] += jnp.sum(dla * dt_all, axis=0, keepdims=True) * a_row
        st_ref[1:2, :] += ddsk
        st_ref[2:3, :] += jnp.sum(ddt_raw, axis=0, keepdims=True)

    rc = lambda s: nch - 1 - s
    blk = pl.BlockSpec((ch, LANES), lambda s: (rc(s), 0))
    wide = lambda dt_: pltpu.VMEM((ch, SSM_INNER), dt_)
    sq = lambda n, dt_: pltpu.VMEM((n, ch, ch), dt_)
    scratch = [pltpu.VMEM((ng, SSM_STATE, GROUP_W), F32), pltpu.VMEM((LANES, ch), F32),
               wide(F32), wide(F32), wide(F32),
               wide(BF16), wide(BF16), wide(BF16), wide(BF16),
               sq(ng, F32), wide(F32), wide(F32), sq(ng, F32), sq(ng, F32),
               pltpu.VMEM((ng, SSM_STATE, GROUP_W), F32),
               sq(SSM_HEADS, F32),
               pltpu.VMEM((SSM_HEADS * ch, ch), BF16),
               pltpu.VMEM((ch, SSM_HEADS * ch), F32),
               sq(ng, BF16), wide(F32), wide(F32), wide(F32),
               pltpu.VMEM((8, SSM_INNER), F32)]
    return pl.pallas_call(
        body, name="ssd_bwd", grid=(nch,),
        in_specs=[pl.BlockSpec((ch, CONV_DIM), lambda s: (rc(s), 0)), blk, blk,
                  pl.BlockSpec((1, SSM_HEADS, ch), lambda s: (rc(s), 0, 0)), blk,
                  pl.BlockSpec((1, LANES), lambda s: (0, 0)),
                  pl.BlockSpec((1, SSM_INNER), lambda s: (0, 0)),
                  pl.BlockSpec((1, SSM_GROUPS, SSM_STATE, GROUP_W), lambda s: (rc(s), 0, 0, 0)),
                  pl.BlockSpec((ch, SSM_INNER), lambda s: (rc(s), 0))],
        out_specs=[pl.BlockSpec((ch, CONV_DIM), lambda s: (rc(s), 0)), blk,
                   pl.BlockSpec((8, LANES), lambda s: (0, 0))],
        out_shape=[jax.ShapeDtypeStruct((S, CONV_DIM), F32), jax.ShapeDtypeStruct((S, LANES), BF16),
                   jax.ShapeDtypeStruct((8, LANES), F32)],
        scratch_shapes=scratch,
        compiler_params=_params(("arbitrary",)),
    )(xact, dt, acs, acst, sig, a_neg, dsk_e, hs, dy)


def _pad_lanes(v, n=LANES):
    return jnp.pad(v, ((0, 0), (0, n - v.shape[1])))


def _local_step(x, target, w, ex=None):
    offs = np.cumsum((0,) + IN_SPLITS)
    wt_in = w["w_in_t"]
    w_qkv = wt_in[offs[0]:offs[3]]
    w_z = wt_in[offs[3]:offs[4]]
    w_xbc = wt_in[offs[4]:offs[5]]
    w_dt = jnp.pad(wt_in[offs[5]:offs[6]], ((0, LANES - SSM_HEADS), (0, 0)))
    w_g = wt_in[offs[6]:offs[7]]
    dt_bias = _pad_lanes(w["dt_bias"])
    a_neg = _pad_lanes(-jnp.exp(w["a_log"]))

    u = _rms_fwd(x, w["norm_mix_pre_w"])
    if ex is None:
        xbc = _mm_nn(u, w_xbc, F32, "proj_xbc", tb=True)
    else:
        xbc, got = _mm_nn(u, w_xbc, F32, "proj_xbc", comm=_gather_comm([ex.mine[REST_EARLY]]), tb=True)
        w = {**w, **ex.rest_weights(got[0], REST_EARLY)}
    n_tok = x.shape[0]
    qkv_outs, _ = _mm_epi(u, w_qkv, _qkv_epi(n_tok), "proj_qkv", tb=True, tn=ATT_WIDTH)
    z = _mm_nn(u, w_z, F32, "proj_z", tb=True)
    dt_raw = _mm_nn(u, w_dt, F32, "proj_dt", tb=True)
    gl = _mm_nn(u, w_g, F32, "proj_gate", tb=True)

    pats = [qkv_outs[0]] + [o.reshape(n_tok, 3 * ATT_WIDTH) for o in qkv_outs[1:]]
    os_, ms_, ls_ = [], [], []
    for i, (d, qkv_p) in enumerate(zip(DILATIONS, pats)):
        if ex is not None and i < len(REST_LATE):
            o, m, l, got = _attn_fwd(qkv_p, d, comm=_gather_comm([ex.mine[REST_LATE[i]]]))
            w = {**w, **ex.rest_weights(got[0], REST_LATE[i])}
        else:
            o, m, l = _attn_fwd(qkv_p, d)
        os_.append(o)
        ms_.append(m)
        ls_.append(l)
    att, lse = _attn_combine(os_, ms_, ls_)
    att_o = _mm_nn(att, w["w_att_proj"], F32, "att_proj")

    xact, conv_pre = _conv_fwd(xbc, w["conv_w"], w["conv_b"])
    dsk_e = jnp.repeat(w["d_skip"], SSM_HEAD_DIM, axis=1)
    dt, acs, acst, sig = _ssd_prep(dt_raw, dt_bias, a_neg)
    y_ssd, hs = _ssd_fwd(xact, dt, acs, acst, dsk_e)
    (ssm_y, ssm_o), _ = _mm_epi(None, w["w_ssm_proj"], _ssm_out_epi(y_ssd, z, w["ssm_norm_w"]), "ssm_proj")

    (mi, mixed, h1, f), _ = _mm_epi(None, w["w_out"], _mix_out_epi(
        att_o, ssm_o, gl, x, w["b_gate"], w["norm_mix_post_w"], w["norm_ffn_pre_w"]), "out_proj")
    r_up, act = _mm_nn(f, w["w_up"], BF16, "ffn_up", mode="relu2")
    (dh2, d_down, loss, g_ffn_post), _ = _mm_epi(
        act, w["w_down"], _final_epi(h1, target, w["norm_ffn_post_w"]), "ffn_down")

    g = {"norm_ffn_post_w": g_ffn_post}
    g["w_down"] = _mm_tn(act, d_down, "dw_down")
    dup = _mm_nn(d_down, w["w_down"], BF16, "d_act", mode="mul2", extra=r_up, tb=True)
    g["w_up"] = _mm_tn(f, dup, "dw_up")
    (dh1, d_mixed, g["norm_ffn_pre_w"], g["norm_mix_post_w"]), _ = _mm_epi(
        dup, w["w_up"], _mid_epi(dh2, h1, mixed, w["norm_ffn_pre_w"], w["norm_mix_post_w"]), "d_f", tb=True)
    g["w_out"] = _mm_tn(mi, d_mixed, "dw_out")
    (d_att_o, d_ssm_o, dgl, g["b_gate"]), _ = _mm_epi(
        d_mixed, w["w_out"], _gate_epi(att_o, ssm_o, gl, w["b_gate"]), "d_mi", tb=True)

    g["w_att_proj"] = _mm_tn(att, d_att_o, "dw_att_proj")
    g["w_ssm_proj"] = _mm_tn(ssm_y, d_ssm_o, "dw_ssm_proj")
    gn_epi = _gnorm_epi(y_ssd, z, w["ssm_norm_w"])
    if ex is None:
        (dy_ssd, dz, g["ssm_norm_w"]), _ = _mm_epi(d_ssm_o, w["w_ssm_proj"], gn_epi, "d_ssm_y", tb=True,
                                                    tn=PACK_COLS)
    else:
        gs_rest = jnp.concatenate(
            [_shards_from_full(n, g[n]).reshape(N_CHIPS, -1, PACK_COLS) for n in REST], axis=1)
        (dy_ssd, dz, g["ssm_norm_w"]), recv = _mm_epi(d_ssm_o, w["w_ssm_proj"], gn_epi, "d_ssm_y", tb=True,
                                                       tn=PACK_COLS, tm=1024, comm=_pair_comm([gs_rest]))
        p_rest = _pair_add(gs_rest, recv[0], ex.c_arr, "rs_pair_add_rest")

    d_att = _mm_nn(d_att_o, w["w_att_proj"], F32, "d_att", tb=True)
    bwd_ins = _attn_delta(d_att, att, lse)
    dqs, dks, dvs = [], [], []
    for d, qkv_p, (do_p, lse_p, delta_p) in zip(DILATIONS, pats, bwd_ins):
        if ex is not None and d == DILATIONS[0]:
            dq, dk, dv, recv3 = _attn_bwd(qkv_p, do_p, lse_p, delta_p, d, comm=_chip_comm([p_rest]))
            q_rest = _chip_add(p_rest, recv3[0], ex.chip_arr, "rs_chip_add_rest")
            ex.finish_reduce("rest", q_rest, _comm_call("rs_share_rest", _share_comm([q_rest]))[0])
        else:
            dq, dk, dv = _attn_bwd(qkv_p, do_p, lse_p, delta_p, d)
        dqs.append(dq)
        dks.append(dk)
        dvs.append(dv)
    dqkv = _sum_qkv(dqs, dks, dvs)

    dxact, ddt_raw, stats = _ssd_bwd(xact, dt, acs, acst, sig, a_neg, dsk_e, hs, dy_ssd)
    g["a_log"] = stats[0:1, :SSM_HEADS]
    g["d_skip"] = stats[1:2, :SSM_HEADS]
    g["dt_bias"] = stats[2:3, :SSM_HEADS]
    dxbc, g["conv_w"], g["conv_b"] = _conv_bwd(xbc, conv_pre, dxact, w["conv_w"])

    pieces = [(dqkv, w_qkv), (dz, w_z), (dxbc, w_xbc), (ddt_raw, w_dt), (dgl, w_g)]
    gw = [_mm_tn(dp, u, f"dw_in_{i}") for i, (dp, _) in enumerate(pieces)]
    gw[3] = gw[3][:SSM_HEADS]
    if ex is None:
        g["w_in_t"] = jnp.concatenate(gw, axis=0)
    du = None
    for i, (dp, wp) in enumerate([pieces[k] for k in (1, 2, 0, 3, 4)]):
        if ex is not None and i == 0:
            gs_in = _rows_to_shards(gw, IN_SHARD_ROWS, IN_SHARD_PAD)
            du, recv = _mm_nn(dp, wp, F32, f"d_u_{i}", acc=du, comm=_pair_comm([gs_in]))
            p_in = _pair_add(gs_in, recv[0], ex.c_arr, "rs_pair_add_in")
            rows = p_in.shape[1] // 2
            p_parts = [p_in[:, :rows], p_in[:, rows:]]
            q_parts = []
        elif ex is not None and i in (1, 2):
            p_part = p_parts[i - 1]
            du, recv3 = _mm_nn(dp, wp, F32, f"d_u_{i}", acc=du, comm=_chip_comm([p_part]))
            q_parts.append(_chip_add(p_part, recv3[0], ex.chip_arr, f"rs_chip_add_in_{i}"))
            if i == 2:
                others = _comm_call("rs_share_in", _share_comm(q_parts))
                ex.finish_reduce("w_in", jnp.concatenate(q_parts, axis=0), jnp.concatenate(others, axis=0))
        elif i == len(pieces) - 1:
            (grad_x, g["norm_mix_pre_w"]), _ = _mm_epi(
                dp, wp, _first_epi(du, dh1, x, w["norm_mix_pre_w"]), f"d_u_{i}")
        else:
            du = _mm_nn(dp, wp, F32, f"d_u_{i}", acc=du)
    return loss, grad_x, g


def _rows_to_shards(pieces, shard_rows, pad_rows):
    cols = pieces[0].shape[1]
    shards = []
    for s in range(N_CHIPS):
        lo, hi = s * shard_rows, (s + 1) * shard_rows
        parts, r0 = [], 0
        for p in pieces:
            a, b = max(lo, r0), min(hi, r0 + p.shape[0])
            if a < b:
                parts.append(p[a - r0:b - r0])
            r0 += p.shape[0]
        parts.append(jnp.zeros((pad_rows - shard_rows, cols), pieces[0].dtype))
        shards.append(jnp.concatenate(parts, axis=0))
    return jnp.stack(shards)


BIG = ("w_in", "w_att_proj", "w_ssm_proj", "w_out", "w_up", "w_down")
BIG_FULL_SHAPES = {"w_in": (D_MODEL, IN_PROJ_WIDTH), "w_att_proj": (ATT_WIDTH, D_MODEL),
                   "w_ssm_proj": (SSM_INNER, D_MODEL), "w_out": (D_MODEL, D_MODEL),
                   "w_up": (D_MODEL, FFN_HIDDEN), "w_down": (FFN_HIDDEN, D_MODEL)}
BIG_COL_SHARDED = {"w_in": True, "w_att_proj": True, "w_ssm_proj": False, "w_out": False, "w_up": True,
                   "w_down": False}
PACK_COLS = 1024
SMALL = ("norm_mix_pre_w", "b_gate", "conv_b", "dt_bias", "a_log", "d_skip", "ssm_norm_w",
         "norm_mix_post_w", "norm_ffn_pre_w", "norm_ffn_post_w")
SMALL_ROWS = 232


def _shard_shape(name):
    r, c = BIG_FULL_SHAPES[name]
    return (r, c // N_CHIPS) if BIG_COL_SHARDED[name] else (r // N_CHIPS, c)


def _mesh_pos():
    return lax.axis_index("x"), lax.axis_index("y"), lax.axis_index("c")


def _other_chips(x, y):
    return [(1 - x, y), (x, 1 - y), (1 - x, 1 - y)]


ANY = pl.BlockSpec(memory_space=pl.ANY)


REST_EARLY = ("w_att_proj", "w_ssm_proj", "w_out")
REST_LATE = (("w_up",), ("w_down",))
REST = REST_EARLY + REST_LATE[0] + REST_LATE[1]
ADD_ROWS_CAP = 800
BF16_ROWS = 16
IN_SHARD_ROWS = IN_PROJ_WIDTH // N_CHIPS
IN_SHARD_PAD = 2688


def _stack_rest(shards, dtype, names=REST):
    return jnp.concatenate([shards[n].astype(dtype).reshape(-1, PACK_COLS) for n in names], axis=0)


def _unstack_rest(stacked, lead=(), names=REST):
    out, r0 = {}, 0
    for n in names:
        shp = _shard_shape(n)
        rows = shp[0] * shp[1] // PACK_COLS
        out[n] = stacked[..., r0:r0 + rows, :].reshape(lead + shp)
        r0 += rows
    return out


def _full_from_shards(name, sh):
    if BIG_COL_SHARDED[name]:
        return sh.transpose(1, 0, 2).reshape(BIG_FULL_SHAPES[name])
    return sh.reshape(BIG_FULL_SHAPES[name])


def _shards_from_full(name, full):
    shp = _shard_shape(name)
    if BIG_COL_SHARDED[name]:
        return full.reshape(shp[0], N_CHIPS, shp[1]).transpose(1, 0, 2)
    return full.reshape((N_CHIPS,) + shp)


def _pair_add(g, recv, c_idx, name):
    _, half, cols = recv.shape
    rb = _row_block(half, ADD_ROWS_CAP, BF16_ROWS)
    nb = half // rb

    def body(c_ref, g_ref, r_ref, o_ref):
        o_ref[...] = _b(g_ref[...] + r_ref[...])

    blk = (1, rb, cols)
    return pl.pallas_call(
        body, name=name,
        grid_spec=pltpu.PrefetchScalarGridSpec(
            num_scalar_prefetch=1, grid=(N_CHIPS, nb),
            in_specs=[pl.BlockSpec(blk, lambda s, i, c: (s, c[0] * nb + i, 0)),
                      pl.BlockSpec(blk, lambda s, i, c: (s, i, 0))],
            out_specs=pl.BlockSpec(blk, lambda s, i, c: (s, i, 0))),
        out_shape=jax.ShapeDtypeStruct(recv.shape, BF16),
        compiler_params=_params(("arbitrary", "arbitrary")),
    )(c_idx, g, recv)


def _chip_add(p, recv, me_idx, name):
    _, half, cols = recv.shape
    rb = _row_block(half, ADD_ROWS_CAP, BF16_ROWS)

    def body(m_ref, p_ref, r0_ref, r1_ref, r2_ref, o_ref):
        o_ref[...] = ((p_ref[0].astype(F32) + r0_ref[0].astype(F32)) + r1_ref[0].astype(F32)) + r2_ref[0].astype(F32)

    blk = (1, rb, cols)
    return pl.pallas_call(
        body, name=name,
        grid_spec=pltpu.PrefetchScalarGridSpec(
            num_scalar_prefetch=1, grid=(half // rb,),
            in_specs=[pl.BlockSpec(blk, lambda i, m: (m[0], i, 0)),
                      pl.BlockSpec(blk, lambda i, m: (0, i, 0)),
                      pl.BlockSpec(blk, lambda i, m: (1, i, 0)),
                      pl.BlockSpec(blk, lambda i, m: (2, i, 0))],
            out_specs=pl.BlockSpec((rb, cols), lambda i, m: (i, 0))),
        out_shape=jax.ShapeDtypeStruct((half, cols), F32),
        compiler_params=_params(("arbitrary",)),
    )(me_idx, p, recv, recv, recv)


def _gather_plan():
    def copies(w_refs, out_refs, send_sems, recv_sems):
        x, y, c = _mesh_pos()
        me = 2 * x + y
        sibling = (x, y, 1 - c)
        chips = _other_chips(x, y)
        idx = [2 * chip[0] + chip[1] for chip in chips]
        plans = []
        for a, (w_ref, out_ref) in enumerate(zip(w_refs, out_refs)):
            half = w_ref.shape[0] // 2

            def copy(k, chip, h, to, src=None, out_ref=out_ref, half=half, a=a):
                rows = out_ref.at[chip, pl.ds(h * half, half), :]
                return pltpu.make_async_remote_copy(
                    src_ref=rows if src is None else src, dst_ref=rows,
                    send_sem=send_sems.at[6 * a + k], recv_sem=recv_sems.at[6 * a + k],
                    device_id=to, device_id_type=MESH)

            mine_half = w_ref.at[pl.ds(c * half, half), :]
            send = [copy(j, me, c, (*chip, c), src=mine_half) for j, chip in enumerate(chips)]
            land = [copy(j, idx[j], c, (x, y, c)) for j in range(N_CHIPS - 1)]
            forward = [copy(3 + j, idx[j], c, sibling) for j in range(N_CHIPS - 1)]
            land_fw = [copy(3 + j, idx[j], 1 - c, (x, y, c)) for j in range(N_CHIPS - 1)]
            plans.append((send, land, forward, land_fw))
        return plans

    def start(*refs):
        for send, _, _, _ in copies(*refs):
            for cp in send:
                cp.start()

    def finish(*refs):
        plans = copies(*refs)
        for _, land, forward, _ in plans:
            for j in range(N_CHIPS - 1):
                land[j].wait_recv()
                forward[j].start()
        for _, _, _, land_fw in plans:
            for cp in land_fw:
                cp.wait_recv()
        for send, _, forward, _ in plans:
            for cp in send + forward:
                cp.wait_send()

    return start, finish


def _pair_plan(halves):
    def copies(in_refs, out_refs, send_sems, recv_sems):
        x, y, c = _mesh_pos()
        cps = []
        for a, (g_ref, out_ref) in enumerate(zip(in_refs, out_refs)):
            if halves:
                half = g_ref.shape[1] // 2
                src = g_ref.at[:, pl.ds((1 - c) * half, half), :]
            else:
                src = g_ref
            cps.append(pltpu.make_async_remote_copy(
                src_ref=src, dst_ref=out_ref, send_sem=send_sems.at[a], recv_sem=recv_sems.at[a],
                device_id=(x, y, 1 - c), device_id_type=MESH))
        return cps

    def start(*refs):
        for cp in copies(*refs):
            cp.start()

    def finish(*refs):
        for cp in copies(*refs):
            cp.wait()

    return start, finish


def _chip_plan():
    def copies(in_refs, out_refs, send_sems, recv_sems):
        x, y, c = _mesh_pos()
        chips = _other_chips(x, y)
        return [pltpu.make_async_remote_copy(
            src_ref=p_ref.at[2 * chip[0] + chip[1]], dst_ref=out_ref.at[j],
            send_sem=send_sems.at[3 * a + j], recv_sem=recv_sems.at[3 * a + j], device_id=(*chip, c),
            device_id_type=MESH)
            for a, (p_ref, out_ref) in enumerate(zip(in_refs, out_refs)) for j, chip in enumerate(chips)]

    def start(*refs):
        for cp in copies(*refs):
            cp.start()

    def finish(*refs):
        cps = copies(*refs)
        for cp in cps:
            cp.wait_recv()
        for cp in cps:
            cp.wait_send()

    return start, finish


def _gather_comm(shards):
    return _Comm(_gather_plan(), shards, [jax.ShapeDtypeStruct((N_CHIPS,) + s.shape, s.dtype) for s in shards],
                 6 * len(shards))


def _pair_comm(gs):
    return _Comm(_pair_plan(True), gs,
                 [jax.ShapeDtypeStruct((N_CHIPS, g.shape[1] // 2, g.shape[2]), g.dtype) for g in gs], len(gs))


def _chip_comm(ps):
    return _Comm(_chip_plan(), ps, [jax.ShapeDtypeStruct((N_CHIPS - 1,) + p.shape[1:], p.dtype) for p in ps],
                 3 * len(ps))


def _share_comm(qs):
    return _Comm(_pair_plan(False), qs, [jax.ShapeDtypeStruct(q.shape, q.dtype) for q in qs], len(qs))


def _comm_call(name, comm):
    n, m = len(comm.ins), len(comm.outs)

    def body(*refs):
        args = (refs[:n], refs[n:n + m], refs[n + m], refs[n + m + 1])
        comm.start(*args)
        comm.finish(*args)

    return pl.pallas_call(
        body, name=name, out_shape=comm.outs, in_specs=[ANY] * n, out_specs=[ANY] * m,
        scratch_shapes=[pltpu.SemaphoreType.DMA((comm.n_sems,))] * 2,
        compiler_params=pltpu.CompilerParams(has_side_effects=True),
    )(*comm.ins)


class _Exchange:
    def __init__(self, chip, ci, mine):
        self.chip, self.ci = chip, ci
        self.mine = mine
        self.c_arr = ci.reshape(1).astype(jnp.int32)
        self.chip_arr = chip.reshape(1).astype(jnp.int32)
        self.reduced = {}

    def rest_weights(self, got, names):
        stacks = lax.dynamic_update_slice(got, self.mine[names][None], (self.chip, 0, 0))
        return {n: _full_from_shards(n, sh) for n, sh in _unstack_rest(stacks, (N_CHIPS,), names).items()}

    def finish_reduce(self, key, mine, other):
        south = self.ci == 0
        self.reduced[key] = jnp.concatenate([jnp.where(south, mine, other), jnp.where(south, other, mine)],
                                            axis=0)


def _allreduce_small(part, name):
    rows = part.shape[0]

    def body(p_ref, out_ref, buf, send_sems, recv_sems, local_sem):
        x, y, c = _mesh_pos()
        me, sibling = (x, y, c), (x, y, 1 - c)
        chips = _other_chips(x, y)

        def slot(px, py, pc):
            return buf.at[pl.ds((4 * px + 2 * py + pc) * rows, rows), :]

        def copy(k, block, to, src=None):
            return pltpu.make_async_remote_copy(
                src_ref=slot(*block) if src is None else src, dst_ref=slot(*block),
                send_sem=send_sems.at[k], recv_sem=recv_sems.at[k], device_id=to, device_id_type=MESH)

        mine = pltpu.make_async_copy(p_ref, slot(*me), local_sem)
        mine.start()
        first = [copy(0, me, sibling, src=p_ref)]
        first += [copy(1 + j, me, (*chip, c), src=p_ref) for j, chip in enumerate(chips)]
        for cp in first:
            cp.start()
        passed = [copy(4 + j, (*chip, c), sibling) for j, chip in enumerate(chips)]
        for j, chip in enumerate(chips):
            copy(1 + j, (*chip, c), me).wait_recv()
            passed[j].start()
        copy(0, sibling, me).wait_recv()
        for j, chip in enumerate(chips):
            copy(4 + j, (*chip, 1 - c), me).wait_recv()
        for cp in first + passed:
            cp.wait_send()
        mine.wait()
        acc = buf[pl.ds(0, rows), :]
        for k in range(1, N_DEV):
            acc = acc + buf[pl.ds(k * rows, rows), :]
        out_ref[...] = acc

    return pl.pallas_call(
        body, name=name,
        out_shape=jax.ShapeDtypeStruct(part.shape, F32),
        in_specs=[pl.BlockSpec(memory_space=pltpu.VMEM)],
        out_specs=pl.BlockSpec(memory_space=pltpu.VMEM),
        scratch_shapes=[pltpu.VMEM((N_DEV * rows, LANES), F32), pltpu.SemaphoreType.DMA((7,)),
                        pltpu.SemaphoreType.DMA((7,)), pltpu.SemaphoreType.DMA],
        compiler_params=pltpu.CompilerParams(has_side_effects=True),
    )(part)


def _adamw(w, g, m, v, name):
    R, C = w.shape
    bs = _row_block(R, 512, 8) if R % 8 == 0 else R
    c1 = 1.0 / (1.0 - ADAM_B1 ** ADAM_STEP)
    c2 = 1.0 / (1.0 - ADAM_B2 ** ADAM_STEP)

    def body(w_ref, g_ref, m_ref, v_ref, d_ref, nm_ref, nv_ref):
        gg = g_ref[...]
        nm = ADAM_B1 * m_ref[...] + (1.0 - ADAM_B1) * gg
        nv = ADAM_B2 * v_ref[...] + (1.0 - ADAM_B2) * (gg * gg)
        nm_ref[...] = nm
        nv_ref[...] = nv
        d_ref[...] = -ADAM_LR * ((nm * c1) / (jnp.sqrt(nv * c2) + ADAM_EPS) + ADAM_WD * w_ref[...])

    spec = pl.BlockSpec((bs, C), lambda i: (i, 0))
    shp = jax.ShapeDtypeStruct((R, C), F32)
    return pl.pallas_call(
        body, name=name, grid=(R // bs,), in_specs=[spec] * 4, out_specs=[spec] * 3, out_shape=[shp] * 3,
        compiler_params=_params(("parallel",)),
    )(w, g, m, v)


WEIGHTS = ("norm_mix_pre_w", "w_in", "b_gate", "conv_w", "conv_b", "dt_bias", "a_log", "d_skip",
           "ssm_norm_w", "w_att_proj", "w_ssm_proj", "w_out", "norm_mix_post_w", "norm_ffn_pre_w", "w_up",
           "w_down", "norm_ffn_post_w")


def _flat_small(vals, conv_w_full):
    flat = [vals[n].reshape(-1) for n in SMALL] + [conv_w_full.reshape(-1)]
    v = jnp.concatenate(flat)
    return jnp.pad(v, (0, SMALL_ROWS * LANES - v.shape[0])).reshape(SMALL_ROWS, LANES)


def kernel(x, norm_mix_pre_w, w_in, b_gate, conv_w, conv_b, dt_bias, a_log, d_skip, ssm_norm_w, w_att_proj, w_ssm_proj, w_out, norm_mix_post_w, norm_ffn_pre_w, w_up, w_down, norm_ffn_post_w, loss_target, m_norm_mix_pre_w, m_w_in, m_b_gate, m_conv_w, m_conv_b, m_dt_bias, m_a_log, m_d_skip, m_ssm_norm_w, m_w_att_proj, m_w_ssm_proj, m_w_out, m_norm_mix_post_w, m_norm_ffn_pre_w, m_w_up, m_w_down, m_norm_ffn_post_w, v_norm_mix_pre_w, v_w_in, v_b_gate, v_conv_w, v_conv_b, v_dt_bias, v_a_log, v_d_skip, v_ssm_norm_w, v_w_att_proj, v_w_ssm_proj, v_w_out, v_norm_mix_post_w, v_norm_ffn_pre_w, v_w_up, v_w_down, v_norm_ffn_post_w):
    args = locals()

    def strip(a):
        return a[0] if a.ndim == 3 else a

    wts = {n: strip(args[n]) for n in WEIGHTS}
    mom = {n: strip(args["m_" + n]) for n in WEIGHTS}
    var = {n: strip(args["v_" + n]) for n in WEIGHTS}
    xi, yi, ci = _mesh_pos()
    chip = 2 * xi + yi

    tr = lambda a: jnp.swapaxes(a, 0, 1)
    w_in_mine = jnp.pad(tr(wts["w_in"]).astype(BF16), ((0, IN_SHARD_PAD - IN_SHARD_ROWS), (0, 0)))
    got_in = _comm_call("allgather_w_in", _gather_comm([w_in_mine]))[0]
    stacks_in = lax.dynamic_update_slice(got_in, w_in_mine[None], (chip, 0, 0))
    full = {"w_in_t": stacks_in[:, :IN_SHARD_ROWS].reshape(IN_PROJ_WIDTH, D_MODEL)}
    ex = _Exchange(chip, ci, {names: _stack_rest(wts, BF16, names) for names in (REST_EARLY,) + REST_LATE})
    cw_cols = CONV_DIM // N_CHIPS
    conv_slab = lax.dynamic_update_slice(jnp.zeros((SSM_CONV, CONV_DIM), F32),
                                         jnp.where(ci == 0, wts["conv_w"], 0.0), (0, chip * cw_cols))
    small_in = jnp.pad(conv_slab.reshape(-1), (0, SMALL_ROWS * LANES - SSM_CONV * CONV_DIM))
    conv_full = _allreduce_small(small_in.reshape(SMALL_ROWS, LANES), "gather_conv_w")
    full["conv_w"] = conv_full.reshape(-1)[:SSM_CONV * CONV_DIM].reshape(SSM_CONV, CONV_DIM)
    for n in SMALL:
        full[n] = wts[n]

    loss_part, grad_x, g = _local_step(x[0], loss_target[0], full, ex)
    loss = lax.psum(loss_part[0, 0], ("x", "y", "c"))

    gshard = _unstack_rest(ex.reduced["rest"])
    g_in_t = ex.reduced["w_in"][:IN_SHARD_ROWS]
    small_sum = _allreduce_small(_flat_small(g, g["conv_w"]), "allreduce_small_grads").reshape(-1)
    grads, off = {}, 0
    for n in SMALL:
        sz = wts[n].size
        grads[n] = small_sum[off:off + sz].reshape(wts[n].shape)
        off += sz
    conv_g = small_sum[off:off + SSM_CONV * CONV_DIM].reshape(SSM_CONV, CONV_DIM)
    grads["conv_w"] = lax.dynamic_slice(conv_g, (0, chip * cw_cols), (SSM_CONV, cw_cols))
    grads.update(gshard)

    delta, new_m, new_v = {}, {}, {}
    for n in REST:
        delta[n], new_m[n], new_v[n] = _adamw(wts[n], grads[n], mom[n], var[n], f"adamw_{n}")
    in_t = _adamw(tr(wts["w_in"]), g_in_t, tr(mom["w_in"]), tr(var["w_in"]), "adamw_w_in")
    grads["w_in"] = tr(g_in_t)
    delta["w_in"], new_m["w_in"], new_v["w_in"] = (tr(a) for a in in_t)
    small_names = SMALL + ("conv_w",)

    def pack_small(d):
        v = jnp.concatenate([d[n].reshape(-1) for n in small_names])
        rows = -(-v.shape[0] // (8 * LANES)) * 8
        return jnp.pad(v, (0, rows * LANES - v.shape[0])).reshape(rows, LANES)

    ds, ms, vs = _adamw(pack_small(wts), pack_small(grads), pack_small(mom), pack_small(var), "adamw_small")
    off = 0
    for n in small_names:
        sz = wts[n].size
        for dst, src in ((delta, ds), (new_m, ms), (new_v, vs)):
            dst[n] = src.reshape(-1)[off:off + sz].reshape(wts[n].shape)
        off += sz

    out = [loss, grad_x[None]]
    for d in (grads, delta, new_m, new_v):
        out += [d[n][None] if args[n].ndim == 3 else d[n] for n in WEIGHTS]
    return tuple(out)
```

```python
import math

import numpy as np
import jax
import jax.numpy as jnp
from jax import lax
from jax.experimental import pallas as pl
from jax.experimental.pallas import tpu as pltpu

F32 = jnp.float32
BF16 = jnp.bfloat16

D_MODEL = 1024
HEAD_DIM = 64
N_ATT_HEADS = 12
ATT_WIDTH = N_ATT_HEADS * HEAD_DIM
DILATIONS = (1, 4, 16)
ATT_BLOCK = 128
SSM_INNER = 2048
SSM_HEADS = 32
SSM_GROUPS = 8
HEADS_PER_GROUP = SSM_HEADS // SSM_GROUPS
SSM_HEAD_DIM = 64
SSM_STATE = 128
SSM_CONV = 4
SSM_CHUNK = 128
CONV_DIM = SSM_INNER + 2 * SSM_GROUPS * SSM_STATE
FFN_HIDDEN = 4 * D_MODEL
IN_SPLITS = (ATT_WIDTH, ATT_WIDTH, ATT_WIDTH, SSM_INNER, CONV_DIM, SSM_HEADS, 2 * D_MODEL)
IN_PROJ_WIDTH = sum(IN_SPLITS)
RMS_EPS = 1e-6
LANES = 128
NEG_BIG = -1e30

ADAM_LR = 0.001
ADAM_B1 = 0.9
ADAM_B2 = 0.999
ADAM_EPS = 1e-08
ADAM_WD = 0.01
ADAM_STEP = 10

N_CHIPS = 4
N_DEV = 8
VMEM_LIMIT = 56 * 1024 * 1024
MESH = pl.DeviceIdType.MESH


def _alibi_slopes(n):
    def pow2(m):
        start = 2.0 ** (-8.0 / m)
        return [start ** (i + 1) for i in range(m)]
    if (n & (n - 1)) == 0:
        s = pow2(n)
    else:
        c = 2 ** int(math.floor(math.log2(n)))
        s = pow2(c) + pow2(2 * c)[0::2][: n - c]
    return [float(v) for v in np.array(s, dtype=np.float32)]


def _params(sem):
    return pltpu.CompilerParams(dimension_semantics=sem, vmem_limit_bytes=VMEM_LIMIT)


def _dot(a, b):
    return lax.dot_general(a, b, (((1,), (0,)), ((), ())), preferred_element_type=F32)


def _dot_nt(a, b):
    return lax.dot_general(a, b, (((1,), (1,)), ((), ())), preferred_element_type=F32)


def _dot_tn(a, b):
    return lax.dot_general(a, b, (((0,), (0,)), ((), ())), preferred_element_type=F32)


def _dot_hi(a, b):
    return lax.dot_general(a, b, (((1,), (0,)), ((), ())), preferred_element_type=F32,
                           precision=lax.Precision.HIGHEST)


def _b(x):
    return x.astype(BF16)


def _sigmoid(x):
    return 1.0 / (1.0 + jnp.exp(-x))


def _pick(n, cands):
    for c in cands:
        if n % c == 0:
            return c
    raise ValueError(f"no tile for {n}")


def _row_block(rows, cap, mult):
    best = max(d for d in range(mult, cap + 1, mult) if rows % d == 0)
    return best


class _Comm:
    def __init__(self, plan, ins, outs, n_sems):
        self.start, self.finish = plan
        self.ins, self.outs, self.n_sems = list(ins), list(outs), n_sems


def _mm_nn(a, b, out_dtype, name, acc=None, mode=None, extra=None, comm=None, tb=False):
    M, K = a.shape
    N = b.shape[0] if tb else b.shape[1]
    tn = _pick(N, ((2048,) if K <= 1024 else ()) + (1024, 768, 512, 256, 128))
    tk = K if K <= 4096 else _pick(K, (2048, 1024))
    tm = 1024 if M % 1024 == 0 and K <= 2304 else 512
    nk = K // tk
    nj, ni = N // tn, M // tm
    side = acc if acc is not None else extra
    n_out = 2 if mode == "relu2" else 1
    n_in = 2 + (side is not None)
    n_ci = len(comm.ins) if comm else 0
    n_co = len(comm.outs) if comm else 0

    def body(*refs):
        a_ref, b_ref = refs[0], refs[1]
        s_ref = refs[2] if side is not None else None
        o_refs = refs[n_in + n_ci:n_in + n_ci + n_out]
        if comm:
            c_args = (refs[n_in:n_in + n_ci], refs[n_in + n_ci + n_out:n_in + n_ci + n_out + n_co],
                      refs[-2], refs[-1])
            pj, pi, pk = pl.program_id(0), pl.program_id(1), pl.program_id(2)

            @pl.when(jnp.logical_and(jnp.logical_and(pj == 0, pi == 0), pk == 0))
            def _():
                comm.start(*c_args)

        def finish(r):
            if mode == "relu2":
                r = jnp.maximum(r, 0.0)
                o_refs[0][...] = _b(r)
                o_refs[1][...] = _b(r * r)
            elif mode == "mul2":
                o_refs[0][...] = _b(r * (2.0 * s_ref[...].astype(F32)))
            else:
                if acc is not None:
                    r = r + s_ref[...]
                o_refs[0][...] = r.astype(out_dtype)

        part = (_dot_nt if tb else _dot)(_b(a_ref[...]), _b(b_ref[...]))
        if nk == 1:
            finish(part)
        else:
            acc_ref = refs[n_in + n_ci + n_out + n_co]
            k = pl.program_id(2)

            @pl.when(k == 0)
            def _():
                acc_ref[...] = part

            @pl.when(jnp.logical_and(k > 0, k < nk - 1))
            def _():
                acc_ref[...] += part

            @pl.when(k == nk - 1)
            def _():
                finish(acc_ref[...] + part)

        if comm:
            @pl.when(jnp.logical_and(jnp.logical_and(pj == nj - 1, pi == ni - 1), pk == nk - 1))
            def _():
                comm.finish(*c_args)

    tile = pl.BlockSpec((tm, tn), lambda j, i, k: (i, j))
    in_specs = [pl.BlockSpec((tm, tk), lambda j, i, k: (i, k)),
                pl.BlockSpec((tn, tk), lambda j, i, k: (j, k)) if tb else
                pl.BlockSpec((tk, tn), lambda j, i, k: (k, j))]
    args = [a, b]
    if side is not None:
        in_specs.append(tile)
        args.append(side)
    odt = BF16 if mode in ("relu2", "mul2") else out_dtype
    scratch = [pltpu.VMEM((tm, tn), F32)] if nk > 1 else []
    if comm:
        scratch += [pltpu.SemaphoreType.DMA((comm.n_sems,))] * 2
        params = pltpu.CompilerParams(dimension_semantics=("arbitrary",) * 3, vmem_limit_bytes=VMEM_LIMIT,
                                      has_side_effects=True)
    else:
        params = _params(("parallel", "parallel", "arbitrary"))
    outs = pl.pallas_call(
        body, name=name, grid=(nj, ni, nk),
        in_specs=in_specs + [ANY] * n_ci,
        out_specs=[tile] * n_out + [ANY] * n_co,
        out_shape=[jax.ShapeDtypeStruct((M, N), odt)] * n_out + list(comm.outs if comm else []),
        scratch_shapes=scratch,
        compiler_params=params,
    )(*args, *(comm.ins if comm else []))
    res = outs[:n_out] if n_out > 1 else outs[0]
    return (res, outs[n_out:]) if comm else res


class _Epi:
    def __init__(self, fn, row_ins=(), full_ins=(), row_outs=(), acc_outs=(), tiled=False, a_fn=None,
                 scratch=()):
        self.fn, self.row_ins, self.full_ins = fn, list(row_ins), list(full_ins)
        self.row_outs, self.acc_outs, self.tiled = list(row_outs), list(acc_outs), tiled
        self.a_fn = a_fn
        self.scratch = list(scratch)


def _acc_into(ref, val, first):
    @pl.when(first)
    def _():
        ref[...] = val

    @pl.when(jnp.logical_not(first))
    def _():
        ref[...] += val


def _mm_epi(a, b, epi, name, tb=False, comm=None, tm=512, tn=None):
    N, K = b.shape if tb else b.shape[::-1]
    M = epi.row_ins[0][0].shape[0] if a is None else a.shape[0]
    tn = tn or N
    assert epi.tiled or tn == N
    tk = K if K <= 4096 else _pick(K, (2048, 1024))
    nk = K // tk
    nj, ni = N // tn, M // tm
    assert a is not None or (nk == 1 and nj == 1)
    n_a = 0 if a is None else 1
    n_ri, n_fi, n_ro, n_ao = len(epi.row_ins), len(epi.full_ins), len(epi.row_outs), len(epi.acc_outs)
    n_ci = len(comm.ins) if comm else 0
    n_co = len(comm.outs) if comm else 0
    i0 = n_a + 1
    o0 = i0 + n_ci + n_ri + n_fi

    def body(*refs):
        b_ref = refs[n_a]
        ri = refs[i0 + n_ci:i0 + n_ci + n_ri]
        fi = refs[i0 + n_ci + n_ri:o0]
        ro = refs[o0 + n_co:o0 + n_co + n_ro]
        ao = refs[o0 + n_co + n_ro:o0 + n_co + n_ro + n_ao]
        pj, pi, pk = pl.program_id(0), pl.program_id(1), pl.program_id(2)
        if comm:
            c_args = (refs[i0:i0 + n_ci], refs[o0:o0 + n_co], refs[-2], refs[-1])

            @pl.when(jnp.logical_and(jnp.logical_and(pj == 0, pi == 0), pk == 0))
            def _():
                comm.start(*c_args)

        s0 = o0 + n_co + n_ro + n_ao + (nk > 1)
        extra = (refs[s0:s0 + len(epi.scratch)],) if epi.scratch else ()
        a_val = epi.a_fn(ri, fi, ro) if a is None else _b(refs[0][...])
        part = (_dot_nt if tb else _dot)(a_val, _b(b_ref[...]))
        if nk == 1:
            epi.fn(part, ri, fi, ro, ao, pi == 0, *extra)
        else:
            acc_ref = refs[o0 + n_co + n_ro + n_ao]

            @pl.when(pk == 0)
            def _():
                acc_ref[...] = part

            @pl.when(jnp.logical_and(pk > 0, pk < nk - 1))
            def _():
                acc_ref[...] += part

            @pl.when(pk == nk - 1)
            def _():
                epi.fn(acc_ref[...] + part, ri, fi, ro, ao, pi == 0, *extra)

        if comm:
            @pl.when(jnp.logical_and(jnp.logical_and(pj == nj - 1, pi == ni - 1), pk == nk - 1))
            def _():
                comm.finish(*c_args)

    def row_spec(width, cb):
        if epi.tiled:
            return pl.BlockSpec((tm, tn), lambda j, i, k: (i, j + cb))
        return pl.BlockSpec((tm, width), lambda j, i, k: (i, cb))

    in_specs = [pl.BlockSpec((tm, tk), lambda j, i, k: (i, k))] * n_a
    in_specs += [pl.BlockSpec((tn, tk), lambda j, i, k: (j, k)) if tb else
                 pl.BlockSpec((tk, tn), lambda j, i, k: (k, j))]
    in_specs += [ANY] * n_ci
    in_specs += [row_spec(w, cb) for (_, w, cb) in epi.row_ins]
    in_specs += [pl.BlockSpec((1, tn), lambda j, i, k: (0, j)) if epi.tiled else
                 pl.BlockSpec(f.shape, lambda j, i, k: (0, 0)) for f in epi.full_ins]
    out_specs = [ANY] * n_co + [row_spec(c, 0) if isinstance(c, int) else spec(tm, tn) for c, spec in epi.row_outs]
    out_specs += [pl.BlockSpec((1, tn), lambda j, i, k: (0, j)) if epi.tiled else
                  pl.BlockSpec((1, c), lambda j, i, k: (0, 0)) for c in epi.acc_outs]
    out_shape = list(comm.outs if comm else [])
    out_shape += [jax.ShapeDtypeStruct((M, c), dt_) if isinstance(c, int) else c for c, dt_ in epi.row_outs]
    out_shape += [jax.ShapeDtypeStruct((1, c), F32) for c in epi.acc_outs]
    scratch = ([pltpu.VMEM((tm, tn), F32)] if nk > 1 else []) + epi.scratch
    if comm:
        scratch += [pltpu.SemaphoreType.DMA((comm.n_sems,))] * 2
    params = pltpu.CompilerParams(dimension_semantics=("arbitrary",) * 3, vmem_limit_bytes=VMEM_LIMIT,
                                  has_side_effects=comm is not None)
    outs = pl.pallas_call(
        body, name=name, grid=(nj, ni, nk), in_specs=in_specs, out_specs=out_specs, out_shape=out_shape,
        scratch_shapes=scratch, compiler_params=params,
    )(*([a] * n_a), b, *(comm.ins if comm else []), *[arr for arr, _, _ in epi.row_ins], *epi.full_ins)
    return outs[n_co:], (outs[:n_co] if comm else None)


def _mm_tn(a, b, name):
    S, Ka = a.shape
    _, N = b.shape
    tka = _pick(Ka, ((2048,) if N <= 1024 else ()) + (1024, 768, 512, 256, 128))
    tn = _pick(N, ((2048,) if Ka <= 1024 else ()) + (1024, 768, 512, 256, 128))
    ts = 1024 if S % 1024 == 0 else 512
    ns = S // ts

    def body(a_ref, b_ref, o_ref, acc_ref):
        s = pl.program_id(2)
        part = _dot_tn(_b(a_ref[...]), _b(b_ref[...]))

        @pl.when(s == 0)
        def _():
            acc_ref[...] = part

        @pl.when(s > 0)
        def _():
            acc_ref[...] += part

        @pl.when(s == ns - 1)
        def _():
            o_ref[...] = acc_ref[...]

    return pl.pallas_call(
        body, name=name, grid=(Ka // tka, N // tn, ns),
        in_specs=[pl.BlockSpec((ts, tka), lambda i, j, s: (s, i)),
                  pl.BlockSpec((ts, tn), lambda i, j, s: (s, j))],
        out_specs=pl.BlockSpec((tka, tn), lambda i, j, s: (i, j)),
        out_shape=jax.ShapeDtypeStruct((Ka, N), F32),
        scratch_shapes=[pltpu.VMEM((tka, tn), F32)],
        compiler_params=_params(("parallel", "parallel", "arbitrary")),
    )(a, b)


def _row_call(body, row_ins, full_ins, row_outs, acc_outs, bs, name):
    S = row_ins[0].shape[0]
    assert S % bs == 0
    in_specs = [pl.BlockSpec((bs, a.shape[1]), lambda i: (i, 0)) for a in row_ins]
    in_specs += [pl.BlockSpec(a.shape, lambda i: (0, 0)) for a in full_ins]
    out_specs = [pl.BlockSpec((bs, c), lambda i: (i, 0)) for c, _ in row_outs]
    out_specs += [pl.BlockSpec(s, lambda i: (0, 0)) for s in acc_outs]
    out_shape = [jax.ShapeDtypeStruct((S, c), dt) for c, dt in row_outs]
    out_shape += [jax.ShapeDtypeStruct(s, F32) for s in acc_outs]
    return pl.pallas_call(
        body, name=name, grid=(S // bs,), in_specs=in_specs, out_specs=out_specs, out_shape=out_shape,
        compiler_params=_params(("arbitrary",)),
    )(*row_ins, *full_ins)


def _rms_vals(x, w):
    r = lax.rsqrt(jnp.mean(x * x, axis=-1, keepdims=True) + RMS_EPS)
    return x * r * w


def _rms_bwd_vals(x, w, dy):
    r = lax.rsqrt(jnp.mean(x * x, axis=-1, keepdims=True) + RMS_EPS)
    xn = x * r
    g = dy * w
    dx = r * (g - xn * jnp.mean(g * xn, axis=-1, keepdims=True))
    dw = jnp.sum(dy * xn, axis=0, keepdims=True)
    return dx, dw


def _rms_fwd(x, w):
    def body(x_ref, w_ref, o_ref):
        o_ref[...] = _b(_rms_vals(x_ref[...], w_ref[...]))
    return _row_call(body, [x], [w], [(x.shape[1], BF16)], [], 512, "rms_fwd")[0]


def _group_rms(t):
    gw = SSM_INNER // SSM_GROUPS
    out = []
    for g in range(SSM_GROUPS):
        tg = t[:, g * gw:(g + 1) * gw]
        out.append(lax.rsqrt(jnp.mean(tg * tg, axis=-1, keepdims=True) + RMS_EPS))
    return out


def _ssm_out_epi(y, z, w):
    gw = SSM_INNER // SSM_GROUPS

    def a_fn(ri, fi, ro):
        zz = ri[1][...]
        t = ri[0][...] * (zz * _sigmoid(zz))
        rs = _group_rms(t)
        for g in range(SSM_GROUPS):
            sl = slice(g * gw, (g + 1) * gw)
            ro[0][:, sl] = _b(t[:, sl] * rs[g] * fi[0][:, sl])
        return ro[0][...]

    def fn(r, ri, fi, ro, ao, first):
        ro[1][...] = r
    return _Epi(fn, [(y, SSM_INNER, 0), (z, SSM_INNER, 0)], [w], [(SSM_INNER, BF16), (D_MODEL, F32)], a_fn=a_fn)


def _mix_out_epi(att_o, ssm_o, gl, x, b_gate, w_post, w_pre):
    def a_fn(ri, fi, ro):
        g = _sigmoid(ri[2][...] + fi[0][...])
        mi = _b(g[:, :D_MODEL] * ri[0][...] + g[:, D_MODEL:] * ri[1][...])
        ro[0][...] = mi
        return mi

    def fn(r, ri, fi, ro, ao, first):
        ro[1][...] = r
        h = ri[3][...] + _rms_vals(r, fi[1][...])
        ro[2][...] = h
        ro[3][...] = _b(_rms_vals(h, fi[2][...]))
    return _Epi(fn, [(att_o, D_MODEL, 0), (ssm_o, D_MODEL, 0), (gl, 2 * D_MODEL, 0), (x, D_MODEL, 0)],
                [b_gate, w_post, w_pre],
                [(D_MODEL, BF16), (D_MODEL, F32), (D_MODEL, F32), (D_MODEL, BF16)], a_fn=a_fn)


def _final_epi(h1, target, w_post):
    def fn(dn, ri, fi, ro, ao, first):
        w = fi[0][...]
        err = ri[0][...] + _rms_vals(dn, w) - ri[1][...]
        row = jnp.mean(err * err, axis=-1, keepdims=True)
        part = 0.5 * jnp.sum(row, axis=0, keepdims=True)
        dh = err * (1.0 / D_MODEL)
        ro[0][...] = dh
        dx, dw = _rms_bwd_vals(dn, w, dh)
        ro[1][...] = _b(dx)
        _acc_into(ao[0], jnp.broadcast_to(part, (1, LANES)), first)
        _acc_into(ao[1], dw, first)
    return _Epi(fn, [(h1, D_MODEL, 0), (target, D_MODEL, 0)], [w_post], [(D_MODEL, F32), (D_MODEL, BF16)],
                [LANES, D_MODEL])


def _mid_epi(dh2, h1, mixed, w_pre, w_post):
    def fn(df, ri, fi, ro, ao, first):
        dx, dwn = _rms_bwd_vals(ri[1][...], fi[0][...], df)
        dh1 = ri[0][...] + dx
        ro[0][...] = dh1
        dm, dwp = _rms_bwd_vals(ri[2][...], fi[1][...], dh1)
        ro[1][...] = _b(dm)
        _acc_into(ao[0], dwn, first)
        _acc_into(ao[1], dwp, first)
    return _Epi(fn, [(dh2, D_MODEL, 0), (h1, D_MODEL, 0), (mixed, D_MODEL, 0)], [w_pre, w_post],
                [(D_MODEL, F32), (D_MODEL, BF16)], [D_MODEL, D_MODEL])


def _gate_epi(att_o, ssm_o, gl, b_gate):
    def fn(d, ri, fi, ro, ao, first):
        g = _sigmoid(ri[2][...] + fi[0][...])
        ga, gs = g[:, :D_MODEL], g[:, D_MODEL:]
        ro[0][...] = _b(ga * d)
        ro[1][...] = _b(gs * d)
        dga = d * ri[0][...] * ga * (1.0 - ga)
        dgs = d * ri[1][...] * gs * (1.0 - gs)
        ro[2][:, :D_MODEL] = _b(dga)
        ro[2][:, D_MODEL:] = _b(dgs)
        _acc_into(ao[0].at[:, pl.ds(0, D_MODEL)], jnp.sum(dga, axis=0, keepdims=True), first)
        _acc_into(ao[0].at[:, pl.ds(D_MODEL, D_MODEL)], jnp.sum(dgs, axis=0, keepdims=True), first)
    return _Epi(fn, [(att_o, D_MODEL, 0), (ssm_o, D_MODEL, 0), (gl, 2 * D_MODEL, 0)], [b_gate],
                [(D_MODEL, BF16), (D_MODEL, BF16), (2 * D_MODEL, BF16)], [2 * D_MODEL])


def _first_epi(du, dh1, x, w_pre):
    def fn(r, ri, fi, ro, ao, first):
        dx, dw = _rms_bwd_vals(ri[2][...], fi[0][...], ri[0][...] + r)
        ro[0][...] = ri[1][...] + dx
        _acc_into(ao[0], dw, first)
    return _Epi(fn, [(du, D_MODEL, 0), (dh1, D_MODEL, 0), (x, D_MODEL, 0)], [w_pre], [(D_MODEL, F32)],
                [D_MODEL])


def _gnorm_epi(y, z, w):
    gw = SSM_INNER // SSM_GROUPS

    def fn(d_all, ri, fi, ro, ao, first):
        zz = ri[1][...]
        yy = ri[0][...]
        sg = _sigmoid(zz)
        sz = zz * sg
        t = yy * sz
        dws = []
        for g in range(d_all.shape[1] // gw):
            sl = slice(g * gw, (g + 1) * gw)
            tg = t[:, sl]
            r = lax.rsqrt(jnp.mean(tg * tg, axis=-1, keepdims=True) + RMS_EPS)
            tn = tg * r
            d = d_all[:, sl]
            gg = d * fi[0][:, sl]
            dt = r * (gg - tn * jnp.mean(gg * tn, axis=-1, keepdims=True))
            ro[0][:, sl] = dt * sz[:, sl]
            ro[1][:, sl] = _b(dt * yy[:, sl] * (sg[:, sl] * (1.0 + zz[:, sl] * (1.0 - sg[:, sl]))))
            dws.append(jnp.sum(d * tn, axis=0, keepdims=True))
        _acc_into(ao[0], jnp.concatenate(dws, axis=1), first)
    return _Epi(fn, [(y, SSM_INNER, 0), (z, SSM_INNER, 0)], [w], [(SSM_INNER, F32), (SSM_INNER, BF16)],
                [SSM_INNER], tiled=True)


def _head_col(stat, h):
    return stat[:, h:h + 1]


def _head_pair_masks(x):
    lane = lax.broadcasted_iota(jnp.int32, x.shape, 1)
    zero = jnp.zeros_like(x)
    return jnp.where(lane < HEAD_DIM, x, zero), jnp.where(lane >= HEAD_DIM, x, zero)


def _attn_fwd(qkv, d, comm=None):
    S = qkv.shape[0]
    blk = ATT_BLOCK
    nblk = S // blk
    nbs = nblk // d
    slopes = _alibi_slopes(N_ATT_HEADS)
    scale = HEAD_DIM ** -0.5
    n_ci = len(comm.ins) if comm else 0
    n_co = len(comm.outs) if comm else 0

    def body(*refs):
        q_ref, kc_ref, kp_ref, vc_ref, vp_ref = refs[:5]
        o_ref, m_ref, l_ref = refs[5 + n_ci:8 + n_ci]
        n = pl.program_id(0)
        if comm:
            c_args = (refs[5:5 + n_ci], refs[8 + n_ci:8 + n_ci + n_co], refs[-2], refs[-1])

            @pl.when(n == 0)
            def _():
                comm.start(*c_args)

        has_prev = (n % nbs) != 0
        ii = lax.broadcasted_iota(jnp.int32, (blk, 2 * blk), 0)
        jj = lax.broadcasted_iota(jnp.int32, (blk, 2 * blk), 1)
        dist_i = blk + ii - jj
        dist = dist_i.astype(F32)
        ok = jnp.logical_and(jnp.logical_and(dist_i >= 0, dist_i <= blk), jnp.logical_or(jj >= blk, has_prev))
        s_scr, p_scr = refs[8 + n_ci + n_co], refs[9 + n_ci + n_co]
        lane = lax.broadcasted_iota(jnp.int32, (blk, LANES), 1)
        for pr in range(N_ATT_HEADS // 2):
            sl = slice(pr * LANES, (pr + 1) * LANES)
            kcat = jnp.concatenate([kp_ref[:, sl], kc_ref[:, sl]], axis=0)
            for h, qh in zip((2 * pr, 2 * pr + 1), _head_pair_masks(q_ref[:, sl])):
                s_scr[h] = _dot_nt(qh, kcat)
        m_all = jnp.zeros((blk, LANES), F32)
        l_all = jnp.zeros((blk, LANES), F32)
        for h in range(N_ATT_HEADS):
            s = jnp.where(ok, s_scr[h] * scale - (slopes[h] * float(d)) * dist, NEG_BIG)
            m = jnp.max(s, axis=-1, keepdims=True)
            p = jnp.exp(s - m)
            l = jnp.sum(p, axis=-1, keepdims=True)
            m_all = jnp.where(lane == h, m, m_all)
            l_all = jnp.where(lane == h, l, l_all)
            p_scr[:, h * 2 * blk:(h + 1) * 2 * blk] = _b(p)
        for pr in range(N_ATT_HEADS // 2):
            sl = slice(pr * LANES, (pr + 1) * LANES)
            vmask = jnp.concatenate(
                _head_pair_masks(jnp.concatenate([vp_ref[:, sl], vc_ref[:, sl]], axis=0)), axis=0)
            o_ref[:, sl] = _dot(p_scr[:, pr * 4 * blk:(pr + 1) * 4 * blk], vmask)
        m_ref[...] = m_all
        l_ref[...] = l_all
        if comm:
            @pl.when(n == nblk - 1)
            def _():
                comm.finish(*c_args)

    cur = lambda c: pl.BlockSpec((blk, ATT_WIDTH), lambda n: (n, c))
    prev = lambda c: pl.BlockSpec((blk, ATT_WIDTH), lambda n: (jnp.maximum(n - 1, 0), c))
    stat = pl.BlockSpec((blk, LANES), lambda n: (n, 0))
    scratch = [pltpu.VMEM((N_ATT_HEADS, blk, 2 * blk), F32), pltpu.VMEM((blk, N_ATT_HEADS * 2 * blk), BF16)]
    if comm:
        scratch += [pltpu.SemaphoreType.DMA((comm.n_sems,))] * 2
        params = pltpu.CompilerParams(dimension_semantics=("arbitrary",), vmem_limit_bytes=VMEM_LIMIT,
                                      has_side_effects=True)
    else:
        params = _params(("parallel",))
    outs = pl.pallas_call(
        body, name=f"attn_fwd_d{d}", grid=(nblk,),
        in_specs=[cur(0), cur(1), prev(1), cur(2), prev(2)] + [ANY] * n_ci,
        out_specs=[cur(0), stat, stat] + [ANY] * n_co,
        out_shape=[jax.ShapeDtypeStruct((S, ATT_WIDTH), F32), jax.ShapeDtypeStruct((S, LANES), F32),
                   jax.ShapeDtypeStruct((S, LANES), F32)] + list(comm.outs if comm else []),
        scratch_shapes=scratch,
        compiler_params=params,
    )(qkv, qkv, qkv, qkv, qkv, *(comm.ins if comm else []))
    return (outs[0], outs[1], outs[2], outs[3:]) if comm else outs


def _attn_bwd(qkv, do, lse, delta, d, comm=None):
    S = qkv.shape[0]
    blk = ATT_BLOCK
    nblk = S // blk
    nbs = nblk // d
    slopes = _alibi_slopes(N_ATT_HEADS)
    scale = HEAD_DIM ** -0.5
    n_ci = len(comm.ins) if comm else 0
    n_co = len(comm.outs) if comm else 0

    def body(*refs):
        qc_ref, qn_ref, k_ref, v_ref, doc_ref, don_ref, lc_ref, ln_ref, dc_ref, dn_ref = refs[:10]
        dq_ref, dk_ref, dv_ref = refs[10 + n_ci:13 + n_ci]
        carry_ref = refs[13 + n_ci + n_co]
        n = pl.program_id(0)
        has_next = ((n + 1) % nbs) != 0
        if comm:
            c_args = (refs[10:10 + n_ci], refs[13 + n_ci:13 + n_ci + n_co], refs[-2], refs[-1])

        @pl.when(n == 0)
        def _():
            carry_ref[...] = jnp.zeros_like(carry_ref)
            if comm:
                comm.start(*c_args)

        rr = lax.broadcasted_iota(jnp.int32, (2 * blk, blk), 0)
        jj = lax.broadcasted_iota(jnp.int32, (2 * blk, blk), 1)
        dist_i = rr - jj
        dist = dist_i.astype(F32)
        ok = jnp.logical_or(jnp.logical_and(rr < blk, dist_i >= 0),
                            jnp.logical_and(jnp.logical_and(rr >= blk, dist_i <= blk), has_next))
        s_scr, dp_scr, p_rows, ds_rows, ds_cols = refs[14 + n_ci + n_co:19 + n_ci + n_co]
        lcat = jnp.concatenate([lc_ref[...], ln_ref[...]], axis=0)
        dcat = jnp.concatenate([dc_ref[...], dn_ref[...]], axis=0)
        rows2 = 2 * blk

        def operands(pr):
            sl = slice(pr * LANES, (pr + 1) * LANES)
            qm = _head_pair_masks(jnp.concatenate([qc_ref[:, sl], qn_ref[:, sl]], axis=0))
            dom = _head_pair_masks(jnp.concatenate([doc_ref[:, sl], don_ref[:, sl]], axis=0))
            return sl, qm, dom

        for pr in range(N_ATT_HEADS // 2):
            sl, qm, dom = operands(pr)
            for h, qh, doh in zip((2 * pr, 2 * pr + 1), qm, dom):
                s_scr[h] = _dot_nt(qh, k_ref[:, sl])
                dp_scr[h] = _dot_nt(doh, v_ref[:, sl])
        for h in range(N_ATT_HEADS):
            s = jnp.where(ok, s_scr[h] * scale - (slopes[h] * float(d)) * dist - lcat[:, h:h + 1], NEG_BIG)
            p = jnp.exp(s)
            dsb = _b(p * (dp_scr[h] - dcat[:, h:h + 1]) * scale)
            p_rows[h * rows2:(h + 1) * rows2, :] = _b(p)
            ds_rows[h * rows2:(h + 1) * rows2, :] = dsb
            ds_cols[:, h * blk:(h + 1) * blk] = dsb
        for pr in range(N_ATT_HEADS // 2):
            sl, qm, dom = operands(pr)
            pair_rows = slice(pr * 2 * rows2, (pr + 1) * 2 * rows2)
            dv_ref[:, sl] = _b(_dot_tn(p_rows[pair_rows, :], jnp.concatenate(dom, axis=0)))
            dk_ref[:, sl] = _b(_dot_tn(ds_rows[pair_rows, :], jnp.concatenate(qm, axis=0)))
            dq = _dot(ds_cols[:, pr * 2 * blk:(pr + 1) * 2 * blk],
                      jnp.concatenate(_head_pair_masks(k_ref[:, sl]), axis=0))
            dq_ref[:, sl] = _b(dq[:blk] + carry_ref[:, sl])
            carry_ref[:, sl] = dq[blk:]

        if comm:
            @pl.when(n == nblk - 1)
            def _():
                comm.finish(*c_args)

    cur = lambda c: pl.BlockSpec((blk, ATT_WIDTH), lambda n: (n, c))
    nxt = lambda c: pl.BlockSpec((blk, ATT_WIDTH), lambda n: (jnp.minimum(n + 1, nblk - 1), c))
    scur = pl.BlockSpec((blk, LANES), lambda n: (n, 0))
    snxt = pl.BlockSpec((blk, LANES), lambda n: (jnp.minimum(n + 1, nblk - 1), 0))
    shp = jax.ShapeDtypeStruct((S, ATT_WIDTH), BF16)
    scratch = [pltpu.VMEM((blk, ATT_WIDTH), F32),
               pltpu.VMEM((N_ATT_HEADS, 2 * blk, blk), F32), pltpu.VMEM((N_ATT_HEADS, 2 * blk, blk), F32),
               pltpu.VMEM((N_ATT_HEADS * 2 * blk, blk), BF16), pltpu.VMEM((N_ATT_HEADS * 2 * blk, blk), BF16),
               pltpu.VMEM((2 * blk, N_ATT_HEADS * blk), BF16)]
    if comm:
        scratch += [pltpu.SemaphoreType.DMA((comm.n_sems,))] * 2
        params = pltpu.CompilerParams(dimension_semantics=("arbitrary",), vmem_limit_bytes=VMEM_LIMIT,
                                      has_side_effects=True)
    else:
        params = _params(("arbitrary",))
    outs = pl.pallas_call(
        body, name=f"attn_bwd_d{d}", grid=(nblk,),
        in_specs=[cur(0), nxt(0), cur(1), cur(2), cur(0), nxt(0), scur, snxt, scur, snxt] + [ANY] * n_ci,
        out_specs=[cur(0), cur(0), cur(0)] + [ANY] * n_co,
        out_shape=[shp, shp, shp] + list(comm.outs if comm else []),
        scratch_shapes=scratch,
        compiler_params=params,
    )(qkv, qkv, qkv, qkv, do, do, lse, lse, delta, delta, *(comm.ins if comm else []))
    return (outs[0], outs[1], outs[2], outs[3:]) if comm else outs


LAYOUT_TILE = 512
DILATED = tuple(d for d in DILATIONS if d > 1)


def _pat_spec(d, cols, col_block=0):
    return pl.BlockSpec((d, LAYOUT_TILE // d, cols), lambda i: (0, i, col_block))


def _pat_view(a, d):
    return a.reshape(d, a.shape[0] // d, a.shape[1])


def _to_slabs(slab_ref, src_ref):
    for cb in range(slab_ref.shape[0]):
        slab_ref[cb] = src_ref[:, cb * LANES:(cb + 1) * LANES].astype(F32)


def _gather_pattern(dst_ref, slab_ref, d, dtype):
    t = slab_ref.shape[1]
    for cb in range(slab_ref.shape[0]):
        one = slab_ref.at[cb]
        for r in range(d):
            dst_ref[r, :, cb * LANES:(cb + 1) * LANES] = one[pl.ds(r, t // d, stride=d), :].astype(dtype)


def _scatter_pattern(slab_ref, src_ref, d, add=False):
    t = slab_ref.shape[1]
    for cb in range(slab_ref.shape[0]):
        one = slab_ref.at[cb]
        for r in range(d):
            idx = pl.ds(r, t // d, stride=d)
            val = src_ref[r, :, cb * LANES:(cb + 1) * LANES]
            if add:
                val = val + one[idx, :]
            one[idx, :] = val


def _pat_out(d, S, cols, dtype, col_tiled):
    def spec(tm, tn):
        if col_tiled:
            return pl.BlockSpec((d, tm // d, tn), lambda j, i, k: (0, i, j))
        return pl.BlockSpec((d, tm // d, cols), lambda j, i, k: (0, i, 0))
    return jax.ShapeDtypeStruct((d, S // d, cols), dtype), spec


def _qkv_epi(S):
    def fn(r, ri, fi, ro, ao, first, scr):
        ro[0][...] = _b(r)
        slab = scr[0]
        for cb in range(slab.shape[0]):
            slab[cb] = r[:, cb * LANES:(cb + 1) * LANES]
        for d, p_ref in zip(DILATED, ro[1:]):
            _gather_pattern(p_ref, slab, d, BF16)
    return _Epi(fn, row_outs=[(3 * ATT_WIDTH, BF16)] + [_pat_out(d, S, 3 * ATT_WIDTH, BF16, True) for d in DILATED],
                tiled=True, scratch=[pltpu.VMEM((ATT_WIDTH // LANES, LAYOUT_TILE, LANES), F32)])


def _attn_combine(os, ms, ls):
    S = os[0].shape[0]
    t = LAYOUT_TILE

    def body(o1, o2, o3, m1, m2, m3, l1, l2, l3, att_ref, lse_ref, so2, so3, sm2, sm3, sl2, sl3):
        for d, src, dst in ((DILATED[0], o2, so2), (DILATED[1], o3, so3), (DILATED[0], m2, sm2),
                            (DILATED[1], m3, sm3), (DILATED[0], l2, sl2), (DILATED[1], l3, sl3)):
            _scatter_pattern(dst, src, d)
        mm = [m1[...], sm2[0], sm3[0]]
        big = jnp.maximum(jnp.maximum(mm[0], mm[1]), mm[2])
        es = [jnp.exp(m - big) for m in mm]
        den = es[0] * l1[...] + es[1] * sl2[0] + es[2] * sl3[0]
        lse_ref[...] = big + jnp.log(den)
        inv = 1.0 / den
        for h in range(N_ATT_HEADS):
            sl = slice(h * HEAD_DIM, (h + 1) * HEAD_DIM)
            cb, hl = divmod(h, 2)
            sll = slice(hl * HEAD_DIM, (hl + 1) * HEAD_DIM)
            num = (_head_col(es[0], h) * o1[:, sl] + _head_col(es[1], h) * so2[cb, :, sll]
                   + _head_col(es[2], h) * so3[cb, :, sll])
            att_ref[:, sl] = num * _head_col(inv, h)

    def specs(c):
        return [pl.BlockSpec((t, c), lambda i: (i, 0))] + [_pat_spec(d, c) for d in DILATED]

    args = [os[0]] + [_pat_view(o, d) for o, d in zip(os[1:], DILATED)]
    args += [ms[0]] + [_pat_view(m, d) for m, d in zip(ms[1:], DILATED)]
    args += [ls[0]] + [_pat_view(l, d) for l, d in zip(ls[1:], DILATED)]
    return pl.pallas_call(
        body, name="attn_combine", grid=(S // t,),
        in_specs=specs(ATT_WIDTH) + specs(LANES) + specs(LANES),
        out_specs=[pl.BlockSpec((t, ATT_WIDTH), lambda i: (i, 0)), pl.BlockSpec((t, LANES), lambda i: (i, 0))],
        out_shape=[jax.ShapeDtypeStruct((S, ATT_WIDTH), F32), jax.ShapeDtypeStruct((S, LANES), F32)],
        scratch_shapes=[pltpu.VMEM((ATT_WIDTH // LANES, t, LANES), F32)] * 2
        + [pltpu.VMEM((1, t, LANES), F32)] * 4,
        compiler_params=_params(("parallel",)),
    )(*args)


def _attn_delta(d_att, att, lse):
    S = d_att.shape[0]
    t = LAYOUT_TILE

    def body(d_ref, a_ref, l_ref, *refs):
        out_refs, d_slab, l_slab, dl_slab = refs[:-3], refs[-3], refs[-2], refs[-1]
        dd = d_ref[...]
        prod = dd * a_ref[...]
        lane = lax.broadcasted_iota(jnp.int32, (t, LANES), 1)
        acc = jnp.zeros((t, LANES), F32)
        for h in range(N_ATT_HEADS):
            s = jnp.sum(prod[:, h * HEAD_DIM:(h + 1) * HEAD_DIM], axis=-1, keepdims=True)
            acc = jnp.where(lane == h, s, acc)
        out_refs[0][...] = _b(dd)
        out_refs[1][...] = acc
        _to_slabs(d_slab, d_ref)
        l_slab[0] = l_ref[...]
        dl_slab[0] = acc
        for k, d in enumerate(DILATED):
            db_ref, ls_ref, dl_ref = out_refs[2 + 3 * k:5 + 3 * k]
            _gather_pattern(db_ref, d_slab, d, BF16)
            _gather_pattern(ls_ref, l_slab, d, F32)
            _gather_pattern(dl_ref, dl_slab, d, F32)

    nat = lambda c: pl.BlockSpec((t, c), lambda i: (i, 0))
    out_specs = [nat(ATT_WIDTH), nat(LANES)]
    out_shape = [jax.ShapeDtypeStruct((S, ATT_WIDTH), BF16), jax.ShapeDtypeStruct((S, LANES), F32)]
    for d in DILATED:
        out_specs += [_pat_spec(d, ATT_WIDTH), _pat_spec(d, LANES), _pat_spec(d, LANES)]
        out_shape += [jax.ShapeDtypeStruct((d, S // d, ATT_WIDTH), BF16),
                      jax.ShapeDtypeStruct((d, S // d, LANES), F32),
                      jax.ShapeDtypeStruct((d, S // d, LANES), F32)]
    outs = pl.pallas_call(
        body, name="attn_delta", grid=(S // t,),
        in_specs=[nat(ATT_WIDTH), nat(ATT_WIDTH), nat(LANES)],
        out_specs=out_specs, out_shape=out_shape,
        scratch_shapes=[pltpu.VMEM((ATT_WIDTH // LANES, t, LANES), F32), pltpu.VMEM((1, t, LANES), F32),
                        pltpu.VMEM((1, t, LANES), F32)],
        compiler_params=_params(("parallel",)),
    )(d_att, att, lse)
    res = [(outs[0], lse, outs[1])]
    for k in range(len(DILATED)):
        db, ls, dl = outs[2 + 3 * k:5 + 3 * k]
        res.append((db.reshape(S, ATT_WIDTH), ls.reshape(S, LANES), dl.reshape(S, LANES)))
    return res


def _sum_qkv(dqs, dks, dvs):
    S = dqs[0].shape[0]
    t = LAYOUT_TILE

    def body(*refs):
        o_ref, scr = refs[-2], refs[-1]
        for part in range(3):
            nat_ref, p_refs = refs[3 * part], refs[3 * part + 1:3 * part + 3]
            _to_slabs(scr, nat_ref)
            for d, p_ref in zip(DILATED, p_refs):
                _scatter_pattern(scr, p_ref, d, add=True)
            for cb in range(ATT_WIDTH // LANES):
                o_ref[:, part * ATT_WIDTH + cb * LANES:part * ATT_WIDTH + (cb + 1) * LANES] = _b(scr[cb])

    in_specs, args = [], []
    for group in (dqs, dks, dvs):
        in_specs += [pl.BlockSpec((t, ATT_WIDTH), lambda i: (i, 0))] + [_pat_spec(d, ATT_WIDTH) for d in DILATED]
        args += [group[0]] + [_pat_view(a, d) for a, d in zip(group[1:], DILATED)]
    return pl.pallas_call(
        body, name="sum_dqkv", grid=(S // t,),
        in_specs=in_specs,
        out_specs=pl.BlockSpec((t, 3 * ATT_WIDTH), lambda i: (i, 0)),
        out_shape=jax.ShapeDtypeStruct((S, 3 * ATT_WIDTH), BF16),
        scratch_shapes=[pltpu.VMEM((ATT_WIDTH // LANES, t, LANES), F32)],
        compiler_params=_params(("parallel",)),
    )(*args)


CONV_COLS = 1024
CONV_ROWS = 512
HALO = 8


def _shift_down(x, k, top_src):
    r8 = lax.broadcasted_iota(jnp.int32, (HALO, x.shape[1]), 0)
    rolled = pltpu.roll(x, k, 0)
    top = jnp.where(r8 < k, pltpu.roll(top_src, k, 0), rolled[0:HALO])
    if x.shape[0] == HALO:
        return top
    return jnp.concatenate([top, rolled[HALO:]], axis=0)


def _shift_up(x, k, bottom_src):
    n = x.shape[0]
    r8 = lax.broadcasted_iota(jnp.int32, (HALO, x.shape[1]), 0)
    rolled = pltpu.roll(x, n - k, 0)
    bottom = jnp.where(r8 >= HALO - k, pltpu.roll(bottom_src, HALO - k, 0), rolled[n - HALO:n])
    return jnp.concatenate([rolled[:n - HALO], bottom], axis=0)


def _conv_pre(x, top_src, w_ref, b_ref):
    shifted = [x] + [_shift_down(x, k, top_src) for k in range(1, SSM_CONV)]
    pre = b_ref[...] + w_ref[SSM_CONV - 1:SSM_CONV, :] * x
    for k in range(1, SSM_CONV):
        pre = pre + w_ref[SSM_CONV - 1 - k:SSM_CONV - k, :] * shifted[k]
    return pre, shifted


def _conv_fwd(xbc, conv_w, conv_b):
    S, C = xbc.shape
    bs, bc = CONV_ROWS, CONV_COLS
    nr = S // bs

    def body(x_ref, halo_ref, w_ref, b_ref, o_ref, pre_ref):
        r = pl.program_id(1)
        halo = jnp.where(r > 0, halo_ref[...], 0.0)
        pre, _ = _conv_pre(x_ref[...], halo, w_ref, b_ref)
        pre_ref[...] = _b(pre)
        o_ref[...] = pre * _sigmoid(pre)

    tile = pl.BlockSpec((bs, bc), lambda c, r: (r, c))
    return pl.pallas_call(
        body, name="conv_fwd", grid=(C // bc, nr),
        in_specs=[tile,
                  pl.BlockSpec((HALO, bc), lambda c, r: (jnp.maximum(r * (bs // HALO) - 1, 0), c)),
                  pl.BlockSpec((SSM_CONV, bc), lambda c, r: (0, c)),
                  pl.BlockSpec((1, bc), lambda c, r: (0, c))],
        out_specs=[tile, tile],
        out_shape=[jax.ShapeDtypeStruct((S, C), F32), jax.ShapeDtypeStruct((S, C), BF16)],
        compiler_params=_params(("parallel", "arbitrary")),
    )(xbc, xbc, conv_w, conv_b)


def _conv_bwd(xbc, pre_all, dact, conv_w):
    S, C = xbc.shape
    bs, bc = CONV_ROWS, CONV_COLS
    nr = S // bs
    hb = bs // HALO
    last_halo = S // HALO - 1

    def dsilu(pre):
        sg = _sigmoid(pre)
        return sg * (1.0 + pre * (1.0 - sg))

    def body(x_ref, p_ref, pn_ref, d_ref, dn_ref, w_ref, dx_ref, dw_ref, db_ref):
        r = pl.program_id(1)
        x = x_ref[...]
        dpre = d_ref[...] * dsilu(p_ref[...].astype(F32))
        dpre_n = jnp.where(r < nr - 1, dn_ref[...], 0.0) * dsilu(pn_ref[...].astype(F32)[0:HALO])
        ups = [dpre] + [_shift_up(dpre, k, dpre_n) for k in range(1, SSM_CONV)]
        dx = w_ref[SSM_CONV - 1:SSM_CONV, :] * dpre
        for k in range(1, SSM_CONV):
            dx = dx + w_ref[SSM_CONV - 1 - k:SSM_CONV - k, :] * ups[k]
        dx_ref[...] = _b(dx)
        parts = [jnp.sum(x * ups[SSM_CONV - 1 - j], axis=0, keepdims=True) for j in range(SSM_CONV)]
        dbp = jnp.sum(dpre, axis=0, keepdims=True)

        @pl.when(r == 0)
        def _():
            for j in range(SSM_CONV):
                dw_ref[j:j + 1, :] = parts[j]
            db_ref[...] = dbp

        @pl.when(r > 0)
        def _():
            for j in range(SSM_CONV):
                dw_ref[j:j + 1, :] += parts[j]
            db_ref[...] += dbp

    tile = pl.BlockSpec((bs, bc), lambda c, r: (r, c))
    nxt = pl.BlockSpec((HALO, bc), lambda c, r: (jnp.minimum((r + 1) * hb, last_halo), c))
    nxt16 = pl.BlockSpec((BF16_ROWS, bc), lambda c, r: (
        jnp.minimum((r + 1) * (bs // BF16_ROWS), S // BF16_ROWS - 1), c))
    return pl.pallas_call(
        body, name="conv_bwd", grid=(C // bc, nr),
        in_specs=[tile, tile, nxt16, tile, nxt, pl.BlockSpec((SSM_CONV, bc), lambda c, r: (0, c))],
        out_specs=[tile,
                   pl.BlockSpec((SSM_CONV, bc), lambda c, r: (0, c)),
                   pl.BlockSpec((1, bc), lambda c, r: (0, c))],
        out_shape=[jax.ShapeDtypeStruct((S, C), BF16), jax.ShapeDtypeStruct((SSM_CONV, C), F32),
                   jax.ShapeDtypeStruct((1, C), F32)],
        compiler_params=_params(("parallel", "arbitrary")),
    )(xbc, pre_all, pre_all, dact, dact, conv_w)


def _softplus(x):
    return jnp.maximum(x, 0.0) + jnp.log(1.0 + jnp.exp(-jnp.abs(x)))


GROUP_W = HEADS_PER_GROUP * SSM_HEAD_DIM
B_COL0 = SSM_INNER
C_COL0 = SSM_INNER + SSM_GROUPS * SSM_STATE


def _ssd_prep(dt_raw, dt_bias, a_neg):
    S = dt_raw.shape[0]
    ch = SSM_CHUNK
    nch = S // ch

    def body(dtr_ref, bias_ref, a_ref, dt_ref, acs_ref, acst_ref, sig_ref):
        x = dtr_ref[...] + bias_ref[...]
        lane = lax.broadcasted_iota(jnp.int32, (ch, LANES), 1)
        dt = jnp.where(lane < SSM_HEADS, _softplus(x), 0.0)
        ii = lax.broadcasted_iota(jnp.int32, (ch, ch), 0)
        jj = lax.broadcasted_iota(jnp.int32, (ch, ch), 1)
        acs = _dot_hi(jnp.where(ii >= jj, 1.0, 0.0), dt * a_ref[...])
        dt_ref[...] = dt
        acs_ref[...] = acs
        acst_ref[0] = acs.T[0:SSM_HEADS, :]
        sig_ref[...] = _sigmoid(x)

    blk = pl.BlockSpec((ch, LANES), lambda c: (c, 0))
    small = pl.BlockSpec((1, LANES), lambda c: (0, 0))
    shp = jax.ShapeDtypeStruct((S, LANES), F32)
    return pl.pallas_call(
        body, name="ssd_prep", grid=(nch,),
        in_specs=[blk, small, small],
        out_specs=[blk, blk, pl.BlockSpec((1, SSM_HEADS, ch), lambda c: (c, 0, 0)), blk],
        out_shape=[shp, shp, jax.ShapeDtypeStruct((nch, SSM_HEADS, ch), F32), shp],
        compiler_params=_params(("parallel",)),
    )(dt_raw, dt_bias, a_neg)


def _expand_heads(arr, g, rows):
    lane = lax.broadcasted_iota(jnp.int32, (rows, GROUP_W), 1) // SSM_HEAD_DIM
    h0 = HEADS_PER_GROUP * g
    out = jnp.broadcast_to(arr[:, h0:h0 + 1], (rows, GROUP_W))
    for j in range(1, HEADS_PER_GROUP):
        out = jnp.where(lane == j, arr[:, h0 + j:h0 + j + 1], out)
    return out


def _seg_matrix(k, lanes_per_head, h0):
    r = lax.broadcasted_iota(jnp.int32, (k, LANES), 0)
    c = lax.broadcasted_iota(jnp.int32, (k, LANES), 1)
    return jnp.where(c == h0 + r // lanes_per_head, 1.0, 0.0).astype(BF16)


def _seg_dot(t, e):
    hi = _b(t)
    lo = _b(t - hi.astype(F32))
    return _dot(hi, e) + _dot(lo, e)


def _head_sums(t, e, rows):
    if rows >= 8:
        return _seg_dot(t, e)
    return _seg_dot(jnp.broadcast_to(t, (8, t.shape[1])), e)[0:rows]


def _pair_masks(x):
    lane = lax.broadcasted_iota(jnp.int32, x.shape, 1)
    zero = jnp.zeros_like(x)
    return jnp.where(lane < SSM_HEAD_DIM, x, zero), jnp.where(lane >= SSM_HEAD_DIM, x, zero)


def _ssd_fwd(xact, dt, acs, acst, dsk_e):
    S = xact.shape[0]
    ch = SSM_CHUNK
    nch = S // ch

    def body(x_ref, dt_ref, acs_ref, acst_ref, dsk_ref, y_ref, hs_ref, h_ref):
        c = pl.program_id(0)

        @pl.when(c == 0)
        def _():
            h_ref[...] = jnp.zeros_like(h_ref)

        dt_all = dt_ref[...]
        acs_all = acs_ref[...]
        acst_all = acst_ref[0]
        alast = acs_all[ch - 1:ch, :]
        eacs = jnp.exp(acs_all)
        wd_all = dt_all * jnp.exp(alast - acs_all)
        dtt = dt_all.T
        cd_all = jnp.exp(alast)
        ii = lax.broadcasted_iota(jnp.int32, (ch, ch), 0)
        jj = lax.broadcasted_iota(jnp.int32, (ch, ch), 1)
        low = ii >= jj
        for g in range(SSM_GROUPS):
            xs = x_ref[:, g * GROUP_W:(g + 1) * GROUP_W]
            bb = _b(x_ref[:, B_COL0 + g * SSM_STATE:B_COL0 + (g + 1) * SSM_STATE])
            cc = _b(x_ref[:, C_COL0 + g * SSM_STATE:C_COL0 + (g + 1) * SSM_STATE])
            cb = _dot_nt(cc, bb)
            xsb = _b(xs)
            ht = h_ref[g]
            rest = (_dot(cc, _b(ht)) * _expand_heads(eacs, g, ch)
                    + dsk_ref[:, g * GROUP_W:(g + 1) * GROUP_W] * xs)
            for p in range(HEADS_PER_GROUP // 2):
                lms = []
                for h in (HEADS_PER_GROUP * g + 2 * p, HEADS_PER_GROUP * g + 2 * p + 1):
                    diff = acs_all[:, h:h + 1] - acst_all[h:h + 1, :]
                    lms.append(_b(cb * jnp.exp(jnp.where(low, diff, -jnp.inf)) * dtt[h:h + 1, :]))
                xa, xb = _pair_masks(xsb[:, p * LANES:(p + 1) * LANES])
                yp = _dot(jnp.concatenate(lms, axis=1), jnp.concatenate([xa, xb], axis=0))
                y_ref[:, g * GROUP_W + p * LANES:g * GROUP_W + (p + 1) * LANES] = (
                    yp + rest[:, p * LANES:(p + 1) * LANES])
            hs_ref[0, g] = ht
            st = _dot_tn(bb, _b(xs * _expand_heads(wd_all, g, ch)))
            h_ref[g] = ht * _expand_heads(cd_all, g, 1) + st

    blk = pl.BlockSpec((ch, LANES), lambda c: (c, 0))
    return pl.pallas_call(
        body, name="ssd_fwd", grid=(nch,),
        in_specs=[pl.BlockSpec((ch, CONV_DIM), lambda c: (c, 0)), blk, blk,
                  pl.BlockSpec((1, SSM_HEADS, ch), lambda c: (c, 0, 0)),
                  pl.BlockSpec((1, SSM_INNER), lambda c: (0, 0))],
        out_specs=[pl.BlockSpec((ch, SSM_INNER), lambda c: (c, 0)),
                   pl.BlockSpec((1, SSM_GROUPS, SSM_STATE, GROUP_W), lambda c: (c, 0, 0, 0))],
        out_shape=[jax.ShapeDtypeStruct((S, SSM_INNER), F32),
                   jax.ShapeDtypeStruct((nch, SSM_GROUPS, SSM_STATE, GROUP_W), F32)],
        scratch_shapes=[pltpu.VMEM((SSM_GROUPS, SSM_STATE, GROUP_W), F32)],
        compiler_params=_params(("arbitrary",)),
    )(xact, dt, acs, acst, dsk_e)


def _ssd_bwd(xact, dt, acs, acst, sig, a_neg, dsk_e, hs, dy):
    S = xact.shape[0]
    ch = SSM_CHUNK
    nch = S // ch
    ng, hg = SSM_GROUPS, HEADS_PER_GROUP
    nbc = SSM_GROUPS * SSM_STATE

    def body(x_ref, dt_ref, acs_ref, acst_ref, sig_ref, a_ref, dsk_ref, hs_ref, dy_ref,
             dx_ref, ddt_ref, st_ref,
             dh_ref, rows_ref, e_dt, e_ea, e_dsd, xdb_s, xddb_s, dzb_s, bcb_s, cb_s, zz_s, ww_s, dc1_s, db1_s,
             dhin_s, dlm_s, lmb_s, gm_s, dcbb_s, t_s, dxd_s, prod_s, csum_s):
        step = pl.program_id(0)

        @pl.when(step == 0)
        def _():
            dh_ref[...] = jnp.zeros_like(dh_ref)
            st_ref[...] = jnp.zeros_like(st_ref)
            rows_ref[...] = jnp.zeros_like(rows_ref)

        dt_all = dt_ref[...]
        acs_all = acs_ref[...]
        alast = acs_all[ch - 1:ch, :]
        eacs = jnp.exp(acs_all)
        dsd_all = jnp.exp(alast - acs_all)
        cd_all = jnp.exp(alast)
        ii = lax.broadcasted_iota(jnp.int32, (ch, ch), 0)
        jj = lax.broadcasted_iota(jnp.int32, (ch, ch), 1)
        low = ii >= jj
        gsl = [slice(g * GROUP_W, (g + 1) * GROUP_W) for g in range(ng)]
        psl = [[slice(g * GROUP_W + p * LANES, g * GROUP_W + (p + 1) * LANES) for p in range(hg // 2)]
               for g in range(ng)]
        seg = [_seg_matrix(GROUP_W, SSM_HEAD_DIM, hg * g) for g in range(ng)]

        def bc(g):
            return (bcb_s[:, g * SSM_STATE:(g + 1) * SSM_STATE],
                    bcb_s[:, nbc + g * SSM_STATE:nbc + (g + 1) * SSM_STATE])

        def dy_pair(g, p):
            return _pair_masks(_b(dy_ref[:, psl[g][p]]))

        bcb_s[...] = _b(x_ref[:, B_COL0:])
        for g in range(ng):
            dt_e = _expand_heads(dt_all, g, ch)
            ea_e = _expand_heads(eacs, g, ch)
            dsd_e = _expand_heads(dsd_all, g, ch)
            e_dt[:, gsl[g]] = dt_e
            e_ea[:, gsl[g]] = ea_e
            e_dsd[:, gsl[g]] = dsd_e
            xd = x_ref[:, gsl[g]] * dt_e
            xdb_s[:, gsl[g]] = _b(xd)
            xddb_s[:, gsl[g]] = _b(xd * dsd_e)
            dzb_s[:, gsl[g]] = _b(dy_ref[:, gsl[g]] * ea_e)
        for g in range(ng):
            bb, cc = bc(g)
            htb = _b(hs_ref[0, g])
            dhnb = _b(dh_ref[g])
            cb_s[g] = _dot_nt(cc, bb)
            zz_s[:, gsl[g]] = _dot(cc, htb)
            ww_s[:, gsl[g]] = _dot(bb, dhnb)
            dc1_s[g] = _dot_nt(dzb_s[:, gsl[g]], htb)
            db1_s[g] = _dot_nt(xddb_s[:, gsl[g]], dhnb)
            dhin_s[g] = _dot_tn(cc, dzb_s[:, gsl[g]])
            for p in range(hg // 2):
                xp = xdb_s[:, psl[g][p]]
                for q, dyh in enumerate(dy_pair(g, p)):
                    dlm_s[hg * g + 2 * p + q] = _dot_nt(dyh, xp)
        for g in range(ng):
            cb = cb_s[g]
            dcb = jnp.zeros((ch, ch), F32)
            for j in range(hg):
                h = hg * g + j
                diff = acs_all[:, h:h + 1] - acst_ref[0, h:h + 1, :]
                decay = jnp.exp(jnp.where(low, diff, -jnp.inf))
                lm = cb * decay
                dlm = dlm_s[h]
                gm = dlm * lm
                dcb = dcb + dlm * decay
                rows_ref[h:h + 1, :] = jnp.sum(gm, axis=0, keepdims=True)
                lmb_s[h * ch:(h + 1) * ch, :] = _b(lm)
                gm_s[:, h * ch:(h + 1) * ch] = gm
            dcbb_s[g] = _b(dcb)
            xs = x_ref[:, gsl[g]]
            dyg = dy_ref[:, gsl[g]]
            ww = ww_s[:, gsl[g]]
            dsd_e = e_dsd[:, gsl[g]]
            t2 = ww * (xs * e_dt[:, gsl[g]] * dsd_e)
            t_s[:, gsl[g]] = dyg * zz_s[:, gsl[g]] * e_ea[:, gsl[g]] - t2
            dhn = dh_ref[g]
            csum_s[0:1, gsl[g]] = jnp.sum(t2, axis=0, keepdims=True)
            csum_s[1:2, gsl[g]] = jnp.sum(dhn * hs_ref[0, g], axis=0, keepdims=True)
            csum_s[2:3, gsl[g]] = jnp.sum(dyg * xs, axis=0, keepdims=True)
            dh_ref[g] = dhin_s[g] + dhn * _expand_heads(cd_all, g, 1)
            dxd_s[:, gsl[g]] = ww * dsd_e
        cols = jnp.zeros((ch, LANES), F32)
        for g in range(ng):
            bb, cc = bc(g)
            dcbb = dcbb_s[g]
            dx_ref[:, C_COL0 + g * SSM_STATE:C_COL0 + (g + 1) * SSM_STATE] = dc1_s[g] + _dot(dcbb, bb)
            dx_ref[:, B_COL0 + g * SSM_STATE:B_COL0 + (g + 1) * SSM_STATE] = db1_s[g] + _dot_tn(dcbb, cc)
            cols = cols + _head_sums(t_s[:, gsl[g]], seg[g], ch)
            for p in range(hg // 2):
                h0 = hg * g + 2 * p
                dxd_s[:, psl[g][p]] += _dot_tn(lmb_s[h0 * ch:(h0 + 2) * ch, :],
                                               jnp.concatenate(dy_pair(g, p), axis=0))
                cols = cols + _head_sums(gm_s[:, h0 * ch:(h0 + 2) * ch], _seg_matrix(2 * ch, ch, h0), ch)
        for g in range(ng):
            dxd = dxd_s[:, gsl[g]]
            xs = x_ref[:, gsl[g]]
            dx_ref[:, gsl[g]] = dsk_ref[:, gsl[g]] * dy_ref[:, gsl[g]] + dxd * e_dt[:, gsl[g]]
            prod_s[:, gsl[g]] = dxd * xs
        ddt = jnp.zeros((ch, LANES), F32)
        dal = jnp.zeros((1, LANES), F32)
        ddsk = jnp.zeros((1, LANES), F32)
        for g in range(ng):
            ddt = ddt + _head_sums(prod_s[:, gsl[g]], seg[g], ch)
            dal = (dal + _head_sums(csum_s[0:1, gsl[g]], seg[g], 1)
                   + cd_all * _head_sums(csum_s[1:2, gsl[g]], seg[g], 1))
            ddsk = ddsk + _head_sums(csum_s[2:3, gsl[g]], seg[g], 1)
        rowi = lax.broadcasted_iota(jnp.int32, (ch, 1), 0)
        dacs = cols - rows_ref[...].T + jnp.where(rowi == ch - 1, dal, 0.0)
        dla = _dot_hi(jnp.where(ii <= jj, 1.0, 0.0), dacs)
        a_row = a_ref[...]
        ddt_raw = (ddt + dla * a_row) * sig_ref[...]
        ddt_ref[...] = _b(ddt_raw)
        st_ref[0:1, :] += jnp.sum(dla * dt_all, axis=0, keepdims=True) * a_row
        st_ref[1:2, :] += ddsk
        st_ref[2:3, :] += jnp.sum(ddt_raw, axis=0, keepdims=True)

    rc = lambda s: nch - 1 - s
    blk = pl.BlockSpec((ch, LANES), lambda s: (rc(s), 0))
    wide = lambda dt_: pltpu.VMEM((ch, SSM_INNER), dt_)
    sq = lambda n, dt_: pltpu.VMEM((n, ch, ch), dt_)
    scratch = [pltpu.VMEM((ng, SSM_STATE, GROUP_W), F32), pltpu.VMEM((LANES, ch), F32),
               wide(F32), wide(F32), wide(F32),
               wide(BF16), wide(BF16), wide(BF16), wide(BF16),
               sq(ng, F32), wide(F32), wide(F32), sq(ng, F32), sq(ng, F32),
               pltpu.VMEM((ng, SSM_STATE, GROUP_W), F32),
               sq(SSM_HEADS, F32),
               pltpu.VMEM((SSM_HEADS * ch, ch), BF16),
               pltpu.VMEM((ch, SSM_HEADS * ch), F32),
               sq(ng, BF16), wide(F32), wide(F32), wide(F32),
               pltpu.VMEM((8, SSM_INNER), F32)]
    return pl.pallas_call(
        body, name="ssd_bwd", grid=(nch,),
        in_specs=[pl.BlockSpec((ch, CONV_DIM), lambda s: (rc(s), 0)), blk, blk,
                  pl.BlockSpec((1, SSM_HEADS, ch), lambda s: (rc(s), 0, 0)), blk,
                  pl.BlockSpec((1, LANES), lambda s: (0, 0)),
                  pl.BlockSpec((1, SSM_INNER), lambda s: (0, 0)),
                  pl.BlockSpec((1, SSM_GROUPS, SSM_STATE, GROUP_W), lambda s: (rc(s), 0, 0, 0)),
                  pl.BlockSpec((ch, SSM_INNER), lambda s: (rc(s), 0))],
        out_specs=[pl.BlockSpec((ch, CONV_DIM), lambda s: (rc(s), 0)), blk,
                   pl.BlockSpec((8, LANES), lambda s: (0, 0))],
        out_shape=[jax.ShapeDtypeStruct((S, CONV_DIM), F32), jax.ShapeDtypeStruct((S, LANES), BF16),
                   jax.ShapeDtypeStruct((8, LANES), F32)],
        scratch_shapes=scratch,
        compiler_params=_params(("arbitrary",)),
    )(xact, dt, acs, acst, sig, a_neg, dsk_e, hs, dy)


def _pad_lanes(v, n=LANES):
    return jnp.pad(v, ((0, 0), (0, n - v.shape[1])))


def _local_step(x, target, w, ex=None):
    offs = np.cumsum((0,) + IN_SPLITS)
    wt_in = w["w_in_t"]
    w_qkv = wt_in[offs[0]:offs[3]]
    w_z = wt_in[offs[3]:offs[4]]
    w_xbc = wt_in[offs[4]:offs[5]]
    w_dt = jnp.pad(wt_in[offs[5]:offs[6]], ((0, LANES - SSM_HEADS), (0, 0)))
    w_g = wt_in[offs[6]:offs[7]]
    dt_bias = _pad_lanes(w["dt_bias"])
    a_neg = _pad_lanes(-jnp.exp(w["a_log"]))

    u = _rms_fwd(x, w["norm_mix_pre_w"])
    if ex is None:
        xbc = _mm_nn(u, w_xbc, F32, "proj_xbc", tb=True)
    else:
        xbc, got = _mm_nn(u, w_xbc, F32, "proj_xbc", comm=_gather_comm([ex.mine[REST_EARLY]]), tb=True)
        w = {**w, **ex.rest_weights(got[0], REST_EARLY)}
    n_tok = x.shape[0]
    qkv_outs, _ = _mm_epi(u, w_qkv, _qkv_epi(n_tok), "proj_qkv", tb=True, tn=ATT_WIDTH)
    z = _mm_nn(u, w_z, F32, "proj_z", tb=True)
    dt_raw = _mm_nn(u, w_dt, F32, "proj_dt", tb=True)
    gl = _mm_nn(u, w_g, F32, "proj_gate", tb=True)

    pats = [qkv_outs[0]] + [o.reshape(n_tok, 3 * ATT_WIDTH) for o in qkv_outs[1:]]
    os_, ms_, ls_ = [], [], []
    for i, (d, qkv_p) in enumerate(zip(DILATIONS, pats)):
        if ex is not None and i < len(REST_LATE):
            o, m, l, got = _attn_fwd(qkv_p, d, comm=_gather_comm([ex.mine[REST_LATE[i]]]))
            w = {**w, **ex.rest_weights(got[0], REST_LATE[i])}
        else:
            o, m, l = _attn_fwd(qkv_p, d)
        os_.append(o)
        ms_.append(m)
        ls_.append(l)
    att, lse = _attn_combine(os_, ms_, ls_)
    att_o = _mm_nn(att, w["w_att_proj"], F32, "att_proj")

    xact, conv_pre = _conv_fwd(xbc, w["conv_w"], w["conv_b"])
    dsk_e = jnp.repeat(w["d_skip"], SSM_HEAD_DIM, axis=1)
    dt, acs, acst, sig = _ssd_prep(dt_raw, dt_bias, a_neg)
    y_ssd, hs = _ssd_fwd(xact, dt, acs, acst, dsk_e)
    (ssm_y, ssm_o), _ = _mm_epi(None, w["w_ssm_proj"], _ssm_out_epi(y_ssd, z, w["ssm_norm_w"]), "ssm_proj")

    (mi, mixed, h1, f), _ = _mm_epi(None, w["w_out"], _mix_out_epi(
        att_o, ssm_o, gl, x, w["b_gate"], w["norm_mix_post_w"], w["norm_ffn_pre_w"]), "out_proj")
    r_up, act = _mm_nn(f, w["w_up"], BF16, "ffn_up", mode="relu2")
    (dh2, d_down, loss, g_ffn_post), _ = _mm_epi(
        act, w["w_down"], _final_epi(h1, target, w["norm_ffn_post_w"]), "ffn_down")

    g = {"norm_ffn_post_w": g_ffn_post}
    g["w_down"] = _mm_tn(act, d_down, "dw_down")
    dup = _mm_nn(d_down, w["w_down"], BF16, "d_act", mode="mul2", extra=r_up, tb=True)
    g["w_up"] = _mm_tn(f, dup, "dw_up")
    (dh1, d_mixed, g["norm_ffn_pre_w"], g["norm_mix_post_w"]), _ = _mm_epi(
        dup, w["w_up"], _mid_epi(dh2, h1, mixed, w["norm_ffn_pre_w"], w["norm_mix_post_w"]), "d_f", tb=True)
    g["w_out"] = _mm_tn(mi, d_mixed, "dw_out")
    (d_att_o, d_ssm_o, dgl, g["b_gate"]), _ = _mm_epi(
        d_mixed, w["w_out"], _gate_epi(att_o, ssm_o, gl, w["b_gate"]), "d_mi", tb=True)

    g["w_att_proj"] = _mm_tn(att, d_att_o, "dw_att_proj")
    g["w_ssm_proj"] = _mm_tn(ssm_y, d_ssm_o, "dw_ssm_proj")
    gn_epi = _gnorm_epi(y_ssd, z, w["ssm_norm_w"])
    if ex is None:
        (dy_ssd, dz, g["ssm_norm_w"]), _ = _mm_epi(d_ssm_o, w["w_ssm_proj"], gn_epi, "d_ssm_y", tb=True,
                                                    tn=PACK_COLS)
    else:
        gs_rest = jnp.concatenate(
            [_shards_from_full(n, g[n]).reshape(N_CHIPS, -1, PACK_COLS) for n in REST], axis=1)
        (dy_ssd, dz, g["ssm_norm_w"]), recv = _mm_epi(d_ssm_o, w["w_ssm_proj"], gn_epi, "d_ssm_y", tb=True,
                                                       tn=PACK_COLS, tm=1024, comm=_pair_comm([gs_rest]))
        p_rest = _pair_add(gs_rest, recv[0], ex.c_arr, "rs_pair_add_rest")

    d_att = _mm_nn(d_att_o, w["w_att_proj"], F32, "d_att", tb=True)
    bwd_ins = _attn_delta(d_att, att, lse)
    dqs, dks, dvs = [], [], []
    for d, qkv_p, (do_p, lse_p, delta_p) in zip(DILATIONS, pats, bwd_ins):
        if ex is not None and d == DILATIONS[0]:
            dq, dk, dv, recv3 = _attn_bwd(qkv_p, do_p, lse_p, delta_p, d, comm=_chip_comm([p_rest]))
            q_rest = _chip_add(p_rest, recv3[0], ex.chip_arr, "rs_chip_add_rest")
            ex.finish_reduce("rest", q_rest, _comm_call("rs_share_rest", _share_comm([q_rest]))[0])
        else:
            dq, dk, dv = _attn_bwd(qkv_p, do_p, lse_p, delta_p, d)
        dqs.append(dq)
        dks.append(dk)
        dvs.append(dv)
    dqkv = _sum_qkv(dqs, dks, dvs)

    dxact, ddt_raw, stats = _ssd_bwd(xact, dt, acs, acst, sig, a_neg, dsk_e, hs, dy_ssd)
    g["a_log"] = stats[0:1, :SSM_HEADS]
    g["d_skip"] = stats[1:2, :SSM_HEADS]
    g["dt_bias"] = stats[2:3, :SSM_HEADS]
    dxbc, g["conv_w"], g["conv_b"] = _conv_bwd(xbc, conv_pre, dxact, w["conv_w"])

    pieces = [(dqkv, w_qkv), (dz, w_z), (dxbc, w_xbc), (ddt_raw, w_dt), (dgl, w_g)]
    gw = [_mm_tn(dp, u, f"dw_in_{i}") for i, (dp, _) in enumerate(pieces)]
    gw[3] = gw[3][:SSM_HEADS]
    if ex is None:
        g["w_in_t"] = jnp.concatenate(gw, axis=0)
    du = None
    for i, (dp, wp) in enumerate([pieces[k] for k in (1, 2, 0, 3, 4)]):
        if ex is not None and i == 0:
            gs_in = _rows_to_shards(gw, IN_SHARD_ROWS, IN_SHARD_PAD)
            du, recv = _mm_nn(dp, wp, F32, f"d_u_{i}", acc=du, comm=_pair_comm([gs_in]))
            p_in = _pair_add(gs_in, recv[0], ex.c_arr, "rs_pair_add_in")
            rows = p_in.shape[1] // 2
            p_parts = [p_in[:, :rows], p_in[:, rows:]]
            q_parts = []
        elif ex is not None and i in (1, 2):
            p_part = p_parts[i - 1]
            du, recv3 = _mm_nn(dp, wp, F32, f"d_u_{i}", acc=du, comm=_chip_comm([p_part]))
            q_parts.append(_chip_add(p_part, recv3[0], ex.chip_arr, f"rs_chip_add_in_{i}"))
            if i == 2:
                others = _comm_call("rs_share_in", _share_comm(q_parts))
                ex.finish_reduce("w_in", jnp.concatenate(q_parts, axis=0), jnp.concatenate(others, axis=0))
        elif i == len(pieces) - 1:
            (grad_x, g["norm_mix_pre_w"]), _ = _mm_epi(
                dp, wp, _first_epi(du, dh1, x, w["norm_mix_pre_w"]), f"d_u_{i}")
        else:
            du = _mm_nn(dp, wp, F32, f"d_u_{i}", acc=du)
    return loss, grad_x, g


def _rows_to_shards(pieces, shard_rows, pad_rows):
    cols = pieces[0].shape[1]
    shards = []
    for s in range(N_CHIPS):
        lo, hi = s * shard_rows, (s + 1) * shard_rows
        parts, r0 = [], 0
        for p in pieces:
            a, b = max(lo, r0), min(hi, r0 + p.shape[0])
            if a < b:
                parts.append(p[a - r0:b - r0])
            r0 += p.shape[0]
        parts.append(jnp.zeros((pad_rows - shard_rows, cols), pieces[0].dtype))
        shards.append(jnp.concatenate(parts, axis=0))
    return jnp.stack(shards)


BIG = ("w_in", "w_att_proj", "w_ssm_proj", "w_out", "w_up", "w_down")
BIG_FULL_SHAPES = {"w_in": (D_MODEL, IN_PROJ_WIDTH), "w_att_proj": (ATT_WIDTH, D_MODEL),
                   "w_ssm_proj": (SSM_INNER, D_MODEL), "w_out": (D_MODEL, D_MODEL),
                   "w_up": (D_MODEL, FFN_HIDDEN), "w_down": (FFN_HIDDEN, D_MODEL)}
BIG_COL_SHARDED = {"w_in": True, "w_att_proj": True, "w_ssm_proj": False, "w_out": False, "w_up": True,
                   "w_down": False}
PACK_COLS = 1024
SMALL = ("norm_mix_pre_w", "b_gate", "conv_b", "dt_bias", "a_log", "d_skip", "ssm_norm_w",
         "norm_mix_post_w", "norm_ffn_pre_w", "norm_ffn_post_w")
SMALL_ROWS = 232


def _shard_shape(name):
    r, c = BIG_FULL_SHAPES[name]
    return (r, c // N_CHIPS) if BIG_COL_SHARDED[name] else (r // N_CHIPS, c)


def _mesh_pos():
    return lax.axis_index("x"), lax.axis_index("y"), lax.axis_index("c")


def _other_chips(x, y):
    return [(1 - x, y), (x, 1 - y), (1 - x, 1 - y)]


ANY = pl.BlockSpec(memory_space=pl.ANY)


REST_EARLY = ("w_att_proj", "w_ssm_proj", "w_out")
REST_LATE = (("w_up",), ("w_down",))
REST = REST_EARLY + REST_LATE[0] + REST_LATE[1]
ADD_ROWS_CAP = 800
BF16_ROWS = 16
IN_SHARD_ROWS = IN_PROJ_WIDTH // N_CHIPS
IN_SHARD_PAD = 2688


def _stack_rest(shards, dtype, names=REST):
    return jnp.concatenate([shards[n].astype(dtype).reshape(-1, PACK_COLS) for n in names], axis=0)


def _unstack_rest(stacked, lead=(), names=REST):
    out, r0 = {}, 0
    for n in names:
        shp = _shard_shape(n)
        rows = shp[0] * shp[1] // PACK_COLS
        out[n] = stacked[..., r0:r0 + rows, :].reshape(lead + shp)
        r0 += rows
    return out


def _full_from_shards(name, sh):
    if BIG_COL_SHARDED[name]:
        return sh.transpose(1, 0, 2).reshape(BIG_FULL_SHAPES[name])
    return sh.reshape(BIG_FULL_SHAPES[name])


def _shards_from_full(name, full):
    shp = _shard_shape(name)
    if BIG_COL_SHARDED[name]:
        return full.reshape(shp[0], N_CHIPS, shp[1]).transpose(1, 0, 2)
    return full.reshape((N_CHIPS,) + shp)


def _pair_add(g, recv, c_idx, name):
    _, half, cols = recv.shape
    rb = _row_block(half, ADD_ROWS_CAP, BF16_ROWS)
    nb = half // rb

    def body(c_ref, g_ref, r_ref, o_ref):
        o_ref[...] = _b(g_ref[...] + r_ref[...])

    blk = (1, rb, cols)
    return pl.pallas_call(
        body, name=name,
        grid_spec=pltpu.PrefetchScalarGridSpec(
            num_scalar_prefetch=1, grid=(N_CHIPS, nb),
            in_specs=[pl.BlockSpec(blk, lambda s, i, c: (s, c[0] * nb + i, 0)),
                      pl.BlockSpec(blk, lambda s, i, c: (s, i, 0))],
            out_specs=pl.BlockSpec(blk, lambda s, i, c: (s, i, 0))),
        out_shape=jax.ShapeDtypeStruct(recv.shape, BF16),
        compiler_params=_params(("arbitrary", "arbitrary")),
    )(c_idx, g, recv)


def _chip_add(p, recv, me_idx, name):
    _, half, cols = recv.shape
    rb = _row_block(half, ADD_ROWS_CAP, BF16_ROWS)

    def body(m_ref, p_ref, r0_ref, r1_ref, r2_ref, o_ref):
        o_ref[...] = ((p_ref[0].astype(F32) + r0_ref[0].astype(F32)) + r1_ref[0].astype(F32)) + r2_ref[0].astype(F32)

    blk = (1, rb, cols)
    return pl.pallas_call(
        body, name=name,
        grid_spec=pltpu.PrefetchScalarGridSpec(
            num_scalar_prefetch=1, grid=(half // rb,),
            in_specs=[pl.BlockSpec(blk, lambda i, m: (m[0], i, 0)),
                      pl.BlockSpec(blk, lambda i, m: (0, i, 0)),
                      pl.BlockSpec(blk, lambda i, m: (1, i, 0)),
                      pl.BlockSpec(blk, lambda i, m: (2, i, 0))],
            out_specs=pl.BlockSpec((rb, cols), lambda i, m: (i, 0))),
        out_shape=jax.ShapeDtypeStruct((half, cols), F32),
        compiler_params=_params(("arbitrary",)),
    )(me_idx, p, recv, recv, recv)


def _gather_plan():
    def copies(w_refs, out_refs, send_sems, recv_sems):
        x, y, c = _mesh_pos()
        me = 2 * x + y
        sibling = (x, y, 1 - c)
        chips = _other_chips(x, y)
        idx = [2 * chip[0] + chip[1] for chip in chips]
        plans = []
        for a, (w_ref, out_ref) in enumerate(zip(w_refs, out_refs)):
            half = w_ref.shape[0] // 2

            def copy(k, chip, h, to, src=None, out_ref=out_ref, half=half, a=a):
                rows = out_ref.at[chip, pl.ds(h * half, half), :]
                return pltpu.make_async_remote_copy(
                    src_ref=rows if src is None else src, dst_ref=rows,
                    send_sem=send_sems.at[7 * a + k], recv_sem=recv_sems.at[7 * a + k],
                    device_id=to, device_id_type=MESH)

            own = pltpu.make_async_remote_copy(
                src_ref=w_ref, dst_ref=out_ref.at[me], send_sem=send_sems.at[7 * a + 6],
                recv_sem=recv_sems.at[7 * a + 6], device_id=sibling, device_id_type=MESH)
            mine_half = w_ref.at[pl.ds(c * half, half), :]
            send = [own] + [copy(j, me, c, (*chip, c), src=mine_half) for j, chip in enumerate(chips)]
            land = [copy(j, idx[j], c, (x, y, c)) for j in range(N_CHIPS - 1)]
            forward = [copy(3 + j, idx[j], c, sibling) for j in range(N_CHIPS - 1)]
            land_fw = [copy(3 + j, idx[j], 1 - c, (x, y, c)) for j in range(N_CHIPS - 1)]
            plans.append((send, land, forward, land_fw))
        return plans

    def start(*refs):
        for send, _, _, _ in copies(*refs):
            for cp in send:
                cp.start()

    def finish(*refs):
        plans = copies(*refs)
        for _, land, forward, _ in plans:
            for j in range(N_CHIPS - 1):
                land[j].wait_recv()
                forward[j].start()
        for send, _, _, land_fw in plans:
            for cp in land_fw:
                cp.wait_recv()
            send[0].wait_recv()
        for send, _, forward, _ in plans:
            for cp in send + forward:
                cp.wait_send()

    return start, finish


def _pair_plan(halves):
    def copies(in_refs, out_refs, send_sems, recv_sems):
        x, y, c = _mesh_pos()
        cps = []
        for a, (g_ref, out_ref) in enumerate(zip(in_refs, out_refs)):
            if halves:
                half = g_ref.shape[1] // 2
                src = g_ref.at[:, pl.ds((1 - c) * half, half), :]
            else:
                src = g_ref
            cps.append(pltpu.make_async_remote_copy(
                src_ref=src, dst_ref=out_ref, send_sem=send_sems.at[a], recv_sem=recv_sems.at[a],
                device_id=(x, y, 1 - c), device_id_type=MESH))
        return cps

    def start(*refs):
        for cp in copies(*refs):
            cp.start()

    def finish(*refs):
        for cp in copies(*refs):
            cp.wait()

    return start, finish


def _chip_plan():
    def copies(in_refs, out_refs, send_sems, recv_sems):
        x, y, c = _mesh_pos()
        chips = _other_chips(x, y)
        return [pltpu.make_async_remote_copy(
            src_ref=p_ref.at[2 * chip[0] + chip[1]], dst_ref=out_ref.at[j],
            send_sem=send_sems.at[3 * a + j], recv_sem=recv_sems.at[3 * a + j], device_id=(*chip, c),
            device_id_type=MESH)
            for a, (p_ref, out_ref) in enumerate(zip(in_refs, out_refs)) for j, chip in enumerate(chips)]

    def start(*refs):
        for cp in copies(*refs):
            cp.start()

    def finish(*refs):
        cps = copies(*refs)
        for cp in cps:
            cp.wait_recv()
        for cp in cps:
            cp.wait_send()

    return start, finish


def _gather_comm(shards):
    return _Comm(_gather_plan(), shards, [jax.ShapeDtypeStruct((N_CHIPS,) + s.shape, s.dtype) for s in shards],
                 7 * len(shards))


def _pair_comm(gs):
    return _Comm(_pair_plan(True), gs,
                 [jax.ShapeDtypeStruct((N_CHIPS, g.shape[1] // 2, g.shape[2]), g.dtype) for g in gs], len(gs))


def _chip_comm(ps):
    return _Comm(_chip_plan(), ps, [jax.ShapeDtypeStruct((N_CHIPS - 1,) + p.shape[1:], p.dtype) for p in ps],
                 3 * len(ps))


def _share_comm(qs):
    return _Comm(_pair_plan(False), qs, [jax.ShapeDtypeStruct(q.shape, q.dtype) for q in qs], len(qs))


def _comm_call(name, comm):
    n, m = len(comm.ins), len(comm.outs)

    def body(*refs):
        args = (refs[:n], refs[n:n + m], refs[n + m], refs[n + m + 1])
        comm.start(*args)
        comm.finish(*args)

    return pl.pallas_call(
        body, name=name, out_shape=comm.outs, in_specs=[ANY] * n, out_specs=[ANY] * m,
        scratch_shapes=[pltpu.SemaphoreType.DMA((comm.n_sems,))] * 2,
        compiler_params=pltpu.CompilerParams(has_side_effects=True),
    )(*comm.ins)


class _Exchange:
    def __init__(self, chip, ci, mine):
        self.chip, self.ci = chip, ci
        self.mine = mine
        self.c_arr = ci.reshape(1).astype(jnp.int32)
        self.chip_arr = chip.reshape(1).astype(jnp.int32)
        self.reduced = {}

    def rest_weights(self, got, names):
        stacks = got
        return {n: _full_from_shards(n, sh) for n, sh in _unstack_rest(stacks, (N_CHIPS,), names).items()}

    def finish_reduce(self, key, mine, other):
        south = self.ci == 0
        self.reduced[key] = jnp.concatenate([jnp.where(south, mine, other), jnp.where(south, other, mine)],
                                            axis=0)


def _allreduce_small(part, name):
    rows = part.shape[0]

    def body(p_ref, out_ref, buf, send_sems, recv_sems, local_sem):
        x, y, c = _mesh_pos()
        me, sibling = (x, y, c), (x, y, 1 - c)
        chips = _other_chips(x, y)

        def slot(px, py, pc):
            return buf.at[pl.ds((4 * px + 2 * py + pc) * rows, rows), :]

        def copy(k, block, to, src=None):
            return pltpu.make_async_remote_copy(
                src_ref=slot(*block) if src is None else src, dst_ref=slot(*block),
                send_sem=send_sems.at[k], recv_sem=recv_sems.at[k], device_id=to, device_id_type=MESH)

        mine = pltpu.make_async_copy(p_ref, slot(*me), local_sem)
        mine.start()
        first = [copy(0, me, sibling, src=p_ref)]
        first += [copy(1 + j, me, (*chip, c), src=p_ref) for j, chip in enumerate(chips)]
        for cp in first:
            cp.start()
        passed = [copy(4 + j, (*chip, c), sibling) for j, chip in enumerate(chips)]
        for j, chip in enumerate(chips):
            copy(1 + j, (*chip, c), me).wait_recv()
            passed[j].start()
        copy(0, sibling, me).wait_recv()
        for j, chip in enumerate(chips):
            copy(4 + j, (*chip, 1 - c), me).wait_recv()
        for cp in first + passed:
            cp.wait_send()
        mine.wait()
        acc = buf[pl.ds(0, rows), :]
        for k in range(1, N_DEV):
            acc = acc + buf[pl.ds(k * rows, rows), :]
        out_ref[...] = acc

    return pl.pallas_call(
        body, name=name,
        out_shape=jax.ShapeDtypeStruct(part.shape, F32),
        in_specs=[pl.BlockSpec(memory_space=pltpu.VMEM)],
        out_specs=pl.BlockSpec(memory_space=pltpu.VMEM),
        scratch_shapes=[pltpu.VMEM((N_DEV * rows, LANES), F32), pltpu.SemaphoreType.DMA((7,)),
                        pltpu.SemaphoreType.DMA((7,)), pltpu.SemaphoreType.DMA],
        compiler_params=pltpu.CompilerParams(has_side_effects=True),
    )(part)


def _adamw(w, g, m, v, name):
    R, C = w.shape
    bs = _row_block(R, 512, 8) if R % 8 == 0 else R
    c1 = 1.0 / (1.0 - ADAM_B1 ** ADAM_STEP)
    c2 = 1.0 / (1.0 - ADAM_B2 ** ADAM_STEP)

    def body(w_ref, g_ref, m_ref, v_ref, d_ref, nm_ref, nv_ref):
        gg = g_ref[...]
        nm = ADAM_B1 * m_ref[...] + (1.0 - ADAM_B1) * gg
        nv = ADAM_B2 * v_ref[...] + (1.0 - ADAM_B2) * (gg * gg)
        nm_ref[...] = nm
        nv_ref[...] = nv
        d_ref[...] = -ADAM_LR * ((nm * c1) / (jnp.sqrt(nv * c2) + ADAM_EPS) + ADAM_WD * w_ref[...])

    spec = pl.BlockSpec((bs, C), lambda i: (i, 0))
    shp = jax.ShapeDtypeStruct((R, C), F32)
    return pl.pallas_call(
        body, name=name, grid=(R // bs,), in_specs=[spec] * 4, out_specs=[spec] * 3, out_shape=[shp] * 3,
        compiler_params=_params(("parallel",)),
    )(w, g, m, v)


WEIGHTS = ("norm_mix_pre_w", "w_in", "b_gate", "conv_w", "conv_b", "dt_bias", "a_log", "d_skip",
           "ssm_norm_w", "w_att_proj", "w_ssm_proj", "w_out", "norm_mix_post_w", "norm_ffn_pre_w", "w_up",
           "w_down", "norm_ffn_post_w")


def _flat_small(vals, conv_w_full):
    flat = [vals[n].reshape(-1) for n in SMALL] + [conv_w_full.reshape(-1)]
    v = jnp.concatenate(flat)
    return jnp.pad(v, (0, SMALL_ROWS * LANES - v.shape[0])).reshape(SMALL_ROWS, LANES)


def kernel(x, norm_mix_pre_w, w_in, b_gate, conv_w, conv_b, dt_bias, a_log, d_skip, ssm_norm_w, w_att_proj, w_ssm_proj, w_out, norm_mix_post_w, norm_ffn_pre_w, w_up, w_down, norm_ffn_post_w, loss_target, m_norm_mix_pre_w, m_w_in, m_b_gate, m_conv_w, m_conv_b, m_dt_bias, m_a_log, m_d_skip, m_ssm_norm_w, m_w_att_proj, m_w_ssm_proj, m_w_out, m_norm_mix_post_w, m_norm_ffn_pre_w, m_w_up, m_w_down, m_norm_ffn_post_w, v_norm_mix_pre_w, v_w_in, v_b_gate, v_conv_w, v_conv_b, v_dt_bias, v_a_log, v_d_skip, v_ssm_norm_w, v_w_att_proj, v_w_ssm_proj, v_w_out, v_norm_mix_post_w, v_norm_ffn_pre_w, v_w_up, v_w_down, v_norm_ffn_post_w):
    args = locals()

    def strip(a):
        return a[0] if a.ndim == 3 else a

    wts = {n: strip(args[n]) for n in WEIGHTS}
    mom = {n: strip(args["m_" + n]) for n in WEIGHTS}
    var = {n: strip(args["v_" + n]) for n in WEIGHTS}
    xi, yi, ci = _mesh_pos()
    chip = 2 * xi + yi

    tr = lambda a: jnp.swapaxes(a, 0, 1)
    w_in_mine = jnp.pad(tr(wts["w_in"]).astype(BF16), ((0, IN_SHARD_PAD - IN_SHARD_ROWS), (0, 0)))
    got_in = _comm_call("allgather_w_in", _gather_comm([w_in_mine]))[0]
    stacks_in = got_in
    full = {"w_in_t": stacks_in[:, :IN_SHARD_ROWS].reshape(IN_PROJ_WIDTH, D_MODEL)}
    ex = _Exchange(chip, ci, {names: _stack_rest(wts, BF16, names) for names in (REST_EARLY,) + REST_LATE})
    cw_cols = CONV_DIM // N_CHIPS
    conv_slab = lax.dynamic_update_slice(jnp.zeros((SSM_CONV, CONV_DIM), F32),
                                         jnp.where(ci == 0, wts["conv_w"], 0.0), (0, chip * cw_cols))
    small_in = jnp.pad(conv_slab.reshape(-1), (0, SMALL_ROWS * LANES - SSM_CONV * CONV_DIM))
    conv_full = _allreduce_small(small_in.reshape(SMALL_ROWS, LANES), "gather_conv_w")
    full["conv_w"] = conv_full.reshape(-1)[:SSM_CONV * CONV_DIM].reshape(SSM_CONV, CONV_DIM)
    for n in SMALL:
        full[n] = wts[n]

    loss_part, grad_x, g = _local_step(x[0], loss_target[0], full, ex)
    loss = lax.psum(loss_part[0, 0], ("x", "y", "c"))

    gshard = _unstack_rest(ex.reduced["rest"])
    g_in_t = ex.reduced["w_in"][:IN_SHARD_ROWS]
    small_sum = _allreduce_small(_flat_small(g, g["conv_w"]), "allreduce_small_grads").reshape(-1)
    grads, off = {}, 0
    for n in SMALL:
        sz = wts[n].size
        grads[n] = small_sum[off:off + sz].reshape(wts[n].shape)
        off += sz
    conv_g = small_sum[off:off + SSM_CONV * CONV_DIM].reshape(SSM_CONV, CONV_DIM)
    grads["conv_w"] = lax.dynamic_slice(conv_g, (0, chip * cw_cols), (SSM_CONV, cw_cols))
    grads.update(gshard)

    delta, new_m, new_v = {}, {}, {}
    for n in REST:
        delta[n], new_m[n], new_v[n] = _adamw(wts[n], grads[n], mom[n], var[n], f"adamw_{n}")
    in_t = _adamw(tr(wts["w_in"]), g_in_t, tr(mom["w_in"]), tr(var["w_in"]), "adamw_w_in")
    grads["w_in"] = tr(g_in_t)
    delta["w_in"], new_m["w_in"], new_v["w_in"] = (tr(a) for a in in_t)
    small_names = SMALL + ("conv_w",)

    def pack_small(d):
        v = jnp.concatenate([d[n].reshape(-1) for n in small_names])
        rows = -(-v.shape[0] // (8 * LANES)) * 8
        return jnp.pad(v, (0, rows * LANES - v.shape[0])).reshape(rows, LANES)

    ds, ms, vs = _adamw(pack_small(wts), pack_small(grads), pack_small(mom), pack_small(var), "adamw_small")
    off = 0
    for n in small_names:
        sz = wts[n].size
        for dst, src in ((delta, ds), (new_m, ms), (new_v, vs)):
            dst[n] = src.reshape(-1)[off:off + sz].reshape(wts[n].shape)
        off += sz

    out = [loss, grad_x[None]]
    for d in (grads, delta, new_m, new_v):
        out += [d[n][None] if args[n].ndim == 3 else d[n] for n in WEIGHTS]
    return tuple(out)
```
